```python
import math
import jax, jax.numpy as jnp
from jax import lax
import numpy as np

D_MODEL = 1024
BATCH = 8
SEQ = 4096
DEPTH = 2

A_WIDTH = D_MODEL // 2
A_GROUPS = 4
A_KERNEL = 31
B_WIDTH = D_MODEL // 2
B_GROUPS = 4
B_GROUP_DIM = B_WIDTH // B_GROUPS
B_CHUNK = 128
HEAD_DIM = 64
N_Q_HEADS = D_MODEL // HEAD_DIM
N_KV_HEADS = N_Q_HEADS // 8
Q_PER_KV = N_Q_HEADS // N_KV_HEADS
WINDOW = 128
ATT_BLOCK = 128
D_FF = 2816
FFN_KERNEL = 3
N_EVEN = (DEPTH + 1) // 2
N_ODD = DEPTH // 2
DEEPNORM_ALPHA = (2.0 * DEPTH) ** 0.25
DEEPNORM_BETA = (8.0 * DEPTH) ** -0.25
LN_EPS = 1e-5

kernel_name = "hybrid_conv_gmlp_swa_sink_deepnorm"


def layer_norm(x, g, b):
    xf = x.astype(jnp.float32)
    mu = jnp.mean(xf, axis=-1, keepdims=True)
    var = jnp.mean(jnp.square(xf - mu), axis=-1, keepdims=True)
    y = (xf - mu) * lax.rsqrt(var + LN_EPS)
    return (y * g.astype(jnp.float32) + b.astype(jnp.float32)).astype(x.dtype)


def causal_dwconv(x, w, b):
    k = w.shape[0]
    c = x.shape[-1]
    y = lax.conv_general_dilated(
        x, w[:, None, :].astype(x.dtype), window_strides=(1,), padding=[(k - 1, 0)],
        dimension_numbers=("NWC", "WIO", "NWC"), feature_group_count=c)
    return y + b.astype(x.dtype)


def conv_gmlp_mixer(x, w_in, a_conv_w, a_conv_b, a_norm_g, a_norm_b,
                    b_norm_g, b_norm_b, b_spatial_w, b_spatial_b, w_out):
    bsz, seq, _ = x.shape
    h = x @ w_in
    a_val, a_gate, b_u, b_v = jnp.split(h, 4, axis=-1)
    a = a_val * jax.nn.sigmoid(a_gate)
    a = causal_dwconv(a, a_conv_w, a_conv_b)
    a = jax.nn.silu(layer_norm(a, a_norm_g, a_norm_b))
    b_u = jax.nn.gelu(b_u)
    b_v = layer_norm(jax.nn.gelu(b_v), b_norm_g, b_norm_b)
    n_chunks = seq // B_CHUNK
    vb = b_v.reshape(bsz, n_chunks, B_CHUNK, B_GROUPS, B_GROUP_DIM)
    causal = jnp.tril(jnp.ones((B_CHUNK, B_CHUNK), dtype=bool))
    ws = jnp.where(causal[None], b_spatial_w, jnp.zeros((), b_spatial_w.dtype))
    mixed = jnp.einsum("gij,bcjgd->bcigd", ws, vb) + b_spatial_b.T[None, None, :, :, None]
    b = b_u * mixed.reshape(bsz, seq, B_WIDTH)
    return jnp.concatenate([a, b], axis=-1) @ w_out


def swa_sink_attention(x, w_qkv, b_qkv, sinks, w_o):
    bsz, seq, _ = x.shape
    nb = seq // ATT_BLOCK
    qkv = x @ w_qkv + b_qkv
    q, k, v = jnp.split(qkv, [N_Q_HEADS * HEAD_DIM, (N_Q_HEADS + N_KV_HEADS) * HEAD_DIM], axis=-1)
    q = q.reshape(bsz, nb, ATT_BLOCK, N_KV_HEADS, Q_PER_KV, HEAD_DIM)
    k = k.reshape(bsz, nb, ATT_BLOCK, N_KV_HEADS, HEAD_DIM)
    v = v.reshape(bsz, nb, ATT_BLOCK, N_KV_HEADS, HEAD_DIM)

    def with_prev(t):
        prev = jnp.concatenate([jnp.zeros_like(t[:, :1]), t[:, :-1]], axis=1)
        return jnp.concatenate([prev, t], axis=2)

    kk, vv = with_prev(k), with_prev(v)
    scale = 1.0 / math.sqrt(HEAD_DIM)
    scores = jnp.einsum("bnqkgd,bnskd->bnkgqs", q, kk).astype(jnp.float32) * scale
    qi = jnp.arange(ATT_BLOCK)[:, None]
    sj = jnp.arange(2 * ATT_BLOCK)[None, :]
    diff = qi + ATT_BLOCK - sj
    in_window = (diff >= 0) & (diff < WINDOW)
    blk = jnp.arange(nb)[:, None, None]
    valid = in_window[None] & ((blk > 0) | (sj[None] >= ATT_BLOCK))
    scores = jnp.where(valid[None, :, None, None], scores, -jnp.inf)
    sink = jnp.broadcast_to(sinks.astype(jnp.float32).reshape(1, 1, N_KV_HEADS, Q_PER_KV, 1, 1),
                            scores.shape[:-1] + (1,))
    probs = jax.nn.softmax(jnp.concatenate([scores, sink], axis=-1), axis=-1)[..., :-1]
    out = jnp.einsum("bnkgqs,bnskd->bnqkgd", probs.astype(vv.dtype), vv)
    return out.reshape(bsz, seq, N_Q_HEADS * HEAD_DIM) @ w_o


def conv_ffn(x, w_up, conv_w, conv_b, w_down):
    h = causal_dwconv(x @ w_up, conv_w, conv_b)
    gate, val = jnp.split(h, 2, axis=-1)
    return (jax.nn.gelu(gate) * val) @ w_down


def _fwd_setup_inputs(seed: int = 0) -> dict:
    key = jax.random.key(seed)
    ks = jax.random.split(key, 24)
    f32 = jnp.float32
    nrm = lambda k, shape, s: jax.random.normal(k, shape, f32) * s
    d_in_ab = 2 * A_WIDTH + 2 * B_WIDTH
    d_qkv = (N_Q_HEADS + 2 * N_KV_HEADS) * HEAD_DIM
    return {
        "x": nrm(ks[0], (BATCH, SEQ, D_MODEL), 1.0),
        "ab_w_in": nrm(ks[1], (N_EVEN, D_MODEL, d_in_ab), D_MODEL ** -0.5),
        "a_conv_w": nrm(ks[2], (N_EVEN, A_KERNEL, A_WIDTH), A_KERNEL ** -0.5),
        "a_conv_b": nrm(ks[3], (N_EVEN, A_WIDTH), 0.02),
        "a_norm_g": 1.0 + nrm(ks[4], (N_EVEN, A_WIDTH), 0.02),
        "a_norm_b": nrm(ks[5], (N_EVEN, A_WIDTH), 0.02),
        "b_norm_g": 1.0 + nrm(ks[6], (N_EVEN, B_WIDTH), 0.02),
        "b_norm_b": nrm(ks[7], (N_EVEN, B_WIDTH), 0.02),
        "b_spatial_w": nrm(ks[8], (N_EVEN, B_GROUPS, B_CHUNK, B_CHUNK), B_CHUNK ** -0.5),
        "b_spatial_b": 1.0 + nrm(ks[9], (N_EVEN, B_GROUPS, B_CHUNK), 0.02),
        "ab_w_out": nrm(ks[10], (N_EVEN, A_WIDTH + B_WIDTH, D_MODEL), (A_WIDTH + B_WIDTH) ** -0.5 * DEEPNORM_BETA),
        "c_w_qkv": nrm(ks[11], (N_ODD, D_MODEL, d_qkv), D_MODEL ** -0.5),
        "c_b_qkv": nrm(ks[12], (N_ODD, d_qkv), 0.02),
        "c_sinks": nrm(ks[13], (N_ODD, N_Q_HEADS), 0.5),
        "c_w_o": nrm(ks[14], (N_ODD, N_Q_HEADS * HEAD_DIM, D_MODEL), (N_Q_HEADS * HEAD_DIM) ** -0.5 * DEEPNORM_BETA),
        "ffn_w_up": nrm(ks[15], (DEPTH, D_MODEL, 2 * D_FF), D_MODEL ** -0.5),
        "ffn_conv_w": nrm(ks[16], (DEPTH, FFN_KERNEL, 2 * D_FF), FFN_KERNEL ** -0.5),
        "ffn_conv_b": nrm(ks[17], (DEPTH, 2 * D_FF), 0.02),
        "ffn_w_down": nrm(ks[18], (DEPTH, D_FF, D_MODEL), D_FF ** -0.5 * DEEPNORM_BETA),
        "ln_g": 1.0 + nrm(ks[19], (DEPTH, 2, D_MODEL), 0.02),
        "ln_b": nrm(ks[20], (DEPTH, 2, D_MODEL), 0.02),
    }


def _fwd_reference(x, ab_w_in, a_conv_w, a_conv_b, a_norm_g, a_norm_b, b_norm_g, b_norm_b,
              b_spatial_w, b_spatial_b, ab_w_out, c_w_qkv, c_b_qkv, c_sinks, c_w_o,
              ffn_w_up, ffn_conv_w, ffn_conv_b, ffn_w_down, ln_g, ln_b):
    alpha = jnp.asarray(DEEPNORM_ALPHA, dtype=x.dtype)
    for i in range(DEPTH):
        j = i // 2
        if i % 2 == 0:
            mix = conv_gmlp_mixer(x, ab_w_in[j], a_conv_w[j], a_conv_b[j], a_norm_g[j], a_norm_b[j],
                                  b_norm_g[j], b_norm_b[j], b_spatial_w[j], b_spatial_b[j], ab_w_out[j])
        else:
            mix = swa_sink_attention(x, c_w_qkv[j], c_b_qkv[j], c_sinks[j], c_w_o[j])
        x = layer_norm(alpha * x + mix, ln_g[i, 0], ln_b[i, 0])
        ffn = conv_ffn(x, ffn_w_up[i], ffn_conv_w[i], ffn_conv_b[i], ffn_w_down[i])
        x = layer_norm(alpha * x + ffn, ln_g[i, 1], ln_b[i, 1])
    return x


import jax as _jax
import jax.numpy as _jnp

TWIN_FORMAT = 'train_step'
FWD_PARAMS = ['x', 'ab_w_in', 'a_conv_w', 'a_conv_b', 'a_norm_g', 'a_norm_b', 'b_norm_g', 'b_norm_b', 'b_spatial_w', 'b_spatial_b', 'ab_w_out', 'c_w_qkv', 'c_b_qkv', 'c_sinks', 'c_w_o', 'ffn_w_up', 'ffn_conv_w', 'ffn_conv_b', 'ffn_w_down', 'ln_g', 'ln_b']
TWIN_WEIGHTS = ['ab_w_in', 'a_conv_w', 'a_conv_b', 'a_norm_g', 'a_norm_b', 'b_norm_g', 'b_norm_b', 'b_spatial_w', 'b_spatial_b', 'ab_w_out', 'c_w_qkv', 'c_b_qkv', 'c_sinks', 'c_w_o', 'ffn_w_up', 'ffn_conv_w', 'ffn_conv_b', 'ffn_w_down', 'ln_g', 'ln_b']
TWIN_DIFF_INPUT = 'x'
TWIN_INPUTS = ['x', 'ab_w_in', 'a_conv_w', 'a_conv_b', 'a_norm_g', 'a_norm_b', 'b_norm_g', 'b_norm_b', 'b_spatial_w', 'b_spatial_b', 'ab_w_out', 'c_w_qkv', 'c_b_qkv', 'c_sinks', 'c_w_o', 'ffn_w_up', 'ffn_conv_w', 'ffn_conv_b', 'ffn_w_down', 'ln_g', 'ln_b', 'loss_target', 'm_ab_w_in', 'm_a_conv_w', 'm_a_conv_b', 'm_a_norm_g', 'm_a_norm_b', 'm_b_norm_g', 'm_b_norm_b', 'm_b_spatial_w', 'm_b_spatial_b', 'm_ab_w_out', 'm_c_w_qkv', 'm_c_b_qkv', 'm_c_sinks', 'm_c_w_o', 'm_ffn_w_up', 'm_ffn_conv_w', 'm_ffn_conv_b', 'm_ffn_w_down', 'm_ln_g', 'm_ln_b', 'v_ab_w_in', 'v_a_conv_w', 'v_a_conv_b', 'v_a_norm_g', 'v_a_norm_b', 'v_b_norm_g', 'v_b_norm_b', 'v_b_spatial_w', 'v_b_spatial_b', 'v_ab_w_out', 'v_c_w_qkv', 'v_c_b_qkv', 'v_c_sinks', 'v_c_w_o', 'v_ffn_w_up', 'v_ffn_conv_w', 'v_ffn_conv_b', 'v_ffn_w_down', 'v_ln_g', 'v_ln_b']
TWIN_OUTPUTS = ['loss', 'grad_x', 'grad_ab_w_in', 'grad_a_conv_w', 'grad_a_conv_b', 'grad_a_norm_g', 'grad_a_norm_b', 'grad_b_norm_g', 'grad_b_norm_b', 'grad_b_spatial_w', 'grad_b_spatial_b', 'grad_ab_w_out', 'grad_c_w_qkv', 'grad_c_b_qkv', 'grad_c_sinks', 'grad_c_w_o', 'grad_ffn_w_up', 'grad_ffn_conv_w', 'grad_ffn_conv_b', 'grad_ffn_w_down', 'grad_ln_g', 'grad_ln_b', 'delta_ab_w_in', 'delta_a_conv_w', 'delta_a_conv_b', 'delta_a_norm_g', 'delta_a_norm_b', 'delta_b_norm_g', 'delta_b_norm_b', 'delta_b_spatial_w', 'delta_b_spatial_b', 'delta_ab_w_out', 'delta_c_w_qkv', 'delta_c_b_qkv', 'delta_c_sinks', 'delta_c_w_o', 'delta_ffn_w_up', 'delta_ffn_conv_w', 'delta_ffn_conv_b', 'delta_ffn_w_down', 'delta_ln_g', 'delta_ln_b', 'new_m_ab_w_in', 'new_m_a_conv_w', 'new_m_a_conv_b', 'new_m_a_norm_g', 'new_m_a_norm_b', 'new_m_b_norm_g', 'new_m_b_norm_b', 'new_m_b_spatial_w', 'new_m_b_spatial_b', 'new_m_ab_w_out', 'new_m_c_w_qkv', 'new_m_c_b_qkv', 'new_m_c_sinks', 'new_m_c_w_o', 'new_m_ffn_w_up', 'new_m_ffn_conv_w', 'new_m_ffn_conv_b', 'new_m_ffn_w_down', 'new_m_ln_g', 'new_m_ln_b', 'new_v_ab_w_in', 'new_v_a_conv_w', 'new_v_a_conv_b', 'new_v_a_norm_g', 'new_v_a_norm_b', 'new_v_b_norm_g', 'new_v_b_norm_b', 'new_v_b_spatial_w', 'new_v_b_spatial_b', 'new_v_ab_w_out', 'new_v_c_w_qkv', 'new_v_c_b_qkv', 'new_v_c_sinks', 'new_v_c_w_o', 'new_v_ffn_w_up', 'new_v_ffn_conv_w', 'new_v_ffn_conv_b', 'new_v_ffn_w_down', 'new_v_ln_g', 'new_v_ln_b']
TWIN_LEAF_KINDS = {'loss': 'loss', 'grad_x': 'grad_x', 'grad_ab_w_in': 'grad_w', 'grad_a_conv_w': 'grad_w', 'grad_a_conv_b': 'grad_w', 'grad_a_norm_g': 'grad_w', 'grad_a_norm_b': 'grad_w', 'grad_b_norm_g': 'grad_w', 'grad_b_norm_b': 'grad_w', 'grad_b_spatial_w': 'grad_w', 'grad_b_spatial_b': 'grad_w', 'grad_ab_w_out': 'grad_w', 'grad_c_w_qkv': 'grad_w', 'grad_c_b_qkv': 'grad_w', 'grad_c_sinks': 'grad_w', 'grad_c_w_o': 'grad_w', 'grad_ffn_w_up': 'grad_w', 'grad_ffn_conv_w': 'grad_w', 'grad_ffn_conv_b': 'grad_w', 'grad_ffn_w_down': 'grad_w', 'grad_ln_g': 'grad_w', 'grad_ln_b': 'grad_w', 'delta_ab_w_in': 'delta_w', 'delta_a_conv_w': 'delta_w', 'delta_a_conv_b': 'delta_w', 'delta_a_norm_g': 'delta_w', 'delta_a_norm_b': 'delta_w', 'delta_b_norm_g': 'delta_w', 'delta_b_norm_b': 'delta_w', 'delta_b_spatial_w': 'delta_w', 'delta_b_spatial_b': 'delta_w', 'delta_ab_w_out': 'delta_w', 'delta_c_w_qkv': 'delta_w', 'delta_c_b_qkv': 'delta_w', 'delta_c_sinks': 'delta_w', 'delta_c_w_o': 'delta_w', 'delta_ffn_w_up': 'delta_w', 'delta_ffn_conv_w': 'delta_w', 'delta_ffn_conv_b': 'delta_w', 'delta_ffn_w_down': 'delta_w', 'delta_ln_g': 'delta_w', 'delta_ln_b': 'delta_w', 'new_m_ab_w_in': 'new_m', 'new_m_a_conv_w': 'new_m', 'new_m_a_conv_b': 'new_m', 'new_m_a_norm_g': 'new_m', 'new_m_a_norm_b': 'new_m', 'new_m_b_norm_g': 'new_m', 'new_m_b_norm_b': 'new_m', 'new_m_b_spatial_w': 'new_m', 'new_m_b_spatial_b': 'new_m', 'new_m_ab_w_out': 'new_m', 'new_m_c_w_qkv': 'new_m', 'new_m_c_b_qkv': 'new_m', 'new_m_c_sinks': 'new_m', 'new_m_c_w_o': 'new_m', 'new_m_ffn_w_up': 'new_m', 'new_m_ffn_conv_w': 'new_m', 'new_m_ffn_conv_b': 'new_m', 'new_m_ffn_w_down': 'new_m', 'new_m_ln_g': 'new_m', 'new_m_ln_b': 'new_m', 'new_v_ab_w_in': 'new_v', 'new_v_a_conv_w': 'new_v', 'new_v_a_conv_b': 'new_v', 'new_v_a_norm_g': 'new_v', 'new_v_a_norm_b': 'new_v', 'new_v_b_norm_g': 'new_v', 'new_v_b_norm_b': 'new_v', 'new_v_b_spatial_w': 'new_v', 'new_v_b_spatial_b': 'new_v', 'new_v_ab_w_out': 'new_v', 'new_v_c_w_qkv': 'new_v', 'new_v_c_b_qkv': 'new_v', 'new_v_c_sinks': 'new_v', 'new_v_c_w_o': 'new_v', 'new_v_ffn_w_up': 'new_v', 'new_v_ffn_conv_w': 'new_v', 'new_v_ffn_conv_b': 'new_v', 'new_v_ffn_w_down': 'new_v', 'new_v_ln_g': 'new_v', 'new_v_ln_b': 'new_v'}


def _forward(args):
    return _fwd_reference(*[args[k] for k in FWD_PARAMS])


def _output_shape():
    def fwd():
        inp = _fwd_setup_inputs(0)
        return _fwd_reference(*[inp[k] for k in FWD_PARAMS])
    out = _jax.eval_shape(fwd)
    return out.shape, out.dtype

N_MICROBATCH = 1
ADAM_LR = 0.001
ADAM_B1 = 0.9
ADAM_B2 = 0.999
ADAM_EPS = 1e-08
ADAM_WD = 0.01
ADAM_STEP = 10
PER_EXAMPLE_BATCH_AXIS = {'x': 0, 'loss_target': 0}
SHARED_INPUTS = []
_WEIGHT_DTYPES = {'ab_w_in': _jnp.float32, 'a_conv_w': _jnp.float32, 'a_conv_b': _jnp.float32, 'a_norm_g': _jnp.float32, 'a_norm_b': _jnp.float32, 'b_norm_g': _jnp.float32, 'b_norm_b': _jnp.float32, 'b_spatial_w': _jnp.float32, 'b_spatial_b': _jnp.float32, 'ab_w_out': _jnp.float32, 'c_w_qkv': _jnp.float32, 'c_b_qkv': _jnp.float32, 'c_sinks': _jnp.float32, 'c_w_o': _jnp.float32, 'ffn_w_up': _jnp.float32, 'ffn_conv_w': _jnp.float32, 'ffn_conv_b': _jnp.float32, 'ffn_w_down': _jnp.float32, 'ln_g': _jnp.float32, 'ln_b': _jnp.float32}
MOMENT_SCALE = {'ab_w_in': 3.816514e-02, 'a_conv_w': 4.118570e-02, 'a_conv_b': 1.875567e-01, 'a_norm_g': 8.021899e-02, 'a_norm_b': 1.088048e-01, 'b_norm_g': 2.858459e-02, 'b_norm_b': 3.013121e-02, 'b_spatial_w': 2.971108e-02, 'b_spatial_b': 4.067466e-02, 'ab_w_out': 1.229794e-01, 'c_w_qkv': 1.985169e-02, 'c_b_qkv': 1.717866e-01, 'c_sinks': 9.327777e-03, 'c_w_o': 3.134692e-02, 'ffn_w_up': 2.495075e-02, 'ffn_conv_w': 2.518125e-02, 'ffn_conv_b': 3.246270e-02, 'ffn_w_down': 8.167523e-02, 'ln_g': 1.607701e+01, 'ln_b': 1.544013e+00}


def _to_microbatches(a, axis):
    t = _jnp.moveaxis(a, axis, 0)
    t = t.reshape((N_MICROBATCH, t.shape[0] // N_MICROBATCH) + t.shape[1:])
    return _jnp.moveaxis(t, 1, axis + 1)


def setup_inputs(seed: int = 0) -> dict:
    inp = _fwd_setup_inputs(seed)
    key = _jax.random.fold_in(_jax.random.key(seed), 7919)
    shape, _ = _output_shape()
    out = dict(inp)
    out["loss_target"] = _jax.random.normal(_jax.random.fold_in(key, 0), shape, _jnp.float32)
    for i, name in enumerate(TWIN_WEIGHTS):
        w = inp[name].astype(_jnp.float32)
        if MOMENT_SCALE is None:
            s = _jnp.sqrt(_jnp.mean(_jnp.square(w)) + 1e-30)
        else:
            s = MOMENT_SCALE[name]
        km, kv = _jax.random.split(_jax.random.fold_in(key, i + 1))
        out[name] = w
        out["m_" + name] = s * _jax.random.normal(km, w.shape, _jnp.float32)
        out["v_" + name] = (s * s) * _jax.random.uniform(kv, w.shape, _jnp.float32, 0.5, 1.5)
    if N_MICROBATCH > 1:
        for name, axis in PER_EXAMPLE_BATCH_AXIS.items():
            out[name] = _to_microbatches(out[name], axis)
    return {'x': out['x'], 'ab_w_in': out['ab_w_in'], 'a_conv_w': out['a_conv_w'], 'a_conv_b': out['a_conv_b'], 'a_norm_g': out['a_norm_g'], 'a_norm_b': out['a_norm_b'], 'b_norm_g': out['b_norm_g'], 'b_norm_b': out['b_norm_b'], 'b_spatial_w': out['b_spatial_w'], 'b_spatial_b': out['b_spatial_b'], 'ab_w_out': out['ab_w_out'], 'c_w_qkv': out['c_w_qkv'], 'c_b_qkv': out['c_b_qkv'], 'c_sinks': out['c_sinks'], 'c_w_o': out['c_w_o'], 'ffn_w_up': out['ffn_w_up'], 'ffn_conv_w': out['ffn_conv_w'], 'ffn_conv_b': out['ffn_conv_b'], 'ffn_w_down': out['ffn_w_down'], 'ln_g': out['ln_g'], 'ln_b': out['ln_b'], 'loss_target': out['loss_target'], 'm_ab_w_in': out['m_ab_w_in'], 'm_a_conv_w': out['m_a_conv_w'], 'm_a_conv_b': out['m_a_conv_b'], 'm_a_norm_g': out['m_a_norm_g'], 'm_a_norm_b': out['m_a_norm_b'], 'm_b_norm_g': out['m_b_norm_g'], 'm_b_norm_b': out['m_b_norm_b'], 'm_b_spatial_w': out['m_b_spatial_w'], 'm_b_spatial_b': out['m_b_spatial_b'], 'm_ab_w_out': out['m_ab_w_out'], 'm_c_w_qkv': out['m_c_w_qkv'], 'm_c_b_qkv': out['m_c_b_qkv'], 'm_c_sinks': out['m_c_sinks'], 'm_c_w_o': out['m_c_w_o'], 'm_ffn_w_up': out['m_ffn_w_up'], 'm_ffn_conv_w': out['m_ffn_conv_w'], 'm_ffn_conv_b': out['m_ffn_conv_b'], 'm_ffn_w_down': out['m_ffn_w_down'], 'm_ln_g': out['m_ln_g'], 'm_ln_b': out['m_ln_b'], 'v_ab_w_in': out['v_ab_w_in'], 'v_a_conv_w': out['v_a_conv_w'], 'v_a_conv_b': out['v_a_conv_b'], 'v_a_norm_g': out['v_a_norm_g'], 'v_a_norm_b': out['v_a_norm_b'], 'v_b_norm_g': out['v_b_norm_g'], 'v_b_norm_b': out['v_b_norm_b'], 'v_b_spatial_w': out['v_b_spatial_w'], 'v_b_spatial_b': out['v_b_spatial_b'], 'v_ab_w_out': out['v_ab_w_out'], 'v_c_w_qkv': out['v_c_w_qkv'], 'v_c_b_qkv': out['v_c_b_qkv'], 'v_c_sinks': out['v_c_sinks'], 'v_c_w_o': out['v_c_w_o'], 'v_ffn_w_up': out['v_ffn_w_up'], 'v_ffn_conv_w': out['v_ffn_conv_w'], 'v_ffn_conv_b': out['v_ffn_conv_b'], 'v_ffn_w_down': out['v_ffn_w_down'], 'v_ln_g': out['v_ln_g'], 'v_ln_b': out['v_ln_b']}


def _loss(weights, diff, rest, loss_target):
    with _jax.named_scope("forward"):
        args = {**rest, TWIN_DIFF_INPUT: diff, **{k: w.astype(_WEIGHT_DTYPES[k]) for k, w in weights.items()}}
        y = _forward(args)
    with _jax.named_scope("loss_head"):
        err = _jnp.square(y.astype(_jnp.float32) - loss_target)
        return 0.5 * _jnp.sum(_jnp.mean(err, axis=-1)) if err.ndim else 0.5 * err


def _adamw(w, g, m, v):
    m = ADAM_B1 * m + (1.0 - ADAM_B1) * g
    v = ADAM_B2 * v + (1.0 - ADAM_B2) * _jnp.square(g)
    m_hat = m / (1.0 - ADAM_B1 ** ADAM_STEP)
    v_hat = v / (1.0 - ADAM_B2 ** ADAM_STEP)
    delta = -ADAM_LR * (m_hat / (_jnp.sqrt(v_hat) + ADAM_EPS) + ADAM_WD * w)
    return delta, m, v


def reference(x, ab_w_in, a_conv_w, a_conv_b, a_norm_g, a_norm_b, b_norm_g, b_norm_b, b_spatial_w, b_spatial_b, ab_w_out, c_w_qkv, c_b_qkv, c_sinks, c_w_o, ffn_w_up, ffn_conv_w, ffn_conv_b, ffn_w_down, ln_g, ln_b, loss_target, m_ab_w_in, m_a_conv_w, m_a_conv_b, m_a_norm_g, m_a_norm_b, m_b_norm_g, m_b_norm_b, m_b_spatial_w, m_b_spatial_b, m_ab_w_out, m_c_w_qkv, m_c_b_qkv, m_c_sinks, m_c_w_o, m_ffn_w_up, m_ffn_conv_w, m_ffn_conv_b, m_ffn_w_down, m_ln_g, m_ln_b, v_ab_w_in, v_a_conv_w, v_a_conv_b, v_a_norm_g, v_a_norm_b, v_b_norm_g, v_b_norm_b, v_b_spatial_w, v_b_spatial_b, v_ab_w_out, v_c_w_qkv, v_c_b_qkv, v_c_sinks, v_c_w_o, v_ffn_w_up, v_ffn_conv_w, v_ffn_conv_b, v_ffn_w_down, v_ln_g, v_ln_b):
    given = dict(x=x, ab_w_in=ab_w_in, a_conv_w=a_conv_w, a_conv_b=a_conv_b, a_norm_g=a_norm_g, a_norm_b=a_norm_b, b_norm_g=b_norm_g, b_norm_b=b_norm_b, b_spatial_w=b_spatial_w, b_spatial_b=b_spatial_b, ab_w_out=ab_w_out, c_w_qkv=c_w_qkv, c_b_qkv=c_b_qkv, c_sinks=c_sinks, c_w_o=c_w_o, ffn_w_up=ffn_w_up, ffn_conv_w=ffn_conv_w, ffn_conv_b=ffn_conv_b, ffn_w_down=ffn_w_down, ln_g=ln_g, ln_b=ln_b, loss_target=loss_target, m_ab_w_in=m_ab_w_in, m_a_conv_w=m_a_conv_w, m_a_conv_b=m_a_conv_b, m_a_norm_g=m_a_norm_g, m_a_norm_b=m_a_norm_b, m_b_norm_g=m_b_norm_g, m_b_norm_b=m_b_norm_b, m_b_spatial_w=m_b_spatial_w, m_b_spatial_b=m_b_spatial_b, m_ab_w_out=m_ab_w_out, m_c_w_qkv=m_c_w_qkv, m_c_b_qkv=m_c_b_qkv, m_c_sinks=m_c_sinks, m_c_w_o=m_c_w_o, m_ffn_w_up=m_ffn_w_up, m_ffn_conv_w=m_ffn_conv_w, m_ffn_conv_b=m_ffn_conv_b, m_ffn_w_down=m_ffn_w_down, m_ln_g=m_ln_g, m_ln_b=m_ln_b, v_ab_w_in=v_ab_w_in, v_a_conv_w=v_a_conv_w, v_a_conv_b=v_a_conv_b, v_a_norm_g=v_a_norm_g, v_a_norm_b=v_a_norm_b, v_b_norm_g=v_b_norm_g, v_b_norm_b=v_b_norm_b, v_b_spatial_w=v_b_spatial_w, v_b_spatial_b=v_b_spatial_b, v_ab_w_out=v_ab_w_out, v_c_w_qkv=v_c_w_qkv, v_c_b_qkv=v_c_b_qkv, v_c_sinks=v_c_sinks, v_c_w_o=v_c_w_o, v_ffn_w_up=v_ffn_w_up, v_ffn_conv_w=v_ffn_conv_w, v_ffn_conv_b=v_ffn_conv_b, v_ffn_w_down=v_ffn_w_down, v_ln_g=v_ln_g, v_ln_b=v_ln_b)
    weights = {n: given[n] for n in TWIN_WEIGHTS}
    shared = {n: given[n] for n in SHARED_INPUTS}
    per_example = {n: given[n] for n in ['x']}
    grad_fn = _jax.value_and_grad(_loss, argnums=(0, 1))

    def one_microbatch(ex, loss_target):
        ex = dict(ex)
        diff = ex.pop(TWIN_DIFF_INPUT)
        return grad_fn(weights, diff, {**shared, **ex}, loss_target)

    if N_MICROBATCH == 1:
        loss, (grad_w, grad_x) = one_microbatch(per_example, given["loss_target"])
    else:
        def body(carry, xs):
            loss_sum, grad_sum = carry
            l_k, (gw_k, gx_k) = one_microbatch(xs[0], xs[1])
            with _jax.named_scope("update"):
                return (loss_sum + l_k, _jax.tree.map(_jnp.add, grad_sum, gw_k)), gx_k

        init = (_jnp.zeros((), _jnp.float32), _jax.tree.map(_jnp.zeros_like, weights))
        (loss, grad_w), grad_x = _jax.lax.scan(body, init, (per_example, given["loss_target"]))
    with _jax.named_scope("update"):
        delta_w, new_m, new_v = {}, {}, {}
        for n in TWIN_WEIGHTS:
            delta_w[n], new_m[n], new_v[n] = _adamw(weights[n], grad_w[n], given["m_" + n], given["v_" + n])
    return (loss, grad_x, *[grad_w[n] for n in TWIN_WEIGHTS], *[delta_w[n] for n in TWIN_WEIGHTS],
            *[new_m[n] for n in TWIN_WEIGHTS], *[new_v[n] for n in TWIN_WEIGHTS])
```

```python
import functools
import math

import jax
import jax.numpy as jnp
from jax import lax
from jax.experimental import pallas as pl
from jax.experimental.pallas import tpu as pltpu

F32 = jnp.float32
BF16 = jnp.bfloat16

N_DEV = 8
D_MODEL = 1024
A_WIDTH = 512
A_KERNEL = 31
B_GROUPS = 4
B_CHUNK = 128
HEAD_DIM = 64
N_Q_HEADS = 16
N_KV_HEADS = 2
ATT_BLOCK = 128
D_FF = 2816
FFN_KERNEL = 3
ALPHA = (2.0 * 2) ** 0.25
LN_EPS = 1e-5
GELU_K = math.sqrt(2.0 / math.pi)
GELU_C = 0.044715
ADAM_LR = 0.001
ADAM_B1 = 0.9
ADAM_B2 = 0.999
ADAM_EPS = 1e-08
ADAM_WD = 0.01
ADAM_STEP = 10
VMEM_LIMIT = 56 * 1024 * 1024
MESH_ID = pl.DeviceIdType.MESH


def _params(*sem):
    return pltpu.CompilerParams(dimension_semantics=sem, vmem_limit_bytes=VMEM_LIMIT)


def _gelu(x):
    t = jnp.tanh(GELU_K * x * (1.0 + GELU_C * x * x))
    return 0.5 * x * (1.0 + t)


def _gelu_and_grad(x):
    x2 = x * x
    t = jnp.tanh(GELU_K * x * (1.0 + GELU_C * x2))
    g = 0.5 * x * (1.0 + t)
    dg = 0.5 * (1.0 + t) + 0.5 * x * (1.0 - t * t) * (GELU_K * (1.0 + 3.0 * GELU_C * x2))
    return g, dg


def _sigmoid(x):
    return 1.0 / (1.0 + jnp.exp(-x))


def _ln_stats(z):
    mu = jnp.mean(z, axis=-1, keepdims=True)
    zc = z - mu
    var = jnp.mean(zc * zc, axis=-1, keepdims=True)
    r = lax.rsqrt(var + LN_EPS)
    return zc * r, r


def _ln_bwd_rows(dn, nh, r):
    return r * (dn - jnp.mean(dn, axis=-1, keepdims=True) - nh * jnp.mean(dn * nh, axis=-1, keepdims=True))


def _colsum(x):
    return jnp.sum(x, axis=0, keepdims=True)


def _dot(a, b, dims):
    return lax.dot_general(a.astype(BF16), b.astype(BF16), (dims, ((), ())), preferred_element_type=F32)


NN = ((1,), (0,))
NT = ((1,), (1,))
TN = ((0,), (0,))


def _matmul(a, b, mode, out_dtype, name, tm, tn, tk, *, bias=None, res=None, res_scale=1.0, b_off=0):
    tm = min(tm, a.shape[1] if mode == "tn" else a.shape[0])
    tk = min(tk, a.shape[0] if mode == "tn" else a.shape[1])
    if mode == "nn":
        (m, k), n = a.shape, b.shape[1]
        a_spec = pl.BlockSpec((tm, tk), lambda i, j, kk: (i, kk))
        b_spec = pl.BlockSpec((tk, tn), lambda i, j, kk: (kk, j))
        dims = NN
    elif mode == "nt":
        (m, k), n = a.shape, b.shape[0]
        a_spec = pl.BlockSpec((tm, tk), lambda i, j, kk: (i, kk))
        b_spec = pl.BlockSpec((tn, tk), lambda i, j, kk: (j, kk + b_off))
        dims = NT
    else:
        (k, m), n = a.shape, b.shape[1]
        a_spec = pl.BlockSpec((tk, tm), lambda i, j, kk: (kk, i))
        b_spec = pl.BlockSpec((tk, tn), lambda i, j, kk: (kk, j))
        dims = TN
    assert m % tm == 0 and n % tn == 0 and k % tk == 0, (name, m, n, k)
    nk = k // tk
    in_specs = [a_spec, b_spec]
    args = [a, b]
    if bias is not None:
        in_specs.append(pl.BlockSpec((1, tn), lambda i, j, kk: (0, j)))
        args.append(bias)
    if res is not None:
        in_specs.append(pl.BlockSpec((tm, tn), lambda i, j, kk: (i, j)))
        args.append(res)

    def body(*refs):
        a_ref, b_ref = refs[0], refs[1]
        o_ref, acc = refs[-2], refs[-1]
        kk = pl.program_id(2)

        @pl.when(kk == 0)
        def _():
            acc[...] = jnp.zeros_like(acc)

        acc[...] += _dot(a_ref[...], b_ref[...], dims)

        @pl.when(kk == nk - 1)
        def _():
            out = acc[...]
            pos = 2
            if bias is not None:
                out = out + refs[pos][...]
                pos += 1
            if res is not None:
                out = out + res_scale * refs[pos][...].astype(F32)
            o_ref[...] = out.astype(out_dtype)

    return pl.pallas_call(
        body, name=name, grid=(m // tm, n // tn, nk),
        in_specs=in_specs, out_specs=pl.BlockSpec((tm, tn), lambda i, j, kk: (i, j)),
        out_shape=jax.ShapeDtypeStruct((m, n), out_dtype),
        scratch_shapes=[pltpu.VMEM((tm, tn), F32)],
        compiler_params=_params("parallel", "parallel", "arbitrary"),
    )(*args)


def _matmul_res_ln(a, b, xres, g, beta, name, tm, tk):
    t, k = a.shape
    d = b.shape[1]
    nk = k // tk
    assert t % tm == 0 and k % tk == 0

    def body(a_ref, b_ref, x_ref, g_ref, beta_ref, z_ref, xo_ref, acc):
        kk = pl.program_id(1)

        @pl.when(kk == 0)
        def _():
            acc[...] = jnp.zeros_like(acc)

        acc[...] += _dot(a_ref[...], b_ref[...], NN)

        @pl.when(kk == nk - 1)
        def _():
            z = ALPHA * x_ref[...] + acc[...]
            nh, _ = _ln_stats(z)
            z_ref[...] = z
            xo_ref[...] = nh * g_ref[...] + beta_ref[...]

    row = pl.BlockSpec((tm, d), lambda i, kk: (i, 0))
    vec = pl.BlockSpec((1, d), lambda i, kk: (0, 0))
    return pl.pallas_call(
        body, name=name, grid=(t // tm, nk),
        in_specs=[pl.BlockSpec((tm, tk), lambda i, kk: (i, kk)), pl.BlockSpec((tk, d), lambda i, kk: (kk, 0)), row, vec, vec],
        out_specs=[row, row],
        out_shape=[jax.ShapeDtypeStruct((t, d), F32), jax.ShapeDtypeStruct((t, d), F32)],
        scratch_shapes=[pltpu.VMEM((tm, d), F32)],
        compiler_params=_params("parallel", "arbitrary"),
    )(a, b, xres, g, beta)


def _ln_bwd(z, g, dres, dbr, name, tm=512):
    t, d = z.shape

    def body(z_ref, g_ref, dres_ref, dbr_ref, dz_ref, dg_ref, db_ref):
        @pl.when(pl.program_id(0) == 0)
        def _():
            dg_ref[...] = jnp.zeros_like(dg_ref)
            db_ref[...] = jnp.zeros_like(db_ref)

        nh, r = _ln_stats(z_ref[...])
        dy = ALPHA * dres_ref[...] + dbr_ref[...].astype(F32)
        dg_ref[...] += _colsum(dy * nh)
        db_ref[...] += _colsum(dy)
        dz_ref[...] = _ln_bwd_rows(dy * g_ref[...], nh, r)

    row = pl.BlockSpec((tm, d), lambda i: (i, 0))
    vec = pl.BlockSpec((1, d), lambda i: (0, 0))
    return pl.pallas_call(
        body, name=name, grid=(t // tm,), in_specs=[row, vec, row, row], out_specs=[row, vec, vec],
        out_shape=[jax.ShapeDtypeStruct((t, d), F32), jax.ShapeDtypeStruct((1, d), F32), jax.ShapeDtypeStruct((1, d), F32)],
        compiler_params=_params("arbitrary"),
    )(z, g, dres, dbr)


def _ln_bwd_loss(z, g, beta, target, name, tm=512):
    t, d = z.shape

    def body(z_ref, g_ref, beta_ref, t_ref, dz_ref, dg_ref, db_ref, loss_ref):
        @pl.when(pl.program_id(0) == 0)
        def _():
            dg_ref[...] = jnp.zeros_like(dg_ref)
            db_ref[...] = jnp.zeros_like(db_ref)
            loss_ref[...] = jnp.zeros_like(loss_ref)

        nh, r = _ln_stats(z_ref[...])
        err = nh * g_ref[...] + beta_ref[...] - t_ref[...]
        loss_ref[...] += _colsum(err * err)
        dy = err * (1.0 / d)
        dg_ref[...] += _colsum(dy * nh)
        db_ref[...] += _colsum(dy)
        dz_ref[...] = _ln_bwd_rows(dy * g_ref[...], nh, r)

    row = pl.BlockSpec((tm, d), lambda i: (i, 0))
    vec = pl.BlockSpec((1, d), lambda i: (0, 0))
    vshape = jax.ShapeDtypeStruct((1, d), F32)
    return pl.pallas_call(
        body, name=name, grid=(t // tm,), in_specs=[row, vec, vec, row], out_specs=[row, vec, vec, vec],
        out_shape=[jax.ShapeDtypeStruct((t, d), F32), vshape, vshape, vshape],
        compiler_params=_params("arbitrary"),
    )(z, g, beta, target)


FFN_HALO = 16


def _ffn_mid_fwd(h, cw, cb, name, tm=512, tc=256):
    t, f2 = h.shape
    f = f2 // 2
    nj, nt, hb = f // tc, t // tm, tm // FFN_HALO

    def body(hg, hgp, hv, hvp, cwg, cwv, cbg, cbv, u_ref, sg, sv):
        i = pl.program_id(1)

        def conv(main, prev, s, w, b):
            s[0:FFN_HALO, :] = jnp.where(i > 0, prev[...].astype(F32), 0.0)
            s[FFN_HALO:, :] = main[...].astype(F32)
            o = FFN_HALO - FFN_KERNEL + 1
            return (w[0:1, :] * s[pl.ds(o, tm), :] + w[1:2, :] * s[pl.ds(o + 1, tm), :]
                    + w[2:3, :] * s[pl.ds(o + 2, tm), :] + b[...])

        cg = conv(hg, hgp, sg, cwg, cbg)
        cv = conv(hv, hvp, sv, cwv, cbv)
        u_ref[...] = (_gelu(cg) * cv).astype(BF16)

    def main_spec(off):
        return pl.BlockSpec((tm, tc), lambda j, i: (i, j + off))

    def prev_spec(off):
        return pl.BlockSpec((FFN_HALO, tc), lambda j, i: (jnp.maximum(i * hb - 1, 0), j + off))

    def par_spec(rows, off):
        return pl.BlockSpec((rows, tc), lambda j, i: (0, j + off))

    return pl.pallas_call(
        body, name=name, grid=(nj, nt),
        in_specs=[main_spec(0), prev_spec(0), main_spec(nj), prev_spec(nj),
                  par_spec(FFN_KERNEL, 0), par_spec(FFN_KERNEL, nj), par_spec(1, 0), par_spec(1, nj)],
        out_specs=pl.BlockSpec((tm, tc), lambda j, i: (i, j)),
        out_shape=jax.ShapeDtypeStruct((t, f), BF16),
        scratch_shapes=[pltpu.VMEM((tm + FFN_HALO, tc), F32), pltpu.VMEM((tm + FFN_HALO, tc), F32)],
        compiler_params=_params("parallel", "arbitrary"),
    )(h, h, h, h, cw, cw, cb, cb)


def _ffn_mid_bwd(h, du, cw, cb, name, tm=512, tc=256):
    t, f2 = h.shape
    f = f2 // 2
    nj, nt, hb = f // tc, t // tm, tm // FFN_HALO
    r = tm + FFN_HALO

    def body(hg, hgp, hgn, hv, hvp, hvn, du_ref, dun_ref, cwg, cwv, cbg, cbv,
             dhg_ref, dhv_ref, dcwg_ref, dcwv_ref, dcbg_ref, dcbv_ref, sg, sv, sdg, sdv):
        i = pl.program_id(1)

        @pl.when(i == 0)
        def _():
            for ref in (dcwg_ref, dcwv_ref, dcbg_ref, dcbv_ref):
                ref[...] = jnp.zeros_like(ref)

        o = FFN_HALO - FFN_KERNEL + 1

        def conv(main, prev, nxt, s, w, b):
            s[0:FFN_HALO, :] = jnp.where(i > 0, prev[...].astype(F32), 0.0)
            s[FFN_HALO:FFN_HALO + tm, :] = main[...].astype(F32)
            s[FFN_HALO + tm:, :] = nxt[...].astype(F32)
            return (w[0:1, :] * s[pl.ds(o, r), :] + w[1:2, :] * s[pl.ds(o + 1, r), :]
                    + w[2:3, :] * s[pl.ds(o + 2, r), :] + b[...])

        cg = conv(hg, hgp, hgn, sg, cwg, cbg)
        cv = conv(hv, hvp, hvn, sv, cwv, cbv)
        du_e = jnp.concatenate([du_ref[...].astype(F32), jnp.where(i < nt - 1, dun_ref[...].astype(F32), 0.0)], axis=0)
        gl, dgl = _gelu_and_grad(cg)
        sdg[...] = du_e * cv * dgl
        sdv[...] = du_e * gl

        def back(sd, s, w, dh_ref, dcw_ref, dcb_ref):
            own = sd[0:tm, :]
            dcb_ref[...] += _colsum(own)
            for k in range(FFN_KERNEL):
                dcw_ref[k:k + 1, :] += _colsum(own * s[pl.ds(o + k, tm), :])
            dh = w[2:3, :] * own + w[1:2, :] * sd[pl.ds(1, tm), :] + w[0:1, :] * sd[pl.ds(2, tm), :]
            dh_ref[...] = dh.astype(BF16)

        back(sdg, sg, cwg, dhg_ref, dcwg_ref, dcbg_ref)
        back(sdv, sv, cwv, dhv_ref, dcwv_ref, dcbv_ref)

    last_blk = t // FFN_HALO - 1

    def main_spec(off):
        return pl.BlockSpec((tm, tc), lambda j, i: (i, j + off))

    def prev_spec(off):
        return pl.BlockSpec((FFN_HALO, tc), lambda j, i: (jnp.maximum(i * hb - 1, 0), j + off))

    def next_spec(off):
        return pl.BlockSpec((FFN_HALO, tc), lambda j, i: (jnp.minimum((i + 1) * hb, last_blk), j + off))

    def par_spec(rows, off):
        return pl.BlockSpec((rows, tc), lambda j, i: (0, j + off))

    out_tile = pl.BlockSpec((tm, tc), lambda j, i: (i, j))
    return pl.pallas_call(
        body, name=name, grid=(nj, nt),
        in_specs=[main_spec(0), prev_spec(0), next_spec(0), main_spec(nj), prev_spec(nj), next_spec(nj),
                  main_spec(0), next_spec(0),
                  par_spec(FFN_KERNEL, 0), par_spec(FFN_KERNEL, nj), par_spec(1, 0), par_spec(1, nj)],
        out_specs=[out_tile, out_tile, par_spec(FFN_KERNEL, 0), par_spec(FFN_KERNEL, 0), par_spec(1, 0), par_spec(1, 0)],
        out_shape=[jax.ShapeDtypeStruct((t, f), BF16), jax.ShapeDtypeStruct((t, f), BF16),
                   jax.ShapeDtypeStruct((FFN_KERNEL, f), F32), jax.ShapeDtypeStruct((FFN_KERNEL, f), F32),
                   jax.ShapeDtypeStruct((1, f), F32), jax.ShapeDtypeStruct((1, f), F32)],
        scratch_shapes=[pltpu.VMEM((tm + 2 * FFN_HALO, tc), F32), pltpu.VMEM((tm + 2 * FFN_HALO, tc), F32),
                        pltpu.VMEM((r, tc), F32), pltpu.VMEM((r, tc), F32)],
        compiler_params=_params("parallel", "arbitrary"),
    )(h, h, h, h, h, h, du, du, cw, cw, cb, cb)


MIX_HALO = 32


def _glu(hh):
    return hh[:, 0:A_WIDTH] * _sigmoid(hh[:, A_WIDTH:2 * A_WIDTH])


def _tril_mask():
    return lax.broadcasted_iota(jnp.int32, (B_CHUNK, B_CHUNK), 0) >= lax.broadcasted_iota(jnp.int32, (B_CHUNK, B_CHUNK), 1)


def _spatial_mix(q, ms_ref, sbt_ref, tm):
    mask = _tril_mask()
    ws = [jnp.where(mask, ms_ref[g], 0.0).astype(BF16) for g in range(B_GROUPS)]
    qb = q.astype(BF16)
    rows = []
    for c in range(tm // B_CHUNK):
        cols = [_dot(ws[g], qb[c * B_CHUNK:(c + 1) * B_CHUNK, g * 128:(g + 1) * 128], NN) + sbt_ref[:, g:g + 1]
                for g in range(B_GROUPS)]
        rows.append(jnp.concatenate(cols, axis=1))
    return jnp.concatenate(rows, axis=0)


def _mixer_mid_fwd(h, cw, cb, ag, ab, bg, bb, ms, sbt, name, tm=256):
    t = h.shape[0]
    nt, hb = t // tm, tm // MIX_HALO
    o = MIX_HALO - A_KERNEL + 1

    def body(h_ref, hp_ref, cw_ref, cb_ref, ag_ref, ab_ref, bg_ref, bb_ref, ms_ref, sbt_ref, cat_ref, sp):
        i = pl.program_id(0)
        sp[0:MIX_HALO, :] = jnp.where(i > 0, _glu(hp_ref[:, 0:2 * A_WIDTH].astype(F32)), 0.0)
        sp[MIX_HALO:, :] = _glu(h_ref[:, 0:2 * A_WIDTH].astype(F32))
        y = jnp.zeros((tm, A_WIDTH), F32) + cb_ref[...]
        for k in range(A_KERNEL):
            y = y + cw_ref[k:k + 1, :] * sp[pl.ds(o + k, tm), :]
        nh, _ = _ln_stats(y)
        ln = nh * ag_ref[...] + ab_ref[...]
        cat_ref[:, 0:A_WIDTH] = (ln * _sigmoid(ln)).astype(BF16)
        u = _gelu(h_ref[:, 1024:1536].astype(F32))
        nb, _ = _ln_stats(_gelu(h_ref[:, 1536:2048].astype(F32)))
        mixed = _spatial_mix(nb * bg_ref[...] + bb_ref[...], ms_ref, sbt_ref, tm)
        cat_ref[:, A_WIDTH:] = (u * mixed).astype(BF16)

    vec = pl.BlockSpec((1, A_WIDTH), lambda i: (0, 0))
    return pl.pallas_call(
        body, name=name, grid=(nt,),
        in_specs=[pl.BlockSpec((tm, 2048), lambda i: (i, 0)),
                  pl.BlockSpec((MIX_HALO, 2048), lambda i: (jnp.maximum(i * hb - 1, 0), 0)),
                  pl.BlockSpec((A_KERNEL, A_WIDTH), lambda i: (0, 0)), vec, vec, vec, vec, vec,
                  pl.BlockSpec((B_GROUPS, B_CHUNK, B_CHUNK), lambda i: (0, 0, 0)),
                  pl.BlockSpec((B_CHUNK, B_GROUPS), lambda i: (0, 0))],
        out_specs=pl.BlockSpec((tm, D_MODEL), lambda i: (i, 0)),
        out_shape=jax.ShapeDtypeStruct((t, D_MODEL), BF16),
        scratch_shapes=[pltpu.VMEM((tm + MIX_HALO, A_WIDTH), F32)],
        compiler_params=_params("parallel"),
    )(h, h, cw, cb, ag, ab, bg, bb, ms, sbt)


def _mixer_mid_bwd(h, dcat, cw, cb, ag, ab, bg, bb, ms, mst, sbt, name, tm=256):
    t = h.shape[0]
    nt, hb = t // tm, tm // MIX_HALO
    o = MIX_HALO - A_KERNEL + 1
    r = tm + MIX_HALO
    nchunk = tm // B_CHUNK

    def body(h_ref, hp_ref, hn_ref, dc_ref, dcn_ref, cw_ref, cb_ref, ag_ref, ab_ref, bg_ref, bb_ref, ms_ref, mst_ref, sbt_ref,
             dh_ref, dcw_ref, dcb_ref, dag_ref, dab_ref, dbg_ref, dbb_ref, dms_ref, dsb_ref, sp, sdy, sbacc):
        i = pl.program_id(0)

        @pl.when(i == 0)
        def _():
            for ref in (dcw_ref, dcb_ref, dag_ref, dab_ref, dbg_ref, dbb_ref, dms_ref, dsb_ref, sbacc):
                ref[...] = jnp.zeros_like(ref)

        sp[0:MIX_HALO, :] = jnp.where(i > 0, _glu(hp_ref[:, 0:2 * A_WIDTH].astype(F32)), 0.0)
        sp[MIX_HALO:MIX_HALO + tm, :] = _glu(h_ref[:, 0:2 * A_WIDTH].astype(F32))
        sp[MIX_HALO + tm:, :] = _glu(hn_ref[:, 0:2 * A_WIDTH].astype(F32))
        y = jnp.zeros((r, A_WIDTH), F32) + cb_ref[...]
        for k in range(A_KERNEL):
            y = y + cw_ref[k:k + 1, :] * sp[pl.ds(o + k, r), :]
        nh, rs = _ln_stats(y)
        ln = nh * ag_ref[...] + ab_ref[...]
        sg = _sigmoid(ln)
        dao = jnp.concatenate([dc_ref[:, 0:A_WIDTH].astype(F32),
                               jnp.where(i < nt - 1, dcn_ref[:, 0:A_WIDTH].astype(F32), 0.0)], axis=0)
        dln = dao * (sg * (1.0 + ln * (1.0 - sg)))
        dag_ref[...] += _colsum(dln[0:tm] * nh[0:tm])
        dab_ref[...] += _colsum(dln[0:tm])
        sdy[...] = _ln_bwd_rows(dln * ag_ref[...], nh, rs)
        dy_own = sdy[0:tm, :]
        dcb_ref[...] += _colsum(dy_own)
        dp = jnp.zeros((tm, A_WIDTH), F32)
        for k in range(A_KERNEL):
            dcw_ref[k:k + 1, :] += _colsum(dy_own * sp[pl.ds(o + k, tm), :])
            dp = dp + cw_ref[k:k + 1, :] * sdy[pl.ds(A_KERNEL - 1 - k, tm), :]
        av = h_ref[:, 0:A_WIDTH].astype(F32)
        s = _sigmoid(h_ref[:, A_WIDTH:2 * A_WIDTH].astype(F32))
        dh_ref[:, 0:A_WIDTH] = (dp * s).astype(BF16)
        dh_ref[:, A_WIDTH:2 * A_WIDTH] = (dp * av * s * (1.0 - s)).astype(BF16)

        u, dgu = _gelu_and_grad(h_ref[:, 1024:1536].astype(F32))
        w, dgw = _gelu_and_grad(h_ref[:, 1536:2048].astype(F32))
        nb, rb = _ln_stats(w)
        q = nb * bg_ref[...] + bb_ref[...]
        mixed = _spatial_mix(q, ms_ref, sbt_ref, tm)
        dbo = dc_ref[:, A_WIDTH:].astype(F32)
        dh_ref[:, 1024:1536] = (dbo * mixed * dgu).astype(BF16)
        dmx = dbo * u
        mask = _tril_mask()
        wst = [jnp.where(mask.T, mst_ref[g], 0.0).astype(BF16) for g in range(B_GROUPS)]
        qb = q.astype(BF16)
        dmb = dmx.astype(BF16)
        rows = []
        for c in range(nchunk):
            cols = []
            for g in range(B_GROUPS):
                rs_, cs_ = slice(c * B_CHUNK, (c + 1) * B_CHUNK), slice(g * 128, (g + 1) * 128)
                sbacc[g] += dmx[rs_, cs_]
                dms_ref[g] += _dot(dmb[rs_, cs_], qb[rs_, cs_], NT)
                cols.append(_dot(wst[g], dmb[rs_, cs_], NN))
            rows.append(jnp.concatenate(cols, axis=1))
        dq = jnp.concatenate(rows, axis=0)
        dbg_ref[...] += _colsum(dq * nb)
        dbb_ref[...] += _colsum(dq)
        dh_ref[:, 1536:2048] = (_ln_bwd_rows(dq * bg_ref[...], nb, rb) * dgw).astype(BF16)

        @pl.when(i == nt - 1)
        def _():
            for g in range(B_GROUPS):
                dms_ref[g] = jnp.where(mask, dms_ref[g], 0.0)
                dsb_ref[g] = jnp.sum(sbacc[g], axis=1, keepdims=True)

    last_blk = t // MIX_HALO - 1
    vec = pl.BlockSpec((1, A_WIDTH), lambda i: (0, 0))
    mat = pl.BlockSpec((B_GROUPS, B_CHUNK, B_CHUNK), lambda i: (0, 0, 0))
    taps = pl.BlockSpec((A_KERNEL, A_WIDTH), lambda i: (0, 0))

    def halo(width, which):
        if which == "prev":
            return pl.BlockSpec((MIX_HALO, width), lambda i: (jnp.maximum(i * hb - 1, 0), 0))
        return pl.BlockSpec((MIX_HALO, width), lambda i: (jnp.minimum((i + 1) * hb, last_blk), 0))

    vshape = jax.ShapeDtypeStruct((1, A_WIDTH), F32)
    return pl.pallas_call(
        body, name=name, grid=(nt,),
        in_specs=[pl.BlockSpec((tm, 2048), lambda i: (i, 0)), halo(2048, "prev"), halo(2048, "next"),
                  pl.BlockSpec((tm, D_MODEL), lambda i: (i, 0)), halo(D_MODEL, "next"),
                  taps, vec, vec, vec, vec, vec, mat, mat, pl.BlockSpec((B_CHUNK, B_GROUPS), lambda i: (0, 0))],
        out_specs=[pl.BlockSpec((tm, 2048), lambda i: (i, 0)), taps, vec, vec, vec, vec, vec, mat,
                   pl.BlockSpec((B_GROUPS, B_CHUNK, 1), lambda i: (0, 0, 0))],
        out_shape=[jax.ShapeDtypeStruct((t, 2048), BF16), jax.ShapeDtypeStruct((A_KERNEL, A_WIDTH), F32),
                   vshape, vshape, vshape, vshape, vshape,
                   jax.ShapeDtypeStruct((B_GROUPS, B_CHUNK, B_CHUNK), F32), jax.ShapeDtypeStruct((B_GROUPS, B_CHUNK, 1), F32)],
        scratch_shapes=[pltpu.VMEM((tm + 2 * MIX_HALO, A_WIDTH), F32), pltpu.VMEM((r, A_WIDTH), F32),
                        pltpu.VMEM((B_GROUPS, B_CHUNK, B_CHUNK), F32)],
        compiler_params=_params("arbitrary"),
    )(h, h, h, dcat, dcat, cw, cb, ag, ab, bg, bb, ms, mst, sbt)


Q_WIDTH = N_Q_HEADS * HEAD_DIM
KV_WIDTH = 2 * N_KV_HEADS * HEAD_DIM
PAIRS_PER_KV = N_Q_HEADS // N_KV_HEADS // 2
ATT_SCALE = 1.0 / math.sqrt(HEAD_DIM)


def _att_mask(n):
    qi = lax.broadcasted_iota(jnp.int32, (ATT_BLOCK, 2 * ATT_BLOCK), 0)
    sj = lax.broadcasted_iota(jnp.int32, (ATT_BLOCK, 2 * ATT_BLOCK), 1)
    diff = qi + ATT_BLOCK - sj
    return (diff >= 0) & (diff < ATT_BLOCK) & ((n > 0) | (sj >= ATT_BLOCK))


def _dup_heads(pair_cols, kv_head):
    lane = lax.broadcasted_iota(jnp.int32, pair_cols.shape, 1)
    rolled = pltpu.roll(pair_cols, HEAD_DIM, 1)
    first = lane < HEAD_DIM
    return jnp.where(first, pair_cols, rolled) if kv_head == 0 else jnp.where(first, rolled, pair_cols)


def _att_probs(qm, k2, mask, sink):
    s = _dot(qm, k2, NT) * ATT_SCALE
    s = jnp.where(mask, s, -jnp.inf)
    m = jnp.maximum(jnp.max(s, axis=-1, keepdims=True), sink)
    e = jnp.exp(s - m)
    es = jnp.exp(sink - m)
    inv = 1.0 / (jnp.sum(e, axis=-1, keepdims=True) + es)
    return e * inv, es * inv


def _attn_fwd(qkv, sinks, name):
    t = qkv.shape[0]
    nb = t // ATT_BLOCK
    kvb = Q_WIDTH // KV_WIDTH

    def body(sink_ref, q_ref, kv_ref, kvp_ref, o_ref):
        n = pl.program_id(0)
        mask = _att_mask(n)
        kv = jnp.concatenate([kvp_ref[...], kv_ref[...]], axis=0).astype(F32)
        lane = lax.broadcasted_iota(jnp.int32, (ATT_BLOCK, 128), 1)
        for kh in range(N_KV_HEADS):
            k2 = _dup_heads(kv[:, 0:128], kh).astype(BF16)
            v2 = _dup_heads(kv[:, 128:256], kh).astype(BF16)
            for pr in range(PAIRS_PER_KV):
                c0 = (kh * PAIRS_PER_KV + pr) * 128
                q2 = q_ref[:, c0:c0 + 128]
                outs = []
                for half in range(2):
                    head = (kh * PAIRS_PER_KV + pr) * 2 + half
                    qm = jnp.where((lane < HEAD_DIM) == (half == 0), q2, jnp.zeros_like(q2))
                    p, _ = _att_probs(qm, k2, mask, sink_ref[0, head])
                    outs.append(_dot(p, v2, NN))
                o_ref[:, c0:c0 + 128] = jnp.where(lane < HEAD_DIM, outs[0], outs[1]).astype(BF16)

    return pl.pallas_call(
        body, name=name, grid=(nb,),
        in_specs=[pl.BlockSpec(memory_space=pltpu.SMEM),
                  pl.BlockSpec((ATT_BLOCK, Q_WIDTH), lambda n: (n, 0)),
                  pl.BlockSpec((ATT_BLOCK, KV_WIDTH), lambda n: (n, kvb)),
                  pl.BlockSpec((ATT_BLOCK, KV_WIDTH), lambda n: (jnp.maximum(n - 1, 0), kvb))],
        out_specs=pl.BlockSpec((ATT_BLOCK, Q_WIDTH), lambda n: (n, 0)),
        out_shape=jax.ShapeDtypeStruct((t, Q_WIDTH), BF16),
        compiler_params=_params("parallel"),
    )(sinks, qkv, qkv, qkv)


def _attn_bwd(qkv, d_o, sinks, name):
    t = qkv.shape[0]
    nb = t // ATT_BLOCK
    kvb = Q_WIDTH // KV_WIDTH

    def body(sink_ref, q_ref, kv_ref, kvp_ref, do_ref, dq_ref, dkv_ref, dbq_ref, dbkv_ref, dsink_ref, carry):
        n = pl.program_id(0)

        @pl.when(n == 0)
        def _():
            for ref in (dbq_ref, dbkv_ref, dsink_ref, carry):
                ref[...] = jnp.zeros_like(ref)
            dkv_ref[...] = jnp.zeros_like(dkv_ref)

        @pl.when(n < nb)
        def _():
            mask = _att_mask(n)
            kv = jnp.concatenate([kvp_ref[...], kv_ref[...]], axis=0).astype(F32)
            lane = lax.broadcasted_iota(jnp.int32, (ATT_BLOCK, 128), 1)
            lane2 = lax.broadcasted_iota(jnp.int32, (2 * ATT_BLOCK, 128), 1)
            sink_lane = lax.broadcasted_iota(jnp.int32, (1, 128), 1)
            dsink = jnp.zeros((1, 128), F32)
            dk_parts, dv_parts = [], []
            for kh in range(N_KV_HEADS):
                k2 = _dup_heads(kv[:, 0:128], kh).astype(BF16)
                v2 = _dup_heads(kv[:, 128:256], kh).astype(BF16)
                dk_acc = jnp.zeros((2 * ATT_BLOCK, 128), F32)
                dv_acc = jnp.zeros((2 * ATT_BLOCK, 128), F32)
                for pr in range(PAIRS_PER_KV):
                    c0 = (kh * PAIRS_PER_KV + pr) * 128
                    q2 = q_ref[:, c0:c0 + 128]
                    do2 = do_ref[:, c0:c0 + 128]
                    dqs = []
                    for half in range(2):
                        head = (kh * PAIRS_PER_KV + pr) * 2 + half
                        sel = (lane < HEAD_DIM) == (half == 0)
                        qm = jnp.where(sel, q2, jnp.zeros_like(q2))
                        dom = jnp.where(sel, do2, jnp.zeros_like(do2))
                        p, ps = _att_probs(qm, k2, mask, sink_ref[0, head])
                        dp = _dot(dom, v2, NT)
                        delta = jnp.sum(p * dp, axis=-1, keepdims=True)
                        ds = p * (dp - delta) * ATT_SCALE
                        dsink = dsink + jnp.where(sink_lane == head, -jnp.sum(ps * delta), 0.0)
                        dqs.append(_dot(ds, k2, NN))
                        dk_acc = dk_acc + _dot(ds, qm, TN)
                        dv_acc = dv_acc + _dot(p, dom, TN)
                    dq2 = jnp.where(lane < HEAD_DIM, dqs[0], dqs[1])
                    dbq_ref[:, c0:c0 + 128] += _colsum(dq2)
                    dq_ref[:, c0:c0 + 128] = dq2.astype(BF16)
                dk_parts.append(dk_acc + pltpu.roll(dk_acc, HEAD_DIM, 1))
                dv_parts.append(dv_acc + pltpu.roll(dv_acc, HEAD_DIM, 1))
            dk = jnp.where(lane2 < HEAD_DIM, dk_parts[0], dk_parts[1])
            dv = jnp.where(lane2 < HEAD_DIM, dv_parts[0], dv_parts[1])
            dkv_new = jnp.concatenate([dk, dv], axis=1)
            done = carry[...] + dkv_new[0:ATT_BLOCK]

            @pl.when(n > 0)
            def _():
                dkv_ref[...] = done.astype(BF16)
                dbkv_ref[...] += _colsum(done)

            carry[...] = dkv_new[ATT_BLOCK:]
            dsink_ref[...] += dsink

        @pl.when(n == nb)
        def _():
            dkv_ref[...] = carry[...].astype(BF16)
            dbkv_ref[...] += _colsum(carry[...])

    def clamp(n):
        return jnp.minimum(n, nb - 1)

    return pl.pallas_call(
        body, name=name, grid=(nb + 1,),
        in_specs=[pl.BlockSpec(memory_space=pltpu.SMEM),
                  pl.BlockSpec((ATT_BLOCK, Q_WIDTH), lambda n: (clamp(n), 0)),
                  pl.BlockSpec((ATT_BLOCK, KV_WIDTH), lambda n: (clamp(n), kvb)),
                  pl.BlockSpec((ATT_BLOCK, KV_WIDTH), lambda n: (jnp.maximum(clamp(n) - 1, 0), kvb)),
                  pl.BlockSpec((ATT_BLOCK, Q_WIDTH), lambda n: (clamp(n), 0))],
        out_specs=[pl.BlockSpec((ATT_BLOCK, Q_WIDTH), lambda n: (clamp(n), 0)),
                   pl.BlockSpec((ATT_BLOCK, KV_WIDTH), lambda n: (jnp.maximum(n - 1, 0), 0)),
                   pl.BlockSpec((1, Q_WIDTH), lambda n: (0, 0)),
                   pl.BlockSpec((1, KV_WIDTH), lambda n: (0, 0)),
                   pl.BlockSpec((1, 128), lambda n: (0, 0))],
        out_shape=[jax.ShapeDtypeStruct((t, Q_WIDTH), BF16), jax.ShapeDtypeStruct((t, KV_WIDTH), BF16),
                   jax.ShapeDtypeStruct((1, Q_WIDTH), F32), jax.ShapeDtypeStruct((1, KV_WIDTH), F32),
                   jax.ShapeDtypeStruct((1, 128), F32)],
        scratch_shapes=[pltpu.VMEM((ATT_BLOCK, KV_WIDTH), F32)],
        compiler_params=_params("arbitrary"),
    )(sinks, qkv, qkv, qkv, d_o)


def _adamw_math(g, w, m, v):
    m = ADAM_B1 * m + (1.0 - ADAM_B1) * g
    v = ADAM_B2 * v + (1.0 - ADAM_B2) * (g * g)
    m_hat = m / (1.0 - ADAM_B1 ** ADAM_STEP)
    v_hat = v / (1.0 - ADAM_B2 ** ADAM_STEP)
    delta = -ADAM_LR * (m_hat / (jnp.sqrt(v_hat) + ADAM_EPS) + ADAM_WD * w)
    return delta, m, v


def _sum_partials(p_ref):
    g = p_ref[0].astype(F32)
    for s in range(1, N_DEV):
        g = g + p_ref[s].astype(F32)
    return g


def _adamw_big(parts, w, m, v, name, tr):
    r, c = w.shape
    assert r % tr == 0

    def body(p_ref, w_ref, m_ref, v_ref, g_out, d_out, m_out, v_out):
        g = _sum_partials(p_ref)
        g_out[...] = g
        d_out[...], m_out[...], v_out[...] = _adamw_math(g, w_ref[...], m_ref[...], v_ref[...])

    tile = pl.BlockSpec((tr, c), lambda i: (i, 0))
    shape = jax.ShapeDtypeStruct((r, c), F32)
    return pl.pallas_call(
        body, name=name, grid=(r // tr,),
        in_specs=[pl.BlockSpec((N_DEV, tr, c), lambda i: (0, i, 0)), tile, tile, tile],
        out_specs=[tile] * 4, out_shape=[shape] * 4,
        compiler_params=_params("parallel"),
    )(parts, w, m, v)


def _adamw_small(parts, ws, ms, vs, name):
    n = len(ws)

    def body(*refs):
        ins, outs = refs[:4 * n], refs[4 * n:]
        for a in range(n):
            g = _sum_partials(ins[a])
            outs[4 * a][...] = g
            outs[4 * a + 1][...], outs[4 * a + 2][...], outs[4 * a + 3][...] = _adamw_math(
                g, ins[n + a][...], ins[2 * n + a][...], ins[3 * n + a][...])

    out_shape = []
    for w in ws:
        out_shape += [jax.ShapeDtypeStruct(w.shape, F32)] * 4
    return pl.pallas_call(body, name=name, out_shape=out_shape, compiler_params=_params())(*parts, *ws, *ms, *vs)


ANY = pl.BlockSpec(memory_space=pl.ANY)


def _my_place():
    return lax.axis_index("x"), lax.axis_index("y"), lax.axis_index("c")


def _all_gather(arrs, name):
    n = len(arrs)

    def body(*refs):
        ins, outs = refs[:n], refs[n:2 * n]
        send_sems, recv_sems, local_sems = refs[2 * n:]
        x, y, c = _my_place()
        me, sibling = (x, y, c), (x, y, 1 - c)
        chips = [(1 - x, y), (x, 1 - y), (1 - x, 1 - y)]

        def block(a, place):
            px, py, pc = place
            return outs[a].at[4 * px + 2 * py + pc]

        def copy(a, k, place, to, src=None):
            return pltpu.make_async_remote_copy(
                src_ref=block(a, place) if src is None else src, dst_ref=block(a, place),
                send_sem=send_sems.at[a, k], recv_sem=recv_sems.at[a, k], device_id=to, device_id_type=MESH_ID)

        started = []
        for a in range(n):
            mine = pltpu.make_async_copy(ins[a], block(a, me), local_sems.at[a])
            mine.start()
            started.append(mine)
        first = []
        for a in range(n):
            first.append(copy(a, 0, me, sibling, src=ins[a]))
            first += [copy(a, 1 + j, me, (*chip, c), src=ins[a]) for j, chip in enumerate(chips)]
        for cp in first:
            cp.start()
        passed = []
        for j, chip in enumerate(chips):
            for a in range(n):
                copy(a, 1 + j, (*chip, c), me).wait_recv()
                fwd = copy(a, 4 + j, (*chip, c), sibling)
                fwd.start()
                passed.append(fwd)
        for a in range(n):
            copy(a, 0, sibling, me).wait_recv()
            for j, chip in enumerate(chips):
                copy(a, 4 + j, (*chip, 1 - c), me).wait_recv()
        for cp in first + passed:
            cp.wait_send()
        for mine in started:
            mine.wait()

    return pl.pallas_call(
        body, name=name, in_specs=[ANY] * n, out_specs=[ANY] * n,
        out_shape=[jax.ShapeDtypeStruct((N_DEV,) + a.shape, a.dtype) for a in arrs],
        scratch_shapes=[pltpu.SemaphoreType.DMA((n, N_DEV - 1)), pltpu.SemaphoreType.DMA((n, N_DEV - 1)),
                        pltpu.SemaphoreType.DMA((n,))],
    )(*arrs)


def _all_to_all(arrs, name):
    n = len(arrs)

    def body(*refs):
        ins, outs = refs[:n], refs[n:2 * n]
        send_sems, recv_sems, local_sems = refs[2 * n:]
        x, y, c = _my_place()
        me = 4 * x + 2 * y + c
        copies = []
        for a in range(n):
            mine = pltpu.make_async_copy(ins[a].at[me], outs[a].at[me], local_sems.at[a])
            mine.start()
            copies.append(mine)
        remote = []
        for k in range(1, N_DEV):
            peer = (x ^ (k >> 2), y ^ ((k >> 1) & 1), c ^ (k & 1))
            peer_id = me ^ k
            for a in range(n):
                cp = pltpu.make_async_remote_copy(
                    src_ref=ins[a].at[peer_id], dst_ref=outs[a].at[me],
                    send_sem=send_sems.at[a, k - 1], recv_sem=recv_sems.at[a, k - 1], device_id=peer, device_id_type=MESH_ID)
                cp.start()
                remote.append((cp, a, k, peer, peer_id))
        for cp, a, k, peer, peer_id in remote:
            pltpu.make_async_remote_copy(
                src_ref=ins[a].at[peer_id], dst_ref=outs[a].at[peer_id],
                send_sem=send_sems.at[a, k - 1], recv_sem=recv_sems.at[a, k - 1], device_id=peer, device_id_type=MESH_ID).wait_recv()
        for cp, *_ in remote:
            cp.wait_send()
        for mine in copies:
            mine.wait()

    return pl.pallas_call(
        body, name=name, in_specs=[ANY] * n, out_specs=[ANY] * n,
        out_shape=[jax.ShapeDtypeStruct(a.shape, a.dtype) for a in arrs],
        scratch_shapes=[pltpu.SemaphoreType.DMA((n, N_DEV - 1)), pltpu.SemaphoreType.DMA((n, N_DEV - 1)),
                        pltpu.SemaphoreType.DMA((n,))],
    )(*arrs)


PACK_LANES = 128
PACK_ROWS = 8


def _pack(arrs):
    flat = jnp.concatenate([a.reshape(-1).astype(F32) for a in arrs])
    unit = PACK_LANES * PACK_ROWS
    total = -(-flat.shape[0] // unit) * unit
    return jnp.pad(flat, (0, total - flat.shape[0])).reshape(-1, PACK_LANES)


def _unpack(buf, shapes):
    flat = buf.reshape(N_DEV, -1)
    out, pos = [], 0
    for s in shapes:
        size = math.prod(s)
        out.append(flat[:, pos:pos + size].reshape((N_DEV,) + tuple(s)))
        pos += size
    return out


def _interleave(g):
    return jnp.transpose(g, (1, 0, 2)).reshape(g.shape[1], -1)


def _deinterleave(w):
    r = w.shape[0]
    return jnp.transpose(w.reshape(r, N_DEV, -1), (1, 0, 2))


def _ffn_forward(x_in, w_up, cw, cb, w_down, g, b, tag):
    h = _matmul(x_in, w_up, "nn", BF16, f"ffn{tag}_up", 1024, 512, 1024)
    u = _ffn_mid_fwd(h, cw, cb, f"ffn{tag}_mid_fwd")
    z, x_out = _matmul_res_ln(u, w_down, x_in, g, b, f"ffn{tag}_down_ln", 512, D_FF)
    return h, u, z, x_out


def _ffn_backward(dz, x_in, h, u, w_up, cw, cb, w_down, tag):
    du = _matmul(dz, w_down, "nt", BF16, f"ffn{tag}_du", 1024, 1408, 1024)
    d_w_down = _matmul(u, dz, "tn", BF16, f"ffn{tag}_dwdown", 1408, 1024, 512)
    dhg, dhv, dcwg, dcwv, dcbg, dcbv = _ffn_mid_bwd(h, du, cw, cb, f"ffn{tag}_mid_bwd")
    d_w_up = jnp.concatenate([_matmul(x_in, dhg, "tn", BF16, f"ffn{tag}_dwup_gate", 1024, 1408, 512),
                              _matmul(x_in, dhv, "tn", BF16, f"ffn{tag}_dwup_value", 1024, 1408, 512)], axis=1)
    dx = _matmul(dhv, w_up, "nt", F32, f"ffn{tag}_dx_value", 1024, 1024, 1408, b_off=D_FF // 1408)
    dx = _matmul(dhg, w_up, "nt", F32, f"ffn{tag}_dx_gate", 1024, 1024, 1408, res=dx)
    return dx, d_w_up, d_w_down, jnp.concatenate([dcwg, dcwv], axis=1), jnp.concatenate([dcbg, dcbv], axis=1)


def kernel(x, ab_w_in, a_conv_w, a_conv_b, a_norm_g, a_norm_b, b_norm_g, b_norm_b, b_spatial_w, b_spatial_b, ab_w_out, c_w_qkv, c_b_qkv, c_sinks, c_w_o, ffn_w_up, ffn_conv_w, ffn_conv_b, ffn_w_down, ln_g, ln_b, loss_target, m_ab_w_in, m_a_conv_w, m_a_conv_b, m_a_norm_g, m_a_norm_b, m_b_norm_g, m_b_norm_b, m_b_spatial_w, m_b_spatial_b, m_ab_w_out, m_c_w_qkv, m_c_b_qkv, m_c_sinks, m_c_w_o, m_ffn_w_up, m_ffn_conv_w, m_ffn_conv_b, m_ffn_w_down, m_ln_g, m_ln_b, v_ab_w_in, v_a_conv_w, v_a_conv_b, v_a_norm_g, v_a_norm_b, v_b_norm_g, v_b_norm_b, v_b_spatial_w, v_b_spatial_b, v_ab_w_out, v_c_w_qkv, v_c_b_qkv, v_c_sinks, v_c_w_o, v_ffn_w_up, v_ffn_conv_w, v_ffn_conv_b, v_ffn_w_down, v_ln_g, v_ln_b):
    me = 4 * lax.axis_index("x") + 2 * lax.axis_index("y") + lax.axis_index("c")
    xt = x[0]
    t = xt.shape[0]

    small_shard_shapes = [a_conv_w.shape, c_b_qkv.shape, ffn_conv_w.shape, ln_g.shape, ln_b.shape]
    gathered = _all_gather(
        [ab_w_in[0].astype(BF16), ab_w_out[0].astype(BF16), c_w_qkv[0].astype(BF16), c_w_o[0].astype(BF16),
         ffn_w_up.astype(BF16), ffn_w_down.astype(BF16), _pack([a_conv_w, c_b_qkv, ffn_conv_w, ln_g, ln_b])],
        "gather_weights")
    w_in = _interleave(gathered[0])
    w_out = gathered[1].reshape(D_MODEL, D_MODEL)
    w_qkv = _interleave(gathered[2])
    w_o = gathered[3].reshape(D_MODEL, D_MODEL)
    w_up = [_interleave(gathered[4][:, l]) for l in range(2)]
    w_down = [gathered[5][:, l].reshape(D_FF, D_MODEL) for l in range(2)]
    g_acw, g_bqkv, g_fcw, g_lng, g_lnb = _unpack(gathered[6], small_shard_shapes)
    acw = _interleave(g_acw[:, 0])
    bqkv = g_bqkv[:, 0].reshape(1, -1)
    fcw = [_interleave(g_fcw[:, l]) for l in range(2)]
    lng = jnp.transpose(g_lng, (1, 2, 0, 3)).reshape(2, 2, 1, D_MODEL)
    lnb = jnp.transpose(g_lnb, (1, 2, 0, 3)).reshape(2, 2, 1, D_MODEL)
    fcb = [ffn_conv_b[l:l + 1] for l in range(2)]
    ms = b_spatial_w[0]
    mst = jnp.swapaxes(ms, 1, 2)
    sbt = b_spatial_b[0].T

    h0 = _matmul(xt, w_in, "nn", BF16, "mix_in", 1024, 512, 1024)
    cat = _mixer_mid_fwd(h0, acw, a_conv_b, a_norm_g, a_norm_b, b_norm_g, b_norm_b, ms, sbt, "mix_mid_fwd")
    z1, x1 = _matmul_res_ln(cat, w_out, xt, lng[0, 0], lnb[0, 0], "mix_out_ln", 512, D_MODEL)
    hf0, u0, z2, x2 = _ffn_forward(x1, w_up[0], fcw[0], fcb[0], w_down[0], lng[0, 1], lnb[0, 1], 0)
    qkv = _matmul(x2, w_qkv, "nn", BF16, "att_qkv", 1024, 1280, 1024, bias=bqkv)
    att = _attn_fwd(qkv, c_sinks, "att_fwd")
    z3, x3 = _matmul_res_ln(att, w_o, x2, lng[1, 0], lnb[1, 0], "att_out_ln", 512, D_MODEL)
    hf1, u1, z4, _ = _ffn_forward(x3, w_up[1], fcw[1], fcb[1], w_down[1], lng[1, 1], lnb[1, 1], 1)

    dz4, dg11, db11, loss_terms = _ln_bwd_loss(z4, lng[1, 1], lnb[1, 1], loss_target[0], "loss_ln_bwd")
    dx3, d_wup1, d_wdown1, d_fcw1, d_fcb1 = _ffn_backward(dz4, x3, hf1, u1, w_up[1], fcw[1], fcb[1], w_down[1], 1)
    dz3, dg10, db10 = _ln_bwd(z3, lng[1, 0], dz4, dx3, "ln10_bwd")
    d_att = _matmul(dz3, w_o, "nt", BF16, "att_dout", 1024, 1024, 1024)
    d_wo = _matmul(att, dz3, "tn", BF16, "att_dwo", 1024, 1024, 512)
    dq, dkv, dbq, dbkv, dsinks = _attn_bwd(qkv, d_att, c_sinks, "att_bwd")
    d_wqkv = jnp.concatenate([_matmul(x2, dq, "tn", BF16, "att_dwq", 1024, 1024, 512),
                              _matmul(x2, dkv, "tn", BF16, "att_dwkv", 1024, KV_WIDTH, 512)], axis=1)
    dx2 = _matmul(dkv, w_qkv, "nt", F32, "att_dx_kv", 1024, 1024, KV_WIDTH, b_off=Q_WIDTH // KV_WIDTH)
    dx2 = _matmul(dq, w_qkv, "nt", F32, "att_dx_q", 1024, 1024, KV_WIDTH, res=dx2)
    dz2, dg01, db01 = _ln_bwd(z2, lng[0, 1], dz3, dx2, "ln01_bwd")
    dx1, d_wup0, d_wdown0, d_fcw0, d_fcb0 = _ffn_backward(dz2, x1, hf0, u0, w_up[0], fcw[0], fcb[0], w_down[0], 0)
    dz1, dg00, db00 = _ln_bwd(z1, lng[0, 0], dz2, dx1, "ln00_bwd")
    dcat = _matmul(dz1, w_out, "nt", BF16, "mix_dcat", 1024, 1024, 1024)
    d_wout = _matmul(cat, dz1, "tn", BF16, "mix_dwout", 1024, 1024, 512)
    dh0, d_acw, d_acb, d_ang, d_anb, d_bng, d_bnb, d_ms, d_sb = _mixer_mid_bwd(
        h0, dcat, acw, a_conv_b, a_norm_g, a_norm_b, b_norm_g, b_norm_b, ms, mst, sbt, "mix_mid_bwd")
    d_win = _matmul(xt, dh0, "tn", BF16, "mix_dwin", 1024, 1024, 512)
    grad_x = _matmul(dh0, w_in, "nt", F32, "mix_dx", 1024, 1024, 1024, res=dz1, res_scale=ALPHA)

    loss = lax.psum(0.5 / D_MODEL * jnp.sum(loss_terms), ("x", "y", "c"))

    parts = _all_to_all(
        [_deinterleave(d_win), d_wout.reshape(N_DEV, -1, D_MODEL), _deinterleave(d_wqkv), d_wo.reshape(N_DEV, -1, D_MODEL),
         jnp.concatenate([_deinterleave(d_wup0), _deinterleave(d_wup1)], axis=1),
         jnp.concatenate([d_wdown0.reshape(N_DEV, -1, D_MODEL), d_wdown1.reshape(N_DEV, -1, D_MODEL)], axis=1)],
        "exchange_weight_grads")
    big = {}
    for nm, p, w, m, v, tr in [
            ("ab_w_in", parts[0], ab_w_in, m_ab_w_in, v_ab_w_in, 256), ("ab_w_out", parts[1], ab_w_out, m_ab_w_out, v_ab_w_out, 128),
            ("c_w_qkv", parts[2], c_w_qkv, m_c_w_qkv, v_c_w_qkv, 256), ("c_w_o", parts[3], c_w_o, m_c_w_o, v_c_w_o, 128),
            ("ffn_w_up", parts[4], ffn_w_up, m_ffn_w_up, v_ffn_w_up, 256), ("ffn_w_down", parts[5], ffn_w_down, m_ffn_w_down, v_ffn_w_down, 176)]:
        two_d = (-1, w.shape[-1])
        outs = _adamw_big(p, w.reshape(two_d), m.reshape(two_d), v.reshape(two_d), "adamw_" + nm, tr)
        big[nm] = [o.reshape(w.shape) for o in outs]

    d_bqkv = jnp.concatenate([dbq, dbkv], axis=1)
    d_lng = jnp.stack([jnp.stack([dg00, dg01]), jnp.stack([dg10, dg11])])
    d_lnb = jnp.stack([jnp.stack([db00, db01]), jnp.stack([db10, db11])])
    small_full = [d_acb, d_ang, d_anb, d_bng, d_bnb, d_ms, d_sb, dsinks[:, :N_Q_HEADS], jnp.concatenate([d_fcb0, d_fcb1], axis=0),
                  d_acw, d_bqkv, jnp.stack([d_fcw0, d_fcw1]), d_lng, d_lnb]
    gs = _unpack(_all_gather([_pack(small_full)], "gather_small_grads")[0], [a.shape for a in small_full])

    def my_shard(g, width):
        g = g.reshape(g.shape[:-1] + (N_DEV, width))
        return lax.dynamic_index_in_dim(g, me, axis=g.ndim - 2, keepdims=False)

    small_names = ["a_conv_b", "a_norm_g", "a_norm_b", "b_norm_g", "b_norm_b", "b_spatial_w", "b_spatial_b", "c_sinks", "ffn_conv_b",
                   "a_conv_w", "c_b_qkv", "ffn_conv_w", "ln_g", "ln_b"]
    small_w = [a_conv_b, a_norm_g, a_norm_b, b_norm_g, b_norm_b, b_spatial_w, b_spatial_b, c_sinks, ffn_conv_b,
               a_conv_w, c_b_qkv, ffn_conv_w, ln_g, ln_b]
    small_m = [m_a_conv_b, m_a_norm_g, m_a_norm_b, m_b_norm_g, m_b_norm_b, m_b_spatial_w, m_b_spatial_b, m_c_sinks, m_ffn_conv_b,
               m_a_conv_w, m_c_b_qkv, m_ffn_conv_w, m_ln_g, m_ln_b]
    small_v = [v_a_conv_b, v_a_norm_g, v_a_norm_b, v_b_norm_g, v_b_norm_b, v_b_spatial_w, v_b_spatial_b, v_c_sinks, v_ffn_conv_b,
               v_a_conv_w, v_c_b_qkv, v_ffn_conv_w, v_ln_g, v_ln_b]
    gs[9:] = [my_shard(g, w.shape[-1]) for g, w in zip(gs[9:], small_w[9:])]
    two_d = [(-1, w.shape[-1]) for w in small_w]
    outs = _adamw_small([g.reshape((N_DEV,) + w.reshape(s).shape) for g, w, s in zip(gs, small_w, two_d)],
                        [w.reshape(s) for w, s in zip(small_w, two_d)], [m.reshape(s) for m, s in zip(small_m, two_d)],
                        [v.reshape(s) for v, s in zip(small_v, two_d)], "adamw_small")
    small = {nm: [o.reshape(w.shape) for o in outs[4 * a:4 * a + 4]] for a, (nm, w) in enumerate(zip(small_names, small_w))}

    res = {**big, **small}
    order = ["ab_w_in", "a_conv_w", "a_conv_b", "a_norm_g", "a_norm_b", "b_norm_g", "b_norm_b", "b_spatial_w", "b_spatial_b", "ab_w_out",
             "c_w_qkv", "c_b_qkv", "c_sinks", "c_w_o", "ffn_w_up", "ffn_conv_w", "ffn_conv_b", "ffn_w_down", "ln_g", "ln_b"]
    return (loss, grad_x[None], *[res[nm][0] for nm in order], *[res[nm][1] for nm in order],
            *[res[nm][2] for nm in order], *[res[nm][3] for nm in order])
```

```python
import functools
import math

import jax
import jax.numpy as jnp
from jax import lax
from jax.experimental import pallas as pl
from jax.experimental.pallas import tpu as pltpu

F32 = jnp.float32
BF16 = jnp.bfloat16

N_DEV = 8
D_MODEL = 1024
A_WIDTH = 512
A_KERNEL = 31
B_GROUPS = 4
B_CHUNK = 128
HEAD_DIM = 64
N_Q_HEADS = 16
N_KV_HEADS = 2
ATT_BLOCK = 128
D_FF = 2816
FFN_KERNEL = 3
ALPHA = (2.0 * 2) ** 0.25
LN_EPS = 1e-5
GELU_K = math.sqrt(2.0 / math.pi)
GELU_C = 0.044715
ADAM_LR = 0.001
ADAM_B1 = 0.9
ADAM_B2 = 0.999
ADAM_EPS = 1e-08
ADAM_WD = 0.01
ADAM_STEP = 10
VMEM_LIMIT = 56 * 1024 * 1024
MESH_ID = pl.DeviceIdType.MESH


def _params(*sem):
    return pltpu.CompilerParams(dimension_semantics=sem, vmem_limit_bytes=VMEM_LIMIT)


def _gelu(x):
    t = jnp.tanh(GELU_K * x * (1.0 + GELU_C * x * x))
    return 0.5 * x * (1.0 + t)


def _gelu_and_grad(x):
    x2 = x * x
    t = jnp.tanh(GELU_K * x * (1.0 + GELU_C * x2))
    g = 0.5 * x * (1.0 + t)
    dg = 0.5 * (1.0 + t) + 0.5 * x * (1.0 - t * t) * (GELU_K * (1.0 + 3.0 * GELU_C * x2))
    return g, dg


def _sigmoid(x):
    return 1.0 / (1.0 + jnp.exp(-x))


def _ln_stats(z):
    mu = jnp.mean(z, axis=-1, keepdims=True)
    zc = z - mu
    var = jnp.mean(zc * zc, axis=-1, keepdims=True)
    r = lax.rsqrt(var + LN_EPS)
    return zc * r, r


def _ln_bwd_rows(dn, nh, r):
    return r * (dn - jnp.mean(dn, axis=-1, keepdims=True) - nh * jnp.mean(dn * nh, axis=-1, keepdims=True))


def _colsum(x):
    return jnp.sum(x, axis=0, keepdims=True)


def _dot(a, b, dims):
    return lax.dot_general(a.astype(BF16), b.astype(BF16), (dims, ((), ())), preferred_element_type=F32)


NN = ((1,), (0,))
NT = ((1,), (1,))
TN = ((0,), (0,))


ANY = pl.BlockSpec(memory_space=pl.ANY)
N_RELATIONS = N_DEV - 1


def _my_place():
    return lax.axis_index("x"), lax.axis_index("y"), lax.axis_index("c")


class _Comm:
    def __init__(self, gather=(), exchange=()):
        self.arrs = list(gather) + list(exchange)
        self.n_gather = len(gather)
        self.n = len(self.arrs)

    def out_shape(self):
        return [jax.ShapeDtypeStruct(((N_DEV,) + a.shape) if i < self.n_gather else a.shape, a.dtype)
                for i, a in enumerate(self.arrs)]

    def sems(self):
        return [pltpu.SemaphoreType.DMA((self.n, N_RELATIONS)), pltpu.SemaphoreType.DMA((self.n, N_RELATIONS)),
                pltpu.SemaphoreType.DMA((self.n,))]

    def _gather_copy(self, ins, outs, sems, a, k, place, to, from_input=False):
        px, py, pc = place
        block = outs[a].at[4 * px + 2 * py + pc]
        return pltpu.make_async_remote_copy(
            src_ref=ins[a] if from_input else block, dst_ref=block,
            send_sem=sems[0].at[a, k], recv_sem=sems[1].at[a, k], device_id=to, device_id_type=MESH_ID)

    def _exchange_copy(self, ins, outs, sems, a, k, landing=False):
        x, y, c = _my_place()
        me = 4 * x + 2 * y + c
        peer = (x ^ (k >> 2), y ^ ((k >> 1) & 1), c ^ (k & 1))
        return pltpu.make_async_remote_copy(
            src_ref=ins[a].at[me ^ k], dst_ref=outs[a].at[(me ^ k) if landing else me],
            send_sem=sems[0].at[a, k - 1], recv_sem=sems[1].at[a, k - 1], device_id=peer, device_id_type=MESH_ID)

    def _local_copy(self, ins, outs, sems, a):
        x, y, c = _my_place()
        me = 4 * x + 2 * y + c
        src = ins[a] if a < self.n_gather else ins[a].at[me]
        return pltpu.make_async_copy(src, outs[a].at[me], sems[2].at[a])

    def _first_stage(self, ins, outs, sems, a):
        x, y, c = _my_place()
        me = (x, y, c)
        chips = [(1 - x, y), (x, 1 - y), (1 - x, 1 - y)]
        return ([self._gather_copy(ins, outs, sems, a, 0, me, (x, y, 1 - c), from_input=True)]
                + [self._gather_copy(ins, outs, sems, a, 1 + j, me, (*chip, c), from_input=True) for j, chip in enumerate(chips)])

    def start(self, ins, outs, sems):
        for a in range(self.n):
            self._local_copy(ins, outs, sems, a).start()
        for a in range(self.n_gather):
            for cp in self._first_stage(ins, outs, sems, a):
                cp.start()
        for k in range(1, N_DEV):
            for a in range(self.n_gather, self.n):
                self._exchange_copy(ins, outs, sems, a, k).start()

    def finish(self, ins, outs, sems):
        x, y, c = _my_place()
        me, sibling = (x, y, c), (x, y, 1 - c)
        chips = [(1 - x, y), (x, 1 - y), (1 - x, 1 - y)]
        passed = []
        for j, chip in enumerate(chips):
            for a in range(self.n_gather):
                self._gather_copy(ins, outs, sems, a, 1 + j, (*chip, c), me).wait_recv()
                fwd = self._gather_copy(ins, outs, sems, a, 4 + j, (*chip, c), sibling)
                fwd.start()
                passed.append(fwd)
        for a in range(self.n_gather):
            self._gather_copy(ins, outs, sems, a, 0, sibling, me).wait_recv()
            for j, chip in enumerate(chips):
                self._gather_copy(ins, outs, sems, a, 4 + j, (*chip, 1 - c), me).wait_recv()
        for k in range(1, N_DEV):
            for a in range(self.n_gather, self.n):
                self._exchange_copy(ins, outs, sems, a, k, landing=True).wait_recv()
        for a in range(self.n_gather):
            for cp in self._first_stage(ins, outs, sems, a):
                cp.wait_send()
        for cp in passed:
            cp.wait_send()
        for k in range(1, N_DEV):
            for a in range(self.n_gather, self.n):
                self._exchange_copy(ins, outs, sems, a, k).wait_send()
        for a in range(self.n):
            self._local_copy(ins, outs, sems, a).wait()


def _comm_only(comm, name):
    def body(*refs):
        ins, outs, sems = refs[:comm.n], refs[comm.n:2 * comm.n], refs[2 * comm.n:]
        comm.start(ins, outs, sems)
        comm.finish(ins, outs, sems)

    return pl.pallas_call(body, name=name, in_specs=[ANY] * comm.n, out_specs=[ANY] * comm.n,
                          out_shape=comm.out_shape(), scratch_shapes=comm.sems())(*comm.arrs)


def _call(body, *, name, grid, in_specs, out_specs, out_shape, args, sem, scratch_shapes=(), comm=None):
    in_specs, out_specs, out_shape, scratch_shapes = list(in_specs), list(out_specs), list(out_shape), list(scratch_shapes)
    if comm is None:
        outs = pl.pallas_call(body, name=name, grid=grid, in_specs=in_specs, out_specs=out_specs, out_shape=out_shape,
                              scratch_shapes=scratch_shapes, compiler_params=_params(*sem))(*args)
        return list(outs), []
    n_in, n_out, n_scr, nc = len(in_specs), len(out_specs), len(scratch_shapes), comm.n

    def wrapped(*refs):
        ins, refs = refs[:n_in], refs[n_in:]
        c_in, refs = refs[:nc], refs[nc:]
        outs, refs = refs[:n_out], refs[n_out:]
        c_out, refs = refs[:nc], refs[nc:]
        scr, sems = refs[:n_scr], refs[n_scr:]
        first = functools.reduce(jnp.logical_and, [pl.program_id(ax) == 0 for ax in range(len(grid))])
        last = functools.reduce(jnp.logical_and, [pl.program_id(ax) == g - 1 for ax, g in enumerate(grid)])

        @pl.when(first)
        def _():
            comm.start(c_in, c_out, sems)

        body(*ins, *outs, *scr)

        @pl.when(last)
        def _():
            comm.finish(c_in, c_out, sems)

    outs = pl.pallas_call(
        wrapped, name=name, grid=grid, in_specs=in_specs + [ANY] * nc, out_specs=out_specs + [ANY] * nc,
        out_shape=out_shape + comm.out_shape(), scratch_shapes=scratch_shapes + comm.sems(),
        compiler_params=_params(*(["arbitrary"] * len(grid))))(*args, *comm.arrs)
    return list(outs[:n_out]), list(outs[n_out:])


def _matmul(a, b, mode, out_dtype, name, tm, tn, tk, *, bias=None, res=None, res_scale=1.0, b_off=0, comm=None):
    tm = min(tm, a.shape[1] if mode == "tn" else a.shape[0])
    tk = min(tk, a.shape[0] if mode == "tn" else a.shape[1])
    if mode == "nn":
        (m, k), n = a.shape, b.shape[1]
        a_spec = pl.BlockSpec((tm, tk), lambda i, j, kk: (i, kk))
        b_spec = pl.BlockSpec((tk, tn), lambda i, j, kk: (kk, j))
        dims = NN
    elif mode == "nt":
        (m, k), n = a.shape, b.shape[0]
        a_spec = pl.BlockSpec((tm, tk), lambda i, j, kk: (i, kk))
        b_spec = pl.BlockSpec((tn, tk), lambda i, j, kk: (j, kk + b_off))
        dims = NT
    else:
        (k, m), n = a.shape, b.shape[1]
        a_spec = pl.BlockSpec((tk, tm), lambda i, j, kk: (kk, i))
        b_spec = pl.BlockSpec((tk, tn), lambda i, j, kk: (kk, j))
        dims = TN
    assert m % tm == 0 and n % tn == 0 and k % tk == 0, (name, m, n, k)
    nk = k // tk
    in_specs = [a_spec, b_spec]
    args = [a, b]
    if bias is not None:
        in_specs.append(pl.BlockSpec((1, tn), lambda i, j, kk: (0, j)))
        args.append(bias)
    if res is not None:
        in_specs.append(pl.BlockSpec((tm, tn), lambda i, j, kk: (i, j)))
        args.append(res)

    def body(*refs):
        a_ref, b_ref = refs[0], refs[1]
        o_ref, acc = refs[-2], refs[-1]
        kk = pl.program_id(2)

        @pl.when(kk == 0)
        def _():
            acc[...] = jnp.zeros_like(acc)

        acc[...] += _dot(a_ref[...], b_ref[...], dims)

        @pl.when(kk == nk - 1)
        def _():
            out = acc[...]
            pos = 2
            if bias is not None:
                out = out + refs[pos][...]
                pos += 1
            if res is not None:
                out = out + res_scale * refs[pos][...].astype(F32)
            o_ref[...] = out.astype(out_dtype)

    (out,), moved = _call(
        body, name=name, grid=(m // tm, n // tn, nk),
        in_specs=in_specs, out_specs=[pl.BlockSpec((tm, tn), lambda i, j, kk: (i, j))],
        out_shape=[jax.ShapeDtypeStruct((m, n), out_dtype)],
        scratch_shapes=[pltpu.VMEM((tm, tn), F32)],
        sem=("parallel", "parallel", "arbitrary"), args=args, comm=comm)
    return out if comm is None else (out, moved)


def _matmul_res_ln(a, b, xres, g, beta, name, tm, tk, comm=None):
    t, k = a.shape
    d = b.shape[1]
    nk = k // tk
    assert t % tm == 0 and k % tk == 0

    def body(a_ref, b_ref, x_ref, g_ref, beta_ref, z_ref, xo_ref, acc):
        kk = pl.program_id(1)

        @pl.when(kk == 0)
        def _():
            acc[...] = jnp.zeros_like(acc)

        acc[...] += _dot(a_ref[...], b_ref[...], NN)

        @pl.when(kk == nk - 1)
        def _():
            z = ALPHA * x_ref[...] + acc[...]
            nh, _ = _ln_stats(z)
            z_ref[...] = z
            xo_ref[...] = nh * g_ref[...] + beta_ref[...]

    row = pl.BlockSpec((tm, d), lambda i, kk: (i, 0))
    vec = pl.BlockSpec((1, d), lambda i, kk: (0, 0))
    outs, moved = _call(
        body, name=name, grid=(t // tm, nk),
        in_specs=[pl.BlockSpec((tm, tk), lambda i, kk: (i, kk)), pl.BlockSpec((tk, d), lambda i, kk: (kk, 0)), row, vec, vec],
        out_specs=[row, row],
        out_shape=[jax.ShapeDtypeStruct((t, d), F32), jax.ShapeDtypeStruct((t, d), F32)],
        scratch_shapes=[pltpu.VMEM((tm, d), F32)],
        sem=("parallel", "arbitrary"), args=(a, b, xres, g, beta), comm=comm)
    return outs if comm is None else (outs, moved)


def _ln_bwd(z, g, dres, dbr, name, tm=512):
    t, d = z.shape

    def body(z_ref, g_ref, dres_ref, dbr_ref, dz_ref, dg_ref, db_ref):
        @pl.when(pl.program_id(0) == 0)
        def _():
            dg_ref[...] = jnp.zeros_like(dg_ref)
            db_ref[...] = jnp.zeros_like(db_ref)

        nh, r = _ln_stats(z_ref[...])
        dy = ALPHA * dres_ref[...] + dbr_ref[...].astype(F32)
        dg_ref[...] += _colsum(dy * nh)
        db_ref[...] += _colsum(dy)
        dz_ref[...] = _ln_bwd_rows(dy * g_ref[...], nh, r)

    row = pl.BlockSpec((tm, d), lambda i: (i, 0))
    vec = pl.BlockSpec((1, d), lambda i: (0, 0))
    return pl.pallas_call(
        body, name=name, grid=(t // tm,), in_specs=[row, vec, row, row], out_specs=[row, vec, vec],
        out_shape=[jax.ShapeDtypeStruct((t, d), F32), jax.ShapeDtypeStruct((1, d), F32), jax.ShapeDtypeStruct((1, d), F32)],
        compiler_params=_params("arbitrary"),
    )(z, g, dres, dbr)


def _ln_bwd_loss(z, g, beta, target, name, tm=512):
    t, d = z.shape

    def body(z_ref, g_ref, beta_ref, t_ref, dz_ref, dg_ref, db_ref, loss_ref):
        @pl.when(pl.program_id(0) == 0)
        def _():
            dg_ref[...] = jnp.zeros_like(dg_ref)
            db_ref[...] = jnp.zeros_like(db_ref)
            loss_ref[...] = jnp.zeros_like(loss_ref)

        nh, r = _ln_stats(z_ref[...])
        err = nh * g_ref[...] + beta_ref[...] - t_ref[...]
        loss_ref[...] += _colsum(err * err)
        dy = err * (1.0 / d)
        dg_ref[...] += _colsum(dy * nh)
        db_ref[...] += _colsum(dy)
        dz_ref[...] = _ln_bwd_rows(dy * g_ref[...], nh, r)

    row = pl.BlockSpec((tm, d), lambda i: (i, 0))
    vec = pl.BlockSpec((1, d), lambda i: (0, 0))
    vshape = jax.ShapeDtypeStruct((1, d), F32)
    return pl.pallas_call(
        body, name=name, grid=(t // tm,), in_specs=[row, vec, vec, row], out_specs=[row, vec, vec, vec],
        out_shape=[jax.ShapeDtypeStruct((t, d), F32), vshape, vshape, vshape],
        compiler_params=_params("arbitrary"),
    )(z, g, beta, target)


FFN_HALO = 16


def _ffn_mid_fwd(h, cw, cb, name, tm=512, tc=256, comm=None):
    t, f2 = h.shape
    f = f2 // 2
    nj, nt, hb = f // tc, t // tm, tm // FFN_HALO

    def body(hg, hgp, hv, hvp, cwg, cwv, cbg, cbv, u_ref, sg, sv):
        i = pl.program_id(1)

        def conv(main, prev, s, w, b):
            s[0:FFN_HALO, :] = jnp.where(i > 0, prev[...].astype(F32), 0.0)
            s[FFN_HALO:, :] = main[...].astype(F32)
            o = FFN_HALO - FFN_KERNEL + 1
            return (w[0:1, :] * s[pl.ds(o, tm), :] + w[1:2, :] * s[pl.ds(o + 1, tm), :]
                    + w[2:3, :] * s[pl.ds(o + 2, tm), :] + b[...])

        cg = conv(hg, hgp, sg, cwg, cbg)
        cv = conv(hv, hvp, sv, cwv, cbv)
        u_ref[...] = (_gelu(cg) * cv).astype(BF16)

    def main_spec(off):
        return pl.BlockSpec((tm, tc), lambda j, i: (i, j + off))

    def prev_spec(off):
        return pl.BlockSpec((FFN_HALO, tc), lambda j, i: (jnp.maximum(i * hb - 1, 0), j + off))

    def par_spec(rows, off):
        return pl.BlockSpec((rows, tc), lambda j, i: (0, j + off))

    (u,), moved = _call(
        body, name=name, grid=(nj, nt),
        in_specs=[main_spec(0), prev_spec(0), main_spec(nj), prev_spec(nj),
                  par_spec(FFN_KERNEL, 0), par_spec(FFN_KERNEL, nj), par_spec(1, 0), par_spec(1, nj)],
        out_specs=[pl.BlockSpec((tm, tc), lambda j, i: (i, j))],
        out_shape=[jax.ShapeDtypeStruct((t, f), BF16)],
        scratch_shapes=[pltpu.VMEM((tm + FFN_HALO, tc), F32), pltpu.VMEM((tm + FFN_HALO, tc), F32)],
        sem=("parallel", "arbitrary"), args=(h, h, h, h, cw, cw, cb, cb), comm=comm)
    return u if comm is None else (u, moved)


def _ffn_mid_bwd(h, du, cw, cb, name, tm=512, tc=256, comm=None):
    t, f2 = h.shape
    f = f2 // 2
    nj, nt, hb = f // tc, t // tm, tm // FFN_HALO
    r = tm + FFN_HALO

    def body(hg, hgp, hgn, hv, hvp, hvn, du_ref, dun_ref, cwg, cwv, cbg, cbv,
             dhg_ref, dhv_ref, dcwg_ref, dcwv_ref, dcbg_ref, dcbv_ref, sg, sv, sdg, sdv):
        i = pl.program_id(1)

        @pl.when(i == 0)
        def _():
            for ref in (dcwg_ref, dcwv_ref, dcbg_ref, dcbv_ref):
                ref[...] = jnp.zeros_like(ref)

        o = FFN_HALO - FFN_KERNEL + 1

        def conv(main, prev, nxt, s, w, b):
            s[0:FFN_HALO, :] = jnp.where(i > 0, prev[...].astype(F32), 0.0)
            s[FFN_HALO:FFN_HALO + tm, :] = main[...].astype(F32)
            s[FFN_HALO + tm:, :] = nxt[...].astype(F32)
            return (w[0:1, :] * s[pl.ds(o, r), :] + w[1:2, :] * s[pl.ds(o + 1, r), :]
                    + w[2:3, :] * s[pl.ds(o + 2, r), :] + b[...])

        cg = conv(hg, hgp, hgn, sg, cwg, cbg)
        cv = conv(hv, hvp, hvn, sv, cwv, cbv)
        du_e = jnp.concatenate([du_ref[...].astype(F32), jnp.where(i < nt - 1, dun_ref[...].astype(F32), 0.0)], axis=0)
        gl, dgl = _gelu_and_grad(cg)
        sdg[...] = du_e * cv * dgl
        sdv[...] = du_e * gl

        def back(sd, s, w, dh_ref, dcw_ref, dcb_ref):
            own = sd[0:tm, :]
            dcb_ref[...] += _colsum(own)
            for k in range(FFN_KERNEL):
                dcw_ref[k:k + 1, :] += _colsum(own * s[pl.ds(o + k, tm), :])
            dh = w[2:3, :] * own + w[1:2, :] * sd[pl.ds(1, tm), :] + w[0:1, :] * sd[pl.ds(2, tm), :]
            dh_ref[...] = dh.astype(BF16)

        back(sdg, sg, cwg, dhg_ref, dcwg_ref, dcbg_ref)
        back(sdv, sv, cwv, dhv_ref, dcwv_ref, dcbv_ref)

    last_blk = t // FFN_HALO - 1

    def main_spec(off):
        return pl.BlockSpec((tm, tc), lambda j, i: (i, j + off))

    def prev_spec(off):
        return pl.BlockSpec((FFN_HALO, tc), lambda j, i: (jnp.maximum(i * hb - 1, 0), j + off))

    def next_spec(off):
        return pl.BlockSpec((FFN_HALO, tc), lambda j, i: (jnp.minimum((i + 1) * hb, last_blk), j + off))

    def par_spec(rows, off):
        return pl.BlockSpec((rows, tc), lambda j, i: (0, j + off))

    out_tile = pl.BlockSpec((tm, tc), lambda j, i: (i, j))
    outs, moved = _call(
        body, name=name, grid=(nj, nt),
        in_specs=[main_spec(0), prev_spec(0), next_spec(0), main_spec(nj), prev_spec(nj), next_spec(nj),
                  main_spec(0), next_spec(0),
                  par_spec(FFN_KERNEL, 0), par_spec(FFN_KERNEL, nj), par_spec(1, 0), par_spec(1, nj)],
        out_specs=[out_tile, out_tile, par_spec(FFN_KERNEL, 0), par_spec(FFN_KERNEL, 0), par_spec(1, 0), par_spec(1, 0)],
        out_shape=[jax.ShapeDtypeStruct((t, f), BF16), jax.ShapeDtypeStruct((t, f), BF16),
                   jax.ShapeDtypeStruct((FFN_KERNEL, f), F32), jax.ShapeDtypeStruct((FFN_KERNEL, f), F32),
                   jax.ShapeDtypeStruct((1, f), F32), jax.ShapeDtypeStruct((1, f), F32)],
        scratch_shapes=[pltpu.VMEM((tm + 2 * FFN_HALO, tc), F32), pltpu.VMEM((tm + 2 * FFN_HALO, tc), F32),
                        pltpu.VMEM((r, tc), F32), pltpu.VMEM((r, tc), F32)],
        sem=("parallel", "arbitrary"), args=(h, h, h, h, h, h, du, du, cw, cw, cb, cb), comm=comm)
    return outs if comm is None else (outs, moved)


MIX_HALO = 32


def _glu(hh):
    return hh[:, 0:A_WIDTH] * _sigmoid(hh[:, A_WIDTH:2 * A_WIDTH])


def _tril_mask():
    return lax.broadcasted_iota(jnp.int32, (B_CHUNK, B_CHUNK), 0) >= lax.broadcasted_iota(jnp.int32, (B_CHUNK, B_CHUNK), 1)


def _spatial_mix(q, ms_ref, sbt_ref, tm):
    mask = _tril_mask()
    ws = [jnp.where(mask, ms_ref[g], 0.0).astype(BF16) for g in range(B_GROUPS)]
    qb = q.astype(BF16)
    rows = []
    for c in range(tm // B_CHUNK):
        cols = [_dot(ws[g], qb[c * B_CHUNK:(c + 1) * B_CHUNK, g * 128:(g + 1) * 128], NN) + sbt_ref[:, g:g + 1]
                for g in range(B_GROUPS)]
        rows.append(jnp.concatenate(cols, axis=1))
    return jnp.concatenate(rows, axis=0)


def _mixer_mid_fwd(h, cw, cb, ag, ab, bg, bb, ms, sbt, name, tm=256, comm=None):
    t = h.shape[0]
    nt, hb = t // tm, tm // MIX_HALO
    o = MIX_HALO - A_KERNEL + 1

    def body(h_ref, hp_ref, cw_ref, cb_ref, ag_ref, ab_ref, bg_ref, bb_ref, ms_ref, sbt_ref, cat_ref, sp):
        i = pl.program_id(0)
        sp[0:MIX_HALO, :] = jnp.where(i > 0, _glu(hp_ref[:, 0:2 * A_WIDTH].astype(F32)), 0.0)
        sp[MIX_HALO:, :] = _glu(h_ref[:, 0:2 * A_WIDTH].astype(F32))
        y = jnp.zeros((tm, A_WIDTH), F32) + cb_ref[...]
        for k in range(A_KERNEL):
            y = y + cw_ref[k:k + 1, :] * sp[pl.ds(o + k, tm), :]
        nh, _ = _ln_stats(y)
        ln = nh * ag_ref[...] + ab_ref[...]
        cat_ref[:, 0:A_WIDTH] = (ln * _sigmoid(ln)).astype(BF16)
        u = _gelu(h_ref[:, 1024:1536].astype(F32))
        nb, _ = _ln_stats(_gelu(h_ref[:, 1536:2048].astype(F32)))
        mixed = _spatial_mix(nb * bg_ref[...] + bb_ref[...], ms_ref, sbt_ref, tm)
        cat_ref[:, A_WIDTH:] = (u * mixed).astype(BF16)

    vec = pl.BlockSpec((1, A_WIDTH), lambda i: (0, 0))
    (cat,), moved = _call(
        body, name=name, grid=(nt,),
        in_specs=[pl.BlockSpec((tm, 2048), lambda i: (i, 0)),
                  pl.BlockSpec((MIX_HALO, 2048), lambda i: (jnp.maximum(i * hb - 1, 0), 0)),
                  pl.BlockSpec((A_KERNEL, A_WIDTH), lambda i: (0, 0)), vec, vec, vec, vec, vec,
                  pl.BlockSpec((B_GROUPS, B_CHUNK, B_CHUNK), lambda i: (0, 0, 0)),
                  pl.BlockSpec((B_CHUNK, B_GROUPS), lambda i: (0, 0))],
        out_specs=[pl.BlockSpec((tm, D_MODEL), lambda i: (i, 0))],
        out_shape=[jax.ShapeDtypeStruct((t, D_MODEL), BF16)],
        scratch_shapes=[pltpu.VMEM((tm + MIX_HALO, A_WIDTH), F32)],
        sem=("parallel",), args=(h, h, cw, cb, ag, ab, bg, bb, ms, sbt), comm=comm)
    return cat if comm is None else (cat, moved)


def _mixer_mid_bwd(h, dcat, cw, cb, ag, ab, bg, bb, ms, mst, sbt, name, tm=256, comm=None):
    t = h.shape[0]
    nt, hb = t // tm, tm // MIX_HALO
    o = MIX_HALO - A_KERNEL + 1
    r = tm + MIX_HALO
    nchunk = tm // B_CHUNK

    def body(h_ref, hp_ref, hn_ref, dc_ref, dcn_ref, cw_ref, cb_ref, ag_ref, ab_ref, bg_ref, bb_ref, ms_ref, mst_ref, sbt_ref,
             dh_ref, dcw_ref, dcb_ref, dag_ref, dab_ref, dbg_ref, dbb_ref, dms_ref, dsb_ref, sp, sdy, sbacc):
        i = pl.program_id(0)

        @pl.when(i == 0)
        def _():
            for ref in (dcw_ref, dcb_ref, dag_ref, dab_ref, dbg_ref, dbb_ref, dms_ref, dsb_ref, sbacc):
                ref[...] = jnp.zeros_like(ref)

        sp[0:MIX_HALO, :] = jnp.where(i > 0, _glu(hp_ref[:, 0:2 * A_WIDTH].astype(F32)), 0.0)
        sp[MIX_HALO:MIX_HALO + tm, :] = _glu(h_ref[:, 0:2 * A_WIDTH].astype(F32))
        sp[MIX_HALO + tm:, :] = _glu(hn_ref[:, 0:2 * A_WIDTH].astype(F32))
        y = jnp.zeros((r, A_WIDTH), F32) + cb_ref[...]
        for k in range(A_KERNEL):
            y = y + cw_ref[k:k + 1, :] * sp[pl.ds(o + k, r), :]
        nh, rs = _ln_stats(y)
        ln = nh * ag_ref[...] + ab_ref[...]
        sg = _sigmoid(ln)
        dao = jnp.concatenate([dc_ref[:, 0:A_WIDTH].astype(F32),
                               jnp.where(i < nt - 1, dcn_ref[:, 0:A_WIDTH].astype(F32), 0.0)], axis=0)
        dln = dao * (sg * (1.0 + ln * (1.0 - sg)))
        dag_ref[...] += _colsum(dln[0:tm] * nh[0:tm])
        dab_ref[...] += _colsum(dln[0:tm])
        sdy[...] = _ln_bwd_rows(dln * ag_ref[...], nh, rs)
        dy_own = sdy[0:tm, :]
        dcb_ref[...] += _colsum(dy_own)
        dp = jnp.zeros((tm, A_WIDTH), F32)
        for k in range(A_KERNEL):
            dcw_ref[k:k + 1, :] += _colsum(dy_own * sp[pl.ds(o + k, tm), :])
            dp = dp + cw_ref[k:k + 1, :] * sdy[pl.ds(A_KERNEL - 1 - k, tm), :]
        av = h_ref[:, 0:A_WIDTH].astype(F32)
        s = _sigmoid(h_ref[:, A_WIDTH:2 * A_WIDTH].astype(F32))
        dh_ref[:, 0:A_WIDTH] = (dp * s).astype(BF16)
        dh_ref[:, A_WIDTH:2 * A_WIDTH] = (dp * av * s * (1.0 - s)).astype(BF16)

        u, dgu = _gelu_and_grad(h_ref[:, 1024:1536].astype(F32))
        w, dgw = _gelu_and_grad(h_ref[:, 1536:2048].astype(F32))
        nb, rb = _ln_stats(w)
        q = nb * bg_ref[...] + bb_ref[...]
        mixed = _spatial_mix(q, ms_ref, sbt_ref, tm)
        dbo = dc_ref[:, A_WIDTH:].astype(F32)
        dh_ref[:, 1024:1536] = (dbo * mixed * dgu).astype(BF16)
        dmx = dbo * u
        mask = _tril_mask()
        wst = [jnp.where(mask.T, mst_ref[g], 0.0).astype(BF16) for g in range(B_GROUPS)]
        qb = q.astype(BF16)
        dmb = dmx.astype(BF16)
        rows = []
        for c in range(nchunk):
            cols = []
            for g in range(B_GROUPS):
                rs_, cs_ = slice(c * B_CHUNK, (c + 1) * B_CHUNK), slice(g * 128, (g + 1) * 128)
                sbacc[g] += dmx[rs_, cs_]
                dms_ref[g] += _dot(dmb[rs_, cs_], qb[rs_, cs_], NT)
                cols.append(_dot(wst[g], dmb[rs_, cs_], NN))
            rows.append(jnp.concatenate(cols, axis=1))
        dq = jnp.concatenate(rows, axis=0)
        dbg_ref[...] += _colsum(dq * nb)
        dbb_ref[...] += _colsum(dq)
        dh_ref[:, 1536:2048] = (_ln_bwd_rows(dq * bg_ref[...], nb, rb) * dgw).astype(BF16)

        @pl.when(i == nt - 1)
        def _():
            for g in range(B_GROUPS):
                dms_ref[g] = jnp.where(mask, dms_ref[g], 0.0)
                dsb_ref[g] = jnp.sum(sbacc[g], axis=1, keepdims=True)

    last_blk = t // MIX_HALO - 1
    vec = pl.BlockSpec((1, A_WIDTH), lambda i: (0, 0))
    mat = pl.BlockSpec((B_GROUPS, B_CHUNK, B_CHUNK), lambda i: (0, 0, 0))
    taps = pl.BlockSpec((A_KERNEL, A_WIDTH), lambda i: (0, 0))

    def halo(width, which):
        if which == "prev":
            return pl.BlockSpec((MIX_HALO, width), lambda i: (jnp.maximum(i * hb - 1, 0), 0))
        return pl.BlockSpec((MIX_HALO, width), lambda i: (jnp.minimum((i + 1) * hb, last_blk), 0))

    vshape = jax.ShapeDtypeStruct((1, A_WIDTH), F32)
    outs, moved = _call(
        body, name=name, grid=(nt,),
        in_specs=[pl.BlockSpec((tm, 2048), lambda i: (i, 0)), halo(2048, "prev"), halo(2048, "next"),
                  pl.BlockSpec((tm, D_MODEL), lambda i: (i, 0)), halo(D_MODEL, "next"),
                  taps, vec, vec, vec, vec, vec, mat, mat, pl.BlockSpec((B_CHUNK, B_GROUPS), lambda i: (0, 0))],
        out_specs=[pl.BlockSpec((tm, 2048), lambda i: (i, 0)), taps, vec, vec, vec, vec, vec, mat,
                   pl.BlockSpec((B_GROUPS, B_CHUNK, 1), lambda i: (0, 0, 0))],
        out_shape=[jax.ShapeDtypeStruct((t, 2048), BF16), jax.ShapeDtypeStruct((A_KERNEL, A_WIDTH), F32),
                   vshape, vshape, vshape, vshape, vshape,
                   jax.ShapeDtypeStruct((B_GROUPS, B_CHUNK, B_CHUNK), F32), jax.ShapeDtypeStruct((B_GROUPS, B_CHUNK, 1), F32)],
        scratch_shapes=[pltpu.VMEM((tm + 2 * MIX_HALO, A_WIDTH), F32), pltpu.VMEM((r, A_WIDTH), F32),
                        pltpu.VMEM((B_GROUPS, B_CHUNK, B_CHUNK), F32)],
        sem=("arbitrary",), args=(h, h, h, dcat, dcat, cw, cb, ag, ab, bg, bb, ms, mst, sbt), comm=comm)
    return outs if comm is None else (outs, moved)


Q_WIDTH = N_Q_HEADS * HEAD_DIM
KV_WIDTH = 2 * N_KV_HEADS * HEAD_DIM
PAIRS_PER_KV = N_Q_HEADS // N_KV_HEADS // 2
ATT_SCALE = 1.0 / math.sqrt(HEAD_DIM)


def _att_mask(n):
    qi = lax.broadcasted_iota(jnp.int32, (ATT_BLOCK, 2 * ATT_BLOCK), 0)
    sj = lax.broadcasted_iota(jnp.int32, (ATT_BLOCK, 2 * ATT_BLOCK), 1)
    diff = qi + ATT_BLOCK - sj
    return (diff >= 0) & (diff < ATT_BLOCK) & ((n > 0) | (sj >= ATT_BLOCK))


def _dup_heads(pair_cols, kv_head):
    lane = lax.broadcasted_iota(jnp.int32, pair_cols.shape, 1)
    rolled = pltpu.roll(pair_cols, HEAD_DIM, 1)
    first = lane < HEAD_DIM
    return jnp.where(first, pair_cols, rolled) if kv_head == 0 else jnp.where(first, rolled, pair_cols)


def _att_probs(qm, k2, mask, sink):
    s = _dot(qm, k2, NT) * ATT_SCALE
    s = jnp.where(mask, s, -jnp.inf)
    m = jnp.maximum(jnp.max(s, axis=-1, keepdims=True), sink)
    e = jnp.exp(s - m)
    es = jnp.exp(sink - m)
    inv = 1.0 / (jnp.sum(e, axis=-1, keepdims=True) + es)
    return e * inv, es * inv


def _attn_fwd(qkv, sinks, name):
    t = qkv.shape[0]
    nb = t // ATT_BLOCK
    kvb = Q_WIDTH // KV_WIDTH

    def body(sink_ref, q_ref, kv_ref, kvp_ref, o_ref):
        n = pl.program_id(0)
        mask = _att_mask(n)
        kv = jnp.concatenate([kvp_ref[...], kv_ref[...]], axis=0).astype(F32)
        lane = lax.broadcasted_iota(jnp.int32, (ATT_BLOCK, 128), 1)
        for kh in range(N_KV_HEADS):
            k2 = _dup_heads(kv[:, 0:128], kh).astype(BF16)
            v2 = _dup_heads(kv[:, 128:256], kh).astype(BF16)
            for pr in range(PAIRS_PER_KV):
                c0 = (kh * PAIRS_PER_KV + pr) * 128
                q2 = q_ref[:, c0:c0 + 128]
                outs = []
                for half in range(2):
                    head = (kh * PAIRS_PER_KV + pr) * 2 + half
                    qm = jnp.where((lane < HEAD_DIM) == (half == 0), q2, jnp.zeros_like(q2))
                    p, _ = _att_probs(qm, k2, mask, sink_ref[0, head])
                    outs.append(_dot(p, v2, NN))
                o_ref[:, c0:c0 + 128] = jnp.where(lane < HEAD_DIM, outs[0], outs[1]).astype(BF16)

    return pl.pallas_call(
        body, name=name, grid=(nb,),
        in_specs=[pl.BlockSpec(memory_space=pltpu.SMEM),
                  pl.BlockSpec((ATT_BLOCK, Q_WIDTH), lambda n: (n, 0)),
                  pl.BlockSpec((ATT_BLOCK, KV_WIDTH), lambda n: (n, kvb)),
                  pl.BlockSpec((ATT_BLOCK, KV_WIDTH), lambda n: (jnp.maximum(n - 1, 0), kvb))],
        out_specs=pl.BlockSpec((ATT_BLOCK, Q_WIDTH), lambda n: (n, 0)),
        out_shape=jax.ShapeDtypeStruct((t, Q_WIDTH), BF16),
        compiler_params=_params("parallel"),
    )(sinks, qkv, qkv, qkv)


def _attn_bwd(qkv, d_o, sinks, name, comm=None):
    t = qkv.shape[0]
    nb = t // ATT_BLOCK
    kvb = Q_WIDTH // KV_WIDTH

    def body(sink_ref, q_ref, kv_ref, kvp_ref, do_ref, dq_ref, dkv_ref, dbq_ref, dbkv_ref, dsink_ref, carry):
        n = pl.program_id(0)

        @pl.when(n == 0)
        def _():
            for ref in (dbq_ref, dbkv_ref, dsink_ref, carry):
                ref[...] = jnp.zeros_like(ref)
            dkv_ref[...] = jnp.zeros_like(dkv_ref)

        @pl.when(n < nb)
        def _():
            mask = _att_mask(n)
            kv = jnp.concatenate([kvp_ref[...], kv_ref[...]], axis=0).astype(F32)
            lane = lax.broadcasted_iota(jnp.int32, (ATT_BLOCK, 128), 1)
            lane2 = lax.broadcasted_iota(jnp.int32, (2 * ATT_BLOCK, 128), 1)
            sink_lane = lax.broadcasted_iota(jnp.int32, (1, 128), 1)
            dsink = jnp.zeros((1, 128), F32)
            dk_parts, dv_parts = [], []
            for kh in range(N_KV_HEADS):
                k2 = _dup_heads(kv[:, 0:128], kh).astype(BF16)
                v2 = _dup_heads(kv[:, 128:256], kh).astype(BF16)
                dk_acc = jnp.zeros((2 * ATT_BLOCK, 128), F32)
                dv_acc = jnp.zeros((2 * ATT_BLOCK, 128), F32)
                for pr in range(PAIRS_PER_KV):
                    c0 = (kh * PAIRS_PER_KV + pr) * 128
                    q2 = q_ref[:, c0:c0 + 128]
                    do2 = do_ref[:, c0:c0 + 128]
                    dqs = []
                    for half in range(2):
                        head = (kh * PAIRS_PER_KV + pr) * 2 + half
                        sel = (lane < HEAD_DIM) == (half == 0)
                        qm = jnp.where(sel, q2, jnp.zeros_like(q2))
                        dom = jnp.where(sel, do2, jnp.zeros_like(do2))
                        p, ps = _att_probs(qm, k2, mask, sink_ref[0, head])
                        dp = _dot(dom, v2, NT)
                        delta = jnp.sum(p * dp, axis=-1, keepdims=True)
                        ds = p * (dp - delta) * ATT_SCALE
                        dsink = dsink + jnp.where(sink_lane == head, -jnp.sum(ps * delta), 0.0)
                        dqs.append(_dot(ds, k2, NN))
                        dk_acc = dk_acc + _dot(ds, qm, TN)
                        dv_acc = dv_acc + _dot(p, dom, TN)
                    dq2 = jnp.where(lane < HEAD_DIM, dqs[0], dqs[1])
                    dbq_ref[:, c0:c0 + 128] += _colsum(dq2)
                    dq_ref[:, c0:c0 + 128] = dq2.astype(BF16)
                dk_parts.append(dk_acc + pltpu.roll(dk_acc, HEAD_DIM, 1))
                dv_parts.append(dv_acc + pltpu.roll(dv_acc, HEAD_DIM, 1))
            dk = jnp.where(lane2 < HEAD_DIM, dk_parts[0], dk_parts[1])
            dv = jnp.where(lane2 < HEAD_DIM, dv_parts[0], dv_parts[1])
            dkv_new = jnp.concatenate([dk, dv], axis=1)
            done = carry[...] + dkv_new[0:ATT_BLOCK]

            @pl.when(n > 0)
            def _():
                dkv_ref[...] = done.astype(BF16)
                dbkv_ref[...] += _colsum(done)

            carry[...] = dkv_new[ATT_BLOCK:]
            dsink_ref[...] += dsink

        @pl.when(n == nb)
        def _():
            dkv_ref[...] = carry[...].astype(BF16)
            dbkv_ref[...] += _colsum(carry[...])

    def clamp(n):
        return jnp.minimum(n, nb - 1)

    outs, moved = _call(
        body, name=name, grid=(nb + 1,),
        in_specs=[pl.BlockSpec(memory_space=pltpu.SMEM),
                  pl.BlockSpec((ATT_BLOCK, Q_WIDTH), lambda n: (clamp(n), 0)),
                  pl.BlockSpec((ATT_BLOCK, KV_WIDTH), lambda n: (clamp(n), kvb)),
                  pl.BlockSpec((ATT_BLOCK, KV_WIDTH), lambda n: (jnp.maximum(clamp(n) - 1, 0), kvb)),
                  pl.BlockSpec((ATT_BLOCK, Q_WIDTH), lambda n: (clamp(n), 0))],
        out_specs=[pl.BlockSpec((ATT_BLOCK, Q_WIDTH), lambda n: (clamp(n), 0)),
                   pl.BlockSpec((ATT_BLOCK, KV_WIDTH), lambda n: (jnp.maximum(n - 1, 0), 0)),
                   pl.BlockSpec((1, Q_WIDTH), lambda n: (0, 0)),
                   pl.BlockSpec((1, KV_WIDTH), lambda n: (0, 0)),
                   pl.BlockSpec((1, 128), lambda n: (0, 0))],
        out_shape=[jax.ShapeDtypeStruct((t, Q_WIDTH), BF16), jax.ShapeDtypeStruct((t, KV_WIDTH), BF16),
                   jax.ShapeDtypeStruct((1, Q_WIDTH), F32), jax.ShapeDtypeStruct((1, KV_WIDTH), F32),
                   jax.ShapeDtypeStruct((1, 128), F32)],
        scratch_shapes=[pltpu.VMEM((ATT_BLOCK, KV_WIDTH), F32)],
        sem=("arbitrary",), args=(sinks, qkv, qkv, qkv, d_o), comm=comm)
    return outs if comm is None else (outs, moved)


def _adamw_math(g, w, m, v):
    m = ADAM_B1 * m + (1.0 - ADAM_B1) * g
    v = ADAM_B2 * v + (1.0 - ADAM_B2) * (g * g)
    m_hat = m / (1.0 - ADAM_B1 ** ADAM_STEP)
    v_hat = v / (1.0 - ADAM_B2 ** ADAM_STEP)
    delta = -ADAM_LR * (m_hat / (jnp.sqrt(v_hat) + ADAM_EPS) + ADAM_WD * w)
    return delta, m, v


def _sum_partials(p_ref):
    g = p_ref[0].astype(F32)
    for s in range(1, N_DEV):
        g = g + p_ref[s].astype(F32)
    return g


def _adamw_big(parts, w, m, v, name, tr):
    r, c = w.shape
    tiles = [p.shape[1] // tr for p in parts]
    starts = [sum(tiles[:l]) for l in range(len(parts))]
    assert all(p.shape[1] % tr == 0 for p in parts) and sum(tiles) * tr == r

    def body(*refs):
        p_refs, (w_ref, m_ref, v_ref, g_out, d_out, m_out, v_out) = refs[:len(parts)], refs[len(parts):]
        i = pl.program_id(0)
        for l, p_ref in enumerate(p_refs):
            @pl.when((i >= starts[l]) & (i < starts[l] + tiles[l]))
            def _():
                g = _sum_partials(p_ref)
                g_out[...] = g
                d_out[...], m_out[...], v_out[...] = _adamw_math(g, w_ref[...], m_ref[...], v_ref[...])

    def part_spec(l):
        return pl.BlockSpec((N_DEV, tr, c), lambda i: (0, jnp.clip(i - starts[l], 0, tiles[l] - 1), 0))

    tile = pl.BlockSpec((tr, c), lambda i: (i, 0))
    shape = jax.ShapeDtypeStruct((r, c), F32)
    return pl.pallas_call(
        body, name=name, grid=(r // tr,),
        in_specs=[part_spec(l) for l in range(len(parts))] + [tile, tile, tile],
        out_specs=[tile] * 4, out_shape=[shape] * 4,
        compiler_params=_params("parallel"),
    )(*parts, w, m, v)


def _adamw_small(parts, ws, ms, vs, name):
    n = len(ws)

    def body(*refs):
        ins, outs = refs[:4 * n], refs[4 * n:]
        for a in range(n):
            g = _sum_partials(ins[a])
            outs[4 * a][...] = g
            outs[4 * a + 1][...], outs[4 * a + 2][...], outs[4 * a + 3][...] = _adamw_math(
                g, ins[n + a][...], ins[2 * n + a][...], ins[3 * n + a][...])

    out_shape = []
    for w in ws:
        out_shape += [jax.ShapeDtypeStruct(w.shape, F32)] * 4
    return pl.pallas_call(body, name=name, out_shape=out_shape, compiler_params=_params())(*parts, *ws, *ms, *vs)


PACK_LANES = 128
PACK_ROWS = 8


def _pack(arrs):
    flat = jnp.concatenate([a.reshape(-1).astype(F32) for a in arrs])
    unit = PACK_LANES * PACK_ROWS
    total = -(-flat.shape[0] // unit) * unit
    return jnp.pad(flat, (0, total - flat.shape[0])).reshape(-1, PACK_LANES)


def _unpack(buf, shapes):
    flat = buf.reshape(N_DEV, -1)
    out, pos = [], 0
    for s in shapes:
        size = math.prod(s)
        out.append(flat[:, pos:pos + size].reshape((N_DEV,) + tuple(s)))
        pos += size
    return out


def _interleave(g):
    return jnp.transpose(g, (1, 0, 2)).reshape(g.shape[1], -1)


def _deinterleave(w):
    r = w.shape[0]
    return jnp.transpose(w.reshape(r, N_DEV, -1), (1, 0, 2))


def _ffn_backward(dz, x_in, h, u, w_up, cw, cb, w_down, tag, comm=None):
    du = _matmul(dz, w_down, "nt", BF16, f"ffn{tag}_du", 1024, 1408, 1024)
    d_w_down = _matmul(u, dz, "tn", BF16, f"ffn{tag}_dwdown", 1408, 1024, 512)
    outs = _ffn_mid_bwd(h, du, cw, cb, f"ffn{tag}_mid_bwd", comm=comm)
    (dhg, dhv, dcwg, dcwv, dcbg, dcbv), moved = outs if comm is not None else (outs, [])
    d_w_up = jnp.concatenate([_matmul(x_in, dhg, "tn", BF16, f"ffn{tag}_dwup_gate", 1024, 1408, 512),
                              _matmul(x_in, dhv, "tn", BF16, f"ffn{tag}_dwup_value", 1024, 1408, 512)], axis=1)
    dx = _matmul(dhv, w_up, "nt", F32, f"ffn{tag}_dx_value", 1024, 1024, 1408, b_off=D_FF // 1408)
    dx = _matmul(dhg, w_up, "nt", F32, f"ffn{tag}_dx_gate", 1024, 1024, 1408, res=dx)
    return (dx, _deinterleave(d_w_up), d_w_down.reshape(N_DEV, -1, D_MODEL),
            jnp.concatenate([dcwg, dcwv], axis=1), jnp.concatenate([dcbg, dcbv], axis=1), moved)


def kernel(x, ab_w_in, a_conv_w, a_conv_b, a_norm_g, a_norm_b, b_norm_g, b_norm_b, b_spatial_w, b_spatial_b, ab_w_out, c_w_qkv, c_b_qkv, c_sinks, c_w_o, ffn_w_up, ffn_conv_w, ffn_conv_b, ffn_w_down, ln_g, ln_b, loss_target, m_ab_w_in, m_a_conv_w, m_a_conv_b, m_a_norm_g, m_a_norm_b, m_b_norm_g, m_b_norm_b, m_b_spatial_w, m_b_spatial_b, m_ab_w_out, m_c_w_qkv, m_c_b_qkv, m_c_sinks, m_c_w_o, m_ffn_w_up, m_ffn_conv_w, m_ffn_conv_b, m_ffn_w_down, m_ln_g, m_ln_b, v_ab_w_in, v_a_conv_w, v_a_conv_b, v_a_norm_g, v_a_norm_b, v_b_norm_g, v_b_norm_b, v_b_spatial_w, v_b_spatial_b, v_ab_w_out, v_c_w_qkv, v_c_b_qkv, v_c_sinks, v_c_w_o, v_ffn_w_up, v_ffn_conv_w, v_ffn_conv_b, v_ffn_w_down, v_ln_g, v_ln_b):
    me = 4 * lax.axis_index("x") + 2 * lax.axis_index("y") + lax.axis_index("c")
    xt = x[0]
    t = xt.shape[0]

    small_shard_shapes = [a_conv_w.shape, c_b_qkv.shape, ffn_conv_w.shape, ln_g.shape, ln_b.shape]
    up_shard = [ffn_w_up[l].astype(BF16) for l in range(2)]
    down_shard = [ffn_w_down[l].astype(BF16) for l in range(2)]
    g_win, g_wout, g_small = _comm_only(
        _Comm(gather=[ab_w_in[0].astype(BF16), ab_w_out[0].astype(BF16), _pack([a_conv_w, c_b_qkv, ffn_conv_w, ln_g, ln_b])]),
        "gather_first")
    w_in = _interleave(g_win)
    w_out = g_wout.reshape(D_MODEL, D_MODEL)
    g_acw, g_bqkv, g_fcw, g_lng, g_lnb = _unpack(g_small, small_shard_shapes)
    acw = _interleave(g_acw[:, 0])
    bqkv = g_bqkv[:, 0].reshape(1, -1)
    fcw = [_interleave(g_fcw[:, l]) for l in range(2)]
    lng = jnp.transpose(g_lng, (1, 2, 0, 3)).reshape(2, 2, 1, D_MODEL)
    lnb = jnp.transpose(g_lnb, (1, 2, 0, 3)).reshape(2, 2, 1, D_MODEL)
    fcb = [ffn_conv_b[l:l + 1] for l in range(2)]
    ms = b_spatial_w[0]
    mst = jnp.swapaxes(ms, 1, 2)
    sbt = b_spatial_b[0].T

    h0 = _matmul(xt, w_in, "nn", BF16, "mix_in", 1024, 512, 1024)
    cat, (g_wup0,) = _mixer_mid_fwd(h0, acw, a_conv_b, a_norm_g, a_norm_b, b_norm_g, b_norm_b, ms, sbt, "mix_mid_fwd",
                                    comm=_Comm(gather=[up_shard[0]]))
    w_up0 = _interleave(g_wup0)
    z1, x1 = _matmul_res_ln(cat, w_out, xt, lng[0, 0], lnb[0, 0], "mix_out_ln", 512, D_MODEL)
    hf0, (g_wdown0, g_wqkv, g_wo) = _matmul(x1, w_up0, "nn", BF16, "ffn0_up", 1024, 512, 1024,
                                            comm=_Comm(gather=[down_shard[0], c_w_qkv[0].astype(BF16), c_w_o[0].astype(BF16)]))
    w_down0 = g_wdown0.reshape(D_FF, D_MODEL)
    w_qkv = _interleave(g_wqkv)
    w_o = g_wo.reshape(D_MODEL, D_MODEL)
    u0, (g_wup1,) = _ffn_mid_fwd(hf0, fcw[0], fcb[0], "ffn0_mid_fwd", comm=_Comm(gather=[up_shard[1]]))
    w_up1 = _interleave(g_wup1)
    (z2, x2), (g_wdown1,) = _matmul_res_ln(u0, w_down0, x1, lng[0, 1], lnb[0, 1], "ffn0_down_ln", 512, D_FF,
                                           comm=_Comm(gather=[down_shard[1]]))
    w_down1 = g_wdown1.reshape(D_FF, D_MODEL)
    qkv = _matmul(x2, w_qkv, "nn", BF16, "att_qkv", 1024, 1280, 1024, bias=bqkv)
    att = _attn_fwd(qkv, c_sinks, "att_fwd")
    z3, x3 = _matmul_res_ln(att, w_o, x2, lng[1, 0], lnb[1, 0], "att_out_ln", 512, D_MODEL)
    hf1 = _matmul(x3, w_up1, "nn", BF16, "ffn1_up", 1024, 512, 1024)
    u1 = _ffn_mid_fwd(hf1, fcw[1], fcb[1], "ffn1_mid_fwd")
    z4, _ = _matmul_res_ln(u1, w_down1, x3, lng[1, 1], lnb[1, 1], "ffn1_down_ln", 512, D_FF)

    dz4, dg11, db11, loss_terms = _ln_bwd_loss(z4, lng[1, 1], lnb[1, 1], loss_target[0], "loss_ln_bwd")
    dx3, d_wup1, d_wdown1, d_fcw1, d_fcb1, _ = _ffn_backward(dz4, x3, hf1, u1, w_up1, fcw[1], fcb[1], w_down1, 1)
    dz3, dg10, db10 = _ln_bwd(z3, lng[1, 0], dz4, dx3, "ln10_bwd")
    d_att = _matmul(dz3, w_o, "nt", BF16, "att_dout", 1024, 1024, 1024)
    d_wo = _matmul(att, dz3, "tn", BF16, "att_dwo", 1024, 1024, 512)
    (dq, dkv, dbq, dbkv, dsinks), (p_wup1, p_wdown1) = _attn_bwd(qkv, d_att, c_sinks, "att_bwd",
                                                                  comm=_Comm(exchange=[d_wup1, d_wdown1]))
    d_wqkv = jnp.concatenate([_matmul(x2, dq, "tn", BF16, "att_dwq", 1024, 1024, 512),
                              _matmul(x2, dkv, "tn", BF16, "att_dwkv", 1024, KV_WIDTH, 512)], axis=1)
    dx2 = _matmul(dkv, w_qkv, "nt", F32, "att_dx_kv", 1024, 1024, KV_WIDTH, b_off=Q_WIDTH // KV_WIDTH)
    dx2 = _matmul(dq, w_qkv, "nt", F32, "att_dx_q", 1024, 1024, KV_WIDTH, res=dx2)
    dz2, dg01, db01 = _ln_bwd(z2, lng[0, 1], dz3, dx2, "ln01_bwd")
    dx1, d_wup0, d_wdown0, d_fcw0, d_fcb0, (p_wqkv, p_wo) = _ffn_backward(
        dz2, x1, hf0, u0, w_up0, fcw[0], fcb[0], w_down0, 0,
        comm=_Comm(exchange=[_deinterleave(d_wqkv), d_wo.reshape(N_DEV, -1, D_MODEL)]))
    dz1, dg00, db00 = _ln_bwd(z1, lng[0, 0], dz2, dx1, "ln00_bwd")
    dcat = _matmul(dz1, w_out, "nt", BF16, "mix_dcat", 1024, 1024, 1024)
    d_wout = _matmul(cat, dz1, "tn", BF16, "mix_dwout", 1024, 1024, 512)
    (dh0, d_acw, d_acb, d_ang, d_anb, d_bng, d_bnb, d_ms, d_sb), (p_wup0, p_wdown0) = _mixer_mid_bwd(
        h0, dcat, acw, a_conv_b, a_norm_g, a_norm_b, b_norm_g, b_norm_b, ms, mst, sbt, "mix_mid_bwd",
        comm=_Comm(exchange=[d_wup0, d_wdown0]))
    d_win = _matmul(xt, dh0, "tn", BF16, "mix_dwin", 1024, 1024, 512)
    grad_x = _matmul(dh0, w_in, "nt", F32, "mix_dx", 1024, 1024, 1024, res=dz1, res_scale=ALPHA)

    loss = lax.psum(0.5 / D_MODEL * jnp.sum(loss_terms), ("x", "y", "c"))

    d_bqkv = jnp.concatenate([dbq, dbkv], axis=1)
    d_lng = jnp.stack([jnp.stack([dg00, dg01]), jnp.stack([dg10, dg11])])
    d_lnb = jnp.stack([jnp.stack([db00, db01]), jnp.stack([db10, db11])])
    small_full = [d_acb, d_ang, d_anb, d_bng, d_bnb, d_ms, d_sb, dsinks[:, :N_Q_HEADS], jnp.concatenate([d_fcb0, d_fcb1], axis=0),
                  d_acw, d_bqkv, jnp.stack([d_fcw0, d_fcw1]), d_lng, d_lnb]
    g_small_grads, p_win, p_wout = _comm_only(
        _Comm(gather=[_pack(small_full)], exchange=[_deinterleave(d_win), d_wout.reshape(N_DEV, -1, D_MODEL)]), "exchange_last")

    big = {}
    for nm, p, w, m, v, tr in [
            ("ab_w_in", [p_win], ab_w_in, m_ab_w_in, v_ab_w_in, 256), ("ab_w_out", [p_wout], ab_w_out, m_ab_w_out, v_ab_w_out, 128),
            ("c_w_qkv", [p_wqkv], c_w_qkv, m_c_w_qkv, v_c_w_qkv, 256), ("c_w_o", [p_wo], c_w_o, m_c_w_o, v_c_w_o, 128),
            ("ffn_w_up", [p_wup0, p_wup1], ffn_w_up, m_ffn_w_up, v_ffn_w_up, 256),
            ("ffn_w_down", [p_wdown0, p_wdown1], ffn_w_down, m_ffn_w_down, v_ffn_w_down, 176)]:
        two_d = (-1, w.shape[-1])
        outs = _adamw_big(p, w.reshape(two_d), m.reshape(two_d), v.reshape(two_d), "adamw_" + nm, tr)
        big[nm] = [o.reshape(w.shape) for o in outs]

    gs = _unpack(g_small_grads, [a.shape for a in small_full])

    def my_shard(g, width):
        g = g.reshape(g.shape[:-1] + (N_DEV, width))
        return lax.dynamic_index_in_dim(g, me, axis=g.ndim - 2, keepdims=False)

    small_names = ["a_conv_b", "a_norm_g", "a_norm_b", "b_norm_g", "b_norm_b", "b_spatial_w", "b_spatial_b", "c_sinks", "ffn_conv_b",
                   "a_conv_w", "c_b_qkv", "ffn_conv_w", "ln_g", "ln_b"]
    small_w = [a_conv_b, a_norm_g, a_norm_b, b_norm_g, b_norm_b, b_spatial_w, b_spatial_b, c_sinks, ffn_conv_b,
               a_conv_w, c_b_qkv, ffn_conv_w, ln_g, ln_b]
    small_m = [m_a_conv_b, m_a_norm_g, m_a_norm_b, m_b_norm_g, m_b_norm_b, m_b_spatial_w, m_b_spatial_b, m_c_sinks, m_ffn_conv_b,
               m_a_conv_w, m_c_b_qkv, m_ffn_conv_w, m_ln_g, m_ln_b]
    small_v = [v_a_conv_b, v_a_norm_g, v_a_norm_b, v_b_norm_g, v_b_norm_b, v_b_spatial_w, v_b_spatial_b, v_c_sinks, v_ffn_conv_b,
               v_a_conv_w, v_c_b_qkv, v_ffn_conv_w, v_ln_g, v_ln_b]
    gs[9:] = [my_shard(g, w.shape[-1]) for g, w in zip(gs[9:], small_w[9:])]
    two_d = [(-1, w.shape[-1]) for w in small_w]
    outs = _adamw_small([g.reshape((N_DEV,) + w.reshape(s).shape) for g, w, s in zip(gs, small_w, two_d)],
                        [w.reshape(s) for w, s in zip(small_w, two_d)], [m.reshape(s) for m, s in zip(small_m, two_d)],
                        [v.reshape(s) for v, s in zip(small_v, two_d)], "adamw_small")
    small = {nm: [o.reshape(w.shape) for o in outs[4 * a:4 * a + 4]] for a, (nm, w) in enumerate(zip(small_names, small_w))}

    res = {**big, **small}
    order = ["ab_w_in", "a_conv_w", "a_conv_b", "a_norm_g", "a_norm_b", "b_norm_g", "b_norm_b", "b_spatial_w", "b_spatial_b", "ab_w_out",
             "c_w_qkv", "c_b_qkv", "c_sinks", "c_w_o", "ffn_w_up", "ffn_conv_w", "ffn_conv_b", "ffn_w_down", "ln_g", "ln_b"]
    return (loss, grad_x[None], *[res[nm][0] for nm in order], *[res[nm][1] for nm in order],
            *[res[nm][2] for nm in order], *[res[nm][3] for nm in order])
```

```python
import functools
import math

import jax
import jax.numpy as jnp
from jax import lax
from jax.experimental import pallas as pl
from jax.experimental.pallas import tpu as pltpu

F32 = jnp.float32
BF16 = jnp.bfloat16

N_DEV = 8
D_MODEL = 1024
A_WIDTH = 512
A_KERNEL = 31
B_GROUPS = 4
B_CHUNK = 128
HEAD_DIM = 64
N_Q_HEADS = 16
N_KV_HEADS = 2
ATT_BLOCK = 128
D_FF = 2816
FFN_KERNEL = 3
ALPHA = (2.0 * 2) ** 0.25
LN_EPS = 1e-5
GELU_K = math.sqrt(2.0 / math.pi)
GELU_C = 0.044715
ADAM_LR = 0.001
ADAM_B1 = 0.9
ADAM_B2 = 0.999
ADAM_EPS = 1e-08
ADAM_WD = 0.01
ADAM_STEP = 10
VMEM_LIMIT = 56 * 1024 * 1024
MESH_ID = pl.DeviceIdType.MESH


def _params(*sem):
    return pltpu.CompilerParams(dimension_semantics=sem, vmem_limit_bytes=VMEM_LIMIT)


def _gelu(x):
    t = jnp.tanh(GELU_K * x * (1.0 + GELU_C * x * x))
    return 0.5 * x * (1.0 + t)


def _gelu_and_grad(x):
    x2 = x * x
    t = jnp.tanh(GELU_K * x * (1.0 + GELU_C * x2))
    g = 0.5 * x * (1.0 + t)
    dg = 0.5 * (1.0 + t) + 0.5 * x * (1.0 - t * t) * (GELU_K * (1.0 + 3.0 * GELU_C * x2))
    return g, dg


def _sigmoid(x):
    return 1.0 / (1.0 + jnp.exp(-x))


def _ln_stats(z):
    mu = jnp.mean(z, axis=-1, keepdims=True)
    zc = z - mu
    var = jnp.mean(zc * zc, axis=-1, keepdims=True)
    r = lax.rsqrt(var + LN_EPS)
    return zc * r, r


def _ln_bwd_rows(dn, nh, r):
    return r * (dn - jnp.mean(dn, axis=-1, keepdims=True) - nh * jnp.mean(dn * nh, axis=-1, keepdims=True))


def _colsum(x):
    return jnp.sum(x, axis=0, keepdims=True)


def _dot(a, b, dims):
    return lax.dot_general(a.astype(BF16), b.astype(BF16), (dims, ((), ())), preferred_element_type=F32)


NN = ((1,), (0,))
NT = ((1,), (1,))
TN = ((0,), (0,))


ANY = pl.BlockSpec(memory_space=pl.ANY)
N_RELATIONS = N_DEV - 1


def _my_place():
    return lax.axis_index("x"), lax.axis_index("y"), lax.axis_index("c")


class _Comm:
    def __init__(self, gather=(), exchange=()):
        self.arrs = list(gather) + list(exchange)
        self.n_gather = len(gather)
        self.n = len(self.arrs)

    def out_shape(self):
        return [jax.ShapeDtypeStruct(((N_DEV,) + a.shape) if i < self.n_gather else a.shape, a.dtype)
                for i, a in enumerate(self.arrs)]

    def sems(self):
        return [pltpu.SemaphoreType.DMA((self.n, N_RELATIONS)), pltpu.SemaphoreType.DMA((self.n, N_RELATIONS)),
                pltpu.SemaphoreType.DMA((self.n,))]

    def _gather_copy(self, ins, outs, sems, a, k, place, to, from_input=False):
        px, py, pc = place
        block = outs[a].at[4 * px + 2 * py + pc]
        return pltpu.make_async_remote_copy(
            src_ref=ins[a] if from_input else block, dst_ref=block,
            send_sem=sems[0].at[a, k], recv_sem=sems[1].at[a, k], device_id=to, device_id_type=MESH_ID)

    def _exchange_copy(self, ins, outs, sems, a, k, landing=False):
        x, y, c = _my_place()
        me = 4 * x + 2 * y + c
        peer = (x ^ (k >> 2), y ^ ((k >> 1) & 1), c ^ (k & 1))
        return pltpu.make_async_remote_copy(
            src_ref=ins[a].at[me ^ k], dst_ref=outs[a].at[(me ^ k) if landing else me],
            send_sem=sems[0].at[a, k - 1], recv_sem=sems[1].at[a, k - 1], device_id=peer, device_id_type=MESH_ID)

    def _local_copy(self, ins, outs, sems, a):
        x, y, c = _my_place()
        me = 4 * x + 2 * y + c
        src = ins[a] if a < self.n_gather else ins[a].at[me]
        return pltpu.make_async_copy(src, outs[a].at[me], sems[2].at[a])

    def _first_stage(self, ins, outs, sems, a):
        x, y, c = _my_place()
        me = (x, y, c)
        chips = [(1 - x, y), (x, 1 - y), (1 - x, 1 - y)]
        return ([self._gather_copy(ins, outs, sems, a, 0, me, (x, y, 1 - c), from_input=True)]
                + [self._gather_copy(ins, outs, sems, a, 1 + j, me, (*chip, c), from_input=True) for j, chip in enumerate(chips)])

    def start(self, ins, outs, sems):
        for a in range(self.n):
            self._local_copy(ins, outs, sems, a).start()
        for a in range(self.n_gather):
            for cp in self._first_stage(ins, outs, sems, a):
                cp.start()
        for k in range(1, N_DEV):
            for a in range(self.n_gather, self.n):
                self._exchange_copy(ins, outs, sems, a, k).start()

    def finish(self, ins, outs, sems):
        x, y, c = _my_place()
        me, sibling = (x, y, c), (x, y, 1 - c)
        chips = [(1 - x, y), (x, 1 - y), (1 - x, 1 - y)]
        passed = []
        for j, chip in enumerate(chips):
            for a in range(self.n_gather):
                self._gather_copy(ins, outs, sems, a, 1 + j, (*chip, c), me).wait_recv()
                fwd = self._gather_copy(ins, outs, sems, a, 4 + j, (*chip, c), sibling)
                fwd.start()
                passed.append(fwd)
        for a in range(self.n_gather):
            self._gather_copy(ins, outs, sems, a, 0, sibling, me).wait_recv()
            for j, chip in enumerate(chips):
                self._gather_copy(ins, outs, sems, a, 4 + j, (*chip, 1 - c), me).wait_recv()
        for k in range(1, N_DEV):
            for a in range(self.n_gather, self.n):
                self._exchange_copy(ins, outs, sems, a, k, landing=True).wait_recv()
        for a in range(self.n_gather):
            for cp in self._first_stage(ins, outs, sems, a):
                cp.wait_send()
        for cp in passed:
            cp.wait_send()
        for k in range(1, N_DEV):
            for a in range(self.n_gather, self.n):
                self._exchange_copy(ins, outs, sems, a, k).wait_send()
        for a in range(self.n):
            self._local_copy(ins, outs, sems, a).wait()


def _comm_only(comm, name):
    def body(*refs):
        ins, outs, sems = refs[:comm.n], refs[comm.n:2 * comm.n], refs[2 * comm.n:]
        comm.start(ins, outs, sems)
        comm.finish(ins, outs, sems)

    return pl.pallas_call(body, name=name, in_specs=[ANY] * comm.n, out_specs=[ANY] * comm.n,
                          out_shape=comm.out_shape(), scratch_shapes=comm.sems())(*comm.arrs)


def _call(body, *, name, grid, in_specs, out_specs, out_shape, args, sem, scratch_shapes=(), comm=None):
    in_specs, out_specs, out_shape, scratch_shapes = list(in_specs), list(out_specs), list(out_shape), list(scratch_shapes)
    if comm is None:
        outs = pl.pallas_call(body, name=name, grid=grid, in_specs=in_specs, out_specs=out_specs, out_shape=out_shape,
                              scratch_shapes=scratch_shapes, compiler_params=_params(*sem))(*args)
        return list(outs), []
    n_in, n_out, n_scr, nc = len(in_specs), len(out_specs), len(scratch_shapes), comm.n

    def wrapped(*refs):
        ins, refs = refs[:n_in], refs[n_in:]
        c_in, refs = refs[:nc], refs[nc:]
        outs, refs = refs[:n_out], refs[n_out:]
        c_out, refs = refs[:nc], refs[nc:]
        scr, sems = refs[:n_scr], refs[n_scr:]
        first = functools.reduce(jnp.logical_and, [pl.program_id(ax) == 0 for ax in range(len(grid))])
        last = functools.reduce(jnp.logical_and, [pl.program_id(ax) == g - 1 for ax, g in enumerate(grid)])

        @pl.when(first)
        def _():
            comm.start(c_in, c_out, sems)

        body(*ins, *outs, *scr)

        @pl.when(last)
        def _():
            comm.finish(c_in, c_out, sems)

    outs = pl.pallas_call(
        wrapped, name=name, grid=grid, in_specs=in_specs + [ANY] * nc, out_specs=out_specs + [ANY] * nc,
        out_shape=out_shape + comm.out_shape(), scratch_shapes=scratch_shapes + comm.sems(),
        compiler_params=_params(*(["arbitrary"] * len(grid))))(*args, *comm.arrs)
    return list(outs[:n_out]), list(outs[n_out:])


def _matmul(a, b, mode, out_dtype, name, tm, tn, tk, *, bias=None, res=None, res_scale=1.0, b_off=0, comm=None):
    tm = min(tm, a.shape[1] if mode == "tn" else a.shape[0])
    tk = min(tk, a.shape[0] if mode == "tn" else a.shape[1])
    if mode == "nn":
        (m, k), n = a.shape, b.shape[1]
        a_spec = pl.BlockSpec((tm, tk), lambda i, j, kk: (i, kk))
        b_spec = pl.BlockSpec((tk, tn), lambda i, j, kk: (kk, j))
        dims = NN
    elif mode == "nt":
        (m, k), n = a.shape, b.shape[0]
        a_spec = pl.BlockSpec((tm, tk), lambda i, j, kk: (i, kk))
        b_spec = pl.BlockSpec((tn, tk), lambda i, j, kk: (j, kk + b_off))
        dims = NT
    else:
        (k, m), n = a.shape, b.shape[1]
        a_spec = pl.BlockSpec((tk, tm), lambda i, j, kk: (kk, i))
        b_spec = pl.BlockSpec((tk, tn), lambda i, j, kk: (kk, j))
        dims = TN
    assert m % tm == 0 and n % tn == 0 and k % tk == 0, (name, m, n, k)
    nk = k // tk
    in_specs = [a_spec, b_spec]
    args = [a, b]
    if bias is not None:
        in_specs.append(pl.BlockSpec((1, tn), lambda i, j, kk: (0, j)))
        args.append(bias)
    if res is not None:
        in_specs.append(pl.BlockSpec((tm, tn), lambda i, j, kk: (i, j)))
        args.append(res)

    def finish(out, refs, o_ref):
        pos = 2
        if bias is not None:
            out = out + refs[pos][...]
            pos += 1
        if res is not None:
            out = out + res_scale * refs[pos][...].astype(F32)
        o_ref[...] = out.astype(out_dtype)

    def body_one_step(*refs):
        finish(_dot(refs[0][...], refs[1][...], dims), refs, refs[-1])

    def body(*refs):
        a_ref, b_ref = refs[0], refs[1]
        o_ref, acc = refs[-2], refs[-1]
        kk = pl.program_id(2)

        @pl.when(kk == 0)
        def _():
            acc[...] = jnp.zeros_like(acc)

        acc[...] += _dot(a_ref[...], b_ref[...], dims)

        @pl.when(kk == nk - 1)
        def _():
            finish(acc[...], refs, o_ref)

    (out,), moved = _call(
        body_one_step if nk == 1 else body, name=name, grid=(m // tm, n // tn, nk),
        in_specs=in_specs, out_specs=[pl.BlockSpec((tm, tn), lambda i, j, kk: (i, j))],
        out_shape=[jax.ShapeDtypeStruct((m, n), out_dtype)],
        scratch_shapes=[] if nk == 1 else [pltpu.VMEM((tm, tn), F32)],
        sem=("parallel", "parallel", "arbitrary"), args=args, comm=comm)
    return out if comm is None else (out, moved)


def _matmul_res_ln(a, b, xres, g, beta, name, tm, tk, comm=None):
    t, k = a.shape
    d = b.shape[1]
    nk = k // tk
    assert t % tm == 0 and k % tk == 0

    def body(a_ref, b_ref, x_ref, g_ref, beta_ref, z_ref, xo_ref, acc):
        kk = pl.program_id(1)

        @pl.when(kk == 0)
        def _():
            acc[...] = jnp.zeros_like(acc)

        acc[...] += _dot(a_ref[...], b_ref[...], NN)

        @pl.when(kk == nk - 1)
        def _():
            z = ALPHA * x_ref[...] + acc[...]
            nh, _ = _ln_stats(z)
            z_ref[...] = z
            xo_ref[...] = nh * g_ref[...] + beta_ref[...]

    row = pl.BlockSpec((tm, d), lambda i, kk: (i, 0))
    vec = pl.BlockSpec((1, d), lambda i, kk: (0, 0))
    outs, moved = _call(
        body, name=name, grid=(t // tm, nk),
        in_specs=[pl.BlockSpec((tm, tk), lambda i, kk: (i, kk)), pl.BlockSpec((tk, d), lambda i, kk: (kk, 0)), row, vec, vec],
        out_specs=[row, row],
        out_shape=[jax.ShapeDtypeStruct((t, d), F32), jax.ShapeDtypeStruct((t, d), F32)],
        scratch_shapes=[pltpu.VMEM((tm, d), F32)],
        sem=("parallel", "arbitrary"), args=(a, b, xres, g, beta), comm=comm)
    return outs if comm is None else (outs, moved)


def _ln_bwd(z, g, dres, dbr, name, tm=512):
    t, d = z.shape

    def body(z_ref, g_ref, dres_ref, dbr_ref, dz_ref, dg_ref, db_ref):
        @pl.when(pl.program_id(0) == 0)
        def _():
            dg_ref[...] = jnp.zeros_like(dg_ref)
            db_ref[...] = jnp.zeros_like(db_ref)

        nh, r = _ln_stats(z_ref[...])
        dy = ALPHA * dres_ref[...] + dbr_ref[...].astype(F32)
        dg_ref[...] += _colsum(dy * nh)
        db_ref[...] += _colsum(dy)
        dz_ref[...] = _ln_bwd_rows(dy * g_ref[...], nh, r)

    row = pl.BlockSpec((tm, d), lambda i: (i, 0))
    vec = pl.BlockSpec((1, d), lambda i: (0, 0))
    return pl.pallas_call(
        body, name=name, grid=(t // tm,), in_specs=[row, vec, row, row], out_specs=[row, vec, vec],
        out_shape=[jax.ShapeDtypeStruct((t, d), F32), jax.ShapeDtypeStruct((1, d), F32), jax.ShapeDtypeStruct((1, d), F32)],
        compiler_params=_params("arbitrary"),
    )(z, g, dres, dbr)


def _ln_bwd_loss(z, g, beta, target, name, tm=512):
    t, d = z.shape

    def body(z_ref, g_ref, beta_ref, t_ref, dz_ref, dg_ref, db_ref, loss_ref):
        @pl.when(pl.program_id(0) == 0)
        def _():
            dg_ref[...] = jnp.zeros_like(dg_ref)
            db_ref[...] = jnp.zeros_like(db_ref)
            loss_ref[...] = jnp.zeros_like(loss_ref)

        nh, r = _ln_stats(z_ref[...])
        err = nh * g_ref[...] + beta_ref[...] - t_ref[...]
        loss_ref[...] += _colsum(err * err)
        dy = err * (1.0 / d)
        dg_ref[...] += _colsum(dy * nh)
        db_ref[...] += _colsum(dy)
        dz_ref[...] = _ln_bwd_rows(dy * g_ref[...], nh, r)

    row = pl.BlockSpec((tm, d), lambda i: (i, 0))
    vec = pl.BlockSpec((1, d), lambda i: (0, 0))
    vshape = jax.ShapeDtypeStruct((1, d), F32)
    return pl.pallas_call(
        body, name=name, grid=(t // tm,), in_specs=[row, vec, vec, row], out_specs=[row, vec, vec, vec],
        out_shape=[jax.ShapeDtypeStruct((t, d), F32), vshape, vshape, vshape],
        compiler_params=_params("arbitrary"),
    )(z, g, beta, target)


FFN_HALO = 16


def _ffn_mid_fwd(h, cw, cb, name, tm=512, tc=256, comm=None):
    t, f2 = h.shape
    f = f2 // 2
    nj, nt, hb = f // tc, t // tm, tm // FFN_HALO

    def body(hg, hgp, hv, hvp, cwg, cwv, cbg, cbv, u_ref, sg, sv):
        i = pl.program_id(1)

        def conv(main, prev, s, w, b):
            s[0:FFN_HALO, :] = jnp.where(i > 0, prev[...].astype(F32), 0.0)
            s[FFN_HALO:, :] = main[...].astype(F32)
            o = FFN_HALO - FFN_KERNEL + 1
            return (w[0:1, :] * s[pl.ds(o, tm), :] + w[1:2, :] * s[pl.ds(o + 1, tm), :]
                    + w[2:3, :] * s[pl.ds(o + 2, tm), :] + b[...])

        cg = conv(hg, hgp, sg, cwg, cbg)
        cv = conv(hv, hvp, sv, cwv, cbv)
        u_ref[...] = (_gelu(cg) * cv).astype(BF16)

    def main_spec(off):
        return pl.BlockSpec((tm, tc), lambda j, i: (i, j + off))

    def prev_spec(off):
        return pl.BlockSpec((FFN_HALO, tc), lambda j, i: (jnp.maximum(i * hb - 1, 0), j + off))

    def par_spec(rows, off):
        return pl.BlockSpec((rows, tc), lambda j, i: (0, j + off))

    (u,), moved = _call(
        body, name=name, grid=(nj, nt),
        in_specs=[main_spec(0), prev_spec(0), main_spec(nj), prev_spec(nj),
                  par_spec(FFN_KERNEL, 0), par_spec(FFN_KERNEL, nj), par_spec(1, 0), par_spec(1, nj)],
        out_specs=[pl.BlockSpec((tm, tc), lambda j, i: (i, j))],
        out_shape=[jax.ShapeDtypeStruct((t, f), BF16)],
        scratch_shapes=[pltpu.VMEM((tm + FFN_HALO, tc), F32), pltpu.VMEM((tm + FFN_HALO, tc), F32)],
        sem=("parallel", "arbitrary"), args=(h, h, h, h, cw, cw, cb, cb), comm=comm)
    return u if comm is None else (u, moved)


def _ffn_mid_bwd(h, du, cw, cb, name, tm=512, tc=256, comm=None):
    t, f2 = h.shape
    f = f2 // 2
    nj, nt, hb = f // tc, t // tm, tm // FFN_HALO
    r = tm + FFN_HALO

    def body(hg, hgp, hgn, hv, hvp, hvn, du_ref, dun_ref, cwg, cwv, cbg, cbv,
             dhg_ref, dhv_ref, dcwg_ref, dcwv_ref, dcbg_ref, dcbv_ref, sg, sv, sdg, sdv):
        i = pl.program_id(1)

        @pl.when(i == 0)
        def _():
            for ref in (dcwg_ref, dcwv_ref, dcbg_ref, dcbv_ref):
                ref[...] = jnp.zeros_like(ref)

        o = FFN_HALO - FFN_KERNEL + 1

        def conv(main, prev, nxt, s, w, b):
            s[0:FFN_HALO, :] = jnp.where(i > 0, prev[...].astype(F32), 0.0)
            s[FFN_HALO:FFN_HALO + tm, :] = main[...].astype(F32)
            s[FFN_HALO + tm:, :] = nxt[...].astype(F32)
            return (w[0:1, :] * s[pl.ds(o, r), :] + w[1:2, :] * s[pl.ds(o + 1, r), :]
                    + w[2:3, :] * s[pl.ds(o + 2, r), :] + b[...])

        cg = conv(hg, hgp, hgn, sg, cwg, cbg)
        cv = conv(hv, hvp, hvn, sv, cwv, cbv)
        du_e = jnp.concatenate([du_ref[...].astype(F32), jnp.where(i < nt - 1, dun_ref[...].astype(F32), 0.0)], axis=0)
        gl, dgl = _gelu_and_grad(cg)
        sdg[...] = du_e * cv * dgl
        sdv[...] = du_e * gl

        def back(sd, s, w, dh_ref, dcw_ref, dcb_ref):
            h_own = s[FFN_HALO:FFN_HALO + tm, :]
            ahead = [sd[0:tm, :], sd[pl.ds(1, tm), :], sd[pl.ds(2, tm), :]]
            dcb_ref[...] += _colsum(ahead[0])
            dh = jnp.zeros((tm, tc), F32)
            for k in range(FFN_KERNEL):
                dcw_ref[k:k + 1, :] += _colsum(ahead[FFN_KERNEL - 1 - k] * h_own)
                dh = dh + w[k:k + 1, :] * ahead[FFN_KERNEL - 1 - k]
            dh_ref[...] = dh.astype(BF16)

        back(sdg, sg, cwg, dhg_ref, dcwg_ref, dcbg_ref)
        back(sdv, sv, cwv, dhv_ref, dcwv_ref, dcbv_ref)

    last_blk = t // FFN_HALO - 1

    def main_spec(off):
        return pl.BlockSpec((tm, tc), lambda j, i: (i, j + off))

    def prev_spec(off):
        return pl.BlockSpec((FFN_HALO, tc), lambda j, i: (jnp.maximum(i * hb - 1, 0), j + off))

    def next_spec(off):
        return pl.BlockSpec((FFN_HALO, tc), lambda j, i: (jnp.minimum((i + 1) * hb, last_blk), j + off))

    def par_spec(rows, off):
        return pl.BlockSpec((rows, tc), lambda j, i: (0, j + off))

    out_tile = pl.BlockSpec((tm, tc), lambda j, i: (i, j))
    outs, moved = _call(
        body, name=name, grid=(nj, nt),
        in_specs=[main_spec(0), prev_spec(0), next_spec(0), main_spec(nj), prev_spec(nj), next_spec(nj),
                  main_spec(0), next_spec(0),
                  par_spec(FFN_KERNEL, 0), par_spec(FFN_KERNEL, nj), par_spec(1, 0), par_spec(1, nj)],
        out_specs=[out_tile, out_tile, par_spec(FFN_KERNEL, 0), par_spec(FFN_KERNEL, 0), par_spec(1, 0), par_spec(1, 0)],
        out_shape=[jax.ShapeDtypeStruct((t, f), BF16), jax.ShapeDtypeStruct((t, f), BF16),
                   jax.ShapeDtypeStruct((FFN_KERNEL, f), F32), jax.ShapeDtypeStruct((FFN_KERNEL, f), F32),
                   jax.ShapeDtypeStruct((1, f), F32), jax.ShapeDtypeStruct((1, f), F32)],
        scratch_shapes=[pltpu.VMEM((tm + 2 * FFN_HALO, tc), F32), pltpu.VMEM((tm + 2 * FFN_HALO, tc), F32),
                        pltpu.VMEM((r, tc), F32), pltpu.VMEM((r, tc), F32)],
        sem=("parallel", "arbitrary"), args=(h, h, h, h, h, h, du, du, cw, cw, cb, cb), comm=comm)
    return outs if comm is None else (outs, moved)


MIX_HALO = 32


def _glu(hh):
    return hh[:, 0:A_WIDTH] * _sigmoid(hh[:, A_WIDTH:2 * A_WIDTH])


def _tril_mask():
    return lax.broadcasted_iota(jnp.int32, (B_CHUNK, B_CHUNK), 0) >= lax.broadcasted_iota(jnp.int32, (B_CHUNK, B_CHUNK), 1)


def _spatial_mix(q, ms_ref, sbt_ref, tm):
    mask = _tril_mask()
    ws = [jnp.where(mask, ms_ref[g], 0.0).astype(BF16) for g in range(B_GROUPS)]
    qb = q.astype(BF16)
    rows = []
    for c in range(tm // B_CHUNK):
        cols = [_dot(ws[g], qb[c * B_CHUNK:(c + 1) * B_CHUNK, g * 128:(g + 1) * 128], NN) + sbt_ref[:, g:g + 1]
                for g in range(B_GROUPS)]
        rows.append(jnp.concatenate(cols, axis=1))
    return jnp.concatenate(rows, axis=0)


def _mixer_mid_fwd(h, cw, cb, ag, ab, bg, bb, ms, sbt, name, tm=256, comm=None):
    t = h.shape[0]
    nt, hb = t // tm, tm // MIX_HALO
    o = MIX_HALO - A_KERNEL + 1

    def body(h_ref, hp_ref, cw_ref, cb_ref, ag_ref, ab_ref, bg_ref, bb_ref, ms_ref, sbt_ref, cat_ref, sp):
        i = pl.program_id(0)
        sp[0:MIX_HALO, :] = jnp.where(i > 0, _glu(hp_ref[:, 0:2 * A_WIDTH].astype(F32)), 0.0)
        sp[MIX_HALO:, :] = _glu(h_ref[:, 0:2 * A_WIDTH].astype(F32))
        y = jnp.zeros((tm, A_WIDTH), F32) + cb_ref[...]
        for k in range(A_KERNEL):
            y = y + cw_ref[k:k + 1, :] * sp[pl.ds(o + k, tm), :]
        nh, _ = _ln_stats(y)
        ln = nh * ag_ref[...] + ab_ref[...]
        cat_ref[:, 0:A_WIDTH] = (ln * _sigmoid(ln)).astype(BF16)
        u = _gelu(h_ref[:, 1024:1536].astype(F32))
        nb, _ = _ln_stats(_gelu(h_ref[:, 1536:2048].astype(F32)))
        mixed = _spatial_mix(nb * bg_ref[...] + bb_ref[...], ms_ref, sbt_ref, tm)
        cat_ref[:, A_WIDTH:] = (u * mixed).astype(BF16)

    vec = pl.BlockSpec((1, A_WIDTH), lambda i: (0, 0))
    (cat,), moved = _call(
        body, name=name, grid=(nt,),
        in_specs=[pl.BlockSpec((tm, 2048), lambda i: (i, 0)),
                  pl.BlockSpec((MIX_HALO, 2048), lambda i: (jnp.maximum(i * hb - 1, 0), 0)),
                  pl.BlockSpec((A_KERNEL, A_WIDTH), lambda i: (0, 0)), vec, vec, vec, vec, vec,
                  pl.BlockSpec((B_GROUPS, B_CHUNK, B_CHUNK), lambda i: (0, 0, 0)),
                  pl.BlockSpec((B_CHUNK, B_GROUPS), lambda i: (0, 0))],
        out_specs=[pl.BlockSpec((tm, D_MODEL), lambda i: (i, 0))],
        out_shape=[jax.ShapeDtypeStruct((t, D_MODEL), BF16)],
        scratch_shapes=[pltpu.VMEM((tm + MIX_HALO, A_WIDTH), F32)],
        sem=("parallel",), args=(h, h, cw, cb, ag, ab, bg, bb, ms, sbt), comm=comm)
    return cat if comm is None else (cat, moved)


def _mixer_mid_bwd(h, dcat, cw, cb, ag, ab, bg, bb, ms, mst, sbt, name, tm=256, comm=None):
    t = h.shape[0]
    nt, hb = t // tm, tm // MIX_HALO
    o = MIX_HALO - A_KERNEL + 1
    r = tm + MIX_HALO
    nchunk = tm // B_CHUNK

    def body(h_ref, hp_ref, hn_ref, dc_ref, dcn_ref, cw_ref, cb_ref, ag_ref, ab_ref, bg_ref, bb_ref, ms_ref, mst_ref, sbt_ref,
             dh_ref, dcw_ref, dcb_ref, dag_ref, dab_ref, dbg_ref, dbb_ref, dms_ref, dsb_ref, sp, sdy, sbacc):
        i = pl.program_id(0)

        @pl.when(i == 0)
        def _():
            for ref in (dcw_ref, dcb_ref, dag_ref, dab_ref, dbg_ref, dbb_ref, dms_ref, dsb_ref, sbacc):
                ref[...] = jnp.zeros_like(ref)

        sp[0:MIX_HALO, :] = jnp.where(i > 0, _glu(hp_ref[:, 0:2 * A_WIDTH].astype(F32)), 0.0)
        sp[MIX_HALO:MIX_HALO + tm, :] = _glu(h_ref[:, 0:2 * A_WIDTH].astype(F32))
        sp[MIX_HALO + tm:, :] = _glu(hn_ref[:, 0:2 * A_WIDTH].astype(F32))
        y = jnp.zeros((r, A_WIDTH), F32) + cb_ref[...]
        for k in range(A_KERNEL):
            y = y + cw_ref[k:k + 1, :] * sp[pl.ds(o + k, r), :]
        nh, rs = _ln_stats(y)
        ln = nh * ag_ref[...] + ab_ref[...]
        sg = _sigmoid(ln)
        dao = jnp.concatenate([dc_ref[:, 0:A_WIDTH].astype(F32),
                               jnp.where(i < nt - 1, dcn_ref[:, 0:A_WIDTH].astype(F32), 0.0)], axis=0)
        dln = dao * (sg * (1.0 + ln * (1.0 - sg)))
        dag_ref[...] += _colsum(dln[0:tm] * nh[0:tm])
        dab_ref[...] += _colsum(dln[0:tm])
        sdy[...] = _ln_bwd_rows(dln * ag_ref[...], nh, rs)
        dy_own = sdy[0:tm, :]
        dcb_ref[...] += _colsum(dy_own)
        p_own = sp[MIX_HALO:MIX_HALO + tm, :]
        dp = jnp.zeros((tm, A_WIDTH), F32)
        for k in range(A_KERNEL):
            ahead = sdy[pl.ds(A_KERNEL - 1 - k, tm), :]
            dcw_ref[k:k + 1, :] += _colsum(ahead * p_own)
            dp = dp + cw_ref[k:k + 1, :] * ahead
        av = h_ref[:, 0:A_WIDTH].astype(F32)
        s = _sigmoid(h_ref[:, A_WIDTH:2 * A_WIDTH].astype(F32))
        dh_ref[:, 0:A_WIDTH] = (dp * s).astype(BF16)
        dh_ref[:, A_WIDTH:2 * A_WIDTH] = (dp * av * s * (1.0 - s)).astype(BF16)

        u, dgu = _gelu_and_grad(h_ref[:, 1024:1536].astype(F32))
        w, dgw = _gelu_and_grad(h_ref[:, 1536:2048].astype(F32))
        nb, rb = _ln_stats(w)
        q = nb * bg_ref[...] + bb_ref[...]
        mixed = _spatial_mix(q, ms_ref, sbt_ref, tm)
        dbo = dc_ref[:, A_WIDTH:].astype(F32)
        dh_ref[:, 1024:1536] = (dbo * mixed * dgu).astype(BF16)
        dmx = dbo * u
        mask = _tril_mask()
        wst = [jnp.where(mask.T, mst_ref[g], 0.0).astype(BF16) for g in range(B_GROUPS)]
        qb = q.astype(BF16)
        dmb = dmx.astype(BF16)
        rows = []
        for c in range(nchunk):
            cols = []
            for g in range(B_GROUPS):
                rs_, cs_ = slice(c * B_CHUNK, (c + 1) * B_CHUNK), slice(g * 128, (g + 1) * 128)
                sbacc[g] += dmx[rs_, cs_]
                dms_ref[g] += _dot(dmb[rs_, cs_], qb[rs_, cs_], NT)
                cols.append(_dot(wst[g], dmb[rs_, cs_], NN))
            rows.append(jnp.concatenate(cols, axis=1))
        dq = jnp.concatenate(rows, axis=0)
        dbg_ref[...] += _colsum(dq * nb)
        dbb_ref[...] += _colsum(dq)
        dh_ref[:, 1536:2048] = (_ln_bwd_rows(dq * bg_ref[...], nb, rb) * dgw).astype(BF16)

        @pl.when(i == nt - 1)
        def _():
            for g in range(B_GROUPS):
                dms_ref[g] = jnp.where(mask, dms_ref[g], 0.0)
                dsb_ref[g] = jnp.sum(sbacc[g], axis=1, keepdims=True)

    last_blk = t // MIX_HALO - 1
    vec = pl.BlockSpec((1, A_WIDTH), lambda i: (0, 0))
    mat = pl.BlockSpec((B_GROUPS, B_CHUNK, B_CHUNK), lambda i: (0, 0, 0))
    taps = pl.BlockSpec((A_KERNEL, A_WIDTH), lambda i: (0, 0))

    def halo(width, which):
        if which == "prev":
            return pl.BlockSpec((MIX_HALO, width), lambda i: (jnp.maximum(i * hb - 1, 0), 0))
        return pl.BlockSpec((MIX_HALO, width), lambda i: (jnp.minimum((i + 1) * hb, last_blk), 0))

    vshape = jax.ShapeDtypeStruct((1, A_WIDTH), F32)
    outs, moved = _call(
        body, name=name, grid=(nt,),
        in_specs=[pl.BlockSpec((tm, 2048), lambda i: (i, 0)), halo(2048, "prev"), halo(2048, "next"),
                  pl.BlockSpec((tm, D_MODEL), lambda i: (i, 0)), halo(D_MODEL, "next"),
                  taps, vec, vec, vec, vec, vec, mat, mat, pl.BlockSpec((B_CHUNK, B_GROUPS), lambda i: (0, 0))],
        out_specs=[pl.BlockSpec((tm, 2048), lambda i: (i, 0)), taps, vec, vec, vec, vec, vec, mat,
                   pl.BlockSpec((B_GROUPS, B_CHUNK, 1), lambda i: (0, 0, 0))],
        out_shape=[jax.ShapeDtypeStruct((t, 2048), BF16), jax.ShapeDtypeStruct((A_KERNEL, A_WIDTH), F32),
                   vshape, vshape, vshape, vshape, vshape,
                   jax.ShapeDtypeStruct((B_GROUPS, B_CHUNK, B_CHUNK), F32), jax.ShapeDtypeStruct((B_GROUPS, B_CHUNK, 1), F32)],
        scratch_shapes=[pltpu.VMEM((tm + 2 * MIX_HALO, A_WIDTH), F32), pltpu.VMEM((r, A_WIDTH), F32),
                        pltpu.VMEM((B_GROUPS, B_CHUNK, B_CHUNK), F32)],
        sem=("arbitrary",), args=(h, h, h, dcat, dcat, cw, cb, ag, ab, bg, bb, ms, mst, sbt), comm=comm)
    return outs if comm is None else (outs, moved)


Q_WIDTH = N_Q_HEADS * HEAD_DIM
KV_WIDTH = 2 * N_KV_HEADS * HEAD_DIM
PAIRS_PER_KV = N_Q_HEADS // N_KV_HEADS // 2
ATT_SCALE = 1.0 / math.sqrt(HEAD_DIM)


def _att_mask(n):
    qi = lax.broadcasted_iota(jnp.int32, (ATT_BLOCK, 2 * ATT_BLOCK), 0)
    sj = lax.broadcasted_iota(jnp.int32, (ATT_BLOCK, 2 * ATT_BLOCK), 1)
    diff = qi + ATT_BLOCK - sj
    return (diff >= 0) & (diff < ATT_BLOCK) & ((n > 0) | (sj >= ATT_BLOCK))


def _dup_heads(pair_cols, kv_head):
    lane = lax.broadcasted_iota(jnp.int32, pair_cols.shape, 1)
    rolled = pltpu.roll(pair_cols, HEAD_DIM, 1)
    first = lane < HEAD_DIM
    return jnp.where(first, pair_cols, rolled) if kv_head == 0 else jnp.where(first, rolled, pair_cols)


HEADS_PER_KV = N_Q_HEADS // N_KV_HEADS


def _stack_heads(ref, kh):
    lane = lax.broadcasted_iota(jnp.int32, (ATT_BLOCK, 128), 1)
    rows = []
    for pr in range(PAIRS_PER_KV):
        c0 = (kh * PAIRS_PER_KV + pr) * 128
        pair = ref[:, c0:c0 + 128]
        rows += [jnp.where(lane < HEAD_DIM, pair, jnp.zeros_like(pair)), jnp.where(lane < HEAD_DIM, jnp.zeros_like(pair), pair)]
    return jnp.concatenate(rows, axis=0)


def _unstack_heads(stacked, kh, write):
    lane = lax.broadcasted_iota(jnp.int32, (ATT_BLOCK, 128), 1)
    for pr in range(PAIRS_PER_KV):
        first = stacked[(2 * pr) * ATT_BLOCK:(2 * pr + 1) * ATT_BLOCK]
        second = stacked[(2 * pr + 1) * ATT_BLOCK:(2 * pr + 2) * ATT_BLOCK]
        write((kh * PAIRS_PER_KV + pr) * 128, jnp.where(lane < HEAD_DIM, first, second))


def _sink_column(sink_ref, kh):
    return jnp.concatenate([jnp.full((ATT_BLOCK, 1), sink_ref[0, kh * HEADS_PER_KV + h], F32) for h in range(HEADS_PER_KV)], axis=0)


def _att_probs(q_all, k2, mask, sink):
    s = _dot(q_all, k2, NT) * ATT_SCALE
    s = jnp.where(jnp.concatenate([mask] * HEADS_PER_KV, axis=0), s, -jnp.inf)
    m = jnp.maximum(jnp.max(s, axis=-1, keepdims=True), sink)
    e = jnp.exp(s - m)
    es = jnp.exp(sink - m)
    inv = 1.0 / (jnp.sum(e, axis=-1, keepdims=True) + es)
    return e * inv, es * inv


def _attn_fwd(qkv, sinks, name):
    t = qkv.shape[0]
    nb = t // ATT_BLOCK
    kvb = Q_WIDTH // KV_WIDTH

    def body(sink_ref, q_ref, kv_ref, kvp_ref, o_ref):
        n = pl.program_id(0)
        mask = _att_mask(n)
        kv = jnp.concatenate([kvp_ref[...], kv_ref[...]], axis=0).astype(F32)

        def write(c0, pair):
            o_ref[:, c0:c0 + 128] = pair.astype(BF16)

        for kh in range(N_KV_HEADS):
            k2 = _dup_heads(kv[:, 0:128], kh).astype(BF16)
            v2 = _dup_heads(kv[:, 128:256], kh).astype(BF16)
            p, _ = _att_probs(_stack_heads(q_ref, kh), k2, mask, _sink_column(sink_ref, kh))
            _unstack_heads(_dot(p, v2, NN), kh, write)

    return pl.pallas_call(
        body, name=name, grid=(nb,),
        in_specs=[pl.BlockSpec(memory_space=pltpu.SMEM),
                  pl.BlockSpec((ATT_BLOCK, Q_WIDTH), lambda n: (n, 0)),
                  pl.BlockSpec((ATT_BLOCK, KV_WIDTH), lambda n: (n, kvb)),
                  pl.BlockSpec((ATT_BLOCK, KV_WIDTH), lambda n: (jnp.maximum(n - 1, 0), kvb))],
        out_specs=pl.BlockSpec((ATT_BLOCK, Q_WIDTH), lambda n: (n, 0)),
        out_shape=jax.ShapeDtypeStruct((t, Q_WIDTH), BF16),
        compiler_params=_params("parallel"),
    )(sinks, qkv, qkv, qkv)


def _attn_bwd(qkv, d_o, sinks, name, comm=None):
    t = qkv.shape[0]
    nb = t // ATT_BLOCK
    kvb = Q_WIDTH // KV_WIDTH

    def body(sink_ref, q_ref, kv_ref, kvp_ref, do_ref, dq_ref, dkv_ref, dbq_ref, dbkv_ref, dsink_ref, carry):
        n = pl.program_id(0)

        @pl.when(n == 0)
        def _():
            for ref in (dbq_ref, dbkv_ref, dsink_ref, carry):
                ref[...] = jnp.zeros_like(ref)
            dkv_ref[...] = jnp.zeros_like(dkv_ref)

        @pl.when(n < nb)
        def _():
            mask = _att_mask(n)
            kv = jnp.concatenate([kvp_ref[...], kv_ref[...]], axis=0).astype(F32)
            lane2 = lax.broadcasted_iota(jnp.int32, (2 * ATT_BLOCK, 128), 1)
            sink_lane = lax.broadcasted_iota(jnp.int32, (1, 128), 1)
            dsink = jnp.zeros((1, 128), F32)
            dk_parts, dv_parts = [], []

            def write(c0, pair):
                dbq_ref[:, c0:c0 + 128] += _colsum(pair)
                dq_ref[:, c0:c0 + 128] = pair.astype(BF16)

            for kh in range(N_KV_HEADS):
                k2 = _dup_heads(kv[:, 0:128], kh).astype(BF16)
                v2 = _dup_heads(kv[:, 128:256], kh).astype(BF16)
                q_all = _stack_heads(q_ref, kh)
                do_all = _stack_heads(do_ref, kh)
                p, ps = _att_probs(q_all, k2, mask, _sink_column(sink_ref, kh))
                dp = _dot(do_all, v2, NT)
                delta = jnp.sum(p * dp, axis=-1, keepdims=True)
                ds = p * (dp - delta) * ATT_SCALE
                psd = ps * delta
                for h in range(HEADS_PER_KV):
                    dsink = dsink + jnp.where(sink_lane == kh * HEADS_PER_KV + h,
                                              -jnp.sum(psd[h * ATT_BLOCK:(h + 1) * ATT_BLOCK]), 0.0)
                _unstack_heads(_dot(ds, k2, NN), kh, write)
                dk_acc = _dot(ds, q_all, TN)
                dv_acc = _dot(p, do_all, TN)
                dk_parts.append(dk_acc + pltpu.roll(dk_acc, HEAD_DIM, 1))
                dv_parts.append(dv_acc + pltpu.roll(dv_acc, HEAD_DIM, 1))
            dk = jnp.where(lane2 < HEAD_DIM, dk_parts[0], dk_parts[1])
            dv = jnp.where(lane2 < HEAD_DIM, dv_parts[0], dv_parts[1])
            dkv_new = jnp.concatenate([dk, dv], axis=1)
            done = carry[...] + dkv_new[0:ATT_BLOCK]

            @pl.when(n > 0)
            def _():
                dkv_ref[...] = done.astype(BF16)
                dbkv_ref[...] += _colsum(done)

            carry[...] = dkv_new[ATT_BLOCK:]
            dsink_ref[...] += dsink

        @pl.when(n == nb)
        def _():
            dkv_ref[...] = carry[...].astype(BF16)
            dbkv_ref[...] += _colsum(carry[...])

    def clamp(n):
        return jnp.minimum(n, nb - 1)

    outs, moved = _call(
        body, name=name, grid=(nb + 1,),
        in_specs=[pl.BlockSpec(memory_space=pltpu.SMEM),
                  pl.BlockSpec((ATT_BLOCK, Q_WIDTH), lambda n: (clamp(n), 0)),
                  pl.BlockSpec((ATT_BLOCK, KV_WIDTH), lambda n: (clamp(n), kvb)),
                  pl.BlockSpec((ATT_BLOCK, KV_WIDTH), lambda n: (jnp.maximum(clamp(n) - 1, 0), kvb)),
                  pl.BlockSpec((ATT_BLOCK, Q_WIDTH), lambda n: (clamp(n), 0))],
        out_specs=[pl.BlockSpec((ATT_BLOCK, Q_WIDTH), lambda n: (clamp(n), 0)),
                   pl.BlockSpec((ATT_BLOCK, KV_WIDTH), lambda n: (jnp.maximum(n - 1, 0), 0)),
                   pl.BlockSpec((1, Q_WIDTH), lambda n: (0, 0)),
                   pl.BlockSpec((1, KV_WIDTH), lambda n: (0, 0)),
                   pl.BlockSpec((1, 128), lambda n: (0, 0))],
        out_shape=[jax.ShapeDtypeStruct((t, Q_WIDTH), BF16), jax.ShapeDtypeStruct((t, KV_WIDTH), BF16),
                   jax.ShapeDtypeStruct((1, Q_WIDTH), F32), jax.ShapeDtypeStruct((1, KV_WIDTH), F32),
                   jax.ShapeDtypeStruct((1, 128), F32)],
        scratch_shapes=[pltpu.VMEM((ATT_BLOCK, KV_WIDTH), F32)],
        sem=("arbitrary",), args=(sinks, qkv, qkv, qkv, d_o), comm=comm)
    return outs if comm is None else (outs, moved)


def _adamw_math(g, w, m, v):
    m = ADAM_B1 * m + (1.0 - ADAM_B1) * g
    v = ADAM_B2 * v + (1.0 - ADAM_B2) * (g * g)
    m_hat = m / (1.0 - ADAM_B1 ** ADAM_STEP)
    v_hat = v / (1.0 - ADAM_B2 ** ADAM_STEP)
    delta = -ADAM_LR * (m_hat / (jnp.sqrt(v_hat) + ADAM_EPS) + ADAM_WD * w)
    return delta, m, v


def _sum_partials(p_ref):
    g = p_ref[0].astype(F32)
    for s in range(1, N_DEV):
        g = g + p_ref[s].astype(F32)
    return g


def _adamw_big(parts, w, m, v, name, tr):
    r, c = w.shape
    tiles = [p.shape[1] // tr for p in parts]
    starts = [sum(tiles[:l]) for l in range(len(parts))]
    assert all(p.shape[1] % tr == 0 for p in parts) and sum(tiles) * tr == r

    def body(*refs):
        p_refs, (w_ref, m_ref, v_ref, g_out, d_out, m_out, v_out) = refs[:len(parts)], refs[len(parts):]
        i = pl.program_id(0)
        for l, p_ref in enumerate(p_refs):
            @pl.when((i >= starts[l]) & (i < starts[l] + tiles[l]))
            def _():
                g = _sum_partials(p_ref)
                g_out[...] = g
                d_out[...], m_out[...], v_out[...] = _adamw_math(g, w_ref[...], m_ref[...], v_ref[...])

    def part_spec(l):
        return pl.BlockSpec((N_DEV, tr, c), lambda i: (0, jnp.clip(i - starts[l], 0, tiles[l] - 1), 0))

    tile = pl.BlockSpec((tr, c), lambda i: (i, 0))
    shape = jax.ShapeDtypeStruct((r, c), F32)
    return pl.pallas_call(
        body, name=name, grid=(r // tr,),
        in_specs=[part_spec(l) for l in range(len(parts))] + [tile, tile, tile],
        out_specs=[tile] * 4, out_shape=[shape] * 4,
        compiler_params=_params("parallel"),
    )(*parts, w, m, v)


def _adamw_small(parts, ws, ms, vs, name):
    n = len(ws)

    def body(*refs):
        ins, outs = refs[:4 * n], refs[4 * n:]
        for a in range(n):
            g = _sum_partials(ins[a])
            outs[4 * a][...] = g
            outs[4 * a + 1][...], outs[4 * a + 2][...], outs[4 * a + 3][...] = _adamw_math(
                g, ins[n + a][...], ins[2 * n + a][...], ins[3 * n + a][...])

    out_shape = []
    for w in ws:
        out_shape += [jax.ShapeDtypeStruct(w.shape, F32)] * 4
    return pl.pallas_call(body, name=name, out_shape=out_shape, compiler_params=_params())(*parts, *ws, *ms, *vs)


PACK_LANES = 128
PACK_ROWS = 8


def _pack(arrs):
    flat = jnp.concatenate([a.reshape(-1).astype(F32) for a in arrs])
    unit = PACK_LANES * PACK_ROWS
    total = -(-flat.shape[0] // unit) * unit
    return jnp.pad(flat, (0, total - flat.shape[0])).reshape(-1, PACK_LANES)


def _unpack(buf, shapes):
    flat = buf.reshape(N_DEV, -1)
    out, pos = [], 0
    for s in shapes:
        size = math.prod(s)
        out.append(flat[:, pos:pos + size].reshape((N_DEV,) + tuple(s)))
        pos += size
    return out


def _interleave(g):
    return jnp.transpose(g, (1, 0, 2)).reshape(g.shape[1], -1)


def _deinterleave(w):
    r = w.shape[0]
    return jnp.transpose(w.reshape(r, N_DEV, -1), (1, 0, 2))


def _ffn_backward(dz, x_in, h, u, w_up, cw, cb, w_down, tag, comm=None):
    du = _matmul(dz, w_down, "nt", BF16, f"ffn{tag}_du", 1024, 1408, 1024)
    d_w_down = _matmul(u, dz, "tn", BF16, f"ffn{tag}_dwdown", 1408, 1024, 512)
    outs = _ffn_mid_bwd(h, du, cw, cb, f"ffn{tag}_mid_bwd", comm=comm)
    (dhg, dhv, dcwg, dcwv, dcbg, dcbv), moved = outs if comm is not None else (outs, [])
    d_w_up = jnp.concatenate([_matmul(x_in, dhg, "tn", BF16, f"ffn{tag}_dwup_gate", 1024, 1408, 1024),
                              _matmul(x_in, dhv, "tn", BF16, f"ffn{tag}_dwup_value", 1024, 1408, 1024)], axis=1)
    dx = _matmul(dhv, w_up, "nt", F32, f"ffn{tag}_dx_value", 1024, 1024, 1408, b_off=D_FF // 1408)
    dx = _matmul(dhg, w_up, "nt", F32, f"ffn{tag}_dx_gate", 1024, 1024, 1408, res=dx)
    return (dx, _deinterleave(d_w_up), d_w_down.reshape(N_DEV, -1, D_MODEL),
            jnp.concatenate([dcwg, dcwv], axis=1), jnp.concatenate([dcbg, dcbv], axis=1), moved)


def kernel(x, ab_w_in, a_conv_w, a_conv_b, a_norm_g, a_norm_b, b_norm_g, b_norm_b, b_spatial_w, b_spatial_b, ab_w_out, c_w_qkv, c_b_qkv, c_sinks, c_w_o, ffn_w_up, ffn_conv_w, ffn_conv_b, ffn_w_down, ln_g, ln_b, loss_target, m_ab_w_in, m_a_conv_w, m_a_conv_b, m_a_norm_g, m_a_norm_b, m_b_norm_g, m_b_norm_b, m_b_spatial_w, m_b_spatial_b, m_ab_w_out, m_c_w_qkv, m_c_b_qkv, m_c_sinks, m_c_w_o, m_ffn_w_up, m_ffn_conv_w, m_ffn_conv_b, m_ffn_w_down, m_ln_g, m_ln_b, v_ab_w_in, v_a_conv_w, v_a_conv_b, v_a_norm_g, v_a_norm_b, v_b_norm_g, v_b_norm_b, v_b_spatial_w, v_b_spatial_b, v_ab_w_out, v_c_w_qkv, v_c_b_qkv, v_c_sinks, v_c_w_o, v_ffn_w_up, v_ffn_conv_w, v_ffn_conv_b, v_ffn_w_down, v_ln_g, v_ln_b):
    me = 4 * lax.axis_index("x") + 2 * lax.axis_index("y") + lax.axis_index("c")
    xt = x[0]
    t = xt.shape[0]

    small_shard_shapes = [a_conv_w.shape, c_b_qkv.shape, ffn_conv_w.shape, ln_g.shape, ln_b.shape]
    up_shard = [ffn_w_up[l].astype(BF16) for l in range(2)]
    down_shard = [ffn_w_down[l].astype(BF16) for l in range(2)]
    g_win, g_wout, g_small = _comm_only(
        _Comm(gather=[ab_w_in[0].astype(BF16), ab_w_out[0].astype(BF16), _pack([a_conv_w, c_b_qkv, ffn_conv_w, ln_g, ln_b])]),
        "gather_first")
    w_in = _interleave(g_win)
    w_out = g_wout.reshape(D_MODEL, D_MODEL)
    g_acw, g_bqkv, g_fcw, g_lng, g_lnb = _unpack(g_small, small_shard_shapes)
    acw = _interleave(g_acw[:, 0])
    bqkv = g_bqkv[:, 0].reshape(1, -1)
    fcw = [_interleave(g_fcw[:, l]) for l in range(2)]
    lng = jnp.transpose(g_lng, (1, 2, 0, 3)).reshape(2, 2, 1, D_MODEL)
    lnb = jnp.transpose(g_lnb, (1, 2, 0, 3)).reshape(2, 2, 1, D_MODEL)
    fcb = [ffn_conv_b[l:l + 1] for l in range(2)]
    ms = b_spatial_w[0]
    mst = jnp.swapaxes(ms, 1, 2)
    sbt = b_spatial_b[0].T

    h0 = _matmul(xt, w_in, "nn", BF16, "mix_in", 1024, 1024, 1024)
    cat, (g_wup0,) = _mixer_mid_fwd(h0, acw, a_conv_b, a_norm_g, a_norm_b, b_norm_g, b_norm_b, ms, sbt, "mix_mid_fwd",
                                    comm=_Comm(gather=[up_shard[0]]))
    w_up0 = _interleave(g_wup0)
    z1, x1 = _matmul_res_ln(cat, w_out, xt, lng[0, 0], lnb[0, 0], "mix_out_ln", 512, D_MODEL)
    hf0, (g_wdown0, g_wqkv, g_wo) = _matmul(x1, w_up0, "nn", BF16, "ffn0_up", 1024, 1408, 1024,
                                            comm=_Comm(gather=[down_shard[0], c_w_qkv[0].astype(BF16), c_w_o[0].astype(BF16)]))
    w_down0 = g_wdown0.reshape(D_FF, D_MODEL)
    w_qkv = _interleave(g_wqkv)
    w_o = g_wo.reshape(D_MODEL, D_MODEL)
    u0, (g_wup1,) = _ffn_mid_fwd(hf0, fcw[0], fcb[0], "ffn0_mid_fwd", comm=_Comm(gather=[up_shard[1]]))
    w_up1 = _interleave(g_wup1)
    (z2, x2), (g_wdown1,) = _matmul_res_ln(u0, w_down0, x1, lng[0, 1], lnb[0, 1], "ffn0_down_ln", 512, D_FF,
                                           comm=_Comm(gather=[down_shard[1]]))
    w_down1 = g_wdown1.reshape(D_FF, D_MODEL)
    qkv = _matmul(x2, w_qkv, "nn", BF16, "att_qkv", 1024, 1280, 1024, bias=bqkv)
    att = _attn_fwd(qkv, c_sinks, "att_fwd")
    z3, x3 = _matmul_res_ln(att, w_o, x2, lng[1, 0], lnb[1, 0], "att_out_ln", 512, D_MODEL)
    hf1 = _matmul(x3, w_up1, "nn", BF16, "ffn1_up", 1024, 1408, 1024)
    u1 = _ffn_mid_fwd(hf1, fcw[1], fcb[1], "ffn1_mid_fwd")
    z4, _ = _matmul_res_ln(u1, w_down1, x3, lng[1, 1], lnb[1, 1], "ffn1_down_ln", 512, D_FF)

    dz4, dg11, db11, loss_terms = _ln_bwd_loss(z4, lng[1, 1], lnb[1, 1], loss_target[0], "loss_ln_bwd")
    dx3, d_wup1, d_wdown1, d_fcw1, d_fcb1, _ = _ffn_backward(dz4, x3, hf1, u1, w_up1, fcw[1], fcb[1], w_down1, 1)
    dz3, dg10, db10 = _ln_bwd(z3, lng[1, 0], dz4, dx3, "ln10_bwd")
    d_att = _matmul(dz3, w_o, "nt", BF16, "att_dout", 1024, 1024, 1024)
    d_wo = _matmul(att, dz3, "tn", BF16, "att_dwo", 1024, 1024, 512)
    (dq, dkv, dbq, dbkv, dsinks), (p_wup1, p_wdown1) = _attn_bwd(qkv, d_att, c_sinks, "att_bwd",
                                                                  comm=_Comm(exchange=[d_wup1, d_wdown1]))
    d_wqkv = jnp.concatenate([_matmul(x2, dq, "tn", BF16, "att_dwq", 1024, 1024, 512),
                              _matmul(x2, dkv, "tn", BF16, "att_dwkv", 1024, KV_WIDTH, 512)], axis=1)
    dx2 = _matmul(dkv, w_qkv, "nt", F32, "att_dx_kv", 1024, 1024, KV_WIDTH, b_off=Q_WIDTH // KV_WIDTH)
    dx2 = _matmul(dq, w_qkv, "nt", F32, "att_dx_q", 1024, 1024, KV_WIDTH, res=dx2)
    dz2, dg01, db01 = _ln_bwd(z2, lng[0, 1], dz3, dx2, "ln01_bwd")
    dx1, d_wup0, d_wdown0, d_fcw0, d_fcb0, (p_wqkv, p_wo) = _ffn_backward(
        dz2, x1, hf0, u0, w_up0, fcw[0], fcb[0], w_down0, 0,
        comm=_Comm(exchange=[_deinterleave(d_wqkv), d_wo.reshape(N_DEV, -1, D_MODEL)]))
    dz1, dg00, db00 = _ln_bwd(z1, lng[0, 0], dz2, dx1, "ln00_bwd")
    dcat = _matmul(dz1, w_out, "nt", BF16, "mix_dcat", 1024, 1024, 1024)
    d_wout = _matmul(cat, dz1, "tn", BF16, "mix_dwout", 1024, 1024, 512)
    (dh0, d_acw, d_acb, d_ang, d_anb, d_bng, d_bnb, d_ms, d_sb), (p_wup0, p_wdown0) = _mixer_mid_bwd(
        h0, dcat, acw, a_conv_b, a_norm_g, a_norm_b, b_norm_g, b_norm_b, ms, mst, sbt, "mix_mid_bwd",
        comm=_Comm(exchange=[d_wup0, d_wdown0]))
    d_win = _matmul(xt, dh0, "tn", BF16, "mix_dwin", 1024, 1024, 512)
    grad_x = _matmul(dh0, w_in, "nt", F32, "mix_dx", 1024, 1024, 1024, res=dz1, res_scale=ALPHA)

    loss = lax.psum(0.5 / D_MODEL * jnp.sum(loss_terms), ("x", "y", "c"))

    d_bqkv = jnp.concatenate([dbq, dbkv], axis=1)
    d_lng = jnp.stack([jnp.stack([dg00, dg01]), jnp.stack([dg10, dg11])])
    d_lnb = jnp.stack([jnp.stack([db00, db01]), jnp.stack([db10, db11])])
    small_full = [d_acb, d_ang, d_anb, d_bng, d_bnb, d_ms, d_sb, dsinks[:, :N_Q_HEADS], jnp.concatenate([d_fcb0, d_fcb1], axis=0),
                  d_acw, d_bqkv, jnp.stack([d_fcw0, d_fcw1]), d_lng, d_lnb]
    g_small_grads, p_win, p_wout = _comm_only(
        _Comm(gather=[_pack(small_full)], exchange=[_deinterleave(d_win), d_wout.reshape(N_DEV, -1, D_MODEL)]), "exchange_last")

    big = {}
    for nm, p, w, m, v, tr in [
            ("ab_w_in", [p_win], ab_w_in, m_ab_w_in, v_ab_w_in, 256), ("ab_w_out", [p_wout], ab_w_out, m_ab_w_out, v_ab_w_out, 128),
            ("c_w_qkv", [p_wqkv], c_w_qkv, m_c_w_qkv, v_c_w_qkv, 256), ("c_w_o", [p_wo], c_w_o, m_c_w_o, v_c_w_o, 128),
            ("ffn_w_up", [p_wup0, p_wup1], ffn_w_up, m_ffn_w_up, v_ffn_w_up, 256),
            ("ffn_w_down", [p_wdown0, p_wdown1], ffn_w_down, m_ffn_w_down, v_ffn_w_down, 176)]:
        two_d = (-1, w.shape[-1])
        outs = _adamw_big(p, w.reshape(two_d), m.reshape(two_d), v.reshape(two_d), "adamw_" + nm, tr)
        big[nm] = [o.reshape(w.shape) for o in outs]

    gs = _unpack(g_small_grads, [a.shape for a in small_full])

    def my_shard(g, width):
        g = g.reshape(g.shape[:-1] + (N_DEV, width))
        return lax.dynamic_index_in_dim(g, me, axis=g.ndim - 2, keepdims=False)

    small_names = ["a_conv_b", "a_norm_g", "a_norm_b", "b_norm_g", "b_norm_b", "b_spatial_w", "b_spatial_b", "c_sinks", "ffn_conv_b",
                   "a_conv_w", "c_b_qkv", "ffn_conv_w", "ln_g", "ln_b"]
    small_w = [a_conv_b, a_norm_g, a_norm_b, b_norm_g, b_norm_b, b_spatial_w, b_spatial_b, c_sinks, ffn_conv_b,
               a_conv_w, c_b_qkv, ffn_conv_w, ln_g, ln_b]
    small_m = [m_a_conv_b, m_a_norm_g, m_a_norm_b, m_b_norm_g, m_b_norm_b, m_b_spatial_w, m_b_spatial_b, m_c_sinks, m_ffn_conv_b,
               m_a_conv_w, m_c_b_qkv, m_ffn_conv_w, m_ln_g, m_ln_b]
    small_v = [v_a_conv_b, v_a_norm_g, v_a_norm_b, v_b_norm_g, v_b_norm_b, v_b_spatial_w, v_b_spatial_b, v_c_sinks, v_ffn_conv_b,
               v_a_conv_w, v_c_b_qkv, v_ffn_conv_w, v_ln_g, v_ln_b]
    gs[9:] = [my_shard(g, w.shape[-1]) for g, w in zip(gs[9:], small_w[9:])]
    two_d = [(-1, w.shape[-1]) for w in small_w]
    outs = _adamw_small([g.reshape((N_DEV,) + w.reshape(s).shape) for g, w, s in zip(gs, small_w, two_d)],
                        [w.reshape(s) for w, s in zip(small_w, two_d)], [m.reshape(s) for m, s in zip(small_m, two_d)],
                        [v.reshape(s) for v, s in zip(small_v, two_d)], "adamw_small")
    small = {nm: [o.reshape(w.shape) for o in outs[4 * a:4 * a + 4]] for a, (nm, w) in enumerate(zip(small_names, small_w))}

    res = {**big, **small}
    order = ["ab_w_in", "a_conv_w", "a_conv_b", "a_norm_g", "a_norm_b", "b_norm_g", "b_norm_b", "b_spatial_w", "b_spatial_b", "ab_w_out",
             "c_w_qkv", "c_b_qkv", "c_sinks", "c_w_o", "ffn_w_up", "ffn_conv_w", "ffn_conv_b", "ffn_w_down", "ln_g", "ln_b"]
    return (loss, grad_x[None], *[res[nm][0] for nm in order], *[res[nm][1] for nm in order],
            *[res[nm][2] for nm in order], *[res[nm][3] for nm in order])
```

```python
import functools
import math

import jax
import jax.numpy as jnp
from jax import lax
from jax.experimental import pallas as pl
from jax.experimental.pallas import tpu as pltpu

F32 = jnp.float32
BF16 = jnp.bfloat16

N_DEV = 8
D_MODEL = 1024
A_WIDTH = 512
A_KERNEL = 31
B_GROUPS = 4
B_CHUNK = 128
HEAD_DIM = 64
N_Q_HEADS = 16
N_KV_HEADS = 2
ATT_BLOCK = 128
D_FF = 2816
FFN_KERNEL = 3
ALPHA = (2.0 * 2) ** 0.25
LN_EPS = 1e-5
GELU_K = math.sqrt(2.0 / math.pi)
GELU_C = 0.044715
ADAM_LR = 0.001
ADAM_B1 = 0.9
ADAM_B2 = 0.999
ADAM_EPS = 1e-08
ADAM_WD = 0.01
ADAM_STEP = 10
VMEM_LIMIT = 56 * 1024 * 1024
MESH_ID = pl.DeviceIdType.MESH


def _params(*sem):
    return pltpu.CompilerParams(dimension_semantics=sem, vmem_limit_bytes=VMEM_LIMIT)


def _gelu(x):
    t = jnp.tanh(GELU_K * x * (1.0 + GELU_C * x * x))
    return 0.5 * x * (1.0 + t)


def _gelu_and_grad(x):
    x2 = x * x
    t = jnp.tanh(GELU_K * x * (1.0 + GELU_C * x2))
    g = 0.5 * x * (1.0 + t)
    dg = 0.5 * (1.0 + t) + 0.5 * x * (1.0 - t * t) * (GELU_K * (1.0 + 3.0 * GELU_C * x2))
    return g, dg


def _sigmoid(x):
    return 1.0 / (1.0 + jnp.exp(-x))


def _ln_stats(z):
    mu = jnp.mean(z, axis=-1, keepdims=True)
    zc = z - mu
    var = jnp.mean(zc * zc, axis=-1, keepdims=True)
    r = lax.rsqrt(var + LN_EPS)
    return zc * r, r


def _ln_bwd_rows(dn, nh, r):
    return r * (dn - jnp.mean(dn, axis=-1, keepdims=True) - nh * jnp.mean(dn * nh, axis=-1, keepdims=True))


def _colsum(x):
    return jnp.sum(x, axis=0, keepdims=True)


def _dot(a, b, dims):
    return lax.dot_general(a.astype(BF16), b.astype(BF16), (dims, ((), ())), preferred_element_type=F32)


NN = ((1,), (0,))
NT = ((1,), (1,))
TN = ((0,), (0,))


ANY = pl.BlockSpec(memory_space=pl.ANY)
N_RELATIONS = N_DEV - 1


def _my_place():
    return lax.axis_index("x"), lax.axis_index("y"), lax.axis_index("c")


class _Comm:
    def __init__(self, gather=(), exchange=()):
        self.arrs = list(gather) + list(exchange)
        self.n_gather = len(gather)
        self.n = len(self.arrs)

    def out_shape(self):
        return [jax.ShapeDtypeStruct(((N_DEV,) + a.shape) if i < self.n_gather else a.shape, a.dtype)
                for i, a in enumerate(self.arrs)]

    def sems(self):
        return [pltpu.SemaphoreType.DMA((self.n, N_RELATIONS)), pltpu.SemaphoreType.DMA((self.n, N_RELATIONS)),
                pltpu.SemaphoreType.DMA((self.n,))]

    def _gather_copy(self, ins, outs, sems, a, k, place, to, from_input=False):
        px, py, pc = place
        block = outs[a].at[4 * px + 2 * py + pc]
        return pltpu.make_async_remote_copy(
            src_ref=ins[a] if from_input else block, dst_ref=block,
            send_sem=sems[0].at[a, k], recv_sem=sems[1].at[a, k], device_id=to, device_id_type=MESH_ID)

    def _exchange_copy(self, ins, outs, sems, a, k, landing=False):
        x, y, c = _my_place()
        me = 4 * x + 2 * y + c
        peer = (x ^ (k >> 2), y ^ ((k >> 1) & 1), c ^ (k & 1))
        return pltpu.make_async_remote_copy(
            src_ref=ins[a].at[me ^ k], dst_ref=outs[a].at[(me ^ k) if landing else me],
            send_sem=sems[0].at[a, k - 1], recv_sem=sems[1].at[a, k - 1], device_id=peer, device_id_type=MESH_ID)

    def _local_copy(self, ins, outs, sems, a):
        x, y, c = _my_place()
        me = 4 * x + 2 * y + c
        src = ins[a] if a < self.n_gather else ins[a].at[me]
        return pltpu.make_async_copy(src, outs[a].at[me], sems[2].at[a])

    def _first_stage(self, ins, outs, sems, a):
        x, y, c = _my_place()
        me = (x, y, c)
        chips = [(1 - x, y), (x, 1 - y), (1 - x, 1 - y)]
        return ([self._gather_copy(ins, outs, sems, a, 0, me, (x, y, 1 - c), from_input=True)]
                + [self._gather_copy(ins, outs, sems, a, 1 + j, me, (*chip, c), from_input=True) for j, chip in enumerate(chips)])

    def start(self, ins, outs, sems):
        for a in range(self.n):
            self._local_copy(ins, outs, sems, a).start()
        for a in range(self.n_gather):
            for cp in self._first_stage(ins, outs, sems, a):
                cp.start()
        for k in range(1, N_DEV):
            for a in range(self.n_gather, self.n):
                self._exchange_copy(ins, outs, sems, a, k).start()

    def finish(self, ins, outs, sems):
        x, y, c = _my_place()
        me, sibling = (x, y, c), (x, y, 1 - c)
        chips = [(1 - x, y), (x, 1 - y), (1 - x, 1 - y)]
        passed = []
        for j, chip in enumerate(chips):
            for a in range(self.n_gather):
                self._gather_copy(ins, outs, sems, a, 1 + j, (*chip, c), me).wait_recv()
                fwd = self._gather_copy(ins, outs, sems, a, 4 + j, (*chip, c), sibling)
                fwd.start()
                passed.append(fwd)
        for a in range(self.n_gather):
            self._gather_copy(ins, outs, sems, a, 0, sibling, me).wait_recv()
            for j, chip in enumerate(chips):
                self._gather_copy(ins, outs, sems, a, 4 + j, (*chip, 1 - c), me).wait_recv()
        for k in range(1, N_DEV):
            for a in range(self.n_gather, self.n):
                self._exchange_copy(ins, outs, sems, a, k, landing=True).wait_recv()
        for a in range(self.n_gather):
            for cp in self._first_stage(ins, outs, sems, a):
                cp.wait_send()
        for cp in passed:
            cp.wait_send()
        for k in range(1, N_DEV):
            for a in range(self.n_gather, self.n):
                self._exchange_copy(ins, outs, sems, a, k).wait_send()
        for a in range(self.n):
            self._local_copy(ins, outs, sems, a).wait()


def _comm_only(comm, name):
    def body(*refs):
        ins, outs, sems = refs[:comm.n], refs[comm.n:2 * comm.n], refs[2 * comm.n:]
        comm.start(ins, outs, sems)
        comm.finish(ins, outs, sems)

    return pl.pallas_call(body, name=name, in_specs=[ANY] * comm.n, out_specs=[ANY] * comm.n,
                          out_shape=comm.out_shape(), scratch_shapes=comm.sems())(*comm.arrs)


def _call(body, *, name, grid, in_specs, out_specs, out_shape, args, sem, scratch_shapes=(), comm=None):
    in_specs, out_specs, out_shape, scratch_shapes = list(in_specs), list(out_specs), list(out_shape), list(scratch_shapes)
    if comm is None:
        outs = pl.pallas_call(body, name=name, grid=grid, in_specs=in_specs, out_specs=out_specs, out_shape=out_shape,
                              scratch_shapes=scratch_shapes, compiler_params=_params(*sem))(*args)
        return list(outs), []
    n_in, n_out, n_scr, nc = len(in_specs), len(out_specs), len(scratch_shapes), comm.n

    def wrapped(*refs):
        ins, refs = refs[:n_in], refs[n_in:]
        c_in, refs = refs[:nc], refs[nc:]
        outs, refs = refs[:n_out], refs[n_out:]
        c_out, refs = refs[:nc], refs[nc:]
        scr, sems = refs[:n_scr], refs[n_scr:]
        first = functools.reduce(jnp.logical_and, [pl.program_id(ax) == 0 for ax in range(len(grid))])
        last = functools.reduce(jnp.logical_and, [pl.program_id(ax) == g - 1 for ax, g in enumerate(grid)])

        @pl.when(first)
        def _():
            comm.start(c_in, c_out, sems)

        body(*ins, *outs, *scr)

        @pl.when(last)
        def _():
            comm.finish(c_in, c_out, sems)

    outs = pl.pallas_call(
        wrapped, name=name, grid=grid, in_specs=in_specs + [ANY] * nc, out_specs=out_specs + [ANY] * nc,
        out_shape=out_shape + comm.out_shape(), scratch_shapes=scratch_shapes + comm.sems(),
        compiler_params=_params(*(["arbitrary"] * len(grid))))(*args, *comm.arrs)
    return list(outs[:n_out]), list(outs[n_out:])


def _matmul(a, b, mode, out_dtype, name, tm, tn, tk, *, bias=None, res=None, res_scale=1.0, b_off=0, comm=None):
    tm = min(tm, a.shape[1] if mode == "tn" else a.shape[0])
    tk = min(tk, a.shape[0] if mode == "tn" else a.shape[1])
    if mode == "nn":
        (m, k), n = a.shape, b.shape[1]
        a_spec = pl.BlockSpec((tm, tk), lambda i, j, kk: (i, kk))
        b_spec = pl.BlockSpec((tk, tn), lambda i, j, kk: (kk, j))
        dims = NN
    elif mode == "nt":
        (m, k), n = a.shape, b.shape[0]
        a_spec = pl.BlockSpec((tm, tk), lambda i, j, kk: (i, kk))
        b_spec = pl.BlockSpec((tn, tk), lambda i, j, kk: (j, kk + b_off))
        dims = NT
    else:
        (k, m), n = a.shape, b.shape[1]
        a_spec = pl.BlockSpec((tk, tm), lambda i, j, kk: (kk, i))
        b_spec = pl.BlockSpec((tk, tn), lambda i, j, kk: (kk, j))
        dims = TN
    assert m % tm == 0 and n % tn == 0 and k % tk == 0, (name, m, n, k)
    nk = k // tk
    in_specs = [a_spec, b_spec]
    args = [a, b]
    if bias is not None:
        in_specs.append(pl.BlockSpec((1, tn), lambda i, j, kk: (0, j)))
        args.append(bias)
    if res is not None:
        in_specs.append(pl.BlockSpec((tm, tn), lambda i, j, kk: (i, j)))
        args.append(res)

    def finish(out, refs, o_ref):
        pos = 2
        if bias is not None:
            out = out + refs[pos][...]
            pos += 1
        if res is not None:
            out = out + res_scale * refs[pos][...].astype(F32)
        o_ref[...] = out.astype(out_dtype)

    def body_one_step(*refs):
        finish(_dot(refs[0][...], refs[1][...], dims), refs, refs[-1])

    def body(*refs):
        a_ref, b_ref = refs[0], refs[1]
        o_ref, acc = refs[-2], refs[-1]
        kk = pl.program_id(2)

        @pl.when(kk == 0)
        def _():
            acc[...] = jnp.zeros_like(acc)

        acc[...] += _dot(a_ref[...], b_ref[...], dims)

        @pl.when(kk == nk - 1)
        def _():
            finish(acc[...], refs, o_ref)

    (out,), moved = _call(
        body_one_step if nk == 1 else body, name=name, grid=(m // tm, n // tn, nk),
        in_specs=in_specs, out_specs=[pl.BlockSpec((tm, tn), lambda i, j, kk: (i, j))],
        out_shape=[jax.ShapeDtypeStruct((m, n), out_dtype)],
        scratch_shapes=[] if nk == 1 else [pltpu.VMEM((tm, tn), F32)],
        sem=("parallel", "parallel", "arbitrary"), args=args, comm=comm)
    return out if comm is None else (out, moved)


def _matmul_res_ln(a, b, xres, g, beta, name, tm, tk, comm=None):
    t, k = a.shape
    d = b.shape[1]
    nk = k // tk
    assert t % tm == 0 and k % tk == 0

    def body(a_ref, b_ref, x_ref, g_ref, beta_ref, z_ref, xo_ref, acc):
        kk = pl.program_id(1)

        @pl.when(kk == 0)
        def _():
            acc[...] = jnp.zeros_like(acc)

        acc[...] += _dot(a_ref[...], b_ref[...], NN)

        @pl.when(kk == nk - 1)
        def _():
            z = ALPHA * x_ref[...] + acc[...]
            nh, _ = _ln_stats(z)
            z_ref[...] = z
            xo_ref[...] = nh * g_ref[...] + beta_ref[...]

    row = pl.BlockSpec((tm, d), lambda i, kk: (i, 0))
    vec = pl.BlockSpec((1, d), lambda i, kk: (0, 0))
    outs, moved = _call(
        body, name=name, grid=(t // tm, nk),
        in_specs=[pl.BlockSpec((tm, tk), lambda i, kk: (i, kk)), pl.BlockSpec((tk, d), lambda i, kk: (kk, 0)), row, vec, vec],
        out_specs=[row, row],
        out_shape=[jax.ShapeDtypeStruct((t, d), F32), jax.ShapeDtypeStruct((t, d), F32)],
        scratch_shapes=[pltpu.VMEM((tm, d), F32)],
        sem=("parallel", "arbitrary"), args=(a, b, xres, g, beta), comm=comm)
    return outs if comm is None else (outs, moved)


def _ln_bwd(z, g, dres, dbr, name, tm=512):
    t, d = z.shape

    def body(z_ref, g_ref, dres_ref, dbr_ref, dz_ref, dg_ref, db_ref):
        @pl.when(pl.program_id(0) == 0)
        def _():
            dg_ref[...] = jnp.zeros_like(dg_ref)
            db_ref[...] = jnp.zeros_like(db_ref)

        nh, r = _ln_stats(z_ref[...])
        dy = ALPHA * dres_ref[...] + dbr_ref[...].astype(F32)
        dg_ref[...] += _colsum(dy * nh)
        db_ref[...] += _colsum(dy)
        dz_ref[...] = _ln_bwd_rows(dy * g_ref[...], nh, r)

    row = pl.BlockSpec((tm, d), lambda i: (i, 0))
    vec = pl.BlockSpec((1, d), lambda i: (0, 0))
    return pl.pallas_call(
        body, name=name, grid=(t // tm,), in_specs=[row, vec, row, row], out_specs=[row, vec, vec],
        out_shape=[jax.ShapeDtypeStruct((t, d), F32), jax.ShapeDtypeStruct((1, d), F32), jax.ShapeDtypeStruct((1, d), F32)],
        compiler_params=_params("arbitrary"),
    )(z, g, dres, dbr)


def _ln_bwd_loss(z, g, beta, target, name, tm=512):
    t, d = z.shape

    def body(z_ref, g_ref, beta_ref, t_ref, dz_ref, dg_ref, db_ref, loss_ref):
        @pl.when(pl.program_id(0) == 0)
        def _():
            dg_ref[...] = jnp.zeros_like(dg_ref)
            db_ref[...] = jnp.zeros_like(db_ref)
            loss_ref[...] = jnp.zeros_like(loss_ref)

        nh, r = _ln_stats(z_ref[...])
        err = nh * g_ref[...] + beta_ref[...] - t_ref[...]
        loss_ref[...] += _colsum(err * err)
        dy = err * (1.0 / d)
        dg_ref[...] += _colsum(dy * nh)
        db_ref[...] += _colsum(dy)
        dz_ref[...] = _ln_bwd_rows(dy * g_ref[...], nh, r)

    row = pl.BlockSpec((tm, d), lambda i: (i, 0))
    vec = pl.BlockSpec((1, d), lambda i: (0, 0))
    vshape = jax.ShapeDtypeStruct((1, d), F32)
    return pl.pallas_call(
        body, name=name, grid=(t // tm,), in_specs=[row, vec, vec, row], out_specs=[row, vec, vec, vec],
        out_shape=[jax.ShapeDtypeStruct((t, d), F32), vshape, vshape, vshape],
        compiler_params=_params("arbitrary"),
    )(z, g, beta, target)


FFN_HALO = 16


def _ffn_mid_fwd(h, cw, cb, name, tm=1024, tc=256, comm=None):
    t, f2 = h.shape
    tm = min(tm, t)
    f = f2 // 2
    nj, nt, hb = f // tc, t // tm, tm // FFN_HALO

    def body(hg, hgp, hv, hvp, cwg, cwv, cbg, cbv, u_ref, sg, sv):
        i = pl.program_id(1)

        def conv(main, prev, s, w, b):
            s[0:FFN_HALO, :] = jnp.where(i > 0, prev[...].astype(F32), 0.0)
            s[FFN_HALO:, :] = main[...].astype(F32)
            o = FFN_HALO - FFN_KERNEL + 1
            return (w[0:1, :] * s[pl.ds(o, tm), :] + w[1:2, :] * s[pl.ds(o + 1, tm), :]
                    + w[2:3, :] * s[pl.ds(o + 2, tm), :] + b[...])

        cg = conv(hg, hgp, sg, cwg, cbg)
        cv = conv(hv, hvp, sv, cwv, cbv)
        u_ref[...] = (_gelu(cg) * cv).astype(BF16)

    def main_spec(off):
        return pl.BlockSpec((tm, tc), lambda j, i: (i, j + off))

    def prev_spec(off):
        return pl.BlockSpec((FFN_HALO, tc), lambda j, i: (jnp.maximum(i * hb - 1, 0), j + off))

    def par_spec(rows, off):
        return pl.BlockSpec((rows, tc), lambda j, i: (0, j + off))

    (u,), moved = _call(
        body, name=name, grid=(nj, nt),
        in_specs=[main_spec(0), prev_spec(0), main_spec(nj), prev_spec(nj),
                  par_spec(FFN_KERNEL, 0), par_spec(FFN_KERNEL, nj), par_spec(1, 0), par_spec(1, nj)],
        out_specs=[pl.BlockSpec((tm, tc), lambda j, i: (i, j))],
        out_shape=[jax.ShapeDtypeStruct((t, f), BF16)],
        scratch_shapes=[pltpu.VMEM((tm + FFN_HALO, tc), F32), pltpu.VMEM((tm + FFN_HALO, tc), F32)],
        sem=("parallel", "arbitrary"), args=(h, h, h, h, cw, cw, cb, cb), comm=comm)
    return u if comm is None else (u, moved)


def _ffn_mid_bwd(h, du, cw, cb, name, tm=1024, tc=256, comm=None):
    t, f2 = h.shape
    tm = min(tm, t)
    f = f2 // 2
    nj, nt, hb = f // tc, t // tm, tm // FFN_HALO
    r = tm + FFN_HALO

    def body(hg, hgp, hgn, hv, hvp, hvn, du_ref, dun_ref, cwg, cwv, cbg, cbv,
             dhg_ref, dhv_ref, dcwg_ref, dcwv_ref, dcbg_ref, dcbv_ref, sg, sv, sdg, sdv):
        i = pl.program_id(1)

        @pl.when(i == 0)
        def _():
            for ref in (dcwg_ref, dcwv_ref, dcbg_ref, dcbv_ref):
                ref[...] = jnp.zeros_like(ref)

        o = FFN_HALO - FFN_KERNEL + 1

        def conv(main, prev, nxt, s, w, b):
            s[0:FFN_HALO, :] = jnp.where(i > 0, prev[...].astype(F32), 0.0)
            s[FFN_HALO:FFN_HALO + tm, :] = main[...].astype(F32)
            s[FFN_HALO + tm:, :] = nxt[...].astype(F32)
            return (w[0:1, :] * s[pl.ds(o, r), :] + w[1:2, :] * s[pl.ds(o + 1, r), :]
                    + w[2:3, :] * s[pl.ds(o + 2, r), :] + b[...])

        cg = conv(hg, hgp, hgn, sg, cwg, cbg)
        cv = conv(hv, hvp, hvn, sv, cwv, cbv)
        du_e = jnp.concatenate([du_ref[...].astype(F32), jnp.where(i < nt - 1, dun_ref[...].astype(F32), 0.0)], axis=0)
        gl, dgl = _gelu_and_grad(cg)
        sdg[...] = du_e * cv * dgl
        sdv[...] = du_e * gl

        def back(sd, s, w, dh_ref, dcw_ref, dcb_ref):
            own = sd[0:tm, :]
            dcb_ref[...] += _colsum(own)
            for k in range(FFN_KERNEL):
                dcw_ref[k:k + 1, :] += _colsum(own * s[pl.ds(o + k, tm), :])
            dh = w[2:3, :] * own + w[1:2, :] * sd[pl.ds(1, tm), :] + w[0:1, :] * sd[pl.ds(2, tm), :]
            dh_ref[...] = dh.astype(BF16)

        back(sdg, sg, cwg, dhg_ref, dcwg_ref, dcbg_ref)
        back(sdv, sv, cwv, dhv_ref, dcwv_ref, dcbv_ref)

    last_blk = t // FFN_HALO - 1

    def main_spec(off):
        return pl.BlockSpec((tm, tc), lambda j, i: (i, j + off))

    def prev_spec(off):
        return pl.BlockSpec((FFN_HALO, tc), lambda j, i: (jnp.maximum(i * hb - 1, 0), j + off))

    def next_spec(off):
        return pl.BlockSpec((FFN_HALO, tc), lambda j, i: (jnp.minimum((i + 1) * hb, last_blk), j + off))

    def par_spec(rows, off):
        return pl.BlockSpec((rows, tc), lambda j, i: (0, j + off))

    out_tile = pl.BlockSpec((tm, tc), lambda j, i: (i, j))
    outs, moved = _call(
        body, name=name, grid=(nj, nt),
        in_specs=[main_spec(0), prev_spec(0), next_spec(0), main_spec(nj), prev_spec(nj), next_spec(nj),
                  main_spec(0), next_spec(0),
                  par_spec(FFN_KERNEL, 0), par_spec(FFN_KERNEL, nj), par_spec(1, 0), par_spec(1, nj)],
        out_specs=[out_tile, out_tile, par_spec(FFN_KERNEL, 0), par_spec(FFN_KERNEL, 0), par_spec(1, 0), par_spec(1, 0)],
        out_shape=[jax.ShapeDtypeStruct((t, f), BF16), jax.ShapeDtypeStruct((t, f), BF16),
                   jax.ShapeDtypeStruct((FFN_KERNEL, f), F32), jax.ShapeDtypeStruct((FFN_KERNEL, f), F32),
                   jax.ShapeDtypeStruct((1, f), F32), jax.ShapeDtypeStruct((1, f), F32)],
        scratch_shapes=[pltpu.VMEM((tm + 2 * FFN_HALO, tc), F32), pltpu.VMEM((tm + 2 * FFN_HALO, tc), F32),
                        pltpu.VMEM((r, tc), F32), pltpu.VMEM((r, tc), F32)],
        sem=("parallel", "arbitrary"), args=(h, h, h, h, h, h, du, du, cw, cw, cb, cb), comm=comm)
    return outs if comm is None else (outs, moved)


MIX_HALO = 32


def _glu(hh):
    return hh[:, 0:A_WIDTH] * _sigmoid(hh[:, A_WIDTH:2 * A_WIDTH])


def _tril_mask():
    return lax.broadcasted_iota(jnp.int32, (B_CHUNK, B_CHUNK), 0) >= lax.broadcasted_iota(jnp.int32, (B_CHUNK, B_CHUNK), 1)


def _spatial_mix(q, ms_ref, sbt_ref, tm):
    mask = _tril_mask()
    ws = [jnp.where(mask, ms_ref[g], 0.0).astype(BF16) for g in range(B_GROUPS)]
    qb = q.astype(BF16)
    rows = []
    for c in range(tm // B_CHUNK):
        cols = [_dot(ws[g], qb[c * B_CHUNK:(c + 1) * B_CHUNK, g * 128:(g + 1) * 128], NN) + sbt_ref[:, g:g + 1]
                for g in range(B_GROUPS)]
        rows.append(jnp.concatenate(cols, axis=1))
    return jnp.concatenate(rows, axis=0)


def _mixer_mid_fwd(h, cw, cb, ag, ab, bg, bb, ms, sbt, name, tm=256, comm=None):
    t = h.shape[0]
    nt, hb = t // tm, tm // MIX_HALO
    o = MIX_HALO - A_KERNEL + 1

    def body(h_ref, hp_ref, cw_ref, cb_ref, ag_ref, ab_ref, bg_ref, bb_ref, ms_ref, sbt_ref, cat_ref, sp):
        i = pl.program_id(0)
        sp[0:MIX_HALO, :] = jnp.where(i > 0, _glu(hp_ref[:, 0:2 * A_WIDTH].astype(F32)), 0.0)
        sp[MIX_HALO:, :] = _glu(h_ref[:, 0:2 * A_WIDTH].astype(F32))
        y = jnp.zeros((tm, A_WIDTH), F32) + cb_ref[...]
        for k in range(A_KERNEL):
            y = y + cw_ref[k:k + 1, :] * sp[pl.ds(o + k, tm), :]
        nh, _ = _ln_stats(y)
        ln = nh * ag_ref[...] + ab_ref[...]
        cat_ref[:, 0:A_WIDTH] = (ln * _sigmoid(ln)).astype(BF16)
        u = _gelu(h_ref[:, 1024:1536].astype(F32))
        nb, _ = _ln_stats(_gelu(h_ref[:, 1536:2048].astype(F32)))
        mixed = _spatial_mix(nb * bg_ref[...] + bb_ref[...], ms_ref, sbt_ref, tm)
        cat_ref[:, A_WIDTH:] = (u * mixed).astype(BF16)

    vec = pl.BlockSpec((1, A_WIDTH), lambda i: (0, 0))
    (cat,), moved = _call(
        body, name=name, grid=(nt,),
        in_specs=[pl.BlockSpec((tm, 2048), lambda i: (i, 0)),
                  pl.BlockSpec((MIX_HALO, 2048), lambda i: (jnp.maximum(i * hb - 1, 0), 0)),
                  pl.BlockSpec((A_KERNEL, A_WIDTH), lambda i: (0, 0)), vec, vec, vec, vec, vec,
                  pl.BlockSpec((B_GROUPS, B_CHUNK, B_CHUNK), lambda i: (0, 0, 0)),
                  pl.BlockSpec((B_CHUNK, B_GROUPS), lambda i: (0, 0))],
        out_specs=[pl.BlockSpec((tm, D_MODEL), lambda i: (i, 0))],
        out_shape=[jax.ShapeDtypeStruct((t, D_MODEL), BF16)],
        scratch_shapes=[pltpu.VMEM((tm + MIX_HALO, A_WIDTH), F32)],
        sem=("parallel",), args=(h, h, cw, cb, ag, ab, bg, bb, ms, sbt), comm=comm)
    return cat if comm is None else (cat, moved)


def _mixer_mid_bwd(h, dcat, cw, cb, ag, ab, bg, bb, ms, mst, sbt, name, tm=256, comm=None):
    t = h.shape[0]
    nt, hb = t // tm, tm // MIX_HALO
    o = MIX_HALO - A_KERNEL + 1
    r = tm + MIX_HALO
    nchunk = tm // B_CHUNK

    def body(h_ref, hp_ref, hn_ref, dc_ref, dcn_ref, cw_ref, cb_ref, ag_ref, ab_ref, bg_ref, bb_ref, ms_ref, mst_ref, sbt_ref,
             dh_ref, dcw_ref, dcb_ref, dag_ref, dab_ref, dbg_ref, dbb_ref, dms_ref, dsb_ref, sp, sdy, sbacc):
        i = pl.program_id(0)

        @pl.when(i == 0)
        def _():
            for ref in (dcw_ref, dcb_ref, dag_ref, dab_ref, dbg_ref, dbb_ref, dms_ref, dsb_ref, sbacc):
                ref[...] = jnp.zeros_like(ref)

        sp[0:MIX_HALO, :] = jnp.where(i > 0, _glu(hp_ref[:, 0:2 * A_WIDTH].astype(F32)), 0.0)
        sp[MIX_HALO:MIX_HALO + tm, :] = _glu(h_ref[:, 0:2 * A_WIDTH].astype(F32))
        sp[MIX_HALO + tm:, :] = _glu(hn_ref[:, 0:2 * A_WIDTH].astype(F32))
        y = jnp.zeros((r, A_WIDTH), F32) + cb_ref[...]
        for k in range(A_KERNEL):
            y = y + cw_ref[k:k + 1, :] * sp[pl.ds(o + k, r), :]
        nh, rs = _ln_stats(y)
        ln = nh * ag_ref[...] + ab_ref[...]
        sg = _sigmoid(ln)
        dao = jnp.concatenate([dc_ref[:, 0:A_WIDTH].astype(F32),
                               jnp.where(i < nt - 1, dcn_ref[:, 0:A_WIDTH].astype(F32), 0.0)], axis=0)
        dln = dao * (sg * (1.0 + ln * (1.0 - sg)))
        dag_ref[...] += _colsum(dln[0:tm] * nh[0:tm])
        dab_ref[...] += _colsum(dln[0:tm])
        sdy[...] = _ln_bwd_rows(dln * ag_ref[...], nh, rs)
        dy_own = sdy[0:tm, :]
        dcb_ref[...] += _colsum(dy_own)
        dp = jnp.zeros((tm, A_WIDTH), F32)
        for k in range(A_KERNEL):
            dcw_ref[k:k + 1, :] += _colsum(dy_own * sp[pl.ds(o + k, tm), :])
            dp = dp + cw_ref[k:k + 1, :] * sdy[pl.ds(A_KERNEL - 1 - k, tm), :]
        av = h_ref[:, 0:A_WIDTH].astype(F32)
        s = _sigmoid(h_ref[:, A_WIDTH:2 * A_WIDTH].astype(F32))
        dh_ref[:, 0:A_WIDTH] = (dp * s).astype(BF16)
        dh_ref[:, A_WIDTH:2 * A_WIDTH] = (dp * av * s * (1.0 - s)).astype(BF16)

        u, dgu = _gelu_and_grad(h_ref[:, 1024:1536].astype(F32))
        w, dgw = _gelu_and_grad(h_ref[:, 1536:2048].astype(F32))
        nb, rb = _ln_stats(w)
        q = nb * bg_ref[...] + bb_ref[...]
        mixed = _spatial_mix(q, ms_ref, sbt_ref, tm)
        dbo = dc_ref[:, A_WIDTH:].astype(F32)
        dh_ref[:, 1024:1536] = (dbo * mixed * dgu).astype(BF16)
        dmx = dbo * u
        mask = _tril_mask()
        wst = [jnp.where(mask.T, mst_ref[g], 0.0).astype(BF16) for g in range(B_GROUPS)]
        qb = q.astype(BF16)
        dmb = dmx.astype(BF16)
        rows = []
        for c in range(nchunk):
            cols = []
            for g in range(B_GROUPS):
                rs_, cs_ = slice(c * B_CHUNK, (c + 1) * B_CHUNK), slice(g * 128, (g + 1) * 128)
                sbacc[g] += dmx[rs_, cs_]
                dms_ref[g] += _dot(dmb[rs_, cs_], qb[rs_, cs_], NT)
                cols.append(_dot(wst[g], dmb[rs_, cs_], NN))
            rows.append(jnp.concatenate(cols, axis=1))
        dq = jnp.concatenate(rows, axis=0)
        dbg_ref[...] += _colsum(dq * nb)
        dbb_ref[...] += _colsum(dq)
        dh_ref[:, 1536:2048] = (_ln_bwd_rows(dq * bg_ref[...], nb, rb) * dgw).astype(BF16)

        @pl.when(i == nt - 1)
        def _():
            for g in range(B_GROUPS):
                dms_ref[g] = jnp.where(mask, dms_ref[g], 0.0)
                dsb_ref[g] = jnp.sum(sbacc[g], axis=1, keepdims=True)

    last_blk = t // MIX_HALO - 1
    vec = pl.BlockSpec((1, A_WIDTH), lambda i: (0, 0))
    mat = pl.BlockSpec((B_GROUPS, B_CHUNK, B_CHUNK), lambda i: (0, 0, 0))
    taps = pl.BlockSpec((A_KERNEL, A_WIDTH), lambda i: (0, 0))

    def halo(width, which):
        if which == "prev":
            return pl.BlockSpec((MIX_HALO, width), lambda i: (jnp.maximum(i * hb - 1, 0), 0))
        return pl.BlockSpec((MIX_HALO, width), lambda i: (jnp.minimum((i + 1) * hb, last_blk), 0))

    vshape = jax.ShapeDtypeStruct((1, A_WIDTH), F32)
    outs, moved = _call(
        body, name=name, grid=(nt,),
        in_specs=[pl.BlockSpec((tm, 2048), lambda i: (i, 0)), halo(2048, "prev"), halo(2048, "next"),
                  pl.BlockSpec((tm, D_MODEL), lambda i: (i, 0)), halo(D_MODEL, "next"),
                  taps, vec, vec, vec, vec, vec, mat, mat, pl.BlockSpec((B_CHUNK, B_GROUPS), lambda i: (0, 0))],
        out_specs=[pl.BlockSpec((tm, 2048), lambda i: (i, 0)), taps, vec, vec, vec, vec, vec, mat,
                   pl.BlockSpec((B_GROUPS, B_CHUNK, 1), lambda i: (0, 0, 0))],
        out_shape=[jax.ShapeDtypeStruct((t, 2048), BF16), jax.ShapeDtypeStruct((A_KERNEL, A_WIDTH), F32),
                   vshape, vshape, vshape, vshape, vshape,
                   jax.ShapeDtypeStruct((B_GROUPS, B_CHUNK, B_CHUNK), F32), jax.ShapeDtypeStruct((B_GROUPS, B_CHUNK, 1), F32)],
        scratch_shapes=[pltpu.VMEM((tm + 2 * MIX_HALO, A_WIDTH), F32), pltpu.VMEM((r, A_WIDTH), F32),
                        pltpu.VMEM((B_GROUPS, B_CHUNK, B_CHUNK), F32)],
        sem=("arbitrary",), args=(h, h, h, dcat, dcat, cw, cb, ag, ab, bg, bb, ms, mst, sbt), comm=comm)
    return outs if comm is None else (outs, moved)


Q_WIDTH = N_Q_HEADS * HEAD_DIM
KV_WIDTH = 2 * N_KV_HEADS * HEAD_DIM
PAIRS_PER_KV = N_Q_HEADS // N_KV_HEADS // 2
ATT_SCALE = 1.0 / math.sqrt(HEAD_DIM)


def _att_mask(n):
    qi = lax.broadcasted_iota(jnp.int32, (ATT_BLOCK, 2 * ATT_BLOCK), 0)
    sj = lax.broadcasted_iota(jnp.int32, (ATT_BLOCK, 2 * ATT_BLOCK), 1)
    diff = qi + ATT_BLOCK - sj
    return (diff >= 0) & (diff < ATT_BLOCK) & ((n > 0) | (sj >= ATT_BLOCK))


def _dup_heads(pair_cols, kv_head):
    lane = lax.broadcasted_iota(jnp.int32, pair_cols.shape, 1)
    rolled = pltpu.roll(pair_cols, HEAD_DIM, 1)
    first = lane < HEAD_DIM
    return jnp.where(first, pair_cols, rolled) if kv_head == 0 else jnp.where(first, rolled, pair_cols)


HEADS_PER_KV = N_Q_HEADS // N_KV_HEADS


def _stack_heads(ref, kh):
    lane = lax.broadcasted_iota(jnp.int32, (ATT_BLOCK, 128), 1)
    rows = []
    for pr in range(PAIRS_PER_KV):
        c0 = (kh * PAIRS_PER_KV + pr) * 128
        pair = ref[:, c0:c0 + 128]
        rows += [jnp.where(lane < HEAD_DIM, pair, jnp.zeros_like(pair)), jnp.where(lane < HEAD_DIM, jnp.zeros_like(pair), pair)]
    return jnp.concatenate(rows, axis=0)


def _unstack_heads(stacked, kh, write):
    lane = lax.broadcasted_iota(jnp.int32, (ATT_BLOCK, 128), 1)
    for pr in range(PAIRS_PER_KV):
        first = stacked[(2 * pr) * ATT_BLOCK:(2 * pr + 1) * ATT_BLOCK]
        second = stacked[(2 * pr + 1) * ATT_BLOCK:(2 * pr + 2) * ATT_BLOCK]
        write((kh * PAIRS_PER_KV + pr) * 128, jnp.where(lane < HEAD_DIM, first, second))


def _sink_column(sink_ref, kh):
    return jnp.concatenate([jnp.full((ATT_BLOCK, 1), sink_ref[0, kh * HEADS_PER_KV + h], F32) for h in range(HEADS_PER_KV)], axis=0)


def _att_probs(q_rows, k2, mask, sink, heads):
    s = _dot(q_rows, k2, NT) * ATT_SCALE
    s = jnp.where(jnp.concatenate([mask] * heads, axis=0), s, -jnp.inf)
    m = jnp.maximum(jnp.max(s, axis=-1, keepdims=True), sink)
    e = jnp.exp(s - m)
    es = jnp.exp(sink - m)
    inv = 1.0 / (jnp.sum(e, axis=-1, keepdims=True) + es)
    return e * inv, es * inv


def _attn_fwd(qkv, sinks, name):
    t = qkv.shape[0]
    nb = t // ATT_BLOCK
    kvb = Q_WIDTH // KV_WIDTH

    def body(sink_ref, q_ref, kv_ref, kvp_ref, o_ref):
        n = pl.program_id(0)
        mask = _att_mask(n)
        kv = jnp.concatenate([kvp_ref[...], kv_ref[...]], axis=0).astype(F32)

        lane = lax.broadcasted_iota(jnp.int32, (ATT_BLOCK, 128), 1)
        for kh in range(N_KV_HEADS):
            k2 = _dup_heads(kv[:, 0:128], kh).astype(BF16)
            v2 = _dup_heads(kv[:, 128:256], kh).astype(BF16)
            for pr in range(PAIRS_PER_KV):
                c0 = (kh * PAIRS_PER_KV + pr) * 128
                q2 = q_ref[:, c0:c0 + 128]
                outs = []
                for half in range(2):
                    head = (kh * PAIRS_PER_KV + pr) * 2 + half
                    qm = jnp.where((lane < HEAD_DIM) == (half == 0), q2, jnp.zeros_like(q2))
                    p, _ = _att_probs(qm, k2, mask, sink_ref[0, head], 1)
                    outs.append(_dot(p, v2, NN))
                o_ref[:, c0:c0 + 128] = jnp.where(lane < HEAD_DIM, outs[0], outs[1]).astype(BF16)

    return pl.pallas_call(
        body, name=name, grid=(nb,),
        in_specs=[pl.BlockSpec(memory_space=pltpu.SMEM),
                  pl.BlockSpec((ATT_BLOCK, Q_WIDTH), lambda n: (n, 0)),
                  pl.BlockSpec((ATT_BLOCK, KV_WIDTH), lambda n: (n, kvb)),
                  pl.BlockSpec((ATT_BLOCK, KV_WIDTH), lambda n: (jnp.maximum(n - 1, 0), kvb))],
        out_specs=pl.BlockSpec((ATT_BLOCK, Q_WIDTH), lambda n: (n, 0)),
        out_shape=jax.ShapeDtypeStruct((t, Q_WIDTH), BF16),
        compiler_params=_params("parallel"),
    )(sinks, qkv, qkv, qkv)


def _attn_bwd(qkv, d_o, sinks, name, comm=None):
    t = qkv.shape[0]
    nb = t // ATT_BLOCK
    kvb = Q_WIDTH // KV_WIDTH

    def body(sink_ref, q_ref, kv_ref, kvp_ref, do_ref, dq_ref, dkv_ref, dbq_ref, dbkv_ref, dsink_ref, carry):
        n = pl.program_id(0)

        @pl.when(n == 0)
        def _():
            for ref in (dbq_ref, dbkv_ref, dsink_ref, carry):
                ref[...] = jnp.zeros_like(ref)
            dkv_ref[...] = jnp.zeros_like(dkv_ref)

        @pl.when(n < nb)
        def _():
            mask = _att_mask(n)
            kv = jnp.concatenate([kvp_ref[...], kv_ref[...]], axis=0).astype(F32)
            lane2 = lax.broadcasted_iota(jnp.int32, (2 * ATT_BLOCK, 128), 1)
            sink_lane = lax.broadcasted_iota(jnp.int32, (1, 128), 1)
            dsink = jnp.zeros((1, 128), F32)
            dk_parts, dv_parts = [], []

            def write(c0, pair):
                dbq_ref[:, c0:c0 + 128] += _colsum(pair)
                dq_ref[:, c0:c0 + 128] = pair.astype(BF16)

            for kh in range(N_KV_HEADS):
                k2 = _dup_heads(kv[:, 0:128], kh).astype(BF16)
                v2 = _dup_heads(kv[:, 128:256], kh).astype(BF16)
                q_all = _stack_heads(q_ref, kh)
                do_all = _stack_heads(do_ref, kh)
                p, ps = _att_probs(q_all, k2, mask, _sink_column(sink_ref, kh), HEADS_PER_KV)
                dp = _dot(do_all, v2, NT)
                delta = jnp.sum(p * dp, axis=-1, keepdims=True)
                ds = p * (dp - delta) * ATT_SCALE
                psd = ps * delta
                for h in range(HEADS_PER_KV):
                    dsink = dsink + jnp.where(sink_lane == kh * HEADS_PER_KV + h,
                                              -jnp.sum(psd[h * ATT_BLOCK:(h + 1) * ATT_BLOCK]), 0.0)
                _unstack_heads(_dot(ds, k2, NN), kh, write)
                dk_acc = _dot(ds, q_all, TN)
                dv_acc = _dot(p, do_all, TN)
                dk_parts.append(dk_acc + pltpu.roll(dk_acc, HEAD_DIM, 1))
                dv_parts.append(dv_acc + pltpu.roll(dv_acc, HEAD_DIM, 1))
            dk = jnp.where(lane2 < HEAD_DIM, dk_parts[0], dk_parts[1])
            dv = jnp.where(lane2 < HEAD_DIM, dv_parts[0], dv_parts[1])
            dkv_new = jnp.concatenate([dk, dv], axis=1)
            done = carry[...] + dkv_new[0:ATT_BLOCK]

            @pl.when(n > 0)
            def _():
                dkv_ref[...] = done.astype(BF16)
                dbkv_ref[...] += _colsum(done)

            carry[...] = dkv_new[ATT_BLOCK:]
            dsink_ref[...] += dsink

        @pl.when(n == nb)
        def _():
            dkv_ref[...] = carry[...].astype(BF16)
            dbkv_ref[...] += _colsum(carry[...])

    def clamp(n):
        return jnp.minimum(n, nb - 1)

    outs, moved = _call(
        body, name=name, grid=(nb + 1,),
        in_specs=[pl.BlockSpec(memory_space=pltpu.SMEM),
                  pl.BlockSpec((ATT_BLOCK, Q_WIDTH), lambda n: (clamp(n), 0)),
                  pl.BlockSpec((ATT_BLOCK, KV_WIDTH), lambda n: (clamp(n), kvb)),
                  pl.BlockSpec((ATT_BLOCK, KV_WIDTH), lambda n: (jnp.maximum(clamp(n) - 1, 0), kvb)),
                  pl.BlockSpec((ATT_BLOCK, Q_WIDTH), lambda n: (clamp(n), 0))],
        out_specs=[pl.BlockSpec((ATT_BLOCK, Q_WIDTH), lambda n: (clamp(n), 0)),
                   pl.BlockSpec((ATT_BLOCK, KV_WIDTH), lambda n: (jnp.maximum(n - 1, 0), 0)),
                   pl.BlockSpec((1, Q_WIDTH), lambda n: (0, 0)),
                   pl.BlockSpec((1, KV_WIDTH), lambda n: (0, 0)),
                   pl.BlockSpec((1, 128), lambda n: (0, 0))],
        out_shape=[jax.ShapeDtypeStruct((t, Q_WIDTH), BF16), jax.ShapeDtypeStruct((t, KV_WIDTH), BF16),
                   jax.ShapeDtypeStruct((1, Q_WIDTH), F32), jax.ShapeDtypeStruct((1, KV_WIDTH), F32),
                   jax.ShapeDtypeStruct((1, 128), F32)],
        scratch_shapes=[pltpu.VMEM((ATT_BLOCK, KV_WIDTH), F32)],
        sem=("arbitrary",), args=(sinks, qkv, qkv, qkv, d_o), comm=comm)
    return outs if comm is None else (outs, moved)


def _adamw_math(g, w, m, v):
    m = ADAM_B1 * m + (1.0 - ADAM_B1) * g
    v = ADAM_B2 * v + (1.0 - ADAM_B2) * (g * g)
    m_hat = m / (1.0 - ADAM_B1 ** ADAM_STEP)
    v_hat = v / (1.0 - ADAM_B2 ** ADAM_STEP)
    delta = -ADAM_LR * (m_hat / (jnp.sqrt(v_hat) + ADAM_EPS) + ADAM_WD * w)
    return delta, m, v


def _sum_partials(p_ref):
    g = p_ref[0].astype(F32)
    for s in range(1, N_DEV):
        g = g + p_ref[s].astype(F32)
    return g


def _adamw_big(parts, w, m, v, name, tr):
    r, c = w.shape
    tiles = [p.shape[1] // tr for p in parts]
    starts = [sum(tiles[:l]) for l in range(len(parts))]
    assert all(p.shape[1] % tr == 0 for p in parts) and sum(tiles) * tr == r

    def body(*refs):
        p_refs, (w_ref, m_ref, v_ref, g_out, d_out, m_out, v_out) = refs[:len(parts)], refs[len(parts):]
        i = pl.program_id(0)
        for l, p_ref in enumerate(p_refs):
            @pl.when((i >= starts[l]) & (i < starts[l] + tiles[l]))
            def _():
                g = _sum_partials(p_ref)
                g_out[...] = g
                d_out[...], m_out[...], v_out[...] = _adamw_math(g, w_ref[...], m_ref[...], v_ref[...])

    def part_spec(l):
        return pl.BlockSpec((N_DEV, tr, c), lambda i: (0, jnp.clip(i - starts[l], 0, tiles[l] - 1), 0))

    tile = pl.BlockSpec((tr, c), lambda i: (i, 0))
    shape = jax.ShapeDtypeStruct((r, c), F32)
    return pl.pallas_call(
        body, name=name, grid=(r // tr,),
        in_specs=[part_spec(l) for l in range(len(parts))] + [tile, tile, tile],
        out_specs=[tile] * 4, out_shape=[shape] * 4,
        compiler_params=_params("parallel"),
    )(*parts, w, m, v)


def _adamw_small(parts, ws, ms, vs, name):
    n = len(ws)

    def body(*refs):
        ins, outs = refs[:4 * n], refs[4 * n:]
        for a in range(n):
            g = _sum_partials(ins[a])
            outs[4 * a][...] = g
            outs[4 * a + 1][...], outs[4 * a + 2][...], outs[4 * a + 3][...] = _adamw_math(
                g, ins[n + a][...], ins[2 * n + a][...], ins[3 * n + a][...])

    out_shape = []
    for w in ws:
        out_shape += [jax.ShapeDtypeStruct(w.shape, F32)] * 4
    return pl.pallas_call(body, name=name, out_shape=out_shape, compiler_params=_params())(*parts, *ws, *ms, *vs)


PACK_LANES = 128
PACK_ROWS = 8


def _pack(arrs):
    flat = jnp.concatenate([a.reshape(-1).astype(F32) for a in arrs])
    unit = PACK_LANES * PACK_ROWS
    total = -(-flat.shape[0] // unit) * unit
    return jnp.pad(flat, (0, total - flat.shape[0])).reshape(-1, PACK_LANES)


def _unpack(buf, shapes):
    flat = buf.reshape(N_DEV, -1)
    out, pos = [], 0
    for s in shapes:
        size = math.prod(s)
        out.append(flat[:, pos:pos + size].reshape((N_DEV,) + tuple(s)))
        pos += size
    return out


def _interleave(g):
    return jnp.transpose(g, (1, 0, 2)).reshape(g.shape[1], -1)


def _deinterleave(w):
    r = w.shape[0]
    return jnp.transpose(w.reshape(r, N_DEV, -1), (1, 0, 2))


def _ffn_backward(dz, x_in, h, u, w_up, cw, cb, w_down, tag, comm=None):
    du = _matmul(dz, w_down, "nt", BF16, f"ffn{tag}_du", 1024, 1408, 1024)
    d_w_down = _matmul(u, dz, "tn", BF16, f"ffn{tag}_dwdown", 1408, 1024, 512)
    outs = _ffn_mid_bwd(h, du, cw, cb, f"ffn{tag}_mid_bwd", comm=comm)
    (dhg, dhv, dcwg, dcwv, dcbg, dcbv), moved = outs if comm is not None else (outs, [])
    d_w_up = jnp.concatenate([_matmul(x_in, dhg, "tn", BF16, f"ffn{tag}_dwup_gate", 1024, 1408, 1024),
                              _matmul(x_in, dhv, "tn", BF16, f"ffn{tag}_dwup_value", 1024, 1408, 1024)], axis=1)
    dx = _matmul(dhv, w_up, "nt", F32, f"ffn{tag}_dx_value", 1024, 1024, 1408, b_off=D_FF // 1408)
    dx = _matmul(dhg, w_up, "nt", F32, f"ffn{tag}_dx_gate", 1024, 1024, 1408, res=dx)
    return (dx, _deinterleave(d_w_up), d_w_down.reshape(N_DEV, -1, D_MODEL),
            jnp.concatenate([dcwg, dcwv], axis=1), jnp.concatenate([dcbg, dcbv], axis=1), moved)


def kernel(x, ab_w_in, a_conv_w, a_conv_b, a_norm_g, a_norm_b, b_norm_g, b_norm_b, b_spatial_w, b_spatial_b, ab_w_out, c_w_qkv, c_b_qkv, c_sinks, c_w_o, ffn_w_up, ffn_conv_w, ffn_conv_b, ffn_w_down, ln_g, ln_b, loss_target, m_ab_w_in, m_a_conv_w, m_a_conv_b, m_a_norm_g, m_a_norm_b, m_b_norm_g, m_b_norm_b, m_b_spatial_w, m_b_spatial_b, m_ab_w_out, m_c_w_qkv, m_c_b_qkv, m_c_sinks, m_c_w_o, m_ffn_w_up, m_ffn_conv_w, m_ffn_conv_b, m_ffn_w_down, m_ln_g, m_ln_b, v_ab_w_in, v_a_conv_w, v_a_conv_b, v_a_norm_g, v_a_norm_b, v_b_norm_g, v_b_norm_b, v_b_spatial_w, v_b_spatial_b, v_ab_w_out, v_c_w_qkv, v_c_b_qkv, v_c_sinks, v_c_w_o, v_ffn_w_up, v_ffn_conv_w, v_ffn_conv_b, v_ffn_w_down, v_ln_g, v_ln_b):
    me = 4 * lax.axis_index("x") + 2 * lax.axis_index("y") + lax.axis_index("c")
    xt = x[0]
    t = xt.shape[0]

    small_shard_shapes = [a_conv_w.shape, c_b_qkv.shape, ffn_conv_w.shape, ln_g.shape, ln_b.shape]
    up_shard = [ffn_w_up[l].astype(BF16) for l in range(2)]
    down_shard = [ffn_w_down[l].astype(BF16) for l in range(2)]
    g_win, g_small = _comm_only(
        _Comm(gather=[ab_w_in[0].astype(BF16), _pack([a_conv_w, c_b_qkv, ffn_conv_w, ln_g, ln_b])]), "gather_first")
    w_in = _interleave(g_win)
    g_acw, g_bqkv, g_fcw, g_lng, g_lnb = _unpack(g_small, small_shard_shapes)
    acw = _interleave(g_acw[:, 0])
    bqkv = g_bqkv[:, 0].reshape(1, -1)
    fcw = [_interleave(g_fcw[:, l]) for l in range(2)]
    lng = jnp.transpose(g_lng, (1, 2, 0, 3)).reshape(2, 2, 1, D_MODEL)
    lnb = jnp.transpose(g_lnb, (1, 2, 0, 3)).reshape(2, 2, 1, D_MODEL)
    fcb = [ffn_conv_b[l:l + 1] for l in range(2)]
    ms = b_spatial_w[0]
    mst = jnp.swapaxes(ms, 1, 2)
    sbt = b_spatial_b[0].T

    h0, (g_wout,) = _matmul(xt, w_in, "nn", BF16, "mix_in", 1024, 1024, 1024, comm=_Comm(gather=[ab_w_out[0].astype(BF16)]))
    w_out = g_wout.reshape(D_MODEL, D_MODEL)
    cat, (g_wup0,) = _mixer_mid_fwd(h0, acw, a_conv_b, a_norm_g, a_norm_b, b_norm_g, b_norm_b, ms, sbt, "mix_mid_fwd",
                                    comm=_Comm(gather=[up_shard[0]]))
    w_up0 = _interleave(g_wup0)
    z1, x1 = _matmul_res_ln(cat, w_out, xt, lng[0, 0], lnb[0, 0], "mix_out_ln", 512, D_MODEL)
    hf0, (g_wdown0, g_wqkv, g_wo) = _matmul(x1, w_up0, "nn", BF16, "ffn0_up", 1024, 1408, 1024,
                                            comm=_Comm(gather=[down_shard[0], c_w_qkv[0].astype(BF16), c_w_o[0].astype(BF16)]))
    w_down0 = g_wdown0.reshape(D_FF, D_MODEL)
    w_qkv = _interleave(g_wqkv)
    w_o = g_wo.reshape(D_MODEL, D_MODEL)
    u0, (g_wup1,) = _ffn_mid_fwd(hf0, fcw[0], fcb[0], "ffn0_mid_fwd", comm=_Comm(gather=[up_shard[1]]))
    w_up1 = _interleave(g_wup1)
    (z2, x2), (g_wdown1,) = _matmul_res_ln(u0, w_down0, x1, lng[0, 1], lnb[0, 1], "ffn0_down_ln", 512, D_FF,
                                           comm=_Comm(gather=[down_shard[1]]))
    w_down1 = g_wdown1.reshape(D_FF, D_MODEL)
    qkv = _matmul(x2, w_qkv, "nn", BF16, "att_qkv", 1024, 1280, 1024, bias=bqkv)
    att = _attn_fwd(qkv, c_sinks, "att_fwd")
    z3, x3 = _matmul_res_ln(att, w_o, x2, lng[1, 0], lnb[1, 0], "att_out_ln", 512, D_MODEL)
    hf1 = _matmul(x3, w_up1, "nn", BF16, "ffn1_up", 1024, 1408, 1024)
    u1 = _ffn_mid_fwd(hf1, fcw[1], fcb[1], "ffn1_mid_fwd")
    z4, _ = _matmul_res_ln(u1, w_down1, x3, lng[1, 1], lnb[1, 1], "ffn1_down_ln", 512, D_FF)

    dz4, dg11, db11, loss_terms = _ln_bwd_loss(z4, lng[1, 1], lnb[1, 1], loss_target[0], "loss_ln_bwd")
    dx3, d_wup1, d_wdown1, d_fcw1, d_fcb1, _ = _ffn_backward(dz4, x3, hf1, u1, w_up1, fcw[1], fcb[1], w_down1, 1)
    dz3, dg10, db10 = _ln_bwd(z3, lng[1, 0], dz4, dx3, "ln10_bwd")
    d_att = _matmul(dz3, w_o, "nt", BF16, "att_dout", 1024, 1024, 1024)
    d_wo = _matmul(att, dz3, "tn", BF16, "att_dwo", 1024, 1024, 512)
    (dq, dkv, dbq, dbkv, dsinks), (p_wup1, p_wdown1) = _attn_bwd(qkv, d_att, c_sinks, "att_bwd",
                                                                  comm=_Comm(exchange=[d_wup1, d_wdown1]))
    d_wqkv = jnp.concatenate([_matmul(x2, dq, "tn", BF16, "att_dwq", 1024, 1024, 512),
                              _matmul(x2, dkv, "tn", BF16, "att_dwkv", 1024, KV_WIDTH, 512)], axis=1)
    dx2 = _matmul(dkv, w_qkv, "nt", F32, "att_dx_kv", 1024, 1024, KV_WIDTH, b_off=Q_WIDTH // KV_WIDTH)
    dx2 = _matmul(dq, w_qkv, "nt", F32, "att_dx_q", 1024, 1024, KV_WIDTH, res=dx2)
    dz2, dg01, db01 = _ln_bwd(z2, lng[0, 1], dz3, dx2, "ln01_bwd")
    dx1, d_wup0, d_wdown0, d_fcw0, d_fcb0, (p_wqkv, p_wo) = _ffn_backward(
        dz2, x1, hf0, u0, w_up0, fcw[0], fcb[0], w_down0, 0,
        comm=_Comm(exchange=[_deinterleave(d_wqkv), d_wo.reshape(N_DEV, -1, D_MODEL)]))
    dz1, dg00, db00 = _ln_bwd(z1, lng[0, 0], dz2, dx1, "ln00_bwd")
    dcat = _matmul(dz1, w_out, "nt", BF16, "mix_dcat", 1024, 1024, 1024)
    d_wout = _matmul(cat, dz1, "tn", BF16, "mix_dwout", 1024, 1024, 512)
    (dh0, d_acw, d_acb, d_ang, d_anb, d_bng, d_bnb, d_ms, d_sb), (p_wup0, p_wdown0, p_wout) = _mixer_mid_bwd(
        h0, dcat, acw, a_conv_b, a_norm_g, a_norm_b, b_norm_g, b_norm_b, ms, mst, sbt, "mix_mid_bwd",
        comm=_Comm(exchange=[d_wup0, d_wdown0, d_wout.reshape(N_DEV, -1, D_MODEL)]))
    d_bqkv = jnp.concatenate([dbq, dbkv], axis=1)
    d_lng = jnp.stack([jnp.stack([dg00, dg01]), jnp.stack([dg10, dg11])])
    d_lnb = jnp.stack([jnp.stack([db00, db01]), jnp.stack([db10, db11])])
    small_full = [d_acb, d_ang, d_anb, d_bng, d_bnb, d_ms, d_sb, dsinks[:, :N_Q_HEADS], jnp.concatenate([d_fcb0, d_fcb1], axis=0),
                  d_acw, d_bqkv, jnp.stack([d_fcw0, d_fcw1]), d_lng, d_lnb]
    d_win, (g_small_grads,) = _matmul(xt, dh0, "tn", BF16, "mix_dwin", 1024, 1024, 512, comm=_Comm(gather=[_pack(small_full)]))
    grad_x, (p_win,) = _matmul(dh0, w_in, "nt", F32, "mix_dx", 1024, 1024, 1024, res=dz1, res_scale=ALPHA,
                               comm=_Comm(exchange=[_deinterleave(d_win)]))

    loss = lax.psum(0.5 / D_MODEL * jnp.sum(loss_terms), ("x", "y", "c"))

    big = {}
    for nm, p, w, m, v, tr in [
            ("ab_w_in", [p_win], ab_w_in, m_ab_w_in, v_ab_w_in, 256), ("ab_w_out", [p_wout], ab_w_out, m_ab_w_out, v_ab_w_out, 128),
            ("c_w_qkv", [p_wqkv], c_w_qkv, m_c_w_qkv, v_c_w_qkv, 256), ("c_w_o", [p_wo], c_w_o, m_c_w_o, v_c_w_o, 128),
            ("ffn_w_up", [p_wup0, p_wup1], ffn_w_up, m_ffn_w_up, v_ffn_w_up, 256),
            ("ffn_w_down", [p_wdown0, p_wdown1], ffn_w_down, m_ffn_w_down, v_ffn_w_down, 176)]:
        two_d = (-1, w.shape[-1])
        outs = _adamw_big(p, w.reshape(two_d), m.reshape(two_d), v.reshape(two_d), "adamw_" + nm, tr)
        big[nm] = [o.reshape(w.shape) for o in outs]

    gs = _unpack(g_small_grads, [a.shape for a in small_full])

    def my_shard(g, width):
        g = g.reshape(g.shape[:-1] + (N_DEV, width))
        return lax.dynamic_index_in_dim(g, me, axis=g.ndim - 2, keepdims=False)

    small_names = ["a_conv_b", "a_norm_g", "a_norm_b", "b_norm_g", "b_norm_b", "b_spatial_w", "b_spatial_b", "c_sinks", "ffn_conv_b",
                   "a_conv_w", "c_b_qkv", "ffn_conv_w", "ln_g", "ln_b"]
    small_w = [a_conv_b, a_norm_g, a_norm_b, b_norm_g, b_norm_b, b_spatial_w, b_spatial_b, c_sinks, ffn_conv_b,
               a_conv_w, c_b_qkv, ffn_conv_w, ln_g, ln_b]
    small_m = [m_a_conv_b, m_a_norm_g, m_a_norm_b, m_b_norm_g, m_b_norm_b, m_b_spatial_w, m_b_spatial_b, m_c_sinks, m_ffn_conv_b,
               m_a_conv_w, m_c_b_qkv, m_ffn_conv_w, m_ln_g, m_ln_b]
    small_v = [v_a_conv_b, v_a_norm_g, v_a_norm_b, v_b_norm_g, v_b_norm_b, v_b_spatial_w, v_b_spatial_b, v_c_sinks, v_ffn_conv_b,
               v_a_conv_w, v_c_b_qkv, v_ffn_conv_w, v_ln_g, v_ln_b]
    gs[9:] = [my_shard(g, w.shape[-1]) for g, w in zip(gs[9:], small_w[9:])]
    two_d = [(-1, w.shape[-1]) for w in small_w]
    outs = _adamw_small([g.reshape((N_DEV,) + w.reshape(s).shape) for g, w, s in zip(gs, small_w, two_d)],
                        [w.reshape(s) for w, s in zip(small_w, two_d)], [m.reshape(s) for m, s in zip(small_m, two_d)],
                        [v.reshape(s) for v, s in zip(small_v, two_d)], "adamw_small")
    small = {nm: [o.reshape(w.shape) for o in outs[4 * a:4 * a + 4]] for a, (nm, w) in enumerate(zip(small_names, small_w))}

    res = {**big, **small}
    order = ["ab_w_in", "a_conv_w", "a_conv_b", "a_norm_g", "a_norm_b", "b_norm_g", "b_norm_b", "b_spatial_w", "b_spatial_b", "ab_w_out",
             "c_w_qkv", "c_b_qkv", "c_sinks", "c_w_o", "ffn_w_up", "ffn_conv_w", "ffn_conv_b", "ffn_w_down", "ln_g", "ln_b"]
    return (loss, grad_x[None], *[res[nm][0] for nm in order], *[res[nm][1] for nm in order],
            *[res[nm][2] for nm in order], *[res[nm][3] for nm in order])
```

```python
import functools
import math

import jax
import jax.numpy as jnp
from jax import lax
from jax.experimental import pallas as pl
from jax.experimental.pallas import tpu as pltpu

F32 = jnp.float32
BF16 = jnp.bfloat16

N_DEV = 8
D_MODEL = 1024
A_WIDTH = 512
A_KERNEL = 31
B_GROUPS = 4
B_CHUNK = 128
HEAD_DIM = 64
N_Q_HEADS = 16
N_KV_HEADS = 2
ATT_BLOCK = 128
D_FF = 2816
FFN_KERNEL = 3
ALPHA = (2.0 * 2) ** 0.25
LN_EPS = 1e-5
GELU_K = math.sqrt(2.0 / math.pi)
GELU_C = 0.044715
ADAM_LR = 0.001
ADAM_B1 = 0.9
ADAM_B2 = 0.999
ADAM_EPS = 1e-08
ADAM_WD = 0.01
ADAM_STEP = 10
VMEM_LIMIT = 56 * 1024 * 1024
MESH_ID = pl.DeviceIdType.MESH


def _params(*sem):
    return pltpu.CompilerParams(dimension_semantics=sem, vmem_limit_bytes=VMEM_LIMIT)


def _gelu(x):
    t = jnp.tanh(GELU_K * x * (1.0 + GELU_C * x * x))
    return 0.5 * x * (1.0 + t)


def _gelu_and_grad(x):
    x2 = x * x
    t = jnp.tanh(GELU_K * x * (1.0 + GELU_C * x2))
    g = 0.5 * x * (1.0 + t)
    dg = 0.5 * (1.0 + t) + 0.5 * x * (1.0 - t * t) * (GELU_K * (1.0 + 3.0 * GELU_C * x2))
    return g, dg


def _sigmoid(x):
    return 1.0 / (1.0 + jnp.exp(-x))


def _ln_stats(z):
    mu = jnp.mean(z, axis=-1, keepdims=True)
    zc = z - mu
    var = jnp.mean(zc * zc, axis=-1, keepdims=True)
    r = lax.rsqrt(var + LN_EPS)
    return zc * r, r


def _ln_bwd_rows(dn, nh, r):
    return r * (dn - jnp.mean(dn, axis=-1, keepdims=True) - nh * jnp.mean(dn * nh, axis=-1, keepdims=True))


def _colsum(x):
    return jnp.sum(x, axis=0, keepdims=True)


def _dot(a, b, dims):
    return lax.dot_general(a.astype(BF16), b.astype(BF16), (dims, ((), ())), preferred_element_type=F32)


NN = ((1,), (0,))
NT = ((1,), (1,))
TN = ((0,), (0,))


ANY = pl.BlockSpec(memory_space=pl.ANY)
N_RELATIONS = N_DEV - 1


def _my_place():
    return lax.axis_index("x"), lax.axis_index("y"), lax.axis_index("c")


class _Comm:
    def __init__(self, gather=(), exchange=()):
        self.arrs = list(gather) + list(exchange)
        self.n_gather = len(gather)
        self.n = len(self.arrs)

    def out_shape(self):
        return [jax.ShapeDtypeStruct(((N_DEV,) + a.shape) if i < self.n_gather else a.shape, a.dtype)
                for i, a in enumerate(self.arrs)]

    def sems(self):
        return [pltpu.SemaphoreType.DMA((self.n, N_RELATIONS)), pltpu.SemaphoreType.DMA((self.n, N_RELATIONS)),
                pltpu.SemaphoreType.DMA((self.n,))]

    def _gather_copy(self, ins, outs, sems, a, k, place, to, from_input=False):
        px, py, pc = place
        block = outs[a].at[4 * px + 2 * py + pc]
        return pltpu.make_async_remote_copy(
            src_ref=ins[a] if from_input else block, dst_ref=block,
            send_sem=sems[0].at[a, k], recv_sem=sems[1].at[a, k], device_id=to, device_id_type=MESH_ID)

    def _exchange_copy(self, ins, outs, sems, a, k, landing=False):
        x, y, c = _my_place()
        me = 4 * x + 2 * y + c
        peer = (x ^ (k >> 2), y ^ ((k >> 1) & 1), c ^ (k & 1))
        return pltpu.make_async_remote_copy(
            src_ref=ins[a].at[me ^ k], dst_ref=outs[a].at[(me ^ k) if landing else me],
            send_sem=sems[0].at[a, k - 1], recv_sem=sems[1].at[a, k - 1], device_id=peer, device_id_type=MESH_ID)

    def _local_copy(self, ins, outs, sems, a):
        x, y, c = _my_place()
        me = 4 * x + 2 * y + c
        src = ins[a] if a < self.n_gather else ins[a].at[me]
        return pltpu.make_async_copy(src, outs[a].at[me], sems[2].at[a])

    def _first_stage(self, ins, outs, sems, a):
        x, y, c = _my_place()
        me = (x, y, c)
        chips = [(1 - x, y), (x, 1 - y), (1 - x, 1 - y)]
        return ([self._gather_copy(ins, outs, sems, a, 0, me, (x, y, 1 - c), from_input=True)]
                + [self._gather_copy(ins, outs, sems, a, 1 + j, me, (*chip, c), from_input=True) for j, chip in enumerate(chips)])

    def start(self, ins, outs, sems):
        for a in range(self.n):
            self._local_copy(ins, outs, sems, a).start()
        for a in range(self.n_gather):
            for cp in self._first_stage(ins, outs, sems, a):
                cp.start()
        for k in range(1, N_DEV):
            for a in range(self.n_gather, self.n):
                self._exchange_copy(ins, outs, sems, a, k).start()

    def finish(self, ins, outs, sems):
        x, y, c = _my_place()
        me, sibling = (x, y, c), (x, y, 1 - c)
        chips = [(1 - x, y), (x, 1 - y), (1 - x, 1 - y)]
        passed = []
        for j, chip in enumerate(chips):
            for a in range(self.n_gather):
                self._gather_copy(ins, outs, sems, a, 1 + j, (*chip, c), me).wait_recv()
                fwd = self._gather_copy(ins, outs, sems, a, 4 + j, (*chip, c), sibling)
                fwd.start()
                passed.append(fwd)
        for a in range(self.n_gather):
            self._gather_copy(ins, outs, sems, a, 0, sibling, me).wait_recv()
            for j, chip in enumerate(chips):
                self._gather_copy(ins, outs, sems, a, 4 + j, (*chip, 1 - c), me).wait_recv()
        for k in range(1, N_DEV):
            for a in range(self.n_gather, self.n):
                self._exchange_copy(ins, outs, sems, a, k, landing=True).wait_recv()
        for a in range(self.n_gather):
            for cp in self._first_stage(ins, outs, sems, a):
                cp.wait_send()
        for cp in passed:
            cp.wait_send()
        for k in range(1, N_DEV):
            for a in range(self.n_gather, self.n):
                self._exchange_copy(ins, outs, sems, a, k).wait_send()
        for a in range(self.n):
            self._local_copy(ins, outs, sems, a).wait()


def _comm_only(comm, name):
    def body(*refs):
        ins, outs, sems = refs[:comm.n], refs[comm.n:2 * comm.n], refs[2 * comm.n:]
        comm.start(ins, outs, sems)
        comm.finish(ins, outs, sems)

    return pl.pallas_call(body, name=name, in_specs=[ANY] * comm.n, out_specs=[ANY] * comm.n,
                          out_shape=comm.out_shape(), scratch_shapes=comm.sems())(*comm.arrs)


def _call(body, *, name, grid, in_specs, out_specs, out_shape, args, sem, scratch_shapes=(), comm=None):
    in_specs, out_specs, out_shape, scratch_shapes = list(in_specs), list(out_specs), list(out_shape), list(scratch_shapes)
    if comm is None:
        outs = pl.pallas_call(body, name=name, grid=grid, in_specs=in_specs, out_specs=out_specs, out_shape=out_shape,
                              scratch_shapes=scratch_shapes, compiler_params=_params(*sem))(*args)
        return list(outs), []
    n_in, n_out, n_scr, nc = len(in_specs), len(out_specs), len(scratch_shapes), comm.n

    def wrapped(*refs):
        ins, refs = refs[:n_in], refs[n_in:]
        c_in, refs = refs[:nc], refs[nc:]
        outs, refs = refs[:n_out], refs[n_out:]
        c_out, refs = refs[:nc], refs[nc:]
        scr, sems = refs[:n_scr], refs[n_scr:]
        first = functools.reduce(jnp.logical_and, [pl.program_id(ax) == 0 for ax in range(len(grid))])
        last = functools.reduce(jnp.logical_and, [pl.program_id(ax) == g - 1 for ax, g in enumerate(grid)])

        @pl.when(first)
        def _():
            comm.start(c_in, c_out, sems)

        body(*ins, *outs, *scr)

        @pl.when(last)
        def _():
            comm.finish(c_in, c_out, sems)

    outs = pl.pallas_call(
        wrapped, name=name, grid=grid, in_specs=in_specs + [ANY] * nc, out_specs=out_specs + [ANY] * nc,
        out_shape=out_shape + comm.out_shape(), scratch_shapes=scratch_shapes + comm.sems(),
        compiler_params=_params(*(["arbitrary"] * len(grid))))(*args, *comm.arrs)
    return list(outs[:n_out]), list(outs[n_out:])


def _matmul(a, b, mode, out_dtype, name, tm, tn, tk, *, bias=None, res=None, res_scale=1.0, b_off=0, comm=None):
    tm = min(tm, a.shape[1] if mode == "tn" else a.shape[0])
    tk = min(tk, a.shape[0] if mode == "tn" else a.shape[1])
    if mode == "nn":
        (m, k), n = a.shape, b.shape[1]
        a_spec = pl.BlockSpec((tm, tk), lambda i, j, kk: (i, kk))
        b_spec = pl.BlockSpec((tk, tn), lambda i, j, kk: (kk + b_off, j))
        dims = NN
    elif mode == "nt":
        (m, k), n = a.shape, b.shape[0]
        a_spec = pl.BlockSpec((tm, tk), lambda i, j, kk: (i, kk))
        b_spec = pl.BlockSpec((tn, tk), lambda i, j, kk: (j, kk + b_off))
        dims = NT
    else:
        (k, m), n = a.shape, b.shape[1]
        a_spec = pl.BlockSpec((tk, tm), lambda i, j, kk: (kk, i))
        b_spec = pl.BlockSpec((tk, tn), lambda i, j, kk: (kk, j))
        dims = TN
    assert m % tm == 0 and n % tn == 0 and k % tk == 0, (name, m, n, k)
    nk = k // tk
    in_specs = [a_spec, b_spec]
    args = [a, b]
    if bias is not None:
        in_specs.append(pl.BlockSpec((1, tn), lambda i, j, kk: (0, j)))
        args.append(bias)
    if res is not None:
        in_specs.append(pl.BlockSpec((tm, tn), lambda i, j, kk: (i, j)))
        args.append(res)

    def finish(out, refs, o_ref):
        pos = 2
        if bias is not None:
            out = out + refs[pos][...]
            pos += 1
        if res is not None:
            out = out + res_scale * refs[pos][...].astype(F32)
        o_ref[...] = out.astype(out_dtype)

    def body_one_step(*refs):
        finish(_dot(refs[0][...], refs[1][...], dims), refs, refs[-1])

    def body(*refs):
        a_ref, b_ref = refs[0], refs[1]
        o_ref, acc = refs[-2], refs[-1]
        kk = pl.program_id(2)

        @pl.when(kk == 0)
        def _():
            acc[...] = jnp.zeros_like(acc)

        acc[...] += _dot(a_ref[...], b_ref[...], dims)

        @pl.when(kk == nk - 1)
        def _():
            finish(acc[...], refs, o_ref)

    (out,), moved = _call(
        body_one_step if nk == 1 else body, name=name, grid=(m // tm, n // tn, nk),
        in_specs=in_specs, out_specs=[pl.BlockSpec((tm, tn), lambda i, j, kk: (i, j))],
        out_shape=[jax.ShapeDtypeStruct((m, n), out_dtype)],
        scratch_shapes=[] if nk == 1 else [pltpu.VMEM((tm, tn), F32)],
        sem=("parallel", "parallel", "arbitrary"), args=args, comm=comm)
    return out if comm is None else (out, moved)


def _matmul_res_ln(a, b, xres, g, beta, name, tm, tk, comm=None):
    t, k = a.shape
    d = b.shape[1]
    nk = k // tk
    assert t % tm == 0 and k % tk == 0

    def body(a_ref, b_ref, x_ref, g_ref, beta_ref, z_ref, xo_ref, acc):
        kk = pl.program_id(1)

        @pl.when(kk == 0)
        def _():
            acc[...] = jnp.zeros_like(acc)

        acc[...] += _dot(a_ref[...], b_ref[...], NN)

        @pl.when(kk == nk - 1)
        def _():
            z = ALPHA * x_ref[...] + acc[...]
            nh, _ = _ln_stats(z)
            z_ref[...] = z
            xo_ref[...] = nh * g_ref[...] + beta_ref[...]

    row = pl.BlockSpec((tm, d), lambda i, kk: (i, 0))
    vec = pl.BlockSpec((1, d), lambda i, kk: (0, 0))
    outs, moved = _call(
        body, name=name, grid=(t // tm, nk),
        in_specs=[pl.BlockSpec((tm, tk), lambda i, kk: (i, kk)), pl.BlockSpec((tk, d), lambda i, kk: (kk, 0)), row, vec, vec],
        out_specs=[row, row],
        out_shape=[jax.ShapeDtypeStruct((t, d), F32), jax.ShapeDtypeStruct((t, d), F32)],
        scratch_shapes=[pltpu.VMEM((tm, d), F32)],
        sem=("parallel", "arbitrary"), args=(a, b, xres, g, beta), comm=comm)
    return outs if comm is None else (outs, moved)


def _ln_bwd(z, g, dres, dbr, name, tm=512):
    t, d = z.shape

    def body(z_ref, g_ref, dres_ref, dbr_ref, dz_ref, dg_ref, db_ref):
        @pl.when(pl.program_id(0) == 0)
        def _():
            dg_ref[...] = jnp.zeros_like(dg_ref)
            db_ref[...] = jnp.zeros_like(db_ref)

        nh, r = _ln_stats(z_ref[...])
        dy = ALPHA * dres_ref[...] + dbr_ref[...].astype(F32)
        dg_ref[...] += _colsum(dy * nh)
        db_ref[...] += _colsum(dy)
        dz_ref[...] = _ln_bwd_rows(dy * g_ref[...], nh, r)

    row = pl.BlockSpec((tm, d), lambda i: (i, 0))
    vec = pl.BlockSpec((1, d), lambda i: (0, 0))
    return pl.pallas_call(
        body, name=name, grid=(t // tm,), in_specs=[row, vec, row, row], out_specs=[row, vec, vec],
        out_shape=[jax.ShapeDtypeStruct((t, d), F32), jax.ShapeDtypeStruct((1, d), F32), jax.ShapeDtypeStruct((1, d), F32)],
        compiler_params=_params("arbitrary"),
    )(z, g, dres, dbr)


def _ln_bwd_loss(z, g, beta, target, name, tm=512):
    t, d = z.shape

    def body(z_ref, g_ref, beta_ref, t_ref, dz_ref, dg_ref, db_ref, loss_ref):
        @pl.when(pl.program_id(0) == 0)
        def _():
            dg_ref[...] = jnp.zeros_like(dg_ref)
            db_ref[...] = jnp.zeros_like(db_ref)
            loss_ref[...] = jnp.zeros_like(loss_ref)

        nh, r = _ln_stats(z_ref[...])
        err = nh * g_ref[...] + beta_ref[...] - t_ref[...]
        loss_ref[...] += _colsum(err * err)
        dy = err * (1.0 / d)
        dg_ref[...] += _colsum(dy * nh)
        db_ref[...] += _colsum(dy)
        dz_ref[...] = _ln_bwd_rows(dy * g_ref[...], nh, r)

    row = pl.BlockSpec((tm, d), lambda i: (i, 0))
    vec = pl.BlockSpec((1, d), lambda i: (0, 0))
    vshape = jax.ShapeDtypeStruct((1, d), F32)
    return pl.pallas_call(
        body, name=name, grid=(t // tm,), in_specs=[row, vec, vec, row], out_specs=[row, vec, vec, vec],
        out_shape=[jax.ShapeDtypeStruct((t, d), F32), vshape, vshape, vshape],
        compiler_params=_params("arbitrary"),
    )(z, g, beta, target)


FFN_HALO = 16
FFN_CHUNK = 32
LANES = 128
SUBLANES = 8


def _fold(x):
    return jnp.sum(x.reshape(x.shape[0] // SUBLANES, SUBLANES, x.shape[1]), axis=0)


def _ffn_mid_fwd(h, cw, cb, name, tm=1024, tc=256, comm=None):
    t, f2 = h.shape
    tm = min(tm, t)
    f = f2 // 2
    nj, nt, hb = f // tc, t // tm, tm // FFN_HALO

    def body(hg, hgp, hv, hvp, cwg, cwv, cbg, cbv, u_ref, sg, sv):
        i = pl.program_id(1)

        def conv(main, prev, s, w, b):
            s[0:FFN_HALO, :] = jnp.where(i > 0, prev[...].astype(F32), 0.0)
            s[FFN_HALO:, :] = main[...].astype(F32)
            o = FFN_HALO - FFN_KERNEL + 1
            return (w[0:1, :] * s[pl.ds(o, tm), :] + w[1:2, :] * s[pl.ds(o + 1, tm), :]
                    + w[2:3, :] * s[pl.ds(o + 2, tm), :] + b[...])

        cg = conv(hg, hgp, sg, cwg, cbg)
        cv = conv(hv, hvp, sv, cwv, cbv)
        u_ref[...] = (_gelu(cg) * cv).astype(BF16)

    def main_spec(off):
        return pl.BlockSpec((tm, tc), lambda j, i: (i, j + off))

    def prev_spec(off):
        return pl.BlockSpec((FFN_HALO, tc), lambda j, i: (jnp.maximum(i * hb - 1, 0), j + off))

    def par_spec(rows, off):
        return pl.BlockSpec((rows, tc), lambda j, i: (0, j + off))

    (u,), moved = _call(
        body, name=name, grid=(nj, nt),
        in_specs=[main_spec(0), prev_spec(0), main_spec(nj), prev_spec(nj),
                  par_spec(FFN_KERNEL, 0), par_spec(FFN_KERNEL, nj), par_spec(1, 0), par_spec(1, nj)],
        out_specs=[pl.BlockSpec((tm, tc), lambda j, i: (i, j))],
        out_shape=[jax.ShapeDtypeStruct((t, f), BF16)],
        scratch_shapes=[pltpu.VMEM((tm + FFN_HALO, tc), F32), pltpu.VMEM((tm + FFN_HALO, tc), F32)],
        sem=("parallel", "arbitrary"), args=(h, h, h, h, cw, cw, cb, cb), comm=comm)
    return u if comm is None else (u, moved)


def _ffn_mid_bwd(h, du, cw, cb, name, tm=1024, tc=256, comm=None):
    t, f2 = h.shape
    tm = min(tm, t)
    f = f2 // 2
    nj, nt, hb = f // tc, t // tm, tm // FFN_HALO

    ch = min(FFN_CHUNK, tm)
    ahead = ch + SUBLANES

    def body(hg, hgp, hgn, hv, hvp, hvn, du_ref, dun_ref, cwg, cwv, cbg, cbv,
             dhg_ref, dhv_ref, dcwg_ref, dcwv_ref, dcbg_ref, dcbv_ref, sg, sv, sdu):
        i = pl.program_id(1)

        @pl.when(i == 0)
        def _():
            for ref in (dcwg_ref, dcwv_ref, dcbg_ref, dcbv_ref):
                ref[...] = jnp.zeros_like(ref)

        for main, prev, nxt, s in ((hg, hgp, hgn, sg), (hv, hvp, hvn, sv)):
            s[0:FFN_HALO, :] = jnp.where(i > 0, prev[...].astype(F32), 0.0)
            s[FFN_HALO:FFN_HALO + tm, :] = main[...].astype(F32)
            s[FFN_HALO + tm:, :] = nxt[...].astype(F32)
        sdu[0:tm, :] = du_ref[...].astype(F32)
        sdu[tm:, :] = jnp.where(i < nt - 1, dun_ref[...].astype(F32), 0.0)
        o = SUBLANES - FFN_KERNEL + 1
        for lg in range(tc // LANES):
            cols = slice(lg * LANES, (lg + 1) * LANES)
            wg, wv = [cwg[k:k + 1, cols] for k in range(FFN_KERNEL)], [cwv[k:k + 1, cols] for k in range(FFN_KERNEL)]
            bg, bv = cbg[:, cols], cbv[:, cols]

            def chunk(c, acc):
                base = pl.multiple_of(c * ch, ch)
                eg = sg[pl.ds(base + FFN_HALO - SUBLANES, ahead + SUBLANES), cols]
                ev = sv[pl.ds(base + FFN_HALO - SUBLANES, ahead + SUBLANES), cols]
                hgs = [eg[o + k:o + k + ahead] for k in range(FFN_KERNEL)]
                hvs = [ev[o + k:o + k + ahead] for k in range(FFN_KERNEL)]
                cg = wg[0] * hgs[0] + wg[1] * hgs[1] + wg[2] * hgs[2] + bg
                cv = wv[0] * hvs[0] + wv[1] * hvs[1] + wv[2] * hvs[2] + bv
                du_e = sdu[pl.ds(base, ahead), cols]
                gl, dgl = _gelu_and_grad(cg)

                def back(d, hs, w, dh_ref):
                    own = d[0:ch]
                    dh = w[2] * own + w[1] * d[1:1 + ch] + w[0] * d[2:2 + ch]
                    dh_ref[pl.ds(base, ch), cols] = dh.astype(BF16)
                    return [_fold(own)] + [_fold(own * hs[k][0:ch]) for k in range(FFN_KERNEL)]

                sums = back(du_e * cv * dgl, hgs, wg, dhg_ref) + back(du_e * gl, hvs, wv, dhv_ref)
                return tuple(a + s_ for a, s_ in zip(acc, sums))

            zero = jnp.zeros((SUBLANES, LANES), F32)
            acc = lax.fori_loop(0, tm // ch, chunk, (zero,) * (2 * (1 + FFN_KERNEL)))
            dcbg_ref[:, cols] += _colsum(acc[0])
            dcbv_ref[:, cols] += _colsum(acc[1 + FFN_KERNEL])
            for k in range(FFN_KERNEL):
                dcwg_ref[k:k + 1, cols] += _colsum(acc[1 + k])
                dcwv_ref[k:k + 1, cols] += _colsum(acc[2 + FFN_KERNEL + k])

    last_blk = t // FFN_HALO - 1

    def main_spec(off):
        return pl.BlockSpec((tm, tc), lambda j, i: (i, j + off))

    def prev_spec(off):
        return pl.BlockSpec((FFN_HALO, tc), lambda j, i: (jnp.maximum(i * hb - 1, 0), j + off))

    def next_spec(off):
        return pl.BlockSpec((FFN_HALO, tc), lambda j, i: (jnp.minimum((i + 1) * hb, last_blk), j + off))

    def par_spec(rows, off):
        return pl.BlockSpec((rows, tc), lambda j, i: (0, j + off))

    out_tile = pl.BlockSpec((tm, tc), lambda j, i: (i, j))
    outs, moved = _call(
        body, name=name, grid=(nj, nt),
        in_specs=[main_spec(0), prev_spec(0), next_spec(0), main_spec(nj), prev_spec(nj), next_spec(nj),
                  main_spec(0), next_spec(0),
                  par_spec(FFN_KERNEL, 0), par_spec(FFN_KERNEL, nj), par_spec(1, 0), par_spec(1, nj)],
        out_specs=[out_tile, out_tile, par_spec(FFN_KERNEL, 0), par_spec(FFN_KERNEL, 0), par_spec(1, 0), par_spec(1, 0)],
        out_shape=[jax.ShapeDtypeStruct((t, f), BF16), jax.ShapeDtypeStruct((t, f), BF16),
                   jax.ShapeDtypeStruct((FFN_KERNEL, f), F32), jax.ShapeDtypeStruct((FFN_KERNEL, f), F32),
                   jax.ShapeDtypeStruct((1, f), F32), jax.ShapeDtypeStruct((1, f), F32)],
        scratch_shapes=[pltpu.VMEM((tm + 2 * FFN_HALO, tc), F32), pltpu.VMEM((tm + 2 * FFN_HALO, tc), F32),
                        pltpu.VMEM((tm + FFN_HALO, tc), F32)],
        sem=("parallel", "arbitrary"), args=(h, h, h, h, h, h, du, du, cw, cw, cb, cb), comm=comm)
    return outs if comm is None else (outs, moved)


MIX_HALO = 32


def _glu(hh):
    return hh[:, 0:A_WIDTH] * _sigmoid(hh[:, A_WIDTH:2 * A_WIDTH])


def _tril_mask():
    return lax.broadcasted_iota(jnp.int32, (B_CHUNK, B_CHUNK), 0) >= lax.broadcasted_iota(jnp.int32, (B_CHUNK, B_CHUNK), 1)


def _spatial_mix(q, ms_ref, sbt_ref, tm):
    mask = _tril_mask()
    ws = [jnp.where(mask, ms_ref[g], 0.0).astype(BF16) for g in range(B_GROUPS)]
    qb = q.astype(BF16)
    rows = []
    for c in range(tm // B_CHUNK):
        cols = [_dot(ws[g], qb[c * B_CHUNK:(c + 1) * B_CHUNK, g * 128:(g + 1) * 128], NN) + sbt_ref[:, g:g + 1]
                for g in range(B_GROUPS)]
        rows.append(jnp.concatenate(cols, axis=1))
    return jnp.concatenate(rows, axis=0)


def _mixer_mid_fwd(h, cw, cb, ag, ab, bg, bb, ms, sbt, name, tm=256, comm=None):
    t = h.shape[0]
    nt, hb = t // tm, tm // MIX_HALO
    o = MIX_HALO - A_KERNEL + 1

    def body(h_ref, hp_ref, cw_ref, cb_ref, ag_ref, ab_ref, bg_ref, bb_ref, ms_ref, sbt_ref, cat_ref, sp):
        i = pl.program_id(0)
        sp[0:MIX_HALO, :] = jnp.where(i > 0, _glu(hp_ref[:, 0:2 * A_WIDTH].astype(F32)), 0.0)
        sp[MIX_HALO:, :] = _glu(h_ref[:, 0:2 * A_WIDTH].astype(F32))
        y = jnp.zeros((tm, A_WIDTH), F32) + cb_ref[...]
        for k in range(A_KERNEL):
            y = y + cw_ref[k:k + 1, :] * sp[pl.ds(o + k, tm), :]
        nh, _ = _ln_stats(y)
        ln = nh * ag_ref[...] + ab_ref[...]
        cat_ref[:, 0:A_WIDTH] = (ln * _sigmoid(ln)).astype(BF16)
        u = _gelu(h_ref[:, 1024:1536].astype(F32))
        nb, _ = _ln_stats(_gelu(h_ref[:, 1536:2048].astype(F32)))
        mixed = _spatial_mix(nb * bg_ref[...] + bb_ref[...], ms_ref, sbt_ref, tm)
        cat_ref[:, A_WIDTH:] = (u * mixed).astype(BF16)

    vec = pl.BlockSpec((1, A_WIDTH), lambda i: (0, 0))
    (cat,), moved = _call(
        body, name=name, grid=(nt,),
        in_specs=[pl.BlockSpec((tm, 2048), lambda i: (i, 0)),
                  pl.BlockSpec((MIX_HALO, 2048), lambda i: (jnp.maximum(i * hb - 1, 0), 0)),
                  pl.BlockSpec((A_KERNEL, A_WIDTH), lambda i: (0, 0)), vec, vec, vec, vec, vec,
                  pl.BlockSpec((B_GROUPS, B_CHUNK, B_CHUNK), lambda i: (0, 0, 0)),
                  pl.BlockSpec((B_CHUNK, B_GROUPS), lambda i: (0, 0))],
        out_specs=[pl.BlockSpec((tm, D_MODEL), lambda i: (i, 0))],
        out_shape=[jax.ShapeDtypeStruct((t, D_MODEL), BF16)],
        scratch_shapes=[pltpu.VMEM((tm + MIX_HALO, A_WIDTH), F32)],
        sem=("parallel",), args=(h, h, cw, cb, ag, ab, bg, bb, ms, sbt), comm=comm)
    return cat if comm is None else (cat, moved)


def _mixer_mid_bwd(h, dcat, cw, cb, ag, ab, bg, bb, ms, mst, sbt, name, tm=256, comm=None):
    t = h.shape[0]
    nt, hb = t // tm, tm // MIX_HALO
    o = MIX_HALO - A_KERNEL + 1
    r = tm + MIX_HALO
    nchunk = tm // B_CHUNK

    def body(h_ref, hp_ref, hn_ref, dc_ref, dcn_ref, cw_ref, cb_ref, ag_ref, ab_ref, bg_ref, bb_ref, ms_ref, mst_ref, sbt_ref,
             dh_ref, dcw_ref, dcb_ref, dag_ref, dab_ref, dbg_ref, dbb_ref, dms_ref, dsb_ref, sp, sdy, sbacc):
        i = pl.program_id(0)

        @pl.when(i == 0)
        def _():
            for ref in (dcw_ref, dcb_ref, dag_ref, dab_ref, dbg_ref, dbb_ref, dms_ref, dsb_ref, sbacc):
                ref[...] = jnp.zeros_like(ref)

        sp[0:MIX_HALO, :] = jnp.where(i > 0, _glu(hp_ref[:, 0:2 * A_WIDTH].astype(F32)), 0.0)
        sp[MIX_HALO:MIX_HALO + tm, :] = _glu(h_ref[:, 0:2 * A_WIDTH].astype(F32))
        sp[MIX_HALO + tm:, :] = _glu(hn_ref[:, 0:2 * A_WIDTH].astype(F32))
        y = jnp.zeros((r, A_WIDTH), F32) + cb_ref[...]
        for k in range(A_KERNEL):
            y = y + cw_ref[k:k + 1, :] * sp[pl.ds(o + k, r), :]
        nh, rs = _ln_stats(y)
        ln = nh * ag_ref[...] + ab_ref[...]
        sg = _sigmoid(ln)
        dao = jnp.concatenate([dc_ref[:, 0:A_WIDTH].astype(F32),
                               jnp.where(i < nt - 1, dcn_ref[:, 0:A_WIDTH].astype(F32), 0.0)], axis=0)
        dln = dao * (sg * (1.0 + ln * (1.0 - sg)))
        dag_ref[...] += _colsum(dln[0:tm] * nh[0:tm])
        dab_ref[...] += _colsum(dln[0:tm])
        sdy[...] = _ln_bwd_rows(dln * ag_ref[...], nh, rs)
        dy_own = sdy[0:tm, :]
        dcb_ref[...] += _colsum(dy_own)
        dp = jnp.zeros((tm, A_WIDTH), F32)
        for k in range(A_KERNEL):
            dcw_ref[k:k + 1, :] += _colsum(dy_own * sp[pl.ds(o + k, tm), :])
            dp = dp + cw_ref[k:k + 1, :] * sdy[pl.ds(A_KERNEL - 1 - k, tm), :]
        av = h_ref[:, 0:A_WIDTH].astype(F32)
        s = _sigmoid(h_ref[:, A_WIDTH:2 * A_WIDTH].astype(F32))
        dh_ref[:, 0:A_WIDTH] = (dp * s).astype(BF16)
        dh_ref[:, A_WIDTH:2 * A_WIDTH] = (dp * av * s * (1.0 - s)).astype(BF16)

        u, dgu = _gelu_and_grad(h_ref[:, 1024:1536].astype(F32))
        w, dgw = _gelu_and_grad(h_ref[:, 1536:2048].astype(F32))
        nb, rb = _ln_stats(w)
        q = nb * bg_ref[...] + bb_ref[...]
        mixed = _spatial_mix(q, ms_ref, sbt_ref, tm)
        dbo = dc_ref[:, A_WIDTH:].astype(F32)
        dh_ref[:, 1024:1536] = (dbo * mixed * dgu).astype(BF16)
        dmx = dbo * u
        mask = _tril_mask()
        wst = [jnp.where(mask.T, mst_ref[g], 0.0).astype(BF16) for g in range(B_GROUPS)]
        qb = q.astype(BF16)
        dmb = dmx.astype(BF16)
        rows = []
        for c in range(nchunk):
            cols = []
            for g in range(B_GROUPS):
                rs_, cs_ = slice(c * B_CHUNK, (c + 1) * B_CHUNK), slice(g * 128, (g + 1) * 128)
                sbacc[g] += dmx[rs_, cs_]
                dms_ref[g] += _dot(dmb[rs_, cs_], qb[rs_, cs_], NT)
                cols.append(_dot(wst[g], dmb[rs_, cs_], NN))
            rows.append(jnp.concatenate(cols, axis=1))
        dq = jnp.concatenate(rows, axis=0)
        dbg_ref[...] += _colsum(dq * nb)
        dbb_ref[...] += _colsum(dq)
        dh_ref[:, 1536:2048] = (_ln_bwd_rows(dq * bg_ref[...], nb, rb) * dgw).astype(BF16)

        @pl.when(i == nt - 1)
        def _():
            for g in range(B_GROUPS):
                dms_ref[g] = jnp.where(mask, dms_ref[g], 0.0)
                dsb_ref[g] = jnp.sum(sbacc[g], axis=1, keepdims=True)

    last_blk = t // MIX_HALO - 1
    vec = pl.BlockSpec((1, A_WIDTH), lambda i: (0, 0))
    mat = pl.BlockSpec((B_GROUPS, B_CHUNK, B_CHUNK), lambda i: (0, 0, 0))
    taps = pl.BlockSpec((A_KERNEL, A_WIDTH), lambda i: (0, 0))

    def halo(width, which):
        if which == "prev":
            return pl.BlockSpec((MIX_HALO, width), lambda i: (jnp.maximum(i * hb - 1, 0), 0))
        return pl.BlockSpec((MIX_HALO, width), lambda i: (jnp.minimum((i + 1) * hb, last_blk), 0))

    vshape = jax.ShapeDtypeStruct((1, A_WIDTH), F32)
    outs, moved = _call(
        body, name=name, grid=(nt,),
        in_specs=[pl.BlockSpec((tm, 2048), lambda i: (i, 0)), halo(2048, "prev"), halo(2048, "next"),
                  pl.BlockSpec((tm, D_MODEL), lambda i: (i, 0)), halo(D_MODEL, "next"),
                  taps, vec, vec, vec, vec, vec, mat, mat, pl.BlockSpec((B_CHUNK, B_GROUPS), lambda i: (0, 0))],
        out_specs=[pl.BlockSpec((tm, 2048), lambda i: (i, 0)), taps, vec, vec, vec, vec, vec, mat,
                   pl.BlockSpec((B_GROUPS, B_CHUNK, 1), lambda i: (0, 0, 0))],
        out_shape=[jax.ShapeDtypeStruct((t, 2048), BF16), jax.ShapeDtypeStruct((A_KERNEL, A_WIDTH), F32),
                   vshape, vshape, vshape, vshape, vshape,
                   jax.ShapeDtypeStruct((B_GROUPS, B_CHUNK, B_CHUNK), F32), jax.ShapeDtypeStruct((B_GROUPS, B_CHUNK, 1), F32)],
        scratch_shapes=[pltpu.VMEM((tm + 2 * MIX_HALO, A_WIDTH), F32), pltpu.VMEM((r, A_WIDTH), F32),
                        pltpu.VMEM((B_GROUPS, B_CHUNK, B_CHUNK), F32)],
        sem=("arbitrary",), args=(h, h, h, dcat, dcat, cw, cb, ag, ab, bg, bb, ms, mst, sbt), comm=comm)
    return outs if comm is None else (outs, moved)


Q_WIDTH = N_Q_HEADS * HEAD_DIM
KV_WIDTH = 2 * N_KV_HEADS * HEAD_DIM
PAIRS_PER_KV = N_Q_HEADS // N_KV_HEADS // 2
ATT_SCALE = 1.0 / math.sqrt(HEAD_DIM)


def _att_mask(n):
    qi = lax.broadcasted_iota(jnp.int32, (ATT_BLOCK, 2 * ATT_BLOCK), 0)
    sj = lax.broadcasted_iota(jnp.int32, (ATT_BLOCK, 2 * ATT_BLOCK), 1)
    diff = qi + ATT_BLOCK - sj
    return (diff >= 0) & (diff < ATT_BLOCK) & ((n > 0) | (sj >= ATT_BLOCK))


def _dup_heads(pair_cols, kv_head):
    lane = lax.broadcasted_iota(jnp.int32, pair_cols.shape, 1)
    rolled = pltpu.roll(pair_cols, HEAD_DIM, 1)
    first = lane < HEAD_DIM
    return jnp.where(first, pair_cols, rolled) if kv_head == 0 else jnp.where(first, rolled, pair_cols)


HEADS_PER_KV = N_Q_HEADS // N_KV_HEADS


def _stack_heads(ref, kh):
    lane = lax.broadcasted_iota(jnp.int32, (ATT_BLOCK, 128), 1)
    rows = []
    for pr in range(PAIRS_PER_KV):
        c0 = (kh * PAIRS_PER_KV + pr) * 128
        pair = ref[:, c0:c0 + 128]
        rows += [jnp.where(lane < HEAD_DIM, pair, jnp.zeros_like(pair)), jnp.where(lane < HEAD_DIM, jnp.zeros_like(pair), pair)]
    return jnp.concatenate(rows, axis=0)


def _unstack_heads(stacked, kh, write):
    lane = lax.broadcasted_iota(jnp.int32, (ATT_BLOCK, 128), 1)
    for pr in range(PAIRS_PER_KV):
        first = stacked[(2 * pr) * ATT_BLOCK:(2 * pr + 1) * ATT_BLOCK]
        second = stacked[(2 * pr + 1) * ATT_BLOCK:(2 * pr + 2) * ATT_BLOCK]
        write((kh * PAIRS_PER_KV + pr) * 128, jnp.where(lane < HEAD_DIM, first, second))


def _sink_column(sink_ref, kh):
    return jnp.concatenate([jnp.full((ATT_BLOCK, 1), sink_ref[0, kh * HEADS_PER_KV + h], F32) for h in range(HEADS_PER_KV)], axis=0)


def _att_probs(q_rows, k2, mask, sink, heads):
    s = _dot(q_rows, k2, NT) * ATT_SCALE
    s = jnp.where(jnp.concatenate([mask] * heads, axis=0), s, -jnp.inf)
    m = jnp.maximum(jnp.max(s, axis=-1, keepdims=True), sink)
    e = jnp.exp(s - m)
    es = jnp.exp(sink - m)
    inv = 1.0 / (jnp.sum(e, axis=-1, keepdims=True) + es)
    return e * inv, es * inv


def _attn_fwd(qkv, sinks, name):
    t = qkv.shape[0]
    nb = t // ATT_BLOCK
    kvb = Q_WIDTH // KV_WIDTH

    def body(sink_ref, q_ref, kv_ref, kvp_ref, o_ref):
        n = pl.program_id(0)
        mask = _att_mask(n)
        kv = jnp.concatenate([kvp_ref[...], kv_ref[...]], axis=0).astype(F32)

        lane = lax.broadcasted_iota(jnp.int32, (ATT_BLOCK, 128), 1)
        for kh in range(N_KV_HEADS):
            k2 = _dup_heads(kv[:, 0:128], kh).astype(BF16)
            v2 = _dup_heads(kv[:, 128:256], kh).astype(BF16)
            for pr in range(PAIRS_PER_KV):
                c0 = (kh * PAIRS_PER_KV + pr) * 128
                q2 = q_ref[:, c0:c0 + 128]
                outs = []
                for half in range(2):
                    head = (kh * PAIRS_PER_KV + pr) * 2 + half
                    qm = jnp.where((lane < HEAD_DIM) == (half == 0), q2, jnp.zeros_like(q2))
                    p, _ = _att_probs(qm, k2, mask, sink_ref[0, head], 1)
                    outs.append(_dot(p, v2, NN))
                o_ref[:, c0:c0 + 128] = jnp.where(lane < HEAD_DIM, outs[0], outs[1]).astype(BF16)

    return pl.pallas_call(
        body, name=name, grid=(nb,),
        in_specs=[pl.BlockSpec(memory_space=pltpu.SMEM),
                  pl.BlockSpec((ATT_BLOCK, Q_WIDTH), lambda n: (n, 0)),
                  pl.BlockSpec((ATT_BLOCK, KV_WIDTH), lambda n: (n, kvb)),
                  pl.BlockSpec((ATT_BLOCK, KV_WIDTH), lambda n: (jnp.maximum(n - 1, 0), kvb))],
        out_specs=pl.BlockSpec((ATT_BLOCK, Q_WIDTH), lambda n: (n, 0)),
        out_shape=jax.ShapeDtypeStruct((t, Q_WIDTH), BF16),
        compiler_params=_params("parallel"),
    )(sinks, qkv, qkv, qkv)


def _attn_bwd(qkv, d_o, sinks, name, comm=None):
    t = qkv.shape[0]
    nb = t // ATT_BLOCK
    kvb = Q_WIDTH // KV_WIDTH

    def body(sink_ref, q_ref, kv_ref, kvp_ref, do_ref, dq_ref, dkv_ref, dbq_ref, dbkv_ref, dsink_ref, carry):
        n = pl.program_id(0)

        @pl.when(n == 0)
        def _():
            for ref in (dbq_ref, dbkv_ref, dsink_ref, carry):
                ref[...] = jnp.zeros_like(ref)
            dkv_ref[...] = jnp.zeros_like(dkv_ref)

        @pl.when(n < nb)
        def _():
            mask = _att_mask(n)
            kv = jnp.concatenate([kvp_ref[...], kv_ref[...]], axis=0).astype(F32)
            lane2 = lax.broadcasted_iota(jnp.int32, (2 * ATT_BLOCK, 128), 1)
            sink_lane = lax.broadcasted_iota(jnp.int32, (1, 128), 1)
            dsink = jnp.zeros((1, 128), F32)
            dk_parts, dv_parts = [], []

            def write(c0, pair):
                dbq_ref[:, c0:c0 + 128] += _colsum(pair)
                dq_ref[:, c0:c0 + 128] = pair.astype(BF16)

            for kh in range(N_KV_HEADS):
                k2 = _dup_heads(kv[:, 0:128], kh).astype(BF16)
                v2 = _dup_heads(kv[:, 128:256], kh).astype(BF16)
                q_all = _stack_heads(q_ref, kh)
                do_all = _stack_heads(do_ref, kh)
                p, ps = _att_probs(q_all, k2, mask, _sink_column(sink_ref, kh), HEADS_PER_KV)
                dp = _dot(do_all, v2, NT)
                delta = jnp.sum(p * dp, axis=-1, keepdims=True)
                ds = p * (dp - delta) * ATT_SCALE
                psd = ps * delta
                for h in range(HEADS_PER_KV):
                    dsink = dsink + jnp.where(sink_lane == kh * HEADS_PER_KV + h,
                                              -jnp.sum(psd[h * ATT_BLOCK:(h + 1) * ATT_BLOCK]), 0.0)
                _unstack_heads(_dot(ds, k2, NN), kh, write)
                dk_acc = _dot(ds, q_all, TN)
                dv_acc = _dot(p, do_all, TN)
                dk_parts.append(dk_acc + pltpu.roll(dk_acc, HEAD_DIM, 1))
                dv_parts.append(dv_acc + pltpu.roll(dv_acc, HEAD_DIM, 1))
            dk = jnp.where(lane2 < HEAD_DIM, dk_parts[0], dk_parts[1])
            dv = jnp.where(lane2 < HEAD_DIM, dv_parts[0], dv_parts[1])
            dkv_new = jnp.concatenate([dk, dv], axis=1)
            done = carry[...] + dkv_new[0:ATT_BLOCK]

            @pl.when(n > 0)
            def _():
                dkv_ref[...] = done.astype(BF16)
                dbkv_ref[...] += _colsum(done)

            carry[...] = dkv_new[ATT_BLOCK:]
            dsink_ref[...] += dsink

        @pl.when(n == nb)
        def _():
            dkv_ref[...] = carry[...].astype(BF16)
            dbkv_ref[...] += _colsum(carry[...])

    def clamp(n):
        return jnp.minimum(n, nb - 1)

    outs, moved = _call(
        body, name=name, grid=(nb + 1,),
        in_specs=[pl.BlockSpec(memory_space=pltpu.SMEM),
                  pl.BlockSpec((ATT_BLOCK, Q_WIDTH), lambda n: (clamp(n), 0)),
                  pl.BlockSpec((ATT_BLOCK, KV_WIDTH), lambda n: (clamp(n), kvb)),
                  pl.BlockSpec((ATT_BLOCK, KV_WIDTH), lambda n: (jnp.maximum(clamp(n) - 1, 0), kvb)),
                  pl.BlockSpec((ATT_BLOCK, Q_WIDTH), lambda n: (clamp(n), 0))],
        out_specs=[pl.BlockSpec((ATT_BLOCK, Q_WIDTH), lambda n: (clamp(n), 0)),
                   pl.BlockSpec((ATT_BLOCK, KV_WIDTH), lambda n: (jnp.maximum(n - 1, 0), 0)),
                   pl.BlockSpec((1, Q_WIDTH), lambda n: (0, 0)),
                   pl.BlockSpec((1, KV_WIDTH), lambda n: (0, 0)),
                   pl.BlockSpec((1, 128), lambda n: (0, 0))],
        out_shape=[jax.ShapeDtypeStruct((t, Q_WIDTH), BF16), jax.ShapeDtypeStruct((t, KV_WIDTH), BF16),
                   jax.ShapeDtypeStruct((1, Q_WIDTH), F32), jax.ShapeDtypeStruct((1, KV_WIDTH), F32),
                   jax.ShapeDtypeStruct((1, 128), F32)],
        scratch_shapes=[pltpu.VMEM((ATT_BLOCK, KV_WIDTH), F32)],
        sem=("arbitrary",), args=(sinks, qkv, qkv, qkv, d_o), comm=comm)
    return outs if comm is None else (outs, moved)


def _adamw_math(g, w, m, v):
    m = ADAM_B1 * m + (1.0 - ADAM_B1) * g
    v = ADAM_B2 * v + (1.0 - ADAM_B2) * (g * g)
    m_hat = m / (1.0 - ADAM_B1 ** ADAM_STEP)
    v_hat = v / (1.0 - ADAM_B2 ** ADAM_STEP)
    delta = -ADAM_LR * (m_hat / (jnp.sqrt(v_hat) + ADAM_EPS) + ADAM_WD * w)
    return delta, m, v


def _sum_partials(p_ref):
    g = p_ref[0].astype(F32)
    for s in range(1, N_DEV):
        g = g + p_ref[s].astype(F32)
    return g


def _adamw_big(parts, w, m, v, name, tr):
    r, c = w.shape
    tiles = [p.shape[1] // tr for p in parts]
    starts = [sum(tiles[:l]) for l in range(len(parts))]
    assert all(p.shape[1] % tr == 0 for p in parts) and sum(tiles) * tr == r

    def body(*refs):
        p_refs, (w_ref, m_ref, v_ref, g_out, d_out, m_out, v_out) = refs[:len(parts)], refs[len(parts):]
        i = pl.program_id(0)
        for l, p_ref in enumerate(p_refs):
            @pl.when((i >= starts[l]) & (i < starts[l] + tiles[l]))
            def _():
                g = _sum_partials(p_ref)
                g_out[...] = g
                d_out[...], m_out[...], v_out[...] = _adamw_math(g, w_ref[...], m_ref[...], v_ref[...])

    def part_spec(l):
        return pl.BlockSpec((N_DEV, tr, c), lambda i: (0, jnp.clip(i - starts[l], 0, tiles[l] - 1), 0))

    tile = pl.BlockSpec((tr, c), lambda i: (i, 0))
    shape = jax.ShapeDtypeStruct((r, c), F32)
    return pl.pallas_call(
        body, name=name, grid=(r // tr,),
        in_specs=[part_spec(l) for l in range(len(parts))] + [tile, tile, tile],
        out_specs=[tile] * 4, out_shape=[shape] * 4,
        compiler_params=_params("parallel"),
    )(*parts, w, m, v)


def _adamw_small(parts, ws, ms, vs, name):
    n = len(ws)

    def body(*refs):
        ins, outs = refs[:4 * n], refs[4 * n:]
        for a in range(n):
            g = _sum_partials(ins[a])
            outs[4 * a][...] = g
            outs[4 * a + 1][...], outs[4 * a + 2][...], outs[4 * a + 3][...] = _adamw_math(
                g, ins[n + a][...], ins[2 * n + a][...], ins[3 * n + a][...])

    out_shape = []
    for w in ws:
        out_shape += [jax.ShapeDtypeStruct(w.shape, F32)] * 4
    return pl.pallas_call(body, name=name, out_shape=out_shape, compiler_params=_params())(*parts, *ws, *ms, *vs)


PACK_LANES = 128
PACK_ROWS = 8


def _pack(arrs):
    flat = jnp.concatenate([a.reshape(-1).astype(F32) for a in arrs])
    unit = PACK_LANES * PACK_ROWS
    total = -(-flat.shape[0] // unit) * unit
    return jnp.pad(flat, (0, total - flat.shape[0])).reshape(-1, PACK_LANES)


def _unpack(buf, shapes):
    flat = buf.reshape(N_DEV, -1)
    out, pos = [], 0
    for s in shapes:
        size = math.prod(s)
        out.append(flat[:, pos:pos + size].reshape((N_DEV,) + tuple(s)))
        pos += size
    return out


def _interleave(g):
    return jnp.transpose(g, (1, 0, 2)).reshape(g.shape[1], -1)


def _deinterleave(w):
    r = w.shape[0]
    return jnp.transpose(w.reshape(r, N_DEV, -1), (1, 0, 2))


def _ffn_backward(dz, x_in, h, u, w_up_t, cw, cb, w_down, tag, comm=None):
    du = _matmul(dz, w_down, "nt", BF16, f"ffn{tag}_du", 1024, 1408, 1024)
    d_w_down = _matmul(u, dz, "tn", BF16, f"ffn{tag}_dwdown", 1408, 1024, 512)
    outs = _ffn_mid_bwd(h, du, cw, cb, f"ffn{tag}_mid_bwd", comm=comm)
    (dhg, dhv, dcwg, dcwv, dcbg, dcbv), moved = outs if comm is not None else (outs, [])
    d_w_up_t = jnp.concatenate([_matmul(dhg, x_in, "tn", BF16, f"ffn{tag}_dwup_gate", 1408, 1024, 1024),
                                _matmul(dhv, x_in, "tn", BF16, f"ffn{tag}_dwup_value", 1408, 1024, 1024)], axis=0)
    dx = _matmul(dhv, w_up_t, "nn", F32, f"ffn{tag}_dx_value", 1024, 1024, 1408, b_off=D_FF // 1408)
    dx = _matmul(dhg, w_up_t, "nn", F32, f"ffn{tag}_dx_gate", 1024, 1024, 1408, res=dx)
    return (dx, d_w_up_t.reshape(N_DEV, -1, D_MODEL), d_w_down.reshape(N_DEV, -1, D_MODEL),
            jnp.concatenate([dcwg, dcwv], axis=1), jnp.concatenate([dcbg, dcbv], axis=1), moved)


def kernel(x, ab_w_in, a_conv_w, a_conv_b, a_norm_g, a_norm_b, b_norm_g, b_norm_b, b_spatial_w, b_spatial_b, ab_w_out, c_w_qkv, c_b_qkv, c_sinks, c_w_o, ffn_w_up, ffn_conv_w, ffn_conv_b, ffn_w_down, ln_g, ln_b, loss_target, m_ab_w_in, m_a_conv_w, m_a_conv_b, m_a_norm_g, m_a_norm_b, m_b_norm_g, m_b_norm_b, m_b_spatial_w, m_b_spatial_b, m_ab_w_out, m_c_w_qkv, m_c_b_qkv, m_c_sinks, m_c_w_o, m_ffn_w_up, m_ffn_conv_w, m_ffn_conv_b, m_ffn_w_down, m_ln_g, m_ln_b, v_ab_w_in, v_a_conv_w, v_a_conv_b, v_a_norm_g, v_a_norm_b, v_b_norm_g, v_b_norm_b, v_b_spatial_w, v_b_spatial_b, v_ab_w_out, v_c_w_qkv, v_c_b_qkv, v_c_sinks, v_c_w_o, v_ffn_w_up, v_ffn_conv_w, v_ffn_conv_b, v_ffn_w_down, v_ln_g, v_ln_b):
    me = 4 * lax.axis_index("x") + 2 * lax.axis_index("y") + lax.axis_index("c")
    xt = x[0]
    t = xt.shape[0]

    small_shard_shapes = [a_conv_w.shape, c_b_qkv.shape, ffn_conv_w.shape, ln_g.shape, ln_b.shape]
    up_shard = [jnp.swapaxes(ffn_w_up[l], 0, 1).astype(BF16) for l in range(2)]
    qkv_shard = jnp.swapaxes(c_w_qkv[0], 0, 1).astype(BF16)
    down_shard = [ffn_w_down[l].astype(BF16) for l in range(2)]
    g_win, g_small = _comm_only(
        _Comm(gather=[ab_w_in[0].astype(BF16), _pack([a_conv_w, c_b_qkv, ffn_conv_w, ln_g, ln_b])]), "gather_first")
    w_in = _interleave(g_win)
    g_acw, g_bqkv, g_fcw, g_lng, g_lnb = _unpack(g_small, small_shard_shapes)
    acw = _interleave(g_acw[:, 0])
    bqkv = g_bqkv[:, 0].reshape(1, -1)
    fcw = [_interleave(g_fcw[:, l]) for l in range(2)]
    lng = jnp.transpose(g_lng, (1, 2, 0, 3)).reshape(2, 2, 1, D_MODEL)
    lnb = jnp.transpose(g_lnb, (1, 2, 0, 3)).reshape(2, 2, 1, D_MODEL)
    fcb = [ffn_conv_b[l:l + 1] for l in range(2)]
    ms = b_spatial_w[0]
    mst = jnp.swapaxes(ms, 1, 2)
    sbt = b_spatial_b[0].T

    h0, (g_wout,) = _matmul(xt, w_in, "nn", BF16, "mix_in", 1024, 1024, 1024, comm=_Comm(gather=[ab_w_out[0].astype(BF16)]))
    w_out = g_wout.reshape(D_MODEL, D_MODEL)
    cat, (g_wup0,) = _mixer_mid_fwd(h0, acw, a_conv_b, a_norm_g, a_norm_b, b_norm_g, b_norm_b, ms, sbt, "mix_mid_fwd",
                                    comm=_Comm(gather=[up_shard[0]]))
    w_up0 = g_wup0.reshape(2 * D_FF, D_MODEL)
    z1, x1 = _matmul_res_ln(cat, w_out, xt, lng[0, 0], lnb[0, 0], "mix_out_ln", 512, D_MODEL)
    hf0, (g_wdown0, g_wqkv, g_wo) = _matmul(x1, w_up0, "nt", BF16, "ffn0_up", 1024, 1408, 1024,
                                            comm=_Comm(gather=[down_shard[0], qkv_shard, c_w_o[0].astype(BF16)]))
    w_down0 = g_wdown0.reshape(D_FF, D_MODEL)
    w_qkv = g_wqkv.reshape(Q_WIDTH + KV_WIDTH, D_MODEL)
    w_o = g_wo.reshape(D_MODEL, D_MODEL)
    u0, (g_wup1,) = _ffn_mid_fwd(hf0, fcw[0], fcb[0], "ffn0_mid_fwd", comm=_Comm(gather=[up_shard[1]]))
    w_up1 = g_wup1.reshape(2 * D_FF, D_MODEL)
    (z2, x2), (g_wdown1,) = _matmul_res_ln(u0, w_down0, x1, lng[0, 1], lnb[0, 1], "ffn0_down_ln", 512, D_FF,
                                           comm=_Comm(gather=[down_shard[1]]))
    w_down1 = g_wdown1.reshape(D_FF, D_MODEL)
    qkv = _matmul(x2, w_qkv, "nt", BF16, "att_qkv", 1024, 1280, 1024, bias=bqkv)
    att = _attn_fwd(qkv, c_sinks, "att_fwd")
    z3, x3 = _matmul_res_ln(att, w_o, x2, lng[1, 0], lnb[1, 0], "att_out_ln", 512, D_MODEL)
    hf1 = _matmul(x3, w_up1, "nt", BF16, "ffn1_up", 1024, 1408, 1024)
    u1 = _ffn_mid_fwd(hf1, fcw[1], fcb[1], "ffn1_mid_fwd")
    z4, _ = _matmul_res_ln(u1, w_down1, x3, lng[1, 1], lnb[1, 1], "ffn1_down_ln", 512, D_FF)

    dz4, dg11, db11, loss_terms = _ln_bwd_loss(z4, lng[1, 1], lnb[1, 1], loss_target[0], "loss_ln_bwd")
    dx3, d_wup1, d_wdown1, d_fcw1, d_fcb1, _ = _ffn_backward(dz4, x3, hf1, u1, w_up1, fcw[1], fcb[1], w_down1, 1)
    dz3, dg10, db10 = _ln_bwd(z3, lng[1, 0], dz4, dx3, "ln10_bwd")
    d_att = _matmul(dz3, w_o, "nt", BF16, "att_dout", 1024, 1024, 1024)
    d_wo = _matmul(att, dz3, "tn", BF16, "att_dwo", 1024, 1024, 512)
    (dq, dkv, dbq, dbkv, dsinks), (p_wup1, p_wdown1) = _attn_bwd(qkv, d_att, c_sinks, "att_bwd",
                                                                  comm=_Comm(exchange=[d_wup1, d_wdown1]))
    d_wqkv = jnp.concatenate([_matmul(dq, x2, "tn", BF16, "att_dwq", 1024, 1024, 1024),
                              _matmul(dkv, x2, "tn", BF16, "att_dwkv", KV_WIDTH, 1024, 1024)], axis=0)
    dx2 = _matmul(dkv, w_qkv, "nn", F32, "att_dx_kv", 1024, 1024, KV_WIDTH, b_off=Q_WIDTH // KV_WIDTH)
    dx2 = _matmul(dq, w_qkv, "nn", F32, "att_dx_q", 1024, 1024, KV_WIDTH, res=dx2)
    dz2, dg01, db01 = _ln_bwd(z2, lng[0, 1], dz3, dx2, "ln01_bwd")
    dx1, d_wup0, d_wdown0, d_fcw0, d_fcb0, (p_wqkv, p_wo) = _ffn_backward(
        dz2, x1, hf0, u0, w_up0, fcw[0], fcb[0], w_down0, 0,
        comm=_Comm(exchange=[d_wqkv.reshape(N_DEV, -1, D_MODEL), d_wo.reshape(N_DEV, -1, D_MODEL)]))
    dz1, dg00, db00 = _ln_bwd(z1, lng[0, 0], dz2, dx1, "ln00_bwd")
    dcat = _matmul(dz1, w_out, "nt", BF16, "mix_dcat", 1024, 1024, 1024)
    d_wout = _matmul(cat, dz1, "tn", BF16, "mix_dwout", 1024, 1024, 512)
    (dh0, d_acw, d_acb, d_ang, d_anb, d_bng, d_bnb, d_ms, d_sb), (p_wup0, p_wdown0, p_wout) = _mixer_mid_bwd(
        h0, dcat, acw, a_conv_b, a_norm_g, a_norm_b, b_norm_g, b_norm_b, ms, mst, sbt, "mix_mid_bwd",
        comm=_Comm(exchange=[d_wup0, d_wdown0, d_wout.reshape(N_DEV, -1, D_MODEL)]))
    d_bqkv = jnp.concatenate([dbq, dbkv], axis=1)
    d_lng = jnp.stack([jnp.stack([dg00, dg01]), jnp.stack([dg10, dg11])])
    d_lnb = jnp.stack([jnp.stack([db00, db01]), jnp.stack([db10, db11])])
    small_full = [d_acb, d_ang, d_anb, d_bng, d_bnb, d_ms, d_sb, dsinks[:, :N_Q_HEADS], jnp.concatenate([d_fcb0, d_fcb1], axis=0),
                  d_acw, d_bqkv, jnp.stack([d_fcw0, d_fcw1]), d_lng, d_lnb]
    d_win, (g_small_grads,) = _matmul(xt, dh0, "tn", BF16, "mix_dwin", 1024, 1024, 512, comm=_Comm(gather=[_pack(small_full)]))
    grad_x, (p_win,) = _matmul(dh0, w_in, "nt", F32, "mix_dx", 1024, 1024, 1024, res=dz1, res_scale=ALPHA,
                               comm=_Comm(exchange=[_deinterleave(d_win)]))

    loss = lax.psum(0.5 / D_MODEL * jnp.sum(loss_terms), ("x", "y", "c"))

    big = {}
    for nm, p, w, m, v, tr, transposed in [
            ("ab_w_in", [p_win], ab_w_in, m_ab_w_in, v_ab_w_in, 256, False),
            ("ab_w_out", [p_wout], ab_w_out, m_ab_w_out, v_ab_w_out, 128, False),
            ("c_w_qkv", [p_wqkv], c_w_qkv, m_c_w_qkv, v_c_w_qkv, 160, True), ("c_w_o", [p_wo], c_w_o, m_c_w_o, v_c_w_o, 128, False),
            ("ffn_w_up", [p_wup0, p_wup1], ffn_w_up, m_ffn_w_up, v_ffn_w_up, 176, True),
            ("ffn_w_down", [p_wdown0, p_wdown1], ffn_w_down, m_ffn_w_down, v_ffn_w_down, 176, False)]:
        def two_d(a):
            a = jnp.swapaxes(a, 1, 2) if transposed else a
            return a.reshape(-1, a.shape[-1])

        def back(o):
            return jnp.swapaxes(o.reshape(w.shape[0], w.shape[2], w.shape[1]), 1, 2) if transposed else o.reshape(w.shape)

        outs = _adamw_big(p, two_d(w), two_d(m), two_d(v), "adamw_" + nm, tr)
        big[nm] = [back(o) for o in outs]

    gs = _unpack(g_small_grads, [a.shape for a in small_full])

    def my_shard(g, width):
        g = g.reshape(g.shape[:-1] + (N_DEV, width))
        return lax.dynamic_index_in_dim(g, me, axis=g.ndim - 2, keepdims=False)

    small_names = ["a_conv_b", "a_norm_g", "a_norm_b", "b_norm_g", "b_norm_b", "b_spatial_w", "b_spatial_b", "c_sinks", "ffn_conv_b",
                   "a_conv_w", "c_b_qkv", "ffn_conv_w", "ln_g", "ln_b"]
    small_w = [a_conv_b, a_norm_g, a_norm_b, b_norm_g, b_norm_b, b_spatial_w, b_spatial_b, c_sinks, ffn_conv_b,
               a_conv_w, c_b_qkv, ffn_conv_w, ln_g, ln_b]
    small_m = [m_a_conv_b, m_a_norm_g, m_a_norm_b, m_b_norm_g, m_b_norm_b, m_b_spatial_w, m_b_spatial_b, m_c_sinks, m_ffn_conv_b,
               m_a_conv_w, m_c_b_qkv, m_ffn_conv_w, m_ln_g, m_ln_b]
    small_v = [v_a_conv_b, v_a_norm_g, v_a_norm_b, v_b_norm_g, v_b_norm_b, v_b_spatial_w, v_b_spatial_b, v_c_sinks, v_ffn_conv_b,
               v_a_conv_w, v_c_b_qkv, v_ffn_conv_w, v_ln_g, v_ln_b]
    gs[9:] = [my_shard(g, w.shape[-1]) for g, w in zip(gs[9:], small_w[9:])]
    two_d = [(-1, w.shape[-1]) for w in small_w]
    outs = _adamw_small([g.reshape((N_DEV,) + w.reshape(s).shape) for g, w, s in zip(gs, small_w, two_d)],
                        [w.reshape(s) for w, s in zip(small_w, two_d)], [m.reshape(s) for m, s in zip(small_m, two_d)],
                        [v.reshape(s) for v, s in zip(small_v, two_d)], "adamw_small")
    small = {nm: [o.reshape(w.shape) for o in outs[4 * a:4 * a + 4]] for a, (nm, w) in enumerate(zip(small_names, small_w))}

    res = {**big, **small}
    order = ["ab_w_in", "a_conv_w", "a_conv_b", "a_norm_g", "a_norm_b", "b_norm_g", "b_norm_b", "b_spatial_w", "b_spatial_b", "ab_w_out",
             "c_w_qkv", "c_b_qkv", "c_sinks", "c_w_o", "ffn_w_up", "ffn_conv_w", "ffn_conv_b", "ffn_w_down", "ln_g", "ln_b"]
    return (loss, grad_x[None], *[res[nm][0] for nm in order], *[res[nm][1] for nm in order],
            *[res[nm][2] for nm in order], *[res[nm][3] for nm in order])
```

```python
import functools
import math

import jax
import jax.numpy as jnp
from jax import lax
from jax.experimental import pallas as pl
from jax.experimental.pallas import tpu as pltpu

F32 = jnp.float32
BF16 = jnp.bfloat16

N_DEV = 8
D_MODEL = 1024
A_WIDTH = 512
A_KERNEL = 31
B_GROUPS = 4
B_CHUNK = 128
HEAD_DIM = 64
N_Q_HEADS = 16
N_KV_HEADS = 2
ATT_BLOCK = 128
D_FF = 2816
FFN_KERNEL = 3
ALPHA = (2.0 * 2) ** 0.25
LN_EPS = 1e-5
GELU_K = math.sqrt(2.0 / math.pi)
GELU_C = 0.044715
ADAM_LR = 0.001
ADAM_B1 = 0.9
ADAM_B2 = 0.999
ADAM_EPS = 1e-08
ADAM_WD = 0.01
ADAM_STEP = 10
VMEM_LIMIT = 56 * 1024 * 1024
MESH_ID = pl.DeviceIdType.MESH


def _params(*sem):
    return pltpu.CompilerParams(dimension_semantics=sem, vmem_limit_bytes=VMEM_LIMIT)


def _gelu(x):
    t = jnp.tanh(GELU_K * x * (1.0 + GELU_C * x * x))
    return 0.5 * x * (1.0 + t)


def _gelu_and_grad(x):
    x2 = x * x
    t = jnp.tanh(GELU_K * x * (1.0 + GELU_C * x2))
    g = 0.5 * x * (1.0 + t)
    dg = 0.5 * (1.0 + t) + 0.5 * x * (1.0 - t * t) * (GELU_K * (1.0 + 3.0 * GELU_C * x2))
    return g, dg


def _sigmoid(x):
    return 1.0 / (1.0 + jnp.exp(-x))


def _ln_stats(z):
    mu = jnp.mean(z, axis=-1, keepdims=True)
    zc = z - mu
    var = jnp.mean(zc * zc, axis=-1, keepdims=True)
    r = lax.rsqrt(var + LN_EPS)
    return zc * r, r


def _ln_bwd_rows(dn, nh, r):
    return r * (dn - jnp.mean(dn, axis=-1, keepdims=True) - nh * jnp.mean(dn * nh, axis=-1, keepdims=True))


def _colsum(x):
    return jnp.sum(x, axis=0, keepdims=True)


def _dot(a, b, dims):
    return lax.dot_general(a.astype(BF16), b.astype(BF16), (dims, ((), ())), preferred_element_type=F32)


NN = ((1,), (0,))
NT = ((1,), (1,))
TN = ((0,), (0,))


ANY = pl.BlockSpec(memory_space=pl.ANY)
N_RELATIONS = N_DEV - 1


def _my_place():
    return lax.axis_index("x"), lax.axis_index("y"), lax.axis_index("c")


class _Comm:
    def __init__(self, gather=(), exchange=()):
        self.arrs = list(gather) + list(exchange)
        self.n_gather = len(gather)
        self.n = len(self.arrs)

    def out_shape(self):
        return [jax.ShapeDtypeStruct(((N_DEV,) + a.shape) if i < self.n_gather else a.shape, a.dtype)
                for i, a in enumerate(self.arrs)]

    def sems(self):
        return [pltpu.SemaphoreType.DMA((self.n, N_RELATIONS)), pltpu.SemaphoreType.DMA((self.n, N_RELATIONS)),
                pltpu.SemaphoreType.DMA((self.n,))]

    def _gather_copy(self, ins, outs, sems, a, k, place, to, from_input=False):
        px, py, pc = place
        block = outs[a].at[4 * px + 2 * py + pc]
        return pltpu.make_async_remote_copy(
            src_ref=ins[a] if from_input else block, dst_ref=block,
            send_sem=sems[0].at[a, k], recv_sem=sems[1].at[a, k], device_id=to, device_id_type=MESH_ID)

    def _exchange_copy(self, ins, outs, sems, a, k, landing=False):
        x, y, c = _my_place()
        me = 4 * x + 2 * y + c
        peer = (x ^ (k >> 2), y ^ ((k >> 1) & 1), c ^ (k & 1))
        return pltpu.make_async_remote_copy(
            src_ref=ins[a].at[me ^ k], dst_ref=outs[a].at[(me ^ k) if landing else me],
            send_sem=sems[0].at[a, k - 1], recv_sem=sems[1].at[a, k - 1], device_id=peer, device_id_type=MESH_ID)

    def _local_copy(self, ins, outs, sems, a):
        x, y, c = _my_place()
        me = 4 * x + 2 * y + c
        src = ins[a] if a < self.n_gather else ins[a].at[me]
        return pltpu.make_async_copy(src, outs[a].at[me], sems[2].at[a])

    def _first_stage(self, ins, outs, sems, a):
        x, y, c = _my_place()
        me = (x, y, c)
        chips = [(1 - x, y), (x, 1 - y), (1 - x, 1 - y)]
        return ([self._gather_copy(ins, outs, sems, a, 0, me, (x, y, 1 - c), from_input=True)]
                + [self._gather_copy(ins, outs, sems, a, 1 + j, me, (*chip, c), from_input=True) for j, chip in enumerate(chips)])

    def start(self, ins, outs, sems):
        for a in range(self.n):
            self._local_copy(ins, outs, sems, a).start()
        for a in range(self.n_gather):
            for cp in self._first_stage(ins, outs, sems, a):
                cp.start()
        for k in range(1, N_DEV):
            for a in range(self.n_gather, self.n):
                self._exchange_copy(ins, outs, sems, a, k).start()

    def finish(self, ins, outs, sems):
        x, y, c = _my_place()
        me, sibling = (x, y, c), (x, y, 1 - c)
        chips = [(1 - x, y), (x, 1 - y), (1 - x, 1 - y)]
        passed = []
        for j, chip in enumerate(chips):
            for a in range(self.n_gather):
                self._gather_copy(ins, outs, sems, a, 1 + j, (*chip, c), me).wait_recv()
                fwd = self._gather_copy(ins, outs, sems, a, 4 + j, (*chip, c), sibling)
                fwd.start()
                passed.append(fwd)
        for a in range(self.n_gather):
            self._gather_copy(ins, outs, sems, a, 0, sibling, me).wait_recv()
            for j, chip in enumerate(chips):
                self._gather_copy(ins, outs, sems, a, 4 + j, (*chip, 1 - c), me).wait_recv()
        for k in range(1, N_DEV):
            for a in range(self.n_gather, self.n):
                self._exchange_copy(ins, outs, sems, a, k, landing=True).wait_recv()
        for a in range(self.n_gather):
            for cp in self._first_stage(ins, outs, sems, a):
                cp.wait_send()
        for cp in passed:
            cp.wait_send()
        for k in range(1, N_DEV):
            for a in range(self.n_gather, self.n):
                self._exchange_copy(ins, outs, sems, a, k).wait_send()
        for a in range(self.n):
            self._local_copy(ins, outs, sems, a).wait()


def _comm_only(comm, name):
    def body(*refs):
        ins, outs, sems = refs[:comm.n], refs[comm.n:2 * comm.n], refs[2 * comm.n:]
        comm.start(ins, outs, sems)
        comm.finish(ins, outs, sems)

    return pl.pallas_call(body, name=name, in_specs=[ANY] * comm.n, out_specs=[ANY] * comm.n,
                          out_shape=comm.out_shape(), scratch_shapes=comm.sems())(*comm.arrs)


def _call(body, *, name, grid, in_specs, out_specs, out_shape, args, sem, scratch_shapes=(), comm=None):
    in_specs, out_specs, out_shape, scratch_shapes = list(in_specs), list(out_specs), list(out_shape), list(scratch_shapes)
    if comm is None:
        outs = pl.pallas_call(body, name=name, grid=grid, in_specs=in_specs, out_specs=out_specs, out_shape=out_shape,
                              scratch_shapes=scratch_shapes, compiler_params=_params(*sem))(*args)
        return list(outs), []
    n_in, n_out, n_scr, nc = len(in_specs), len(out_specs), len(scratch_shapes), comm.n

    def wrapped(*refs):
        ins, refs = refs[:n_in], refs[n_in:]
        c_in, refs = refs[:nc], refs[nc:]
        outs, refs = refs[:n_out], refs[n_out:]
        c_out, refs = refs[:nc], refs[nc:]
        scr, sems = refs[:n_scr], refs[n_scr:]
        first = functools.reduce(jnp.logical_and, [pl.program_id(ax) == 0 for ax in range(len(grid))])
        last = functools.reduce(jnp.logical_and, [pl.program_id(ax) == g - 1 for ax, g in enumerate(grid)])

        @pl.when(first)
        def _():
            comm.start(c_in, c_out, sems)

        body(*ins, *outs, *scr)

        @pl.when(last)
        def _():
            comm.finish(c_in, c_out, sems)

    outs = pl.pallas_call(
        wrapped, name=name, grid=grid, in_specs=in_specs + [ANY] * nc, out_specs=out_specs + [ANY] * nc,
        out_shape=out_shape + comm.out_shape(), scratch_shapes=scratch_shapes + comm.sems(),
        compiler_params=_params(*(["arbitrary"] * len(grid))))(*args, *comm.arrs)
    return list(outs[:n_out]), list(outs[n_out:])


def _matmul(a, b, mode, out_dtype, name, tm, tn, tk, *, bias=None, res=None, res_scale=1.0, b_off=0, comm=None):
    tm = min(tm, a.shape[1] if mode == "tn" else a.shape[0])
    tk = min(tk, a.shape[0] if mode == "tn" else a.shape[1])
    if mode == "nn":
        (m, k), n = a.shape, b.shape[1]
        a_spec = pl.BlockSpec((tm, tk), lambda i, j, kk: (i, kk))
        b_spec = pl.BlockSpec((tk, tn), lambda i, j, kk: (kk + b_off, j))
        dims = NN
    elif mode == "nt":
        (m, k), n = a.shape, b.shape[0]
        a_spec = pl.BlockSpec((tm, tk), lambda i, j, kk: (i, kk))
        b_spec = pl.BlockSpec((tn, tk), lambda i, j, kk: (j, kk + b_off))
        dims = NT
    else:
        (k, m), n = a.shape, b.shape[1]
        a_spec = pl.BlockSpec((tk, tm), lambda i, j, kk: (kk, i))
        b_spec = pl.BlockSpec((tk, tn), lambda i, j, kk: (kk, j))
        dims = TN
    assert m % tm == 0 and n % tn == 0 and k % tk == 0, (name, m, n, k)
    nk = k // tk
    in_specs = [a_spec, b_spec]
    args = [a, b]
    if bias is not None:
        in_specs.append(pl.BlockSpec((1, tn), lambda i, j, kk: (0, j)))
        args.append(bias)
    if res is not None:
        in_specs.append(pl.BlockSpec((tm, tn), lambda i, j, kk: (i, j)))
        args.append(res)

    def finish(out, refs, o_ref):
        pos = 2
        if bias is not None:
            out = out + refs[pos][...]
            pos += 1
        if res is not None:
            out = out + res_scale * refs[pos][...].astype(F32)
        o_ref[...] = out.astype(out_dtype)

    def body_one_step(*refs):
        finish(_dot(refs[0][...], refs[1][...], dims), refs, refs[-1])

    def body(*refs):
        a_ref, b_ref = refs[0], refs[1]
        o_ref, acc = refs[-2], refs[-1]
        kk = pl.program_id(2)

        @pl.when(kk == 0)
        def _():
            acc[...] = jnp.zeros_like(acc)

        acc[...] += _dot(a_ref[...], b_ref[...], dims)

        @pl.when(kk == nk - 1)
        def _():
            finish(acc[...], refs, o_ref)

    (out,), moved = _call(
        body_one_step if nk == 1 else body, name=name, grid=(m // tm, n // tn, nk),
        in_specs=in_specs, out_specs=[pl.BlockSpec((tm, tn), lambda i, j, kk: (i, j))],
        out_shape=[jax.ShapeDtypeStruct((m, n), out_dtype)],
        scratch_shapes=[] if nk == 1 else [pltpu.VMEM((tm, tn), F32)],
        sem=("parallel", "parallel", "arbitrary"), args=args, comm=comm)
    return out if comm is None else (out, moved)


def _matmul_res_ln(a, b, xres, g, beta, name, tm, tk, comm=None):
    t, k = a.shape
    d = b.shape[1]
    nk = k // tk
    assert t % tm == 0 and k % tk == 0

    def body(a_ref, b_ref, x_ref, g_ref, beta_ref, z_ref, xo_ref, acc):
        kk = pl.program_id(1)

        @pl.when(kk == 0)
        def _():
            acc[...] = jnp.zeros_like(acc)

        acc[...] += _dot(a_ref[...], b_ref[...], NN)

        @pl.when(kk == nk - 1)
        def _():
            z = ALPHA * x_ref[...] + acc[...]
            nh, _ = _ln_stats(z)
            z_ref[...] = z
            xo_ref[...] = nh * g_ref[...] + beta_ref[...]

    row = pl.BlockSpec((tm, d), lambda i, kk: (i, 0))
    vec = pl.BlockSpec((1, d), lambda i, kk: (0, 0))
    outs, moved = _call(
        body, name=name, grid=(t // tm, nk),
        in_specs=[pl.BlockSpec((tm, tk), lambda i, kk: (i, kk)), pl.BlockSpec((tk, d), lambda i, kk: (kk, 0)), row, vec, vec],
        out_specs=[row, row],
        out_shape=[jax.ShapeDtypeStruct((t, d), F32), jax.ShapeDtypeStruct((t, d), F32)],
        scratch_shapes=[pltpu.VMEM((tm, d), F32)],
        sem=("parallel", "arbitrary"), args=(a, b, xres, g, beta), comm=comm)
    return outs if comm is None else (outs, moved)


def _ln_bwd(z, g, dres, dbr, name, tm=512):
    t, d = z.shape

    def body(z_ref, g_ref, dres_ref, dbr_ref, dz_ref, dg_ref, db_ref):
        @pl.when(pl.program_id(0) == 0)
        def _():
            dg_ref[...] = jnp.zeros_like(dg_ref)
            db_ref[...] = jnp.zeros_like(db_ref)

        nh, r = _ln_stats(z_ref[...])
        dy = ALPHA * dres_ref[...] + dbr_ref[...].astype(F32)
        dg_ref[...] += _colsum(dy * nh)
        db_ref[...] += _colsum(dy)
        dz_ref[...] = _ln_bwd_rows(dy * g_ref[...], nh, r)

    row = pl.BlockSpec((tm, d), lambda i: (i, 0))
    vec = pl.BlockSpec((1, d), lambda i: (0, 0))
    return pl.pallas_call(
        body, name=name, grid=(t // tm,), in_specs=[row, vec, row, row], out_specs=[row, vec, vec],
        out_shape=[jax.ShapeDtypeStruct((t, d), F32), jax.ShapeDtypeStruct((1, d), F32), jax.ShapeDtypeStruct((1, d), F32)],
        compiler_params=_params("arbitrary"),
    )(z, g, dres, dbr)


def _ln_bwd_loss(z, g, beta, target, name, tm=512):
    t, d = z.shape

    def body(z_ref, g_ref, beta_ref, t_ref, dz_ref, dg_ref, db_ref, loss_ref):
        @pl.when(pl.program_id(0) == 0)
        def _():
            dg_ref[...] = jnp.zeros_like(dg_ref)
            db_ref[...] = jnp.zeros_like(db_ref)
            loss_ref[...] = jnp.zeros_like(loss_ref)

        nh, r = _ln_stats(z_ref[...])
        err = nh * g_ref[...] + beta_ref[...] - t_ref[...]
        loss_ref[...] += _colsum(err * err)
        dy = err * (1.0 / d)
        dg_ref[...] += _colsum(dy * nh)
        db_ref[...] += _colsum(dy)
        dz_ref[...] = _ln_bwd_rows(dy * g_ref[...], nh, r)

    row = pl.BlockSpec((tm, d), lambda i: (i, 0))
    vec = pl.BlockSpec((1, d), lambda i: (0, 0))
    vshape = jax.ShapeDtypeStruct((1, d), F32)
    return pl.pallas_call(
        body, name=name, grid=(t // tm,), in_specs=[row, vec, vec, row], out_specs=[row, vec, vec, vec],
        out_shape=[jax.ShapeDtypeStruct((t, d), F32), vshape, vshape, vshape],
        compiler_params=_params("arbitrary"),
    )(z, g, beta, target)


FFN_HALO = 16
FFN_CHUNK = 256
LANES = 128
SUBLANES = 8


def _fold(x):
    return jnp.sum(x.reshape(x.shape[0] // SUBLANES, SUBLANES, x.shape[1]), axis=0)


def _ffn_mid_fwd(h, cw, cb, name, tm=1024, tc=256, comm=None):
    t, f2 = h.shape
    tm = min(tm, t)
    f = f2 // 2
    nj, nt, hb = f // tc, t // tm, tm // FFN_HALO

    ch = min(FFN_CHUNK, tm)

    def body(hg, hgp, hv, hvp, cwg, cwv, cbg, cbv, u_ref, sg, sv):
        i = pl.program_id(1)
        for main, prev, s in ((hg, hgp, sg), (hv, hvp, sv)):
            s[0:FFN_HALO, :] = jnp.where(i > 0, prev[...].astype(F32), 0.0)
            s[FFN_HALO:, :] = main[...].astype(F32)
        o = SUBLANES - FFN_KERNEL + 1
        for lg in range(tc // LANES):
            cols = slice(lg * LANES, (lg + 1) * LANES)
            wg, wv = [cwg[k:k + 1, cols] for k in range(FFN_KERNEL)], [cwv[k:k + 1, cols] for k in range(FFN_KERNEL)]
            bg, bv = cbg[:, cols], cbv[:, cols]

            def chunk(c, carry):
                base = pl.multiple_of(c * ch, ch)
                eg = sg[pl.ds(base + FFN_HALO - SUBLANES, ch + SUBLANES), cols]
                ev = sv[pl.ds(base + FFN_HALO - SUBLANES, ch + SUBLANES), cols]
                cg = wg[0] * eg[o:o + ch] + wg[1] * eg[o + 1:o + 1 + ch] + wg[2] * eg[o + 2:o + 2 + ch] + bg
                cv = wv[0] * ev[o:o + ch] + wv[1] * ev[o + 1:o + 1 + ch] + wv[2] * ev[o + 2:o + 2 + ch] + bv
                u_ref[pl.ds(base, ch), cols] = (_gelu(cg) * cv).astype(BF16)
                return carry

            lax.fori_loop(0, tm // ch, chunk, 0)

    def main_spec(off):
        return pl.BlockSpec((tm, tc), lambda j, i: (i, j + off))

    def prev_spec(off):
        return pl.BlockSpec((FFN_HALO, tc), lambda j, i: (jnp.maximum(i * hb - 1, 0), j + off))

    def par_spec(rows, off):
        return pl.BlockSpec((rows, tc), lambda j, i: (0, j + off))

    (u,), moved = _call(
        body, name=name, grid=(nj, nt),
        in_specs=[main_spec(0), prev_spec(0), main_spec(nj), prev_spec(nj),
                  par_spec(FFN_KERNEL, 0), par_spec(FFN_KERNEL, nj), par_spec(1, 0), par_spec(1, nj)],
        out_specs=[pl.BlockSpec((tm, tc), lambda j, i: (i, j))],
        out_shape=[jax.ShapeDtypeStruct((t, f), BF16)],
        scratch_shapes=[pltpu.VMEM((tm + FFN_HALO, tc), F32), pltpu.VMEM((tm + FFN_HALO, tc), F32)],
        sem=("parallel", "arbitrary"), args=(h, h, h, h, cw, cw, cb, cb), comm=comm)
    return u if comm is None else (u, moved)


def _ffn_mid_bwd(h, du, cw, cb, name, tm=1024, tc=256, comm=None):
    t, f2 = h.shape
    tm = min(tm, t)
    f = f2 // 2
    nj, nt, hb = f // tc, t // tm, tm // FFN_HALO

    ch = min(FFN_CHUNK, tm)
    ahead = ch + SUBLANES

    def body(hg, hgp, hgn, hv, hvp, hvn, du_ref, dun_ref, cwg, cwv, cbg, cbv,
             dhg_ref, dhv_ref, dcwg_ref, dcwv_ref, dcbg_ref, dcbv_ref, sg, sv, sdu):
        i = pl.program_id(1)

        @pl.when(i == 0)
        def _():
            for ref in (dcwg_ref, dcwv_ref, dcbg_ref, dcbv_ref):
                ref[...] = jnp.zeros_like(ref)

        for main, prev, nxt, s in ((hg, hgp, hgn, sg), (hv, hvp, hvn, sv)):
            s[0:FFN_HALO, :] = jnp.where(i > 0, prev[...].astype(F32), 0.0)
            s[FFN_HALO:FFN_HALO + tm, :] = main[...].astype(F32)
            s[FFN_HALO + tm:, :] = nxt[...].astype(F32)
        sdu[0:tm, :] = du_ref[...].astype(F32)
        sdu[tm:, :] = jnp.where(i < nt - 1, dun_ref[...].astype(F32), 0.0)
        o = SUBLANES - FFN_KERNEL + 1
        for lg in range(tc // LANES):
            cols = slice(lg * LANES, (lg + 1) * LANES)
            wg, wv = [cwg[k:k + 1, cols] for k in range(FFN_KERNEL)], [cwv[k:k + 1, cols] for k in range(FFN_KERNEL)]
            bg, bv = cbg[:, cols], cbv[:, cols]

            def chunk(c, acc):
                base = pl.multiple_of(c * ch, ch)
                eg = sg[pl.ds(base + FFN_HALO - SUBLANES, ahead + SUBLANES), cols]
                ev = sv[pl.ds(base + FFN_HALO - SUBLANES, ahead + SUBLANES), cols]
                hgs = [eg[o + k:o + k + ahead] for k in range(FFN_KERNEL)]
                hvs = [ev[o + k:o + k + ahead] for k in range(FFN_KERNEL)]
                cg = wg[0] * hgs[0] + wg[1] * hgs[1] + wg[2] * hgs[2] + bg
                cv = wv[0] * hvs[0] + wv[1] * hvs[1] + wv[2] * hvs[2] + bv
                du_e = sdu[pl.ds(base, ahead), cols]
                gl, dgl = _gelu_and_grad(cg)

                def back(d, hs, w, dh_ref):
                    own = d[0:ch]
                    dh = w[2] * own + w[1] * d[1:1 + ch] + w[0] * d[2:2 + ch]
                    dh_ref[pl.ds(base, ch), cols] = dh.astype(BF16)
                    return [_fold(own)] + [_fold(own * hs[k][0:ch]) for k in range(FFN_KERNEL)]

                sums = back(du_e * cv * dgl, hgs, wg, dhg_ref) + back(du_e * gl, hvs, wv, dhv_ref)
                return tuple(a + s_ for a, s_ in zip(acc, sums))

            zero = jnp.zeros((SUBLANES, LANES), F32)
            acc = lax.fori_loop(0, tm // ch, chunk, (zero,) * (2 * (1 + FFN_KERNEL)))
            dcbg_ref[:, cols] += _colsum(acc[0])
            dcbv_ref[:, cols] += _colsum(acc[1 + FFN_KERNEL])
            for k in range(FFN_KERNEL):
                dcwg_ref[k:k + 1, cols] += _colsum(acc[1 + k])
                dcwv_ref[k:k + 1, cols] += _colsum(acc[2 + FFN_KERNEL + k])

    last_blk = t // FFN_HALO - 1

    def main_spec(off):
        return pl.BlockSpec((tm, tc), lambda j, i: (i, j + off))

    def prev_spec(off):
        return pl.BlockSpec((FFN_HALO, tc), lambda j, i: (jnp.maximum(i * hb - 1, 0), j + off))

    def next_spec(off):
        return pl.BlockSpec((FFN_HALO, tc), lambda j, i: (jnp.minimum((i + 1) * hb, last_blk), j + off))

    def par_spec(rows, off):
        return pl.BlockSpec((rows, tc), lambda j, i: (0, j + off))

    out_tile = pl.BlockSpec((tm, tc), lambda j, i: (i, j))
    outs, moved = _call(
        body, name=name, grid=(nj, nt),
        in_specs=[main_spec(0), prev_spec(0), next_spec(0), main_spec(nj), prev_spec(nj), next_spec(nj),
                  main_spec(0), next_spec(0),
                  par_spec(FFN_KERNEL, 0), par_spec(FFN_KERNEL, nj), par_spec(1, 0), par_spec(1, nj)],
        out_specs=[out_tile, out_tile, par_spec(FFN_KERNEL, 0), par_spec(FFN_KERNEL, 0), par_spec(1, 0), par_spec(1, 0)],
        out_shape=[jax.ShapeDtypeStruct((t, f), BF16), jax.ShapeDtypeStruct((t, f), BF16),
                   jax.ShapeDtypeStruct((FFN_KERNEL, f), F32), jax.ShapeDtypeStruct((FFN_KERNEL, f), F32),
                   jax.ShapeDtypeStruct((1, f), F32), jax.ShapeDtypeStruct((1, f), F32)],
        scratch_shapes=[pltpu.VMEM((tm + 2 * FFN_HALO, tc), F32), pltpu.VMEM((tm + 2 * FFN_HALO, tc), F32),
                        pltpu.VMEM((tm + FFN_HALO, tc), F32)],
        sem=("parallel", "arbitrary"), args=(h, h, h, h, h, h, du, du, cw, cw, cb, cb), comm=comm)
    return outs if comm is None else (outs, moved)


MIX_HALO = 32


def _glu(hh):
    return hh[:, 0:A_WIDTH] * _sigmoid(hh[:, A_WIDTH:2 * A_WIDTH])


def _fill_row_shifts(s):
    rows = s.shape[1] - SUBLANES
    for j in range(1, SUBLANES):
        s[j, 0:rows, :] = s[0, pl.ds(j, rows), :]


def _rows_from(s, start, rows):
    j = start % SUBLANES
    return s[j, start - j:start - j + rows, :]


def _tril_mask():
    return lax.broadcasted_iota(jnp.int32, (B_CHUNK, B_CHUNK), 0) >= lax.broadcasted_iota(jnp.int32, (B_CHUNK, B_CHUNK), 1)


def _spatial_mix(q, ms_ref, sbt_ref, tm):
    mask = _tril_mask()
    ws = [jnp.where(mask, ms_ref[g], 0.0).astype(BF16) for g in range(B_GROUPS)]
    qb = q.astype(BF16)
    rows = []
    for c in range(tm // B_CHUNK):
        cols = [_dot(ws[g], qb[c * B_CHUNK:(c + 1) * B_CHUNK, g * 128:(g + 1) * 128], NN) + sbt_ref[:, g:g + 1]
                for g in range(B_GROUPS)]
        rows.append(jnp.concatenate(cols, axis=1))
    return jnp.concatenate(rows, axis=0)


def _mixer_mid_fwd(h, cw, cb, ag, ab, bg, bb, ms, sbt, name, tm=256, comm=None):
    t = h.shape[0]
    nt, hb = t // tm, tm // MIX_HALO
    o = MIX_HALO - A_KERNEL + 1

    def body(h_ref, hp_ref, cw_ref, cb_ref, ag_ref, ab_ref, bg_ref, bb_ref, ms_ref, sbt_ref, cat_ref, sp):
        i = pl.program_id(0)
        sp[0, 0:MIX_HALO, :] = jnp.where(i > 0, _glu(hp_ref[:, 0:2 * A_WIDTH].astype(F32)), 0.0)
        sp[0, MIX_HALO:, :] = _glu(h_ref[:, 0:2 * A_WIDTH].astype(F32))
        _fill_row_shifts(sp)
        y = jnp.zeros((tm, A_WIDTH), F32) + cb_ref[...]
        for k in range(A_KERNEL):
            y = y + cw_ref[k:k + 1, :] * _rows_from(sp, o + k, tm)
        nh, _ = _ln_stats(y)
        ln = nh * ag_ref[...] + ab_ref[...]
        cat_ref[:, 0:A_WIDTH] = (ln * _sigmoid(ln)).astype(BF16)
        u = _gelu(h_ref[:, 1024:1536].astype(F32))
        nb, _ = _ln_stats(_gelu(h_ref[:, 1536:2048].astype(F32)))
        mixed = _spatial_mix(nb * bg_ref[...] + bb_ref[...], ms_ref, sbt_ref, tm)
        cat_ref[:, A_WIDTH:] = (u * mixed).astype(BF16)

    vec = pl.BlockSpec((1, A_WIDTH), lambda i: (0, 0))
    (cat,), moved = _call(
        body, name=name, grid=(nt,),
        in_specs=[pl.BlockSpec((tm, 2048), lambda i: (i, 0)),
                  pl.BlockSpec((MIX_HALO, 2048), lambda i: (jnp.maximum(i * hb - 1, 0), 0)),
                  pl.BlockSpec((A_KERNEL, A_WIDTH), lambda i: (0, 0)), vec, vec, vec, vec, vec,
                  pl.BlockSpec((B_GROUPS, B_CHUNK, B_CHUNK), lambda i: (0, 0, 0)),
                  pl.BlockSpec((B_CHUNK, B_GROUPS), lambda i: (0, 0))],
        out_specs=[pl.BlockSpec((tm, D_MODEL), lambda i: (i, 0))],
        out_shape=[jax.ShapeDtypeStruct((t, D_MODEL), BF16)],
        scratch_shapes=[pltpu.VMEM((SUBLANES, tm + MIX_HALO, A_WIDTH), F32)],
        sem=("parallel",), args=(h, h, cw, cb, ag, ab, bg, bb, ms, sbt), comm=comm)
    return cat if comm is None else (cat, moved)


def _mixer_mid_bwd(h, dcat, cw, cb, ag, ab, bg, bb, ms, mst, sbt, name, tm=256, comm=None):
    t = h.shape[0]
    nt, hb = t // tm, tm // MIX_HALO
    o = MIX_HALO - A_KERNEL + 1
    r = tm + MIX_HALO
    nchunk = tm // B_CHUNK

    def body(h_ref, hp_ref, hn_ref, dc_ref, dcn_ref, cw_ref, cb_ref, ag_ref, ab_ref, bg_ref, bb_ref, ms_ref, mst_ref, sbt_ref,
             dh_ref, dcw_ref, dcb_ref, dag_ref, dab_ref, dbg_ref, dbb_ref, dms_ref, dsb_ref, sp, sdy, sbacc):
        i = pl.program_id(0)

        @pl.when(i == 0)
        def _():
            for ref in (dcw_ref, dcb_ref, dag_ref, dab_ref, dbg_ref, dbb_ref, dms_ref, dsb_ref, sbacc):
                ref[...] = jnp.zeros_like(ref)

        sp[0, 0:MIX_HALO, :] = jnp.where(i > 0, _glu(hp_ref[:, 0:2 * A_WIDTH].astype(F32)), 0.0)
        sp[0, MIX_HALO:MIX_HALO + tm, :] = _glu(h_ref[:, 0:2 * A_WIDTH].astype(F32))
        sp[0, MIX_HALO + tm:, :] = _glu(hn_ref[:, 0:2 * A_WIDTH].astype(F32))
        _fill_row_shifts(sp)
        y = jnp.zeros((r, A_WIDTH), F32) + cb_ref[...]
        for k in range(A_KERNEL):
            y = y + cw_ref[k:k + 1, :] * _rows_from(sp, o + k, r)
        nh, rs = _ln_stats(y)
        ln = nh * ag_ref[...] + ab_ref[...]
        sg = _sigmoid(ln)
        dao = jnp.concatenate([dc_ref[:, 0:A_WIDTH].astype(F32),
                               jnp.where(i < nt - 1, dcn_ref[:, 0:A_WIDTH].astype(F32), 0.0)], axis=0)
        dln = dao * (sg * (1.0 + ln * (1.0 - sg)))
        dag_ref[...] += _colsum(dln[0:tm] * nh[0:tm])
        dab_ref[...] += _colsum(dln[0:tm])
        sdy[0] = _ln_bwd_rows(dln * ag_ref[...], nh, rs)
        _fill_row_shifts(sdy)
        dy_own = sdy[0, 0:tm, :]
        dcb_ref[...] += _colsum(dy_own)
        dp = jnp.zeros((tm, A_WIDTH), F32)
        for k in range(A_KERNEL):
            dcw_ref[k:k + 1, :] += _colsum(dy_own * _rows_from(sp, o + k, tm))
            dp = dp + cw_ref[k:k + 1, :] * _rows_from(sdy, A_KERNEL - 1 - k, tm)
        av = h_ref[:, 0:A_WIDTH].astype(F32)
        s = _sigmoid(h_ref[:, A_WIDTH:2 * A_WIDTH].astype(F32))
        dh_ref[:, 0:A_WIDTH] = (dp * s).astype(BF16)
        dh_ref[:, A_WIDTH:2 * A_WIDTH] = (dp * av * s * (1.0 - s)).astype(BF16)

        u, dgu = _gelu_and_grad(h_ref[:, 1024:1536].astype(F32))
        w, dgw = _gelu_and_grad(h_ref[:, 1536:2048].astype(F32))
        nb, rb = _ln_stats(w)
        q = nb * bg_ref[...] + bb_ref[...]
        mixed = _spatial_mix(q, ms_ref, sbt_ref, tm)
        dbo = dc_ref[:, A_WIDTH:].astype(F32)
        dh_ref[:, 1024:1536] = (dbo * mixed * dgu).astype(BF16)
        dmx = dbo * u
        mask = _tril_mask()
        wst = [jnp.where(mask.T, mst_ref[g], 0.0).astype(BF16) for g in range(B_GROUPS)]
        qb = q.astype(BF16)
        dmb = dmx.astype(BF16)
        rows = []
        for c in range(nchunk):
            cols = []
            for g in range(B_GROUPS):
                rs_, cs_ = slice(c * B_CHUNK, (c + 1) * B_CHUNK), slice(g * 128, (g + 1) * 128)
                sbacc[g] += dmx[rs_, cs_]
                dms_ref[g] += _dot(dmb[rs_, cs_], qb[rs_, cs_], NT)
                cols.append(_dot(wst[g], dmb[rs_, cs_], NN))
            rows.append(jnp.concatenate(cols, axis=1))
        dq = jnp.concatenate(rows, axis=0)
        dbg_ref[...] += _colsum(dq * nb)
        dbb_ref[...] += _colsum(dq)
        dh_ref[:, 1536:2048] = (_ln_bwd_rows(dq * bg_ref[...], nb, rb) * dgw).astype(BF16)

        @pl.when(i == nt - 1)
        def _():
            for g in range(B_GROUPS):
                dms_ref[g] = jnp.where(mask, dms_ref[g], 0.0)
                dsb_ref[g] = jnp.sum(sbacc[g], axis=1, keepdims=True)

    last_blk = t // MIX_HALO - 1
    vec = pl.BlockSpec((1, A_WIDTH), lambda i: (0, 0))
    mat = pl.BlockSpec((B_GROUPS, B_CHUNK, B_CHUNK), lambda i: (0, 0, 0))
    taps = pl.BlockSpec((A_KERNEL, A_WIDTH), lambda i: (0, 0))

    def halo(width, which):
        if which == "prev":
            return pl.BlockSpec((MIX_HALO, width), lambda i: (jnp.maximum(i * hb - 1, 0), 0))
        return pl.BlockSpec((MIX_HALO, width), lambda i: (jnp.minimum((i + 1) * hb, last_blk), 0))

    vshape = jax.ShapeDtypeStruct((1, A_WIDTH), F32)
    outs, moved = _call(
        body, name=name, grid=(nt,),
        in_specs=[pl.BlockSpec((tm, 2048), lambda i: (i, 0)), halo(2048, "prev"), halo(2048, "next"),
                  pl.BlockSpec((tm, D_MODEL), lambda i: (i, 0)), halo(D_MODEL, "next"),
                  taps, vec, vec, vec, vec, vec, mat, mat, pl.BlockSpec((B_CHUNK, B_GROUPS), lambda i: (0, 0))],
        out_specs=[pl.BlockSpec((tm, 2048), lambda i: (i, 0)), taps, vec, vec, vec, vec, vec, mat,
                   pl.BlockSpec((B_GROUPS, B_CHUNK, 1), lambda i: (0, 0, 0))],
        out_shape=[jax.ShapeDtypeStruct((t, 2048), BF16), jax.ShapeDtypeStruct((A_KERNEL, A_WIDTH), F32),
                   vshape, vshape, vshape, vshape, vshape,
                   jax.ShapeDtypeStruct((B_GROUPS, B_CHUNK, B_CHUNK), F32), jax.ShapeDtypeStruct((B_GROUPS, B_CHUNK, 1), F32)],
        scratch_shapes=[pltpu.VMEM((SUBLANES, tm + 2 * MIX_HALO, A_WIDTH), F32), pltpu.VMEM((SUBLANES, r, A_WIDTH), F32),
                        pltpu.VMEM((B_GROUPS, B_CHUNK, B_CHUNK), F32)],
        sem=("arbitrary",), args=(h, h, h, dcat, dcat, cw, cb, ag, ab, bg, bb, ms, mst, sbt), comm=comm)
    return outs if comm is None else (outs, moved)


Q_WIDTH = N_Q_HEADS * HEAD_DIM
KV_WIDTH = 2 * N_KV_HEADS * HEAD_DIM
PAIRS_PER_KV = N_Q_HEADS // N_KV_HEADS // 2
ATT_SCALE = 1.0 / math.sqrt(HEAD_DIM)


def _att_mask(n):
    qi = lax.broadcasted_iota(jnp.int32, (ATT_BLOCK, 2 * ATT_BLOCK), 0)
    sj = lax.broadcasted_iota(jnp.int32, (ATT_BLOCK, 2 * ATT_BLOCK), 1)
    diff = qi + ATT_BLOCK - sj
    return (diff >= 0) & (diff < ATT_BLOCK) & ((n > 0) | (sj >= ATT_BLOCK))


def _dup_heads(pair_cols, kv_head):
    lane = lax.broadcasted_iota(jnp.int32, pair_cols.shape, 1)
    rolled = pltpu.roll(pair_cols, HEAD_DIM, 1)
    first = lane < HEAD_DIM
    return jnp.where(first, pair_cols, rolled) if kv_head == 0 else jnp.where(first, rolled, pair_cols)


HEADS_PER_KV = N_Q_HEADS // N_KV_HEADS


def _stack_heads(ref, kh):
    lane = lax.broadcasted_iota(jnp.int32, (ATT_BLOCK, 128), 1)
    rows = []
    for pr in range(PAIRS_PER_KV):
        c0 = (kh * PAIRS_PER_KV + pr) * 128
        pair = ref[:, c0:c0 + 128]
        rows += [jnp.where(lane < HEAD_DIM, pair, jnp.zeros_like(pair)), jnp.where(lane < HEAD_DIM, jnp.zeros_like(pair), pair)]
    return jnp.concatenate(rows, axis=0)


def _unstack_heads(stacked, kh, write):
    lane = lax.broadcasted_iota(jnp.int32, (ATT_BLOCK, 128), 1)
    for pr in range(PAIRS_PER_KV):
        first = stacked[(2 * pr) * ATT_BLOCK:(2 * pr + 1) * ATT_BLOCK]
        second = stacked[(2 * pr + 1) * ATT_BLOCK:(2 * pr + 2) * ATT_BLOCK]
        write((kh * PAIRS_PER_KV + pr) * 128, jnp.where(lane < HEAD_DIM, first, second))


def _sink_column(sink_ref, kh):
    return jnp.concatenate([jnp.full((ATT_BLOCK, 1), sink_ref[0, kh * HEADS_PER_KV + h], F32) for h in range(HEADS_PER_KV)], axis=0)


def _att_probs(q_rows, k2, mask, sink, heads):
    s = _dot(q_rows, k2, NT) * ATT_SCALE
    s = jnp.where(jnp.concatenate([mask] * heads, axis=0), s, -jnp.inf)
    m = jnp.maximum(jnp.max(s, axis=-1, keepdims=True), sink)
    e = jnp.exp(s - m)
    es = jnp.exp(sink - m)
    inv = 1.0 / (jnp.sum(e, axis=-1, keepdims=True) + es)
    return e * inv, es * inv


def _attn_fwd(qkv, sinks, name):
    t = qkv.shape[0]
    nb = t // ATT_BLOCK
    kvb = Q_WIDTH // KV_WIDTH

    def body(sink_ref, q_ref, kv_ref, kvp_ref, o_ref):
        n = pl.program_id(0)
        mask = _att_mask(n)
        kv = jnp.concatenate([kvp_ref[...], kv_ref[...]], axis=0).astype(F32)

        lane = lax.broadcasted_iota(jnp.int32, (ATT_BLOCK, 128), 1)
        for kh in range(N_KV_HEADS):
            k2 = _dup_heads(kv[:, 0:128], kh).astype(BF16)
            v2 = _dup_heads(kv[:, 128:256], kh).astype(BF16)
            for pr in range(PAIRS_PER_KV):
                c0 = (kh * PAIRS_PER_KV + pr) * 128
                q2 = q_ref[:, c0:c0 + 128]
                outs = []
                for half in range(2):
                    head = (kh * PAIRS_PER_KV + pr) * 2 + half
                    qm = jnp.where((lane < HEAD_DIM) == (half == 0), q2, jnp.zeros_like(q2))
                    p, _ = _att_probs(qm, k2, mask, sink_ref[0, head], 1)
                    outs.append(_dot(p, v2, NN))
                o_ref[:, c0:c0 + 128] = jnp.where(lane < HEAD_DIM, outs[0], outs[1]).astype(BF16)

    return pl.pallas_call(
        body, name=name, grid=(nb,),
        in_specs=[pl.BlockSpec(memory_space=pltpu.SMEM),
                  pl.BlockSpec((ATT_BLOCK, Q_WIDTH), lambda n: (n, 0)),
                  pl.BlockSpec((ATT_BLOCK, KV_WIDTH), lambda n: (n, kvb)),
                  pl.BlockSpec((ATT_BLOCK, KV_WIDTH), lambda n: (jnp.maximum(n - 1, 0), kvb))],
        out_specs=pl.BlockSpec((ATT_BLOCK, Q_WIDTH), lambda n: (n, 0)),
        out_shape=jax.ShapeDtypeStruct((t, Q_WIDTH), BF16),
        compiler_params=_params("parallel"),
    )(sinks, qkv, qkv, qkv)


def _attn_bwd(qkv, d_o, sinks, name, comm=None):
    t = qkv.shape[0]
    nb = t // ATT_BLOCK
    kvb = Q_WIDTH // KV_WIDTH

    def body(sink_ref, q_ref, kv_ref, kvp_ref, do_ref, dq_ref, dkv_ref, dbq_ref, dbkv_ref, dsink_ref, carry):
        n = pl.program_id(0)

        @pl.when(n == 0)
        def _():
            for ref in (dbq_ref, dbkv_ref, dsink_ref, carry):
                ref[...] = jnp.zeros_like(ref)
            dkv_ref[...] = jnp.zeros_like(dkv_ref)

        @pl.when(n < nb)
        def _():
            mask = _att_mask(n)
            kv = jnp.concatenate([kvp_ref[...], kv_ref[...]], axis=0).astype(F32)
            lane2 = lax.broadcasted_iota(jnp.int32, (2 * ATT_BLOCK, 128), 1)
            sink_lane = lax.broadcasted_iota(jnp.int32, (1, 128), 1)
            dsink = jnp.zeros((1, 128), F32)
            dk_parts, dv_parts = [], []

            def write(c0, pair):
                dbq_ref[:, c0:c0 + 128] += _colsum(pair)
                dq_ref[:, c0:c0 + 128] = pair.astype(BF16)

            for kh in range(N_KV_HEADS):
                k2 = _dup_heads(kv[:, 0:128], kh).astype(BF16)
                v2 = _dup_heads(kv[:, 128:256], kh).astype(BF16)
                q_all = _stack_heads(q_ref, kh)
                do_all = _stack_heads(do_ref, kh)
                p, ps = _att_probs(q_all, k2, mask, _sink_column(sink_ref, kh), HEADS_PER_KV)
                dp = _dot(do_all, v2, NT)
                delta = jnp.sum(p * dp, axis=-1, keepdims=True)
                ds = p * (dp - delta) * ATT_SCALE
                psd = ps * delta
                for h in range(HEADS_PER_KV):
                    dsink = dsink + jnp.where(sink_lane == kh * HEADS_PER_KV + h,
                                              -jnp.sum(psd[h * ATT_BLOCK:(h + 1) * ATT_BLOCK]), 0.0)
                _unstack_heads(_dot(ds, k2, NN), kh, write)
                dk_acc = _dot(ds, q_all, TN)
                dv_acc = _dot(p, do_all, TN)
                dk_parts.append(dk_acc + pltpu.roll(dk_acc, HEAD_DIM, 1))
                dv_parts.append(dv_acc + pltpu.roll(dv_acc, HEAD_DIM, 1))
            dk = jnp.where(lane2 < HEAD_DIM, dk_parts[0], dk_parts[1])
            dv = jnp.where(lane2 < HEAD_DIM, dv_parts[0], dv_parts[1])
            dkv_new = jnp.concatenate([dk, dv], axis=1)
            done = carry[...] + dkv_new[0:ATT_BLOCK]

            @pl.when(n > 0)
            def _():
                dkv_ref[...] = done.astype(BF16)
                dbkv_ref[...] += _colsum(done)

            carry[...] = dkv_new[ATT_BLOCK:]
            dsink_ref[...] += dsink

        @pl.when(n == nb)
        def _():
            dkv_ref[...] = carry[...].astype(BF16)
            dbkv_ref[...] += _colsum(carry[...])

    def clamp(n):
        return jnp.minimum(n, nb - 1)

    outs, moved = _call(
        body, name=name, grid=(nb + 1,),
        in_specs=[pl.BlockSpec(memory_space=pltpu.SMEM),
                  pl.BlockSpec((ATT_BLOCK, Q_WIDTH), lambda n: (clamp(n), 0)),
                  pl.BlockSpec((ATT_BLOCK, KV_WIDTH), lambda n: (clamp(n), kvb)),
                  pl.BlockSpec((ATT_BLOCK, KV_WIDTH), lambda n: (jnp.maximum(clamp(n) - 1, 0), kvb)),
                  pl.BlockSpec((ATT_BLOCK, Q_WIDTH), lambda n: (clamp(n), 0))],
        out_specs=[pl.BlockSpec((ATT_BLOCK, Q_WIDTH), lambda n: (clamp(n), 0)),
                   pl.BlockSpec((ATT_BLOCK, KV_WIDTH), lambda n: (jnp.maximum(n - 1, 0), 0)),
                   pl.BlockSpec((1, Q_WIDTH), lambda n: (0, 0)),
                   pl.BlockSpec((1, KV_WIDTH), lambda n: (0, 0)),
                   pl.BlockSpec((1, 128), lambda n: (0, 0))],
        out_shape=[jax.ShapeDtypeStruct((t, Q_WIDTH), BF16), jax.ShapeDtypeStruct((t, KV_WIDTH), BF16),
                   jax.ShapeDtypeStruct((1, Q_WIDTH), F32), jax.ShapeDtypeStruct((1, KV_WIDTH), F32),
                   jax.ShapeDtypeStruct((1, 128), F32)],
        scratch_shapes=[pltpu.VMEM((ATT_BLOCK, KV_WIDTH), F32)],
        sem=("arbitrary",), args=(sinks, qkv, qkv, qkv, d_o), comm=comm)
    return outs if comm is None else (outs, moved)


def _adamw_math(g, w, m, v):
    m = ADAM_B1 * m + (1.0 - ADAM_B1) * g
    v = ADAM_B2 * v + (1.0 - ADAM_B2) * (g * g)
    m_hat = m / (1.0 - ADAM_B1 ** ADAM_STEP)
    v_hat = v / (1.0 - ADAM_B2 ** ADAM_STEP)
    delta = -ADAM_LR * (m_hat / (jnp.sqrt(v_hat) + ADAM_EPS) + ADAM_WD * w)
    return delta, m, v


def _sum_partials(p_ref):
    g = p_ref[0].astype(F32)
    for s in range(1, N_DEV):
        g = g + p_ref[s].astype(F32)
    return g


def _adamw_big(parts, w, m, v, name, tr):
    r, c = w.shape
    tiles = [p.shape[1] // tr for p in parts]
    starts = [sum(tiles[:l]) for l in range(len(parts))]
    assert all(p.shape[1] % tr == 0 for p in parts) and sum(tiles) * tr == r

    def body(*refs):
        p_refs, (w_ref, m_ref, v_ref, g_out, d_out, m_out, v_out) = refs[:len(parts)], refs[len(parts):]
        i = pl.program_id(0)
        for l, p_ref in enumerate(p_refs):
            @pl.when((i >= starts[l]) & (i < starts[l] + tiles[l]))
            def _():
                g = _sum_partials(p_ref)
                g_out[...] = g
                d_out[...], m_out[...], v_out[...] = _adamw_math(g, w_ref[...], m_ref[...], v_ref[...])

    def part_spec(l):
        return pl.BlockSpec((N_DEV, tr, c), lambda i: (0, jnp.clip(i - starts[l], 0, tiles[l] - 1), 0))

    tile = pl.BlockSpec((tr, c), lambda i: (i, 0))
    shape = jax.ShapeDtypeStruct((r, c), F32)
    return pl.pallas_call(
        body, name=name, grid=(r // tr,),
        in_specs=[part_spec(l) for l in range(len(parts))] + [tile, tile, tile],
        out_specs=[tile] * 4, out_shape=[shape] * 4,
        compiler_params=_params("parallel"),
    )(*parts, w, m, v)


def _adamw_small(parts, ws, ms, vs, name):
    n = len(ws)

    def body(*refs):
        ins, outs = refs[:4 * n], refs[4 * n:]
        for a in range(n):
            g = _sum_partials(ins[a])
            outs[4 * a][...] = g
            outs[4 * a + 1][...], outs[4 * a + 2][...], outs[4 * a + 3][...] = _adamw_math(
                g, ins[n + a][...], ins[2 * n + a][...], ins[3 * n + a][...])

    out_shape = []
    for w in ws:
        out_shape += [jax.ShapeDtypeStruct(w.shape, F32)] * 4
    return pl.pallas_call(body, name=name, out_shape=out_shape, compiler_params=_params())(*parts, *ws, *ms, *vs)


PACK_LANES = 128
PACK_ROWS = 8


def _pack(arrs):
    flat = jnp.concatenate([a.reshape(-1).astype(F32) for a in arrs])
    unit = PACK_LANES * PACK_ROWS
    total = -(-flat.shape[0] // unit) * unit
    return jnp.pad(flat, (0, total - flat.shape[0])).reshape(-1, PACK_LANES)


def _unpack(buf, shapes):
    flat = buf.reshape(N_DEV, -1)
    out, pos = [], 0
    for s in shapes:
        size = math.prod(s)
        out.append(flat[:, pos:pos + size].reshape((N_DEV,) + tuple(s)))
        pos += size
    return out


def _interleave(g):
    return jnp.transpose(g, (1, 0, 2)).reshape(g.shape[1], -1)


def _deinterleave(w):
    r = w.shape[0]
    return jnp.transpose(w.reshape(r, N_DEV, -1), (1, 0, 2))


def _ffn_backward(dz, x_in, h, u, w_up_t, cw, cb, w_down, tag, comm=None):
    du = _matmul(dz, w_down, "nt", BF16, f"ffn{tag}_du", 1024, 1408, 1024)
    d_w_down = _matmul(u, dz, "tn", BF16, f"ffn{tag}_dwdown", 1408, 1024, 512)
    outs = _ffn_mid_bwd(h, du, cw, cb, f"ffn{tag}_mid_bwd", comm=comm)
    (dhg, dhv, dcwg, dcwv, dcbg, dcbv), moved = outs if comm is not None else (outs, [])
    d_w_up_t = jnp.concatenate([_matmul(dhg, x_in, "tn", BF16, f"ffn{tag}_dwup_gate", 1408, 1024, 1024),
                                _matmul(dhv, x_in, "tn", BF16, f"ffn{tag}_dwup_value", 1408, 1024, 1024)], axis=0)
    dx = _matmul(dhv, w_up_t, "nn", F32, f"ffn{tag}_dx_value", 1024, 1024, 1408, b_off=D_FF // 1408)
    dx = _matmul(dhg, w_up_t, "nn", F32, f"ffn{tag}_dx_gate", 1024, 1024, 1408, res=dx)
    return (dx, d_w_up_t.reshape(N_DEV, -1, D_MODEL), d_w_down.reshape(N_DEV, -1, D_MODEL),
            jnp.concatenate([dcwg, dcwv], axis=1), jnp.concatenate([dcbg, dcbv], axis=1), moved)


def kernel(x, ab_w_in, a_conv_w, a_conv_b, a_norm_g, a_norm_b, b_norm_g, b_norm_b, b_spatial_w, b_spatial_b, ab_w_out, c_w_qkv, c_b_qkv, c_sinks, c_w_o, ffn_w_up, ffn_conv_w, ffn_conv_b, ffn_w_down, ln_g, ln_b, loss_target, m_ab_w_in, m_a_conv_w, m_a_conv_b, m_a_norm_g, m_a_norm_b, m_b_norm_g, m_b_norm_b, m_b_spatial_w, m_b_spatial_b, m_ab_w_out, m_c_w_qkv, m_c_b_qkv, m_c_sinks, m_c_w_o, m_ffn_w_up, m_ffn_conv_w, m_ffn_conv_b, m_ffn_w_down, m_ln_g, m_ln_b, v_ab_w_in, v_a_conv_w, v_a_conv_b, v_a_norm_g, v_a_norm_b, v_b_norm_g, v_b_norm_b, v_b_spatial_w, v_b_spatial_b, v_ab_w_out, v_c_w_qkv, v_c_b_qkv, v_c_sinks, v_c_w_o, v_ffn_w_up, v_ffn_conv_w, v_ffn_conv_b, v_ffn_w_down, v_ln_g, v_ln_b):
    me = 4 * lax.axis_index("x") + 2 * lax.axis_index("y") + lax.axis_index("c")
    xt = x[0]
    t = xt.shape[0]

    small_shard_shapes = [a_conv_w.shape, c_b_qkv.shape, ffn_conv_w.shape, ln_g.shape, ln_b.shape]
    up_shard = [jnp.swapaxes(ffn_w_up[l], 0, 1).astype(BF16) for l in range(2)]
    qkv_shard = jnp.swapaxes(c_w_qkv[0], 0, 1).astype(BF16)
    down_shard = [ffn_w_down[l].astype(BF16) for l in range(2)]
    g_win, g_small = _comm_only(
        _Comm(gather=[ab_w_in[0].astype(BF16), _pack([a_conv_w, c_b_qkv, ffn_conv_w, ln_g, ln_b])]), "gather_first")
    w_in = _interleave(g_win)
    g_acw, g_bqkv, g_fcw, g_lng, g_lnb = _unpack(g_small, small_shard_shapes)
    acw = _interleave(g_acw[:, 0])
    bqkv = g_bqkv[:, 0].reshape(1, -1)
    fcw = [_interleave(g_fcw[:, l]) for l in range(2)]
    lng = jnp.transpose(g_lng, (1, 2, 0, 3)).reshape(2, 2, 1, D_MODEL)
    lnb = jnp.transpose(g_lnb, (1, 2, 0, 3)).reshape(2, 2, 1, D_MODEL)
    fcb = [ffn_conv_b[l:l + 1] for l in range(2)]
    ms = b_spatial_w[0]
    mst = jnp.swapaxes(ms, 1, 2)
    sbt = b_spatial_b[0].T

    h0, (g_wout,) = _matmul(xt, w_in, "nn", BF16, "mix_in", 1024, 1024, 1024, comm=_Comm(gather=[ab_w_out[0].astype(BF16)]))
    w_out = g_wout.reshape(D_MODEL, D_MODEL)
    cat, (g_wup0,) = _mixer_mid_fwd(h0, acw, a_conv_b, a_norm_g, a_norm_b, b_norm_g, b_norm_b, ms, sbt, "mix_mid_fwd",
                                    comm=_Comm(gather=[up_shard[0]]))
    w_up0 = g_wup0.reshape(2 * D_FF, D_MODEL)
    z1, x1 = _matmul_res_ln(cat, w_out, xt, lng[0, 0], lnb[0, 0], "mix_out_ln", 512, D_MODEL)
    hf0, (g_wdown0, g_wqkv, g_wo) = _matmul(x1, w_up0, "nt", BF16, "ffn0_up", 1024, 1408, 1024,
                                            comm=_Comm(gather=[down_shard[0], qkv_shard, c_w_o[0].astype(BF16)]))
    w_down0 = g_wdown0.reshape(D_FF, D_MODEL)
    w_qkv = g_wqkv.reshape(Q_WIDTH + KV_WIDTH, D_MODEL)
    w_o = g_wo.reshape(D_MODEL, D_MODEL)
    u0, (g_wup1,) = _ffn_mid_fwd(hf0, fcw[0], fcb[0], "ffn0_mid_fwd", comm=_Comm(gather=[up_shard[1]]))
    w_up1 = g_wup1.reshape(2 * D_FF, D_MODEL)
    (z2, x2), (g_wdown1,) = _matmul_res_ln(u0, w_down0, x1, lng[0, 1], lnb[0, 1], "ffn0_down_ln", 512, D_FF,
                                           comm=_Comm(gather=[down_shard[1]]))
    w_down1 = g_wdown1.reshape(D_FF, D_MODEL)
    qkv = _matmul(x2, w_qkv, "nt", BF16, "att_qkv", 1024, 1280, 1024, bias=bqkv)
    att = _attn_fwd(qkv, c_sinks, "att_fwd")
    z3, x3 = _matmul_res_ln(att, w_o, x2, lng[1, 0], lnb[1, 0], "att_out_ln", 512, D_MODEL)
    hf1 = _matmul(x3, w_up1, "nt", BF16, "ffn1_up", 1024, 1408, 1024)
    u1 = _ffn_mid_fwd(hf1, fcw[1], fcb[1], "ffn1_mid_fwd")
    z4, _ = _matmul_res_ln(u1, w_down1, x3, lng[1, 1], lnb[1, 1], "ffn1_down_ln", 512, D_FF)

    dz4, dg11, db11, loss_terms = _ln_bwd_loss(z4, lng[1, 1], lnb[1, 1], loss_target[0], "loss_ln_bwd")
    dx3, d_wup1, d_wdown1, d_fcw1, d_fcb1, _ = _ffn_backward(dz4, x3, hf1, u1, w_up1, fcw[1], fcb[1], w_down1, 1)
    dz3, dg10, db10 = _ln_bwd(z3, lng[1, 0], dz4, dx3, "ln10_bwd")
    d_att = _matmul(dz3, w_o, "nt", BF16, "att_dout", 1024, 1024, 1024)
    d_wo = _matmul(att, dz3, "tn", BF16, "att_dwo", 1024, 1024, 512)
    (dq, dkv, dbq, dbkv, dsinks), (p_wup1, p_wdown1) = _attn_bwd(qkv, d_att, c_sinks, "att_bwd",
                                                                  comm=_Comm(exchange=[d_wup1, d_wdown1]))
    d_wqkv = jnp.concatenate([_matmul(dq, x2, "tn", BF16, "att_dwq", 1024, 1024, 1024),
                              _matmul(dkv, x2, "tn", BF16, "att_dwkv", KV_WIDTH, 1024, 1024)], axis=0)
    dx2 = _matmul(dkv, w_qkv, "nn", F32, "att_dx_kv", 1024, 1024, KV_WIDTH, b_off=Q_WIDTH // KV_WIDTH)
    dx2 = _matmul(dq, w_qkv, "nn", F32, "att_dx_q", 1024, 1024, KV_WIDTH, res=dx2)
    dz2, dg01, db01 = _ln_bwd(z2, lng[0, 1], dz3, dx2, "ln01_bwd")
    dx1, d_wup0, d_wdown0, d_fcw0, d_fcb0, (p_wqkv, p_wo) = _ffn_backward(
        dz2, x1, hf0, u0, w_up0, fcw[0], fcb[0], w_down0, 0,
        comm=_Comm(exchange=[d_wqkv.reshape(N_DEV, -1, D_MODEL), d_wo.reshape(N_DEV, -1, D_MODEL)]))
    dz1, dg00, db00 = _ln_bwd(z1, lng[0, 0], dz2, dx1, "ln00_bwd")
    dcat = _matmul(dz1, w_out, "nt", BF16, "mix_dcat", 1024, 1024, 1024)
    d_wout = _matmul(cat, dz1, "tn", BF16, "mix_dwout", 1024, 1024, 512)
    (dh0, d_acw, d_acb, d_ang, d_anb, d_bng, d_bnb, d_ms, d_sb), (p_wup0, p_wdown0, p_wout) = _mixer_mid_bwd(
        h0, dcat, acw, a_conv_b, a_norm_g, a_norm_b, b_norm_g, b_norm_b, ms, mst, sbt, "mix_mid_bwd",
        comm=_Comm(exchange=[d_wup0, d_wdown0, d_wout.reshape(N_DEV, -1, D_MODEL)]))
    d_bqkv = jnp.concatenate([dbq, dbkv], axis=1)
    d_lng = jnp.stack([jnp.stack([dg00, dg01]), jnp.stack([dg10, dg11])])
    d_lnb = jnp.stack([jnp.stack([db00, db01]), jnp.stack([db10, db11])])
    small_full = [d_acb, d_ang, d_anb, d_bng, d_bnb, d_ms, d_sb, dsinks[:, :N_Q_HEADS], jnp.concatenate([d_fcb0, d_fcb1], axis=0),
                  d_acw, d_bqkv, jnp.stack([d_fcw0, d_fcw1]), d_lng, d_lnb]
    d_win, (g_small_grads,) = _matmul(xt, dh0, "tn", BF16, "mix_dwin", 1024, 1024, 512, comm=_Comm(gather=[_pack(small_full)]))
    grad_x, (p_win,) = _matmul(dh0, w_in, "nt", F32, "mix_dx", 1024, 1024, 1024, res=dz1, res_scale=ALPHA,
                               comm=_Comm(exchange=[_deinterleave(d_win)]))

    loss = lax.psum(0.5 / D_MODEL * jnp.sum(loss_terms), ("x", "y", "c"))

    big = {}
    for nm, p, w, m, v, tr, transposed in [
            ("ab_w_in", [p_win], ab_w_in, m_ab_w_in, v_ab_w_in, 256, False),
            ("ab_w_out", [p_wout], ab_w_out, m_ab_w_out, v_ab_w_out, 128, False),
            ("c_w_qkv", [p_wqkv], c_w_qkv, m_c_w_qkv, v_c_w_qkv, 160, True), ("c_w_o", [p_wo], c_w_o, m_c_w_o, v_c_w_o, 128, False),
            ("ffn_w_up", [p_wup0, p_wup1], ffn_w_up, m_ffn_w_up, v_ffn_w_up, 176, True),
            ("ffn_w_down", [p_wdown0, p_wdown1], ffn_w_down, m_ffn_w_down, v_ffn_w_down, 176, False)]:
        def two_d(a):
            a = jnp.swapaxes(a, 1, 2) if transposed else a
            return a.reshape(-1, a.shape[-1])

        def back(o):
            return jnp.swapaxes(o.reshape(w.shape[0], w.shape[2], w.shape[1]), 1, 2) if transposed else o.reshape(w.shape)

        outs = _adamw_big(p, two_d(w), two_d(m), two_d(v), "adamw_" + nm, tr)
        big[nm] = [back(o) for o in outs]

    gs = _unpack(g_small_grads, [a.shape for a in small_full])

    def my_shard(g, width):
        g = g.reshape(g.shape[:-1] + (N_DEV, width))
        return lax.dynamic_index_in_dim(g, me, axis=g.ndim - 2, keepdims=False)

    small_names = ["a_conv_b", "a_norm_g", "a_norm_b", "b_norm_g", "b_norm_b", "b_spatial_w", "b_spatial_b", "c_sinks", "ffn_conv_b",
                   "a_conv_w", "c_b_qkv", "ffn_conv_w", "ln_g", "ln_b"]
    small_w = [a_conv_b, a_norm_g, a_norm_b, b_norm_g, b_norm_b, b_spatial_w, b_spatial_b, c_sinks, ffn_conv_b,
               a_conv_w, c_b_qkv, ffn_conv_w, ln_g, ln_b]
    small_m = [m_a_conv_b, m_a_norm_g, m_a_norm_b, m_b_norm_g, m_b_norm_b, m_b_spatial_w, m_b_spatial_b, m_c_sinks, m_ffn_conv_b,
               m_a_conv_w, m_c_b_qkv, m_ffn_conv_w, m_ln_g, m_ln_b]
    small_v = [v_a_conv_b, v_a_norm_g, v_a_norm_b, v_b_norm_g, v_b_norm_b, v_b_spatial_w, v_b_spatial_b, v_c_sinks, v_ffn_conv_b,
               v_a_conv_w, v_c_b_qkv, v_ffn_conv_w, v_ln_g, v_ln_b]
    gs[9:] = [my_shard(g, w.shape[-1]) for g, w in zip(gs[9:], small_w[9:])]
    two_d = [(-1, w.shape[-1]) for w in small_w]
    outs = _adamw_small([g.reshape((N_DEV,) + w.reshape(s).shape) for g, w, s in zip(gs, small_w, two_d)],
                        [w.reshape(s) for w, s in zip(small_w, two_d)], [m.reshape(s) for m, s in zip(small_m, two_d)],
                        [v.reshape(s) for v, s in zip(small_v, two_d)], "adamw_small")
    small = {nm: [o.reshape(w.shape) for o in outs[4 * a:4 * a + 4]] for a, (nm, w) in enumerate(zip(small_names, small_w))}

    res = {**big, **small}
    order = ["ab_w_in", "a_conv_w", "a_conv_b", "a_norm_g", "a_norm_b", "b_norm_g", "b_norm_b", "b_spatial_w", "b_spatial_b", "ab_w_out",
             "c_w_qkv", "c_b_qkv", "c_sinks", "c_w_o", "ffn_w_up", "ffn_conv_w", "ffn_conv_b", "ffn_w_down", "ln_g", "ln_b"]
    return (loss, grad_x[None], *[res[nm][0] for nm in order], *[res[nm][1] for nm in order],
            *[res[nm][2] for nm in order], *[res[nm][3] for nm in order])
```

```python
import functools
import math

import jax
import jax.numpy as jnp
from jax import lax
from jax.experimental import pallas as pl
from jax.experimental.pallas import tpu as pltpu

F32 = jnp.float32
BF16 = jnp.bfloat16

N_DEV = 8
D_MODEL = 1024
A_WIDTH = 512
A_KERNEL = 31
B_GROUPS = 4
B_CHUNK = 128
HEAD_DIM = 64
N_Q_HEADS = 16
N_KV_HEADS = 2
ATT_BLOCK = 128
D_FF = 2816
FFN_KERNEL = 3
ALPHA = (2.0 * 2) ** 0.25
LN_EPS = 1e-5
GELU_K = math.sqrt(2.0 / math.pi)
GELU_C = 0.044715
ADAM_LR = 0.001
ADAM_B1 = 0.9
ADAM_B2 = 0.999
ADAM_EPS = 1e-08
ADAM_WD = 0.01
ADAM_STEP = 10
VMEM_LIMIT = 56 * 1024 * 1024
MESH_ID = pl.DeviceIdType.MESH


def _params(*sem):
    return pltpu.CompilerParams(dimension_semantics=sem, vmem_limit_bytes=VMEM_LIMIT)


def _gelu(x):
    t = jnp.tanh(GELU_K * x * (1.0 + GELU_C * x * x))
    return 0.5 * x * (1.0 + t)


def _gelu_and_grad(x):
    x2 = x * x
    t = jnp.tanh(GELU_K * x * (1.0 + GELU_C * x2))
    g = 0.5 * x * (1.0 + t)
    dg = 0.5 * (1.0 + t) + 0.5 * x * (1.0 - t * t) * (GELU_K * (1.0 + 3.0 * GELU_C * x2))
    return g, dg


def _sigmoid(x):
    return 1.0 / (1.0 + jnp.exp(-x))


def _ln_stats(z):
    mu = jnp.mean(z, axis=-1, keepdims=True)
    zc = z - mu
    var = jnp.mean(zc * zc, axis=-1, keepdims=True)
    r = lax.rsqrt(var + LN_EPS)
    return zc * r, r


def _ln_bwd_rows(dn, nh, r):
    return r * (dn - jnp.mean(dn, axis=-1, keepdims=True) - nh * jnp.mean(dn * nh, axis=-1, keepdims=True))


def _colsum(x):
    return jnp.sum(x, axis=0, keepdims=True)


def _dot(a, b, dims):
    return lax.dot_general(a.astype(BF16), b.astype(BF16), (dims, ((), ())), preferred_element_type=F32)


NN = ((1,), (0,))
NT = ((1,), (1,))
TN = ((0,), (0,))


ANY = pl.BlockSpec(memory_space=pl.ANY)
N_RELATIONS = N_DEV - 1


def _my_place():
    return lax.axis_index("x"), lax.axis_index("y"), lax.axis_index("c")


class _Comm:
    def __init__(self, gather=(), exchange=()):
        self.arrs = list(gather) + list(exchange)
        self.n_gather = len(gather)
        self.n = len(self.arrs)

    def out_shape(self):
        return [jax.ShapeDtypeStruct(((N_DEV,) + a.shape) if i < self.n_gather else a.shape, a.dtype)
                for i, a in enumerate(self.arrs)]

    def sems(self):
        return [pltpu.SemaphoreType.DMA((self.n, N_RELATIONS)), pltpu.SemaphoreType.DMA((self.n, N_RELATIONS)),
                pltpu.SemaphoreType.DMA((self.n,))]

    def _gather_copy(self, ins, outs, sems, a, k, place, to, from_input=False):
        px, py, pc = place
        block = outs[a].at[4 * px + 2 * py + pc]
        return pltpu.make_async_remote_copy(
            src_ref=ins[a] if from_input else block, dst_ref=block,
            send_sem=sems[0].at[a, k], recv_sem=sems[1].at[a, k], device_id=to, device_id_type=MESH_ID)

    def _exchange_copy(self, ins, outs, sems, a, k, landing=False):
        x, y, c = _my_place()
        me = 4 * x + 2 * y + c
        peer = (x ^ (k >> 2), y ^ ((k >> 1) & 1), c ^ (k & 1))
        return pltpu.make_async_remote_copy(
            src_ref=ins[a].at[me ^ k], dst_ref=outs[a].at[(me ^ k) if landing else me],
            send_sem=sems[0].at[a, k - 1], recv_sem=sems[1].at[a, k - 1], device_id=peer, device_id_type=MESH_ID)

    def _local_copy(self, ins, outs, sems, a):
        x, y, c = _my_place()
        me = 4 * x + 2 * y + c
        src = ins[a] if a < self.n_gather else ins[a].at[me]
        return pltpu.make_async_copy(src, outs[a].at[me], sems[2].at[a])

    def _first_stage(self, ins, outs, sems, a):
        x, y, c = _my_place()
        me = (x, y, c)
        chips = [(1 - x, y), (x, 1 - y), (1 - x, 1 - y)]
        return ([self._gather_copy(ins, outs, sems, a, 0, me, (x, y, 1 - c), from_input=True)]
                + [self._gather_copy(ins, outs, sems, a, 1 + j, me, (*chip, c), from_input=True) for j, chip in enumerate(chips)])

    def start(self, ins, outs, sems):
        for a in range(self.n):
            self._local_copy(ins, outs, sems, a).start()
        for a in range(self.n_gather):
            for cp in self._first_stage(ins, outs, sems, a):
                cp.start()
        for k in range(1, N_DEV):
            for a in range(self.n_gather, self.n):
                self._exchange_copy(ins, outs, sems, a, k).start()

    def finish(self, ins, outs, sems):
        x, y, c = _my_place()
        me, sibling = (x, y, c), (x, y, 1 - c)
        chips = [(1 - x, y), (x, 1 - y), (1 - x, 1 - y)]
        passed = []
        for j, chip in enumerate(chips):
            for a in range(self.n_gather):
                self._gather_copy(ins, outs, sems, a, 1 + j, (*chip, c), me).wait_recv()
                fwd = self._gather_copy(ins, outs, sems, a, 4 + j, (*chip, c), sibling)
                fwd.start()
                passed.append(fwd)
        for a in range(self.n_gather):
            self._gather_copy(ins, outs, sems, a, 0, sibling, me).wait_recv()
            for j, chip in enumerate(chips):
                self._gather_copy(ins, outs, sems, a, 4 + j, (*chip, 1 - c), me).wait_recv()
        for k in range(1, N_DEV):
            for a in range(self.n_gather, self.n):
                self._exchange_copy(ins, outs, sems, a, k, landing=True).wait_recv()
        for a in range(self.n_gather):
            for cp in self._first_stage(ins, outs, sems, a):
                cp.wait_send()
        for cp in passed:
            cp.wait_send()
        for k in range(1, N_DEV):
            for a in range(self.n_gather, self.n):
                self._exchange_copy(ins, outs, sems, a, k).wait_send()
        for a in range(self.n):
            self._local_copy(ins, outs, sems, a).wait()


def _comm_only(comm, name):
    def body(*refs):
        ins, outs, sems = refs[:comm.n], refs[comm.n:2 * comm.n], refs[2 * comm.n:]
        comm.start(ins, outs, sems)
        comm.finish(ins, outs, sems)

    return pl.pallas_call(body, name=name, in_specs=[ANY] * comm.n, out_specs=[ANY] * comm.n,
                          out_shape=comm.out_shape(), scratch_shapes=comm.sems())(*comm.arrs)


def _call(body, *, name, grid, in_specs, out_specs, out_shape, args, sem, scratch_shapes=(), comm=None):
    in_specs, out_specs, out_shape, scratch_shapes = list(in_specs), list(out_specs), list(out_shape), list(scratch_shapes)
    if comm is None:
        outs = pl.pallas_call(body, name=name, grid=grid, in_specs=in_specs, out_specs=out_specs, out_shape=out_shape,
                              scratch_shapes=scratch_shapes, compiler_params=_params(*sem))(*args)
        return list(outs), []
    n_in, n_out, n_scr, nc = len(in_specs), len(out_specs), len(scratch_shapes), comm.n

    def wrapped(*refs):
        ins, refs = refs[:n_in], refs[n_in:]
        c_in, refs = refs[:nc], refs[nc:]
        outs, refs = refs[:n_out], refs[n_out:]
        c_out, refs = refs[:nc], refs[nc:]
        scr, sems = refs[:n_scr], refs[n_scr:]
        first = functools.reduce(jnp.logical_and, [pl.program_id(ax) == 0 for ax in range(len(grid))])
        last = functools.reduce(jnp.logical_and, [pl.program_id(ax) == g - 1 for ax, g in enumerate(grid)])

        @pl.when(first)
        def _():
            comm.start(c_in, c_out, sems)

        body(*ins, *outs, *scr)

        @pl.when(last)
        def _():
            comm.finish(c_in, c_out, sems)

    outs = pl.pallas_call(
        wrapped, name=name, grid=grid, in_specs=in_specs + [ANY] * nc, out_specs=out_specs + [ANY] * nc,
        out_shape=out_shape + comm.out_shape(), scratch_shapes=scratch_shapes + comm.sems(),
        compiler_params=_params(*(["arbitrary"] * len(grid))))(*args, *comm.arrs)
    return list(outs[:n_out]), list(outs[n_out:])


def _matmul(a, b, mode, out_dtype, name, tm, tn, tk, *, bias=None, res=None, res_scale=1.0, b_off=0, comm=None):
    tm = min(tm, a.shape[1] if mode == "tn" else a.shape[0])
    tk = min(tk, a.shape[0] if mode == "tn" else a.shape[1])
    if mode == "nn":
        (m, k), n = a.shape, b.shape[1]
        a_spec = pl.BlockSpec((tm, tk), lambda i, j, kk: (i, kk))
        b_spec = pl.BlockSpec((tk, tn), lambda i, j, kk: (kk + b_off, j))
        dims = NN
    elif mode == "nt":
        (m, k), n = a.shape, b.shape[0]
        a_spec = pl.BlockSpec((tm, tk), lambda i, j, kk: (i, kk))
        b_spec = pl.BlockSpec((tn, tk), lambda i, j, kk: (j, kk + b_off))
        dims = NT
    else:
        (k, m), n = a.shape, b.shape[1]
        a_spec = pl.BlockSpec((tk, tm), lambda i, j, kk: (kk, i))
        b_spec = pl.BlockSpec((tk, tn), lambda i, j, kk: (kk, j))
        dims = TN
    assert m % tm == 0 and n % tn == 0 and k % tk == 0, (name, m, n, k)
    nk = k // tk
    in_specs = [a_spec, b_spec]
    args = [a, b]
    if bias is not None:
        in_specs.append(pl.BlockSpec((1, tn), lambda i, j, kk: (0, j)))
        args.append(bias)
    if res is not None:
        in_specs.append(pl.BlockSpec((tm, tn), lambda i, j, kk: (i, j)))
        args.append(res)

    def finish(out, refs, o_ref):
        pos = 2
        if bias is not None:
            out = out + refs[pos][...]
            pos += 1
        if res is not None:
            out = out + res_scale * refs[pos][...].astype(F32)
        o_ref[...] = out.astype(out_dtype)

    def body_one_step(*refs):
        finish(_dot(refs[0][...], refs[1][...], dims), refs, refs[-1])

    def body(*refs):
        a_ref, b_ref = refs[0], refs[1]
        o_ref, acc = refs[-2], refs[-1]
        kk = pl.program_id(2)

        @pl.when(kk == 0)
        def _():
            acc[...] = jnp.zeros_like(acc)

        acc[...] += _dot(a_ref[...], b_ref[...], dims)

        @pl.when(kk == nk - 1)
        def _():
            finish(acc[...], refs, o_ref)

    (out,), moved = _call(
        body_one_step if nk == 1 else body, name=name, grid=(m // tm, n // tn, nk),
        in_specs=in_specs, out_specs=[pl.BlockSpec((tm, tn), lambda i, j, kk: (i, j))],
        out_shape=[jax.ShapeDtypeStruct((m, n), out_dtype)],
        scratch_shapes=[] if nk == 1 else [pltpu.VMEM((tm, tn), F32)],
        sem=("parallel", "parallel", "arbitrary"), args=args, comm=comm)
    return out if comm is None else (out, moved)


def _matmul_tn_pair(a0, a1, b, out_dtype, name, tm, tn, tk):
    (k, m), n = a0.shape, b.shape[1]
    tk = min(tk, k)
    assert a1.shape == a0.shape and m % tm == 0 and n % tn == 0 and k % tk == 0, (name, m, n, k)
    mi, nk = m // tm, k // tk

    def body(a0_ref, a1_ref, b_ref, o_ref, acc):
        i, kk = pl.program_id(0), pl.program_id(2)

        @pl.when(kk == 0)
        def _():
            acc[...] = jnp.zeros_like(acc)

        @pl.when(i < mi)
        def _():
            acc[...] += _dot(a0_ref[...], b_ref[...], TN)

        @pl.when(i >= mi)
        def _():
            acc[...] += _dot(a1_ref[...], b_ref[...], TN)

        @pl.when(kk == nk - 1)
        def _():
            o_ref[...] = acc[...].astype(out_dtype)

    return pl.pallas_call(
        body, name=name, grid=(2 * mi, n // tn, nk),
        in_specs=[pl.BlockSpec((tk, tm), lambda i, j, kk: (jnp.where(i < mi, kk, nk - 1), jnp.minimum(i, mi - 1))),
                  pl.BlockSpec((tk, tm), lambda i, j, kk: (jnp.where(i >= mi, kk, 0), jnp.maximum(i - mi, 0))),
                  pl.BlockSpec((tk, tn), lambda i, j, kk: (kk, j))],
        out_specs=pl.BlockSpec((tm, tn), lambda i, j, kk: (i, j)),
        out_shape=jax.ShapeDtypeStruct((2 * m, n), out_dtype),
        scratch_shapes=[pltpu.VMEM((tm, tn), F32)],
        compiler_params=_params("parallel", "parallel", "arbitrary"),
    )(a0, a1, b)


def _matmul_res_ln(a, b, xres, g, beta, name, tm, tk, comm=None):
    t, k = a.shape
    d = b.shape[1]
    nk = k // tk
    assert t % tm == 0 and k % tk == 0

    def body(a_ref, b_ref, x_ref, g_ref, beta_ref, z_ref, xo_ref, acc):
        kk = pl.program_id(1)

        @pl.when(kk == 0)
        def _():
            acc[...] = jnp.zeros_like(acc)

        acc[...] += _dot(a_ref[...], b_ref[...], NN)

        @pl.when(kk == nk - 1)
        def _():
            z = ALPHA * x_ref[...] + acc[...]
            nh, _ = _ln_stats(z)
            z_ref[...] = z
            xo_ref[...] = nh * g_ref[...] + beta_ref[...]

    row = pl.BlockSpec((tm, d), lambda i, kk: (i, 0))
    vec = pl.BlockSpec((1, d), lambda i, kk: (0, 0))
    outs, moved = _call(
        body, name=name, grid=(t // tm, nk),
        in_specs=[pl.BlockSpec((tm, tk), lambda i, kk: (i, kk)), pl.BlockSpec((tk, d), lambda i, kk: (kk, 0)), row, vec, vec],
        out_specs=[row, row],
        out_shape=[jax.ShapeDtypeStruct((t, d), F32), jax.ShapeDtypeStruct((t, d), F32)],
        scratch_shapes=[pltpu.VMEM((tm, d), F32)],
        sem=("parallel", "arbitrary"), args=(a, b, xres, g, beta), comm=comm)
    return outs if comm is None else (outs, moved)


def _ln_bwd(z, g, dres, dbr, name, tm=512):
    t, d = z.shape

    def body(z_ref, g_ref, dres_ref, dbr_ref, dz_ref, dg_ref, db_ref):
        @pl.when(pl.program_id(0) == 0)
        def _():
            dg_ref[...] = jnp.zeros_like(dg_ref)
            db_ref[...] = jnp.zeros_like(db_ref)

        nh, r = _ln_stats(z_ref[...])
        dy = ALPHA * dres_ref[...] + dbr_ref[...].astype(F32)
        dg_ref[...] += _colsum(dy * nh)
        db_ref[...] += _colsum(dy)
        dz_ref[...] = _ln_bwd_rows(dy * g_ref[...], nh, r)

    row = pl.BlockSpec((tm, d), lambda i: (i, 0))
    vec = pl.BlockSpec((1, d), lambda i: (0, 0))
    return pl.pallas_call(
        body, name=name, grid=(t // tm,), in_specs=[row, vec, row, row], out_specs=[row, vec, vec],
        out_shape=[jax.ShapeDtypeStruct((t, d), F32), jax.ShapeDtypeStruct((1, d), F32), jax.ShapeDtypeStruct((1, d), F32)],
        compiler_params=_params("arbitrary"),
    )(z, g, dres, dbr)


def _ln_bwd_loss(z, g, beta, target, name, tm=512):
    t, d = z.shape

    def body(z_ref, g_ref, beta_ref, t_ref, dz_ref, dg_ref, db_ref, loss_ref):
        @pl.when(pl.program_id(0) == 0)
        def _():
            dg_ref[...] = jnp.zeros_like(dg_ref)
            db_ref[...] = jnp.zeros_like(db_ref)
            loss_ref[...] = jnp.zeros_like(loss_ref)

        nh, r = _ln_stats(z_ref[...])
        err = nh * g_ref[...] + beta_ref[...] - t_ref[...]
        loss_ref[...] += _colsum(err * err)
        dy = err * (1.0 / d)
        dg_ref[...] += _colsum(dy * nh)
        db_ref[...] += _colsum(dy)
        dz_ref[...] = _ln_bwd_rows(dy * g_ref[...], nh, r)

    row = pl.BlockSpec((tm, d), lambda i: (i, 0))
    vec = pl.BlockSpec((1, d), lambda i: (0, 0))
    vshape = jax.ShapeDtypeStruct((1, d), F32)
    return pl.pallas_call(
        body, name=name, grid=(t // tm,), in_specs=[row, vec, vec, row], out_specs=[row, vec, vec, vec],
        out_shape=[jax.ShapeDtypeStruct((t, d), F32), vshape, vshape, vshape],
        compiler_params=_params("arbitrary"),
    )(z, g, beta, target)


FFN_HALO = 16
FFN_CHUNK = 256
LANES = 128
SUBLANES = 8


def _fold(x):
    return jnp.sum(x.reshape(x.shape[0] // SUBLANES, SUBLANES, x.shape[1]), axis=0)


def _ffn_mid_fwd(h, cw, cb, name, tm=1024, tc=256, comm=None):
    t, f2 = h.shape
    tm = min(tm, t)
    f = f2 // 2
    nj, nt, hb = f // tc, t // tm, tm // FFN_HALO

    ch = min(FFN_CHUNK, tm)

    def body(hg, hgp, hv, hvp, cwg, cwv, cbg, cbv, u_ref, sg, sv):
        i = pl.program_id(1)
        for main, prev, s in ((hg, hgp, sg), (hv, hvp, sv)):
            s[0:FFN_HALO, :] = jnp.where(i > 0, prev[...].astype(F32), 0.0)
            s[FFN_HALO:, :] = main[...].astype(F32)
        o = SUBLANES - FFN_KERNEL + 1
        for lg in range(tc // LANES):
            cols = slice(lg * LANES, (lg + 1) * LANES)
            wg, wv = [cwg[k:k + 1, cols] for k in range(FFN_KERNEL)], [cwv[k:k + 1, cols] for k in range(FFN_KERNEL)]
            bg, bv = cbg[:, cols], cbv[:, cols]

            def chunk(c, carry):
                base = pl.multiple_of(c * ch, ch)
                eg = sg[pl.ds(base + FFN_HALO - SUBLANES, ch + SUBLANES), cols]
                ev = sv[pl.ds(base + FFN_HALO - SUBLANES, ch + SUBLANES), cols]
                cg = wg[0] * eg[o:o + ch] + wg[1] * eg[o + 1:o + 1 + ch] + wg[2] * eg[o + 2:o + 2 + ch] + bg
                cv = wv[0] * ev[o:o + ch] + wv[1] * ev[o + 1:o + 1 + ch] + wv[2] * ev[o + 2:o + 2 + ch] + bv
                u_ref[pl.ds(base, ch), cols] = (_gelu(cg) * cv).astype(BF16)
                return carry

            lax.fori_loop(0, tm // ch, chunk, 0)

    def main_spec(off):
        return pl.BlockSpec((tm, tc), lambda j, i: (i, j + off))

    def prev_spec(off):
        return pl.BlockSpec((FFN_HALO, tc), lambda j, i: (jnp.maximum(i * hb - 1, 0), j + off))

    def par_spec(rows, off):
        return pl.BlockSpec((rows, tc), lambda j, i: (0, j + off))

    (u,), moved = _call(
        body, name=name, grid=(nj, nt),
        in_specs=[main_spec(0), prev_spec(0), main_spec(nj), prev_spec(nj),
                  par_spec(FFN_KERNEL, 0), par_spec(FFN_KERNEL, nj), par_spec(1, 0), par_spec(1, nj)],
        out_specs=[pl.BlockSpec((tm, tc), lambda j, i: (i, j))],
        out_shape=[jax.ShapeDtypeStruct((t, f), BF16)],
        scratch_shapes=[pltpu.VMEM((tm + FFN_HALO, tc), F32), pltpu.VMEM((tm + FFN_HALO, tc), F32)],
        sem=("parallel", "arbitrary"), args=(h, h, h, h, cw, cw, cb, cb), comm=comm)
    return u if comm is None else (u, moved)


def _ffn_mid_bwd(h, du, cw, cb, name, tm=1024, tc=256, comm=None):
    t, f2 = h.shape
    tm = min(tm, t)
    f = f2 // 2
    nj, nt, hb = f // tc, t // tm, tm // FFN_HALO

    ch = min(FFN_CHUNK, tm)
    ahead = ch + SUBLANES

    def body(hg, hgp, hgn, hv, hvp, hvn, du_ref, dun_ref, cwg, cwv, cbg, cbv,
             dhg_ref, dhv_ref, dcwg_ref, dcwv_ref, dcbg_ref, dcbv_ref, sg, sv, sdu):
        i = pl.program_id(1)

        @pl.when(i == 0)
        def _():
            for ref in (dcwg_ref, dcwv_ref, dcbg_ref, dcbv_ref):
                ref[...] = jnp.zeros_like(ref)

        for main, prev, nxt, s in ((hg, hgp, hgn, sg), (hv, hvp, hvn, sv)):
            s[0:FFN_HALO, :] = jnp.where(i > 0, prev[...].astype(F32), 0.0)
            s[FFN_HALO:FFN_HALO + tm, :] = main[...].astype(F32)
            s[FFN_HALO + tm:, :] = nxt[...].astype(F32)
        sdu[0:tm, :] = du_ref[...].astype(F32)
        sdu[tm:, :] = jnp.where(i < nt - 1, dun_ref[...].astype(F32), 0.0)
        o = SUBLANES - FFN_KERNEL + 1
        for lg in range(tc // LANES):
            cols = slice(lg * LANES, (lg + 1) * LANES)
            wg, wv = [cwg[k:k + 1, cols] for k in range(FFN_KERNEL)], [cwv[k:k + 1, cols] for k in range(FFN_KERNEL)]
            bg, bv = cbg[:, cols], cbv[:, cols]

            def chunk(c, acc):
                base = pl.multiple_of(c * ch, ch)
                eg = sg[pl.ds(base + FFN_HALO - SUBLANES, ahead + SUBLANES), cols]
                ev = sv[pl.ds(base + FFN_HALO - SUBLANES, ahead + SUBLANES), cols]
                hgs = [eg[o + k:o + k + ahead] for k in range(FFN_KERNEL)]
                hvs = [ev[o + k:o + k + ahead] for k in range(FFN_KERNEL)]
                cg = wg[0] * hgs[0] + wg[1] * hgs[1] + wg[2] * hgs[2] + bg
                cv = wv[0] * hvs[0] + wv[1] * hvs[1] + wv[2] * hvs[2] + bv
                du_e = sdu[pl.ds(base, ahead), cols]
                gl, dgl = _gelu_and_grad(cg)

                def back(d, hs, w, dh_ref):
                    own = d[0:ch]
                    dh = w[2] * own + w[1] * d[1:1 + ch] + w[0] * d[2:2 + ch]
                    dh_ref[pl.ds(base, ch), cols] = dh.astype(BF16)
                    return [_fold(own)] + [_fold(own * hs[k][0:ch]) for k in range(FFN_KERNEL)]

                sums = back(du_e * cv * dgl, hgs, wg, dhg_ref) + back(du_e * gl, hvs, wv, dhv_ref)
                return tuple(a + s_ for a, s_ in zip(acc, sums))

            zero = jnp.zeros((SUBLANES, LANES), F32)
            acc = lax.fori_loop(0, tm // ch, chunk, (zero,) * (2 * (1 + FFN_KERNEL)))
            dcbg_ref[:, cols] += _colsum(acc[0])
            dcbv_ref[:, cols] += _colsum(acc[1 + FFN_KERNEL])
            for k in range(FFN_KERNEL):
                dcwg_ref[k:k + 1, cols] += _colsum(acc[1 + k])
                dcwv_ref[k:k + 1, cols] += _colsum(acc[2 + FFN_KERNEL + k])

    last_blk = t // FFN_HALO - 1

    def main_spec(off):
        return pl.BlockSpec((tm, tc), lambda j, i: (i, j + off))

    def prev_spec(off):
        return pl.BlockSpec((FFN_HALO, tc), lambda j, i: (jnp.maximum(i * hb - 1, 0), j + off))

    def next_spec(off):
        return pl.BlockSpec((FFN_HALO, tc), lambda j, i: (jnp.minimum((i + 1) * hb, last_blk), j + off))

    def par_spec(rows, off):
        return pl.BlockSpec((rows, tc), lambda j, i: (0, j + off))

    out_tile = pl.BlockSpec((tm, tc), lambda j, i: (i, j))
    outs, moved = _call(
        body, name=name, grid=(nj, nt),
        in_specs=[main_spec(0), prev_spec(0), next_spec(0), main_spec(nj), prev_spec(nj), next_spec(nj),
                  main_spec(0), next_spec(0),
                  par_spec(FFN_KERNEL, 0), par_spec(FFN_KERNEL, nj), par_spec(1, 0), par_spec(1, nj)],
        out_specs=[out_tile, out_tile, par_spec(FFN_KERNEL, 0), par_spec(FFN_KERNEL, 0), par_spec(1, 0), par_spec(1, 0)],
        out_shape=[jax.ShapeDtypeStruct((t, f), BF16), jax.ShapeDtypeStruct((t, f), BF16),
                   jax.ShapeDtypeStruct((FFN_KERNEL, f), F32), jax.ShapeDtypeStruct((FFN_KERNEL, f), F32),
                   jax.ShapeDtypeStruct((1, f), F32), jax.ShapeDtypeStruct((1, f), F32)],
        scratch_shapes=[pltpu.VMEM((tm + 2 * FFN_HALO, tc), F32), pltpu.VMEM((tm + 2 * FFN_HALO, tc), F32),
                        pltpu.VMEM((tm + FFN_HALO, tc), F32)],
        sem=("parallel", "arbitrary"), args=(h, h, h, h, h, h, du, du, cw, cw, cb, cb), comm=comm)
    return outs if comm is None else (outs, moved)


MIX_HALO = 32


def _glu(hh):
    return hh[:, 0:A_WIDTH] * _sigmoid(hh[:, A_WIDTH:2 * A_WIDTH])


def _fill_row_shifts(s):
    rows = s.shape[1] - SUBLANES
    for j in range(1, SUBLANES):
        s[j, 0:rows, :] = s[0, pl.ds(j, rows), :]


def _rows_from(s, start, rows):
    j = start % SUBLANES
    return s[j, start - j:start - j + rows, :]


def _tril_mask():
    return lax.broadcasted_iota(jnp.int32, (B_CHUNK, B_CHUNK), 0) >= lax.broadcasted_iota(jnp.int32, (B_CHUNK, B_CHUNK), 1)


def _spatial_mix(q, ms_ref, sbt_ref, tm):
    mask = _tril_mask()
    ws = [jnp.where(mask, ms_ref[g], 0.0).astype(BF16) for g in range(B_GROUPS)]
    qb = q.astype(BF16)
    rows = []
    for c in range(tm // B_CHUNK):
        cols = [_dot(ws[g], qb[c * B_CHUNK:(c + 1) * B_CHUNK, g * 128:(g + 1) * 128], NN) + sbt_ref[:, g:g + 1]
                for g in range(B_GROUPS)]
        rows.append(jnp.concatenate(cols, axis=1))
    return jnp.concatenate(rows, axis=0)


def _mixer_mid_fwd(h, cw, cb, ag, ab, bg, bb, ms, sbt, name, tm=256, comm=None):
    t = h.shape[0]
    nt, hb = t // tm, tm // MIX_HALO
    o = MIX_HALO - A_KERNEL + 1

    def body(h_ref, hp_ref, cw_ref, cb_ref, ag_ref, ab_ref, bg_ref, bb_ref, ms_ref, sbt_ref, cat_ref, sp):
        i = pl.program_id(0)
        sp[0, 0:MIX_HALO, :] = jnp.where(i > 0, _glu(hp_ref[:, 0:2 * A_WIDTH].astype(F32)), 0.0)
        sp[0, MIX_HALO:, :] = _glu(h_ref[:, 0:2 * A_WIDTH].astype(F32))
        _fill_row_shifts(sp)
        y = jnp.zeros((tm, A_WIDTH), F32) + cb_ref[...]
        for k in range(A_KERNEL):
            y = y + cw_ref[k:k + 1, :] * _rows_from(sp, o + k, tm)
        nh, _ = _ln_stats(y)
        ln = nh * ag_ref[...] + ab_ref[...]
        cat_ref[:, 0:A_WIDTH] = (ln * _sigmoid(ln)).astype(BF16)
        u = _gelu(h_ref[:, 1024:1536].astype(F32))
        nb, _ = _ln_stats(_gelu(h_ref[:, 1536:2048].astype(F32)))
        mixed = _spatial_mix(nb * bg_ref[...] + bb_ref[...], ms_ref, sbt_ref, tm)
        cat_ref[:, A_WIDTH:] = (u * mixed).astype(BF16)

    vec = pl.BlockSpec((1, A_WIDTH), lambda i: (0, 0))
    (cat,), moved = _call(
        body, name=name, grid=(nt,),
        in_specs=[pl.BlockSpec((tm, 2048), lambda i: (i, 0)),
                  pl.BlockSpec((MIX_HALO, 2048), lambda i: (jnp.maximum(i * hb - 1, 0), 0)),
                  pl.BlockSpec((A_KERNEL, A_WIDTH), lambda i: (0, 0)), vec, vec, vec, vec, vec,
                  pl.BlockSpec((B_GROUPS, B_CHUNK, B_CHUNK), lambda i: (0, 0, 0)),
                  pl.BlockSpec((B_CHUNK, B_GROUPS), lambda i: (0, 0))],
        out_specs=[pl.BlockSpec((tm, D_MODEL), lambda i: (i, 0))],
        out_shape=[jax.ShapeDtypeStruct((t, D_MODEL), BF16)],
        scratch_shapes=[pltpu.VMEM((SUBLANES, tm + MIX_HALO, A_WIDTH), F32)],
        sem=("parallel",), args=(h, h, cw, cb, ag, ab, bg, bb, ms, sbt), comm=comm)
    return cat if comm is None else (cat, moved)


def _mixer_mid_bwd(h, dcat, cw, cb, ag, ab, bg, bb, ms, mst, sbt, name, tm=256, comm=None):
    t = h.shape[0]
    nt, hb = t // tm, tm // MIX_HALO
    o = MIX_HALO - A_KERNEL + 1
    r = tm + MIX_HALO
    nchunk = tm // B_CHUNK

    def body(h_ref, hp_ref, hn_ref, dc_ref, dcn_ref, cw_ref, cb_ref, ag_ref, ab_ref, bg_ref, bb_ref, ms_ref, mst_ref, sbt_ref,
             dh_ref, dcw_ref, dcb_ref, dag_ref, dab_ref, dbg_ref, dbb_ref, dms_ref, dsb_ref, sp, sdy, sbacc):
        i = pl.program_id(0)

        @pl.when(i == 0)
        def _():
            for ref in (dcw_ref, dcb_ref, dag_ref, dab_ref, dbg_ref, dbb_ref, dms_ref, dsb_ref, sbacc):
                ref[...] = jnp.zeros_like(ref)

        sp[0, 0:MIX_HALO, :] = jnp.where(i > 0, _glu(hp_ref[:, 0:2 * A_WIDTH].astype(F32)), 0.0)
        sp[0, MIX_HALO:MIX_HALO + tm, :] = _glu(h_ref[:, 0:2 * A_WIDTH].astype(F32))
        sp[0, MIX_HALO + tm:, :] = _glu(hn_ref[:, 0:2 * A_WIDTH].astype(F32))
        _fill_row_shifts(sp)
        y = jnp.zeros((r, A_WIDTH), F32) + cb_ref[...]
        for k in range(A_KERNEL):
            y = y + cw_ref[k:k + 1, :] * _rows_from(sp, o + k, r)
        nh, rs = _ln_stats(y)
        ln = nh * ag_ref[...] + ab_ref[...]
        sg = _sigmoid(ln)
        dao = jnp.concatenate([dc_ref[:, 0:A_WIDTH].astype(F32),
                               jnp.where(i < nt - 1, dcn_ref[:, 0:A_WIDTH].astype(F32), 0.0)], axis=0)
        dln = dao * (sg * (1.0 + ln * (1.0 - sg)))
        dag_ref[...] += _colsum(dln[0:tm] * nh[0:tm])
        dab_ref[...] += _colsum(dln[0:tm])
        sdy[0] = _ln_bwd_rows(dln * ag_ref[...], nh, rs)
        _fill_row_shifts(sdy)
        dy_own = sdy[0, 0:tm, :]
        dcb_ref[...] += _colsum(dy_own)
        dp = jnp.zeros((tm, A_WIDTH), F32)
        for k in range(A_KERNEL):
            dcw_ref[k:k + 1, :] += _colsum(dy_own * _rows_from(sp, o + k, tm))
            dp = dp + cw_ref[k:k + 1, :] * _rows_from(sdy, A_KERNEL - 1 - k, tm)
        av = h_ref[:, 0:A_WIDTH].astype(F32)
        s = _sigmoid(h_ref[:, A_WIDTH:2 * A_WIDTH].astype(F32))
        dh_ref[:, 0:A_WIDTH] = (dp * s).astype(BF16)
        dh_ref[:, A_WIDTH:2 * A_WIDTH] = (dp * av * s * (1.0 - s)).astype(BF16)

        u, dgu = _gelu_and_grad(h_ref[:, 1024:1536].astype(F32))
        w, dgw = _gelu_and_grad(h_ref[:, 1536:2048].astype(F32))
        nb, rb = _ln_stats(w)
        q = nb * bg_ref[...] + bb_ref[...]
        mixed = _spatial_mix(q, ms_ref, sbt_ref, tm)
        dbo = dc_ref[:, A_WIDTH:].astype(F32)
        dh_ref[:, 1024:1536] = (dbo * mixed * dgu).astype(BF16)
        dmx = dbo * u
        mask = _tril_mask()
        wst = [jnp.where(mask.T, mst_ref[g], 0.0).astype(BF16) for g in range(B_GROUPS)]
        qb = q.astype(BF16)
        dmb = dmx.astype(BF16)
        rows = []
        for c in range(nchunk):
            cols = []
            for g in range(B_GROUPS):
                rs_, cs_ = slice(c * B_CHUNK, (c + 1) * B_CHUNK), slice(g * 128, (g + 1) * 128)
                sbacc[g] += dmx[rs_, cs_]
                dms_ref[g] += _dot(dmb[rs_, cs_], qb[rs_, cs_], NT)
                cols.append(_dot(wst[g], dmb[rs_, cs_], NN))
            rows.append(jnp.concatenate(cols, axis=1))
        dq = jnp.concatenate(rows, axis=0)
        dbg_ref[...] += _colsum(dq * nb)
        dbb_ref[...] += _colsum(dq)
        dh_ref[:, 1536:2048] = (_ln_bwd_rows(dq * bg_ref[...], nb, rb) * dgw).astype(BF16)

        @pl.when(i == nt - 1)
        def _():
            for g in range(B_GROUPS):
                dms_ref[g] = jnp.where(mask, dms_ref[g], 0.0)
                dsb_ref[g] = jnp.sum(sbacc[g], axis=1, keepdims=True)

    last_blk = t // MIX_HALO - 1
    vec = pl.BlockSpec((1, A_WIDTH), lambda i: (0, 0))
    mat = pl.BlockSpec((B_GROUPS, B_CHUNK, B_CHUNK), lambda i: (0, 0, 0))
    taps = pl.BlockSpec((A_KERNEL, A_WIDTH), lambda i: (0, 0))

    def halo(width, which):
        if which == "prev":
            return pl.BlockSpec((MIX_HALO, width), lambda i: (jnp.maximum(i * hb - 1, 0), 0))
        return pl.BlockSpec((MIX_HALO, width), lambda i: (jnp.minimum((i + 1) * hb, last_blk), 0))

    vshape = jax.ShapeDtypeStruct((1, A_WIDTH), F32)
    outs, moved = _call(
        body, name=name, grid=(nt,),
        in_specs=[pl.BlockSpec((tm, 2048), lambda i: (i, 0)), halo(2048, "prev"), halo(2048, "next"),
                  pl.BlockSpec((tm, D_MODEL), lambda i: (i, 0)), halo(D_MODEL, "next"),
                  taps, vec, vec, vec, vec, vec, mat, mat, pl.BlockSpec((B_CHUNK, B_GROUPS), lambda i: (0, 0))],
        out_specs=[pl.BlockSpec((tm, 2048), lambda i: (i, 0)), taps, vec, vec, vec, vec, vec, mat,
                   pl.BlockSpec((B_GROUPS, B_CHUNK, 1), lambda i: (0, 0, 0))],
        out_shape=[jax.ShapeDtypeStruct((t, 2048), BF16), jax.ShapeDtypeStruct((A_KERNEL, A_WIDTH), F32),
                   vshape, vshape, vshape, vshape, vshape,
                   jax.ShapeDtypeStruct((B_GROUPS, B_CHUNK, B_CHUNK), F32), jax.ShapeDtypeStruct((B_GROUPS, B_CHUNK, 1), F32)],
        scratch_shapes=[pltpu.VMEM((SUBLANES, tm + 2 * MIX_HALO, A_WIDTH), F32), pltpu.VMEM((SUBLANES, r, A_WIDTH), F32),
                        pltpu.VMEM((B_GROUPS, B_CHUNK, B_CHUNK), F32)],
        sem=("arbitrary",), args=(h, h, h, dcat, dcat, cw, cb, ag, ab, bg, bb, ms, mst, sbt), comm=comm)
    return outs if comm is None else (outs, moved)


Q_WIDTH = N_Q_HEADS * HEAD_DIM
KV_WIDTH = 2 * N_KV_HEADS * HEAD_DIM
PAIRS_PER_KV = N_Q_HEADS // N_KV_HEADS // 2
ATT_SCALE = 1.0 / math.sqrt(HEAD_DIM)


def _att_mask(n):
    qi = lax.broadcasted_iota(jnp.int32, (ATT_BLOCK, 2 * ATT_BLOCK), 0)
    sj = lax.broadcasted_iota(jnp.int32, (ATT_BLOCK, 2 * ATT_BLOCK), 1)
    diff = qi + ATT_BLOCK - sj
    return (diff >= 0) & (diff < ATT_BLOCK) & ((n > 0) | (sj >= ATT_BLOCK))


def _dup_heads(pair_cols, kv_head):
    lane = lax.broadcasted_iota(jnp.int32, pair_cols.shape, 1)
    rolled = pltpu.roll(pair_cols, HEAD_DIM, 1)
    first = lane < HEAD_DIM
    return jnp.where(first, pair_cols, rolled) if kv_head == 0 else jnp.where(first, rolled, pair_cols)


HEADS_PER_KV = N_Q_HEADS // N_KV_HEADS


def _stack_heads(ref, kh):
    lane = lax.broadcasted_iota(jnp.int32, (ATT_BLOCK, 128), 1)
    rows = []
    for pr in range(PAIRS_PER_KV):
        c0 = (kh * PAIRS_PER_KV + pr) * 128
        pair = ref[:, c0:c0 + 128]
        rows += [jnp.where(lane < HEAD_DIM, pair, jnp.zeros_like(pair)), jnp.where(lane < HEAD_DIM, jnp.zeros_like(pair), pair)]
    return jnp.concatenate(rows, axis=0)


def _unstack_heads(stacked, kh, write):
    lane = lax.broadcasted_iota(jnp.int32, (ATT_BLOCK, 128), 1)
    for pr in range(PAIRS_PER_KV):
        first = stacked[(2 * pr) * ATT_BLOCK:(2 * pr + 1) * ATT_BLOCK]
        second = stacked[(2 * pr + 1) * ATT_BLOCK:(2 * pr + 2) * ATT_BLOCK]
        write((kh * PAIRS_PER_KV + pr) * 128, jnp.where(lane < HEAD_DIM, first, second))


def _sink_column(sink_ref, kh):
    return jnp.concatenate([jnp.full((ATT_BLOCK, 1), sink_ref[0, kh * HEADS_PER_KV + h], F32) for h in range(HEADS_PER_KV)], axis=0)


def _att_probs(q_rows, k2, mask, sink, heads):
    s = _dot(q_rows, k2, NT) * ATT_SCALE
    s = jnp.where(jnp.concatenate([mask] * heads, axis=0), s, -jnp.inf)
    m = jnp.maximum(jnp.max(s, axis=-1, keepdims=True), sink)
    e = jnp.exp(s - m)
    es = jnp.exp(sink - m)
    inv = 1.0 / (jnp.sum(e, axis=-1, keepdims=True) + es)
    return e * inv, es * inv


def _attn_fwd(qkv, sinks, name, comm=None):
    t = qkv.shape[0]
    nb = t // ATT_BLOCK
    kvb = Q_WIDTH // KV_WIDTH

    def body(sink_ref, q_ref, kv_ref, kvp_ref, o_ref):
        n = pl.program_id(0)
        mask = _att_mask(n)
        kv = jnp.concatenate([kvp_ref[...], kv_ref[...]], axis=0).astype(F32)

        lane = lax.broadcasted_iota(jnp.int32, (ATT_BLOCK, 128), 1)
        for kh in range(N_KV_HEADS):
            k2 = _dup_heads(kv[:, 0:128], kh).astype(BF16)
            v2 = _dup_heads(kv[:, 128:256], kh).astype(BF16)
            for pr in range(PAIRS_PER_KV):
                c0 = (kh * PAIRS_PER_KV + pr) * 128
                q2 = q_ref[:, c0:c0 + 128]
                outs = []
                for half in range(2):
                    head = (kh * PAIRS_PER_KV + pr) * 2 + half
                    qm = jnp.where((lane < HEAD_DIM) == (half == 0), q2, jnp.zeros_like(q2))
                    p, _ = _att_probs(qm, k2, mask, sink_ref[0, head], 1)
                    outs.append(_dot(p, v2, NN))
                o_ref[:, c0:c0 + 128] = jnp.where(lane < HEAD_DIM, outs[0], outs[1]).astype(BF16)

    (out,), moved = _call(
        body, name=name, grid=(nb,),
        in_specs=[pl.BlockSpec(memory_space=pltpu.SMEM),
                  pl.BlockSpec((ATT_BLOCK, Q_WIDTH), lambda n: (n, 0)),
                  pl.BlockSpec((ATT_BLOCK, KV_WIDTH), lambda n: (n, kvb)),
                  pl.BlockSpec((ATT_BLOCK, KV_WIDTH), lambda n: (jnp.maximum(n - 1, 0), kvb))],
        out_specs=[pl.BlockSpec((ATT_BLOCK, Q_WIDTH), lambda n: (n, 0))],
        out_shape=[jax.ShapeDtypeStruct((t, Q_WIDTH), BF16)],
        sem=("parallel",), args=(sinks, qkv, qkv, qkv), comm=comm)
    return out if comm is None else (out, moved)


def _attn_bwd(qkv, d_o, sinks, name, comm=None):
    t = qkv.shape[0]
    nb = t // ATT_BLOCK
    kvb = Q_WIDTH // KV_WIDTH

    def body(sink_ref, q_ref, kv_ref, kvp_ref, do_ref, dq_ref, dkv_ref, dbq_ref, dbkv_ref, dsink_ref, carry):
        n = pl.program_id(0)

        @pl.when(n == 0)
        def _():
            for ref in (dbq_ref, dbkv_ref, dsink_ref, carry):
                ref[...] = jnp.zeros_like(ref)
            dkv_ref[...] = jnp.zeros_like(dkv_ref)

        @pl.when(n < nb)
        def _():
            mask = _att_mask(n)
            kv = jnp.concatenate([kvp_ref[...], kv_ref[...]], axis=0).astype(F32)
            lane2 = lax.broadcasted_iota(jnp.int32, (2 * ATT_BLOCK, 128), 1)
            sink_lane = lax.broadcasted_iota(jnp.int32, (1, 128), 1)
            dsink = jnp.zeros((1, 128), F32)
            dk_parts, dv_parts = [], []

            def write(c0, pair):
                dbq_ref[:, c0:c0 + 128] += _colsum(pair)
                dq_ref[:, c0:c0 + 128] = pair.astype(BF16)

            for kh in range(N_KV_HEADS):
                k2 = _dup_heads(kv[:, 0:128], kh).astype(BF16)
                v2 = _dup_heads(kv[:, 128:256], kh).astype(BF16)
                q_all = _stack_heads(q_ref, kh)
                do_all = _stack_heads(do_ref, kh)
                p, ps = _att_probs(q_all, k2, mask, _sink_column(sink_ref, kh), HEADS_PER_KV)
                dp = _dot(do_all, v2, NT)
                delta = jnp.sum(p * dp, axis=-1, keepdims=True)
                ds = p * (dp - delta) * ATT_SCALE
                psd = ps * delta
                for h in range(HEADS_PER_KV):
                    dsink = dsink + jnp.where(sink_lane == kh * HEADS_PER_KV + h,
                                              -jnp.sum(psd[h * ATT_BLOCK:(h + 1) * ATT_BLOCK]), 0.0)
                _unstack_heads(_dot(ds, k2, NN), kh, write)
                dk_acc = _dot(ds, q_all, TN)
                dv_acc = _dot(p, do_all, TN)
                dk_parts.append(dk_acc + pltpu.roll(dk_acc, HEAD_DIM, 1))
                dv_parts.append(dv_acc + pltpu.roll(dv_acc, HEAD_DIM, 1))
            dk = jnp.where(lane2 < HEAD_DIM, dk_parts[0], dk_parts[1])
            dv = jnp.where(lane2 < HEAD_DIM, dv_parts[0], dv_parts[1])
            dkv_new = jnp.concatenate([dk, dv], axis=1)
            done = carry[...] + dkv_new[0:ATT_BLOCK]

            @pl.when(n > 0)
            def _():
                dkv_ref[...] = done.astype(BF16)
                dbkv_ref[...] += _colsum(done)

            carry[...] = dkv_new[ATT_BLOCK:]
            dsink_ref[...] += dsink

        @pl.when(n == nb)
        def _():
            dkv_ref[...] = carry[...].astype(BF16)
            dbkv_ref[...] += _colsum(carry[...])

    def clamp(n):
        return jnp.minimum(n, nb - 1)

    outs, moved = _call(
        body, name=name, grid=(nb + 1,),
        in_specs=[pl.BlockSpec(memory_space=pltpu.SMEM),
                  pl.BlockSpec((ATT_BLOCK, Q_WIDTH), lambda n: (clamp(n), 0)),
                  pl.BlockSpec((ATT_BLOCK, KV_WIDTH), lambda n: (clamp(n), kvb)),
                  pl.BlockSpec((ATT_BLOCK, KV_WIDTH), lambda n: (jnp.maximum(clamp(n) - 1, 0), kvb)),
                  pl.BlockSpec((ATT_BLOCK, Q_WIDTH), lambda n: (clamp(n), 0))],
        out_specs=[pl.BlockSpec((ATT_BLOCK, Q_WIDTH), lambda n: (clamp(n), 0)),
                   pl.BlockSpec((ATT_BLOCK, KV_WIDTH), lambda n: (jnp.maximum(n - 1, 0), 0)),
                   pl.BlockSpec((1, Q_WIDTH), lambda n: (0, 0)),
                   pl.BlockSpec((1, KV_WIDTH), lambda n: (0, 0)),
                   pl.BlockSpec((1, 128), lambda n: (0, 0))],
        out_shape=[jax.ShapeDtypeStruct((t, Q_WIDTH), BF16), jax.ShapeDtypeStruct((t, KV_WIDTH), BF16),
                   jax.ShapeDtypeStruct((1, Q_WIDTH), F32), jax.ShapeDtypeStruct((1, KV_WIDTH), F32),
                   jax.ShapeDtypeStruct((1, 128), F32)],
        scratch_shapes=[pltpu.VMEM((ATT_BLOCK, KV_WIDTH), F32)],
        sem=("arbitrary",), args=(sinks, qkv, qkv, qkv, d_o), comm=comm)
    return outs if comm is None else (outs, moved)


def _adamw_math(g, w, m, v):
    m = ADAM_B1 * m + (1.0 - ADAM_B1) * g
    v = ADAM_B2 * v + (1.0 - ADAM_B2) * (g * g)
    m_hat = m / (1.0 - ADAM_B1 ** ADAM_STEP)
    v_hat = v / (1.0 - ADAM_B2 ** ADAM_STEP)
    delta = -ADAM_LR * (m_hat / (jnp.sqrt(v_hat) + ADAM_EPS) + ADAM_WD * w)
    return delta, m, v


def _sum_partials(p_ref):
    g = p_ref[0].astype(F32)
    for s in range(1, N_DEV):
        g = g + p_ref[s].astype(F32)
    return g


def _adamw_big(parts, w, m, v, name, tr):
    r, c = w.shape
    tiles = [p.shape[1] // tr for p in parts]
    starts = [sum(tiles[:l]) for l in range(len(parts))]
    assert all(p.shape[1] % tr == 0 for p in parts) and sum(tiles) * tr == r

    def body(*refs):
        p_refs, (w_ref, m_ref, v_ref, g_out, d_out, m_out, v_out) = refs[:len(parts)], refs[len(parts):]
        i = pl.program_id(0)
        for l, p_ref in enumerate(p_refs):
            @pl.when((i >= starts[l]) & (i < starts[l] + tiles[l]))
            def _():
                g = _sum_partials(p_ref)
                g_out[...] = g
                d_out[...], m_out[...], v_out[...] = _adamw_math(g, w_ref[...], m_ref[...], v_ref[...])

    def part_spec(l):
        return pl.BlockSpec((N_DEV, tr, c), lambda i: (0, jnp.clip(i - starts[l], 0, tiles[l] - 1), 0))

    tile = pl.BlockSpec((tr, c), lambda i: (i, 0))
    shape = jax.ShapeDtypeStruct((r, c), F32)
    return pl.pallas_call(
        body, name=name, grid=(r // tr,),
        in_specs=[part_spec(l) for l in range(len(parts))] + [tile, tile, tile],
        out_specs=[tile] * 4, out_shape=[shape] * 4,
        compiler_params=_params("parallel"),
    )(*parts, w, m, v)


def _adamw_small(parts, ws, ms, vs, name):
    n = len(ws)

    def body(*refs):
        ins, outs = refs[:4 * n], refs[4 * n:]
        for a in range(n):
            g = _sum_partials(ins[a])
            outs[4 * a][...] = g
            outs[4 * a + 1][...], outs[4 * a + 2][...], outs[4 * a + 3][...] = _adamw_math(
                g, ins[n + a][...], ins[2 * n + a][...], ins[3 * n + a][...])

    out_shape = []
    for w in ws:
        out_shape += [jax.ShapeDtypeStruct(w.shape, F32)] * 4
    return pl.pallas_call(body, name=name, out_shape=out_shape, compiler_params=_params())(*parts, *ws, *ms, *vs)


PACK_LANES = 128
PACK_ROWS = 8


def _pack(arrs):
    flat = jnp.concatenate([a.reshape(-1).astype(F32) for a in arrs])
    unit = PACK_LANES * PACK_ROWS
    total = -(-flat.shape[0] // unit) * unit
    return jnp.pad(flat, (0, total - flat.shape[0])).reshape(-1, PACK_LANES)


def _unpack(buf, shapes):
    flat = buf.reshape(N_DEV, -1)
    out, pos = [], 0
    for s in shapes:
        size = math.prod(s)
        out.append(flat[:, pos:pos + size].reshape((N_DEV,) + tuple(s)))
        pos += size
    return out


def _interleave(g):
    return jnp.transpose(g, (1, 0, 2)).reshape(g.shape[1], -1)


def _deinterleave(w):
    r = w.shape[0]
    return jnp.transpose(w.reshape(r, N_DEV, -1), (1, 0, 2))


def _ffn_backward(dz, x_in, h, u, w_up_t, cw, cb, w_down, tag, exchange=()):
    du = _matmul(dz, w_down, "nt", BF16, f"ffn{tag}_du", 1024, 1408, 1024)
    d_w_down = _matmul(u, dz, "tn", BF16, f"ffn{tag}_dwdown", 1408, 1024, 512)
    (dhg, dhv, dcwg, dcwv, dcbg, dcbv), moved = _ffn_mid_bwd(
        h, du, cw, cb, f"ffn{tag}_mid_bwd", comm=_Comm(exchange=[d_w_down.reshape(N_DEV, -1, D_MODEL), *exchange]))
    d_w_up_t = _matmul_tn_pair(dhg, dhv, x_in, BF16, f"ffn{tag}_dwup", 1408, 1024, 1024)
    dx = _matmul(dhv, w_up_t, "nn", F32, f"ffn{tag}_dx_value", 1024, 1024, 1408, b_off=D_FF // 1408)
    dx = _matmul(dhg, w_up_t, "nn", F32, f"ffn{tag}_dx_gate", 1024, 1024, 1408, res=dx)
    return (dx, d_w_up_t.reshape(N_DEV, -1, D_MODEL),
            jnp.concatenate([dcwg, dcwv], axis=1), jnp.concatenate([dcbg, dcbv], axis=1), moved)


def kernel(x, ab_w_in, a_conv_w, a_conv_b, a_norm_g, a_norm_b, b_norm_g, b_norm_b, b_spatial_w, b_spatial_b, ab_w_out, c_w_qkv, c_b_qkv, c_sinks, c_w_o, ffn_w_up, ffn_conv_w, ffn_conv_b, ffn_w_down, ln_g, ln_b, loss_target, m_ab_w_in, m_a_conv_w, m_a_conv_b, m_a_norm_g, m_a_norm_b, m_b_norm_g, m_b_norm_b, m_b_spatial_w, m_b_spatial_b, m_ab_w_out, m_c_w_qkv, m_c_b_qkv, m_c_sinks, m_c_w_o, m_ffn_w_up, m_ffn_conv_w, m_ffn_conv_b, m_ffn_w_down, m_ln_g, m_ln_b, v_ab_w_in, v_a_conv_w, v_a_conv_b, v_a_norm_g, v_a_norm_b, v_b_norm_g, v_b_norm_b, v_b_spatial_w, v_b_spatial_b, v_ab_w_out, v_c_w_qkv, v_c_b_qkv, v_c_sinks, v_c_w_o, v_ffn_w_up, v_ffn_conv_w, v_ffn_conv_b, v_ffn_w_down, v_ln_g, v_ln_b):
    me = 4 * lax.axis_index("x") + 2 * lax.axis_index("y") + lax.axis_index("c")
    xt = x[0]
    t = xt.shape[0]

    small_shard_shapes = [a_conv_w.shape, c_b_qkv.shape, ffn_conv_w.shape, ln_g.shape, ln_b.shape]
    up_shard = [jnp.swapaxes(ffn_w_up[l], 0, 1).astype(BF16) for l in range(2)]
    qkv_shard = jnp.swapaxes(c_w_qkv[0], 0, 1).astype(BF16)
    down_shard = [ffn_w_down[l].astype(BF16) for l in range(2)]
    g_win, g_small = _comm_only(
        _Comm(gather=[ab_w_in[0].astype(BF16), _pack([a_conv_w, c_b_qkv, ffn_conv_w, ln_g, ln_b])]), "gather_first")
    w_in = _interleave(g_win)
    g_acw, g_bqkv, g_fcw, g_lng, g_lnb = _unpack(g_small, small_shard_shapes)
    acw = _interleave(g_acw[:, 0])
    bqkv = g_bqkv[:, 0].reshape(1, -1)
    fcw = [_interleave(g_fcw[:, l]) for l in range(2)]
    lng = jnp.transpose(g_lng, (1, 2, 0, 3)).reshape(2, 2, 1, D_MODEL)
    lnb = jnp.transpose(g_lnb, (1, 2, 0, 3)).reshape(2, 2, 1, D_MODEL)
    fcb = [ffn_conv_b[l:l + 1] for l in range(2)]
    ms = b_spatial_w[0]
    mst = jnp.swapaxes(ms, 1, 2)
    sbt = b_spatial_b[0].T

    h0, (g_wout,) = _matmul(xt, w_in, "nn", BF16, "mix_in", 1024, 1024, 1024, comm=_Comm(gather=[ab_w_out[0].astype(BF16)]))
    w_out = g_wout.reshape(D_MODEL, D_MODEL)
    cat, (g_wup0,) = _mixer_mid_fwd(h0, acw, a_conv_b, a_norm_g, a_norm_b, b_norm_g, b_norm_b, ms, sbt, "mix_mid_fwd",
                                    comm=_Comm(gather=[up_shard[0]]))
    w_up0 = g_wup0.reshape(2 * D_FF, D_MODEL)
    z1, x1 = _matmul_res_ln(cat, w_out, xt, lng[0, 0], lnb[0, 0], "mix_out_ln", 512, D_MODEL)
    hf0, (g_wdown0, g_wqkv) = _matmul(x1, w_up0, "nt", BF16, "ffn0_up", 1024, 1408, 1024,
                                      comm=_Comm(gather=[down_shard[0], qkv_shard]))
    w_down0 = g_wdown0.reshape(D_FF, D_MODEL)
    w_qkv = g_wqkv.reshape(Q_WIDTH + KV_WIDTH, D_MODEL)
    u0, (g_wup1,) = _ffn_mid_fwd(hf0, fcw[0], fcb[0], "ffn0_mid_fwd", comm=_Comm(gather=[up_shard[1]]))
    w_up1 = g_wup1.reshape(2 * D_FF, D_MODEL)
    (z2, x2), (g_wo,) = _matmul_res_ln(u0, w_down0, x1, lng[0, 1], lnb[0, 1], "ffn0_down_ln", 512, D_FF,
                                       comm=_Comm(gather=[c_w_o[0].astype(BF16)]))
    w_o = g_wo.reshape(D_MODEL, D_MODEL)
    qkv = _matmul(x2, w_qkv, "nt", BF16, "att_qkv", 1024, 1280, 1024, bias=bqkv)
    att, (g_wdown1,) = _attn_fwd(qkv, c_sinks, "att_fwd", comm=_Comm(gather=[down_shard[1]]))
    w_down1 = g_wdown1.reshape(D_FF, D_MODEL)
    z3, x3 = _matmul_res_ln(att, w_o, x2, lng[1, 0], lnb[1, 0], "att_out_ln", 512, D_MODEL)
    hf1 = _matmul(x3, w_up1, "nt", BF16, "ffn1_up", 1024, 1408, 1024)
    u1 = _ffn_mid_fwd(hf1, fcw[1], fcb[1], "ffn1_mid_fwd")
    z4, _ = _matmul_res_ln(u1, w_down1, x3, lng[1, 1], lnb[1, 1], "ffn1_down_ln", 512, D_FF)

    dz4, dg11, db11, loss_terms = _ln_bwd_loss(z4, lng[1, 1], lnb[1, 1], loss_target[0], "loss_ln_bwd")
    dx3, d_wup1, d_fcw1, d_fcb1, (p_wdown1,) = _ffn_backward(dz4, x3, hf1, u1, w_up1, fcw[1], fcb[1], w_down1, 1)
    dz3, dg10, db10 = _ln_bwd(z3, lng[1, 0], dz4, dx3, "ln10_bwd")
    d_att = _matmul(dz3, w_o, "nt", BF16, "att_dout", 1024, 1024, 1024)
    d_wo = _matmul(att, dz3, "tn", BF16, "att_dwo", 1024, 1024, 512)
    (dq, dkv, dbq, dbkv, dsinks), (p_wup1,) = _attn_bwd(qkv, d_att, c_sinks, "att_bwd", comm=_Comm(exchange=[d_wup1]))
    d_wqkv = jnp.concatenate([_matmul(dq, x2, "tn", BF16, "att_dwq", 1024, 1024, 1024),
                              _matmul(dkv, x2, "tn", BF16, "att_dwkv", KV_WIDTH, 1024, 1024)], axis=0)
    dx2 = _matmul(dkv, w_qkv, "nn", F32, "att_dx_kv", 1024, 1024, KV_WIDTH, b_off=Q_WIDTH // KV_WIDTH)
    dx2 = _matmul(dq, w_qkv, "nn", F32, "att_dx_q", 1024, 1024, KV_WIDTH, res=dx2)
    dz2, dg01, db01 = _ln_bwd(z2, lng[0, 1], dz3, dx2, "ln01_bwd")
    dx1, d_wup0, d_fcw0, d_fcb0, (p_wdown0, p_wqkv, p_wo) = _ffn_backward(
        dz2, x1, hf0, u0, w_up0, fcw[0], fcb[0], w_down0, 0,
        exchange=[d_wqkv.reshape(N_DEV, -1, D_MODEL), d_wo.reshape(N_DEV, -1, D_MODEL)])
    dz1, dg00, db00 = _ln_bwd(z1, lng[0, 0], dz2, dx1, "ln00_bwd")
    dcat = _matmul(dz1, w_out, "nt", BF16, "mix_dcat", 1024, 1024, 1024)
    d_wout = _matmul(cat, dz1, "tn", BF16, "mix_dwout", 1024, 1024, 512)
    (dh0, d_acw, d_acb, d_ang, d_anb, d_bng, d_bnb, d_ms, d_sb), (p_wup0, p_wout) = _mixer_mid_bwd(
        h0, dcat, acw, a_conv_b, a_norm_g, a_norm_b, b_norm_g, b_norm_b, ms, mst, sbt, "mix_mid_bwd",
        comm=_Comm(exchange=[d_wup0, d_wout.reshape(N_DEV, -1, D_MODEL)]))
    d_bqkv = jnp.concatenate([dbq, dbkv], axis=1)
    d_lng = jnp.stack([jnp.stack([dg00, dg01]), jnp.stack([dg10, dg11])])
    d_lnb = jnp.stack([jnp.stack([db00, db01]), jnp.stack([db10, db11])])
    small_full = [d_acb, d_ang, d_anb, d_bng, d_bnb, d_ms, d_sb, dsinks[:, :N_Q_HEADS], jnp.concatenate([d_fcb0, d_fcb1], axis=0),
                  d_acw, d_bqkv, jnp.stack([d_fcw0, d_fcw1]), d_lng, d_lnb]
    d_win, (g_small_grads,) = _matmul(xt, dh0, "tn", BF16, "mix_dwin", 1024, 1024, 512, comm=_Comm(gather=[_pack(small_full)]))
    grad_x, (p_win,) = _matmul(dh0, w_in, "nt", F32, "mix_dx", 1024, 1024, 1024, res=dz1, res_scale=ALPHA,
                               comm=_Comm(exchange=[_deinterleave(d_win)]))

    loss = lax.psum(0.5 / D_MODEL * jnp.sum(loss_terms), ("x", "y", "c"))

    big = {}
    for nm, p, w, m, v, tr, transposed in [
            ("ab_w_in", [p_win], ab_w_in, m_ab_w_in, v_ab_w_in, 256, False),
            ("ab_w_out", [p_wout], ab_w_out, m_ab_w_out, v_ab_w_out, 128, False),
            ("c_w_qkv", [p_wqkv], c_w_qkv, m_c_w_qkv, v_c_w_qkv, 160, True), ("c_w_o", [p_wo], c_w_o, m_c_w_o, v_c_w_o, 128, False),
            ("ffn_w_up", [p_wup0, p_wup1], ffn_w_up, m_ffn_w_up, v_ffn_w_up, 176, True),
            ("ffn_w_down", [p_wdown0, p_wdown1], ffn_w_down, m_ffn_w_down, v_ffn_w_down, 176, False)]:
        def two_d(a):
            a = jnp.swapaxes(a, 1, 2) if transposed else a
            return a.reshape(-1, a.shape[-1])

        def back(o):
            return jnp.swapaxes(o.reshape(w.shape[0], w.shape[2], w.shape[1]), 1, 2) if transposed else o.reshape(w.shape)

        outs = _adamw_big(p, two_d(w), two_d(m), two_d(v), "adamw_" + nm, tr)
        big[nm] = [back(o) for o in outs]

    gs = _unpack(g_small_grads, [a.shape for a in small_full])

    def my_shard(g, width):
        g = g.reshape(g.shape[:-1] + (N_DEV, width))
        return lax.dynamic_index_in_dim(g, me, axis=g.ndim - 2, keepdims=False)

    small_names = ["a_conv_b", "a_norm_g", "a_norm_b", "b_norm_g", "b_norm_b", "b_spatial_w", "b_spatial_b", "c_sinks", "ffn_conv_b",
                   "a_conv_w", "c_b_qkv", "ffn_conv_w", "ln_g", "ln_b"]
    small_w = [a_conv_b, a_norm_g, a_norm_b, b_norm_g, b_norm_b, b_spatial_w, b_spatial_b, c_sinks, ffn_conv_b,
               a_conv_w, c_b_qkv, ffn_conv_w, ln_g, ln_b]
    small_m = [m_a_conv_b, m_a_norm_g, m_a_norm_b, m_b_norm_g, m_b_norm_b, m_b_spatial_w, m_b_spatial_b, m_c_sinks, m_ffn_conv_b,
               m_a_conv_w, m_c_b_qkv, m_ffn_conv_w, m_ln_g, m_ln_b]
    small_v = [v_a_conv_b, v_a_norm_g, v_a_norm_b, v_b_norm_g, v_b_norm_b, v_b_spatial_w, v_b_spatial_b, v_c_sinks, v_ffn_conv_b,
               v_a_conv_w, v_c_b_qkv, v_ffn_conv_w, v_ln_g, v_ln_b]
    gs[9:] = [my_shard(g, w.shape[-1]) for g, w in zip(gs[9:], small_w[9:])]
    two_d = [(-1, w.shape[-1]) for w in small_w]
    outs = _adamw_small([g.reshape((N_DEV,) + w.reshape(s).shape) for g, w, s in zip(gs, small_w, two_d)],
                        [w.reshape(s) for w, s in zip(small_w, two_d)], [m.reshape(s) for m, s in zip(small_m, two_d)],
                        [v.reshape(s) for v, s in zip(small_v, two_d)], "adamw_small")
    small = {nm: [o.reshape(w.shape) for o in outs[4 * a:4 * a + 4]] for a, (nm, w) in enumerate(zip(small_names, small_w))}

    res = {**big, **small}
    order = ["ab_w_in", "a_conv_w", "a_conv_b", "a_norm_g", "a_norm_b", "b_norm_g", "b_norm_b", "b_spatial_w", "b_spatial_b", "ab_w_out",
             "c_w_qkv", "c_b_qkv", "c_sinks", "c_w_o", "ffn_w_up", "ffn_conv_w", "ffn_conv_b", "ffn_w_down", "ln_g", "ln_b"]
    return (loss, grad_x[None], *[res[nm][0] for nm in order], *[res[nm][1] for nm in order],
            *[res[nm][2] for nm in order], *[res[nm][3] for nm in order])
```

```python
import functools
import math

import jax
import jax.numpy as jnp
from jax import lax
from jax.experimental import pallas as pl
from jax.experimental.pallas import tpu as pltpu

F32 = jnp.float32
BF16 = jnp.bfloat16

N_DEV = 8
D_MODEL = 1024
A_WIDTH = 512
A_KERNEL = 31
B_GROUPS = 4
B_CHUNK = 128
HEAD_DIM = 64
N_Q_HEADS = 16
N_KV_HEADS = 2
ATT_BLOCK = 128
D_FF = 2816
FFN_KERNEL = 3
ALPHA = (2.0 * 2) ** 0.25
LN_EPS = 1e-5
GELU_K = math.sqrt(2.0 / math.pi)
GELU_C = 0.044715
ADAM_LR = 0.001
ADAM_B1 = 0.9
ADAM_B2 = 0.999
ADAM_EPS = 1e-08
ADAM_WD = 0.01
ADAM_STEP = 10
VMEM_LIMIT = 56 * 1024 * 1024
MESH_ID = pl.DeviceIdType.MESH


def _params(*sem):
    return pltpu.CompilerParams(dimension_semantics=sem, vmem_limit_bytes=VMEM_LIMIT)


def _gelu(x):
    t = jnp.tanh(GELU_K * x * (1.0 + GELU_C * x * x))
    return 0.5 * x * (1.0 + t)


def _gelu_and_grad(x):
    x2 = x * x
    t = jnp.tanh(GELU_K * x * (1.0 + GELU_C * x2))
    g = 0.5 * x * (1.0 + t)
    dg = 0.5 * (1.0 + t) + 0.5 * x * (1.0 - t * t) * (GELU_K * (1.0 + 3.0 * GELU_C * x2))
    return g, dg


def _sigmoid(x):
    return 1.0 / (1.0 + jnp.exp(-x))


def _ln_stats(z):
    mu = jnp.mean(z, axis=-1, keepdims=True)
    zc = z - mu
    var = jnp.mean(zc * zc, axis=-1, keepdims=True)
    r = lax.rsqrt(var + LN_EPS)
    return zc * r, r


def _ln_bwd_rows(dn, nh, r):
    return r * (dn - jnp.mean(dn, axis=-1, keepdims=True) - nh * jnp.mean(dn * nh, axis=-1, keepdims=True))


def _colsum(x):
    return jnp.sum(x, axis=0, keepdims=True)


def _dot(a, b, dims):
    return lax.dot_general(a.astype(BF16), b.astype(BF16), (dims, ((), ())), preferred_element_type=F32)


NN = ((1,), (0,))
NT = ((1,), (1,))
TN = ((0,), (0,))


ANY = pl.BlockSpec(memory_space=pl.ANY)
N_RELATIONS = N_DEV - 1


def _my_place():
    return lax.axis_index("x"), lax.axis_index("y"), lax.axis_index("c")


class _Comm:
    def __init__(self, gather=(), exchange=()):
        self.arrs = list(gather) + list(exchange)
        self.n_gather = len(gather)
        self.n = len(self.arrs)

    def out_shape(self):
        return [jax.ShapeDtypeStruct(((N_DEV,) + a.shape) if i < self.n_gather else a.shape, a.dtype)
                for i, a in enumerate(self.arrs)]

    def sems(self):
        return [pltpu.SemaphoreType.DMA((self.n, N_RELATIONS)), pltpu.SemaphoreType.DMA((self.n, N_RELATIONS)),
                pltpu.SemaphoreType.DMA((self.n,))]

    def _gather_copy(self, ins, outs, sems, a, k, place, to, from_input=False):
        px, py, pc = place
        block = outs[a].at[4 * px + 2 * py + pc]
        return pltpu.make_async_remote_copy(
            src_ref=ins[a] if from_input else block, dst_ref=block,
            send_sem=sems[0].at[a, k], recv_sem=sems[1].at[a, k], device_id=to, device_id_type=MESH_ID)

    def _exchange_copy(self, ins, outs, sems, a, k, landing=False):
        x, y, c = _my_place()
        me = 4 * x + 2 * y + c
        peer = (x ^ (k >> 2), y ^ ((k >> 1) & 1), c ^ (k & 1))
        return pltpu.make_async_remote_copy(
            src_ref=ins[a].at[me ^ k], dst_ref=outs[a].at[(me ^ k) if landing else me],
            send_sem=sems[0].at[a, k - 1], recv_sem=sems[1].at[a, k - 1], device_id=peer, device_id_type=MESH_ID)

    def _local_copy(self, ins, outs, sems, a):
        x, y, c = _my_place()
        me = 4 * x + 2 * y + c
        src = ins[a] if a < self.n_gather else ins[a].at[me]
        return pltpu.make_async_copy(src, outs[a].at[me], sems[2].at[a])

    def _first_stage(self, ins, outs, sems, a):
        x, y, c = _my_place()
        me = (x, y, c)
        chips = [(1 - x, y), (x, 1 - y), (1 - x, 1 - y)]
        return ([self._gather_copy(ins, outs, sems, a, 0, me, (x, y, 1 - c), from_input=True)]
                + [self._gather_copy(ins, outs, sems, a, 1 + j, me, (*chip, c), from_input=True) for j, chip in enumerate(chips)])

    def start(self, ins, outs, sems):
        for a in range(self.n):
            self._local_copy(ins, outs, sems, a).start()
        for a in range(self.n_gather):
            for cp in self._first_stage(ins, outs, sems, a):
                cp.start()
        for k in range(1, N_DEV):
            for a in range(self.n_gather, self.n):
                self._exchange_copy(ins, outs, sems, a, k).start()

    def finish(self, ins, outs, sems):
        x, y, c = _my_place()
        me, sibling = (x, y, c), (x, y, 1 - c)
        chips = [(1 - x, y), (x, 1 - y), (1 - x, 1 - y)]
        passed = []
        for j, chip in enumerate(chips):
            for a in range(self.n_gather):
                self._gather_copy(ins, outs, sems, a, 1 + j, (*chip, c), me).wait_recv()
                fwd = self._gather_copy(ins, outs, sems, a, 4 + j, (*chip, c), sibling)
                fwd.start()
                passed.append(fwd)
        for a in range(self.n_gather):
            self._gather_copy(ins, outs, sems, a, 0, sibling, me).wait_recv()
            for j, chip in enumerate(chips):
                self._gather_copy(ins, outs, sems, a, 4 + j, (*chip, 1 - c), me).wait_recv()
        for k in range(1, N_DEV):
            for a in range(self.n_gather, self.n):
                self._exchange_copy(ins, outs, sems, a, k, landing=True).wait_recv()
        for a in range(self.n_gather):
            for cp in self._first_stage(ins, outs, sems, a):
                cp.wait_send()
        for cp in passed:
            cp.wait_send()
        for k in range(1, N_DEV):
            for a in range(self.n_gather, self.n):
                self._exchange_copy(ins, outs, sems, a, k).wait_send()
        for a in range(self.n):
            self._local_copy(ins, outs, sems, a).wait()


def _comm_only(comm, name):
    def body(*refs):
        ins, outs, sems = refs[:comm.n], refs[comm.n:2 * comm.n], refs[2 * comm.n:]
        comm.start(ins, outs, sems)
        comm.finish(ins, outs, sems)

    return pl.pallas_call(body, name=name, in_specs=[ANY] * comm.n, out_specs=[ANY] * comm.n,
                          out_shape=comm.out_shape(), scratch_shapes=comm.sems())(*comm.arrs)


def _call(body, *, name, grid, in_specs, out_specs, out_shape, args, sem, scratch_shapes=(), comm=None):
    in_specs, out_specs, out_shape, scratch_shapes = list(in_specs), list(out_specs), list(out_shape), list(scratch_shapes)
    if comm is None:
        outs = pl.pallas_call(body, name=name, grid=grid, in_specs=in_specs, out_specs=out_specs, out_shape=out_shape,
                              scratch_shapes=scratch_shapes, compiler_params=_params(*sem))(*args)
        return list(outs), []
    n_in, n_out, n_scr, nc = len(in_specs), len(out_specs), len(scratch_shapes), comm.n

    def wrapped(*refs):
        ins, refs = refs[:n_in], refs[n_in:]
        c_in, refs = refs[:nc], refs[nc:]
        outs, refs = refs[:n_out], refs[n_out:]
        c_out, refs = refs[:nc], refs[nc:]
        scr, sems = refs[:n_scr], refs[n_scr:]
        first = functools.reduce(jnp.logical_and, [pl.program_id(ax) == 0 for ax in range(len(grid))])
        last = functools.reduce(jnp.logical_and, [pl.program_id(ax) == g - 1 for ax, g in enumerate(grid)])

        @pl.when(first)
        def _():
            comm.start(c_in, c_out, sems)

        body(*ins, *outs, *scr)

        @pl.when(last)
        def _():
            comm.finish(c_in, c_out, sems)

    outs = pl.pallas_call(
        wrapped, name=name, grid=grid, in_specs=in_specs + [ANY] * nc, out_specs=out_specs + [ANY] * nc,
        out_shape=out_shape + comm.out_shape(), scratch_shapes=scratch_shapes + comm.sems(),
        compiler_params=_params(*(["arbitrary"] * len(grid))))(*args, *comm.arrs)
    return list(outs[:n_out]), list(outs[n_out:])


def _matmul(a, b, mode, out_dtype, name, tm, tn, tk, *, bias=None, res=None, res_scale=1.0, b_off=0, comm=None):
    tm = min(tm, a.shape[1] if mode == "tn" else a.shape[0])
    tk = min(tk, a.shape[0] if mode == "tn" else a.shape[1])
    if mode == "nn":
        (m, k), n = a.shape, b.shape[1]
        a_spec = pl.BlockSpec((tm, tk), lambda i, j, kk: (i, kk))
        b_spec = pl.BlockSpec((tk, tn), lambda i, j, kk: (kk + b_off, j))
        dims = NN
    elif mode == "nt":
        (m, k), n = a.shape, b.shape[0]
        a_spec = pl.BlockSpec((tm, tk), lambda i, j, kk: (i, kk))
        b_spec = pl.BlockSpec((tn, tk), lambda i, j, kk: (j, kk + b_off))
        dims = NT
    else:
        (k, m), n = a.shape, b.shape[1]
        a_spec = pl.BlockSpec((tk, tm), lambda i, j, kk: (kk, i))
        b_spec = pl.BlockSpec((tk, tn), lambda i, j, kk: (kk, j))
        dims = TN
    assert m % tm == 0 and n % tn == 0 and k % tk == 0, (name, m, n, k)
    nk = k // tk
    in_specs = [a_spec, b_spec]
    args = [a, b]
    if bias is not None:
        in_specs.append(pl.BlockSpec((1, tn), lambda i, j, kk: (0, j)))
        args.append(bias)
    if res is not None:
        in_specs.append(pl.BlockSpec((tm, tn), lambda i, j, kk: (i, j)))
        args.append(res)

    def finish(out, refs, o_ref):
        pos = 2
        if bias is not None:
            out = out + refs[pos][...]
            pos += 1
        if res is not None:
            out = out + res_scale * refs[pos][...].astype(F32)
        o_ref[...] = out.astype(out_dtype)

    def body_one_step(*refs):
        finish(_dot(refs[0][...], refs[1][...], dims), refs, refs[-1])

    def body(*refs):
        a_ref, b_ref = refs[0], refs[1]
        o_ref, acc = refs[-2], refs[-1]
        kk = pl.program_id(2)

        @pl.when(kk == 0)
        def _():
            acc[...] = jnp.zeros_like(acc)

        acc[...] += _dot(a_ref[...], b_ref[...], dims)

        @pl.when(kk == nk - 1)
        def _():
            finish(acc[...], refs, o_ref)

    (out,), moved = _call(
        body_one_step if nk == 1 else body, name=name, grid=(m // tm, n // tn, nk),
        in_specs=in_specs, out_specs=[pl.BlockSpec((tm, tn), lambda i, j, kk: (i, j))],
        out_shape=[jax.ShapeDtypeStruct((m, n), out_dtype)],
        scratch_shapes=[] if nk == 1 else [pltpu.VMEM((tm, tn), F32)],
        sem=("parallel", "parallel", "arbitrary"), args=args, comm=comm)
    return out if comm is None else (out, moved)


def _matmul_tn_pair(a0, a1, b, out_dtype, name, tm, tn, tk):
    (k, m), n = a0.shape, b.shape[1]
    tk = min(tk, k)
    assert a1.shape == a0.shape and m % tm == 0 and n % tn == 0 and k % tk == 0, (name, m, n, k)
    mi, nk = m // tm, k // tk

    def body(a0_ref, a1_ref, b_ref, o_ref, acc):
        i, kk = pl.program_id(0), pl.program_id(2)

        @pl.when(kk == 0)
        def _():
            acc[...] = jnp.zeros_like(acc)

        @pl.when(i < mi)
        def _():
            acc[...] += _dot(a0_ref[...], b_ref[...], TN)

        @pl.when(i >= mi)
        def _():
            acc[...] += _dot(a1_ref[...], b_ref[...], TN)

        @pl.when(kk == nk - 1)
        def _():
            o_ref[...] = acc[...].astype(out_dtype)

    return pl.pallas_call(
        body, name=name, grid=(2 * mi, n // tn, nk),
        in_specs=[pl.BlockSpec((tk, tm), lambda i, j, kk: (jnp.where(i < mi, kk, nk - 1), jnp.minimum(i, mi - 1))),
                  pl.BlockSpec((tk, tm), lambda i, j, kk: (jnp.where(i >= mi, kk, 0), jnp.maximum(i - mi, 0))),
                  pl.BlockSpec((tk, tn), lambda i, j, kk: (kk, j))],
        out_specs=pl.BlockSpec((tm, tn), lambda i, j, kk: (i, j)),
        out_shape=jax.ShapeDtypeStruct((2 * m, n), out_dtype),
        scratch_shapes=[pltpu.VMEM((tm, tn), F32)],
        compiler_params=_params("parallel", "parallel", "arbitrary"),
    )(a0, a1, b)


def _matmul_res_ln(a, b, xres, g, beta, name, tm, tk, comm=None):
    t, k = a.shape
    d = b.shape[1]
    nk = k // tk
    assert t % tm == 0 and k % tk == 0

    def body(a_ref, b_ref, x_ref, g_ref, beta_ref, z_ref, xo_ref, acc):
        kk = pl.program_id(1)

        @pl.when(kk == 0)
        def _():
            acc[...] = jnp.zeros_like(acc)

        acc[...] += _dot(a_ref[...], b_ref[...], NN)

        @pl.when(kk == nk - 1)
        def _():
            z = ALPHA * x_ref[...] + acc[...]
            nh, _ = _ln_stats(z)
            z_ref[...] = z
            xo_ref[...] = nh * g_ref[...] + beta_ref[...]

    row = pl.BlockSpec((tm, d), lambda i, kk: (i, 0))
    vec = pl.BlockSpec((1, d), lambda i, kk: (0, 0))
    outs, moved = _call(
        body, name=name, grid=(t // tm, nk),
        in_specs=[pl.BlockSpec((tm, tk), lambda i, kk: (i, kk)), pl.BlockSpec((tk, d), lambda i, kk: (kk, 0)), row, vec, vec],
        out_specs=[row, row],
        out_shape=[jax.ShapeDtypeStruct((t, d), F32), jax.ShapeDtypeStruct((t, d), F32)],
        scratch_shapes=[pltpu.VMEM((tm, d), F32)],
        sem=("parallel", "arbitrary"), args=(a, b, xres, g, beta), comm=comm)
    return outs if comm is None else (outs, moved)


def _ln_bwd(z, g, dres, dbr, name, tm=512):
    t, d = z.shape

    def body(z_ref, g_ref, dres_ref, dbr_ref, dz_ref, dg_ref, db_ref):
        @pl.when(pl.program_id(0) == 0)
        def _():
            dg_ref[...] = jnp.zeros_like(dg_ref)
            db_ref[...] = jnp.zeros_like(db_ref)

        nh, r = _ln_stats(z_ref[...])
        dy = ALPHA * dres_ref[...] + dbr_ref[...].astype(F32)
        dg_ref[...] += _colsum(dy * nh)
        db_ref[...] += _colsum(dy)
        dz_ref[...] = _ln_bwd_rows(dy * g_ref[...], nh, r)

    row = pl.BlockSpec((tm, d), lambda i: (i, 0))
    vec = pl.BlockSpec((1, d), lambda i: (0, 0))
    return pl.pallas_call(
        body, name=name, grid=(t // tm,), in_specs=[row, vec, row, row], out_specs=[row, vec, vec],
        out_shape=[jax.ShapeDtypeStruct((t, d), F32), jax.ShapeDtypeStruct((1, d), F32), jax.ShapeDtypeStruct((1, d), F32)],
        compiler_params=_params("arbitrary"),
    )(z, g, dres, dbr)


def _ln_bwd_loss(z, g, beta, target, name, tm=512):
    t, d = z.shape

    def body(z_ref, g_ref, beta_ref, t_ref, dz_ref, dg_ref, db_ref, loss_ref):
        @pl.when(pl.program_id(0) == 0)
        def _():
            dg_ref[...] = jnp.zeros_like(dg_ref)
            db_ref[...] = jnp.zeros_like(db_ref)
            loss_ref[...] = jnp.zeros_like(loss_ref)

        nh, r = _ln_stats(z_ref[...])
        err = nh * g_ref[...] + beta_ref[...] - t_ref[...]
        loss_ref[...] += _colsum(err * err)
        dy = err * (1.0 / d)
        dg_ref[...] += _colsum(dy * nh)
        db_ref[...] += _colsum(dy)
        dz_ref[...] = _ln_bwd_rows(dy * g_ref[...], nh, r)

    row = pl.BlockSpec((tm, d), lambda i: (i, 0))
    vec = pl.BlockSpec((1, d), lambda i: (0, 0))
    vshape = jax.ShapeDtypeStruct((1, d), F32)
    return pl.pallas_call(
        body, name=name, grid=(t // tm,), in_specs=[row, vec, vec, row], out_specs=[row, vec, vec, vec],
        out_shape=[jax.ShapeDtypeStruct((t, d), F32), vshape, vshape, vshape],
        compiler_params=_params("arbitrary"),
    )(z, g, beta, target)


FFN_HALO = 16
FFN_CHUNK = 256
LANES = 128
SUBLANES = 8


def _fold(x):
    return jnp.sum(x.reshape(x.shape[0] // SUBLANES, SUBLANES, x.shape[1]), axis=0)


def _ffn_mid_fwd(h, cw, cb, name, tm=1024, tc=256, comm=None):
    t, f2 = h.shape
    tm = min(tm, t)
    f = f2 // 2
    nj, nt, hb = f // tc, t // tm, tm // FFN_HALO

    ch = min(FFN_CHUNK, tm)

    def body(hg, hgp, hv, hvp, cwg, cwv, cbg, cbv, u_ref, sg, sv):
        i = pl.program_id(1)
        for main, prev, s in ((hg, hgp, sg), (hv, hvp, sv)):
            s[0:FFN_HALO, :] = jnp.where(i > 0, prev[...].astype(F32), 0.0)
            s[FFN_HALO:, :] = main[...].astype(F32)
        o = SUBLANES - FFN_KERNEL + 1
        for lg in range(tc // LANES):
            cols = slice(lg * LANES, (lg + 1) * LANES)
            wg, wv = [cwg[k:k + 1, cols] for k in range(FFN_KERNEL)], [cwv[k:k + 1, cols] for k in range(FFN_KERNEL)]
            bg, bv = cbg[:, cols], cbv[:, cols]

            def chunk(c, carry):
                base = pl.multiple_of(c * ch, ch)
                eg = sg[pl.ds(base + FFN_HALO - SUBLANES, ch + SUBLANES), cols]
                ev = sv[pl.ds(base + FFN_HALO - SUBLANES, ch + SUBLANES), cols]
                cg = wg[0] * eg[o:o + ch] + wg[1] * eg[o + 1:o + 1 + ch] + wg[2] * eg[o + 2:o + 2 + ch] + bg
                cv = wv[0] * ev[o:o + ch] + wv[1] * ev[o + 1:o + 1 + ch] + wv[2] * ev[o + 2:o + 2 + ch] + bv
                u_ref[pl.ds(base, ch), cols] = (_gelu(cg) * cv).astype(BF16)
                return carry

            lax.fori_loop(0, tm // ch, chunk, 0)

    def main_spec(off):
        return pl.BlockSpec((tm, tc), lambda j, i: (i, j + off))

    def prev_spec(off):
        return pl.BlockSpec((FFN_HALO, tc), lambda j, i: (jnp.maximum(i * hb - 1, 0), j + off))

    def par_spec(rows, off):
        return pl.BlockSpec((rows, tc), lambda j, i: (0, j + off))

    (u,), moved = _call(
        body, name=name, grid=(nj, nt),
        in_specs=[main_spec(0), prev_spec(0), main_spec(nj), prev_spec(nj),
                  par_spec(FFN_KERNEL, 0), par_spec(FFN_KERNEL, nj), par_spec(1, 0), par_spec(1, nj)],
        out_specs=[pl.BlockSpec((tm, tc), lambda j, i: (i, j))],
        out_shape=[jax.ShapeDtypeStruct((t, f), BF16)],
        scratch_shapes=[pltpu.VMEM((tm + FFN_HALO, tc), F32), pltpu.VMEM((tm + FFN_HALO, tc), F32)],
        sem=("parallel", "arbitrary"), args=(h, h, h, h, cw, cw, cb, cb), comm=comm)
    return u if comm is None else (u, moved)


def _ffn_mid_bwd(h, du, cw, cb, name, tm=1024, tc=256, comm=None):
    t, f2 = h.shape
    tm = min(tm, t)
    f = f2 // 2
    nj, nt, hb = f // tc, t // tm, tm // FFN_HALO

    ch = min(FFN_CHUNK, tm)
    ahead = ch + SUBLANES

    def body(hg, hgp, hgn, hv, hvp, hvn, du_ref, dun_ref, cwg, cwv, cbg, cbv,
             dhg_ref, dhv_ref, dcwg_ref, dcwv_ref, dcbg_ref, dcbv_ref, sg, sv, sdu):
        i = pl.program_id(1)

        @pl.when(i == 0)
        def _():
            for ref in (dcwg_ref, dcwv_ref, dcbg_ref, dcbv_ref):
                ref[...] = jnp.zeros_like(ref)

        for main, prev, nxt, s in ((hg, hgp, hgn, sg), (hv, hvp, hvn, sv)):
            s[0:FFN_HALO, :] = jnp.where(i > 0, prev[...].astype(F32), 0.0)
            s[FFN_HALO:FFN_HALO + tm, :] = main[...].astype(F32)
            s[FFN_HALO + tm:, :] = nxt[...].astype(F32)
        sdu[0:tm, :] = du_ref[...].astype(F32)
        sdu[tm:, :] = jnp.where(i < nt - 1, dun_ref[...].astype(F32), 0.0)
        o = SUBLANES - FFN_KERNEL + 1
        for lg in range(tc // LANES):
            cols = slice(lg * LANES, (lg + 1) * LANES)
            wg, wv = [cwg[k:k + 1, cols] for k in range(FFN_KERNEL)], [cwv[k:k + 1, cols] for k in range(FFN_KERNEL)]
            bg, bv = cbg[:, cols], cbv[:, cols]

            def chunk(c, acc):
                base = pl.multiple_of(c * ch, ch)
                eg = sg[pl.ds(base + FFN_HALO - SUBLANES, ahead + SUBLANES), cols]
                ev = sv[pl.ds(base + FFN_HALO - SUBLANES, ahead + SUBLANES), cols]
                hgs = [eg[o + k:o + k + ahead] for k in range(FFN_KERNEL)]
                hvs = [ev[o + k:o + k + ahead] for k in range(FFN_KERNEL)]
                cg = wg[0] * hgs[0] + wg[1] * hgs[1] + wg[2] * hgs[2] + bg
                cv = wv[0] * hvs[0] + wv[1] * hvs[1] + wv[2] * hvs[2] + bv
                du_e = sdu[pl.ds(base, ahead), cols]
                gl, dgl = _gelu_and_grad(cg)

                def back(d, hs, w, dh_ref):
                    own = d[0:ch]
                    dh = w[2] * own + w[1] * d[1:1 + ch] + w[0] * d[2:2 + ch]
                    dh_ref[pl.ds(base, ch), cols] = dh.astype(BF16)
                    return [_fold(own)] + [_fold(own * hs[k][0:ch]) for k in range(FFN_KERNEL)]

                sums = back(du_e * cv * dgl, hgs, wg, dhg_ref) + back(du_e * gl, hvs, wv, dhv_ref)
                return tuple(a + s_ for a, s_ in zip(acc, sums))

            zero = jnp.zeros((SUBLANES, LANES), F32)
            acc = lax.fori_loop(0, tm // ch, chunk, (zero,) * (2 * (1 + FFN_KERNEL)))
            dcbg_ref[:, cols] += _colsum(acc[0])
            dcbv_ref[:, cols] += _colsum(acc[1 + FFN_KERNEL])
            for k in range(FFN_KERNEL):
                dcwg_ref[k:k + 1, cols] += _colsum(acc[1 + k])
                dcwv_ref[k:k + 1, cols] += _colsum(acc[2 + FFN_KERNEL + k])

    last_blk = t // FFN_HALO - 1

    def main_spec(off):
        return pl.BlockSpec((tm, tc), lambda j, i: (i, j + off))

    def prev_spec(off):
        return pl.BlockSpec((FFN_HALO, tc), lambda j, i: (jnp.maximum(i * hb - 1, 0), j + off))

    def next_spec(off):
        return pl.BlockSpec((FFN_HALO, tc), lambda j, i: (jnp.minimum((i + 1) * hb, last_blk), j + off))

    def par_spec(rows, off):
        return pl.BlockSpec((rows, tc), lambda j, i: (0, j + off))

    out_tile = pl.BlockSpec((tm, tc), lambda j, i: (i, j))
    outs, moved = _call(
        body, name=name, grid=(nj, nt),
        in_specs=[main_spec(0), prev_spec(0), next_spec(0), main_spec(nj), prev_spec(nj), next_spec(nj),
                  main_spec(0), next_spec(0),
                  par_spec(FFN_KERNEL, 0), par_spec(FFN_KERNEL, nj), par_spec(1, 0), par_spec(1, nj)],
        out_specs=[out_tile, out_tile, par_spec(FFN_KERNEL, 0), par_spec(FFN_KERNEL, 0), par_spec(1, 0), par_spec(1, 0)],
        out_shape=[jax.ShapeDtypeStruct((t, f), BF16), jax.ShapeDtypeStruct((t, f), BF16),
                   jax.ShapeDtypeStruct((FFN_KERNEL, f), F32), jax.ShapeDtypeStruct((FFN_KERNEL, f), F32),
                   jax.ShapeDtypeStruct((1, f), F32), jax.ShapeDtypeStruct((1, f), F32)],
        scratch_shapes=[pltpu.VMEM((tm + 2 * FFN_HALO, tc), F32), pltpu.VMEM((tm + 2 * FFN_HALO, tc), F32),
                        pltpu.VMEM((tm + FFN_HALO, tc), F32)],
        sem=("parallel", "arbitrary"), args=(h, h, h, h, h, h, du, du, cw, cw, cb, cb), comm=comm)
    return outs if comm is None else (outs, moved)


MIX_HALO = 32


def _glu(hh):
    return hh[:, 0:A_WIDTH] * _sigmoid(hh[:, A_WIDTH:2 * A_WIDTH])


def _fill_row_shifts(s):
    rows = s.shape[1] - SUBLANES
    for j in range(1, SUBLANES):
        s[j, 0:rows, :] = s[0, pl.ds(j, rows), :]


def _rows_from(s, start, rows):
    j = start % SUBLANES
    return s[j, start - j:start - j + rows, :]


def _tril_mask():
    return lax.broadcasted_iota(jnp.int32, (B_CHUNK, B_CHUNK), 0) >= lax.broadcasted_iota(jnp.int32, (B_CHUNK, B_CHUNK), 1)


def _spatial_mix(q, ms_ref, sbt_ref, tm):
    mask = _tril_mask()
    ws = [jnp.where(mask, ms_ref[g], 0.0).astype(BF16) for g in range(B_GROUPS)]
    qb = q.astype(BF16)
    rows = []
    for c in range(tm // B_CHUNK):
        cols = [_dot(ws[g], qb[c * B_CHUNK:(c + 1) * B_CHUNK, g * 128:(g + 1) * 128], NN) + sbt_ref[:, g:g + 1]
                for g in range(B_GROUPS)]
        rows.append(jnp.concatenate(cols, axis=1))
    return jnp.concatenate(rows, axis=0)


def _mixer_mid_fwd(h, cw, cb, ag, ab, bg, bb, ms, sbt, name, tm=256, comm=None):
    t = h.shape[0]
    nt, hb = t // tm, tm // MIX_HALO
    o = MIX_HALO - A_KERNEL + 1

    def body(h_ref, hp_ref, cw_ref, cb_ref, ag_ref, ab_ref, bg_ref, bb_ref, ms_ref, sbt_ref, cat_ref, sp):
        i = pl.program_id(0)
        sp[0, 0:MIX_HALO, :] = jnp.where(i > 0, _glu(hp_ref[:, 0:2 * A_WIDTH].astype(F32)), 0.0)
        sp[0, MIX_HALO:, :] = _glu(h_ref[:, 0:2 * A_WIDTH].astype(F32))
        _fill_row_shifts(sp)
        y = jnp.zeros((tm, A_WIDTH), F32) + cb_ref[...]
        for k in range(A_KERNEL):
            y = y + cw_ref[k:k + 1, :] * _rows_from(sp, o + k, tm)
        nh, _ = _ln_stats(y)
        ln = nh * ag_ref[...] + ab_ref[...]
        cat_ref[:, 0:A_WIDTH] = (ln * _sigmoid(ln)).astype(BF16)
        u = _gelu(h_ref[:, 1024:1536].astype(F32))
        nb, _ = _ln_stats(_gelu(h_ref[:, 1536:2048].astype(F32)))
        mixed = _spatial_mix(nb * bg_ref[...] + bb_ref[...], ms_ref, sbt_ref, tm)
        cat_ref[:, A_WIDTH:] = (u * mixed).astype(BF16)

    vec = pl.BlockSpec((1, A_WIDTH), lambda i: (0, 0))
    (cat,), moved = _call(
        body, name=name, grid=(nt,),
        in_specs=[pl.BlockSpec((tm, 2048), lambda i: (i, 0)),
                  pl.BlockSpec((MIX_HALO, 2048), lambda i: (jnp.maximum(i * hb - 1, 0), 0)),
                  pl.BlockSpec((A_KERNEL, A_WIDTH), lambda i: (0, 0)), vec, vec, vec, vec, vec,
                  pl.BlockSpec((B_GROUPS, B_CHUNK, B_CHUNK), lambda i: (0, 0, 0)),
                  pl.BlockSpec((B_CHUNK, B_GROUPS), lambda i: (0, 0))],
        out_specs=[pl.BlockSpec((tm, D_MODEL), lambda i: (i, 0))],
        out_shape=[jax.ShapeDtypeStruct((t, D_MODEL), BF16)],
        scratch_shapes=[pltpu.VMEM((SUBLANES, tm + MIX_HALO, A_WIDTH), F32)],
        sem=("parallel",), args=(h, h, cw, cb, ag, ab, bg, bb, ms, sbt), comm=comm)
    return cat if comm is None else (cat, moved)


def _mixer_mid_bwd(h, dcat, cw, cb, ag, ab, bg, bb, ms, mst, sbt, name, tm=256, comm=None):
    t = h.shape[0]
    nt, hb = t // tm, tm // MIX_HALO
    o = MIX_HALO - A_KERNEL + 1
    r = tm + MIX_HALO
    nchunk = tm // B_CHUNK

    def body(h_ref, hp_ref, hn_ref, dc_ref, dcn_ref, cw_ref, cb_ref, ag_ref, ab_ref, bg_ref, bb_ref, ms_ref, mst_ref, sbt_ref,
             dh_ref, dcw_ref, dcb_ref, dag_ref, dab_ref, dbg_ref, dbb_ref, dms_ref, dsb_ref, sp, sdy, sbacc):
        i = pl.program_id(0)

        @pl.when(i == 0)
        def _():
            for ref in (dcw_ref, dcb_ref, dag_ref, dab_ref, dbg_ref, dbb_ref, dms_ref, dsb_ref, sbacc):
                ref[...] = jnp.zeros_like(ref)

        sp[0, 0:MIX_HALO, :] = jnp.where(i > 0, _glu(hp_ref[:, 0:2 * A_WIDTH].astype(F32)), 0.0)
        sp[0, MIX_HALO:MIX_HALO + tm, :] = _glu(h_ref[:, 0:2 * A_WIDTH].astype(F32))
        sp[0, MIX_HALO + tm:, :] = _glu(hn_ref[:, 0:2 * A_WIDTH].astype(F32))
        _fill_row_shifts(sp)
        y = jnp.zeros((r, A_WIDTH), F32) + cb_ref[...]
        for k in range(A_KERNEL):
            y = y + cw_ref[k:k + 1, :] * _rows_from(sp, o + k, r)
        nh, rs = _ln_stats(y)
        ln = nh * ag_ref[...] + ab_ref[...]
        sg = _sigmoid(ln)
        dao = jnp.concatenate([dc_ref[:, 0:A_WIDTH].astype(F32),
                               jnp.where(i < nt - 1, dcn_ref[:, 0:A_WIDTH].astype(F32), 0.0)], axis=0)
        dln = dao * (sg * (1.0 + ln * (1.0 - sg)))
        dag_ref[...] += _colsum(dln[0:tm] * nh[0:tm])
        dab_ref[...] += _colsum(dln[0:tm])
        sdy[0] = _ln_bwd_rows(dln * ag_ref[...], nh, rs)
        _fill_row_shifts(sdy)
        dy_own = sdy[0, 0:tm, :]
        dcb_ref[...] += _colsum(dy_own)
        dp = jnp.zeros((tm, A_WIDTH), F32)
        for k in range(A_KERNEL):
            dcw_ref[k:k + 1, :] += _colsum(dy_own * _rows_from(sp, o + k, tm))
            dp = dp + cw_ref[k:k + 1, :] * _rows_from(sdy, A_KERNEL - 1 - k, tm)
        av = h_ref[:, 0:A_WIDTH].astype(F32)
        s = _sigmoid(h_ref[:, A_WIDTH:2 * A_WIDTH].astype(F32))
        dh_ref[:, 0:A_WIDTH] = (dp * s).astype(BF16)
        dh_ref[:, A_WIDTH:2 * A_WIDTH] = (dp * av * s * (1.0 - s)).astype(BF16)

        u, dgu = _gelu_and_grad(h_ref[:, 1024:1536].astype(F32))
        w, dgw = _gelu_and_grad(h_ref[:, 1536:2048].astype(F32))
        nb, rb = _ln_stats(w)
        q = nb * bg_ref[...] + bb_ref[...]
        mixed = _spatial_mix(q, ms_ref, sbt_ref, tm)
        dbo = dc_ref[:, A_WIDTH:].astype(F32)
        dh_ref[:, 1024:1536] = (dbo * mixed * dgu).astype(BF16)
        dmx = dbo * u
        mask = _tril_mask()
        wst = [jnp.where(mask.T, mst_ref[g], 0.0).astype(BF16) for g in range(B_GROUPS)]
        qb = q.astype(BF16)
        dmb = dmx.astype(BF16)
        rows = []
        for c in range(nchunk):
            cols = []
            for g in range(B_GROUPS):
                rs_, cs_ = slice(c * B_CHUNK, (c + 1) * B_CHUNK), slice(g * 128, (g + 1) * 128)
                sbacc[g] += dmx[rs_, cs_]
                dms_ref[g] += _dot(dmb[rs_, cs_], qb[rs_, cs_], NT)
                cols.append(_dot(wst[g], dmb[rs_, cs_], NN))
            rows.append(jnp.concatenate(cols, axis=1))
        dq = jnp.concatenate(rows, axis=0)
        dbg_ref[...] += _colsum(dq * nb)
        dbb_ref[...] += _colsum(dq)
        dh_ref[:, 1536:2048] = (_ln_bwd_rows(dq * bg_ref[...], nb, rb) * dgw).astype(BF16)

        @pl.when(i == nt - 1)
        def _():
            for g in range(B_GROUPS):
                dms_ref[g] = jnp.where(mask, dms_ref[g], 0.0)
                dsb_ref[g] = jnp.sum(sbacc[g], axis=1, keepdims=True)

    last_blk = t // MIX_HALO - 1
    vec = pl.BlockSpec((1, A_WIDTH), lambda i: (0, 0))
    mat = pl.BlockSpec((B_GROUPS, B_CHUNK, B_CHUNK), lambda i: (0, 0, 0))
    taps = pl.BlockSpec((A_KERNEL, A_WIDTH), lambda i: (0, 0))

    def halo(width, which):
        if which == "prev":
            return pl.BlockSpec((MIX_HALO, width), lambda i: (jnp.maximum(i * hb - 1, 0), 0))
        return pl.BlockSpec((MIX_HALO, width), lambda i: (jnp.minimum((i + 1) * hb, last_blk), 0))

    vshape = jax.ShapeDtypeStruct((1, A_WIDTH), F32)
    outs, moved = _call(
        body, name=name, grid=(nt,),
        in_specs=[pl.BlockSpec((tm, 2048), lambda i: (i, 0)), halo(2048, "prev"), halo(2048, "next"),
                  pl.BlockSpec((tm, D_MODEL), lambda i: (i, 0)), halo(D_MODEL, "next"),
                  taps, vec, vec, vec, vec, vec, mat, mat, pl.BlockSpec((B_CHUNK, B_GROUPS), lambda i: (0, 0))],
        out_specs=[pl.BlockSpec((tm, 2048), lambda i: (i, 0)), taps, vec, vec, vec, vec, vec, mat,
                   pl.BlockSpec((B_GROUPS, B_CHUNK, 1), lambda i: (0, 0, 0))],
        out_shape=[jax.ShapeDtypeStruct((t, 2048), BF16), jax.ShapeDtypeStruct((A_KERNEL, A_WIDTH), F32),
                   vshape, vshape, vshape, vshape, vshape,
                   jax.ShapeDtypeStruct((B_GROUPS, B_CHUNK, B_CHUNK), F32), jax.ShapeDtypeStruct((B_GROUPS, B_CHUNK, 1), F32)],
        scratch_shapes=[pltpu.VMEM((SUBLANES, tm + 2 * MIX_HALO, A_WIDTH), F32), pltpu.VMEM((SUBLANES, r, A_WIDTH), F32),
                        pltpu.VMEM((B_GROUPS, B_CHUNK, B_CHUNK), F32)],
        sem=("arbitrary",), args=(h, h, h, dcat, dcat, cw, cb, ag, ab, bg, bb, ms, mst, sbt), comm=comm)
    return outs if comm is None else (outs, moved)


Q_WIDTH = N_Q_HEADS * HEAD_DIM
KV_WIDTH = 2 * N_KV_HEADS * HEAD_DIM
PAIRS_PER_KV = N_Q_HEADS // N_KV_HEADS // 2
ATT_SCALE = 1.0 / math.sqrt(HEAD_DIM)


def _dup_heads(pair_cols, kv_head):
    lane = lax.broadcasted_iota(jnp.int32, pair_cols.shape, 1)
    rolled = pltpu.roll(pair_cols, HEAD_DIM, 1)
    first = lane < HEAD_DIM
    return jnp.where(first, pair_cols, rolled) if kv_head == 0 else jnp.where(first, rolled, pair_cols)


HEADS_PER_KV = N_Q_HEADS // N_KV_HEADS


def _stack_heads(ref, kh):
    lane = lax.broadcasted_iota(jnp.int32, (ATT_BLOCK, 128), 1)
    rows = []
    for pr in range(PAIRS_PER_KV):
        c0 = (kh * PAIRS_PER_KV + pr) * 128
        pair = ref[:, c0:c0 + 128]
        rows += [jnp.where(lane < HEAD_DIM, pair, jnp.zeros_like(pair)), jnp.where(lane < HEAD_DIM, jnp.zeros_like(pair), pair)]
    return jnp.concatenate(rows, axis=0)


def _unstack_heads(stacked, kh, write):
    lane = lax.broadcasted_iota(jnp.int32, (ATT_BLOCK, 128), 1)
    for pr in range(PAIRS_PER_KV):
        first = stacked[(2 * pr) * ATT_BLOCK:(2 * pr + 1) * ATT_BLOCK]
        second = stacked[(2 * pr + 1) * ATT_BLOCK:(2 * pr + 2) * ATT_BLOCK]
        write((kh * PAIRS_PER_KV + pr) * 128, jnp.where(lane < HEAD_DIM, first, second))


def _sink_row(sink_ref, kh):
    return jnp.concatenate([jnp.full((1, ATT_BLOCK), sink_ref[0, kh * HEADS_PER_KV + h], F32) for h in range(HEADS_PER_KV)], axis=1)


def _att_mask_t(n):
    sj = lax.broadcasted_iota(jnp.int32, (2 * ATT_BLOCK, HEADS_PER_KV * ATT_BLOCK), 0)
    qi = lax.broadcasted_iota(jnp.int32, (2 * ATT_BLOCK, HEADS_PER_KV * ATT_BLOCK), 1) & (ATT_BLOCK - 1)
    diff = qi + ATT_BLOCK - sj
    return (diff >= 0) & (diff < ATT_BLOCK) & ((n > 0) | (sj >= ATT_BLOCK))


def _att_probs_t(q_all, k2, mask_t, sink):
    st = _dot(k2, q_all, NT) * ATT_SCALE
    st = jnp.where(mask_t, st, -jnp.inf)
    m = jnp.maximum(jnp.max(st, axis=0, keepdims=True), sink)
    e = jnp.exp(st - m)
    es = jnp.exp(sink - m)
    inv = 1.0 / (jnp.sum(e, axis=0, keepdims=True) + es)
    return e * inv, es * inv


def _attn_fwd(qkv, sinks, name, comm=None):
    t = qkv.shape[0]
    nb = t // ATT_BLOCK
    kvb = Q_WIDTH // KV_WIDTH

    def body(sink_ref, q_ref, kv_ref, kvp_ref, o_ref):
        n = pl.program_id(0)
        mask_t = _att_mask_t(n)
        kv = jnp.concatenate([kvp_ref[...], kv_ref[...]], axis=0).astype(F32)

        def write(c0, pair):
            o_ref[:, c0:c0 + 128] = pair.astype(BF16)

        for kh in range(N_KV_HEADS):
            k2 = _dup_heads(kv[:, 0:128], kh).astype(BF16)
            v2 = _dup_heads(kv[:, 128:256], kh).astype(BF16)
            pt, _ = _att_probs_t(_stack_heads(q_ref, kh), k2, mask_t, _sink_row(sink_ref, kh))
            _unstack_heads(_dot(v2, pt, TN).T, kh, write)

    (out,), moved = _call(
        body, name=name, grid=(nb,),
        in_specs=[pl.BlockSpec(memory_space=pltpu.SMEM),
                  pl.BlockSpec((ATT_BLOCK, Q_WIDTH), lambda n: (n, 0)),
                  pl.BlockSpec((ATT_BLOCK, KV_WIDTH), lambda n: (n, kvb)),
                  pl.BlockSpec((ATT_BLOCK, KV_WIDTH), lambda n: (jnp.maximum(n - 1, 0), kvb))],
        out_specs=[pl.BlockSpec((ATT_BLOCK, Q_WIDTH), lambda n: (n, 0))],
        out_shape=[jax.ShapeDtypeStruct((t, Q_WIDTH), BF16)],
        sem=("parallel",), args=(sinks, qkv, qkv, qkv), comm=comm)
    return out if comm is None else (out, moved)


def _attn_bwd(qkv, d_o, sinks, name, comm=None):
    t = qkv.shape[0]
    nb = t // ATT_BLOCK
    kvb = Q_WIDTH // KV_WIDTH

    def body(sink_ref, q_ref, kv_ref, kvp_ref, do_ref, dq_ref, dkv_ref, dbq_ref, dbkv_ref, dsink_ref, carry):
        n = pl.program_id(0)

        @pl.when(n == 0)
        def _():
            for ref in (dbq_ref, dbkv_ref, dsink_ref, carry):
                ref[...] = jnp.zeros_like(ref)
            dkv_ref[...] = jnp.zeros_like(dkv_ref)

        @pl.when(n < nb)
        def _():
            mask_t = _att_mask_t(n)
            kv = jnp.concatenate([kvp_ref[...], kv_ref[...]], axis=0).astype(F32)
            lane2 = lax.broadcasted_iota(jnp.int32, (2 * ATT_BLOCK, 128), 1)
            sink_lane = lax.broadcasted_iota(jnp.int32, (1, 128), 1)
            dsink = jnp.zeros((1, 128), F32)
            dk_parts, dv_parts = [], []

            def write(c0, pair):
                dbq_ref[:, c0:c0 + 128] += _colsum(pair)
                dq_ref[:, c0:c0 + 128] = pair.astype(BF16)

            for kh in range(N_KV_HEADS):
                k2 = _dup_heads(kv[:, 0:128], kh).astype(BF16)
                v2 = _dup_heads(kv[:, 128:256], kh).astype(BF16)
                q_all = _stack_heads(q_ref, kh)
                do_all = _stack_heads(do_ref, kh)
                pt, ps = _att_probs_t(q_all, k2, mask_t, _sink_row(sink_ref, kh))
                dpt = _dot(v2, do_all, NT)
                delta = jnp.sum(pt * dpt, axis=0, keepdims=True)
                dst = pt * (dpt - delta) * ATT_SCALE
                psd = ps * delta
                for h in range(HEADS_PER_KV):
                    dsink = dsink + jnp.where(sink_lane == kh * HEADS_PER_KV + h,
                                              -jnp.sum(psd[:, h * ATT_BLOCK:(h + 1) * ATT_BLOCK]), 0.0)
                _unstack_heads(_dot(k2, dst, TN).T, kh, write)
                dk_acc = _dot(dst, q_all, NN)
                dv_acc = _dot(pt, do_all, NN)
                dk_parts.append(dk_acc + pltpu.roll(dk_acc, HEAD_DIM, 1))
                dv_parts.append(dv_acc + pltpu.roll(dv_acc, HEAD_DIM, 1))
            dk = jnp.where(lane2 < HEAD_DIM, dk_parts[0], dk_parts[1])
            dv = jnp.where(lane2 < HEAD_DIM, dv_parts[0], dv_parts[1])
            dkv_new = jnp.concatenate([dk, dv], axis=1)
            done = carry[...] + dkv_new[0:ATT_BLOCK]

            @pl.when(n > 0)
            def _():
                dkv_ref[...] = done.astype(BF16)
                dbkv_ref[...] += _colsum(done)

            carry[...] = dkv_new[ATT_BLOCK:]
            dsink_ref[...] += dsink

        @pl.when(n == nb)
        def _():
            dkv_ref[...] = carry[...].astype(BF16)
            dbkv_ref[...] += _colsum(carry[...])

    def clamp(n):
        return jnp.minimum(n, nb - 1)

    outs, moved = _call(
        body, name=name, grid=(nb + 1,),
        in_specs=[pl.BlockSpec(memory_space=pltpu.SMEM),
                  pl.BlockSpec((ATT_BLOCK, Q_WIDTH), lambda n: (clamp(n), 0)),
                  pl.BlockSpec((ATT_BLOCK, KV_WIDTH), lambda n: (clamp(n), kvb)),
                  pl.BlockSpec((ATT_BLOCK, KV_WIDTH), lambda n: (jnp.maximum(clamp(n) - 1, 0), kvb)),
                  pl.BlockSpec((ATT_BLOCK, Q_WIDTH), lambda n: (clamp(n), 0))],
        out_specs=[pl.BlockSpec((ATT_BLOCK, Q_WIDTH), lambda n: (clamp(n), 0)),
                   pl.BlockSpec((ATT_BLOCK, KV_WIDTH), lambda n: (jnp.maximum(n - 1, 0), 0)),
                   pl.BlockSpec((1, Q_WIDTH), lambda n: (0, 0)),
                   pl.BlockSpec((1, KV_WIDTH), lambda n: (0, 0)),
                   pl.BlockSpec((1, 128), lambda n: (0, 0))],
        out_shape=[jax.ShapeDtypeStruct((t, Q_WIDTH), BF16), jax.ShapeDtypeStruct((t, KV_WIDTH), BF16),
                   jax.ShapeDtypeStruct((1, Q_WIDTH), F32), jax.ShapeDtypeStruct((1, KV_WIDTH), F32),
                   jax.ShapeDtypeStruct((1, 128), F32)],
        scratch_shapes=[pltpu.VMEM((ATT_BLOCK, KV_WIDTH), F32)],
        sem=("arbitrary",), args=(sinks, qkv, qkv, qkv, d_o), comm=comm)
    return outs if comm is None else (outs, moved)


def _adamw_math(g, w, m, v):
    m = ADAM_B1 * m + (1.0 - ADAM_B1) * g
    v = ADAM_B2 * v + (1.0 - ADAM_B2) * (g * g)
    m_hat = m / (1.0 - ADAM_B1 ** ADAM_STEP)
    v_hat = v / (1.0 - ADAM_B2 ** ADAM_STEP)
    delta = -ADAM_LR * (m_hat / (jnp.sqrt(v_hat) + ADAM_EPS) + ADAM_WD * w)
    return delta, m, v


def _sum_partials(p_ref):
    g = p_ref[0].astype(F32)
    for s in range(1, N_DEV):
        g = g + p_ref[s].astype(F32)
    return g


def _adamw_big(parts, w, m, v, name, tr):
    r, c = w.shape
    tiles = [p.shape[1] // tr for p in parts]
    starts = [sum(tiles[:l]) for l in range(len(parts))]
    assert all(p.shape[1] % tr == 0 for p in parts) and sum(tiles) * tr == r

    def body(*refs):
        p_refs, (w_ref, m_ref, v_ref, g_out, d_out, m_out, v_out) = refs[:len(parts)], refs[len(parts):]
        i = pl.program_id(0)
        for l, p_ref in enumerate(p_refs):
            @pl.when((i >= starts[l]) & (i < starts[l] + tiles[l]))
            def _():
                g = _sum_partials(p_ref)
                g_out[...] = g
                d_out[...], m_out[...], v_out[...] = _adamw_math(g, w_ref[...], m_ref[...], v_ref[...])

    def part_spec(l):
        return pl.BlockSpec((N_DEV, tr, c), lambda i: (0, jnp.clip(i - starts[l], 0, tiles[l] - 1), 0))

    tile = pl.BlockSpec((tr, c), lambda i: (i, 0))
    shape = jax.ShapeDtypeStruct((r, c), F32)
    return pl.pallas_call(
        body, name=name, grid=(r // tr,),
        in_specs=[part_spec(l) for l in range(len(parts))] + [tile, tile, tile],
        out_specs=[tile] * 4, out_shape=[shape] * 4,
        compiler_params=_params("parallel"),
    )(*parts, w, m, v)


def _adamw_small(parts, ws, ms, vs, name):
    n = len(ws)

    def body(*refs):
        ins, outs = refs[:4 * n], refs[4 * n:]
        for a in range(n):
            g = _sum_partials(ins[a])
            outs[4 * a][...] = g
            outs[4 * a + 1][...], outs[4 * a + 2][...], outs[4 * a + 3][...] = _adamw_math(
                g, ins[n + a][...], ins[2 * n + a][...], ins[3 * n + a][...])

    out_shape = []
    for w in ws:
        out_shape += [jax.ShapeDtypeStruct(w.shape, F32)] * 4
    return pl.pallas_call(body, name=name, out_shape=out_shape, compiler_params=_params())(*parts, *ws, *ms, *vs)


PACK_LANES = 128
PACK_ROWS = 8


def _pack(arrs):
    flat = jnp.concatenate([a.reshape(-1).astype(F32) for a in arrs])
    unit = PACK_LANES * PACK_ROWS
    total = -(-flat.shape[0] // unit) * unit
    return jnp.pad(flat, (0, total - flat.shape[0])).reshape(-1, PACK_LANES)


def _unpack(buf, shapes):
    flat = buf.reshape(N_DEV, -1)
    out, pos = [], 0
    for s in shapes:
        size = math.prod(s)
        out.append(flat[:, pos:pos + size].reshape((N_DEV,) + tuple(s)))
        pos += size
    return out


def _interleave(g):
    return jnp.transpose(g, (1, 0, 2)).reshape(g.shape[1], -1)


def _deinterleave(w):
    r = w.shape[0]
    return jnp.transpose(w.reshape(r, N_DEV, -1), (1, 0, 2))


def _ffn_backward(dz, x_in, h, u, w_up_t, cw, cb, w_down, tag, exchange=()):
    du = _matmul(dz, w_down, "nt", BF16, f"ffn{tag}_du", 1024, 1408, 1024)
    d_w_down = _matmul(u, dz, "tn", BF16, f"ffn{tag}_dwdown", 1408, 1024, 2048)
    (dhg, dhv, dcwg, dcwv, dcbg, dcbv), moved = _ffn_mid_bwd(
        h, du, cw, cb, f"ffn{tag}_mid_bwd", comm=_Comm(exchange=[d_w_down.reshape(N_DEV, -1, D_MODEL), *exchange]))
    d_w_up_t = _matmul_tn_pair(dhg, dhv, x_in, BF16, f"ffn{tag}_dwup", 1408, 1024, 1024)
    dx = _matmul(dhv, w_up_t, "nn", F32, f"ffn{tag}_dx_value", 1024, 1024, D_FF, b_off=1)
    dx = _matmul(dhg, w_up_t, "nn", F32, f"ffn{tag}_dx_gate", 1024, 1024, D_FF, res=dx)
    return (dx, d_w_up_t.reshape(N_DEV, -1, D_MODEL),
            jnp.concatenate([dcwg, dcwv], axis=1), jnp.concatenate([dcbg, dcbv], axis=1), moved)


def kernel(x, ab_w_in, a_conv_w, a_conv_b, a_norm_g, a_norm_b, b_norm_g, b_norm_b, b_spatial_w, b_spatial_b, ab_w_out, c_w_qkv, c_b_qkv, c_sinks, c_w_o, ffn_w_up, ffn_conv_w, ffn_conv_b, ffn_w_down, ln_g, ln_b, loss_target, m_ab_w_in, m_a_conv_w, m_a_conv_b, m_a_norm_g, m_a_norm_b, m_b_norm_g, m_b_norm_b, m_b_spatial_w, m_b_spatial_b, m_ab_w_out, m_c_w_qkv, m_c_b_qkv, m_c_sinks, m_c_w_o, m_ffn_w_up, m_ffn_conv_w, m_ffn_conv_b, m_ffn_w_down, m_ln_g, m_ln_b, v_ab_w_in, v_a_conv_w, v_a_conv_b, v_a_norm_g, v_a_norm_b, v_b_norm_g, v_b_norm_b, v_b_spatial_w, v_b_spatial_b, v_ab_w_out, v_c_w_qkv, v_c_b_qkv, v_c_sinks, v_c_w_o, v_ffn_w_up, v_ffn_conv_w, v_ffn_conv_b, v_ffn_w_down, v_ln_g, v_ln_b):
    me = 4 * lax.axis_index("x") + 2 * lax.axis_index("y") + lax.axis_index("c")
    xt = x[0]
    t = xt.shape[0]

    small_shard_shapes = [a_conv_w.shape, c_b_qkv.shape, ffn_conv_w.shape, ln_g.shape, ln_b.shape]
    up_shard = [jnp.swapaxes(ffn_w_up[l], 0, 1).astype(BF16) for l in range(2)]
    qkv_shard = jnp.swapaxes(c_w_qkv[0], 0, 1).astype(BF16)
    down_shard = [ffn_w_down[l].astype(BF16) for l in range(2)]
    g_win, g_small = _comm_only(
        _Comm(gather=[ab_w_in[0].astype(BF16), _pack([a_conv_w, c_b_qkv, ffn_conv_w, ln_g, ln_b])]), "gather_first")
    w_in = _interleave(g_win)
    g_acw, g_bqkv, g_fcw, g_lng, g_lnb = _unpack(g_small, small_shard_shapes)
    acw = _interleave(g_acw[:, 0])
    bqkv = g_bqkv[:, 0].reshape(1, -1)
    fcw = [_interleave(g_fcw[:, l]) for l in range(2)]
    lng = jnp.transpose(g_lng, (1, 2, 0, 3)).reshape(2, 2, 1, D_MODEL)
    lnb = jnp.transpose(g_lnb, (1, 2, 0, 3)).reshape(2, 2, 1, D_MODEL)
    fcb = [ffn_conv_b[l:l + 1] for l in range(2)]
    ms = b_spatial_w[0]
    mst = jnp.swapaxes(ms, 1, 2)
    sbt = b_spatial_b[0].T

    h0, (g_wout,) = _matmul(xt, w_in, "nn", BF16, "mix_in", 1024, 1024, 1024, comm=_Comm(gather=[ab_w_out[0].astype(BF16)]))
    w_out = g_wout.reshape(D_MODEL, D_MODEL)
    cat, (g_wup0,) = _mixer_mid_fwd(h0, acw, a_conv_b, a_norm_g, a_norm_b, b_norm_g, b_norm_b, ms, sbt, "mix_mid_fwd",
                                    comm=_Comm(gather=[up_shard[0]]))
    w_up0 = g_wup0.reshape(2 * D_FF, D_MODEL)
    z1, x1 = _matmul_res_ln(cat, w_out, xt, lng[0, 0], lnb[0, 0], "mix_out_ln", 512, D_MODEL)
    hf0, (g_wdown0, g_wqkv) = _matmul(x1, w_up0, "nt", BF16, "ffn0_up", 1024, 1408, 1024,
                                      comm=_Comm(gather=[down_shard[0], qkv_shard]))
    w_down0 = g_wdown0.reshape(D_FF, D_MODEL)
    w_qkv = g_wqkv.reshape(Q_WIDTH + KV_WIDTH, D_MODEL)
    u0, (g_wup1,) = _ffn_mid_fwd(hf0, fcw[0], fcb[0], "ffn0_mid_fwd", comm=_Comm(gather=[up_shard[1]]))
    w_up1 = g_wup1.reshape(2 * D_FF, D_MODEL)
    (z2, x2), (g_wo,) = _matmul_res_ln(u0, w_down0, x1, lng[0, 1], lnb[0, 1], "ffn0_down_ln", 512, D_FF,
                                       comm=_Comm(gather=[c_w_o[0].astype(BF16)]))
    w_o = g_wo.reshape(D_MODEL, D_MODEL)
    qkv = _matmul(x2, w_qkv, "nt", BF16, "att_qkv", 1024, 1280, 1024, bias=bqkv)
    att, (g_wdown1,) = _attn_fwd(qkv, c_sinks, "att_fwd", comm=_Comm(gather=[down_shard[1]]))
    w_down1 = g_wdown1.reshape(D_FF, D_MODEL)
    z3, x3 = _matmul_res_ln(att, w_o, x2, lng[1, 0], lnb[1, 0], "att_out_ln", 512, D_MODEL)
    hf1 = _matmul(x3, w_up1, "nt", BF16, "ffn1_up", 1024, 1408, 1024)
    u1 = _ffn_mid_fwd(hf1, fcw[1], fcb[1], "ffn1_mid_fwd")
    z4, _ = _matmul_res_ln(u1, w_down1, x3, lng[1, 1], lnb[1, 1], "ffn1_down_ln", 512, D_FF)

    dz4, dg11, db11, loss_terms = _ln_bwd_loss(z4, lng[1, 1], lnb[1, 1], loss_target[0], "loss_ln_bwd")
    dx3, d_wup1, d_fcw1, d_fcb1, (p_wdown1,) = _ffn_backward(dz4, x3, hf1, u1, w_up1, fcw[1], fcb[1], w_down1, 1)
    dz3, dg10, db10 = _ln_bwd(z3, lng[1, 0], dz4, dx3, "ln10_bwd")
    d_att = _matmul(dz3, w_o, "nt", BF16, "att_dout", 1024, 1024, 1024)
    d_wo = _matmul(att, dz3, "tn", BF16, "att_dwo", 1024, 1024, 512)
    (dq, dkv, dbq, dbkv, dsinks), (p_wup1,) = _attn_bwd(qkv, d_att, c_sinks, "att_bwd", comm=_Comm(exchange=[d_wup1]))
    d_wqkv = jnp.concatenate([_matmul(dq, x2, "tn", BF16, "att_dwq", 1024, 1024, 1024),
                              _matmul(dkv, x2, "tn", BF16, "att_dwkv", KV_WIDTH, 1024, 1024)], axis=0)
    dx2 = _matmul(dkv, w_qkv, "nn", F32, "att_dx_kv", 1024, 1024, KV_WIDTH, b_off=Q_WIDTH // KV_WIDTH)
    dx2 = _matmul(dq, w_qkv, "nn", F32, "att_dx_q", 1024, 1024, KV_WIDTH, res=dx2)
    dz2, dg01, db01 = _ln_bwd(z2, lng[0, 1], dz3, dx2, "ln01_bwd")
    dx1, d_wup0, d_fcw0, d_fcb0, (p_wdown0, p_wqkv, p_wo) = _ffn_backward(
        dz2, x1, hf0, u0, w_up0, fcw[0], fcb[0], w_down0, 0,
        exchange=[d_wqkv.reshape(N_DEV, -1, D_MODEL), d_wo.reshape(N_DEV, -1, D_MODEL)])
    dz1, dg00, db00 = _ln_bwd(z1, lng[0, 0], dz2, dx1, "ln00_bwd")
    dcat = _matmul(dz1, w_out, "nt", BF16, "mix_dcat", 1024, 1024, 1024)
    d_wout = _matmul(cat, dz1, "tn", BF16, "mix_dwout", 1024, 1024, 512)
    (dh0, d_acw, d_acb, d_ang, d_anb, d_bng, d_bnb, d_ms, d_sb), (p_wup0, p_wout) = _mixer_mid_bwd(
        h0, dcat, acw, a_conv_b, a_norm_g, a_norm_b, b_norm_g, b_norm_b, ms, mst, sbt, "mix_mid_bwd",
        comm=_Comm(exchange=[d_wup0, d_wout.reshape(N_DEV, -1, D_MODEL)]))
    d_bqkv = jnp.concatenate([dbq, dbkv], axis=1)
    d_lng = jnp.stack([jnp.stack([dg00, dg01]), jnp.stack([dg10, dg11])])
    d_lnb = jnp.stack([jnp.stack([db00, db01]), jnp.stack([db10, db11])])
    small_full = [d_acb, d_ang, d_anb, d_bng, d_bnb, d_ms, d_sb, dsinks[:, :N_Q_HEADS], jnp.concatenate([d_fcb0, d_fcb1], axis=0),
                  d_acw, d_bqkv, jnp.stack([d_fcw0, d_fcw1]), d_lng, d_lnb]
    d_win, (g_small_grads,) = _matmul(xt, dh0, "tn", BF16, "mix_dwin", 1024, 1024, 512, comm=_Comm(gather=[_pack(small_full)]))
    grad_x, (p_win,) = _matmul(dh0, w_in, "nt", F32, "mix_dx", 1024, 1024, 1024, res=dz1, res_scale=ALPHA,
                               comm=_Comm(exchange=[_deinterleave(d_win)]))

    loss = lax.psum(0.5 / D_MODEL * jnp.sum(loss_terms), ("x", "y", "c"))

    big = {}
    for nm, p, w, m, v, tr, transposed in [
            ("ab_w_in", [p_win], ab_w_in, m_ab_w_in, v_ab_w_in, 256, False),
            ("ab_w_out", [p_wout], ab_w_out, m_ab_w_out, v_ab_w_out, 128, False),
            ("c_w_qkv", [p_wqkv], c_w_qkv, m_c_w_qkv, v_c_w_qkv, 160, True), ("c_w_o", [p_wo], c_w_o, m_c_w_o, v_c_w_o, 128, False),
            ("ffn_w_up", [p_wup0, p_wup1], ffn_w_up, m_ffn_w_up, v_ffn_w_up, 176, True),
            ("ffn_w_down", [p_wdown0, p_wdown1], ffn_w_down, m_ffn_w_down, v_ffn_w_down, 176, False)]:
        def two_d(a):
            a = jnp.swapaxes(a, 1, 2) if transposed else a
            return a.reshape(-1, a.shape[-1])

        def back(o):
            return jnp.swapaxes(o.reshape(w.shape[0], w.shape[2], w.shape[1]), 1, 2) if transposed else o.reshape(w.shape)

        outs = _adamw_big(p, two_d(w), two_d(m), two_d(v), "adamw_" + nm, tr)
        big[nm] = [back(o) for o in outs]

    gs = _unpack(g_small_grads, [a.shape for a in small_full])

    def my_shard(g, width):
        g = g.reshape(g.shape[:-1] + (N_DEV, width))
        return lax.dynamic_index_in_dim(g, me, axis=g.ndim - 2, keepdims=False)

    small_names = ["a_conv_b", "a_norm_g", "a_norm_b", "b_norm_g", "b_norm_b", "b_spatial_w", "b_spatial_b", "c_sinks", "ffn_conv_b",
                   "a_conv_w", "c_b_qkv", "ffn_conv_w", "ln_g", "ln_b"]
    small_w = [a_conv_b, a_norm_g, a_norm_b, b_norm_g, b_norm_b, b_spatial_w, b_spatial_b, c_sinks, ffn_conv_b,
               a_conv_w, c_b_qkv, ffn_conv_w, ln_g, ln_b]
    small_m = [m_a_conv_b, m_a_norm_g, m_a_norm_b, m_b_norm_g, m_b_norm_b, m_b_spatial_w, m_b_spatial_b, m_c_sinks, m_ffn_conv_b,
               m_a_conv_w, m_c_b_qkv, m_ffn_conv_w, m_ln_g, m_ln_b]
    small_v = [v_a_conv_b, v_a_norm_g, v_a_norm_b, v_b_norm_g, v_b_norm_b, v_b_spatial_w, v_b_spatial_b, v_c_sinks, v_ffn_conv_b,
               v_a_conv_w, v_c_b_qkv, v_ffn_conv_w, v_ln_g, v_ln_b]
    gs[9:] = [my_shard(g, w.shape[-1]) for g, w in zip(gs[9:], small_w[9:])]
    two_d = [(-1, w.shape[-1]) for w in small_w]
    outs = _adamw_small([g.reshape((N_DEV,) + w.reshape(s).shape) for g, w, s in zip(gs, small_w, two_d)],
                        [w.reshape(s) for w, s in zip(small_w, two_d)], [m.reshape(s) for m, s in zip(small_m, two_d)],
                        [v.reshape(s) for v, s in zip(small_v, two_d)], "adamw_small")
    small = {nm: [o.reshape(w.shape) for o in outs[4 * a:4 * a + 4]] for a, (nm, w) in enumerate(zip(small_names, small_w))}

    res = {**big, **small}
    order = ["ab_w_in", "a_conv_w", "a_conv_b", "a_norm_g", "a_norm_b", "b_norm_g", "b_norm_b", "b_spatial_w", "b_spatial_b", "ab_w_out",
             "c_w_qkv", "c_b_qkv", "c_sinks", "c_w_o", "ffn_w_up", "ffn_conv_w", "ffn_conv_b", "ffn_w_down", "ln_g", "ln_b"]
    return (loss, grad_x[None], *[res[nm][0] for nm in order], *[res[nm][1] for nm in order],
            *[res[nm][2] for nm in order], *[res[nm][3] for nm in order])
```

```python
import functools
import math

import jax
import jax.numpy as jnp
from jax import lax
from jax.experimental import pallas as pl
from jax.experimental.pallas import tpu as pltpu

F32 = jnp.float32
BF16 = jnp.bfloat16

N_DEV = 8
D_MODEL = 1024
A_WIDTH = 512
A_KERNEL = 31
B_GROUPS = 4
B_CHUNK = 128
HEAD_DIM = 64
N_Q_HEADS = 16
N_KV_HEADS = 2
ATT_BLOCK = 128
D_FF = 2816
FFN_KERNEL = 3
ALPHA = (2.0 * 2) ** 0.25
LN_EPS = 1e-5
GELU_K = math.sqrt(2.0 / math.pi)
GELU_C = 0.044715
ADAM_LR = 0.001
ADAM_B1 = 0.9
ADAM_B2 = 0.999
ADAM_EPS = 1e-08
ADAM_WD = 0.01
ADAM_STEP = 10
VMEM_LIMIT = 56 * 1024 * 1024
MESH_ID = pl.DeviceIdType.MESH


def _params(*sem):
    return pltpu.CompilerParams(dimension_semantics=sem, vmem_limit_bytes=VMEM_LIMIT)


def _gelu(x):
    t = jnp.tanh(GELU_K * x * (1.0 + GELU_C * x * x))
    return 0.5 * x * (1.0 + t)


def _gelu_and_grad(x):
    x2 = x * x
    t = jnp.tanh(GELU_K * x * (1.0 + GELU_C * x2))
    g = 0.5 * x * (1.0 + t)
    dg = 0.5 * (1.0 + t) + 0.5 * x * (1.0 - t * t) * (GELU_K * (1.0 + 3.0 * GELU_C * x2))
    return g, dg


def _sigmoid(x):
    return 1.0 / (1.0 + jnp.exp(-x))


def _ln_stats(z):
    mu = jnp.mean(z, axis=-1, keepdims=True)
    zc = z - mu
    var = jnp.mean(zc * zc, axis=-1, keepdims=True)
    r = lax.rsqrt(var + LN_EPS)
    return zc * r, r


def _ln_bwd_rows(dn, nh, r):
    return r * (dn - jnp.mean(dn, axis=-1, keepdims=True) - nh * jnp.mean(dn * nh, axis=-1, keepdims=True))


def _colsum(x):
    return jnp.sum(x, axis=0, keepdims=True)


def _dot(a, b, dims):
    return lax.dot_general(a.astype(BF16), b.astype(BF16), (dims, ((), ())), preferred_element_type=F32)


NN = ((1,), (0,))
NT = ((1,), (1,))
TN = ((0,), (0,))


ANY = pl.BlockSpec(memory_space=pl.ANY)
N_RELATIONS = N_DEV - 1


def _my_place():
    return lax.axis_index("x"), lax.axis_index("y"), lax.axis_index("c")


class _Comm:
    def __init__(self, gather=(), exchange=()):
        self.arrs = list(gather) + list(exchange)
        self.n_gather = len(gather)
        self.n = len(self.arrs)

    def out_shape(self):
        return [jax.ShapeDtypeStruct(((N_DEV,) + a.shape) if i < self.n_gather else a.shape, a.dtype)
                for i, a in enumerate(self.arrs)]

    def sems(self):
        return [pltpu.SemaphoreType.DMA((self.n, N_RELATIONS)), pltpu.SemaphoreType.DMA((self.n, N_RELATIONS)),
                pltpu.SemaphoreType.DMA((self.n,))]

    def _gather_copy(self, ins, outs, sems, a, k, place, to, from_input=False):
        px, py, pc = place
        block = outs[a].at[4 * px + 2 * py + pc]
        return pltpu.make_async_remote_copy(
            src_ref=ins[a] if from_input else block, dst_ref=block,
            send_sem=sems[0].at[a, k], recv_sem=sems[1].at[a, k], device_id=to, device_id_type=MESH_ID)

    def _exchange_copy(self, ins, outs, sems, a, k, landing=False):
        x, y, c = _my_place()
        me = 4 * x + 2 * y + c
        peer = (x ^ (k >> 2), y ^ ((k >> 1) & 1), c ^ (k & 1))
        return pltpu.make_async_remote_copy(
            src_ref=ins[a].at[me ^ k], dst_ref=outs[a].at[(me ^ k) if landing else me],
            send_sem=sems[0].at[a, k - 1], recv_sem=sems[1].at[a, k - 1], device_id=peer, device_id_type=MESH_ID)

    def _local_copy(self, ins, outs, sems, a):
        x, y, c = _my_place()
        me = 4 * x + 2 * y + c
        src = ins[a] if a < self.n_gather else ins[a].at[me]
        return pltpu.make_async_copy(src, outs[a].at[me], sems[2].at[a])

    def _first_stage(self, ins, outs, sems, a):
        x, y, c = _my_place()
        me = (x, y, c)
        chips = [(1 - x, y), (x, 1 - y), (1 - x, 1 - y)]
        return ([self._gather_copy(ins, outs, sems, a, 0, me, (x, y, 1 - c), from_input=True)]
                + [self._gather_copy(ins, outs, sems, a, 1 + j, me, (*chip, c), from_input=True) for j, chip in enumerate(chips)])

    def start(self, ins, outs, sems):
        for a in range(self.n):
            self._local_copy(ins, outs, sems, a).start()
        for a in range(self.n_gather):
            for cp in self._first_stage(ins, outs, sems, a):
                cp.start()
        for k in range(1, N_DEV):
            for a in range(self.n_gather, self.n):
                self._exchange_copy(ins, outs, sems, a, k).start()

    def finish(self, ins, outs, sems):
        x, y, c = _my_place()
        me, sibling = (x, y, c), (x, y, 1 - c)
        chips = [(1 - x, y), (x, 1 - y), (1 - x, 1 - y)]
        passed = []
        for j, chip in enumerate(chips):
            for a in range(self.n_gather):
                self._gather_copy(ins, outs, sems, a, 1 + j, (*chip, c), me).wait_recv()
                fwd = self._gather_copy(ins, outs, sems, a, 4 + j, (*chip, c), sibling)
                fwd.start()
                passed.append(fwd)
        for a in range(self.n_gather):
            self._gather_copy(ins, outs, sems, a, 0, sibling, me).wait_recv()
            for j, chip in enumerate(chips):
                self._gather_copy(ins, outs, sems, a, 4 + j, (*chip, 1 - c), me).wait_recv()
        for k in range(1, N_DEV):
            for a in range(self.n_gather, self.n):
                self._exchange_copy(ins, outs, sems, a, k, landing=True).wait_recv()
        for a in range(self.n_gather):
            for cp in self._first_stage(ins, outs, sems, a):
                cp.wait_send()
        for cp in passed:
            cp.wait_send()
        for k in range(1, N_DEV):
            for a in range(self.n_gather, self.n):
                self._exchange_copy(ins, outs, sems, a, k).wait_send()
        for a in range(self.n):
            self._local_copy(ins, outs, sems, a).wait()


def _comm_only(comm, name):
    def body(*refs):
        ins, outs, sems = refs[:comm.n], refs[comm.n:2 * comm.n], refs[2 * comm.n:]
        comm.start(ins, outs, sems)
        comm.finish(ins, outs, sems)

    return pl.pallas_call(body, name=name, in_specs=[ANY] * comm.n, out_specs=[ANY] * comm.n,
                          out_shape=comm.out_shape(), scratch_shapes=comm.sems())(*comm.arrs)


def _call(body, *, name, grid, in_specs, out_specs, out_shape, args, sem, scratch_shapes=(), comm=None):
    in_specs, out_specs, out_shape, scratch_shapes = list(in_specs), list(out_specs), list(out_shape), list(scratch_shapes)
    if comm is None:
        outs = pl.pallas_call(body, name=name, grid=grid, in_specs=in_specs, out_specs=out_specs, out_shape=out_shape,
                              scratch_shapes=scratch_shapes, compiler_params=_params(*sem))(*args)
        return list(outs), []
    n_in, n_out, n_scr, nc = len(in_specs), len(out_specs), len(scratch_shapes), comm.n

    def wrapped(*refs):
        ins, refs = refs[:n_in], refs[n_in:]
        c_in, refs = refs[:nc], refs[nc:]
        outs, refs = refs[:n_out], refs[n_out:]
        c_out, refs = refs[:nc], refs[nc:]
        scr, sems = refs[:n_scr], refs[n_scr:]
        first = functools.reduce(jnp.logical_and, [pl.program_id(ax) == 0 for ax in range(len(grid))])
        last = functools.reduce(jnp.logical_and, [pl.program_id(ax) == g - 1 for ax, g in enumerate(grid)])

        @pl.when(first)
        def _():
            comm.start(c_in, c_out, sems)

        body(*ins, *outs, *scr)

        @pl.when(last)
        def _():
            comm.finish(c_in, c_out, sems)

    outs = pl.pallas_call(
        wrapped, name=name, grid=grid, in_specs=in_specs + [ANY] * nc, out_specs=out_specs + [ANY] * nc,
        out_shape=out_shape + comm.out_shape(), scratch_shapes=scratch_shapes + comm.sems(),
        compiler_params=_params(*(["arbitrary"] * len(grid))))(*args, *comm.arrs)
    return list(outs[:n_out]), list(outs[n_out:])


def _matmul(a, b, mode, out_dtype, name, tm, tn, tk, *, bias=None, res=None, res_scale=1.0, b_off=0, comm=None):
    tm = min(tm, a.shape[1] if mode == "tn" else a.shape[0])
    tk = min(tk, a.shape[0] if mode == "tn" else a.shape[1])
    if mode == "nn":
        (m, k), n = a.shape, b.shape[1]
        a_spec = pl.BlockSpec((tm, tk), lambda i, j, kk: (i, kk))
        b_spec = pl.BlockSpec((tk, tn), lambda i, j, kk: (kk + b_off, j))
        dims = NN
    elif mode == "nt":
        (m, k), n = a.shape, b.shape[0]
        a_spec = pl.BlockSpec((tm, tk), lambda i, j, kk: (i, kk))
        b_spec = pl.BlockSpec((tn, tk), lambda i, j, kk: (j, kk + b_off))
        dims = NT
    else:
        (k, m), n = a.shape, b.shape[1]
        a_spec = pl.BlockSpec((tk, tm), lambda i, j, kk: (kk, i))
        b_spec = pl.BlockSpec((tk, tn), lambda i, j, kk: (kk, j))
        dims = TN
    assert m % tm == 0 and n % tn == 0 and k % tk == 0, (name, m, n, k)
    nk = k // tk
    in_specs = [a_spec, b_spec]
    args = [a, b]
    if bias is not None:
        in_specs.append(pl.BlockSpec((1, tn), lambda i, j, kk: (0, j)))
        args.append(bias)
    if res is not None:
        in_specs.append(pl.BlockSpec((tm, tn), lambda i, j, kk: (i, j)))
        args.append(res)

    def finish(out, refs, o_ref):
        pos = 2
        if bias is not None:
            out = out + refs[pos][...]
            pos += 1
        if res is not None:
            out = out + res_scale * refs[pos][...].astype(F32)
        o_ref[...] = out.astype(out_dtype)

    def body_one_step(*refs):
        finish(_dot(refs[0][...], refs[1][...], dims), refs, refs[-1])

    def body(*refs):
        a_ref, b_ref = refs[0], refs[1]
        o_ref, acc = refs[-2], refs[-1]
        kk = pl.program_id(2)

        @pl.when(kk == 0)
        def _():
            acc[...] = jnp.zeros_like(acc)

        acc[...] += _dot(a_ref[...], b_ref[...], dims)

        @pl.when(kk == nk - 1)
        def _():
            finish(acc[...], refs, o_ref)

    (out,), moved = _call(
        body_one_step if nk == 1 else body, name=name, grid=(m // tm, n // tn, nk),
        in_specs=in_specs, out_specs=[pl.BlockSpec((tm, tn), lambda i, j, kk: (i, j))],
        out_shape=[jax.ShapeDtypeStruct((m, n), out_dtype)],
        scratch_shapes=[] if nk == 1 else [pltpu.VMEM((tm, tn), F32)],
        sem=("parallel", "parallel", "arbitrary"), args=args, comm=comm)
    return out if comm is None else (out, moved)


def _matmul_tn_pair(a0, a1, b, out_dtype, name, tm, tn, tk):
    (k, m), n = a0.shape, b.shape[1]
    tk = min(tk, k)
    assert a1.shape == a0.shape and m % tm == 0 and n % tn == 0 and k % tk == 0, (name, m, n, k)
    mi, nk = m // tm, k // tk

    def body(a0_ref, a1_ref, b_ref, o_ref, acc):
        i, kk = pl.program_id(0), pl.program_id(2)

        @pl.when(kk == 0)
        def _():
            acc[...] = jnp.zeros_like(acc)

        @pl.when(i < mi)
        def _():
            acc[...] += _dot(a0_ref[...], b_ref[...], TN)

        @pl.when(i >= mi)
        def _():
            acc[...] += _dot(a1_ref[...], b_ref[...], TN)

        @pl.when(kk == nk - 1)
        def _():
            o_ref[...] = acc[...].astype(out_dtype)

    return pl.pallas_call(
        body, name=name, grid=(2 * mi, n // tn, nk),
        in_specs=[pl.BlockSpec((tk, tm), lambda i, j, kk: (jnp.where(i < mi, kk, nk - 1), jnp.minimum(i, mi - 1))),
                  pl.BlockSpec((tk, tm), lambda i, j, kk: (jnp.where(i >= mi, kk, 0), jnp.maximum(i - mi, 0))),
                  pl.BlockSpec((tk, tn), lambda i, j, kk: (kk, j))],
        out_specs=pl.BlockSpec((tm, tn), lambda i, j, kk: (i, j)),
        out_shape=jax.ShapeDtypeStruct((2 * m, n), out_dtype),
        scratch_shapes=[pltpu.VMEM((tm, tn), F32)],
        compiler_params=_params("parallel", "parallel", "arbitrary"),
    )(a0, a1, b)


def _matmul_res_ln(a, b, xres, g, beta, name, tm, tk, comm=None):
    t, k = a.shape
    d = b.shape[1]
    nk = k // tk
    assert t % tm == 0 and k % tk == 0

    def body(a_ref, b_ref, x_ref, g_ref, beta_ref, z_ref, xo_ref, acc):
        kk = pl.program_id(1)

        @pl.when(kk == 0)
        def _():
            acc[...] = jnp.zeros_like(acc)

        acc[...] += _dot(a_ref[...], b_ref[...], NN)

        @pl.when(kk == nk - 1)
        def _():
            z = ALPHA * x_ref[...] + acc[...]
            nh, _ = _ln_stats(z)
            z_ref[...] = z
            xo_ref[...] = nh * g_ref[...] + beta_ref[...]

    row = pl.BlockSpec((tm, d), lambda i, kk: (i, 0))
    vec = pl.BlockSpec((1, d), lambda i, kk: (0, 0))
    outs, moved = _call(
        body, name=name, grid=(t // tm, nk),
        in_specs=[pl.BlockSpec((tm, tk), lambda i, kk: (i, kk)), pl.BlockSpec((tk, d), lambda i, kk: (kk, 0)), row, vec, vec],
        out_specs=[row, row],
        out_shape=[jax.ShapeDtypeStruct((t, d), F32), jax.ShapeDtypeStruct((t, d), F32)],
        scratch_shapes=[pltpu.VMEM((tm, d), F32)],
        sem=("parallel", "arbitrary"), args=(a, b, xres, g, beta), comm=comm)
    return outs if comm is None else (outs, moved)


def _matmul_ln_bwd(a, b, z, g, dres, name, tm, *, res=None, b_off=0):
    m, k = a.shape
    d = b.shape[1]
    tm = min(tm, m)
    assert m % tm == 0

    def body(*refs):
        a_ref, b_ref, z_ref, g_ref, dres_ref = refs[:5]
        dz_ref, dg_ref, db_ref = refs[-3:]

        @pl.when(pl.program_id(0) == 0)
        def _():
            dg_ref[...] = jnp.zeros_like(dg_ref)
            db_ref[...] = jnp.zeros_like(db_ref)

        dbr = _dot(a_ref[...], b_ref[...], NN)
        if res is not None:
            dbr = dbr + refs[5][...]
        nh, r = _ln_stats(z_ref[...])
        dy = ALPHA * dres_ref[...] + dbr
        dg_ref[...] += _colsum(dy * nh)
        db_ref[...] += _colsum(dy)
        dz_ref[...] = _ln_bwd_rows(dy * g_ref[...], nh, r)

    row = pl.BlockSpec((tm, d), lambda i: (i, 0))
    vec = pl.BlockSpec((1, d), lambda i: (0, 0))
    vshape = jax.ShapeDtypeStruct((1, d), F32)
    return pl.pallas_call(
        body, name=name, grid=(m // tm,),
        in_specs=[pl.BlockSpec((tm, k), lambda i: (i, 0)), pl.BlockSpec((k, d), lambda i: (b_off, 0)), row, vec, row]
        + ([row] if res is not None else []),
        out_specs=[row, vec, vec], out_shape=[jax.ShapeDtypeStruct((m, d), F32), vshape, vshape],
        compiler_params=_params("arbitrary"),
    )(a, b, z, g, dres, *([res] if res is not None else []))


def _ln_bwd_loss(z, g, beta, target, name, tm=512):
    t, d = z.shape

    def body(z_ref, g_ref, beta_ref, t_ref, dz_ref, dg_ref, db_ref, loss_ref):
        @pl.when(pl.program_id(0) == 0)
        def _():
            dg_ref[...] = jnp.zeros_like(dg_ref)
            db_ref[...] = jnp.zeros_like(db_ref)
            loss_ref[...] = jnp.zeros_like(loss_ref)

        nh, r = _ln_stats(z_ref[...])
        err = nh * g_ref[...] + beta_ref[...] - t_ref[...]
        loss_ref[...] += _colsum(err * err)
        dy = err * (1.0 / d)
        dg_ref[...] += _colsum(dy * nh)
        db_ref[...] += _colsum(dy)
        dz_ref[...] = _ln_bwd_rows(dy * g_ref[...], nh, r)

    row = pl.BlockSpec((tm, d), lambda i: (i, 0))
    vec = pl.BlockSpec((1, d), lambda i: (0, 0))
    vshape = jax.ShapeDtypeStruct((1, d), F32)
    return pl.pallas_call(
        body, name=name, grid=(t // tm,), in_specs=[row, vec, vec, row], out_specs=[row, vec, vec, vec],
        out_shape=[jax.ShapeDtypeStruct((t, d), F32), vshape, vshape, vshape],
        compiler_params=_params("arbitrary"),
    )(z, g, beta, target)


FFN_HALO = 16
FFN_CHUNK = 256
LANES = 128
SUBLANES = 8


def _fold(x):
    return jnp.sum(x.reshape(x.shape[0] // SUBLANES, SUBLANES, x.shape[1]), axis=0)


def _ffn_mid_fwd(h, cw, cb, name, tm=1024, tc=256, comm=None):
    t, f2 = h.shape
    tm = min(tm, t)
    f = f2 // 2
    nj, nt, hb = f // tc, t // tm, tm // FFN_HALO

    ch = min(FFN_CHUNK, tm)

    def body(hg, hgp, hv, hvp, cwg, cwv, cbg, cbv, u_ref, sg, sv):
        i = pl.program_id(1)
        for main, prev, s in ((hg, hgp, sg), (hv, hvp, sv)):
            s[0:FFN_HALO, :] = jnp.where(i > 0, prev[...].astype(F32), 0.0)
            s[FFN_HALO:, :] = main[...].astype(F32)
        o = SUBLANES - FFN_KERNEL + 1
        for lg in range(tc // LANES):
            cols = slice(lg * LANES, (lg + 1) * LANES)
            wg, wv = [cwg[k:k + 1, cols] for k in range(FFN_KERNEL)], [cwv[k:k + 1, cols] for k in range(FFN_KERNEL)]
            bg, bv = cbg[:, cols], cbv[:, cols]

            def chunk(c, carry):
                base = pl.multiple_of(c * ch, ch)
                eg = sg[pl.ds(base + FFN_HALO - SUBLANES, ch + SUBLANES), cols]
                ev = sv[pl.ds(base + FFN_HALO - SUBLANES, ch + SUBLANES), cols]
                cg = wg[0] * eg[o:o + ch] + wg[1] * eg[o + 1:o + 1 + ch] + wg[2] * eg[o + 2:o + 2 + ch] + bg
                cv = wv[0] * ev[o:o + ch] + wv[1] * ev[o + 1:o + 1 + ch] + wv[2] * ev[o + 2:o + 2 + ch] + bv
                u_ref[pl.ds(base, ch), cols] = (_gelu(cg) * cv).astype(BF16)
                return carry

            lax.fori_loop(0, tm // ch, chunk, 0)

    def main_spec(off):
        return pl.BlockSpec((tm, tc), lambda j, i: (i, j + off))

    def prev_spec(off):
        return pl.BlockSpec((FFN_HALO, tc), lambda j, i: (jnp.maximum(i * hb - 1, 0), j + off))

    def par_spec(rows, off):
        return pl.BlockSpec((rows, tc), lambda j, i: (0, j + off))

    (u,), moved = _call(
        body, name=name, grid=(nj, nt),
        in_specs=[main_spec(0), prev_spec(0), main_spec(nj), prev_spec(nj),
                  par_spec(FFN_KERNEL, 0), par_spec(FFN_KERNEL, nj), par_spec(1, 0), par_spec(1, nj)],
        out_specs=[pl.BlockSpec((tm, tc), lambda j, i: (i, j))],
        out_shape=[jax.ShapeDtypeStruct((t, f), BF16)],
        scratch_shapes=[pltpu.VMEM((tm + FFN_HALO, tc), F32), pltpu.VMEM((tm + FFN_HALO, tc), F32)],
        sem=("parallel", "arbitrary"), args=(h, h, h, h, cw, cw, cb, cb), comm=comm)
    return u if comm is None else (u, moved)


def _ffn_mid_bwd(h, du, cw, cb, name, tm=1024, tc=256, comm=None):
    t, f2 = h.shape
    tm = min(tm, t)
    f = f2 // 2
    nj, nt, hb = f // tc, t // tm, tm // FFN_HALO

    ch = min(FFN_CHUNK, tm)
    ahead = ch + SUBLANES

    def body(hg, hgp, hgn, hv, hvp, hvn, du_ref, dun_ref, cwg, cwv, cbg, cbv,
             dhg_ref, dhv_ref, dcwg_ref, dcwv_ref, dcbg_ref, dcbv_ref, sg, sv, sdu):
        i = pl.program_id(1)

        @pl.when(i == 0)
        def _():
            for ref in (dcwg_ref, dcwv_ref, dcbg_ref, dcbv_ref):
                ref[...] = jnp.zeros_like(ref)

        for main, prev, nxt, s in ((hg, hgp, hgn, sg), (hv, hvp, hvn, sv)):
            s[0:FFN_HALO, :] = jnp.where(i > 0, prev[...].astype(F32), 0.0)
            s[FFN_HALO:FFN_HALO + tm, :] = main[...].astype(F32)
            s[FFN_HALO + tm:, :] = nxt[...].astype(F32)
        sdu[0:tm, :] = du_ref[...].astype(F32)
        sdu[tm:, :] = jnp.where(i < nt - 1, dun_ref[...].astype(F32), 0.0)
        o = SUBLANES - FFN_KERNEL + 1
        for lg in range(tc // LANES):
            cols = slice(lg * LANES, (lg + 1) * LANES)
            wg, wv = [cwg[k:k + 1, cols] for k in range(FFN_KERNEL)], [cwv[k:k + 1, cols] for k in range(FFN_KERNEL)]
            bg, bv = cbg[:, cols], cbv[:, cols]

            def chunk(c, acc):
                base = pl.multiple_of(c * ch, ch)
                eg = sg[pl.ds(base + FFN_HALO - SUBLANES, ahead + SUBLANES), cols]
                ev = sv[pl.ds(base + FFN_HALO - SUBLANES, ahead + SUBLANES), cols]
                hgs = [eg[o + k:o + k + ahead] for k in range(FFN_KERNEL)]
                hvs = [ev[o + k:o + k + ahead] for k in range(FFN_KERNEL)]
                cg = wg[0] * hgs[0] + wg[1] * hgs[1] + wg[2] * hgs[2] + bg
                cv = wv[0] * hvs[0] + wv[1] * hvs[1] + wv[2] * hvs[2] + bv
                du_e = sdu[pl.ds(base, ahead), cols]
                gl, dgl = _gelu_and_grad(cg)

                def back(d, hs, w, dh_ref):
                    own = d[0:ch]
                    dh = w[2] * own + w[1] * d[1:1 + ch] + w[0] * d[2:2 + ch]
                    dh_ref[pl.ds(base, ch), cols] = dh.astype(BF16)
                    return [_fold(own)] + [_fold(own * hs[k][0:ch]) for k in range(FFN_KERNEL)]

                sums = back(du_e * cv * dgl, hgs, wg, dhg_ref) + back(du_e * gl, hvs, wv, dhv_ref)
                return tuple(a + s_ for a, s_ in zip(acc, sums))

            zero = jnp.zeros((SUBLANES, LANES), F32)
            acc = lax.fori_loop(0, tm // ch, chunk, (zero,) * (2 * (1 + FFN_KERNEL)))
            dcbg_ref[:, cols] += _colsum(acc[0])
            dcbv_ref[:, cols] += _colsum(acc[1 + FFN_KERNEL])
            for k in range(FFN_KERNEL):
                dcwg_ref[k:k + 1, cols] += _colsum(acc[1 + k])
                dcwv_ref[k:k + 1, cols] += _colsum(acc[2 + FFN_KERNEL + k])

    last_blk = t // FFN_HALO - 1

    def main_spec(off):
        return pl.BlockSpec((tm, tc), lambda j, i: (i, j + off))

    def prev_spec(off):
        return pl.BlockSpec((FFN_HALO, tc), lambda j, i: (jnp.maximum(i * hb - 1, 0), j + off))

    def next_spec(off):
        return pl.BlockSpec((FFN_HALO, tc), lambda j, i: (jnp.minimum((i + 1) * hb, last_blk), j + off))

    def par_spec(rows, off):
        return pl.BlockSpec((rows, tc), lambda j, i: (0, j + off))

    out_tile = pl.BlockSpec((tm, tc), lambda j, i: (i, j))
    outs, moved = _call(
        body, name=name, grid=(nj, nt),
        in_specs=[main_spec(0), prev_spec(0), next_spec(0), main_spec(nj), prev_spec(nj), next_spec(nj),
                  main_spec(0), next_spec(0),
                  par_spec(FFN_KERNEL, 0), par_spec(FFN_KERNEL, nj), par_spec(1, 0), par_spec(1, nj)],
        out_specs=[out_tile, out_tile, par_spec(FFN_KERNEL, 0), par_spec(FFN_KERNEL, 0), par_spec(1, 0), par_spec(1, 0)],
        out_shape=[jax.ShapeDtypeStruct((t, f), BF16), jax.ShapeDtypeStruct((t, f), BF16),
                   jax.ShapeDtypeStruct((FFN_KERNEL, f), F32), jax.ShapeDtypeStruct((FFN_KERNEL, f), F32),
                   jax.ShapeDtypeStruct((1, f), F32), jax.ShapeDtypeStruct((1, f), F32)],
        scratch_shapes=[pltpu.VMEM((tm + 2 * FFN_HALO, tc), F32), pltpu.VMEM((tm + 2 * FFN_HALO, tc), F32),
                        pltpu.VMEM((tm + FFN_HALO, tc), F32)],
        sem=("parallel", "arbitrary"), args=(h, h, h, h, h, h, du, du, cw, cw, cb, cb), comm=comm)
    return outs if comm is None else (outs, moved)


MIX_HALO = 32


def _glu(hh):
    return hh[:, 0:A_WIDTH] * _sigmoid(hh[:, A_WIDTH:2 * A_WIDTH])


def _fill_row_shifts(s):
    rows = s.shape[1] - SUBLANES
    for j in range(1, SUBLANES):
        s[j, 0:rows, :] = s[0, pl.ds(j, rows), :]


def _rows_from(s, start, rows):
    j = start % SUBLANES
    return s[j, start - j:start - j + rows, :]


def _tril_mask():
    return lax.broadcasted_iota(jnp.int32, (B_CHUNK, B_CHUNK), 0) >= lax.broadcasted_iota(jnp.int32, (B_CHUNK, B_CHUNK), 1)


def _spatial_mix(q, ms_ref, sbt_ref, tm):
    mask = _tril_mask()
    ws = [jnp.where(mask, ms_ref[g], 0.0).astype(BF16) for g in range(B_GROUPS)]
    qb = q.astype(BF16)
    rows = []
    for c in range(tm // B_CHUNK):
        cols = [_dot(ws[g], qb[c * B_CHUNK:(c + 1) * B_CHUNK, g * 128:(g + 1) * 128], NN) + sbt_ref[:, g:g + 1]
                for g in range(B_GROUPS)]
        rows.append(jnp.concatenate(cols, axis=1))
    return jnp.concatenate(rows, axis=0)


def _mixer_mid_fwd(h, cw, cb, ag, ab, bg, bb, ms, sbt, name, tm=256, comm=None):
    t = h.shape[0]
    nt, hb = t // tm, tm // MIX_HALO
    o = MIX_HALO - A_KERNEL + 1

    def body(h_ref, hp_ref, cw_ref, cb_ref, ag_ref, ab_ref, bg_ref, bb_ref, ms_ref, sbt_ref, cat_ref, sp):
        i = pl.program_id(0)
        sp[0, 0:MIX_HALO, :] = jnp.where(i > 0, _glu(hp_ref[:, 0:2 * A_WIDTH].astype(F32)), 0.0)
        sp[0, MIX_HALO:, :] = _glu(h_ref[:, 0:2 * A_WIDTH].astype(F32))
        _fill_row_shifts(sp)
        y = jnp.zeros((tm, A_WIDTH), F32) + cb_ref[...]
        for k in range(A_KERNEL):
            y = y + cw_ref[k:k + 1, :] * _rows_from(sp, o + k, tm)
        nh, _ = _ln_stats(y)
        ln = nh * ag_ref[...] + ab_ref[...]
        cat_ref[:, 0:A_WIDTH] = (ln * _sigmoid(ln)).astype(BF16)
        u = _gelu(h_ref[:, 1024:1536].astype(F32))
        nb, _ = _ln_stats(_gelu(h_ref[:, 1536:2048].astype(F32)))
        mixed = _spatial_mix(nb * bg_ref[...] + bb_ref[...], ms_ref, sbt_ref, tm)
        cat_ref[:, A_WIDTH:] = (u * mixed).astype(BF16)

    vec = pl.BlockSpec((1, A_WIDTH), lambda i: (0, 0))
    (cat,), moved = _call(
        body, name=name, grid=(nt,),
        in_specs=[pl.BlockSpec((tm, 2048), lambda i: (i, 0)),
                  pl.BlockSpec((MIX_HALO, 2048), lambda i: (jnp.maximum(i * hb - 1, 0), 0)),
                  pl.BlockSpec((A_KERNEL, A_WIDTH), lambda i: (0, 0)), vec, vec, vec, vec, vec,
                  pl.BlockSpec((B_GROUPS, B_CHUNK, B_CHUNK), lambda i: (0, 0, 0)),
                  pl.BlockSpec((B_CHUNK, B_GROUPS), lambda i: (0, 0))],
        out_specs=[pl.BlockSpec((tm, D_MODEL), lambda i: (i, 0))],
        out_shape=[jax.ShapeDtypeStruct((t, D_MODEL), BF16)],
        scratch_shapes=[pltpu.VMEM((SUBLANES, tm + MIX_HALO, A_WIDTH), F32)],
        sem=("parallel",), args=(h, h, cw, cb, ag, ab, bg, bb, ms, sbt), comm=comm)
    return cat if comm is None else (cat, moved)


def _mixer_mid_bwd(h, dcat, cw, cb, ag, ab, bg, bb, ms, mst, sbt, name, tm=256, comm=None):
    t = h.shape[0]
    nt, hb = t // tm, tm // MIX_HALO
    o = MIX_HALO - A_KERNEL + 1
    r = tm + MIX_HALO
    nchunk = tm // B_CHUNK

    def body(h_ref, hp_ref, hn_ref, dc_ref, dcn_ref, cw_ref, cb_ref, ag_ref, ab_ref, bg_ref, bb_ref, ms_ref, mst_ref, sbt_ref,
             dh_ref, dcw_ref, dcb_ref, dag_ref, dab_ref, dbg_ref, dbb_ref, dms_ref, dsb_ref, sp, sdy, sbacc):
        i = pl.program_id(0)

        @pl.when(i == 0)
        def _():
            for ref in (dcw_ref, dcb_ref, dag_ref, dab_ref, dbg_ref, dbb_ref, dms_ref, dsb_ref, sbacc):
                ref[...] = jnp.zeros_like(ref)

        sp[0, 0:MIX_HALO, :] = jnp.where(i > 0, _glu(hp_ref[:, 0:2 * A_WIDTH].astype(F32)), 0.0)
        sp[0, MIX_HALO:MIX_HALO + tm, :] = _glu(h_ref[:, 0:2 * A_WIDTH].astype(F32))
        sp[0, MIX_HALO + tm:, :] = _glu(hn_ref[:, 0:2 * A_WIDTH].astype(F32))
        _fill_row_shifts(sp)
        y = jnp.zeros((r, A_WIDTH), F32) + cb_ref[...]
        for k in range(A_KERNEL):
            y = y + cw_ref[k:k + 1, :] * _rows_from(sp, o + k, r)
        nh, rs = _ln_stats(y)
        ln = nh * ag_ref[...] + ab_ref[...]
        sg = _sigmoid(ln)
        dao = jnp.concatenate([dc_ref[:, 0:A_WIDTH].astype(F32),
                               jnp.where(i < nt - 1, dcn_ref[:, 0:A_WIDTH].astype(F32), 0.0)], axis=0)
        dln = dao * (sg * (1.0 + ln * (1.0 - sg)))
        dag_ref[...] += _colsum(dln[0:tm] * nh[0:tm])
        dab_ref[...] += _colsum(dln[0:tm])
        sdy[0] = _ln_bwd_rows(dln * ag_ref[...], nh, rs)
        _fill_row_shifts(sdy)
        dy_own = sdy[0, 0:tm, :]
        dcb_ref[...] += _colsum(dy_own)
        dp = jnp.zeros((tm, A_WIDTH), F32)
        for k in range(A_KERNEL):
            dcw_ref[k:k + 1, :] += _colsum(dy_own * _rows_from(sp, o + k, tm))
            dp = dp + cw_ref[k:k + 1, :] * _rows_from(sdy, A_KERNEL - 1 - k, tm)
        av = h_ref[:, 0:A_WIDTH].astype(F32)
        s = _sigmoid(h_ref[:, A_WIDTH:2 * A_WIDTH].astype(F32))
        dh_ref[:, 0:A_WIDTH] = (dp * s).astype(BF16)
        dh_ref[:, A_WIDTH:2 * A_WIDTH] = (dp * av * s * (1.0 - s)).astype(BF16)

        u, dgu = _gelu_and_grad(h_ref[:, 1024:1536].astype(F32))
        w, dgw = _gelu_and_grad(h_ref[:, 1536:2048].astype(F32))
        nb, rb = _ln_stats(w)
        q = nb * bg_ref[...] + bb_ref[...]
        mixed = _spatial_mix(q, ms_ref, sbt_ref, tm)
        dbo = dc_ref[:, A_WIDTH:].astype(F32)
        dh_ref[:, 1024:1536] = (dbo * mixed * dgu).astype(BF16)
        dmx = dbo * u
        mask = _tril_mask()
        wst = [jnp.where(mask.T, mst_ref[g], 0.0).astype(BF16) for g in range(B_GROUPS)]
        qb = q.astype(BF16)
        dmb = dmx.astype(BF16)
        rows = []
        for c in range(nchunk):
            cols = []
            for g in range(B_GROUPS):
                rs_, cs_ = slice(c * B_CHUNK, (c + 1) * B_CHUNK), slice(g * 128, (g + 1) * 128)
                sbacc[g] += dmx[rs_, cs_]
                dms_ref[g] += _dot(dmb[rs_, cs_], qb[rs_, cs_], NT)
                cols.append(_dot(wst[g], dmb[rs_, cs_], NN))
            rows.append(jnp.concatenate(cols, axis=1))
        dq = jnp.concatenate(rows, axis=0)
        dbg_ref[...] += _colsum(dq * nb)
        dbb_ref[...] += _colsum(dq)
        dh_ref[:, 1536:2048] = (_ln_bwd_rows(dq * bg_ref[...], nb, rb) * dgw).astype(BF16)

        @pl.when(i == nt - 1)
        def _():
            for g in range(B_GROUPS):
                dms_ref[g] = jnp.where(mask, dms_ref[g], 0.0)
                dsb_ref[g] = jnp.sum(sbacc[g], axis=1, keepdims=True)

    last_blk = t // MIX_HALO - 1
    vec = pl.BlockSpec((1, A_WIDTH), lambda i: (0, 0))
    mat = pl.BlockSpec((B_GROUPS, B_CHUNK, B_CHUNK), lambda i: (0, 0, 0))
    taps = pl.BlockSpec((A_KERNEL, A_WIDTH), lambda i: (0, 0))

    def halo(width, which):
        if which == "prev":
            return pl.BlockSpec((MIX_HALO, width), lambda i: (jnp.maximum(i * hb - 1, 0), 0))
        return pl.BlockSpec((MIX_HALO, width), lambda i: (jnp.minimum((i + 1) * hb, last_blk), 0))

    vshape = jax.ShapeDtypeStruct((1, A_WIDTH), F32)
    outs, moved = _call(
        body, name=name, grid=(nt,),
        in_specs=[pl.BlockSpec((tm, 2048), lambda i: (i, 0)), halo(2048, "prev"), halo(2048, "next"),
                  pl.BlockSpec((tm, D_MODEL), lambda i: (i, 0)), halo(D_MODEL, "next"),
                  taps, vec, vec, vec, vec, vec, mat, mat, pl.BlockSpec((B_CHUNK, B_GROUPS), lambda i: (0, 0))],
        out_specs=[pl.BlockSpec((tm, 2048), lambda i: (i, 0)), taps, vec, vec, vec, vec, vec, mat,
                   pl.BlockSpec((B_GROUPS, B_CHUNK, 1), lambda i: (0, 0, 0))],
        out_shape=[jax.ShapeDtypeStruct((t, 2048), BF16), jax.ShapeDtypeStruct((A_KERNEL, A_WIDTH), F32),
                   vshape, vshape, vshape, vshape, vshape,
                   jax.ShapeDtypeStruct((B_GROUPS, B_CHUNK, B_CHUNK), F32), jax.ShapeDtypeStruct((B_GROUPS, B_CHUNK, 1), F32)],
        scratch_shapes=[pltpu.VMEM((SUBLANES, tm + 2 * MIX_HALO, A_WIDTH), F32), pltpu.VMEM((SUBLANES, r, A_WIDTH), F32),
                        pltpu.VMEM((B_GROUPS, B_CHUNK, B_CHUNK), F32)],
        sem=("arbitrary",), args=(h, h, h, dcat, dcat, cw, cb, ag, ab, bg, bb, ms, mst, sbt), comm=comm)
    return outs if comm is None else (outs, moved)


Q_WIDTH = N_Q_HEADS * HEAD_DIM
KV_WIDTH = 2 * N_KV_HEADS * HEAD_DIM
PAIRS_PER_KV = N_Q_HEADS // N_KV_HEADS // 2
ATT_SCALE = 1.0 / math.sqrt(HEAD_DIM)


def _dup_heads(pair_cols, kv_head):
    lane = lax.broadcasted_iota(jnp.int32, pair_cols.shape, 1)
    rolled = pltpu.roll(pair_cols, HEAD_DIM, 1)
    first = lane < HEAD_DIM
    return jnp.where(first, pair_cols, rolled) if kv_head == 0 else jnp.where(first, rolled, pair_cols)


HEADS_PER_KV = N_Q_HEADS // N_KV_HEADS


def _stack_heads(ref, kh):
    lane = lax.broadcasted_iota(jnp.int32, (ATT_BLOCK, 128), 1)
    rows = []
    for pr in range(PAIRS_PER_KV):
        c0 = (kh * PAIRS_PER_KV + pr) * 128
        pair = ref[:, c0:c0 + 128]
        rows += [jnp.where(lane < HEAD_DIM, pair, jnp.zeros_like(pair)), jnp.where(lane < HEAD_DIM, jnp.zeros_like(pair), pair)]
    return jnp.concatenate(rows, axis=0)


def _unstack_heads(stacked, kh, write):
    lane = lax.broadcasted_iota(jnp.int32, (ATT_BLOCK, 128), 1)
    for pr in range(PAIRS_PER_KV):
        first = stacked[(2 * pr) * ATT_BLOCK:(2 * pr + 1) * ATT_BLOCK]
        second = stacked[(2 * pr + 1) * ATT_BLOCK:(2 * pr + 2) * ATT_BLOCK]
        write((kh * PAIRS_PER_KV + pr) * 128, jnp.where(lane < HEAD_DIM, first, second))


def _sink_row(sink_ref, kh):
    return jnp.concatenate([jnp.full((1, ATT_BLOCK), sink_ref[0, kh * HEADS_PER_KV + h], F32) for h in range(HEADS_PER_KV)], axis=1)


def _att_mask_t(n):
    sj = lax.broadcasted_iota(jnp.int32, (2 * ATT_BLOCK, HEADS_PER_KV * ATT_BLOCK), 0)
    qi = lax.broadcasted_iota(jnp.int32, (2 * ATT_BLOCK, HEADS_PER_KV * ATT_BLOCK), 1) & (ATT_BLOCK - 1)
    diff = qi + ATT_BLOCK - sj
    return (diff >= 0) & (diff < ATT_BLOCK) & ((n > 0) | (sj >= ATT_BLOCK))


def _att_probs_t(q_all, k2, mask_t, sink):
    st = _dot(k2, q_all, NT) * ATT_SCALE
    st = jnp.where(mask_t, st, -jnp.inf)
    m = jnp.maximum(jnp.max(st, axis=0, keepdims=True), sink)
    e = jnp.exp(st - m)
    es = jnp.exp(sink - m)
    inv = 1.0 / (jnp.sum(e, axis=0, keepdims=True) + es)
    return e * inv, es * inv


def _attn_fwd(qkv, sinks, name, comm=None):
    t = qkv.shape[0]
    nb = t // ATT_BLOCK
    kvb = Q_WIDTH // KV_WIDTH

    def body(sink_ref, q_ref, kv_ref, kvp_ref, o_ref):
        n = pl.program_id(0)
        mask_t = _att_mask_t(n)
        kv = jnp.concatenate([kvp_ref[...], kv_ref[...]], axis=0).astype(F32)

        def write(c0, pair):
            o_ref[:, c0:c0 + 128] = pair.astype(BF16)

        for kh in range(N_KV_HEADS):
            k2 = _dup_heads(kv[:, 0:128], kh).astype(BF16)
            v2 = _dup_heads(kv[:, 128:256], kh).astype(BF16)
            pt, _ = _att_probs_t(_stack_heads(q_ref, kh), k2, mask_t, _sink_row(sink_ref, kh))
            _unstack_heads(_dot(v2, pt, TN).T, kh, write)

    (out,), moved = _call(
        body, name=name, grid=(nb,),
        in_specs=[pl.BlockSpec(memory_space=pltpu.SMEM),
                  pl.BlockSpec((ATT_BLOCK, Q_WIDTH), lambda n: (n, 0)),
                  pl.BlockSpec((ATT_BLOCK, KV_WIDTH), lambda n: (n, kvb)),
                  pl.BlockSpec((ATT_BLOCK, KV_WIDTH), lambda n: (jnp.maximum(n - 1, 0), kvb))],
        out_specs=[pl.BlockSpec((ATT_BLOCK, Q_WIDTH), lambda n: (n, 0))],
        out_shape=[jax.ShapeDtypeStruct((t, Q_WIDTH), BF16)],
        sem=("parallel",), args=(sinks, qkv, qkv, qkv), comm=comm)
    return out if comm is None else (out, moved)


def _attn_bwd(qkv, d_o, sinks, name, comm=None):
    t = qkv.shape[0]
    nb = t // ATT_BLOCK
    kvb = Q_WIDTH // KV_WIDTH

    def body(sink_ref, q_ref, kv_ref, kvp_ref, do_ref, dq_ref, dkv_ref, dbq_ref, dbkv_ref, dsink_ref, carry):
        n = pl.program_id(0)

        @pl.when(n == 0)
        def _():
            for ref in (dbq_ref, dbkv_ref, dsink_ref, carry):
                ref[...] = jnp.zeros_like(ref)
            dkv_ref[...] = jnp.zeros_like(dkv_ref)

        @pl.when(n < nb)
        def _():
            mask_t = _att_mask_t(n)
            kv = jnp.concatenate([kvp_ref[...], kv_ref[...]], axis=0).astype(F32)
            lane2 = lax.broadcasted_iota(jnp.int32, (2 * ATT_BLOCK, 128), 1)
            sink_lane = lax.broadcasted_iota(jnp.int32, (1, 128), 1)
            dsink = jnp.zeros((1, 128), F32)
            dk_parts, dv_parts = [], []

            def write(c0, pair):
                dbq_ref[:, c0:c0 + 128] += _colsum(pair)
                dq_ref[:, c0:c0 + 128] = pair.astype(BF16)

            for kh in range(N_KV_HEADS):
                k2 = _dup_heads(kv[:, 0:128], kh).astype(BF16)
                v2 = _dup_heads(kv[:, 128:256], kh).astype(BF16)
                q_all = _stack_heads(q_ref, kh)
                do_all = _stack_heads(do_ref, kh)
                pt, ps = _att_probs_t(q_all, k2, mask_t, _sink_row(sink_ref, kh))
                dpt = _dot(v2, do_all, NT)
                delta = jnp.sum(pt * dpt, axis=0, keepdims=True)
                dst = pt * (dpt - delta) * ATT_SCALE
                psd = ps * delta
                for h in range(HEADS_PER_KV):
                    dsink = dsink + jnp.where(sink_lane == kh * HEADS_PER_KV + h,
                                              -jnp.sum(psd[:, h * ATT_BLOCK:(h + 1) * ATT_BLOCK]), 0.0)
                _unstack_heads(_dot(k2, dst, TN).T, kh, write)
                dk_acc = _dot(dst, q_all, NN)
                dv_acc = _dot(pt, do_all, NN)
                dk_parts.append(dk_acc + pltpu.roll(dk_acc, HEAD_DIM, 1))
                dv_parts.append(dv_acc + pltpu.roll(dv_acc, HEAD_DIM, 1))
            dk = jnp.where(lane2 < HEAD_DIM, dk_parts[0], dk_parts[1])
            dv = jnp.where(lane2 < HEAD_DIM, dv_parts[0], dv_parts[1])
            dkv_new = jnp.concatenate([dk, dv], axis=1)
            done = carry[...] + dkv_new[0:ATT_BLOCK]

            @pl.when(n > 0)
            def _():
                dkv_ref[...] = done.astype(BF16)
                dbkv_ref[...] += _colsum(done)

            carry[...] = dkv_new[ATT_BLOCK:]
            dsink_ref[...] += dsink

        @pl.when(n == nb)
        def _():
            dkv_ref[...] = carry[...].astype(BF16)
            dbkv_ref[...] += _colsum(carry[...])

    def clamp(n):
        return jnp.minimum(n, nb - 1)

    outs, moved = _call(
        body, name=name, grid=(nb + 1,),
        in_specs=[pl.BlockSpec(memory_space=pltpu.SMEM),
                  pl.BlockSpec((ATT_BLOCK, Q_WIDTH), lambda n: (clamp(n), 0)),
                  pl.BlockSpec((ATT_BLOCK, KV_WIDTH), lambda n: (clamp(n), kvb)),
                  pl.BlockSpec((ATT_BLOCK, KV_WIDTH), lambda n: (jnp.maximum(clamp(n) - 1, 0), kvb)),
                  pl.BlockSpec((ATT_BLOCK, Q_WIDTH), lambda n: (clamp(n), 0))],
        out_specs=[pl.BlockSpec((ATT_BLOCK, Q_WIDTH), lambda n: (clamp(n), 0)),
                   pl.BlockSpec((ATT_BLOCK, KV_WIDTH), lambda n: (jnp.maximum(n - 1, 0), 0)),
                   pl.BlockSpec((1, Q_WIDTH), lambda n: (0, 0)),
                   pl.BlockSpec((1, KV_WIDTH), lambda n: (0, 0)),
                   pl.BlockSpec((1, 128), lambda n: (0, 0))],
        out_shape=[jax.ShapeDtypeStruct((t, Q_WIDTH), BF16), jax.ShapeDtypeStruct((t, KV_WIDTH), BF16),
                   jax.ShapeDtypeStruct((1, Q_WIDTH), F32), jax.ShapeDtypeStruct((1, KV_WIDTH), F32),
                   jax.ShapeDtypeStruct((1, 128), F32)],
        scratch_shapes=[pltpu.VMEM((ATT_BLOCK, KV_WIDTH), F32)],
        sem=("arbitrary",), args=(sinks, qkv, qkv, qkv, d_o), comm=comm)
    return outs if comm is None else (outs, moved)


def _adamw_math(g, w, m, v):
    m = ADAM_B1 * m + (1.0 - ADAM_B1) * g
    v = ADAM_B2 * v + (1.0 - ADAM_B2) * (g * g)
    m_hat = m / (1.0 - ADAM_B1 ** ADAM_STEP)
    v_hat = v / (1.0 - ADAM_B2 ** ADAM_STEP)
    delta = -ADAM_LR * (m_hat / (jnp.sqrt(v_hat) + ADAM_EPS) + ADAM_WD * w)
    return delta, m, v


def _sum_partials(p_ref):
    g = p_ref[0].astype(F32)
    for s in range(1, N_DEV):
        g = g + p_ref[s].astype(F32)
    return g


def _adamw_big(parts, w, m, v, name, tr):
    r, c = w.shape
    tiles = [p.shape[1] // tr for p in parts]
    starts = [sum(tiles[:l]) for l in range(len(parts))]
    assert all(p.shape[1] % tr == 0 for p in parts) and sum(tiles) * tr == r

    def body(*refs):
        p_refs, (w_ref, m_ref, v_ref, g_out, d_out, m_out, v_out) = refs[:len(parts)], refs[len(parts):]
        i = pl.program_id(0)
        for l, p_ref in enumerate(p_refs):
            @pl.when((i >= starts[l]) & (i < starts[l] + tiles[l]))
            def _():
                g = _sum_partials(p_ref)
                g_out[...] = g
                d_out[...], m_out[...], v_out[...] = _adamw_math(g, w_ref[...], m_ref[...], v_ref[...])

    def part_spec(l):
        return pl.BlockSpec((N_DEV, tr, c), lambda i: (0, jnp.clip(i - starts[l], 0, tiles[l] - 1), 0))

    tile = pl.BlockSpec((tr, c), lambda i: (i, 0))
    shape = jax.ShapeDtypeStruct((r, c), F32)
    return pl.pallas_call(
        body, name=name, grid=(r // tr,),
        in_specs=[part_spec(l) for l in range(len(parts))] + [tile, tile, tile],
        out_specs=[tile] * 4, out_shape=[shape] * 4,
        compiler_params=_params("parallel"),
    )(*parts, w, m, v)


def _adamw_small(parts, ws, ms, vs, name):
    n = len(ws)

    def body(*refs):
        ins, outs = refs[:4 * n], refs[4 * n:]
        for a in range(n):
            g = _sum_partials(ins[a])
            outs[4 * a][...] = g
            outs[4 * a + 1][...], outs[4 * a + 2][...], outs[4 * a + 3][...] = _adamw_math(
                g, ins[n + a][...], ins[2 * n + a][...], ins[3 * n + a][...])

    out_shape = []
    for w in ws:
        out_shape += [jax.ShapeDtypeStruct(w.shape, F32)] * 4
    return pl.pallas_call(body, name=name, out_shape=out_shape, compiler_params=_params())(*parts, *ws, *ms, *vs)


PACK_LANES = 128
PACK_ROWS = 8


def _pack(arrs):
    flat = jnp.concatenate([a.reshape(-1).astype(F32) for a in arrs])
    unit = PACK_LANES * PACK_ROWS
    total = -(-flat.shape[0] // unit) * unit
    return jnp.pad(flat, (0, total - flat.shape[0])).reshape(-1, PACK_LANES)


def _unpack(buf, shapes):
    flat = buf.reshape(N_DEV, -1)
    out, pos = [], 0
    for s in shapes:
        size = math.prod(s)
        out.append(flat[:, pos:pos + size].reshape((N_DEV,) + tuple(s)))
        pos += size
    return out


def _interleave(g):
    return jnp.transpose(g, (1, 0, 2)).reshape(g.shape[1], -1)


def _deinterleave(w):
    r = w.shape[0]
    return jnp.transpose(w.reshape(r, N_DEV, -1), (1, 0, 2))


def _ffn_backward(dz, x_in, z_in, g_in, h, u, w_up_t, cw, cb, w_down, tag, exchange=()):
    du = _matmul(dz, w_down, "nt", BF16, f"ffn{tag}_du", 1024, 1408, 1024)
    d_w_down = _matmul(u, dz, "tn", BF16, f"ffn{tag}_dwdown", 1408, 1024, 2048)
    (dhg, dhv, dcwg, dcwv, dcbg, dcbv), moved = _ffn_mid_bwd(
        h, du, cw, cb, f"ffn{tag}_mid_bwd", comm=_Comm(exchange=[d_w_down.reshape(N_DEV, -1, D_MODEL), *exchange]))
    d_w_up_t = _matmul_tn_pair(dhg, dhv, x_in, BF16, f"ffn{tag}_dwup", 1408, 1024, 1024)
    dx = _matmul(dhv, w_up_t, "nn", F32, f"ffn{tag}_dx_value", 1024, 1024, D_FF, b_off=1)
    dz_in, dg_in, db_in = _matmul_ln_bwd(dhg, w_up_t, z_in, g_in, dz, f"ffn{tag}_dx_gate_ln_bwd", 512, res=dx)
    return (dz_in, dg_in, db_in, d_w_up_t.reshape(N_DEV, -1, D_MODEL),
            jnp.concatenate([dcwg, dcwv], axis=1), jnp.concatenate([dcbg, dcbv], axis=1), moved)


def kernel(x, ab_w_in, a_conv_w, a_conv_b, a_norm_g, a_norm_b, b_norm_g, b_norm_b, b_spatial_w, b_spatial_b, ab_w_out, c_w_qkv, c_b_qkv, c_sinks, c_w_o, ffn_w_up, ffn_conv_w, ffn_conv_b, ffn_w_down, ln_g, ln_b, loss_target, m_ab_w_in, m_a_conv_w, m_a_conv_b, m_a_norm_g, m_a_norm_b, m_b_norm_g, m_b_norm_b, m_b_spatial_w, m_b_spatial_b, m_ab_w_out, m_c_w_qkv, m_c_b_qkv, m_c_sinks, m_c_w_o, m_ffn_w_up, m_ffn_conv_w, m_ffn_conv_b, m_ffn_w_down, m_ln_g, m_ln_b, v_ab_w_in, v_a_conv_w, v_a_conv_b, v_a_norm_g, v_a_norm_b, v_b_norm_g, v_b_norm_b, v_b_spatial_w, v_b_spatial_b, v_ab_w_out, v_c_w_qkv, v_c_b_qkv, v_c_sinks, v_c_w_o, v_ffn_w_up, v_ffn_conv_w, v_ffn_conv_b, v_ffn_w_down, v_ln_g, v_ln_b):
    me = 4 * lax.axis_index("x") + 2 * lax.axis_index("y") + lax.axis_index("c")
    xt = x[0]
    t = xt.shape[0]

    small_shard_shapes = [a_conv_w.shape, c_b_qkv.shape, ffn_conv_w.shape, ln_g.shape, ln_b.shape]
    up_shard = [jnp.swapaxes(ffn_w_up[l], 0, 1).astype(BF16) for l in range(2)]
    qkv_shard = jnp.swapaxes(c_w_qkv[0], 0, 1).astype(BF16)
    down_shard = [ffn_w_down[l].astype(BF16) for l in range(2)]
    g_win, g_small = _comm_only(
        _Comm(gather=[ab_w_in[0].astype(BF16), _pack([a_conv_w, c_b_qkv, ffn_conv_w, ln_g, ln_b])]), "gather_first")
    w_in = _interleave(g_win)
    g_acw, g_bqkv, g_fcw, g_lng, g_lnb = _unpack(g_small, small_shard_shapes)
    acw = _interleave(g_acw[:, 0])
    bqkv = g_bqkv[:, 0].reshape(1, -1)
    fcw = [_interleave(g_fcw[:, l]) for l in range(2)]
    lng = jnp.transpose(g_lng, (1, 2, 0, 3)).reshape(2, 2, 1, D_MODEL)
    lnb = jnp.transpose(g_lnb, (1, 2, 0, 3)).reshape(2, 2, 1, D_MODEL)
    fcb = [ffn_conv_b[l:l + 1] for l in range(2)]
    ms = b_spatial_w[0]
    mst = jnp.swapaxes(ms, 1, 2)
    sbt = b_spatial_b[0].T

    h0, (g_wout,) = _matmul(xt, w_in, "nn", BF16, "mix_in", 1024, 1024, 1024, comm=_Comm(gather=[ab_w_out[0].astype(BF16)]))
    w_out = g_wout.reshape(D_MODEL, D_MODEL)
    cat, (g_wup0,) = _mixer_mid_fwd(h0, acw, a_conv_b, a_norm_g, a_norm_b, b_norm_g, b_norm_b, ms, sbt, "mix_mid_fwd",
                                    comm=_Comm(gather=[up_shard[0]]))
    w_up0 = g_wup0.reshape(2 * D_FF, D_MODEL)
    z1, x1 = _matmul_res_ln(cat, w_out, xt, lng[0, 0], lnb[0, 0], "mix_out_ln", 512, D_MODEL)
    hf0, (g_wdown0, g_wqkv) = _matmul(x1, w_up0, "nt", BF16, "ffn0_up", 1024, 1408, 1024,
                                      comm=_Comm(gather=[down_shard[0], qkv_shard]))
    w_down0 = g_wdown0.reshape(D_FF, D_MODEL)
    w_qkv = g_wqkv.reshape(Q_WIDTH + KV_WIDTH, D_MODEL)
    u0, (g_wup1,) = _ffn_mid_fwd(hf0, fcw[0], fcb[0], "ffn0_mid_fwd", comm=_Comm(gather=[up_shard[1]]))
    w_up1 = g_wup1.reshape(2 * D_FF, D_MODEL)
    (z2, x2), (g_wo,) = _matmul_res_ln(u0, w_down0, x1, lng[0, 1], lnb[0, 1], "ffn0_down_ln", 512, D_FF,
                                       comm=_Comm(gather=[c_w_o[0].astype(BF16)]))
    w_o = g_wo.reshape(D_MODEL, D_MODEL)
    qkv = _matmul(x2, w_qkv, "nt", BF16, "att_qkv", 1024, 1280, 1024, bias=bqkv)
    att, (g_wdown1,) = _attn_fwd(qkv, c_sinks, "att_fwd", comm=_Comm(gather=[down_shard[1]]))
    w_down1 = g_wdown1.reshape(D_FF, D_MODEL)
    z3, x3 = _matmul_res_ln(att, w_o, x2, lng[1, 0], lnb[1, 0], "att_out_ln", 512, D_MODEL)
    hf1 = _matmul(x3, w_up1, "nt", BF16, "ffn1_up", 1024, 1408, 1024)
    u1 = _ffn_mid_fwd(hf1, fcw[1], fcb[1], "ffn1_mid_fwd")
    z4, _ = _matmul_res_ln(u1, w_down1, x3, lng[1, 1], lnb[1, 1], "ffn1_down_ln", 512, D_FF)

    dz4, dg11, db11, loss_terms = _ln_bwd_loss(z4, lng[1, 1], lnb[1, 1], loss_target[0], "loss_ln_bwd")
    dz3, dg10, db10, d_wup1, d_fcw1, d_fcb1, (p_wdown1,) = _ffn_backward(
        dz4, x3, z3, lng[1, 0], hf1, u1, w_up1, fcw[1], fcb[1], w_down1, 1)
    d_att = _matmul(dz3, w_o, "nt", BF16, "att_dout", 1024, 1024, 1024)
    d_wo = _matmul(att, dz3, "tn", BF16, "att_dwo", 1024, 1024, 512)
    (dq, dkv, dbq, dbkv, dsinks), (p_wup1,) = _attn_bwd(qkv, d_att, c_sinks, "att_bwd", comm=_Comm(exchange=[d_wup1]))
    d_wqkv = jnp.concatenate([_matmul(dq, x2, "tn", BF16, "att_dwq", 1024, 1024, 1024),
                              _matmul(dkv, x2, "tn", BF16, "att_dwkv", KV_WIDTH, 1024, 1024)], axis=0)
    dx2 = _matmul(dkv, w_qkv, "nn", F32, "att_dx_kv", 1024, 1024, KV_WIDTH, b_off=Q_WIDTH // KV_WIDTH)
    dz2, dg01, db01 = _matmul_ln_bwd(dq, w_qkv, z2, lng[0, 1], dz3, "att_dx_q_ln_bwd", 512, res=dx2)
    dz1, dg00, db00, d_wup0, d_fcw0, d_fcb0, (p_wdown0, p_wqkv, p_wo) = _ffn_backward(
        dz2, x1, z1, lng[0, 0], hf0, u0, w_up0, fcw[0], fcb[0], w_down0, 0,
        exchange=[d_wqkv.reshape(N_DEV, -1, D_MODEL), d_wo.reshape(N_DEV, -1, D_MODEL)])
    dcat = _matmul(dz1, w_out, "nt", BF16, "mix_dcat", 1024, 1024, 1024)
    d_wout = _matmul(cat, dz1, "tn", BF16, "mix_dwout", 1024, 1024, 512)
    (dh0, d_acw, d_acb, d_ang, d_anb, d_bng, d_bnb, d_ms, d_sb), (p_wup0, p_wout) = _mixer_mid_bwd(
        h0, dcat, acw, a_conv_b, a_norm_g, a_norm_b, b_norm_g, b_norm_b, ms, mst, sbt, "mix_mid_bwd",
        comm=_Comm(exchange=[d_wup0, d_wout.reshape(N_DEV, -1, D_MODEL)]))
    d_bqkv = jnp.concatenate([dbq, dbkv], axis=1)
    d_lng = jnp.stack([jnp.stack([dg00, dg01]), jnp.stack([dg10, dg11])])
    d_lnb = jnp.stack([jnp.stack([db00, db01]), jnp.stack([db10, db11])])
    small_full = [d_acb, d_ang, d_anb, d_bng, d_bnb, d_ms, d_sb, dsinks[:, :N_Q_HEADS], jnp.concatenate([d_fcb0, d_fcb1], axis=0),
                  d_acw, d_bqkv, jnp.stack([d_fcw0, d_fcw1]), d_lng, d_lnb]
    d_win, (g_small_grads,) = _matmul(xt, dh0, "tn", BF16, "mix_dwin", 1024, 1024, 512, comm=_Comm(gather=[_pack(small_full)]))
    grad_x, (p_win,) = _matmul(dh0, w_in, "nt", F32, "mix_dx", 1024, 1024, 1024, res=dz1, res_scale=ALPHA,
                               comm=_Comm(exchange=[_deinterleave(d_win)]))

    loss = lax.psum(0.5 / D_MODEL * jnp.sum(loss_terms), ("x", "y", "c"))

    big = {}
    for nm, p, w, m, v, tr, transposed in [
            ("ab_w_in", [p_win], ab_w_in, m_ab_w_in, v_ab_w_in, 256, False),
            ("ab_w_out", [p_wout], ab_w_out, m_ab_w_out, v_ab_w_out, 128, False),
            ("c_w_qkv", [p_wqkv], c_w_qkv, m_c_w_qkv, v_c_w_qkv, 160, True), ("c_w_o", [p_wo], c_w_o, m_c_w_o, v_c_w_o, 128, False),
            ("ffn_w_up", [p_wup0, p_wup1], ffn_w_up, m_ffn_w_up, v_ffn_w_up, 176, True),
            ("ffn_w_down", [p_wdown0, p_wdown1], ffn_w_down, m_ffn_w_down, v_ffn_w_down, 176, False)]:
        def two_d(a):
            a = jnp.swapaxes(a, 1, 2) if transposed else a
            return a.reshape(-1, a.shape[-1])

        def back(o):
            return jnp.swapaxes(o.reshape(w.shape[0], w.shape[2], w.shape[1]), 1, 2) if transposed else o.reshape(w.shape)

        outs = _adamw_big(p, two_d(w), two_d(m), two_d(v), "adamw_" + nm, tr)
        big[nm] = [back(o) for o in outs]

    gs = _unpack(g_small_grads, [a.shape for a in small_full])

    def my_shard(g, width):
        g = g.reshape(g.shape[:-1] + (N_DEV, width))
        return lax.dynamic_index_in_dim(g, me, axis=g.ndim - 2, keepdims=False)

    small_names = ["a_conv_b", "a_norm_g", "a_norm_b", "b_norm_g", "b_norm_b", "b_spatial_w", "b_spatial_b", "c_sinks", "ffn_conv_b",
                   "a_conv_w", "c_b_qkv", "ffn_conv_w", "ln_g", "ln_b"]
    small_w = [a_conv_b, a_norm_g, a_norm_b, b_norm_g, b_norm_b, b_spatial_w, b_spatial_b, c_sinks, ffn_conv_b,
               a_conv_w, c_b_qkv, ffn_conv_w, ln_g, ln_b]
    small_m = [m_a_conv_b, m_a_norm_g, m_a_norm_b, m_b_norm_g, m_b_norm_b, m_b_spatial_w, m_b_spatial_b, m_c_sinks, m_ffn_conv_b,
               m_a_conv_w, m_c_b_qkv, m_ffn_conv_w, m_ln_g, m_ln_b]
    small_v = [v_a_conv_b, v_a_norm_g, v_a_norm_b, v_b_norm_g, v_b_norm_b, v_b_spatial_w, v_b_spatial_b, v_c_sinks, v_ffn_conv_b,
               v_a_conv_w, v_c_b_qkv, v_ffn_conv_w, v_ln_g, v_ln_b]
    gs[9:] = [my_shard(g, w.shape[-1]) for g, w in zip(gs[9:], small_w[9:])]
    two_d = [(-1, w.shape[-1]) for w in small_w]
    outs = _adamw_small([g.reshape((N_DEV,) + w.reshape(s).shape) for g, w, s in zip(gs, small_w, two_d)],
                        [w.reshape(s) for w, s in zip(small_w, two_d)], [m.reshape(s) for m, s in zip(small_m, two_d)],
                        [v.reshape(s) for v, s in zip(small_v, two_d)], "adamw_small")
    small = {nm: [o.reshape(w.shape) for o in outs[4 * a:4 * a + 4]] for a, (nm, w) in enumerate(zip(small_names, small_w))}

    res = {**big, **small}
    order = ["ab_w_in", "a_conv_w", "a_conv_b", "a_norm_g", "a_norm_b", "b_norm_g", "b_norm_b", "b_spatial_w", "b_spatial_b", "ab_w_out",
             "c_w_qkv", "c_b_qkv", "c_sinks", "c_w_o", "ffn_w_up", "ffn_conv_w", "ffn_conv_b", "ffn_w_down", "ln_g", "ln_b"]
    return (loss, grad_x[None], *[res[nm][0] for nm in order], *[res[nm][1] for nm in order],
            *[res[nm][2] for nm in order], *[res[nm][3] for nm in order])
```

```python
import functools
import math

import jax
import jax.numpy as jnp
from jax import lax
from jax.experimental import pallas as pl
from jax.experimental.pallas import tpu as pltpu

F32 = jnp.float32
BF16 = jnp.bfloat16

N_DEV = 8
D_MODEL = 1024
A_WIDTH = 512
A_KERNEL = 31
B_GROUPS = 4
B_CHUNK = 128
HEAD_DIM = 64
N_Q_HEADS = 16
N_KV_HEADS = 2
ATT_BLOCK = 128
D_FF = 2816
FFN_KERNEL = 3
ALPHA = (2.0 * 2) ** 0.25
LN_EPS = 1e-5
GELU_K = math.sqrt(2.0 / math.pi)
GELU_C = 0.044715
ADAM_LR = 0.001
ADAM_B1 = 0.9
ADAM_B2 = 0.999
ADAM_EPS = 1e-08
ADAM_WD = 0.01
ADAM_STEP = 10
VMEM_LIMIT = 56 * 1024 * 1024
MESH_ID = pl.DeviceIdType.MESH


def _params(*sem):
    return pltpu.CompilerParams(dimension_semantics=sem, vmem_limit_bytes=VMEM_LIMIT)


def _gelu(x):
    t = jnp.tanh(GELU_K * x * (1.0 + GELU_C * x * x))
    return 0.5 * x * (1.0 + t)


def _gelu_and_grad(x):
    x2 = x * x
    t = jnp.tanh(GELU_K * x * (1.0 + GELU_C * x2))
    g = 0.5 * x * (1.0 + t)
    dg = 0.5 * (1.0 + t) + 0.5 * x * (1.0 - t * t) * (GELU_K * (1.0 + 3.0 * GELU_C * x2))
    return g, dg


def _sigmoid(x):
    return 1.0 / (1.0 + jnp.exp(-x))


def _ln_stats(z):
    mu = jnp.mean(z, axis=-1, keepdims=True)
    zc = z - mu
    var = jnp.mean(zc * zc, axis=-1, keepdims=True)
    r = lax.rsqrt(var + LN_EPS)
    return zc * r, r


def _ln_bwd_rows(dn, nh, r):
    return r * (dn - jnp.mean(dn, axis=-1, keepdims=True) - nh * jnp.mean(dn * nh, axis=-1, keepdims=True))


def _colsum(x):
    return jnp.sum(x, axis=0, keepdims=True)


def _dot(a, b, dims):
    return lax.dot_general(a.astype(BF16), b.astype(BF16), (dims, ((), ())), preferred_element_type=F32)


NN = ((1,), (0,))
NT = ((1,), (1,))
TN = ((0,), (0,))


ANY = pl.BlockSpec(memory_space=pl.ANY)
N_RELATIONS = N_DEV - 1


def _my_place():
    return lax.axis_index("x"), lax.axis_index("y"), lax.axis_index("c")


class _Comm:
    def __init__(self, gather=(), exchange=()):
        self.arrs = list(gather) + list(exchange)
        self.n_gather = len(gather)
        self.n = len(self.arrs)

    def out_shape(self):
        return [jax.ShapeDtypeStruct(((N_DEV,) + a.shape) if i < self.n_gather else a.shape, a.dtype)
                for i, a in enumerate(self.arrs)]

    def sems(self):
        return [pltpu.SemaphoreType.DMA((self.n, N_RELATIONS)), pltpu.SemaphoreType.DMA((self.n, N_RELATIONS)),
                pltpu.SemaphoreType.DMA((self.n,))]

    def _gather_copy(self, ins, outs, sems, a, k, place, to, from_input=False):
        px, py, pc = place
        block = outs[a].at[4 * px + 2 * py + pc]
        return pltpu.make_async_remote_copy(
            src_ref=ins[a] if from_input else block, dst_ref=block,
            send_sem=sems[0].at[a, k], recv_sem=sems[1].at[a, k], device_id=to, device_id_type=MESH_ID)

    def _exchange_copy(self, ins, outs, sems, a, k, landing=False):
        x, y, c = _my_place()
        me = 4 * x + 2 * y + c
        peer = (x ^ (k >> 2), y ^ ((k >> 1) & 1), c ^ (k & 1))
        return pltpu.make_async_remote_copy(
            src_ref=ins[a].at[me ^ k], dst_ref=outs[a].at[(me ^ k) if landing else me],
            send_sem=sems[0].at[a, k - 1], recv_sem=sems[1].at[a, k - 1], device_id=peer, device_id_type=MESH_ID)

    def _local_copy(self, ins, outs, sems, a):
        x, y, c = _my_place()
        me = 4 * x + 2 * y + c
        src = ins[a] if a < self.n_gather else ins[a].at[me]
        return pltpu.make_async_copy(src, outs[a].at[me], sems[2].at[a])

    def _first_stage(self, ins, outs, sems, a):
        x, y, c = _my_place()
        me = (x, y, c)
        chips = [(1 - x, y), (x, 1 - y), (1 - x, 1 - y)]
        return ([self._gather_copy(ins, outs, sems, a, 0, me, (x, y, 1 - c), from_input=True)]
                + [self._gather_copy(ins, outs, sems, a, 1 + j, me, (*chip, c), from_input=True) for j, chip in enumerate(chips)])

    def start(self, ins, outs, sems):
        for a in range(self.n):
            self._local_copy(ins, outs, sems, a).start()
        for a in range(self.n_gather):
            for cp in self._first_stage(ins, outs, sems, a):
                cp.start()
        for k in range(1, N_DEV):
            for a in range(self.n_gather, self.n):
                self._exchange_copy(ins, outs, sems, a, k).start()

    def finish(self, ins, outs, sems):
        x, y, c = _my_place()
        me, sibling = (x, y, c), (x, y, 1 - c)
        chips = [(1 - x, y), (x, 1 - y), (1 - x, 1 - y)]
        passed = []
        for j, chip in enumerate(chips):
            for a in range(self.n_gather):
                self._gather_copy(ins, outs, sems, a, 1 + j, (*chip, c), me).wait_recv()
                fwd = self._gather_copy(ins, outs, sems, a, 4 + j, (*chip, c), sibling)
                fwd.start()
                passed.append(fwd)
        for a in range(self.n_gather):
            self._gather_copy(ins, outs, sems, a, 0, sibling, me).wait_recv()
            for j, chip in enumerate(chips):
                self._gather_copy(ins, outs, sems, a, 4 + j, (*chip, 1 - c), me).wait_recv()
        for k in range(1, N_DEV):
            for a in range(self.n_gather, self.n):
                self._exchange_copy(ins, outs, sems, a, k, landing=True).wait_recv()
        for a in range(self.n_gather):
            for cp in self._first_stage(ins, outs, sems, a):
                cp.wait_send()
        for cp in passed:
            cp.wait_send()
        for k in range(1, N_DEV):
            for a in range(self.n_gather, self.n):
                self._exchange_copy(ins, outs, sems, a, k).wait_send()
        for a in range(self.n):
            self._local_copy(ins, outs, sems, a).wait()


def _comm_only(comm, name):
    def body(*refs):
        ins, outs, sems = refs[:comm.n], refs[comm.n:2 * comm.n], refs[2 * comm.n:]
        comm.start(ins, outs, sems)
        comm.finish(ins, outs, sems)

    return pl.pallas_call(body, name=name, in_specs=[ANY] * comm.n, out_specs=[ANY] * comm.n,
                          out_shape=comm.out_shape(), scratch_shapes=comm.sems())(*comm.arrs)


def _call(body, *, name, grid, in_specs, out_specs, out_shape, args, sem, scratch_shapes=(), comm=None):
    in_specs, out_specs, out_shape, scratch_shapes = list(in_specs), list(out_specs), list(out_shape), list(scratch_shapes)
    if comm is None:
        outs = pl.pallas_call(body, name=name, grid=grid, in_specs=in_specs, out_specs=out_specs, out_shape=out_shape,
                              scratch_shapes=scratch_shapes, compiler_params=_params(*sem))(*args)
        return list(outs), []
    n_in, n_out, n_scr, nc = len(in_specs), len(out_specs), len(scratch_shapes), comm.n

    def wrapped(*refs):
        ins, refs = refs[:n_in], refs[n_in:]
        c_in, refs = refs[:nc], refs[nc:]
        outs, refs = refs[:n_out], refs[n_out:]
        c_out, refs = refs[:nc], refs[nc:]
        scr, sems = refs[:n_scr], refs[n_scr:]
        first = functools.reduce(jnp.logical_and, [pl.program_id(ax) == 0 for ax in range(len(grid))])
        last = functools.reduce(jnp.logical_and, [pl.program_id(ax) == g - 1 for ax, g in enumerate(grid)])

        @pl.when(first)
        def _():
            comm.start(c_in, c_out, sems)

        body(*ins, *outs, *scr)

        @pl.when(last)
        def _():
            comm.finish(c_in, c_out, sems)

    outs = pl.pallas_call(
        wrapped, name=name, grid=grid, in_specs=in_specs + [ANY] * nc, out_specs=out_specs + [ANY] * nc,
        out_shape=out_shape + comm.out_shape(), scratch_shapes=scratch_shapes + comm.sems(),
        compiler_params=_params(*(["arbitrary"] * len(grid))))(*args, *comm.arrs)
    return list(outs[:n_out]), list(outs[n_out:])


def _matmul(a, b, mode, out_dtype, name, tm, tn, tk, *, bias=None, res=None, res_scale=1.0, b_off=0, comm=None):
    tm = min(tm, a.shape[1] if mode == "tn" else a.shape[0])
    tk = min(tk, a.shape[0] if mode == "tn" else a.shape[1])
    if mode == "nn":
        (m, k), n = a.shape, b.shape[1]
        a_spec = pl.BlockSpec((tm, tk), lambda i, j, kk: (i, kk))
        b_spec = pl.BlockSpec((tk, tn), lambda i, j, kk: (kk + b_off, j))
        dims = NN
    elif mode == "nt":
        (m, k), n = a.shape, b.shape[0]
        a_spec = pl.BlockSpec((tm, tk), lambda i, j, kk: (i, kk))
        b_spec = pl.BlockSpec((tn, tk), lambda i, j, kk: (j, kk + b_off))
        dims = NT
    else:
        (k, m), n = a.shape, b.shape[1]
        a_spec = pl.BlockSpec((tk, tm), lambda i, j, kk: (kk, i))
        b_spec = pl.BlockSpec((tk, tn), lambda i, j, kk: (kk, j))
        dims = TN
    assert m % tm == 0 and n % tn == 0 and k % tk == 0, (name, m, n, k)
    nk = k // tk
    in_specs = [a_spec, b_spec]
    args = [a, b]
    if bias is not None:
        in_specs.append(pl.BlockSpec((1, tn), lambda i, j, kk: (0, j)))
        args.append(bias)
    if res is not None:
        in_specs.append(pl.BlockSpec((tm, tn), lambda i, j, kk: (i, j)))
        args.append(res)

    def finish(out, refs, o_ref):
        pos = 2
        if bias is not None:
            out = out + refs[pos][...]
            pos += 1
        if res is not None:
            out = out + res_scale * refs[pos][...].astype(F32)
        o_ref[...] = out.astype(out_dtype)

    def body_one_step(*refs):
        finish(_dot(refs[0][...], refs[1][...], dims), refs, refs[-1])

    def body(*refs):
        a_ref, b_ref = refs[0], refs[1]
        o_ref, acc = refs[-2], refs[-1]
        kk = pl.program_id(2)

        @pl.when(kk == 0)
        def _():
            acc[...] = jnp.zeros_like(acc)

        acc[...] += _dot(a_ref[...], b_ref[...], dims)

        @pl.when(kk == nk - 1)
        def _():
            finish(acc[...], refs, o_ref)

    (out,), moved = _call(
        body_one_step if nk == 1 else body, name=name, grid=(m // tm, n // tn, nk),
        in_specs=in_specs, out_specs=[pl.BlockSpec((tm, tn), lambda i, j, kk: (i, j))],
        out_shape=[jax.ShapeDtypeStruct((m, n), out_dtype)],
        scratch_shapes=[] if nk == 1 else [pltpu.VMEM((tm, tn), F32)],
        sem=("parallel", "parallel", "arbitrary"), args=args, comm=comm)
    return out if comm is None else (out, moved)


def _matmul_tn_pair(a0, a1, b, out_dtype, name, tm, tn, tk):
    (k, m), n = a0.shape, b.shape[1]
    tk = min(tk, k)
    assert a1.shape == a0.shape and m % tm == 0 and n % tn == 0 and k % tk == 0, (name, m, n, k)
    mi, nk = m // tm, k // tk

    def body(a0_ref, a1_ref, b_ref, o_ref, acc):
        i, kk = pl.program_id(0), pl.program_id(2)

        @pl.when(kk == 0)
        def _():
            acc[...] = jnp.zeros_like(acc)

        @pl.when(i < mi)
        def _():
            acc[...] += _dot(a0_ref[...], b_ref[...], TN)

        @pl.when(i >= mi)
        def _():
            acc[...] += _dot(a1_ref[...], b_ref[...], TN)

        @pl.when(kk == nk - 1)
        def _():
            o_ref[...] = acc[...].astype(out_dtype)

    return pl.pallas_call(
        body, name=name, grid=(2 * mi, n // tn, nk),
        in_specs=[pl.BlockSpec((tk, tm), lambda i, j, kk: (jnp.where(i < mi, kk, nk - 1), jnp.minimum(i, mi - 1))),
                  pl.BlockSpec((tk, tm), lambda i, j, kk: (jnp.where(i >= mi, kk, 0), jnp.maximum(i - mi, 0))),
                  pl.BlockSpec((tk, tn), lambda i, j, kk: (kk, j))],
        out_specs=pl.BlockSpec((tm, tn), lambda i, j, kk: (i, j)),
        out_shape=jax.ShapeDtypeStruct((2 * m, n), out_dtype),
        scratch_shapes=[pltpu.VMEM((tm, tn), F32)],
        compiler_params=_params("parallel", "parallel", "arbitrary"),
    )(a0, a1, b)


def _matmul_res_ln(a, b, xres, g, beta, name, tm, tk, comm=None):
    t, k = a.shape
    d = b.shape[1]
    nk = k // tk
    assert t % tm == 0 and k % tk == 0

    def body(a_ref, b_ref, x_ref, g_ref, beta_ref, z_ref, xo_ref, acc):
        kk = pl.program_id(1)

        @pl.when(kk == 0)
        def _():
            acc[...] = jnp.zeros_like(acc)

        acc[...] += _dot(a_ref[...], b_ref[...], NN)

        @pl.when(kk == nk - 1)
        def _():
            z = ALPHA * x_ref[...] + acc[...]
            nh, _ = _ln_stats(z)
            z_ref[...] = z
            xo_ref[...] = nh * g_ref[...] + beta_ref[...]

    row = pl.BlockSpec((tm, d), lambda i, kk: (i, 0))
    vec = pl.BlockSpec((1, d), lambda i, kk: (0, 0))
    outs, moved = _call(
        body, name=name, grid=(t // tm, nk),
        in_specs=[pl.BlockSpec((tm, tk), lambda i, kk: (i, kk)), pl.BlockSpec((tk, d), lambda i, kk: (kk, 0)), row, vec, vec],
        out_specs=[row, row],
        out_shape=[jax.ShapeDtypeStruct((t, d), F32), jax.ShapeDtypeStruct((t, d), F32)],
        scratch_shapes=[pltpu.VMEM((tm, d), F32)],
        sem=("parallel", "arbitrary"), args=(a, b, xres, g, beta), comm=comm)
    return outs if comm is None else (outs, moved)


def _matmul_ln_bwd(a, b, z, g, dres, name, tm, *, res=None, b_off=0):
    m, k = a.shape
    d = b.shape[1]
    tm = min(tm, m)
    assert m % tm == 0

    def body(*refs):
        a_ref, b_ref, z_ref, g_ref, dres_ref = refs[:5]
        dz_ref, dg_ref, db_ref = refs[-3:]

        @pl.when(pl.program_id(0) == 0)
        def _():
            dg_ref[...] = jnp.zeros_like(dg_ref)
            db_ref[...] = jnp.zeros_like(db_ref)

        dbr = _dot(a_ref[...], b_ref[...], NN)
        if res is not None:
            dbr = dbr + refs[5][...]
        nh, r = _ln_stats(z_ref[...])
        dy = ALPHA * dres_ref[...] + dbr
        dg_ref[...] += _colsum(dy * nh)
        db_ref[...] += _colsum(dy)
        dz_ref[...] = _ln_bwd_rows(dy * g_ref[...], nh, r)

    row = pl.BlockSpec((tm, d), lambda i: (i, 0))
    vec = pl.BlockSpec((1, d), lambda i: (0, 0))
    vshape = jax.ShapeDtypeStruct((1, d), F32)
    return pl.pallas_call(
        body, name=name, grid=(m // tm,),
        in_specs=[pl.BlockSpec((tm, k), lambda i: (i, 0)), pl.BlockSpec((k, d), lambda i: (b_off, 0)), row, vec, row]
        + ([row] if res is not None else []),
        out_specs=[row, vec, vec], out_shape=[jax.ShapeDtypeStruct((m, d), F32), vshape, vshape],
        compiler_params=_params("arbitrary"),
    )(a, b, z, g, dres, *([res] if res is not None else []))


def _matmul_res_ln_loss(a, b, xres, g, beta, target, name, tm):
    t, k = a.shape
    d = b.shape[1]
    tm = min(tm, t)

    def body(a_ref, b_ref, x_ref, g_ref, beta_ref, t_ref, dz_ref, dg_ref, db_ref, loss_ref):
        @pl.when(pl.program_id(0) == 0)
        def _():
            dg_ref[...] = jnp.zeros_like(dg_ref)
            db_ref[...] = jnp.zeros_like(db_ref)
            loss_ref[...] = jnp.zeros_like(loss_ref)

        nh, r = _ln_stats(ALPHA * x_ref[...] + _dot(a_ref[...], b_ref[...], NN))
        err = nh * g_ref[...] + beta_ref[...] - t_ref[...]
        loss_ref[...] += _colsum(err * err)
        dy = err * (1.0 / d)
        dg_ref[...] += _colsum(dy * nh)
        db_ref[...] += _colsum(dy)
        dz_ref[...] = _ln_bwd_rows(dy * g_ref[...], nh, r)

    row = pl.BlockSpec((tm, d), lambda i: (i, 0))
    vec = pl.BlockSpec((1, d), lambda i: (0, 0))
    vshape = jax.ShapeDtypeStruct((1, d), F32)
    return pl.pallas_call(
        body, name=name, grid=(t // tm,),
        in_specs=[pl.BlockSpec((tm, k), lambda i: (i, 0)), pl.BlockSpec((k, d), lambda i: (0, 0)), row, vec, vec, row],
        out_specs=[row, vec, vec, vec],
        out_shape=[jax.ShapeDtypeStruct((t, d), F32), vshape, vshape, vshape],
        compiler_params=_params("arbitrary"),
    )(a, b, xres, g, beta, target)


FFN_HALO = 16
FFN_CHUNK = 256
LANES = 128
SUBLANES = 8


def _rows_up(e, start, rows):
    if start % SUBLANES == 0:
        return e[start:start + rows]
    return pltpu.roll(e, e.shape[0] - start, 0)[0:rows]


def _fold(x):
    return jnp.sum(x.reshape(x.shape[0] // SUBLANES, SUBLANES, x.shape[1]), axis=0)


def _ffn_mid_fwd(h, cw, cb, name, tm=1024, tc=256, comm=None):
    t, f2 = h.shape
    tm = min(tm, t)
    f = f2 // 2
    nj, nt, hb = f // tc, t // tm, tm // FFN_HALO

    ch = min(FFN_CHUNK, tm)

    def body(hg, hgp, hv, hvp, cwg, cwv, cbg, cbv, u_ref, sg, sv):
        i = pl.program_id(1)
        for main, prev, s in ((hg, hgp, sg), (hv, hvp, sv)):
            s[0:FFN_HALO, :] = jnp.where(i > 0, prev[...].astype(F32), 0.0)
            s[FFN_HALO:, :] = main[...].astype(F32)
        o = SUBLANES - FFN_KERNEL + 1
        for lg in range(tc // LANES):
            cols = slice(lg * LANES, (lg + 1) * LANES)
            wg, wv = [cwg[k:k + 1, cols] for k in range(FFN_KERNEL)], [cwv[k:k + 1, cols] for k in range(FFN_KERNEL)]
            bg, bv = cbg[:, cols], cbv[:, cols]

            def chunk(c, carry):
                base = pl.multiple_of(c * ch, ch)
                eg = sg[pl.ds(base + FFN_HALO - SUBLANES, ch + SUBLANES), cols]
                ev = sv[pl.ds(base + FFN_HALO - SUBLANES, ch + SUBLANES), cols]
                cg = wg[0] * _rows_up(eg, o, ch) + wg[1] * _rows_up(eg, o + 1, ch) + wg[2] * _rows_up(eg, o + 2, ch) + bg
                cv = wv[0] * _rows_up(ev, o, ch) + wv[1] * _rows_up(ev, o + 1, ch) + wv[2] * _rows_up(ev, o + 2, ch) + bv
                u_ref[pl.ds(base, ch), cols] = (_gelu(cg) * cv).astype(BF16)
                return carry

            lax.fori_loop(0, tm // ch, chunk, 0)

    def main_spec(off):
        return pl.BlockSpec((tm, tc), lambda j, i: (i, j + off))

    def prev_spec(off):
        return pl.BlockSpec((FFN_HALO, tc), lambda j, i: (jnp.maximum(i * hb - 1, 0), j + off))

    def par_spec(rows, off):
        return pl.BlockSpec((rows, tc), lambda j, i: (0, j + off))

    (u,), moved = _call(
        body, name=name, grid=(nj, nt),
        in_specs=[main_spec(0), prev_spec(0), main_spec(nj), prev_spec(nj),
                  par_spec(FFN_KERNEL, 0), par_spec(FFN_KERNEL, nj), par_spec(1, 0), par_spec(1, nj)],
        out_specs=[pl.BlockSpec((tm, tc), lambda j, i: (i, j))],
        out_shape=[jax.ShapeDtypeStruct((t, f), BF16)],
        scratch_shapes=[pltpu.VMEM((tm + FFN_HALO, tc), F32), pltpu.VMEM((tm + FFN_HALO, tc), F32)],
        sem=("parallel", "arbitrary"), args=(h, h, h, h, cw, cw, cb, cb), comm=comm)
    return u if comm is None else (u, moved)


def _ffn_mid_bwd(h, du, cw, cb, name, tm=1024, tc=256, comm=None):
    t, f2 = h.shape
    tm = min(tm, t)
    f = f2 // 2
    nj, nt, hb = f // tc, t // tm, tm // FFN_HALO

    ch = min(FFN_CHUNK, tm)
    ahead = ch + SUBLANES

    def body(hg, hgp, hgn, hv, hvp, hvn, du_ref, dun_ref, cwg, cwv, cbg, cbv,
             dhg_ref, dhv_ref, dcwg_ref, dcwv_ref, dcbg_ref, dcbv_ref, sg, sv, sdu):
        i = pl.program_id(1)

        @pl.when(i == 0)
        def _():
            for ref in (dcwg_ref, dcwv_ref, dcbg_ref, dcbv_ref):
                ref[...] = jnp.zeros_like(ref)

        for main, prev, nxt, s in ((hg, hgp, hgn, sg), (hv, hvp, hvn, sv)):
            s[0:FFN_HALO, :] = jnp.where(i > 0, prev[...].astype(F32), 0.0)
            s[FFN_HALO:FFN_HALO + tm, :] = main[...].astype(F32)
            s[FFN_HALO + tm:, :] = nxt[...].astype(F32)
        sdu[0:tm, :] = du_ref[...].astype(F32)
        sdu[tm:, :] = jnp.where(i < nt - 1, dun_ref[...].astype(F32), 0.0)
        o = SUBLANES - FFN_KERNEL + 1
        for lg in range(tc // LANES):
            cols = slice(lg * LANES, (lg + 1) * LANES)
            wg, wv = [cwg[k:k + 1, cols] for k in range(FFN_KERNEL)], [cwv[k:k + 1, cols] for k in range(FFN_KERNEL)]
            bg, bv = cbg[:, cols], cbv[:, cols]

            def chunk(c, acc):
                base = pl.multiple_of(c * ch, ch)
                eg = sg[pl.ds(base + FFN_HALO - SUBLANES, ahead + SUBLANES), cols]
                ev = sv[pl.ds(base + FFN_HALO - SUBLANES, ahead + SUBLANES), cols]
                hgs = [_rows_up(eg, o + k, ahead) for k in range(FFN_KERNEL)]
                hvs = [_rows_up(ev, o + k, ahead) for k in range(FFN_KERNEL)]
                cg = wg[0] * hgs[0] + wg[1] * hgs[1] + wg[2] * hgs[2] + bg
                cv = wv[0] * hvs[0] + wv[1] * hvs[1] + wv[2] * hvs[2] + bv
                du_e = sdu[pl.ds(base, ahead), cols]
                gl, dgl = _gelu_and_grad(cg)

                def back(d, hs, w, dh_ref):
                    own = d[0:ch]
                    dh = w[2] * own + w[1] * _rows_up(d, 1, ch) + w[0] * _rows_up(d, 2, ch)
                    dh_ref[pl.ds(base, ch), cols] = dh.astype(BF16)
                    return [_fold(own)] + [_fold(own * hs[k][0:ch]) for k in range(FFN_KERNEL)]

                sums = back(du_e * cv * dgl, hgs, wg, dhg_ref) + back(du_e * gl, hvs, wv, dhv_ref)
                return tuple(a + s_ for a, s_ in zip(acc, sums))

            zero = jnp.zeros((SUBLANES, LANES), F32)
            acc = lax.fori_loop(0, tm // ch, chunk, (zero,) * (2 * (1 + FFN_KERNEL)))
            dcbg_ref[:, cols] += _colsum(acc[0])
            dcbv_ref[:, cols] += _colsum(acc[1 + FFN_KERNEL])
            for k in range(FFN_KERNEL):
                dcwg_ref[k:k + 1, cols] += _colsum(acc[1 + k])
                dcwv_ref[k:k + 1, cols] += _colsum(acc[2 + FFN_KERNEL + k])

    last_blk = t // FFN_HALO - 1

    def main_spec(off):
        return pl.BlockSpec((tm, tc), lambda j, i: (i, j + off))

    def prev_spec(off):
        return pl.BlockSpec((FFN_HALO, tc), lambda j, i: (jnp.maximum(i * hb - 1, 0), j + off))

    def next_spec(off):
        return pl.BlockSpec((FFN_HALO, tc), lambda j, i: (jnp.minimum((i + 1) * hb, last_blk), j + off))

    def par_spec(rows, off):
        return pl.BlockSpec((rows, tc), lambda j, i: (0, j + off))

    out_tile = pl.BlockSpec((tm, tc), lambda j, i: (i, j))
    outs, moved = _call(
        body, name=name, grid=(nj, nt),
        in_specs=[main_spec(0), prev_spec(0), next_spec(0), main_spec(nj), prev_spec(nj), next_spec(nj),
                  main_spec(0), next_spec(0),
                  par_spec(FFN_KERNEL, 0), par_spec(FFN_KERNEL, nj), par_spec(1, 0), par_spec(1, nj)],
        out_specs=[out_tile, out_tile, par_spec(FFN_KERNEL, 0), par_spec(FFN_KERNEL, 0), par_spec(1, 0), par_spec(1, 0)],
        out_shape=[jax.ShapeDtypeStruct((t, f), BF16), jax.ShapeDtypeStruct((t, f), BF16),
                   jax.ShapeDtypeStruct((FFN_KERNEL, f), F32), jax.ShapeDtypeStruct((FFN_KERNEL, f), F32),
                   jax.ShapeDtypeStruct((1, f), F32), jax.ShapeDtypeStruct((1, f), F32)],
        scratch_shapes=[pltpu.VMEM((tm + 2 * FFN_HALO, tc), F32), pltpu.VMEM((tm + 2 * FFN_HALO, tc), F32),
                        pltpu.VMEM((tm + FFN_HALO, tc), F32)],
        sem=("parallel", "arbitrary"), args=(h, h, h, h, h, h, du, du, cw, cw, cb, cb), comm=comm)
    return outs if comm is None else (outs, moved)


MIX_HALO = 32


def _glu(hh):
    return hh[:, 0:A_WIDTH] * _sigmoid(hh[:, A_WIDTH:2 * A_WIDTH])


def _fill_row_shifts(s):
    rows = s.shape[1] - SUBLANES
    for j in range(1, SUBLANES):
        s[j, 0:rows, :] = s[0, pl.ds(j, rows), :]


def _rows_from(s, start, rows):
    j = start % SUBLANES
    return s[j, start - j:start - j + rows, :]


def _tril_mask():
    return lax.broadcasted_iota(jnp.int32, (B_CHUNK, B_CHUNK), 0) >= lax.broadcasted_iota(jnp.int32, (B_CHUNK, B_CHUNK), 1)


def _spatial_mix(q, ms_ref, sbt_ref, tm):
    mask = _tril_mask()
    ws = [jnp.where(mask, ms_ref[g], 0.0).astype(BF16) for g in range(B_GROUPS)]
    qb = q.astype(BF16)
    rows = []
    for c in range(tm // B_CHUNK):
        cols = [_dot(ws[g], qb[c * B_CHUNK:(c + 1) * B_CHUNK, g * 128:(g + 1) * 128], NN) + sbt_ref[:, g:g + 1]
                for g in range(B_GROUPS)]
        rows.append(jnp.concatenate(cols, axis=1))
    return jnp.concatenate(rows, axis=0)


def _mixer_mid_fwd(h, cw, cb, ag, ab, bg, bb, ms, sbt, name, tm=256, comm=None):
    t = h.shape[0]
    nt, hb = t // tm, tm // MIX_HALO
    o = MIX_HALO - A_KERNEL + 1

    def body(h_ref, hp_ref, cw_ref, cb_ref, ag_ref, ab_ref, bg_ref, bb_ref, ms_ref, sbt_ref, cat_ref, sp):
        i = pl.program_id(0)
        sp[0, 0:MIX_HALO, :] = jnp.where(i > 0, _glu(hp_ref[:, 0:2 * A_WIDTH].astype(F32)), 0.0)
        sp[0, MIX_HALO:, :] = _glu(h_ref[:, 0:2 * A_WIDTH].astype(F32))
        _fill_row_shifts(sp)
        y = jnp.zeros((tm, A_WIDTH), F32) + cb_ref[...]
        for k in range(A_KERNEL):
            y = y + cw_ref[k:k + 1, :] * _rows_from(sp, o + k, tm)
        nh, _ = _ln_stats(y)
        ln = nh * ag_ref[...] + ab_ref[...]
        cat_ref[:, 0:A_WIDTH] = (ln * _sigmoid(ln)).astype(BF16)
        u = _gelu(h_ref[:, 1024:1536].astype(F32))
        nb, _ = _ln_stats(_gelu(h_ref[:, 1536:2048].astype(F32)))
        mixed = _spatial_mix(nb * bg_ref[...] + bb_ref[...], ms_ref, sbt_ref, tm)
        cat_ref[:, A_WIDTH:] = (u * mixed).astype(BF16)

    vec = pl.BlockSpec((1, A_WIDTH), lambda i: (0, 0))
    (cat,), moved = _call(
        body, name=name, grid=(nt,),
        in_specs=[pl.BlockSpec((tm, 2048), lambda i: (i, 0)),
                  pl.BlockSpec((MIX_HALO, 2048), lambda i: (jnp.maximum(i * hb - 1, 0), 0)),
                  pl.BlockSpec((A_KERNEL, A_WIDTH), lambda i: (0, 0)), vec, vec, vec, vec, vec,
                  pl.BlockSpec((B_GROUPS, B_CHUNK, B_CHUNK), lambda i: (0, 0, 0)),
                  pl.BlockSpec((B_CHUNK, B_GROUPS), lambda i: (0, 0))],
        out_specs=[pl.BlockSpec((tm, D_MODEL), lambda i: (i, 0))],
        out_shape=[jax.ShapeDtypeStruct((t, D_MODEL), BF16)],
        scratch_shapes=[pltpu.VMEM((SUBLANES, tm + MIX_HALO, A_WIDTH), F32)],
        sem=("parallel",), args=(h, h, cw, cb, ag, ab, bg, bb, ms, sbt), comm=comm)
    return cat if comm is None else (cat, moved)


def _mixer_mid_bwd(h, dcat, cw, cb, ag, ab, bg, bb, ms, mst, sbt, name, tm=256, comm=None):
    t = h.shape[0]
    nt, hb = t // tm, tm // MIX_HALO
    o = MIX_HALO - A_KERNEL + 1
    r = tm + MIX_HALO
    nchunk = tm // B_CHUNK

    def body(h_ref, hp_ref, hn_ref, dc_ref, dcn_ref, cw_ref, cb_ref, ag_ref, ab_ref, bg_ref, bb_ref, ms_ref, mst_ref, sbt_ref,
             dh_ref, dcw_ref, dcb_ref, dag_ref, dab_ref, dbg_ref, dbb_ref, dms_ref, dsb_ref, sp, sdy, sbacc):
        i = pl.program_id(0)

        @pl.when(i == 0)
        def _():
            for ref in (dcw_ref, dcb_ref, dag_ref, dab_ref, dbg_ref, dbb_ref, dms_ref, dsb_ref, sbacc):
                ref[...] = jnp.zeros_like(ref)

        sp[0, 0:MIX_HALO, :] = jnp.where(i > 0, _glu(hp_ref[:, 0:2 * A_WIDTH].astype(F32)), 0.0)
        sp[0, MIX_HALO:MIX_HALO + tm, :] = _glu(h_ref[:, 0:2 * A_WIDTH].astype(F32))
        sp[0, MIX_HALO + tm:, :] = _glu(hn_ref[:, 0:2 * A_WIDTH].astype(F32))
        _fill_row_shifts(sp)
        y = jnp.zeros((r, A_WIDTH), F32) + cb_ref[...]
        for k in range(A_KERNEL):
            y = y + cw_ref[k:k + 1, :] * _rows_from(sp, o + k, r)
        nh, rs = _ln_stats(y)
        ln = nh * ag_ref[...] + ab_ref[...]
        sg = _sigmoid(ln)
        dao = jnp.concatenate([dc_ref[:, 0:A_WIDTH].astype(F32),
                               jnp.where(i < nt - 1, dcn_ref[:, 0:A_WIDTH].astype(F32), 0.0)], axis=0)
        dln = dao * (sg * (1.0 + ln * (1.0 - sg)))
        dag_ref[...] += _colsum(dln[0:tm] * nh[0:tm])
        dab_ref[...] += _colsum(dln[0:tm])
        sdy[0] = _ln_bwd_rows(dln * ag_ref[...], nh, rs)
        _fill_row_shifts(sdy)
        dy_own = sdy[0, 0:tm, :]
        dcb_ref[...] += _colsum(dy_own)
        dp = jnp.zeros((tm, A_WIDTH), F32)
        for k in range(A_KERNEL):
            dcw_ref[k:k + 1, :] += _colsum(dy_own * _rows_from(sp, o + k, tm))
            dp = dp + cw_ref[k:k + 1, :] * _rows_from(sdy, A_KERNEL - 1 - k, tm)
        av = h_ref[:, 0:A_WIDTH].astype(F32)
        s = _sigmoid(h_ref[:, A_WIDTH:2 * A_WIDTH].astype(F32))
        dh_ref[:, 0:A_WIDTH] = (dp * s).astype(BF16)
        dh_ref[:, A_WIDTH:2 * A_WIDTH] = (dp * av * s * (1.0 - s)).astype(BF16)

        u, dgu = _gelu_and_grad(h_ref[:, 1024:1536].astype(F32))
        w, dgw = _gelu_and_grad(h_ref[:, 1536:2048].astype(F32))
        nb, rb = _ln_stats(w)
        q = nb * bg_ref[...] + bb_ref[...]
        mixed = _spatial_mix(q, ms_ref, sbt_ref, tm)
        dbo = dc_ref[:, A_WIDTH:].astype(F32)
        dh_ref[:, 1024:1536] = (dbo * mixed * dgu).astype(BF16)
        dmx = dbo * u
        mask = _tril_mask()
        wst = [jnp.where(mask.T, mst_ref[g], 0.0).astype(BF16) for g in range(B_GROUPS)]
        qb = q.astype(BF16)
        dmb = dmx.astype(BF16)
        rows = []
        for c in range(nchunk):
            cols = []
            for g in range(B_GROUPS):
                rs_, cs_ = slice(c * B_CHUNK, (c + 1) * B_CHUNK), slice(g * 128, (g + 1) * 128)
                sbacc[g] += dmx[rs_, cs_]
                dms_ref[g] += _dot(dmb[rs_, cs_], qb[rs_, cs_], NT)
                cols.append(_dot(wst[g], dmb[rs_, cs_], NN))
            rows.append(jnp.concatenate(cols, axis=1))
        dq = jnp.concatenate(rows, axis=0)
        dbg_ref[...] += _colsum(dq * nb)
        dbb_ref[...] += _colsum(dq)
        dh_ref[:, 1536:2048] = (_ln_bwd_rows(dq * bg_ref[...], nb, rb) * dgw).astype(BF16)

        @pl.when(i == nt - 1)
        def _():
            for g in range(B_GROUPS):
                dms_ref[g] = jnp.where(mask, dms_ref[g], 0.0)
                dsb_ref[g] = jnp.sum(sbacc[g], axis=1, keepdims=True)

    last_blk = t // MIX_HALO - 1
    vec = pl.BlockSpec((1, A_WIDTH), lambda i: (0, 0))
    mat = pl.BlockSpec((B_GROUPS, B_CHUNK, B_CHUNK), lambda i: (0, 0, 0))
    taps = pl.BlockSpec((A_KERNEL, A_WIDTH), lambda i: (0, 0))

    def halo(width, which):
        if which == "prev":
            return pl.BlockSpec((MIX_HALO, width), lambda i: (jnp.maximum(i * hb - 1, 0), 0))
        return pl.BlockSpec((MIX_HALO, width), lambda i: (jnp.minimum((i + 1) * hb, last_blk), 0))

    vshape = jax.ShapeDtypeStruct((1, A_WIDTH), F32)
    outs, moved = _call(
        body, name=name, grid=(nt,),
        in_specs=[pl.BlockSpec((tm, 2048), lambda i: (i, 0)), halo(2048, "prev"), halo(2048, "next"),
                  pl.BlockSpec((tm, D_MODEL), lambda i: (i, 0)), halo(D_MODEL, "next"),
                  taps, vec, vec, vec, vec, vec, mat, mat, pl.BlockSpec((B_CHUNK, B_GROUPS), lambda i: (0, 0))],
        out_specs=[pl.BlockSpec((tm, 2048), lambda i: (i, 0)), taps, vec, vec, vec, vec, vec, mat,
                   pl.BlockSpec((B_GROUPS, B_CHUNK, 1), lambda i: (0, 0, 0))],
        out_shape=[jax.ShapeDtypeStruct((t, 2048), BF16), jax.ShapeDtypeStruct((A_KERNEL, A_WIDTH), F32),
                   vshape, vshape, vshape, vshape, vshape,
                   jax.ShapeDtypeStruct((B_GROUPS, B_CHUNK, B_CHUNK), F32), jax.ShapeDtypeStruct((B_GROUPS, B_CHUNK, 1), F32)],
        scratch_shapes=[pltpu.VMEM((SUBLANES, tm + 2 * MIX_HALO, A_WIDTH), F32), pltpu.VMEM((SUBLANES, r, A_WIDTH), F32),
                        pltpu.VMEM((B_GROUPS, B_CHUNK, B_CHUNK), F32)],
        sem=("arbitrary",), args=(h, h, h, dcat, dcat, cw, cb, ag, ab, bg, bb, ms, mst, sbt), comm=comm)
    return outs if comm is None else (outs, moved)


Q_WIDTH = N_Q_HEADS * HEAD_DIM
KV_WIDTH = 2 * N_KV_HEADS * HEAD_DIM
PAIRS_PER_KV = N_Q_HEADS // N_KV_HEADS // 2
ATT_SCALE = 1.0 / math.sqrt(HEAD_DIM)


def _dup_heads(pair_cols, kv_head):
    lane = lax.broadcasted_iota(jnp.int32, pair_cols.shape, 1)
    rolled = pltpu.roll(pair_cols, HEAD_DIM, 1)
    first = lane < HEAD_DIM
    return jnp.where(first, pair_cols, rolled) if kv_head == 0 else jnp.where(first, rolled, pair_cols)


HEADS_PER_KV = N_Q_HEADS // N_KV_HEADS


def _stack_heads(ref, kh):
    lane = lax.broadcasted_iota(jnp.int32, (ATT_BLOCK, 128), 1)
    rows = []
    for pr in range(PAIRS_PER_KV):
        c0 = (kh * PAIRS_PER_KV + pr) * 128
        pair = ref[:, c0:c0 + 128]
        rows += [jnp.where(lane < HEAD_DIM, pair, jnp.zeros_like(pair)), jnp.where(lane < HEAD_DIM, jnp.zeros_like(pair), pair)]
    return jnp.concatenate(rows, axis=0)


def _unstack_heads(stacked, kh, write):
    lane = lax.broadcasted_iota(jnp.int32, (ATT_BLOCK, 128), 1)
    for pr in range(PAIRS_PER_KV):
        first = stacked[(2 * pr) * ATT_BLOCK:(2 * pr + 1) * ATT_BLOCK]
        second = stacked[(2 * pr + 1) * ATT_BLOCK:(2 * pr + 2) * ATT_BLOCK]
        write((kh * PAIRS_PER_KV + pr) * 128, jnp.where(lane < HEAD_DIM, first, second))


def _sink_row(sink_ref, kh):
    return jnp.concatenate([jnp.full((1, ATT_BLOCK), sink_ref[0, kh * HEADS_PER_KV + h], F32) for h in range(HEADS_PER_KV)], axis=1)


def _att_mask_t(n):
    sj = lax.broadcasted_iota(jnp.int32, (2 * ATT_BLOCK, HEADS_PER_KV * ATT_BLOCK), 0)
    qi = lax.broadcasted_iota(jnp.int32, (2 * ATT_BLOCK, HEADS_PER_KV * ATT_BLOCK), 1) & (ATT_BLOCK - 1)
    diff = qi + ATT_BLOCK - sj
    return (diff >= 0) & (diff < ATT_BLOCK) & ((n > 0) | (sj >= ATT_BLOCK))


def _att_probs_t(q_all, k2, mask_t, sink):
    st = _dot(k2, q_all, NT) * ATT_SCALE
    st = jnp.where(mask_t, st, -jnp.inf)
    m = jnp.maximum(jnp.max(st, axis=0, keepdims=True), sink)
    e = jnp.exp(st - m)
    es = jnp.exp(sink - m)
    inv = 1.0 / (jnp.sum(e, axis=0, keepdims=True) + es)
    return e * inv, es * inv


def _attn_fwd(qkv, sinks, name, comm=None):
    t = qkv.shape[0]
    nb = t // ATT_BLOCK
    kvb = Q_WIDTH // KV_WIDTH

    def body(sink_ref, q_ref, kv_ref, kvp_ref, o_ref):
        n = pl.program_id(0)
        mask_t = _att_mask_t(n)
        kv = jnp.concatenate([kvp_ref[...], kv_ref[...]], axis=0).astype(F32)

        def write(c0, pair):
            o_ref[:, c0:c0 + 128] = pair.astype(BF16)

        for kh in range(N_KV_HEADS):
            k2 = _dup_heads(kv[:, 0:128], kh).astype(BF16)
            v2 = _dup_heads(kv[:, 128:256], kh).astype(BF16)
            pt, _ = _att_probs_t(_stack_heads(q_ref, kh), k2, mask_t, _sink_row(sink_ref, kh))
            _unstack_heads(_dot(v2, pt, TN).T, kh, write)

    (out,), moved = _call(
        body, name=name, grid=(nb,),
        in_specs=[pl.BlockSpec(memory_space=pltpu.SMEM),
                  pl.BlockSpec((ATT_BLOCK, Q_WIDTH), lambda n: (n, 0)),
                  pl.BlockSpec((ATT_BLOCK, KV_WIDTH), lambda n: (n, kvb)),
                  pl.BlockSpec((ATT_BLOCK, KV_WIDTH), lambda n: (jnp.maximum(n - 1, 0), kvb))],
        out_specs=[pl.BlockSpec((ATT_BLOCK, Q_WIDTH), lambda n: (n, 0))],
        out_shape=[jax.ShapeDtypeStruct((t, Q_WIDTH), BF16)],
        sem=("parallel",), args=(sinks, qkv, qkv, qkv), comm=comm)
    return out if comm is None else (out, moved)


def _attn_bwd(qkv, d_o, sinks, name, comm=None):
    t = qkv.shape[0]
    nb = t // ATT_BLOCK
    kvb = Q_WIDTH // KV_WIDTH

    def body(sink_ref, q_ref, kv_ref, kvp_ref, do_ref, dq_ref, dkv_ref, dbq_ref, dbkv_ref, dsink_ref, carry):
        n = pl.program_id(0)

        @pl.when(n == 0)
        def _():
            for ref in (dbq_ref, dbkv_ref, dsink_ref, carry):
                ref[...] = jnp.zeros_like(ref)
            dkv_ref[...] = jnp.zeros_like(dkv_ref)

        @pl.when(n < nb)
        def _():
            mask_t = _att_mask_t(n)
            kv = jnp.concatenate([kvp_ref[...], kv_ref[...]], axis=0).astype(F32)
            lane2 = lax.broadcasted_iota(jnp.int32, (2 * ATT_BLOCK, 128), 1)
            sink_lane = lax.broadcasted_iota(jnp.int32, (1, 128), 1)
            dsink = jnp.zeros((1, 128), F32)
            dk_parts, dv_parts = [], []

            def write(c0, pair):
                dbq_ref[:, c0:c0 + 128] += _colsum(pair)
                dq_ref[:, c0:c0 + 128] = pair.astype(BF16)

            for kh in range(N_KV_HEADS):
                k2 = _dup_heads(kv[:, 0:128], kh).astype(BF16)
                v2 = _dup_heads(kv[:, 128:256], kh).astype(BF16)
                q_all = _stack_heads(q_ref, kh)
                do_all = _stack_heads(do_ref, kh)
                pt, ps = _att_probs_t(q_all, k2, mask_t, _sink_row(sink_ref, kh))
                dpt = _dot(v2, do_all, NT)
                delta = jnp.sum(pt * dpt, axis=0, keepdims=True)
                dst = pt * (dpt - delta) * ATT_SCALE
                psd = ps * delta
                for h in range(HEADS_PER_KV):
                    dsink = dsink + jnp.where(sink_lane == kh * HEADS_PER_KV + h,
                                              -jnp.sum(psd[:, h * ATT_BLOCK:(h + 1) * ATT_BLOCK]), 0.0)
                _unstack_heads(_dot(k2, dst, TN).T, kh, write)
                dk_acc = _dot(dst, q_all, NN)
                dv_acc = _dot(pt, do_all, NN)
                dk_parts.append(dk_acc + pltpu.roll(dk_acc, HEAD_DIM, 1))
                dv_parts.append(dv_acc + pltpu.roll(dv_acc, HEAD_DIM, 1))
            dk = jnp.where(lane2 < HEAD_DIM, dk_parts[0], dk_parts[1])
            dv = jnp.where(lane2 < HEAD_DIM, dv_parts[0], dv_parts[1])
            dkv_new = jnp.concatenate([dk, dv], axis=1)
            done = carry[...] + dkv_new[0:ATT_BLOCK]

            @pl.when(n > 0)
            def _():
                dkv_ref[...] = done.astype(BF16)
                dbkv_ref[...] += _colsum(done)

            carry[...] = dkv_new[ATT_BLOCK:]
            dsink_ref[...] += dsink

        @pl.when(n == nb)
        def _():
            dkv_ref[...] = carry[...].astype(BF16)
            dbkv_ref[...] += _colsum(carry[...])

    def clamp(n):
        return jnp.minimum(n, nb - 1)

    outs, moved = _call(
        body, name=name, grid=(nb + 1,),
        in_specs=[pl.BlockSpec(memory_space=pltpu.SMEM),
                  pl.BlockSpec((ATT_BLOCK, Q_WIDTH), lambda n: (clamp(n), 0)),
                  pl.BlockSpec((ATT_BLOCK, KV_WIDTH), lambda n: (clamp(n), kvb)),
                  pl.BlockSpec((ATT_BLOCK, KV_WIDTH), lambda n: (jnp.maximum(clamp(n) - 1, 0), kvb)),
                  pl.BlockSpec((ATT_BLOCK, Q_WIDTH), lambda n: (clamp(n), 0))],
        out_specs=[pl.BlockSpec((ATT_BLOCK, Q_WIDTH), lambda n: (clamp(n), 0)),
                   pl.BlockSpec((ATT_BLOCK, KV_WIDTH), lambda n: (jnp.maximum(n - 1, 0), 0)),
                   pl.BlockSpec((1, Q_WIDTH), lambda n: (0, 0)),
                   pl.BlockSpec((1, KV_WIDTH), lambda n: (0, 0)),
                   pl.BlockSpec((1, 128), lambda n: (0, 0))],
        out_shape=[jax.ShapeDtypeStruct((t, Q_WIDTH), BF16), jax.ShapeDtypeStruct((t, KV_WIDTH), BF16),
                   jax.ShapeDtypeStruct((1, Q_WIDTH), F32), jax.ShapeDtypeStruct((1, KV_WIDTH), F32),
                   jax.ShapeDtypeStruct((1, 128), F32)],
        scratch_shapes=[pltpu.VMEM((ATT_BLOCK, KV_WIDTH), F32)],
        sem=("arbitrary",), args=(sinks, qkv, qkv, qkv, d_o), comm=comm)
    return outs if comm is None else (outs, moved)


def _adamw_math(g, w, m, v):
    m = ADAM_B1 * m + (1.0 - ADAM_B1) * g
    v = ADAM_B2 * v + (1.0 - ADAM_B2) * (g * g)
    m_hat = m / (1.0 - ADAM_B1 ** ADAM_STEP)
    v_hat = v / (1.0 - ADAM_B2 ** ADAM_STEP)
    delta = -ADAM_LR * (m_hat / (jnp.sqrt(v_hat) + ADAM_EPS) + ADAM_WD * w)
    return delta, m, v


def _sum_partials(p_ref):
    g = p_ref[0].astype(F32)
    for s in range(1, N_DEV):
        g = g + p_ref[s].astype(F32)
    return g


def _adamw_big(parts, w, m, v, name, tr):
    r, c = w.shape
    tiles = [p.shape[1] // tr for p in parts]
    starts = [sum(tiles[:l]) for l in range(len(parts))]
    assert all(p.shape[1] % tr == 0 for p in parts) and sum(tiles) * tr == r

    def body(*refs):
        p_refs, (w_ref, m_ref, v_ref, g_out, d_out, m_out, v_out) = refs[:len(parts)], refs[len(parts):]
        i = pl.program_id(0)
        for l, p_ref in enumerate(p_refs):
            @pl.when((i >= starts[l]) & (i < starts[l] + tiles[l]))
            def _():
                g = _sum_partials(p_ref)
                g_out[...] = g
                d_out[...], m_out[...], v_out[...] = _adamw_math(g, w_ref[...], m_ref[...], v_ref[...])

    def part_spec(l):
        return pl.BlockSpec((N_DEV, tr, c), lambda i: (0, jnp.clip(i - starts[l], 0, tiles[l] - 1), 0))

    tile = pl.BlockSpec((tr, c), lambda i: (i, 0))
    shape = jax.ShapeDtypeStruct((r, c), F32)
    return pl.pallas_call(
        body, name=name, grid=(r // tr,),
        in_specs=[part_spec(l) for l in range(len(parts))] + [tile, tile, tile],
        out_specs=[tile] * 4, out_shape=[shape] * 4,
        compiler_params=_params("parallel"),
    )(*parts, w, m, v)


def _adamw_small(parts, ws, ms, vs, name):
    n = len(ws)

    def body(*refs):
        ins, outs = refs[:4 * n], refs[4 * n:]
        for a in range(n):
            g = _sum_partials(ins[a])
            outs[4 * a][...] = g
            outs[4 * a + 1][...], outs[4 * a + 2][...], outs[4 * a + 3][...] = _adamw_math(
                g, ins[n + a][...], ins[2 * n + a][...], ins[3 * n + a][...])

    out_shape = []
    for w in ws:
        out_shape += [jax.ShapeDtypeStruct(w.shape, F32)] * 4
    return pl.pallas_call(body, name=name, out_shape=out_shape, compiler_params=_params())(*parts, *ws, *ms, *vs)


PACK_LANES = 128
PACK_ROWS = 8


def _pack(arrs):
    flat = jnp.concatenate([a.reshape(-1).astype(F32) for a in arrs])
    unit = PACK_LANES * PACK_ROWS
    total = -(-flat.shape[0] // unit) * unit
    return jnp.pad(flat, (0, total - flat.shape[0])).reshape(-1, PACK_LANES)


def _unpack(buf, shapes):
    flat = buf.reshape(N_DEV, -1)
    out, pos = [], 0
    for s in shapes:
        size = math.prod(s)
        out.append(flat[:, pos:pos + size].reshape((N_DEV,) + tuple(s)))
        pos += size
    return out


def _interleave(g):
    return jnp.transpose(g, (1, 0, 2)).reshape(g.shape[1], -1)


def _deinterleave(w):
    r = w.shape[0]
    return jnp.transpose(w.reshape(r, N_DEV, -1), (1, 0, 2))


def _ffn_backward(dz, x_in, z_in, g_in, h, u, w_up_t, cw, cb, w_down, tag, exchange=()):
    du = _matmul(dz, w_down, "nt", BF16, f"ffn{tag}_du", 1024, 1408, 1024)
    d_w_down = _matmul(u, dz, "tn", BF16, f"ffn{tag}_dwdown", 1408, 1024, 2048)
    (dhg, dhv, dcwg, dcwv, dcbg, dcbv), moved = _ffn_mid_bwd(
        h, du, cw, cb, f"ffn{tag}_mid_bwd", comm=_Comm(exchange=[d_w_down.reshape(N_DEV, -1, D_MODEL), *exchange]))
    d_w_up_t = _matmul_tn_pair(dhg, dhv, x_in, BF16, f"ffn{tag}_dwup", 1408, 1024, 1024)
    dx = _matmul(dhv, w_up_t, "nn", F32, f"ffn{tag}_dx_value", 1024, 1024, D_FF, b_off=1)
    dz_in, dg_in, db_in = _matmul_ln_bwd(dhg, w_up_t, z_in, g_in, dz, f"ffn{tag}_dx_gate_ln_bwd", 512, res=dx)
    return (dz_in, dg_in, db_in, d_w_up_t.reshape(N_DEV, -1, D_MODEL),
            jnp.concatenate([dcwg, dcwv], axis=1), jnp.concatenate([dcbg, dcbv], axis=1), moved)


def kernel(x, ab_w_in, a_conv_w, a_conv_b, a_norm_g, a_norm_b, b_norm_g, b_norm_b, b_spatial_w, b_spatial_b, ab_w_out, c_w_qkv, c_b_qkv, c_sinks, c_w_o, ffn_w_up, ffn_conv_w, ffn_conv_b, ffn_w_down, ln_g, ln_b, loss_target, m_ab_w_in, m_a_conv_w, m_a_conv_b, m_a_norm_g, m_a_norm_b, m_b_norm_g, m_b_norm_b, m_b_spatial_w, m_b_spatial_b, m_ab_w_out, m_c_w_qkv, m_c_b_qkv, m_c_sinks, m_c_w_o, m_ffn_w_up, m_ffn_conv_w, m_ffn_conv_b, m_ffn_w_down, m_ln_g, m_ln_b, v_ab_w_in, v_a_conv_w, v_a_conv_b, v_a_norm_g, v_a_norm_b, v_b_norm_g, v_b_norm_b, v_b_spatial_w, v_b_spatial_b, v_ab_w_out, v_c_w_qkv, v_c_b_qkv, v_c_sinks, v_c_w_o, v_ffn_w_up, v_ffn_conv_w, v_ffn_conv_b, v_ffn_w_down, v_ln_g, v_ln_b):
    me = 4 * lax.axis_index("x") + 2 * lax.axis_index("y") + lax.axis_index("c")
    xt = x[0]
    t = xt.shape[0]

    small_shard_shapes = [a_conv_w.shape, c_b_qkv.shape, ffn_conv_w.shape, ln_g.shape, ln_b.shape]
    up_shard = [jnp.swapaxes(ffn_w_up[l], 0, 1).astype(BF16) for l in range(2)]
    qkv_shard = jnp.swapaxes(c_w_qkv[0], 0, 1).astype(BF16)
    down_shard = [ffn_w_down[l].astype(BF16) for l in range(2)]
    g_win, g_small = _comm_only(
        _Comm(gather=[ab_w_in[0].astype(BF16), _pack([a_conv_w, c_b_qkv, ffn_conv_w, ln_g, ln_b])]), "gather_first")
    w_in = _interleave(g_win)
    g_acw, g_bqkv, g_fcw, g_lng, g_lnb = _unpack(g_small, small_shard_shapes)
    acw = _interleave(g_acw[:, 0])
    bqkv = g_bqkv[:, 0].reshape(1, -1)
    fcw = [_interleave(g_fcw[:, l]) for l in range(2)]
    lng = jnp.transpose(g_lng, (1, 2, 0, 3)).reshape(2, 2, 1, D_MODEL)
    lnb = jnp.transpose(g_lnb, (1, 2, 0, 3)).reshape(2, 2, 1, D_MODEL)
    fcb = [ffn_conv_b[l:l + 1] for l in range(2)]
    ms = b_spatial_w[0]
    mst = jnp.swapaxes(ms, 1, 2)
    sbt = b_spatial_b[0].T

    h0, (g_wout,) = _matmul(xt, w_in, "nn", BF16, "mix_in", 1024, 1024, 1024, comm=_Comm(gather=[ab_w_out[0].astype(BF16)]))
    w_out = g_wout.reshape(D_MODEL, D_MODEL)
    cat, (g_wup0,) = _mixer_mid_fwd(h0, acw, a_conv_b, a_norm_g, a_norm_b, b_norm_g, b_norm_b, ms, sbt, "mix_mid_fwd",
                                    comm=_Comm(gather=[up_shard[0]]))
    w_up0 = g_wup0.reshape(2 * D_FF, D_MODEL)
    z1, x1 = _matmul_res_ln(cat, w_out, xt, lng[0, 0], lnb[0, 0], "mix_out_ln", 512, D_MODEL)
    hf0, (g_wdown0, g_wqkv) = _matmul(x1, w_up0, "nt", BF16, "ffn0_up", 1024, 1408, 1024,
                                      comm=_Comm(gather=[down_shard[0], qkv_shard]))
    w_down0 = g_wdown0.reshape(D_FF, D_MODEL)
    w_qkv = g_wqkv.reshape(Q_WIDTH + KV_WIDTH, D_MODEL)
    u0, (g_wup1,) = _ffn_mid_fwd(hf0, fcw[0], fcb[0], "ffn0_mid_fwd", comm=_Comm(gather=[up_shard[1]]))
    w_up1 = g_wup1.reshape(2 * D_FF, D_MODEL)
    (z2, x2), (g_wo,) = _matmul_res_ln(u0, w_down0, x1, lng[0, 1], lnb[0, 1], "ffn0_down_ln", 512, D_FF,
                                       comm=_Comm(gather=[c_w_o[0].astype(BF16)]))
    w_o = g_wo.reshape(D_MODEL, D_MODEL)
    qkv = _matmul(x2, w_qkv, "nt", BF16, "att_qkv", 1024, 1280, 1024, bias=bqkv)
    att, (g_wdown1,) = _attn_fwd(qkv, c_sinks, "att_fwd", comm=_Comm(gather=[down_shard[1]]))
    w_down1 = g_wdown1.reshape(D_FF, D_MODEL)
    z3, x3 = _matmul_res_ln(att, w_o, x2, lng[1, 0], lnb[1, 0], "att_out_ln", 512, D_MODEL)
    hf1 = _matmul(x3, w_up1, "nt", BF16, "ffn1_up", 1024, 1408, 1024)
    u1 = _ffn_mid_fwd(hf1, fcw[1], fcb[1], "ffn1_mid_fwd")

    dz4, dg11, db11, loss_terms = _matmul_res_ln_loss(u1, w_down1, x3, lng[1, 1], lnb[1, 1], loss_target[0],
                                                      "ffn1_down_ln_loss", 512)
    dz3, dg10, db10, d_wup1, d_fcw1, d_fcb1, (p_wdown1,) = _ffn_backward(
        dz4, x3, z3, lng[1, 0], hf1, u1, w_up1, fcw[1], fcb[1], w_down1, 1)
    d_att = _matmul(dz3, w_o, "nt", BF16, "att_dout", 1024, 1024, 1024)
    d_wo = _matmul(att, dz3, "tn", BF16, "att_dwo", 1024, 1024, 512)
    (dq, dkv, dbq, dbkv, dsinks), (p_wup1,) = _attn_bwd(qkv, d_att, c_sinks, "att_bwd", comm=_Comm(exchange=[d_wup1]))
    d_wqkv = jnp.concatenate([_matmul(dq, x2, "tn", BF16, "att_dwq", 1024, 1024, 1024),
                              _matmul(dkv, x2, "tn", BF16, "att_dwkv", KV_WIDTH, 1024, 1024)], axis=0)
    dx2 = _matmul(dkv, w_qkv, "nn", F32, "att_dx_kv", 1024, 1024, KV_WIDTH, b_off=Q_WIDTH // KV_WIDTH)
    dz2, dg01, db01 = _matmul_ln_bwd(dq, w_qkv, z2, lng[0, 1], dz3, "att_dx_q_ln_bwd", 512, res=dx2)
    dz1, dg00, db00, d_wup0, d_fcw0, d_fcb0, (p_wdown0, p_wqkv, p_wo) = _ffn_backward(
        dz2, x1, z1, lng[0, 0], hf0, u0, w_up0, fcw[0], fcb[0], w_down0, 0,
        exchange=[d_wqkv.reshape(N_DEV, -1, D_MODEL), d_wo.reshape(N_DEV, -1, D_MODEL)])
    dcat = _matmul(dz1, w_out, "nt", BF16, "mix_dcat", 1024, 1024, 1024)
    d_wout = _matmul(cat, dz1, "tn", BF16, "mix_dwout", 1024, 1024, 512)
    (dh0, d_acw, d_acb, d_ang, d_anb, d_bng, d_bnb, d_ms, d_sb), (p_wup0, p_wout) = _mixer_mid_bwd(
        h0, dcat, acw, a_conv_b, a_norm_g, a_norm_b, b_norm_g, b_norm_b, ms, mst, sbt, "mix_mid_bwd",
        comm=_Comm(exchange=[d_wup0, d_wout.reshape(N_DEV, -1, D_MODEL)]))
    d_bqkv = jnp.concatenate([dbq, dbkv], axis=1)
    d_lng = jnp.stack([jnp.stack([dg00, dg01]), jnp.stack([dg10, dg11])])
    d_lnb = jnp.stack([jnp.stack([db00, db01]), jnp.stack([db10, db11])])
    small_full = [d_acb, d_ang, d_anb, d_bng, d_bnb, d_ms, d_sb, dsinks[:, :N_Q_HEADS], jnp.concatenate([d_fcb0, d_fcb1], axis=0),
                  d_acw, d_bqkv, jnp.stack([d_fcw0, d_fcw1]), d_lng, d_lnb]
    d_win, (g_small_grads,) = _matmul(xt, dh0, "tn", BF16, "mix_dwin", 1024, 1024, 512, comm=_Comm(gather=[_pack(small_full)]))
    grad_x, (p_win,) = _matmul(dh0, w_in, "nt", F32, "mix_dx", 1024, 1024, 1024, res=dz1, res_scale=ALPHA,
                               comm=_Comm(exchange=[_deinterleave(d_win)]))

    loss = lax.psum(0.5 / D_MODEL * jnp.sum(loss_terms), ("x", "y", "c"))

    big = {}
    for nm, p, w, m, v, tr, transposed in [
            ("ab_w_in", [p_win], ab_w_in, m_ab_w_in, v_ab_w_in, 256, False),
            ("ab_w_out", [p_wout], ab_w_out, m_ab_w_out, v_ab_w_out, 128, False),
            ("c_w_qkv", [p_wqkv], c_w_qkv, m_c_w_qkv, v_c_w_qkv, 160, True), ("c_w_o", [p_wo], c_w_o, m_c_w_o, v_c_w_o, 128, False),
            ("ffn_w_up", [p_wup0, p_wup1], ffn_w_up, m_ffn_w_up, v_ffn_w_up, 176, True),
            ("ffn_w_down", [p_wdown0, p_wdown1], ffn_w_down, m_ffn_w_down, v_ffn_w_down, 176, False)]:
        def two_d(a):
            a = jnp.swapaxes(a, 1, 2) if transposed else a
            return a.reshape(-1, a.shape[-1])

        def back(o):
            return jnp.swapaxes(o.reshape(w.shape[0], w.shape[2], w.shape[1]), 1, 2) if transposed else o.reshape(w.shape)

        outs = _adamw_big(p, two_d(w), two_d(m), two_d(v), "adamw_" + nm, tr)
        big[nm] = [back(o) for o in outs]

    gs = _unpack(g_small_grads, [a.shape for a in small_full])

    def my_shard(g, width):
        g = g.reshape(g.shape[:-1] + (N_DEV, width))
        return lax.dynamic_index_in_dim(g, me, axis=g.ndim - 2, keepdims=False)

    small_names = ["a_conv_b", "a_norm_g", "a_norm_b", "b_norm_g", "b_norm_b", "b_spatial_w", "b_spatial_b", "c_sinks", "ffn_conv_b",
                   "a_conv_w", "c_b_qkv", "ffn_conv_w", "ln_g", "ln_b"]
    small_w = [a_conv_b, a_norm_g, a_norm_b, b_norm_g, b_norm_b, b_spatial_w, b_spatial_b, c_sinks, ffn_conv_b,
               a_conv_w, c_b_qkv, ffn_conv_w, ln_g, ln_b]
    small_m = [m_a_conv_b, m_a_norm_g, m_a_norm_b, m_b_norm_g, m_b_norm_b, m_b_spatial_w, m_b_spatial_b, m_c_sinks, m_ffn_conv_b,
               m_a_conv_w, m_c_b_qkv, m_ffn_conv_w, m_ln_g, m_ln_b]
    small_v = [v_a_conv_b, v_a_norm_g, v_a_norm_b, v_b_norm_g, v_b_norm_b, v_b_spatial_w, v_b_spatial_b, v_c_sinks, v_ffn_conv_b,
               v_a_conv_w, v_c_b_qkv, v_ffn_conv_w, v_ln_g, v_ln_b]
    gs[9:] = [my_shard(g, w.shape[-1]) for g, w in zip(gs[9:], small_w[9:])]
    two_d = [(-1, w.shape[-1]) for w in small_w]
    outs = _adamw_small([g.reshape((N_DEV,) + w.reshape(s).shape) for g, w, s in zip(gs, small_w, two_d)],
                        [w.reshape(s) for w, s in zip(small_w, two_d)], [m.reshape(s) for m, s in zip(small_m, two_d)],
                        [v.reshape(s) for v, s in zip(small_v, two_d)], "adamw_small")
    small = {nm: [o.reshape(w.shape) for o in outs[4 * a:4 * a + 4]] for a, (nm, w) in enumerate(zip(small_names, small_w))}

    res = {**big, **small}
    order = ["ab_w_in", "a_conv_w", "a_conv_b", "a_norm_g", "a_norm_b", "b_norm_g", "b_norm_b", "b_spatial_w", "b_spatial_b", "ab_w_out",
             "c_w_qkv", "c_b_qkv", "c_sinks", "c_w_o", "ffn_w_up", "ffn_conv_w", "ffn_conv_b", "ffn_w_down", "ln_g", "ln_b"]
    return (loss, grad_x[None], *[res[nm][0] for nm in order], *[res[nm][1] for nm in order],
            *[res[nm][2] for nm in order], *[res[nm][3] for nm in order])
```

```python
import functools
import math

import jax
import jax.numpy as jnp
from jax import lax
from jax.experimental import pallas as pl
from jax.experimental.pallas import tpu as pltpu

F32 = jnp.float32
BF16 = jnp.bfloat16

N_DEV = 8
D_MODEL = 1024
A_WIDTH = 512
A_KERNEL = 31
B_GROUPS = 4
B_CHUNK = 128
HEAD_DIM = 64
N_Q_HEADS = 16
N_KV_HEADS = 2
ATT_BLOCK = 128
D_FF = 2816
FFN_KERNEL = 3
ALPHA = (2.0 * 2) ** 0.25
LN_EPS = 1e-5
GELU_K = math.sqrt(2.0 / math.pi)
GELU_C = 0.044715
ADAM_LR = 0.001
ADAM_B1 = 0.9
ADAM_B2 = 0.999
ADAM_EPS = 1e-08
ADAM_WD = 0.01
ADAM_STEP = 10
VMEM_LIMIT = 56 * 1024 * 1024
MESH_ID = pl.DeviceIdType.MESH


def _params(*sem):
    return pltpu.CompilerParams(dimension_semantics=sem, vmem_limit_bytes=VMEM_LIMIT)


def _gelu(x):
    t = jnp.tanh(GELU_K * x * (1.0 + GELU_C * x * x))
    return 0.5 * x * (1.0 + t)


def _gelu_and_grad(x):
    x2 = x * x
    t = jnp.tanh(GELU_K * x * (1.0 + GELU_C * x2))
    g = 0.5 * x * (1.0 + t)
    dg = 0.5 * (1.0 + t) + 0.5 * x * (1.0 - t * t) * (GELU_K * (1.0 + 3.0 * GELU_C * x2))
    return g, dg


def _sigmoid(x):
    return 1.0 / (1.0 + jnp.exp(-x))


def _ln_stats(z):
    mu = jnp.mean(z, axis=-1, keepdims=True)
    zc = z - mu
    var = jnp.mean(zc * zc, axis=-1, keepdims=True)
    r = lax.rsqrt(var + LN_EPS)
    return zc * r, r


def _ln_bwd_rows(dn, nh, r):
    return r * (dn - jnp.mean(dn, axis=-1, keepdims=True) - nh * jnp.mean(dn * nh, axis=-1, keepdims=True))


def _colsum(x):
    return jnp.sum(x, axis=0, keepdims=True)


def _dot(a, b, dims):
    return lax.dot_general(a.astype(BF16), b.astype(BF16), (dims, ((), ())), preferred_element_type=F32)


NN = ((1,), (0,))
NT = ((1,), (1,))
TN = ((0,), (0,))


ANY = pl.BlockSpec(memory_space=pl.ANY)
N_RELATIONS = N_DEV - 1


def _my_place():
    return lax.axis_index("x"), lax.axis_index("y"), lax.axis_index("c")


class _Comm:
    def __init__(self, gather=(), exchange=()):
        exchange = [e if isinstance(e, tuple) else (e, 0, e.shape[1]) for e in exchange]
        self.arrs = list(gather) + [e[0] for e in exchange]
        self.n_gather = len(gather)
        self.n = len(self.arrs)
        self.rows = [None] * self.n_gather + [pl.ds(lo, n) for _, lo, n in exchange]

    def out_shape(self):
        return [jax.ShapeDtypeStruct(((N_DEV,) + a.shape) if i < self.n_gather else a.shape, a.dtype)
                for i, a in enumerate(self.arrs)]

    def sems(self):
        return [pltpu.SemaphoreType.DMA((self.n, N_RELATIONS)), pltpu.SemaphoreType.DMA((self.n, N_RELATIONS)),
                pltpu.SemaphoreType.DMA((self.n,))]

    def _gather_copy(self, ins, outs, sems, a, k, place, to, from_input=False):
        px, py, pc = place
        block = outs[a].at[4 * px + 2 * py + pc]
        return pltpu.make_async_remote_copy(
            src_ref=ins[a] if from_input else block, dst_ref=block,
            send_sem=sems[0].at[a, k], recv_sem=sems[1].at[a, k], device_id=to, device_id_type=MESH_ID)

    def _exchange_copy(self, ins, outs, sems, a, k, landing=False):
        x, y, c = _my_place()
        me = 4 * x + 2 * y + c
        peer = (x ^ (k >> 2), y ^ ((k >> 1) & 1), c ^ (k & 1))
        return pltpu.make_async_remote_copy(
            src_ref=ins[a].at[me ^ k, self.rows[a]], dst_ref=outs[a].at[(me ^ k) if landing else me, self.rows[a]],
            send_sem=sems[0].at[a, k - 1], recv_sem=sems[1].at[a, k - 1], device_id=peer, device_id_type=MESH_ID)

    def _local_copy(self, ins, outs, sems, a):
        x, y, c = _my_place()
        me = 4 * x + 2 * y + c
        if a < self.n_gather:
            return pltpu.make_async_copy(ins[a], outs[a].at[me], sems[2].at[a])
        return pltpu.make_async_copy(ins[a].at[me, self.rows[a]], outs[a].at[me, self.rows[a]], sems[2].at[a])

    def _first_stage(self, ins, outs, sems, a):
        x, y, c = _my_place()
        me = (x, y, c)
        chips = [(1 - x, y), (x, 1 - y), (1 - x, 1 - y)]
        return ([self._gather_copy(ins, outs, sems, a, 0, me, (x, y, 1 - c), from_input=True)]
                + [self._gather_copy(ins, outs, sems, a, 1 + j, me, (*chip, c), from_input=True) for j, chip in enumerate(chips)])

    def start(self, ins, outs, sems):
        for a in range(self.n):
            self._local_copy(ins, outs, sems, a).start()
        for a in range(self.n_gather):
            for cp in self._first_stage(ins, outs, sems, a):
                cp.start()
        for k in range(1, N_DEV):
            for a in range(self.n_gather, self.n):
                self._exchange_copy(ins, outs, sems, a, k).start()

    def finish(self, ins, outs, sems):
        x, y, c = _my_place()
        me, sibling = (x, y, c), (x, y, 1 - c)
        chips = [(1 - x, y), (x, 1 - y), (1 - x, 1 - y)]
        passed = []
        for j, chip in enumerate(chips):
            for a in range(self.n_gather):
                self._gather_copy(ins, outs, sems, a, 1 + j, (*chip, c), me).wait_recv()
                fwd = self._gather_copy(ins, outs, sems, a, 4 + j, (*chip, c), sibling)
                fwd.start()
                passed.append(fwd)
        for a in range(self.n_gather):
            self._gather_copy(ins, outs, sems, a, 0, sibling, me).wait_recv()
            for j, chip in enumerate(chips):
                self._gather_copy(ins, outs, sems, a, 4 + j, (*chip, 1 - c), me).wait_recv()
        for k in range(1, N_DEV):
            for a in range(self.n_gather, self.n):
                self._exchange_copy(ins, outs, sems, a, k, landing=True).wait_recv()
        for a in range(self.n_gather):
            for cp in self._first_stage(ins, outs, sems, a):
                cp.wait_send()
        for cp in passed:
            cp.wait_send()
        for k in range(1, N_DEV):
            for a in range(self.n_gather, self.n):
                self._exchange_copy(ins, outs, sems, a, k).wait_send()
        for a in range(self.n):
            self._local_copy(ins, outs, sems, a).wait()


def _comm_only(comm, name):
    def body(*refs):
        ins, outs, sems = refs[:comm.n], refs[comm.n:2 * comm.n], refs[2 * comm.n:]
        comm.start(ins, outs, sems)
        comm.finish(ins, outs, sems)

    return pl.pallas_call(body, name=name, in_specs=[ANY] * comm.n, out_specs=[ANY] * comm.n,
                          out_shape=comm.out_shape(), scratch_shapes=comm.sems())(*comm.arrs)


def _call(body, *, name, grid, in_specs, out_specs, out_shape, args, sem, scratch_shapes=(), comm=None):
    in_specs, out_specs, out_shape, scratch_shapes = list(in_specs), list(out_specs), list(out_shape), list(scratch_shapes)
    if comm is None:
        outs = pl.pallas_call(body, name=name, grid=grid, in_specs=in_specs, out_specs=out_specs, out_shape=out_shape,
                              scratch_shapes=scratch_shapes, compiler_params=_params(*sem))(*args)
        return list(outs), []
    n_in, n_out, n_scr, nc = len(in_specs), len(out_specs), len(scratch_shapes), comm.n

    def wrapped(*refs):
        ins, refs = refs[:n_in], refs[n_in:]
        c_in, refs = refs[:nc], refs[nc:]
        outs, refs = refs[:n_out], refs[n_out:]
        c_out, refs = refs[:nc], refs[nc:]
        scr, sems = refs[:n_scr], refs[n_scr:]
        first = functools.reduce(jnp.logical_and, [pl.program_id(ax) == 0 for ax in range(len(grid))])
        last = functools.reduce(jnp.logical_and, [pl.program_id(ax) == g - 1 for ax, g in enumerate(grid)])

        @pl.when(first)
        def _():
            comm.start(c_in, c_out, sems)

        body(*ins, *outs, *scr)

        @pl.when(last)
        def _():
            comm.finish(c_in, c_out, sems)

    outs = pl.pallas_call(
        wrapped, name=name, grid=grid, in_specs=in_specs + [ANY] * nc, out_specs=out_specs + [ANY] * nc,
        out_shape=out_shape + comm.out_shape(), scratch_shapes=scratch_shapes + comm.sems(),
        compiler_params=_params(*(["arbitrary"] * len(grid))))(*args, *comm.arrs)
    return list(outs[:n_out]), list(outs[n_out:])


def _matmul(a, b, mode, out_dtype, name, tm, tn, tk, *, bias=None, res=None, res_scale=1.0, b_off=0, comm=None):
    tm = min(tm, a.shape[1] if mode == "tn" else a.shape[0])
    tk = min(tk, a.shape[0] if mode == "tn" else a.shape[1])
    if mode == "nn":
        (m, k), n = a.shape, b.shape[1]
        a_spec = pl.BlockSpec((tm, tk), lambda i, j, kk: (i, kk))
        b_spec = pl.BlockSpec((tk, tn), lambda i, j, kk: (kk + b_off, j))
        dims = NN
    elif mode == "nt":
        (m, k), n = a.shape, b.shape[0]
        a_spec = pl.BlockSpec((tm, tk), lambda i, j, kk: (i, kk))
        b_spec = pl.BlockSpec((tn, tk), lambda i, j, kk: (j, kk + b_off))
        dims = NT
    else:
        (k, m), n = a.shape, b.shape[1]
        a_spec = pl.BlockSpec((tk, tm), lambda i, j, kk: (kk, i))
        b_spec = pl.BlockSpec((tk, tn), lambda i, j, kk: (kk, j))
        dims = TN
    assert m % tm == 0 and n % tn == 0 and k % tk == 0, (name, m, n, k)
    nk = k // tk
    in_specs = [a_spec, b_spec]
    args = [a, b]
    if bias is not None:
        in_specs.append(pl.BlockSpec((1, tn), lambda i, j, kk: (0, j)))
        args.append(bias)
    if res is not None:
        in_specs.append(pl.BlockSpec((tm, tn), lambda i, j, kk: (i, j)))
        args.append(res)

    def finish(out, refs, o_ref):
        pos = 2
        if bias is not None:
            out = out + refs[pos][...]
            pos += 1
        if res is not None:
            out = out + res_scale * refs[pos][...].astype(F32)
        o_ref[...] = out.astype(out_dtype)

    def body_one_step(*refs):
        finish(_dot(refs[0][...], refs[1][...], dims), refs, refs[-1])

    def body(*refs):
        a_ref, b_ref = refs[0], refs[1]
        o_ref, acc = refs[-2], refs[-1]
        kk = pl.program_id(2)

        @pl.when(kk == 0)
        def _():
            acc[...] = jnp.zeros_like(acc)

        acc[...] += _dot(a_ref[...], b_ref[...], dims)

        @pl.when(kk == nk - 1)
        def _():
            finish(acc[...], refs, o_ref)

    (out,), moved = _call(
        body_one_step if nk == 1 else body, name=name, grid=(m // tm, n // tn, nk),
        in_specs=in_specs, out_specs=[pl.BlockSpec((tm, tn), lambda i, j, kk: (i, j))],
        out_shape=[jax.ShapeDtypeStruct((m, n), out_dtype)],
        scratch_shapes=[] if nk == 1 else [pltpu.VMEM((tm, tn), F32)],
        sem=("parallel", "parallel", "arbitrary"), args=args, comm=comm)
    return out if comm is None else (out, moved)


def _matmul_tn_pair(a0, a1, b, out_dtype, name, tm, tn, tk, comm=None):
    (k, m), n = a0.shape, b.shape[1]
    tk = min(tk, k)
    assert a1.shape == a0.shape and m % tm == 0 and n % tn == 0 and k % tk == 0, (name, m, n, k)
    mi, nk = m // tm, k // tk

    def body(a0_ref, a1_ref, b_ref, o_ref, acc):
        i, kk = pl.program_id(0), pl.program_id(2)

        @pl.when(kk == 0)
        def _():
            acc[...] = jnp.zeros_like(acc)

        @pl.when(i < mi)
        def _():
            acc[...] += _dot(a0_ref[...], b_ref[...], TN)

        @pl.when(i >= mi)
        def _():
            acc[...] += _dot(a1_ref[...], b_ref[...], TN)

        @pl.when(kk == nk - 1)
        def _():
            o_ref[...] = acc[...].astype(out_dtype)

    (out,), moved = _call(
        body, name=name, grid=(2 * mi, n // tn, nk),
        in_specs=[pl.BlockSpec((tk, tm), lambda i, j, kk: (jnp.where(i < mi, kk, nk - 1), jnp.minimum(i, mi - 1))),
                  pl.BlockSpec((tk, tm), lambda i, j, kk: (jnp.where(i >= mi, kk, 0), jnp.maximum(i - mi, 0))),
                  pl.BlockSpec((tk, tn), lambda i, j, kk: (kk, j))],
        out_specs=[pl.BlockSpec((tm, tn), lambda i, j, kk: (i, j))],
        out_shape=[jax.ShapeDtypeStruct((2 * m, n), out_dtype)],
        scratch_shapes=[pltpu.VMEM((tm, tn), F32)],
        sem=("parallel", "parallel", "arbitrary"), args=(a0, a1, b), comm=comm)
    return out if comm is None else (out, moved)


def _matmul_res_ln(a, b, xres, g, beta, name, tm, tk, comm=None):
    t, k = a.shape
    d = b.shape[1]
    nk = k // tk
    assert t % tm == 0 and k % tk == 0

    def body(a_ref, b_ref, x_ref, g_ref, beta_ref, z_ref, xo_ref, acc):
        kk = pl.program_id(1)

        @pl.when(kk == 0)
        def _():
            acc[...] = jnp.zeros_like(acc)

        acc[...] += _dot(a_ref[...], b_ref[...], NN)

        @pl.when(kk == nk - 1)
        def _():
            z = ALPHA * x_ref[...] + acc[...]
            nh, _ = _ln_stats(z)
            z_ref[...] = z
            xo_ref[...] = nh * g_ref[...] + beta_ref[...]

    row = pl.BlockSpec((tm, d), lambda i, kk: (i, 0))
    vec = pl.BlockSpec((1, d), lambda i, kk: (0, 0))
    outs, moved = _call(
        body, name=name, grid=(t // tm, nk),
        in_specs=[pl.BlockSpec((tm, tk), lambda i, kk: (i, kk)), pl.BlockSpec((tk, d), lambda i, kk: (kk, 0)), row, vec, vec],
        out_specs=[row, row],
        out_shape=[jax.ShapeDtypeStruct((t, d), F32), jax.ShapeDtypeStruct((t, d), F32)],
        scratch_shapes=[pltpu.VMEM((tm, d), F32)],
        sem=("parallel", "arbitrary"), args=(a, b, xres, g, beta), comm=comm)
    return outs if comm is None else (outs, moved)


def _matmul_ln_bwd(a, b, z, g, dres, name, tm, *, res=None, b_off=0):
    m, k = a.shape
    d = b.shape[1]
    tm = min(tm, m)
    assert m % tm == 0

    def body(*refs):
        a_ref, b_ref, z_ref, g_ref, dres_ref = refs[:5]
        dz_ref, dg_ref, db_ref = refs[-3:]

        @pl.when(pl.program_id(0) == 0)
        def _():
            dg_ref[...] = jnp.zeros_like(dg_ref)
            db_ref[...] = jnp.zeros_like(db_ref)

        dbr = _dot(a_ref[...], b_ref[...], NN)
        if res is not None:
            dbr = dbr + refs[5][...]
        nh, r = _ln_stats(z_ref[...])
        dy = ALPHA * dres_ref[...] + dbr
        dg_ref[...] += _colsum(dy * nh)
        db_ref[...] += _colsum(dy)
        dz_ref[...] = _ln_bwd_rows(dy * g_ref[...], nh, r)

    row = pl.BlockSpec((tm, d), lambda i: (i, 0))
    vec = pl.BlockSpec((1, d), lambda i: (0, 0))
    vshape = jax.ShapeDtypeStruct((1, d), F32)
    return pl.pallas_call(
        body, name=name, grid=(m // tm,),
        in_specs=[pl.BlockSpec((tm, k), lambda i: (i, 0)), pl.BlockSpec((k, d), lambda i: (b_off, 0)), row, vec, row]
        + ([row] if res is not None else []),
        out_specs=[row, vec, vec], out_shape=[jax.ShapeDtypeStruct((m, d), F32), vshape, vshape],
        compiler_params=_params("arbitrary"),
    )(a, b, z, g, dres, *([res] if res is not None else []))


def _matmul_res_ln_loss(a, b, xres, g, beta, target, name, tm):
    t, k = a.shape
    d = b.shape[1]
    tm = min(tm, t)

    def body(a_ref, b_ref, x_ref, g_ref, beta_ref, t_ref, dz_ref, dg_ref, db_ref, loss_ref):
        @pl.when(pl.program_id(0) == 0)
        def _():
            dg_ref[...] = jnp.zeros_like(dg_ref)
            db_ref[...] = jnp.zeros_like(db_ref)
            loss_ref[...] = jnp.zeros_like(loss_ref)

        nh, r = _ln_stats(ALPHA * x_ref[...] + _dot(a_ref[...], b_ref[...], NN))
        err = nh * g_ref[...] + beta_ref[...] - t_ref[...]
        loss_ref[...] += _colsum(err * err)
        dy = err * (1.0 / d)
        dg_ref[...] += _colsum(dy * nh)
        db_ref[...] += _colsum(dy)
        dz_ref[...] = _ln_bwd_rows(dy * g_ref[...], nh, r)

    row = pl.BlockSpec((tm, d), lambda i: (i, 0))
    vec = pl.BlockSpec((1, d), lambda i: (0, 0))
    vshape = jax.ShapeDtypeStruct((1, d), F32)
    return pl.pallas_call(
        body, name=name, grid=(t // tm,),
        in_specs=[pl.BlockSpec((tm, k), lambda i: (i, 0)), pl.BlockSpec((k, d), lambda i: (0, 0)), row, vec, vec, row],
        out_specs=[row, vec, vec, vec],
        out_shape=[jax.ShapeDtypeStruct((t, d), F32), vshape, vshape, vshape],
        compiler_params=_params("arbitrary"),
    )(a, b, xres, g, beta, target)


FFN_HALO = 16
FFN_CHUNK = 256
LANES = 128
SUBLANES = 8


def _rows_up(e, start, rows):
    if start % SUBLANES == 0:
        return e[start:start + rows]
    return pltpu.roll(e, e.shape[0] - start, 0)[0:rows]


def _fold(x):
    return jnp.sum(x.reshape(x.shape[0] // SUBLANES, SUBLANES, x.shape[1]), axis=0)


def _ffn_mid_fwd(h, cw, cb, name, tm=1024, tc=256, comm=None):
    t, f2 = h.shape
    tm = min(tm, t)
    f = f2 // 2
    nj, nt, hb = f // tc, t // tm, tm // FFN_HALO

    ch = min(FFN_CHUNK, tm)

    def body(hg, hgp, hv, hvp, cwg, cwv, cbg, cbv, u_ref, sg, sv):
        i = pl.program_id(1)
        for main, prev, s in ((hg, hgp, sg), (hv, hvp, sv)):
            s[0:FFN_HALO, :] = jnp.where(i > 0, prev[...].astype(F32), 0.0)
            s[FFN_HALO:, :] = main[...].astype(F32)
        o = SUBLANES - FFN_KERNEL + 1
        for lg in range(tc // LANES):
            cols = slice(lg * LANES, (lg + 1) * LANES)
            wg, wv = [cwg[k:k + 1, cols] for k in range(FFN_KERNEL)], [cwv[k:k + 1, cols] for k in range(FFN_KERNEL)]
            bg, bv = cbg[:, cols], cbv[:, cols]

            def chunk(c, carry):
                base = pl.multiple_of(c * ch, ch)
                eg = sg[pl.ds(base + FFN_HALO - SUBLANES, ch + SUBLANES), cols]
                ev = sv[pl.ds(base + FFN_HALO - SUBLANES, ch + SUBLANES), cols]
                cg = wg[0] * _rows_up(eg, o, ch) + wg[1] * _rows_up(eg, o + 1, ch) + wg[2] * _rows_up(eg, o + 2, ch) + bg
                cv = wv[0] * _rows_up(ev, o, ch) + wv[1] * _rows_up(ev, o + 1, ch) + wv[2] * _rows_up(ev, o + 2, ch) + bv
                u_ref[pl.ds(base, ch), cols] = (_gelu(cg) * cv).astype(BF16)
                return carry

            lax.fori_loop(0, tm // ch, chunk, 0)

    def main_spec(off):
        return pl.BlockSpec((tm, tc), lambda j, i: (i, j + off))

    def prev_spec(off):
        return pl.BlockSpec((FFN_HALO, tc), lambda j, i: (jnp.maximum(i * hb - 1, 0), j + off))

    def par_spec(rows, off):
        return pl.BlockSpec((rows, tc), lambda j, i: (0, j + off))

    (u,), moved = _call(
        body, name=name, grid=(nj, nt),
        in_specs=[main_spec(0), prev_spec(0), main_spec(nj), prev_spec(nj),
                  par_spec(FFN_KERNEL, 0), par_spec(FFN_KERNEL, nj), par_spec(1, 0), par_spec(1, nj)],
        out_specs=[pl.BlockSpec((tm, tc), lambda j, i: (i, j))],
        out_shape=[jax.ShapeDtypeStruct((t, f), BF16)],
        scratch_shapes=[pltpu.VMEM((tm + FFN_HALO, tc), F32), pltpu.VMEM((tm + FFN_HALO, tc), F32)],
        sem=("parallel", "arbitrary"), args=(h, h, h, h, cw, cw, cb, cb), comm=comm)
    return u if comm is None else (u, moved)


def _ffn_mid_bwd(h, du, cw, cb, name, tm=1024, tc=256, comm=None):
    t, f2 = h.shape
    tm = min(tm, t)
    f = f2 // 2
    nj, nt, hb = f // tc, t // tm, tm // FFN_HALO

    ch = min(FFN_CHUNK, tm)
    ahead = ch + SUBLANES

    def body(hg, hgp, hgn, hv, hvp, hvn, du_ref, dun_ref, cwg, cwv, cbg, cbv,
             dhg_ref, dhv_ref, dcwg_ref, dcwv_ref, dcbg_ref, dcbv_ref, sg, sv, sdu):
        i = pl.program_id(1)

        @pl.when(i == 0)
        def _():
            for ref in (dcwg_ref, dcwv_ref, dcbg_ref, dcbv_ref):
                ref[...] = jnp.zeros_like(ref)

        for main, prev, nxt, s in ((hg, hgp, hgn, sg), (hv, hvp, hvn, sv)):
            s[0:FFN_HALO, :] = jnp.where(i > 0, prev[...].astype(F32), 0.0)
            s[FFN_HALO:FFN_HALO + tm, :] = main[...].astype(F32)
            s[FFN_HALO + tm:, :] = nxt[...].astype(F32)
        sdu[0:tm, :] = du_ref[...].astype(F32)
        sdu[tm:, :] = jnp.where(i < nt - 1, dun_ref[...].astype(F32), 0.0)
        o = SUBLANES - FFN_KERNEL + 1
        for lg in range(tc // LANES):
            cols = slice(lg * LANES, (lg + 1) * LANES)
            wg, wv = [cwg[k:k + 1, cols] for k in range(FFN_KERNEL)], [cwv[k:k + 1, cols] for k in range(FFN_KERNEL)]
            bg, bv = cbg[:, cols], cbv[:, cols]

            def chunk(c, acc):
                base = pl.multiple_of(c * ch, ch)
                eg = sg[pl.ds(base + FFN_HALO - SUBLANES, ahead + SUBLANES), cols]
                ev = sv[pl.ds(base + FFN_HALO - SUBLANES, ahead + SUBLANES), cols]
                hgs = [_rows_up(eg, o + k, ahead) for k in range(FFN_KERNEL)]
                hvs = [_rows_up(ev, o + k, ahead) for k in range(FFN_KERNEL)]
                cg = wg[0] * hgs[0] + wg[1] * hgs[1] + wg[2] * hgs[2] + bg
                cv = wv[0] * hvs[0] + wv[1] * hvs[1] + wv[2] * hvs[2] + bv
                du_e = sdu[pl.ds(base, ahead), cols]
                gl, dgl = _gelu_and_grad(cg)

                def back(d, hs, w, dh_ref):
                    own = d[0:ch]
                    dh = w[2] * own + w[1] * _rows_up(d, 1, ch) + w[0] * _rows_up(d, 2, ch)
                    dh_ref[pl.ds(base, ch), cols] = dh.astype(BF16)
                    return [_fold(own)] + [_fold(own * hs[k][0:ch]) for k in range(FFN_KERNEL)]

                sums = back(du_e * cv * dgl, hgs, wg, dhg_ref) + back(du_e * gl, hvs, wv, dhv_ref)
                return tuple(a + s_ for a, s_ in zip(acc, sums))

            zero = jnp.zeros((SUBLANES, LANES), F32)
            acc = lax.fori_loop(0, tm // ch, chunk, (zero,) * (2 * (1 + FFN_KERNEL)))
            dcbg_ref[:, cols] += _colsum(acc[0])
            dcbv_ref[:, cols] += _colsum(acc[1 + FFN_KERNEL])
            for k in range(FFN_KERNEL):
                dcwg_ref[k:k + 1, cols] += _colsum(acc[1 + k])
                dcwv_ref[k:k + 1, cols] += _colsum(acc[2 + FFN_KERNEL + k])

    last_blk = t // FFN_HALO - 1

    def main_spec(off):
        return pl.BlockSpec((tm, tc), lambda j, i: (i, j + off))

    def prev_spec(off):
        return pl.BlockSpec((FFN_HALO, tc), lambda j, i: (jnp.maximum(i * hb - 1, 0), j + off))

    def next_spec(off):
        return pl.BlockSpec((FFN_HALO, tc), lambda j, i: (jnp.minimum((i + 1) * hb, last_blk), j + off))

    def par_spec(rows, off):
        return pl.BlockSpec((rows, tc), lambda j, i: (0, j + off))

    out_tile = pl.BlockSpec((tm, tc), lambda j, i: (i, j))
    outs, moved = _call(
        body, name=name, grid=(nj, nt),
        in_specs=[main_spec(0), prev_spec(0), next_spec(0), main_spec(nj), prev_spec(nj), next_spec(nj),
                  main_spec(0), next_spec(0),
                  par_spec(FFN_KERNEL, 0), par_spec(FFN_KERNEL, nj), par_spec(1, 0), par_spec(1, nj)],
        out_specs=[out_tile, out_tile, par_spec(FFN_KERNEL, 0), par_spec(FFN_KERNEL, 0), par_spec(1, 0), par_spec(1, 0)],
        out_shape=[jax.ShapeDtypeStruct((t, f), BF16), jax.ShapeDtypeStruct((t, f), BF16),
                   jax.ShapeDtypeStruct((FFN_KERNEL, f), F32), jax.ShapeDtypeStruct((FFN_KERNEL, f), F32),
                   jax.ShapeDtypeStruct((1, f), F32), jax.ShapeDtypeStruct((1, f), F32)],
        scratch_shapes=[pltpu.VMEM((tm + 2 * FFN_HALO, tc), F32), pltpu.VMEM((tm + 2 * FFN_HALO, tc), F32),
                        pltpu.VMEM((tm + FFN_HALO, tc), F32)],
        sem=("parallel", "arbitrary"), args=(h, h, h, h, h, h, du, du, cw, cw, cb, cb), comm=comm)
    return outs if comm is None else (outs, moved)


MIX_HALO = 32


def _glu(hh):
    return hh[:, 0:A_WIDTH] * _sigmoid(hh[:, A_WIDTH:2 * A_WIDTH])


def _fill_row_shifts(s):
    rows = s.shape[1] - SUBLANES
    for j in range(1, SUBLANES):
        s[j, 0:rows, :] = s[0, pl.ds(j, rows), :]


def _rows_from(s, start, rows):
    j = start % SUBLANES
    return s[j, start - j:start - j + rows, :]


def _tril_mask():
    return lax.broadcasted_iota(jnp.int32, (B_CHUNK, B_CHUNK), 0) >= lax.broadcasted_iota(jnp.int32, (B_CHUNK, B_CHUNK), 1)


def _spatial_mix(q, ms_ref, sbt_ref, tm):
    mask = _tril_mask()
    ws = [jnp.where(mask, ms_ref[g], 0.0).astype(BF16) for g in range(B_GROUPS)]
    qb = q.astype(BF16)
    rows = []
    for c in range(tm // B_CHUNK):
        cols = [_dot(ws[g], qb[c * B_CHUNK:(c + 1) * B_CHUNK, g * 128:(g + 1) * 128], NN) + sbt_ref[:, g:g + 1]
                for g in range(B_GROUPS)]
        rows.append(jnp.concatenate(cols, axis=1))
    return jnp.concatenate(rows, axis=0)


def _mixer_mid_fwd(h, cw, cb, ag, ab, bg, bb, ms, sbt, name, tm=256, comm=None):
    t = h.shape[0]
    nt, hb = t // tm, tm // MIX_HALO
    o = MIX_HALO - A_KERNEL + 1

    def body(h_ref, hp_ref, cw_ref, cb_ref, ag_ref, ab_ref, bg_ref, bb_ref, ms_ref, sbt_ref, cat_ref, sp):
        i = pl.program_id(0)
        sp[0, 0:MIX_HALO, :] = jnp.where(i > 0, _glu(hp_ref[:, 0:2 * A_WIDTH].astype(F32)), 0.0)
        sp[0, MIX_HALO:, :] = _glu(h_ref[:, 0:2 * A_WIDTH].astype(F32))
        _fill_row_shifts(sp)
        y = jnp.zeros((tm, A_WIDTH), F32) + cb_ref[...]
        for k in range(A_KERNEL):
            y = y + cw_ref[k:k + 1, :] * _rows_from(sp, o + k, tm)
        nh, _ = _ln_stats(y)
        ln = nh * ag_ref[...] + ab_ref[...]
        cat_ref[:, 0:A_WIDTH] = (ln * _sigmoid(ln)).astype(BF16)
        u = _gelu(h_ref[:, 1024:1536].astype(F32))
        nb, _ = _ln_stats(_gelu(h_ref[:, 1536:2048].astype(F32)))
        mixed = _spatial_mix(nb * bg_ref[...] + bb_ref[...], ms_ref, sbt_ref, tm)
        cat_ref[:, A_WIDTH:] = (u * mixed).astype(BF16)

    vec = pl.BlockSpec((1, A_WIDTH), lambda i: (0, 0))
    (cat,), moved = _call(
        body, name=name, grid=(nt,),
        in_specs=[pl.BlockSpec((tm, 2048), lambda i: (i, 0)),
                  pl.BlockSpec((MIX_HALO, 2048), lambda i: (jnp.maximum(i * hb - 1, 0), 0)),
                  pl.BlockSpec((A_KERNEL, A_WIDTH), lambda i: (0, 0)), vec, vec, vec, vec, vec,
                  pl.BlockSpec((B_GROUPS, B_CHUNK, B_CHUNK), lambda i: (0, 0, 0)),
                  pl.BlockSpec((B_CHUNK, B_GROUPS), lambda i: (0, 0))],
        out_specs=[pl.BlockSpec((tm, D_MODEL), lambda i: (i, 0))],
        out_shape=[jax.ShapeDtypeStruct((t, D_MODEL), BF16)],
        scratch_shapes=[pltpu.VMEM((SUBLANES, tm + MIX_HALO, A_WIDTH), F32)],
        sem=("parallel",), args=(h, h, cw, cb, ag, ab, bg, bb, ms, sbt), comm=comm)
    return cat if comm is None else (cat, moved)


def _mixer_mid_bwd(h, dcat, cw, cb, ag, ab, bg, bb, ms, mst, sbt, name, tm=256, comm=None):
    t = h.shape[0]
    nt, hb = t // tm, tm // MIX_HALO
    o = MIX_HALO - A_KERNEL + 1
    r = tm + MIX_HALO
    nchunk = tm // B_CHUNK

    def body(h_ref, hp_ref, hn_ref, dc_ref, dcn_ref, cw_ref, cb_ref, ag_ref, ab_ref, bg_ref, bb_ref, ms_ref, mst_ref, sbt_ref,
             dh_ref, dcw_ref, dcb_ref, dag_ref, dab_ref, dbg_ref, dbb_ref, dms_ref, dsb_ref, sp, sdy, sbacc):
        i = pl.program_id(0)

        @pl.when(i == 0)
        def _():
            for ref in (dcw_ref, dcb_ref, dag_ref, dab_ref, dbg_ref, dbb_ref, dms_ref, dsb_ref, sbacc):
                ref[...] = jnp.zeros_like(ref)

        sp[0, 0:MIX_HALO, :] = jnp.where(i > 0, _glu(hp_ref[:, 0:2 * A_WIDTH].astype(F32)), 0.0)
        sp[0, MIX_HALO:MIX_HALO + tm, :] = _glu(h_ref[:, 0:2 * A_WIDTH].astype(F32))
        sp[0, MIX_HALO + tm:, :] = _glu(hn_ref[:, 0:2 * A_WIDTH].astype(F32))
        _fill_row_shifts(sp)
        y = jnp.zeros((r, A_WIDTH), F32) + cb_ref[...]
        for k in range(A_KERNEL):
            y = y + cw_ref[k:k + 1, :] * _rows_from(sp, o + k, r)
        nh, rs = _ln_stats(y)
        ln = nh * ag_ref[...] + ab_ref[...]
        sg = _sigmoid(ln)
        dao = jnp.concatenate([dc_ref[:, 0:A_WIDTH].astype(F32),
                               jnp.where(i < nt - 1, dcn_ref[:, 0:A_WIDTH].astype(F32), 0.0)], axis=0)
        dln = dao * (sg * (1.0 + ln * (1.0 - sg)))
        dag_ref[...] += _colsum(dln[0:tm] * nh[0:tm])
        dab_ref[...] += _colsum(dln[0:tm])
        sdy[0] = _ln_bwd_rows(dln * ag_ref[...], nh, rs)
        _fill_row_shifts(sdy)
        dy_own = sdy[0, 0:tm, :]
        dcb_ref[...] += _colsum(dy_own)
        dp = jnp.zeros((tm, A_WIDTH), F32)
        for k in range(A_KERNEL):
            dcw_ref[k:k + 1, :] += _colsum(dy_own * _rows_from(sp, o + k, tm))
            dp = dp + cw_ref[k:k + 1, :] * _rows_from(sdy, A_KERNEL - 1 - k, tm)
        av = h_ref[:, 0:A_WIDTH].astype(F32)
        s = _sigmoid(h_ref[:, A_WIDTH:2 * A_WIDTH].astype(F32))
        dh_ref[:, 0:A_WIDTH] = (dp * s).astype(BF16)
        dh_ref[:, A_WIDTH:2 * A_WIDTH] = (dp * av * s * (1.0 - s)).astype(BF16)

        u, dgu = _gelu_and_grad(h_ref[:, 1024:1536].astype(F32))
        w, dgw = _gelu_and_grad(h_ref[:, 1536:2048].astype(F32))
        nb, rb = _ln_stats(w)
        q = nb * bg_ref[...] + bb_ref[...]
        mixed = _spatial_mix(q, ms_ref, sbt_ref, tm)
        dbo = dc_ref[:, A_WIDTH:].astype(F32)
        dh_ref[:, 1024:1536] = (dbo * mixed * dgu).astype(BF16)
        dmx = dbo * u
        mask = _tril_mask()
        wst = [jnp.where(mask.T, mst_ref[g], 0.0).astype(BF16) for g in range(B_GROUPS)]
        qb = q.astype(BF16)
        dmb = dmx.astype(BF16)
        rows = []
        for c in range(nchunk):
            cols = []
            for g in range(B_GROUPS):
                rs_, cs_ = slice(c * B_CHUNK, (c + 1) * B_CHUNK), slice(g * 128, (g + 1) * 128)
                sbacc[g] += dmx[rs_, cs_]
                dms_ref[g] += _dot(dmb[rs_, cs_], qb[rs_, cs_], NT)
                cols.append(_dot(wst[g], dmb[rs_, cs_], NN))
            rows.append(jnp.concatenate(cols, axis=1))
        dq = jnp.concatenate(rows, axis=0)
        dbg_ref[...] += _colsum(dq * nb)
        dbb_ref[...] += _colsum(dq)
        dh_ref[:, 1536:2048] = (_ln_bwd_rows(dq * bg_ref[...], nb, rb) * dgw).astype(BF16)

        @pl.when(i == nt - 1)
        def _():
            for g in range(B_GROUPS):
                dms_ref[g] = jnp.where(mask, dms_ref[g], 0.0)
                dsb_ref[g] = jnp.sum(sbacc[g], axis=1, keepdims=True)

    last_blk = t // MIX_HALO - 1
    vec = pl.BlockSpec((1, A_WIDTH), lambda i: (0, 0))
    mat = pl.BlockSpec((B_GROUPS, B_CHUNK, B_CHUNK), lambda i: (0, 0, 0))
    taps = pl.BlockSpec((A_KERNEL, A_WIDTH), lambda i: (0, 0))

    def halo(width, which):
        if which == "prev":
            return pl.BlockSpec((MIX_HALO, width), lambda i: (jnp.maximum(i * hb - 1, 0), 0))
        return pl.BlockSpec((MIX_HALO, width), lambda i: (jnp.minimum((i + 1) * hb, last_blk), 0))

    vshape = jax.ShapeDtypeStruct((1, A_WIDTH), F32)
    outs, moved = _call(
        body, name=name, grid=(nt,),
        in_specs=[pl.BlockSpec((tm, 2048), lambda i: (i, 0)), halo(2048, "prev"), halo(2048, "next"),
                  pl.BlockSpec((tm, D_MODEL), lambda i: (i, 0)), halo(D_MODEL, "next"),
                  taps, vec, vec, vec, vec, vec, mat, mat, pl.BlockSpec((B_CHUNK, B_GROUPS), lambda i: (0, 0))],
        out_specs=[pl.BlockSpec((tm, 2048), lambda i: (i, 0)), taps, vec, vec, vec, vec, vec, mat,
                   pl.BlockSpec((B_GROUPS, B_CHUNK, 1), lambda i: (0, 0, 0))],
        out_shape=[jax.ShapeDtypeStruct((t, 2048), BF16), jax.ShapeDtypeStruct((A_KERNEL, A_WIDTH), F32),
                   vshape, vshape, vshape, vshape, vshape,
                   jax.ShapeDtypeStruct((B_GROUPS, B_CHUNK, B_CHUNK), F32), jax.ShapeDtypeStruct((B_GROUPS, B_CHUNK, 1), F32)],
        scratch_shapes=[pltpu.VMEM((SUBLANES, tm + 2 * MIX_HALO, A_WIDTH), F32), pltpu.VMEM((SUBLANES, r, A_WIDTH), F32),
                        pltpu.VMEM((B_GROUPS, B_CHUNK, B_CHUNK), F32)],
        sem=("arbitrary",), args=(h, h, h, dcat, dcat, cw, cb, ag, ab, bg, bb, ms, mst, sbt), comm=comm)
    return outs if comm is None else (outs, moved)


Q_WIDTH = N_Q_HEADS * HEAD_DIM
KV_WIDTH = 2 * N_KV_HEADS * HEAD_DIM
PAIRS_PER_KV = N_Q_HEADS // N_KV_HEADS // 2
ATT_SCALE = 1.0 / math.sqrt(HEAD_DIM)


def _dup_heads(pair_cols, kv_head):
    lane = lax.broadcasted_iota(jnp.int32, pair_cols.shape, 1)
    rolled = pltpu.roll(pair_cols, HEAD_DIM, 1)
    first = lane < HEAD_DIM
    return jnp.where(first, pair_cols, rolled) if kv_head == 0 else jnp.where(first, rolled, pair_cols)


HEADS_PER_KV = N_Q_HEADS // N_KV_HEADS


def _stack_heads(ref, kh):
    lane = lax.broadcasted_iota(jnp.int32, (ATT_BLOCK, 128), 1)
    rows = []
    for pr in range(PAIRS_PER_KV):
        c0 = (kh * PAIRS_PER_KV + pr) * 128
        pair = ref[:, c0:c0 + 128]
        rows += [jnp.where(lane < HEAD_DIM, pair, jnp.zeros_like(pair)), jnp.where(lane < HEAD_DIM, jnp.zeros_like(pair), pair)]
    return jnp.concatenate(rows, axis=0)


def _unstack_heads(stacked, kh, write):
    lane = lax.broadcasted_iota(jnp.int32, (ATT_BLOCK, 128), 1)
    for pr in range(PAIRS_PER_KV):
        first = stacked[(2 * pr) * ATT_BLOCK:(2 * pr + 1) * ATT_BLOCK]
        second = stacked[(2 * pr + 1) * ATT_BLOCK:(2 * pr + 2) * ATT_BLOCK]
        write((kh * PAIRS_PER_KV + pr) * 128, jnp.where(lane < HEAD_DIM, first, second))


def _sink_row(sink_ref, kh):
    return jnp.concatenate([jnp.full((1, ATT_BLOCK), sink_ref[0, kh * HEADS_PER_KV + h], F32) for h in range(HEADS_PER_KV)], axis=1)


def _att_mask_t(n):
    sj = lax.broadcasted_iota(jnp.int32, (2 * ATT_BLOCK, HEADS_PER_KV * ATT_BLOCK), 0)
    qi = lax.broadcasted_iota(jnp.int32, (2 * ATT_BLOCK, HEADS_PER_KV * ATT_BLOCK), 1) & (ATT_BLOCK - 1)
    diff = qi + ATT_BLOCK - sj
    return (diff >= 0) & (diff < ATT_BLOCK) & ((n > 0) | (sj >= ATT_BLOCK))


def _att_probs_t(q_all, k2, mask_t, sink):
    st = _dot(k2, q_all, NT) * ATT_SCALE
    st = jnp.where(mask_t, st, -jnp.inf)
    m = jnp.maximum(jnp.max(st, axis=0, keepdims=True), sink)
    e = jnp.exp(st - m)
    es = jnp.exp(sink - m)
    inv = 1.0 / (jnp.sum(e, axis=0, keepdims=True) + es)
    return e * inv, es * inv


def _attn_fwd(qkv, sinks, name, comm=None):
    t = qkv.shape[0]
    nb = t // ATT_BLOCK
    kvb = Q_WIDTH // KV_WIDTH

    def body(sink_ref, q_ref, kv_ref, kvp_ref, o_ref):
        n = pl.program_id(0)
        mask_t = _att_mask_t(n)
        kv = jnp.concatenate([kvp_ref[...], kv_ref[...]], axis=0).astype(F32)

        def write(c0, pair):
            o_ref[:, c0:c0 + 128] = pair.astype(BF16)

        for kh in range(N_KV_HEADS):
            k2 = _dup_heads(kv[:, 0:128], kh).astype(BF16)
            v2 = _dup_heads(kv[:, 128:256], kh).astype(BF16)
            pt, _ = _att_probs_t(_stack_heads(q_ref, kh), k2, mask_t, _sink_row(sink_ref, kh))
            _unstack_heads(_dot(v2, pt, TN).T, kh, write)

    (out,), moved = _call(
        body, name=name, grid=(nb,),
        in_specs=[pl.BlockSpec(memory_space=pltpu.SMEM),
                  pl.BlockSpec((ATT_BLOCK, Q_WIDTH), lambda n: (n, 0)),
                  pl.BlockSpec((ATT_BLOCK, KV_WIDTH), lambda n: (n, kvb)),
                  pl.BlockSpec((ATT_BLOCK, KV_WIDTH), lambda n: (jnp.maximum(n - 1, 0), kvb))],
        out_specs=[pl.BlockSpec((ATT_BLOCK, Q_WIDTH), lambda n: (n, 0))],
        out_shape=[jax.ShapeDtypeStruct((t, Q_WIDTH), BF16)],
        sem=("parallel",), args=(sinks, qkv, qkv, qkv), comm=comm)
    return out if comm is None else (out, moved)


def _attn_bwd(qkv, d_o, sinks, name, comm=None):
    t = qkv.shape[0]
    nb = t // ATT_BLOCK
    kvb = Q_WIDTH // KV_WIDTH

    def body(sink_ref, q_ref, kv_ref, kvp_ref, do_ref, dq_ref, dkv_ref, dbq_ref, dbkv_ref, dsink_ref, carry):
        n = pl.program_id(0)

        @pl.when(n == 0)
        def _():
            for ref in (dbq_ref, dbkv_ref, dsink_ref, carry):
                ref[...] = jnp.zeros_like(ref)
            dkv_ref[...] = jnp.zeros_like(dkv_ref)

        @pl.when(n < nb)
        def _():
            mask_t = _att_mask_t(n)
            kv = jnp.concatenate([kvp_ref[...], kv_ref[...]], axis=0).astype(F32)
            lane2 = lax.broadcasted_iota(jnp.int32, (2 * ATT_BLOCK, 128), 1)
            sink_lane = lax.broadcasted_iota(jnp.int32, (1, 128), 1)
            dsink = jnp.zeros((1, 128), F32)
            dk_parts, dv_parts = [], []

            def write(c0, pair):
                dbq_ref[:, c0:c0 + 128] += _colsum(pair)
                dq_ref[:, c0:c0 + 128] = pair.astype(BF16)

            for kh in range(N_KV_HEADS):
                k2 = _dup_heads(kv[:, 0:128], kh).astype(BF16)
                v2 = _dup_heads(kv[:, 128:256], kh).astype(BF16)
                q_all = _stack_heads(q_ref, kh)
                do_all = _stack_heads(do_ref, kh)
                pt, ps = _att_probs_t(q_all, k2, mask_t, _sink_row(sink_ref, kh))
                dpt = _dot(v2, do_all, NT)
                delta = jnp.sum(pt * dpt, axis=0, keepdims=True)
                dst = pt * (dpt - delta) * ATT_SCALE
                psd = ps * delta
                for h in range(HEADS_PER_KV):
                    dsink = dsink + jnp.where(sink_lane == kh * HEADS_PER_KV + h,
                                              -jnp.sum(psd[:, h * ATT_BLOCK:(h + 1) * ATT_BLOCK]), 0.0)
                _unstack_heads(_dot(k2, dst, TN).T, kh, write)
                dk_acc = _dot(dst, q_all, NN)
                dv_acc = _dot(pt, do_all, NN)
                dk_parts.append(dk_acc + pltpu.roll(dk_acc, HEAD_DIM, 1))
                dv_parts.append(dv_acc + pltpu.roll(dv_acc, HEAD_DIM, 1))
            dk = jnp.where(lane2 < HEAD_DIM, dk_parts[0], dk_parts[1])
            dv = jnp.where(lane2 < HEAD_DIM, dv_parts[0], dv_parts[1])
            dkv_new = jnp.concatenate([dk, dv], axis=1)
            done = carry[...] + dkv_new[0:ATT_BLOCK]

            @pl.when(n > 0)
            def _():
                dkv_ref[...] = done.astype(BF16)
                dbkv_ref[...] += _colsum(done)

            carry[...] = dkv_new[ATT_BLOCK:]
            dsink_ref[...] += dsink

        @pl.when(n == nb)
        def _():
            dkv_ref[...] = carry[...].astype(BF16)
            dbkv_ref[...] += _colsum(carry[...])

    def clamp(n):
        return jnp.minimum(n, nb - 1)

    outs, moved = _call(
        body, name=name, grid=(nb + 1,),
        in_specs=[pl.BlockSpec(memory_space=pltpu.SMEM),
                  pl.BlockSpec((ATT_BLOCK, Q_WIDTH), lambda n: (clamp(n), 0)),
                  pl.BlockSpec((ATT_BLOCK, KV_WIDTH), lambda n: (clamp(n), kvb)),
                  pl.BlockSpec((ATT_BLOCK, KV_WIDTH), lambda n: (jnp.maximum(clamp(n) - 1, 0), kvb)),
                  pl.BlockSpec((ATT_BLOCK, Q_WIDTH), lambda n: (clamp(n), 0))],
        out_specs=[pl.BlockSpec((ATT_BLOCK, Q_WIDTH), lambda n: (clamp(n), 0)),
                   pl.BlockSpec((ATT_BLOCK, KV_WIDTH), lambda n: (jnp.maximum(n - 1, 0), 0)),
                   pl.BlockSpec((1, Q_WIDTH), lambda n: (0, 0)),
                   pl.BlockSpec((1, KV_WIDTH), lambda n: (0, 0)),
                   pl.BlockSpec((1, 128), lambda n: (0, 0))],
        out_shape=[jax.ShapeDtypeStruct((t, Q_WIDTH), BF16), jax.ShapeDtypeStruct((t, KV_WIDTH), BF16),
                   jax.ShapeDtypeStruct((1, Q_WIDTH), F32), jax.ShapeDtypeStruct((1, KV_WIDTH), F32),
                   jax.ShapeDtypeStruct((1, 128), F32)],
        scratch_shapes=[pltpu.VMEM((ATT_BLOCK, KV_WIDTH), F32)],
        sem=("arbitrary",), args=(sinks, qkv, qkv, qkv, d_o), comm=comm)
    return outs if comm is None else (outs, moved)


def _adamw_math(g, w, m, v):
    m = ADAM_B1 * m + (1.0 - ADAM_B1) * g
    v = ADAM_B2 * v + (1.0 - ADAM_B2) * (g * g)
    m_hat = m / (1.0 - ADAM_B1 ** ADAM_STEP)
    v_hat = v / (1.0 - ADAM_B2 ** ADAM_STEP)
    delta = -ADAM_LR * (m_hat / (jnp.sqrt(v_hat) + ADAM_EPS) + ADAM_WD * w)
    return delta, m, v


def _sum_partials(p_ref):
    g = p_ref[0].astype(F32)
    for s in range(1, N_DEV):
        g = g + p_ref[s].astype(F32)
    return g


def _adamw_big(parts, w, m, v, name, tr):
    r, c = w.shape
    parts = [p if isinstance(p, tuple) else (p, 0, p.shape[1]) for p in parts]
    tiles = [rows // tr for _, _, rows in parts]
    starts = [sum(tiles[:l]) for l in range(len(parts))]
    assert all(lo % tr == 0 and rows % tr == 0 for _, lo, rows in parts) and sum(tiles) * tr == r

    def body(*refs):
        p_refs, (w_ref, m_ref, v_ref, g_out, d_out, m_out, v_out) = refs[:len(parts)], refs[len(parts):]
        i = pl.program_id(0)
        for l, p_ref in enumerate(p_refs):
            @pl.when((i >= starts[l]) & (i < starts[l] + tiles[l]))
            def _():
                g = _sum_partials(p_ref)
                g_out[...] = g
                d_out[...], m_out[...], v_out[...] = _adamw_math(g, w_ref[...], m_ref[...], v_ref[...])

    def part_spec(l):
        return pl.BlockSpec((N_DEV, tr, c), lambda i: (0, jnp.clip(i - starts[l], 0, tiles[l] - 1) + parts[l][1] // tr, 0))

    tile = pl.BlockSpec((tr, c), lambda i: (i, 0))
    shape = jax.ShapeDtypeStruct((r, c), F32)
    return pl.pallas_call(
        body, name=name, grid=(r // tr,),
        in_specs=[part_spec(l) for l in range(len(parts))] + [tile, tile, tile],
        out_specs=[tile] * 4, out_shape=[shape] * 4,
        compiler_params=_params("parallel"),
    )(*[p[0] for p in parts], w, m, v)


def _adamw_small(parts, ws, ms, vs, name):
    n = len(ws)

    def body(*refs):
        ins, outs = refs[:4 * n], refs[4 * n:]
        for a in range(n):
            g = _sum_partials(ins[a])
            outs[4 * a][...] = g
            outs[4 * a + 1][...], outs[4 * a + 2][...], outs[4 * a + 3][...] = _adamw_math(
                g, ins[n + a][...], ins[2 * n + a][...], ins[3 * n + a][...])

    out_shape = []
    for w in ws:
        out_shape += [jax.ShapeDtypeStruct(w.shape, F32)] * 4
    return pl.pallas_call(body, name=name, out_shape=out_shape, compiler_params=_params())(*parts, *ws, *ms, *vs)


PACK_LANES = 128
PACK_ROWS = 8


def _pack(arrs):
    flat = jnp.concatenate([a.reshape(-1).astype(F32) for a in arrs])
    unit = PACK_LANES * PACK_ROWS
    total = -(-flat.shape[0] // unit) * unit
    return jnp.pad(flat, (0, total - flat.shape[0])).reshape(-1, PACK_LANES)


def _unpack(buf, shapes):
    flat = buf.reshape(N_DEV, -1)
    out, pos = [], 0
    for s in shapes:
        size = math.prod(s)
        out.append(flat[:, pos:pos + size].reshape((N_DEV,) + tuple(s)))
        pos += size
    return out


def _interleave(g):
    return jnp.transpose(g, (1, 0, 2)).reshape(g.shape[1], -1)


def _deinterleave(w):
    r = w.shape[0]
    return jnp.transpose(w.reshape(r, N_DEV, -1), (1, 0, 2))


def _ffn_backward(dz, x_in, z_in, g_in, h, u, w_up_t, cw, cb, w_down, tag, exchange=(), exchange_late=()):
    du = _matmul(dz, w_down, "nt", BF16, f"ffn{tag}_du", 1024, 1408, 1024)
    d_w_down = _matmul(u, dz, "tn", BF16, f"ffn{tag}_dwdown", 1408, 1024, 2048)
    (dhg, dhv, dcwg, dcwv, dcbg, dcbv), moved = _ffn_mid_bwd(
        h, du, cw, cb, f"ffn{tag}_mid_bwd", comm=_Comm(exchange=[d_w_down.reshape(N_DEV, -1, D_MODEL), *exchange]))
    d_w_up_t = _matmul_tn_pair(dhg, dhv, x_in, BF16, f"ffn{tag}_dwup", 1408, 1024, 1024,
                               comm=_Comm(exchange=exchange_late) if exchange_late else None)
    if exchange_late:
        d_w_up_t, late = d_w_up_t
        moved = moved + late
    dx = _matmul(dhv, w_up_t, "nn", F32, f"ffn{tag}_dx_value", 1024, 1024, D_FF, b_off=1)
    dz_in, dg_in, db_in = _matmul_ln_bwd(dhg, w_up_t, z_in, g_in, dz, f"ffn{tag}_dx_gate_ln_bwd", 512, res=dx)
    return (dz_in, dg_in, db_in, d_w_up_t.reshape(N_DEV, -1, D_MODEL),
            jnp.concatenate([dcwg, dcwv], axis=1), jnp.concatenate([dcbg, dcbv], axis=1), moved)


def kernel(x, ab_w_in, a_conv_w, a_conv_b, a_norm_g, a_norm_b, b_norm_g, b_norm_b, b_spatial_w, b_spatial_b, ab_w_out, c_w_qkv, c_b_qkv, c_sinks, c_w_o, ffn_w_up, ffn_conv_w, ffn_conv_b, ffn_w_down, ln_g, ln_b, loss_target, m_ab_w_in, m_a_conv_w, m_a_conv_b, m_a_norm_g, m_a_norm_b, m_b_norm_g, m_b_norm_b, m_b_spatial_w, m_b_spatial_b, m_ab_w_out, m_c_w_qkv, m_c_b_qkv, m_c_sinks, m_c_w_o, m_ffn_w_up, m_ffn_conv_w, m_ffn_conv_b, m_ffn_w_down, m_ln_g, m_ln_b, v_ab_w_in, v_a_conv_w, v_a_conv_b, v_a_norm_g, v_a_norm_b, v_b_norm_g, v_b_norm_b, v_b_spatial_w, v_b_spatial_b, v_ab_w_out, v_c_w_qkv, v_c_b_qkv, v_c_sinks, v_c_w_o, v_ffn_w_up, v_ffn_conv_w, v_ffn_conv_b, v_ffn_w_down, v_ln_g, v_ln_b):
    me = 4 * lax.axis_index("x") + 2 * lax.axis_index("y") + lax.axis_index("c")
    xt = x[0]
    t = xt.shape[0]

    small_shard_shapes = [a_conv_w.shape, c_b_qkv.shape, ffn_conv_w.shape, ln_g.shape, ln_b.shape]
    up_shard = [jnp.swapaxes(ffn_w_up[l], 0, 1).astype(BF16) for l in range(2)]
    qkv_shard = jnp.swapaxes(c_w_qkv[0], 0, 1).astype(BF16)
    down_shard = [ffn_w_down[l].astype(BF16) for l in range(2)]
    g_win, g_small = _comm_only(
        _Comm(gather=[ab_w_in[0].astype(BF16), _pack([a_conv_w, c_b_qkv, ffn_conv_w, ln_g, ln_b])]), "gather_first")
    w_in = _interleave(g_win)
    g_acw, g_bqkv, g_fcw, g_lng, g_lnb = _unpack(g_small, small_shard_shapes)
    acw = _interleave(g_acw[:, 0])
    bqkv = g_bqkv[:, 0].reshape(1, -1)
    fcw = [_interleave(g_fcw[:, l]) for l in range(2)]
    lng = jnp.transpose(g_lng, (1, 2, 0, 3)).reshape(2, 2, 1, D_MODEL)
    lnb = jnp.transpose(g_lnb, (1, 2, 0, 3)).reshape(2, 2, 1, D_MODEL)
    fcb = [ffn_conv_b[l:l + 1] for l in range(2)]
    ms = b_spatial_w[0]
    mst = jnp.swapaxes(ms, 1, 2)
    sbt = b_spatial_b[0].T

    h0, (g_wout,) = _matmul(xt, w_in, "nn", BF16, "mix_in", 1024, 1024, 1024, comm=_Comm(gather=[ab_w_out[0].astype(BF16)]))
    w_out = g_wout.reshape(D_MODEL, D_MODEL)
    cat, (g_wup0,) = _mixer_mid_fwd(h0, acw, a_conv_b, a_norm_g, a_norm_b, b_norm_g, b_norm_b, ms, sbt, "mix_mid_fwd",
                                    comm=_Comm(gather=[up_shard[0]]))
    w_up0 = g_wup0.reshape(2 * D_FF, D_MODEL)
    z1, x1 = _matmul_res_ln(cat, w_out, xt, lng[0, 0], lnb[0, 0], "mix_out_ln", 512, D_MODEL)
    hf0, (g_wdown0, g_wqkv) = _matmul(x1, w_up0, "nt", BF16, "ffn0_up", 1024, 1408, 1024,
                                      comm=_Comm(gather=[down_shard[0], qkv_shard]))
    w_down0 = g_wdown0.reshape(D_FF, D_MODEL)
    w_qkv = g_wqkv.reshape(Q_WIDTH + KV_WIDTH, D_MODEL)
    u0, (g_wup1,) = _ffn_mid_fwd(hf0, fcw[0], fcb[0], "ffn0_mid_fwd", comm=_Comm(gather=[up_shard[1]]))
    w_up1 = g_wup1.reshape(2 * D_FF, D_MODEL)
    (z2, x2), (g_wo,) = _matmul_res_ln(u0, w_down0, x1, lng[0, 1], lnb[0, 1], "ffn0_down_ln", 512, D_FF,
                                       comm=_Comm(gather=[c_w_o[0].astype(BF16)]))
    w_o = g_wo.reshape(D_MODEL, D_MODEL)
    qkv = _matmul(x2, w_qkv, "nt", BF16, "att_qkv", 1024, 1280, 1024, bias=bqkv)
    att, (g_wdown1,) = _attn_fwd(qkv, c_sinks, "att_fwd", comm=_Comm(gather=[down_shard[1]]))
    w_down1 = g_wdown1.reshape(D_FF, D_MODEL)
    z3, x3 = _matmul_res_ln(att, w_o, x2, lng[1, 0], lnb[1, 0], "att_out_ln", 512, D_MODEL)
    hf1 = _matmul(x3, w_up1, "nt", BF16, "ffn1_up", 1024, 1408, 1024)
    u1 = _ffn_mid_fwd(hf1, fcw[1], fcb[1], "ffn1_mid_fwd")

    dz4, dg11, db11, loss_terms = _matmul_res_ln_loss(u1, w_down1, x3, lng[1, 1], lnb[1, 1], loss_target[0],
                                                      "ffn1_down_ln_loss", 512)
    dz3, dg10, db10, d_wup1, d_fcw1, d_fcb1, (p_wdown1,) = _ffn_backward(
        dz4, x3, z3, lng[1, 0], hf1, u1, w_up1, fcw[1], fcb[1], w_down1, 1)
    d_att = _matmul(dz3, w_o, "nt", BF16, "att_dout", 1024, 1024, 1024)
    d_wo = _matmul(att, dz3, "tn", BF16, "att_dwo", 1024, 1024, 512)
    half = d_wup1.shape[1] // 2
    (dq, dkv, dbq, dbkv, dsinks), (p_wup1a,) = _attn_bwd(qkv, d_att, c_sinks, "att_bwd",
                                                        comm=_Comm(exchange=[(d_wup1, 0, half)]))
    d_wqkv = jnp.concatenate([_matmul(dq, x2, "tn", BF16, "att_dwq", 1024, 1024, 1024),
                              _matmul(dkv, x2, "tn", BF16, "att_dwkv", KV_WIDTH, 1024, 1024)], axis=0)
    dx2 = _matmul(dkv, w_qkv, "nn", F32, "att_dx_kv", 1024, 1024, KV_WIDTH, b_off=Q_WIDTH // KV_WIDTH)
    dz2, dg01, db01 = _matmul_ln_bwd(dq, w_qkv, z2, lng[0, 1], dz3, "att_dx_q_ln_bwd", 512, res=dx2)
    dz1, dg00, db00, d_wup0, d_fcw0, d_fcb0, (p_wdown0, p_wup1b, p_wqkv, p_wo) = _ffn_backward(
        dz2, x1, z1, lng[0, 0], hf0, u0, w_up0, fcw[0], fcb[0], w_down0, 0, exchange=[(d_wup1, half, half)],
        exchange_late=[d_wqkv.reshape(N_DEV, -1, D_MODEL), d_wo.reshape(N_DEV, -1, D_MODEL)])
    dcat = _matmul(dz1, w_out, "nt", BF16, "mix_dcat", 1024, 1024, 1024)
    d_wout = _matmul(cat, dz1, "tn", BF16, "mix_dwout", 1024, 1024, 512)
    (dh0, d_acw, d_acb, d_ang, d_anb, d_bng, d_bnb, d_ms, d_sb), (p_wup0,) = _mixer_mid_bwd(
        h0, dcat, acw, a_conv_b, a_norm_g, a_norm_b, b_norm_g, b_norm_b, ms, mst, sbt, "mix_mid_bwd",
        comm=_Comm(exchange=[d_wup0]))
    d_bqkv = jnp.concatenate([dbq, dbkv], axis=1)
    d_lng = jnp.stack([jnp.stack([dg00, dg01]), jnp.stack([dg10, dg11])])
    d_lnb = jnp.stack([jnp.stack([db00, db01]), jnp.stack([db10, db11])])
    small_full = [d_acb, d_ang, d_anb, d_bng, d_bnb, d_ms, d_sb, dsinks[:, :N_Q_HEADS], jnp.concatenate([d_fcb0, d_fcb1], axis=0),
                  d_acw, d_bqkv, jnp.stack([d_fcw0, d_fcw1]), d_lng, d_lnb]
    d_win, (g_small_grads, p_wout) = _matmul(
        xt, dh0, "tn", BF16, "mix_dwin", 1024, 1024, 512,
        comm=_Comm(gather=[_pack(small_full)], exchange=[d_wout.reshape(N_DEV, -1, D_MODEL)]))
    grad_x, (p_win,) = _matmul(dh0, w_in, "nt", F32, "mix_dx", 1024, 1024, 1024, res=dz1, res_scale=ALPHA,
                               comm=_Comm(exchange=[_deinterleave(d_win)]))

    loss = lax.psum(0.5 / D_MODEL * jnp.sum(loss_terms), ("x", "y", "c"))

    big = {}
    for nm, p, w, m, v, tr, transposed in [
            ("ab_w_in", [p_win], ab_w_in, m_ab_w_in, v_ab_w_in, 256, False),
            ("ab_w_out", [p_wout], ab_w_out, m_ab_w_out, v_ab_w_out, 128, False),
            ("c_w_qkv", [p_wqkv], c_w_qkv, m_c_w_qkv, v_c_w_qkv, 160, True), ("c_w_o", [p_wo], c_w_o, m_c_w_o, v_c_w_o, 128, False),
            ("ffn_w_up", [p_wup0, (p_wup1a, 0, half), (p_wup1b, half, half)], ffn_w_up, m_ffn_w_up, v_ffn_w_up, 176, True),
            ("ffn_w_down", [p_wdown0, p_wdown1], ffn_w_down, m_ffn_w_down, v_ffn_w_down, 176, False)]:
        def two_d(a):
            a = jnp.swapaxes(a, 1, 2) if transposed else a
            return a.reshape(-1, a.shape[-1])

        def back(o):
            return jnp.swapaxes(o.reshape(w.shape[0], w.shape[2], w.shape[1]), 1, 2) if transposed else o.reshape(w.shape)

        outs = _adamw_big(p, two_d(w), two_d(m), two_d(v), "adamw_" + nm, tr)
        big[nm] = [back(o) for o in outs]

    gs = _unpack(g_small_grads, [a.shape for a in small_full])

    def my_shard(g, width):
        g = g.reshape(g.shape[:-1] + (N_DEV, width))
        return lax.dynamic_index_in_dim(g, me, axis=g.ndim - 2, keepdims=False)

    small_names = ["a_conv_b", "a_norm_g", "a_norm_b", "b_norm_g", "b_norm_b", "b_spatial_w", "b_spatial_b", "c_sinks", "ffn_conv_b",
                   "a_conv_w", "c_b_qkv", "ffn_conv_w", "ln_g", "ln_b"]
    small_w = [a_conv_b, a_norm_g, a_norm_b, b_norm_g, b_norm_b, b_spatial_w, b_spatial_b, c_sinks, ffn_conv_b,
               a_conv_w, c_b_qkv, ffn_conv_w, ln_g, ln_b]
    small_m = [m_a_conv_b, m_a_norm_g, m_a_norm_b, m_b_norm_g, m_b_norm_b, m_b_spatial_w, m_b_spatial_b, m_c_sinks, m_ffn_conv_b,
               m_a_conv_w, m_c_b_qkv, m_ffn_conv_w, m_ln_g, m_ln_b]
    small_v = [v_a_conv_b, v_a_norm_g, v_a_norm_b, v_b_norm_g, v_b_norm_b, v_b_spatial_w, v_b_spatial_b, v_c_sinks, v_ffn_conv_b,
               v_a_conv_w, v_c_b_qkv, v_ffn_conv_w, v_ln_g, v_ln_b]
    gs[9:] = [my_shard(g, w.shape[-1]) for g, w in zip(gs[9:], small_w[9:])]
    two_d = [(-1, w.shape[-1]) for w in small_w]
    outs = _adamw_small([g.reshape((N_DEV,) + w.reshape(s).shape) for g, w, s in zip(gs, small_w, two_d)],
                        [w.reshape(s) for w, s in zip(small_w, two_d)], [m.reshape(s) for m, s in zip(small_m, two_d)],
                        [v.reshape(s) for v, s in zip(small_v, two_d)], "adamw_small")
    small = {nm: [o.reshape(w.shape) for o in outs[4 * a:4 * a + 4]] for a, (nm, w) in enumerate(zip(small_names, small_w))}

    res = {**big, **small}
    order = ["ab_w_in", "a_conv_w", "a_conv_b", "a_norm_g", "a_norm_b", "b_norm_g", "b_norm_b", "b_spatial_w", "b_spatial_b", "ab_w_out",
             "c_w_qkv", "c_b_qkv", "c_sinks", "c_w_o", "ffn_w_up", "ffn_conv_w", "ffn_conv_b", "ffn_w_down", "ln_g", "ln_b"]
    return (loss, grad_x[None], *[res[nm][0] for nm in order], *[res[nm][1] for nm in order],
            *[res[nm][2] for nm in order], *[res[nm][3] for nm in order])
```

```python
import functools
import math

import jax
import jax.numpy as jnp
from jax import lax
from jax.experimental import pallas as pl
from jax.experimental.pallas import tpu as pltpu

F32 = jnp.float32
BF16 = jnp.bfloat16

N_DEV = 8
D_MODEL = 1024
A_WIDTH = 512
A_KERNEL = 31
B_GROUPS = 4
B_CHUNK = 128
HEAD_DIM = 64
N_Q_HEADS = 16
N_KV_HEADS = 2
ATT_BLOCK = 128
D_FF = 2816
FFN_KERNEL = 3
ALPHA = (2.0 * 2) ** 0.25
LN_EPS = 1e-5
GELU_K = math.sqrt(2.0 / math.pi)
GELU_C = 0.044715
ADAM_LR = 0.001
ADAM_B1 = 0.9
ADAM_B2 = 0.999
ADAM_EPS = 1e-08
ADAM_WD = 0.01
ADAM_STEP = 10
VMEM_LIMIT = 56 * 1024 * 1024
MESH_ID = pl.DeviceIdType.MESH


def _params(*sem):
    return pltpu.CompilerParams(dimension_semantics=sem, vmem_limit_bytes=VMEM_LIMIT)


def _gelu(x):
    t = jnp.tanh(GELU_K * x * (1.0 + GELU_C * x * x))
    return 0.5 * x * (1.0 + t)


def _gelu_and_grad(x):
    x2 = x * x
    t = jnp.tanh(GELU_K * x * (1.0 + GELU_C * x2))
    g = 0.5 * x * (1.0 + t)
    dg = 0.5 * (1.0 + t) + 0.5 * x * (1.0 - t * t) * (GELU_K * (1.0 + 3.0 * GELU_C * x2))
    return g, dg


def _sigmoid(x):
    return 1.0 / (1.0 + jnp.exp(-x))


def _ln_stats(z):
    mu = jnp.mean(z, axis=-1, keepdims=True)
    zc = z - mu
    var = jnp.mean(zc * zc, axis=-1, keepdims=True)
    r = lax.rsqrt(var + LN_EPS)
    return zc * r, r


def _ln_bwd_rows(dn, nh, r):
    return r * (dn - jnp.mean(dn, axis=-1, keepdims=True) - nh * jnp.mean(dn * nh, axis=-1, keepdims=True))


def _colsum(x):
    return jnp.sum(x, axis=0, keepdims=True)


def _dot(a, b, dims):
    return lax.dot_general(a.astype(BF16), b.astype(BF16), (dims, ((), ())), preferred_element_type=F32)


NN = ((1,), (0,))
NT = ((1,), (1,))
TN = ((0,), (0,))


ANY = pl.BlockSpec(memory_space=pl.ANY)
N_RELATIONS = N_DEV - 1


def _my_place():
    return lax.axis_index("x"), lax.axis_index("y"), lax.axis_index("c")


class _Comm:
    def __init__(self, gather=(), exchange=()):
        exchange = [e if isinstance(e, tuple) else (e, 0, e.shape[1]) for e in exchange]
        self.arrs = list(gather) + [e[0] for e in exchange]
        self.n_gather = len(gather)
        self.n = len(self.arrs)
        self.rows = [None] * self.n_gather + [pl.ds(lo, n) for _, lo, n in exchange]

    def out_shape(self):
        return [jax.ShapeDtypeStruct(((N_DEV,) + a.shape) if i < self.n_gather else a.shape, a.dtype)
                for i, a in enumerate(self.arrs)]

    def sems(self):
        return [pltpu.SemaphoreType.DMA((self.n, N_RELATIONS)), pltpu.SemaphoreType.DMA((self.n, N_RELATIONS)),
                pltpu.SemaphoreType.DMA((self.n,))]

    def _gather_copy(self, ins, outs, sems, a, k, place, to, from_input=False):
        px, py, pc = place
        block = outs[a].at[4 * px + 2 * py + pc]
        return pltpu.make_async_remote_copy(
            src_ref=ins[a] if from_input else block, dst_ref=block,
            send_sem=sems[0].at[a, k], recv_sem=sems[1].at[a, k], device_id=to, device_id_type=MESH_ID)

    def _exchange_copy(self, ins, outs, sems, a, k, landing=False):
        x, y, c = _my_place()
        me = 4 * x + 2 * y + c
        peer = (x ^ (k >> 2), y ^ ((k >> 1) & 1), c ^ (k & 1))
        return pltpu.make_async_remote_copy(
            src_ref=ins[a].at[me ^ k, self.rows[a]], dst_ref=outs[a].at[(me ^ k) if landing else me, self.rows[a]],
            send_sem=sems[0].at[a, k - 1], recv_sem=sems[1].at[a, k - 1], device_id=peer, device_id_type=MESH_ID)

    def _local_copy(self, ins, outs, sems, a):
        x, y, c = _my_place()
        me = 4 * x + 2 * y + c
        if a < self.n_gather:
            return pltpu.make_async_copy(ins[a], outs[a].at[me], sems[2].at[a])
        return pltpu.make_async_copy(ins[a].at[me, self.rows[a]], outs[a].at[me, self.rows[a]], sems[2].at[a])

    def _first_stage(self, ins, outs, sems, a):
        x, y, c = _my_place()
        me = (x, y, c)
        chips = [(1 - x, y), (x, 1 - y), (1 - x, 1 - y)]
        return ([self._gather_copy(ins, outs, sems, a, 0, me, (x, y, 1 - c), from_input=True)]
                + [self._gather_copy(ins, outs, sems, a, 1 + j, me, (*chip, c), from_input=True) for j, chip in enumerate(chips)])

    def start(self, ins, outs, sems):
        for a in range(self.n):
            self._local_copy(ins, outs, sems, a).start()
        for a in range(self.n_gather):
            for cp in self._first_stage(ins, outs, sems, a):
                cp.start()
        for k in range(1, N_DEV):
            for a in range(self.n_gather, self.n):
                self._exchange_copy(ins, outs, sems, a, k).start()

    def finish(self, ins, outs, sems):
        x, y, c = _my_place()
        me, sibling = (x, y, c), (x, y, 1 - c)
        chips = [(1 - x, y), (x, 1 - y), (1 - x, 1 - y)]
        passed = []
        for j, chip in enumerate(chips):
            for a in range(self.n_gather):
                self._gather_copy(ins, outs, sems, a, 1 + j, (*chip, c), me).wait_recv()
                fwd = self._gather_copy(ins, outs, sems, a, 4 + j, (*chip, c), sibling)
                fwd.start()
                passed.append(fwd)
        for a in range(self.n_gather):
            self._gather_copy(ins, outs, sems, a, 0, sibling, me).wait_recv()
            for j, chip in enumerate(chips):
                self._gather_copy(ins, outs, sems, a, 4 + j, (*chip, 1 - c), me).wait_recv()
        for k in range(1, N_DEV):
            for a in range(self.n_gather, self.n):
                self._exchange_copy(ins, outs, sems, a, k, landing=True).wait_recv()
        for a in range(self.n_gather):
            for cp in self._first_stage(ins, outs, sems, a):
                cp.wait_send()
        for cp in passed:
            cp.wait_send()
        for k in range(1, N_DEV):
            for a in range(self.n_gather, self.n):
                self._exchange_copy(ins, outs, sems, a, k).wait_send()
        for a in range(self.n):
            self._local_copy(ins, outs, sems, a).wait()


def _comm_only(comm, name):
    def body(*refs):
        ins, outs, sems = refs[:comm.n], refs[comm.n:2 * comm.n], refs[2 * comm.n:]
        comm.start(ins, outs, sems)
        comm.finish(ins, outs, sems)

    return pl.pallas_call(body, name=name, in_specs=[ANY] * comm.n, out_specs=[ANY] * comm.n,
                          out_shape=comm.out_shape(), scratch_shapes=comm.sems())(*comm.arrs)


def _call(body, *, name, grid, in_specs, out_specs, out_shape, args, sem, scratch_shapes=(), comm=None):
    in_specs, out_specs, out_shape, scratch_shapes = list(in_specs), list(out_specs), list(out_shape), list(scratch_shapes)
    if comm is None:
        outs = pl.pallas_call(body, name=name, grid=grid, in_specs=in_specs, out_specs=out_specs, out_shape=out_shape,
                              scratch_shapes=scratch_shapes, compiler_params=_params(*sem))(*args)
        return list(outs), []
    n_in, n_out, n_scr, nc = len(in_specs), len(out_specs), len(scratch_shapes), comm.n

    def wrapped(*refs):
        ins, refs = refs[:n_in], refs[n_in:]
        c_in, refs = refs[:nc], refs[nc:]
        outs, refs = refs[:n_out], refs[n_out:]
        c_out, refs = refs[:nc], refs[nc:]
        scr, sems = refs[:n_scr], refs[n_scr:]
        first = functools.reduce(jnp.logical_and, [pl.program_id(ax) == 0 for ax in range(len(grid))])
        last = functools.reduce(jnp.logical_and, [pl.program_id(ax) == g - 1 for ax, g in enumerate(grid)])

        @pl.when(first)
        def _():
            comm.start(c_in, c_out, sems)

        body(*ins, *outs, *scr)

        @pl.when(last)
        def _():
            comm.finish(c_in, c_out, sems)

    outs = pl.pallas_call(
        wrapped, name=name, grid=grid, in_specs=in_specs + [ANY] * nc, out_specs=out_specs + [ANY] * nc,
        out_shape=out_shape + comm.out_shape(), scratch_shapes=scratch_shapes + comm.sems(),
        compiler_params=_params(*(["arbitrary"] * len(grid))))(*args, *comm.arrs)
    return list(outs[:n_out]), list(outs[n_out:])


def _matmul(a, b, mode, out_dtype, name, tm, tn, tk, *, bias=None, res=None, res_scale=1.0, b_off=0, comm=None):
    tm = min(tm, a.shape[1] if mode == "tn" else a.shape[0])
    tk = min(tk, a.shape[0] if mode == "tn" else a.shape[1])
    if mode == "nn":
        (m, k), n = a.shape, b.shape[1]
        a_spec = pl.BlockSpec((tm, tk), lambda i, j, kk: (i, kk))
        b_spec = pl.BlockSpec((tk, tn), lambda i, j, kk: (kk + b_off, j))
        dims = NN
    elif mode == "nt":
        (m, k), n = a.shape, b.shape[0]
        a_spec = pl.BlockSpec((tm, tk), lambda i, j, kk: (i, kk))
        b_spec = pl.BlockSpec((tn, tk), lambda i, j, kk: (j, kk + b_off))
        dims = NT
    else:
        (k, m), n = a.shape, b.shape[1]
        a_spec = pl.BlockSpec((tk, tm), lambda i, j, kk: (kk, i))
        b_spec = pl.BlockSpec((tk, tn), lambda i, j, kk: (kk, j))
        dims = TN
    assert m % tm == 0 and n % tn == 0 and k % tk == 0, (name, m, n, k)
    nk = k // tk
    in_specs = [a_spec, b_spec]
    args = [a, b]
    if bias is not None:
        in_specs.append(pl.BlockSpec((1, tn), lambda i, j, kk: (0, j)))
        args.append(bias)
    if res is not None:
        in_specs.append(pl.BlockSpec((tm, tn), lambda i, j, kk: (i, j)))
        args.append(res)

    def finish(out, refs, o_ref):
        pos = 2
        if bias is not None:
            out = out + refs[pos][...]
            pos += 1
        if res is not None:
            out = out + res_scale * refs[pos][...].astype(F32)
        o_ref[...] = out.astype(out_dtype)

    def body_one_step(*refs):
        finish(_dot(refs[0][...], refs[1][...], dims), refs, refs[-1])

    def body(*refs):
        a_ref, b_ref = refs[0], refs[1]
        o_ref, acc = refs[-2], refs[-1]
        kk = pl.program_id(2)

        @pl.when(kk == 0)
        def _():
            acc[...] = jnp.zeros_like(acc)

        acc[...] += _dot(a_ref[...], b_ref[...], dims)

        @pl.when(kk == nk - 1)
        def _():
            finish(acc[...], refs, o_ref)

    (out,), moved = _call(
        body_one_step if nk == 1 else body, name=name, grid=(m // tm, n // tn, nk),
        in_specs=in_specs, out_specs=[pl.BlockSpec((tm, tn), lambda i, j, kk: (i, j))],
        out_shape=[jax.ShapeDtypeStruct((m, n), out_dtype)],
        scratch_shapes=[] if nk == 1 else [pltpu.VMEM((tm, tn), F32)],
        sem=("parallel", "parallel", "arbitrary"), args=args, comm=comm)
    return out if comm is None else (out, moved)


def _matmul_tn_pair(a0, a1, b, out_dtype, name, tm, tn, tk, comm=None):
    (k, m), n = a0.shape, b.shape[1]
    tk = min(tk, k)
    assert a1.shape == a0.shape and m % tm == 0 and n % tn == 0 and k % tk == 0, (name, m, n, k)
    mi, nk = m // tm, k // tk

    def body(a0_ref, a1_ref, b_ref, o_ref, acc):
        i, kk = pl.program_id(0), pl.program_id(2)

        @pl.when(kk == 0)
        def _():
            acc[...] = jnp.zeros_like(acc)

        @pl.when(i < mi)
        def _():
            acc[...] += _dot(a0_ref[...], b_ref[...], TN)

        @pl.when(i >= mi)
        def _():
            acc[...] += _dot(a1_ref[...], b_ref[...], TN)

        @pl.when(kk == nk - 1)
        def _():
            o_ref[...] = acc[...].astype(out_dtype)

    (out,), moved = _call(
        body, name=name, grid=(2 * mi, n // tn, nk),
        in_specs=[pl.BlockSpec((tk, tm), lambda i, j, kk: (jnp.where(i < mi, kk, nk - 1), jnp.minimum(i, mi - 1))),
                  pl.BlockSpec((tk, tm), lambda i, j, kk: (jnp.where(i >= mi, kk, 0), jnp.maximum(i - mi, 0))),
                  pl.BlockSpec((tk, tn), lambda i, j, kk: (kk, j))],
        out_specs=[pl.BlockSpec((tm, tn), lambda i, j, kk: (i, j))],
        out_shape=[jax.ShapeDtypeStruct((2 * m, n), out_dtype)],
        scratch_shapes=[pltpu.VMEM((tm, tn), F32)],
        sem=("parallel", "parallel", "arbitrary"), args=(a0, a1, b), comm=comm)
    return out if comm is None else (out, moved)


ROW_SPLIT = 4


def _matmul_res_ln(a, b, xres, g, beta, name, tm, comm=None):
    t, k = a.shape
    d = b.shape[1]
    tm = min(tm, t)
    assert t % tm == 0 and tm % ROW_SPLIT == 0
    sub = tm // ROW_SPLIT

    def body(a_ref, b_ref, x_ref, g_ref, beta_ref, z_ref, xo_ref):
        for s in range(ROW_SPLIT):
            rows = slice(s * sub, (s + 1) * sub)
            z = ALPHA * x_ref[rows, :] + _dot(a_ref[rows, :], b_ref[...], NN)
            nh, _ = _ln_stats(z)
            z_ref[rows, :] = z
            xo_ref[rows, :] = nh * g_ref[...] + beta_ref[...]

    row = pl.BlockSpec((tm, d), lambda i: (i, 0))
    vec = pl.BlockSpec((1, d), lambda i: (0, 0))
    outs, moved = _call(
        body, name=name, grid=(t // tm,),
        in_specs=[pl.BlockSpec((tm, k), lambda i: (i, 0)), pl.BlockSpec((k, d), lambda i: (0, 0)), row, vec, vec],
        out_specs=[row, row],
        out_shape=[jax.ShapeDtypeStruct((t, d), F32), jax.ShapeDtypeStruct((t, d), F32)],
        sem=("parallel",), args=(a, b, xres, g, beta), comm=comm)
    return outs if comm is None else (outs, moved)


def _matmul_ln_bwd(a, b, z, g, dres, name, tm, *, res=None, b_off=0):
    m, k = a.shape
    d = b.shape[1]
    tm = min(tm, m)
    assert m % tm == 0 and tm % ROW_SPLIT == 0
    sub = tm // ROW_SPLIT

    def body(*refs):
        a_ref, b_ref, z_ref, g_ref, dres_ref = refs[:5]
        dz_ref, dg_ref, db_ref = refs[-3:]

        @pl.when(pl.program_id(0) == 0)
        def _():
            dg_ref[...] = jnp.zeros_like(dg_ref)
            db_ref[...] = jnp.zeros_like(db_ref)

        for s in range(ROW_SPLIT):
            rows = slice(s * sub, (s + 1) * sub)
            dbr = _dot(a_ref[rows, :], b_ref[...], NN)
            if res is not None:
                dbr = dbr + refs[5][rows, :]
            nh, r = _ln_stats(z_ref[rows, :])
            dy = ALPHA * dres_ref[rows, :] + dbr
            dg_ref[...] += _colsum(dy * nh)
            db_ref[...] += _colsum(dy)
            dz_ref[rows, :] = _ln_bwd_rows(dy * g_ref[...], nh, r)

    row = pl.BlockSpec((tm, d), lambda i: (i, 0))
    vec = pl.BlockSpec((1, d), lambda i: (0, 0))
    vshape = jax.ShapeDtypeStruct((1, d), F32)
    return pl.pallas_call(
        body, name=name, grid=(m // tm,),
        in_specs=[pl.BlockSpec((tm, k), lambda i: (i, 0)), pl.BlockSpec((k, d), lambda i: (b_off, 0)), row, vec, row]
        + ([row] if res is not None else []),
        out_specs=[row, vec, vec], out_shape=[jax.ShapeDtypeStruct((m, d), F32), vshape, vshape],
        compiler_params=_params("arbitrary"),
    )(a, b, z, g, dres, *([res] if res is not None else []))


def _matmul_res_ln_loss(a, b, xres, g, beta, target, name, tm):
    t, k = a.shape
    d = b.shape[1]
    tm = min(tm, t)
    sub = tm // ROW_SPLIT

    def body(a_ref, b_ref, x_ref, g_ref, beta_ref, t_ref, dz_ref, dg_ref, db_ref, loss_ref):
        @pl.when(pl.program_id(0) == 0)
        def _():
            dg_ref[...] = jnp.zeros_like(dg_ref)
            db_ref[...] = jnp.zeros_like(db_ref)
            loss_ref[...] = jnp.zeros_like(loss_ref)

        for s in range(ROW_SPLIT):
            rows = slice(s * sub, (s + 1) * sub)
            nh, r = _ln_stats(ALPHA * x_ref[rows, :] + _dot(a_ref[rows, :], b_ref[...], NN))
            err = nh * g_ref[...] + beta_ref[...] - t_ref[rows, :]
            loss_ref[...] += _colsum(err * err)
            dy = err * (1.0 / d)
            dg_ref[...] += _colsum(dy * nh)
            db_ref[...] += _colsum(dy)
            dz_ref[rows, :] = _ln_bwd_rows(dy * g_ref[...], nh, r)

    row = pl.BlockSpec((tm, d), lambda i: (i, 0))
    vec = pl.BlockSpec((1, d), lambda i: (0, 0))
    vshape = jax.ShapeDtypeStruct((1, d), F32)
    return pl.pallas_call(
        body, name=name, grid=(t // tm,),
        in_specs=[pl.BlockSpec((tm, k), lambda i: (i, 0)), pl.BlockSpec((k, d), lambda i: (0, 0)), row, vec, vec, row],
        out_specs=[row, vec, vec, vec],
        out_shape=[jax.ShapeDtypeStruct((t, d), F32), vshape, vshape, vshape],
        compiler_params=_params("arbitrary"),
    )(a, b, xres, g, beta, target)


FFN_HALO = 16
FFN_CHUNK = 256
LANES = 128
SUBLANES = 8


def _rows_up(e, start, rows):
    if start % SUBLANES == 0:
        return e[start:start + rows]
    return pltpu.roll(e, e.shape[0] - start, 0)[0:rows]


def _fold(x):
    return jnp.sum(x.reshape(x.shape[0] // SUBLANES, SUBLANES, x.shape[1]), axis=0)


def _ffn_mid_fwd(h, cw, cb, name, tm=1024, tc=256, comm=None):
    t, f2 = h.shape
    tm = min(tm, t)
    f = f2 // 2
    nj, nt, hb = f // tc, t // tm, tm // FFN_HALO

    ch = min(FFN_CHUNK, tm)

    def body(hg, hgp, hv, hvp, cwg, cwv, cbg, cbv, u_ref, sg, sv):
        i = pl.program_id(1)
        for main, prev, s in ((hg, hgp, sg), (hv, hvp, sv)):
            s[0:FFN_HALO, :] = jnp.where(i > 0, prev[...].astype(F32), 0.0)
            s[FFN_HALO:, :] = main[...].astype(F32)
        o = SUBLANES - FFN_KERNEL + 1
        for lg in range(tc // LANES):
            cols = slice(lg * LANES, (lg + 1) * LANES)
            wg, wv = [cwg[k:k + 1, cols] for k in range(FFN_KERNEL)], [cwv[k:k + 1, cols] for k in range(FFN_KERNEL)]
            bg, bv = cbg[:, cols], cbv[:, cols]

            def chunk(c, carry):
                base = pl.multiple_of(c * ch, ch)
                eg = sg[pl.ds(base + FFN_HALO - SUBLANES, ch + SUBLANES), cols]
                ev = sv[pl.ds(base + FFN_HALO - SUBLANES, ch + SUBLANES), cols]
                cg = wg[0] * _rows_up(eg, o, ch) + wg[1] * _rows_up(eg, o + 1, ch) + wg[2] * _rows_up(eg, o + 2, ch) + bg
                cv = wv[0] * _rows_up(ev, o, ch) + wv[1] * _rows_up(ev, o + 1, ch) + wv[2] * _rows_up(ev, o + 2, ch) + bv
                u_ref[pl.ds(base, ch), cols] = (_gelu(cg) * cv).astype(BF16)
                return carry

            lax.fori_loop(0, tm // ch, chunk, 0)

    def main_spec(off):
        return pl.BlockSpec((tm, tc), lambda j, i: (i, j + off))

    def prev_spec(off):
        return pl.BlockSpec((FFN_HALO, tc), lambda j, i: (jnp.maximum(i * hb - 1, 0), j + off))

    def par_spec(rows, off):
        return pl.BlockSpec((rows, tc), lambda j, i: (0, j + off))

    (u,), moved = _call(
        body, name=name, grid=(nj, nt),
        in_specs=[main_spec(0), prev_spec(0), main_spec(nj), prev_spec(nj),
                  par_spec(FFN_KERNEL, 0), par_spec(FFN_KERNEL, nj), par_spec(1, 0), par_spec(1, nj)],
        out_specs=[pl.BlockSpec((tm, tc), lambda j, i: (i, j))],
        out_shape=[jax.ShapeDtypeStruct((t, f), BF16)],
        scratch_shapes=[pltpu.VMEM((tm + FFN_HALO, tc), F32), pltpu.VMEM((tm + FFN_HALO, tc), F32)],
        sem=("parallel", "arbitrary"), args=(h, h, h, h, cw, cw, cb, cb), comm=comm)
    return u if comm is None else (u, moved)


def _ffn_mid_bwd(h, du, cw, cb, name, tm=1024, tc=256, comm=None):
    t, f2 = h.shape
    tm = min(tm, t)
    f = f2 // 2
    nj, nt, hb = f // tc, t // tm, tm // FFN_HALO

    ch = min(FFN_CHUNK, tm)
    ahead = ch + SUBLANES

    def body(hg, hgp, hgn, hv, hvp, hvn, du_ref, dun_ref, cwg, cwv, cbg, cbv,
             dhg_ref, dhv_ref, dcwg_ref, dcwv_ref, dcbg_ref, dcbv_ref, sg, sv, sdu):
        i = pl.program_id(1)

        @pl.when(i == 0)
        def _():
            for ref in (dcwg_ref, dcwv_ref, dcbg_ref, dcbv_ref):
                ref[...] = jnp.zeros_like(ref)

        for main, prev, nxt, s in ((hg, hgp, hgn, sg), (hv, hvp, hvn, sv)):
            s[0:FFN_HALO, :] = jnp.where(i > 0, prev[...].astype(F32), 0.0)
            s[FFN_HALO:FFN_HALO + tm, :] = main[...].astype(F32)
            s[FFN_HALO + tm:, :] = nxt[...].astype(F32)
        sdu[0:tm, :] = du_ref[...].astype(F32)
        sdu[tm:, :] = jnp.where(i < nt - 1, dun_ref[...].astype(F32), 0.0)
        o = SUBLANES - FFN_KERNEL + 1
        for lg in range(tc // LANES):
            cols = slice(lg * LANES, (lg + 1) * LANES)
            wg, wv = [cwg[k:k + 1, cols] for k in range(FFN_KERNEL)], [cwv[k:k + 1, cols] for k in range(FFN_KERNEL)]
            bg, bv = cbg[:, cols], cbv[:, cols]

            def chunk(c, acc):
                base = pl.multiple_of(c * ch, ch)
                eg = sg[pl.ds(base + FFN_HALO - SUBLANES, ahead + SUBLANES), cols]
                ev = sv[pl.ds(base + FFN_HALO - SUBLANES, ahead + SUBLANES), cols]
                hgs = [_rows_up(eg, o + k, ahead) for k in range(FFN_KERNEL)]
                hvs = [_rows_up(ev, o + k, ahead) for k in range(FFN_KERNEL)]
                cg = wg[0] * hgs[0] + wg[1] * hgs[1] + wg[2] * hgs[2] + bg
                cv = wv[0] * hvs[0] + wv[1] * hvs[1] + wv[2] * hvs[2] + bv
                du_e = sdu[pl.ds(base, ahead), cols]
                gl, dgl = _gelu_and_grad(cg)

                def back(d, hs, w, dh_ref):
                    own = d[0:ch]
                    dh = w[2] * own + w[1] * _rows_up(d, 1, ch) + w[0] * _rows_up(d, 2, ch)
                    dh_ref[pl.ds(base, ch), cols] = dh.astype(BF16)
                    return [_fold(own)] + [_fold(own * hs[k][0:ch]) for k in range(FFN_KERNEL)]

                sums = back(du_e * cv * dgl, hgs, wg, dhg_ref) + back(du_e * gl, hvs, wv, dhv_ref)
                return tuple(a + s_ for a, s_ in zip(acc, sums))

            zero = jnp.zeros((SUBLANES, LANES), F32)
            acc = lax.fori_loop(0, tm // ch, chunk, (zero,) * (2 * (1 + FFN_KERNEL)))
            dcbg_ref[:, cols] += _colsum(acc[0])
            dcbv_ref[:, cols] += _colsum(acc[1 + FFN_KERNEL])
            for k in range(FFN_KERNEL):
                dcwg_ref[k:k + 1, cols] += _colsum(acc[1 + k])
                dcwv_ref[k:k + 1, cols] += _colsum(acc[2 + FFN_KERNEL + k])

    last_blk = t // FFN_HALO - 1

    def main_spec(off):
        return pl.BlockSpec((tm, tc), lambda j, i: (i, j + off))

    def prev_spec(off):
        return pl.BlockSpec((FFN_HALO, tc), lambda j, i: (jnp.maximum(i * hb - 1, 0), j + off))

    def next_spec(off):
        return pl.BlockSpec((FFN_HALO, tc), lambda j, i: (jnp.minimum((i + 1) * hb, last_blk), j + off))

    def par_spec(rows, off):
        return pl.BlockSpec((rows, tc), lambda j, i: (0, j + off))

    out_tile = pl.BlockSpec((tm, tc), lambda j, i: (i, j))
    outs, moved = _call(
        body, name=name, grid=(nj, nt),
        in_specs=[main_spec(0), prev_spec(0), next_spec(0), main_spec(nj), prev_spec(nj), next_spec(nj),
                  main_spec(0), next_spec(0),
                  par_spec(FFN_KERNEL, 0), par_spec(FFN_KERNEL, nj), par_spec(1, 0), par_spec(1, nj)],
        out_specs=[out_tile, out_tile, par_spec(FFN_KERNEL, 0), par_spec(FFN_KERNEL, 0), par_spec(1, 0), par_spec(1, 0)],
        out_shape=[jax.ShapeDtypeStruct((t, f), BF16), jax.ShapeDtypeStruct((t, f), BF16),
                   jax.ShapeDtypeStruct((FFN_KERNEL, f), F32), jax.ShapeDtypeStruct((FFN_KERNEL, f), F32),
                   jax.ShapeDtypeStruct((1, f), F32), jax.ShapeDtypeStruct((1, f), F32)],
        scratch_shapes=[pltpu.VMEM((tm + 2 * FFN_HALO, tc), F32), pltpu.VMEM((tm + 2 * FFN_HALO, tc), F32),
                        pltpu.VMEM((tm + FFN_HALO, tc), F32)],
        sem=("parallel", "arbitrary"), args=(h, h, h, h, h, h, du, du, cw, cw, cb, cb), comm=comm)
    return outs if comm is None else (outs, moved)


MIX_HALO = 32


def _glu(hh):
    return hh[:, 0:A_WIDTH] * _sigmoid(hh[:, A_WIDTH:2 * A_WIDTH])


def _fill_row_shifts(s):
    rows = s.shape[1] - SUBLANES
    for j in range(1, SUBLANES):
        s[j, 0:rows, :] = s[0, pl.ds(j, rows), :]


def _rows_from(s, start, rows):
    j = start % SUBLANES
    return s[j, start - j:start - j + rows, :]


def _tril_mask():
    return lax.broadcasted_iota(jnp.int32, (B_CHUNK, B_CHUNK), 0) >= lax.broadcasted_iota(jnp.int32, (B_CHUNK, B_CHUNK), 1)


def _spatial_mix(q, ms_ref, sbt_ref, tm):
    mask = _tril_mask()
    ws = [jnp.where(mask, ms_ref[g], 0.0).astype(BF16) for g in range(B_GROUPS)]
    qb = q.astype(BF16)
    rows = []
    for c in range(tm // B_CHUNK):
        cols = [_dot(ws[g], qb[c * B_CHUNK:(c + 1) * B_CHUNK, g * 128:(g + 1) * 128], NN) + sbt_ref[:, g:g + 1]
                for g in range(B_GROUPS)]
        rows.append(jnp.concatenate(cols, axis=1))
    return jnp.concatenate(rows, axis=0)


def _mixer_mid_fwd(h, cw, cb, ag, ab, bg, bb, ms, sbt, name, tm=256, comm=None):
    t = h.shape[0]
    nt, hb = t // tm, tm // MIX_HALO
    o = MIX_HALO - A_KERNEL + 1

    def body(h_ref, hp_ref, cw_ref, cb_ref, ag_ref, ab_ref, bg_ref, bb_ref, ms_ref, sbt_ref, cat_ref, sp):
        i = pl.program_id(0)
        sp[0, 0:MIX_HALO, :] = jnp.where(i > 0, _glu(hp_ref[:, 0:2 * A_WIDTH].astype(F32)), 0.0)
        sp[0, MIX_HALO:, :] = _glu(h_ref[:, 0:2 * A_WIDTH].astype(F32))
        _fill_row_shifts(sp)
        y = jnp.zeros((tm, A_WIDTH), F32) + cb_ref[...]
        for k in range(A_KERNEL):
            y = y + cw_ref[k:k + 1, :] * _rows_from(sp, o + k, tm)
        nh, _ = _ln_stats(y)
        ln = nh * ag_ref[...] + ab_ref[...]
        cat_ref[:, 0:A_WIDTH] = (ln * _sigmoid(ln)).astype(BF16)
        u = _gelu(h_ref[:, 1024:1536].astype(F32))
        nb, _ = _ln_stats(_gelu(h_ref[:, 1536:2048].astype(F32)))
        mixed = _spatial_mix(nb * bg_ref[...] + bb_ref[...], ms_ref, sbt_ref, tm)
        cat_ref[:, A_WIDTH:] = (u * mixed).astype(BF16)

    vec = pl.BlockSpec((1, A_WIDTH), lambda i: (0, 0))
    (cat,), moved = _call(
        body, name=name, grid=(nt,),
        in_specs=[pl.BlockSpec((tm, 2048), lambda i: (i, 0)),
                  pl.BlockSpec((MIX_HALO, 2048), lambda i: (jnp.maximum(i * hb - 1, 0), 0)),
                  pl.BlockSpec((A_KERNEL, A_WIDTH), lambda i: (0, 0)), vec, vec, vec, vec, vec,
                  pl.BlockSpec((B_GROUPS, B_CHUNK, B_CHUNK), lambda i: (0, 0, 0)),
                  pl.BlockSpec((B_CHUNK, B_GROUPS), lambda i: (0, 0))],
        out_specs=[pl.BlockSpec((tm, D_MODEL), lambda i: (i, 0))],
        out_shape=[jax.ShapeDtypeStruct((t, D_MODEL), BF16)],
        scratch_shapes=[pltpu.VMEM((SUBLANES, tm + MIX_HALO, A_WIDTH), F32)],
        sem=("parallel",), args=(h, h, cw, cb, ag, ab, bg, bb, ms, sbt), comm=comm)
    return cat if comm is None else (cat, moved)


def _mixer_mid_bwd(h, dcat, cw, cb, ag, ab, bg, bb, ms, mst, sbt, name, tm=256, comm=None):
    t = h.shape[0]
    nt, hb = t // tm, tm // MIX_HALO
    o = MIX_HALO - A_KERNEL + 1
    r = tm + MIX_HALO
    nchunk = tm // B_CHUNK

    def body(h_ref, hp_ref, hn_ref, dc_ref, dcn_ref, cw_ref, cb_ref, ag_ref, ab_ref, bg_ref, bb_ref, ms_ref, mst_ref, sbt_ref,
             dh_ref, dcw_ref, dcb_ref, dag_ref, dab_ref, dbg_ref, dbb_ref, dms_ref, dsb_ref, sp, sdy, sbacc):
        i = pl.program_id(0)

        @pl.when(i == 0)
        def _():
            for ref in (dcw_ref, dcb_ref, dag_ref, dab_ref, dbg_ref, dbb_ref, dms_ref, dsb_ref, sbacc):
                ref[...] = jnp.zeros_like(ref)

        sp[0, 0:MIX_HALO, :] = jnp.where(i > 0, _glu(hp_ref[:, 0:2 * A_WIDTH].astype(F32)), 0.0)
        sp[0, MIX_HALO:MIX_HALO + tm, :] = _glu(h_ref[:, 0:2 * A_WIDTH].astype(F32))
        sp[0, MIX_HALO + tm:, :] = _glu(hn_ref[:, 0:2 * A_WIDTH].astype(F32))
        _fill_row_shifts(sp)
        y = jnp.zeros((r, A_WIDTH), F32) + cb_ref[...]
        for k in range(A_KERNEL):
            y = y + cw_ref[k:k + 1, :] * _rows_from(sp, o + k, r)
        nh, rs = _ln_stats(y)
        ln = nh * ag_ref[...] + ab_ref[...]
        sg = _sigmoid(ln)
        dao = jnp.concatenate([dc_ref[:, 0:A_WIDTH].astype(F32),
                               jnp.where(i < nt - 1, dcn_ref[:, 0:A_WIDTH].astype(F32), 0.0)], axis=0)
        dln = dao * (sg * (1.0 + ln * (1.0 - sg)))
        dag_ref[...] += _colsum(dln[0:tm] * nh[0:tm])
        dab_ref[...] += _colsum(dln[0:tm])
        sdy[0] = _ln_bwd_rows(dln * ag_ref[...], nh, rs)
        _fill_row_shifts(sdy)
        dy_own = sdy[0, 0:tm, :]
        dcb_ref[...] += _colsum(dy_own)
        dp = jnp.zeros((tm, A_WIDTH), F32)
        for k in range(A_KERNEL):
            dcw_ref[k:k + 1, :] += _colsum(dy_own * _rows_from(sp, o + k, tm))
            dp = dp + cw_ref[k:k + 1, :] * _rows_from(sdy, A_KERNEL - 1 - k, tm)
        av = h_ref[:, 0:A_WIDTH].astype(F32)
        s = _sigmoid(h_ref[:, A_WIDTH:2 * A_WIDTH].astype(F32))
        dh_ref[:, 0:A_WIDTH] = (dp * s).astype(BF16)
        dh_ref[:, A_WIDTH:2 * A_WIDTH] = (dp * av * s * (1.0 - s)).astype(BF16)

        u, dgu = _gelu_and_grad(h_ref[:, 1024:1536].astype(F32))
        w, dgw = _gelu_and_grad(h_ref[:, 1536:2048].astype(F32))
        nb, rb = _ln_stats(w)
        q = nb * bg_ref[...] + bb_ref[...]
        mixed = _spatial_mix(q, ms_ref, sbt_ref, tm)
        dbo = dc_ref[:, A_WIDTH:].astype(F32)
        dh_ref[:, 1024:1536] = (dbo * mixed * dgu).astype(BF16)
        dmx = dbo * u
        mask = _tril_mask()
        wst = [jnp.where(mask.T, mst_ref[g], 0.0).astype(BF16) for g in range(B_GROUPS)]
        qb = q.astype(BF16)
        dmb = dmx.astype(BF16)
        rows = []
        for c in range(nchunk):
            cols = []
            for g in range(B_GROUPS):
                rs_, cs_ = slice(c * B_CHUNK, (c + 1) * B_CHUNK), slice(g * 128, (g + 1) * 128)
                sbacc[g] += dmx[rs_, cs_]
                dms_ref[g] += _dot(dmb[rs_, cs_], qb[rs_, cs_], NT)
                cols.append(_dot(wst[g], dmb[rs_, cs_], NN))
            rows.append(jnp.concatenate(cols, axis=1))
        dq = jnp.concatenate(rows, axis=0)
        dbg_ref[...] += _colsum(dq * nb)
        dbb_ref[...] += _colsum(dq)
        dh_ref[:, 1536:2048] = (_ln_bwd_rows(dq * bg_ref[...], nb, rb) * dgw).astype(BF16)

        @pl.when(i == nt - 1)
        def _():
            for g in range(B_GROUPS):
                dms_ref[g] = jnp.where(mask, dms_ref[g], 0.0)
                dsb_ref[g] = jnp.sum(sbacc[g], axis=1, keepdims=True)

    last_blk = t // MIX_HALO - 1
    vec = pl.BlockSpec((1, A_WIDTH), lambda i: (0, 0))
    mat = pl.BlockSpec((B_GROUPS, B_CHUNK, B_CHUNK), lambda i: (0, 0, 0))
    taps = pl.BlockSpec((A_KERNEL, A_WIDTH), lambda i: (0, 0))

    def halo(width, which):
        if which == "prev":
            return pl.BlockSpec((MIX_HALO, width), lambda i: (jnp.maximum(i * hb - 1, 0), 0))
        return pl.BlockSpec((MIX_HALO, width), lambda i: (jnp.minimum((i + 1) * hb, last_blk), 0))

    vshape = jax.ShapeDtypeStruct((1, A_WIDTH), F32)
    outs, moved = _call(
        body, name=name, grid=(nt,),
        in_specs=[pl.BlockSpec((tm, 2048), lambda i: (i, 0)), halo(2048, "prev"), halo(2048, "next"),
                  pl.BlockSpec((tm, D_MODEL), lambda i: (i, 0)), halo(D_MODEL, "next"),
                  taps, vec, vec, vec, vec, vec, mat, mat, pl.BlockSpec((B_CHUNK, B_GROUPS), lambda i: (0, 0))],
        out_specs=[pl.BlockSpec((tm, 2048), lambda i: (i, 0)), taps, vec, vec, vec, vec, vec, mat,
                   pl.BlockSpec((B_GROUPS, B_CHUNK, 1), lambda i: (0, 0, 0))],
        out_shape=[jax.ShapeDtypeStruct((t, 2048), BF16), jax.ShapeDtypeStruct((A_KERNEL, A_WIDTH), F32),
                   vshape, vshape, vshape, vshape, vshape,
                   jax.ShapeDtypeStruct((B_GROUPS, B_CHUNK, B_CHUNK), F32), jax.ShapeDtypeStruct((B_GROUPS, B_CHUNK, 1), F32)],
        scratch_shapes=[pltpu.VMEM((SUBLANES, tm + 2 * MIX_HALO, A_WIDTH), F32), pltpu.VMEM((SUBLANES, r, A_WIDTH), F32),
                        pltpu.VMEM((B_GROUPS, B_CHUNK, B_CHUNK), F32)],
        sem=("arbitrary",), args=(h, h, h, dcat, dcat, cw, cb, ag, ab, bg, bb, ms, mst, sbt), comm=comm)
    return outs if comm is None else (outs, moved)


Q_WIDTH = N_Q_HEADS * HEAD_DIM
KV_WIDTH = 2 * N_KV_HEADS * HEAD_DIM
PAIRS_PER_KV = N_Q_HEADS // N_KV_HEADS // 2
ATT_SCALE = 1.0 / math.sqrt(HEAD_DIM)


def _dup_heads(pair_cols, kv_head):
    lane = lax.broadcasted_iota(jnp.int32, pair_cols.shape, 1)
    rolled = pltpu.roll(pair_cols, HEAD_DIM, 1)
    first = lane < HEAD_DIM
    return jnp.where(first, pair_cols, rolled) if kv_head == 0 else jnp.where(first, rolled, pair_cols)


HEADS_PER_KV = N_Q_HEADS // N_KV_HEADS


def _stack_heads(ref, kh):
    lane = lax.broadcasted_iota(jnp.int32, (ATT_BLOCK, 128), 1)
    rows = []
    for pr in range(PAIRS_PER_KV):
        c0 = (kh * PAIRS_PER_KV + pr) * 128
        pair = ref[:, c0:c0 + 128]
        rows += [jnp.where(lane < HEAD_DIM, pair, jnp.zeros_like(pair)), jnp.where(lane < HEAD_DIM, jnp.zeros_like(pair), pair)]
    return jnp.concatenate(rows, axis=0)


def _unstack_heads(stacked, kh, write):
    lane = lax.broadcasted_iota(jnp.int32, (ATT_BLOCK, 128), 1)
    for pr in range(PAIRS_PER_KV):
        first = stacked[(2 * pr) * ATT_BLOCK:(2 * pr + 1) * ATT_BLOCK]
        second = stacked[(2 * pr + 1) * ATT_BLOCK:(2 * pr + 2) * ATT_BLOCK]
        write((kh * PAIRS_PER_KV + pr) * 128, jnp.where(lane < HEAD_DIM, first, second))


def _sink_row(sink_ref, kh):
    return jnp.concatenate([jnp.full((1, ATT_BLOCK), sink_ref[0, kh * HEADS_PER_KV + h], F32) for h in range(HEADS_PER_KV)], axis=1)


def _att_mask_t(n):
    sj = lax.broadcasted_iota(jnp.int32, (2 * ATT_BLOCK, HEADS_PER_KV * ATT_BLOCK), 0)
    qi = lax.broadcasted_iota(jnp.int32, (2 * ATT_BLOCK, HEADS_PER_KV * ATT_BLOCK), 1) & (ATT_BLOCK - 1)
    diff = qi + ATT_BLOCK - sj
    return (diff >= 0) & (diff < ATT_BLOCK) & ((n > 0) | (sj >= ATT_BLOCK))


def _att_probs_t(q_all, k2, mask_t, sink):
    st = _dot(k2, q_all, NT) * ATT_SCALE
    st = jnp.where(mask_t, st, -jnp.inf)
    m = jnp.maximum(jnp.max(st, axis=0, keepdims=True), sink)
    e = jnp.exp(st - m)
    es = jnp.exp(sink - m)
    inv = 1.0 / (jnp.sum(e, axis=0, keepdims=True) + es)
    return e * inv, es * inv


def _attn_fwd(qkv, sinks, name, comm=None):
    t = qkv.shape[0]
    nb = t // ATT_BLOCK
    kvb = Q_WIDTH // KV_WIDTH

    def body(sink_ref, q_ref, kv_ref, kvp_ref, o_ref):
        n = pl.program_id(0)
        mask_t = _att_mask_t(n)
        kv = jnp.concatenate([kvp_ref[...], kv_ref[...]], axis=0).astype(F32)

        def write(c0, pair):
            o_ref[:, c0:c0 + 128] = pair.astype(BF16)

        for kh in range(N_KV_HEADS):
            k2 = _dup_heads(kv[:, 0:128], kh).astype(BF16)
            v2 = _dup_heads(kv[:, 128:256], kh).astype(BF16)
            pt, _ = _att_probs_t(_stack_heads(q_ref, kh), k2, mask_t, _sink_row(sink_ref, kh))
            _unstack_heads(_dot(v2, pt, TN).T, kh, write)

    (out,), moved = _call(
        body, name=name, grid=(nb,),
        in_specs=[pl.BlockSpec(memory_space=pltpu.SMEM),
                  pl.BlockSpec((ATT_BLOCK, Q_WIDTH), lambda n: (n, 0)),
                  pl.BlockSpec((ATT_BLOCK, KV_WIDTH), lambda n: (n, kvb)),
                  pl.BlockSpec((ATT_BLOCK, KV_WIDTH), lambda n: (jnp.maximum(n - 1, 0), kvb))],
        out_specs=[pl.BlockSpec((ATT_BLOCK, Q_WIDTH), lambda n: (n, 0))],
        out_shape=[jax.ShapeDtypeStruct((t, Q_WIDTH), BF16)],
        sem=("parallel",), args=(sinks, qkv, qkv, qkv), comm=comm)
    return out if comm is None else (out, moved)


def _attn_bwd(qkv, d_o, sinks, name, comm=None):
    t = qkv.shape[0]
    nb = t // ATT_BLOCK
    kvb = Q_WIDTH // KV_WIDTH

    def body(sink_ref, q_ref, kv_ref, kvp_ref, do_ref, dq_ref, dkv_ref, dbq_ref, dbkv_ref, dsink_ref, carry):
        n = pl.program_id(0)

        @pl.when(n == 0)
        def _():
            for ref in (dbq_ref, dbkv_ref, dsink_ref, carry):
                ref[...] = jnp.zeros_like(ref)
            dkv_ref[...] = jnp.zeros_like(dkv_ref)

        @pl.when(n < nb)
        def _():
            mask_t = _att_mask_t(n)
            kv = jnp.concatenate([kvp_ref[...], kv_ref[...]], axis=0).astype(F32)
            lane2 = lax.broadcasted_iota(jnp.int32, (2 * ATT_BLOCK, 128), 1)
            sink_lane = lax.broadcasted_iota(jnp.int32, (1, 128), 1)
            dsink = jnp.zeros((1, 128), F32)
            dk_parts, dv_parts = [], []

            def write(c0, pair):
                dbq_ref[:, c0:c0 + 128] += _colsum(pair)
                dq_ref[:, c0:c0 + 128] = pair.astype(BF16)

            for kh in range(N_KV_HEADS):
                k2 = _dup_heads(kv[:, 0:128], kh).astype(BF16)
                v2 = _dup_heads(kv[:, 128:256], kh).astype(BF16)
                q_all = _stack_heads(q_ref, kh)
                do_all = _stack_heads(do_ref, kh)
                pt, ps = _att_probs_t(q_all, k2, mask_t, _sink_row(sink_ref, kh))
                dpt = _dot(v2, do_all, NT)
                delta = jnp.sum(pt * dpt, axis=0, keepdims=True)
                dst = pt * (dpt - delta) * ATT_SCALE
                psd = ps * delta
                for h in range(HEADS_PER_KV):
                    dsink = dsink + jnp.where(sink_lane == kh * HEADS_PER_KV + h,
                                              -jnp.sum(psd[:, h * ATT_BLOCK:(h + 1) * ATT_BLOCK]), 0.0)
                _unstack_heads(_dot(k2, dst, TN).T, kh, write)
                dk_acc = _dot(dst, q_all, NN)
                dv_acc = _dot(pt, do_all, NN)
                dk_parts.append(dk_acc + pltpu.roll(dk_acc, HEAD_DIM, 1))
                dv_parts.append(dv_acc + pltpu.roll(dv_acc, HEAD_DIM, 1))
            dk = jnp.where(lane2 < HEAD_DIM, dk_parts[0], dk_parts[1])
            dv = jnp.where(lane2 < HEAD_DIM, dv_parts[0], dv_parts[1])
            dkv_new = jnp.concatenate([dk, dv], axis=1)
            done = carry[...] + dkv_new[0:ATT_BLOCK]

            @pl.when(n > 0)
            def _():
                dkv_ref[...] = done.astype(BF16)
                dbkv_ref[...] += _colsum(done)

            carry[...] = dkv_new[ATT_BLOCK:]
            dsink_ref[...] += dsink

        @pl.when(n == nb)
        def _():
            dkv_ref[...] = carry[...].astype(BF16)
            dbkv_ref[...] += _colsum(carry[...])

    def clamp(n):
        return jnp.minimum(n, nb - 1)

    outs, moved = _call(
        body, name=name, grid=(nb + 1,),
        in_specs=[pl.BlockSpec(memory_space=pltpu.SMEM),
                  pl.BlockSpec((ATT_BLOCK, Q_WIDTH), lambda n: (clamp(n), 0)),
                  pl.BlockSpec((ATT_BLOCK, KV_WIDTH), lambda n: (clamp(n), kvb)),
                  pl.BlockSpec((ATT_BLOCK, KV_WIDTH), lambda n: (jnp.maximum(clamp(n) - 1, 0), kvb)),
                  pl.BlockSpec((ATT_BLOCK, Q_WIDTH), lambda n: (clamp(n), 0))],
        out_specs=[pl.BlockSpec((ATT_BLOCK, Q_WIDTH), lambda n: (clamp(n), 0)),
                   pl.BlockSpec((ATT_BLOCK, KV_WIDTH), lambda n: (jnp.maximum(n - 1, 0), 0)),
                   pl.BlockSpec((1, Q_WIDTH), lambda n: (0, 0)),
                   pl.BlockSpec((1, KV_WIDTH), lambda n: (0, 0)),
                   pl.BlockSpec((1, 128), lambda n: (0, 0))],
        out_shape=[jax.ShapeDtypeStruct((t, Q_WIDTH), BF16), jax.ShapeDtypeStruct((t, KV_WIDTH), BF16),
                   jax.ShapeDtypeStruct((1, Q_WIDTH), F32), jax.ShapeDtypeStruct((1, KV_WIDTH), F32),
                   jax.ShapeDtypeStruct((1, 128), F32)],
        scratch_shapes=[pltpu.VMEM((ATT_BLOCK, KV_WIDTH), F32)],
        sem=("arbitrary",), args=(sinks, qkv, qkv, qkv, d_o), comm=comm)
    return outs if comm is None else (outs, moved)


def _adamw_math(g, w, m, v):
    m = ADAM_B1 * m + (1.0 - ADAM_B1) * g
    v = ADAM_B2 * v + (1.0 - ADAM_B2) * (g * g)
    m_hat = m / (1.0 - ADAM_B1 ** ADAM_STEP)
    v_hat = v / (1.0 - ADAM_B2 ** ADAM_STEP)
    delta = -ADAM_LR * (m_hat / (jnp.sqrt(v_hat) + ADAM_EPS) + ADAM_WD * w)
    return delta, m, v


def _sum_partials(p_ref):
    g = p_ref[0].astype(F32)
    for s in range(1, N_DEV):
        g = g + p_ref[s].astype(F32)
    return g


def _adamw_big(parts, w, m, v, name, tr):
    r, c = w.shape
    parts = [p if isinstance(p, tuple) else (p, 0, p.shape[1]) for p in parts]
    tiles = [rows // tr for _, _, rows in parts]
    starts = [sum(tiles[:l]) for l in range(len(parts))]
    assert all(lo % tr == 0 and rows % tr == 0 for _, lo, rows in parts) and sum(tiles) * tr == r

    def body(*refs):
        p_refs, (w_ref, m_ref, v_ref, g_out, d_out, m_out, v_out) = refs[:len(parts)], refs[len(parts):]
        i = pl.program_id(0)
        for l, p_ref in enumerate(p_refs):
            @pl.when((i >= starts[l]) & (i < starts[l] + tiles[l]))
            def _():
                g = _sum_partials(p_ref)
                g_out[...] = g
                d_out[...], m_out[...], v_out[...] = _adamw_math(g, w_ref[...], m_ref[...], v_ref[...])

    def part_spec(l):
        return pl.BlockSpec((N_DEV, tr, c), lambda i: (0, jnp.clip(i - starts[l], 0, tiles[l] - 1) + parts[l][1] // tr, 0))

    tile = pl.BlockSpec((tr, c), lambda i: (i, 0))
    shape = jax.ShapeDtypeStruct((r, c), F32)
    return pl.pallas_call(
        body, name=name, grid=(r // tr,),
        in_specs=[part_spec(l) for l in range(len(parts))] + [tile, tile, tile],
        out_specs=[tile] * 4, out_shape=[shape] * 4,
        compiler_params=_params("parallel"),
    )(*[p[0] for p in parts], w, m, v)


def _adamw_small(parts, ws, ms, vs, name):
    n = len(ws)

    def body(*refs):
        ins, outs = refs[:4 * n], refs[4 * n:]
        for a in range(n):
            g = _sum_partials(ins[a])
            outs[4 * a][...] = g
            outs[4 * a + 1][...], outs[4 * a + 2][...], outs[4 * a + 3][...] = _adamw_math(
                g, ins[n + a][...], ins[2 * n + a][...], ins[3 * n + a][...])

    out_shape = []
    for w in ws:
        out_shape += [jax.ShapeDtypeStruct(w.shape, F32)] * 4
    return pl.pallas_call(body, name=name, out_shape=out_shape, compiler_params=_params())(*parts, *ws, *ms, *vs)


PACK_LANES = 128
PACK_ROWS = 8


def _pack(arrs):
    flat = jnp.concatenate([a.reshape(-1).astype(F32) for a in arrs])
    unit = PACK_LANES * PACK_ROWS
    total = -(-flat.shape[0] // unit) * unit
    return jnp.pad(flat, (0, total - flat.shape[0])).reshape(-1, PACK_LANES)


def _unpack(buf, shapes):
    flat = buf.reshape(N_DEV, -1)
    out, pos = [], 0
    for s in shapes:
        size = math.prod(s)
        out.append(flat[:, pos:pos + size].reshape((N_DEV,) + tuple(s)))
        pos += size
    return out


def _interleave(g):
    return jnp.transpose(g, (1, 0, 2)).reshape(g.shape[1], -1)


def _deinterleave(w):
    r = w.shape[0]
    return jnp.transpose(w.reshape(r, N_DEV, -1), (1, 0, 2))


def _ffn_backward(dz, x_in, z_in, g_in, h, u, w_up_t, cw, cb, w_down, tag, exchange=(), exchange_late=()):
    du = _matmul(dz, w_down, "nt", BF16, f"ffn{tag}_du", 1024, 1408, 1024)
    d_w_down = _matmul(u, dz, "tn", BF16, f"ffn{tag}_dwdown", 1408, 1024, 2048)
    (dhg, dhv, dcwg, dcwv, dcbg, dcbv), moved = _ffn_mid_bwd(
        h, du, cw, cb, f"ffn{tag}_mid_bwd", comm=_Comm(exchange=[d_w_down.reshape(N_DEV, -1, D_MODEL), *exchange]))
    d_w_up_t = _matmul_tn_pair(dhg, dhv, x_in, BF16, f"ffn{tag}_dwup", 1408, 1024, 1024,
                               comm=_Comm(exchange=exchange_late) if exchange_late else None)
    if exchange_late:
        d_w_up_t, late = d_w_up_t
        moved = moved + late
    dx = _matmul(dhv, w_up_t, "nn", F32, f"ffn{tag}_dx_value", 1024, 1024, D_FF, b_off=1)
    dz_in, dg_in, db_in = _matmul_ln_bwd(dhg, w_up_t, z_in, g_in, dz, f"ffn{tag}_dx_gate_ln_bwd", 512, res=dx)
    return (dz_in, dg_in, db_in, d_w_up_t.reshape(N_DEV, -1, D_MODEL),
            jnp.concatenate([dcwg, dcwv], axis=1), jnp.concatenate([dcbg, dcbv], axis=1), moved)


def kernel(x, ab_w_in, a_conv_w, a_conv_b, a_norm_g, a_norm_b, b_norm_g, b_norm_b, b_spatial_w, b_spatial_b, ab_w_out, c_w_qkv, c_b_qkv, c_sinks, c_w_o, ffn_w_up, ffn_conv_w, ffn_conv_b, ffn_w_down, ln_g, ln_b, loss_target, m_ab_w_in, m_a_conv_w, m_a_conv_b, m_a_norm_g, m_a_norm_b, m_b_norm_g, m_b_norm_b, m_b_spatial_w, m_b_spatial_b, m_ab_w_out, m_c_w_qkv, m_c_b_qkv, m_c_sinks, m_c_w_o, m_ffn_w_up, m_ffn_conv_w, m_ffn_conv_b, m_ffn_w_down, m_ln_g, m_ln_b, v_ab_w_in, v_a_conv_w, v_a_conv_b, v_a_norm_g, v_a_norm_b, v_b_norm_g, v_b_norm_b, v_b_spatial_w, v_b_spatial_b, v_ab_w_out, v_c_w_qkv, v_c_b_qkv, v_c_sinks, v_c_w_o, v_ffn_w_up, v_ffn_conv_w, v_ffn_conv_b, v_ffn_w_down, v_ln_g, v_ln_b):
    me = 4 * lax.axis_index("x") + 2 * lax.axis_index("y") + lax.axis_index("c")
    xt = x[0]
    t = xt.shape[0]

    small_shard_shapes = [a_conv_w.shape, c_b_qkv.shape, ffn_conv_w.shape, ln_g.shape, ln_b.shape]
    up_shard = [jnp.swapaxes(ffn_w_up[l], 0, 1).astype(BF16) for l in range(2)]
    qkv_shard = jnp.swapaxes(c_w_qkv[0], 0, 1).astype(BF16)
    down_shard = [ffn_w_down[l].astype(BF16) for l in range(2)]
    g_win, g_small = _comm_only(
        _Comm(gather=[ab_w_in[0].astype(BF16), _pack([a_conv_w, c_b_qkv, ffn_conv_w, ln_g, ln_b])]), "gather_first")
    w_in = _interleave(g_win)
    g_acw, g_bqkv, g_fcw, g_lng, g_lnb = _unpack(g_small, small_shard_shapes)
    acw = _interleave(g_acw[:, 0])
    bqkv = g_bqkv[:, 0].reshape(1, -1)
    fcw = [_interleave(g_fcw[:, l]) for l in range(2)]
    lng = jnp.transpose(g_lng, (1, 2, 0, 3)).reshape(2, 2, 1, D_MODEL)
    lnb = jnp.transpose(g_lnb, (1, 2, 0, 3)).reshape(2, 2, 1, D_MODEL)
    fcb = [ffn_conv_b[l:l + 1] for l in range(2)]
    ms = b_spatial_w[0]
    mst = jnp.swapaxes(ms, 1, 2)
    sbt = b_spatial_b[0].T

    h0, (g_wout,) = _matmul(xt, w_in, "nn", BF16, "mix_in", 1024, 1024, 1024, comm=_Comm(gather=[ab_w_out[0].astype(BF16)]))
    w_out = g_wout.reshape(D_MODEL, D_MODEL)
    cat, (g_wup0,) = _mixer_mid_fwd(h0, acw, a_conv_b, a_norm_g, a_norm_b, b_norm_g, b_norm_b, ms, sbt, "mix_mid_fwd",
                                    comm=_Comm(gather=[up_shard[0]]))
    w_up0 = g_wup0.reshape(2 * D_FF, D_MODEL)
    z1, x1 = _matmul_res_ln(cat, w_out, xt, lng[0, 0], lnb[0, 0], "mix_out_ln", 512)
    hf0, (g_wdown0, g_wqkv) = _matmul(x1, w_up0, "nt", BF16, "ffn0_up", 1024, 1408, 1024,
                                      comm=_Comm(gather=[down_shard[0], qkv_shard]))
    w_down0 = g_wdown0.reshape(D_FF, D_MODEL)
    w_qkv = g_wqkv.reshape(Q_WIDTH + KV_WIDTH, D_MODEL)
    u0, (g_wup1,) = _ffn_mid_fwd(hf0, fcw[0], fcb[0], "ffn0_mid_fwd", comm=_Comm(gather=[up_shard[1]]))
    w_up1 = g_wup1.reshape(2 * D_FF, D_MODEL)
    (z2, x2), (g_wo,) = _matmul_res_ln(u0, w_down0, x1, lng[0, 1], lnb[0, 1], "ffn0_down_ln", 512,
                                       comm=_Comm(gather=[c_w_o[0].astype(BF16)]))
    w_o = g_wo.reshape(D_MODEL, D_MODEL)
    qkv = _matmul(x2, w_qkv, "nt", BF16, "att_qkv", 1024, 1280, 1024, bias=bqkv)
    att, (g_wdown1,) = _attn_fwd(qkv, c_sinks, "att_fwd", comm=_Comm(gather=[down_shard[1]]))
    w_down1 = g_wdown1.reshape(D_FF, D_MODEL)
    z3, x3 = _matmul_res_ln(att, w_o, x2, lng[1, 0], lnb[1, 0], "att_out_ln", 512)
    hf1 = _matmul(x3, w_up1, "nt", BF16, "ffn1_up", 1024, 1408, 1024)
    u1 = _ffn_mid_fwd(hf1, fcw[1], fcb[1], "ffn1_mid_fwd")

    dz4, dg11, db11, loss_terms = _matmul_res_ln_loss(u1, w_down1, x3, lng[1, 1], lnb[1, 1], loss_target[0],
                                                      "ffn1_down_ln_loss", 512)
    dz3, dg10, db10, d_wup1, d_fcw1, d_fcb1, (p_wdown1,) = _ffn_backward(
        dz4, x3, z3, lng[1, 0], hf1, u1, w_up1, fcw[1], fcb[1], w_down1, 1)
    d_att = _matmul(dz3, w_o, "nt", BF16, "att_dout", 1024, 1024, 1024)
    d_wo = _matmul(att, dz3, "tn", BF16, "att_dwo", 1024, 1024, 512)
    half = d_wup1.shape[1] // 2
    (dq, dkv, dbq, dbkv, dsinks), (p_wup1a,) = _attn_bwd(qkv, d_att, c_sinks, "att_bwd",
                                                        comm=_Comm(exchange=[(d_wup1, 0, half)]))
    d_wqkv = jnp.concatenate([_matmul(dq, x2, "tn", BF16, "att_dwq", 1024, 1024, 1024),
                              _matmul(dkv, x2, "tn", BF16, "att_dwkv", KV_WIDTH, 1024, 1024)], axis=0)
    dx2 = _matmul(dkv, w_qkv, "nn", F32, "att_dx_kv", 1024, 1024, KV_WIDTH, b_off=Q_WIDTH // KV_WIDTH)
    dz2, dg01, db01 = _matmul_ln_bwd(dq, w_qkv, z2, lng[0, 1], dz3, "att_dx_q_ln_bwd", 512, res=dx2)
    dz1, dg00, db00, d_wup0, d_fcw0, d_fcb0, (p_wdown0, p_wup1b, p_wqkv, p_wo) = _ffn_backward(
        dz2, x1, z1, lng[0, 0], hf0, u0, w_up0, fcw[0], fcb[0], w_down0, 0, exchange=[(d_wup1, half, half)],
        exchange_late=[d_wqkv.reshape(N_DEV, -1, D_MODEL), d_wo.reshape(N_DEV, -1, D_MODEL)])
    dcat = _matmul(dz1, w_out, "nt", BF16, "mix_dcat", 1024, 1024, 1024)
    d_wout = _matmul(cat, dz1, "tn", BF16, "mix_dwout", 1024, 1024, 512)
    (dh0, d_acw, d_acb, d_ang, d_anb, d_bng, d_bnb, d_ms, d_sb), (p_wup0, p_wout) = _mixer_mid_bwd(
        h0, dcat, acw, a_conv_b, a_norm_g, a_norm_b, b_norm_g, b_norm_b, ms, mst, sbt, "mix_mid_bwd",
        comm=_Comm(exchange=[d_wup0, d_wout.reshape(N_DEV, -1, D_MODEL)]))
    d_bqkv = jnp.concatenate([dbq, dbkv], axis=1)
    d_lng = jnp.stack([jnp.stack([dg00, dg01]), jnp.stack([dg10, dg11])])
    d_lnb = jnp.stack([jnp.stack([db00, db01]), jnp.stack([db10, db11])])
    small_full = [d_acb, d_ang, d_anb, d_bng, d_bnb, d_ms, d_sb, dsinks[:, :N_Q_HEADS], jnp.concatenate([d_fcb0, d_fcb1], axis=0),
                  d_acw, d_bqkv, jnp.stack([d_fcw0, d_fcw1]), d_lng, d_lnb]
    d_win, (g_small_grads,) = _matmul(xt, dh0, "tn", BF16, "mix_dwin", 1024, 1024, 512, comm=_Comm(gather=[_pack(small_full)]))
    grad_x, (p_win,) = _matmul(dh0, w_in, "nt", F32, "mix_dx", 1024, 1024, 1024, res=dz1, res_scale=ALPHA,
                               comm=_Comm(exchange=[_deinterleave(d_win)]))

    loss = lax.psum(0.5 / D_MODEL * jnp.sum(loss_terms), ("x", "y", "c"))

    big = {}
    for nm, p, w, m, v, tr, transposed in [
            ("ab_w_in", [p_win], ab_w_in, m_ab_w_in, v_ab_w_in, 256, False),
            ("ab_w_out", [p_wout], ab_w_out, m_ab_w_out, v_ab_w_out, 128, False),
            ("c_w_qkv", [p_wqkv], c_w_qkv, m_c_w_qkv, v_c_w_qkv, 160, True), ("c_w_o", [p_wo], c_w_o, m_c_w_o, v_c_w_o, 128, False),
            ("ffn_w_up", [p_wup0, (p_wup1a, 0, half), (p_wup1b, half, half)], ffn_w_up, m_ffn_w_up, v_ffn_w_up, 176, True),
            ("ffn_w_down", [p_wdown0, p_wdown1], ffn_w_down, m_ffn_w_down, v_ffn_w_down, 176, False)]:
        def two_d(a):
            a = jnp.swapaxes(a, 1, 2) if transposed else a
            return a.reshape(-1, a.shape[-1])

        def back(o):
            return jnp.swapaxes(o.reshape(w.shape[0], w.shape[2], w.shape[1]), 1, 2) if transposed else o.reshape(w.shape)

        outs = _adamw_big(p, two_d(w), two_d(m), two_d(v), "adamw_" + nm, tr)
        big[nm] = [back(o) for o in outs]

    gs = _unpack(g_small_grads, [a.shape for a in small_full])

    def my_shard(g, width):
        g = g.reshape(g.shape[:-1] + (N_DEV, width))
        return lax.dynamic_index_in_dim(g, me, axis=g.ndim - 2, keepdims=False)

    small_names = ["a_conv_b", "a_norm_g", "a_norm_b", "b_norm_g", "b_norm_b", "b_spatial_w", "b_spatial_b", "c_sinks", "ffn_conv_b",
                   "a_conv_w", "c_b_qkv", "ffn_conv_w", "ln_g", "ln_b"]
    small_w = [a_conv_b, a_norm_g, a_norm_b, b_norm_g, b_norm_b, b_spatial_w, b_spatial_b, c_sinks, ffn_conv_b,
               a_conv_w, c_b_qkv, ffn_conv_w, ln_g, ln_b]
    small_m = [m_a_conv_b, m_a_norm_g, m_a_norm_b, m_b_norm_g, m_b_norm_b, m_b_spatial_w, m_b_spatial_b, m_c_sinks, m_ffn_conv_b,
               m_a_conv_w, m_c_b_qkv, m_ffn_conv_w, m_ln_g, m_ln_b]
    small_v = [v_a_conv_b, v_a_norm_g, v_a_norm_b, v_b_norm_g, v_b_norm_b, v_b_spatial_w, v_b_spatial_b, v_c_sinks, v_ffn_conv_b,
               v_a_conv_w, v_c_b_qkv, v_ffn_conv_w, v_ln_g, v_ln_b]
    gs[9:] = [my_shard(g, w.shape[-1]) for g, w in zip(gs[9:], small_w[9:])]
    two_d = [(-1, w.shape[-1]) for w in small_w]
    outs = _adamw_small([g.reshape((N_DEV,) + w.reshape(s).shape) for g, w, s in zip(gs, small_w, two_d)],
                        [w.reshape(s) for w, s in zip(small_w, two_d)], [m.reshape(s) for m, s in zip(small_m, two_d)],
                        [v.reshape(s) for v, s in zip(small_v, two_d)], "adamw_small")
    small = {nm: [o.reshape(w.shape) for o in outs[4 * a:4 * a + 4]] for a, (nm, w) in enumerate(zip(small_names, small_w))}

    res = {**big, **small}
    order = ["ab_w_in", "a_conv_w", "a_conv_b", "a_norm_g", "a_norm_b", "b_norm_g", "b_norm_b", "b_spatial_w", "b_spatial_b", "ab_w_out",
             "c_w_qkv", "c_b_qkv", "c_sinks", "c_w_o", "ffn_w_up", "ffn_conv_w", "ffn_conv_b", "ffn_w_down", "ln_g", "ln_b"]
    return (loss, grad_x[None], *[res[nm][0] for nm in order], *[res[nm][1] for nm in order],
            *[res[nm][2] for nm in order], *[res[nm][3] for nm in order])
```

```python
import functools
import math

import jax
import jax.numpy as jnp
from jax import lax
from jax.experimental import pallas as pl
from jax.experimental.pallas import tpu as pltpu

F32 = jnp.float32
BF16 = jnp.bfloat16

N_DEV = 8
D_MODEL = 1024
A_WIDTH = 512
A_KERNEL = 31
B_GROUPS = 4
B_CHUNK = 128
HEAD_DIM = 64
N_Q_HEADS = 16
N_KV_HEADS = 2
ATT_BLOCK = 128
D_FF = 2816
FFN_KERNEL = 3
ALPHA = (2.0 * 2) ** 0.25
LN_EPS = 1e-5
GELU_K = math.sqrt(2.0 / math.pi)
GELU_C = 0.044715
ADAM_LR = 0.001
ADAM_B1 = 0.9
ADAM_B2 = 0.999
ADAM_EPS = 1e-08
ADAM_WD = 0.01
ADAM_STEP = 10
VMEM_LIMIT = 56 * 1024 * 1024
MESH_ID = pl.DeviceIdType.MESH


def _params(*sem):
    return pltpu.CompilerParams(dimension_semantics=sem, vmem_limit_bytes=VMEM_LIMIT)


def _gelu(x):
    t = jnp.tanh(GELU_K * x * (1.0 + GELU_C * x * x))
    return 0.5 * x * (1.0 + t)


def _gelu_and_grad(x):
    x2 = x * x
    t = jnp.tanh(GELU_K * x * (1.0 + GELU_C * x2))
    g = 0.5 * x * (1.0 + t)
    dg = 0.5 * (1.0 + t) + 0.5 * x * (1.0 - t * t) * (GELU_K * (1.0 + 3.0 * GELU_C * x2))
    return g, dg


def _sigmoid(x):
    return 1.0 / (1.0 + jnp.exp(-x))


def _ln_stats(z):
    mu = jnp.mean(z, axis=-1, keepdims=True)
    zc = z - mu
    var = jnp.mean(zc * zc, axis=-1, keepdims=True)
    r = lax.rsqrt(var + LN_EPS)
    return zc * r, r


def _ln_bwd_rows(dn, nh, r):
    return r * (dn - jnp.mean(dn, axis=-1, keepdims=True) - nh * jnp.mean(dn * nh, axis=-1, keepdims=True))


def _colsum(x):
    return jnp.sum(x, axis=0, keepdims=True)


def _dot(a, b, dims):
    return lax.dot_general(a.astype(BF16), b.astype(BF16), (dims, ((), ())), preferred_element_type=F32)


NN = ((1,), (0,))
NT = ((1,), (1,))
TN = ((0,), (0,))


ANY = pl.BlockSpec(memory_space=pl.ANY)
N_RELATIONS = N_DEV - 1


def _my_place():
    return lax.axis_index("x"), lax.axis_index("y"), lax.axis_index("c")


class _Comm:
    def __init__(self, gather=(), exchange=()):
        exchange = [e if isinstance(e, tuple) else (e, 0, e.shape[1]) for e in exchange]
        self.arrs = list(gather) + [e[0] for e in exchange]
        self.n_gather = len(gather)
        self.n = len(self.arrs)
        self.rows = [None] * self.n_gather + [pl.ds(lo, n) for _, lo, n in exchange]

    def out_shape(self):
        return [jax.ShapeDtypeStruct(((N_DEV,) + a.shape) if i < self.n_gather else a.shape, a.dtype)
                for i, a in enumerate(self.arrs)]

    def sems(self):
        return [pltpu.SemaphoreType.DMA((self.n, N_RELATIONS)), pltpu.SemaphoreType.DMA((self.n, N_RELATIONS)),
                pltpu.SemaphoreType.DMA((self.n,))]

    def _gather_copy(self, ins, outs, sems, a, k, place, to, from_input=False):
        px, py, pc = place
        block = outs[a].at[4 * px + 2 * py + pc]
        return pltpu.make_async_remote_copy(
            src_ref=ins[a] if from_input else block, dst_ref=block,
            send_sem=sems[0].at[a, k], recv_sem=sems[1].at[a, k], device_id=to, device_id_type=MESH_ID)

    def _exchange_copy(self, ins, outs, sems, a, k, landing=False):
        x, y, c = _my_place()
        me = 4 * x + 2 * y + c
        peer = (x ^ (k >> 2), y ^ ((k >> 1) & 1), c ^ (k & 1))
        return pltpu.make_async_remote_copy(
            src_ref=ins[a].at[me ^ k, self.rows[a]], dst_ref=outs[a].at[(me ^ k) if landing else me, self.rows[a]],
            send_sem=sems[0].at[a, k - 1], recv_sem=sems[1].at[a, k - 1], device_id=peer, device_id_type=MESH_ID)

    def _local_copy(self, ins, outs, sems, a):
        x, y, c = _my_place()
        me = 4 * x + 2 * y + c
        if a < self.n_gather:
            return pltpu.make_async_copy(ins[a], outs[a].at[me], sems[2].at[a])
        return pltpu.make_async_copy(ins[a].at[me, self.rows[a]], outs[a].at[me, self.rows[a]], sems[2].at[a])

    def _first_stage(self, ins, outs, sems, a):
        x, y, c = _my_place()
        me = (x, y, c)
        chips = [(1 - x, y), (x, 1 - y), (1 - x, 1 - y)]
        return ([self._gather_copy(ins, outs, sems, a, 0, me, (x, y, 1 - c), from_input=True)]
                + [self._gather_copy(ins, outs, sems, a, 1 + j, me, (*chip, c), from_input=True) for j, chip in enumerate(chips)])

    def start(self, ins, outs, sems):
        for a in range(self.n):
            self._local_copy(ins, outs, sems, a).start()
        for a in range(self.n_gather):
            for cp in self._first_stage(ins, outs, sems, a):
                cp.start()
        for k in range(1, N_DEV):
            for a in range(self.n_gather, self.n):
                self._exchange_copy(ins, outs, sems, a, k).start()

    def finish(self, ins, outs, sems):
        x, y, c = _my_place()
        me, sibling = (x, y, c), (x, y, 1 - c)
        chips = [(1 - x, y), (x, 1 - y), (1 - x, 1 - y)]
        passed = []
        for j, chip in enumerate(chips):
            for a in range(self.n_gather):
                self._gather_copy(ins, outs, sems, a, 1 + j, (*chip, c), me).wait_recv()
                fwd = self._gather_copy(ins, outs, sems, a, 4 + j, (*chip, c), sibling)
                fwd.start()
                passed.append(fwd)
        for a in range(self.n_gather):
            self._gather_copy(ins, outs, sems, a, 0, sibling, me).wait_recv()
            for j, chip in enumerate(chips):
                self._gather_copy(ins, outs, sems, a, 4 + j, (*chip, 1 - c), me).wait_recv()
        for k in range(1, N_DEV):
            for a in range(self.n_gather, self.n):
                self._exchange_copy(ins, outs, sems, a, k, landing=True).wait_recv()
        for a in range(self.n_gather):
            for cp in self._first_stage(ins, outs, sems, a):
                cp.wait_send()
        for cp in passed:
            cp.wait_send()
        for k in range(1, N_DEV):
            for a in range(self.n_gather, self.n):
                self._exchange_copy(ins, outs, sems, a, k).wait_send()
        for a in range(self.n):
            self._local_copy(ins, outs, sems, a).wait()


def _comm_only(comm, name):
    def body(*refs):
        ins, outs, sems = refs[:comm.n], refs[comm.n:2 * comm.n], refs[2 * comm.n:]
        comm.start(ins, outs, sems)
        comm.finish(ins, outs, sems)

    return pl.pallas_call(body, name=name, in_specs=[ANY] * comm.n, out_specs=[ANY] * comm.n,
                          out_shape=comm.out_shape(), scratch_shapes=comm.sems())(*comm.arrs)


def _call(body, *, name, grid, in_specs, out_specs, out_shape, args, sem, scratch_shapes=(), comm=None):
    in_specs, out_specs, out_shape, scratch_shapes = list(in_specs), list(out_specs), list(out_shape), list(scratch_shapes)
    if comm is None:
        outs = pl.pallas_call(body, name=name, grid=grid, in_specs=in_specs, out_specs=out_specs, out_shape=out_shape,
                              scratch_shapes=scratch_shapes, compiler_params=_params(*sem))(*args)
        return list(outs), []
    n_in, n_out, n_scr, nc = len(in_specs), len(out_specs), len(scratch_shapes), comm.n

    def wrapped(*refs):
        ins, refs = refs[:n_in], refs[n_in:]
        c_in, refs = refs[:nc], refs[nc:]
        outs, refs = refs[:n_out], refs[n_out:]
        c_out, refs = refs[:nc], refs[nc:]
        scr, sems = refs[:n_scr], refs[n_scr:]
        first = functools.reduce(jnp.logical_and, [pl.program_id(ax) == 0 for ax in range(len(grid))])
        last = functools.reduce(jnp.logical_and, [pl.program_id(ax) == g - 1 for ax, g in enumerate(grid)])

        @pl.when(first)
        def _():
            comm.start(c_in, c_out, sems)

        body(*ins, *outs, *scr)

        @pl.when(last)
        def _():
            comm.finish(c_in, c_out, sems)

    outs = pl.pallas_call(
        wrapped, name=name, grid=grid, in_specs=in_specs + [ANY] * nc, out_specs=out_specs + [ANY] * nc,
        out_shape=out_shape + comm.out_shape(), scratch_shapes=scratch_shapes + comm.sems(),
        compiler_params=_params(*(["arbitrary"] * len(grid))))(*args, *comm.arrs)
    return list(outs[:n_out]), list(outs[n_out:])


def _matmul(a, b, mode, out_dtype, name, tm, tn, tk, *, bias=None, res=None, res_scale=1.0, b_off=0, comm=None):
    tm = min(tm, a.shape[1] if mode == "tn" else a.shape[0])
    tk = min(tk, a.shape[0] if mode == "tn" else a.shape[1])
    if mode == "nn":
        (m, k), n = a.shape, b.shape[1]
        a_spec = pl.BlockSpec((tm, tk), lambda i, j, kk: (i, kk))
        b_spec = pl.BlockSpec((tk, tn), lambda i, j, kk: (kk + b_off, j))
        dims = NN
    elif mode == "nt":
        (m, k), n = a.shape, b.shape[0]
        a_spec = pl.BlockSpec((tm, tk), lambda i, j, kk: (i, kk))
        b_spec = pl.BlockSpec((tn, tk), lambda i, j, kk: (j, kk + b_off))
        dims = NT
    else:
        (k, m), n = a.shape, b.shape[1]
        a_spec = pl.BlockSpec((tk, tm), lambda i, j, kk: (kk, i))
        b_spec = pl.BlockSpec((tk, tn), lambda i, j, kk: (kk, j))
        dims = TN
    assert m % tm == 0 and n % tn == 0 and k % tk == 0, (name, m, n, k)
    nk = k // tk
    in_specs = [a_spec, b_spec]
    args = [a, b]
    if bias is not None:
        in_specs.append(pl.BlockSpec((1, tn), lambda i, j, kk: (0, j)))
        args.append(bias)
    if res is not None:
        in_specs.append(pl.BlockSpec((tm, tn), lambda i, j, kk: (i, j)))
        args.append(res)

    def finish(out, refs, o_ref):
        pos = 2
        if bias is not None:
            out = out + refs[pos][...]
            pos += 1
        if res is not None:
            out = out + res_scale * refs[pos][...].astype(F32)
        o_ref[...] = out.astype(out_dtype)

    def body_one_step(*refs):
        finish(_dot(refs[0][...], refs[1][...], dims), refs, refs[-1])

    def body(*refs):
        a_ref, b_ref = refs[0], refs[1]
        o_ref, acc = refs[-2], refs[-1]
        kk = pl.program_id(2)

        @pl.when(kk == 0)
        def _():
            acc[...] = jnp.zeros_like(acc)

        acc[...] += _dot(a_ref[...], b_ref[...], dims)

        @pl.when(kk == nk - 1)
        def _():
            finish(acc[...], refs, o_ref)

    (out,), moved = _call(
        body_one_step if nk == 1 else body, name=name, grid=(m // tm, n // tn, nk),
        in_specs=in_specs, out_specs=[pl.BlockSpec((tm, tn), lambda i, j, kk: (i, j))],
        out_shape=[jax.ShapeDtypeStruct((m, n), out_dtype)],
        scratch_shapes=[] if nk == 1 else [pltpu.VMEM((tm, tn), F32)],
        sem=("parallel", "parallel", "arbitrary"), args=args, comm=comm)
    return out if comm is None else (out, moved)


def _matmul_tn_pair(a0, a1, b, out_dtype, name, tm, tn, tk, comm=None):
    (k, m), n = a0.shape, b.shape[1]
    tk = min(tk, k)
    assert a1.shape == a0.shape and m % tm == 0 and n % tn == 0 and k % tk == 0, (name, m, n, k)
    mi, nk = m // tm, k // tk

    def body(a0_ref, a1_ref, b_ref, o_ref, acc):
        i, kk = pl.program_id(0), pl.program_id(2)

        @pl.when(kk == 0)
        def _():
            acc[...] = jnp.zeros_like(acc)

        @pl.when(i < mi)
        def _():
            acc[...] += _dot(a0_ref[...], b_ref[...], TN)

        @pl.when(i >= mi)
        def _():
            acc[...] += _dot(a1_ref[...], b_ref[...], TN)

        @pl.when(kk == nk - 1)
        def _():
            o_ref[...] = acc[...].astype(out_dtype)

    (out,), moved = _call(
        body, name=name, grid=(2 * mi, n // tn, nk),
        in_specs=[pl.BlockSpec((tk, tm), lambda i, j, kk: (jnp.where(i < mi, kk, nk - 1), jnp.minimum(i, mi - 1))),
                  pl.BlockSpec((tk, tm), lambda i, j, kk: (jnp.where(i >= mi, kk, 0), jnp.maximum(i - mi, 0))),
                  pl.BlockSpec((tk, tn), lambda i, j, kk: (kk, j))],
        out_specs=[pl.BlockSpec((tm, tn), lambda i, j, kk: (i, j))],
        out_shape=[jax.ShapeDtypeStruct((2 * m, n), out_dtype)],
        scratch_shapes=[pltpu.VMEM((tm, tn), F32)],
        sem=("parallel", "parallel", "arbitrary"), args=(a0, a1, b), comm=comm)
    return out if comm is None else (out, moved)


def _residual_input(x_ref, prev_refs):
    if not prev_refs:
        return x_ref[...]
    nh, _ = _ln_stats(x_ref[...])
    return nh * prev_refs[0][...] + prev_refs[1][...]


def _matmul_res_ln(a, b, x, g, beta, name, tm, prev=None, comm=None):
    t, k = a.shape
    d = b.shape[1]
    tm = min(tm, t)
    assert t % tm == 0
    n_prev = 0 if prev is None else 2

    def body(a_ref, b_ref, x_ref, g_ref, beta_ref, *rest):
        z_ref, xo_ref = rest[n_prev:]
        z = ALPHA * _residual_input(x_ref, rest[:n_prev]) + _dot(a_ref[...], b_ref[...], NN)
        nh, _ = _ln_stats(z)
        z_ref[...] = z
        xo_ref[...] = (nh * g_ref[...] + beta_ref[...]).astype(BF16)

    row = pl.BlockSpec((tm, d), lambda i: (i, 0))
    vec = pl.BlockSpec((1, d), lambda i: (0, 0))
    outs, moved = _call(
        body, name=name, grid=(t // tm,),
        in_specs=[pl.BlockSpec((tm, k), lambda i: (i, 0)), pl.BlockSpec((k, d), lambda i: (0, 0)), row, vec, vec] + [vec] * n_prev,
        out_specs=[row, row],
        out_shape=[jax.ShapeDtypeStruct((t, d), F32), jax.ShapeDtypeStruct((t, d), BF16)],
        sem=("parallel",), args=(a, b, x, g, beta, *(prev or ())), comm=comm)
    return outs if comm is None else (outs, moved)


def _matmul_ln_bwd(a, b, z, g, dres, name, tm, *, res=None, b_off=0):
    m, k = a.shape
    d = b.shape[1]
    tm = min(tm, m)
    assert m % tm == 0

    def body(*refs):
        a_ref, b_ref, z_ref, g_ref, dres_ref = refs[:5]
        dz_ref, dg_ref, db_ref = refs[-3:]

        @pl.when(pl.program_id(0) == 0)
        def _():
            dg_ref[...] = jnp.zeros_like(dg_ref)
            db_ref[...] = jnp.zeros_like(db_ref)

        dbr = _dot(a_ref[...], b_ref[...], NN)
        if res is not None:
            dbr = dbr + refs[5][...]
        nh, r = _ln_stats(z_ref[...])
        dy = ALPHA * dres_ref[...] + dbr
        dg_ref[...] += _colsum(dy * nh)
        db_ref[...] += _colsum(dy)
        dz_ref[...] = _ln_bwd_rows(dy * g_ref[...], nh, r)

    row = pl.BlockSpec((tm, d), lambda i: (i, 0))
    vec = pl.BlockSpec((1, d), lambda i: (0, 0))
    vshape = jax.ShapeDtypeStruct((1, d), F32)
    return pl.pallas_call(
        body, name=name, grid=(m // tm,),
        in_specs=[pl.BlockSpec((tm, k), lambda i: (i, 0)), pl.BlockSpec((k, d), lambda i: (b_off, 0)), row, vec, row]
        + ([row] if res is not None else []),
        out_specs=[row, vec, vec], out_shape=[jax.ShapeDtypeStruct((m, d), F32), vshape, vshape],
        compiler_params=_params("arbitrary"),
    )(a, b, z, g, dres, *([res] if res is not None else []))


def _matmul_res_ln_loss(a, b, x, g, beta, target, name, tm, prev):
    t, k = a.shape
    d = b.shape[1]
    tm = min(tm, t)

    def body(a_ref, b_ref, x_ref, g_ref, beta_ref, t_ref, gp_ref, bp_ref, dz_ref, dg_ref, db_ref, loss_ref):
        @pl.when(pl.program_id(0) == 0)
        def _():
            dg_ref[...] = jnp.zeros_like(dg_ref)
            db_ref[...] = jnp.zeros_like(db_ref)
            loss_ref[...] = jnp.zeros_like(loss_ref)

        nh, r = _ln_stats(ALPHA * _residual_input(x_ref, (gp_ref, bp_ref)) + _dot(a_ref[...], b_ref[...], NN))
        err = nh * g_ref[...] + beta_ref[...] - t_ref[...]
        loss_ref[...] += _colsum(err * err)
        dy = err * (1.0 / d)
        dg_ref[...] += _colsum(dy * nh)
        db_ref[...] += _colsum(dy)
        dz_ref[...] = _ln_bwd_rows(dy * g_ref[...], nh, r)

    row = pl.BlockSpec((tm, d), lambda i: (i, 0))
    vec = pl.BlockSpec((1, d), lambda i: (0, 0))
    vshape = jax.ShapeDtypeStruct((1, d), F32)
    return pl.pallas_call(
        body, name=name, grid=(t // tm,),
        in_specs=[pl.BlockSpec((tm, k), lambda i: (i, 0)), pl.BlockSpec((k, d), lambda i: (0, 0)), row, vec, vec, row, vec, vec],
        out_specs=[row, vec, vec, vec],
        out_shape=[jax.ShapeDtypeStruct((t, d), F32), vshape, vshape, vshape],
        compiler_params=_params("arbitrary"),
    )(a, b, x, g, beta, target, *prev)


FFN_HALO = 16
FFN_CHUNK = 256
LANES = 128
SUBLANES = 8


def _rows_up(e, start, rows):
    if start % SUBLANES == 0:
        return e[start:start + rows]
    return pltpu.roll(e, e.shape[0] - start, 0)[0:rows]


def _fold(x):
    return jnp.sum(x.reshape(x.shape[0] // SUBLANES, SUBLANES, x.shape[1]), axis=0)


def _ffn_mid_fwd(h, cw, cb, name, tm=1024, tc=256, comm=None):
    t, f2 = h.shape
    tm = min(tm, t)
    f = f2 // 2
    nj, nt, hb = f // tc, t // tm, tm // FFN_HALO

    ch = min(FFN_CHUNK, tm)

    def body(hg, hgp, hv, hvp, cwg, cwv, cbg, cbv, u_ref, sg, sv):
        i = pl.program_id(1)
        for main, prev, s in ((hg, hgp, sg), (hv, hvp, sv)):
            s[0:FFN_HALO, :] = jnp.where(i > 0, prev[...].astype(F32), 0.0)
            s[FFN_HALO:, :] = main[...].astype(F32)
        o = SUBLANES - FFN_KERNEL + 1
        for lg in range(tc // LANES):
            cols = slice(lg * LANES, (lg + 1) * LANES)
            wg, wv = [cwg[k:k + 1, cols] for k in range(FFN_KERNEL)], [cwv[k:k + 1, cols] for k in range(FFN_KERNEL)]
            bg, bv = cbg[:, cols], cbv[:, cols]

            def chunk(c, carry):
                base = pl.multiple_of(c * ch, ch)
                eg = sg[pl.ds(base + FFN_HALO - SUBLANES, ch + SUBLANES), cols]
                ev = sv[pl.ds(base + FFN_HALO - SUBLANES, ch + SUBLANES), cols]
                cg = wg[0] * _rows_up(eg, o, ch) + wg[1] * _rows_up(eg, o + 1, ch) + wg[2] * _rows_up(eg, o + 2, ch) + bg
                cv = wv[0] * _rows_up(ev, o, ch) + wv[1] * _rows_up(ev, o + 1, ch) + wv[2] * _rows_up(ev, o + 2, ch) + bv
                u_ref[pl.ds(base, ch), cols] = (_gelu(cg) * cv).astype(BF16)
                return carry

            lax.fori_loop(0, tm // ch, chunk, 0)

    def main_spec(off):
        return pl.BlockSpec((tm, tc), lambda j, i: (i, j + off))

    def prev_spec(off):
        return pl.BlockSpec((FFN_HALO, tc), lambda j, i: (jnp.maximum(i * hb - 1, 0), j + off))

    def par_spec(rows, off):
        return pl.BlockSpec((rows, tc), lambda j, i: (0, j + off))

    (u,), moved = _call(
        body, name=name, grid=(nj, nt),
        in_specs=[main_spec(0), prev_spec(0), main_spec(nj), prev_spec(nj),
                  par_spec(FFN_KERNEL, 0), par_spec(FFN_KERNEL, nj), par_spec(1, 0), par_spec(1, nj)],
        out_specs=[pl.BlockSpec((tm, tc), lambda j, i: (i, j))],
        out_shape=[jax.ShapeDtypeStruct((t, f), BF16)],
        scratch_shapes=[pltpu.VMEM((tm + FFN_HALO, tc), F32), pltpu.VMEM((tm + FFN_HALO, tc), F32)],
        sem=("parallel", "arbitrary"), args=(h, h, h, h, cw, cw, cb, cb), comm=comm)
    return u if comm is None else (u, moved)


def _ffn_mid_bwd(h, du, cw, cb, name, tm=1024, tc=256, comm=None):
    t, f2 = h.shape
    tm = min(tm, t)
    f = f2 // 2
    nj, nt, hb = f // tc, t // tm, tm // FFN_HALO

    ch = min(FFN_CHUNK, tm)
    ahead = ch + SUBLANES

    def body(hg, hgp, hgn, hv, hvp, hvn, du_ref, dun_ref, cwg, cwv, cbg, cbv,
             dhg_ref, dhv_ref, dcwg_ref, dcwv_ref, dcbg_ref, dcbv_ref, sg, sv, sdu):
        i = pl.program_id(1)

        @pl.when(i == 0)
        def _():
            for ref in (dcwg_ref, dcwv_ref, dcbg_ref, dcbv_ref):
                ref[...] = jnp.zeros_like(ref)

        for main, prev, nxt, s in ((hg, hgp, hgn, sg), (hv, hvp, hvn, sv)):
            s[0:FFN_HALO, :] = jnp.where(i > 0, prev[...].astype(F32), 0.0)
            s[FFN_HALO:FFN_HALO + tm, :] = main[...].astype(F32)
            s[FFN_HALO + tm:, :] = nxt[...].astype(F32)
        sdu[0:tm, :] = du_ref[...].astype(F32)
        sdu[tm:, :] = jnp.where(i < nt - 1, dun_ref[...].astype(F32), 0.0)
        o = SUBLANES - FFN_KERNEL + 1
        for lg in range(tc // LANES):
            cols = slice(lg * LANES, (lg + 1) * LANES)
            wg, wv = [cwg[k:k + 1, cols] for k in range(FFN_KERNEL)], [cwv[k:k + 1, cols] for k in range(FFN_KERNEL)]
            bg, bv = cbg[:, cols], cbv[:, cols]

            def chunk(c, acc):
                base = pl.multiple_of(c * ch, ch)
                eg = sg[pl.ds(base + FFN_HALO - SUBLANES, ahead + SUBLANES), cols]
                ev = sv[pl.ds(base + FFN_HALO - SUBLANES, ahead + SUBLANES), cols]
                hgs = [_rows_up(eg, o + k, ahead) for k in range(FFN_KERNEL)]
                hvs = [_rows_up(ev, o + k, ahead) for k in range(FFN_KERNEL)]
                cg = wg[0] * hgs[0] + wg[1] * hgs[1] + wg[2] * hgs[2] + bg
                cv = wv[0] * hvs[0] + wv[1] * hvs[1] + wv[2] * hvs[2] + bv
                du_e = sdu[pl.ds(base, ahead), cols]
                gl, dgl = _gelu_and_grad(cg)

                def back(d, hs, w, dh_ref):
                    own = d[0:ch]
                    dh = w[2] * own + w[1] * _rows_up(d, 1, ch) + w[0] * _rows_up(d, 2, ch)
                    dh_ref[pl.ds(base, ch), cols] = dh.astype(BF16)
                    return [_fold(own)] + [_fold(own * hs[k][0:ch]) for k in range(FFN_KERNEL)]

                sums = back(du_e * cv * dgl, hgs, wg, dhg_ref) + back(du_e * gl, hvs, wv, dhv_ref)
                return tuple(a + s_ for a, s_ in zip(acc, sums))

            zero = jnp.zeros((SUBLANES, LANES), F32)
            acc = lax.fori_loop(0, tm // ch, chunk, (zero,) * (2 * (1 + FFN_KERNEL)))
            dcbg_ref[:, cols] += _colsum(acc[0])
            dcbv_ref[:, cols] += _colsum(acc[1 + FFN_KERNEL])
            for k in range(FFN_KERNEL):
                dcwg_ref[k:k + 1, cols] += _colsum(acc[1 + k])
                dcwv_ref[k:k + 1, cols] += _colsum(acc[2 + FFN_KERNEL + k])

    last_blk = t // FFN_HALO - 1

    def main_spec(off):
        return pl.BlockSpec((tm, tc), lambda j, i: (i, j + off))

    def prev_spec(off):
        return pl.BlockSpec((FFN_HALO, tc), lambda j, i: (jnp.maximum(i * hb - 1, 0), j + off))

    def next_spec(off):
        return pl.BlockSpec((FFN_HALO, tc), lambda j, i: (jnp.minimum((i + 1) * hb, last_blk), j + off))

    def par_spec(rows, off):
        return pl.BlockSpec((rows, tc), lambda j, i: (0, j + off))

    out_tile = pl.BlockSpec((tm, tc), lambda j, i: (i, j))
    outs, moved = _call(
        body, name=name, grid=(nj, nt),
        in_specs=[main_spec(0), prev_spec(0), next_spec(0), main_spec(nj), prev_spec(nj), next_spec(nj),
                  main_spec(0), next_spec(0),
                  par_spec(FFN_KERNEL, 0), par_spec(FFN_KERNEL, nj), par_spec(1, 0), par_spec(1, nj)],
        out_specs=[out_tile, out_tile, par_spec(FFN_KERNEL, 0), par_spec(FFN_KERNEL, 0), par_spec(1, 0), par_spec(1, 0)],
        out_shape=[jax.ShapeDtypeStruct((t, f), BF16), jax.ShapeDtypeStruct((t, f), BF16),
                   jax.ShapeDtypeStruct((FFN_KERNEL, f), F32), jax.ShapeDtypeStruct((FFN_KERNEL, f), F32),
                   jax.ShapeDtypeStruct((1, f), F32), jax.ShapeDtypeStruct((1, f), F32)],
        scratch_shapes=[pltpu.VMEM((tm + 2 * FFN_HALO, tc), F32), pltpu.VMEM((tm + 2 * FFN_HALO, tc), F32),
                        pltpu.VMEM((tm + FFN_HALO, tc), F32)],
        sem=("parallel", "arbitrary"), args=(h, h, h, h, h, h, du, du, cw, cw, cb, cb), comm=comm)
    return outs if comm is None else (outs, moved)


MIX_HALO = 32


def _glu(hh):
    return hh[:, 0:A_WIDTH] * _sigmoid(hh[:, A_WIDTH:2 * A_WIDTH])


def _fill_row_shifts(s):
    rows = s.shape[1] - SUBLANES
    for j in range(1, SUBLANES):
        s[j, 0:rows, :] = s[0, pl.ds(j, rows), :]


def _rows_from(s, start, rows):
    j = start % SUBLANES
    return s[j, start - j:start - j + rows, :]


def _tril_mask():
    return lax.broadcasted_iota(jnp.int32, (B_CHUNK, B_CHUNK), 0) >= lax.broadcasted_iota(jnp.int32, (B_CHUNK, B_CHUNK), 1)


def _spatial_mix(q, ms_ref, sbt_ref, tm):
    mask = _tril_mask()
    ws = [jnp.where(mask, ms_ref[g], 0.0).astype(BF16) for g in range(B_GROUPS)]
    qb = q.astype(BF16)
    rows = []
    for c in range(tm // B_CHUNK):
        cols = [_dot(ws[g], qb[c * B_CHUNK:(c + 1) * B_CHUNK, g * 128:(g + 1) * 128], NN) + sbt_ref[:, g:g + 1]
                for g in range(B_GROUPS)]
        rows.append(jnp.concatenate(cols, axis=1))
    return jnp.concatenate(rows, axis=0)


def _mixer_mid_fwd(h, cw, cb, ag, ab, bg, bb, ms, sbt, name, tm=256, comm=None):
    t = h.shape[0]
    nt, hb = t // tm, tm // MIX_HALO
    o = MIX_HALO - A_KERNEL + 1

    def body(h_ref, hp_ref, cw_ref, cb_ref, ag_ref, ab_ref, bg_ref, bb_ref, ms_ref, sbt_ref, cat_ref, sp):
        i = pl.program_id(0)
        sp[0, 0:MIX_HALO, :] = jnp.where(i > 0, _glu(hp_ref[:, 0:2 * A_WIDTH].astype(F32)), 0.0)
        sp[0, MIX_HALO:, :] = _glu(h_ref[:, 0:2 * A_WIDTH].astype(F32))
        _fill_row_shifts(sp)
        y = jnp.zeros((tm, A_WIDTH), F32) + cb_ref[...]
        for k in range(A_KERNEL):
            y = y + cw_ref[k:k + 1, :] * _rows_from(sp, o + k, tm)
        nh, _ = _ln_stats(y)
        ln = nh * ag_ref[...] + ab_ref[...]
        cat_ref[:, 0:A_WIDTH] = (ln * _sigmoid(ln)).astype(BF16)
        u = _gelu(h_ref[:, 1024:1536].astype(F32))
        nb, _ = _ln_stats(_gelu(h_ref[:, 1536:2048].astype(F32)))
        mixed = _spatial_mix(nb * bg_ref[...] + bb_ref[...], ms_ref, sbt_ref, tm)
        cat_ref[:, A_WIDTH:] = (u * mixed).astype(BF16)

    vec = pl.BlockSpec((1, A_WIDTH), lambda i: (0, 0))
    (cat,), moved = _call(
        body, name=name, grid=(nt,),
        in_specs=[pl.BlockSpec((tm, 2048), lambda i: (i, 0)),
                  pl.BlockSpec((MIX_HALO, 2048), lambda i: (jnp.maximum(i * hb - 1, 0), 0)),
                  pl.BlockSpec((A_KERNEL, A_WIDTH), lambda i: (0, 0)), vec, vec, vec, vec, vec,
                  pl.BlockSpec((B_GROUPS, B_CHUNK, B_CHUNK), lambda i: (0, 0, 0)),
                  pl.BlockSpec((B_CHUNK, B_GROUPS), lambda i: (0, 0))],
        out_specs=[pl.BlockSpec((tm, D_MODEL), lambda i: (i, 0))],
        out_shape=[jax.ShapeDtypeStruct((t, D_MODEL), BF16)],
        scratch_shapes=[pltpu.VMEM((SUBLANES, tm + MIX_HALO, A_WIDTH), F32)],
        sem=("parallel",), args=(h, h, cw, cb, ag, ab, bg, bb, ms, sbt), comm=comm)
    return cat if comm is None else (cat, moved)


def _mixer_mid_bwd(h, dcat, cw, cb, ag, ab, bg, bb, ms, mst, sbt, name, tm=256, comm=None):
    t = h.shape[0]
    nt, hb = t // tm, tm // MIX_HALO
    o = MIX_HALO - A_KERNEL + 1
    r = tm + MIX_HALO
    nchunk = tm // B_CHUNK

    def body(h_ref, hp_ref, hn_ref, dc_ref, dcn_ref, cw_ref, cb_ref, ag_ref, ab_ref, bg_ref, bb_ref, ms_ref, mst_ref, sbt_ref,
             dh_ref, dcw_ref, dcb_ref, dag_ref, dab_ref, dbg_ref, dbb_ref, dms_ref, dsb_ref, sp, sdy, sbacc):
        i = pl.program_id(0)

        @pl.when(i == 0)
        def _():
            for ref in (dcw_ref, dcb_ref, dag_ref, dab_ref, dbg_ref, dbb_ref, dms_ref, dsb_ref, sbacc):
                ref[...] = jnp.zeros_like(ref)

        sp[0, 0:MIX_HALO, :] = jnp.where(i > 0, _glu(hp_ref[:, 0:2 * A_WIDTH].astype(F32)), 0.0)
        sp[0, MIX_HALO:MIX_HALO + tm, :] = _glu(h_ref[:, 0:2 * A_WIDTH].astype(F32))
        sp[0, MIX_HALO + tm:, :] = _glu(hn_ref[:, 0:2 * A_WIDTH].astype(F32))
        _fill_row_shifts(sp)
        y = jnp.zeros((r, A_WIDTH), F32) + cb_ref[...]
        for k in range(A_KERNEL):
            y = y + cw_ref[k:k + 1, :] * _rows_from(sp, o + k, r)
        nh, rs = _ln_stats(y)
        ln = nh * ag_ref[...] + ab_ref[...]
        sg = _sigmoid(ln)
        dao = jnp.concatenate([dc_ref[:, 0:A_WIDTH].astype(F32),
                               jnp.where(i < nt - 1, dcn_ref[:, 0:A_WIDTH].astype(F32), 0.0)], axis=0)
        dln = dao * (sg * (1.0 + ln * (1.0 - sg)))
        dag_ref[...] += _colsum(dln[0:tm] * nh[0:tm])
        dab_ref[...] += _colsum(dln[0:tm])
        sdy[0] = _ln_bwd_rows(dln * ag_ref[...], nh, rs)
        _fill_row_shifts(sdy)
        dy_own = sdy[0, 0:tm, :]
        dcb_ref[...] += _colsum(dy_own)
        dp = jnp.zeros((tm, A_WIDTH), F32)
        for k in range(A_KERNEL):
            dcw_ref[k:k + 1, :] += _colsum(dy_own * _rows_from(sp, o + k, tm))
            dp = dp + cw_ref[k:k + 1, :] * _rows_from(sdy, A_KERNEL - 1 - k, tm)
        av = h_ref[:, 0:A_WIDTH].astype(F32)
        s = _sigmoid(h_ref[:, A_WIDTH:2 * A_WIDTH].astype(F32))
        dh_ref[:, 0:A_WIDTH] = (dp * s).astype(BF16)
        dh_ref[:, A_WIDTH:2 * A_WIDTH] = (dp * av * s * (1.0 - s)).astype(BF16)

        u, dgu = _gelu_and_grad(h_ref[:, 1024:1536].astype(F32))
        w, dgw = _gelu_and_grad(h_ref[:, 1536:2048].astype(F32))
        nb, rb = _ln_stats(w)
        q = nb * bg_ref[...] + bb_ref[...]
        mixed = _spatial_mix(q, ms_ref, sbt_ref, tm)
        dbo = dc_ref[:, A_WIDTH:].astype(F32)
        dh_ref[:, 1024:1536] = (dbo * mixed * dgu).astype(BF16)
        dmx = dbo * u
        mask = _tril_mask()
        wst = [jnp.where(mask.T, mst_ref[g], 0.0).astype(BF16) for g in range(B_GROUPS)]
        qb = q.astype(BF16)
        dmb = dmx.astype(BF16)
        rows = []
        for c in range(nchunk):
            cols = []
            for g in range(B_GROUPS):
                rs_, cs_ = slice(c * B_CHUNK, (c + 1) * B_CHUNK), slice(g * 128, (g + 1) * 128)
                sbacc[g] += dmx[rs_, cs_]
                dms_ref[g] += _dot(dmb[rs_, cs_], qb[rs_, cs_], NT)
                cols.append(_dot(wst[g], dmb[rs_, cs_], NN))
            rows.append(jnp.concatenate(cols, axis=1))
        dq = jnp.concatenate(rows, axis=0)
        dbg_ref[...] += _colsum(dq * nb)
        dbb_ref[...] += _colsum(dq)
        dh_ref[:, 1536:2048] = (_ln_bwd_rows(dq * bg_ref[...], nb, rb) * dgw).astype(BF16)

        @pl.when(i == nt - 1)
        def _():
            for g in range(B_GROUPS):
                dms_ref[g] = jnp.where(mask, dms_ref[g], 0.0)
                dsb_ref[g] = jnp.sum(sbacc[g], axis=1, keepdims=True)

    last_blk = t // MIX_HALO - 1
    vec = pl.BlockSpec((1, A_WIDTH), lambda i: (0, 0))
    mat = pl.BlockSpec((B_GROUPS, B_CHUNK, B_CHUNK), lambda i: (0, 0, 0))
    taps = pl.BlockSpec((A_KERNEL, A_WIDTH), lambda i: (0, 0))

    def halo(width, which):
        if which == "prev":
            return pl.BlockSpec((MIX_HALO, width), lambda i: (jnp.maximum(i * hb - 1, 0), 0))
        return pl.BlockSpec((MIX_HALO, width), lambda i: (jnp.minimum((i + 1) * hb, last_blk), 0))

    vshape = jax.ShapeDtypeStruct((1, A_WIDTH), F32)
    outs, moved = _call(
        body, name=name, grid=(nt,),
        in_specs=[pl.BlockSpec((tm, 2048), lambda i: (i, 0)), halo(2048, "prev"), halo(2048, "next"),
                  pl.BlockSpec((tm, D_MODEL), lambda i: (i, 0)), halo(D_MODEL, "next"),
                  taps, vec, vec, vec, vec, vec, mat, mat, pl.BlockSpec((B_CHUNK, B_GROUPS), lambda i: (0, 0))],
        out_specs=[pl.BlockSpec((tm, 2048), lambda i: (i, 0)), taps, vec, vec, vec, vec, vec, mat,
                   pl.BlockSpec((B_GROUPS, B_CHUNK, 1), lambda i: (0, 0, 0))],
        out_shape=[jax.ShapeDtypeStruct((t, 2048), BF16), jax.ShapeDtypeStruct((A_KERNEL, A_WIDTH), F32),
                   vshape, vshape, vshape, vshape, vshape,
                   jax.ShapeDtypeStruct((B_GROUPS, B_CHUNK, B_CHUNK), F32), jax.ShapeDtypeStruct((B_GROUPS, B_CHUNK, 1), F32)],
        scratch_shapes=[pltpu.VMEM((SUBLANES, tm + 2 * MIX_HALO, A_WIDTH), F32), pltpu.VMEM((SUBLANES, r, A_WIDTH), F32),
                        pltpu.VMEM((B_GROUPS, B_CHUNK, B_CHUNK), F32)],
        sem=("arbitrary",), args=(h, h, h, dcat, dcat, cw, cb, ag, ab, bg, bb, ms, mst, sbt), comm=comm)
    return outs if comm is None else (outs, moved)


Q_WIDTH = N_Q_HEADS * HEAD_DIM
KV_WIDTH = 2 * N_KV_HEADS * HEAD_DIM
PAIRS_PER_KV = N_Q_HEADS // N_KV_HEADS // 2
ATT_SCALE = 1.0 / math.sqrt(HEAD_DIM)


def _dup_heads(pair_cols, kv_head):
    lane = lax.broadcasted_iota(jnp.int32, pair_cols.shape, 1)
    rolled = pltpu.roll(pair_cols, HEAD_DIM, 1)
    first = lane < HEAD_DIM
    return jnp.where(first, pair_cols, rolled) if kv_head == 0 else jnp.where(first, rolled, pair_cols)


HEADS_PER_KV = N_Q_HEADS // N_KV_HEADS


def _stack_heads(ref, kh):
    lane = lax.broadcasted_iota(jnp.int32, (ATT_BLOCK, 128), 1)
    rows = []
    for pr in range(PAIRS_PER_KV):
        c0 = (kh * PAIRS_PER_KV + pr) * 128
        pair = ref[:, c0:c0 + 128]
        rows += [jnp.where(lane < HEAD_DIM, pair, jnp.zeros_like(pair)), jnp.where(lane < HEAD_DIM, jnp.zeros_like(pair), pair)]
    return jnp.concatenate(rows, axis=0)


def _unstack_heads(stacked, kh, write):
    lane = lax.broadcasted_iota(jnp.int32, (ATT_BLOCK, 128), 1)
    for pr in range(PAIRS_PER_KV):
        first = stacked[(2 * pr) * ATT_BLOCK:(2 * pr + 1) * ATT_BLOCK]
        second = stacked[(2 * pr + 1) * ATT_BLOCK:(2 * pr + 2) * ATT_BLOCK]
        write((kh * PAIRS_PER_KV + pr) * 128, jnp.where(lane < HEAD_DIM, first, second))


def _sink_row(sink_ref, kh):
    return jnp.concatenate([jnp.full((1, ATT_BLOCK), sink_ref[0, kh * HEADS_PER_KV + h], F32) for h in range(HEADS_PER_KV)], axis=1)


def _att_mask_t(n):
    sj = lax.broadcasted_iota(jnp.int32, (2 * ATT_BLOCK, HEADS_PER_KV * ATT_BLOCK), 0)
    qi = lax.broadcasted_iota(jnp.int32, (2 * ATT_BLOCK, HEADS_PER_KV * ATT_BLOCK), 1) & (ATT_BLOCK - 1)
    diff = qi + ATT_BLOCK - sj
    return (diff >= 0) & (diff < ATT_BLOCK) & ((n > 0) | (sj >= ATT_BLOCK))


def _att_probs_t(q_all, k2, mask_t, sink):
    st = _dot(k2, q_all, NT) * ATT_SCALE
    st = jnp.where(mask_t, st, -jnp.inf)
    m = jnp.maximum(jnp.max(st, axis=0, keepdims=True), sink)
    e = jnp.exp(st - m)
    es = jnp.exp(sink - m)
    inv = 1.0 / (jnp.sum(e, axis=0, keepdims=True) + es)
    return e * inv, es * inv


def _attn_fwd(qkv, sinks, name, comm=None):
    t = qkv.shape[0]
    nb = t // ATT_BLOCK
    kvb = Q_WIDTH // KV_WIDTH

    def body(sink_ref, q_ref, kv_ref, kvp_ref, o_ref):
        n = pl.program_id(0)
        mask_t = _att_mask_t(n)
        kv = jnp.concatenate([kvp_ref[...], kv_ref[...]], axis=0).astype(F32)

        def write(c0, pair):
            o_ref[:, c0:c0 + 128] = pair.astype(BF16)

        for kh in range(N_KV_HEADS):
            k2 = _dup_heads(kv[:, 0:128], kh).astype(BF16)
            v2 = _dup_heads(kv[:, 128:256], kh).astype(BF16)
            pt, _ = _att_probs_t(_stack_heads(q_ref, kh), k2, mask_t, _sink_row(sink_ref, kh))
            _unstack_heads(_dot(v2, pt, TN).T, kh, write)

    (out,), moved = _call(
        body, name=name, grid=(nb,),
        in_specs=[pl.BlockSpec(memory_space=pltpu.SMEM),
                  pl.BlockSpec((ATT_BLOCK, Q_WIDTH), lambda n: (n, 0)),
                  pl.BlockSpec((ATT_BLOCK, KV_WIDTH), lambda n: (n, kvb)),
                  pl.BlockSpec((ATT_BLOCK, KV_WIDTH), lambda n: (jnp.maximum(n - 1, 0), kvb))],
        out_specs=[pl.BlockSpec((ATT_BLOCK, Q_WIDTH), lambda n: (n, 0))],
        out_shape=[jax.ShapeDtypeStruct((t, Q_WIDTH), BF16)],
        sem=("parallel",), args=(sinks, qkv, qkv, qkv), comm=comm)
    return out if comm is None else (out, moved)


def _attn_bwd(qkv, d_o, sinks, name, comm=None):
    t = qkv.shape[0]
    nb = t // ATT_BLOCK
    kvb = Q_WIDTH // KV_WIDTH

    def body(sink_ref, q_ref, kv_ref, kvp_ref, do_ref, dq_ref, dkv_ref, dbq_ref, dbkv_ref, dsink_ref, carry):
        n = pl.program_id(0)

        @pl.when(n == 0)
        def _():
            for ref in (dbq_ref, dbkv_ref, dsink_ref, carry):
                ref[...] = jnp.zeros_like(ref)
            dkv_ref[...] = jnp.zeros_like(dkv_ref)

        @pl.when(n < nb)
        def _():
            mask_t = _att_mask_t(n)
            kv = jnp.concatenate([kvp_ref[...], kv_ref[...]], axis=0).astype(F32)
            lane2 = lax.broadcasted_iota(jnp.int32, (2 * ATT_BLOCK, 128), 1)
            sink_lane = lax.broadcasted_iota(jnp.int32, (1, 128), 1)
            dsink = jnp.zeros((1, 128), F32)
            dk_parts, dv_parts = [], []

            def write(c0, pair):
                dbq_ref[:, c0:c0 + 128] += _colsum(pair)
                dq_ref[:, c0:c0 + 128] = pair.astype(BF16)

            for kh in range(N_KV_HEADS):
                k2 = _dup_heads(kv[:, 0:128], kh).astype(BF16)
                v2 = _dup_heads(kv[:, 128:256], kh).astype(BF16)
                q_all = _stack_heads(q_ref, kh)
                do_all = _stack_heads(do_ref, kh)
                pt, ps = _att_probs_t(q_all, k2, mask_t, _sink_row(sink_ref, kh))
                dpt = _dot(v2, do_all, NT)
                delta = jnp.sum(pt * dpt, axis=0, keepdims=True)
                dst = pt * (dpt - delta) * ATT_SCALE
                psd = ps * delta
                for h in range(HEADS_PER_KV):
                    dsink = dsink + jnp.where(sink_lane == kh * HEADS_PER_KV + h,
                                              -jnp.sum(psd[:, h * ATT_BLOCK:(h + 1) * ATT_BLOCK]), 0.0)
                _unstack_heads(_dot(k2, dst, TN).T, kh, write)
                dk_acc = _dot(dst, q_all, NN)
                dv_acc = _dot(pt, do_all, NN)
                dk_parts.append(dk_acc + pltpu.roll(dk_acc, HEAD_DIM, 1))
                dv_parts.append(dv_acc + pltpu.roll(dv_acc, HEAD_DIM, 1))
            dk = jnp.where(lane2 < HEAD_DIM, dk_parts[0], dk_parts[1])
            dv = jnp.where(lane2 < HEAD_DIM, dv_parts[0], dv_parts[1])
            dkv_new = jnp.concatenate([dk, dv], axis=1)
            done = carry[...] + dkv_new[0:ATT_BLOCK]

            @pl.when(n > 0)
            def _():
                dkv_ref[...] = done.astype(BF16)
                dbkv_ref[...] += _colsum(done)

            carry[...] = dkv_new[ATT_BLOCK:]
            dsink_ref[...] += dsink

        @pl.when(n == nb)
        def _():
            dkv_ref[...] = carry[...].astype(BF16)
            dbkv_ref[...] += _colsum(carry[...])

    def clamp(n):
        return jnp.minimum(n, nb - 1)

    outs, moved = _call(
        body, name=name, grid=(nb + 1,),
        in_specs=[pl.BlockSpec(memory_space=pltpu.SMEM),
                  pl.BlockSpec((ATT_BLOCK, Q_WIDTH), lambda n: (clamp(n), 0)),
                  pl.BlockSpec((ATT_BLOCK, KV_WIDTH), lambda n: (clamp(n), kvb)),
                  pl.BlockSpec((ATT_BLOCK, KV_WIDTH), lambda n: (jnp.maximum(clamp(n) - 1, 0), kvb)),
                  pl.BlockSpec((ATT_BLOCK, Q_WIDTH), lambda n: (clamp(n), 0))],
        out_specs=[pl.BlockSpec((ATT_BLOCK, Q_WIDTH), lambda n: (clamp(n), 0)),
                   pl.BlockSpec((ATT_BLOCK, KV_WIDTH), lambda n: (jnp.maximum(n - 1, 0), 0)),
                   pl.BlockSpec((1, Q_WIDTH), lambda n: (0, 0)),
                   pl.BlockSpec((1, KV_WIDTH), lambda n: (0, 0)),
                   pl.BlockSpec((1, 128), lambda n: (0, 0))],
        out_shape=[jax.ShapeDtypeStruct((t, Q_WIDTH), BF16), jax.ShapeDtypeStruct((t, KV_WIDTH), BF16),
                   jax.ShapeDtypeStruct((1, Q_WIDTH), F32), jax.ShapeDtypeStruct((1, KV_WIDTH), F32),
                   jax.ShapeDtypeStruct((1, 128), F32)],
        scratch_shapes=[pltpu.VMEM((ATT_BLOCK, KV_WIDTH), F32)],
        sem=("arbitrary",), args=(sinks, qkv, qkv, qkv, d_o), comm=comm)
    return outs if comm is None else (outs, moved)


def _adamw_math(g, w, m, v):
    m = ADAM_B1 * m + (1.0 - ADAM_B1) * g
    v = ADAM_B2 * v + (1.0 - ADAM_B2) * (g * g)
    m_hat = m / (1.0 - ADAM_B1 ** ADAM_STEP)
    v_hat = v / (1.0 - ADAM_B2 ** ADAM_STEP)
    delta = -ADAM_LR * (m_hat / (jnp.sqrt(v_hat) + ADAM_EPS) + ADAM_WD * w)
    return delta, m, v


def _sum_partials(p_ref):
    g = p_ref[0].astype(F32)
    for s in range(1, N_DEV):
        g = g + p_ref[s].astype(F32)
    return g


def _adamw_big(parts, w, m, v, name, tr):
    r, c = w.shape
    parts = [p if isinstance(p, tuple) else (p, 0, p.shape[1]) for p in parts]
    tiles = [rows // tr for _, _, rows in parts]
    starts = [sum(tiles[:l]) for l in range(len(parts))]
    assert all(lo % tr == 0 and rows % tr == 0 for _, lo, rows in parts) and sum(tiles) * tr == r

    def body(*refs):
        p_refs, (w_ref, m_ref, v_ref, g_out, d_out, m_out, v_out) = refs[:len(parts)], refs[len(parts):]
        i = pl.program_id(0)
        for l, p_ref in enumerate(p_refs):
            @pl.when((i >= starts[l]) & (i < starts[l] + tiles[l]))
            def _():
                g = _sum_partials(p_ref)
                g_out[...] = g
                d_out[...], m_out[...], v_out[...] = _adamw_math(g, w_ref[...], m_ref[...], v_ref[...])

    def part_spec(l):
        return pl.BlockSpec((N_DEV, tr, c), lambda i: (0, jnp.clip(i - starts[l], 0, tiles[l] - 1) + parts[l][1] // tr, 0))

    tile = pl.BlockSpec((tr, c), lambda i: (i, 0))
    shape = jax.ShapeDtypeStruct((r, c), F32)
    return pl.pallas_call(
        body, name=name, grid=(r // tr,),
        in_specs=[part_spec(l) for l in range(len(parts))] + [tile, tile, tile],
        out_specs=[tile] * 4, out_shape=[shape] * 4,
        compiler_params=_params("parallel"),
    )(*[p[0] for p in parts], w, m, v)


def _adamw_small(parts, ws, ms, vs, name):
    n = len(ws)

    def body(*refs):
        ins, outs = refs[:4 * n], refs[4 * n:]
        for a in range(n):
            g = _sum_partials(ins[a])
            outs[4 * a][...] = g
            outs[4 * a + 1][...], outs[4 * a + 2][...], outs[4 * a + 3][...] = _adamw_math(
                g, ins[n + a][...], ins[2 * n + a][...], ins[3 * n + a][...])

    out_shape = []
    for w in ws:
        out_shape += [jax.ShapeDtypeStruct(w.shape, F32)] * 4
    return pl.pallas_call(body, name=name, out_shape=out_shape, compiler_params=_params())(*parts, *ws, *ms, *vs)


PACK_LANES = 128
PACK_ROWS = 8


def _pack(arrs):
    flat = jnp.concatenate([a.reshape(-1).astype(F32) for a in arrs])
    unit = PACK_LANES * PACK_ROWS
    total = -(-flat.shape[0] // unit) * unit
    return jnp.pad(flat, (0, total - flat.shape[0])).reshape(-1, PACK_LANES)


def _unpack(buf, shapes):
    flat = buf.reshape(N_DEV, -1)
    out, pos = [], 0
    for s in shapes:
        size = math.prod(s)
        out.append(flat[:, pos:pos + size].reshape((N_DEV,) + tuple(s)))
        pos += size
    return out


def _interleave(g):
    return jnp.transpose(g, (1, 0, 2)).reshape(g.shape[1], -1)


def _deinterleave(w):
    r = w.shape[0]
    return jnp.transpose(w.reshape(r, N_DEV, -1), (1, 0, 2))


def _ffn_backward(dz, x_in, z_in, g_in, h, u, w_up_t, cw, cb, w_down, tag, exchange=(), exchange_late=()):
    du = _matmul(dz, w_down, "nt", BF16, f"ffn{tag}_du", 1024, 1408, 1024)
    d_w_down = _matmul(u, dz, "tn", BF16, f"ffn{tag}_dwdown", 1408, 1024, 2048)
    (dhg, dhv, dcwg, dcwv, dcbg, dcbv), moved = _ffn_mid_bwd(
        h, du, cw, cb, f"ffn{tag}_mid_bwd", comm=_Comm(exchange=[d_w_down.reshape(N_DEV, -1, D_MODEL), *exchange]))
    d_w_up_t = _matmul_tn_pair(dhg, dhv, x_in, BF16, f"ffn{tag}_dwup", 1408, 1024, 1024,
                               comm=_Comm(exchange=exchange_late) if exchange_late else None)
    if exchange_late:
        d_w_up_t, late = d_w_up_t
        moved = moved + late
    dx = _matmul(dhv, w_up_t, "nn", F32, f"ffn{tag}_dx_value", 1024, 1024, D_FF, b_off=1)
    dz_in, dg_in, db_in = _matmul_ln_bwd(dhg, w_up_t, z_in, g_in, dz, f"ffn{tag}_dx_gate_ln_bwd", 512, res=dx)
    return (dz_in, dg_in, db_in, d_w_up_t.reshape(N_DEV, -1, D_MODEL),
            jnp.concatenate([dcwg, dcwv], axis=1), jnp.concatenate([dcbg, dcbv], axis=1), moved)


def kernel(x, ab_w_in, a_conv_w, a_conv_b, a_norm_g, a_norm_b, b_norm_g, b_norm_b, b_spatial_w, b_spatial_b, ab_w_out, c_w_qkv, c_b_qkv, c_sinks, c_w_o, ffn_w_up, ffn_conv_w, ffn_conv_b, ffn_w_down, ln_g, ln_b, loss_target, m_ab_w_in, m_a_conv_w, m_a_conv_b, m_a_norm_g, m_a_norm_b, m_b_norm_g, m_b_norm_b, m_b_spatial_w, m_b_spatial_b, m_ab_w_out, m_c_w_qkv, m_c_b_qkv, m_c_sinks, m_c_w_o, m_ffn_w_up, m_ffn_conv_w, m_ffn_conv_b, m_ffn_w_down, m_ln_g, m_ln_b, v_ab_w_in, v_a_conv_w, v_a_conv_b, v_a_norm_g, v_a_norm_b, v_b_norm_g, v_b_norm_b, v_b_spatial_w, v_b_spatial_b, v_ab_w_out, v_c_w_qkv, v_c_b_qkv, v_c_sinks, v_c_w_o, v_ffn_w_up, v_ffn_conv_w, v_ffn_conv_b, v_ffn_w_down, v_ln_g, v_ln_b):
    me = 4 * lax.axis_index("x") + 2 * lax.axis_index("y") + lax.axis_index("c")
    xt = x[0]
    t = xt.shape[0]

    small_shard_shapes = [a_conv_w.shape, c_b_qkv.shape, ffn_conv_w.shape, ln_g.shape, ln_b.shape]
    up_shard = [jnp.swapaxes(ffn_w_up[l], 0, 1).astype(BF16) for l in range(2)]
    qkv_shard = jnp.swapaxes(c_w_qkv[0], 0, 1).astype(BF16)
    down_shard = [ffn_w_down[l].astype(BF16) for l in range(2)]
    g_win, g_small = _comm_only(
        _Comm(gather=[ab_w_in[0].astype(BF16), _pack([a_conv_w, c_b_qkv, ffn_conv_w, ln_g, ln_b])]), "gather_first")
    w_in = _interleave(g_win)
    g_acw, g_bqkv, g_fcw, g_lng, g_lnb = _unpack(g_small, small_shard_shapes)
    acw = _interleave(g_acw[:, 0])
    bqkv = g_bqkv[:, 0].reshape(1, -1)
    fcw = [_interleave(g_fcw[:, l]) for l in range(2)]
    lng = jnp.transpose(g_lng, (1, 2, 0, 3)).reshape(2, 2, 1, D_MODEL)
    lnb = jnp.transpose(g_lnb, (1, 2, 0, 3)).reshape(2, 2, 1, D_MODEL)
    fcb = [ffn_conv_b[l:l + 1] for l in range(2)]
    ms = b_spatial_w[0]
    mst = jnp.swapaxes(ms, 1, 2)
    sbt = b_spatial_b[0].T

    h0, (g_wout,) = _matmul(xt, w_in, "nn", BF16, "mix_in", 1024, 1024, 1024, comm=_Comm(gather=[ab_w_out[0].astype(BF16)]))
    w_out = g_wout.reshape(D_MODEL, D_MODEL)
    cat, (g_wup0,) = _mixer_mid_fwd(h0, acw, a_conv_b, a_norm_g, a_norm_b, b_norm_g, b_norm_b, ms, sbt, "mix_mid_fwd",
                                    comm=_Comm(gather=[up_shard[0]]))
    w_up0 = g_wup0.reshape(2 * D_FF, D_MODEL)
    z1, x1 = _matmul_res_ln(cat, w_out, xt, lng[0, 0], lnb[0, 0], "mix_out_ln", 512)
    hf0, (g_wdown0, g_wqkv) = _matmul(x1, w_up0, "nt", BF16, "ffn0_up", 1024, 1408, 1024,
                                      comm=_Comm(gather=[down_shard[0], qkv_shard]))
    w_down0 = g_wdown0.reshape(D_FF, D_MODEL)
    w_qkv = g_wqkv.reshape(Q_WIDTH + KV_WIDTH, D_MODEL)
    u0, (g_wup1,) = _ffn_mid_fwd(hf0, fcw[0], fcb[0], "ffn0_mid_fwd", comm=_Comm(gather=[up_shard[1]]))
    w_up1 = g_wup1.reshape(2 * D_FF, D_MODEL)
    (z2, x2), (g_wo,) = _matmul_res_ln(u0, w_down0, z1, lng[0, 1], lnb[0, 1], "ffn0_down_ln", 512, prev=(lng[0, 0], lnb[0, 0]),
                                       comm=_Comm(gather=[c_w_o[0].astype(BF16)]))
    w_o = g_wo.reshape(D_MODEL, D_MODEL)
    qkv = _matmul(x2, w_qkv, "nt", BF16, "att_qkv", 1024, 1280, 1024, bias=bqkv)
    att, (g_wdown1,) = _attn_fwd(qkv, c_sinks, "att_fwd", comm=_Comm(gather=[down_shard[1]]))
    w_down1 = g_wdown1.reshape(D_FF, D_MODEL)
    z3, x3 = _matmul_res_ln(att, w_o, z2, lng[1, 0], lnb[1, 0], "att_out_ln", 512, prev=(lng[0, 1], lnb[0, 1]))
    hf1 = _matmul(x3, w_up1, "nt", BF16, "ffn1_up", 1024, 1408, 1024)
    u1 = _ffn_mid_fwd(hf1, fcw[1], fcb[1], "ffn1_mid_fwd")

    dz4, dg11, db11, loss_terms = _matmul_res_ln_loss(u1, w_down1, z3, lng[1, 1], lnb[1, 1], loss_target[0],
                                                      "ffn1_down_ln_loss", 512, prev=(lng[1, 0], lnb[1, 0]))
    dz3, dg10, db10, d_wup1, d_fcw1, d_fcb1, (p_wdown1,) = _ffn_backward(
        dz4, x3, z3, lng[1, 0], hf1, u1, w_up1, fcw[1], fcb[1], w_down1, 1)
    d_att = _matmul(dz3, w_o, "nt", BF16, "att_dout", 1024, 1024, 1024)
    d_wo = _matmul(att, dz3, "tn", BF16, "att_dwo", 1024, 1024, 512)
    half = d_wup1.shape[1] // 2
    (dq, dkv, dbq, dbkv, dsinks), (p_wup1a,) = _attn_bwd(qkv, d_att, c_sinks, "att_bwd",
                                                        comm=_Comm(exchange=[(d_wup1, 0, half)]))
    d_wqkv = jnp.concatenate([_matmul(dq, x2, "tn", BF16, "att_dwq", 1024, 1024, 1024),
                              _matmul(dkv, x2, "tn", BF16, "att_dwkv", KV_WIDTH, 1024, 1024)], axis=0)
    dx2 = _matmul(dkv, w_qkv, "nn", F32, "att_dx_kv", 1024, 1024, KV_WIDTH, b_off=Q_WIDTH // KV_WIDTH)
    dz2, dg01, db01 = _matmul_ln_bwd(dq, w_qkv, z2, lng[0, 1], dz3, "att_dx_q_ln_bwd", 512, res=dx2)
    dz1, dg00, db00, d_wup0, d_fcw0, d_fcb0, (p_wdown0, p_wup1b, p_wqkv, p_wo) = _ffn_backward(
        dz2, x1, z1, lng[0, 0], hf0, u0, w_up0, fcw[0], fcb[0], w_down0, 0, exchange=[(d_wup1, half, half)],
        exchange_late=[d_wqkv.reshape(N_DEV, -1, D_MODEL), d_wo.reshape(N_DEV, -1, D_MODEL)])
    dcat = _matmul(dz1, w_out, "nt", BF16, "mix_dcat", 1024, 1024, 1024)
    d_wout = _matmul(cat, dz1, "tn", BF16, "mix_dwout", 1024, 1024, 512)
    (dh0, d_acw, d_acb, d_ang, d_anb, d_bng, d_bnb, d_ms, d_sb), (p_wup0, p_wout) = _mixer_mid_bwd(
        h0, dcat, acw, a_conv_b, a_norm_g, a_norm_b, b_norm_g, b_norm_b, ms, mst, sbt, "mix_mid_bwd",
        comm=_Comm(exchange=[d_wup0, d_wout.reshape(N_DEV, -1, D_MODEL)]))
    d_bqkv = jnp.concatenate([dbq, dbkv], axis=1)
    d_lng = jnp.stack([jnp.stack([dg00, dg01]), jnp.stack([dg10, dg11])])
    d_lnb = jnp.stack([jnp.stack([db00, db01]), jnp.stack([db10, db11])])
    small_full = [d_acb, d_ang, d_anb, d_bng, d_bnb, d_ms, d_sb, dsinks[:, :N_Q_HEADS], jnp.concatenate([d_fcb0, d_fcb1], axis=0),
                  d_acw, d_bqkv, jnp.stack([d_fcw0, d_fcw1]), d_lng, d_lnb]
    d_win, (g_small_grads,) = _matmul(xt, dh0, "tn", BF16, "mix_dwin", 1024, 1024, 512, comm=_Comm(gather=[_pack(small_full)]))
    grad_x, (p_win,) = _matmul(dh0, w_in, "nt", F32, "mix_dx", 1024, 1024, 1024, res=dz1, res_scale=ALPHA,
                               comm=_Comm(exchange=[_deinterleave(d_win)]))

    loss = lax.psum(0.5 / D_MODEL * jnp.sum(loss_terms), ("x", "y", "c"))

    big = {}
    for nm, p, w, m, v, tr, transposed in [
            ("ab_w_in", [p_win], ab_w_in, m_ab_w_in, v_ab_w_in, 256, False),
            ("ab_w_out", [p_wout], ab_w_out, m_ab_w_out, v_ab_w_out, 128, False),
            ("c_w_qkv", [p_wqkv], c_w_qkv, m_c_w_qkv, v_c_w_qkv, 160, True), ("c_w_o", [p_wo], c_w_o, m_c_w_o, v_c_w_o, 128, False),
            ("ffn_w_up", [p_wup0, (p_wup1a, 0, half), (p_wup1b, half, half)], ffn_w_up, m_ffn_w_up, v_ffn_w_up, 176, True),
            ("ffn_w_down", [p_wdown0, p_wdown1], ffn_w_down, m_ffn_w_down, v_ffn_w_down, 176, False)]:
        def two_d(a):
            a = jnp.swapaxes(a, 1, 2) if transposed else a
            return a.reshape(-1, a.shape[-1])

        def back(o):
            return jnp.swapaxes(o.reshape(w.shape[0], w.shape[2], w.shape[1]), 1, 2) if transposed else o.reshape(w.shape)

        outs = _adamw_big(p, two_d(w), two_d(m), two_d(v), "adamw_" + nm, tr)
        big[nm] = [back(o) for o in outs]

    gs = _unpack(g_small_grads, [a.shape for a in small_full])

    def my_shard(g, width):
        g = g.reshape(g.shape[:-1] + (N_DEV, width))
        return lax.dynamic_index_in_dim(g, me, axis=g.ndim - 2, keepdims=False)

    small_names = ["a_conv_b", "a_norm_g", "a_norm_b", "b_norm_g", "b_norm_b", "b_spatial_w", "b_spatial_b", "c_sinks", "ffn_conv_b",
                   "a_conv_w", "c_b_qkv", "ffn_conv_w", "ln_g", "ln_b"]
    small_w = [a_conv_b, a_norm_g, a_norm_b, b_norm_g, b_norm_b, b_spatial_w, b_spatial_b, c_sinks, ffn_conv_b,
               a_conv_w, c_b_qkv, ffn_conv_w, ln_g, ln_b]
    small_m = [m_a_conv_b, m_a_norm_g, m_a_norm_b, m_b_norm_g, m_b_norm_b, m_b_spatial_w, m_b_spatial_b, m_c_sinks, m_ffn_conv_b,
               m_a_conv_w, m_c_b_qkv, m_ffn_conv_w, m_ln_g, m_ln_b]
    small_v = [v_a_conv_b, v_a_norm_g, v_a_norm_b, v_b_norm_g, v_b_norm_b, v_b_spatial_w, v_b_spatial_b, v_c_sinks, v_ffn_conv_b,
               v_a_conv_w, v_c_b_qkv, v_ffn_conv_w, v_ln_g, v_ln_b]
    gs[9:] = [my_shard(g, w.shape[-1]) for g, w in zip(gs[9:], small_w[9:])]
    two_d = [(-1, w.shape[-1]) for w in small_w]
    outs = _adamw_small([g.reshape((N_DEV,) + w.reshape(s).shape) for g, w, s in zip(gs, small_w, two_d)],
                        [w.reshape(s) for w, s in zip(small_w, two_d)], [m.reshape(s) for m, s in zip(small_m, two_d)],
                        [v.reshape(s) for v, s in zip(small_v, two_d)], "adamw_small")
    small = {nm: [o.reshape(w.shape) for o in outs[4 * a:4 * a + 4]] for a, (nm, w) in enumerate(zip(small_names, small_w))}

    res = {**big, **small}
    order = ["ab_w_in", "a_conv_w", "a_conv_b", "a_norm_g", "a_norm_b", "b_norm_g", "b_norm_b", "b_spatial_w", "b_spatial_b", "ab_w_out",
             "c_w_qkv", "c_b_qkv", "c_sinks", "c_w_o", "ffn_w_up", "ffn_conv_w", "ffn_conv_b", "ffn_w_down", "ln_g", "ln_b"]
    return (loss, grad_x[None], *[res[nm][0] for nm in order], *[res[nm][1] for nm in order],
            *[res[nm][2] for nm in order], *[res[nm][3] for nm in order])
```

```python
import functools
import math

import jax
import jax.numpy as jnp
from jax import lax
from jax.experimental import pallas as pl
from jax.experimental.pallas import tpu as pltpu

F32 = jnp.float32
BF16 = jnp.bfloat16

N_DEV = 8
D_MODEL = 1024
A_WIDTH = 512
A_KERNEL = 31
B_GROUPS = 4
B_CHUNK = 128
HEAD_DIM = 64
N_Q_HEADS = 16
N_KV_HEADS = 2
ATT_BLOCK = 128
D_FF = 2816
FFN_KERNEL = 3
ALPHA = (2.0 * 2) ** 0.25
LN_EPS = 1e-5
GELU_K = math.sqrt(2.0 / math.pi)
GELU_C = 0.044715
ADAM_LR = 0.001
ADAM_B1 = 0.9
ADAM_B2 = 0.999
ADAM_EPS = 1e-08
ADAM_WD = 0.01
ADAM_STEP = 10
VMEM_LIMIT = 56 * 1024 * 1024
MESH_ID = pl.DeviceIdType.MESH


def _params(*sem):
    return pltpu.CompilerParams(dimension_semantics=sem, vmem_limit_bytes=VMEM_LIMIT)


def _gelu(x):
    t = jnp.tanh(GELU_K * x * (1.0 + GELU_C * x * x))
    return 0.5 * x * (1.0 + t)


def _gelu_and_grad(x):
    x2 = x * x
    t = jnp.tanh(GELU_K * x * (1.0 + GELU_C * x2))
    g = 0.5 * x * (1.0 + t)
    dg = 0.5 * (1.0 + t) + 0.5 * x * (1.0 - t * t) * (GELU_K * (1.0 + 3.0 * GELU_C * x2))
    return g, dg


def _sigmoid(x):
    return 1.0 / (1.0 + jnp.exp(-x))


def _ln_stats(z):
    mu = jnp.mean(z, axis=-1, keepdims=True)
    zc = z - mu
    var = jnp.mean(zc * zc, axis=-1, keepdims=True)
    r = lax.rsqrt(var + LN_EPS)
    return zc * r, r


def _ln_bwd_rows(dn, nh, r):
    return r * (dn - jnp.mean(dn, axis=-1, keepdims=True) - nh * jnp.mean(dn * nh, axis=-1, keepdims=True))


def _colsum(x):
    return jnp.sum(x, axis=0, keepdims=True)


def _dot(a, b, dims):
    return lax.dot_general(a.astype(BF16), b.astype(BF16), (dims, ((), ())), preferred_element_type=F32)


NN = ((1,), (0,))
NT = ((1,), (1,))
TN = ((0,), (0,))


ANY = pl.BlockSpec(memory_space=pl.ANY)
N_RELATIONS = N_DEV - 1


def _my_place():
    return lax.axis_index("x"), lax.axis_index("y"), lax.axis_index("c")


class _Comm:
    def __init__(self, gather=(), exchange=()):
        exchange = [e if isinstance(e, tuple) else (e, 0, e.shape[1]) for e in exchange]
        self.arrs = list(gather) + [e[0] for e in exchange]
        self.n_gather = len(gather)
        self.n = len(self.arrs)
        self.rows = [None] * self.n_gather + [pl.ds(lo, n) for _, lo, n in exchange]

    def out_shape(self):
        return [jax.ShapeDtypeStruct(((N_DEV,) + a.shape) if i < self.n_gather else a.shape, a.dtype)
                for i, a in enumerate(self.arrs)]

    def sems(self):
        return [pltpu.SemaphoreType.DMA((self.n, N_RELATIONS)), pltpu.SemaphoreType.DMA((self.n, N_RELATIONS)),
                pltpu.SemaphoreType.DMA((self.n,))]

    def _gather_copy(self, ins, outs, sems, a, k, place, to, from_input=False):
        px, py, pc = place
        block = outs[a].at[4 * px + 2 * py + pc]
        return pltpu.make_async_remote_copy(
            src_ref=ins[a] if from_input else block, dst_ref=block,
            send_sem=sems[0].at[a, k], recv_sem=sems[1].at[a, k], device_id=to, device_id_type=MESH_ID)

    def _exchange_copy(self, ins, outs, sems, a, k, landing=False):
        x, y, c = _my_place()
        me = 4 * x + 2 * y + c
        peer = (x ^ (k >> 2), y ^ ((k >> 1) & 1), c ^ (k & 1))
        return pltpu.make_async_remote_copy(
            src_ref=ins[a].at[me ^ k, self.rows[a]], dst_ref=outs[a].at[(me ^ k) if landing else me, self.rows[a]],
            send_sem=sems[0].at[a, k - 1], recv_sem=sems[1].at[a, k - 1], device_id=peer, device_id_type=MESH_ID)

    def _local_copy(self, ins, outs, sems, a):
        x, y, c = _my_place()
        me = 4 * x + 2 * y + c
        if a < self.n_gather:
            return pltpu.make_async_copy(ins[a], outs[a].at[me], sems[2].at[a])
        return pltpu.make_async_copy(ins[a].at[me, self.rows[a]], outs[a].at[me, self.rows[a]], sems[2].at[a])

    def _first_stage(self, ins, outs, sems, a):
        x, y, c = _my_place()
        me = (x, y, c)
        chips = [(1 - x, y), (x, 1 - y), (1 - x, 1 - y)]
        return ([self._gather_copy(ins, outs, sems, a, 0, me, (x, y, 1 - c), from_input=True)]
                + [self._gather_copy(ins, outs, sems, a, 1 + j, me, (*chip, c), from_input=True) for j, chip in enumerate(chips)])

    def start(self, ins, outs, sems):
        for a in range(self.n):
            self._local_copy(ins, outs, sems, a).start()
        for a in range(self.n_gather):
            for cp in self._first_stage(ins, outs, sems, a):
                cp.start()
        for k in range(1, N_DEV):
            for a in range(self.n_gather, self.n):
                self._exchange_copy(ins, outs, sems, a, k).start()

    def finish(self, ins, outs, sems):
        x, y, c = _my_place()
        me, sibling = (x, y, c), (x, y, 1 - c)
        chips = [(1 - x, y), (x, 1 - y), (1 - x, 1 - y)]
        passed = []
        for j, chip in enumerate(chips):
            for a in range(self.n_gather):
                self._gather_copy(ins, outs, sems, a, 1 + j, (*chip, c), me).wait_recv()
                fwd = self._gather_copy(ins, outs, sems, a, 4 + j, (*chip, c), sibling)
                fwd.start()
                passed.append(fwd)
        for a in range(self.n_gather):
            self._gather_copy(ins, outs, sems, a, 0, sibling, me).wait_recv()
            for j, chip in enumerate(chips):
                self._gather_copy(ins, outs, sems, a, 4 + j, (*chip, 1 - c), me).wait_recv()
        for k in range(1, N_DEV):
            for a in range(self.n_gather, self.n):
                self._exchange_copy(ins, outs, sems, a, k, landing=True).wait_recv()
        for a in range(self.n_gather):
            for cp in self._first_stage(ins, outs, sems, a):
                cp.wait_send()
        for cp in passed:
            cp.wait_send()
        for k in range(1, N_DEV):
            for a in range(self.n_gather, self.n):
                self._exchange_copy(ins, outs, sems, a, k).wait_send()
        for a in range(self.n):
            self._local_copy(ins, outs, sems, a).wait()


def _comm_only(comm, name):
    def body(*refs):
        ins, outs, sems = refs[:comm.n], refs[comm.n:2 * comm.n], refs[2 * comm.n:]
        comm.start(ins, outs, sems)
        comm.finish(ins, outs, sems)

    return pl.pallas_call(body, name=name, in_specs=[ANY] * comm.n, out_specs=[ANY] * comm.n,
                          out_shape=comm.out_shape(), scratch_shapes=comm.sems())(*comm.arrs)


def _call(body, *, name, grid, in_specs, out_specs, out_shape, args, sem, scratch_shapes=(), comm=None):
    in_specs, out_specs, out_shape, scratch_shapes = list(in_specs), list(out_specs), list(out_shape), list(scratch_shapes)
    if comm is None:
        outs = pl.pallas_call(body, name=name, grid=grid, in_specs=in_specs, out_specs=out_specs, out_shape=out_shape,
                              scratch_shapes=scratch_shapes, compiler_params=_params(*sem))(*args)
        return list(outs), []
    n_in, n_out, n_scr, nc = len(in_specs), len(out_specs), len(scratch_shapes), comm.n

    def wrapped(*refs):
        ins, refs = refs[:n_in], refs[n_in:]
        c_in, refs = refs[:nc], refs[nc:]
        outs, refs = refs[:n_out], refs[n_out:]
        c_out, refs = refs[:nc], refs[nc:]
        scr, sems = refs[:n_scr], refs[n_scr:]
        first = functools.reduce(jnp.logical_and, [pl.program_id(ax) == 0 for ax in range(len(grid))])
        last = functools.reduce(jnp.logical_and, [pl.program_id(ax) == g - 1 for ax, g in enumerate(grid)])

        @pl.when(first)
        def _():
            comm.start(c_in, c_out, sems)

        body(*ins, *outs, *scr)

        @pl.when(last)
        def _():
            comm.finish(c_in, c_out, sems)

    outs = pl.pallas_call(
        wrapped, name=name, grid=grid, in_specs=in_specs + [ANY] * nc, out_specs=out_specs + [ANY] * nc,
        out_shape=out_shape + comm.out_shape(), scratch_shapes=scratch_shapes + comm.sems(),
        compiler_params=_params(*(["arbitrary"] * len(grid))))(*args, *comm.arrs)
    return list(outs[:n_out]), list(outs[n_out:])


def _matmul(a, b, mode, out_dtype, name, tm, tn, tk, *, bias=None, res=None, res_scale=1.0, b_off=0, comm=None):
    tm = min(tm, a.shape[1] if mode == "tn" else a.shape[0])
    tk = min(tk, a.shape[0] if mode == "tn" else a.shape[1])
    if mode == "nn":
        (m, k), n = a.shape, b.shape[1]
        a_spec = pl.BlockSpec((tm, tk), lambda i, j, kk: (i, kk))
        b_spec = pl.BlockSpec((tk, tn), lambda i, j, kk: (kk + b_off, j))
        dims = NN
    elif mode == "nt":
        (m, k), n = a.shape, b.shape[0]
        a_spec = pl.BlockSpec((tm, tk), lambda i, j, kk: (i, kk))
        b_spec = pl.BlockSpec((tn, tk), lambda i, j, kk: (j, kk + b_off))
        dims = NT
    else:
        (k, m), n = a.shape, b.shape[1]
        a_spec = pl.BlockSpec((tk, tm), lambda i, j, kk: (kk, i))
        b_spec = pl.BlockSpec((tk, tn), lambda i, j, kk: (kk, j))
        dims = TN
    assert m % tm == 0 and n % tn == 0 and k % tk == 0, (name, m, n, k)
    nk = k // tk
    in_specs = [a_spec, b_spec]
    args = [a, b]
    if bias is not None:
        in_specs.append(pl.BlockSpec((1, tn), lambda i, j, kk: (0, j)))
        args.append(bias)
    if res is not None:
        in_specs.append(pl.BlockSpec((tm, tn), lambda i, j, kk: (i, j)))
        args.append(res)

    def finish(out, refs, o_ref):
        pos = 2
        if bias is not None:
            out = out + refs[pos][...]
            pos += 1
        if res is not None:
            out = out + res_scale * refs[pos][...].astype(F32)
        o_ref[...] = out.astype(out_dtype)

    def body_one_step(*refs):
        finish(_dot(refs[0][...], refs[1][...], dims), refs, refs[-1])

    def body(*refs):
        a_ref, b_ref = refs[0], refs[1]
        o_ref, acc = refs[-2], refs[-1]
        kk = pl.program_id(2)

        @pl.when(kk == 0)
        def _():
            acc[...] = jnp.zeros_like(acc)

        acc[...] += _dot(a_ref[...], b_ref[...], dims)

        @pl.when(kk == nk - 1)
        def _():
            finish(acc[...], refs, o_ref)

    (out,), moved = _call(
        body_one_step if nk == 1 else body, name=name, grid=(m // tm, n // tn, nk),
        in_specs=in_specs, out_specs=[pl.BlockSpec((tm, tn), lambda i, j, kk: (i, j))],
        out_shape=[jax.ShapeDtypeStruct((m, n), out_dtype)],
        scratch_shapes=[] if nk == 1 else [pltpu.VMEM((tm, tn), F32)],
        sem=("parallel", "parallel", "arbitrary"), args=args, comm=comm)
    return out if comm is None else (out, moved)


def _matmul_tn_pair(a0, a1, b, out_dtype, name, tm, tn, tk, comm=None):
    (k, m), n = a0.shape, b.shape[1]
    tk = min(tk, k)
    assert a1.shape == a0.shape and m % tm == 0 and n % tn == 0 and k % tk == 0, (name, m, n, k)
    mi, nk = m // tm, k // tk

    def body(a0_ref, a1_ref, b_ref, o_ref, acc):
        i, kk = pl.program_id(0), pl.program_id(2)

        @pl.when(kk == 0)
        def _():
            acc[...] = jnp.zeros_like(acc)

        @pl.when(i < mi)
        def _():
            acc[...] += _dot(a0_ref[...], b_ref[...], TN)

        @pl.when(i >= mi)
        def _():
            acc[...] += _dot(a1_ref[...], b_ref[...], TN)

        @pl.when(kk == nk - 1)
        def _():
            o_ref[...] = acc[...].astype(out_dtype)

    (out,), moved = _call(
        body, name=name, grid=(2 * mi, n // tn, nk),
        in_specs=[pl.BlockSpec((tk, tm), lambda i, j, kk: (jnp.where(i < mi, kk, nk - 1), jnp.minimum(i, mi - 1))),
                  pl.BlockSpec((tk, tm), lambda i, j, kk: (jnp.where(i >= mi, kk, 0), jnp.maximum(i - mi, 0))),
                  pl.BlockSpec((tk, tn), lambda i, j, kk: (kk, j))],
        out_specs=[pl.BlockSpec((tm, tn), lambda i, j, kk: (i, j))],
        out_shape=[jax.ShapeDtypeStruct((2 * m, n), out_dtype)],
        scratch_shapes=[pltpu.VMEM((tm, tn), F32)],
        sem=("parallel", "parallel", "arbitrary"), args=(a0, a1, b), comm=comm)
    return out if comm is None else (out, moved)


def _residual_input(x_ref, prev_refs):
    if not prev_refs:
        return x_ref[...]
    nh, _ = _ln_stats(x_ref[...])
    return nh * prev_refs[0][...] + prev_refs[1][...]


def _matmul_res_ln(a, b, x, g, beta, name, tm, prev=None, comm=None):
    t, k = a.shape
    d = b.shape[1]
    tm = min(tm, t)
    assert t % tm == 0
    n_prev = 0 if prev is None else 2

    def body(a_ref, b_ref, x_ref, g_ref, beta_ref, *rest):
        z_ref, xo_ref = rest[n_prev:]
        z = ALPHA * _residual_input(x_ref, rest[:n_prev]) + _dot(a_ref[...], b_ref[...], NN)
        nh, _ = _ln_stats(z)
        z_ref[...] = z
        xo_ref[...] = (nh * g_ref[...] + beta_ref[...]).astype(BF16)

    row = pl.BlockSpec((tm, d), lambda i: (i, 0))
    vec = pl.BlockSpec((1, d), lambda i: (0, 0))
    outs, moved = _call(
        body, name=name, grid=(t // tm,),
        in_specs=[pl.BlockSpec((tm, k), lambda i: (i, 0)), pl.BlockSpec((k, d), lambda i: (0, 0)), row, vec, vec] + [vec] * n_prev,
        out_specs=[row, row],
        out_shape=[jax.ShapeDtypeStruct((t, d), F32), jax.ShapeDtypeStruct((t, d), BF16)],
        sem=("parallel",), args=(a, b, x, g, beta, *(prev or ())), comm=comm)
    return outs if comm is None else (outs, moved)


def _matmul_ln_bwd(a, b, z, g, dres, name, tm, *, res=None, b_off=0):
    m, k = a.shape
    d = b.shape[1]
    tm = min(tm, m)
    assert m % tm == 0

    def body(*refs):
        a_ref, b_ref, z_ref, g_ref, dres_ref = refs[:5]
        dz_ref, dg_ref, db_ref = refs[-3:]

        @pl.when(pl.program_id(0) == 0)
        def _():
            dg_ref[...] = jnp.zeros_like(dg_ref)
            db_ref[...] = jnp.zeros_like(db_ref)

        dbr = _dot(a_ref[...], b_ref[...], NN)
        if res is not None:
            dbr = dbr + refs[5][...]
        nh, r = _ln_stats(z_ref[...])
        dy = ALPHA * dres_ref[...] + dbr
        dg_ref[...] += _colsum(dy * nh)
        db_ref[...] += _colsum(dy)
        dz_ref[...] = _ln_bwd_rows(dy * g_ref[...], nh, r)

    row = pl.BlockSpec((tm, d), lambda i: (i, 0))
    vec = pl.BlockSpec((1, d), lambda i: (0, 0))
    vshape = jax.ShapeDtypeStruct((1, d), F32)
    return pl.pallas_call(
        body, name=name, grid=(m // tm,),
        in_specs=[pl.BlockSpec((tm, k), lambda i: (i, 0)), pl.BlockSpec((k, d), lambda i: (b_off, 0)), row, vec, row]
        + ([row] if res is not None else []),
        out_specs=[row, vec, vec], out_shape=[jax.ShapeDtypeStruct((m, d), F32), vshape, vshape],
        compiler_params=_params("arbitrary"),
    )(a, b, z, g, dres, *([res] if res is not None else []))


def _matmul_res_ln_loss(a, b, x, g, beta, target, name, tm, prev):
    t, k = a.shape
    d = b.shape[1]
    tm = min(tm, t)

    def body(a_ref, b_ref, x_ref, g_ref, beta_ref, t_ref, gp_ref, bp_ref, dz_ref, dg_ref, db_ref, loss_ref):
        @pl.when(pl.program_id(0) == 0)
        def _():
            dg_ref[...] = jnp.zeros_like(dg_ref)
            db_ref[...] = jnp.zeros_like(db_ref)
            loss_ref[...] = jnp.zeros_like(loss_ref)

        nh, r = _ln_stats(ALPHA * _residual_input(x_ref, (gp_ref, bp_ref)) + _dot(a_ref[...], b_ref[...], NN))
        err = nh * g_ref[...] + beta_ref[...] - t_ref[...]
        loss_ref[...] += _colsum(err * err)
        dy = err * (1.0 / d)
        dg_ref[...] += _colsum(dy * nh)
        db_ref[...] += _colsum(dy)
        dz_ref[...] = _ln_bwd_rows(dy * g_ref[...], nh, r)

    row = pl.BlockSpec((tm, d), lambda i: (i, 0))
    vec = pl.BlockSpec((1, d), lambda i: (0, 0))
    vshape = jax.ShapeDtypeStruct((1, d), F32)
    return pl.pallas_call(
        body, name=name, grid=(t // tm,),
        in_specs=[pl.BlockSpec((tm, k), lambda i: (i, 0)), pl.BlockSpec((k, d), lambda i: (0, 0)), row, vec, vec, row, vec, vec],
        out_specs=[row, vec, vec, vec],
        out_shape=[jax.ShapeDtypeStruct((t, d), F32), vshape, vshape, vshape],
        compiler_params=_params("arbitrary"),
    )(a, b, x, g, beta, target, *prev)


FFN_HALO = 16
FFN_CHUNK = 256
LANES = 128
SUBLANES = 8


def _rows_up(e, start, rows):
    if start % SUBLANES == 0:
        return e[start:start + rows]
    return pltpu.roll(e, e.shape[0] - start, 0)[0:rows]


def _fold(x):
    return jnp.sum(x.reshape(x.shape[0] // SUBLANES, SUBLANES, x.shape[1]), axis=0)


def _ffn_mid_fwd(h, cw, cb, name, tm=1024, tc=256, comm=None):
    t, f2 = h.shape
    tm = min(tm, t)
    f = f2 // 2
    nj, nt, hb = f // tc, t // tm, tm // FFN_HALO

    ch = min(FFN_CHUNK, tm)

    def body(hg, hgp, hv, hvp, cwg, cwv, cbg, cbv, u_ref):
        i = pl.program_id(1)
        o = FFN_HALO - FFN_KERNEL + 1
        for lg in range(tc // LANES):
            cols = slice(lg * LANES, (lg + 1) * LANES)
            wg, wv = [cwg[k:k + 1, cols] for k in range(FFN_KERNEL)], [cwv[k:k + 1, cols] for k in range(FFN_KERNEL)]
            bg, bv = cbg[:, cols], cbv[:, cols]

            def emit(base, eg, ev):
                cg = wg[0] * _rows_up(eg, o, ch) + wg[1] * _rows_up(eg, o + 1, ch) + wg[2] * _rows_up(eg, o + 2, ch) + bg
                cv = wv[0] * _rows_up(ev, o, ch) + wv[1] * _rows_up(ev, o + 1, ch) + wv[2] * _rows_up(ev, o + 2, ch) + bv
                u_ref[pl.ds(base, ch), cols] = (_gelu(cg) * cv).astype(BF16)

            def first(main, prev):
                return jnp.concatenate([jnp.where(i > 0, prev[:, cols].astype(F32), 0.0), main[0:ch, cols].astype(F32)], axis=0)

            def inner(c, carry):
                base = pl.multiple_of(c * ch, ch)
                emit(base, hg[pl.ds(base - FFN_HALO, ch + FFN_HALO), cols].astype(F32),
                     hv[pl.ds(base - FFN_HALO, ch + FFN_HALO), cols].astype(F32))
                return carry

            emit(0, first(hg, hgp), first(hv, hvp))
            if tm > ch:
                lax.fori_loop(1, tm // ch, inner, 0)

    def main_spec(off):
        return pl.BlockSpec((tm, tc), lambda j, i: (i, j + off))

    def prev_spec(off):
        return pl.BlockSpec((FFN_HALO, tc), lambda j, i: (jnp.maximum(i * hb - 1, 0), j + off))

    def par_spec(rows, off):
        return pl.BlockSpec((rows, tc), lambda j, i: (0, j + off))

    (u,), moved = _call(
        body, name=name, grid=(nj, nt),
        in_specs=[main_spec(0), prev_spec(0), main_spec(nj), prev_spec(nj),
                  par_spec(FFN_KERNEL, 0), par_spec(FFN_KERNEL, nj), par_spec(1, 0), par_spec(1, nj)],
        out_specs=[pl.BlockSpec((tm, tc), lambda j, i: (i, j))],
        out_shape=[jax.ShapeDtypeStruct((t, f), BF16)],
        sem=("parallel", "arbitrary"), args=(h, h, h, h, cw, cw, cb, cb), comm=comm)
    return u if comm is None else (u, moved)


def _ffn_mid_bwd(h, du, cw, cb, name, tm=1024, tc=256, comm=None):
    t, f2 = h.shape
    tm = min(tm, t)
    f = f2 // 2
    nj, nt, hb = f // tc, t // tm, tm // FFN_HALO

    ch = min(FFN_CHUNK, tm)
    ahead = ch + SUBLANES

    n_ch = tm // ch

    def body(hg, hgp, hgn, hv, hvp, hvn, du_ref, dun_ref, cwg, cwv, cbg, cbv,
             dhg_ref, dhv_ref, dcwg_ref, dcwv_ref, dcbg_ref, dcbv_ref):
        i = pl.program_id(1)

        @pl.when(i == 0)
        def _():
            for ref in (dcwg_ref, dcwv_ref, dcbg_ref, dcbv_ref):
                ref[...] = jnp.zeros_like(ref)

        o = FFN_HALO - FFN_KERNEL + 1
        for lg in range(tc // LANES):
            cols = slice(lg * LANES, (lg + 1) * LANES)
            wg, wv = [cwg[k:k + 1, cols] for k in range(FFN_KERNEL)], [cwv[k:k + 1, cols] for k in range(FFN_KERNEL)]
            bg, bv = cbg[:, cols], cbv[:, cols]

            def emit(base, eg, ev, du_e, acc):
                hgs = [_rows_up(eg, o + k, ahead) for k in range(FFN_KERNEL)]
                hvs = [_rows_up(ev, o + k, ahead) for k in range(FFN_KERNEL)]
                cg = wg[0] * hgs[0] + wg[1] * hgs[1] + wg[2] * hgs[2] + bg
                cv = wv[0] * hvs[0] + wv[1] * hvs[1] + wv[2] * hvs[2] + bv
                du_a = du_e[0:ahead]
                gl, dgl = _gelu_and_grad(cg)

                def back(d, hs, w, dh_ref):
                    own = d[0:ch]
                    dh = w[2] * own + w[1] * _rows_up(d, 1, ch) + w[0] * _rows_up(d, 2, ch)
                    dh_ref[pl.ds(base, ch), cols] = dh.astype(BF16)
                    return [_fold(own)] + [_fold(own * hs[k][0:ch]) for k in range(FFN_KERNEL)]

                sums = back(du_a * cv * dgl, hgs, wg, dhg_ref) + back(du_a * gl, hvs, wv, dhv_ref)
                return tuple(a + s_ for a, s_ in zip(acc, sums))

            def edge(c, acc):
                def rows(main, before, after, lo, hi):
                    parts = [] if lo >= 0 else [jnp.where(i > 0, before[:, cols].astype(F32), 0.0)]
                    parts.append(main[max(lo, 0):min(hi, tm), cols].astype(F32))
                    if hi > tm:
                        parts.append(after)
                    return jnp.concatenate(parts, axis=0) if len(parts) > 1 else parts[0]

                lo, hi = c * ch - FFN_HALO, (c + 1) * ch + FFN_HALO
                du_next = jnp.where(i < nt - 1, dun_ref[:, cols].astype(F32), 0.0)
                return emit(c * ch, rows(hg, hgp, hgn[:, cols].astype(F32), lo, hi), rows(hv, hvp, hvn[:, cols].astype(F32), lo, hi),
                            rows(du_ref, None, du_next, c * ch, hi), acc)

            def inner(c, acc):
                base = pl.multiple_of(c * ch, ch)
                return emit(base, hg[pl.ds(base - FFN_HALO, ch + 2 * FFN_HALO), cols].astype(F32),
                            hv[pl.ds(base - FFN_HALO, ch + 2 * FFN_HALO), cols].astype(F32),
                            du_ref[pl.ds(base, ch + FFN_HALO), cols].astype(F32), acc)

            zero = jnp.zeros((SUBLANES, LANES), F32)
            acc = edge(0, (zero,) * (2 * (1 + FFN_KERNEL)))
            if n_ch > 2:
                acc = lax.fori_loop(1, n_ch - 1, inner, acc)
            if n_ch > 1:
                acc = edge(n_ch - 1, acc)
            dcbg_ref[:, cols] += _colsum(acc[0])
            dcbv_ref[:, cols] += _colsum(acc[1 + FFN_KERNEL])
            for k in range(FFN_KERNEL):
                dcwg_ref[k:k + 1, cols] += _colsum(acc[1 + k])
                dcwv_ref[k:k + 1, cols] += _colsum(acc[2 + FFN_KERNEL + k])

    last_blk = t // FFN_HALO - 1

    def main_spec(off):
        return pl.BlockSpec((tm, tc), lambda j, i: (i, j + off))

    def prev_spec(off):
        return pl.BlockSpec((FFN_HALO, tc), lambda j, i: (jnp.maximum(i * hb - 1, 0), j + off))

    def next_spec(off):
        return pl.BlockSpec((FFN_HALO, tc), lambda j, i: (jnp.minimum((i + 1) * hb, last_blk), j + off))

    def par_spec(rows, off):
        return pl.BlockSpec((rows, tc), lambda j, i: (0, j + off))

    out_tile = pl.BlockSpec((tm, tc), lambda j, i: (i, j))
    outs, moved = _call(
        body, name=name, grid=(nj, nt),
        in_specs=[main_spec(0), prev_spec(0), next_spec(0), main_spec(nj), prev_spec(nj), next_spec(nj),
                  main_spec(0), next_spec(0),
                  par_spec(FFN_KERNEL, 0), par_spec(FFN_KERNEL, nj), par_spec(1, 0), par_spec(1, nj)],
        out_specs=[out_tile, out_tile, par_spec(FFN_KERNEL, 0), par_spec(FFN_KERNEL, 0), par_spec(1, 0), par_spec(1, 0)],
        out_shape=[jax.ShapeDtypeStruct((t, f), BF16), jax.ShapeDtypeStruct((t, f), BF16),
                   jax.ShapeDtypeStruct((FFN_KERNEL, f), F32), jax.ShapeDtypeStruct((FFN_KERNEL, f), F32),
                   jax.ShapeDtypeStruct((1, f), F32), jax.ShapeDtypeStruct((1, f), F32)],
        sem=("parallel", "arbitrary"), args=(h, h, h, h, h, h, du, du, cw, cw, cb, cb), comm=comm)
    return outs if comm is None else (outs, moved)


MIX_HALO = 32


def _glu(hh):
    return hh[:, 0:A_WIDTH] * _sigmoid(hh[:, A_WIDTH:2 * A_WIDTH])


def _fill_row_shifts(s):
    rows = s.shape[1] - SUBLANES
    for j in range(1, SUBLANES):
        s[j, 0:rows, :] = s[0, pl.ds(j, rows), :]


def _rows_from(s, start, rows):
    j = start % SUBLANES
    return s[j, start - j:start - j + rows, :]


def _tril_mask():
    return lax.broadcasted_iota(jnp.int32, (B_CHUNK, B_CHUNK), 0) >= lax.broadcasted_iota(jnp.int32, (B_CHUNK, B_CHUNK), 1)


def _spatial_mix(q, ms_ref, sbt_ref, tm):
    mask = _tril_mask()
    ws = [jnp.where(mask, ms_ref[g], 0.0).astype(BF16) for g in range(B_GROUPS)]
    qb = q.astype(BF16)
    rows = []
    for c in range(tm // B_CHUNK):
        cols = [_dot(ws[g], qb[c * B_CHUNK:(c + 1) * B_CHUNK, g * 128:(g + 1) * 128], NN) + sbt_ref[:, g:g + 1]
                for g in range(B_GROUPS)]
        rows.append(jnp.concatenate(cols, axis=1))
    return jnp.concatenate(rows, axis=0)


def _mixer_mid_fwd(h, cw, cb, ag, ab, bg, bb, ms, sbt, name, tm=256, comm=None):
    t = h.shape[0]
    nt, hb = t // tm, tm // MIX_HALO
    o = MIX_HALO - A_KERNEL + 1

    def body(h_ref, hp_ref, cw_ref, cb_ref, ag_ref, ab_ref, bg_ref, bb_ref, ms_ref, sbt_ref, cat_ref, sp):
        i = pl.program_id(0)
        sp[0, 0:MIX_HALO, :] = jnp.where(i > 0, _glu(hp_ref[:, 0:2 * A_WIDTH].astype(F32)), 0.0)
        sp[0, MIX_HALO:, :] = _glu(h_ref[:, 0:2 * A_WIDTH].astype(F32))
        _fill_row_shifts(sp)
        y = jnp.zeros((tm, A_WIDTH), F32) + cb_ref[...]
        for k in range(A_KERNEL):
            y = y + cw_ref[k:k + 1, :] * _rows_from(sp, o + k, tm)
        nh, _ = _ln_stats(y)
        ln = nh * ag_ref[...] + ab_ref[...]
        cat_ref[:, 0:A_WIDTH] = (ln * _sigmoid(ln)).astype(BF16)
        u = _gelu(h_ref[:, 1024:1536].astype(F32))
        nb, _ = _ln_stats(_gelu(h_ref[:, 1536:2048].astype(F32)))
        mixed = _spatial_mix(nb * bg_ref[...] + bb_ref[...], ms_ref, sbt_ref, tm)
        cat_ref[:, A_WIDTH:] = (u * mixed).astype(BF16)

    vec = pl.BlockSpec((1, A_WIDTH), lambda i: (0, 0))
    (cat,), moved = _call(
        body, name=name, grid=(nt,),
        in_specs=[pl.BlockSpec((tm, 2048), lambda i: (i, 0)),
                  pl.BlockSpec((MIX_HALO, 2048), lambda i: (jnp.maximum(i * hb - 1, 0), 0)),
                  pl.BlockSpec((A_KERNEL, A_WIDTH), lambda i: (0, 0)), vec, vec, vec, vec, vec,
                  pl.BlockSpec((B_GROUPS, B_CHUNK, B_CHUNK), lambda i: (0, 0, 0)),
                  pl.BlockSpec((B_CHUNK, B_GROUPS), lambda i: (0, 0))],
        out_specs=[pl.BlockSpec((tm, D_MODEL), lambda i: (i, 0))],
        out_shape=[jax.ShapeDtypeStruct((t, D_MODEL), BF16)],
        scratch_shapes=[pltpu.VMEM((SUBLANES, tm + MIX_HALO, A_WIDTH), F32)],
        sem=("parallel",), args=(h, h, cw, cb, ag, ab, bg, bb, ms, sbt), comm=comm)
    return cat if comm is None else (cat, moved)


def _mixer_mid_bwd(h, dcat, cw, cb, ag, ab, bg, bb, ms, mst, sbt, name, tm=256, comm=None):
    t = h.shape[0]
    nt, hb = t // tm, tm // MIX_HALO
    o = MIX_HALO - A_KERNEL + 1
    r = tm + MIX_HALO
    nchunk = tm // B_CHUNK

    def body(h_ref, hp_ref, hn_ref, dc_ref, dcn_ref, cw_ref, cb_ref, ag_ref, ab_ref, bg_ref, bb_ref, ms_ref, mst_ref, sbt_ref,
             dh_ref, dcw_ref, dcb_ref, dag_ref, dab_ref, dbg_ref, dbb_ref, dms_ref, dsb_ref, sp, sdy, sbacc):
        i = pl.program_id(0)

        @pl.when(i == 0)
        def _():
            for ref in (dcw_ref, dcb_ref, dag_ref, dab_ref, dbg_ref, dbb_ref, dms_ref, dsb_ref, sbacc):
                ref[...] = jnp.zeros_like(ref)

        sp[0, 0:MIX_HALO, :] = jnp.where(i > 0, _glu(hp_ref[:, 0:2 * A_WIDTH].astype(F32)), 0.0)
        sp[0, MIX_HALO:MIX_HALO + tm, :] = _glu(h_ref[:, 0:2 * A_WIDTH].astype(F32))
        sp[0, MIX_HALO + tm:, :] = _glu(hn_ref[:, 0:2 * A_WIDTH].astype(F32))
        _fill_row_shifts(sp)
        y = jnp.zeros((r, A_WIDTH), F32) + cb_ref[...]
        for k in range(A_KERNEL):
            y = y + cw_ref[k:k + 1, :] * _rows_from(sp, o + k, r)
        nh, rs = _ln_stats(y)
        ln = nh * ag_ref[...] + ab_ref[...]
        sg = _sigmoid(ln)
        dao = jnp.concatenate([dc_ref[:, 0:A_WIDTH].astype(F32),
                               jnp.where(i < nt - 1, dcn_ref[:, 0:A_WIDTH].astype(F32), 0.0)], axis=0)
        dln = dao * (sg * (1.0 + ln * (1.0 - sg)))
        dag_ref[...] += _colsum(dln[0:tm] * nh[0:tm])
        dab_ref[...] += _colsum(dln[0:tm])
        sdy[0] = _ln_bwd_rows(dln * ag_ref[...], nh, rs)
        _fill_row_shifts(sdy)
        dy_own = sdy[0, 0:tm, :]
        dcb_ref[...] += _colsum(dy_own)
        dp = jnp.zeros((tm, A_WIDTH), F32)
        for k in range(A_KERNEL):
            dcw_ref[k:k + 1, :] += _colsum(dy_own * _rows_from(sp, o + k, tm))
            dp = dp + cw_ref[k:k + 1, :] * _rows_from(sdy, A_KERNEL - 1 - k, tm)
        av = h_ref[:, 0:A_WIDTH].astype(F32)
        s = _sigmoid(h_ref[:, A_WIDTH:2 * A_WIDTH].astype(F32))
        dh_ref[:, 0:A_WIDTH] = (dp * s).astype(BF16)
        dh_ref[:, A_WIDTH:2 * A_WIDTH] = (dp * av * s * (1.0 - s)).astype(BF16)

        u, dgu = _gelu_and_grad(h_ref[:, 1024:1536].astype(F32))
        w, dgw = _gelu_and_grad(h_ref[:, 1536:2048].astype(F32))
        nb, rb = _ln_stats(w)
        q = nb * bg_ref[...] + bb_ref[...]
        mixed = _spatial_mix(q, ms_ref, sbt_ref, tm)
        dbo = dc_ref[:, A_WIDTH:].astype(F32)
        dh_ref[:, 1024:1536] = (dbo * mixed * dgu).astype(BF16)
        dmx = dbo * u
        mask = _tril_mask()
        wst = [jnp.where(mask.T, mst_ref[g], 0.0).astype(BF16) for g in range(B_GROUPS)]
        qb = q.astype(BF16)
        dmb = dmx.astype(BF16)
        rows = []
        for c in range(nchunk):
            cols = []
            for g in range(B_GROUPS):
                rs_, cs_ = slice(c * B_CHUNK, (c + 1) * B_CHUNK), slice(g * 128, (g + 1) * 128)
                sbacc[g] += dmx[rs_, cs_]
                dms_ref[g] += _dot(dmb[rs_, cs_], qb[rs_, cs_], NT)
                cols.append(_dot(wst[g], dmb[rs_, cs_], NN))
            rows.append(jnp.concatenate(cols, axis=1))
        dq = jnp.concatenate(rows, axis=0)
        dbg_ref[...] += _colsum(dq * nb)
        dbb_ref[...] += _colsum(dq)
        dh_ref[:, 1536:2048] = (_ln_bwd_rows(dq * bg_ref[...], nb, rb) * dgw).astype(BF16)

        @pl.when(i == nt - 1)
        def _():
            for g in range(B_GROUPS):
                dms_ref[g] = jnp.where(mask, dms_ref[g], 0.0)
                dsb_ref[g] = jnp.sum(sbacc[g], axis=1, keepdims=True)

    last_blk = t // MIX_HALO - 1
    vec = pl.BlockSpec((1, A_WIDTH), lambda i: (0, 0))
    mat = pl.BlockSpec((B_GROUPS, B_CHUNK, B_CHUNK), lambda i: (0, 0, 0))
    taps = pl.BlockSpec((A_KERNEL, A_WIDTH), lambda i: (0, 0))

    def halo(width, which):
        if which == "prev":
            return pl.BlockSpec((MIX_HALO, width), lambda i: (jnp.maximum(i * hb - 1, 0), 0))
        return pl.BlockSpec((MIX_HALO, width), lambda i: (jnp.minimum((i + 1) * hb, last_blk), 0))

    vshape = jax.ShapeDtypeStruct((1, A_WIDTH), F32)
    outs, moved = _call(
        body, name=name, grid=(nt,),
        in_specs=[pl.BlockSpec((tm, 2048), lambda i: (i, 0)), halo(2048, "prev"), halo(2048, "next"),
                  pl.BlockSpec((tm, D_MODEL), lambda i: (i, 0)), halo(D_MODEL, "next"),
                  taps, vec, vec, vec, vec, vec, mat, mat, pl.BlockSpec((B_CHUNK, B_GROUPS), lambda i: (0, 0))],
        out_specs=[pl.BlockSpec((tm, 2048), lambda i: (i, 0)), taps, vec, vec, vec, vec, vec, mat,
                   pl.BlockSpec((B_GROUPS, B_CHUNK, 1), lambda i: (0, 0, 0))],
        out_shape=[jax.ShapeDtypeStruct((t, 2048), BF16), jax.ShapeDtypeStruct((A_KERNEL, A_WIDTH), F32),
                   vshape, vshape, vshape, vshape, vshape,
                   jax.ShapeDtypeStruct((B_GROUPS, B_CHUNK, B_CHUNK), F32), jax.ShapeDtypeStruct((B_GROUPS, B_CHUNK, 1), F32)],
        scratch_shapes=[pltpu.VMEM((SUBLANES, tm + 2 * MIX_HALO, A_WIDTH), F32), pltpu.VMEM((SUBLANES, r, A_WIDTH), F32),
                        pltpu.VMEM((B_GROUPS, B_CHUNK, B_CHUNK), F32)],
        sem=("arbitrary",), args=(h, h, h, dcat, dcat, cw, cb, ag, ab, bg, bb, ms, mst, sbt), comm=comm)
    return outs if comm is None else (outs, moved)


Q_WIDTH = N_Q_HEADS * HEAD_DIM
KV_WIDTH = 2 * N_KV_HEADS * HEAD_DIM
PAIRS_PER_KV = N_Q_HEADS // N_KV_HEADS // 2
ATT_SCALE = 1.0 / math.sqrt(HEAD_DIM)


def _dup_heads(pair_cols, kv_head):
    lane = lax.broadcasted_iota(jnp.int32, pair_cols.shape, 1)
    rolled = pltpu.roll(pair_cols, HEAD_DIM, 1)
    first = lane < HEAD_DIM
    return jnp.where(first, pair_cols, rolled) if kv_head == 0 else jnp.where(first, rolled, pair_cols)


HEADS_PER_KV = N_Q_HEADS // N_KV_HEADS


def _stack_heads(ref, kh):
    lane = lax.broadcasted_iota(jnp.int32, (ATT_BLOCK, 128), 1)
    rows = []
    for pr in range(PAIRS_PER_KV):
        c0 = (kh * PAIRS_PER_KV + pr) * 128
        pair = ref[:, c0:c0 + 128]
        rows += [jnp.where(lane < HEAD_DIM, pair, jnp.zeros_like(pair)), jnp.where(lane < HEAD_DIM, jnp.zeros_like(pair), pair)]
    return jnp.concatenate(rows, axis=0)


def _unstack_heads(stacked, kh, write):
    lane = lax.broadcasted_iota(jnp.int32, (ATT_BLOCK, 128), 1)
    for pr in range(PAIRS_PER_KV):
        first = stacked[(2 * pr) * ATT_BLOCK:(2 * pr + 1) * ATT_BLOCK]
        second = stacked[(2 * pr + 1) * ATT_BLOCK:(2 * pr + 2) * ATT_BLOCK]
        write((kh * PAIRS_PER_KV + pr) * 128, jnp.where(lane < HEAD_DIM, first, second))


def _sink_row(sink_ref, kh):
    return jnp.concatenate([jnp.full((1, ATT_BLOCK), sink_ref[0, kh * HEADS_PER_KV + h], F32) for h in range(HEADS_PER_KV)], axis=1)


def _att_mask_t(n):
    sj = lax.broadcasted_iota(jnp.int32, (2 * ATT_BLOCK, HEADS_PER_KV * ATT_BLOCK), 0)
    qi = lax.broadcasted_iota(jnp.int32, (2 * ATT_BLOCK, HEADS_PER_KV * ATT_BLOCK), 1) & (ATT_BLOCK - 1)
    diff = qi + ATT_BLOCK - sj
    return (diff >= 0) & (diff < ATT_BLOCK) & ((n > 0) | (sj >= ATT_BLOCK))


def _att_probs_t(q_all, k2, mask_t, sink):
    st = _dot(k2, q_all, NT) * ATT_SCALE
    st = jnp.where(mask_t, st, -jnp.inf)
    m = jnp.maximum(jnp.max(st, axis=0, keepdims=True), sink)
    e = jnp.exp(st - m)
    es = jnp.exp(sink - m)
    inv = 1.0 / (jnp.sum(e, axis=0, keepdims=True) + es)
    return e * inv, es * inv


def _attn_fwd(qkv, sinks, name, comm=None):
    t = qkv.shape[0]
    nb = t // ATT_BLOCK
    kvb = Q_WIDTH // KV_WIDTH

    def body(sink_ref, q_ref, kv_ref, kvp_ref, o_ref):
        n = pl.program_id(0)
        mask_t = _att_mask_t(n)
        kv = jnp.concatenate([kvp_ref[...], kv_ref[...]], axis=0).astype(F32)

        def write(c0, pair):
            o_ref[:, c0:c0 + 128] = pair.astype(BF16)

        for kh in range(N_KV_HEADS):
            k2 = _dup_heads(kv[:, 0:128], kh).astype(BF16)
            v2 = _dup_heads(kv[:, 128:256], kh).astype(BF16)
            pt, _ = _att_probs_t(_stack_heads(q_ref, kh), k2, mask_t, _sink_row(sink_ref, kh))
            _unstack_heads(_dot(v2, pt, TN).T, kh, write)

    (out,), moved = _call(
        body, name=name, grid=(nb,),
        in_specs=[pl.BlockSpec(memory_space=pltpu.SMEM),
                  pl.BlockSpec((ATT_BLOCK, Q_WIDTH), lambda n: (n, 0)),
                  pl.BlockSpec((ATT_BLOCK, KV_WIDTH), lambda n: (n, kvb)),
                  pl.BlockSpec((ATT_BLOCK, KV_WIDTH), lambda n: (jnp.maximum(n - 1, 0), kvb))],
        out_specs=[pl.BlockSpec((ATT_BLOCK, Q_WIDTH), lambda n: (n, 0))],
        out_shape=[jax.ShapeDtypeStruct((t, Q_WIDTH), BF16)],
        sem=("parallel",), args=(sinks, qkv, qkv, qkv), comm=comm)
    return out if comm is None else (out, moved)


def _attn_bwd(qkv, d_o, sinks, name, comm=None):
    t = qkv.shape[0]
    nb = t // ATT_BLOCK
    kvb = Q_WIDTH // KV_WIDTH

    def body(sink_ref, q_ref, kv_ref, kvp_ref, do_ref, dq_ref, dkv_ref, dbq_ref, dbkv_ref, dsink_ref, carry):
        n = pl.program_id(0)

        @pl.when(n == 0)
        def _():
            for ref in (dbq_ref, dbkv_ref, dsink_ref, carry):
                ref[...] = jnp.zeros_like(ref)
            dkv_ref[...] = jnp.zeros_like(dkv_ref)

        @pl.when(n < nb)
        def _():
            mask_t = _att_mask_t(n)
            kv = jnp.concatenate([kvp_ref[...], kv_ref[...]], axis=0).astype(F32)
            lane2 = lax.broadcasted_iota(jnp.int32, (2 * ATT_BLOCK, 128), 1)
            sink_lane = lax.broadcasted_iota(jnp.int32, (1, 128), 1)
            dsink = jnp.zeros((1, 128), F32)
            dk_parts, dv_parts = [], []

            def write(c0, pair):
                dbq_ref[:, c0:c0 + 128] += _colsum(pair)
                dq_ref[:, c0:c0 + 128] = pair.astype(BF16)

            for kh in range(N_KV_HEADS):
                k2 = _dup_heads(kv[:, 0:128], kh).astype(BF16)
                v2 = _dup_heads(kv[:, 128:256], kh).astype(BF16)
                q_all = _stack_heads(q_ref, kh)
                do_all = _stack_heads(do_ref, kh)
                pt, ps = _att_probs_t(q_all, k2, mask_t, _sink_row(sink_ref, kh))
                dpt = _dot(v2, do_all, NT)
                delta = jnp.sum(pt * dpt, axis=0, keepdims=True)
                dst = pt * (dpt - delta) * ATT_SCALE
                psd = ps * delta
                for h in range(HEADS_PER_KV):
                    dsink = dsink + jnp.where(sink_lane == kh * HEADS_PER_KV + h,
                                              -jnp.sum(psd[:, h * ATT_BLOCK:(h + 1) * ATT_BLOCK]), 0.0)
                _unstack_heads(_dot(k2, dst, TN).T, kh, write)
                dk_acc = _dot(dst, q_all, NN)
                dv_acc = _dot(pt, do_all, NN)
                dk_parts.append(dk_acc + pltpu.roll(dk_acc, HEAD_DIM, 1))
                dv_parts.append(dv_acc + pltpu.roll(dv_acc, HEAD_DIM, 1))
            dk = jnp.where(lane2 < HEAD_DIM, dk_parts[0], dk_parts[1])
            dv = jnp.where(lane2 < HEAD_DIM, dv_parts[0], dv_parts[1])
            dkv_new = jnp.concatenate([dk, dv], axis=1)
            done = carry[...] + dkv_new[0:ATT_BLOCK]

            @pl.when(n > 0)
            def _():
                dkv_ref[...] = done.astype(BF16)
                dbkv_ref[...] += _colsum(done)

            carry[...] = dkv_new[ATT_BLOCK:]
            dsink_ref[...] += dsink

        @pl.when(n == nb)
        def _():
            dkv_ref[...] = carry[...].astype(BF16)
            dbkv_ref[...] += _colsum(carry[...])

    def clamp(n):
        return jnp.minimum(n, nb - 1)

    outs, moved = _call(
        body, name=name, grid=(nb + 1,),
        in_specs=[pl.BlockSpec(memory_space=pltpu.SMEM),
                  pl.BlockSpec((ATT_BLOCK, Q_WIDTH), lambda n: (clamp(n), 0)),
                  pl.BlockSpec((ATT_BLOCK, KV_WIDTH), lambda n: (clamp(n), kvb)),
                  pl.BlockSpec((ATT_BLOCK, KV_WIDTH), lambda n: (jnp.maximum(clamp(n) - 1, 0), kvb)),
                  pl.BlockSpec((ATT_BLOCK, Q_WIDTH), lambda n: (clamp(n), 0))],
        out_specs=[pl.BlockSpec((ATT_BLOCK, Q_WIDTH), lambda n: (clamp(n), 0)),
                   pl.BlockSpec((ATT_BLOCK, KV_WIDTH), lambda n: (jnp.maximum(n - 1, 0), 0)),
                   pl.BlockSpec((1, Q_WIDTH), lambda n: (0, 0)),
                   pl.BlockSpec((1, KV_WIDTH), lambda n: (0, 0)),
                   pl.BlockSpec((1, 128), lambda n: (0, 0))],
        out_shape=[jax.ShapeDtypeStruct((t, Q_WIDTH), BF16), jax.ShapeDtypeStruct((t, KV_WIDTH), BF16),
                   jax.ShapeDtypeStruct((1, Q_WIDTH), F32), jax.ShapeDtypeStruct((1, KV_WIDTH), F32),
                   jax.ShapeDtypeStruct((1, 128), F32)],
        scratch_shapes=[pltpu.VMEM((ATT_BLOCK, KV_WIDTH), F32)],
        sem=("arbitrary",), args=(sinks, qkv, qkv, qkv, d_o), comm=comm)
    return outs if comm is None else (outs, moved)


def _adamw_math(g, w, m, v):
    m = ADAM_B1 * m + (1.0 - ADAM_B1) * g
    v = ADAM_B2 * v + (1.0 - ADAM_B2) * (g * g)
    m_hat = m / (1.0 - ADAM_B1 ** ADAM_STEP)
    v_hat = v / (1.0 - ADAM_B2 ** ADAM_STEP)
    delta = -ADAM_LR * (m_hat / (jnp.sqrt(v_hat) + ADAM_EPS) + ADAM_WD * w)
    return delta, m, v


def _sum_partials(p_ref):
    g = p_ref[0].astype(F32)
    for s in range(1, N_DEV):
        g = g + p_ref[s].astype(F32)
    return g


def _adamw_big(parts, w, m, v, name, tr):
    r, c = w.shape
    parts = [p if isinstance(p, tuple) else (p, 0, p.shape[1]) for p in parts]
    tiles = [rows // tr for _, _, rows in parts]
    starts = [sum(tiles[:l]) for l in range(len(parts))]
    assert all(lo % tr == 0 and rows % tr == 0 for _, lo, rows in parts) and sum(tiles) * tr == r

    def body(*refs):
        p_refs, (w_ref, m_ref, v_ref, g_out, d_out, m_out, v_out) = refs[:len(parts)], refs[len(parts):]
        i = pl.program_id(0)
        for l, p_ref in enumerate(p_refs):
            @pl.when((i >= starts[l]) & (i < starts[l] + tiles[l]))
            def _():
                g = _sum_partials(p_ref)
                g_out[...] = g
                d_out[...], m_out[...], v_out[...] = _adamw_math(g, w_ref[...], m_ref[...], v_ref[...])

    def part_spec(l):
        return pl.BlockSpec((N_DEV, tr, c), lambda i: (0, jnp.clip(i - starts[l], 0, tiles[l] - 1) + parts[l][1] // tr, 0))

    tile = pl.BlockSpec((tr, c), lambda i: (i, 0))
    shape = jax.ShapeDtypeStruct((r, c), F32)
    return pl.pallas_call(
        body, name=name, grid=(r // tr,),
        in_specs=[part_spec(l) for l in range(len(parts))] + [tile, tile, tile],
        out_specs=[tile] * 4, out_shape=[shape] * 4,
        compiler_params=_params("parallel"),
    )(*[p[0] for p in parts], w, m, v)


def _adamw_small(parts, ws, ms, vs, name):
    n = len(ws)

    def body(*refs):
        ins, outs = refs[:4 * n], refs[4 * n:]
        for a in range(n):
            g = _sum_partials(ins[a])
            outs[4 * a][...] = g
            outs[4 * a + 1][...], outs[4 * a + 2][...], outs[4 * a + 3][...] = _adamw_math(
                g, ins[n + a][...], ins[2 * n + a][...], ins[3 * n + a][...])

    out_shape = []
    for w in ws:
        out_shape += [jax.ShapeDtypeStruct(w.shape, F32)] * 4
    return pl.pallas_call(body, name=name, out_shape=out_shape, compiler_params=_params())(*parts, *ws, *ms, *vs)


PACK_LANES = 128
PACK_ROWS = 8


def _pack(arrs):
    flat = jnp.concatenate([a.reshape(-1).astype(F32) for a in arrs])
    unit = PACK_LANES * PACK_ROWS
    total = -(-flat.shape[0] // unit) * unit
    return jnp.pad(flat, (0, total - flat.shape[0])).reshape(-1, PACK_LANES)


def _unpack(buf, shapes):
    flat = buf.reshape(N_DEV, -1)
    out, pos = [], 0
    for s in shapes:
        size = math.prod(s)
        out.append(flat[:, pos:pos + size].reshape((N_DEV,) + tuple(s)))
        pos += size
    return out


def _interleave(g):
    return jnp.transpose(g, (1, 0, 2)).reshape(g.shape[1], -1)


def _deinterleave(w):
    r = w.shape[0]
    return jnp.transpose(w.reshape(r, N_DEV, -1), (1, 0, 2))


def _ffn_backward(dz, x_in, z_in, g_in, h, u, w_up_t, cw, cb, w_down, tag, exchange=(), exchange_late=()):
    du = _matmul(dz, w_down, "nt", BF16, f"ffn{tag}_du", 1024, 1408, 1024)
    d_w_down = _matmul(u, dz, "tn", BF16, f"ffn{tag}_dwdown", 1408, 1024, 2048)
    (dhg, dhv, dcwg, dcwv, dcbg, dcbv), moved = _ffn_mid_bwd(
        h, du, cw, cb, f"ffn{tag}_mid_bwd", comm=_Comm(exchange=[d_w_down.reshape(N_DEV, -1, D_MODEL), *exchange]))
    d_w_up_t = _matmul_tn_pair(dhg, dhv, x_in, BF16, f"ffn{tag}_dwup", 1408, 1024, 1024,
                               comm=_Comm(exchange=exchange_late) if exchange_late else None)
    if exchange_late:
        d_w_up_t, late = d_w_up_t
        moved = moved + late
    dx = _matmul(dhv, w_up_t, "nn", F32, f"ffn{tag}_dx_value", 1024, 1024, D_FF, b_off=1)
    dz_in, dg_in, db_in = _matmul_ln_bwd(dhg, w_up_t, z_in, g_in, dz, f"ffn{tag}_dx_gate_ln_bwd", 512, res=dx)
    return (dz_in, dg_in, db_in, d_w_up_t.reshape(N_DEV, -1, D_MODEL),
            jnp.concatenate([dcwg, dcwv], axis=1), jnp.concatenate([dcbg, dcbv], axis=1), moved)


def kernel(x, ab_w_in, a_conv_w, a_conv_b, a_norm_g, a_norm_b, b_norm_g, b_norm_b, b_spatial_w, b_spatial_b, ab_w_out, c_w_qkv, c_b_qkv, c_sinks, c_w_o, ffn_w_up, ffn_conv_w, ffn_conv_b, ffn_w_down, ln_g, ln_b, loss_target, m_ab_w_in, m_a_conv_w, m_a_conv_b, m_a_norm_g, m_a_norm_b, m_b_norm_g, m_b_norm_b, m_b_spatial_w, m_b_spatial_b, m_ab_w_out, m_c_w_qkv, m_c_b_qkv, m_c_sinks, m_c_w_o, m_ffn_w_up, m_ffn_conv_w, m_ffn_conv_b, m_ffn_w_down, m_ln_g, m_ln_b, v_ab_w_in, v_a_conv_w, v_a_conv_b, v_a_norm_g, v_a_norm_b, v_b_norm_g, v_b_norm_b, v_b_spatial_w, v_b_spatial_b, v_ab_w_out, v_c_w_qkv, v_c_b_qkv, v_c_sinks, v_c_w_o, v_ffn_w_up, v_ffn_conv_w, v_ffn_conv_b, v_ffn_w_down, v_ln_g, v_ln_b):
    me = 4 * lax.axis_index("x") + 2 * lax.axis_index("y") + lax.axis_index("c")
    xt = x[0]
    t = xt.shape[0]

    small_shard_shapes = [a_conv_w.shape, c_b_qkv.shape, ffn_conv_w.shape, ln_g.shape, ln_b.shape]
    up_shard = [jnp.swapaxes(ffn_w_up[l], 0, 1).astype(BF16) for l in range(2)]
    qkv_shard = jnp.swapaxes(c_w_qkv[0], 0, 1).astype(BF16)
    down_shard = [ffn_w_down[l].astype(BF16) for l in range(2)]
    g_win, g_small = _comm_only(
        _Comm(gather=[ab_w_in[0].astype(BF16), _pack([a_conv_w, c_b_qkv, ffn_conv_w, ln_g, ln_b])]), "gather_first")
    w_in = _interleave(g_win)
    g_acw, g_bqkv, g_fcw, g_lng, g_lnb = _unpack(g_small, small_shard_shapes)
    acw = _interleave(g_acw[:, 0])
    bqkv = g_bqkv[:, 0].reshape(1, -1)
    fcw = [_interleave(g_fcw[:, l]) for l in range(2)]
    lng = jnp.transpose(g_lng, (1, 2, 0, 3)).reshape(2, 2, 1, D_MODEL)
    lnb = jnp.transpose(g_lnb, (1, 2, 0, 3)).reshape(2, 2, 1, D_MODEL)
    fcb = [ffn_conv_b[l:l + 1] for l in range(2)]
    ms = b_spatial_w[0]
    mst = jnp.swapaxes(ms, 1, 2)
    sbt = b_spatial_b[0].T

    h0, (g_wout,) = _matmul(xt, w_in, "nn", BF16, "mix_in", 1024, 1024, 1024, comm=_Comm(gather=[ab_w_out[0].astype(BF16)]))
    w_out = g_wout.reshape(D_MODEL, D_MODEL)
    cat, (g_wup0,) = _mixer_mid_fwd(h0, acw, a_conv_b, a_norm_g, a_norm_b, b_norm_g, b_norm_b, ms, sbt, "mix_mid_fwd",
                                    comm=_Comm(gather=[up_shard[0]]))
    w_up0 = g_wup0.reshape(2 * D_FF, D_MODEL)
    z1, x1 = _matmul_res_ln(cat, w_out, xt, lng[0, 0], lnb[0, 0], "mix_out_ln", 512)
    hf0, (g_wdown0, g_wqkv) = _matmul(x1, w_up0, "nt", BF16, "ffn0_up", 1024, 1408, 1024,
                                      comm=_Comm(gather=[down_shard[0], qkv_shard]))
    w_down0 = g_wdown0.reshape(D_FF, D_MODEL)
    w_qkv = g_wqkv.reshape(Q_WIDTH + KV_WIDTH, D_MODEL)
    u0, (g_wup1,) = _ffn_mid_fwd(hf0, fcw[0], fcb[0], "ffn0_mid_fwd", comm=_Comm(gather=[up_shard[1]]))
    w_up1 = g_wup1.reshape(2 * D_FF, D_MODEL)
    (z2, x2), (g_wo,) = _matmul_res_ln(u0, w_down0, z1, lng[0, 1], lnb[0, 1], "ffn0_down_ln", 512, prev=(lng[0, 0], lnb[0, 0]),
                                       comm=_Comm(gather=[c_w_o[0].astype(BF16)]))
    w_o = g_wo.reshape(D_MODEL, D_MODEL)
    qkv = _matmul(x2, w_qkv, "nt", BF16, "att_qkv", 1024, 1280, 1024, bias=bqkv)
    att, (g_wdown1,) = _attn_fwd(qkv, c_sinks, "att_fwd", comm=_Comm(gather=[down_shard[1]]))
    w_down1 = g_wdown1.reshape(D_FF, D_MODEL)
    z3, x3 = _matmul_res_ln(att, w_o, z2, lng[1, 0], lnb[1, 0], "att_out_ln", 512, prev=(lng[0, 1], lnb[0, 1]))
    hf1 = _matmul(x3, w_up1, "nt", BF16, "ffn1_up", 1024, 1408, 1024)
    u1 = _ffn_mid_fwd(hf1, fcw[1], fcb[1], "ffn1_mid_fwd")

    dz4, dg11, db11, loss_terms = _matmul_res_ln_loss(u1, w_down1, z3, lng[1, 1], lnb[1, 1], loss_target[0],
                                                      "ffn1_down_ln_loss", 512, prev=(lng[1, 0], lnb[1, 0]))
    dz3, dg10, db10, d_wup1, d_fcw1, d_fcb1, (p_wdown1,) = _ffn_backward(
        dz4, x3, z3, lng[1, 0], hf1, u1, w_up1, fcw[1], fcb[1], w_down1, 1)
    d_att = _matmul(dz3, w_o, "nt", BF16, "att_dout", 1024, 1024, 1024)
    d_wo = _matmul(att, dz3, "tn", BF16, "att_dwo", 1024, 1024, 512)
    half = d_wup1.shape[1] // 2
    (dq, dkv, dbq, dbkv, dsinks), (p_wup1a,) = _attn_bwd(qkv, d_att, c_sinks, "att_bwd",
                                                        comm=_Comm(exchange=[(d_wup1, 0, half)]))
    d_wqkv = jnp.concatenate([_matmul(dq, x2, "tn", BF16, "att_dwq", 1024, 1024, 1024),
                              _matmul(dkv, x2, "tn", BF16, "att_dwkv", KV_WIDTH, 1024, 1024)], axis=0)
    dx2 = _matmul(dkv, w_qkv, "nn", F32, "att_dx_kv", 1024, 1024, KV_WIDTH, b_off=Q_WIDTH // KV_WIDTH)
    dz2, dg01, db01 = _matmul_ln_bwd(dq, w_qkv, z2, lng[0, 1], dz3, "att_dx_q_ln_bwd", 512, res=dx2)
    dz1, dg00, db00, d_wup0, d_fcw0, d_fcb0, (p_wdown0, p_wup1b, p_wqkv, p_wo) = _ffn_backward(
        dz2, x1, z1, lng[0, 0], hf0, u0, w_up0, fcw[0], fcb[0], w_down0, 0, exchange=[(d_wup1, half, half)],
        exchange_late=[d_wqkv.reshape(N_DEV, -1, D_MODEL), d_wo.reshape(N_DEV, -1, D_MODEL)])
    dcat = _matmul(dz1, w_out, "nt", BF16, "mix_dcat", 1024, 1024, 1024)
    d_wout = _matmul(cat, dz1, "tn", BF16, "mix_dwout", 1024, 1024, 512)
    (dh0, d_acw, d_acb, d_ang, d_anb, d_bng, d_bnb, d_ms, d_sb), (p_wup0, p_wout) = _mixer_mid_bwd(
        h0, dcat, acw, a_conv_b, a_norm_g, a_norm_b, b_norm_g, b_norm_b, ms, mst, sbt, "mix_mid_bwd",
        comm=_Comm(exchange=[d_wup0, d_wout.reshape(N_DEV, -1, D_MODEL)]))
    d_bqkv = jnp.concatenate([dbq, dbkv], axis=1)
    d_lng = jnp.stack([jnp.stack([dg00, dg01]), jnp.stack([dg10, dg11])])
    d_lnb = jnp.stack([jnp.stack([db00, db01]), jnp.stack([db10, db11])])
    small_full = [d_acb, d_ang, d_anb, d_bng, d_bnb, d_ms, d_sb, dsinks[:, :N_Q_HEADS], jnp.concatenate([d_fcb0, d_fcb1], axis=0),
                  d_acw, d_bqkv, jnp.stack([d_fcw0, d_fcw1]), d_lng, d_lnb, loss_terms]
    d_win, (g_small_grads,) = _matmul(xt, dh0, "tn", BF16, "mix_dwin", 1024, 1024, 512, comm=_Comm(gather=[_pack(small_full)]))
    grad_x, (p_win,) = _matmul(dh0, w_in, "nt", F32, "mix_dx", 1024, 1024, 1024, res=dz1, res_scale=ALPHA,
                               comm=_Comm(exchange=[_deinterleave(d_win)]))


    big = {}
    for nm, p, w, m, v, tr, transposed in [
            ("ab_w_in", [p_win], ab_w_in, m_ab_w_in, v_ab_w_in, 256, False),
            ("ab_w_out", [p_wout], ab_w_out, m_ab_w_out, v_ab_w_out, 128, False),
            ("c_w_qkv", [p_wqkv], c_w_qkv, m_c_w_qkv, v_c_w_qkv, 160, True), ("c_w_o", [p_wo], c_w_o, m_c_w_o, v_c_w_o, 128, False),
            ("ffn_w_up", [p_wup0, (p_wup1a, 0, half), (p_wup1b, half, half)], ffn_w_up, m_ffn_w_up, v_ffn_w_up, 176, True),
            ("ffn_w_down", [p_wdown0, p_wdown1], ffn_w_down, m_ffn_w_down, v_ffn_w_down, 176, False)]:
        def two_d(a):
            a = jnp.swapaxes(a, 1, 2) if transposed else a
            return a.reshape(-1, a.shape[-1])

        def back(o):
            return jnp.swapaxes(o.reshape(w.shape[0], w.shape[2], w.shape[1]), 1, 2) if transposed else o.reshape(w.shape)

        outs = _adamw_big(p, two_d(w), two_d(m), two_d(v), "adamw_" + nm, tr)
        big[nm] = [back(o) for o in outs]

    *gs, loss_parts = _unpack(g_small_grads, [a.shape for a in small_full])
    loss = 0.5 / D_MODEL * jnp.sum(loss_parts)

    def my_shard(g, width):
        g = g.reshape(g.shape[:-1] + (N_DEV, width))
        return lax.dynamic_index_in_dim(g, me, axis=g.ndim - 2, keepdims=False)

    small_names = ["a_conv_b", "a_norm_g", "a_norm_b", "b_norm_g", "b_norm_b", "b_spatial_w", "b_spatial_b", "c_sinks", "ffn_conv_b",
                   "a_conv_w", "c_b_qkv", "ffn_conv_w", "ln_g", "ln_b"]
    small_w = [a_conv_b, a_norm_g, a_norm_b, b_norm_g, b_norm_b, b_spatial_w, b_spatial_b, c_sinks, ffn_conv_b,
               a_conv_w, c_b_qkv, ffn_conv_w, ln_g, ln_b]
    small_m = [m_a_conv_b, m_a_norm_g, m_a_norm_b, m_b_norm_g, m_b_norm_b, m_b_spatial_w, m_b_spatial_b, m_c_sinks, m_ffn_conv_b,
               m_a_conv_w, m_c_b_qkv, m_ffn_conv_w, m_ln_g, m_ln_b]
    small_v = [v_a_conv_b, v_a_norm_g, v_a_norm_b, v_b_norm_g, v_b_norm_b, v_b_spatial_w, v_b_spatial_b, v_c_sinks, v_ffn_conv_b,
               v_a_conv_w, v_c_b_qkv, v_ffn_conv_w, v_ln_g, v_ln_b]
    gs[9:] = [my_shard(g, w.shape[-1]) for g, w in zip(gs[9:], small_w[9:])]
    two_d = [(-1, w.shape[-1]) for w in small_w]
    outs = _adamw_small([g.reshape((N_DEV,) + w.reshape(s).shape) for g, w, s in zip(gs, small_w, two_d)],
                        [w.reshape(s) for w, s in zip(small_w, two_d)], [m.reshape(s) for m, s in zip(small_m, two_d)],
                        [v.reshape(s) for v, s in zip(small_v, two_d)], "adamw_small")
    small = {nm: [o.reshape(w.shape) for o in outs[4 * a:4 * a + 4]] for a, (nm, w) in enumerate(zip(small_names, small_w))}

    res = {**big, **small}
    order = ["ab_w_in", "a_conv_w", "a_conv_b", "a_norm_g", "a_norm_b", "b_norm_g", "b_norm_b", "b_spatial_w", "b_spatial_b", "ab_w_out",
             "c_w_qkv", "c_b_qkv", "c_sinks", "c_w_o", "ffn_w_up", "ffn_conv_w", "ffn_conv_b", "ffn_w_down", "ln_g", "ln_b"]
    return (loss, grad_x[None], *[res[nm][0] for nm in order], *[res[nm][1] for nm in order],
            *[res[nm][2] for nm in order], *[res[nm][3] for nm in order])
```

```python
import functools
import math

import jax
import jax.numpy as jnp
from jax import lax
from jax.experimental import pallas as pl
from jax.experimental.pallas import tpu as pltpu

F32 = jnp.float32
BF16 = jnp.bfloat16

N_DEV = 8
D_MODEL = 1024
A_WIDTH = 512
A_KERNEL = 31
B_GROUPS = 4
B_CHUNK = 128
HEAD_DIM = 64
N_Q_HEADS = 16
N_KV_HEADS = 2
ATT_BLOCK = 128
D_FF = 2816
FFN_KERNEL = 3
ALPHA = (2.0 * 2) ** 0.25
LN_EPS = 1e-5
GELU_K = math.sqrt(2.0 / math.pi)
GELU_C = 0.044715
ADAM_LR = 0.001
ADAM_B1 = 0.9
ADAM_B2 = 0.999
ADAM_EPS = 1e-08
ADAM_WD = 0.01
ADAM_STEP = 10
VMEM_LIMIT = 56 * 1024 * 1024
MESH_ID = pl.DeviceIdType.MESH


def _params(*sem):
    return pltpu.CompilerParams(dimension_semantics=sem, vmem_limit_bytes=VMEM_LIMIT)


def _gelu(x):
    t = jnp.tanh(GELU_K * x * (1.0 + GELU_C * x * x))
    return 0.5 * x * (1.0 + t)


def _gelu_and_grad(x):
    x2 = x * x
    t = jnp.tanh(GELU_K * x * (1.0 + GELU_C * x2))
    g = 0.5 * x * (1.0 + t)
    dg = 0.5 * (1.0 + t) + 0.5 * x * (1.0 - t * t) * (GELU_K * (1.0 + 3.0 * GELU_C * x2))
    return g, dg


def _sigmoid(x):
    return 1.0 / (1.0 + jnp.exp(-x))


def _ln_stats(z):
    mu = jnp.mean(z, axis=-1, keepdims=True)
    zc = z - mu
    var = jnp.mean(zc * zc, axis=-1, keepdims=True)
    r = lax.rsqrt(var + LN_EPS)
    return zc * r, r


def _ln_bwd_rows(dn, nh, r):
    return r * (dn - jnp.mean(dn, axis=-1, keepdims=True) - nh * jnp.mean(dn * nh, axis=-1, keepdims=True))


def _colsum(x):
    return jnp.sum(x, axis=0, keepdims=True)


def _dot(a, b, dims):
    return lax.dot_general(a.astype(BF16), b.astype(BF16), (dims, ((), ())), preferred_element_type=F32)


NN = ((1,), (0,))
NT = ((1,), (1,))
TN = ((0,), (0,))


ANY = pl.BlockSpec(memory_space=pl.ANY)
N_RELATIONS = N_DEV - 1


def _my_place():
    return lax.axis_index("x"), lax.axis_index("y"), lax.axis_index("c")


class _Comm:
    def __init__(self, gather=(), exchange=()):
        exchange = [e if isinstance(e, tuple) else (e, 0, e.shape[1]) for e in exchange]
        self.arrs = list(gather) + [e[0] for e in exchange]
        self.n_gather = len(gather)
        self.n = len(self.arrs)
        self.rows = [None] * self.n_gather + [pl.ds(lo, n) for _, lo, n in exchange]

    def out_shape(self):
        return [jax.ShapeDtypeStruct(((N_DEV,) + a.shape) if i < self.n_gather else a.shape, a.dtype)
                for i, a in enumerate(self.arrs)]

    def sems(self):
        return [pltpu.SemaphoreType.DMA((self.n, N_RELATIONS)), pltpu.SemaphoreType.DMA((self.n, N_RELATIONS)),
                pltpu.SemaphoreType.DMA((self.n,))]

    def _gather_copy(self, ins, outs, sems, a, k, place, to, from_input=False):
        px, py, pc = place
        block = outs[a].at[4 * px + 2 * py + pc]
        return pltpu.make_async_remote_copy(
            src_ref=ins[a] if from_input else block, dst_ref=block,
            send_sem=sems[0].at[a, k], recv_sem=sems[1].at[a, k], device_id=to, device_id_type=MESH_ID)

    def _exchange_copy(self, ins, outs, sems, a, k, landing=False):
        x, y, c = _my_place()
        me = 4 * x + 2 * y + c
        peer = (x ^ (k >> 2), y ^ ((k >> 1) & 1), c ^ (k & 1))
        return pltpu.make_async_remote_copy(
            src_ref=ins[a].at[me ^ k, self.rows[a]], dst_ref=outs[a].at[(me ^ k) if landing else me, self.rows[a]],
            send_sem=sems[0].at[a, k - 1], recv_sem=sems[1].at[a, k - 1], device_id=peer, device_id_type=MESH_ID)

    def _local_copy(self, ins, outs, sems, a):
        x, y, c = _my_place()
        me = 4 * x + 2 * y + c
        if a < self.n_gather:
            return pltpu.make_async_copy(ins[a], outs[a].at[me], sems[2].at[a])
        return pltpu.make_async_copy(ins[a].at[me, self.rows[a]], outs[a].at[me, self.rows[a]], sems[2].at[a])

    def _first_stage(self, ins, outs, sems, a):
        x, y, c = _my_place()
        me = (x, y, c)
        chips = [(1 - x, y), (x, 1 - y), (1 - x, 1 - y)]
        return ([self._gather_copy(ins, outs, sems, a, 0, me, (x, y, 1 - c), from_input=True)]
                + [self._gather_copy(ins, outs, sems, a, 1 + j, me, (*chip, c), from_input=True) for j, chip in enumerate(chips)])

    def start(self, ins, outs, sems):
        for a in range(self.n):
            self._local_copy(ins, outs, sems, a).start()
        for a in range(self.n_gather):
            for cp in self._first_stage(ins, outs, sems, a):
                cp.start()
        for k in range(1, N_DEV):
            for a in range(self.n_gather, self.n):
                self._exchange_copy(ins, outs, sems, a, k).start()

    def forward(self, ins, outs, sems):
        x, y, c = _my_place()
        me, sibling = (x, y, c), (x, y, 1 - c)
        for j, chip in enumerate([(1 - x, y), (x, 1 - y), (1 - x, 1 - y)]):
            for a in range(self.n_gather):
                self._gather_copy(ins, outs, sems, a, 1 + j, (*chip, c), me).wait_recv()
                self._gather_copy(ins, outs, sems, a, 4 + j, (*chip, c), sibling).start()

    def finish(self, ins, outs, sems):
        x, y, c = _my_place()
        me, sibling = (x, y, c), (x, y, 1 - c)
        chips = [(1 - x, y), (x, 1 - y), (1 - x, 1 - y)]
        passed = [self._gather_copy(ins, outs, sems, a, 4 + j, (*chip, c), sibling)
                  for j, chip in enumerate(chips) for a in range(self.n_gather)]
        for a in range(self.n_gather):
            self._gather_copy(ins, outs, sems, a, 0, sibling, me).wait_recv()
            for j, chip in enumerate(chips):
                self._gather_copy(ins, outs, sems, a, 4 + j, (*chip, 1 - c), me).wait_recv()
        for k in range(1, N_DEV):
            for a in range(self.n_gather, self.n):
                self._exchange_copy(ins, outs, sems, a, k, landing=True).wait_recv()
        for a in range(self.n_gather):
            for cp in self._first_stage(ins, outs, sems, a):
                cp.wait_send()
        for cp in passed:
            cp.wait_send()
        for k in range(1, N_DEV):
            for a in range(self.n_gather, self.n):
                self._exchange_copy(ins, outs, sems, a, k).wait_send()
        for a in range(self.n):
            self._local_copy(ins, outs, sems, a).wait()


def _comm_only(comm, name):
    def body(*refs):
        ins, outs, sems = refs[:comm.n], refs[comm.n:2 * comm.n], refs[2 * comm.n:]
        comm.start(ins, outs, sems)
        comm.forward(ins, outs, sems)
        comm.finish(ins, outs, sems)

    return pl.pallas_call(body, name=name, in_specs=[ANY] * comm.n, out_specs=[ANY] * comm.n,
                          out_shape=comm.out_shape(), scratch_shapes=comm.sems())(*comm.arrs)


def _call(body, *, name, grid, in_specs, out_specs, out_shape, args, sem, scratch_shapes=(), comm=None):
    in_specs, out_specs, out_shape, scratch_shapes = list(in_specs), list(out_specs), list(out_shape), list(scratch_shapes)
    if comm is None:
        outs = pl.pallas_call(body, name=name, grid=grid, in_specs=in_specs, out_specs=out_specs, out_shape=out_shape,
                              scratch_shapes=scratch_shapes, compiler_params=_params(*sem))(*args)
        return list(outs), []
    n_in, n_out, n_scr, nc = len(in_specs), len(out_specs), len(scratch_shapes), comm.n

    def wrapped(*refs):
        ins, refs = refs[:n_in], refs[n_in:]
        c_in, refs = refs[:nc], refs[nc:]
        outs, refs = refs[:n_out], refs[n_out:]
        c_out, refs = refs[:nc], refs[nc:]
        scr, sems = refs[:n_scr], refs[n_scr:]
        step = functools.reduce(lambda acc, ax: acc * grid[ax] + pl.program_id(ax), range(len(grid)), 0)
        steps = math.prod(grid)

        @pl.when(step == 0)
        def _():
            comm.start(c_in, c_out, sems)

        @pl.when(step == steps // 2)
        def _():
            comm.forward(c_in, c_out, sems)

        body(*ins, *outs, *scr)

        @pl.when(step == steps - 1)
        def _():
            comm.finish(c_in, c_out, sems)

    outs = pl.pallas_call(
        wrapped, name=name, grid=grid, in_specs=in_specs + [ANY] * nc, out_specs=out_specs + [ANY] * nc,
        out_shape=out_shape + comm.out_shape(), scratch_shapes=scratch_shapes + comm.sems(),
        compiler_params=_params(*(["arbitrary"] * len(grid))))(*args, *comm.arrs)
    return list(outs[:n_out]), list(outs[n_out:])


def _matmul(a, b, mode, out_dtype, name, tm, tn, tk, *, bias=None, res=None, res_scale=1.0, b_off=0, comm=None):
    tm = min(tm, a.shape[1] if mode == "tn" else a.shape[0])
    tk = min(tk, a.shape[0] if mode == "tn" else a.shape[1])
    if mode == "nn":
        (m, k), n = a.shape, b.shape[1]
        a_spec = pl.BlockSpec((tm, tk), lambda i, j, kk: (i, kk))
        b_spec = pl.BlockSpec((tk, tn), lambda i, j, kk: (kk + b_off, j))
        dims = NN
    elif mode == "nt":
        (m, k), n = a.shape, b.shape[0]
        a_spec = pl.BlockSpec((tm, tk), lambda i, j, kk: (i, kk))
        b_spec = pl.BlockSpec((tn, tk), lambda i, j, kk: (j, kk + b_off))
        dims = NT
    else:
        (k, m), n = a.shape, b.shape[1]
        a_spec = pl.BlockSpec((tk, tm), lambda i, j, kk: (kk, i))
        b_spec = pl.BlockSpec((tk, tn), lambda i, j, kk: (kk, j))
        dims = TN
    assert m % tm == 0 and n % tn == 0 and k % tk == 0, (name, m, n, k)
    nk = k // tk
    in_specs = [a_spec, b_spec]
    args = [a, b]
    if bias is not None:
        in_specs.append(pl.BlockSpec((1, tn), lambda i, j, kk: (0, j)))
        args.append(bias)
    if res is not None:
        in_specs.append(pl.BlockSpec((tm, tn), lambda i, j, kk: (i, j)))
        args.append(res)

    def finish(out, refs, o_ref):
        pos = 2
        if bias is not None:
            out = out + refs[pos][...]
            pos += 1
        if res is not None:
            out = out + res_scale * refs[pos][...].astype(F32)
        o_ref[...] = out.astype(out_dtype)

    def body_one_step(*refs):
        finish(_dot(refs[0][...], refs[1][...], dims), refs, refs[-1])

    def body(*refs):
        a_ref, b_ref = refs[0], refs[1]
        o_ref, acc = refs[-2], refs[-1]
        kk = pl.program_id(2)

        @pl.when(kk == 0)
        def _():
            acc[...] = jnp.zeros_like(acc)

        acc[...] += _dot(a_ref[...], b_ref[...], dims)

        @pl.when(kk == nk - 1)
        def _():
            finish(acc[...], refs, o_ref)

    (out,), moved = _call(
        body_one_step if nk == 1 else body, name=name, grid=(m // tm, n // tn, nk),
        in_specs=in_specs, out_specs=[pl.BlockSpec((tm, tn), lambda i, j, kk: (i, j))],
        out_shape=[jax.ShapeDtypeStruct((m, n), out_dtype)],
        scratch_shapes=[] if nk == 1 else [pltpu.VMEM((tm, tn), F32)],
        sem=("parallel", "parallel", "arbitrary"), args=args, comm=comm)
    return out if comm is None else (out, moved)


def _matmul_tn_pair(a0, a1, b, out_dtype, name, tm, tn, tk, comm=None):
    (k, m), n = a0.shape, b.shape[1]
    tk = min(tk, k)
    assert a1.shape == a0.shape and m % tm == 0 and n % tn == 0 and k % tk == 0, (name, m, n, k)
    mi, nk = m // tm, k // tk

    def body(a0_ref, a1_ref, b_ref, o_ref, acc):
        i, kk = pl.program_id(0), pl.program_id(2)

        @pl.when(kk == 0)
        def _():
            acc[...] = jnp.zeros_like(acc)

        @pl.when(i < mi)
        def _():
            acc[...] += _dot(a0_ref[...], b_ref[...], TN)

        @pl.when(i >= mi)
        def _():
            acc[...] += _dot(a1_ref[...], b_ref[...], TN)

        @pl.when(kk == nk - 1)
        def _():
            o_ref[...] = acc[...].astype(out_dtype)

    (out,), moved = _call(
        body, name=name, grid=(2 * mi, n // tn, nk),
        in_specs=[pl.BlockSpec((tk, tm), lambda i, j, kk: (jnp.where(i < mi, kk, nk - 1), jnp.minimum(i, mi - 1))),
                  pl.BlockSpec((tk, tm), lambda i, j, kk: (jnp.where(i >= mi, kk, 0), jnp.maximum(i - mi, 0))),
                  pl.BlockSpec((tk, tn), lambda i, j, kk: (kk, j))],
        out_specs=[pl.BlockSpec((tm, tn), lambda i, j, kk: (i, j))],
        out_shape=[jax.ShapeDtypeStruct((2 * m, n), out_dtype)],
        scratch_shapes=[pltpu.VMEM((tm, tn), F32)],
        sem=("parallel", "parallel", "arbitrary"), args=(a0, a1, b), comm=comm)
    return out if comm is None else (out, moved)


def _residual_input(x_ref, prev_refs):
    if not prev_refs:
        return x_ref[...]
    nh, _ = _ln_stats(x_ref[...])
    return nh * prev_refs[0][...] + prev_refs[1][...]


def _matmul_res_ln(a, b, x, g, beta, name, tm, prev=None, comm=None):
    t, k = a.shape
    d = b.shape[1]
    tm = min(tm, t)
    assert t % tm == 0
    n_prev = 0 if prev is None else 2

    def body(a_ref, b_ref, x_ref, g_ref, beta_ref, *rest):
        z_ref, xo_ref = rest[n_prev:]
        z = ALPHA * _residual_input(x_ref, rest[:n_prev]) + _dot(a_ref[...], b_ref[...], NN)
        nh, _ = _ln_stats(z)
        z_ref[...] = z
        xo_ref[...] = (nh * g_ref[...] + beta_ref[...]).astype(BF16)

    row = pl.BlockSpec((tm, d), lambda i: (i, 0))
    vec = pl.BlockSpec((1, d), lambda i: (0, 0))
    outs, moved = _call(
        body, name=name, grid=(t // tm,),
        in_specs=[pl.BlockSpec((tm, k), lambda i: (i, 0)), pl.BlockSpec((k, d), lambda i: (0, 0)), row, vec, vec] + [vec] * n_prev,
        out_specs=[row, row],
        out_shape=[jax.ShapeDtypeStruct((t, d), F32), jax.ShapeDtypeStruct((t, d), BF16)],
        sem=("parallel",), args=(a, b, x, g, beta, *(prev or ())), comm=comm)
    return outs if comm is None else (outs, moved)


def _matmul_ln_bwd(a, b, z, g, dres, name, tm, *, res=None, b_off=0):
    m, k = a.shape
    d = b.shape[1]
    tm = min(tm, m)
    assert m % tm == 0

    def body(*refs):
        a_ref, b_ref, z_ref, g_ref, dres_ref = refs[:5]
        dz_ref, dg_ref, db_ref = refs[-3:]

        @pl.when(pl.program_id(0) == 0)
        def _():
            dg_ref[...] = jnp.zeros_like(dg_ref)
            db_ref[...] = jnp.zeros_like(db_ref)

        dbr = _dot(a_ref[...], b_ref[...], NN)
        if res is not None:
            dbr = dbr + refs[5][...]
        nh, r = _ln_stats(z_ref[...])
        dy = ALPHA * dres_ref[...] + dbr
        dg_ref[...] += _colsum(dy * nh)
        db_ref[...] += _colsum(dy)
        dz_ref[...] = _ln_bwd_rows(dy * g_ref[...], nh, r)

    row = pl.BlockSpec((tm, d), lambda i: (i, 0))
    vec = pl.BlockSpec((1, d), lambda i: (0, 0))
    vshape = jax.ShapeDtypeStruct((1, d), F32)
    return pl.pallas_call(
        body, name=name, grid=(m // tm,),
        in_specs=[pl.BlockSpec((tm, k), lambda i: (i, 0)), pl.BlockSpec((k, d), lambda i: (b_off, 0)), row, vec, row]
        + ([row] if res is not None else []),
        out_specs=[row, vec, vec], out_shape=[jax.ShapeDtypeStruct((m, d), F32), vshape, vshape],
        compiler_params=_params("arbitrary"),
    )(a, b, z, g, dres, *([res] if res is not None else []))


def _matmul_res_ln_loss(a, b, x, g, beta, target, name, tm, prev):
    t, k = a.shape
    d = b.shape[1]
    tm = min(tm, t)

    def body(a_ref, b_ref, x_ref, g_ref, beta_ref, t_ref, gp_ref, bp_ref, dz_ref, dg_ref, db_ref, loss_ref):
        @pl.when(pl.program_id(0) == 0)
        def _():
            dg_ref[...] = jnp.zeros_like(dg_ref)
            db_ref[...] = jnp.zeros_like(db_ref)
            loss_ref[...] = jnp.zeros_like(loss_ref)

        nh, r = _ln_stats(ALPHA * _residual_input(x_ref, (gp_ref, bp_ref)) + _dot(a_ref[...], b_ref[...], NN))
        err = nh * g_ref[...] + beta_ref[...] - t_ref[...]
        loss_ref[...] += _colsum(err * err)
        dy = err * (1.0 / d)
        dg_ref[...] += _colsum(dy * nh)
        db_ref[...] += _colsum(dy)
        dz_ref[...] = _ln_bwd_rows(dy * g_ref[...], nh, r)

    row = pl.BlockSpec((tm, d), lambda i: (i, 0))
    vec = pl.BlockSpec((1, d), lambda i: (0, 0))
    vshape = jax.ShapeDtypeStruct((1, d), F32)
    return pl.pallas_call(
        body, name=name, grid=(t // tm,),
        in_specs=[pl.BlockSpec((tm, k), lambda i: (i, 0)), pl.BlockSpec((k, d), lambda i: (0, 0)), row, vec, vec, row, vec, vec],
        out_specs=[row, vec, vec, vec],
        out_shape=[jax.ShapeDtypeStruct((t, d), F32), vshape, vshape, vshape],
        compiler_params=_params("arbitrary"),
    )(a, b, x, g, beta, target, *prev)


FFN_HALO = 16
FFN_CHUNK = 256
LANES = 128
SUBLANES = 8


def _rows_up(e, start, rows):
    if start % SUBLANES == 0:
        return e[start:start + rows]
    return pltpu.roll(e, e.shape[0] - start, 0)[0:rows]


def _fold(x):
    return jnp.sum(x.reshape(x.shape[0] // SUBLANES, SUBLANES, x.shape[1]), axis=0)


def _ffn_mid_fwd(h, cw, cb, name, tm=1024, tc=1408, comm=None):
    t, f2 = h.shape
    tm = min(tm, t)
    f = f2 // 2
    nj, nt, hb = f // tc, t // tm, tm // FFN_HALO

    ch = min(FFN_CHUNK, tm)

    def body(hg, hgp, hv, hvp, cwg, cwv, cbg, cbv, u_ref):
        i = pl.program_id(1)
        o = FFN_HALO - FFN_KERNEL + 1
        for lg in range(tc // LANES):
            cols = slice(lg * LANES, (lg + 1) * LANES)
            wg, wv = [cwg[k:k + 1, cols] for k in range(FFN_KERNEL)], [cwv[k:k + 1, cols] for k in range(FFN_KERNEL)]
            bg, bv = cbg[:, cols], cbv[:, cols]

            def emit(base, eg, ev):
                cg = wg[0] * _rows_up(eg, o, ch) + wg[1] * _rows_up(eg, o + 1, ch) + wg[2] * _rows_up(eg, o + 2, ch) + bg
                cv = wv[0] * _rows_up(ev, o, ch) + wv[1] * _rows_up(ev, o + 1, ch) + wv[2] * _rows_up(ev, o + 2, ch) + bv
                u_ref[pl.ds(base, ch), cols] = (_gelu(cg) * cv).astype(BF16)

            def first(main, prev):
                return jnp.concatenate([jnp.where(i > 0, prev[:, cols].astype(F32), 0.0), main[0:ch, cols].astype(F32)], axis=0)

            def inner(c, carry):
                base = pl.multiple_of(c * ch, ch)
                emit(base, hg[pl.ds(base - FFN_HALO, ch + FFN_HALO), cols].astype(F32),
                     hv[pl.ds(base - FFN_HALO, ch + FFN_HALO), cols].astype(F32))
                return carry

            emit(0, first(hg, hgp), first(hv, hvp))
            if tm > ch:
                lax.fori_loop(1, tm // ch, inner, 0)

    def main_spec(off):
        return pl.BlockSpec((tm, tc), lambda j, i: (i, j + off))

    def prev_spec(off):
        return pl.BlockSpec((FFN_HALO, tc), lambda j, i: (jnp.maximum(i * hb - 1, 0), j + off))

    def par_spec(rows, off):
        return pl.BlockSpec((rows, tc), lambda j, i: (0, j + off))

    (u,), moved = _call(
        body, name=name, grid=(nj, nt),
        in_specs=[main_spec(0), prev_spec(0), main_spec(nj), prev_spec(nj),
                  par_spec(FFN_KERNEL, 0), par_spec(FFN_KERNEL, nj), par_spec(1, 0), par_spec(1, nj)],
        out_specs=[pl.BlockSpec((tm, tc), lambda j, i: (i, j))],
        out_shape=[jax.ShapeDtypeStruct((t, f), BF16)],
        sem=("parallel", "arbitrary"), args=(h, h, h, h, cw, cw, cb, cb), comm=comm)
    return u if comm is None else (u, moved)


def _ffn_mid_bwd(h, du, cw, cb, name, tm=1024, tc=1408, comm=None):
    t, f2 = h.shape
    tm = min(tm, t)
    f = f2 // 2
    nj, nt, hb = f // tc, t // tm, tm // FFN_HALO

    ch = min(FFN_CHUNK, tm)
    ahead = ch + SUBLANES

    n_ch = tm // ch

    def body(hg, hgp, hgn, hv, hvp, hvn, du_ref, dun_ref, cwg, cwv, cbg, cbv,
             dhg_ref, dhv_ref, dcwg_ref, dcwv_ref, dcbg_ref, dcbv_ref):
        i = pl.program_id(1)

        @pl.when(i == 0)
        def _():
            for ref in (dcwg_ref, dcwv_ref, dcbg_ref, dcbv_ref):
                ref[...] = jnp.zeros_like(ref)

        o = FFN_HALO - FFN_KERNEL + 1
        for lg in range(tc // LANES):
            cols = slice(lg * LANES, (lg + 1) * LANES)
            wg, wv = [cwg[k:k + 1, cols] for k in range(FFN_KERNEL)], [cwv[k:k + 1, cols] for k in range(FFN_KERNEL)]
            bg, bv = cbg[:, cols], cbv[:, cols]

            def emit(base, eg, ev, du_e, acc):
                hgs = [_rows_up(eg, o + k, ahead) for k in range(FFN_KERNEL)]
                hvs = [_rows_up(ev, o + k, ahead) for k in range(FFN_KERNEL)]
                cg = wg[0] * hgs[0] + wg[1] * hgs[1] + wg[2] * hgs[2] + bg
                cv = wv[0] * hvs[0] + wv[1] * hvs[1] + wv[2] * hvs[2] + bv
                du_a = du_e[0:ahead]
                gl, dgl = _gelu_and_grad(cg)

                def back(d, hs, w, dh_ref):
                    own = d[0:ch]
                    dh = w[2] * own + w[1] * _rows_up(d, 1, ch) + w[0] * _rows_up(d, 2, ch)
                    dh_ref[pl.ds(base, ch), cols] = dh.astype(BF16)
                    return [_fold(own)] + [_fold(own * hs[k][0:ch]) for k in range(FFN_KERNEL)]

                sums = back(du_a * cv * dgl, hgs, wg, dhg_ref) + back(du_a * gl, hvs, wv, dhv_ref)
                return tuple(a + s_ for a, s_ in zip(acc, sums))

            def edge(c, acc):
                def rows(main, before, after, lo, hi):
                    parts = [] if lo >= 0 else [jnp.where(i > 0, before[:, cols].astype(F32), 0.0)]
                    parts.append(main[max(lo, 0):min(hi, tm), cols].astype(F32))
                    if hi > tm:
                        parts.append(after)
                    return jnp.concatenate(parts, axis=0) if len(parts) > 1 else parts[0]

                lo, hi = c * ch - FFN_HALO, (c + 1) * ch + FFN_HALO
                du_next = jnp.where(i < nt - 1, dun_ref[:, cols].astype(F32), 0.0)
                return emit(c * ch, rows(hg, hgp, hgn[:, cols].astype(F32), lo, hi), rows(hv, hvp, hvn[:, cols].astype(F32), lo, hi),
                            rows(du_ref, None, du_next, c * ch, hi), acc)

            def inner(c, acc):
                base = pl.multiple_of(c * ch, ch)
                return emit(base, hg[pl.ds(base - FFN_HALO, ch + 2 * FFN_HALO), cols].astype(F32),
                            hv[pl.ds(base - FFN_HALO, ch + 2 * FFN_HALO), cols].astype(F32),
                            du_ref[pl.ds(base, ch + FFN_HALO), cols].astype(F32), acc)

            zero = jnp.zeros((SUBLANES, LANES), F32)
            acc = edge(0, (zero,) * (2 * (1 + FFN_KERNEL)))
            if n_ch > 2:
                acc = lax.fori_loop(1, n_ch - 1, inner, acc)
            if n_ch > 1:
                acc = edge(n_ch - 1, acc)
            dcbg_ref[:, cols] += _colsum(acc[0])
            dcbv_ref[:, cols] += _colsum(acc[1 + FFN_KERNEL])
            for k in range(FFN_KERNEL):
                dcwg_ref[k:k + 1, cols] += _colsum(acc[1 + k])
                dcwv_ref[k:k + 1, cols] += _colsum(acc[2 + FFN_KERNEL + k])

    last_blk = t // FFN_HALO - 1

    def main_spec(off):
        return pl.BlockSpec((tm, tc), lambda j, i: (i, j + off))

    def prev_spec(off):
        return pl.BlockSpec((FFN_HALO, tc), lambda j, i: (jnp.maximum(i * hb - 1, 0), j + off))

    def next_spec(off):
        return pl.BlockSpec((FFN_HALO, tc), lambda j, i: (jnp.minimum((i + 1) * hb, last_blk), j + off))

    def par_spec(rows, off):
        return pl.BlockSpec((rows, tc), lambda j, i: (0, j + off))

    out_tile = pl.BlockSpec((tm, tc), lambda j, i: (i, j))
    outs, moved = _call(
        body, name=name, grid=(nj, nt),
        in_specs=[main_spec(0), prev_spec(0), next_spec(0), main_spec(nj), prev_spec(nj), next_spec(nj),
                  main_spec(0), next_spec(0),
                  par_spec(FFN_KERNEL, 0), par_spec(FFN_KERNEL, nj), par_spec(1, 0), par_spec(1, nj)],
        out_specs=[out_tile, out_tile, par_spec(FFN_KERNEL, 0), par_spec(FFN_KERNEL, 0), par_spec(1, 0), par_spec(1, 0)],
        out_shape=[jax.ShapeDtypeStruct((t, f), BF16), jax.ShapeDtypeStruct((t, f), BF16),
                   jax.ShapeDtypeStruct((FFN_KERNEL, f), F32), jax.ShapeDtypeStruct((FFN_KERNEL, f), F32),
                   jax.ShapeDtypeStruct((1, f), F32), jax.ShapeDtypeStruct((1, f), F32)],
        sem=("parallel", "arbitrary"), args=(h, h, h, h, h, h, du, du, cw, cw, cb, cb), comm=comm)
    return outs if comm is None else (outs, moved)


MIX_HALO = 32


def _glu(hh):
    return hh[:, 0:A_WIDTH] * _sigmoid(hh[:, A_WIDTH:2 * A_WIDTH])


def _fill_row_shifts(s):
    rows = s.shape[1] - SUBLANES
    for j in range(1, SUBLANES):
        s[j, 0:rows, :] = s[0, pl.ds(j, rows), :]


def _rows_from(s, start, rows):
    j = start % SUBLANES
    return s[j, start - j:start - j + rows, :]


def _tril_mask():
    return lax.broadcasted_iota(jnp.int32, (B_CHUNK, B_CHUNK), 0) >= lax.broadcasted_iota(jnp.int32, (B_CHUNK, B_CHUNK), 1)


def _spatial_mix(q, ms_ref, sbt_ref, tm):
    mask = _tril_mask()
    ws = [jnp.where(mask, ms_ref[g], 0.0).astype(BF16) for g in range(B_GROUPS)]
    qb = q.astype(BF16)
    rows = []
    for c in range(tm // B_CHUNK):
        cols = [_dot(ws[g], qb[c * B_CHUNK:(c + 1) * B_CHUNK, g * 128:(g + 1) * 128], NN) + sbt_ref[:, g:g + 1]
                for g in range(B_GROUPS)]
        rows.append(jnp.concatenate(cols, axis=1))
    return jnp.concatenate(rows, axis=0)


def _mixer_mid_fwd(h, cw, cb, ag, ab, bg, bb, ms, sbt, name, tm=256, comm=None):
    t = h.shape[0]
    nt, hb = t // tm, tm // MIX_HALO
    o = MIX_HALO - A_KERNEL + 1

    def body(h_ref, hp_ref, cw_ref, cb_ref, ag_ref, ab_ref, bg_ref, bb_ref, ms_ref, sbt_ref, cat_ref, sp):
        i = pl.program_id(0)
        sp[0, 0:MIX_HALO, :] = jnp.where(i > 0, _glu(hp_ref[:, 0:2 * A_WIDTH].astype(F32)), 0.0)
        sp[0, MIX_HALO:, :] = _glu(h_ref[:, 0:2 * A_WIDTH].astype(F32))
        _fill_row_shifts(sp)
        y = jnp.zeros((tm, A_WIDTH), F32) + cb_ref[...]
        for k in range(A_KERNEL):
            y = y + cw_ref[k:k + 1, :] * _rows_from(sp, o + k, tm)
        nh, _ = _ln_stats(y)
        ln = nh * ag_ref[...] + ab_ref[...]
        cat_ref[:, 0:A_WIDTH] = (ln * _sigmoid(ln)).astype(BF16)
        u = _gelu(h_ref[:, 1024:1536].astype(F32))
        nb, _ = _ln_stats(_gelu(h_ref[:, 1536:2048].astype(F32)))
        mixed = _spatial_mix(nb * bg_ref[...] + bb_ref[...], ms_ref, sbt_ref, tm)
        cat_ref[:, A_WIDTH:] = (u * mixed).astype(BF16)

    vec = pl.BlockSpec((1, A_WIDTH), lambda i: (0, 0))
    (cat,), moved = _call(
        body, name=name, grid=(nt,),
        in_specs=[pl.BlockSpec((tm, 2048), lambda i: (i, 0)),
                  pl.BlockSpec((MIX_HALO, 2048), lambda i: (jnp.maximum(i * hb - 1, 0), 0)),
                  pl.BlockSpec((A_KERNEL, A_WIDTH), lambda i: (0, 0)), vec, vec, vec, vec, vec,
                  pl.BlockSpec((B_GROUPS, B_CHUNK, B_CHUNK), lambda i: (0, 0, 0)),
                  pl.BlockSpec((B_CHUNK, B_GROUPS), lambda i: (0, 0))],
        out_specs=[pl.BlockSpec((tm, D_MODEL), lambda i: (i, 0))],
        out_shape=[jax.ShapeDtypeStruct((t, D_MODEL), BF16)],
        scratch_shapes=[pltpu.VMEM((SUBLANES, tm + MIX_HALO, A_WIDTH), F32)],
        sem=("parallel",), args=(h, h, cw, cb, ag, ab, bg, bb, ms, sbt), comm=comm)
    return cat if comm is None else (cat, moved)


def _mixer_mid_bwd(h, dcat, cw, cb, ag, ab, bg, bb, ms, mst, sbt, name, tm=256, comm=None):
    t = h.shape[0]
    nt, hb = t // tm, tm // MIX_HALO
    o = MIX_HALO - A_KERNEL + 1
    r = tm + MIX_HALO
    nchunk = tm // B_CHUNK

    def body(h_ref, hp_ref, hn_ref, dc_ref, dcn_ref, cw_ref, cb_ref, ag_ref, ab_ref, bg_ref, bb_ref, ms_ref, mst_ref, sbt_ref,
             dh_ref, dcw_ref, dcb_ref, dag_ref, dab_ref, dbg_ref, dbb_ref, dms_ref, dsb_ref, sp, sdy, sbacc):
        i = pl.program_id(0)

        @pl.when(i == 0)
        def _():
            for ref in (dcw_ref, dcb_ref, dag_ref, dab_ref, dbg_ref, dbb_ref, dms_ref, dsb_ref, sbacc):
                ref[...] = jnp.zeros_like(ref)

        sp[0, 0:MIX_HALO, :] = jnp.where(i > 0, _glu(hp_ref[:, 0:2 * A_WIDTH].astype(F32)), 0.0)
        sp[0, MIX_HALO:MIX_HALO + tm, :] = _glu(h_ref[:, 0:2 * A_WIDTH].astype(F32))
        sp[0, MIX_HALO + tm:, :] = _glu(hn_ref[:, 0:2 * A_WIDTH].astype(F32))
        _fill_row_shifts(sp)
        y = jnp.zeros((r, A_WIDTH), F32) + cb_ref[...]
        for k in range(A_KERNEL):
            y = y + cw_ref[k:k + 1, :] * _rows_from(sp, o + k, r)
        nh, rs = _ln_stats(y)
        ln = nh * ag_ref[...] + ab_ref[...]
        sg = _sigmoid(ln)
        dao = jnp.concatenate([dc_ref[:, 0:A_WIDTH].astype(F32),
                               jnp.where(i < nt - 1, dcn_ref[:, 0:A_WIDTH].astype(F32), 0.0)], axis=0)
        dln = dao * (sg * (1.0 + ln * (1.0 - sg)))
        dag_ref[...] += _colsum(dln[0:tm] * nh[0:tm])
        dab_ref[...] += _colsum(dln[0:tm])
        sdy[0] = _ln_bwd_rows(dln * ag_ref[...], nh, rs)
        _fill_row_shifts(sdy)
        dy_own = sdy[0, 0:tm, :]
        dcb_ref[...] += _colsum(dy_own)
        dp = jnp.zeros((tm, A_WIDTH), F32)
        for k in range(A_KERNEL):
            dcw_ref[k:k + 1, :] += _colsum(dy_own * _rows_from(sp, o + k, tm))
            dp = dp + cw_ref[k:k + 1, :] * _rows_from(sdy, A_KERNEL - 1 - k, tm)
        av = h_ref[:, 0:A_WIDTH].astype(F32)
        s = _sigmoid(h_ref[:, A_WIDTH:2 * A_WIDTH].astype(F32))
        dh_ref[:, 0:A_WIDTH] = (dp * s).astype(BF16)
        dh_ref[:, A_WIDTH:2 * A_WIDTH] = (dp * av * s * (1.0 - s)).astype(BF16)

        u, dgu = _gelu_and_grad(h_ref[:, 1024:1536].astype(F32))
        w, dgw = _gelu_and_grad(h_ref[:, 1536:2048].astype(F32))
        nb, rb = _ln_stats(w)
        q = nb * bg_ref[...] + bb_ref[...]
        mixed = _spatial_mix(q, ms_ref, sbt_ref, tm)
        dbo = dc_ref[:, A_WIDTH:].astype(F32)
        dh_ref[:, 1024:1536] = (dbo * mixed * dgu).astype(BF16)
        dmx = dbo * u
        mask = _tril_mask()
        wst = [jnp.where(mask.T, mst_ref[g], 0.0).astype(BF16) for g in range(B_GROUPS)]
        qb = q.astype(BF16)
        dmb = dmx.astype(BF16)
        rows = []
        for c in range(nchunk):
            cols = []
            for g in range(B_GROUPS):
                rs_, cs_ = slice(c * B_CHUNK, (c + 1) * B_CHUNK), slice(g * 128, (g + 1) * 128)
                sbacc[g] += dmx[rs_, cs_]
                dms_ref[g] += _dot(dmb[rs_, cs_], qb[rs_, cs_], NT)
                cols.append(_dot(wst[g], dmb[rs_, cs_], NN))
            rows.append(jnp.concatenate(cols, axis=1))
        dq = jnp.concatenate(rows, axis=0)
        dbg_ref[...] += _colsum(dq * nb)
        dbb_ref[...] += _colsum(dq)
        dh_ref[:, 1536:2048] = (_ln_bwd_rows(dq * bg_ref[...], nb, rb) * dgw).astype(BF16)

        @pl.when(i == nt - 1)
        def _():
            for g in range(B_GROUPS):
                dms_ref[g] = jnp.where(mask, dms_ref[g], 0.0)
                dsb_ref[g] = jnp.sum(sbacc[g], axis=1, keepdims=True)

    last_blk = t // MIX_HALO - 1
    vec = pl.BlockSpec((1, A_WIDTH), lambda i: (0, 0))
    mat = pl.BlockSpec((B_GROUPS, B_CHUNK, B_CHUNK), lambda i: (0, 0, 0))
    taps = pl.BlockSpec((A_KERNEL, A_WIDTH), lambda i: (0, 0))

    def halo(width, which):
        if which == "prev":
            return pl.BlockSpec((MIX_HALO, width), lambda i: (jnp.maximum(i * hb - 1, 0), 0))
        return pl.BlockSpec((MIX_HALO, width), lambda i: (jnp.minimum((i + 1) * hb, last_blk), 0))

    vshape = jax.ShapeDtypeStruct((1, A_WIDTH), F32)
    outs, moved = _call(
        body, name=name, grid=(nt,),
        in_specs=[pl.BlockSpec((tm, 2048), lambda i: (i, 0)), halo(2048, "prev"), halo(2048, "next"),
                  pl.BlockSpec((tm, D_MODEL), lambda i: (i, 0)), halo(D_MODEL, "next"),
                  taps, vec, vec, vec, vec, vec, mat, mat, pl.BlockSpec((B_CHUNK, B_GROUPS), lambda i: (0, 0))],
        out_specs=[pl.BlockSpec((tm, 2048), lambda i: (i, 0)), taps, vec, vec, vec, vec, vec, mat,
                   pl.BlockSpec((B_GROUPS, B_CHUNK, 1), lambda i: (0, 0, 0))],
        out_shape=[jax.ShapeDtypeStruct((t, 2048), BF16), jax.ShapeDtypeStruct((A_KERNEL, A_WIDTH), F32),
                   vshape, vshape, vshape, vshape, vshape,
                   jax.ShapeDtypeStruct((B_GROUPS, B_CHUNK, B_CHUNK), F32), jax.ShapeDtypeStruct((B_GROUPS, B_CHUNK, 1), F32)],
        scratch_shapes=[pltpu.VMEM((SUBLANES, tm + 2 * MIX_HALO, A_WIDTH), F32), pltpu.VMEM((SUBLANES, r, A_WIDTH), F32),
                        pltpu.VMEM((B_GROUPS, B_CHUNK, B_CHUNK), F32)],
        sem=("arbitrary",), args=(h, h, h, dcat, dcat, cw, cb, ag, ab, bg, bb, ms, mst, sbt), comm=comm)
    return outs if comm is None else (outs, moved)


Q_WIDTH = N_Q_HEADS * HEAD_DIM
KV_WIDTH = 2 * N_KV_HEADS * HEAD_DIM
PAIRS_PER_KV = N_Q_HEADS // N_KV_HEADS // 2
ATT_SCALE = 1.0 / math.sqrt(HEAD_DIM)


def _dup_heads(pair_cols, kv_head):
    lane = lax.broadcasted_iota(jnp.int32, pair_cols.shape, 1)
    rolled = pltpu.roll(pair_cols, HEAD_DIM, 1)
    first = lane < HEAD_DIM
    return jnp.where(first, pair_cols, rolled) if kv_head == 0 else jnp.where(first, rolled, pair_cols)


HEADS_PER_KV = N_Q_HEADS // N_KV_HEADS


def _stack_heads(ref, kh):
    lane = lax.broadcasted_iota(jnp.int32, (ATT_BLOCK, 128), 1)
    rows = []
    for pr in range(PAIRS_PER_KV):
        c0 = (kh * PAIRS_PER_KV + pr) * 128
        pair = ref[:, c0:c0 + 128]
        rows += [jnp.where(lane < HEAD_DIM, pair, jnp.zeros_like(pair)), jnp.where(lane < HEAD_DIM, jnp.zeros_like(pair), pair)]
    return jnp.concatenate(rows, axis=0)


def _unstack_heads(stacked, kh, write):
    lane = lax.broadcasted_iota(jnp.int32, (ATT_BLOCK, 128), 1)
    for pr in range(PAIRS_PER_KV):
        first = stacked[(2 * pr) * ATT_BLOCK:(2 * pr + 1) * ATT_BLOCK]
        second = stacked[(2 * pr + 1) * ATT_BLOCK:(2 * pr + 2) * ATT_BLOCK]
        write((kh * PAIRS_PER_KV + pr) * 128, jnp.where(lane < HEAD_DIM, first, second))


def _sink_row(sink_ref, kh):
    return jnp.concatenate([jnp.full((1, ATT_BLOCK), sink_ref[0, kh * HEADS_PER_KV + h], F32) for h in range(HEADS_PER_KV)], axis=1)


def _att_mask_t(n):
    sj = lax.broadcasted_iota(jnp.int32, (2 * ATT_BLOCK, HEADS_PER_KV * ATT_BLOCK), 0)
    qi = lax.broadcasted_iota(jnp.int32, (2 * ATT_BLOCK, HEADS_PER_KV * ATT_BLOCK), 1) & (ATT_BLOCK - 1)
    diff = qi + ATT_BLOCK - sj
    return (diff >= 0) & (diff < ATT_BLOCK) & ((n > 0) | (sj >= ATT_BLOCK))


def _att_probs_t(q_all, k2, mask_t, sink):
    st = _dot(k2, q_all, NT) * ATT_SCALE
    st = jnp.where(mask_t, st, -jnp.inf)
    m = jnp.maximum(jnp.max(st, axis=0, keepdims=True), sink)
    e = jnp.exp(st - m)
    es = jnp.exp(sink - m)
    inv = 1.0 / (jnp.sum(e, axis=0, keepdims=True) + es)
    return e * inv, es * inv


def _attn_fwd(qkv, sinks, name, comm=None):
    t = qkv.shape[0]
    nb = t // ATT_BLOCK
    kvb = Q_WIDTH // KV_WIDTH

    def body(sink_ref, q_ref, kv_ref, kvp_ref, o_ref):
        n = pl.program_id(0)
        mask_t = _att_mask_t(n)
        kv = jnp.concatenate([kvp_ref[...], kv_ref[...]], axis=0).astype(F32)

        def write(c0, pair):
            o_ref[:, c0:c0 + 128] = pair.astype(BF16)

        for kh in range(N_KV_HEADS):
            k2 = _dup_heads(kv[:, 0:128], kh).astype(BF16)
            v2 = _dup_heads(kv[:, 128:256], kh).astype(BF16)
            pt, _ = _att_probs_t(_stack_heads(q_ref, kh), k2, mask_t, _sink_row(sink_ref, kh))
            _unstack_heads(_dot(v2, pt, TN).T, kh, write)

    (out,), moved = _call(
        body, name=name, grid=(nb,),
        in_specs=[pl.BlockSpec(memory_space=pltpu.SMEM),
                  pl.BlockSpec((ATT_BLOCK, Q_WIDTH), lambda n: (n, 0)),
                  pl.BlockSpec((ATT_BLOCK, KV_WIDTH), lambda n: (n, kvb)),
                  pl.BlockSpec((ATT_BLOCK, KV_WIDTH), lambda n: (jnp.maximum(n - 1, 0), kvb))],
        out_specs=[pl.BlockSpec((ATT_BLOCK, Q_WIDTH), lambda n: (n, 0))],
        out_shape=[jax.ShapeDtypeStruct((t, Q_WIDTH), BF16)],
        sem=("parallel",), args=(sinks, qkv, qkv, qkv), comm=comm)
    return out if comm is None else (out, moved)


def _attn_bwd(qkv, d_o, sinks, name, comm=None):
    t = qkv.shape[0]
    nb = t // ATT_BLOCK
    kvb = Q_WIDTH // KV_WIDTH

    def body(sink_ref, q_ref, kv_ref, kvp_ref, do_ref, dq_ref, dkv_ref, dbq_ref, dbkv_ref, dsink_ref, carry):
        n = pl.program_id(0)

        @pl.when(n == 0)
        def _():
            for ref in (dbq_ref, dbkv_ref, dsink_ref, carry):
                ref[...] = jnp.zeros_like(ref)
            dkv_ref[...] = jnp.zeros_like(dkv_ref)

        @pl.when(n < nb)
        def _():
            mask_t = _att_mask_t(n)
            kv = jnp.concatenate([kvp_ref[...], kv_ref[...]], axis=0).astype(F32)
            lane2 = lax.broadcasted_iota(jnp.int32, (2 * ATT_BLOCK, 128), 1)
            sink_lane = lax.broadcasted_iota(jnp.int32, (1, 128), 1)
            dsink = jnp.zeros((1, 128), F32)
            dk_parts, dv_parts = [], []

            def write(c0, pair):
                dbq_ref[:, c0:c0 + 128] += _colsum(pair)
                dq_ref[:, c0:c0 + 128] = pair.astype(BF16)

            for kh in range(N_KV_HEADS):
                k2 = _dup_heads(kv[:, 0:128], kh).astype(BF16)
                v2 = _dup_heads(kv[:, 128:256], kh).astype(BF16)
                q_all = _stack_heads(q_ref, kh)
                do_all = _stack_heads(do_ref, kh)
                pt, ps = _att_probs_t(q_all, k2, mask_t, _sink_row(sink_ref, kh))
                dpt = _dot(v2, do_all, NT)
                delta = jnp.sum(pt * dpt, axis=0, keepdims=True)
                dst = pt * (dpt - delta) * ATT_SCALE
                psd = ps * delta
                for h in range(HEADS_PER_KV):
                    dsink = dsink + jnp.where(sink_lane == kh * HEADS_PER_KV + h,
                                              -jnp.sum(psd[:, h * ATT_BLOCK:(h + 1) * ATT_BLOCK]), 0.0)
                _unstack_heads(_dot(k2, dst, TN).T, kh, write)
                dk_acc = _dot(dst, q_all, NN)
                dv_acc = _dot(pt, do_all, NN)
                dk_parts.append(dk_acc + pltpu.roll(dk_acc, HEAD_DIM, 1))
                dv_parts.append(dv_acc + pltpu.roll(dv_acc, HEAD_DIM, 1))
            dk = jnp.where(lane2 < HEAD_DIM, dk_parts[0], dk_parts[1])
            dv = jnp.where(lane2 < HEAD_DIM, dv_parts[0], dv_parts[1])
            dkv_new = jnp.concatenate([dk, dv], axis=1)
            done = carry[...] + dkv_new[0:ATT_BLOCK]

            @pl.when(n > 0)
            def _():
                dkv_ref[...] = done.astype(BF16)
                dbkv_ref[...] += _colsum(done)

            carry[...] = dkv_new[ATT_BLOCK:]
            dsink_ref[...] += dsink

        @pl.when(n == nb)
        def _():
            dkv_ref[...] = carry[...].astype(BF16)
            dbkv_ref[...] += _colsum(carry[...])

    def clamp(n):
        return jnp.minimum(n, nb - 1)

    outs, moved = _call(
        body, name=name, grid=(nb + 1,),
        in_specs=[pl.BlockSpec(memory_space=pltpu.SMEM),
                  pl.BlockSpec((ATT_BLOCK, Q_WIDTH), lambda n: (clamp(n), 0)),
                  pl.BlockSpec((ATT_BLOCK, KV_WIDTH), lambda n: (clamp(n), kvb)),
                  pl.BlockSpec((ATT_BLOCK, KV_WIDTH), lambda n: (jnp.maximum(clamp(n) - 1, 0), kvb)),
                  pl.BlockSpec((ATT_BLOCK, Q_WIDTH), lambda n: (clamp(n), 0))],
        out_specs=[pl.BlockSpec((ATT_BLOCK, Q_WIDTH), lambda n: (clamp(n), 0)),
                   pl.BlockSpec((ATT_BLOCK, KV_WIDTH), lambda n: (jnp.maximum(n - 1, 0), 0)),
                   pl.BlockSpec((1, Q_WIDTH), lambda n: (0, 0)),
                   pl.BlockSpec((1, KV_WIDTH), lambda n: (0, 0)),
                   pl.BlockSpec((1, 128), lambda n: (0, 0))],
        out_shape=[jax.ShapeDtypeStruct((t, Q_WIDTH), BF16), jax.ShapeDtypeStruct((t, KV_WIDTH), BF16),
                   jax.ShapeDtypeStruct((1, Q_WIDTH), F32), jax.ShapeDtypeStruct((1, KV_WIDTH), F32),
                   jax.ShapeDtypeStruct((1, 128), F32)],
        scratch_shapes=[pltpu.VMEM((ATT_BLOCK, KV_WIDTH), F32)],
        sem=("arbitrary",), args=(sinks, qkv, qkv, qkv, d_o), comm=comm)
    return outs if comm is None else (outs, moved)


def _adamw_math(g, w, m, v):
    m = ADAM_B1 * m + (1.0 - ADAM_B1) * g
    v = ADAM_B2 * v + (1.0 - ADAM_B2) * (g * g)
    m_hat = m / (1.0 - ADAM_B1 ** ADAM_STEP)
    v_hat = v / (1.0 - ADAM_B2 ** ADAM_STEP)
    delta = -ADAM_LR * (m_hat / (jnp.sqrt(v_hat) + ADAM_EPS) + ADAM_WD * w)
    return delta, m, v


def _sum_partials(p_ref):
    g = p_ref[0].astype(F32)
    for s in range(1, N_DEV):
        g = g + p_ref[s].astype(F32)
    return g


def _adamw_big(parts, w, m, v, name, tr):
    r, c = w.shape
    parts = [p if isinstance(p, tuple) else (p, 0, p.shape[1]) for p in parts]
    tiles = [rows // tr for _, _, rows in parts]
    starts = [sum(tiles[:l]) for l in range(len(parts))]
    assert all(lo % tr == 0 and rows % tr == 0 for _, lo, rows in parts) and sum(tiles) * tr == r

    def body(*refs):
        p_refs, (w_ref, m_ref, v_ref, g_out, d_out, m_out, v_out) = refs[:len(parts)], refs[len(parts):]
        i = pl.program_id(0)
        for l, p_ref in enumerate(p_refs):
            @pl.when((i >= starts[l]) & (i < starts[l] + tiles[l]))
            def _():
                g = _sum_partials(p_ref)
                g_out[...] = g
                d_out[...], m_out[...], v_out[...] = _adamw_math(g, w_ref[...], m_ref[...], v_ref[...])

    def part_spec(l):
        return pl.BlockSpec((N_DEV, tr, c), lambda i: (0, jnp.clip(i - starts[l], 0, tiles[l] - 1) + parts[l][1] // tr, 0))

    tile = pl.BlockSpec((tr, c), lambda i: (i, 0))
    shape = jax.ShapeDtypeStruct((r, c), F32)
    return pl.pallas_call(
        body, name=name, grid=(r // tr,),
        in_specs=[part_spec(l) for l in range(len(parts))] + [tile, tile, tile],
        out_specs=[tile] * 4, out_shape=[shape] * 4,
        compiler_params=_params("parallel"),
    )(*[p[0] for p in parts], w, m, v)


def _adamw_small(parts, ws, ms, vs, name):
    n = len(ws)

    def body(*refs):
        ins, outs = refs[:4 * n], refs[4 * n:]
        for a in range(n):
            g = _sum_partials(ins[a])
            outs[4 * a][...] = g
            outs[4 * a + 1][...], outs[4 * a + 2][...], outs[4 * a + 3][...] = _adamw_math(
                g, ins[n + a][...], ins[2 * n + a][...], ins[3 * n + a][...])

    out_shape = []
    for w in ws:
        out_shape += [jax.ShapeDtypeStruct(w.shape, F32)] * 4
    return pl.pallas_call(body, name=name, out_shape=out_shape, compiler_params=_params())(*parts, *ws, *ms, *vs)


PACK_LANES = 128
PACK_ROWS = 8


def _pack(arrs):
    flat = jnp.concatenate([a.reshape(-1).astype(F32) for a in arrs])
    unit = PACK_LANES * PACK_ROWS
    total = -(-flat.shape[0] // unit) * unit
    return jnp.pad(flat, (0, total - flat.shape[0])).reshape(-1, PACK_LANES)


def _unpack(buf, shapes):
    flat = buf.reshape(N_DEV, -1)
    out, pos = [], 0
    for s in shapes:
        size = math.prod(s)
        out.append(flat[:, pos:pos + size].reshape((N_DEV,) + tuple(s)))
        pos += size
    return out


def _interleave(g):
    return jnp.transpose(g, (1, 0, 2)).reshape(g.shape[1], -1)


def _deinterleave(w):
    r = w.shape[0]
    return jnp.transpose(w.reshape(r, N_DEV, -1), (1, 0, 2))


def _ffn_backward(dz, x_in, z_in, g_in, h, u, w_up_t, cw, cb, w_down, tag, exchange=(), exchange_late=()):
    du = _matmul(dz, w_down, "nt", BF16, f"ffn{tag}_du", 1024, 1408, 1024)
    d_w_down = _matmul(u, dz, "tn", BF16, f"ffn{tag}_dwdown", 1408, 1024, 2048)
    (dhg, dhv, dcwg, dcwv, dcbg, dcbv), moved = _ffn_mid_bwd(
        h, du, cw, cb, f"ffn{tag}_mid_bwd", comm=_Comm(exchange=[d_w_down.reshape(N_DEV, -1, D_MODEL), *exchange]))
    d_w_up_t = _matmul_tn_pair(dhg, dhv, x_in, BF16, f"ffn{tag}_dwup", 1408, 1024, 1024,
                               comm=_Comm(exchange=exchange_late) if exchange_late else None)
    if exchange_late:
        d_w_up_t, late = d_w_up_t
        moved = moved + late
    dx = _matmul(dhv, w_up_t, "nn", F32, f"ffn{tag}_dx_value", 1024, 1024, D_FF, b_off=1)
    dz_in, dg_in, db_in = _matmul_ln_bwd(dhg, w_up_t, z_in, g_in, dz, f"ffn{tag}_dx_gate_ln_bwd", 512, res=dx)
    return (dz_in, dg_in, db_in, d_w_up_t.reshape(N_DEV, -1, D_MODEL),
            jnp.concatenate([dcwg, dcwv], axis=1), jnp.concatenate([dcbg, dcbv], axis=1), moved)


def kernel(x, ab_w_in, a_conv_w, a_conv_b, a_norm_g, a_norm_b, b_norm_g, b_norm_b, b_spatial_w, b_spatial_b, ab_w_out, c_w_qkv, c_b_qkv, c_sinks, c_w_o, ffn_w_up, ffn_conv_w, ffn_conv_b, ffn_w_down, ln_g, ln_b, loss_target, m_ab_w_in, m_a_conv_w, m_a_conv_b, m_a_norm_g, m_a_norm_b, m_b_norm_g, m_b_norm_b, m_b_spatial_w, m_b_spatial_b, m_ab_w_out, m_c_w_qkv, m_c_b_qkv, m_c_sinks, m_c_w_o, m_ffn_w_up, m_ffn_conv_w, m_ffn_conv_b, m_ffn_w_down, m_ln_g, m_ln_b, v_ab_w_in, v_a_conv_w, v_a_conv_b, v_a_norm_g, v_a_norm_b, v_b_norm_g, v_b_norm_b, v_b_spatial_w, v_b_spatial_b, v_ab_w_out, v_c_w_qkv, v_c_b_qkv, v_c_sinks, v_c_w_o, v_ffn_w_up, v_ffn_conv_w, v_ffn_conv_b, v_ffn_w_down, v_ln_g, v_ln_b):
    me = 4 * lax.axis_index("x") + 2 * lax.axis_index("y") + lax.axis_index("c")
    xt = x[0]
    t = xt.shape[0]

    small_shard_shapes = [a_conv_w.shape, c_b_qkv.shape, ffn_conv_w.shape, ln_g.shape, ln_b.shape]
    up_shard = [jnp.swapaxes(ffn_w_up[l], 0, 1).astype(BF16) for l in range(2)]
    qkv_shard = jnp.swapaxes(c_w_qkv[0], 0, 1).astype(BF16)
    down_shard = [ffn_w_down[l].astype(BF16) for l in range(2)]
    g_win, g_small = _comm_only(
        _Comm(gather=[ab_w_in[0].astype(BF16), _pack([a_conv_w, c_b_qkv, ffn_conv_w, ln_g, ln_b])]), "gather_first")
    w_in = _interleave(g_win)
    g_acw, g_bqkv, g_fcw, g_lng, g_lnb = _unpack(g_small, small_shard_shapes)
    acw = _interleave(g_acw[:, 0])
    bqkv = g_bqkv[:, 0].reshape(1, -1)
    fcw = [_interleave(g_fcw[:, l]) for l in range(2)]
    lng = jnp.transpose(g_lng, (1, 2, 0, 3)).reshape(2, 2, 1, D_MODEL)
    lnb = jnp.transpose(g_lnb, (1, 2, 0, 3)).reshape(2, 2, 1, D_MODEL)
    fcb = [ffn_conv_b[l:l + 1] for l in range(2)]
    ms = b_spatial_w[0]
    mst = jnp.swapaxes(ms, 1, 2)
    sbt = b_spatial_b[0].T

    h0, (g_wout,) = _matmul(xt, w_in, "nn", BF16, "mix_in", 1024, 1024, 1024, comm=_Comm(gather=[ab_w_out[0].astype(BF16)]))
    w_out = g_wout.reshape(D_MODEL, D_MODEL)
    cat, (g_wup0,) = _mixer_mid_fwd(h0, acw, a_conv_b, a_norm_g, a_norm_b, b_norm_g, b_norm_b, ms, sbt, "mix_mid_fwd",
                                    comm=_Comm(gather=[up_shard[0]]))
    w_up0 = g_wup0.reshape(2 * D_FF, D_MODEL)
    z1, x1 = _matmul_res_ln(cat, w_out, xt, lng[0, 0], lnb[0, 0], "mix_out_ln", 512)
    hf0, (g_wdown0, g_wqkv) = _matmul(x1, w_up0, "nt", BF16, "ffn0_up", 1024, 1408, 1024,
                                      comm=_Comm(gather=[down_shard[0], qkv_shard]))
    w_down0 = g_wdown0.reshape(D_FF, D_MODEL)
    w_qkv = g_wqkv.reshape(Q_WIDTH + KV_WIDTH, D_MODEL)
    u0, (g_wup1,) = _ffn_mid_fwd(hf0, fcw[0], fcb[0], "ffn0_mid_fwd", comm=_Comm(gather=[up_shard[1]]))
    w_up1 = g_wup1.reshape(2 * D_FF, D_MODEL)
    (z2, x2), (g_wo,) = _matmul_res_ln(u0, w_down0, z1, lng[0, 1], lnb[0, 1], "ffn0_down_ln", 512, prev=(lng[0, 0], lnb[0, 0]),
                                       comm=_Comm(gather=[c_w_o[0].astype(BF16)]))
    w_o = g_wo.reshape(D_MODEL, D_MODEL)
    qkv = _matmul(x2, w_qkv, "nt", BF16, "att_qkv", 1024, 1280, 1024, bias=bqkv)
    att, (g_wdown1,) = _attn_fwd(qkv, c_sinks, "att_fwd", comm=_Comm(gather=[down_shard[1]]))
    w_down1 = g_wdown1.reshape(D_FF, D_MODEL)
    z3, x3 = _matmul_res_ln(att, w_o, z2, lng[1, 0], lnb[1, 0], "att_out_ln", 512, prev=(lng[0, 1], lnb[0, 1]))
    hf1 = _matmul(x3, w_up1, "nt", BF16, "ffn1_up", 1024, 1408, 1024)
    u1 = _ffn_mid_fwd(hf1, fcw[1], fcb[1], "ffn1_mid_fwd")

    dz4, dg11, db11, loss_terms = _matmul_res_ln_loss(u1, w_down1, z3, lng[1, 1], lnb[1, 1], loss_target[0],
                                                      "ffn1_down_ln_loss", 512, prev=(lng[1, 0], lnb[1, 0]))
    dz3, dg10, db10, d_wup1, d_fcw1, d_fcb1, (p_wdown1,) = _ffn_backward(
        dz4, x3, z3, lng[1, 0], hf1, u1, w_up1, fcw[1], fcb[1], w_down1, 1)
    d_att = _matmul(dz3, w_o, "nt", BF16, "att_dout", 1024, 1024, 1024)
    d_wo = _matmul(att, dz3, "tn", BF16, "att_dwo", 1024, 1024, 512)
    half = d_wup1.shape[1] // 2
    (dq, dkv, dbq, dbkv, dsinks), (p_wup1a,) = _attn_bwd(qkv, d_att, c_sinks, "att_bwd",
                                                        comm=_Comm(exchange=[(d_wup1, 0, half)]))
    d_wqkv = jnp.concatenate([_matmul(dq, x2, "tn", BF16, "att_dwq", 1024, 1024, 1024),
                              _matmul(dkv, x2, "tn", BF16, "att_dwkv", KV_WIDTH, 1024, 1024)], axis=0)
    dx2 = _matmul(dkv, w_qkv, "nn", F32, "att_dx_kv", 1024, 1024, KV_WIDTH, b_off=Q_WIDTH // KV_WIDTH)
    dz2, dg01, db01 = _matmul_ln_bwd(dq, w_qkv, z2, lng[0, 1], dz3, "att_dx_q_ln_bwd", 512, res=dx2)
    dz1, dg00, db00, d_wup0, d_fcw0, d_fcb0, (p_wdown0, p_wup1b, p_wqkv, p_wo) = _ffn_backward(
        dz2, x1, z1, lng[0, 0], hf0, u0, w_up0, fcw[0], fcb[0], w_down0, 0, exchange=[(d_wup1, half, half)],
        exchange_late=[d_wqkv.reshape(N_DEV, -1, D_MODEL), d_wo.reshape(N_DEV, -1, D_MODEL)])
    dcat = _matmul(dz1, w_out, "nt", BF16, "mix_dcat", 1024, 1024, 1024)
    d_wout = _matmul(cat, dz1, "tn", BF16, "mix_dwout", 1024, 1024, 512)
    (dh0, d_acw, d_acb, d_ang, d_anb, d_bng, d_bnb, d_ms, d_sb), (p_wup0, p_wout) = _mixer_mid_bwd(
        h0, dcat, acw, a_conv_b, a_norm_g, a_norm_b, b_norm_g, b_norm_b, ms, mst, sbt, "mix_mid_bwd",
        comm=_Comm(exchange=[d_wup0, d_wout.reshape(N_DEV, -1, D_MODEL)]))
    d_bqkv = jnp.concatenate([dbq, dbkv], axis=1)
    d_lng = jnp.stack([jnp.stack([dg00, dg01]), jnp.stack([dg10, dg11])])
    d_lnb = jnp.stack([jnp.stack([db00, db01]), jnp.stack([db10, db11])])
    small_full = [d_acb, d_ang, d_anb, d_bng, d_bnb, d_ms, d_sb, dsinks[:, :N_Q_HEADS], jnp.concatenate([d_fcb0, d_fcb1], axis=0),
                  d_acw, d_bqkv, jnp.stack([d_fcw0, d_fcw1]), d_lng, d_lnb, loss_terms]
    d_win, (g_small_grads,) = _matmul(xt, dh0, "tn", BF16, "mix_dwin", 1024, 1024, 512, comm=_Comm(gather=[_pack(small_full)]))
    grad_x, (p_win,) = _matmul(dh0, w_in, "nt", F32, "mix_dx", 1024, 1024, 1024, res=dz1, res_scale=ALPHA,
                               comm=_Comm(exchange=[_deinterleave(d_win)]))


    big = {}
    for nm, p, w, m, v, tr, transposed in [
            ("ab_w_in", [p_win], ab_w_in, m_ab_w_in, v_ab_w_in, 256, False),
            ("ab_w_out", [p_wout], ab_w_out, m_ab_w_out, v_ab_w_out, 128, False),
            ("c_w_qkv", [p_wqkv], c_w_qkv, m_c_w_qkv, v_c_w_qkv, 160, True), ("c_w_o", [p_wo], c_w_o, m_c_w_o, v_c_w_o, 128, False),
            ("ffn_w_up", [p_wup0, (p_wup1a, 0, half), (p_wup1b, half, half)], ffn_w_up, m_ffn_w_up, v_ffn_w_up, 176, True),
            ("ffn_w_down", [p_wdown0, p_wdown1], ffn_w_down, m_ffn_w_down, v_ffn_w_down, 176, False)]:
        def two_d(a):
            a = jnp.swapaxes(a, 1, 2) if transposed else a
            return a.reshape(-1, a.shape[-1])

        def back(o):
            return jnp.swapaxes(o.reshape(w.shape[0], w.shape[2], w.shape[1]), 1, 2) if transposed else o.reshape(w.shape)

        outs = _adamw_big(p, two_d(w), two_d(m), two_d(v), "adamw_" + nm, tr)
        big[nm] = [back(o) for o in outs]

    *gs, loss_parts = _unpack(g_small_grads, [a.shape for a in small_full])
    loss = 0.5 / D_MODEL * jnp.sum(loss_parts)

    def my_shard(g, width):
        g = g.reshape(g.shape[:-1] + (N_DEV, width))
        return lax.dynamic_index_in_dim(g, me, axis=g.ndim - 2, keepdims=False)

    small_names = ["a_conv_b", "a_norm_g", "a_norm_b", "b_norm_g", "b_norm_b", "b_spatial_w", "b_spatial_b", "c_sinks", "ffn_conv_b",
                   "a_conv_w", "c_b_qkv", "ffn_conv_w", "ln_g", "ln_b"]
    small_w = [a_conv_b, a_norm_g, a_norm_b, b_norm_g, b_norm_b, b_spatial_w, b_spatial_b, c_sinks, ffn_conv_b,
               a_conv_w, c_b_qkv, ffn_conv_w, ln_g, ln_b]
    small_m = [m_a_conv_b, m_a_norm_g, m_a_norm_b, m_b_norm_g, m_b_norm_b, m_b_spatial_w, m_b_spatial_b, m_c_sinks, m_ffn_conv_b,
               m_a_conv_w, m_c_b_qkv, m_ffn_conv_w, m_ln_g, m_ln_b]
    small_v = [v_a_conv_b, v_a_norm_g, v_a_norm_b, v_b_norm_g, v_b_norm_b, v_b_spatial_w, v_b_spatial_b, v_c_sinks, v_ffn_conv_b,
               v_a_conv_w, v_c_b_qkv, v_ffn_conv_w, v_ln_g, v_ln_b]
    gs[9:] = [my_shard(g, w.shape[-1]) for g, w in zip(gs[9:], small_w[9:])]
    two_d = [(-1, w.shape[-1]) for w in small_w]
    outs = _adamw_small([g.reshape((N_DEV,) + w.reshape(s).shape) for g, w, s in zip(gs, small_w, two_d)],
                        [w.reshape(s) for w, s in zip(small_w, two_d)], [m.reshape(s) for m, s in zip(small_m, two_d)],
                        [v.reshape(s) for v, s in zip(small_v, two_d)], "adamw_small")
    small = {nm: [o.reshape(w.shape) for o in outs[4 * a:4 * a + 4]] for a, (nm, w) in enumerate(zip(small_names, small_w))}

    res = {**big, **small}
    order = ["ab_w_in", "a_conv_w", "a_conv_b", "a_norm_g", "a_norm_b", "b_norm_g", "b_norm_b", "b_spatial_w", "b_spatial_b", "ab_w_out",
             "c_w_qkv", "c_b_qkv", "c_sinks", "c_w_o", "ffn_w_up", "ffn_conv_w", "ffn_conv_b", "ffn_w_down", "ln_g", "ln_b"]
    return (loss, grad_x[None], *[res[nm][0] for nm in order], *[res[nm][1] for nm in order],
            *[res[nm][2] for nm in order], *[res[nm][3] for nm in order])
```

```python
import functools
import math

import jax
import jax.numpy as jnp
from jax import lax
from jax.experimental import pallas as pl
from jax.experimental.pallas import tpu as pltpu

F32 = jnp.float32
BF16 = jnp.bfloat16

N_DEV = 8
D_MODEL = 1024
A_WIDTH = 512
A_KERNEL = 31
B_GROUPS = 4
B_CHUNK = 128
HEAD_DIM = 64
N_Q_HEADS = 16
N_KV_HEADS = 2
ATT_BLOCK = 128
D_FF = 2816
FFN_KERNEL = 3
ALPHA = (2.0 * 2) ** 0.25
LN_EPS = 1e-5
GELU_K = math.sqrt(2.0 / math.pi)
GELU_C = 0.044715
ADAM_LR = 0.001
ADAM_B1 = 0.9
ADAM_B2 = 0.999
ADAM_EPS = 1e-08
ADAM_WD = 0.01
ADAM_STEP = 10
VMEM_LIMIT = 56 * 1024 * 1024
MESH_ID = pl.DeviceIdType.MESH


def _params(*sem):
    return pltpu.CompilerParams(dimension_semantics=sem, vmem_limit_bytes=VMEM_LIMIT)


def _gelu(x):
    t = jnp.tanh(GELU_K * x * (1.0 + GELU_C * x * x))
    return 0.5 * x * (1.0 + t)


def _gelu_and_grad(x):
    x2 = x * x
    t = jnp.tanh(GELU_K * x * (1.0 + GELU_C * x2))
    g = 0.5 * x * (1.0 + t)
    dg = 0.5 * (1.0 + t) + 0.5 * x * (1.0 - t * t) * (GELU_K * (1.0 + 3.0 * GELU_C * x2))
    return g, dg


def _sigmoid(x):
    return 1.0 / (1.0 + jnp.exp(-x))


def _ln_stats(z):
    mu = jnp.mean(z, axis=-1, keepdims=True)
    zc = z - mu
    var = jnp.mean(zc * zc, axis=-1, keepdims=True)
    r = lax.rsqrt(var + LN_EPS)
    return zc * r, r


def _ln_bwd_rows(dn, nh, r):
    return r * (dn - jnp.mean(dn, axis=-1, keepdims=True) - nh * jnp.mean(dn * nh, axis=-1, keepdims=True))


def _colsum(x):
    return jnp.sum(x, axis=0, keepdims=True)


def _dot(a, b, dims):
    return lax.dot_general(a.astype(BF16), b.astype(BF16), (dims, ((), ())), preferred_element_type=F32)


NN = ((1,), (0,))
NT = ((1,), (1,))
TN = ((0,), (0,))


ANY = pl.BlockSpec(memory_space=pl.ANY)
N_RELATIONS = N_DEV - 1


def _my_place():
    return lax.axis_index("x"), lax.axis_index("y"), lax.axis_index("c")


class _Comm:
    def __init__(self, gather=(), exchange=()):
        exchange = [e if isinstance(e, tuple) else (e, 0, e.shape[1]) for e in exchange]
        self.arrs = list(gather) + [e[0] for e in exchange]
        self.n_gather = len(gather)
        self.n = len(self.arrs)
        self.rows = [None] * self.n_gather + [pl.ds(lo, n) for _, lo, n in exchange]

    def out_shape(self):
        return [jax.ShapeDtypeStruct(((N_DEV,) + a.shape) if i < self.n_gather else a.shape, a.dtype)
                for i, a in enumerate(self.arrs)]

    def sems(self):
        return [pltpu.SemaphoreType.DMA((self.n, N_RELATIONS)), pltpu.SemaphoreType.DMA((self.n, N_RELATIONS)),
                pltpu.SemaphoreType.DMA((self.n,))]

    def _gather_copy(self, ins, outs, sems, a, k, place, to, from_input=False):
        px, py, pc = place
        block = outs[a].at[4 * px + 2 * py + pc]
        return pltpu.make_async_remote_copy(
            src_ref=ins[a] if from_input else block, dst_ref=block,
            send_sem=sems[0].at[a, k], recv_sem=sems[1].at[a, k], device_id=to, device_id_type=MESH_ID)

    def _exchange_copy(self, ins, outs, sems, a, k, landing=False):
        x, y, c = _my_place()
        me = 4 * x + 2 * y + c
        peer = (x ^ (k >> 2), y ^ ((k >> 1) & 1), c ^ (k & 1))
        return pltpu.make_async_remote_copy(
            src_ref=ins[a].at[me ^ k, self.rows[a]], dst_ref=outs[a].at[(me ^ k) if landing else me, self.rows[a]],
            send_sem=sems[0].at[a, k - 1], recv_sem=sems[1].at[a, k - 1], device_id=peer, device_id_type=MESH_ID)

    def _local_copy(self, ins, outs, sems, a):
        x, y, c = _my_place()
        me = 4 * x + 2 * y + c
        if a < self.n_gather:
            return pltpu.make_async_copy(ins[a], outs[a].at[me], sems[2].at[a])
        return pltpu.make_async_copy(ins[a].at[me, self.rows[a]], outs[a].at[me, self.rows[a]], sems[2].at[a])

    def _first_stage(self, ins, outs, sems, a):
        x, y, c = _my_place()
        me = (x, y, c)
        chips = [(1 - x, y), (x, 1 - y), (1 - x, 1 - y)]
        return ([self._gather_copy(ins, outs, sems, a, 0, me, (x, y, 1 - c), from_input=True)]
                + [self._gather_copy(ins, outs, sems, a, 1 + j, me, (*chip, c), from_input=True) for j, chip in enumerate(chips)])

    def start(self, ins, outs, sems):
        for a in range(self.n):
            self._local_copy(ins, outs, sems, a).start()
        for a in range(self.n_gather):
            for cp in self._first_stage(ins, outs, sems, a):
                cp.start()
        for k in range(1, N_DEV):
            for a in range(self.n_gather, self.n):
                self._exchange_copy(ins, outs, sems, a, k).start()

    def forward(self, ins, outs, sems):
        x, y, c = _my_place()
        me, sibling = (x, y, c), (x, y, 1 - c)
        for j, chip in enumerate([(1 - x, y), (x, 1 - y), (1 - x, 1 - y)]):
            for a in range(self.n_gather):
                self._gather_copy(ins, outs, sems, a, 1 + j, (*chip, c), me).wait_recv()
                self._gather_copy(ins, outs, sems, a, 4 + j, (*chip, c), sibling).start()

    def finish(self, ins, outs, sems):
        x, y, c = _my_place()
        me, sibling = (x, y, c), (x, y, 1 - c)
        chips = [(1 - x, y), (x, 1 - y), (1 - x, 1 - y)]
        passed = [self._gather_copy(ins, outs, sems, a, 4 + j, (*chip, c), sibling)
                  for j, chip in enumerate(chips) for a in range(self.n_gather)]
        for a in range(self.n_gather):
            self._gather_copy(ins, outs, sems, a, 0, sibling, me).wait_recv()
            for j, chip in enumerate(chips):
                self._gather_copy(ins, outs, sems, a, 4 + j, (*chip, 1 - c), me).wait_recv()
        for k in range(1, N_DEV):
            for a in range(self.n_gather, self.n):
                self._exchange_copy(ins, outs, sems, a, k, landing=True).wait_recv()
        for a in range(self.n_gather):
            for cp in self._first_stage(ins, outs, sems, a):
                cp.wait_send()
        for cp in passed:
            cp.wait_send()
        for k in range(1, N_DEV):
            for a in range(self.n_gather, self.n):
                self._exchange_copy(ins, outs, sems, a, k).wait_send()
        for a in range(self.n):
            self._local_copy(ins, outs, sems, a).wait()


def _comm_only(comm, name):
    def body(*refs):
        ins, outs, sems = refs[:comm.n], refs[comm.n:2 * comm.n], refs[2 * comm.n:]
        comm.start(ins, outs, sems)
        comm.forward(ins, outs, sems)
        comm.finish(ins, outs, sems)

    return pl.pallas_call(body, name=name, in_specs=[ANY] * comm.n, out_specs=[ANY] * comm.n,
                          out_shape=comm.out_shape(), scratch_shapes=comm.sems())(*comm.arrs)


def _call(body, *, name, grid, in_specs, out_specs, out_shape, args, sem, scratch_shapes=(), comm=None):
    in_specs, out_specs, out_shape, scratch_shapes = list(in_specs), list(out_specs), list(out_shape), list(scratch_shapes)
    if comm is None:
        outs = pl.pallas_call(body, name=name, grid=grid, in_specs=in_specs, out_specs=out_specs, out_shape=out_shape,
                              scratch_shapes=scratch_shapes, compiler_params=_params(*sem))(*args)
        return list(outs), []
    n_in, n_out, n_scr, nc = len(in_specs), len(out_specs), len(scratch_shapes), comm.n

    def wrapped(*refs):
        ins, refs = refs[:n_in], refs[n_in:]
        c_in, refs = refs[:nc], refs[nc:]
        outs, refs = refs[:n_out], refs[n_out:]
        c_out, refs = refs[:nc], refs[nc:]
        scr, sems = refs[:n_scr], refs[n_scr:]
        step = functools.reduce(lambda acc, ax: acc * grid[ax] + pl.program_id(ax), range(len(grid)), 0)
        steps = math.prod(grid)

        @pl.when(step == 0)
        def _():
            comm.start(c_in, c_out, sems)

        @pl.when(step == steps - 1)
        def _():
            comm.forward(c_in, c_out, sems)

        body(*ins, *outs, *scr)

        @pl.when(step == steps - 1)
        def _():
            comm.finish(c_in, c_out, sems)

    outs = pl.pallas_call(
        wrapped, name=name, grid=grid, in_specs=in_specs + [ANY] * nc, out_specs=out_specs + [ANY] * nc,
        out_shape=out_shape + comm.out_shape(), scratch_shapes=scratch_shapes + comm.sems(),
        compiler_params=_params(*(["arbitrary"] * len(grid))))(*args, *comm.arrs)
    return list(outs[:n_out]), list(outs[n_out:])


def _matmul(a, b, mode, out_dtype, name, tm, tn, tk, *, bias=None, res=None, res_scale=1.0, b_off=0, comm=None):
    tm = min(tm, a.shape[1] if mode == "tn" else a.shape[0])
    tk = min(tk, a.shape[0] if mode == "tn" else a.shape[1])
    if mode == "nn":
        (m, k), n = a.shape, b.shape[1]
        a_spec = pl.BlockSpec((tm, tk), lambda i, j, kk: (i, kk))
        b_spec = pl.BlockSpec((tk, tn), lambda i, j, kk: (kk + b_off, j))
        dims = NN
    elif mode == "nt":
        (m, k), n = a.shape, b.shape[0]
        a_spec = pl.BlockSpec((tm, tk), lambda i, j, kk: (i, kk))
        b_spec = pl.BlockSpec((tn, tk), lambda i, j, kk: (j, kk + b_off))
        dims = NT
    else:
        (k, m), n = a.shape, b.shape[1]
        a_spec = pl.BlockSpec((tk, tm), lambda i, j, kk: (kk, i))
        b_spec = pl.BlockSpec((tk, tn), lambda i, j, kk: (kk, j))
        dims = TN
    assert m % tm == 0 and n % tn == 0 and k % tk == 0, (name, m, n, k)
    nk = k // tk
    in_specs = [a_spec, b_spec]
    args = [a, b]
    if bias is not None:
        in_specs.append(pl.BlockSpec((1, tn), lambda i, j, kk: (0, j)))
        args.append(bias)
    if res is not None:
        in_specs.append(pl.BlockSpec((tm, tn), lambda i, j, kk: (i, j)))
        args.append(res)

    def finish(out, refs, o_ref):
        pos = 2
        if bias is not None:
            out = out + refs[pos][...]
            pos += 1
        if res is not None:
            out = out + res_scale * refs[pos][...].astype(F32)
        o_ref[...] = out.astype(out_dtype)

    def body_one_step(*refs):
        finish(_dot(refs[0][...], refs[1][...], dims), refs, refs[-1])

    def body(*refs):
        a_ref, b_ref = refs[0], refs[1]
        o_ref, acc = refs[-2], refs[-1]
        kk = pl.program_id(2)

        @pl.when(kk == 0)
        def _():
            acc[...] = jnp.zeros_like(acc)

        acc[...] += _dot(a_ref[...], b_ref[...], dims)

        @pl.when(kk == nk - 1)
        def _():
            finish(acc[...], refs, o_ref)

    (out,), moved = _call(
        body_one_step if nk == 1 else body, name=name, grid=(m // tm, n // tn, nk),
        in_specs=in_specs, out_specs=[pl.BlockSpec((tm, tn), lambda i, j, kk: (i, j))],
        out_shape=[jax.ShapeDtypeStruct((m, n), out_dtype)],
        scratch_shapes=[] if nk == 1 else [pltpu.VMEM((tm, tn), F32)],
        sem=("parallel", "parallel", "arbitrary"), args=args, comm=comm)
    return out if comm is None else (out, moved)


def _matmul_tn_pair(a0, a1, b, out_dtype, name, tm, tn, tk, comm=None):
    (k, m), n = a0.shape, b.shape[1]
    tk = min(tk, k)
    assert a1.shape == a0.shape and m % tm == 0 and n % tn == 0 and k % tk == 0, (name, m, n, k)
    mi, nk = m // tm, k // tk

    def body(a0_ref, a1_ref, b_ref, o_ref, acc):
        i, kk = pl.program_id(0), pl.program_id(2)

        @pl.when(kk == 0)
        def _():
            acc[...] = jnp.zeros_like(acc)

        @pl.when(i < mi)
        def _():
            acc[...] += _dot(a0_ref[...], b_ref[...], TN)

        @pl.when(i >= mi)
        def _():
            acc[...] += _dot(a1_ref[...], b_ref[...], TN)

        @pl.when(kk == nk - 1)
        def _():
            o_ref[...] = acc[...].astype(out_dtype)

    (out,), moved = _call(
        body, name=name, grid=(2 * mi, n // tn, nk),
        in_specs=[pl.BlockSpec((tk, tm), lambda i, j, kk: (jnp.where(i < mi, kk, nk - 1), jnp.minimum(i, mi - 1))),
                  pl.BlockSpec((tk, tm), lambda i, j, kk: (jnp.where(i >= mi, kk, 0), jnp.maximum(i - mi, 0))),
                  pl.BlockSpec((tk, tn), lambda i, j, kk: (kk, j))],
        out_specs=[pl.BlockSpec((tm, tn), lambda i, j, kk: (i, j))],
        out_shape=[jax.ShapeDtypeStruct((2 * m, n), out_dtype)],
        scratch_shapes=[pltpu.VMEM((tm, tn), F32)],
        sem=("parallel", "parallel", "arbitrary"), args=(a0, a1, b), comm=comm)
    return out if comm is None else (out, moved)


def _residual_input(x_ref, prev_refs):
    if not prev_refs:
        return x_ref[...]
    nh, _ = _ln_stats(x_ref[...])
    return nh * prev_refs[0][...] + prev_refs[1][...]


def _matmul_res_ln(a, b, x, g, beta, name, tm, prev=None, comm=None):
    t, k = a.shape
    d = b.shape[1]
    tm = min(tm, t)
    assert t % tm == 0
    n_prev = 0 if prev is None else 2

    def body(a_ref, b_ref, x_ref, g_ref, beta_ref, *rest):
        z_ref, xo_ref = rest[n_prev:]
        z = ALPHA * _residual_input(x_ref, rest[:n_prev]) + _dot(a_ref[...], b_ref[...], NN)
        nh, _ = _ln_stats(z)
        z_ref[...] = z
        xo_ref[...] = (nh * g_ref[...] + beta_ref[...]).astype(BF16)

    row = pl.BlockSpec((tm, d), lambda i: (i, 0))
    vec = pl.BlockSpec((1, d), lambda i: (0, 0))
    outs, moved = _call(
        body, name=name, grid=(t // tm,),
        in_specs=[pl.BlockSpec((tm, k), lambda i: (i, 0)), pl.BlockSpec((k, d), lambda i: (0, 0)), row, vec, vec] + [vec] * n_prev,
        out_specs=[row, row],
        out_shape=[jax.ShapeDtypeStruct((t, d), F32), jax.ShapeDtypeStruct((t, d), BF16)],
        sem=("parallel",), args=(a, b, x, g, beta, *(prev or ())), comm=comm)
    return outs if comm is None else (outs, moved)


def _matmul_ln_bwd(a, b, z, g, dres, name, tm, *, res=None, b_off=0):
    m, k = a.shape
    d = b.shape[1]
    tm = min(tm, m)
    assert m % tm == 0

    def body(*refs):
        a_ref, b_ref, z_ref, g_ref, dres_ref = refs[:5]
        dz_ref, dg_ref, db_ref = refs[-3:]

        @pl.when(pl.program_id(0) == 0)
        def _():
            dg_ref[...] = jnp.zeros_like(dg_ref)
            db_ref[...] = jnp.zeros_like(db_ref)

        dbr = _dot(a_ref[...], b_ref[...], NN)
        if res is not None:
            dbr = dbr + refs[5][...]
        nh, r = _ln_stats(z_ref[...])
        dy = ALPHA * dres_ref[...] + dbr
        dg_ref[...] += _colsum(dy * nh)
        db_ref[...] += _colsum(dy)
        dz_ref[...] = _ln_bwd_rows(dy * g_ref[...], nh, r)

    row = pl.BlockSpec((tm, d), lambda i: (i, 0))
    vec = pl.BlockSpec((1, d), lambda i: (0, 0))
    vshape = jax.ShapeDtypeStruct((1, d), F32)
    return pl.pallas_call(
        body, name=name, grid=(m // tm,),
        in_specs=[pl.BlockSpec((tm, k), lambda i: (i, 0)), pl.BlockSpec((k, d), lambda i: (b_off, 0)), row, vec, row]
        + ([row] if res is not None else []),
        out_specs=[row, vec, vec], out_shape=[jax.ShapeDtypeStruct((m, d), F32), vshape, vshape],
        compiler_params=_params("arbitrary"),
    )(a, b, z, g, dres, *([res] if res is not None else []))


def _matmul_res_ln_loss(a, b, x, g, beta, target, name, tm, prev):
    t, k = a.shape
    d = b.shape[1]
    tm = min(tm, t)

    def body(a_ref, b_ref, x_ref, g_ref, beta_ref, t_ref, gp_ref, bp_ref, dz_ref, dg_ref, db_ref, loss_ref):
        @pl.when(pl.program_id(0) == 0)
        def _():
            dg_ref[...] = jnp.zeros_like(dg_ref)
            db_ref[...] = jnp.zeros_like(db_ref)
            loss_ref[...] = jnp.zeros_like(loss_ref)

        nh, r = _ln_stats(ALPHA * _residual_input(x_ref, (gp_ref, bp_ref)) + _dot(a_ref[...], b_ref[...], NN))
        err = nh * g_ref[...] + beta_ref[...] - t_ref[...]
        loss_ref[...] += _colsum(err * err)
        dy = err * (1.0 / d)
        dg_ref[...] += _colsum(dy * nh)
        db_ref[...] += _colsum(dy)
        dz_ref[...] = _ln_bwd_rows(dy * g_ref[...], nh, r)

    row = pl.BlockSpec((tm, d), lambda i: (i, 0))
    vec = pl.BlockSpec((1, d), lambda i: (0, 0))
    vshape = jax.ShapeDtypeStruct((1, d), F32)
    return pl.pallas_call(
        body, name=name, grid=(t // tm,),
        in_specs=[pl.BlockSpec((tm, k), lambda i: (i, 0)), pl.BlockSpec((k, d), lambda i: (0, 0)), row, vec, vec, row, vec, vec],
        out_specs=[row, vec, vec, vec],
        out_shape=[jax.ShapeDtypeStruct((t, d), F32), vshape, vshape, vshape],
        compiler_params=_params("arbitrary"),
    )(a, b, x, g, beta, target, *prev)


FFN_HALO = 16
FFN_CHUNK = 256
LANES = 128
SUBLANES = 8


def _rows_up(e, start, rows):
    if start % SUBLANES == 0:
        return e[start:start + rows]
    return pltpu.roll(e, e.shape[0] - start, 0)[0:rows]


def _fold(x):
    return jnp.sum(x.reshape(x.shape[0] // SUBLANES, SUBLANES, x.shape[1]), axis=0)


def _ffn_mid_fwd(h, cw, cb, name, tm=1024, tc=1408, comm=None):
    t, f2 = h.shape
    tm = min(tm, t)
    f = f2 // 2
    nj, nt, hb = f // tc, t // tm, tm // FFN_HALO

    ch = min(FFN_CHUNK, tm)

    def body(hg, hgp, hv, hvp, cwg, cwv, cbg, cbv, u_ref):
        i = pl.program_id(1)
        o = FFN_HALO - FFN_KERNEL + 1
        for lg in range(tc // LANES):
            cols = slice(lg * LANES, (lg + 1) * LANES)
            wg, wv = [cwg[k:k + 1, cols] for k in range(FFN_KERNEL)], [cwv[k:k + 1, cols] for k in range(FFN_KERNEL)]
            bg, bv = cbg[:, cols], cbv[:, cols]

            def emit(base, eg, ev):
                cg = wg[0] * _rows_up(eg, o, ch) + wg[1] * _rows_up(eg, o + 1, ch) + wg[2] * _rows_up(eg, o + 2, ch) + bg
                cv = wv[0] * _rows_up(ev, o, ch) + wv[1] * _rows_up(ev, o + 1, ch) + wv[2] * _rows_up(ev, o + 2, ch) + bv
                u_ref[pl.ds(base, ch), cols] = (_gelu(cg) * cv).astype(BF16)

            def first(main, prev):
                return jnp.concatenate([jnp.where(i > 0, prev[:, cols].astype(F32), 0.0), main[0:ch, cols].astype(F32)], axis=0)

            def inner(c, carry):
                base = pl.multiple_of(c * ch, ch)
                emit(base, hg[pl.ds(base - FFN_HALO, ch + FFN_HALO), cols].astype(F32),
                     hv[pl.ds(base - FFN_HALO, ch + FFN_HALO), cols].astype(F32))
                return carry

            emit(0, first(hg, hgp), first(hv, hvp))
            if tm > ch:
                lax.fori_loop(1, tm // ch, inner, 0)

    def main_spec(off):
        return pl.BlockSpec((tm, tc), lambda j, i: (i, j + off))

    def prev_spec(off):
        return pl.BlockSpec((FFN_HALO, tc), lambda j, i: (jnp.maximum(i * hb - 1, 0), j + off))

    def par_spec(rows, off):
        return pl.BlockSpec((rows, tc), lambda j, i: (0, j + off))

    (u,), moved = _call(
        body, name=name, grid=(nj, nt),
        in_specs=[main_spec(0), prev_spec(0), main_spec(nj), prev_spec(nj),
                  par_spec(FFN_KERNEL, 0), par_spec(FFN_KERNEL, nj), par_spec(1, 0), par_spec(1, nj)],
        out_specs=[pl.BlockSpec((tm, tc), lambda j, i: (i, j))],
        out_shape=[jax.ShapeDtypeStruct((t, f), BF16)],
        sem=("parallel", "arbitrary"), args=(h, h, h, h, cw, cw, cb, cb), comm=comm)
    return u if comm is None else (u, moved)


def _ffn_mid_bwd(h, du, cw, cb, name, tm=1024, tc=1408, comm=None):
    t, f2 = h.shape
    tm = min(tm, t)
    f = f2 // 2
    nj, nt, hb = f // tc, t // tm, tm // FFN_HALO

    ch = min(FFN_CHUNK, tm)
    ahead = ch + SUBLANES

    n_ch = tm // ch

    def body(hg, hgp, hgn, hv, hvp, hvn, du_ref, dun_ref, cwg, cwv, cbg, cbv,
             dhg_ref, dhv_ref, dcwg_ref, dcwv_ref, dcbg_ref, dcbv_ref):
        i = pl.program_id(1)

        @pl.when(i == 0)
        def _():
            for ref in (dcwg_ref, dcwv_ref, dcbg_ref, dcbv_ref):
                ref[...] = jnp.zeros_like(ref)

        o = FFN_HALO - FFN_KERNEL + 1
        for lg in range(tc // LANES):
            cols = slice(lg * LANES, (lg + 1) * LANES)
            wg, wv = [cwg[k:k + 1, cols] for k in range(FFN_KERNEL)], [cwv[k:k + 1, cols] for k in range(FFN_KERNEL)]
            bg, bv = cbg[:, cols], cbv[:, cols]

            def emit(base, eg, ev, du_e, acc):
                hgs = [_rows_up(eg, o + k, ahead) for k in range(FFN_KERNEL)]
                hvs = [_rows_up(ev, o + k, ahead) for k in range(FFN_KERNEL)]
                cg = wg[0] * hgs[0] + wg[1] * hgs[1] + wg[2] * hgs[2] + bg
                cv = wv[0] * hvs[0] + wv[1] * hvs[1] + wv[2] * hvs[2] + bv
                du_a = du_e[0:ahead]
                gl, dgl = _gelu_and_grad(cg)

                def back(d, hs, w, dh_ref):
                    own = d[0:ch]
                    dh = w[2] * own + w[1] * _rows_up(d, 1, ch) + w[0] * _rows_up(d, 2, ch)
                    dh_ref[pl.ds(base, ch), cols] = dh.astype(BF16)
                    return [_fold(own)] + [_fold(own * hs[k][0:ch]) for k in range(FFN_KERNEL)]

                sums = back(du_a * cv * dgl, hgs, wg, dhg_ref) + back(du_a * gl, hvs, wv, dhv_ref)
                return tuple(a + s_ for a, s_ in zip(acc, sums))

            def edge(c, acc):
                def rows(main, before, after, lo, hi):
                    parts = [] if lo >= 0 else [jnp.where(i > 0, before[:, cols].astype(F32), 0.0)]
                    parts.append(main[max(lo, 0):min(hi, tm), cols].astype(F32))
                    if hi > tm:
                        parts.append(after)
                    return jnp.concatenate(parts, axis=0) if len(parts) > 1 else parts[0]

                lo, hi = c * ch - FFN_HALO, (c + 1) * ch + FFN_HALO
                du_next = jnp.where(i < nt - 1, dun_ref[:, cols].astype(F32), 0.0)
                return emit(c * ch, rows(hg, hgp, hgn[:, cols].astype(F32), lo, hi), rows(hv, hvp, hvn[:, cols].astype(F32), lo, hi),
                            rows(du_ref, None, du_next, c * ch, hi), acc)

            def inner(c, acc):
                base = pl.multiple_of(c * ch, ch)
                return emit(base, hg[pl.ds(base - FFN_HALO, ch + 2 * FFN_HALO), cols].astype(F32),
                            hv[pl.ds(base - FFN_HALO, ch + 2 * FFN_HALO), cols].astype(F32),
                            du_ref[pl.ds(base, ch + FFN_HALO), cols].astype(F32), acc)

            zero = jnp.zeros((SUBLANES, LANES), F32)
            acc = edge(0, (zero,) * (2 * (1 + FFN_KERNEL)))
            if n_ch > 2:
                acc = lax.fori_loop(1, n_ch - 1, inner, acc)
            if n_ch > 1:
                acc = edge(n_ch - 1, acc)
            dcbg_ref[:, cols] += _colsum(acc[0])
            dcbv_ref[:, cols] += _colsum(acc[1 + FFN_KERNEL])
            for k in range(FFN_KERNEL):
                dcwg_ref[k:k + 1, cols] += _colsum(acc[1 + k])
                dcwv_ref[k:k + 1, cols] += _colsum(acc[2 + FFN_KERNEL + k])

    last_blk = t // FFN_HALO - 1

    def main_spec(off):
        return pl.BlockSpec((tm, tc), lambda j, i: (i, j + off))

    def prev_spec(off):
        return pl.BlockSpec((FFN_HALO, tc), lambda j, i: (jnp.maximum(i * hb - 1, 0), j + off))

    def next_spec(off):
        return pl.BlockSpec((FFN_HALO, tc), lambda j, i: (jnp.minimum((i + 1) * hb, last_blk), j + off))

    def par_spec(rows, off):
        return pl.BlockSpec((rows, tc), lambda j, i: (0, j + off))

    out_tile = pl.BlockSpec((tm, tc), lambda j, i: (i, j))
    outs, moved = _call(
        body, name=name, grid=(nj, nt),
        in_specs=[main_spec(0), prev_spec(0), next_spec(0), main_spec(nj), prev_spec(nj), next_spec(nj),
                  main_spec(0), next_spec(0),
                  par_spec(FFN_KERNEL, 0), par_spec(FFN_KERNEL, nj), par_spec(1, 0), par_spec(1, nj)],
        out_specs=[out_tile, out_tile, par_spec(FFN_KERNEL, 0), par_spec(FFN_KERNEL, 0), par_spec(1, 0), par_spec(1, 0)],
        out_shape=[jax.ShapeDtypeStruct((t, f), BF16), jax.ShapeDtypeStruct((t, f), BF16),
                   jax.ShapeDtypeStruct((FFN_KERNEL, f), F32), jax.ShapeDtypeStruct((FFN_KERNEL, f), F32),
                   jax.ShapeDtypeStruct((1, f), F32), jax.ShapeDtypeStruct((1, f), F32)],
        sem=("parallel", "arbitrary"), args=(h, h, h, h, h, h, du, du, cw, cw, cb, cb), comm=comm)
    return outs if comm is None else (outs, moved)


MIX_HALO = 32


def _glu(hh):
    return hh[:, 0:A_WIDTH] * _sigmoid(hh[:, A_WIDTH:2 * A_WIDTH])


def _fill_row_shifts(s):
    rows = s.shape[1] - SUBLANES
    for j in range(1, SUBLANES):
        s[j, 0:rows, :] = s[0, pl.ds(j, rows), :]


def _rows_from(s, start, rows):
    j = start % SUBLANES
    return s[j, start - j:start - j + rows, :]


def _tril_mask():
    return lax.broadcasted_iota(jnp.int32, (B_CHUNK, B_CHUNK), 0) >= lax.broadcasted_iota(jnp.int32, (B_CHUNK, B_CHUNK), 1)


def _spatial_mix(q, ms_ref, sbt_ref, tm):
    mask = _tril_mask()
    ws = [jnp.where(mask, ms_ref[g], 0.0).astype(BF16) for g in range(B_GROUPS)]
    qb = q.astype(BF16)
    rows = []
    for c in range(tm // B_CHUNK):
        cols = [_dot(ws[g], qb[c * B_CHUNK:(c + 1) * B_CHUNK, g * 128:(g + 1) * 128], NN) + sbt_ref[:, g:g + 1]
                for g in range(B_GROUPS)]
        rows.append(jnp.concatenate(cols, axis=1))
    return jnp.concatenate(rows, axis=0)


def _mixer_mid_fwd(h, cw, cb, ag, ab, bg, bb, ms, sbt, name, tm=256, comm=None):
    t = h.shape[0]
    nt, hb = t // tm, tm // MIX_HALO
    o = MIX_HALO - A_KERNEL + 1

    def body(h_ref, hp_ref, cw_ref, cb_ref, ag_ref, ab_ref, bg_ref, bb_ref, ms_ref, sbt_ref, cat_ref, sp):
        i = pl.program_id(0)
        sp[0, 0:MIX_HALO, :] = jnp.where(i > 0, _glu(hp_ref[:, 0:2 * A_WIDTH].astype(F32)), 0.0)
        sp[0, MIX_HALO:, :] = _glu(h_ref[:, 0:2 * A_WIDTH].astype(F32))
        _fill_row_shifts(sp)
        y = jnp.zeros((tm, A_WIDTH), F32) + cb_ref[...]
        for k in range(A_KERNEL):
            y = y + cw_ref[k:k + 1, :] * _rows_from(sp, o + k, tm)
        nh, _ = _ln_stats(y)
        ln = nh * ag_ref[...] + ab_ref[...]
        cat_ref[:, 0:A_WIDTH] = (ln * _sigmoid(ln)).astype(BF16)
        u = _gelu(h_ref[:, 1024:1536].astype(F32))
        nb, _ = _ln_stats(_gelu(h_ref[:, 1536:2048].astype(F32)))
        mixed = _spatial_mix(nb * bg_ref[...] + bb_ref[...], ms_ref, sbt_ref, tm)
        cat_ref[:, A_WIDTH:] = (u * mixed).astype(BF16)

    vec = pl.BlockSpec((1, A_WIDTH), lambda i: (0, 0))
    (cat,), moved = _call(
        body, name=name, grid=(nt,),
        in_specs=[pl.BlockSpec((tm, 2048), lambda i: (i, 0)),
                  pl.BlockSpec((MIX_HALO, 2048), lambda i: (jnp.maximum(i * hb - 1, 0), 0)),
                  pl.BlockSpec((A_KERNEL, A_WIDTH), lambda i: (0, 0)), vec, vec, vec, vec, vec,
                  pl.BlockSpec((B_GROUPS, B_CHUNK, B_CHUNK), lambda i: (0, 0, 0)),
                  pl.BlockSpec((B_CHUNK, B_GROUPS), lambda i: (0, 0))],
        out_specs=[pl.BlockSpec((tm, D_MODEL), lambda i: (i, 0))],
        out_shape=[jax.ShapeDtypeStruct((t, D_MODEL), BF16)],
        scratch_shapes=[pltpu.VMEM((SUBLANES, tm + MIX_HALO, A_WIDTH), F32)],
        sem=("parallel",), args=(h, h, cw, cb, ag, ab, bg, bb, ms, sbt), comm=comm)
    return cat if comm is None else (cat, moved)


def _mixer_mid_bwd(h, dcat, cw, cb, ag, ab, bg, bb, ms, mst, sbt, name, tm=256, comm=None):
    t = h.shape[0]
    nt, hb = t // tm, tm // MIX_HALO
    o = MIX_HALO - A_KERNEL + 1
    r = tm + MIX_HALO
    nchunk = tm // B_CHUNK

    def body(h_ref, hp_ref, hn_ref, dc_ref, dcn_ref, cw_ref, cb_ref, ag_ref, ab_ref, bg_ref, bb_ref, ms_ref, mst_ref, sbt_ref,
             dh_ref, dcw_ref, dcb_ref, dag_ref, dab_ref, dbg_ref, dbb_ref, dms_ref, dsb_ref, sp, sdy, sbacc):
        i = pl.program_id(0)

        @pl.when(i == 0)
        def _():
            for ref in (dcw_ref, dcb_ref, dag_ref, dab_ref, dbg_ref, dbb_ref, dms_ref, dsb_ref, sbacc):
                ref[...] = jnp.zeros_like(ref)

        sp[0, 0:MIX_HALO, :] = jnp.where(i > 0, _glu(hp_ref[:, 0:2 * A_WIDTH].astype(F32)), 0.0)
        sp[0, MIX_HALO:MIX_HALO + tm, :] = _glu(h_ref[:, 0:2 * A_WIDTH].astype(F32))
        sp[0, MIX_HALO + tm:, :] = _glu(hn_ref[:, 0:2 * A_WIDTH].astype(F32))
        _fill_row_shifts(sp)
        y = jnp.zeros((r, A_WIDTH), F32) + cb_ref[...]
        for k in range(A_KERNEL):
            y = y + cw_ref[k:k + 1, :] * _rows_from(sp, o + k, r)
        nh, rs = _ln_stats(y)
        ln = nh * ag_ref[...] + ab_ref[...]
        sg = _sigmoid(ln)
        dao = jnp.concatenate([dc_ref[:, 0:A_WIDTH].astype(F32),
                               jnp.where(i < nt - 1, dcn_ref[:, 0:A_WIDTH].astype(F32), 0.0)], axis=0)
        dln = dao * (sg * (1.0 + ln * (1.0 - sg)))
        dag_ref[...] += _colsum(dln[0:tm] * nh[0:tm])
        dab_ref[...] += _colsum(dln[0:tm])
        sdy[0] = _ln_bwd_rows(dln * ag_ref[...], nh, rs)
        _fill_row_shifts(sdy)
        dy_own = sdy[0, 0:tm, :]
        dcb_ref[...] += _colsum(dy_own)
        dp = jnp.zeros((tm, A_WIDTH), F32)
        for k in range(A_KERNEL):
            dcw_ref[k:k + 1, :] += _colsum(dy_own * _rows_from(sp, o + k, tm))
            dp = dp + cw_ref[k:k + 1, :] * _rows_from(sdy, A_KERNEL - 1 - k, tm)
        av = h_ref[:, 0:A_WIDTH].astype(F32)
        s = _sigmoid(h_ref[:, A_WIDTH:2 * A_WIDTH].astype(F32))
        dh_ref[:, 0:A_WIDTH] = (dp * s).astype(BF16)
        dh_ref[:, A_WIDTH:2 * A_WIDTH] = (dp * av * s * (1.0 - s)).astype(BF16)

        u, dgu = _gelu_and_grad(h_ref[:, 1024:1536].astype(F32))
        w, dgw = _gelu_and_grad(h_ref[:, 1536:2048].astype(F32))
        nb, rb = _ln_stats(w)
        q = nb * bg_ref[...] + bb_ref[...]
        mixed = _spatial_mix(q, ms_ref, sbt_ref, tm)
        dbo = dc_ref[:, A_WIDTH:].astype(F32)
        dh_ref[:, 1024:1536] = (dbo * mixed * dgu).astype(BF16)
        dmx = dbo * u
        mask = _tril_mask()
        wst = [jnp.where(mask.T, mst_ref[g], 0.0).astype(BF16) for g in range(B_GROUPS)]
        qb = q.astype(BF16)
        dmb = dmx.astype(BF16)
        rows = []
        for c in range(nchunk):
            cols = []
            for g in range(B_GROUPS):
                rs_, cs_ = slice(c * B_CHUNK, (c + 1) * B_CHUNK), slice(g * 128, (g + 1) * 128)
                sbacc[g] += dmx[rs_, cs_]
                dms_ref[g] += _dot(dmb[rs_, cs_], qb[rs_, cs_], NT)
                cols.append(_dot(wst[g], dmb[rs_, cs_], NN))
            rows.append(jnp.concatenate(cols, axis=1))
        dq = jnp.concatenate(rows, axis=0)
        dbg_ref[...] += _colsum(dq * nb)
        dbb_ref[...] += _colsum(dq)
        dh_ref[:, 1536:2048] = (_ln_bwd_rows(dq * bg_ref[...], nb, rb) * dgw).astype(BF16)

        @pl.when(i == nt - 1)
        def _():
            for g in range(B_GROUPS):
                dms_ref[g] = jnp.where(mask, dms_ref[g], 0.0)
                dsb_ref[g] = jnp.sum(sbacc[g], axis=1, keepdims=True)

    last_blk = t // MIX_HALO - 1
    vec = pl.BlockSpec((1, A_WIDTH), lambda i: (0, 0))
    mat = pl.BlockSpec((B_GROUPS, B_CHUNK, B_CHUNK), lambda i: (0, 0, 0))
    taps = pl.BlockSpec((A_KERNEL, A_WIDTH), lambda i: (0, 0))

    def halo(width, which):
        if which == "prev":
            return pl.BlockSpec((MIX_HALO, width), lambda i: (jnp.maximum(i * hb - 1, 0), 0))
        return pl.BlockSpec((MIX_HALO, width), lambda i: (jnp.minimum((i + 1) * hb, last_blk), 0))

    vshape = jax.ShapeDtypeStruct((1, A_WIDTH), F32)
    outs, moved = _call(
        body, name=name, grid=(nt,),
        in_specs=[pl.BlockSpec((tm, 2048), lambda i: (i, 0)), halo(2048, "prev"), halo(2048, "next"),
                  pl.BlockSpec((tm, D_MODEL), lambda i: (i, 0)), halo(D_MODEL, "next"),
                  taps, vec, vec, vec, vec, vec, mat, mat, pl.BlockSpec((B_CHUNK, B_GROUPS), lambda i: (0, 0))],
        out_specs=[pl.BlockSpec((tm, 2048), lambda i: (i, 0)), taps, vec, vec, vec, vec, vec, mat,
                   pl.BlockSpec((B_GROUPS, B_CHUNK, 1), lambda i: (0, 0, 0))],
        out_shape=[jax.ShapeDtypeStruct((t, 2048), BF16), jax.ShapeDtypeStruct((A_KERNEL, A_WIDTH), F32),
                   vshape, vshape, vshape, vshape, vshape,
                   jax.ShapeDtypeStruct((B_GROUPS, B_CHUNK, B_CHUNK), F32), jax.ShapeDtypeStruct((B_GROUPS, B_CHUNK, 1), F32)],
        scratch_shapes=[pltpu.VMEM((SUBLANES, tm + 2 * MIX_HALO, A_WIDTH), F32), pltpu.VMEM((SUBLANES, r, A_WIDTH), F32),
                        pltpu.VMEM((B_GROUPS, B_CHUNK, B_CHUNK), F32)],
        sem=("arbitrary",), args=(h, h, h, dcat, dcat, cw, cb, ag, ab, bg, bb, ms, mst, sbt), comm=comm)
    return outs if comm is None else (outs, moved)


Q_WIDTH = N_Q_HEADS * HEAD_DIM
KV_WIDTH = 2 * N_KV_HEADS * HEAD_DIM
PAIRS_PER_KV = N_Q_HEADS // N_KV_HEADS // 2
ATT_SCALE = 1.0 / math.sqrt(HEAD_DIM)


def _dup_heads(pair_cols, kv_head):
    lane = lax.broadcasted_iota(jnp.int32, pair_cols.shape, 1)
    rolled = pltpu.roll(pair_cols, HEAD_DIM, 1)
    first = lane < HEAD_DIM
    return jnp.where(first, pair_cols, rolled) if kv_head == 0 else jnp.where(first, rolled, pair_cols)


HEADS_PER_KV = N_Q_HEADS // N_KV_HEADS


def _stack_heads(ref, kh):
    lane = lax.broadcasted_iota(jnp.int32, (ATT_BLOCK, 128), 1)
    rows = []
    for pr in range(PAIRS_PER_KV):
        c0 = (kh * PAIRS_PER_KV + pr) * 128
        pair = ref[:, c0:c0 + 128]
        rows += [jnp.where(lane < HEAD_DIM, pair, jnp.zeros_like(pair)), jnp.where(lane < HEAD_DIM, jnp.zeros_like(pair), pair)]
    return jnp.concatenate(rows, axis=0)


def _unstack_heads(stacked, kh, write):
    lane = lax.broadcasted_iota(jnp.int32, (ATT_BLOCK, 128), 1)
    for pr in range(PAIRS_PER_KV):
        first = stacked[(2 * pr) * ATT_BLOCK:(2 * pr + 1) * ATT_BLOCK]
        second = stacked[(2 * pr + 1) * ATT_BLOCK:(2 * pr + 2) * ATT_BLOCK]
        write((kh * PAIRS_PER_KV + pr) * 128, jnp.where(lane < HEAD_DIM, first, second))


def _sink_row(sink_ref, kh):
    return jnp.concatenate([jnp.full((1, ATT_BLOCK), sink_ref[0, kh * HEADS_PER_KV + h], F32) for h in range(HEADS_PER_KV)], axis=1)


def _att_mask_t(n):
    sj = lax.broadcasted_iota(jnp.int32, (2 * ATT_BLOCK, HEADS_PER_KV * ATT_BLOCK), 0)
    qi = lax.broadcasted_iota(jnp.int32, (2 * ATT_BLOCK, HEADS_PER_KV * ATT_BLOCK), 1) & (ATT_BLOCK - 1)
    diff = qi + ATT_BLOCK - sj
    return (diff >= 0) & (diff < ATT_BLOCK) & ((n > 0) | (sj >= ATT_BLOCK))


def _att_probs_t(q_all, k2, mask_t, sink):
    st = _dot(k2, q_all, NT) * ATT_SCALE
    st = jnp.where(mask_t, st, -jnp.inf)
    m = jnp.maximum(jnp.max(st, axis=0, keepdims=True), sink)
    e = jnp.exp(st - m)
    es = jnp.exp(sink - m)
    inv = 1.0 / (jnp.sum(e, axis=0, keepdims=True) + es)
    return e * inv, es * inv


def _attn_fwd(qkv, sinks, name, comm=None):
    t = qkv.shape[0]
    nb = t // ATT_BLOCK
    kvb = Q_WIDTH // KV_WIDTH

    def body(sink_ref, q_ref, kv_ref, kvp_ref, o_ref):
        n = pl.program_id(0)
        mask_t = _att_mask_t(n)
        kv = jnp.concatenate([kvp_ref[...], kv_ref[...]], axis=0).astype(F32)

        def write(c0, pair):
            o_ref[:, c0:c0 + 128] = pair.astype(BF16)

        for kh in range(N_KV_HEADS):
            k2 = _dup_heads(kv[:, 0:128], kh).astype(BF16)
            v2 = _dup_heads(kv[:, 128:256], kh).astype(BF16)
            pt, _ = _att_probs_t(_stack_heads(q_ref, kh), k2, mask_t, _sink_row(sink_ref, kh))
            _unstack_heads(_dot(v2, pt, TN).T, kh, write)

    (out,), moved = _call(
        body, name=name, grid=(nb,),
        in_specs=[pl.BlockSpec(memory_space=pltpu.SMEM),
                  pl.BlockSpec((ATT_BLOCK, Q_WIDTH), lambda n: (n, 0)),
                  pl.BlockSpec((ATT_BLOCK, KV_WIDTH), lambda n: (n, kvb)),
                  pl.BlockSpec((ATT_BLOCK, KV_WIDTH), lambda n: (jnp.maximum(n - 1, 0), kvb))],
        out_specs=[pl.BlockSpec((ATT_BLOCK, Q_WIDTH), lambda n: (n, 0))],
        out_shape=[jax.ShapeDtypeStruct((t, Q_WIDTH), BF16)],
        sem=("parallel",), args=(sinks, qkv, qkv, qkv), comm=comm)
    return out if comm is None else (out, moved)


def _attn_bwd(qkv, d_o, sinks, name, comm=None):
    t = qkv.shape[0]
    nb = t // ATT_BLOCK
    kvb = Q_WIDTH // KV_WIDTH

    def body(sink_ref, q_ref, kv_ref, kvp_ref, do_ref, dq_ref, dkv_ref, dbq_ref, dbkv_ref, dsink_ref, carry):
        n = pl.program_id(0)

        @pl.when(n == 0)
        def _():
            for ref in (dbq_ref, dbkv_ref, dsink_ref, carry):
                ref[...] = jnp.zeros_like(ref)
            dkv_ref[...] = jnp.zeros_like(dkv_ref)

        @pl.when(n < nb)
        def _():
            mask_t = _att_mask_t(n)
            kv = jnp.concatenate([kvp_ref[...], kv_ref[...]], axis=0).astype(F32)
            lane2 = lax.broadcasted_iota(jnp.int32, (2 * ATT_BLOCK, 128), 1)
            sink_lane = lax.broadcasted_iota(jnp.int32, (1, 128), 1)
            dsink = jnp.zeros((1, 128), F32)
            dk_parts, dv_parts = [], []

            def write(c0, pair):
                dbq_ref[:, c0:c0 + 128] += _colsum(pair)
                dq_ref[:, c0:c0 + 128] = pair.astype(BF16)

            for kh in range(N_KV_HEADS):
                k2 = _dup_heads(kv[:, 0:128], kh).astype(BF16)
                v2 = _dup_heads(kv[:, 128:256], kh).astype(BF16)
                q_all = _stack_heads(q_ref, kh)
                do_all = _stack_heads(do_ref, kh)
                pt, ps = _att_probs_t(q_all, k2, mask_t, _sink_row(sink_ref, kh))
                dpt = _dot(v2, do_all, NT)
                delta = jnp.sum(pt * dpt, axis=0, keepdims=True)
                dst = pt * (dpt - delta) * ATT_SCALE
                psd = ps * delta
                for h in range(HEADS_PER_KV):
                    dsink = dsink + jnp.where(sink_lane == kh * HEADS_PER_KV + h,
                                              -jnp.sum(psd[:, h * ATT_BLOCK:(h + 1) * ATT_BLOCK]), 0.0)
                _unstack_heads(_dot(k2, dst, TN).T, kh, write)
                dk_acc = _dot(dst, q_all, NN)
                dv_acc = _dot(pt, do_all, NN)
                dk_parts.append(dk_acc + pltpu.roll(dk_acc, HEAD_DIM, 1))
                dv_parts.append(dv_acc + pltpu.roll(dv_acc, HEAD_DIM, 1))
            dk = jnp.where(lane2 < HEAD_DIM, dk_parts[0], dk_parts[1])
            dv = jnp.where(lane2 < HEAD_DIM, dv_parts[0], dv_parts[1])
            dkv_new = jnp.concatenate([dk, dv], axis=1)
            done = carry[...] + dkv_new[0:ATT_BLOCK]

            @pl.when(n > 0)
            def _():
                dkv_ref[...] = done.astype(BF16)
                dbkv_ref[...] += _colsum(done)

            carry[...] = dkv_new[ATT_BLOCK:]
            dsink_ref[...] += dsink

        @pl.when(n == nb)
        def _():
            dkv_ref[...] = carry[...].astype(BF16)
            dbkv_ref[...] += _colsum(carry[...])

    def clamp(n):
        return jnp.minimum(n, nb - 1)

    outs, moved = _call(
        body, name=name, grid=(nb + 1,),
        in_specs=[pl.BlockSpec(memory_space=pltpu.SMEM),
                  pl.BlockSpec((ATT_BLOCK, Q_WIDTH), lambda n: (clamp(n), 0)),
                  pl.BlockSpec((ATT_BLOCK, KV_WIDTH), lambda n: (clamp(n), kvb)),
                  pl.BlockSpec((ATT_BLOCK, KV_WIDTH), lambda n: (jnp.maximum(clamp(n) - 1, 0), kvb)),
                  pl.BlockSpec((ATT_BLOCK, Q_WIDTH), lambda n: (clamp(n), 0))],
        out_specs=[pl.BlockSpec((ATT_BLOCK, Q_WIDTH), lambda n: (clamp(n), 0)),
                   pl.BlockSpec((ATT_BLOCK, KV_WIDTH), lambda n: (jnp.maximum(n - 1, 0), 0)),
                   pl.BlockSpec((1, Q_WIDTH), lambda n: (0, 0)),
                   pl.BlockSpec((1, KV_WIDTH), lambda n: (0, 0)),
                   pl.BlockSpec((1, 128), lambda n: (0, 0))],
        out_shape=[jax.ShapeDtypeStruct((t, Q_WIDTH), BF16), jax.ShapeDtypeStruct((t, KV_WIDTH), BF16),
                   jax.ShapeDtypeStruct((1, Q_WIDTH), F32), jax.ShapeDtypeStruct((1, KV_WIDTH), F32),
                   jax.ShapeDtypeStruct((1, 128), F32)],
        scratch_shapes=[pltpu.VMEM((ATT_BLOCK, KV_WIDTH), F32)],
        sem=("arbitrary",), args=(sinks, qkv, qkv, qkv, d_o), comm=comm)
    return outs if comm is None else (outs, moved)


def _adamw_math(g, w, m, v):
    m = ADAM_B1 * m + (1.0 - ADAM_B1) * g
    v = ADAM_B2 * v + (1.0 - ADAM_B2) * (g * g)
    m_hat = m / (1.0 - ADAM_B1 ** ADAM_STEP)
    v_hat = v / (1.0 - ADAM_B2 ** ADAM_STEP)
    delta = -ADAM_LR * (m_hat / (jnp.sqrt(v_hat) + ADAM_EPS) + ADAM_WD * w)
    return delta, m, v


def _sum_partials(p_ref):
    g = p_ref[0].astype(F32)
    for s in range(1, N_DEV):
        g = g + p_ref[s].astype(F32)
    return g


def _adamw_big(parts, w, m, v, name, tr):
    r, c = w.shape
    parts = [p if isinstance(p, tuple) else (p, 0, p.shape[1]) for p in parts]
    tiles = [rows // tr for _, _, rows in parts]
    starts = [sum(tiles[:l]) for l in range(len(parts))]
    assert all(lo % tr == 0 and rows % tr == 0 for _, lo, rows in parts) and sum(tiles) * tr == r

    def body(*refs):
        p_refs, (w_ref, m_ref, v_ref, g_out, d_out, m_out, v_out) = refs[:len(parts)], refs[len(parts):]
        i = pl.program_id(0)
        for l, p_ref in enumerate(p_refs):
            @pl.when((i >= starts[l]) & (i < starts[l] + tiles[l]))
            def _():
                g = _sum_partials(p_ref)
                g_out[...] = g
                d_out[...], m_out[...], v_out[...] = _adamw_math(g, w_ref[...], m_ref[...], v_ref[...])

    def part_spec(l):
        return pl.BlockSpec((N_DEV, tr, c), lambda i: (0, jnp.clip(i - starts[l], 0, tiles[l] - 1) + parts[l][1] // tr, 0))

    tile = pl.BlockSpec((tr, c), lambda i: (i, 0))
    shape = jax.ShapeDtypeStruct((r, c), F32)
    return pl.pallas_call(
        body, name=name, grid=(r // tr,),
        in_specs=[part_spec(l) for l in range(len(parts))] + [tile, tile, tile],
        out_specs=[tile] * 4, out_shape=[shape] * 4,
        compiler_params=_params("parallel"),
    )(*[p[0] for p in parts], w, m, v)


def _adamw_small(parts, ws, ms, vs, name):
    n = len(ws)

    def body(*refs):
        ins, outs = refs[:4 * n], refs[4 * n:]
        for a in range(n):
            g = _sum_partials(ins[a])
            outs[4 * a][...] = g
            outs[4 * a + 1][...], outs[4 * a + 2][...], outs[4 * a + 3][...] = _adamw_math(
                g, ins[n + a][...], ins[2 * n + a][...], ins[3 * n + a][...])

    out_shape = []
    for w in ws:
        out_shape += [jax.ShapeDtypeStruct(w.shape, F32)] * 4
    return pl.pallas_call(body, name=name, out_shape=out_shape, compiler_params=_params())(*parts, *ws, *ms, *vs)


PACK_LANES = 128
PACK_ROWS = 8


def _pack(arrs):
    flat = jnp.concatenate([a.reshape(-1).astype(F32) for a in arrs])
    unit = PACK_LANES * PACK_ROWS
    total = -(-flat.shape[0] // unit) * unit
    return jnp.pad(flat, (0, total - flat.shape[0])).reshape(-1, PACK_LANES)


def _unpack(buf, shapes):
    flat = buf.reshape(N_DEV, -1)
    out, pos = [], 0
    for s in shapes:
        size = math.prod(s)
        out.append(flat[:, pos:pos + size].reshape((N_DEV,) + tuple(s)))
        pos += size
    return out


def _interleave(g):
    return jnp.transpose(g, (1, 0, 2)).reshape(g.shape[1], -1)


def _deinterleave(w):
    r = w.shape[0]
    return jnp.transpose(w.reshape(r, N_DEV, -1), (1, 0, 2))


def _ffn_backward(dz, x_in, z_in, g_in, h, u, w_up_t, cw, cb, w_down, tag, exchange=(), exchange_late=()):
    du = _matmul(dz, w_down, "nt", BF16, f"ffn{tag}_du", 1024, 1408, 1024)
    d_w_down = _matmul(u, dz, "tn", BF16, f"ffn{tag}_dwdown", 1408, 1024, 2048)
    (dhg, dhv, dcwg, dcwv, dcbg, dcbv), moved = _ffn_mid_bwd(
        h, du, cw, cb, f"ffn{tag}_mid_bwd", comm=_Comm(exchange=[d_w_down.reshape(N_DEV, -1, D_MODEL), *exchange]))
    d_w_up_t = _matmul_tn_pair(dhg, dhv, x_in, BF16, f"ffn{tag}_dwup", 1408, 1024, 1024,
                               comm=_Comm(exchange=exchange_late) if exchange_late else None)
    if exchange_late:
        d_w_up_t, late = d_w_up_t
        moved = moved + late
    dx = _matmul(dhv, w_up_t, "nn", F32, f"ffn{tag}_dx_value", 1024, 1024, D_FF, b_off=1)
    dz_in, dg_in, db_in = _matmul_ln_bwd(dhg, w_up_t, z_in, g_in, dz, f"ffn{tag}_dx_gate_ln_bwd", 512, res=dx)
    return (dz_in, dg_in, db_in, d_w_up_t.reshape(N_DEV, -1, D_MODEL),
            jnp.concatenate([dcwg, dcwv], axis=1), jnp.concatenate([dcbg, dcbv], axis=1), moved)


def kernel(x, ab_w_in, a_conv_w, a_conv_b, a_norm_g, a_norm_b, b_norm_g, b_norm_b, b_spatial_w, b_spatial_b, ab_w_out, c_w_qkv, c_b_qkv, c_sinks, c_w_o, ffn_w_up, ffn_conv_w, ffn_conv_b, ffn_w_down, ln_g, ln_b, loss_target, m_ab_w_in, m_a_conv_w, m_a_conv_b, m_a_norm_g, m_a_norm_b, m_b_norm_g, m_b_norm_b, m_b_spatial_w, m_b_spatial_b, m_ab_w_out, m_c_w_qkv, m_c_b_qkv, m_c_sinks, m_c_w_o, m_ffn_w_up, m_ffn_conv_w, m_ffn_conv_b, m_ffn_w_down, m_ln_g, m_ln_b, v_ab_w_in, v_a_conv_w, v_a_conv_b, v_a_norm_g, v_a_norm_b, v_b_norm_g, v_b_norm_b, v_b_spatial_w, v_b_spatial_b, v_ab_w_out, v_c_w_qkv, v_c_b_qkv, v_c_sinks, v_c_w_o, v_ffn_w_up, v_ffn_conv_w, v_ffn_conv_b, v_ffn_w_down, v_ln_g, v_ln_b):
    me = 4 * lax.axis_index("x") + 2 * lax.axis_index("y") + lax.axis_index("c")
    xt = x[0]
    t = xt.shape[0]

    small_shard_shapes = [a_conv_w.shape, c_b_qkv.shape, ffn_conv_w.shape, ln_g.shape, ln_b.shape]
    up_shard = [jnp.swapaxes(ffn_w_up[l], 0, 1).astype(BF16) for l in range(2)]
    qkv_shard = jnp.swapaxes(c_w_qkv[0], 0, 1).astype(BF16)
    down_shard = [ffn_w_down[l].astype(BF16) for l in range(2)]
    g_win, g_small = _comm_only(
        _Comm(gather=[ab_w_in[0].astype(BF16), _pack([a_conv_w, c_b_qkv, ffn_conv_w, ln_g, ln_b])]), "gather_first")
    w_in = _interleave(g_win)
    g_acw, g_bqkv, g_fcw, g_lng, g_lnb = _unpack(g_small, small_shard_shapes)
    acw = _interleave(g_acw[:, 0])
    bqkv = g_bqkv[:, 0].reshape(1, -1)
    fcw = [_interleave(g_fcw[:, l]) for l in range(2)]
    lng = jnp.transpose(g_lng, (1, 2, 0, 3)).reshape(2, 2, 1, D_MODEL)
    lnb = jnp.transpose(g_lnb, (1, 2, 0, 3)).reshape(2, 2, 1, D_MODEL)
    fcb = [ffn_conv_b[l:l + 1] for l in range(2)]
    ms = b_spatial_w[0]
    mst = jnp.swapaxes(ms, 1, 2)
    sbt = b_spatial_b[0].T

    h0, (g_wout,) = _matmul(xt, w_in, "nn", BF16, "mix_in", 1024, 1024, 1024, comm=_Comm(gather=[ab_w_out[0].astype(BF16)]))
    w_out = g_wout.reshape(D_MODEL, D_MODEL)
    cat, (g_wup0,) = _mixer_mid_fwd(h0, acw, a_conv_b, a_norm_g, a_norm_b, b_norm_g, b_norm_b, ms, sbt, "mix_mid_fwd",
                                    comm=_Comm(gather=[up_shard[0]]))
    w_up0 = g_wup0.reshape(2 * D_FF, D_MODEL)
    z1, x1 = _matmul_res_ln(cat, w_out, xt, lng[0, 0], lnb[0, 0], "mix_out_ln", 512)
    hf0, (g_wdown0, g_wqkv) = _matmul(x1, w_up0, "nt", BF16, "ffn0_up", 1024, 1408, 1024,
                                      comm=_Comm(gather=[down_shard[0], qkv_shard]))
    w_down0 = g_wdown0.reshape(D_FF, D_MODEL)
    w_qkv = g_wqkv.reshape(Q_WIDTH + KV_WIDTH, D_MODEL)
    u0, (g_wup1,) = _ffn_mid_fwd(hf0, fcw[0], fcb[0], "ffn0_mid_fwd", comm=_Comm(gather=[up_shard[1]]))
    w_up1 = g_wup1.reshape(2 * D_FF, D_MODEL)
    (z2, x2), (g_wo,) = _matmul_res_ln(u0, w_down0, z1, lng[0, 1], lnb[0, 1], "ffn0_down_ln", 512, prev=(lng[0, 0], lnb[0, 0]),
                                       comm=_Comm(gather=[c_w_o[0].astype(BF16)]))
    w_o = g_wo.reshape(D_MODEL, D_MODEL)
    qkv = _matmul(x2, w_qkv, "nt", BF16, "att_qkv", 1024, 1280, 1024, bias=bqkv)
    att, (g_wdown1,) = _attn_fwd(qkv, c_sinks, "att_fwd", comm=_Comm(gather=[down_shard[1]]))
    w_down1 = g_wdown1.reshape(D_FF, D_MODEL)
    z3, x3 = _matmul_res_ln(att, w_o, z2, lng[1, 0], lnb[1, 0], "att_out_ln", 512, prev=(lng[0, 1], lnb[0, 1]))
    hf1 = _matmul(x3, w_up1, "nt", BF16, "ffn1_up", 1024, 1408, 1024)
    u1 = _ffn_mid_fwd(hf1, fcw[1], fcb[1], "ffn1_mid_fwd")

    dz4, dg11, db11, loss_terms = _matmul_res_ln_loss(u1, w_down1, z3, lng[1, 1], lnb[1, 1], loss_target[0],
                                                      "ffn1_down_ln_loss", 512, prev=(lng[1, 0], lnb[1, 0]))
    dz3, dg10, db10, d_wup1, d_fcw1, d_fcb1, (p_wdown1,) = _ffn_backward(
        dz4, x3, z3, lng[1, 0], hf1, u1, w_up1, fcw[1], fcb[1], w_down1, 1)
    d_att = _matmul(dz3, w_o, "nt", BF16, "att_dout", 1024, 1024, 1024)
    d_wo = _matmul(att, dz3, "tn", BF16, "att_dwo", 1024, 1024, 512)
    half = d_wup1.shape[1] // 2
    (dq, dkv, dbq, dbkv, dsinks), (p_wup1a,) = _attn_bwd(qkv, d_att, c_sinks, "att_bwd",
                                                        comm=_Comm(exchange=[(d_wup1, 0, half)]))
    d_wqkv = jnp.concatenate([_matmul(dq, x2, "tn", BF16, "att_dwq", 1024, 1024, 1024),
                              _matmul(dkv, x2, "tn", BF16, "att_dwkv", KV_WIDTH, 1024, 1024)], axis=0)
    dx2 = _matmul(dkv, w_qkv, "nn", F32, "att_dx_kv", 1024, 1024, KV_WIDTH, b_off=Q_WIDTH // KV_WIDTH)
    dz2, dg01, db01 = _matmul_ln_bwd(dq, w_qkv, z2, lng[0, 1], dz3, "att_dx_q_ln_bwd", 512, res=dx2)
    dz1, dg00, db00, d_wup0, d_fcw0, d_fcb0, (p_wdown0, p_wup1b, p_wqkv, p_wo) = _ffn_backward(
        dz2, x1, z1, lng[0, 0], hf0, u0, w_up0, fcw[0], fcb[0], w_down0, 0, exchange=[(d_wup1, half, half)],
        exchange_late=[d_wqkv.reshape(N_DEV, -1, D_MODEL), d_wo.reshape(N_DEV, -1, D_MODEL)])
    dcat = _matmul(dz1, w_out, "nt", BF16, "mix_dcat", 1024, 1024, 1024)
    d_wout = _matmul(cat, dz1, "tn", BF16, "mix_dwout", 1024, 1024, 512)
    (dh0, d_acw, d_acb, d_ang, d_anb, d_bng, d_bnb, d_ms, d_sb), (p_wup0, p_wout) = _mixer_mid_bwd(
        h0, dcat, acw, a_conv_b, a_norm_g, a_norm_b, b_norm_g, b_norm_b, ms, mst, sbt, "mix_mid_bwd",
        comm=_Comm(exchange=[d_wup0, d_wout.reshape(N_DEV, -1, D_MODEL)]))
    d_bqkv = jnp.concatenate([dbq, dbkv], axis=1)
    d_lng = jnp.stack([jnp.stack([dg00, dg01]), jnp.stack([dg10, dg11])])
    d_lnb = jnp.stack([jnp.stack([db00, db01]), jnp.stack([db10, db11])])
    small_full = [d_acb, d_ang, d_anb, d_bng, d_bnb, d_ms, d_sb, dsinks[:, :N_Q_HEADS], jnp.concatenate([d_fcb0, d_fcb1], axis=0),
                  d_acw, d_bqkv, jnp.stack([d_fcw0, d_fcw1]), d_lng, d_lnb, loss_terms]
    d_win, (g_small_grads,) = _matmul(xt, dh0, "tn", BF16, "mix_dwin", 1024, 1024, 512, comm=_Comm(gather=[_pack(small_full)]))
    grad_x, (p_win,) = _matmul(dh0, w_in, "nt", F32, "mix_dx", 1024, 1024, 1024, res=dz1, res_scale=ALPHA,
                               comm=_Comm(exchange=[_deinterleave(d_win)]))


    big = {}
    for nm, p, w, m, v, tr, transposed in [
            ("ab_w_in", [p_win], ab_w_in, m_ab_w_in, v_ab_w_in, 256, False),
            ("ab_w_out", [p_wout], ab_w_out, m_ab_w_out, v_ab_w_out, 128, False),
            ("c_w_qkv", [p_wqkv], c_w_qkv, m_c_w_qkv, v_c_w_qkv, 160, True), ("c_w_o", [p_wo], c_w_o, m_c_w_o, v_c_w_o, 128, False),
            ("ffn_w_up", [p_wup0, (p_wup1a, 0, half), (p_wup1b, half, half)], ffn_w_up, m_ffn_w_up, v_ffn_w_up, 176, True),
            ("ffn_w_down", [p_wdown0, p_wdown1], ffn_w_down, m_ffn_w_down, v_ffn_w_down, 176, False)]:
        def two_d(a):
            a = jnp.swapaxes(a, 1, 2) if transposed else a
            return a.reshape(-1, a.shape[-1])

        def back(o):
            return jnp.swapaxes(o.reshape(w.shape[0], w.shape[2], w.shape[1]), 1, 2) if transposed else o.reshape(w.shape)

        outs = _adamw_big(p, two_d(w), two_d(m), two_d(v), "adamw_" + nm, tr)
        big[nm] = [back(o) for o in outs]

    *gs, loss_parts = _unpack(g_small_grads, [a.shape for a in small_full])
    loss = 0.5 / D_MODEL * jnp.sum(loss_parts)

    def my_shard(g, width):
        g = g.reshape(g.shape[:-1] + (N_DEV, width))
        return lax.dynamic_index_in_dim(g, me, axis=g.ndim - 2, keepdims=False)

    small_names = ["a_conv_b", "a_norm_g", "a_norm_b", "b_norm_g", "b_norm_b", "b_spatial_w", "b_spatial_b", "c_sinks", "ffn_conv_b",
                   "a_conv_w", "c_b_qkv", "ffn_conv_w", "ln_g", "ln_b"]
    small_w = [a_conv_b, a_norm_g, a_norm_b, b_norm_g, b_norm_b, b_spatial_w, b_spatial_b, c_sinks, ffn_conv_b,
               a_conv_w, c_b_qkv, ffn_conv_w, ln_g, ln_b]
    small_m = [m_a_conv_b, m_a_norm_g, m_a_norm_b, m_b_norm_g, m_b_norm_b, m_b_spatial_w, m_b_spatial_b, m_c_sinks, m_ffn_conv_b,
               m_a_conv_w, m_c_b_qkv, m_ffn_conv_w, m_ln_g, m_ln_b]
    small_v = [v_a_conv_b, v_a_norm_g, v_a_norm_b, v_b_norm_g, v_b_norm_b, v_b_spatial_w, v_b_spatial_b, v_c_sinks, v_ffn_conv_b,
               v_a_conv_w, v_c_b_qkv, v_ffn_conv_w, v_ln_g, v_ln_b]
    gs[9:] = [my_shard(g, w.shape[-1]) for g, w in zip(gs[9:], small_w[9:])]
    two_d = [(-1, w.shape[-1]) for w in small_w]
    outs = _adamw_small([g.reshape((N_DEV,) + w.reshape(s).shape) for g, w, s in zip(gs, small_w, two_d)],
                        [w.reshape(s) for w, s in zip(small_w, two_d)], [m.reshape(s) for m, s in zip(small_m, two_d)],
                        [v.reshape(s) for v, s in zip(small_v, two_d)], "adamw_small")
    small = {nm: [o.reshape(w.shape) for o in outs[4 * a:4 * a + 4]] for a, (nm, w) in enumerate(zip(small_names, small_w))}

    res = {**big, **small}
    order = ["ab_w_in", "a_conv_w", "a_conv_b", "a_norm_g", "a_norm_b", "b_norm_g", "b_norm_b", "b_spatial_w", "b_spatial_b", "ab_w_out",
             "c_w_qkv", "c_b_qkv", "c_sinks", "c_w_o", "ffn_w_up", "ffn_conv_w", "ffn_conv_b", "ffn_w_down", "ln_g", "ln_b"]
    return (loss, grad_x[None], *[res[nm][0] for nm in order], *[res[nm][1] for nm in order],
            *[res[nm][2] for nm in order], *[res[nm][3] for nm in order])
```

```python
import functools
import math

import jax
import jax.numpy as jnp
from jax import lax
from jax.experimental import pallas as pl
from jax.experimental.pallas import tpu as pltpu

F32 = jnp.float32
BF16 = jnp.bfloat16

N_DEV = 8
D_MODEL = 1024
A_WIDTH = 512
A_KERNEL = 31
B_GROUPS = 4
B_CHUNK = 128
HEAD_DIM = 64
N_Q_HEADS = 16
N_KV_HEADS = 2
ATT_BLOCK = 128
D_FF = 2816
FFN_KERNEL = 3
ALPHA = (2.0 * 2) ** 0.25
LN_EPS = 1e-5
GELU_K = math.sqrt(2.0 / math.pi)
GELU_C = 0.044715
ADAM_LR = 0.001
ADAM_B1 = 0.9
ADAM_B2 = 0.999
ADAM_EPS = 1e-08
ADAM_WD = 0.01
ADAM_STEP = 10
VMEM_LIMIT = 56 * 1024 * 1024
MESH_ID = pl.DeviceIdType.MESH


def _params(*sem):
    return pltpu.CompilerParams(dimension_semantics=sem, vmem_limit_bytes=VMEM_LIMIT)


def _gelu(x):
    t = jnp.tanh(GELU_K * x * (1.0 + GELU_C * x * x))
    return 0.5 * x * (1.0 + t)


def _gelu_and_grad(x):
    x2 = x * x
    t = jnp.tanh(GELU_K * x * (1.0 + GELU_C * x2))
    g = 0.5 * x * (1.0 + t)
    dg = 0.5 * (1.0 + t) + 0.5 * x * (1.0 - t * t) * (GELU_K * (1.0 + 3.0 * GELU_C * x2))
    return g, dg


def _sigmoid(x):
    return 1.0 / (1.0 + jnp.exp(-x))


def _ln_stats(z):
    mu = jnp.mean(z, axis=-1, keepdims=True)
    zc = z - mu
    var = jnp.mean(zc * zc, axis=-1, keepdims=True)
    r = lax.rsqrt(var + LN_EPS)
    return zc * r, r


def _ln_bwd_rows(dn, nh, r):
    return r * (dn - jnp.mean(dn, axis=-1, keepdims=True) - nh * jnp.mean(dn * nh, axis=-1, keepdims=True))


def _colsum(x):
    return jnp.sum(x, axis=0, keepdims=True)


def _dot(a, b, dims):
    return lax.dot_general(a.astype(BF16), b.astype(BF16), (dims, ((), ())), preferred_element_type=F32)


NN = ((1,), (0,))
NT = ((1,), (1,))
TN = ((0,), (0,))


ANY = pl.BlockSpec(memory_space=pl.ANY)
N_RELATIONS = N_DEV - 1


def _my_place():
    return lax.axis_index("x"), lax.axis_index("y"), lax.axis_index("c")


class _Comm:
    def __init__(self, gather=(), exchange=()):
        exchange = [e if isinstance(e, tuple) else (e, 0, e.shape[1]) for e in exchange]
        self.arrs = list(gather) + [e[0] for e in exchange]
        self.n_gather = len(gather)
        self.n = len(self.arrs)
        self.rows = [None] * self.n_gather + [pl.ds(lo, n) for _, lo, n in exchange]

    def out_shape(self):
        return [jax.ShapeDtypeStruct(((N_DEV,) + a.shape) if i < self.n_gather else a.shape, a.dtype)
                for i, a in enumerate(self.arrs)]

    def sems(self):
        return [pltpu.SemaphoreType.DMA((self.n, N_RELATIONS)), pltpu.SemaphoreType.DMA((self.n, N_RELATIONS)),
                pltpu.SemaphoreType.DMA((self.n,))]

    def _gather_copy(self, ins, outs, sems, a, k, place, to, from_input=False):
        px, py, pc = place
        block = outs[a].at[4 * px + 2 * py + pc]
        return pltpu.make_async_remote_copy(
            src_ref=ins[a] if from_input else block, dst_ref=block,
            send_sem=sems[0].at[a, k], recv_sem=sems[1].at[a, k], device_id=to, device_id_type=MESH_ID)

    def _exchange_copy(self, ins, outs, sems, a, k, landing=False):
        x, y, c = _my_place()
        me = 4 * x + 2 * y + c
        peer = (x ^ (k >> 2), y ^ ((k >> 1) & 1), c ^ (k & 1))
        return pltpu.make_async_remote_copy(
            src_ref=ins[a].at[me ^ k, self.rows[a]], dst_ref=outs[a].at[(me ^ k) if landing else me, self.rows[a]],
            send_sem=sems[0].at[a, k - 1], recv_sem=sems[1].at[a, k - 1], device_id=peer, device_id_type=MESH_ID)

    def _local_copy(self, ins, outs, sems, a):
        x, y, c = _my_place()
        me = 4 * x + 2 * y + c
        if a < self.n_gather:
            return pltpu.make_async_copy(ins[a], outs[a].at[me], sems[2].at[a])
        return pltpu.make_async_copy(ins[a].at[me, self.rows[a]], outs[a].at[me, self.rows[a]], sems[2].at[a])

    def _first_stage(self, ins, outs, sems, a):
        x, y, c = _my_place()
        me = (x, y, c)
        chips = [(1 - x, y), (x, 1 - y), (1 - x, 1 - y)]
        return ([self._gather_copy(ins, outs, sems, a, 0, me, (x, y, 1 - c), from_input=True)]
                + [self._gather_copy(ins, outs, sems, a, 1 + j, me, (*chip, c), from_input=True) for j, chip in enumerate(chips)])

    def start(self, ins, outs, sems):
        for a in range(self.n):
            self._local_copy(ins, outs, sems, a).start()
        for a in range(self.n_gather):
            for cp in self._first_stage(ins, outs, sems, a):
                cp.start()
        for k in range(1, N_DEV):
            for a in range(self.n_gather, self.n):
                self._exchange_copy(ins, outs, sems, a, k).start()

    def forward(self, ins, outs, sems):
        x, y, c = _my_place()
        me, sibling = (x, y, c), (x, y, 1 - c)
        for j, chip in enumerate([(1 - x, y), (x, 1 - y), (1 - x, 1 - y)]):
            for a in range(self.n_gather):
                self._gather_copy(ins, outs, sems, a, 1 + j, (*chip, c), me).wait_recv()
                self._gather_copy(ins, outs, sems, a, 4 + j, (*chip, c), sibling).start()

    def finish(self, ins, outs, sems):
        x, y, c = _my_place()
        me, sibling = (x, y, c), (x, y, 1 - c)
        chips = [(1 - x, y), (x, 1 - y), (1 - x, 1 - y)]
        passed = [self._gather_copy(ins, outs, sems, a, 4 + j, (*chip, c), sibling)
                  for j, chip in enumerate(chips) for a in range(self.n_gather)]
        for a in range(self.n_gather):
            self._gather_copy(ins, outs, sems, a, 0, sibling, me).wait_recv()
            for j, chip in enumerate(chips):
                self._gather_copy(ins, outs, sems, a, 4 + j, (*chip, 1 - c), me).wait_recv()
        for k in range(1, N_DEV):
            for a in range(self.n_gather, self.n):
                self._exchange_copy(ins, outs, sems, a, k, landing=True).wait_recv()
        for a in range(self.n_gather):
            for cp in self._first_stage(ins, outs, sems, a):
                cp.wait_send()
        for cp in passed:
            cp.wait_send()
        for k in range(1, N_DEV):
            for a in range(self.n_gather, self.n):
                self._exchange_copy(ins, outs, sems, a, k).wait_send()
        for a in range(self.n):
            self._local_copy(ins, outs, sems, a).wait()


def _comm_only(comm, name):
    def body(*refs):
        ins, outs, sems = refs[:comm.n], refs[comm.n:2 * comm.n], refs[2 * comm.n:]
        comm.start(ins, outs, sems)
        comm.forward(ins, outs, sems)
        comm.finish(ins, outs, sems)

    return pl.pallas_call(body, name=name, in_specs=[ANY] * comm.n, out_specs=[ANY] * comm.n,
                          out_shape=comm.out_shape(), scratch_shapes=comm.sems())(*comm.arrs)


def _call(body, *, name, grid, in_specs, out_specs, out_shape, args, sem, scratch_shapes=(), comm=None):
    in_specs, out_specs, out_shape, scratch_shapes = list(in_specs), list(out_specs), list(out_shape), list(scratch_shapes)
    if comm is None:
        outs = pl.pallas_call(body, name=name, grid=grid, in_specs=in_specs, out_specs=out_specs, out_shape=out_shape,
                              scratch_shapes=scratch_shapes, compiler_params=_params(*sem))(*args)
        return list(outs), []
    n_in, n_out, n_scr, nc = len(in_specs), len(out_specs), len(scratch_shapes), comm.n

    def wrapped(*refs):
        ins, refs = refs[:n_in], refs[n_in:]
        c_in, refs = refs[:nc], refs[nc:]
        outs, refs = refs[:n_out], refs[n_out:]
        c_out, refs = refs[:nc], refs[nc:]
        scr, sems = refs[:n_scr], refs[n_scr:]
        step = functools.reduce(lambda acc, ax: acc * grid[ax] + pl.program_id(ax), range(len(grid)), 0)
        steps = math.prod(grid)

        @pl.when(step == 0)
        def _():
            comm.start(c_in, c_out, sems)

        @pl.when(step == steps - 1)
        def _():
            comm.forward(c_in, c_out, sems)

        body(*ins, *outs, *scr)

        @pl.when(step == steps - 1)
        def _():
            comm.finish(c_in, c_out, sems)

    outs = pl.pallas_call(
        wrapped, name=name, grid=grid, in_specs=in_specs + [ANY] * nc, out_specs=out_specs + [ANY] * nc,
        out_shape=out_shape + comm.out_shape(), scratch_shapes=scratch_shapes + comm.sems(),
        compiler_params=_params(*(["arbitrary"] * len(grid))))(*args, *comm.arrs)
    return list(outs[:n_out]), list(outs[n_out:])


def _matmul(a, b, mode, out_dtype, name, tm, tn, tk, *, bias=None, res=None, res_scale=1.0, b_off=0, comm=None):
    tm = min(tm, a.shape[1] if mode == "tn" else a.shape[0])
    tk = min(tk, a.shape[0] if mode == "tn" else a.shape[1])
    if mode == "nn":
        (m, k), n = a.shape, b.shape[1]
        a_spec = pl.BlockSpec((tm, tk), lambda i, j, kk: (i, kk))
        b_spec = pl.BlockSpec((tk, tn), lambda i, j, kk: (kk + b_off, j))
        dims = NN
    elif mode == "nt":
        (m, k), n = a.shape, b.shape[0]
        a_spec = pl.BlockSpec((tm, tk), lambda i, j, kk: (i, kk))
        b_spec = pl.BlockSpec((tn, tk), lambda i, j, kk: (j, kk + b_off))
        dims = NT
    else:
        (k, m), n = a.shape, b.shape[1]
        a_spec = pl.BlockSpec((tk, tm), lambda i, j, kk: (kk, i))
        b_spec = pl.BlockSpec((tk, tn), lambda i, j, kk: (kk, j))
        dims = TN
    assert m % tm == 0 and n % tn == 0 and k % tk == 0, (name, m, n, k)
    nk = k // tk
    in_specs = [a_spec, b_spec]
    args = [a, b]
    if bias is not None:
        in_specs.append(pl.BlockSpec((1, tn), lambda i, j, kk: (0, j)))
        args.append(bias)
    if res is not None:
        in_specs.append(pl.BlockSpec((tm, tn), lambda i, j, kk: (i, j)))
        args.append(res)

    def finish(out, refs, o_ref):
        pos = 2
        if bias is not None:
            out = out + refs[pos][...]
            pos += 1
        if res is not None:
            out = out + res_scale * refs[pos][...].astype(F32)
        o_ref[...] = out.astype(out_dtype)

    def body_one_step(*refs):
        finish(_dot(refs[0][...], refs[1][...], dims), refs, refs[-1])

    def body(*refs):
        a_ref, b_ref = refs[0], refs[1]
        o_ref, acc = refs[-2], refs[-1]
        kk = pl.program_id(2)

        @pl.when(kk == 0)
        def _():
            acc[...] = jnp.zeros_like(acc)

        acc[...] += _dot(a_ref[...], b_ref[...], dims)

        @pl.when(kk == nk - 1)
        def _():
            finish(acc[...], refs, o_ref)

    (out,), moved = _call(
        body_one_step if nk == 1 else body, name=name, grid=(m // tm, n // tn, nk),
        in_specs=in_specs, out_specs=[pl.BlockSpec((tm, tn), lambda i, j, kk: (i, j))],
        out_shape=[jax.ShapeDtypeStruct((m, n), out_dtype)],
        scratch_shapes=[] if nk == 1 else [pltpu.VMEM((tm, tn), F32)],
        sem=("parallel", "parallel", "arbitrary"), args=args, comm=comm)
    return out if comm is None else (out, moved)


def _matmul_tn_pair(a0, a1, b, out_dtype, name, tm, tn, tk, comm=None):
    (k, m), n = a0.shape, b.shape[1]
    tk = min(tk, k)
    assert a1.shape == a0.shape and m % tm == 0 and n % tn == 0 and k % tk == 0, (name, m, n, k)
    mi, nk = m // tm, k // tk

    def body(a0_ref, a1_ref, b_ref, o_ref, acc):
        i, kk = pl.program_id(0), pl.program_id(2)

        @pl.when(kk == 0)
        def _():
            acc[...] = jnp.zeros_like(acc)

        @pl.when(i < mi)
        def _():
            acc[...] += _dot(a0_ref[...], b_ref[...], TN)

        @pl.when(i >= mi)
        def _():
            acc[...] += _dot(a1_ref[...], b_ref[...], TN)

        @pl.when(kk == nk - 1)
        def _():
            o_ref[...] = acc[...].astype(out_dtype)

    (out,), moved = _call(
        body, name=name, grid=(2 * mi, n // tn, nk),
        in_specs=[pl.BlockSpec((tk, tm), lambda i, j, kk: (jnp.where(i < mi, kk, nk - 1), jnp.minimum(i, mi - 1))),
                  pl.BlockSpec((tk, tm), lambda i, j, kk: (jnp.where(i >= mi, kk, 0), jnp.maximum(i - mi, 0))),
                  pl.BlockSpec((tk, tn), lambda i, j, kk: (kk, j))],
        out_specs=[pl.BlockSpec((tm, tn), lambda i, j, kk: (i, j))],
        out_shape=[jax.ShapeDtypeStruct((2 * m, n), out_dtype)],
        scratch_shapes=[pltpu.VMEM((tm, tn), F32)],
        sem=("parallel", "parallel", "arbitrary"), args=(a0, a1, b), comm=comm)
    return out if comm is None else (out, moved)


def _residual_input(x_ref, prev_refs):
    if not prev_refs:
        return x_ref[...]
    nh, _ = _ln_stats(x_ref[...])
    return nh * prev_refs[0][...] + prev_refs[1][...]


def _matmul_res_ln(a, b, x, g, beta, name, tm, prev=None, comm=None):
    t, k = a.shape
    d = b.shape[1]
    tm = min(tm, t)
    assert t % tm == 0
    n_prev = 0 if prev is None else 2

    def body(a_ref, b_ref, x_ref, g_ref, beta_ref, *rest):
        z_ref, xo_ref = rest[n_prev:]
        z = ALPHA * _residual_input(x_ref, rest[:n_prev]) + _dot(a_ref[...], b_ref[...], NN)
        nh, _ = _ln_stats(z)
        z_ref[...] = z
        xo_ref[...] = (nh * g_ref[...] + beta_ref[...]).astype(BF16)

    row = pl.BlockSpec((tm, d), lambda i: (i, 0))
    vec = pl.BlockSpec((1, d), lambda i: (0, 0))
    outs, moved = _call(
        body, name=name, grid=(t // tm,),
        in_specs=[pl.BlockSpec((tm, k), lambda i: (i, 0)), pl.BlockSpec((k, d), lambda i: (0, 0)), row, vec, vec] + [vec] * n_prev,
        out_specs=[row, row],
        out_shape=[jax.ShapeDtypeStruct((t, d), F32), jax.ShapeDtypeStruct((t, d), BF16)],
        sem=("parallel",), args=(a, b, x, g, beta, *(prev or ())), comm=comm)
    return outs if comm is None else (outs, moved)


def _matmul_ln_bwd(a, b, z, g, dres, name, tm, *, res=None, b_off=0):
    m, k = a.shape
    d = b.shape[1]
    tm = min(tm, m)
    assert m % tm == 0

    def body(*refs):
        a_ref, b_ref, z_ref, g_ref, dres_ref = refs[:5]
        dz_ref, dg_ref, db_ref = refs[-3:]

        @pl.when(pl.program_id(0) == 0)
        def _():
            dg_ref[...] = jnp.zeros_like(dg_ref)
            db_ref[...] = jnp.zeros_like(db_ref)

        dbr = _dot(a_ref[...], b_ref[...], NN)
        if res is not None:
            dbr = dbr + refs[5][...]
        nh, r = _ln_stats(z_ref[...])
        dy = ALPHA * dres_ref[...] + dbr
        dg_ref[...] += _colsum(dy * nh)
        db_ref[...] += _colsum(dy)
        dz_ref[...] = _ln_bwd_rows(dy * g_ref[...], nh, r)

    row = pl.BlockSpec((tm, d), lambda i: (i, 0))
    vec = pl.BlockSpec((1, d), lambda i: (0, 0))
    vshape = jax.ShapeDtypeStruct((1, d), F32)
    return pl.pallas_call(
        body, name=name, grid=(m // tm,),
        in_specs=[pl.BlockSpec((tm, k), lambda i: (i, 0)), pl.BlockSpec((k, d), lambda i: (b_off, 0)), row, vec, row]
        + ([row] if res is not None else []),
        out_specs=[row, vec, vec], out_shape=[jax.ShapeDtypeStruct((m, d), F32), vshape, vshape],
        compiler_params=_params("arbitrary"),
    )(a, b, z, g, dres, *([res] if res is not None else []))


def _matmul_res_ln_loss(a, b, x, g, beta, target, name, tm, prev):
    t, k = a.shape
    d = b.shape[1]
    tm = min(tm, t)

    def body(a_ref, b_ref, x_ref, g_ref, beta_ref, t_ref, gp_ref, bp_ref, dz_ref, dg_ref, db_ref, loss_ref):
        @pl.when(pl.program_id(0) == 0)
        def _():
            dg_ref[...] = jnp.zeros_like(dg_ref)
            db_ref[...] = jnp.zeros_like(db_ref)
            loss_ref[...] = jnp.zeros_like(loss_ref)

        nh, r = _ln_stats(ALPHA * _residual_input(x_ref, (gp_ref, bp_ref)) + _dot(a_ref[...], b_ref[...], NN))
        err = nh * g_ref[...] + beta_ref[...] - t_ref[...]
        loss_ref[...] += _colsum(err * err)
        dy = err * (1.0 / d)
        dg_ref[...] += _colsum(dy * nh)
        db_ref[...] += _colsum(dy)
        dz_ref[...] = _ln_bwd_rows(dy * g_ref[...], nh, r)

    row = pl.BlockSpec((tm, d), lambda i: (i, 0))
    vec = pl.BlockSpec((1, d), lambda i: (0, 0))
    vshape = jax.ShapeDtypeStruct((1, d), F32)
    return pl.pallas_call(
        body, name=name, grid=(t // tm,),
        in_specs=[pl.BlockSpec((tm, k), lambda i: (i, 0)), pl.BlockSpec((k, d), lambda i: (0, 0)), row, vec, vec, row, vec, vec],
        out_specs=[row, vec, vec, vec],
        out_shape=[jax.ShapeDtypeStruct((t, d), F32), vshape, vshape, vshape],
        compiler_params=_params("arbitrary"),
    )(a, b, x, g, beta, target, *prev)


FFN_HALO = 16
FFN_CHUNK = 256
LANES = 128
SUBLANES = 8


def _rows_up(e, start, rows):
    if start % SUBLANES == 0:
        return e[start:start + rows]
    return pltpu.roll(e, e.shape[0] - start, 0)[0:rows]


def _fold(x):
    return jnp.sum(x.reshape(x.shape[0] // SUBLANES, SUBLANES, x.shape[1]), axis=0)


def _ffn_mid_fwd(h, cw, cb, name, tm=1024, tc=1408, comm=None):
    t, f2 = h.shape
    tm = min(tm, t)
    f = f2 // 2
    nj, nt, hb = f // tc, t // tm, tm // FFN_HALO

    ch = min(FFN_CHUNK, tm)

    def body(hg, hgp, hv, hvp, cwg, cwv, cbg, cbv, u_ref, cg_ref, cv_ref):
        i = pl.program_id(1)
        o = FFN_HALO - FFN_KERNEL + 1
        for lg in range(tc // LANES):
            cols = slice(lg * LANES, (lg + 1) * LANES)
            wg, wv = [cwg[k:k + 1, cols] for k in range(FFN_KERNEL)], [cwv[k:k + 1, cols] for k in range(FFN_KERNEL)]
            bg, bv = cbg[:, cols], cbv[:, cols]

            def emit(base, eg, ev):
                cg = wg[0] * _rows_up(eg, o, ch) + wg[1] * _rows_up(eg, o + 1, ch) + wg[2] * _rows_up(eg, o + 2, ch) + bg
                cv = wv[0] * _rows_up(ev, o, ch) + wv[1] * _rows_up(ev, o + 1, ch) + wv[2] * _rows_up(ev, o + 2, ch) + bv
                u_ref[pl.ds(base, ch), cols] = (_gelu(cg) * cv).astype(BF16)
                cg_ref[pl.ds(base, ch), cols] = cg.astype(BF16)
                cv_ref[pl.ds(base, ch), cols] = cv.astype(BF16)

            def first(main, prev):
                return jnp.concatenate([jnp.where(i > 0, prev[:, cols].astype(F32), 0.0), main[0:ch, cols].astype(F32)], axis=0)

            def inner(c, carry):
                base = pl.multiple_of(c * ch, ch)
                emit(base, hg[pl.ds(base - FFN_HALO, ch + FFN_HALO), cols].astype(F32),
                     hv[pl.ds(base - FFN_HALO, ch + FFN_HALO), cols].astype(F32))
                return carry

            emit(0, first(hg, hgp), first(hv, hvp))
            if tm > ch:
                lax.fori_loop(1, tm // ch, inner, 0)

    def main_spec(off):
        return pl.BlockSpec((tm, tc), lambda j, i: (i, j + off))

    def prev_spec(off):
        return pl.BlockSpec((FFN_HALO, tc), lambda j, i: (jnp.maximum(i * hb - 1, 0), j + off))

    def par_spec(rows, off):
        return pl.BlockSpec((rows, tc), lambda j, i: (0, j + off))

    outs, moved = _call(
        body, name=name, grid=(nj, nt),
        in_specs=[main_spec(0), prev_spec(0), main_spec(nj), prev_spec(nj),
                  par_spec(FFN_KERNEL, 0), par_spec(FFN_KERNEL, nj), par_spec(1, 0), par_spec(1, nj)],
        out_specs=[pl.BlockSpec((tm, tc), lambda j, i: (i, j))] * 3,
        out_shape=[jax.ShapeDtypeStruct((t, f), BF16)] * 3,
        sem=("parallel", "arbitrary"), args=(h, h, h, h, cw, cw, cb, cb), comm=comm)
    return outs if comm is None else (outs, moved)


def _ffn_mid_bwd(h, cg, cv, du, cw, name, tm=1024, tc=1408, comm=None):
    t, f2 = h.shape
    tm = min(tm, t)
    f = f2 // 2
    nj, nt, hb = f // tc, t // tm, tm // FFN_HALO

    ch = min(FFN_CHUNK, tm)
    ahead = ch + SUBLANES
    n_ch = tm // ch

    def body(hg, hv, cg_ref, cgn_ref, cv_ref, cvn_ref, du_ref, dun_ref, cwg, cwv,
             dhg_ref, dhv_ref, dcwg_ref, dcwv_ref, dcbg_ref, dcbv_ref):
        i = pl.program_id(1)

        @pl.when(i == 0)
        def _():
            for ref in (dcwg_ref, dcwv_ref, dcbg_ref, dcbv_ref):
                ref[...] = jnp.zeros_like(ref)

        for lg in range(tc // LANES):
            cols = slice(lg * LANES, (lg + 1) * LANES)
            wg, wv = [cwg[k:k + 1, cols] for k in range(FFN_KERNEL)], [cwv[k:k + 1, cols] for k in range(FFN_KERNEL)]

            def emit(base, cg_e, cv_e, du_e, acc):
                cg_a, cv_a, du_a = cg_e[0:ahead], cv_e[0:ahead], du_e[0:ahead]
                gl, dgl = _gelu_and_grad(cg_a)

                def back(d, h_ref, w, dh_ref):
                    later = [d[0:ch], _rows_up(d, 1, ch), _rows_up(d, 2, ch)]
                    dh_ref[pl.ds(base, ch), cols] = (w[2] * later[0] + w[1] * later[1] + w[0] * later[2]).astype(BF16)
                    h_own = h_ref[pl.ds(base, ch), cols].astype(F32)
                    return [_fold(later[0])] + [_fold(later[FFN_KERNEL - 1 - k] * h_own) for k in range(FFN_KERNEL)]

                sums = back(du_a * cv_a * dgl, hg, wg, dhg_ref) + back(du_a * gl, hv, wv, dhv_ref)
                return tuple(a + s_ for a, s_ in zip(acc, sums))

            def inner(c, acc):
                base = pl.multiple_of(c * ch, ch)
                rows = pl.ds(base, ch + FFN_HALO)
                return emit(base, cg_ref[rows, cols].astype(F32), cv_ref[rows, cols].astype(F32), du_ref[rows, cols].astype(F32), acc)

            def last(acc):
                def rows(main, after):
                    return jnp.concatenate([main[tm - ch:tm, cols].astype(F32), after], axis=0)

                du_next = jnp.where(i < nt - 1, dun_ref[:, cols].astype(F32), 0.0)
                return emit(tm - ch, rows(cg_ref, cgn_ref[:, cols].astype(F32)), rows(cv_ref, cvn_ref[:, cols].astype(F32)),
                            rows(du_ref, du_next), acc)

            acc = (jnp.zeros((SUBLANES, LANES), F32),) * (2 * (1 + FFN_KERNEL))
            if n_ch > 1:
                acc = lax.fori_loop(0, n_ch - 1, inner, acc)
            acc = last(acc)
            dcbg_ref[:, cols] += _colsum(acc[0])
            dcbv_ref[:, cols] += _colsum(acc[1 + FFN_KERNEL])
            for k in range(FFN_KERNEL):
                dcwg_ref[k:k + 1, cols] += _colsum(acc[1 + k])
                dcwv_ref[k:k + 1, cols] += _colsum(acc[2 + FFN_KERNEL + k])

    last_blk = t // FFN_HALO - 1

    def main_spec(off):
        return pl.BlockSpec((tm, tc), lambda j, i: (i, j + off))

    def next_spec(off):
        return pl.BlockSpec((FFN_HALO, tc), lambda j, i: (jnp.minimum((i + 1) * hb, last_blk), j + off))

    def par_spec(rows, off):
        return pl.BlockSpec((rows, tc), lambda j, i: (0, j + off))

    out_tile = pl.BlockSpec((tm, tc), lambda j, i: (i, j))
    outs, moved = _call(
        body, name=name, grid=(nj, nt),
        in_specs=[main_spec(0), main_spec(nj), main_spec(0), next_spec(0), main_spec(0), next_spec(0), main_spec(0), next_spec(0),
                  par_spec(FFN_KERNEL, 0), par_spec(FFN_KERNEL, nj)],
        out_specs=[out_tile, out_tile, par_spec(FFN_KERNEL, 0), par_spec(FFN_KERNEL, 0), par_spec(1, 0), par_spec(1, 0)],
        out_shape=[jax.ShapeDtypeStruct((t, f), BF16), jax.ShapeDtypeStruct((t, f), BF16),
                   jax.ShapeDtypeStruct((FFN_KERNEL, f), F32), jax.ShapeDtypeStruct((FFN_KERNEL, f), F32),
                   jax.ShapeDtypeStruct((1, f), F32), jax.ShapeDtypeStruct((1, f), F32)],
        sem=("parallel", "arbitrary"), args=(h, h, cg, cg, cv, cv, du, du, cw, cw), comm=comm)
    return outs if comm is None else (outs, moved)


MIX_HALO = 32


def _glu(hh):
    return hh[:, 0:A_WIDTH] * _sigmoid(hh[:, A_WIDTH:2 * A_WIDTH])


def _fill_row_shifts(s):
    rows = s.shape[1] - SUBLANES
    for j in range(1, SUBLANES):
        s[j, 0:rows, :] = s[0, pl.ds(j, rows), :]


def _rows_from(s, start, rows):
    j = start % SUBLANES
    return s[j, start - j:start - j + rows, :]


def _tril_mask():
    return lax.broadcasted_iota(jnp.int32, (B_CHUNK, B_CHUNK), 0) >= lax.broadcasted_iota(jnp.int32, (B_CHUNK, B_CHUNK), 1)


def _spatial_mix(q, ms_ref, sbt_ref, tm):
    mask = _tril_mask()
    ws = [jnp.where(mask, ms_ref[g], 0.0).astype(BF16) for g in range(B_GROUPS)]
    qb = q.astype(BF16)
    rows = []
    for c in range(tm // B_CHUNK):
        cols = [_dot(ws[g], qb[c * B_CHUNK:(c + 1) * B_CHUNK, g * 128:(g + 1) * 128], NN) + sbt_ref[:, g:g + 1]
                for g in range(B_GROUPS)]
        rows.append(jnp.concatenate(cols, axis=1))
    return jnp.concatenate(rows, axis=0)


def _mixer_mid_fwd(h, cw, cb, ag, ab, bg, bb, ms, sbt, name, tm=256, comm=None):
    t = h.shape[0]
    nt, hb = t // tm, tm // MIX_HALO
    o = MIX_HALO - A_KERNEL + 1

    def body(h_ref, hp_ref, cw_ref, cb_ref, ag_ref, ab_ref, bg_ref, bb_ref, ms_ref, sbt_ref, cat_ref, sp):
        i = pl.program_id(0)
        sp[0, 0:MIX_HALO, :] = jnp.where(i > 0, _glu(hp_ref[:, 0:2 * A_WIDTH].astype(F32)), 0.0)
        sp[0, MIX_HALO:, :] = _glu(h_ref[:, 0:2 * A_WIDTH].astype(F32))
        _fill_row_shifts(sp)
        y = jnp.zeros((tm, A_WIDTH), F32) + cb_ref[...]
        for k in range(A_KERNEL):
            y = y + cw_ref[k:k + 1, :] * _rows_from(sp, o + k, tm)
        nh, _ = _ln_stats(y)
        ln = nh * ag_ref[...] + ab_ref[...]
        cat_ref[:, 0:A_WIDTH] = (ln * _sigmoid(ln)).astype(BF16)
        u = _gelu(h_ref[:, 1024:1536].astype(F32))
        nb, _ = _ln_stats(_gelu(h_ref[:, 1536:2048].astype(F32)))
        mixed = _spatial_mix(nb * bg_ref[...] + bb_ref[...], ms_ref, sbt_ref, tm)
        cat_ref[:, A_WIDTH:] = (u * mixed).astype(BF16)

    vec = pl.BlockSpec((1, A_WIDTH), lambda i: (0, 0))
    (cat,), moved = _call(
        body, name=name, grid=(nt,),
        in_specs=[pl.BlockSpec((tm, 2048), lambda i: (i, 0)),
                  pl.BlockSpec((MIX_HALO, 2048), lambda i: (jnp.maximum(i * hb - 1, 0), 0)),
                  pl.BlockSpec((A_KERNEL, A_WIDTH), lambda i: (0, 0)), vec, vec, vec, vec, vec,
                  pl.BlockSpec((B_GROUPS, B_CHUNK, B_CHUNK), lambda i: (0, 0, 0)),
                  pl.BlockSpec((B_CHUNK, B_GROUPS), lambda i: (0, 0))],
        out_specs=[pl.BlockSpec((tm, D_MODEL), lambda i: (i, 0))],
        out_shape=[jax.ShapeDtypeStruct((t, D_MODEL), BF16)],
        scratch_shapes=[pltpu.VMEM((SUBLANES, tm + MIX_HALO, A_WIDTH), F32)],
        sem=("parallel",), args=(h, h, cw, cb, ag, ab, bg, bb, ms, sbt), comm=comm)
    return cat if comm is None else (cat, moved)


def _mixer_mid_bwd(h, dcat, cw, cb, ag, ab, bg, bb, ms, mst, sbt, name, tm=256, comm=None):
    t = h.shape[0]
    nt, hb = t // tm, tm // MIX_HALO
    o = MIX_HALO - A_KERNEL + 1
    r = tm + MIX_HALO
    nchunk = tm // B_CHUNK

    def body(h_ref, hp_ref, hn_ref, dc_ref, dcn_ref, cw_ref, cb_ref, ag_ref, ab_ref, bg_ref, bb_ref, ms_ref, mst_ref, sbt_ref,
             dh_ref, dcw_ref, dcb_ref, dag_ref, dab_ref, dbg_ref, dbb_ref, dms_ref, dsb_ref, sp, sdy, sbacc):
        i = pl.program_id(0)

        @pl.when(i == 0)
        def _():
            for ref in (dcw_ref, dcb_ref, dag_ref, dab_ref, dbg_ref, dbb_ref, dms_ref, dsb_ref, sbacc):
                ref[...] = jnp.zeros_like(ref)

        sp[0, 0:MIX_HALO, :] = jnp.where(i > 0, _glu(hp_ref[:, 0:2 * A_WIDTH].astype(F32)), 0.0)
        sp[0, MIX_HALO:MIX_HALO + tm, :] = _glu(h_ref[:, 0:2 * A_WIDTH].astype(F32))
        sp[0, MIX_HALO + tm:, :] = _glu(hn_ref[:, 0:2 * A_WIDTH].astype(F32))
        _fill_row_shifts(sp)
        y = jnp.zeros((r, A_WIDTH), F32) + cb_ref[...]
        for k in range(A_KERNEL):
            y = y + cw_ref[k:k + 1, :] * _rows_from(sp, o + k, r)
        nh, rs = _ln_stats(y)
        ln = nh * ag_ref[...] + ab_ref[...]
        sg = _sigmoid(ln)
        dao = jnp.concatenate([dc_ref[:, 0:A_WIDTH].astype(F32),
                               jnp.where(i < nt - 1, dcn_ref[:, 0:A_WIDTH].astype(F32), 0.0)], axis=0)
        dln = dao * (sg * (1.0 + ln * (1.0 - sg)))
        dag_ref[...] += _colsum(dln[0:tm] * nh[0:tm])
        dab_ref[...] += _colsum(dln[0:tm])
        sdy[0] = _ln_bwd_rows(dln * ag_ref[...], nh, rs)
        _fill_row_shifts(sdy)
        dy_own = sdy[0, 0:tm, :]
        dcb_ref[...] += _colsum(dy_own)
        dp = jnp.zeros((tm, A_WIDTH), F32)
        for k in range(A_KERNEL):
            dcw_ref[k:k + 1, :] += _colsum(dy_own * _rows_from(sp, o + k, tm))
            dp = dp + cw_ref[k:k + 1, :] * _rows_from(sdy, A_KERNEL - 1 - k, tm)
        av = h_ref[:, 0:A_WIDTH].astype(F32)
        s = _sigmoid(h_ref[:, A_WIDTH:2 * A_WIDTH].astype(F32))
        dh_ref[:, 0:A_WIDTH] = (dp * s).astype(BF16)
        dh_ref[:, A_WIDTH:2 * A_WIDTH] = (dp * av * s * (1.0 - s)).astype(BF16)

        u, dgu = _gelu_and_grad(h_ref[:, 1024:1536].astype(F32))
        w, dgw = _gelu_and_grad(h_ref[:, 1536:2048].astype(F32))
        nb, rb = _ln_stats(w)
        q = nb * bg_ref[...] + bb_ref[...]
        mixed = _spatial_mix(q, ms_ref, sbt_ref, tm)
        dbo = dc_ref[:, A_WIDTH:].astype(F32)
        dh_ref[:, 1024:1536] = (dbo * mixed * dgu).astype(BF16)
        dmx = dbo * u
        mask = _tril_mask()
        wst = [jnp.where(mask.T, mst_ref[g], 0.0).astype(BF16) for g in range(B_GROUPS)]
        qb = q.astype(BF16)
        dmb = dmx.astype(BF16)
        rows = []
        for c in range(nchunk):
            cols = []
            for g in range(B_GROUPS):
                rs_, cs_ = slice(c * B_CHUNK, (c + 1) * B_CHUNK), slice(g * 128, (g + 1) * 128)
                sbacc[g] += dmx[rs_, cs_]
                dms_ref[g] += _dot(dmb[rs_, cs_], qb[rs_, cs_], NT)
                cols.append(_dot(wst[g], dmb[rs_, cs_], NN))
            rows.append(jnp.concatenate(cols, axis=1))
        dq = jnp.concatenate(rows, axis=0)
        dbg_ref[...] += _colsum(dq * nb)
        dbb_ref[...] += _colsum(dq)
        dh_ref[:, 1536:2048] = (_ln_bwd_rows(dq * bg_ref[...], nb, rb) * dgw).astype(BF16)

        @pl.when(i == nt - 1)
        def _():
            for g in range(B_GROUPS):
                dms_ref[g] = jnp.where(mask, dms_ref[g], 0.0)
                dsb_ref[g] = jnp.sum(sbacc[g], axis=1, keepdims=True)

    last_blk = t // MIX_HALO - 1
    vec = pl.BlockSpec((1, A_WIDTH), lambda i: (0, 0))
    mat = pl.BlockSpec((B_GROUPS, B_CHUNK, B_CHUNK), lambda i: (0, 0, 0))
    taps = pl.BlockSpec((A_KERNEL, A_WIDTH), lambda i: (0, 0))

    def halo(width, which):
        if which == "prev":
            return pl.BlockSpec((MIX_HALO, width), lambda i: (jnp.maximum(i * hb - 1, 0), 0))
        return pl.BlockSpec((MIX_HALO, width), lambda i: (jnp.minimum((i + 1) * hb, last_blk), 0))

    vshape = jax.ShapeDtypeStruct((1, A_WIDTH), F32)
    outs, moved = _call(
        body, name=name, grid=(nt,),
        in_specs=[pl.BlockSpec((tm, 2048), lambda i: (i, 0)), halo(2048, "prev"), halo(2048, "next"),
                  pl.BlockSpec((tm, D_MODEL), lambda i: (i, 0)), halo(D_MODEL, "next"),
                  taps, vec, vec, vec, vec, vec, mat, mat, pl.BlockSpec((B_CHUNK, B_GROUPS), lambda i: (0, 0))],
        out_specs=[pl.BlockSpec((tm, 2048), lambda i: (i, 0)), taps, vec, vec, vec, vec, vec, mat,
                   pl.BlockSpec((B_GROUPS, B_CHUNK, 1), lambda i: (0, 0, 0))],
        out_shape=[jax.ShapeDtypeStruct((t, 2048), BF16), jax.ShapeDtypeStruct((A_KERNEL, A_WIDTH), F32),
                   vshape, vshape, vshape, vshape, vshape,
                   jax.ShapeDtypeStruct((B_GROUPS, B_CHUNK, B_CHUNK), F32), jax.ShapeDtypeStruct((B_GROUPS, B_CHUNK, 1), F32)],
        scratch_shapes=[pltpu.VMEM((SUBLANES, tm + 2 * MIX_HALO, A_WIDTH), F32), pltpu.VMEM((SUBLANES, r, A_WIDTH), F32),
                        pltpu.VMEM((B_GROUPS, B_CHUNK, B_CHUNK), F32)],
        sem=("arbitrary",), args=(h, h, h, dcat, dcat, cw, cb, ag, ab, bg, bb, ms, mst, sbt), comm=comm)
    return outs if comm is None else (outs, moved)


Q_WIDTH = N_Q_HEADS * HEAD_DIM
KV_WIDTH = 2 * N_KV_HEADS * HEAD_DIM
PAIRS_PER_KV = N_Q_HEADS // N_KV_HEADS // 2
ATT_SCALE = 1.0 / math.sqrt(HEAD_DIM)


def _dup_heads(pair_cols, kv_head):
    lane = lax.broadcasted_iota(jnp.int32, pair_cols.shape, 1)
    rolled = pltpu.roll(pair_cols, HEAD_DIM, 1)
    first = lane < HEAD_DIM
    return jnp.where(first, pair_cols, rolled) if kv_head == 0 else jnp.where(first, rolled, pair_cols)


HEADS_PER_KV = N_Q_HEADS // N_KV_HEADS


def _stack_heads(ref, kh):
    lane = lax.broadcasted_iota(jnp.int32, (ATT_BLOCK, 128), 1)
    rows = []
    for pr in range(PAIRS_PER_KV):
        c0 = (kh * PAIRS_PER_KV + pr) * 128
        pair = ref[:, c0:c0 + 128]
        rows += [jnp.where(lane < HEAD_DIM, pair, jnp.zeros_like(pair)), jnp.where(lane < HEAD_DIM, jnp.zeros_like(pair), pair)]
    return jnp.concatenate(rows, axis=0)


def _unstack_heads(stacked, kh, write):
    lane = lax.broadcasted_iota(jnp.int32, (ATT_BLOCK, 128), 1)
    for pr in range(PAIRS_PER_KV):
        first = stacked[(2 * pr) * ATT_BLOCK:(2 * pr + 1) * ATT_BLOCK]
        second = stacked[(2 * pr + 1) * ATT_BLOCK:(2 * pr + 2) * ATT_BLOCK]
        write((kh * PAIRS_PER_KV + pr) * 128, jnp.where(lane < HEAD_DIM, first, second))


def _sink_row(sink_ref, kh):
    return jnp.concatenate([jnp.full((1, ATT_BLOCK), sink_ref[0, kh * HEADS_PER_KV + h], F32) for h in range(HEADS_PER_KV)], axis=1)


def _att_mask_t(n):
    sj = lax.broadcasted_iota(jnp.int32, (2 * ATT_BLOCK, HEADS_PER_KV * ATT_BLOCK), 0)
    qi = lax.broadcasted_iota(jnp.int32, (2 * ATT_BLOCK, HEADS_PER_KV * ATT_BLOCK), 1) & (ATT_BLOCK - 1)
    diff = qi + ATT_BLOCK - sj
    return (diff >= 0) & (diff < ATT_BLOCK) & ((n > 0) | (sj >= ATT_BLOCK))


def _att_probs_t(q_all, k2, mask_t, sink):
    st = _dot(k2, q_all, NT) * ATT_SCALE
    st = jnp.where(mask_t, st, -jnp.inf)
    m = jnp.maximum(jnp.max(st, axis=0, keepdims=True), sink)
    e = jnp.exp(st - m)
    es = jnp.exp(sink - m)
    inv = 1.0 / (jnp.sum(e, axis=0, keepdims=True) + es)
    return e * inv, es * inv


def _attn_fwd(qkv, sinks, name, comm=None):
    t = qkv.shape[0]
    nb = t // ATT_BLOCK
    kvb = Q_WIDTH // KV_WIDTH

    def body(sink_ref, q_ref, kv_ref, kvp_ref, o_ref):
        n = pl.program_id(0)
        mask_t = _att_mask_t(n)
        kv = jnp.concatenate([kvp_ref[...], kv_ref[...]], axis=0).astype(F32)

        def write(c0, pair):
            o_ref[:, c0:c0 + 128] = pair.astype(BF16)

        for kh in range(N_KV_HEADS):
            k2 = _dup_heads(kv[:, 0:128], kh).astype(BF16)
            v2 = _dup_heads(kv[:, 128:256], kh).astype(BF16)
            pt, _ = _att_probs_t(_stack_heads(q_ref, kh), k2, mask_t, _sink_row(sink_ref, kh))
            _unstack_heads(_dot(v2, pt, TN).T, kh, write)

    (out,), moved = _call(
        body, name=name, grid=(nb,),
        in_specs=[pl.BlockSpec(memory_space=pltpu.SMEM),
                  pl.BlockSpec((ATT_BLOCK, Q_WIDTH), lambda n: (n, 0)),
                  pl.BlockSpec((ATT_BLOCK, KV_WIDTH), lambda n: (n, kvb)),
                  pl.BlockSpec((ATT_BLOCK, KV_WIDTH), lambda n: (jnp.maximum(n - 1, 0), kvb))],
        out_specs=[pl.BlockSpec((ATT_BLOCK, Q_WIDTH), lambda n: (n, 0))],
        out_shape=[jax.ShapeDtypeStruct((t, Q_WIDTH), BF16)],
        sem=("parallel",), args=(sinks, qkv, qkv, qkv), comm=comm)
    return out if comm is None else (out, moved)


def _attn_bwd(qkv, d_o, sinks, name, comm=None):
    t = qkv.shape[0]
    nb = t // ATT_BLOCK
    kvb = Q_WIDTH // KV_WIDTH

    def body(sink_ref, q_ref, kv_ref, kvp_ref, do_ref, dq_ref, dkv_ref, dbq_ref, dbkv_ref, dsink_ref, carry):
        n = pl.program_id(0)

        @pl.when(n == 0)
        def _():
            for ref in (dbq_ref, dbkv_ref, dsink_ref, carry):
                ref[...] = jnp.zeros_like(ref)
            dkv_ref[...] = jnp.zeros_like(dkv_ref)

        @pl.when(n < nb)
        def _():
            mask_t = _att_mask_t(n)
            kv = jnp.concatenate([kvp_ref[...], kv_ref[...]], axis=0).astype(F32)
            lane2 = lax.broadcasted_iota(jnp.int32, (2 * ATT_BLOCK, 128), 1)
            sink_lane = lax.broadcasted_iota(jnp.int32, (1, 128), 1)
            dsink = jnp.zeros((1, 128), F32)
            dk_parts, dv_parts = [], []

            def write(c0, pair):
                dbq_ref[:, c0:c0 + 128] += _colsum(pair)
                dq_ref[:, c0:c0 + 128] = pair.astype(BF16)

            for kh in range(N_KV_HEADS):
                k2 = _dup_heads(kv[:, 0:128], kh).astype(BF16)
                v2 = _dup_heads(kv[:, 128:256], kh).astype(BF16)
                q_all = _stack_heads(q_ref, kh)
                do_all = _stack_heads(do_ref, kh)
                pt, ps = _att_probs_t(q_all, k2, mask_t, _sink_row(sink_ref, kh))
                dpt = _dot(v2, do_all, NT)
                delta = jnp.sum(pt * dpt, axis=0, keepdims=True)
                dst = pt * (dpt - delta) * ATT_SCALE
                psd = ps * delta
                for h in range(HEADS_PER_KV):
                    dsink = dsink + jnp.where(sink_lane == kh * HEADS_PER_KV + h,
                                              -jnp.sum(psd[:, h * ATT_BLOCK:(h + 1) * ATT_BLOCK]), 0.0)
                _unstack_heads(_dot(k2, dst, TN).T, kh, write)
                dk_acc = _dot(dst, q_all, NN)
                dv_acc = _dot(pt, do_all, NN)
                dk_parts.append(dk_acc + pltpu.roll(dk_acc, HEAD_DIM, 1))
                dv_parts.append(dv_acc + pltpu.roll(dv_acc, HEAD_DIM, 1))
            dk = jnp.where(lane2 < HEAD_DIM, dk_parts[0], dk_parts[1])
            dv = jnp.where(lane2 < HEAD_DIM, dv_parts[0], dv_parts[1])
            dkv_new = jnp.concatenate([dk, dv], axis=1)
            done = carry[...] + dkv_new[0:ATT_BLOCK]

            @pl.when(n > 0)
            def _():
                dkv_ref[...] = done.astype(BF16)
                dbkv_ref[...] += _colsum(done)

            carry[...] = dkv_new[ATT_BLOCK:]
            dsink_ref[...] += dsink

        @pl.when(n == nb)
        def _():
            dkv_ref[...] = carry[...].astype(BF16)
            dbkv_ref[...] += _colsum(carry[...])

    def clamp(n):
        return jnp.minimum(n, nb - 1)

    outs, moved = _call(
        body, name=name, grid=(nb + 1,),
        in_specs=[pl.BlockSpec(memory_space=pltpu.SMEM),
                  pl.BlockSpec((ATT_BLOCK, Q_WIDTH), lambda n: (clamp(n), 0)),
                  pl.BlockSpec((ATT_BLOCK, KV_WIDTH), lambda n: (clamp(n), kvb)),
                  pl.BlockSpec((ATT_BLOCK, KV_WIDTH), lambda n: (jnp.maximum(clamp(n) - 1, 0), kvb)),
                  pl.BlockSpec((ATT_BLOCK, Q_WIDTH), lambda n: (clamp(n), 0))],
        out_specs=[pl.BlockSpec((ATT_BLOCK, Q_WIDTH), lambda n: (clamp(n), 0)),
                   pl.BlockSpec((ATT_BLOCK, KV_WIDTH), lambda n: (jnp.maximum(n - 1, 0), 0)),
                   pl.BlockSpec((1, Q_WIDTH), lambda n: (0, 0)),
                   pl.BlockSpec((1, KV_WIDTH), lambda n: (0, 0)),
                   pl.BlockSpec((1, 128), lambda n: (0, 0))],
        out_shape=[jax.ShapeDtypeStruct((t, Q_WIDTH), BF16), jax.ShapeDtypeStruct((t, KV_WIDTH), BF16),
                   jax.ShapeDtypeStruct((1, Q_WIDTH), F32), jax.ShapeDtypeStruct((1, KV_WIDTH), F32),
                   jax.ShapeDtypeStruct((1, 128), F32)],
        scratch_shapes=[pltpu.VMEM((ATT_BLOCK, KV_WIDTH), F32)],
        sem=("arbitrary",), args=(sinks, qkv, qkv, qkv, d_o), comm=comm)
    return outs if comm is None else (outs, moved)


def _adamw_math(g, w, m, v):
    m = ADAM_B1 * m + (1.0 - ADAM_B1) * g
    v = ADAM_B2 * v + (1.0 - ADAM_B2) * (g * g)
    m_hat = m / (1.0 - ADAM_B1 ** ADAM_STEP)
    v_hat = v / (1.0 - ADAM_B2 ** ADAM_STEP)
    delta = -ADAM_LR * (m_hat / (jnp.sqrt(v_hat) + ADAM_EPS) + ADAM_WD * w)
    return delta, m, v


def _sum_partials(p_ref):
    g = p_ref[0].astype(F32)
    for s in range(1, N_DEV):
        g = g + p_ref[s].astype(F32)
    return g


def _adamw_big(parts, w, m, v, name, tr):
    r, c = w.shape
    parts = [p if isinstance(p, tuple) else (p, 0, p.shape[1]) for p in parts]
    tiles = [rows // tr for _, _, rows in parts]
    starts = [sum(tiles[:l]) for l in range(len(parts))]
    assert all(lo % tr == 0 and rows % tr == 0 for _, lo, rows in parts) and sum(tiles) * tr == r

    def body(*refs):
        p_refs, (w_ref, m_ref, v_ref, g_out, d_out, m_out, v_out) = refs[:len(parts)], refs[len(parts):]
        i = pl.program_id(0)
        for l, p_ref in enumerate(p_refs):
            @pl.when((i >= starts[l]) & (i < starts[l] + tiles[l]))
            def _():
                g = _sum_partials(p_ref)
                g_out[...] = g
                d_out[...], m_out[...], v_out[...] = _adamw_math(g, w_ref[...], m_ref[...], v_ref[...])

    def part_spec(l):
        return pl.BlockSpec((N_DEV, tr, c), lambda i: (0, jnp.clip(i - starts[l], 0, tiles[l] - 1) + parts[l][1] // tr, 0))

    tile = pl.BlockSpec((tr, c), lambda i: (i, 0))
    shape = jax.ShapeDtypeStruct((r, c), F32)
    return pl.pallas_call(
        body, name=name, grid=(r // tr,),
        in_specs=[part_spec(l) for l in range(len(parts))] + [tile, tile, tile],
        out_specs=[tile] * 4, out_shape=[shape] * 4,
        compiler_params=_params("parallel"),
    )(*[p[0] for p in parts], w, m, v)


def _adamw_small(parts, ws, ms, vs, name):
    n = len(ws)

    def body(*refs):
        ins, outs = refs[:4 * n], refs[4 * n:]
        for a in range(n):
            g = _sum_partials(ins[a])
            outs[4 * a][...] = g
            outs[4 * a + 1][...], outs[4 * a + 2][...], outs[4 * a + 3][...] = _adamw_math(
                g, ins[n + a][...], ins[2 * n + a][...], ins[3 * n + a][...])

    out_shape = []
    for w in ws:
        out_shape += [jax.ShapeDtypeStruct(w.shape, F32)] * 4
    return pl.pallas_call(body, name=name, out_shape=out_shape, compiler_params=_params())(*parts, *ws, *ms, *vs)


PACK_LANES = 128
PACK_ROWS = 8


def _pack(arrs):
    flat = jnp.concatenate([a.reshape(-1).astype(F32) for a in arrs])
    unit = PACK_LANES * PACK_ROWS
    total = -(-flat.shape[0] // unit) * unit
    return jnp.pad(flat, (0, total - flat.shape[0])).reshape(-1, PACK_LANES)


def _unpack(buf, shapes):
    flat = buf.reshape(N_DEV, -1)
    out, pos = [], 0
    for s in shapes:
        size = math.prod(s)
        out.append(flat[:, pos:pos + size].reshape((N_DEV,) + tuple(s)))
        pos += size
    return out


def _interleave(g):
    return jnp.transpose(g, (1, 0, 2)).reshape(g.shape[1], -1)


def _deinterleave(w):
    r = w.shape[0]
    return jnp.transpose(w.reshape(r, N_DEV, -1), (1, 0, 2))


def _ffn_backward(dz, x_in, z_in, g_in, h, cg, cv, u, w_up_t, cw, w_down, tag, exchange=(), exchange_late=()):
    du = _matmul(dz, w_down, "nt", BF16, f"ffn{tag}_du", 1024, 1408, 1024)
    d_w_down = _matmul(u, dz, "tn", BF16, f"ffn{tag}_dwdown", 1408, 1024, 2048)
    (dhg, dhv, dcwg, dcwv, dcbg, dcbv), moved = _ffn_mid_bwd(
        h, cg, cv, du, cw, f"ffn{tag}_mid_bwd", comm=_Comm(exchange=[d_w_down.reshape(N_DEV, -1, D_MODEL), *exchange]))
    d_w_up_t = _matmul_tn_pair(dhg, dhv, x_in, BF16, f"ffn{tag}_dwup", 1408, 1024, 1024,
                               comm=_Comm(exchange=exchange_late) if exchange_late else None)
    if exchange_late:
        d_w_up_t, late = d_w_up_t
        moved = moved + late
    dx = _matmul(dhv, w_up_t, "nn", F32, f"ffn{tag}_dx_value", 1024, 1024, D_FF, b_off=1)
    dz_in, dg_in, db_in = _matmul_ln_bwd(dhg, w_up_t, z_in, g_in, dz, f"ffn{tag}_dx_gate_ln_bwd", 512, res=dx)
    return (dz_in, dg_in, db_in, d_w_up_t.reshape(N_DEV, -1, D_MODEL),
            jnp.concatenate([dcwg, dcwv], axis=1), jnp.concatenate([dcbg, dcbv], axis=1), moved)


def kernel(x, ab_w_in, a_conv_w, a_conv_b, a_norm_g, a_norm_b, b_norm_g, b_norm_b, b_spatial_w, b_spatial_b, ab_w_out, c_w_qkv, c_b_qkv, c_sinks, c_w_o, ffn_w_up, ffn_conv_w, ffn_conv_b, ffn_w_down, ln_g, ln_b, loss_target, m_ab_w_in, m_a_conv_w, m_a_conv_b, m_a_norm_g, m_a_norm_b, m_b_norm_g, m_b_norm_b, m_b_spatial_w, m_b_spatial_b, m_ab_w_out, m_c_w_qkv, m_c_b_qkv, m_c_sinks, m_c_w_o, m_ffn_w_up, m_ffn_conv_w, m_ffn_conv_b, m_ffn_w_down, m_ln_g, m_ln_b, v_ab_w_in, v_a_conv_w, v_a_conv_b, v_a_norm_g, v_a_norm_b, v_b_norm_g, v_b_norm_b, v_b_spatial_w, v_b_spatial_b, v_ab_w_out, v_c_w_qkv, v_c_b_qkv, v_c_sinks, v_c_w_o, v_ffn_w_up, v_ffn_conv_w, v_ffn_conv_b, v_ffn_w_down, v_ln_g, v_ln_b):
    me = 4 * lax.axis_index("x") + 2 * lax.axis_index("y") + lax.axis_index("c")
    xt = x[0]
    t = xt.shape[0]

    small_shard_shapes = [a_conv_w.shape, c_b_qkv.shape, ffn_conv_w.shape, ln_g.shape, ln_b.shape]
    up_shard = [jnp.swapaxes(ffn_w_up[l], 0, 1).astype(BF16) for l in range(2)]
    qkv_shard = jnp.swapaxes(c_w_qkv[0], 0, 1).astype(BF16)
    down_shard = [ffn_w_down[l].astype(BF16) for l in range(2)]
    g_win, g_small = _comm_only(
        _Comm(gather=[ab_w_in[0].astype(BF16), _pack([a_conv_w, c_b_qkv, ffn_conv_w, ln_g, ln_b])]), "gather_first")
    w_in = _interleave(g_win)
    g_acw, g_bqkv, g_fcw, g_lng, g_lnb = _unpack(g_small, small_shard_shapes)
    acw = _interleave(g_acw[:, 0])
    bqkv = g_bqkv[:, 0].reshape(1, -1)
    fcw = [_interleave(g_fcw[:, l]) for l in range(2)]
    lng = jnp.transpose(g_lng, (1, 2, 0, 3)).reshape(2, 2, 1, D_MODEL)
    lnb = jnp.transpose(g_lnb, (1, 2, 0, 3)).reshape(2, 2, 1, D_MODEL)
    fcb = [ffn_conv_b[l:l + 1] for l in range(2)]
    ms = b_spatial_w[0]
    mst = jnp.swapaxes(ms, 1, 2)
    sbt = b_spatial_b[0].T

    h0, (g_wout,) = _matmul(xt, w_in, "nn", BF16, "mix_in", 1024, 1024, 1024, comm=_Comm(gather=[ab_w_out[0].astype(BF16)]))
    w_out = g_wout.reshape(D_MODEL, D_MODEL)
    cat, (g_wup0,) = _mixer_mid_fwd(h0, acw, a_conv_b, a_norm_g, a_norm_b, b_norm_g, b_norm_b, ms, sbt, "mix_mid_fwd",
                                    comm=_Comm(gather=[up_shard[0]]))
    w_up0 = g_wup0.reshape(2 * D_FF, D_MODEL)
    z1, x1 = _matmul_res_ln(cat, w_out, xt, lng[0, 0], lnb[0, 0], "mix_out_ln", 512)
    hf0, (g_wdown0, g_wqkv) = _matmul(x1, w_up0, "nt", BF16, "ffn0_up", 1024, 1408, 1024,
                                      comm=_Comm(gather=[down_shard[0], qkv_shard]))
    w_down0 = g_wdown0.reshape(D_FF, D_MODEL)
    w_qkv = g_wqkv.reshape(Q_WIDTH + KV_WIDTH, D_MODEL)
    (u0, cg0, cv0), (g_wup1,) = _ffn_mid_fwd(hf0, fcw[0], fcb[0], "ffn0_mid_fwd", comm=_Comm(gather=[up_shard[1]]))
    w_up1 = g_wup1.reshape(2 * D_FF, D_MODEL)
    (z2, x2), (g_wo,) = _matmul_res_ln(u0, w_down0, z1, lng[0, 1], lnb[0, 1], "ffn0_down_ln", 512, prev=(lng[0, 0], lnb[0, 0]),
                                       comm=_Comm(gather=[c_w_o[0].astype(BF16)]))
    w_o = g_wo.reshape(D_MODEL, D_MODEL)
    qkv = _matmul(x2, w_qkv, "nt", BF16, "att_qkv", 1024, 1280, 1024, bias=bqkv)
    att, (g_wdown1,) = _attn_fwd(qkv, c_sinks, "att_fwd", comm=_Comm(gather=[down_shard[1]]))
    w_down1 = g_wdown1.reshape(D_FF, D_MODEL)
    z3, x3 = _matmul_res_ln(att, w_o, z2, lng[1, 0], lnb[1, 0], "att_out_ln", 512, prev=(lng[0, 1], lnb[0, 1]))
    hf1 = _matmul(x3, w_up1, "nt", BF16, "ffn1_up", 1024, 1408, 1024)
    u1, cg1, cv1 = _ffn_mid_fwd(hf1, fcw[1], fcb[1], "ffn1_mid_fwd")

    dz4, dg11, db11, loss_terms = _matmul_res_ln_loss(u1, w_down1, z3, lng[1, 1], lnb[1, 1], loss_target[0],
                                                      "ffn1_down_ln_loss", 512, prev=(lng[1, 0], lnb[1, 0]))
    dz3, dg10, db10, d_wup1, d_fcw1, d_fcb1, (p_wdown1,) = _ffn_backward(
        dz4, x3, z3, lng[1, 0], hf1, cg1, cv1, u1, w_up1, fcw[1], w_down1, 1)
    d_att = _matmul(dz3, w_o, "nt", BF16, "att_dout", 1024, 1024, 1024)
    d_wo = _matmul(att, dz3, "tn", BF16, "att_dwo", 1024, 1024, 512)
    half = d_wup1.shape[1] // 2
    (dq, dkv, dbq, dbkv, dsinks), (p_wup1a,) = _attn_bwd(qkv, d_att, c_sinks, "att_bwd",
                                                        comm=_Comm(exchange=[(d_wup1, 0, half)]))
    d_wqkv = jnp.concatenate([_matmul(dq, x2, "tn", BF16, "att_dwq", 1024, 1024, 1024),
                              _matmul(dkv, x2, "tn", BF16, "att_dwkv", KV_WIDTH, 1024, 1024)], axis=0)
    dx2 = _matmul(dkv, w_qkv, "nn", F32, "att_dx_kv", 1024, 1024, KV_WIDTH, b_off=Q_WIDTH // KV_WIDTH)
    dz2, dg01, db01 = _matmul_ln_bwd(dq, w_qkv, z2, lng[0, 1], dz3, "att_dx_q_ln_bwd", 512, res=dx2)
    dz1, dg00, db00, d_wup0, d_fcw0, d_fcb0, (p_wdown0, p_wup1b, p_wqkv, p_wo) = _ffn_backward(
        dz2, x1, z1, lng[0, 0], hf0, cg0, cv0, u0, w_up0, fcw[0], w_down0, 0, exchange=[(d_wup1, half, half)],
        exchange_late=[d_wqkv.reshape(N_DEV, -1, D_MODEL), d_wo.reshape(N_DEV, -1, D_MODEL)])
    dcat = _matmul(dz1, w_out, "nt", BF16, "mix_dcat", 1024, 1024, 1024)
    d_wout = _matmul(cat, dz1, "tn", BF16, "mix_dwout", 1024, 1024, 512)
    (dh0, d_acw, d_acb, d_ang, d_anb, d_bng, d_bnb, d_ms, d_sb), (p_wup0, p_wout) = _mixer_mid_bwd(
        h0, dcat, acw, a_conv_b, a_norm_g, a_norm_b, b_norm_g, b_norm_b, ms, mst, sbt, "mix_mid_bwd",
        comm=_Comm(exchange=[d_wup0, d_wout.reshape(N_DEV, -1, D_MODEL)]))
    d_bqkv = jnp.concatenate([dbq, dbkv], axis=1)
    d_lng = jnp.stack([jnp.stack([dg00, dg01]), jnp.stack([dg10, dg11])])
    d_lnb = jnp.stack([jnp.stack([db00, db01]), jnp.stack([db10, db11])])
    small_full = [d_acb, d_ang, d_anb, d_bng, d_bnb, d_ms, d_sb, dsinks[:, :N_Q_HEADS], jnp.concatenate([d_fcb0, d_fcb1], axis=0),
                  d_acw, d_bqkv, jnp.stack([d_fcw0, d_fcw1]), d_lng, d_lnb, loss_terms]
    d_win, (g_small_grads,) = _matmul(xt, dh0, "tn", BF16, "mix_dwin", 1024, 1024, 512, comm=_Comm(gather=[_pack(small_full)]))
    grad_x, (p_win,) = _matmul(dh0, w_in, "nt", F32, "mix_dx", 1024, 1024, 1024, res=dz1, res_scale=ALPHA,
                               comm=_Comm(exchange=[_deinterleave(d_win)]))


    big = {}
    for nm, p, w, m, v, tr, transposed in [
            ("ab_w_in", [p_win], ab_w_in, m_ab_w_in, v_ab_w_in, 256, False),
            ("ab_w_out", [p_wout], ab_w_out, m_ab_w_out, v_ab_w_out, 128, False),
            ("c_w_qkv", [p_wqkv], c_w_qkv, m_c_w_qkv, v_c_w_qkv, 160, True), ("c_w_o", [p_wo], c_w_o, m_c_w_o, v_c_w_o, 128, False),
            ("ffn_w_up", [p_wup0, (p_wup1a, 0, half), (p_wup1b, half, half)], ffn_w_up, m_ffn_w_up, v_ffn_w_up, 176, True),
            ("ffn_w_down", [p_wdown0, p_wdown1], ffn_w_down, m_ffn_w_down, v_ffn_w_down, 176, False)]:
        def two_d(a):
            a = jnp.swapaxes(a, 1, 2) if transposed else a
            return a.reshape(-1, a.shape[-1])

        def back(o):
            return jnp.swapaxes(o.reshape(w.shape[0], w.shape[2], w.shape[1]), 1, 2) if transposed else o.reshape(w.shape)

        outs = _adamw_big(p, two_d(w), two_d(m), two_d(v), "adamw_" + nm, tr)
        big[nm] = [back(o) for o in outs]

    *gs, loss_parts = _unpack(g_small_grads, [a.shape for a in small_full])
    loss = 0.5 / D_MODEL * jnp.sum(loss_parts)

    def my_shard(g, width):
        g = g.reshape(g.shape[:-1] + (N_DEV, width))
        return lax.dynamic_index_in_dim(g, me, axis=g.ndim - 2, keepdims=False)

    small_names = ["a_conv_b", "a_norm_g", "a_norm_b", "b_norm_g", "b_norm_b", "b_spatial_w", "b_spatial_b", "c_sinks", "ffn_conv_b",
                   "a_conv_w", "c_b_qkv", "ffn_conv_w", "ln_g", "ln_b"]
    small_w = [a_conv_b, a_norm_g, a_norm_b, b_norm_g, b_norm_b, b_spatial_w, b_spatial_b, c_sinks, ffn_conv_b,
               a_conv_w, c_b_qkv, ffn_conv_w, ln_g, ln_b]
    small_m = [m_a_conv_b, m_a_norm_g, m_a_norm_b, m_b_norm_g, m_b_norm_b, m_b_spatial_w, m_b_spatial_b, m_c_sinks, m_ffn_conv_b,
               m_a_conv_w, m_c_b_qkv, m_ffn_conv_w, m_ln_g, m_ln_b]
    small_v = [v_a_conv_b, v_a_norm_g, v_a_norm_b, v_b_norm_g, v_b_norm_b, v_b_spatial_w, v_b_spatial_b, v_c_sinks, v_ffn_conv_b,
               v_a_conv_w, v_c_b_qkv, v_ffn_conv_w, v_ln_g, v_ln_b]
    gs[9:] = [my_shard(g, w.shape[-1]) for g, w in zip(gs[9:], small_w[9:])]
    two_d = [(-1, w.shape[-1]) for w in small_w]
    outs = _adamw_small([g.reshape((N_DEV,) + w.reshape(s).shape) for g, w, s in zip(gs, small_w, two_d)],
                        [w.reshape(s) for w, s in zip(small_w, two_d)], [m.reshape(s) for m, s in zip(small_m, two_d)],
                        [v.reshape(s) for v, s in zip(small_v, two_d)], "adamw_small")
    small = {nm: [o.reshape(w.shape) for o in outs[4 * a:4 * a + 4]] for a, (nm, w) in enumerate(zip(small_names, small_w))}

    res = {**big, **small}
    order = ["ab_w_in", "a_conv_w", "a_conv_b", "a_norm_g", "a_norm_b", "b_norm_g", "b_norm_b", "b_spatial_w", "b_spatial_b", "ab_w_out",
             "c_w_qkv", "c_b_qkv", "c_sinks", "c_w_o", "ffn_w_up", "ffn_conv_w", "ffn_conv_b", "ffn_w_down", "ln_g", "ln_b"]
    return (loss, grad_x[None], *[res[nm][0] for nm in order], *[res[nm][1] for nm in order],
            *[res[nm][2] for nm in order], *[res[nm][3] for nm in order])
```

```python
import functools
import math

import jax
import jax.numpy as jnp
from jax import lax
from jax.experimental import pallas as pl
from jax.experimental.pallas import tpu as pltpu

F32 = jnp.float32
BF16 = jnp.bfloat16

N_DEV = 8
D_MODEL = 1024
A_WIDTH = 512
A_KERNEL = 31
B_GROUPS = 4
B_CHUNK = 128
HEAD_DIM = 64
N_Q_HEADS = 16
N_KV_HEADS = 2
ATT_BLOCK = 128
D_FF = 2816
FFN_KERNEL = 3
ALPHA = (2.0 * 2) ** 0.25
LN_EPS = 1e-5
GELU_K = math.sqrt(2.0 / math.pi)
GELU_C = 0.044715
ADAM_LR = 0.001
ADAM_B1 = 0.9
ADAM_B2 = 0.999
ADAM_EPS = 1e-08
ADAM_WD = 0.01
ADAM_STEP = 10
VMEM_LIMIT = 56 * 1024 * 1024
MESH_ID = pl.DeviceIdType.MESH


def _params(*sem):
    return pltpu.CompilerParams(dimension_semantics=sem, vmem_limit_bytes=VMEM_LIMIT)


def _gelu(x):
    t = jnp.tanh(GELU_K * x * (1.0 + GELU_C * x * x))
    return 0.5 * x * (1.0 + t)


def _gelu_and_grad(x):
    x2 = x * x
    t = jnp.tanh(GELU_K * x * (1.0 + GELU_C * x2))
    g = 0.5 * x * (1.0 + t)
    dg = 0.5 * (1.0 + t) + 0.5 * x * (1.0 - t * t) * (GELU_K * (1.0 + 3.0 * GELU_C * x2))
    return g, dg


def _sigmoid(x):
    return 1.0 / (1.0 + jnp.exp(-x))


def _ln_stats(z):
    mu = jnp.mean(z, axis=-1, keepdims=True)
    zc = z - mu
    var = jnp.mean(zc * zc, axis=-1, keepdims=True)
    r = lax.rsqrt(var + LN_EPS)
    return zc * r, r


def _ln_bwd_rows(dn, nh, r):
    return r * (dn - jnp.mean(dn, axis=-1, keepdims=True) - nh * jnp.mean(dn * nh, axis=-1, keepdims=True))


def _colsum(x):
    return jnp.sum(x, axis=0, keepdims=True)


def _dot(a, b, dims):
    return lax.dot_general(a.astype(BF16), b.astype(BF16), (dims, ((), ())), preferred_element_type=F32)


NN = ((1,), (0,))
NT = ((1,), (1,))
TN = ((0,), (0,))


ANY = pl.BlockSpec(memory_space=pl.ANY)
N_RELATIONS = N_DEV - 1


def _my_place():
    return lax.axis_index("x"), lax.axis_index("y"), lax.axis_index("c")


class _Comm:
    def __init__(self, gather=(), exchange=()):
        exchange = [e if isinstance(e, tuple) else (e, 0, e.shape[1]) for e in exchange]
        self.arrs = list(gather) + [e[0] for e in exchange]
        self.n_gather = len(gather)
        self.n = len(self.arrs)
        self.rows = [None] * self.n_gather + [pl.ds(lo, n) for _, lo, n in exchange]

    def out_shape(self):
        return [jax.ShapeDtypeStruct(((N_DEV,) + a.shape) if i < self.n_gather else a.shape, a.dtype)
                for i, a in enumerate(self.arrs)]

    def sems(self):
        return [pltpu.SemaphoreType.DMA((self.n, N_RELATIONS)), pltpu.SemaphoreType.DMA((self.n, N_RELATIONS)),
                pltpu.SemaphoreType.DMA((self.n,))]

    def _gather_copy(self, ins, outs, sems, a, k, place, to, from_input=False):
        px, py, pc = place
        block = outs[a].at[4 * px + 2 * py + pc]
        return pltpu.make_async_remote_copy(
            src_ref=ins[a] if from_input else block, dst_ref=block,
            send_sem=sems[0].at[a, k], recv_sem=sems[1].at[a, k], device_id=to, device_id_type=MESH_ID)

    def _exchange_copy(self, ins, outs, sems, a, k, landing=False):
        x, y, c = _my_place()
        me = 4 * x + 2 * y + c
        peer = (x ^ (k >> 2), y ^ ((k >> 1) & 1), c ^ (k & 1))
        return pltpu.make_async_remote_copy(
            src_ref=ins[a].at[me ^ k, self.rows[a]], dst_ref=outs[a].at[(me ^ k) if landing else me, self.rows[a]],
            send_sem=sems[0].at[a, k - 1], recv_sem=sems[1].at[a, k - 1], device_id=peer, device_id_type=MESH_ID)

    def _local_copy(self, ins, outs, sems, a):
        x, y, c = _my_place()
        me = 4 * x + 2 * y + c
        if a < self.n_gather:
            return pltpu.make_async_copy(ins[a], outs[a].at[me], sems[2].at[a])
        return pltpu.make_async_copy(ins[a].at[me, self.rows[a]], outs[a].at[me, self.rows[a]], sems[2].at[a])

    def _first_stage(self, ins, outs, sems, a):
        x, y, c = _my_place()
        me = (x, y, c)
        chips = [(1 - x, y), (x, 1 - y), (1 - x, 1 - y)]
        return ([self._gather_copy(ins, outs, sems, a, 0, me, (x, y, 1 - c), from_input=True)]
                + [self._gather_copy(ins, outs, sems, a, 1 + j, me, (*chip, c), from_input=True) for j, chip in enumerate(chips)])

    def start(self, ins, outs, sems):
        for a in range(self.n):
            self._local_copy(ins, outs, sems, a).start()
        for a in range(self.n_gather):
            for cp in self._first_stage(ins, outs, sems, a):
                cp.start()
        for k in range(1, N_DEV):
            for a in range(self.n_gather, self.n):
                self._exchange_copy(ins, outs, sems, a, k).start()

    def forward(self, ins, outs, sems):
        x, y, c = _my_place()
        me, sibling = (x, y, c), (x, y, 1 - c)
        for j, chip in enumerate([(1 - x, y), (x, 1 - y), (1 - x, 1 - y)]):
            for a in range(self.n_gather):
                self._gather_copy(ins, outs, sems, a, 1 + j, (*chip, c), me).wait_recv()
                self._gather_copy(ins, outs, sems, a, 4 + j, (*chip, c), sibling).start()

    def finish(self, ins, outs, sems):
        x, y, c = _my_place()
        me, sibling = (x, y, c), (x, y, 1 - c)
        chips = [(1 - x, y), (x, 1 - y), (1 - x, 1 - y)]
        passed = [self._gather_copy(ins, outs, sems, a, 4 + j, (*chip, c), sibling)
                  for j, chip in enumerate(chips) for a in range(self.n_gather)]
        for a in range(self.n_gather):
            self._gather_copy(ins, outs, sems, a, 0, sibling, me).wait_recv()
            for j, chip in enumerate(chips):
                self._gather_copy(ins, outs, sems, a, 4 + j, (*chip, 1 - c), me).wait_recv()
        for k in range(1, N_DEV):
            for a in range(self.n_gather, self.n):
                self._exchange_copy(ins, outs, sems, a, k, landing=True).wait_recv()
        for a in range(self.n_gather):
            for cp in self._first_stage(ins, outs, sems, a):
                cp.wait_send()
        for cp in passed:
            cp.wait_send()
        for k in range(1, N_DEV):
            for a in range(self.n_gather, self.n):
                self._exchange_copy(ins, outs, sems, a, k).wait_send()
        for a in range(self.n):
            self._local_copy(ins, outs, sems, a).wait()


def _comm_only(comm, name):
    def body(*refs):
        ins, outs, sems = refs[:comm.n], refs[comm.n:2 * comm.n], refs[2 * comm.n:]
        comm.start(ins, outs, sems)
        comm.forward(ins, outs, sems)
        comm.finish(ins, outs, sems)

    return pl.pallas_call(body, name=name, in_specs=[ANY] * comm.n, out_specs=[ANY] * comm.n,
                          out_shape=comm.out_shape(), scratch_shapes=comm.sems())(*comm.arrs)


def _call(body, *, name, grid, in_specs, out_specs, out_shape, args, sem, scratch_shapes=(), comm=None):
    in_specs, out_specs, out_shape, scratch_shapes = list(in_specs), list(out_specs), list(out_shape), list(scratch_shapes)
    if comm is None:
        outs = pl.pallas_call(body, name=name, grid=grid, in_specs=in_specs, out_specs=out_specs, out_shape=out_shape,
                              scratch_shapes=scratch_shapes, compiler_params=_params(*sem))(*args)
        return list(outs), []
    n_in, n_out, n_scr, nc = len(in_specs), len(out_specs), len(scratch_shapes), comm.n

    def wrapped(*refs):
        ins, refs = refs[:n_in], refs[n_in:]
        c_in, refs = refs[:nc], refs[nc:]
        outs, refs = refs[:n_out], refs[n_out:]
        c_out, refs = refs[:nc], refs[nc:]
        scr, sems = refs[:n_scr], refs[n_scr:]
        step = functools.reduce(lambda acc, ax: acc * grid[ax] + pl.program_id(ax), range(len(grid)), 0)
        steps = math.prod(grid)

        @pl.when(step == 0)
        def _():
            comm.start(c_in, c_out, sems)

        @pl.when(step == steps - 1)
        def _():
            comm.forward(c_in, c_out, sems)

        body(*ins, *outs, *scr)

        @pl.when(step == steps - 1)
        def _():
            comm.finish(c_in, c_out, sems)

    outs = pl.pallas_call(
        wrapped, name=name, grid=grid, in_specs=in_specs + [ANY] * nc, out_specs=out_specs + [ANY] * nc,
        out_shape=out_shape + comm.out_shape(), scratch_shapes=scratch_shapes + comm.sems(),
        compiler_params=_params(*(["arbitrary"] * len(grid))))(*args, *comm.arrs)
    return list(outs[:n_out]), list(outs[n_out:])


def _matmul(a, b, mode, out_dtype, name, tm, tn, tk, *, bias=None, res=None, res_scale=1.0, b_off=0, comm=None):
    tm = min(tm, a.shape[1] if mode == "tn" else a.shape[0])
    tk = min(tk, a.shape[0] if mode == "tn" else a.shape[1])
    if mode == "nn":
        (m, k), n = a.shape, b.shape[1]
        a_spec = pl.BlockSpec((tm, tk), lambda i, j, kk: (i, kk))
        b_spec = pl.BlockSpec((tk, tn), lambda i, j, kk: (kk + b_off, j))
        dims = NN
    elif mode == "nt":
        (m, k), n = a.shape, b.shape[0]
        a_spec = pl.BlockSpec((tm, tk), lambda i, j, kk: (i, kk))
        b_spec = pl.BlockSpec((tn, tk), lambda i, j, kk: (j, kk + b_off))
        dims = NT
    else:
        (k, m), n = a.shape, b.shape[1]
        a_spec = pl.BlockSpec((tk, tm), lambda i, j, kk: (kk, i))
        b_spec = pl.BlockSpec((tk, tn), lambda i, j, kk: (kk, j))
        dims = TN
    assert m % tm == 0 and n % tn == 0 and k % tk == 0, (name, m, n, k)
    nk = k // tk
    in_specs = [a_spec, b_spec]
    args = [a, b]
    if bias is not None:
        in_specs.append(pl.BlockSpec((1, tn), lambda i, j, kk: (0, j)))
        args.append(bias)
    if res is not None:
        in_specs.append(pl.BlockSpec((tm, tn), lambda i, j, kk: (i, j)))
        args.append(res)

    def finish(out, refs, o_ref):
        pos = 2
        if bias is not None:
            out = out + refs[pos][...]
            pos += 1
        if res is not None:
            out = out + res_scale * refs[pos][...].astype(F32)
        o_ref[...] = out.astype(out_dtype)

    def body_one_step(*refs):
        finish(_dot(refs[0][...], refs[1][...], dims), refs, refs[-1])

    def body(*refs):
        a_ref, b_ref = refs[0], refs[1]
        o_ref, acc = refs[-2], refs[-1]
        kk = pl.program_id(2)

        @pl.when(kk == 0)
        def _():
            acc[...] = jnp.zeros_like(acc)

        acc[...] += _dot(a_ref[...], b_ref[...], dims)

        @pl.when(kk == nk - 1)
        def _():
            finish(acc[...], refs, o_ref)

    (out,), moved = _call(
        body_one_step if nk == 1 else body, name=name, grid=(m // tm, n // tn, nk),
        in_specs=in_specs, out_specs=[pl.BlockSpec((tm, tn), lambda i, j, kk: (i, j))],
        out_shape=[jax.ShapeDtypeStruct((m, n), out_dtype)],
        scratch_shapes=[] if nk == 1 else [pltpu.VMEM((tm, tn), F32)],
        sem=("parallel", "parallel", "arbitrary"), args=args, comm=comm)
    return out if comm is None else (out, moved)


def _matmul_tn_pair(a0, a1, b, out_dtype, name, tm, tn, tk, comm=None):
    (k, m), n = a0.shape, b.shape[1]
    tk = min(tk, k)
    assert a1.shape == a0.shape and m % tm == 0 and n % tn == 0 and k % tk == 0, (name, m, n, k)
    mi, nk = m // tm, k // tk

    def body(a0_ref, a1_ref, b_ref, o_ref, acc):
        i, kk = pl.program_id(0), pl.program_id(2)

        @pl.when(kk == 0)
        def _():
            acc[...] = jnp.zeros_like(acc)

        @pl.when(i < mi)
        def _():
            acc[...] += _dot(a0_ref[...], b_ref[...], TN)

        @pl.when(i >= mi)
        def _():
            acc[...] += _dot(a1_ref[...], b_ref[...], TN)

        @pl.when(kk == nk - 1)
        def _():
            o_ref[...] = acc[...].astype(out_dtype)

    (out,), moved = _call(
        body, name=name, grid=(2 * mi, n // tn, nk),
        in_specs=[pl.BlockSpec((tk, tm), lambda i, j, kk: (jnp.where(i < mi, kk, nk - 1), jnp.minimum(i, mi - 1))),
                  pl.BlockSpec((tk, tm), lambda i, j, kk: (jnp.where(i >= mi, kk, 0), jnp.maximum(i - mi, 0))),
                  pl.BlockSpec((tk, tn), lambda i, j, kk: (kk, j))],
        out_specs=[pl.BlockSpec((tm, tn), lambda i, j, kk: (i, j))],
        out_shape=[jax.ShapeDtypeStruct((2 * m, n), out_dtype)],
        scratch_shapes=[pltpu.VMEM((tm, tn), F32)],
        sem=("parallel", "parallel", "arbitrary"), args=(a0, a1, b), comm=comm)
    return out if comm is None else (out, moved)


def _residual_input(x_ref, prev_refs):
    if not prev_refs:
        return x_ref[...]
    nh, _ = _ln_stats(x_ref[...])
    return nh * prev_refs[0][...] + prev_refs[1][...]


def _matmul_res_ln(a, b, x, g, beta, name, tm, prev=None, comm=None):
    t, k = a.shape
    d = b.shape[1]
    tm = min(tm, t)
    assert t % tm == 0
    n_prev = 0 if prev is None else 2

    def body(a_ref, b_ref, x_ref, g_ref, beta_ref, *rest):
        z_ref, xo_ref = rest[n_prev:]
        z = ALPHA * _residual_input(x_ref, rest[:n_prev]) + _dot(a_ref[...], b_ref[...], NN)
        nh, _ = _ln_stats(z)
        z_ref[...] = z
        xo_ref[...] = (nh * g_ref[...] + beta_ref[...]).astype(BF16)

    row = pl.BlockSpec((tm, d), lambda i: (i, 0))
    vec = pl.BlockSpec((1, d), lambda i: (0, 0))
    outs, moved = _call(
        body, name=name, grid=(t // tm,),
        in_specs=[pl.BlockSpec((tm, k), lambda i: (i, 0)), pl.BlockSpec((k, d), lambda i: (0, 0)), row, vec, vec] + [vec] * n_prev,
        out_specs=[row, row],
        out_shape=[jax.ShapeDtypeStruct((t, d), F32), jax.ShapeDtypeStruct((t, d), BF16)],
        sem=("parallel",), args=(a, b, x, g, beta, *(prev or ())), comm=comm)
    return outs if comm is None else (outs, moved)


def _matmul_ln_bwd(a, b, z, g, dres, name, tm, *, res=None, b_off=0):
    m, k = a.shape
    d = b.shape[1]
    tm = min(tm, m)
    assert m % tm == 0

    def body(*refs):
        a_ref, b_ref, z_ref, g_ref, dres_ref = refs[:5]
        dz_ref, dg_ref, db_ref = refs[-3:]

        @pl.when(pl.program_id(0) == 0)
        def _():
            dg_ref[...] = jnp.zeros_like(dg_ref)
            db_ref[...] = jnp.zeros_like(db_ref)

        dbr = _dot(a_ref[...], b_ref[...], NN)
        if res is not None:
            dbr = dbr + refs[5][...]
        nh, r = _ln_stats(z_ref[...])
        dy = ALPHA * dres_ref[...] + dbr
        dg_ref[...] += _colsum(dy * nh)
        db_ref[...] += _colsum(dy)
        dz_ref[...] = _ln_bwd_rows(dy * g_ref[...], nh, r)

    row = pl.BlockSpec((tm, d), lambda i: (i, 0))
    vec = pl.BlockSpec((1, d), lambda i: (0, 0))
    vshape = jax.ShapeDtypeStruct((1, d), F32)
    return pl.pallas_call(
        body, name=name, grid=(m // tm,),
        in_specs=[pl.BlockSpec((tm, k), lambda i: (i, 0)), pl.BlockSpec((k, d), lambda i: (b_off, 0)), row, vec, row]
        + ([row] if res is not None else []),
        out_specs=[row, vec, vec], out_shape=[jax.ShapeDtypeStruct((m, d), F32), vshape, vshape],
        compiler_params=_params("arbitrary"),
    )(a, b, z, g, dres, *([res] if res is not None else []))


def _matmul_res_ln_loss(a, b, x, g, beta, target, name, tm, prev):
    t, k = a.shape
    d = b.shape[1]
    tm = min(tm, t)

    def body(a_ref, b_ref, x_ref, g_ref, beta_ref, t_ref, gp_ref, bp_ref, dz_ref, dg_ref, db_ref, loss_ref):
        @pl.when(pl.program_id(0) == 0)
        def _():
            dg_ref[...] = jnp.zeros_like(dg_ref)
            db_ref[...] = jnp.zeros_like(db_ref)
            loss_ref[...] = jnp.zeros_like(loss_ref)

        nh, r = _ln_stats(ALPHA * _residual_input(x_ref, (gp_ref, bp_ref)) + _dot(a_ref[...], b_ref[...], NN))
        err = nh * g_ref[...] + beta_ref[...] - t_ref[...]
        loss_ref[...] += _colsum(err * err)
        dy = err * (1.0 / d)
        dg_ref[...] += _colsum(dy * nh)
        db_ref[...] += _colsum(dy)
        dz_ref[...] = _ln_bwd_rows(dy * g_ref[...], nh, r)

    row = pl.BlockSpec((tm, d), lambda i: (i, 0))
    vec = pl.BlockSpec((1, d), lambda i: (0, 0))
    vshape = jax.ShapeDtypeStruct((1, d), F32)
    return pl.pallas_call(
        body, name=name, grid=(t // tm,),
        in_specs=[pl.BlockSpec((tm, k), lambda i: (i, 0)), pl.BlockSpec((k, d), lambda i: (0, 0)), row, vec, vec, row, vec, vec],
        out_specs=[row, vec, vec, vec],
        out_shape=[jax.ShapeDtypeStruct((t, d), F32), vshape, vshape, vshape],
        compiler_params=_params("arbitrary"),
    )(a, b, x, g, beta, target, *prev)


FFN_HALO = 16
FFN_CHUNK = 256
LANES = 128
SUBLANES = 8


def _rows_up(e, start, rows):
    if start % SUBLANES == 0:
        return e[start:start + rows]
    return pltpu.roll(e, e.shape[0] - start, 0)[0:rows]


def _fold(x):
    return jnp.sum(x.reshape(x.shape[0] // SUBLANES, SUBLANES, x.shape[1]), axis=0)


def _ffn_mid_fwd(h, cw, cb, name, tm=1024, tc=1408, comm=None):
    t, f2 = h.shape
    tm = min(tm, t)
    f = f2 // 2
    nj, nt, hb = f // tc, t // tm, tm // FFN_HALO

    ch = min(FFN_CHUNK, tm)

    def body(hg, hgp, hv, hvp, cwg, cwv, cbg, cbv, u_ref, cg_ref, cv_ref):
        i = pl.program_id(1)
        o = FFN_HALO - FFN_KERNEL + 1
        for lg in range(tc // LANES):
            cols = slice(lg * LANES, (lg + 1) * LANES)
            wg, wv = [cwg[k:k + 1, cols] for k in range(FFN_KERNEL)], [cwv[k:k + 1, cols] for k in range(FFN_KERNEL)]
            bg, bv = cbg[:, cols], cbv[:, cols]

            def emit(base, eg, ev):
                cg = wg[0] * _rows_up(eg, o, ch) + wg[1] * _rows_up(eg, o + 1, ch) + wg[2] * _rows_up(eg, o + 2, ch) + bg
                cv = wv[0] * _rows_up(ev, o, ch) + wv[1] * _rows_up(ev, o + 1, ch) + wv[2] * _rows_up(ev, o + 2, ch) + bv
                u_ref[pl.ds(base, ch), cols] = (_gelu(cg) * cv).astype(BF16)
                cg_ref[pl.ds(base, ch), cols] = cg.astype(BF16)
                cv_ref[pl.ds(base, ch), cols] = cv.astype(BF16)

            def first(main, prev):
                return jnp.concatenate([jnp.where(i > 0, prev[:, cols].astype(F32), 0.0), main[0:ch, cols].astype(F32)], axis=0)

            def inner(c, carry):
                base = pl.multiple_of(c * ch, ch)
                emit(base, hg[pl.ds(base - FFN_HALO, ch + FFN_HALO), cols].astype(F32),
                     hv[pl.ds(base - FFN_HALO, ch + FFN_HALO), cols].astype(F32))
                return carry

            emit(0, first(hg, hgp), first(hv, hvp))
            if tm > ch:
                lax.fori_loop(1, tm // ch, inner, 0)

    def main_spec(off):
        return pl.BlockSpec((tm, tc), lambda j, i: (i, j + off))

    def prev_spec(off):
        return pl.BlockSpec((FFN_HALO, tc), lambda j, i: (jnp.maximum(i * hb - 1, 0), j + off))

    def par_spec(rows, off):
        return pl.BlockSpec((rows, tc), lambda j, i: (0, j + off))

    outs, moved = _call(
        body, name=name, grid=(nj, nt),
        in_specs=[main_spec(0), prev_spec(0), main_spec(nj), prev_spec(nj),
                  par_spec(FFN_KERNEL, 0), par_spec(FFN_KERNEL, nj), par_spec(1, 0), par_spec(1, nj)],
        out_specs=[pl.BlockSpec((tm, tc), lambda j, i: (i, j))] * 3,
        out_shape=[jax.ShapeDtypeStruct((t, f), BF16)] * 3,
        sem=("parallel", "arbitrary"), args=(h, h, h, h, cw, cw, cb, cb), comm=comm)
    return outs if comm is None else (outs, moved)


def _ffn_mid_bwd(h, cg, cv, du, cw, name, tm=1024, tc=1408, comm=None):
    t, f2 = h.shape
    tm = min(tm, t)
    f = f2 // 2
    nj, nt, hb = f // tc, t // tm, tm // FFN_HALO

    ch = min(FFN_CHUNK, tm)
    ahead = ch + SUBLANES
    n_ch = tm // ch

    def body(hg, hv, cg_ref, cgn_ref, cv_ref, cvn_ref, du_ref, dun_ref, cwg, cwv,
             dhg_ref, dhv_ref, dcwg_ref, dcwv_ref, dcbg_ref, dcbv_ref):
        i = pl.program_id(1)

        @pl.when(i == 0)
        def _():
            for ref in (dcwg_ref, dcwv_ref, dcbg_ref, dcbv_ref):
                ref[...] = jnp.zeros_like(ref)

        for lg in range(tc // LANES):
            cols = slice(lg * LANES, (lg + 1) * LANES)
            wg, wv = [cwg[k:k + 1, cols] for k in range(FFN_KERNEL)], [cwv[k:k + 1, cols] for k in range(FFN_KERNEL)]

            def emit(base, cg_e, cv_e, du_e, acc):
                cg_a, cv_a, du_a = cg_e[0:ahead], cv_e[0:ahead], du_e[0:ahead]
                gl, dgl = _gelu_and_grad(cg_a)

                def back(d, h_ref, w, dh_ref):
                    later = [d[0:ch], _rows_up(d, 1, ch), _rows_up(d, 2, ch)]
                    dh_ref[pl.ds(base, ch), cols] = (w[2] * later[0] + w[1] * later[1] + w[0] * later[2]).astype(BF16)
                    h_own = h_ref[pl.ds(base, ch), cols].astype(F32)
                    return [_fold(later[0])] + [_fold(later[FFN_KERNEL - 1 - k] * h_own) for k in range(FFN_KERNEL)]

                sums = back(du_a * cv_a * dgl, hg, wg, dhg_ref) + back(du_a * gl, hv, wv, dhv_ref)
                return tuple(a + s_ for a, s_ in zip(acc, sums))

            def inner(c, acc):
                base = pl.multiple_of(c * ch, ch)
                rows = pl.ds(base, ch + FFN_HALO)
                return emit(base, cg_ref[rows, cols].astype(F32), cv_ref[rows, cols].astype(F32), du_ref[rows, cols].astype(F32), acc)

            def last(acc):
                def rows(main, after):
                    return jnp.concatenate([main[tm - ch:tm, cols].astype(F32), after], axis=0)

                du_next = jnp.where(i < nt - 1, dun_ref[:, cols].astype(F32), 0.0)
                return emit(tm - ch, rows(cg_ref, cgn_ref[:, cols].astype(F32)), rows(cv_ref, cvn_ref[:, cols].astype(F32)),
                            rows(du_ref, du_next), acc)

            acc = (jnp.zeros((SUBLANES, LANES), F32),) * (2 * (1 + FFN_KERNEL))
            if n_ch > 1:
                acc = lax.fori_loop(0, n_ch - 1, inner, acc)
            acc = last(acc)
            dcbg_ref[:, cols] += _colsum(acc[0])
            dcbv_ref[:, cols] += _colsum(acc[1 + FFN_KERNEL])
            for k in range(FFN_KERNEL):
                dcwg_ref[k:k + 1, cols] += _colsum(acc[1 + k])
                dcwv_ref[k:k + 1, cols] += _colsum(acc[2 + FFN_KERNEL + k])

    last_blk = t // FFN_HALO - 1

    def main_spec(off):
        return pl.BlockSpec((tm, tc), lambda j, i: (i, j + off))

    def next_spec(off):
        return pl.BlockSpec((FFN_HALO, tc), lambda j, i: (jnp.minimum((i + 1) * hb, last_blk), j + off))

    def par_spec(rows, off):
        return pl.BlockSpec((rows, tc), lambda j, i: (0, j + off))

    out_tile = pl.BlockSpec((tm, tc), lambda j, i: (i, j))
    outs, moved = _call(
        body, name=name, grid=(nj, nt),
        in_specs=[main_spec(0), main_spec(nj), main_spec(0), next_spec(0), main_spec(0), next_spec(0), main_spec(0), next_spec(0),
                  par_spec(FFN_KERNEL, 0), par_spec(FFN_KERNEL, nj)],
        out_specs=[out_tile, out_tile, par_spec(FFN_KERNEL, 0), par_spec(FFN_KERNEL, 0), par_spec(1, 0), par_spec(1, 0)],
        out_shape=[jax.ShapeDtypeStruct((t, f), BF16), jax.ShapeDtypeStruct((t, f), BF16),
                   jax.ShapeDtypeStruct((FFN_KERNEL, f), F32), jax.ShapeDtypeStruct((FFN_KERNEL, f), F32),
                   jax.ShapeDtypeStruct((1, f), F32), jax.ShapeDtypeStruct((1, f), F32)],
        sem=("parallel", "arbitrary"), args=(h, h, cg, cg, cv, cv, du, du, cw, cw), comm=comm)
    return outs if comm is None else (outs, moved)


MIX_HALO = 32


def _glu(hh):
    return hh[:, 0:A_WIDTH] * _sigmoid(hh[:, A_WIDTH:2 * A_WIDTH])


def _fill_row_shifts(s):
    rows = s.shape[1] - SUBLANES
    for j in range(1, SUBLANES):
        s[j, 0:rows, :] = s[0, pl.ds(j, rows), :]


def _rows_from(s, start, rows):
    j = start % SUBLANES
    return s[j, start - j:start - j + rows, :]


def _tril_mask():
    return lax.broadcasted_iota(jnp.int32, (B_CHUNK, B_CHUNK), 0) >= lax.broadcasted_iota(jnp.int32, (B_CHUNK, B_CHUNK), 1)


def _spatial_mix(q, ms_ref, sbt_ref, tm):
    mask = _tril_mask()
    ws = [jnp.where(mask, ms_ref[g], 0.0).astype(BF16) for g in range(B_GROUPS)]
    qb = q.astype(BF16)
    rows = []
    for c in range(tm // B_CHUNK):
        cols = [_dot(ws[g], qb[c * B_CHUNK:(c + 1) * B_CHUNK, g * 128:(g + 1) * 128], NN) + sbt_ref[:, g:g + 1]
                for g in range(B_GROUPS)]
        rows.append(jnp.concatenate(cols, axis=1))
    return jnp.concatenate(rows, axis=0)


def _mixer_mid_fwd(h, cw, cb, ag, ab, bg, bb, ms, sbt, name, tm=256, comm=None):
    t = h.shape[0]
    nt, hb = t // tm, tm // MIX_HALO
    o = MIX_HALO - A_KERNEL + 1

    def body(h_ref, hp_ref, cw_ref, cb_ref, ag_ref, ab_ref, bg_ref, bb_ref, ms_ref, sbt_ref, cat_ref, sp):
        i = pl.program_id(0)
        sp[0, 0:MIX_HALO, :] = jnp.where(i > 0, _glu(hp_ref[:, 0:2 * A_WIDTH].astype(F32)), 0.0)
        sp[0, MIX_HALO:, :] = _glu(h_ref[:, 0:2 * A_WIDTH].astype(F32))
        _fill_row_shifts(sp)
        y = jnp.zeros((tm, A_WIDTH), F32) + cb_ref[...]
        for k in range(A_KERNEL):
            y = y + cw_ref[k:k + 1, :] * _rows_from(sp, o + k, tm)
        nh, _ = _ln_stats(y)
        ln = nh * ag_ref[...] + ab_ref[...]
        cat_ref[:, 0:A_WIDTH] = (ln * _sigmoid(ln)).astype(BF16)
        u = _gelu(h_ref[:, 1024:1536].astype(F32))
        nb, _ = _ln_stats(_gelu(h_ref[:, 1536:2048].astype(F32)))
        mixed = _spatial_mix(nb * bg_ref[...] + bb_ref[...], ms_ref, sbt_ref, tm)
        cat_ref[:, A_WIDTH:] = (u * mixed).astype(BF16)

    vec = pl.BlockSpec((1, A_WIDTH), lambda i: (0, 0))
    (cat,), moved = _call(
        body, name=name, grid=(nt,),
        in_specs=[pl.BlockSpec((tm, 2048), lambda i: (i, 0)),
                  pl.BlockSpec((MIX_HALO, 2048), lambda i: (jnp.maximum(i * hb - 1, 0), 0)),
                  pl.BlockSpec((A_KERNEL, A_WIDTH), lambda i: (0, 0)), vec, vec, vec, vec, vec,
                  pl.BlockSpec((B_GROUPS, B_CHUNK, B_CHUNK), lambda i: (0, 0, 0)),
                  pl.BlockSpec((B_CHUNK, B_GROUPS), lambda i: (0, 0))],
        out_specs=[pl.BlockSpec((tm, D_MODEL), lambda i: (i, 0))],
        out_shape=[jax.ShapeDtypeStruct((t, D_MODEL), BF16)],
        scratch_shapes=[pltpu.VMEM((SUBLANES, tm + MIX_HALO, A_WIDTH), F32)],
        sem=("parallel",), args=(h, h, cw, cb, ag, ab, bg, bb, ms, sbt), comm=comm)
    return cat if comm is None else (cat, moved)


def _mixer_mid_bwd(h, dcat, cw, cb, ag, ab, bg, bb, ms, mst, sbt, name, tm=256, comm=None):
    t = h.shape[0]
    nt, hb = t // tm, tm // MIX_HALO
    o = MIX_HALO - A_KERNEL + 1
    r = tm + MIX_HALO
    nchunk = tm // B_CHUNK

    def body(h_ref, hp_ref, hn_ref, dc_ref, dcn_ref, cw_ref, cb_ref, ag_ref, ab_ref, bg_ref, bb_ref, ms_ref, mst_ref, sbt_ref,
             dh_ref, dcw_ref, dcb_ref, dag_ref, dab_ref, dbg_ref, dbb_ref, dms_ref, dsb_ref, sp, sdy, sbacc):
        i = pl.program_id(0)

        @pl.when(i == 0)
        def _():
            for ref in (dcw_ref, dcb_ref, dag_ref, dab_ref, dbg_ref, dbb_ref, dms_ref, dsb_ref, sbacc):
                ref[...] = jnp.zeros_like(ref)

        sp[0, 0:MIX_HALO, :] = jnp.where(i > 0, _glu(hp_ref[:, 0:2 * A_WIDTH].astype(F32)), 0.0)
        sp[0, MIX_HALO:MIX_HALO + tm, :] = _glu(h_ref[:, 0:2 * A_WIDTH].astype(F32))
        sp[0, MIX_HALO + tm:, :] = _glu(hn_ref[:, 0:2 * A_WIDTH].astype(F32))
        _fill_row_shifts(sp)
        y = jnp.zeros((r, A_WIDTH), F32) + cb_ref[...]
        for k in range(A_KERNEL):
            y = y + cw_ref[k:k + 1, :] * _rows_from(sp, o + k, r)
        nh, rs = _ln_stats(y)
        ln = nh * ag_ref[...] + ab_ref[...]
        sg = _sigmoid(ln)
        dao = jnp.concatenate([dc_ref[:, 0:A_WIDTH].astype(F32),
                               jnp.where(i < nt - 1, dcn_ref[:, 0:A_WIDTH].astype(F32), 0.0)], axis=0)
        dln = dao * (sg * (1.0 + ln * (1.0 - sg)))
        dag_ref[...] += _colsum(dln[0:tm] * nh[0:tm])
        dab_ref[...] += _colsum(dln[0:tm])
        sdy[0] = _ln_bwd_rows(dln * ag_ref[...], nh, rs)
        _fill_row_shifts(sdy)
        dy_own = sdy[0, 0:tm, :]
        dcb_ref[...] += _colsum(dy_own)
        dp = jnp.zeros((tm, A_WIDTH), F32)
        for k in range(A_KERNEL):
            dcw_ref[k:k + 1, :] += _colsum(dy_own * _rows_from(sp, o + k, tm))
            dp = dp + cw_ref[k:k + 1, :] * _rows_from(sdy, A_KERNEL - 1 - k, tm)
        av = h_ref[:, 0:A_WIDTH].astype(F32)
        s = _sigmoid(h_ref[:, A_WIDTH:2 * A_WIDTH].astype(F32))
        dh_ref[:, 0:A_WIDTH] = (dp * s).astype(BF16)
        dh_ref[:, A_WIDTH:2 * A_WIDTH] = (dp * av * s * (1.0 - s)).astype(BF16)

        u, dgu = _gelu_and_grad(h_ref[:, 1024:1536].astype(F32))
        w, dgw = _gelu_and_grad(h_ref[:, 1536:2048].astype(F32))
        nb, rb = _ln_stats(w)
        q = nb * bg_ref[...] + bb_ref[...]
        mixed = _spatial_mix(q, ms_ref, sbt_ref, tm)
        dbo = dc_ref[:, A_WIDTH:].astype(F32)
        dh_ref[:, 1024:1536] = (dbo * mixed * dgu).astype(BF16)
        dmx = dbo * u
        mask = _tril_mask()
        wst = [jnp.where(mask.T, mst_ref[g], 0.0).astype(BF16) for g in range(B_GROUPS)]
        qb = q.astype(BF16)
        dmb = dmx.astype(BF16)
        rows = []
        for c in range(nchunk):
            cols = []
            for g in range(B_GROUPS):
                rs_, cs_ = slice(c * B_CHUNK, (c + 1) * B_CHUNK), slice(g * 128, (g + 1) * 128)
                sbacc[g] += dmx[rs_, cs_]
                dms_ref[g] += _dot(dmb[rs_, cs_], qb[rs_, cs_], NT)
                cols.append(_dot(wst[g], dmb[rs_, cs_], NN))
            rows.append(jnp.concatenate(cols, axis=1))
        dq = jnp.concatenate(rows, axis=0)
        dbg_ref[...] += _colsum(dq * nb)
        dbb_ref[...] += _colsum(dq)
        dh_ref[:, 1536:2048] = (_ln_bwd_rows(dq * bg_ref[...], nb, rb) * dgw).astype(BF16)

        @pl.when(i == nt - 1)
        def _():
            for g in range(B_GROUPS):
                dms_ref[g] = jnp.where(mask, dms_ref[g], 0.0)
                dsb_ref[g] = jnp.sum(sbacc[g], axis=1, keepdims=True)

    last_blk = t // MIX_HALO - 1
    vec = pl.BlockSpec((1, A_WIDTH), lambda i: (0, 0))
    mat = pl.BlockSpec((B_GROUPS, B_CHUNK, B_CHUNK), lambda i: (0, 0, 0))
    taps = pl.BlockSpec((A_KERNEL, A_WIDTH), lambda i: (0, 0))

    def halo(width, which):
        if which == "prev":
            return pl.BlockSpec((MIX_HALO, width), lambda i: (jnp.maximum(i * hb - 1, 0), 0))
        return pl.BlockSpec((MIX_HALO, width), lambda i: (jnp.minimum((i + 1) * hb, last_blk), 0))

    vshape = jax.ShapeDtypeStruct((1, A_WIDTH), F32)
    outs, moved = _call(
        body, name=name, grid=(nt,),
        in_specs=[pl.BlockSpec((tm, 2048), lambda i: (i, 0)), halo(2048, "prev"), halo(2048, "next"),
                  pl.BlockSpec((tm, D_MODEL), lambda i: (i, 0)), halo(D_MODEL, "next"),
                  taps, vec, vec, vec, vec, vec, mat, mat, pl.BlockSpec((B_CHUNK, B_GROUPS), lambda i: (0, 0))],
        out_specs=[pl.BlockSpec((tm, 2048), lambda i: (i, 0)), taps, vec, vec, vec, vec, vec, mat,
                   pl.BlockSpec((B_GROUPS, B_CHUNK, 1), lambda i: (0, 0, 0))],
        out_shape=[jax.ShapeDtypeStruct((t, 2048), BF16), jax.ShapeDtypeStruct((A_KERNEL, A_WIDTH), F32),
                   vshape, vshape, vshape, vshape, vshape,
                   jax.ShapeDtypeStruct((B_GROUPS, B_CHUNK, B_CHUNK), F32), jax.ShapeDtypeStruct((B_GROUPS, B_CHUNK, 1), F32)],
        scratch_shapes=[pltpu.VMEM((SUBLANES, tm + 2 * MIX_HALO, A_WIDTH), F32), pltpu.VMEM((SUBLANES, r, A_WIDTH), F32),
                        pltpu.VMEM((B_GROUPS, B_CHUNK, B_CHUNK), F32)],
        sem=("arbitrary",), args=(h, h, h, dcat, dcat, cw, cb, ag, ab, bg, bb, ms, mst, sbt), comm=comm)
    return outs if comm is None else (outs, moved)


Q_WIDTH = N_Q_HEADS * HEAD_DIM
KV_WIDTH = 2 * N_KV_HEADS * HEAD_DIM
PAIRS_PER_KV = N_Q_HEADS // N_KV_HEADS // 2
ATT_SCALE = 1.0 / math.sqrt(HEAD_DIM)


def _dup_heads(pair_cols, kv_head):
    lane = lax.broadcasted_iota(jnp.int32, pair_cols.shape, 1)
    rolled = pltpu.roll(pair_cols, HEAD_DIM, 1)
    first = lane < HEAD_DIM
    return jnp.where(first, pair_cols, rolled) if kv_head == 0 else jnp.where(first, rolled, pair_cols)


HEADS_PER_KV = N_Q_HEADS // N_KV_HEADS


def _stack_heads(ref, kh):
    lane = lax.broadcasted_iota(jnp.int32, (ATT_BLOCK, 128), 1)
    rows = []
    for pr in range(PAIRS_PER_KV):
        c0 = (kh * PAIRS_PER_KV + pr) * 128
        pair = ref[:, c0:c0 + 128]
        rows += [jnp.where(lane < HEAD_DIM, pair, jnp.zeros_like(pair)), jnp.where(lane < HEAD_DIM, jnp.zeros_like(pair), pair)]
    return jnp.concatenate(rows, axis=0)


def _unstack_heads(stacked, kh, write):
    lane = lax.broadcasted_iota(jnp.int32, (ATT_BLOCK, 128), 1)
    for pr in range(PAIRS_PER_KV):
        first = stacked[(2 * pr) * ATT_BLOCK:(2 * pr + 1) * ATT_BLOCK]
        second = stacked[(2 * pr + 1) * ATT_BLOCK:(2 * pr + 2) * ATT_BLOCK]
        write((kh * PAIRS_PER_KV + pr) * 128, jnp.where(lane < HEAD_DIM, first, second))


def _sink_row(sink_ref, kh):
    return jnp.concatenate([jnp.full((1, ATT_BLOCK), sink_ref[0, kh * HEADS_PER_KV + h], F32) for h in range(HEADS_PER_KV)], axis=1)


def _att_mask_t(n):
    sj = lax.broadcasted_iota(jnp.int32, (2 * ATT_BLOCK, HEADS_PER_KV * ATT_BLOCK), 0)
    qi = lax.broadcasted_iota(jnp.int32, (2 * ATT_BLOCK, HEADS_PER_KV * ATT_BLOCK), 1) & (ATT_BLOCK - 1)
    diff = qi + ATT_BLOCK - sj
    return (diff >= 0) & (diff < ATT_BLOCK) & ((n > 0) | (sj >= ATT_BLOCK))


def _att_probs_t(q_all, k2, mask_t, sink):
    st = _dot(k2, q_all, NT) * ATT_SCALE
    st = jnp.where(mask_t, st, -jnp.inf)
    m = jnp.maximum(jnp.max(st, axis=0, keepdims=True), sink)
    e = jnp.exp(st - m)
    es = jnp.exp(sink - m)
    inv = 1.0 / (jnp.sum(e, axis=0, keepdims=True) + es)
    return e * inv, es * inv


def _attn_fwd(qkv, sinks, name, comm=None):
    t = qkv.shape[0]
    nb = t // ATT_BLOCK
    kvb = Q_WIDTH // KV_WIDTH

    def body(sink_ref, q_ref, kv_ref, kvp_ref, o_ref):
        n = pl.program_id(0)
        mask_t = _att_mask_t(n)
        kv = jnp.concatenate([kvp_ref[...], kv_ref[...]], axis=0).astype(F32)

        def write(c0, pair):
            o_ref[:, c0:c0 + 128] = pair.astype(BF16)

        for kh in range(N_KV_HEADS):
            k2 = _dup_heads(kv[:, 0:128], kh).astype(BF16)
            v2 = _dup_heads(kv[:, 128:256], kh).astype(BF16)
            pt, _ = _att_probs_t(_stack_heads(q_ref, kh), k2, mask_t, _sink_row(sink_ref, kh))
            _unstack_heads(_dot(v2, pt, TN).T, kh, write)

    (out,), moved = _call(
        body, name=name, grid=(nb,),
        in_specs=[pl.BlockSpec(memory_space=pltpu.SMEM),
                  pl.BlockSpec((ATT_BLOCK, Q_WIDTH), lambda n: (n, 0)),
                  pl.BlockSpec((ATT_BLOCK, KV_WIDTH), lambda n: (n, kvb)),
                  pl.BlockSpec((ATT_BLOCK, KV_WIDTH), lambda n: (jnp.maximum(n - 1, 0), kvb))],
        out_specs=[pl.BlockSpec((ATT_BLOCK, Q_WIDTH), lambda n: (n, 0))],
        out_shape=[jax.ShapeDtypeStruct((t, Q_WIDTH), BF16)],
        sem=("parallel",), args=(sinks, qkv, qkv, qkv), comm=comm)
    return out if comm is None else (out, moved)


def _attn_bwd(qkv, d_o, sinks, name, comm=None):
    t = qkv.shape[0]
    nb = t // ATT_BLOCK
    kvb = Q_WIDTH // KV_WIDTH

    def body(sink_ref, q_ref, kv_ref, kvp_ref, do_ref, dq_ref, dkv_ref, dbq_ref, dbkv_ref, dsink_ref, carry):
        n = pl.program_id(0)

        @pl.when(n == 0)
        def _():
            for ref in (dbq_ref, dbkv_ref, dsink_ref, carry):
                ref[...] = jnp.zeros_like(ref)
            dkv_ref[...] = jnp.zeros_like(dkv_ref)

        @pl.when(n < nb)
        def _():
            mask_t = _att_mask_t(n)
            kv = jnp.concatenate([kvp_ref[...], kv_ref[...]], axis=0).astype(F32)
            lane2 = lax.broadcasted_iota(jnp.int32, (2 * ATT_BLOCK, 128), 1)
            sink_lane = lax.broadcasted_iota(jnp.int32, (1, 128), 1)
            dsink = jnp.zeros((1, 128), F32)
            dk_parts, dv_parts = [], []

            def write(c0, pair):
                dbq_ref[:, c0:c0 + 128] += _colsum(pair)
                dq_ref[:, c0:c0 + 128] = pair.astype(BF16)

            for kh in range(N_KV_HEADS):
                k2 = _dup_heads(kv[:, 0:128], kh).astype(BF16)
                v2 = _dup_heads(kv[:, 128:256], kh).astype(BF16)
                q_all = _stack_heads(q_ref, kh)
                do_all = _stack_heads(do_ref, kh)
                pt, ps = _att_probs_t(q_all, k2, mask_t, _sink_row(sink_ref, kh))
                dpt = _dot(v2, do_all, NT)
                delta = jnp.sum(pt * dpt, axis=0, keepdims=True)
                dst = pt * (dpt - delta) * ATT_SCALE
                psd = ps * delta
                for h in range(HEADS_PER_KV):
                    dsink = dsink + jnp.where(sink_lane == kh * HEADS_PER_KV + h,
                                              -jnp.sum(psd[:, h * ATT_BLOCK:(h + 1) * ATT_BLOCK]), 0.0)
                _unstack_heads(_dot(k2, dst, TN).T, kh, write)
                dk_acc = _dot(dst, q_all, NN)
                dv_acc = _dot(pt, do_all, NN)
                dk_parts.append(dk_acc + pltpu.roll(dk_acc, HEAD_DIM, 1))
                dv_parts.append(dv_acc + pltpu.roll(dv_acc, HEAD_DIM, 1))
            dk = jnp.where(lane2 < HEAD_DIM, dk_parts[0], dk_parts[1])
            dv = jnp.where(lane2 < HEAD_DIM, dv_parts[0], dv_parts[1])
            dkv_new = jnp.concatenate([dk, dv], axis=1)
            done = carry[...] + dkv_new[0:ATT_BLOCK]

            @pl.when(n > 0)
            def _():
                dkv_ref[...] = done.astype(BF16)
                dbkv_ref[...] += _colsum(done)

            carry[...] = dkv_new[ATT_BLOCK:]
            dsink_ref[...] += dsink

        @pl.when(n == nb)
        def _():
            dkv_ref[...] = carry[...].astype(BF16)
            dbkv_ref[...] += _colsum(carry[...])

    def clamp(n):
        return jnp.minimum(n, nb - 1)

    outs, moved = _call(
        body, name=name, grid=(nb + 1,),
        in_specs=[pl.BlockSpec(memory_space=pltpu.SMEM),
                  pl.BlockSpec((ATT_BLOCK, Q_WIDTH), lambda n: (clamp(n), 0)),
                  pl.BlockSpec((ATT_BLOCK, KV_WIDTH), lambda n: (clamp(n), kvb)),
                  pl.BlockSpec((ATT_BLOCK, KV_WIDTH), lambda n: (jnp.maximum(clamp(n) - 1, 0), kvb)),
                  pl.BlockSpec((ATT_BLOCK, Q_WIDTH), lambda n: (clamp(n), 0))],
        out_specs=[pl.BlockSpec((ATT_BLOCK, Q_WIDTH), lambda n: (clamp(n), 0)),
                   pl.BlockSpec((ATT_BLOCK, KV_WIDTH), lambda n: (jnp.maximum(n - 1, 0), 0)),
                   pl.BlockSpec((1, Q_WIDTH), lambda n: (0, 0)),
                   pl.BlockSpec((1, KV_WIDTH), lambda n: (0, 0)),
                   pl.BlockSpec((1, 128), lambda n: (0, 0))],
        out_shape=[jax.ShapeDtypeStruct((t, Q_WIDTH), BF16), jax.ShapeDtypeStruct((t, KV_WIDTH), BF16),
                   jax.ShapeDtypeStruct((1, Q_WIDTH), F32), jax.ShapeDtypeStruct((1, KV_WIDTH), F32),
                   jax.ShapeDtypeStruct((1, 128), F32)],
        scratch_shapes=[pltpu.VMEM((ATT_BLOCK, KV_WIDTH), F32)],
        sem=("arbitrary",), args=(sinks, qkv, qkv, qkv, d_o), comm=comm)
    return outs if comm is None else (outs, moved)


def _adamw_math(g, w, m, v):
    m = ADAM_B1 * m + (1.0 - ADAM_B1) * g
    v = ADAM_B2 * v + (1.0 - ADAM_B2) * (g * g)
    m_hat = m / (1.0 - ADAM_B1 ** ADAM_STEP)
    v_hat = v / (1.0 - ADAM_B2 ** ADAM_STEP)
    delta = -ADAM_LR * (m_hat / (jnp.sqrt(v_hat) + ADAM_EPS) + ADAM_WD * w)
    return delta, m, v


def _sum_partials(p_ref):
    g = p_ref[0].astype(F32)
    for s in range(1, N_DEV):
        g = g + p_ref[s].astype(F32)
    return g


def _adamw_big(parts, w, m, v, name, tr):
    r, c = w.shape
    parts = [p if isinstance(p, tuple) else (p, 0, p.shape[1]) for p in parts]
    tiles = [rows // tr for _, _, rows in parts]
    starts = [sum(tiles[:l]) for l in range(len(parts))]
    assert all(lo % tr == 0 and rows % tr == 0 for _, lo, rows in parts) and sum(tiles) * tr == r

    def body(*refs):
        p_refs, (w_ref, m_ref, v_ref, g_out, d_out, m_out, v_out) = refs[:len(parts)], refs[len(parts):]
        i = pl.program_id(0)
        for l, p_ref in enumerate(p_refs):
            @pl.when((i >= starts[l]) & (i < starts[l] + tiles[l]))
            def _():
                g = _sum_partials(p_ref)
                g_out[...] = g
                d_out[...], m_out[...], v_out[...] = _adamw_math(g, w_ref[...], m_ref[...], v_ref[...])

    def part_spec(l):
        return pl.BlockSpec((N_DEV, tr, c), lambda i: (0, jnp.clip(i - starts[l], 0, tiles[l] - 1) + parts[l][1] // tr, 0))

    tile = pl.BlockSpec((tr, c), lambda i: (i, 0))
    shape = jax.ShapeDtypeStruct((r, c), F32)
    return pl.pallas_call(
        body, name=name, grid=(r // tr,),
        in_specs=[part_spec(l) for l in range(len(parts))] + [tile, tile, tile],
        out_specs=[tile] * 4, out_shape=[shape] * 4,
        compiler_params=_params("parallel"),
    )(*[p[0] for p in parts], w, m, v)


def _adamw_small(parts, ws, ms, vs, name):
    n = len(ws)

    def body(*refs):
        ins, outs = refs[:4 * n], refs[4 * n:]
        for a in range(n):
            g = _sum_partials(ins[a])
            outs[4 * a][...] = g
            outs[4 * a + 1][...], outs[4 * a + 2][...], outs[4 * a + 3][...] = _adamw_math(
                g, ins[n + a][...], ins[2 * n + a][...], ins[3 * n + a][...])

    out_shape = []
    for w in ws:
        out_shape += [jax.ShapeDtypeStruct(w.shape, F32)] * 4
    return pl.pallas_call(body, name=name, out_shape=out_shape, compiler_params=_params())(*parts, *ws, *ms, *vs)


PACK_LANES = 128
PACK_ROWS = 8


def _pack(arrs):
    flat = jnp.concatenate([a.reshape(-1).astype(F32) for a in arrs])
    unit = PACK_LANES * PACK_ROWS
    total = -(-flat.shape[0] // unit) * unit
    return jnp.pad(flat, (0, total - flat.shape[0])).reshape(-1, PACK_LANES)


def _unpack(buf, shapes):
    flat = buf.reshape(N_DEV, -1)
    out, pos = [], 0
    for s in shapes:
        size = math.prod(s)
        out.append(flat[:, pos:pos + size].reshape((N_DEV,) + tuple(s)))
        pos += size
    return out


def _interleave(g):
    return jnp.transpose(g, (1, 0, 2)).reshape(g.shape[1], -1)


def _deinterleave(w):
    r = w.shape[0]
    return jnp.transpose(w.reshape(r, N_DEV, -1), (1, 0, 2))


def _ffn_backward(dz, x_in, z_in, g_in, h, cg, cv, u, w_up_t, cw, w_down, tag, exchange=(), exchange_late=()):
    du = _matmul(dz, w_down, "nt", BF16, f"ffn{tag}_du", 1024, 1408, 1024)
    d_w_down = _matmul(u, dz, "tn", BF16, f"ffn{tag}_dwdown", 1408, 1024, 2048)
    (dhg, dhv, dcwg, dcwv, dcbg, dcbv), moved = _ffn_mid_bwd(
        h, cg, cv, du, cw, f"ffn{tag}_mid_bwd", comm=_Comm(exchange=[d_w_down.reshape(N_DEV, -1, D_MODEL), *exchange]))
    d_w_up_t = _matmul_tn_pair(dhg, dhv, x_in, BF16, f"ffn{tag}_dwup", 1408, 1024, 1024,
                               comm=_Comm(exchange=exchange_late) if exchange_late else None)
    if exchange_late:
        d_w_up_t, late = d_w_up_t
        moved = moved + late
    dx = _matmul(dhv, w_up_t, "nn", F32, f"ffn{tag}_dx_value", 1024, 1024, D_FF, b_off=1)
    dz_in, dg_in, db_in = _matmul_ln_bwd(dhg, w_up_t, z_in, g_in, dz, f"ffn{tag}_dx_gate_ln_bwd", 512, res=dx)
    return (dz_in, dg_in, db_in, d_w_up_t.reshape(N_DEV, -1, D_MODEL),
            jnp.concatenate([dcwg, dcwv], axis=1), jnp.concatenate([dcbg, dcbv], axis=1), moved)


def kernel(x, ab_w_in, a_conv_w, a_conv_b, a_norm_g, a_norm_b, b_norm_g, b_norm_b, b_spatial_w, b_spatial_b, ab_w_out, c_w_qkv, c_b_qkv, c_sinks, c_w_o, ffn_w_up, ffn_conv_w, ffn_conv_b, ffn_w_down, ln_g, ln_b, loss_target, m_ab_w_in, m_a_conv_w, m_a_conv_b, m_a_norm_g, m_a_norm_b, m_b_norm_g, m_b_norm_b, m_b_spatial_w, m_b_spatial_b, m_ab_w_out, m_c_w_qkv, m_c_b_qkv, m_c_sinks, m_c_w_o, m_ffn_w_up, m_ffn_conv_w, m_ffn_conv_b, m_ffn_w_down, m_ln_g, m_ln_b, v_ab_w_in, v_a_conv_w, v_a_conv_b, v_a_norm_g, v_a_norm_b, v_b_norm_g, v_b_norm_b, v_b_spatial_w, v_b_spatial_b, v_ab_w_out, v_c_w_qkv, v_c_b_qkv, v_c_sinks, v_c_w_o, v_ffn_w_up, v_ffn_conv_w, v_ffn_conv_b, v_ffn_w_down, v_ln_g, v_ln_b):
    me = 4 * lax.axis_index("x") + 2 * lax.axis_index("y") + lax.axis_index("c")
    xt = x[0]
    t = xt.shape[0]

    small_shard_shapes = [a_conv_w.shape, c_b_qkv.shape, ffn_conv_w.shape, ln_g.shape, ln_b.shape]
    up_shard = [jnp.swapaxes(ffn_w_up[l], 0, 1).astype(BF16) for l in range(2)]
    qkv_shard = jnp.swapaxes(c_w_qkv[0], 0, 1).astype(BF16)
    down_shard = [ffn_w_down[l].astype(BF16) for l in range(2)]
    g_win, g_small = _comm_only(
        _Comm(gather=[ab_w_in[0].astype(BF16), _pack([a_conv_w, c_b_qkv, ffn_conv_w, ln_g, ln_b])]), "gather_first")
    w_in = _interleave(g_win)
    g_acw, g_bqkv, g_fcw, g_lng, g_lnb = _unpack(g_small, small_shard_shapes)
    acw = _interleave(g_acw[:, 0])
    bqkv = g_bqkv[:, 0].reshape(1, -1)
    fcw = [_interleave(g_fcw[:, l]) for l in range(2)]
    lng = jnp.transpose(g_lng, (1, 2, 0, 3)).reshape(2, 2, 1, D_MODEL)
    lnb = jnp.transpose(g_lnb, (1, 2, 0, 3)).reshape(2, 2, 1, D_MODEL)
    fcb = [ffn_conv_b[l:l + 1] for l in range(2)]
    ms = b_spatial_w[0]
    mst = jnp.swapaxes(ms, 1, 2)
    sbt = b_spatial_b[0].T

    h0, (g_wout,) = _matmul(xt, w_in, "nn", BF16, "mix_in", 1024, 1024, 1024, comm=_Comm(gather=[ab_w_out[0].astype(BF16)]))
    w_out = g_wout.reshape(D_MODEL, D_MODEL)
    cat, (g_wup0,) = _mixer_mid_fwd(h0, acw, a_conv_b, a_norm_g, a_norm_b, b_norm_g, b_norm_b, ms, sbt, "mix_mid_fwd",
                                    comm=_Comm(gather=[up_shard[0]]))
    w_up0 = g_wup0.reshape(2 * D_FF, D_MODEL)
    z1, x1 = _matmul_res_ln(cat, w_out, xt, lng[0, 0], lnb[0, 0], "mix_out_ln", 512)
    hf0, (g_wdown0, g_wqkv) = _matmul(x1, w_up0, "nt", BF16, "ffn0_up", 1024, 1408, 1024,
                                      comm=_Comm(gather=[down_shard[0], qkv_shard]))
    w_down0 = g_wdown0.reshape(D_FF, D_MODEL)
    w_qkv = g_wqkv.reshape(Q_WIDTH + KV_WIDTH, D_MODEL)
    (u0, cg0, cv0), (g_wup1,) = _ffn_mid_fwd(hf0, fcw[0], fcb[0], "ffn0_mid_fwd", comm=_Comm(gather=[up_shard[1]]))
    w_up1 = g_wup1.reshape(2 * D_FF, D_MODEL)
    (z2, x2), (g_wo,) = _matmul_res_ln(u0, w_down0, z1, lng[0, 1], lnb[0, 1], "ffn0_down_ln", 512, prev=(lng[0, 0], lnb[0, 0]),
                                       comm=_Comm(gather=[c_w_o[0].astype(BF16)]))
    w_o = g_wo.reshape(D_MODEL, D_MODEL)
    qkv = _matmul(x2, w_qkv, "nt", BF16, "att_qkv", 1024, 1280, 1024, bias=bqkv)
    att, (g_wdown1,) = _attn_fwd(qkv, c_sinks, "att_fwd", comm=_Comm(gather=[down_shard[1]]))
    w_down1 = g_wdown1.reshape(D_FF, D_MODEL)
    z3, x3 = _matmul_res_ln(att, w_o, z2, lng[1, 0], lnb[1, 0], "att_out_ln", 512, prev=(lng[0, 1], lnb[0, 1]))
    hf1 = _matmul(x3, w_up1, "nt", BF16, "ffn1_up", 1024, 1408, 1024)
    u1, cg1, cv1 = _ffn_mid_fwd(hf1, fcw[1], fcb[1], "ffn1_mid_fwd")

    dz4, dg11, db11, loss_terms = _matmul_res_ln_loss(u1, w_down1, z3, lng[1, 1], lnb[1, 1], loss_target[0],
                                                      "ffn1_down_ln_loss", 512, prev=(lng[1, 0], lnb[1, 0]))
    dz3, dg10, db10, d_wup1, d_fcw1, d_fcb1, (p_wdown1,) = _ffn_backward(
        dz4, x3, z3, lng[1, 0], hf1, cg1, cv1, u1, w_up1, fcw[1], w_down1, 1)
    d_att = _matmul(dz3, w_o, "nt", BF16, "att_dout", 1024, 1024, 1024)
    d_wo = _matmul(att, dz3, "tn", BF16, "att_dwo", 1024, 1024, 512)
    rows_up = d_wup1.shape[1]
    first = 3 * rows_up // 4
    (dq, dkv, dbq, dbkv, dsinks), (p_wup1a,) = _attn_bwd(qkv, d_att, c_sinks, "att_bwd",
                                                        comm=_Comm(exchange=[(d_wup1, 0, first)]))
    d_wqkv = jnp.concatenate([_matmul(dq, x2, "tn", BF16, "att_dwq", 1024, 1024, 1024),
                              _matmul(dkv, x2, "tn", BF16, "att_dwkv", KV_WIDTH, 1024, 1024)], axis=0)
    dx2 = _matmul(dkv, w_qkv, "nn", F32, "att_dx_kv", 1024, 1024, KV_WIDTH, b_off=Q_WIDTH // KV_WIDTH)
    dz2, dg01, db01 = _matmul_ln_bwd(dq, w_qkv, z2, lng[0, 1], dz3, "att_dx_q_ln_bwd", 512, res=dx2)
    dz1, dg00, db00, d_wup0, d_fcw0, d_fcb0, (p_wdown0, p_wup1b, p_wqkv, p_wo) = _ffn_backward(
        dz2, x1, z1, lng[0, 0], hf0, cg0, cv0, u0, w_up0, fcw[0], w_down0, 0, exchange=[(d_wup1, first, rows_up - first)],
        exchange_late=[d_wqkv.reshape(N_DEV, -1, D_MODEL), d_wo.reshape(N_DEV, -1, D_MODEL)])
    dcat = _matmul(dz1, w_out, "nt", BF16, "mix_dcat", 1024, 1024, 1024)
    d_wout = _matmul(cat, dz1, "tn", BF16, "mix_dwout", 1024, 1024, 512)
    (dh0, d_acw, d_acb, d_ang, d_anb, d_bng, d_bnb, d_ms, d_sb), (p_wup0, p_wout) = _mixer_mid_bwd(
        h0, dcat, acw, a_conv_b, a_norm_g, a_norm_b, b_norm_g, b_norm_b, ms, mst, sbt, "mix_mid_bwd",
        comm=_Comm(exchange=[d_wup0, d_wout.reshape(N_DEV, -1, D_MODEL)]))
    d_bqkv = jnp.concatenate([dbq, dbkv], axis=1)
    d_lng = jnp.stack([jnp.stack([dg00, dg01]), jnp.stack([dg10, dg11])])
    d_lnb = jnp.stack([jnp.stack([db00, db01]), jnp.stack([db10, db11])])
    small_full = [d_acb, d_ang, d_anb, d_bng, d_bnb, d_ms, d_sb, dsinks[:, :N_Q_HEADS], jnp.concatenate([d_fcb0, d_fcb1], axis=0),
                  d_acw, d_bqkv, jnp.stack([d_fcw0, d_fcw1]), d_lng, d_lnb, loss_terms]
    d_win, (g_small_grads,) = _matmul(xt, dh0, "tn", BF16, "mix_dwin", 1024, 1024, 512, comm=_Comm(gather=[_pack(small_full)]))
    grad_x, (p_win,) = _matmul(dh0, w_in, "nt", F32, "mix_dx", 1024, 1024, 1024, res=dz1, res_scale=ALPHA,
                               comm=_Comm(exchange=[_deinterleave(d_win)]))


    big = {}
    for nm, p, w, m, v, tr, transposed in [
            ("ab_w_in", [p_win], ab_w_in, m_ab_w_in, v_ab_w_in, 256, False),
            ("ab_w_out", [p_wout], ab_w_out, m_ab_w_out, v_ab_w_out, 128, False),
            ("c_w_qkv", [p_wqkv], c_w_qkv, m_c_w_qkv, v_c_w_qkv, 160, True), ("c_w_o", [p_wo], c_w_o, m_c_w_o, v_c_w_o, 128, False),
            ("ffn_w_up", [p_wup0, (p_wup1a, 0, first), (p_wup1b, first, rows_up - first)], ffn_w_up, m_ffn_w_up, v_ffn_w_up, 176, True),
            ("ffn_w_down", [p_wdown0, p_wdown1], ffn_w_down, m_ffn_w_down, v_ffn_w_down, 176, False)]:
        def two_d(a):
            a = jnp.swapaxes(a, 1, 2) if transposed else a
            return a.reshape(-1, a.shape[-1])

        def back(o):
            return jnp.swapaxes(o.reshape(w.shape[0], w.shape[2], w.shape[1]), 1, 2) if transposed else o.reshape(w.shape)

        outs = _adamw_big(p, two_d(w), two_d(m), two_d(v), "adamw_" + nm, tr)
        big[nm] = [back(o) for o in outs]

    *gs, loss_parts = _unpack(g_small_grads, [a.shape for a in small_full])
    loss = 0.5 / D_MODEL * jnp.sum(loss_parts)

    def my_shard(g, width):
        g = g.reshape(g.shape[:-1] + (N_DEV, width))
        return lax.dynamic_index_in_dim(g, me, axis=g.ndim - 2, keepdims=False)

    small_names = ["a_conv_b", "a_norm_g", "a_norm_b", "b_norm_g", "b_norm_b", "b_spatial_w", "b_spatial_b", "c_sinks", "ffn_conv_b",
                   "a_conv_w", "c_b_qkv", "ffn_conv_w", "ln_g", "ln_b"]
    small_w = [a_conv_b, a_norm_g, a_norm_b, b_norm_g, b_norm_b, b_spatial_w, b_spatial_b, c_sinks, ffn_conv_b,
               a_conv_w, c_b_qkv, ffn_conv_w, ln_g, ln_b]
    small_m = [m_a_conv_b, m_a_norm_g, m_a_norm_b, m_b_norm_g, m_b_norm_b, m_b_spatial_w, m_b_spatial_b, m_c_sinks, m_ffn_conv_b,
               m_a_conv_w, m_c_b_qkv, m_ffn_conv_w, m_ln_g, m_ln_b]
    small_v = [v_a_conv_b, v_a_norm_g, v_a_norm_b, v_b_norm_g, v_b_norm_b, v_b_spatial_w, v_b_spatial_b, v_c_sinks, v_ffn_conv_b,
               v_a_conv_w, v_c_b_qkv, v_ffn_conv_w, v_ln_g, v_ln_b]
    gs[9:] = [my_shard(g, w.shape[-1]) for g, w in zip(gs[9:], small_w[9:])]
    two_d = [(-1, w.shape[-1]) for w in small_w]
    outs = _adamw_small([g.reshape((N_DEV,) + w.reshape(s).shape) for g, w, s in zip(gs, small_w, two_d)],
                        [w.reshape(s) for w, s in zip(small_w, two_d)], [m.reshape(s) for m, s in zip(small_m, two_d)],
                        [v.reshape(s) for v, s in zip(small_v, two_d)], "adamw_small")
    small = {nm: [o.reshape(w.shape) for o in outs[4 * a:4 * a + 4]] for a, (nm, w) in enumerate(zip(small_names, small_w))}

    res = {**big, **small}
    order = ["ab_w_in", "a_conv_w", "a_conv_b", "a_norm_g", "a_norm_b", "b_norm_g", "b_norm_b", "b_spatial_w", "b_spatial_b", "ab_w_out",
             "c_w_qkv", "c_b_qkv", "c_sinks", "c_w_o", "ffn_w_up", "ffn_conv_w", "ffn_conv_b", "ffn_w_down", "ln_g", "ln_b"]
    return (loss, grad_x[None], *[res[nm][0] for nm in order], *[res[nm][1] for nm in order],
            *[res[nm][2] for nm in order], *[res[nm][3] for nm in order])
```

```python
import functools
import math

import jax
import jax.numpy as jnp
from jax import lax
from jax.experimental import pallas as pl
from jax.experimental.pallas import tpu as pltpu

F32 = jnp.float32
BF16 = jnp.bfloat16

N_DEV = 8
D_MODEL = 1024
A_WIDTH = 512
A_KERNEL = 31
B_GROUPS = 4
B_CHUNK = 128
HEAD_DIM = 64
N_Q_HEADS = 16
N_KV_HEADS = 2
ATT_BLOCK = 128
D_FF = 2816
FFN_KERNEL = 3
ALPHA = (2.0 * 2) ** 0.25
LN_EPS = 1e-5
GELU_K = math.sqrt(2.0 / math.pi)
GELU_C = 0.044715
ADAM_LR = 0.001
ADAM_B1 = 0.9
ADAM_B2 = 0.999
ADAM_EPS = 1e-08
ADAM_WD = 0.01
ADAM_STEP = 10
VMEM_LIMIT = 56 * 1024 * 1024
MESH_ID = pl.DeviceIdType.MESH


def _params(*sem):
    return pltpu.CompilerParams(dimension_semantics=sem, vmem_limit_bytes=VMEM_LIMIT)


def _gelu(x):
    t = jnp.tanh(GELU_K * x * (1.0 + GELU_C * x * x))
    return 0.5 * x * (1.0 + t)


def _gelu_and_grad(x):
    x2 = x * x
    t = jnp.tanh(GELU_K * x * (1.0 + GELU_C * x2))
    g = 0.5 * x * (1.0 + t)
    dg = 0.5 * (1.0 + t) + 0.5 * x * (1.0 - t * t) * (GELU_K * (1.0 + 3.0 * GELU_C * x2))
    return g, dg


def _sigmoid(x):
    return 1.0 / (1.0 + jnp.exp(-x))


def _ln_stats(z):
    mu = jnp.mean(z, axis=-1, keepdims=True)
    zc = z - mu
    var = jnp.mean(zc * zc, axis=-1, keepdims=True)
    r = lax.rsqrt(var + LN_EPS)
    return zc * r, r


def _ln_bwd_rows(dn, nh, r):
    return r * (dn - jnp.mean(dn, axis=-1, keepdims=True) - nh * jnp.mean(dn * nh, axis=-1, keepdims=True))


def _colsum(x):
    return jnp.sum(x, axis=0, keepdims=True)


def _dot(a, b, dims):
    return lax.dot_general(a.astype(BF16), b.astype(BF16), (dims, ((), ())), preferred_element_type=F32)


NN = ((1,), (0,))
NT = ((1,), (1,))
TN = ((0,), (0,))


ANY = pl.BlockSpec(memory_space=pl.ANY)
N_RELATIONS = N_DEV - 1


def _my_place():
    return lax.axis_index("x"), lax.axis_index("y"), lax.axis_index("c")


class _Comm:
    def __init__(self, gather=(), exchange=()):
        exchange = [e if isinstance(e, tuple) else (e, 0, e.shape[1]) for e in exchange]
        self.arrs = list(gather) + [e[0] for e in exchange]
        self.n_gather = len(gather)
        self.n = len(self.arrs)
        self.rows = [None] * self.n_gather + [pl.ds(lo, n) for _, lo, n in exchange]

    def out_shape(self):
        return [jax.ShapeDtypeStruct(((N_DEV,) + a.shape) if i < self.n_gather else a.shape, a.dtype)
                for i, a in enumerate(self.arrs)]

    def sems(self):
        return [pltpu.SemaphoreType.DMA((self.n, N_RELATIONS)), pltpu.SemaphoreType.DMA((self.n, N_RELATIONS)),
                pltpu.SemaphoreType.DMA((self.n,))]

    def _gather_copy(self, ins, outs, sems, a, k, place, to, from_input=False):
        px, py, pc = place
        block = outs[a].at[4 * px + 2 * py + pc]
        return pltpu.make_async_remote_copy(
            src_ref=ins[a] if from_input else block, dst_ref=block,
            send_sem=sems[0].at[a, k], recv_sem=sems[1].at[a, k], device_id=to, device_id_type=MESH_ID)

    def _exchange_copy(self, ins, outs, sems, a, k, landing=False):
        x, y, c = _my_place()
        me = 4 * x + 2 * y + c
        peer = (x ^ (k >> 2), y ^ ((k >> 1) & 1), c ^ (k & 1))
        return pltpu.make_async_remote_copy(
            src_ref=ins[a].at[me ^ k, self.rows[a]], dst_ref=outs[a].at[(me ^ k) if landing else me, self.rows[a]],
            send_sem=sems[0].at[a, k - 1], recv_sem=sems[1].at[a, k - 1], device_id=peer, device_id_type=MESH_ID)

    def _local_copy(self, ins, outs, sems, a):
        x, y, c = _my_place()
        me = 4 * x + 2 * y + c
        if a < self.n_gather:
            return pltpu.make_async_copy(ins[a], outs[a].at[me], sems[2].at[a])
        return pltpu.make_async_copy(ins[a].at[me, self.rows[a]], outs[a].at[me, self.rows[a]], sems[2].at[a])

    def _first_stage(self, ins, outs, sems, a):
        x, y, c = _my_place()
        me = (x, y, c)
        chips = [(1 - x, y), (x, 1 - y), (1 - x, 1 - y)]
        return ([self._gather_copy(ins, outs, sems, a, 0, me, (x, y, 1 - c), from_input=True)]
                + [self._gather_copy(ins, outs, sems, a, 1 + j, me, (*chip, c), from_input=True) for j, chip in enumerate(chips)])

    def start(self, ins, outs, sems):
        for a in range(self.n):
            self._local_copy(ins, outs, sems, a).start()
        for a in range(self.n_gather):
            for cp in self._first_stage(ins, outs, sems, a):
                cp.start()
        for k in range(1, N_DEV):
            for a in range(self.n_gather, self.n):
                self._exchange_copy(ins, outs, sems, a, k).start()

    def forward(self, ins, outs, sems):
        x, y, c = _my_place()
        me, sibling = (x, y, c), (x, y, 1 - c)
        for j, chip in enumerate([(1 - x, y), (x, 1 - y), (1 - x, 1 - y)]):
            for a in range(self.n_gather):
                self._gather_copy(ins, outs, sems, a, 1 + j, (*chip, c), me).wait_recv()
                self._gather_copy(ins, outs, sems, a, 4 + j, (*chip, c), sibling).start()

    def finish(self, ins, outs, sems):
        x, y, c = _my_place()
        me, sibling = (x, y, c), (x, y, 1 - c)
        chips = [(1 - x, y), (x, 1 - y), (1 - x, 1 - y)]
        passed = [self._gather_copy(ins, outs, sems, a, 4 + j, (*chip, c), sibling)
                  for j, chip in enumerate(chips) for a in range(self.n_gather)]
        for a in range(self.n_gather):
            self._gather_copy(ins, outs, sems, a, 0, sibling, me).wait_recv()
            for j, chip in enumerate(chips):
                self._gather_copy(ins, outs, sems, a, 4 + j, (*chip, 1 - c), me).wait_recv()
        for k in range(1, N_DEV):
            for a in range(self.n_gather, self.n):
                self._exchange_copy(ins, outs, sems, a, k, landing=True).wait_recv()
        for a in range(self.n_gather):
            for cp in self._first_stage(ins, outs, sems, a):
                cp.wait_send()
        for cp in passed:
            cp.wait_send()
        for k in range(1, N_DEV):
            for a in range(self.n_gather, self.n):
                self._exchange_copy(ins, outs, sems, a, k).wait_send()
        for a in range(self.n):
            self._local_copy(ins, outs, sems, a).wait()


def _comm_only(comm, name):
    def body(*refs):
        ins, outs, sems = refs[:comm.n], refs[comm.n:2 * comm.n], refs[2 * comm.n:]
        comm.start(ins, outs, sems)
        comm.forward(ins, outs, sems)
        comm.finish(ins, outs, sems)

    return pl.pallas_call(body, name=name, in_specs=[ANY] * comm.n, out_specs=[ANY] * comm.n,
                          out_shape=comm.out_shape(), scratch_shapes=comm.sems())(*comm.arrs)


def _call(body, *, name, grid, in_specs, out_specs, out_shape, args, sem, scratch_shapes=(), comm=None):
    in_specs, out_specs, out_shape, scratch_shapes = list(in_specs), list(out_specs), list(out_shape), list(scratch_shapes)
    if comm is None:
        outs = pl.pallas_call(body, name=name, grid=grid, in_specs=in_specs, out_specs=out_specs, out_shape=out_shape,
                              scratch_shapes=scratch_shapes, compiler_params=_params(*sem))(*args)
        return list(outs), []
    n_in, n_out, n_scr, nc = len(in_specs), len(out_specs), len(scratch_shapes), comm.n

    def wrapped(*refs):
        ins, refs = refs[:n_in], refs[n_in:]
        c_in, refs = refs[:nc], refs[nc:]
        outs, refs = refs[:n_out], refs[n_out:]
        c_out, refs = refs[:nc], refs[nc:]
        scr, sems = refs[:n_scr], refs[n_scr:]
        step = functools.reduce(lambda acc, ax: acc * grid[ax] + pl.program_id(ax), range(len(grid)), 0)
        steps = math.prod(grid)

        @pl.when(step == 0)
        def _():
            comm.start(c_in, c_out, sems)

        @pl.when(step == steps - 1)
        def _():
            comm.forward(c_in, c_out, sems)

        body(*ins, *outs, *scr)

        @pl.when(step == steps - 1)
        def _():
            comm.finish(c_in, c_out, sems)

    outs = pl.pallas_call(
        wrapped, name=name, grid=grid, in_specs=in_specs + [ANY] * nc, out_specs=out_specs + [ANY] * nc,
        out_shape=out_shape + comm.out_shape(), scratch_shapes=scratch_shapes + comm.sems(),
        compiler_params=_params(*(["arbitrary"] * len(grid))))(*args, *comm.arrs)
    return list(outs[:n_out]), list(outs[n_out:])


def _matmul(a, b, mode, out_dtype, name, tm, tn, tk, *, bias=None, res=None, res_scale=1.0, b_off=0, comm=None):
    tm = min(tm, a.shape[1] if mode == "tn" else a.shape[0])
    tk = min(tk, a.shape[0] if mode == "tn" else a.shape[1])
    if mode == "nn":
        (m, k), n = a.shape, b.shape[1]
        a_spec = pl.BlockSpec((tm, tk), lambda i, j, kk: (i, kk))
        b_spec = pl.BlockSpec((tk, tn), lambda i, j, kk: (kk + b_off, j))
        dims = NN
    elif mode == "nt":
        (m, k), n = a.shape, b.shape[0]
        a_spec = pl.BlockSpec((tm, tk), lambda i, j, kk: (i, kk))
        b_spec = pl.BlockSpec((tn, tk), lambda i, j, kk: (j, kk + b_off))
        dims = NT
    else:
        (k, m), n = a.shape, b.shape[1]
        a_spec = pl.BlockSpec((tk, tm), lambda i, j, kk: (kk, i))
        b_spec = pl.BlockSpec((tk, tn), lambda i, j, kk: (kk, j))
        dims = TN
    assert m % tm == 0 and n % tn == 0 and k % tk == 0, (name, m, n, k)
    nk = k // tk
    in_specs = [a_spec, b_spec]
    args = [a, b]
    if bias is not None:
        in_specs.append(pl.BlockSpec((1, tn), lambda i, j, kk: (0, j)))
        args.append(bias)
    if res is not None:
        in_specs.append(pl.BlockSpec((tm, tn), lambda i, j, kk: (i, j)))
        args.append(res)

    def finish(out, refs, o_ref):
        pos = 2
        if bias is not None:
            out = out + refs[pos][...]
            pos += 1
        if res is not None:
            out = out + res_scale * refs[pos][...].astype(F32)
        o_ref[...] = out.astype(out_dtype)

    def body_one_step(*refs):
        finish(_dot(refs[0][...], refs[1][...], dims), refs, refs[-1])

    def body(*refs):
        a_ref, b_ref = refs[0], refs[1]
        o_ref, acc = refs[-2], refs[-1]
        kk = pl.program_id(2)

        @pl.when(kk == 0)
        def _():
            acc[...] = jnp.zeros_like(acc)

        acc[...] += _dot(a_ref[...], b_ref[...], dims)

        @pl.when(kk == nk - 1)
        def _():
            finish(acc[...], refs, o_ref)

    (out,), moved = _call(
        body_one_step if nk == 1 else body, name=name, grid=(m // tm, n // tn, nk),
        in_specs=in_specs, out_specs=[pl.BlockSpec((tm, tn), lambda i, j, kk: (i, j))],
        out_shape=[jax.ShapeDtypeStruct((m, n), out_dtype)],
        scratch_shapes=[] if nk == 1 else [pltpu.VMEM((tm, tn), F32)],
        sem=("parallel", "parallel", "arbitrary"), args=args, comm=comm)
    return out if comm is None else (out, moved)


def _matmul_tn_pair(a0, a1, b, out_dtype, name, tm, tn, tk, comm=None):
    (k, m), n = a0.shape, b.shape[1]
    tk = min(tk, k)
    assert a1.shape == a0.shape and m % tm == 0 and n % tn == 0 and k % tk == 0, (name, m, n, k)
    mi, nk = m // tm, k // tk

    def body(a0_ref, a1_ref, b_ref, o_ref, acc):
        i, kk = pl.program_id(0), pl.program_id(2)

        @pl.when(kk == 0)
        def _():
            acc[...] = jnp.zeros_like(acc)

        @pl.when(i < mi)
        def _():
            acc[...] += _dot(a0_ref[...], b_ref[...], TN)

        @pl.when(i >= mi)
        def _():
            acc[...] += _dot(a1_ref[...], b_ref[...], TN)

        @pl.when(kk == nk - 1)
        def _():
            o_ref[...] = acc[...].astype(out_dtype)

    (out,), moved = _call(
        body, name=name, grid=(2 * mi, n // tn, nk),
        in_specs=[pl.BlockSpec((tk, tm), lambda i, j, kk: (jnp.where(i < mi, kk, nk - 1), jnp.minimum(i, mi - 1))),
                  pl.BlockSpec((tk, tm), lambda i, j, kk: (jnp.where(i >= mi, kk, 0), jnp.maximum(i - mi, 0))),
                  pl.BlockSpec((tk, tn), lambda i, j, kk: (kk, j))],
        out_specs=[pl.BlockSpec((tm, tn), lambda i, j, kk: (i, j))],
        out_shape=[jax.ShapeDtypeStruct((2 * m, n), out_dtype)],
        scratch_shapes=[pltpu.VMEM((tm, tn), F32)],
        sem=("parallel", "parallel", "arbitrary"), args=(a0, a1, b), comm=comm)
    return out if comm is None else (out, moved)


def _residual_input(x_ref, prev_refs):
    if not prev_refs:
        return x_ref[...]
    nh, _ = _ln_stats(x_ref[...])
    return nh * prev_refs[0][...] + prev_refs[1][...]


def _matmul_res_ln(a, b, x, g, beta, name, tm, prev=None, comm=None):
    t, k = a.shape
    d = b.shape[1]
    tm = min(tm, t)
    assert t % tm == 0
    n_prev = 0 if prev is None else 2

    def body(a_ref, b_ref, x_ref, g_ref, beta_ref, *rest):
        z_ref, xo_ref = rest[n_prev:]
        z = ALPHA * _residual_input(x_ref, rest[:n_prev]) + _dot(a_ref[...], b_ref[...], NN)
        nh, _ = _ln_stats(z)
        z_ref[...] = z
        xo_ref[...] = (nh * g_ref[...] + beta_ref[...]).astype(BF16)

    row = pl.BlockSpec((tm, d), lambda i: (i, 0))
    vec = pl.BlockSpec((1, d), lambda i: (0, 0))
    outs, moved = _call(
        body, name=name, grid=(t // tm,),
        in_specs=[pl.BlockSpec((tm, k), lambda i: (i, 0)), pl.BlockSpec((k, d), lambda i: (0, 0)), row, vec, vec] + [vec] * n_prev,
        out_specs=[row, row],
        out_shape=[jax.ShapeDtypeStruct((t, d), F32), jax.ShapeDtypeStruct((t, d), BF16)],
        sem=("parallel",), args=(a, b, x, g, beta, *(prev or ())), comm=comm)
    return outs if comm is None else (outs, moved)


def _matmul_ln_bwd(a, b, z, g, dres, name, tm, *, res=None, b_off=0):
    m, k = a.shape
    d = b.shape[1]
    tm = min(tm, m)
    assert m % tm == 0

    def body(*refs):
        a_ref, b_ref, z_ref, g_ref, dres_ref = refs[:5]
        dz_ref, dg_ref, db_ref = refs[-3:]

        @pl.when(pl.program_id(0) == 0)
        def _():
            dg_ref[...] = jnp.zeros_like(dg_ref)
            db_ref[...] = jnp.zeros_like(db_ref)

        dbr = _dot(a_ref[...], b_ref[...], NN)
        if res is not None:
            dbr = dbr + refs[5][...]
        nh, r = _ln_stats(z_ref[...])
        dy = ALPHA * dres_ref[...] + dbr
        dg_ref[...] += _colsum(dy * nh)
        db_ref[...] += _colsum(dy)
        dz_ref[...] = _ln_bwd_rows(dy * g_ref[...], nh, r)

    row = pl.BlockSpec((tm, d), lambda i: (i, 0))
    vec = pl.BlockSpec((1, d), lambda i: (0, 0))
    vshape = jax.ShapeDtypeStruct((1, d), F32)
    return pl.pallas_call(
        body, name=name, grid=(m // tm,),
        in_specs=[pl.BlockSpec((tm, k), lambda i: (i, 0)), pl.BlockSpec((k, d), lambda i: (b_off, 0)), row, vec, row]
        + ([row] if res is not None else []),
        out_specs=[row, vec, vec], out_shape=[jax.ShapeDtypeStruct((m, d), F32), vshape, vshape],
        compiler_params=_params("arbitrary"),
    )(a, b, z, g, dres, *([res] if res is not None else []))


def _matmul_res_ln_loss(a, b, x, g, beta, target, name, tm, prev):
    t, k = a.shape
    d = b.shape[1]
    tm = min(tm, t)

    def body(a_ref, b_ref, x_ref, g_ref, beta_ref, t_ref, gp_ref, bp_ref, dz_ref, dg_ref, db_ref, loss_ref):
        @pl.when(pl.program_id(0) == 0)
        def _():
            dg_ref[...] = jnp.zeros_like(dg_ref)
            db_ref[...] = jnp.zeros_like(db_ref)
            loss_ref[...] = jnp.zeros_like(loss_ref)

        nh, r = _ln_stats(ALPHA * _residual_input(x_ref, (gp_ref, bp_ref)) + _dot(a_ref[...], b_ref[...], NN))
        err = nh * g_ref[...] + beta_ref[...] - t_ref[...]
        loss_ref[...] += _colsum(err * err)
        dy = err * (1.0 / d)
        dg_ref[...] += _colsum(dy * nh)
        db_ref[...] += _colsum(dy)
        dz_ref[...] = _ln_bwd_rows(dy * g_ref[...], nh, r)

    row = pl.BlockSpec((tm, d), lambda i: (i, 0))
    vec = pl.BlockSpec((1, d), lambda i: (0, 0))
    vshape = jax.ShapeDtypeStruct((1, d), F32)
    return pl.pallas_call(
        body, name=name, grid=(t // tm,),
        in_specs=[pl.BlockSpec((tm, k), lambda i: (i, 0)), pl.BlockSpec((k, d), lambda i: (0, 0)), row, vec, vec, row, vec, vec],
        out_specs=[row, vec, vec, vec],
        out_shape=[jax.ShapeDtypeStruct((t, d), F32), vshape, vshape, vshape],
        compiler_params=_params("arbitrary"),
    )(a, b, x, g, beta, target, *prev)


FFN_HALO = 16
FFN_CHUNK = 256
LANES = 128
SUBLANES = 8


def _rows_up(e, start, rows):
    if start % SUBLANES == 0:
        return e[start:start + rows]
    return pltpu.roll(e, e.shape[0] - start, 0)[0:rows]


def _fold(x):
    return jnp.sum(x.reshape(x.shape[0] // SUBLANES, SUBLANES, x.shape[1]), axis=0)


def _ffn_mid_fwd(h, cw, cb, name, tm=1024, tc=1408, comm=None):
    t, f2 = h.shape
    tm = min(tm, t)
    f = f2 // 2
    nj, nt, hb = f // tc, t // tm, tm // FFN_HALO

    ch = min(FFN_CHUNK, tm)

    def body(hg, hgp, hv, hvp, cwg, cwv, cbg, cbv, u_ref, cg_ref, cv_ref):
        i = pl.program_id(1)
        o = FFN_HALO - FFN_KERNEL + 1
        for lg in range(tc // LANES):
            cols = slice(lg * LANES, (lg + 1) * LANES)
            wg, wv = [cwg[k:k + 1, cols] for k in range(FFN_KERNEL)], [cwv[k:k + 1, cols] for k in range(FFN_KERNEL)]
            bg, bv = cbg[:, cols], cbv[:, cols]

            def emit(base, eg, ev):
                cg = wg[0] * _rows_up(eg, o, ch) + wg[1] * _rows_up(eg, o + 1, ch) + wg[2] * _rows_up(eg, o + 2, ch) + bg
                cv = wv[0] * _rows_up(ev, o, ch) + wv[1] * _rows_up(ev, o + 1, ch) + wv[2] * _rows_up(ev, o + 2, ch) + bv
                u_ref[pl.ds(base, ch), cols] = (_gelu(cg) * cv).astype(BF16)
                cg_ref[pl.ds(base, ch), cols] = cg.astype(BF16)
                cv_ref[pl.ds(base, ch), cols] = cv.astype(BF16)

            def first(main, prev):
                return jnp.concatenate([jnp.where(i > 0, prev[:, cols].astype(F32), 0.0), main[0:ch, cols].astype(F32)], axis=0)

            def inner(c, carry):
                base = pl.multiple_of(c * ch, ch)
                emit(base, hg[pl.ds(base - FFN_HALO, ch + FFN_HALO), cols].astype(F32),
                     hv[pl.ds(base - FFN_HALO, ch + FFN_HALO), cols].astype(F32))
                return carry

            emit(0, first(hg, hgp), first(hv, hvp))
            if tm > ch:
                lax.fori_loop(1, tm // ch, inner, 0)

    def main_spec(off):
        return pl.BlockSpec((tm, tc), lambda j, i: (i, j + off))

    def prev_spec(off):
        return pl.BlockSpec((FFN_HALO, tc), lambda j, i: (jnp.maximum(i * hb - 1, 0), j + off))

    def par_spec(rows, off):
        return pl.BlockSpec((rows, tc), lambda j, i: (0, j + off))

    outs, moved = _call(
        body, name=name, grid=(nj, nt),
        in_specs=[main_spec(0), prev_spec(0), main_spec(nj), prev_spec(nj),
                  par_spec(FFN_KERNEL, 0), par_spec(FFN_KERNEL, nj), par_spec(1, 0), par_spec(1, nj)],
        out_specs=[pl.BlockSpec((tm, tc), lambda j, i: (i, j))] * 3,
        out_shape=[jax.ShapeDtypeStruct((t, f), BF16)] * 3,
        sem=("parallel", "arbitrary"), args=(h, h, h, h, cw, cw, cb, cb), comm=comm)
    return outs if comm is None else (outs, moved)


def _ffn_mid_bwd(h, cg, cv, du, cw, name, tm=1024, tc=1408, comm=None):
    t, f2 = h.shape
    tm = min(tm, t)
    f = f2 // 2
    nj, nt, hb = f // tc, t // tm, tm // FFN_HALO

    ch = min(FFN_CHUNK, tm)
    ahead = ch + SUBLANES
    n_ch = tm // ch

    def body(hg, hv, cg_ref, cgn_ref, cv_ref, cvn_ref, du_ref, dun_ref, cwg, cwv,
             dhg_ref, dhv_ref, dcwg_ref, dcwv_ref, dcbg_ref, dcbv_ref):
        i = pl.program_id(1)

        @pl.when(i == 0)
        def _():
            for ref in (dcwg_ref, dcwv_ref, dcbg_ref, dcbv_ref):
                ref[...] = jnp.zeros_like(ref)

        for lg in range(tc // LANES):
            cols = slice(lg * LANES, (lg + 1) * LANES)
            wg, wv = [cwg[k:k + 1, cols] for k in range(FFN_KERNEL)], [cwv[k:k + 1, cols] for k in range(FFN_KERNEL)]

            def emit(base, cg_e, cv_e, du_e, acc):
                cg_a, cv_a, du_a = cg_e[0:ahead], cv_e[0:ahead], du_e[0:ahead]
                gl, dgl = _gelu_and_grad(cg_a)

                def back(d, h_ref, w, dh_ref):
                    later = [d[0:ch], _rows_up(d, 1, ch), _rows_up(d, 2, ch)]
                    dh_ref[pl.ds(base, ch), cols] = (w[2] * later[0] + w[1] * later[1] + w[0] * later[2]).astype(BF16)
                    h_own = h_ref[pl.ds(base, ch), cols].astype(F32)
                    return [_fold(later[0])] + [_fold(later[FFN_KERNEL - 1 - k] * h_own) for k in range(FFN_KERNEL)]

                sums = back(du_a * cv_a * dgl, hg, wg, dhg_ref) + back(du_a * gl, hv, wv, dhv_ref)
                return tuple(a + s_ for a, s_ in zip(acc, sums))

            def inner(c, acc):
                base = pl.multiple_of(c * ch, ch)
                rows = pl.ds(base, ch + FFN_HALO)
                return emit(base, cg_ref[rows, cols].astype(F32), cv_ref[rows, cols].astype(F32), du_ref[rows, cols].astype(F32), acc)

            def last(acc):
                def rows(main, after):
                    return jnp.concatenate([main[tm - ch:tm, cols].astype(F32), after], axis=0)

                du_next = jnp.where(i < nt - 1, dun_ref[:, cols].astype(F32), 0.0)
                return emit(tm - ch, rows(cg_ref, cgn_ref[:, cols].astype(F32)), rows(cv_ref, cvn_ref[:, cols].astype(F32)),
                            rows(du_ref, du_next), acc)

            acc = (jnp.zeros((SUBLANES, LANES), F32),) * (2 * (1 + FFN_KERNEL))
            if n_ch > 1:
                acc = lax.fori_loop(0, n_ch - 1, inner, acc)
            acc = last(acc)
            dcbg_ref[:, cols] += _colsum(acc[0])
            dcbv_ref[:, cols] += _colsum(acc[1 + FFN_KERNEL])
            for k in range(FFN_KERNEL):
                dcwg_ref[k:k + 1, cols] += _colsum(acc[1 + k])
                dcwv_ref[k:k + 1, cols] += _colsum(acc[2 + FFN_KERNEL + k])

    last_blk = t // FFN_HALO - 1

    def main_spec(off):
        return pl.BlockSpec((tm, tc), lambda j, i: (i, j + off))

    def next_spec(off):
        return pl.BlockSpec((FFN_HALO, tc), lambda j, i: (jnp.minimum((i + 1) * hb, last_blk), j + off))

    def par_spec(rows, off):
        return pl.BlockSpec((rows, tc), lambda j, i: (0, j + off))

    out_tile = pl.BlockSpec((tm, tc), lambda j, i: (i, j))
    outs, moved = _call(
        body, name=name, grid=(nj, nt),
        in_specs=[main_spec(0), main_spec(nj), main_spec(0), next_spec(0), main_spec(0), next_spec(0), main_spec(0), next_spec(0),
                  par_spec(FFN_KERNEL, 0), par_spec(FFN_KERNEL, nj)],
        out_specs=[out_tile, out_tile, par_spec(FFN_KERNEL, 0), par_spec(FFN_KERNEL, 0), par_spec(1, 0), par_spec(1, 0)],
        out_shape=[jax.ShapeDtypeStruct((t, f), BF16), jax.ShapeDtypeStruct((t, f), BF16),
                   jax.ShapeDtypeStruct((FFN_KERNEL, f), F32), jax.ShapeDtypeStruct((FFN_KERNEL, f), F32),
                   jax.ShapeDtypeStruct((1, f), F32), jax.ShapeDtypeStruct((1, f), F32)],
        sem=("parallel", "arbitrary"), args=(h, h, cg, cg, cv, cv, du, du, cw, cw), comm=comm)
    return outs if comm is None else (outs, moved)


MIX_HALO = 32


def _glu(hh):
    return hh[:, 0:A_WIDTH] * _sigmoid(hh[:, A_WIDTH:2 * A_WIDTH])


def _fill_row_shifts(s):
    rows = s.shape[1] - SUBLANES
    for j in range(1, SUBLANES):
        s[j, 0:rows, :] = s[0, pl.ds(j, rows), :]


def _rows_from(s, start, rows):
    j = start % SUBLANES
    return s[j, start - j:start - j + rows, :]


def _tril_mask():
    return lax.broadcasted_iota(jnp.int32, (B_CHUNK, B_CHUNK), 0) >= lax.broadcasted_iota(jnp.int32, (B_CHUNK, B_CHUNK), 1)


def _spatial_mix(q, ms_ref, sbt_ref, tm):
    mask = _tril_mask()
    ws = [jnp.where(mask, ms_ref[g], 0.0).astype(BF16) for g in range(B_GROUPS)]
    qb = q.astype(BF16)
    rows = []
    for c in range(tm // B_CHUNK):
        cols = [_dot(ws[g], qb[c * B_CHUNK:(c + 1) * B_CHUNK, g * 128:(g + 1) * 128], NN) + sbt_ref[:, g:g + 1]
                for g in range(B_GROUPS)]
        rows.append(jnp.concatenate(cols, axis=1))
    return jnp.concatenate(rows, axis=0)


def _mixer_mid_fwd(h, cw, cb, ag, ab, bg, bb, ms, sbt, name, tm=256, comm=None):
    t = h.shape[0]
    nt, hb = t // tm, tm // MIX_HALO
    o = MIX_HALO - A_KERNEL + 1

    def body(h_ref, hp_ref, cw_ref, cb_ref, ag_ref, ab_ref, bg_ref, bb_ref, ms_ref, sbt_ref, cat_ref, y_ref, sp):
        i = pl.program_id(0)
        sp[0, 0:MIX_HALO, :] = jnp.where(i > 0, _glu(hp_ref[:, 0:2 * A_WIDTH].astype(F32)), 0.0)
        sp[0, MIX_HALO:, :] = _glu(h_ref[:, 0:2 * A_WIDTH].astype(F32))
        _fill_row_shifts(sp)
        y = jnp.zeros((tm, A_WIDTH), F32) + cb_ref[...]
        for k in range(A_KERNEL):
            y = y + cw_ref[k:k + 1, :] * _rows_from(sp, o + k, tm)
        y_ref[...] = y.astype(BF16)
        nh, _ = _ln_stats(y)
        ln = nh * ag_ref[...] + ab_ref[...]
        cat_ref[:, 0:A_WIDTH] = (ln * _sigmoid(ln)).astype(BF16)
        u = _gelu(h_ref[:, 1024:1536].astype(F32))
        nb, _ = _ln_stats(_gelu(h_ref[:, 1536:2048].astype(F32)))
        mixed = _spatial_mix(nb * bg_ref[...] + bb_ref[...], ms_ref, sbt_ref, tm)
        cat_ref[:, A_WIDTH:] = (u * mixed).astype(BF16)

    vec = pl.BlockSpec((1, A_WIDTH), lambda i: (0, 0))
    outs, moved = _call(
        body, name=name, grid=(nt,),
        in_specs=[pl.BlockSpec((tm, 2048), lambda i: (i, 0)),
                  pl.BlockSpec((MIX_HALO, 2048), lambda i: (jnp.maximum(i * hb - 1, 0), 0)),
                  pl.BlockSpec((A_KERNEL, A_WIDTH), lambda i: (0, 0)), vec, vec, vec, vec, vec,
                  pl.BlockSpec((B_GROUPS, B_CHUNK, B_CHUNK), lambda i: (0, 0, 0)),
                  pl.BlockSpec((B_CHUNK, B_GROUPS), lambda i: (0, 0))],
        out_specs=[pl.BlockSpec((tm, D_MODEL), lambda i: (i, 0)), pl.BlockSpec((tm, A_WIDTH), lambda i: (i, 0))],
        out_shape=[jax.ShapeDtypeStruct((t, D_MODEL), BF16), jax.ShapeDtypeStruct((t, A_WIDTH), BF16)],
        scratch_shapes=[pltpu.VMEM((SUBLANES, tm + MIX_HALO, A_WIDTH), F32)],
        sem=("parallel",), args=(h, h, cw, cb, ag, ab, bg, bb, ms, sbt), comm=comm)
    return outs if comm is None else (outs, moved)


def _mixer_mid_bwd(h, y, dcat, cw, ag, ab, bg, bb, ms, mst, sbt, name, tm=256, comm=None):
    t = h.shape[0]
    nt, hb = t // tm, tm // MIX_HALO
    r = tm + MIX_HALO
    nchunk = tm // B_CHUNK

    def body(h_ref, y_ref, yn_ref, dc_ref, dcn_ref, cw_ref, ag_ref, ab_ref, bg_ref, bb_ref, ms_ref, mst_ref, sbt_ref,
             dh_ref, dcw_ref, dcb_ref, dag_ref, dab_ref, dbg_ref, dbb_ref, dms_ref, dsb_ref, sdy, sbacc):
        i = pl.program_id(0)

        @pl.when(i == 0)
        def _():
            for ref in (dcw_ref, dcb_ref, dag_ref, dab_ref, dbg_ref, dbb_ref, dms_ref, dsb_ref, sbacc):
                ref[...] = jnp.zeros_like(ref)

        nh, rs = _ln_stats(jnp.concatenate([y_ref[...].astype(F32), yn_ref[...].astype(F32)], axis=0))
        ln = nh * ag_ref[...] + ab_ref[...]
        sg = _sigmoid(ln)
        dao = jnp.concatenate([dc_ref[:, 0:A_WIDTH].astype(F32),
                               jnp.where(i < nt - 1, dcn_ref[:, 0:A_WIDTH].astype(F32), 0.0)], axis=0)
        dln = dao * (sg * (1.0 + ln * (1.0 - sg)))
        dag_ref[...] += _colsum(dln[0:tm] * nh[0:tm])
        dab_ref[...] += _colsum(dln[0:tm])
        sdy[0] = _ln_bwd_rows(dln * ag_ref[...], nh, rs)
        _fill_row_shifts(sdy)
        dcb_ref[...] += _colsum(sdy[0, 0:tm, :])
        av = h_ref[:, 0:A_WIDTH].astype(F32)
        s = _sigmoid(h_ref[:, A_WIDTH:2 * A_WIDTH].astype(F32))
        p_own = av * s
        dp = jnp.zeros((tm, A_WIDTH), F32)
        for k in range(A_KERNEL):
            later = _rows_from(sdy, A_KERNEL - 1 - k, tm)
            dcw_ref[k:k + 1, :] += _colsum(later * p_own)
            dp = dp + cw_ref[k:k + 1, :] * later
        dh_ref[:, 0:A_WIDTH] = (dp * s).astype(BF16)
        dh_ref[:, A_WIDTH:2 * A_WIDTH] = (dp * av * s * (1.0 - s)).astype(BF16)

        u, dgu = _gelu_and_grad(h_ref[:, 1024:1536].astype(F32))
        w, dgw = _gelu_and_grad(h_ref[:, 1536:2048].astype(F32))
        nb, rb = _ln_stats(w)
        q = nb * bg_ref[...] + bb_ref[...]
        mixed = _spatial_mix(q, ms_ref, sbt_ref, tm)
        dbo = dc_ref[:, A_WIDTH:].astype(F32)
        dh_ref[:, 1024:1536] = (dbo * mixed * dgu).astype(BF16)
        dmx = dbo * u
        mask = _tril_mask()
        wst = [jnp.where(mask.T, mst_ref[g], 0.0).astype(BF16) for g in range(B_GROUPS)]
        qb = q.astype(BF16)
        dmb = dmx.astype(BF16)
        rows = []
        for c in range(nchunk):
            cols = []
            for g in range(B_GROUPS):
                rs_, cs_ = slice(c * B_CHUNK, (c + 1) * B_CHUNK), slice(g * 128, (g + 1) * 128)
                sbacc[g] += dmx[rs_, cs_]
                dms_ref[g] += _dot(dmb[rs_, cs_], qb[rs_, cs_], NT)
                cols.append(_dot(wst[g], dmb[rs_, cs_], NN))
            rows.append(jnp.concatenate(cols, axis=1))
        dq = jnp.concatenate(rows, axis=0)
        dbg_ref[...] += _colsum(dq * nb)
        dbb_ref[...] += _colsum(dq)
        dh_ref[:, 1536:2048] = (_ln_bwd_rows(dq * bg_ref[...], nb, rb) * dgw).astype(BF16)

        @pl.when(i == nt - 1)
        def _():
            for g in range(B_GROUPS):
                dms_ref[g] = jnp.where(mask, dms_ref[g], 0.0)
                dsb_ref[g] = jnp.sum(sbacc[g], axis=1, keepdims=True)

    last_blk = t // MIX_HALO - 1
    vec = pl.BlockSpec((1, A_WIDTH), lambda i: (0, 0))
    mat = pl.BlockSpec((B_GROUPS, B_CHUNK, B_CHUNK), lambda i: (0, 0, 0))
    taps = pl.BlockSpec((A_KERNEL, A_WIDTH), lambda i: (0, 0))

    def halo(width):
        return pl.BlockSpec((MIX_HALO, width), lambda i: (jnp.minimum((i + 1) * hb, last_blk), 0))

    vshape = jax.ShapeDtypeStruct((1, A_WIDTH), F32)
    outs, moved = _call(
        body, name=name, grid=(nt,),
        in_specs=[pl.BlockSpec((tm, 2048), lambda i: (i, 0)), pl.BlockSpec((tm, A_WIDTH), lambda i: (i, 0)), halo(A_WIDTH),
                  pl.BlockSpec((tm, D_MODEL), lambda i: (i, 0)), halo(D_MODEL),
                  taps, vec, vec, vec, vec, mat, mat, pl.BlockSpec((B_CHUNK, B_GROUPS), lambda i: (0, 0))],
        out_specs=[pl.BlockSpec((tm, 2048), lambda i: (i, 0)), taps, vec, vec, vec, vec, vec, mat,
                   pl.BlockSpec((B_GROUPS, B_CHUNK, 1), lambda i: (0, 0, 0))],
        out_shape=[jax.ShapeDtypeStruct((t, 2048), BF16), jax.ShapeDtypeStruct((A_KERNEL, A_WIDTH), F32),
                   vshape, vshape, vshape, vshape, vshape,
                   jax.ShapeDtypeStruct((B_GROUPS, B_CHUNK, B_CHUNK), F32), jax.ShapeDtypeStruct((B_GROUPS, B_CHUNK, 1), F32)],
        scratch_shapes=[pltpu.VMEM((SUBLANES, r, A_WIDTH), F32), pltpu.VMEM((B_GROUPS, B_CHUNK, B_CHUNK), F32)],
        sem=("arbitrary",), args=(h, y, y, dcat, dcat, cw, ag, ab, bg, bb, ms, mst, sbt), comm=comm)
    return outs if comm is None else (outs, moved)


Q_WIDTH = N_Q_HEADS * HEAD_DIM
KV_WIDTH = 2 * N_KV_HEADS * HEAD_DIM
PAIRS_PER_KV = N_Q_HEADS // N_KV_HEADS // 2
ATT_SCALE = 1.0 / math.sqrt(HEAD_DIM)


def _dup_heads(pair_cols, kv_head):
    lane = lax.broadcasted_iota(jnp.int32, pair_cols.shape, 1)
    rolled = pltpu.roll(pair_cols, HEAD_DIM, 1)
    first = lane < HEAD_DIM
    return jnp.where(first, pair_cols, rolled) if kv_head == 0 else jnp.where(first, rolled, pair_cols)


HEADS_PER_KV = N_Q_HEADS // N_KV_HEADS


def _stack_heads(ref, kh):
    lane = lax.broadcasted_iota(jnp.int32, (ATT_BLOCK, 128), 1)
    rows = []
    for pr in range(PAIRS_PER_KV):
        c0 = (kh * PAIRS_PER_KV + pr) * 128
        pair = ref[:, c0:c0 + 128]
        rows += [jnp.where(lane < HEAD_DIM, pair, jnp.zeros_like(pair)), jnp.where(lane < HEAD_DIM, jnp.zeros_like(pair), pair)]
    return jnp.concatenate(rows, axis=0)


def _unstack_heads(stacked, kh, write):
    lane = lax.broadcasted_iota(jnp.int32, (ATT_BLOCK, 128), 1)
    for pr in range(PAIRS_PER_KV):
        first = stacked[(2 * pr) * ATT_BLOCK:(2 * pr + 1) * ATT_BLOCK]
        second = stacked[(2 * pr + 1) * ATT_BLOCK:(2 * pr + 2) * ATT_BLOCK]
        write((kh * PAIRS_PER_KV + pr) * 128, jnp.where(lane < HEAD_DIM, first, second))


def _sink_row(sink_ref, kh):
    return jnp.concatenate([jnp.full((1, ATT_BLOCK), sink_ref[0, kh * HEADS_PER_KV + h], F32) for h in range(HEADS_PER_KV)], axis=1)


def _att_mask_t(n):
    sj = lax.broadcasted_iota(jnp.int32, (2 * ATT_BLOCK, HEADS_PER_KV * ATT_BLOCK), 0)
    qi = lax.broadcasted_iota(jnp.int32, (2 * ATT_BLOCK, HEADS_PER_KV * ATT_BLOCK), 1) & (ATT_BLOCK - 1)
    diff = qi + ATT_BLOCK - sj
    return (diff >= 0) & (diff < ATT_BLOCK) & ((n > 0) | (sj >= ATT_BLOCK))


def _att_probs_t(q_all, k2, mask_t, sink):
    st = _dot(k2, q_all, NT) * ATT_SCALE
    st = jnp.where(mask_t, st, -jnp.inf)
    m = jnp.maximum(jnp.max(st, axis=0, keepdims=True), sink)
    e = jnp.exp(st - m)
    es = jnp.exp(sink - m)
    inv = 1.0 / (jnp.sum(e, axis=0, keepdims=True) + es)
    return e * inv, es * inv


def _attn_fwd(qkv, sinks, name, comm=None):
    t = qkv.shape[0]
    nb = t // ATT_BLOCK
    kvb = Q_WIDTH // KV_WIDTH

    def body(sink_ref, q_ref, kv_ref, kvp_ref, o_ref):
        n = pl.program_id(0)
        mask_t = _att_mask_t(n)
        kv = jnp.concatenate([kvp_ref[...], kv_ref[...]], axis=0).astype(F32)

        def write(c0, pair):
            o_ref[:, c0:c0 + 128] = pair.astype(BF16)

        for kh in range(N_KV_HEADS):
            k2 = _dup_heads(kv[:, 0:128], kh).astype(BF16)
            v2 = _dup_heads(kv[:, 128:256], kh).astype(BF16)
            pt, _ = _att_probs_t(_stack_heads(q_ref, kh), k2, mask_t, _sink_row(sink_ref, kh))
            _unstack_heads(_dot(v2, pt, TN).T, kh, write)

    (out,), moved = _call(
        body, name=name, grid=(nb,),
        in_specs=[pl.BlockSpec(memory_space=pltpu.SMEM),
                  pl.BlockSpec((ATT_BLOCK, Q_WIDTH), lambda n: (n, 0)),
                  pl.BlockSpec((ATT_BLOCK, KV_WIDTH), lambda n: (n, kvb)),
                  pl.BlockSpec((ATT_BLOCK, KV_WIDTH), lambda n: (jnp.maximum(n - 1, 0), kvb))],
        out_specs=[pl.BlockSpec((ATT_BLOCK, Q_WIDTH), lambda n: (n, 0))],
        out_shape=[jax.ShapeDtypeStruct((t, Q_WIDTH), BF16)],
        sem=("parallel",), args=(sinks, qkv, qkv, qkv), comm=comm)
    return out if comm is None else (out, moved)


def _attn_bwd(qkv, d_o, sinks, name, comm=None):
    t = qkv.shape[0]
    nb = t // ATT_BLOCK
    kvb = Q_WIDTH // KV_WIDTH

    def body(sink_ref, q_ref, kv_ref, kvp_ref, do_ref, dq_ref, dkv_ref, dbq_ref, dbkv_ref, dsink_ref, carry):
        n = pl.program_id(0)

        @pl.when(n == 0)
        def _():
            for ref in (dbq_ref, dbkv_ref, dsink_ref, carry):
                ref[...] = jnp.zeros_like(ref)
            dkv_ref[...] = jnp.zeros_like(dkv_ref)

        @pl.when(n < nb)
        def _():
            mask_t = _att_mask_t(n)
            kv = jnp.concatenate([kvp_ref[...], kv_ref[...]], axis=0).astype(F32)
            lane2 = lax.broadcasted_iota(jnp.int32, (2 * ATT_BLOCK, 128), 1)
            sink_lane = lax.broadcasted_iota(jnp.int32, (1, 128), 1)
            dsink = jnp.zeros((1, 128), F32)
            dk_parts, dv_parts = [], []

            def write(c0, pair):
                dbq_ref[:, c0:c0 + 128] += _colsum(pair)
                dq_ref[:, c0:c0 + 128] = pair.astype(BF16)

            for kh in range(N_KV_HEADS):
                k2 = _dup_heads(kv[:, 0:128], kh).astype(BF16)
                v2 = _dup_heads(kv[:, 128:256], kh).astype(BF16)
                q_all = _stack_heads(q_ref, kh)
                do_all = _stack_heads(do_ref, kh)
                pt, ps = _att_probs_t(q_all, k2, mask_t, _sink_row(sink_ref, kh))
                dpt = _dot(v2, do_all, NT)
                delta = jnp.sum(pt * dpt, axis=0, keepdims=True)
                dst = pt * (dpt - delta) * ATT_SCALE
                psd = ps * delta
                for h in range(HEADS_PER_KV):
                    dsink = dsink + jnp.where(sink_lane == kh * HEADS_PER_KV + h,
                                              -jnp.sum(psd[:, h * ATT_BLOCK:(h + 1) * ATT_BLOCK]), 0.0)
                _unstack_heads(_dot(k2, dst, TN).T, kh, write)
                dk_acc = _dot(dst, q_all, NN)
                dv_acc = _dot(pt, do_all, NN)
                dk_parts.append(dk_acc + pltpu.roll(dk_acc, HEAD_DIM, 1))
                dv_parts.append(dv_acc + pltpu.roll(dv_acc, HEAD_DIM, 1))
            dk = jnp.where(lane2 < HEAD_DIM, dk_parts[0], dk_parts[1])
            dv = jnp.where(lane2 < HEAD_DIM, dv_parts[0], dv_parts[1])
            dkv_new = jnp.concatenate([dk, dv], axis=1)
            done = carry[...] + dkv_new[0:ATT_BLOCK]

            @pl.when(n > 0)
            def _():
                dkv_ref[...] = done.astype(BF16)
                dbkv_ref[...] += _colsum(done)

            carry[...] = dkv_new[ATT_BLOCK:]
            dsink_ref[...] += dsink

        @pl.when(n == nb)
        def _():
            dkv_ref[...] = carry[...].astype(BF16)
            dbkv_ref[...] += _colsum(carry[...])

    def clamp(n):
        return jnp.minimum(n, nb - 1)

    outs, moved = _call(
        body, name=name, grid=(nb + 1,),
        in_specs=[pl.BlockSpec(memory_space=pltpu.SMEM),
                  pl.BlockSpec((ATT_BLOCK, Q_WIDTH), lambda n: (clamp(n), 0)),
                  pl.BlockSpec((ATT_BLOCK, KV_WIDTH), lambda n: (clamp(n), kvb)),
                  pl.BlockSpec((ATT_BLOCK, KV_WIDTH), lambda n: (jnp.maximum(clamp(n) - 1, 0), kvb)),
                  pl.BlockSpec((ATT_BLOCK, Q_WIDTH), lambda n: (clamp(n), 0))],
        out_specs=[pl.BlockSpec((ATT_BLOCK, Q_WIDTH), lambda n: (clamp(n), 0)),
                   pl.BlockSpec((ATT_BLOCK, KV_WIDTH), lambda n: (jnp.maximum(n - 1, 0), 0)),
                   pl.BlockSpec((1, Q_WIDTH), lambda n: (0, 0)),
                   pl.BlockSpec((1, KV_WIDTH), lambda n: (0, 0)),
                   pl.BlockSpec((1, 128), lambda n: (0, 0))],
        out_shape=[jax.ShapeDtypeStruct((t, Q_WIDTH), BF16), jax.ShapeDtypeStruct((t, KV_WIDTH), BF16),
                   jax.ShapeDtypeStruct((1, Q_WIDTH), F32), jax.ShapeDtypeStruct((1, KV_WIDTH), F32),
                   jax.ShapeDtypeStruct((1, 128), F32)],
        scratch_shapes=[pltpu.VMEM((ATT_BLOCK, KV_WIDTH), F32)],
        sem=("arbitrary",), args=(sinks, qkv, qkv, qkv, d_o), comm=comm)
    return outs if comm is None else (outs, moved)


def _adamw_math(g, w, m, v):
    m = ADAM_B1 * m + (1.0 - ADAM_B1) * g
    v = ADAM_B2 * v + (1.0 - ADAM_B2) * (g * g)
    m_hat = m / (1.0 - ADAM_B1 ** ADAM_STEP)
    v_hat = v / (1.0 - ADAM_B2 ** ADAM_STEP)
    delta = -ADAM_LR * (m_hat / (jnp.sqrt(v_hat) + ADAM_EPS) + ADAM_WD * w)
    return delta, m, v


def _sum_partials(p_ref):
    g = p_ref[0].astype(F32)
    for s in range(1, N_DEV):
        g = g + p_ref[s].astype(F32)
    return g


def _adamw_big(parts, w, m, v, name, tr):
    r, c = w.shape
    parts = [p if isinstance(p, tuple) else (p, 0, p.shape[1]) for p in parts]
    tiles = [rows // tr for _, _, rows in parts]
    starts = [sum(tiles[:l]) for l in range(len(parts))]
    assert all(lo % tr == 0 and rows % tr == 0 for _, lo, rows in parts) and sum(tiles) * tr == r

    def body(*refs):
        p_refs, (w_ref, m_ref, v_ref, g_out, d_out, m_out, v_out) = refs[:len(parts)], refs[len(parts):]
        i = pl.program_id(0)
        for l, p_ref in enumerate(p_refs):
            @pl.when((i >= starts[l]) & (i < starts[l] + tiles[l]))
            def _():
                g = _sum_partials(p_ref)
                g_out[...] = g
                d_out[...], m_out[...], v_out[...] = _adamw_math(g, w_ref[...], m_ref[...], v_ref[...])

    def part_spec(l):
        return pl.BlockSpec((N_DEV, tr, c), lambda i: (0, jnp.clip(i - starts[l], 0, tiles[l] - 1) + parts[l][1] // tr, 0))

    tile = pl.BlockSpec((tr, c), lambda i: (i, 0))
    shape = jax.ShapeDtypeStruct((r, c), F32)
    return pl.pallas_call(
        body, name=name, grid=(r // tr,),
        in_specs=[part_spec(l) for l in range(len(parts))] + [tile, tile, tile],
        out_specs=[tile] * 4, out_shape=[shape] * 4,
        compiler_params=_params("parallel"),
    )(*[p[0] for p in parts], w, m, v)


def _adamw_small(parts, ws, ms, vs, name):
    n = len(ws)

    def body(*refs):
        ins, outs = refs[:4 * n], refs[4 * n:]
        for a in range(n):
            g = _sum_partials(ins[a])
            outs[4 * a][...] = g
            outs[4 * a + 1][...], outs[4 * a + 2][...], outs[4 * a + 3][...] = _adamw_math(
                g, ins[n + a][...], ins[2 * n + a][...], ins[3 * n + a][...])

    out_shape = []
    for w in ws:
        out_shape += [jax.ShapeDtypeStruct(w.shape, F32)] * 4
    return pl.pallas_call(body, name=name, out_shape=out_shape, compiler_params=_params())(*parts, *ws, *ms, *vs)


PACK_LANES = 128
PACK_ROWS = 8


def _pack(arrs):
    flat = jnp.concatenate([a.reshape(-1).astype(F32) for a in arrs])
    unit = PACK_LANES * PACK_ROWS
    total = -(-flat.shape[0] // unit) * unit
    return jnp.pad(flat, (0, total - flat.shape[0])).reshape(-1, PACK_LANES)


def _unpack(buf, shapes):
    flat = buf.reshape(N_DEV, -1)
    out, pos = [], 0
    for s in shapes:
        size = math.prod(s)
        out.append(flat[:, pos:pos + size].reshape((N_DEV,) + tuple(s)))
        pos += size
    return out


def _interleave(g):
    return jnp.transpose(g, (1, 0, 2)).reshape(g.shape[1], -1)


def _deinterleave(w):
    r = w.shape[0]
    return jnp.transpose(w.reshape(r, N_DEV, -1), (1, 0, 2))


def _ffn_backward(dz, x_in, z_in, g_in, h, cg, cv, u, w_up_t, cw, w_down, tag, exchange=(), exchange_late=(), own_rows=0):
    du = _matmul(dz, w_down, "nt", BF16, f"ffn{tag}_du", 1024, 1408, 1024)
    d_w_down = _matmul(u, dz, "tn", BF16, f"ffn{tag}_dwdown", 1408, 1024, 2048)
    (dhg, dhv, dcwg, dcwv, dcbg, dcbv), moved = _ffn_mid_bwd(
        h, cg, cv, du, cw, f"ffn{tag}_mid_bwd", comm=_Comm(exchange=[d_w_down.reshape(N_DEV, -1, D_MODEL), *exchange]))
    d_w_up_t = _matmul_tn_pair(dhg, dhv, x_in, BF16, f"ffn{tag}_dwup", 1408, 1024, 1024,
                               comm=_Comm(exchange=exchange_late) if exchange_late else None)
    if exchange_late:
        d_w_up_t, late = d_w_up_t
        moved = moved + late
    d_up_blocks = d_w_up_t.reshape(N_DEV, -1, D_MODEL)
    dx = _matmul(dhv, w_up_t, "nn", F32, f"ffn{tag}_dx_value", 1024, 1024, D_FF, b_off=1,
                 comm=_Comm(exchange=[(d_up_blocks, 0, own_rows)]) if own_rows else None)
    if own_rows:
        dx, own = dx
        moved = moved + own
    dz_in, dg_in, db_in = _matmul_ln_bwd(dhg, w_up_t, z_in, g_in, dz, f"ffn{tag}_dx_gate_ln_bwd", 512, res=dx)
    return (dz_in, dg_in, db_in, d_up_blocks,
            jnp.concatenate([dcwg, dcwv], axis=1), jnp.concatenate([dcbg, dcbv], axis=1), moved)


def kernel(x, ab_w_in, a_conv_w, a_conv_b, a_norm_g, a_norm_b, b_norm_g, b_norm_b, b_spatial_w, b_spatial_b, ab_w_out, c_w_qkv, c_b_qkv, c_sinks, c_w_o, ffn_w_up, ffn_conv_w, ffn_conv_b, ffn_w_down, ln_g, ln_b, loss_target, m_ab_w_in, m_a_conv_w, m_a_conv_b, m_a_norm_g, m_a_norm_b, m_b_norm_g, m_b_norm_b, m_b_spatial_w, m_b_spatial_b, m_ab_w_out, m_c_w_qkv, m_c_b_qkv, m_c_sinks, m_c_w_o, m_ffn_w_up, m_ffn_conv_w, m_ffn_conv_b, m_ffn_w_down, m_ln_g, m_ln_b, v_ab_w_in, v_a_conv_w, v_a_conv_b, v_a_norm_g, v_a_norm_b, v_b_norm_g, v_b_norm_b, v_b_spatial_w, v_b_spatial_b, v_ab_w_out, v_c_w_qkv, v_c_b_qkv, v_c_sinks, v_c_w_o, v_ffn_w_up, v_ffn_conv_w, v_ffn_conv_b, v_ffn_w_down, v_ln_g, v_ln_b):
    me = 4 * lax.axis_index("x") + 2 * lax.axis_index("y") + lax.axis_index("c")
    xt = x[0]
    t = xt.shape[0]

    small_shard_shapes = [a_conv_w.shape, c_b_qkv.shape, ffn_conv_w.shape, ln_g.shape, ln_b.shape]
    up_shard = [jnp.swapaxes(ffn_w_up[l], 0, 1).astype(BF16) for l in range(2)]
    qkv_shard = jnp.swapaxes(c_w_qkv[0], 0, 1).astype(BF16)
    down_shard = [ffn_w_down[l].astype(BF16) for l in range(2)]
    g_win, g_small = _comm_only(
        _Comm(gather=[ab_w_in[0].astype(BF16), _pack([a_conv_w, c_b_qkv, ffn_conv_w, ln_g, ln_b])]), "gather_first")
    w_in = _interleave(g_win)
    g_acw, g_bqkv, g_fcw, g_lng, g_lnb = _unpack(g_small, small_shard_shapes)
    acw = _interleave(g_acw[:, 0])
    bqkv = g_bqkv[:, 0].reshape(1, -1)
    fcw = [_interleave(g_fcw[:, l]) for l in range(2)]
    lng = jnp.transpose(g_lng, (1, 2, 0, 3)).reshape(2, 2, 1, D_MODEL)
    lnb = jnp.transpose(g_lnb, (1, 2, 0, 3)).reshape(2, 2, 1, D_MODEL)
    fcb = [ffn_conv_b[l:l + 1] for l in range(2)]
    ms = b_spatial_w[0]
    mst = jnp.swapaxes(ms, 1, 2)
    sbt = b_spatial_b[0].T

    h0, (g_wout,) = _matmul(xt, w_in, "nn", BF16, "mix_in", 1024, 1024, 1024, comm=_Comm(gather=[ab_w_out[0].astype(BF16)]))
    w_out = g_wout.reshape(D_MODEL, D_MODEL)
    (cat, y0), (g_wup0,) = _mixer_mid_fwd(h0, acw, a_conv_b, a_norm_g, a_norm_b, b_norm_g, b_norm_b, ms, sbt, "mix_mid_fwd",
                                    comm=_Comm(gather=[up_shard[0]]))
    w_up0 = g_wup0.reshape(2 * D_FF, D_MODEL)
    z1, x1 = _matmul_res_ln(cat, w_out, xt, lng[0, 0], lnb[0, 0], "mix_out_ln", 512)
    hf0, (g_wdown0, g_wqkv) = _matmul(x1, w_up0, "nt", BF16, "ffn0_up", 1024, 1408, 1024,
                                      comm=_Comm(gather=[down_shard[0], qkv_shard]))
    w_down0 = g_wdown0.reshape(D_FF, D_MODEL)
    w_qkv = g_wqkv.reshape(Q_WIDTH + KV_WIDTH, D_MODEL)
    (u0, cg0, cv0), (g_wup1,) = _ffn_mid_fwd(hf0, fcw[0], fcb[0], "ffn0_mid_fwd", comm=_Comm(gather=[up_shard[1]]))
    w_up1 = g_wup1.reshape(2 * D_FF, D_MODEL)
    (z2, x2), (g_wo,) = _matmul_res_ln(u0, w_down0, z1, lng[0, 1], lnb[0, 1], "ffn0_down_ln", 512, prev=(lng[0, 0], lnb[0, 0]),
                                       comm=_Comm(gather=[c_w_o[0].astype(BF16)]))
    w_o = g_wo.reshape(D_MODEL, D_MODEL)
    qkv = _matmul(x2, w_qkv, "nt", BF16, "att_qkv", 1024, 1280, 1024, bias=bqkv)
    att, (g_wdown1,) = _attn_fwd(qkv, c_sinks, "att_fwd", comm=_Comm(gather=[down_shard[1]]))
    w_down1 = g_wdown1.reshape(D_FF, D_MODEL)
    z3, x3 = _matmul_res_ln(att, w_o, z2, lng[1, 0], lnb[1, 0], "att_out_ln", 512, prev=(lng[0, 1], lnb[0, 1]))
    hf1 = _matmul(x3, w_up1, "nt", BF16, "ffn1_up", 1024, 1408, 1024)
    u1, cg1, cv1 = _ffn_mid_fwd(hf1, fcw[1], fcb[1], "ffn1_mid_fwd")

    dz4, dg11, db11, loss_terms = _matmul_res_ln_loss(u1, w_down1, z3, lng[1, 1], lnb[1, 1], loss_target[0],
                                                      "ffn1_down_ln_loss", 512, prev=(lng[1, 0], lnb[1, 0]))
    dz3, dg10, db10, d_wup1, d_fcw1, d_fcb1, (p_wdown1,) = _ffn_backward(
        dz4, x3, z3, lng[1, 0], hf1, cg1, cv1, u1, w_up1, fcw[1], w_down1, 1)
    d_att = _matmul(dz3, w_o, "nt", BF16, "att_dout", 1024, 1024, 1024)
    d_wo = _matmul(att, dz3, "tn", BF16, "att_dwo", 1024, 1024, 512)
    rows_up = d_wup1.shape[1]
    first = 3 * rows_up // 4
    (dq, dkv, dbq, dbkv, dsinks), (p_wup1a,) = _attn_bwd(qkv, d_att, c_sinks, "att_bwd",
                                                        comm=_Comm(exchange=[(d_wup1, 0, first)]))
    d_wqkv = jnp.concatenate([_matmul(dq, x2, "tn", BF16, "att_dwq", 1024, 1024, 1024),
                              _matmul(dkv, x2, "tn", BF16, "att_dwkv", KV_WIDTH, 1024, 1024)], axis=0)
    dx2 = _matmul(dkv, w_qkv, "nn", F32, "att_dx_kv", 1024, 1024, KV_WIDTH, b_off=Q_WIDTH // KV_WIDTH)
    dz2, dg01, db01 = _matmul_ln_bwd(dq, w_qkv, z2, lng[0, 1], dz3, "att_dx_q_ln_bwd", 512, res=dx2)
    early = rows_up // 4
    dz1, dg00, db00, d_wup0, d_fcw0, d_fcb0, (p_wdown0, p_wup1b, p_wqkv, p_wo, p_wup0a) = _ffn_backward(
        dz2, x1, z1, lng[0, 0], hf0, cg0, cv0, u0, w_up0, fcw[0], w_down0, 0, exchange=[(d_wup1, first, rows_up - first)],
        exchange_late=[d_wqkv.reshape(N_DEV, -1, D_MODEL), d_wo.reshape(N_DEV, -1, D_MODEL)], own_rows=early)
    dcat = _matmul(dz1, w_out, "nt", BF16, "mix_dcat", 1024, 1024, 1024)
    d_wout = _matmul(cat, dz1, "tn", BF16, "mix_dwout", 1024, 1024, 512)
    (dh0, d_acw, d_acb, d_ang, d_anb, d_bng, d_bnb, d_ms, d_sb), (p_wup0b, p_wout) = _mixer_mid_bwd(
        h0, y0, dcat, acw, a_norm_g, a_norm_b, b_norm_g, b_norm_b, ms, mst, sbt, "mix_mid_bwd",
        comm=_Comm(exchange=[(d_wup0, early, rows_up - early), d_wout.reshape(N_DEV, -1, D_MODEL)]))
    d_bqkv = jnp.concatenate([dbq, dbkv], axis=1)
    d_lng = jnp.stack([jnp.stack([dg00, dg01]), jnp.stack([dg10, dg11])])
    d_lnb = jnp.stack([jnp.stack([db00, db01]), jnp.stack([db10, db11])])
    small_full = [d_acb, d_ang, d_anb, d_bng, d_bnb, d_ms, d_sb, dsinks[:, :N_Q_HEADS], jnp.concatenate([d_fcb0, d_fcb1], axis=0),
                  d_acw, d_bqkv, jnp.stack([d_fcw0, d_fcw1]), d_lng, d_lnb, loss_terms]
    d_win, (g_small_grads,) = _matmul(xt, dh0, "tn", BF16, "mix_dwin", 1024, 1024, 512, comm=_Comm(gather=[_pack(small_full)]))
    grad_x, (p_win,) = _matmul(dh0, w_in, "nt", F32, "mix_dx", 1024, 1024, 1024, res=dz1, res_scale=ALPHA,
                               comm=_Comm(exchange=[_deinterleave(d_win)]))


    big = {}
    for nm, p, w, m, v, tr, transposed in [
            ("ab_w_in", [p_win], ab_w_in, m_ab_w_in, v_ab_w_in, 256, False),
            ("ab_w_out", [p_wout], ab_w_out, m_ab_w_out, v_ab_w_out, 128, False),
            ("c_w_qkv", [p_wqkv], c_w_qkv, m_c_w_qkv, v_c_w_qkv, 160, True), ("c_w_o", [p_wo], c_w_o, m_c_w_o, v_c_w_o, 128, False),
            ("ffn_w_up", [(p_wup0a, 0, early), (p_wup0b, early, rows_up - early), (p_wup1a, 0, first), (p_wup1b, first, rows_up - first)], ffn_w_up, m_ffn_w_up, v_ffn_w_up, 176, True),
            ("ffn_w_down", [p_wdown0, p_wdown1], ffn_w_down, m_ffn_w_down, v_ffn_w_down, 176, False)]:
        def two_d(a):
            a = jnp.swapaxes(a, 1, 2) if transposed else a
            return a.reshape(-1, a.shape[-1])

        def back(o):
            return jnp.swapaxes(o.reshape(w.shape[0], w.shape[2], w.shape[1]), 1, 2) if transposed else o.reshape(w.shape)

        outs = _adamw_big(p, two_d(w), two_d(m), two_d(v), "adamw_" + nm, tr)
        big[nm] = [back(o) for o in outs]

    *gs, loss_parts = _unpack(g_small_grads, [a.shape for a in small_full])
    loss = 0.5 / D_MODEL * jnp.sum(loss_parts)

    def my_shard(g, width):
        g = g.reshape(g.shape[:-1] + (N_DEV, width))
        return lax.dynamic_index_in_dim(g, me, axis=g.ndim - 2, keepdims=False)

    small_names = ["a_conv_b", "a_norm_g", "a_norm_b", "b_norm_g", "b_norm_b", "b_spatial_w", "b_spatial_b", "c_sinks", "ffn_conv_b",
                   "a_conv_w", "c_b_qkv", "ffn_conv_w", "ln_g", "ln_b"]
    small_w = [a_conv_b, a_norm_g, a_norm_b, b_norm_g, b_norm_b, b_spatial_w, b_spatial_b, c_sinks, ffn_conv_b,
               a_conv_w, c_b_qkv, ffn_conv_w, ln_g, ln_b]
    small_m = [m_a_conv_b, m_a_norm_g, m_a_norm_b, m_b_norm_g, m_b_norm_b, m_b_spatial_w, m_b_spatial_b, m_c_sinks, m_ffn_conv_b,
               m_a_conv_w, m_c_b_qkv, m_ffn_conv_w, m_ln_g, m_ln_b]
    small_v = [v_a_conv_b, v_a_norm_g, v_a_norm_b, v_b_norm_g, v_b_norm_b, v_b_spatial_w, v_b_spatial_b, v_c_sinks, v_ffn_conv_b,
               v_a_conv_w, v_c_b_qkv, v_ffn_conv_w, v_ln_g, v_ln_b]
    gs[9:] = [my_shard(g, w.shape[-1]) for g, w in zip(gs[9:], small_w[9:])]
    two_d = [(-1, w.shape[-1]) for w in small_w]
    outs = _adamw_small([g.reshape((N_DEV,) + w.reshape(s).shape) for g, w, s in zip(gs, small_w, two_d)],
                        [w.reshape(s) for w, s in zip(small_w, two_d)], [m.reshape(s) for m, s in zip(small_m, two_d)],
                        [v.reshape(s) for v, s in zip(small_v, two_d)], "adamw_small")
    small = {nm: [o.reshape(w.shape) for o in outs[4 * a:4 * a + 4]] for a, (nm, w) in enumerate(zip(small_names, small_w))}

    res = {**big, **small}
    order = ["ab_w_in", "a_conv_w", "a_conv_b", "a_norm_g", "a_norm_b", "b_norm_g", "b_norm_b", "b_spatial_w", "b_spatial_b", "ab_w_out",
             "c_w_qkv", "c_b_qkv", "c_sinks", "c_w_o", "ffn_w_up", "ffn_conv_w", "ffn_conv_b", "ffn_w_down", "ln_g", "ln_b"]
    return (loss, grad_x[None], *[res[nm][0] for nm in order], *[res[nm][1] for nm in order],
            *[res[nm][2] for nm in order], *[res[nm][3] for nm in order])
```

```python
import functools
import math

import jax
import jax.numpy as jnp
from jax import lax
from jax.experimental import pallas as pl
from jax.experimental.pallas import tpu as pltpu

F32 = jnp.float32
BF16 = jnp.bfloat16

N_DEV = 8
D_MODEL = 1024
A_WIDTH = 512
A_KERNEL = 31
B_GROUPS = 4
B_CHUNK = 128
HEAD_DIM = 64
N_Q_HEADS = 16
N_KV_HEADS = 2
ATT_BLOCK = 128
D_FF = 2816
FFN_KERNEL = 3
ALPHA = (2.0 * 2) ** 0.25
LN_EPS = 1e-5
GELU_K = math.sqrt(2.0 / math.pi)
GELU_C = 0.044715
ADAM_LR = 0.001
ADAM_B1 = 0.9
ADAM_B2 = 0.999
ADAM_EPS = 1e-08
ADAM_WD = 0.01
ADAM_STEP = 10
VMEM_LIMIT = 56 * 1024 * 1024
MESH_ID = pl.DeviceIdType.MESH


def _params(*sem):
    return pltpu.CompilerParams(dimension_semantics=sem, vmem_limit_bytes=VMEM_LIMIT)


def _gelu(x):
    t = jnp.tanh(GELU_K * x * (1.0 + GELU_C * x * x))
    return 0.5 * x * (1.0 + t)


def _gelu_and_grad(x):
    x2 = x * x
    t = jnp.tanh(GELU_K * x * (1.0 + GELU_C * x2))
    g = 0.5 * x * (1.0 + t)
    dg = 0.5 * (1.0 + t) + 0.5 * x * (1.0 - t * t) * (GELU_K * (1.0 + 3.0 * GELU_C * x2))
    return g, dg


def _sigmoid(x):
    return 1.0 / (1.0 + jnp.exp(-x))


def _ln_stats(z):
    mu = jnp.mean(z, axis=-1, keepdims=True)
    zc = z - mu
    var = jnp.mean(zc * zc, axis=-1, keepdims=True)
    r = lax.rsqrt(var + LN_EPS)
    return zc * r, r


def _ln_bwd_rows(dn, nh, r):
    return r * (dn - jnp.mean(dn, axis=-1, keepdims=True) - nh * jnp.mean(dn * nh, axis=-1, keepdims=True))


def _colsum(x):
    return jnp.sum(x, axis=0, keepdims=True)


def _dot(a, b, dims):
    return lax.dot_general(a.astype(BF16), b.astype(BF16), (dims, ((), ())), preferred_element_type=F32)


NN = ((1,), (0,))
NT = ((1,), (1,))
TN = ((0,), (0,))


ANY = pl.BlockSpec(memory_space=pl.ANY)
N_RELATIONS = N_DEV - 1


def _my_place():
    return lax.axis_index("x"), lax.axis_index("y"), lax.axis_index("c")


class _Comm:
    def __init__(self, gather=(), exchange=()):
        exchange = [e if isinstance(e, tuple) else (e, 0, e.shape[1]) for e in exchange]
        self.arrs = list(gather) + [e[0] for e in exchange]
        self.n_gather = len(gather)
        self.n = len(self.arrs)
        self.rows = [None] * self.n_gather + [pl.ds(lo, n) for _, lo, n in exchange]

    def out_shape(self):
        return [jax.ShapeDtypeStruct(((N_DEV,) + a.shape) if i < self.n_gather else a.shape, a.dtype)
                for i, a in enumerate(self.arrs)]

    def sems(self):
        return [pltpu.SemaphoreType.DMA((self.n, N_RELATIONS)), pltpu.SemaphoreType.DMA((self.n, N_RELATIONS)),
                pltpu.SemaphoreType.DMA((self.n,))]

    def _gather_copy(self, ins, outs, sems, a, k, place, to, from_input=False):
        px, py, pc = place
        block = outs[a].at[4 * px + 2 * py + pc]
        return pltpu.make_async_remote_copy(
            src_ref=ins[a] if from_input else block, dst_ref=block,
            send_sem=sems[0].at[a, k], recv_sem=sems[1].at[a, k], device_id=to, device_id_type=MESH_ID)

    def _exchange_copy(self, ins, outs, sems, a, k, landing=False):
        x, y, c = _my_place()
        me = 4 * x + 2 * y + c
        peer = (x ^ (k >> 2), y ^ ((k >> 1) & 1), c ^ (k & 1))
        return pltpu.make_async_remote_copy(
            src_ref=ins[a].at[me ^ k, self.rows[a]], dst_ref=outs[a].at[(me ^ k) if landing else me, self.rows[a]],
            send_sem=sems[0].at[a, k - 1], recv_sem=sems[1].at[a, k - 1], device_id=peer, device_id_type=MESH_ID)

    def _local_copy(self, ins, outs, sems, a):
        x, y, c = _my_place()
        me = 4 * x + 2 * y + c
        if a < self.n_gather:
            return pltpu.make_async_copy(ins[a], outs[a].at[me], sems[2].at[a])
        return pltpu.make_async_copy(ins[a].at[me, self.rows[a]], outs[a].at[me, self.rows[a]], sems[2].at[a])

    def _first_stage(self, ins, outs, sems, a):
        x, y, c = _my_place()
        me = (x, y, c)
        chips = [(1 - x, y), (x, 1 - y), (1 - x, 1 - y)]
        return ([self._gather_copy(ins, outs, sems, a, 0, me, (x, y, 1 - c), from_input=True)]
                + [self._gather_copy(ins, outs, sems, a, 1 + j, me, (*chip, c), from_input=True) for j, chip in enumerate(chips)])

    def start(self, ins, outs, sems):
        for a in range(self.n):
            self._local_copy(ins, outs, sems, a).start()
        for a in range(self.n_gather):
            for cp in self._first_stage(ins, outs, sems, a):
                cp.start()
        for k in range(1, N_DEV):
            for a in range(self.n_gather, self.n):
                self._exchange_copy(ins, outs, sems, a, k).start()

    def forward(self, ins, outs, sems):
        x, y, c = _my_place()
        me, sibling = (x, y, c), (x, y, 1 - c)
        for j, chip in enumerate([(1 - x, y), (x, 1 - y), (1 - x, 1 - y)]):
            for a in range(self.n_gather):
                self._gather_copy(ins, outs, sems, a, 1 + j, (*chip, c), me).wait_recv()
                self._gather_copy(ins, outs, sems, a, 4 + j, (*chip, c), sibling).start()

    def finish(self, ins, outs, sems):
        x, y, c = _my_place()
        me, sibling = (x, y, c), (x, y, 1 - c)
        chips = [(1 - x, y), (x, 1 - y), (1 - x, 1 - y)]
        passed = [self._gather_copy(ins, outs, sems, a, 4 + j, (*chip, c), sibling)
                  for j, chip in enumerate(chips) for a in range(self.n_gather)]
        for a in range(self.n_gather):
            self._gather_copy(ins, outs, sems, a, 0, sibling, me).wait_recv()
            for j, chip in enumerate(chips):
                self._gather_copy(ins, outs, sems, a, 4 + j, (*chip, 1 - c), me).wait_recv()
        for k in range(1, N_DEV):
            for a in range(self.n_gather, self.n):
                self._exchange_copy(ins, outs, sems, a, k, landing=True).wait_recv()
        for a in range(self.n_gather):
            for cp in self._first_stage(ins, outs, sems, a):
                cp.wait_send()
        for cp in passed:
            cp.wait_send()
        for k in range(1, N_DEV):
            for a in range(self.n_gather, self.n):
                self._exchange_copy(ins, outs, sems, a, k).wait_send()
        for a in range(self.n):
            self._local_copy(ins, outs, sems, a).wait()


def _comm_only(comm, name):
    def body(*refs):
        ins, outs, sems = refs[:comm.n], refs[comm.n:2 * comm.n], refs[2 * comm.n:]
        comm.start(ins, outs, sems)
        comm.forward(ins, outs, sems)
        comm.finish(ins, outs, sems)

    return pl.pallas_call(body, name=name, in_specs=[ANY] * comm.n, out_specs=[ANY] * comm.n,
                          out_shape=comm.out_shape(), scratch_shapes=comm.sems())(*comm.arrs)


def _call(body, *, name, grid, in_specs, out_specs, out_shape, args, sem, scratch_shapes=(), comm=None):
    in_specs, out_specs, out_shape, scratch_shapes = list(in_specs), list(out_specs), list(out_shape), list(scratch_shapes)
    if comm is None:
        outs = pl.pallas_call(body, name=name, grid=grid, in_specs=in_specs, out_specs=out_specs, out_shape=out_shape,
                              scratch_shapes=scratch_shapes, compiler_params=_params(*sem))(*args)
        return list(outs), []
    n_in, n_out, n_scr, nc = len(in_specs), len(out_specs), len(scratch_shapes), comm.n

    def wrapped(*refs):
        ins, refs = refs[:n_in], refs[n_in:]
        c_in, refs = refs[:nc], refs[nc:]
        outs, refs = refs[:n_out], refs[n_out:]
        c_out, refs = refs[:nc], refs[nc:]
        scr, sems = refs[:n_scr], refs[n_scr:]
        step = functools.reduce(lambda acc, ax: acc * grid[ax] + pl.program_id(ax), range(len(grid)), 0)
        steps = math.prod(grid)

        @pl.when(step == 0)
        def _():
            comm.start(c_in, c_out, sems)

        @pl.when(step == steps - 1)
        def _():
            comm.forward(c_in, c_out, sems)

        body(*ins, *outs, *scr)

        @pl.when(step == steps - 1)
        def _():
            comm.finish(c_in, c_out, sems)

    outs = pl.pallas_call(
        wrapped, name=name, grid=grid, in_specs=in_specs + [ANY] * nc, out_specs=out_specs + [ANY] * nc,
        out_shape=out_shape + comm.out_shape(), scratch_shapes=scratch_shapes + comm.sems(),
        compiler_params=_params(*(["arbitrary"] * len(grid))))(*args, *comm.arrs)
    return list(outs[:n_out]), list(outs[n_out:])


def _matmul(a, b, mode, out_dtype, name, tm, tn, tk, *, bias=None, res=None, res_scale=1.0, b_off=0, comm=None):
    tm = min(tm, a.shape[1] if mode == "tn" else a.shape[0])
    tk = min(tk, a.shape[0] if mode == "tn" else a.shape[1])
    if mode == "nn":
        (m, k), n = a.shape, b.shape[1]
        a_spec = pl.BlockSpec((tm, tk), lambda i, j, kk: (i, kk))
        b_spec = pl.BlockSpec((tk, tn), lambda i, j, kk: (kk + b_off, j))
        dims = NN
    elif mode == "nt":
        (m, k), n = a.shape, b.shape[0]
        a_spec = pl.BlockSpec((tm, tk), lambda i, j, kk: (i, kk))
        b_spec = pl.BlockSpec((tn, tk), lambda i, j, kk: (j, kk + b_off))
        dims = NT
    else:
        (k, m), n = a.shape, b.shape[1]
        a_spec = pl.BlockSpec((tk, tm), lambda i, j, kk: (kk, i))
        b_spec = pl.BlockSpec((tk, tn), lambda i, j, kk: (kk, j))
        dims = TN
    assert m % tm == 0 and n % tn == 0 and k % tk == 0, (name, m, n, k)
    nk = k // tk
    in_specs = [a_spec, b_spec]
    args = [a, b]
    if bias is not None:
        in_specs.append(pl.BlockSpec((1, tn), lambda i, j, kk: (0, j)))
        args.append(bias)
    if res is not None:
        in_specs.append(pl.BlockSpec((tm, tn), lambda i, j, kk: (i, j)))
        args.append(res)

    def finish(out, refs, o_ref):
        pos = 2
        if bias is not None:
            out = out + refs[pos][...]
            pos += 1
        if res is not None:
            out = out + res_scale * refs[pos][...].astype(F32)
        o_ref[...] = out.astype(out_dtype)

    def body_one_step(*refs):
        finish(_dot(refs[0][...], refs[1][...], dims), refs, refs[-1])

    def body(*refs):
        a_ref, b_ref = refs[0], refs[1]
        o_ref, acc = refs[-2], refs[-1]
        kk = pl.program_id(2)

        @pl.when(kk == 0)
        def _():
            acc[...] = jnp.zeros_like(acc)

        acc[...] += _dot(a_ref[...], b_ref[...], dims)

        @pl.when(kk == nk - 1)
        def _():
            finish(acc[...], refs, o_ref)

    (out,), moved = _call(
        body_one_step if nk == 1 else body, name=name, grid=(m // tm, n // tn, nk),
        in_specs=in_specs, out_specs=[pl.BlockSpec((tm, tn), lambda i, j, kk: (i, j))],
        out_shape=[jax.ShapeDtypeStruct((m, n), out_dtype)],
        scratch_shapes=[] if nk == 1 else [pltpu.VMEM((tm, tn), F32)],
        sem=("parallel", "parallel", "arbitrary"), args=args, comm=comm)
    return out if comm is None else (out, moved)


def _matmul_tn_pair(a0, a1, b, out_dtype, name, tm, tn, tk, comm=None):
    (k, m), n = a0.shape, b.shape[1]
    tk = min(tk, k)
    assert a1.shape == a0.shape and m % tm == 0 and n % tn == 0 and k % tk == 0, (name, m, n, k)
    mi, nk = m // tm, k // tk

    def body(a0_ref, a1_ref, b_ref, o_ref, acc):
        i, kk = pl.program_id(0), pl.program_id(2)

        @pl.when(kk == 0)
        def _():
            acc[...] = jnp.zeros_like(acc)

        @pl.when(i < mi)
        def _():
            acc[...] += _dot(a0_ref[...], b_ref[...], TN)

        @pl.when(i >= mi)
        def _():
            acc[...] += _dot(a1_ref[...], b_ref[...], TN)

        @pl.when(kk == nk - 1)
        def _():
            o_ref[...] = acc[...].astype(out_dtype)

    (out,), moved = _call(
        body, name=name, grid=(2 * mi, n // tn, nk),
        in_specs=[pl.BlockSpec((tk, tm), lambda i, j, kk: (jnp.where(i < mi, kk, nk - 1), jnp.minimum(i, mi - 1))),
                  pl.BlockSpec((tk, tm), lambda i, j, kk: (jnp.where(i >= mi, kk, 0), jnp.maximum(i - mi, 0))),
                  pl.BlockSpec((tk, tn), lambda i, j, kk: (kk, j))],
        out_specs=[pl.BlockSpec((tm, tn), lambda i, j, kk: (i, j))],
        out_shape=[jax.ShapeDtypeStruct((2 * m, n), out_dtype)],
        scratch_shapes=[pltpu.VMEM((tm, tn), F32)],
        sem=("parallel", "parallel", "arbitrary"), args=(a0, a1, b), comm=comm)
    return out if comm is None else (out, moved)


def _residual_input(x_ref, prev_refs):
    if not prev_refs:
        return x_ref[...]
    nh, _ = _ln_stats(x_ref[...])
    return nh * prev_refs[0][...] + prev_refs[1][...]


def _matmul_res_ln(a, b, x, g, beta, name, tm, prev=None, comm=None):
    t, k = a.shape
    d = b.shape[1]
    tm = min(tm, t)
    assert t % tm == 0
    n_prev = 0 if prev is None else 2

    def body(a_ref, b_ref, x_ref, g_ref, beta_ref, *rest):
        z_ref, xo_ref = rest[n_prev:]
        z = ALPHA * _residual_input(x_ref, rest[:n_prev]) + _dot(a_ref[...], b_ref[...], NN)
        nh, _ = _ln_stats(z)
        z_ref[...] = z
        xo_ref[...] = (nh * g_ref[...] + beta_ref[...]).astype(BF16)

    row = pl.BlockSpec((tm, d), lambda i: (i, 0))
    vec = pl.BlockSpec((1, d), lambda i: (0, 0))
    outs, moved = _call(
        body, name=name, grid=(t // tm,),
        in_specs=[pl.BlockSpec((tm, k), lambda i: (i, 0)), pl.BlockSpec((k, d), lambda i: (0, 0)), row, vec, vec] + [vec] * n_prev,
        out_specs=[row, row],
        out_shape=[jax.ShapeDtypeStruct((t, d), F32), jax.ShapeDtypeStruct((t, d), BF16)],
        sem=("parallel",), args=(a, b, x, g, beta, *(prev or ())), comm=comm)
    return outs if comm is None else (outs, moved)


def _matmul_ln_bwd(a, b, z, g, dres, name, tm, *, res=None, b_off=0):
    m, k = a.shape
    d = b.shape[1]
    tm = min(tm, m)
    assert m % tm == 0

    def body(*refs):
        a_ref, b_ref, z_ref, g_ref, dres_ref = refs[:5]
        dz_ref, dg_ref, db_ref = refs[-3:]

        @pl.when(pl.program_id(0) == 0)
        def _():
            dg_ref[...] = jnp.zeros_like(dg_ref)
            db_ref[...] = jnp.zeros_like(db_ref)

        dbr = _dot(a_ref[...], b_ref[...], NN)
        if res is not None:
            dbr = dbr + refs[5][...]
        nh, r = _ln_stats(z_ref[...])
        dy = ALPHA * dres_ref[...] + dbr
        dg_ref[...] += _colsum(dy * nh)
        db_ref[...] += _colsum(dy)
        dz_ref[...] = _ln_bwd_rows(dy * g_ref[...], nh, r)

    row = pl.BlockSpec((tm, d), lambda i: (i, 0))
    vec = pl.BlockSpec((1, d), lambda i: (0, 0))
    vshape = jax.ShapeDtypeStruct((1, d), F32)
    return pl.pallas_call(
        body, name=name, grid=(m // tm,),
        in_specs=[pl.BlockSpec((tm, k), lambda i: (i, 0)), pl.BlockSpec((k, d), lambda i: (b_off, 0)), row, vec, row]
        + ([row] if res is not None else []),
        out_specs=[row, vec, vec], out_shape=[jax.ShapeDtypeStruct((m, d), F32), vshape, vshape],
        compiler_params=_params("arbitrary"),
    )(a, b, z, g, dres, *([res] if res is not None else []))


def _matmul_res_ln_loss(a, b, x, g, beta, target, name, tm, prev):
    t, k = a.shape
    d = b.shape[1]
    tm = min(tm, t)

    def body(a_ref, b_ref, x_ref, g_ref, beta_ref, t_ref, gp_ref, bp_ref, dz_ref, dg_ref, db_ref, loss_ref):
        @pl.when(pl.program_id(0) == 0)
        def _():
            dg_ref[...] = jnp.zeros_like(dg_ref)
            db_ref[...] = jnp.zeros_like(db_ref)
            loss_ref[...] = jnp.zeros_like(loss_ref)

        nh, r = _ln_stats(ALPHA * _residual_input(x_ref, (gp_ref, bp_ref)) + _dot(a_ref[...], b_ref[...], NN))
        err = nh * g_ref[...] + beta_ref[...] - t_ref[...]
        loss_ref[...] += _colsum(err * err)
        dy = err * (1.0 / d)
        dg_ref[...] += _colsum(dy * nh)
        db_ref[...] += _colsum(dy)
        dz_ref[...] = _ln_bwd_rows(dy * g_ref[...], nh, r)

    row = pl.BlockSpec((tm, d), lambda i: (i, 0))
    vec = pl.BlockSpec((1, d), lambda i: (0, 0))
    vshape = jax.ShapeDtypeStruct((1, d), F32)
    return pl.pallas_call(
        body, name=name, grid=(t // tm,),
        in_specs=[pl.BlockSpec((tm, k), lambda i: (i, 0)), pl.BlockSpec((k, d), lambda i: (0, 0)), row, vec, vec, row, vec, vec],
        out_specs=[row, vec, vec, vec],
        out_shape=[jax.ShapeDtypeStruct((t, d), F32), vshape, vshape, vshape],
        compiler_params=_params("arbitrary"),
    )(a, b, x, g, beta, target, *prev)


FFN_HALO = 16
FFN_CHUNK = 256
LANES = 128
SUBLANES = 8


def _rows_up(e, start, rows):
    if start % SUBLANES == 0:
        return e[start:start + rows]
    return pltpu.roll(e, e.shape[0] - start, 0)[0:rows]


def _fold(x):
    return jnp.sum(x.reshape(x.shape[0] // SUBLANES, SUBLANES, x.shape[1]), axis=0)


def _ffn_mid_fwd(h, cw, cb, name, tm=1024, tc=1408, comm=None):
    t, f2 = h.shape
    tm = min(tm, t)
    f = f2 // 2
    nj, nt, hb = f // tc, t // tm, tm // FFN_HALO

    ch = min(FFN_CHUNK, tm)

    def body(hg, hgp, hv, hvp, cwg, cwv, cbg, cbv, u_ref, cg_ref, cv_ref):
        i = pl.program_id(1)
        o = FFN_HALO - FFN_KERNEL + 1
        for lg in range(tc // LANES):
            cols = slice(lg * LANES, (lg + 1) * LANES)
            wg, wv = [cwg[k:k + 1, cols] for k in range(FFN_KERNEL)], [cwv[k:k + 1, cols] for k in range(FFN_KERNEL)]
            bg, bv = cbg[:, cols], cbv[:, cols]

            def emit(base, eg, ev):
                cg = wg[0] * _rows_up(eg, o, ch) + wg[1] * _rows_up(eg, o + 1, ch) + wg[2] * _rows_up(eg, o + 2, ch) + bg
                cv = wv[0] * _rows_up(ev, o, ch) + wv[1] * _rows_up(ev, o + 1, ch) + wv[2] * _rows_up(ev, o + 2, ch) + bv
                u_ref[pl.ds(base, ch), cols] = (_gelu(cg) * cv).astype(BF16)
                cg_ref[pl.ds(base, ch), cols] = cg.astype(BF16)
                cv_ref[pl.ds(base, ch), cols] = cv.astype(BF16)

            def first(main, prev):
                return jnp.concatenate([jnp.where(i > 0, prev[:, cols].astype(F32), 0.0), main[0:ch, cols].astype(F32)], axis=0)

            def inner(c, carry):
                base = pl.multiple_of(c * ch, ch)
                emit(base, hg[pl.ds(base - FFN_HALO, ch + FFN_HALO), cols].astype(F32),
                     hv[pl.ds(base - FFN_HALO, ch + FFN_HALO), cols].astype(F32))
                return carry

            emit(0, first(hg, hgp), first(hv, hvp))
            if tm > ch:
                lax.fori_loop(1, tm // ch, inner, 0)

    def main_spec(off):
        return pl.BlockSpec((tm, tc), lambda j, i: (i, j + off))

    def prev_spec(off):
        return pl.BlockSpec((FFN_HALO, tc), lambda j, i: (jnp.maximum(i * hb - 1, 0), j + off))

    def par_spec(rows, off):
        return pl.BlockSpec((rows, tc), lambda j, i: (0, j + off))

    outs, moved = _call(
        body, name=name, grid=(nj, nt),
        in_specs=[main_spec(0), prev_spec(0), main_spec(nj), prev_spec(nj),
                  par_spec(FFN_KERNEL, 0), par_spec(FFN_KERNEL, nj), par_spec(1, 0), par_spec(1, nj)],
        out_specs=[pl.BlockSpec((tm, tc), lambda j, i: (i, j))] * 3,
        out_shape=[jax.ShapeDtypeStruct((t, f), BF16)] * 3,
        sem=("parallel", "arbitrary"), args=(h, h, h, h, cw, cw, cb, cb), comm=comm)
    return outs if comm is None else (outs, moved)


def _ffn_mid_bwd(h, cg, cv, du, cw, name, tm=1024, tc=1408, comm=None):
    t, f2 = h.shape
    tm = min(tm, t)
    f = f2 // 2
    nj, nt, hb = f // tc, t // tm, tm // FFN_HALO

    ch = min(FFN_CHUNK, tm)
    ahead = ch + SUBLANES
    n_ch = tm // ch

    def body(hg, hv, cg_ref, cgn_ref, cv_ref, cvn_ref, du_ref, dun_ref, cwg, cwv,
             dhg_ref, dhv_ref, dcwg_ref, dcwv_ref, dcbg_ref, dcbv_ref):
        i = pl.program_id(1)

        @pl.when(i == 0)
        def _():
            for ref in (dcwg_ref, dcwv_ref, dcbg_ref, dcbv_ref):
                ref[...] = jnp.zeros_like(ref)

        for lg in range(tc // LANES):
            cols = slice(lg * LANES, (lg + 1) * LANES)
            wg, wv = [cwg[k:k + 1, cols] for k in range(FFN_KERNEL)], [cwv[k:k + 1, cols] for k in range(FFN_KERNEL)]

            def emit(base, cg_e, cv_e, du_e, acc):
                cg_a, cv_a, du_a = cg_e[0:ahead], cv_e[0:ahead], du_e[0:ahead]
                gl, dgl = _gelu_and_grad(cg_a)

                def back(d, h_ref, w, dh_ref):
                    later = [d[0:ch], _rows_up(d, 1, ch), _rows_up(d, 2, ch)]
                    dh_ref[pl.ds(base, ch), cols] = (w[2] * later[0] + w[1] * later[1] + w[0] * later[2]).astype(BF16)
                    h_own = h_ref[pl.ds(base, ch), cols].astype(F32)
                    return [_fold(later[0])] + [_fold(later[FFN_KERNEL - 1 - k] * h_own) for k in range(FFN_KERNEL)]

                sums = back(du_a * cv_a * dgl, hg, wg, dhg_ref) + back(du_a * gl, hv, wv, dhv_ref)
                return tuple(a + s_ for a, s_ in zip(acc, sums))

            def inner(c, acc):
                base = pl.multiple_of(c * ch, ch)
                rows = pl.ds(base, ch + FFN_HALO)
                return emit(base, cg_ref[rows, cols].astype(F32), cv_ref[rows, cols].astype(F32), du_ref[rows, cols].astype(F32), acc)

            def last(acc):
                def rows(main, after):
                    return jnp.concatenate([main[tm - ch:tm, cols].astype(F32), after], axis=0)

                du_next = jnp.where(i < nt - 1, dun_ref[:, cols].astype(F32), 0.0)
                return emit(tm - ch, rows(cg_ref, cgn_ref[:, cols].astype(F32)), rows(cv_ref, cvn_ref[:, cols].astype(F32)),
                            rows(du_ref, du_next), acc)

            acc = (jnp.zeros((SUBLANES, LANES), F32),) * (2 * (1 + FFN_KERNEL))
            if n_ch > 1:
                acc = lax.fori_loop(0, n_ch - 1, inner, acc)
            acc = last(acc)
            dcbg_ref[:, cols] += _colsum(acc[0])
            dcbv_ref[:, cols] += _colsum(acc[1 + FFN_KERNEL])
            for k in range(FFN_KERNEL):
                dcwg_ref[k:k + 1, cols] += _colsum(acc[1 + k])
                dcwv_ref[k:k + 1, cols] += _colsum(acc[2 + FFN_KERNEL + k])

    last_blk = t // FFN_HALO - 1

    def main_spec(off):
        return pl.BlockSpec((tm, tc), lambda j, i: (i, j + off))

    def next_spec(off):
        return pl.BlockSpec((FFN_HALO, tc), lambda j, i: (jnp.minimum((i + 1) * hb, last_blk), j + off))

    def par_spec(rows, off):
        return pl.BlockSpec((rows, tc), lambda j, i: (0, j + off))

    out_tile = pl.BlockSpec((tm, tc), lambda j, i: (i, j))
    outs, moved = _call(
        body, name=name, grid=(nj, nt),
        in_specs=[main_spec(0), main_spec(nj), main_spec(0), next_spec(0), main_spec(0), next_spec(0), main_spec(0), next_spec(0),
                  par_spec(FFN_KERNEL, 0), par_spec(FFN_KERNEL, nj)],
        out_specs=[out_tile, out_tile, par_spec(FFN_KERNEL, 0), par_spec(FFN_KERNEL, 0), par_spec(1, 0), par_spec(1, 0)],
        out_shape=[jax.ShapeDtypeStruct((t, f), BF16), jax.ShapeDtypeStruct((t, f), BF16),
                   jax.ShapeDtypeStruct((FFN_KERNEL, f), F32), jax.ShapeDtypeStruct((FFN_KERNEL, f), F32),
                   jax.ShapeDtypeStruct((1, f), F32), jax.ShapeDtypeStruct((1, f), F32)],
        sem=("parallel", "arbitrary"), args=(h, h, cg, cg, cv, cv, du, du, cw, cw), comm=comm)
    return outs if comm is None else (outs, moved)


MIX_HALO = 32


def _glu(hh):
    return hh[:, 0:A_WIDTH] * _sigmoid(hh[:, A_WIDTH:2 * A_WIDTH])


def _fill_row_shifts(s):
    rows = s.shape[1] - SUBLANES
    for j in range(1, SUBLANES):
        s[j, 0:rows, :] = s[0, pl.ds(j, rows), :]


def _rows_from(s, start, rows):
    j = start % SUBLANES
    return s[j, start - j:start - j + rows, :]


def _tril_mask():
    return lax.broadcasted_iota(jnp.int32, (B_CHUNK, B_CHUNK), 0) >= lax.broadcasted_iota(jnp.int32, (B_CHUNK, B_CHUNK), 1)


def _spatial_mix(q, ms_ref, sbt_ref, tm):
    mask = _tril_mask()
    ws = [jnp.where(mask, ms_ref[g], 0.0).astype(BF16) for g in range(B_GROUPS)]
    qb = q.astype(BF16)
    rows = []
    for c in range(tm // B_CHUNK):
        cols = [_dot(ws[g], qb[c * B_CHUNK:(c + 1) * B_CHUNK, g * 128:(g + 1) * 128], NN) + sbt_ref[:, g:g + 1]
                for g in range(B_GROUPS)]
        rows.append(jnp.concatenate(cols, axis=1))
    return jnp.concatenate(rows, axis=0)


def _mixer_mid_fwd(h, cw, cb, ag, ab, bg, bb, ms, sbt, name, tm=256, comm=None):
    t = h.shape[0]
    nt, hb = t // tm, tm // MIX_HALO
    o = MIX_HALO - A_KERNEL + 1

    def body(h_ref, hp_ref, cw_ref, cb_ref, ag_ref, ab_ref, bg_ref, bb_ref, ms_ref, sbt_ref, cat_ref, y_ref, sp):
        i = pl.program_id(0)
        sp[0, 0:MIX_HALO, :] = jnp.where(i > 0, _glu(hp_ref[:, 0:2 * A_WIDTH].astype(F32)), 0.0)
        sp[0, MIX_HALO:, :] = _glu(h_ref[:, 0:2 * A_WIDTH].astype(F32))
        _fill_row_shifts(sp)
        y = jnp.zeros((tm, A_WIDTH), F32) + cb_ref[...]
        for k in range(A_KERNEL):
            y = y + cw_ref[k:k + 1, :] * _rows_from(sp, o + k, tm)
        y_ref[...] = y.astype(BF16)
        nh, _ = _ln_stats(y)
        ln = nh * ag_ref[...] + ab_ref[...]
        cat_ref[:, 0:A_WIDTH] = (ln * _sigmoid(ln)).astype(BF16)
        u = _gelu(h_ref[:, 1024:1536].astype(F32))
        nb, _ = _ln_stats(_gelu(h_ref[:, 1536:2048].astype(F32)))
        mixed = _spatial_mix(nb * bg_ref[...] + bb_ref[...], ms_ref, sbt_ref, tm)
        cat_ref[:, A_WIDTH:] = (u * mixed).astype(BF16)

    vec = pl.BlockSpec((1, A_WIDTH), lambda i: (0, 0))
    outs, moved = _call(
        body, name=name, grid=(nt,),
        in_specs=[pl.BlockSpec((tm, 2048), lambda i: (i, 0)),
                  pl.BlockSpec((MIX_HALO, 2048), lambda i: (jnp.maximum(i * hb - 1, 0), 0)),
                  pl.BlockSpec((A_KERNEL, A_WIDTH), lambda i: (0, 0)), vec, vec, vec, vec, vec,
                  pl.BlockSpec((B_GROUPS, B_CHUNK, B_CHUNK), lambda i: (0, 0, 0)),
                  pl.BlockSpec((B_CHUNK, B_GROUPS), lambda i: (0, 0))],
        out_specs=[pl.BlockSpec((tm, D_MODEL), lambda i: (i, 0)), pl.BlockSpec((tm, A_WIDTH), lambda i: (i, 0))],
        out_shape=[jax.ShapeDtypeStruct((t, D_MODEL), BF16), jax.ShapeDtypeStruct((t, A_WIDTH), BF16)],
        scratch_shapes=[pltpu.VMEM((SUBLANES, tm + MIX_HALO, A_WIDTH), F32)],
        sem=("parallel",), args=(h, h, cw, cb, ag, ab, bg, bb, ms, sbt), comm=comm)
    return outs if comm is None else (outs, moved)


def _mixer_mid_bwd(h, y, dcat, cw, ag, ab, bg, bb, ms, mst, sbt, name, tm=256, comm=None):
    t = h.shape[0]
    nt, hb = t // tm, tm // MIX_HALO
    r = tm + MIX_HALO
    nchunk = tm // B_CHUNK

    def body(h_ref, y_ref, yn_ref, dc_ref, dcn_ref, cw_ref, ag_ref, ab_ref, bg_ref, bb_ref, ms_ref, mst_ref, sbt_ref,
             dh_ref, dcw_ref, dcb_ref, dag_ref, dab_ref, dbg_ref, dbb_ref, dms_ref, dsb_ref, sdy, sbacc):
        i = pl.program_id(0)

        @pl.when(i == 0)
        def _():
            for ref in (dcw_ref, dcb_ref, dag_ref, dab_ref, dbg_ref, dbb_ref, dms_ref, dsb_ref, sbacc):
                ref[...] = jnp.zeros_like(ref)

        nh, rs = _ln_stats(jnp.concatenate([y_ref[...].astype(F32), yn_ref[...].astype(F32)], axis=0))
        ln = nh * ag_ref[...] + ab_ref[...]
        sg = _sigmoid(ln)
        dao = jnp.concatenate([dc_ref[:, 0:A_WIDTH].astype(F32),
                               jnp.where(i < nt - 1, dcn_ref[:, 0:A_WIDTH].astype(F32), 0.0)], axis=0)
        dln = dao * (sg * (1.0 + ln * (1.0 - sg)))
        dag_ref[...] += _colsum(dln[0:tm] * nh[0:tm])
        dab_ref[...] += _colsum(dln[0:tm])
        sdy[0] = _ln_bwd_rows(dln * ag_ref[...], nh, rs)
        _fill_row_shifts(sdy)
        dcb_ref[...] += _colsum(sdy[0, 0:tm, :])
        av = h_ref[:, 0:A_WIDTH].astype(F32)
        s = _sigmoid(h_ref[:, A_WIDTH:2 * A_WIDTH].astype(F32))
        p_own = av * s
        dp = jnp.zeros((tm, A_WIDTH), F32)
        for k in range(A_KERNEL):
            later = _rows_from(sdy, A_KERNEL - 1 - k, tm)
            dcw_ref[k:k + 1, :] += _colsum(later * p_own)
            dp = dp + cw_ref[k:k + 1, :] * later
        dh_ref[:, 0:A_WIDTH] = (dp * s).astype(BF16)
        dh_ref[:, A_WIDTH:2 * A_WIDTH] = (dp * av * s * (1.0 - s)).astype(BF16)

        u, dgu = _gelu_and_grad(h_ref[:, 1024:1536].astype(F32))
        w, dgw = _gelu_and_grad(h_ref[:, 1536:2048].astype(F32))
        nb, rb = _ln_stats(w)
        q = nb * bg_ref[...] + bb_ref[...]
        mixed = _spatial_mix(q, ms_ref, sbt_ref, tm)
        dbo = dc_ref[:, A_WIDTH:].astype(F32)
        dh_ref[:, 1024:1536] = (dbo * mixed * dgu).astype(BF16)
        dmx = dbo * u
        mask = _tril_mask()
        wst = [jnp.where(mask.T, mst_ref[g], 0.0).astype(BF16) for g in range(B_GROUPS)]
        qb = q.astype(BF16)
        dmb = dmx.astype(BF16)
        rows = []
        for c in range(nchunk):
            cols = []
            for g in range(B_GROUPS):
                rs_, cs_ = slice(c * B_CHUNK, (c + 1) * B_CHUNK), slice(g * 128, (g + 1) * 128)
                sbacc[g] += dmx[rs_, cs_]
                dms_ref[g] += _dot(dmb[rs_, cs_], qb[rs_, cs_], NT)
                cols.append(_dot(wst[g], dmb[rs_, cs_], NN))
            rows.append(jnp.concatenate(cols, axis=1))
        dq = jnp.concatenate(rows, axis=0)
        dbg_ref[...] += _colsum(dq * nb)
        dbb_ref[...] += _colsum(dq)
        dh_ref[:, 1536:2048] = (_ln_bwd_rows(dq * bg_ref[...], nb, rb) * dgw).astype(BF16)

        @pl.when(i == nt - 1)
        def _():
            for g in range(B_GROUPS):
                dms_ref[g] = jnp.where(mask, dms_ref[g], 0.0)
                dsb_ref[g] = jnp.sum(sbacc[g], axis=1, keepdims=True)

    last_blk = t // MIX_HALO - 1
    vec = pl.BlockSpec((1, A_WIDTH), lambda i: (0, 0))
    mat = pl.BlockSpec((B_GROUPS, B_CHUNK, B_CHUNK), lambda i: (0, 0, 0))
    taps = pl.BlockSpec((A_KERNEL, A_WIDTH), lambda i: (0, 0))

    def halo(width):
        return pl.BlockSpec((MIX_HALO, width), lambda i: (jnp.minimum((i + 1) * hb, last_blk), 0))

    vshape = jax.ShapeDtypeStruct((1, A_WIDTH), F32)
    outs, moved = _call(
        body, name=name, grid=(nt,),
        in_specs=[pl.BlockSpec((tm, 2048), lambda i: (i, 0)), pl.BlockSpec((tm, A_WIDTH), lambda i: (i, 0)), halo(A_WIDTH),
                  pl.BlockSpec((tm, D_MODEL), lambda i: (i, 0)), halo(D_MODEL),
                  taps, vec, vec, vec, vec, mat, mat, pl.BlockSpec((B_CHUNK, B_GROUPS), lambda i: (0, 0))],
        out_specs=[pl.BlockSpec((tm, 2048), lambda i: (i, 0)), taps, vec, vec, vec, vec, vec, mat,
                   pl.BlockSpec((B_GROUPS, B_CHUNK, 1), lambda i: (0, 0, 0))],
        out_shape=[jax.ShapeDtypeStruct((t, 2048), BF16), jax.ShapeDtypeStruct((A_KERNEL, A_WIDTH), F32),
                   vshape, vshape, vshape, vshape, vshape,
                   jax.ShapeDtypeStruct((B_GROUPS, B_CHUNK, B_CHUNK), F32), jax.ShapeDtypeStruct((B_GROUPS, B_CHUNK, 1), F32)],
        scratch_shapes=[pltpu.VMEM((SUBLANES, r, A_WIDTH), F32), pltpu.VMEM((B_GROUPS, B_CHUNK, B_CHUNK), F32)],
        sem=("arbitrary",), args=(h, y, y, dcat, dcat, cw, ag, ab, bg, bb, ms, mst, sbt), comm=comm)
    return outs if comm is None else (outs, moved)


Q_WIDTH = N_Q_HEADS * HEAD_DIM
KV_WIDTH = 2 * N_KV_HEADS * HEAD_DIM
PAIRS_PER_KV = N_Q_HEADS // N_KV_HEADS // 2
ATT_SCALE = 1.0 / math.sqrt(HEAD_DIM)


def _dup_heads(pair_cols, kv_head):
    lane = lax.broadcasted_iota(jnp.int32, pair_cols.shape, 1)
    rolled = pltpu.roll(pair_cols, HEAD_DIM, 1)
    first = lane < HEAD_DIM
    return jnp.where(first, pair_cols, rolled) if kv_head == 0 else jnp.where(first, rolled, pair_cols)


HEADS_PER_KV = N_Q_HEADS // N_KV_HEADS


def _stack_heads(ref, kh):
    lane = lax.broadcasted_iota(jnp.int32, (ATT_BLOCK, 128), 1)
    rows = []
    for pr in range(PAIRS_PER_KV):
        c0 = (kh * PAIRS_PER_KV + pr) * 128
        pair = ref[:, c0:c0 + 128]
        rows += [jnp.where(lane < HEAD_DIM, pair, jnp.zeros_like(pair)), jnp.where(lane < HEAD_DIM, jnp.zeros_like(pair), pair)]
    return jnp.concatenate(rows, axis=0)


def _unstack_heads(stacked, kh, write):
    lane = lax.broadcasted_iota(jnp.int32, (ATT_BLOCK, 128), 1)
    for pr in range(PAIRS_PER_KV):
        first = stacked[(2 * pr) * ATT_BLOCK:(2 * pr + 1) * ATT_BLOCK]
        second = stacked[(2 * pr + 1) * ATT_BLOCK:(2 * pr + 2) * ATT_BLOCK]
        write((kh * PAIRS_PER_KV + pr) * 128, jnp.where(lane < HEAD_DIM, first, second))


def _sink_row(sink_ref, kh):
    return jnp.concatenate([jnp.full((1, ATT_BLOCK), sink_ref[0, kh * HEADS_PER_KV + h], F32) for h in range(HEADS_PER_KV)], axis=1)


def _att_window_bias():
    sj = lax.broadcasted_iota(jnp.int32, (2 * ATT_BLOCK, HEADS_PER_KV * ATT_BLOCK), 0)
    qi = lax.broadcasted_iota(jnp.int32, (2 * ATT_BLOCK, HEADS_PER_KV * ATT_BLOCK), 1) & (ATT_BLOCK - 1)
    diff = qi + ATT_BLOCK - sj
    return jnp.where((diff >= 0) & (diff < ATT_BLOCK), 0.0, -jnp.inf)


def _att_probs_t(q_all, k2, bias_ref, n, sink):
    st = _dot(k2, q_all, NT) * ATT_SCALE + bias_ref[...]
    st = jnp.concatenate([jnp.where(n > 0, st[0:ATT_BLOCK], -jnp.inf), st[ATT_BLOCK:]], axis=0)
    m = jnp.maximum(jnp.max(st, axis=0, keepdims=True), sink)
    e = jnp.exp(st - m)
    es = jnp.exp(sink - m)
    inv = 1.0 / (jnp.sum(e, axis=0, keepdims=True) + es)
    return e * inv, es * inv


def _attn_fwd(qkv, sinks, name, comm=None):
    t = qkv.shape[0]
    nb = t // ATT_BLOCK
    kvb = Q_WIDTH // KV_WIDTH

    def body(sink_ref, q_ref, kv_ref, kvp_ref, o_ref, bias):
        n = pl.program_id(0)

        @pl.when(n == 0)
        def _():
            bias[...] = _att_window_bias()

        kv = jnp.concatenate([kvp_ref[...], kv_ref[...]], axis=0).astype(F32)

        def write(c0, pair):
            o_ref[:, c0:c0 + 128] = pair.astype(BF16)

        for kh in range(N_KV_HEADS):
            k2 = _dup_heads(kv[:, 0:128], kh).astype(BF16)
            v2 = _dup_heads(kv[:, 128:256], kh).astype(BF16)
            pt, _ = _att_probs_t(_stack_heads(q_ref, kh), k2, bias, n, _sink_row(sink_ref, kh))
            _unstack_heads(_dot(v2, pt, TN).T, kh, write)

    (out,), moved = _call(
        body, name=name, grid=(nb,),
        in_specs=[pl.BlockSpec(memory_space=pltpu.SMEM),
                  pl.BlockSpec((ATT_BLOCK, Q_WIDTH), lambda n: (n, 0)),
                  pl.BlockSpec((ATT_BLOCK, KV_WIDTH), lambda n: (n, kvb)),
                  pl.BlockSpec((ATT_BLOCK, KV_WIDTH), lambda n: (jnp.maximum(n - 1, 0), kvb))],
        out_specs=[pl.BlockSpec((ATT_BLOCK, Q_WIDTH), lambda n: (n, 0))],
        out_shape=[jax.ShapeDtypeStruct((t, Q_WIDTH), BF16)],
        scratch_shapes=[pltpu.VMEM((2 * ATT_BLOCK, HEADS_PER_KV * ATT_BLOCK), F32)],
        sem=("arbitrary",), args=(sinks, qkv, qkv, qkv), comm=comm)
    return out if comm is None else (out, moved)


def _attn_bwd(qkv, d_o, sinks, name, comm=None):
    t = qkv.shape[0]
    nb = t // ATT_BLOCK
    kvb = Q_WIDTH // KV_WIDTH

    def body(sink_ref, q_ref, kv_ref, kvp_ref, do_ref, dq_ref, dkv_ref, dbq_ref, dbkv_ref, dsink_ref, carry, bias):
        n = pl.program_id(0)

        @pl.when(n == 0)
        def _():
            for ref in (dbq_ref, dbkv_ref, dsink_ref, carry):
                ref[...] = jnp.zeros_like(ref)
            dkv_ref[...] = jnp.zeros_like(dkv_ref)
            bias[...] = _att_window_bias()

        @pl.when(n < nb)
        def _():
            kv = jnp.concatenate([kvp_ref[...], kv_ref[...]], axis=0).astype(F32)
            lane2 = lax.broadcasted_iota(jnp.int32, (2 * ATT_BLOCK, 128), 1)
            sink_lane = lax.broadcasted_iota(jnp.int32, (1, 128), 1)
            dsink = jnp.zeros((1, 128), F32)
            dk_parts, dv_parts = [], []

            def write(c0, pair):
                dbq_ref[:, c0:c0 + 128] += _colsum(pair)
                dq_ref[:, c0:c0 + 128] = pair.astype(BF16)

            for kh in range(N_KV_HEADS):
                k2 = _dup_heads(kv[:, 0:128], kh).astype(BF16)
                v2 = _dup_heads(kv[:, 128:256], kh).astype(BF16)
                q_all = _stack_heads(q_ref, kh)
                do_all = _stack_heads(do_ref, kh)
                pt, ps = _att_probs_t(q_all, k2, bias, n, _sink_row(sink_ref, kh))
                dpt = _dot(v2, do_all, NT)
                delta = jnp.sum(pt * dpt, axis=0, keepdims=True)
                dst = pt * (dpt - delta) * ATT_SCALE
                psd = ps * delta
                for h in range(HEADS_PER_KV):
                    dsink = dsink + jnp.where(sink_lane == kh * HEADS_PER_KV + h,
                                              -jnp.sum(psd[:, h * ATT_BLOCK:(h + 1) * ATT_BLOCK]), 0.0)
                _unstack_heads(_dot(k2, dst, TN).T, kh, write)
                dk_acc = _dot(dst, q_all, NN)
                dv_acc = _dot(pt, do_all, NN)
                dk_parts.append(dk_acc + pltpu.roll(dk_acc, HEAD_DIM, 1))
                dv_parts.append(dv_acc + pltpu.roll(dv_acc, HEAD_DIM, 1))
            dk = jnp.where(lane2 < HEAD_DIM, dk_parts[0], dk_parts[1])
            dv = jnp.where(lane2 < HEAD_DIM, dv_parts[0], dv_parts[1])
            dkv_new = jnp.concatenate([dk, dv], axis=1)
            done = carry[...] + dkv_new[0:ATT_BLOCK]

            @pl.when(n > 0)
            def _():
                dkv_ref[...] = done.astype(BF16)
                dbkv_ref[...] += _colsum(done)

            carry[...] = dkv_new[ATT_BLOCK:]
            dsink_ref[...] += dsink

        @pl.when(n == nb)
        def _():
            dkv_ref[...] = carry[...].astype(BF16)
            dbkv_ref[...] += _colsum(carry[...])

    def clamp(n):
        return jnp.minimum(n, nb - 1)

    outs, moved = _call(
        body, name=name, grid=(nb + 1,),
        in_specs=[pl.BlockSpec(memory_space=pltpu.SMEM),
                  pl.BlockSpec((ATT_BLOCK, Q_WIDTH), lambda n: (clamp(n), 0)),
                  pl.BlockSpec((ATT_BLOCK, KV_WIDTH), lambda n: (clamp(n), kvb)),
                  pl.BlockSpec((ATT_BLOCK, KV_WIDTH), lambda n: (jnp.maximum(clamp(n) - 1, 0), kvb)),
                  pl.BlockSpec((ATT_BLOCK, Q_WIDTH), lambda n: (clamp(n), 0))],
        out_specs=[pl.BlockSpec((ATT_BLOCK, Q_WIDTH), lambda n: (clamp(n), 0)),
                   pl.BlockSpec((ATT_BLOCK, KV_WIDTH), lambda n: (jnp.maximum(n - 1, 0), 0)),
                   pl.BlockSpec((1, Q_WIDTH), lambda n: (0, 0)),
                   pl.BlockSpec((1, KV_WIDTH), lambda n: (0, 0)),
                   pl.BlockSpec((1, 128), lambda n: (0, 0))],
        out_shape=[jax.ShapeDtypeStruct((t, Q_WIDTH), BF16), jax.ShapeDtypeStruct((t, KV_WIDTH), BF16),
                   jax.ShapeDtypeStruct((1, Q_WIDTH), F32), jax.ShapeDtypeStruct((1, KV_WIDTH), F32),
                   jax.ShapeDtypeStruct((1, 128), F32)],
        scratch_shapes=[pltpu.VMEM((ATT_BLOCK, KV_WIDTH), F32), pltpu.VMEM((2 * ATT_BLOCK, HEADS_PER_KV * ATT_BLOCK), F32)],
        sem=("arbitrary",), args=(sinks, qkv, qkv, qkv, d_o), comm=comm)
    return outs if comm is None else (outs, moved)


def _adamw_math(g, w, m, v):
    m = ADAM_B1 * m + (1.0 - ADAM_B1) * g
    v = ADAM_B2 * v + (1.0 - ADAM_B2) * (g * g)
    m_hat = m / (1.0 - ADAM_B1 ** ADAM_STEP)
    v_hat = v / (1.0 - ADAM_B2 ** ADAM_STEP)
    delta = -ADAM_LR * (m_hat / (jnp.sqrt(v_hat) + ADAM_EPS) + ADAM_WD * w)
    return delta, m, v


def _sum_partials(p_ref):
    g = p_ref[0].astype(F32)
    for s in range(1, N_DEV):
        g = g + p_ref[s].astype(F32)
    return g


def _adamw_big(parts, w, m, v, name, tr):
    r, c = w.shape
    parts = [p if isinstance(p, tuple) else (p, 0, p.shape[1]) for p in parts]
    tiles = [rows // tr for _, _, rows in parts]
    starts = [sum(tiles[:l]) for l in range(len(parts))]
    assert all(lo % tr == 0 and rows % tr == 0 for _, lo, rows in parts) and sum(tiles) * tr == r

    def body(*refs):
        p_refs, (w_ref, m_ref, v_ref, g_out, d_out, m_out, v_out) = refs[:len(parts)], refs[len(parts):]
        i = pl.program_id(0)
        for l, p_ref in enumerate(p_refs):
            @pl.when((i >= starts[l]) & (i < starts[l] + tiles[l]))
            def _():
                g = _sum_partials(p_ref)
                g_out[...] = g
                d_out[...], m_out[...], v_out[...] = _adamw_math(g, w_ref[...], m_ref[...], v_ref[...])

    def part_spec(l):
        return pl.BlockSpec((N_DEV, tr, c), lambda i: (0, jnp.clip(i - starts[l], 0, tiles[l] - 1) + parts[l][1] // tr, 0))

    tile = pl.BlockSpec((tr, c), lambda i: (i, 0))
    shape = jax.ShapeDtypeStruct((r, c), F32)
    return pl.pallas_call(
        body, name=name, grid=(r // tr,),
        in_specs=[part_spec(l) for l in range(len(parts))] + [tile, tile, tile],
        out_specs=[tile] * 4, out_shape=[shape] * 4,
        compiler_params=_params("parallel"),
    )(*[p[0] for p in parts], w, m, v)


def _adamw_small(parts, ws, ms, vs, name):
    n = len(ws)

    def body(*refs):
        ins, outs = refs[:4 * n], refs[4 * n:]
        for a in range(n):
            g = _sum_partials(ins[a])
            outs[4 * a][...] = g
            outs[4 * a + 1][...], outs[4 * a + 2][...], outs[4 * a + 3][...] = _adamw_math(
                g, ins[n + a][...], ins[2 * n + a][...], ins[3 * n + a][...])

    out_shape = []
    for w in ws:
        out_shape += [jax.ShapeDtypeStruct(w.shape, F32)] * 4
    return pl.pallas_call(body, name=name, out_shape=out_shape, compiler_params=_params())(*parts, *ws, *ms, *vs)


PACK_LANES = 128
PACK_ROWS = 8


def _pack(arrs):
    flat = jnp.concatenate([a.reshape(-1).astype(F32) for a in arrs])
    unit = PACK_LANES * PACK_ROWS
    total = -(-flat.shape[0] // unit) * unit
    return jnp.pad(flat, (0, total - flat.shape[0])).reshape(-1, PACK_LANES)


def _unpack(buf, shapes):
    flat = buf.reshape(N_DEV, -1)
    out, pos = [], 0
    for s in shapes:
        size = math.prod(s)
        out.append(flat[:, pos:pos + size].reshape((N_DEV,) + tuple(s)))
        pos += size
    return out


def _interleave(g):
    return jnp.transpose(g, (1, 0, 2)).reshape(g.shape[1], -1)


def _ffn_backward(dz, x_in, z_in, g_in, h, cg, cv, u, w_up_t, cw, w_down, tag, exchange=(), exchange_late=(), own_rows=0):
    du = _matmul(dz, w_down, "nt", BF16, f"ffn{tag}_du", 1024, 1408, 1024)
    d_w_down = _matmul(u, dz, "tn", BF16, f"ffn{tag}_dwdown", 1408, 1024, 2048)
    (dhg, dhv, dcwg, dcwv, dcbg, dcbv), moved = _ffn_mid_bwd(
        h, cg, cv, du, cw, f"ffn{tag}_mid_bwd", comm=_Comm(exchange=[d_w_down.reshape(N_DEV, -1, D_MODEL), *exchange]))
    d_w_up_t = _matmul_tn_pair(dhg, dhv, x_in, BF16, f"ffn{tag}_dwup", 1408, 1024, 1024,
                               comm=_Comm(exchange=exchange_late) if exchange_late else None)
    if exchange_late:
        d_w_up_t, late = d_w_up_t
        moved = moved + late
    d_up_blocks = d_w_up_t.reshape(N_DEV, -1, D_MODEL)
    dx = _matmul(dhv, w_up_t, "nn", F32, f"ffn{tag}_dx_value", 1024, 1024, D_FF, b_off=1,
                 comm=_Comm(exchange=[(d_up_blocks, 0, own_rows)]) if own_rows else None)
    if own_rows:
        dx, own = dx
        moved = moved + own
    dz_in, dg_in, db_in = _matmul_ln_bwd(dhg, w_up_t, z_in, g_in, dz, f"ffn{tag}_dx_gate_ln_bwd", 512, res=dx)
    return (dz_in, dg_in, db_in, d_up_blocks,
            jnp.concatenate([dcwg, dcwv], axis=1), jnp.concatenate([dcbg, dcbv], axis=1), moved)


def kernel(x, ab_w_in, a_conv_w, a_conv_b, a_norm_g, a_norm_b, b_norm_g, b_norm_b, b_spatial_w, b_spatial_b, ab_w_out, c_w_qkv, c_b_qkv, c_sinks, c_w_o, ffn_w_up, ffn_conv_w, ffn_conv_b, ffn_w_down, ln_g, ln_b, loss_target, m_ab_w_in, m_a_conv_w, m_a_conv_b, m_a_norm_g, m_a_norm_b, m_b_norm_g, m_b_norm_b, m_b_spatial_w, m_b_spatial_b, m_ab_w_out, m_c_w_qkv, m_c_b_qkv, m_c_sinks, m_c_w_o, m_ffn_w_up, m_ffn_conv_w, m_ffn_conv_b, m_ffn_w_down, m_ln_g, m_ln_b, v_ab_w_in, v_a_conv_w, v_a_conv_b, v_a_norm_g, v_a_norm_b, v_b_norm_g, v_b_norm_b, v_b_spatial_w, v_b_spatial_b, v_ab_w_out, v_c_w_qkv, v_c_b_qkv, v_c_sinks, v_c_w_o, v_ffn_w_up, v_ffn_conv_w, v_ffn_conv_b, v_ffn_w_down, v_ln_g, v_ln_b):
    me = 4 * lax.axis_index("x") + 2 * lax.axis_index("y") + lax.axis_index("c")
    xt = x[0]
    t = xt.shape[0]

    small_shard_shapes = [a_conv_w.shape, c_b_qkv.shape, ffn_conv_w.shape, ln_g.shape, ln_b.shape]
    up_shard = [jnp.swapaxes(ffn_w_up[l], 0, 1).astype(BF16) for l in range(2)]
    qkv_shard = jnp.swapaxes(c_w_qkv[0], 0, 1).astype(BF16)
    down_shard = [ffn_w_down[l].astype(BF16) for l in range(2)]
    g_win, g_small = _comm_only(
        _Comm(gather=[jnp.swapaxes(ab_w_in[0], 0, 1).astype(BF16), _pack([a_conv_w, c_b_qkv, ffn_conv_w, ln_g, ln_b])]),
        "gather_first")
    w_in = g_win.reshape(-1, D_MODEL)
    g_acw, g_bqkv, g_fcw, g_lng, g_lnb = _unpack(g_small, small_shard_shapes)
    acw = _interleave(g_acw[:, 0])
    bqkv = g_bqkv[:, 0].reshape(1, -1)
    fcw = [_interleave(g_fcw[:, l]) for l in range(2)]
    lng = jnp.transpose(g_lng, (1, 2, 0, 3)).reshape(2, 2, 1, D_MODEL)
    lnb = jnp.transpose(g_lnb, (1, 2, 0, 3)).reshape(2, 2, 1, D_MODEL)
    fcb = [ffn_conv_b[l:l + 1] for l in range(2)]
    ms = b_spatial_w[0]
    mst = jnp.swapaxes(ms, 1, 2)
    sbt = b_spatial_b[0].T

    h0, (g_wout,) = _matmul(xt, w_in, "nt", BF16, "mix_in", 1024, 1024, 1024, comm=_Comm(gather=[ab_w_out[0].astype(BF16)]))
    w_out = g_wout.reshape(D_MODEL, D_MODEL)
    (cat, y0), (g_wup0,) = _mixer_mid_fwd(h0, acw, a_conv_b, a_norm_g, a_norm_b, b_norm_g, b_norm_b, ms, sbt, "mix_mid_fwd",
                                    comm=_Comm(gather=[up_shard[0]]))
    w_up0 = g_wup0.reshape(2 * D_FF, D_MODEL)
    z1, x1 = _matmul_res_ln(cat, w_out, xt, lng[0, 0], lnb[0, 0], "mix_out_ln", 512)
    hf0, (g_wdown0, g_wqkv) = _matmul(x1, w_up0, "nt", BF16, "ffn0_up", 1024, 1408, 1024,
                                      comm=_Comm(gather=[down_shard[0], qkv_shard]))
    w_down0 = g_wdown0.reshape(D_FF, D_MODEL)
    w_qkv = g_wqkv.reshape(Q_WIDTH + KV_WIDTH, D_MODEL)
    (u0, cg0, cv0), (g_wup1,) = _ffn_mid_fwd(hf0, fcw[0], fcb[0], "ffn0_mid_fwd", comm=_Comm(gather=[up_shard[1]]))
    w_up1 = g_wup1.reshape(2 * D_FF, D_MODEL)
    (z2, x2), (g_wo,) = _matmul_res_ln(u0, w_down0, z1, lng[0, 1], lnb[0, 1], "ffn0_down_ln", 512, prev=(lng[0, 0], lnb[0, 0]),
                                       comm=_Comm(gather=[c_w_o[0].astype(BF16)]))
    w_o = g_wo.reshape(D_MODEL, D_MODEL)
    qkv = _matmul(x2, w_qkv, "nt", BF16, "att_qkv", 1024, 1280, 1024, bias=bqkv)
    att, (g_wdown1,) = _attn_fwd(qkv, c_sinks, "att_fwd", comm=_Comm(gather=[down_shard[1]]))
    w_down1 = g_wdown1.reshape(D_FF, D_MODEL)
    z3, x3 = _matmul_res_ln(att, w_o, z2, lng[1, 0], lnb[1, 0], "att_out_ln", 512, prev=(lng[0, 1], lnb[0, 1]))
    hf1 = _matmul(x3, w_up1, "nt", BF16, "ffn1_up", 1024, 1408, 1024)
    u1, cg1, cv1 = _ffn_mid_fwd(hf1, fcw[1], fcb[1], "ffn1_mid_fwd")

    dz4, dg11, db11, loss_terms = _matmul_res_ln_loss(u1, w_down1, z3, lng[1, 1], lnb[1, 1], loss_target[0],
                                                      "ffn1_down_ln_loss", 512, prev=(lng[1, 0], lnb[1, 0]))
    dz3, dg10, db10, d_wup1, d_fcw1, d_fcb1, (p_wdown1,) = _ffn_backward(
        dz4, x3, z3, lng[1, 0], hf1, cg1, cv1, u1, w_up1, fcw[1], w_down1, 1)
    d_att = _matmul(dz3, w_o, "nt", BF16, "att_dout", 1024, 1024, 1024)
    d_wo = _matmul(att, dz3, "tn", BF16, "att_dwo", 1024, 1024, 512)
    rows_up = d_wup1.shape[1]
    first = 3 * rows_up // 4
    (dq, dkv, dbq, dbkv, dsinks), (p_wup1a,) = _attn_bwd(qkv, d_att, c_sinks, "att_bwd",
                                                        comm=_Comm(exchange=[(d_wup1, 0, first)]))
    d_wqkv = jnp.concatenate([_matmul(dq, x2, "tn", BF16, "att_dwq", 1024, 1024, 1024),
                              _matmul(dkv, x2, "tn", BF16, "att_dwkv", KV_WIDTH, 1024, 1024)], axis=0)
    dx2 = _matmul(dkv, w_qkv, "nn", F32, "att_dx_kv", 1024, 1024, KV_WIDTH, b_off=Q_WIDTH // KV_WIDTH)
    dz2, dg01, db01 = _matmul_ln_bwd(dq, w_qkv, z2, lng[0, 1], dz3, "att_dx_q_ln_bwd", 512, res=dx2)
    early = rows_up // 4
    dz1, dg00, db00, d_wup0, d_fcw0, d_fcb0, (p_wdown0, p_wup1b, p_wqkv, p_wo, p_wup0a) = _ffn_backward(
        dz2, x1, z1, lng[0, 0], hf0, cg0, cv0, u0, w_up0, fcw[0], w_down0, 0, exchange=[(d_wup1, first, rows_up - first)],
        exchange_late=[d_wqkv.reshape(N_DEV, -1, D_MODEL), d_wo.reshape(N_DEV, -1, D_MODEL)], own_rows=early)
    dcat = _matmul(dz1, w_out, "nt", BF16, "mix_dcat", 1024, 1024, 1024)
    d_wout = _matmul(cat, dz1, "tn", BF16, "mix_dwout", 1024, 1024, 512)
    (dh0, d_acw, d_acb, d_ang, d_anb, d_bng, d_bnb, d_ms, d_sb), (p_wup0b, p_wout) = _mixer_mid_bwd(
        h0, y0, dcat, acw, a_norm_g, a_norm_b, b_norm_g, b_norm_b, ms, mst, sbt, "mix_mid_bwd",
        comm=_Comm(exchange=[(d_wup0, early, rows_up - early), d_wout.reshape(N_DEV, -1, D_MODEL)]))
    d_bqkv = jnp.concatenate([dbq, dbkv], axis=1)
    d_lng = jnp.stack([jnp.stack([dg00, dg01]), jnp.stack([dg10, dg11])])
    d_lnb = jnp.stack([jnp.stack([db00, db01]), jnp.stack([db10, db11])])
    small_full = [d_acb, d_ang, d_anb, d_bng, d_bnb, d_ms, d_sb, dsinks[:, :N_Q_HEADS], jnp.concatenate([d_fcb0, d_fcb1], axis=0),
                  d_acw, d_bqkv, jnp.stack([d_fcw0, d_fcw1]), d_lng, d_lnb, loss_terms]
    d_win, (g_small_grads,) = _matmul(dh0, xt, "tn", BF16, "mix_dwin", 1024, 1024, 512, comm=_Comm(gather=[_pack(small_full)]))
    grad_x, (p_win,) = _matmul(dh0, w_in, "nn", F32, "mix_dx", 1024, 1024, 1024, res=dz1, res_scale=ALPHA,
                               comm=_Comm(exchange=[d_win.reshape(N_DEV, -1, D_MODEL)]))


    big = {}
    for nm, p, w, m, v, tr, transposed in [
            ("ab_w_in", [p_win], ab_w_in, m_ab_w_in, v_ab_w_in, 256, True),
            ("ab_w_out", [p_wout], ab_w_out, m_ab_w_out, v_ab_w_out, 128, False),
            ("c_w_qkv", [p_wqkv], c_w_qkv, m_c_w_qkv, v_c_w_qkv, 160, True), ("c_w_o", [p_wo], c_w_o, m_c_w_o, v_c_w_o, 128, False),
            ("ffn_w_up", [(p_wup0a, 0, early), (p_wup0b, early, rows_up - early), (p_wup1a, 0, first), (p_wup1b, first, rows_up - first)], ffn_w_up, m_ffn_w_up, v_ffn_w_up, 176, True),
            ("ffn_w_down", [p_wdown0, p_wdown1], ffn_w_down, m_ffn_w_down, v_ffn_w_down, 176, False)]:
        def two_d(a):
            a = jnp.swapaxes(a, 1, 2) if transposed else a
            return a.reshape(-1, a.shape[-1])

        def back(o):
            return jnp.swapaxes(o.reshape(w.shape[0], w.shape[2], w.shape[1]), 1, 2) if transposed else o.reshape(w.shape)

        outs = _adamw_big(p, two_d(w), two_d(m), two_d(v), "adamw_" + nm, tr)
        big[nm] = [back(o) for o in outs]

    *gs, loss_parts = _unpack(g_small_grads, [a.shape for a in small_full])
    loss = 0.5 / D_MODEL * jnp.sum(loss_parts)

    def my_shard(g, width):
        g = g.reshape(g.shape[:-1] + (N_DEV, width))
        return lax.dynamic_index_in_dim(g, me, axis=g.ndim - 2, keepdims=False)

    small_names = ["a_conv_b", "a_norm_g", "a_norm_b", "b_norm_g", "b_norm_b", "b_spatial_w", "b_spatial_b", "c_sinks", "ffn_conv_b",
                   "a_conv_w", "c_b_qkv", "ffn_conv_w", "ln_g", "ln_b"]
    small_w = [a_conv_b, a_norm_g, a_norm_b, b_norm_g, b_norm_b, b_spatial_w, b_spatial_b, c_sinks, ffn_conv_b,
               a_conv_w, c_b_qkv, ffn_conv_w, ln_g, ln_b]
    small_m = [m_a_conv_b, m_a_norm_g, m_a_norm_b, m_b_norm_g, m_b_norm_b, m_b_spatial_w, m_b_spatial_b, m_c_sinks, m_ffn_conv_b,
               m_a_conv_w, m_c_b_qkv, m_ffn_conv_w, m_ln_g, m_ln_b]
    small_v = [v_a_conv_b, v_a_norm_g, v_a_norm_b, v_b_norm_g, v_b_norm_b, v_b_spatial_w, v_b_spatial_b, v_c_sinks, v_ffn_conv_b,
               v_a_conv_w, v_c_b_qkv, v_ffn_conv_w, v_ln_g, v_ln_b]
    gs[9:] = [my_shard(g, w.shape[-1]) for g, w in zip(gs[9:], small_w[9:])]
    two_d = [(-1, w.shape[-1]) for w in small_w]
    outs = _adamw_small([g.reshape((N_DEV,) + w.reshape(s).shape) for g, w, s in zip(gs, small_w, two_d)],
                        [w.reshape(s) for w, s in zip(small_w, two_d)], [m.reshape(s) for m, s in zip(small_m, two_d)],
                        [v.reshape(s) for v, s in zip(small_v, two_d)], "adamw_small")
    small = {nm: [o.reshape(w.shape) for o in outs[4 * a:4 * a + 4]] for a, (nm, w) in enumerate(zip(small_names, small_w))}

    res = {**big, **small}
    order = ["ab_w_in", "a_conv_w", "a_conv_b", "a_norm_g", "a_norm_b", "b_norm_g", "b_norm_b", "b_spatial_w", "b_spatial_b", "ab_w_out",
             "c_w_qkv", "c_b_qkv", "c_sinks", "c_w_o", "ffn_w_up", "ffn_conv_w", "ffn_conv_b", "ffn_w_down", "ln_g", "ln_b"]
    return (loss, grad_x[None], *[res[nm][0] for nm in order], *[res[nm][1] for nm in order],
            *[res[nm][2] for nm in order], *[res[nm][3] for nm in order])
```

```python
import functools
import math

import jax
import jax.numpy as jnp
from jax import lax
from jax.experimental import pallas as pl
from jax.experimental.pallas import tpu as pltpu

F32 = jnp.float32
BF16 = jnp.bfloat16

N_DEV = 8
D_MODEL = 1024
A_WIDTH = 512
A_KERNEL = 31
B_GROUPS = 4
B_CHUNK = 128
HEAD_DIM = 64
N_Q_HEADS = 16
N_KV_HEADS = 2
ATT_BLOCK = 128
D_FF = 2816
FFN_KERNEL = 3
ALPHA = (2.0 * 2) ** 0.25
LN_EPS = 1e-5
GELU_K = math.sqrt(2.0 / math.pi)
GELU_C = 0.044715
ADAM_LR = 0.001
ADAM_B1 = 0.9
ADAM_B2 = 0.999
ADAM_EPS = 1e-08
ADAM_WD = 0.01
ADAM_STEP = 10
VMEM_LIMIT = 56 * 1024 * 1024
MESH_ID = pl.DeviceIdType.MESH


def _params(*sem):
    return pltpu.CompilerParams(dimension_semantics=sem, vmem_limit_bytes=VMEM_LIMIT)


def _gelu(x):
    t = jnp.tanh(GELU_K * x * (1.0 + GELU_C * x * x))
    return 0.5 * x * (1.0 + t)


def _gelu_and_grad(x):
    x2 = x * x
    t = jnp.tanh(GELU_K * x * (1.0 + GELU_C * x2))
    g = 0.5 * x * (1.0 + t)
    dg = 0.5 * (1.0 + t) + 0.5 * x * (1.0 - t * t) * (GELU_K * (1.0 + 3.0 * GELU_C * x2))
    return g, dg


def _sigmoid(x):
    return 1.0 / (1.0 + jnp.exp(-x))


def _ln_stats(z):
    mu = jnp.mean(z, axis=-1, keepdims=True)
    zc = z - mu
    var = jnp.mean(zc * zc, axis=-1, keepdims=True)
    r = lax.rsqrt(var + LN_EPS)
    return zc * r, r


def _ln_bwd_rows(dn, nh, r):
    return r * (dn - jnp.mean(dn, axis=-1, keepdims=True) - nh * jnp.mean(dn * nh, axis=-1, keepdims=True))


def _colsum(x):
    return jnp.sum(x, axis=0, keepdims=True)


def _dot(a, b, dims):
    return lax.dot_general(a.astype(BF16), b.astype(BF16), (dims, ((), ())), preferred_element_type=F32)


NN = ((1,), (0,))
NT = ((1,), (1,))
TN = ((0,), (0,))


ANY = pl.BlockSpec(memory_space=pl.ANY)
N_RELATIONS = N_DEV - 1


def _my_place():
    return lax.axis_index("x"), lax.axis_index("y"), lax.axis_index("c")


class _Comm:
    def __init__(self, gather=(), exchange=()):
        exchange = [e if isinstance(e, tuple) else (e, 0, e.shape[1]) for e in exchange]
        self.arrs = list(gather) + [e[0] for e in exchange]
        self.n_gather = len(gather)
        self.n = len(self.arrs)
        self.rows = [None] * self.n_gather + [pl.ds(lo, n) for _, lo, n in exchange]

    def out_shape(self):
        return [jax.ShapeDtypeStruct(((N_DEV,) + a.shape) if i < self.n_gather else a.shape, a.dtype)
                for i, a in enumerate(self.arrs)]

    def sems(self):
        return [pltpu.SemaphoreType.DMA((self.n, N_RELATIONS)), pltpu.SemaphoreType.DMA((self.n, N_RELATIONS)),
                pltpu.SemaphoreType.DMA((self.n,))]

    def _gather_copy(self, ins, outs, sems, a, k, place, to, from_input=False):
        px, py, pc = place
        block = outs[a].at[4 * px + 2 * py + pc]
        return pltpu.make_async_remote_copy(
            src_ref=ins[a] if from_input else block, dst_ref=block,
            send_sem=sems[0].at[a, k], recv_sem=sems[1].at[a, k], device_id=to, device_id_type=MESH_ID)

    def _exchange_copy(self, ins, outs, sems, a, k, landing=False):
        x, y, c = _my_place()
        me = 4 * x + 2 * y + c
        peer = (x ^ (k >> 2), y ^ ((k >> 1) & 1), c ^ (k & 1))
        return pltpu.make_async_remote_copy(
            src_ref=ins[a].at[me ^ k, self.rows[a]], dst_ref=outs[a].at[(me ^ k) if landing else me, self.rows[a]],
            send_sem=sems[0].at[a, k - 1], recv_sem=sems[1].at[a, k - 1], device_id=peer, device_id_type=MESH_ID)

    def _local_copy(self, ins, outs, sems, a):
        x, y, c = _my_place()
        me = 4 * x + 2 * y + c
        if a < self.n_gather:
            return pltpu.make_async_copy(ins[a], outs[a].at[me], sems[2].at[a])
        return pltpu.make_async_copy(ins[a].at[me, self.rows[a]], outs[a].at[me, self.rows[a]], sems[2].at[a])

    def _first_stage(self, ins, outs, sems, a):
        x, y, c = _my_place()
        me = (x, y, c)
        chips = [(1 - x, y), (x, 1 - y), (1 - x, 1 - y)]
        return ([self._gather_copy(ins, outs, sems, a, 0, me, (x, y, 1 - c), from_input=True)]
                + [self._gather_copy(ins, outs, sems, a, 1 + j, me, (*chip, c), from_input=True) for j, chip in enumerate(chips)])

    def start(self, ins, outs, sems):
        for a in range(self.n):
            self._local_copy(ins, outs, sems, a).start()
        for a in range(self.n_gather):
            for cp in self._first_stage(ins, outs, sems, a):
                cp.start()
        for k in range(1, N_DEV):
            for a in range(self.n_gather, self.n):
                self._exchange_copy(ins, outs, sems, a, k).start()

    def forward(self, ins, outs, sems):
        x, y, c = _my_place()
        me, sibling = (x, y, c), (x, y, 1 - c)
        for j, chip in enumerate([(1 - x, y), (x, 1 - y), (1 - x, 1 - y)]):
            for a in range(self.n_gather):
                self._gather_copy(ins, outs, sems, a, 1 + j, (*chip, c), me).wait_recv()
                self._gather_copy(ins, outs, sems, a, 4 + j, (*chip, c), sibling).start()

    def finish(self, ins, outs, sems):
        x, y, c = _my_place()
        me, sibling = (x, y, c), (x, y, 1 - c)
        chips = [(1 - x, y), (x, 1 - y), (1 - x, 1 - y)]
        passed = [self._gather_copy(ins, outs, sems, a, 4 + j, (*chip, c), sibling)
                  for j, chip in enumerate(chips) for a in range(self.n_gather)]
        for a in range(self.n_gather):
            self._gather_copy(ins, outs, sems, a, 0, sibling, me).wait_recv()
            for j, chip in enumerate(chips):
                self._gather_copy(ins, outs, sems, a, 4 + j, (*chip, 1 - c), me).wait_recv()
        for k in range(1, N_DEV):
            for a in range(self.n_gather, self.n):
                self._exchange_copy(ins, outs, sems, a, k, landing=True).wait_recv()
        for a in range(self.n_gather):
            for cp in self._first_stage(ins, outs, sems, a):
                cp.wait_send()
        for cp in passed:
            cp.wait_send()
        for k in range(1, N_DEV):
            for a in range(self.n_gather, self.n):
                self._exchange_copy(ins, outs, sems, a, k).wait_send()
        for a in range(self.n):
            self._local_copy(ins, outs, sems, a).wait()


def _comm_only(comm, name):
    def body(*refs):
        ins, outs, sems = refs[:comm.n], refs[comm.n:2 * comm.n], refs[2 * comm.n:]
        comm.start(ins, outs, sems)
        comm.forward(ins, outs, sems)
        comm.finish(ins, outs, sems)

    return pl.pallas_call(body, name=name, in_specs=[ANY] * comm.n, out_specs=[ANY] * comm.n,
                          out_shape=comm.out_shape(), scratch_shapes=comm.sems())(*comm.arrs)


def _call(body, *, name, grid, in_specs, out_specs, out_shape, args, sem, scratch_shapes=(), comm=None):
    in_specs, out_specs, out_shape, scratch_shapes = list(in_specs), list(out_specs), list(out_shape), list(scratch_shapes)
    if comm is None:
        outs = pl.pallas_call(body, name=name, grid=grid, in_specs=in_specs, out_specs=out_specs, out_shape=out_shape,
                              scratch_shapes=scratch_shapes, compiler_params=_params(*sem))(*args)
        return list(outs), []
    n_in, n_out, n_scr, nc = len(in_specs), len(out_specs), len(scratch_shapes), comm.n

    def wrapped(*refs):
        ins, refs = refs[:n_in], refs[n_in:]
        c_in, refs = refs[:nc], refs[nc:]
        outs, refs = refs[:n_out], refs[n_out:]
        c_out, refs = refs[:nc], refs[nc:]
        scr, sems = refs[:n_scr], refs[n_scr:]
        step = functools.reduce(lambda acc, ax: acc * grid[ax] + pl.program_id(ax), range(len(grid)), 0)
        steps = math.prod(grid)

        @pl.when(step == 0)
        def _():
            comm.start(c_in, c_out, sems)

        @pl.when(step == steps - 1)
        def _():
            comm.forward(c_in, c_out, sems)

        body(*ins, *outs, *scr)

        @pl.when(step == steps - 1)
        def _():
            comm.finish(c_in, c_out, sems)

    outs = pl.pallas_call(
        wrapped, name=name, grid=grid, in_specs=in_specs + [ANY] * nc, out_specs=out_specs + [ANY] * nc,
        out_shape=out_shape + comm.out_shape(), scratch_shapes=scratch_shapes + comm.sems(),
        compiler_params=_params(*(["arbitrary"] * len(grid))))(*args, *comm.arrs)
    return list(outs[:n_out]), list(outs[n_out:])


def _matmul(a, b, mode, out_dtype, name, tm, tn, tk, *, bias=None, res=None, res_scale=1.0, b_off=0, comm=None):
    tm = min(tm, a.shape[1] if mode == "tn" else a.shape[0])
    tk = min(tk, a.shape[0] if mode == "tn" else a.shape[1])
    if mode == "nn":
        (m, k), n = a.shape, b.shape[1]
        a_spec = pl.BlockSpec((tm, tk), lambda i, j, kk: (i, kk))
        b_spec = pl.BlockSpec((tk, tn), lambda i, j, kk: (kk + b_off, j))
        dims = NN
    elif mode == "nt":
        (m, k), n = a.shape, b.shape[0]
        a_spec = pl.BlockSpec((tm, tk), lambda i, j, kk: (i, kk))
        b_spec = pl.BlockSpec((tn, tk), lambda i, j, kk: (j, kk + b_off))
        dims = NT
    else:
        (k, m), n = a.shape, b.shape[1]
        a_spec = pl.BlockSpec((tk, tm), lambda i, j, kk: (kk, i))
        b_spec = pl.BlockSpec((tk, tn), lambda i, j, kk: (kk, j))
        dims = TN
    assert m % tm == 0 and n % tn == 0 and k % tk == 0, (name, m, n, k)
    nk = k // tk
    in_specs = [a_spec, b_spec]
    args = [a, b]
    if bias is not None:
        in_specs.append(pl.BlockSpec((1, tn), lambda i, j, kk: (0, j)))
        args.append(bias)
    if res is not None:
        in_specs.append(pl.BlockSpec((tm, tn), lambda i, j, kk: (i, j)))
        args.append(res)

    def finish(out, refs, o_ref):
        pos = 2
        if bias is not None:
            out = out + refs[pos][...]
            pos += 1
        if res is not None:
            out = out + res_scale * refs[pos][...].astype(F32)
        o_ref[...] = out.astype(out_dtype)

    def body_one_step(*refs):
        finish(_dot(refs[0][...], refs[1][...], dims), refs, refs[-1])

    def body(*refs):
        a_ref, b_ref = refs[0], refs[1]
        o_ref, acc = refs[-2], refs[-1]
        kk = pl.program_id(2)

        @pl.when(kk == 0)
        def _():
            acc[...] = jnp.zeros_like(acc)

        acc[...] += _dot(a_ref[...], b_ref[...], dims)

        @pl.when(kk == nk - 1)
        def _():
            finish(acc[...], refs, o_ref)

    (out,), moved = _call(
        body_one_step if nk == 1 else body, name=name, grid=(m // tm, n // tn, nk),
        in_specs=in_specs, out_specs=[pl.BlockSpec((tm, tn), lambda i, j, kk: (i, j))],
        out_shape=[jax.ShapeDtypeStruct((m, n), out_dtype)],
        scratch_shapes=[] if nk == 1 else [pltpu.VMEM((tm, tn), F32)],
        sem=("parallel", "parallel", "arbitrary"), args=args, comm=comm)
    return out if comm is None else (out, moved)


def _matmul_tn_pair(a0, a1, b, out_dtype, name, tm, tn, tk, comm=None):
    (k, m), n = a0.shape, b.shape[1]
    tk = min(tk, k)
    assert a1.shape == a0.shape and m % tm == 0 and n % tn == 0 and k % tk == 0, (name, m, n, k)
    mi, nk = m // tm, k // tk

    def body(a0_ref, a1_ref, b_ref, o_ref, acc):
        i, kk = pl.program_id(0), pl.program_id(2)

        @pl.when(kk == 0)
        def _():
            acc[...] = jnp.zeros_like(acc)

        @pl.when(i < mi)
        def _():
            acc[...] += _dot(a0_ref[...], b_ref[...], TN)

        @pl.when(i >= mi)
        def _():
            acc[...] += _dot(a1_ref[...], b_ref[...], TN)

        @pl.when(kk == nk - 1)
        def _():
            o_ref[...] = acc[...].astype(out_dtype)

    (out,), moved = _call(
        body, name=name, grid=(2 * mi, n // tn, nk),
        in_specs=[pl.BlockSpec((tk, tm), lambda i, j, kk: (jnp.where(i < mi, kk, nk - 1), jnp.minimum(i, mi - 1))),
                  pl.BlockSpec((tk, tm), lambda i, j, kk: (jnp.where(i >= mi, kk, 0), jnp.maximum(i - mi, 0))),
                  pl.BlockSpec((tk, tn), lambda i, j, kk: (kk, j))],
        out_specs=[pl.BlockSpec((tm, tn), lambda i, j, kk: (i, j))],
        out_shape=[jax.ShapeDtypeStruct((2 * m, n), out_dtype)],
        scratch_shapes=[pltpu.VMEM((tm, tn), F32)],
        sem=("parallel", "parallel", "arbitrary"), args=(a0, a1, b), comm=comm)
    return out if comm is None else (out, moved)


def _residual_input(x_ref, prev_refs):
    if not prev_refs:
        return x_ref[...]
    nh, _ = _ln_stats(x_ref[...])
    return nh * prev_refs[0][...] + prev_refs[1][...]


def _matmul_res_ln(a, b, x, g, beta, name, tm, prev=None, comm=None):
    t, k = a.shape
    d = b.shape[1]
    tm = min(tm, t)
    assert t % tm == 0
    n_prev = 0 if prev is None else 2

    def body(a_ref, b_ref, x_ref, g_ref, beta_ref, *rest):
        z_ref, xo_ref = rest[n_prev:]
        z = ALPHA * _residual_input(x_ref, rest[:n_prev]) + _dot(a_ref[...], b_ref[...], NN)
        nh, _ = _ln_stats(z)
        z_ref[...] = z
        xo_ref[...] = (nh * g_ref[...] + beta_ref[...]).astype(BF16)

    row = pl.BlockSpec((tm, d), lambda i: (i, 0))
    vec = pl.BlockSpec((1, d), lambda i: (0, 0))
    outs, moved = _call(
        body, name=name, grid=(t // tm,),
        in_specs=[pl.BlockSpec((tm, k), lambda i: (i, 0)), pl.BlockSpec((k, d), lambda i: (0, 0)), row, vec, vec] + [vec] * n_prev,
        out_specs=[row, row],
        out_shape=[jax.ShapeDtypeStruct((t, d), F32), jax.ShapeDtypeStruct((t, d), BF16)],
        sem=("parallel",), args=(a, b, x, g, beta, *(prev or ())), comm=comm)
    return outs if comm is None else (outs, moved)


def _matmul_ln_bwd(parts, b, z, g, dres, name, tm, comm=None):
    m = parts[0][0].shape[0]
    d = b.shape[1]
    tm = min(tm, m)
    n = len(parts)
    assert m % tm == 0 and all(row % a.shape[1] == 0 for a, row in parts)

    def body(*refs):
        z_ref, g_ref, dres_ref = refs[2 * n:2 * n + 3]
        dz_ref, dg_ref, db_ref = refs[-3:]

        @pl.when(pl.program_id(0) == 0)
        def _():
            dg_ref[...] = jnp.zeros_like(dg_ref)
            db_ref[...] = jnp.zeros_like(db_ref)

        dy = ALPHA * dres_ref[...]
        for p in range(n):
            dy = dy + _dot(refs[p][...], refs[n + p][...], NN)
        nh, r = _ln_stats(z_ref[...])
        dg_ref[...] += _colsum(dy * nh)
        db_ref[...] += _colsum(dy)
        dz_ref[...] = _ln_bwd_rows(dy * g_ref[...], nh, r)

    def b_spec(a, row):
        blk = row // a.shape[1]
        return pl.BlockSpec((a.shape[1], d), lambda i: (blk, 0))

    row = pl.BlockSpec((tm, d), lambda i: (i, 0))
    vec = pl.BlockSpec((1, d), lambda i: (0, 0))
    vshape = jax.ShapeDtypeStruct((1, d), F32)
    outs, moved = _call(
        body, name=name, grid=(m // tm,),
        in_specs=[pl.BlockSpec((tm, a.shape[1]), lambda i: (i, 0)) for a, _ in parts] + [b_spec(a, r_) for a, r_ in parts]
        + [row, vec, row],
        out_specs=[row, vec, vec], out_shape=[jax.ShapeDtypeStruct((m, d), F32), vshape, vshape],
        sem=("arbitrary",), args=(*[a for a, _ in parts], *([b] * n), z, g, dres), comm=comm)
    return outs if comm is None else (outs, moved)


def _matmul_res_ln_loss(a, b, x, g, beta, target, name, tm, prev):
    t, k = a.shape
    d = b.shape[1]
    tm = min(tm, t)

    def body(a_ref, b_ref, x_ref, g_ref, beta_ref, t_ref, gp_ref, bp_ref, dz_ref, dg_ref, db_ref, loss_ref):
        @pl.when(pl.program_id(0) == 0)
        def _():
            dg_ref[...] = jnp.zeros_like(dg_ref)
            db_ref[...] = jnp.zeros_like(db_ref)
            loss_ref[...] = jnp.zeros_like(loss_ref)

        nh, r = _ln_stats(ALPHA * _residual_input(x_ref, (gp_ref, bp_ref)) + _dot(a_ref[...], b_ref[...], NN))
        err = nh * g_ref[...] + beta_ref[...] - t_ref[...]
        loss_ref[...] += _colsum(err * err)
        dy = err * (1.0 / d)
        dg_ref[...] += _colsum(dy * nh)
        db_ref[...] += _colsum(dy)
        dz_ref[...] = _ln_bwd_rows(dy * g_ref[...], nh, r)

    row = pl.BlockSpec((tm, d), lambda i: (i, 0))
    vec = pl.BlockSpec((1, d), lambda i: (0, 0))
    vshape = jax.ShapeDtypeStruct((1, d), F32)
    return pl.pallas_call(
        body, name=name, grid=(t // tm,),
        in_specs=[pl.BlockSpec((tm, k), lambda i: (i, 0)), pl.BlockSpec((k, d), lambda i: (0, 0)), row, vec, vec, row, vec, vec],
        out_specs=[row, vec, vec, vec],
        out_shape=[jax.ShapeDtypeStruct((t, d), F32), vshape, vshape, vshape],
        compiler_params=_params("arbitrary"),
    )(a, b, x, g, beta, target, *prev)


FFN_HALO = 16
FFN_CHUNK = 256
LANES = 128
SUBLANES = 8


def _rows_up(e, start, rows):
    if start % SUBLANES == 0:
        return e[start:start + rows]
    return pltpu.roll(e, e.shape[0] - start, 0)[0:rows]


def _fold(x):
    return jnp.sum(x.reshape(x.shape[0] // SUBLANES, SUBLANES, x.shape[1]), axis=0)


def _ffn_mid_fwd(h, cw, cb, name, tm=1024, tc=1408, comm=None):
    t, f2 = h.shape
    tm = min(tm, t)
    f = f2 // 2
    nj, nt, hb = f // tc, t // tm, tm // FFN_HALO

    ch = min(FFN_CHUNK, tm)

    def body(hg, hgp, hv, hvp, cwg, cwv, cbg, cbv, u_ref, cg_ref, cv_ref):
        i = pl.program_id(1)
        o = FFN_HALO - FFN_KERNEL + 1
        for lg in range(tc // LANES):
            cols = slice(lg * LANES, (lg + 1) * LANES)
            wg, wv = [cwg[k:k + 1, cols] for k in range(FFN_KERNEL)], [cwv[k:k + 1, cols] for k in range(FFN_KERNEL)]
            bg, bv = cbg[:, cols], cbv[:, cols]

            def emit(base, eg, ev):
                cg = wg[0] * _rows_up(eg, o, ch) + wg[1] * _rows_up(eg, o + 1, ch) + wg[2] * _rows_up(eg, o + 2, ch) + bg
                cv = wv[0] * _rows_up(ev, o, ch) + wv[1] * _rows_up(ev, o + 1, ch) + wv[2] * _rows_up(ev, o + 2, ch) + bv
                u_ref[pl.ds(base, ch), cols] = (_gelu(cg) * cv).astype(BF16)
                cg_ref[pl.ds(base, ch), cols] = cg.astype(BF16)
                cv_ref[pl.ds(base, ch), cols] = cv.astype(BF16)

            def first(main, prev):
                return jnp.concatenate([jnp.where(i > 0, prev[:, cols].astype(F32), 0.0), main[0:ch, cols].astype(F32)], axis=0)

            def inner(c, carry):
                base = pl.multiple_of(c * ch, ch)
                emit(base, hg[pl.ds(base - FFN_HALO, ch + FFN_HALO), cols].astype(F32),
                     hv[pl.ds(base - FFN_HALO, ch + FFN_HALO), cols].astype(F32))
                return carry

            emit(0, first(hg, hgp), first(hv, hvp))
            if tm > ch:
                lax.fori_loop(1, tm // ch, inner, 0)

    def main_spec(off):
        return pl.BlockSpec((tm, tc), lambda j, i: (i, j + off))

    def prev_spec(off):
        return pl.BlockSpec((FFN_HALO, tc), lambda j, i: (jnp.maximum(i * hb - 1, 0), j + off))

    def par_spec(rows, off):
        return pl.BlockSpec((rows, tc), lambda j, i: (0, j + off))

    outs, moved = _call(
        body, name=name, grid=(nj, nt),
        in_specs=[main_spec(0), prev_spec(0), main_spec(nj), prev_spec(nj),
                  par_spec(FFN_KERNEL, 0), par_spec(FFN_KERNEL, nj), par_spec(1, 0), par_spec(1, nj)],
        out_specs=[pl.BlockSpec((tm, tc), lambda j, i: (i, j))] * 3,
        out_shape=[jax.ShapeDtypeStruct((t, f), BF16)] * 3,
        sem=("parallel", "arbitrary"), args=(h, h, h, h, cw, cw, cb, cb), comm=comm)
    return outs if comm is None else (outs, moved)


def _ffn_mid_bwd(h, cg, cv, du, cw, name, tm=1024, tc=1408, comm=None):
    t, f2 = h.shape
    tm = min(tm, t)
    f = f2 // 2
    nj, nt, hb = f // tc, t // tm, tm // FFN_HALO

    ch = min(FFN_CHUNK, tm)
    ahead = ch + SUBLANES
    n_ch = tm // ch

    def body(hg, hv, cg_ref, cgn_ref, cv_ref, cvn_ref, du_ref, dun_ref, cwg, cwv,
             dhg_ref, dhv_ref, dcwg_ref, dcwv_ref, dcbg_ref, dcbv_ref):
        i = pl.program_id(1)

        @pl.when(i == 0)
        def _():
            for ref in (dcwg_ref, dcwv_ref, dcbg_ref, dcbv_ref):
                ref[...] = jnp.zeros_like(ref)

        for lg in range(tc // LANES):
            cols = slice(lg * LANES, (lg + 1) * LANES)
            wg, wv = [cwg[k:k + 1, cols] for k in range(FFN_KERNEL)], [cwv[k:k + 1, cols] for k in range(FFN_KERNEL)]

            def emit(base, cg_e, cv_e, du_e, acc):
                cg_a, cv_a, du_a = cg_e[0:ahead], cv_e[0:ahead], du_e[0:ahead]
                gl, dgl = _gelu_and_grad(cg_a)

                def back(d, h_ref, w, dh_ref):
                    later = [d[0:ch], _rows_up(d, 1, ch), _rows_up(d, 2, ch)]
                    dh_ref[pl.ds(base, ch), cols] = (w[2] * later[0] + w[1] * later[1] + w[0] * later[2]).astype(BF16)
                    h_own = h_ref[pl.ds(base, ch), cols].astype(F32)
                    return [_fold(later[0])] + [_fold(later[FFN_KERNEL - 1 - k] * h_own) for k in range(FFN_KERNEL)]

                sums = back(du_a * cv_a * dgl, hg, wg, dhg_ref) + back(du_a * gl, hv, wv, dhv_ref)
                return tuple(a + s_ for a, s_ in zip(acc, sums))

            def inner(c, acc):
                base = pl.multiple_of(c * ch, ch)
                rows = pl.ds(base, ch + FFN_HALO)
                return emit(base, cg_ref[rows, cols].astype(F32), cv_ref[rows, cols].astype(F32), du_ref[rows, cols].astype(F32), acc)

            def last(acc):
                def rows(main, after):
                    return jnp.concatenate([main[tm - ch:tm, cols].astype(F32), after], axis=0)

                du_next = jnp.where(i < nt - 1, dun_ref[:, cols].astype(F32), 0.0)
                return emit(tm - ch, rows(cg_ref, cgn_ref[:, cols].astype(F32)), rows(cv_ref, cvn_ref[:, cols].astype(F32)),
                            rows(du_ref, du_next), acc)

            acc = (jnp.zeros((SUBLANES, LANES), F32),) * (2 * (1 + FFN_KERNEL))
            if n_ch > 1:
                acc = lax.fori_loop(0, n_ch - 1, inner, acc)
            acc = last(acc)
            dcbg_ref[:, cols] += _colsum(acc[0])
            dcbv_ref[:, cols] += _colsum(acc[1 + FFN_KERNEL])
            for k in range(FFN_KERNEL):
                dcwg_ref[k:k + 1, cols] += _colsum(acc[1 + k])
                dcwv_ref[k:k + 1, cols] += _colsum(acc[2 + FFN_KERNEL + k])

    last_blk = t // FFN_HALO - 1

    def main_spec(off):
        return pl.BlockSpec((tm, tc), lambda j, i: (i, j + off))

    def next_spec(off):
        return pl.BlockSpec((FFN_HALO, tc), lambda j, i: (jnp.minimum((i + 1) * hb, last_blk), j + off))

    def par_spec(rows, off):
        return pl.BlockSpec((rows, tc), lambda j, i: (0, j + off))

    out_tile = pl.BlockSpec((tm, tc), lambda j, i: (i, j))
    outs, moved = _call(
        body, name=name, grid=(nj, nt),
        in_specs=[main_spec(0), main_spec(nj), main_spec(0), next_spec(0), main_spec(0), next_spec(0), main_spec(0), next_spec(0),
                  par_spec(FFN_KERNEL, 0), par_spec(FFN_KERNEL, nj)],
        out_specs=[out_tile, out_tile, par_spec(FFN_KERNEL, 0), par_spec(FFN_KERNEL, 0), par_spec(1, 0), par_spec(1, 0)],
        out_shape=[jax.ShapeDtypeStruct((t, f), BF16), jax.ShapeDtypeStruct((t, f), BF16),
                   jax.ShapeDtypeStruct((FFN_KERNEL, f), F32), jax.ShapeDtypeStruct((FFN_KERNEL, f), F32),
                   jax.ShapeDtypeStruct((1, f), F32), jax.ShapeDtypeStruct((1, f), F32)],
        sem=("parallel", "arbitrary"), args=(h, h, cg, cg, cv, cv, du, du, cw, cw), comm=comm)
    return outs if comm is None else (outs, moved)


MIX_HALO = 32


def _glu(hh):
    return hh[:, 0:A_WIDTH] * _sigmoid(hh[:, A_WIDTH:2 * A_WIDTH])


def _fill_row_shifts(s):
    rows = s.shape[1] - SUBLANES
    for j in range(1, SUBLANES):
        s[j, 0:rows, :] = s[0, pl.ds(j, rows), :]


def _rows_from(s, start, rows):
    j = start % SUBLANES
    return s[j, start - j:start - j + rows, :]


def _tril_mask():
    return lax.broadcasted_iota(jnp.int32, (B_CHUNK, B_CHUNK), 0) >= lax.broadcasted_iota(jnp.int32, (B_CHUNK, B_CHUNK), 1)


def _spatial_mix(q, ms_ref, sbt_ref, tm):
    mask = _tril_mask()
    ws = [jnp.where(mask, ms_ref[g], 0.0).astype(BF16) for g in range(B_GROUPS)]
    qb = q.astype(BF16)
    rows = []
    for c in range(tm // B_CHUNK):
        cols = [_dot(ws[g], qb[c * B_CHUNK:(c + 1) * B_CHUNK, g * 128:(g + 1) * 128], NN) + sbt_ref[:, g:g + 1]
                for g in range(B_GROUPS)]
        rows.append(jnp.concatenate(cols, axis=1))
    return jnp.concatenate(rows, axis=0)


def _mixer_mid_fwd(h, cw, cb, ag, ab, bg, bb, ms, sbt, name, tm=256, comm=None):
    t = h.shape[0]
    nt, hb = t // tm, tm // MIX_HALO
    o = MIX_HALO - A_KERNEL + 1

    def body(h_ref, hp_ref, cw_ref, cb_ref, ag_ref, ab_ref, bg_ref, bb_ref, ms_ref, sbt_ref, cat_ref, y_ref, sp):
        i = pl.program_id(0)
        sp[0, 0:MIX_HALO, :] = jnp.where(i > 0, _glu(hp_ref[:, 0:2 * A_WIDTH].astype(F32)), 0.0)
        sp[0, MIX_HALO:, :] = _glu(h_ref[:, 0:2 * A_WIDTH].astype(F32))
        _fill_row_shifts(sp)
        y = jnp.zeros((tm, A_WIDTH), F32) + cb_ref[...]
        for k in range(A_KERNEL):
            y = y + cw_ref[k:k + 1, :] * _rows_from(sp, o + k, tm)
        y_ref[...] = y.astype(BF16)
        nh, _ = _ln_stats(y)
        ln = nh * ag_ref[...] + ab_ref[...]
        cat_ref[:, 0:A_WIDTH] = (ln * _sigmoid(ln)).astype(BF16)
        u = _gelu(h_ref[:, 1024:1536].astype(F32))
        nb, _ = _ln_stats(_gelu(h_ref[:, 1536:2048].astype(F32)))
        mixed = _spatial_mix(nb * bg_ref[...] + bb_ref[...], ms_ref, sbt_ref, tm)
        cat_ref[:, A_WIDTH:] = (u * mixed).astype(BF16)

    vec = pl.BlockSpec((1, A_WIDTH), lambda i: (0, 0))
    outs, moved = _call(
        body, name=name, grid=(nt,),
        in_specs=[pl.BlockSpec((tm, 2048), lambda i: (i, 0)),
                  pl.BlockSpec((MIX_HALO, 2048), lambda i: (jnp.maximum(i * hb - 1, 0), 0)),
                  pl.BlockSpec((A_KERNEL, A_WIDTH), lambda i: (0, 0)), vec, vec, vec, vec, vec,
                  pl.BlockSpec((B_GROUPS, B_CHUNK, B_CHUNK), lambda i: (0, 0, 0)),
                  pl.BlockSpec((B_CHUNK, B_GROUPS), lambda i: (0, 0))],
        out_specs=[pl.BlockSpec((tm, D_MODEL), lambda i: (i, 0)), pl.BlockSpec((tm, A_WIDTH), lambda i: (i, 0))],
        out_shape=[jax.ShapeDtypeStruct((t, D_MODEL), BF16), jax.ShapeDtypeStruct((t, A_WIDTH), BF16)],
        scratch_shapes=[pltpu.VMEM((SUBLANES, tm + MIX_HALO, A_WIDTH), F32)],
        sem=("parallel",), args=(h, h, cw, cb, ag, ab, bg, bb, ms, sbt), comm=comm)
    return outs if comm is None else (outs, moved)


def _mixer_mid_bwd(h, y, dcat, cw, ag, ab, bg, bb, ms, mst, sbt, name, tm=256, comm=None):
    t = h.shape[0]
    nt, hb = t // tm, tm // MIX_HALO
    r = tm + MIX_HALO
    nchunk = tm // B_CHUNK

    def body(h_ref, y_ref, yn_ref, dc_ref, dcn_ref, cw_ref, ag_ref, ab_ref, bg_ref, bb_ref, ms_ref, mst_ref, sbt_ref,
             dh_ref, dcw_ref, dcb_ref, dag_ref, dab_ref, dbg_ref, dbb_ref, dms_ref, dsb_ref, sdy, sbacc):
        i = pl.program_id(0)

        @pl.when(i == 0)
        def _():
            for ref in (dcw_ref, dcb_ref, dag_ref, dab_ref, dbg_ref, dbb_ref, dms_ref, dsb_ref, sbacc):
                ref[...] = jnp.zeros_like(ref)

        nh, rs = _ln_stats(jnp.concatenate([y_ref[...].astype(F32), yn_ref[...].astype(F32)], axis=0))
        ln = nh * ag_ref[...] + ab_ref[...]
        sg = _sigmoid(ln)
        dao = jnp.concatenate([dc_ref[:, 0:A_WIDTH].astype(F32),
                               jnp.where(i < nt - 1, dcn_ref[:, 0:A_WIDTH].astype(F32), 0.0)], axis=0)
        dln = dao * (sg * (1.0 + ln * (1.0 - sg)))
        dag_ref[...] += _colsum(dln[0:tm] * nh[0:tm])
        dab_ref[...] += _colsum(dln[0:tm])
        sdy[0] = _ln_bwd_rows(dln * ag_ref[...], nh, rs)
        _fill_row_shifts(sdy)
        dcb_ref[...] += _colsum(sdy[0, 0:tm, :])
        av = h_ref[:, 0:A_WIDTH].astype(F32)
        s = _sigmoid(h_ref[:, A_WIDTH:2 * A_WIDTH].astype(F32))
        p_own = av * s
        dp = jnp.zeros((tm, A_WIDTH), F32)
        for k in range(A_KERNEL):
            later = _rows_from(sdy, A_KERNEL - 1 - k, tm)
            dcw_ref[k:k + 1, :] += _colsum(later * p_own)
            dp = dp + cw_ref[k:k + 1, :] * later
        dh_ref[:, 0:A_WIDTH] = (dp * s).astype(BF16)
        dh_ref[:, A_WIDTH:2 * A_WIDTH] = (dp * av * s * (1.0 - s)).astype(BF16)

        u, dgu = _gelu_and_grad(h_ref[:, 1024:1536].astype(F32))
        w, dgw = _gelu_and_grad(h_ref[:, 1536:2048].astype(F32))
        nb, rb = _ln_stats(w)
        q = nb * bg_ref[...] + bb_ref[...]
        mixed = _spatial_mix(q, ms_ref, sbt_ref, tm)
        dbo = dc_ref[:, A_WIDTH:].astype(F32)
        dh_ref[:, 1024:1536] = (dbo * mixed * dgu).astype(BF16)
        dmx = dbo * u
        mask = _tril_mask()
        wst = [jnp.where(mask.T, mst_ref[g], 0.0).astype(BF16) for g in range(B_GROUPS)]
        qb = q.astype(BF16)
        dmb = dmx.astype(BF16)
        rows = []
        for c in range(nchunk):
            cols = []
            for g in range(B_GROUPS):
                rs_, cs_ = slice(c * B_CHUNK, (c + 1) * B_CHUNK), slice(g * 128, (g + 1) * 128)
                sbacc[g] += dmx[rs_, cs_]
                dms_ref[g] += _dot(dmb[rs_, cs_], qb[rs_, cs_], NT)
                cols.append(_dot(wst[g], dmb[rs_, cs_], NN))
            rows.append(jnp.concatenate(cols, axis=1))
        dq = jnp.concatenate(rows, axis=0)
        dbg_ref[...] += _colsum(dq * nb)
        dbb_ref[...] += _colsum(dq)
        dh_ref[:, 1536:2048] = (_ln_bwd_rows(dq * bg_ref[...], nb, rb) * dgw).astype(BF16)

        @pl.when(i == nt - 1)
        def _():
            for g in range(B_GROUPS):
                dms_ref[g] = jnp.where(mask, dms_ref[g], 0.0)
                dsb_ref[g] = jnp.sum(sbacc[g], axis=1, keepdims=True)

    last_blk = t // MIX_HALO - 1
    vec = pl.BlockSpec((1, A_WIDTH), lambda i: (0, 0))
    mat = pl.BlockSpec((B_GROUPS, B_CHUNK, B_CHUNK), lambda i: (0, 0, 0))
    taps = pl.BlockSpec((A_KERNEL, A_WIDTH), lambda i: (0, 0))

    def halo(width):
        return pl.BlockSpec((MIX_HALO, width), lambda i: (jnp.minimum((i + 1) * hb, last_blk), 0))

    vshape = jax.ShapeDtypeStruct((1, A_WIDTH), F32)
    outs, moved = _call(
        body, name=name, grid=(nt,),
        in_specs=[pl.BlockSpec((tm, 2048), lambda i: (i, 0)), pl.BlockSpec((tm, A_WIDTH), lambda i: (i, 0)), halo(A_WIDTH),
                  pl.BlockSpec((tm, D_MODEL), lambda i: (i, 0)), halo(D_MODEL),
                  taps, vec, vec, vec, vec, mat, mat, pl.BlockSpec((B_CHUNK, B_GROUPS), lambda i: (0, 0))],
        out_specs=[pl.BlockSpec((tm, 2048), lambda i: (i, 0)), taps, vec, vec, vec, vec, vec, mat,
                   pl.BlockSpec((B_GROUPS, B_CHUNK, 1), lambda i: (0, 0, 0))],
        out_shape=[jax.ShapeDtypeStruct((t, 2048), BF16), jax.ShapeDtypeStruct((A_KERNEL, A_WIDTH), F32),
                   vshape, vshape, vshape, vshape, vshape,
                   jax.ShapeDtypeStruct((B_GROUPS, B_CHUNK, B_CHUNK), F32), jax.ShapeDtypeStruct((B_GROUPS, B_CHUNK, 1), F32)],
        scratch_shapes=[pltpu.VMEM((SUBLANES, r, A_WIDTH), F32), pltpu.VMEM((B_GROUPS, B_CHUNK, B_CHUNK), F32)],
        sem=("arbitrary",), args=(h, y, y, dcat, dcat, cw, ag, ab, bg, bb, ms, mst, sbt), comm=comm)
    return outs if comm is None else (outs, moved)


Q_WIDTH = N_Q_HEADS * HEAD_DIM
KV_WIDTH = 2 * N_KV_HEADS * HEAD_DIM
PAIRS_PER_KV = N_Q_HEADS // N_KV_HEADS // 2
ATT_SCALE = 1.0 / math.sqrt(HEAD_DIM)


def _dup_heads(pair_cols, kv_head):
    lane = lax.broadcasted_iota(jnp.int32, pair_cols.shape, 1)
    rolled = pltpu.roll(pair_cols, HEAD_DIM, 1)
    first = lane < HEAD_DIM
    return jnp.where(first, pair_cols, rolled) if kv_head == 0 else jnp.where(first, rolled, pair_cols)


HEADS_PER_KV = N_Q_HEADS // N_KV_HEADS


def _stack_heads(ref, kh):
    lane = lax.broadcasted_iota(jnp.int32, (ATT_BLOCK, 128), 1)
    rows = []
    for pr in range(PAIRS_PER_KV):
        c0 = (kh * PAIRS_PER_KV + pr) * 128
        pair = ref[:, c0:c0 + 128]
        rows += [jnp.where(lane < HEAD_DIM, pair, jnp.zeros_like(pair)), jnp.where(lane < HEAD_DIM, jnp.zeros_like(pair), pair)]
    return jnp.concatenate(rows, axis=0)


def _unstack_heads(stacked, kh, write):
    lane = lax.broadcasted_iota(jnp.int32, (ATT_BLOCK, 128), 1)
    for pr in range(PAIRS_PER_KV):
        first = stacked[(2 * pr) * ATT_BLOCK:(2 * pr + 1) * ATT_BLOCK]
        second = stacked[(2 * pr + 1) * ATT_BLOCK:(2 * pr + 2) * ATT_BLOCK]
        write((kh * PAIRS_PER_KV + pr) * 128, jnp.where(lane < HEAD_DIM, first, second))


def _sink_row(sink_ref, kh):
    return jnp.concatenate([jnp.full((1, ATT_BLOCK), sink_ref[0, kh * HEADS_PER_KV + h], F32) for h in range(HEADS_PER_KV)], axis=1)


def _att_window_bias():
    sj = lax.broadcasted_iota(jnp.int32, (2 * ATT_BLOCK, HEADS_PER_KV * ATT_BLOCK), 0)
    qi = lax.broadcasted_iota(jnp.int32, (2 * ATT_BLOCK, HEADS_PER_KV * ATT_BLOCK), 1) & (ATT_BLOCK - 1)
    diff = qi + ATT_BLOCK - sj
    return jnp.where((diff >= 0) & (diff < ATT_BLOCK), 0.0, -jnp.inf)


def _att_probs_t(q_all, k2, bias_ref, n, sink):
    st = _dot(k2, q_all, NT) * ATT_SCALE + bias_ref[...]
    st = jnp.concatenate([jnp.where(n > 0, st[0:ATT_BLOCK], -jnp.inf), st[ATT_BLOCK:]], axis=0)
    m = jnp.maximum(jnp.max(st, axis=0, keepdims=True), sink)
    e = jnp.exp(st - m)
    es = jnp.exp(sink - m)
    inv = 1.0 / (jnp.sum(e, axis=0, keepdims=True) + es)
    return e * inv, es * inv


def _attn_fwd(qkv, sinks, name, comm=None):
    t = qkv.shape[0]
    nb = t // ATT_BLOCK
    kvb = Q_WIDTH // KV_WIDTH

    def body(sink_ref, q_ref, kv_ref, kvp_ref, o_ref, bias):
        n = pl.program_id(0)

        @pl.when(n == 0)
        def _():
            bias[...] = _att_window_bias()

        kv = jnp.concatenate([kvp_ref[...], kv_ref[...]], axis=0).astype(F32)

        def write(c0, pair):
            o_ref[:, c0:c0 + 128] = pair.astype(BF16)

        for kh in range(N_KV_HEADS):
            k2 = _dup_heads(kv[:, 0:128], kh).astype(BF16)
            v2 = _dup_heads(kv[:, 128:256], kh).astype(BF16)
            pt, _ = _att_probs_t(_stack_heads(q_ref, kh), k2, bias, n, _sink_row(sink_ref, kh))
            _unstack_heads(_dot(v2, pt, TN).T, kh, write)

    (out,), moved = _call(
        body, name=name, grid=(nb,),
        in_specs=[pl.BlockSpec(memory_space=pltpu.SMEM),
                  pl.BlockSpec((ATT_BLOCK, Q_WIDTH), lambda n: (n, 0)),
                  pl.BlockSpec((ATT_BLOCK, KV_WIDTH), lambda n: (n, kvb)),
                  pl.BlockSpec((ATT_BLOCK, KV_WIDTH), lambda n: (jnp.maximum(n - 1, 0), kvb))],
        out_specs=[pl.BlockSpec((ATT_BLOCK, Q_WIDTH), lambda n: (n, 0))],
        out_shape=[jax.ShapeDtypeStruct((t, Q_WIDTH), BF16)],
        scratch_shapes=[pltpu.VMEM((2 * ATT_BLOCK, HEADS_PER_KV * ATT_BLOCK), F32)],
        sem=("arbitrary",), args=(sinks, qkv, qkv, qkv), comm=comm)
    return out if comm is None else (out, moved)


def _attn_bwd(qkv, d_o, sinks, name, comm=None):
    t = qkv.shape[0]
    nb = t // ATT_BLOCK
    kvb = Q_WIDTH // KV_WIDTH

    def body(sink_ref, q_ref, kv_ref, kvp_ref, do_ref, dq_ref, dkv_ref, dbq_ref, dbkv_ref, dsink_ref, carry, bias):
        n = pl.program_id(0)

        @pl.when(n == 0)
        def _():
            for ref in (dbq_ref, dbkv_ref, dsink_ref, carry):
                ref[...] = jnp.zeros_like(ref)
            dkv_ref[...] = jnp.zeros_like(dkv_ref)
            bias[...] = _att_window_bias()

        @pl.when(n < nb)
        def _():
            kv = jnp.concatenate([kvp_ref[...], kv_ref[...]], axis=0).astype(F32)
            lane2 = lax.broadcasted_iota(jnp.int32, (2 * ATT_BLOCK, 128), 1)
            sink_lane = lax.broadcasted_iota(jnp.int32, (1, 128), 1)
            dsink = jnp.zeros((1, 128), F32)
            dk_parts, dv_parts = [], []

            def write(c0, pair):
                dbq_ref[:, c0:c0 + 128] += _colsum(pair)
                dq_ref[:, c0:c0 + 128] = pair.astype(BF16)

            for kh in range(N_KV_HEADS):
                k2 = _dup_heads(kv[:, 0:128], kh).astype(BF16)
                v2 = _dup_heads(kv[:, 128:256], kh).astype(BF16)
                q_all = _stack_heads(q_ref, kh)
                do_all = _stack_heads(do_ref, kh)
                pt, ps = _att_probs_t(q_all, k2, bias, n, _sink_row(sink_ref, kh))
                dpt = _dot(v2, do_all, NT)
                delta = jnp.sum(pt * dpt, axis=0, keepdims=True)
                dst = pt * (dpt - delta) * ATT_SCALE
                psd = ps * delta
                for h in range(HEADS_PER_KV):
                    dsink = dsink + jnp.where(sink_lane == kh * HEADS_PER_KV + h,
                                              -jnp.sum(psd[:, h * ATT_BLOCK:(h + 1) * ATT_BLOCK]), 0.0)
                _unstack_heads(_dot(k2, dst, TN).T, kh, write)
                dk_acc = _dot(dst, q_all, NN)
                dv_acc = _dot(pt, do_all, NN)
                dk_parts.append(dk_acc + pltpu.roll(dk_acc, HEAD_DIM, 1))
                dv_parts.append(dv_acc + pltpu.roll(dv_acc, HEAD_DIM, 1))
            dk = jnp.where(lane2 < HEAD_DIM, dk_parts[0], dk_parts[1])
            dv = jnp.where(lane2 < HEAD_DIM, dv_parts[0], dv_parts[1])
            dkv_new = jnp.concatenate([dk, dv], axis=1)
            done = carry[...] + dkv_new[0:ATT_BLOCK]

            @pl.when(n > 0)
            def _():
                dkv_ref[...] = done.astype(BF16)
                dbkv_ref[...] += _colsum(done)

            carry[...] = dkv_new[ATT_BLOCK:]
            dsink_ref[...] += dsink

        @pl.when(n == nb)
        def _():
            dkv_ref[...] = carry[...].astype(BF16)
            dbkv_ref[...] += _colsum(carry[...])

    def clamp(n):
        return jnp.minimum(n, nb - 1)

    outs, moved = _call(
        body, name=name, grid=(nb + 1,),
        in_specs=[pl.BlockSpec(memory_space=pltpu.SMEM),
                  pl.BlockSpec((ATT_BLOCK, Q_WIDTH), lambda n: (clamp(n), 0)),
                  pl.BlockSpec((ATT_BLOCK, KV_WIDTH), lambda n: (clamp(n), kvb)),
                  pl.BlockSpec((ATT_BLOCK, KV_WIDTH), lambda n: (jnp.maximum(clamp(n) - 1, 0), kvb)),
                  pl.BlockSpec((ATT_BLOCK, Q_WIDTH), lambda n: (clamp(n), 0))],
        out_specs=[pl.BlockSpec((ATT_BLOCK, Q_WIDTH), lambda n: (clamp(n), 0)),
                   pl.BlockSpec((ATT_BLOCK, KV_WIDTH), lambda n: (jnp.maximum(n - 1, 0), 0)),
                   pl.BlockSpec((1, Q_WIDTH), lambda n: (0, 0)),
                   pl.BlockSpec((1, KV_WIDTH), lambda n: (0, 0)),
                   pl.BlockSpec((1, 128), lambda n: (0, 0))],
        out_shape=[jax.ShapeDtypeStruct((t, Q_WIDTH), BF16), jax.ShapeDtypeStruct((t, KV_WIDTH), BF16),
                   jax.ShapeDtypeStruct((1, Q_WIDTH), F32), jax.ShapeDtypeStruct((1, KV_WIDTH), F32),
                   jax.ShapeDtypeStruct((1, 128), F32)],
        scratch_shapes=[pltpu.VMEM((ATT_BLOCK, KV_WIDTH), F32), pltpu.VMEM((2 * ATT_BLOCK, HEADS_PER_KV * ATT_BLOCK), F32)],
        sem=("arbitrary",), args=(sinks, qkv, qkv, qkv, d_o), comm=comm)
    return outs if comm is None else (outs, moved)


def _adamw_math(g, w, m, v):
    m = ADAM_B1 * m + (1.0 - ADAM_B1) * g
    v = ADAM_B2 * v + (1.0 - ADAM_B2) * (g * g)
    m_hat = m / (1.0 - ADAM_B1 ** ADAM_STEP)
    v_hat = v / (1.0 - ADAM_B2 ** ADAM_STEP)
    delta = -ADAM_LR * (m_hat / (jnp.sqrt(v_hat) + ADAM_EPS) + ADAM_WD * w)
    return delta, m, v


def _sum_partials(p_ref):
    g = p_ref[0].astype(F32)
    for s in range(1, N_DEV):
        g = g + p_ref[s].astype(F32)
    return g


def _adamw_big(parts, w, m, v, name, tr):
    r, c = w.shape
    parts = [p if isinstance(p, tuple) else (p, 0, p.shape[1]) for p in parts]
    tiles = [rows // tr for _, _, rows in parts]
    starts = [sum(tiles[:l]) for l in range(len(parts))]
    assert all(lo % tr == 0 and rows % tr == 0 for _, lo, rows in parts) and sum(tiles) * tr == r

    def body(*refs):
        p_refs, (w_ref, m_ref, v_ref, g_out, d_out, m_out, v_out) = refs[:len(parts)], refs[len(parts):]
        i = pl.program_id(0)
        for l, p_ref in enumerate(p_refs):
            @pl.when((i >= starts[l]) & (i < starts[l] + tiles[l]))
            def _():
                g = _sum_partials(p_ref)
                g_out[...] = g
                d_out[...], m_out[...], v_out[...] = _adamw_math(g, w_ref[...], m_ref[...], v_ref[...])

    def part_spec(l):
        return pl.BlockSpec((N_DEV, tr, c), lambda i: (0, jnp.clip(i - starts[l], 0, tiles[l] - 1) + parts[l][1] // tr, 0))

    tile = pl.BlockSpec((tr, c), lambda i: (i, 0))
    shape = jax.ShapeDtypeStruct((r, c), F32)
    return pl.pallas_call(
        body, name=name, grid=(r // tr,),
        in_specs=[part_spec(l) for l in range(len(parts))] + [tile, tile, tile],
        out_specs=[tile] * 4, out_shape=[shape] * 4,
        compiler_params=_params("parallel"),
    )(*[p[0] for p in parts], w, m, v)


def _adamw_small(parts, ws, ms, vs, name):
    n = len(ws)

    def body(*refs):
        ins, outs = refs[:4 * n], refs[4 * n:]
        for a in range(n):
            g = _sum_partials(ins[a])
            outs[4 * a][...] = g
            outs[4 * a + 1][...], outs[4 * a + 2][...], outs[4 * a + 3][...] = _adamw_math(
                g, ins[n + a][...], ins[2 * n + a][...], ins[3 * n + a][...])

    out_shape = []
    for w in ws:
        out_shape += [jax.ShapeDtypeStruct(w.shape, F32)] * 4
    return pl.pallas_call(body, name=name, out_shape=out_shape, compiler_params=_params())(*parts, *ws, *ms, *vs)


PACK_LANES = 128
PACK_ROWS = 8


def _pack(arrs):
    flat = jnp.concatenate([a.reshape(-1).astype(F32) for a in arrs])
    unit = PACK_LANES * PACK_ROWS
    total = -(-flat.shape[0] // unit) * unit
    return jnp.pad(flat, (0, total - flat.shape[0])).reshape(-1, PACK_LANES)


def _unpack(buf, shapes):
    flat = buf.reshape(N_DEV, -1)
    out, pos = [], 0
    for s in shapes:
        size = math.prod(s)
        out.append(flat[:, pos:pos + size].reshape((N_DEV,) + tuple(s)))
        pos += size
    return out


def _interleave(g):
    return jnp.transpose(g, (1, 0, 2)).reshape(g.shape[1], -1)


def _ffn_backward(dz, x_in, z_in, g_in, h, cg, cv, u, w_up_t, cw, w_down, tag, exchange=(), exchange_late=(), own_rows=0):
    du = _matmul(dz, w_down, "nt", BF16, f"ffn{tag}_du", 1024, 1408, 1024)
    d_w_down = _matmul(u, dz, "tn", BF16, f"ffn{tag}_dwdown", 1408, 1024, 2048)
    (dhg, dhv, dcwg, dcwv, dcbg, dcbv), moved = _ffn_mid_bwd(
        h, cg, cv, du, cw, f"ffn{tag}_mid_bwd", comm=_Comm(exchange=[d_w_down.reshape(N_DEV, -1, D_MODEL), *exchange]))
    d_w_up_t = _matmul_tn_pair(dhg, dhv, x_in, BF16, f"ffn{tag}_dwup", 1408, 1024, 1024,
                               comm=_Comm(exchange=exchange_late) if exchange_late else None)
    if exchange_late:
        d_w_up_t, late = d_w_up_t
        moved = moved + late
    d_up_blocks = d_w_up_t.reshape(N_DEV, -1, D_MODEL)
    outs = _matmul_ln_bwd([(dhg, 0), (dhv, D_FF)], w_up_t, z_in, g_in, dz, f"ffn{tag}_dx_ln_bwd", 256,
                          comm=_Comm(exchange=[(d_up_blocks, 0, own_rows)]) if own_rows else None)
    (dz_in, dg_in, db_in), own = outs if own_rows else (outs, [])
    moved = moved + own
    return (dz_in, dg_in, db_in, d_up_blocks,
            jnp.concatenate([dcwg, dcwv], axis=1), jnp.concatenate([dcbg, dcbv], axis=1), moved)


def kernel(x, ab_w_in, a_conv_w, a_conv_b, a_norm_g, a_norm_b, b_norm_g, b_norm_b, b_spatial_w, b_spatial_b, ab_w_out, c_w_qkv, c_b_qkv, c_sinks, c_w_o, ffn_w_up, ffn_conv_w, ffn_conv_b, ffn_w_down, ln_g, ln_b, loss_target, m_ab_w_in, m_a_conv_w, m_a_conv_b, m_a_norm_g, m_a_norm_b, m_b_norm_g, m_b_norm_b, m_b_spatial_w, m_b_spatial_b, m_ab_w_out, m_c_w_qkv, m_c_b_qkv, m_c_sinks, m_c_w_o, m_ffn_w_up, m_ffn_conv_w, m_ffn_conv_b, m_ffn_w_down, m_ln_g, m_ln_b, v_ab_w_in, v_a_conv_w, v_a_conv_b, v_a_norm_g, v_a_norm_b, v_b_norm_g, v_b_norm_b, v_b_spatial_w, v_b_spatial_b, v_ab_w_out, v_c_w_qkv, v_c_b_qkv, v_c_sinks, v_c_w_o, v_ffn_w_up, v_ffn_conv_w, v_ffn_conv_b, v_ffn_w_down, v_ln_g, v_ln_b):
    me = 4 * lax.axis_index("x") + 2 * lax.axis_index("y") + lax.axis_index("c")
    xt = x[0]
    t = xt.shape[0]

    small_shard_shapes = [a_conv_w.shape, c_b_qkv.shape, ffn_conv_w.shape, ln_g.shape, ln_b.shape]
    up_shard = [jnp.swapaxes(ffn_w_up[l], 0, 1).astype(BF16) for l in range(2)]
    qkv_shard = jnp.swapaxes(c_w_qkv[0], 0, 1).astype(BF16)
    down_shard = [ffn_w_down[l].astype(BF16) for l in range(2)]
    g_win, g_small = _comm_only(
        _Comm(gather=[jnp.swapaxes(ab_w_in[0], 0, 1).astype(BF16), _pack([a_conv_w, c_b_qkv, ffn_conv_w, ln_g, ln_b])]),
        "gather_first")
    w_in = g_win.reshape(-1, D_MODEL)
    g_acw, g_bqkv, g_fcw, g_lng, g_lnb = _unpack(g_small, small_shard_shapes)
    acw = _interleave(g_acw[:, 0])
    bqkv = g_bqkv[:, 0].reshape(1, -1)
    fcw = [_interleave(g_fcw[:, l]) for l in range(2)]
    lng = jnp.transpose(g_lng, (1, 2, 0, 3)).reshape(2, 2, 1, D_MODEL)
    lnb = jnp.transpose(g_lnb, (1, 2, 0, 3)).reshape(2, 2, 1, D_MODEL)
    fcb = [ffn_conv_b[l:l + 1] for l in range(2)]
    ms = b_spatial_w[0]
    mst = jnp.swapaxes(ms, 1, 2)
    sbt = b_spatial_b[0].T

    h0, (g_wout,) = _matmul(xt, w_in, "nt", BF16, "mix_in", 1024, 1024, 1024, comm=_Comm(gather=[ab_w_out[0].astype(BF16)]))
    w_out = g_wout.reshape(D_MODEL, D_MODEL)
    (cat, y0), (g_wup0,) = _mixer_mid_fwd(h0, acw, a_conv_b, a_norm_g, a_norm_b, b_norm_g, b_norm_b, ms, sbt, "mix_mid_fwd",
                                    comm=_Comm(gather=[up_shard[0]]))
    w_up0 = g_wup0.reshape(2 * D_FF, D_MODEL)
    z1, x1 = _matmul_res_ln(cat, w_out, xt, lng[0, 0], lnb[0, 0], "mix_out_ln", 512)
    hf0, (g_wdown0, g_wqkv) = _matmul(x1, w_up0, "nt", BF16, "ffn0_up", 1024, 1408, 1024,
                                      comm=_Comm(gather=[down_shard[0], qkv_shard]))
    w_down0 = g_wdown0.reshape(D_FF, D_MODEL)
    w_qkv = g_wqkv.reshape(Q_WIDTH + KV_WIDTH, D_MODEL)
    (u0, cg0, cv0), (g_wup1,) = _ffn_mid_fwd(hf0, fcw[0], fcb[0], "ffn0_mid_fwd", comm=_Comm(gather=[up_shard[1]]))
    w_up1 = g_wup1.reshape(2 * D_FF, D_MODEL)
    (z2, x2), (g_wo,) = _matmul_res_ln(u0, w_down0, z1, lng[0, 1], lnb[0, 1], "ffn0_down_ln", 512, prev=(lng[0, 0], lnb[0, 0]),
                                       comm=_Comm(gather=[c_w_o[0].astype(BF16)]))
    w_o = g_wo.reshape(D_MODEL, D_MODEL)
    qkv = _matmul(x2, w_qkv, "nt", BF16, "att_qkv", 1024, 1280, 1024, bias=bqkv)
    att, (g_wdown1,) = _attn_fwd(qkv, c_sinks, "att_fwd", comm=_Comm(gather=[down_shard[1]]))
    w_down1 = g_wdown1.reshape(D_FF, D_MODEL)
    z3, x3 = _matmul_res_ln(att, w_o, z2, lng[1, 0], lnb[1, 0], "att_out_ln", 512, prev=(lng[0, 1], lnb[0, 1]))
    hf1 = _matmul(x3, w_up1, "nt", BF16, "ffn1_up", 1024, 1408, 1024)
    u1, cg1, cv1 = _ffn_mid_fwd(hf1, fcw[1], fcb[1], "ffn1_mid_fwd")

    dz4, dg11, db11, loss_terms = _matmul_res_ln_loss(u1, w_down1, z3, lng[1, 1], lnb[1, 1], loss_target[0],
                                                      "ffn1_down_ln_loss", 512, prev=(lng[1, 0], lnb[1, 0]))
    dz3, dg10, db10, d_wup1, d_fcw1, d_fcb1, (p_wdown1,) = _ffn_backward(
        dz4, x3, z3, lng[1, 0], hf1, cg1, cv1, u1, w_up1, fcw[1], w_down1, 1)
    d_att = _matmul(dz3, w_o, "nt", BF16, "att_dout", 1024, 1024, 1024)
    d_wo = _matmul(att, dz3, "tn", BF16, "att_dwo", 1024, 1024, 512)
    rows_up = d_wup1.shape[1]
    first = 3 * rows_up // 4
    (dq, dkv, dbq, dbkv, dsinks), (p_wup1a,) = _attn_bwd(qkv, d_att, c_sinks, "att_bwd",
                                                        comm=_Comm(exchange=[(d_wup1, 0, first)]))
    d_wqkv = jnp.concatenate([_matmul(dq, x2, "tn", BF16, "att_dwq", 1024, 1024, 1024),
                              _matmul(dkv, x2, "tn", BF16, "att_dwkv", KV_WIDTH, 1024, 1024)], axis=0)
    dz2, dg01, db01 = _matmul_ln_bwd([(dq, 0), (dkv, Q_WIDTH)], w_qkv, z2, lng[0, 1], dz3, "att_dx_ln_bwd", 512)
    early = rows_up // 4
    dz1, dg00, db00, d_wup0, d_fcw0, d_fcb0, (p_wdown0, p_wup1b, p_wqkv, p_wo, p_wup0a) = _ffn_backward(
        dz2, x1, z1, lng[0, 0], hf0, cg0, cv0, u0, w_up0, fcw[0], w_down0, 0, exchange=[(d_wup1, first, rows_up - first)],
        exchange_late=[d_wqkv.reshape(N_DEV, -1, D_MODEL), d_wo.reshape(N_DEV, -1, D_MODEL)], own_rows=early)
    dcat = _matmul(dz1, w_out, "nt", BF16, "mix_dcat", 1024, 1024, 1024)
    d_wout = _matmul(cat, dz1, "tn", BF16, "mix_dwout", 1024, 1024, 512)
    (dh0, d_acw, d_acb, d_ang, d_anb, d_bng, d_bnb, d_ms, d_sb), (p_wup0b, p_wout) = _mixer_mid_bwd(
        h0, y0, dcat, acw, a_norm_g, a_norm_b, b_norm_g, b_norm_b, ms, mst, sbt, "mix_mid_bwd",
        comm=_Comm(exchange=[(d_wup0, early, rows_up - early), d_wout.reshape(N_DEV, -1, D_MODEL)]))
    d_bqkv = jnp.concatenate([dbq, dbkv], axis=1)
    d_lng = jnp.stack([jnp.stack([dg00, dg01]), jnp.stack([dg10, dg11])])
    d_lnb = jnp.stack([jnp.stack([db00, db01]), jnp.stack([db10, db11])])
    small_full = [d_acb, d_ang, d_anb, d_bng, d_bnb, d_ms, d_sb, dsinks[:, :N_Q_HEADS], jnp.concatenate([d_fcb0, d_fcb1], axis=0),
                  d_acw, d_bqkv, jnp.stack([d_fcw0, d_fcw1]), d_lng, d_lnb, loss_terms]
    d_win, (g_small_grads,) = _matmul(dh0, xt, "tn", BF16, "mix_dwin", 1024, 1024, 512, comm=_Comm(gather=[_pack(small_full)]))
    grad_x, (p_win,) = _matmul(dh0, w_in, "nn", F32, "mix_dx", 1024, 1024, 1024, res=dz1, res_scale=ALPHA,
                               comm=_Comm(exchange=[d_win.reshape(N_DEV, -1, D_MODEL)]))


    big = {}
    for nm, p, w, m, v, tr, transposed in [
            ("ab_w_in", [p_win], ab_w_in, m_ab_w_in, v_ab_w_in, 256, True),
            ("ab_w_out", [p_wout], ab_w_out, m_ab_w_out, v_ab_w_out, 128, False),
            ("c_w_qkv", [p_wqkv], c_w_qkv, m_c_w_qkv, v_c_w_qkv, 160, True), ("c_w_o", [p_wo], c_w_o, m_c_w_o, v_c_w_o, 128, False),
            ("ffn_w_up", [(p_wup0a, 0, early), (p_wup0b, early, rows_up - early), (p_wup1a, 0, first), (p_wup1b, first, rows_up - first)], ffn_w_up, m_ffn_w_up, v_ffn_w_up, 176, True),
            ("ffn_w_down", [p_wdown0, p_wdown1], ffn_w_down, m_ffn_w_down, v_ffn_w_down, 176, False)]:
        def two_d(a):
            a = jnp.swapaxes(a, 1, 2) if transposed else a
            return a.reshape(-1, a.shape[-1])

        def back(o):
            return jnp.swapaxes(o.reshape(w.shape[0], w.shape[2], w.shape[1]), 1, 2) if transposed else o.reshape(w.shape)

        outs = _adamw_big(p, two_d(w), two_d(m), two_d(v), "adamw_" + nm, tr)
        big[nm] = [back(o) for o in outs]

    *gs, loss_parts = _unpack(g_small_grads, [a.shape for a in small_full])
    loss = 0.5 / D_MODEL * jnp.sum(loss_parts)

    def my_shard(g, width):
        g = g.reshape(g.shape[:-1] + (N_DEV, width))
        return lax.dynamic_index_in_dim(g, me, axis=g.ndim - 2, keepdims=False)

    small_names = ["a_conv_b", "a_norm_g", "a_norm_b", "b_norm_g", "b_norm_b", "b_spatial_w", "b_spatial_b", "c_sinks", "ffn_conv_b",
                   "a_conv_w", "c_b_qkv", "ffn_conv_w", "ln_g", "ln_b"]
    small_w = [a_conv_b, a_norm_g, a_norm_b, b_norm_g, b_norm_b, b_spatial_w, b_spatial_b, c_sinks, ffn_conv_b,
               a_conv_w, c_b_qkv, ffn_conv_w, ln_g, ln_b]
    small_m = [m_a_conv_b, m_a_norm_g, m_a_norm_b, m_b_norm_g, m_b_norm_b, m_b_spatial_w, m_b_spatial_b, m_c_sinks, m_ffn_conv_b,
               m_a_conv_w, m_c_b_qkv, m_ffn_conv_w, m_ln_g, m_ln_b]
    small_v = [v_a_conv_b, v_a_norm_g, v_a_norm_b, v_b_norm_g, v_b_norm_b, v_b_spatial_w, v_b_spatial_b, v_c_sinks, v_ffn_conv_b,
               v_a_conv_w, v_c_b_qkv, v_ffn_conv_w, v_ln_g, v_ln_b]
    gs[9:] = [my_shard(g, w.shape[-1]) for g, w in zip(gs[9:], small_w[9:])]
    two_d = [(-1, w.shape[-1]) for w in small_w]
    outs = _adamw_small([g.reshape((N_DEV,) + w.reshape(s).shape) for g, w, s in zip(gs, small_w, two_d)],
                        [w.reshape(s) for w, s in zip(small_w, two_d)], [m.reshape(s) for m, s in zip(small_m, two_d)],
                        [v.reshape(s) for v, s in zip(small_v, two_d)], "adamw_small")
    small = {nm: [o.reshape(w.shape) for o in outs[4 * a:4 * a + 4]] for a, (nm, w) in enumerate(zip(small_names, small_w))}

    res = {**big, **small}
    order = ["ab_w_in", "a_conv_w", "a_conv_b", "a_norm_g", "a_norm_b", "b_norm_g", "b_norm_b", "b_spatial_w", "b_spatial_b", "ab_w_out",
             "c_w_qkv", "c_b_qkv", "c_sinks", "c_w_o", "ffn_w_up", "ffn_conv_w", "ffn_conv_b", "ffn_w_down", "ln_g", "ln_b"]
    return (loss, grad_x[None], *[res[nm][0] for nm in order], *[res[nm][1] for nm in order],
            *[res[nm][2] for nm in order], *[res[nm][3] for nm in order])
```

```python
import functools
import math

import jax
import jax.numpy as jnp
from jax import lax
from jax.experimental import pallas as pl
from jax.experimental.pallas import tpu as pltpu

F32 = jnp.float32
BF16 = jnp.bfloat16

N_DEV = 8
D_MODEL = 1024
A_WIDTH = 512
A_KERNEL = 31
B_GROUPS = 4
B_CHUNK = 128
HEAD_DIM = 64
N_Q_HEADS = 16
N_KV_HEADS = 2
ATT_BLOCK = 128
D_FF = 2816
FFN_KERNEL = 3
ALPHA = (2.0 * 2) ** 0.25
LN_EPS = 1e-5
GELU_K = math.sqrt(2.0 / math.pi)
GELU_C = 0.044715
ADAM_LR = 0.001
ADAM_B1 = 0.9
ADAM_B2 = 0.999
ADAM_EPS = 1e-08
ADAM_WD = 0.01
ADAM_STEP = 10
VMEM_LIMIT = 56 * 1024 * 1024
MESH_ID = pl.DeviceIdType.MESH


def _params(*sem):
    return pltpu.CompilerParams(dimension_semantics=sem, vmem_limit_bytes=VMEM_LIMIT)


def _gelu(x):
    t = jnp.tanh(GELU_K * x * (1.0 + GELU_C * x * x))
    return 0.5 * x * (1.0 + t)


def _gelu_and_grad(x):
    x2 = x * x
    t = jnp.tanh(GELU_K * x * (1.0 + GELU_C * x2))
    g = 0.5 * x * (1.0 + t)
    dg = 0.5 * (1.0 + t) + 0.5 * x * (1.0 - t * t) * (GELU_K * (1.0 + 3.0 * GELU_C * x2))
    return g, dg


def _sigmoid(x):
    return 1.0 / (1.0 + jnp.exp(-x))


def _ln_stats(z):
    mu = jnp.mean(z, axis=-1, keepdims=True)
    zc = z - mu
    var = jnp.mean(zc * zc, axis=-1, keepdims=True)
    r = lax.rsqrt(var + LN_EPS)
    return zc * r, r


def _ln_bwd_rows(dn, nh, r):
    return r * (dn - jnp.mean(dn, axis=-1, keepdims=True) - nh * jnp.mean(dn * nh, axis=-1, keepdims=True))


def _colsum(x):
    return jnp.sum(x, axis=0, keepdims=True)


def _dot(a, b, dims):
    return lax.dot_general(a.astype(BF16), b.astype(BF16), (dims, ((), ())), preferred_element_type=F32)


NN = ((1,), (0,))
NT = ((1,), (1,))
TN = ((0,), (0,))


ANY = pl.BlockSpec(memory_space=pl.ANY)
N_RELATIONS = N_DEV - 1


def _my_place():
    return lax.axis_index("x"), lax.axis_index("y"), lax.axis_index("c")


class _Comm:
    def __init__(self, gather=(), exchange=()):
        exchange = [e if isinstance(e, tuple) else (e, 0, e.shape[1]) for e in exchange]
        self.arrs = list(gather) + [e[0] for e in exchange]
        self.n_gather = len(gather)
        self.n = len(self.arrs)
        self.rows = [None] * self.n_gather + [pl.ds(lo, n) for _, lo, n in exchange]

    def out_shape(self):
        return [jax.ShapeDtypeStruct(((N_DEV,) + a.shape) if i < self.n_gather else a.shape, a.dtype)
                for i, a in enumerate(self.arrs)]

    def sems(self):
        return [pltpu.SemaphoreType.DMA((self.n, N_RELATIONS)), pltpu.SemaphoreType.DMA((self.n, N_RELATIONS)),
                pltpu.SemaphoreType.DMA((self.n,))]

    def _gather_copy(self, ins, outs, sems, a, k, place, to, from_input=False):
        px, py, pc = place
        block = outs[a].at[4 * px + 2 * py + pc]
        return pltpu.make_async_remote_copy(
            src_ref=ins[a] if from_input else block, dst_ref=block,
            send_sem=sems[0].at[a, k], recv_sem=sems[1].at[a, k], device_id=to, device_id_type=MESH_ID)

    def _exchange_copy(self, ins, outs, sems, a, k, landing=False):
        x, y, c = _my_place()
        me = 4 * x + 2 * y + c
        peer = (x ^ (k >> 2), y ^ ((k >> 1) & 1), c ^ (k & 1))
        return pltpu.make_async_remote_copy(
            src_ref=ins[a].at[me ^ k, self.rows[a]], dst_ref=outs[a].at[(me ^ k) if landing else me, self.rows[a]],
            send_sem=sems[0].at[a, k - 1], recv_sem=sems[1].at[a, k - 1], device_id=peer, device_id_type=MESH_ID)

    def _local_copy(self, ins, outs, sems, a):
        x, y, c = _my_place()
        me = 4 * x + 2 * y + c
        if a < self.n_gather:
            return pltpu.make_async_copy(ins[a], outs[a].at[me], sems[2].at[a])
        return pltpu.make_async_copy(ins[a].at[me, self.rows[a]], outs[a].at[me, self.rows[a]], sems[2].at[a])

    def _first_stage(self, ins, outs, sems, a):
        x, y, c = _my_place()
        me = (x, y, c)
        chips = [(1 - x, y), (x, 1 - y), (1 - x, 1 - y)]
        return ([self._gather_copy(ins, outs, sems, a, 0, me, (x, y, 1 - c), from_input=True)]
                + [self._gather_copy(ins, outs, sems, a, 1 + j, me, (*chip, c), from_input=True) for j, chip in enumerate(chips)])

    def start(self, ins, outs, sems):
        for a in range(self.n):
            self._local_copy(ins, outs, sems, a).start()
        for a in range(self.n_gather):
            for cp in self._first_stage(ins, outs, sems, a):
                cp.start()
        for k in range(1, N_DEV):
            for a in range(self.n_gather, self.n):
                self._exchange_copy(ins, outs, sems, a, k).start()

    def forward(self, ins, outs, sems):
        x, y, c = _my_place()
        me, sibling = (x, y, c), (x, y, 1 - c)
        for j, chip in enumerate([(1 - x, y), (x, 1 - y), (1 - x, 1 - y)]):
            for a in range(self.n_gather):
                self._gather_copy(ins, outs, sems, a, 1 + j, (*chip, c), me).wait_recv()
                self._gather_copy(ins, outs, sems, a, 4 + j, (*chip, c), sibling).start()

    def finish(self, ins, outs, sems):
        x, y, c = _my_place()
        me, sibling = (x, y, c), (x, y, 1 - c)
        chips = [(1 - x, y), (x, 1 - y), (1 - x, 1 - y)]
        passed = [self._gather_copy(ins, outs, sems, a, 4 + j, (*chip, c), sibling)
                  for j, chip in enumerate(chips) for a in range(self.n_gather)]
        for a in range(self.n_gather):
            self._gather_copy(ins, outs, sems, a, 0, sibling, me).wait_recv()
            for j, chip in enumerate(chips):
                self._gather_copy(ins, outs, sems, a, 4 + j, (*chip, 1 - c), me).wait_recv()
        for k in range(1, N_DEV):
            for a in range(self.n_gather, self.n):
                self._exchange_copy(ins, outs, sems, a, k, landing=True).wait_recv()
        for a in range(self.n_gather):
            for cp in self._first_stage(ins, outs, sems, a):
                cp.wait_send()
        for cp in passed:
            cp.wait_send()
        for k in range(1, N_DEV):
            for a in range(self.n_gather, self.n):
                self._exchange_copy(ins, outs, sems, a, k).wait_send()
        for a in range(self.n):
            self._local_copy(ins, outs, sems, a).wait()


def _comm_only(comm, name):
    def body(*refs):
        ins, outs, sems = refs[:comm.n], refs[comm.n:2 * comm.n], refs[2 * comm.n:]
        comm.start(ins, outs, sems)
        comm.forward(ins, outs, sems)
        comm.finish(ins, outs, sems)

    return pl.pallas_call(body, name=name, in_specs=[ANY] * comm.n, out_specs=[ANY] * comm.n,
                          out_shape=comm.out_shape(), scratch_shapes=comm.sems())(*comm.arrs)


def _call(body, *, name, grid, in_specs, out_specs, out_shape, args, sem, scratch_shapes=(), comm=None):
    in_specs, out_specs, out_shape, scratch_shapes = list(in_specs), list(out_specs), list(out_shape), list(scratch_shapes)
    if comm is None:
        outs = pl.pallas_call(body, name=name, grid=grid, in_specs=in_specs, out_specs=out_specs, out_shape=out_shape,
                              scratch_shapes=scratch_shapes, compiler_params=_params(*sem))(*args)
        return list(outs), []
    n_in, n_out, n_scr, nc = len(in_specs), len(out_specs), len(scratch_shapes), comm.n

    def wrapped(*refs):
        ins, refs = refs[:n_in], refs[n_in:]
        c_in, refs = refs[:nc], refs[nc:]
        outs, refs = refs[:n_out], refs[n_out:]
        c_out, refs = refs[:nc], refs[nc:]
        scr, sems = refs[:n_scr], refs[n_scr:]
        step = functools.reduce(lambda acc, ax: acc * grid[ax] + pl.program_id(ax), range(len(grid)), 0)
        steps = math.prod(grid)

        @pl.when(step == 0)
        def _():
            comm.start(c_in, c_out, sems)

        @pl.when(step == steps - 1)
        def _():
            comm.forward(c_in, c_out, sems)

        body(*ins, *outs, *scr)

        @pl.when(step == steps - 1)
        def _():
            comm.finish(c_in, c_out, sems)

    outs = pl.pallas_call(
        wrapped, name=name, grid=grid, in_specs=in_specs + [ANY] * nc, out_specs=out_specs + [ANY] * nc,
        out_shape=out_shape + comm.out_shape(), scratch_shapes=scratch_shapes + comm.sems(),
        compiler_params=_params(*(["arbitrary"] * len(grid))))(*args, *comm.arrs)
    return list(outs[:n_out]), list(outs[n_out:])


def _matmul(a, b, mode, out_dtype, name, tm, tn, tk, *, bias=None, res=None, res_scale=1.0, b_off=0, comm=None):
    tm = min(tm, a.shape[1] if mode == "tn" else a.shape[0])
    tk = min(tk, a.shape[0] if mode == "tn" else a.shape[1])
    if mode == "nn":
        (m, k), n = a.shape, b.shape[1]
        a_spec = pl.BlockSpec((tm, tk), lambda i, j, kk: (i, kk))
        b_spec = pl.BlockSpec((tk, tn), lambda i, j, kk: (kk + b_off, j))
        dims = NN
    elif mode == "nt":
        (m, k), n = a.shape, b.shape[0]
        a_spec = pl.BlockSpec((tm, tk), lambda i, j, kk: (i, kk))
        b_spec = pl.BlockSpec((tn, tk), lambda i, j, kk: (j, kk + b_off))
        dims = NT
    else:
        (k, m), n = a.shape, b.shape[1]
        a_spec = pl.BlockSpec((tk, tm), lambda i, j, kk: (kk, i))
        b_spec = pl.BlockSpec((tk, tn), lambda i, j, kk: (kk, j))
        dims = TN
    assert m % tm == 0 and n % tn == 0 and k % tk == 0, (name, m, n, k)
    nk = k // tk
    in_specs = [a_spec, b_spec]
    args = [a, b]
    if bias is not None:
        in_specs.append(pl.BlockSpec((1, tn), lambda i, j, kk: (0, j)))
        args.append(bias)
    if res is not None:
        in_specs.append(pl.BlockSpec((tm, tn), lambda i, j, kk: (i, j)))
        args.append(res)

    def finish(out, refs, o_ref):
        pos = 2
        if bias is not None:
            out = out + refs[pos][...]
            pos += 1
        if res is not None:
            out = out + res_scale * refs[pos][...].astype(F32)
        o_ref[...] = out.astype(out_dtype)

    def body_one_step(*refs):
        finish(_dot(refs[0][...], refs[1][...], dims), refs, refs[-1])

    def body(*refs):
        a_ref, b_ref = refs[0], refs[1]
        o_ref, acc = refs[-2], refs[-1]
        kk = pl.program_id(2)

        @pl.when(kk == 0)
        def _():
            acc[...] = jnp.zeros_like(acc)

        acc[...] += _dot(a_ref[...], b_ref[...], dims)

        @pl.when(kk == nk - 1)
        def _():
            finish(acc[...], refs, o_ref)

    (out,), moved = _call(
        body_one_step if nk == 1 else body, name=name, grid=(m // tm, n // tn, nk),
        in_specs=in_specs, out_specs=[pl.BlockSpec((tm, tn), lambda i, j, kk: (i, j))],
        out_shape=[jax.ShapeDtypeStruct((m, n), out_dtype)],
        scratch_shapes=[] if nk == 1 else [pltpu.VMEM((tm, tn), F32)],
        sem=("parallel", "parallel", "arbitrary"), args=args, comm=comm)
    return out if comm is None else (out, moved)


def _matmul_tn_pair(a0, a1, b, out_dtype, name, tm, tn, tk, comm=None):
    (k, m), n = a0.shape, b.shape[1]
    tk = min(tk, k)
    assert a1.shape == a0.shape and m % tm == 0 and n % tn == 0 and k % tk == 0, (name, m, n, k)
    mi, nk = m // tm, k // tk

    def body(a0_ref, a1_ref, b_ref, o_ref, acc):
        i, kk = pl.program_id(0), pl.program_id(2)

        @pl.when(kk == 0)
        def _():
            acc[...] = jnp.zeros_like(acc)

        @pl.when(i < mi)
        def _():
            acc[...] += _dot(a0_ref[...], b_ref[...], TN)

        @pl.when(i >= mi)
        def _():
            acc[...] += _dot(a1_ref[...], b_ref[...], TN)

        @pl.when(kk == nk - 1)
        def _():
            o_ref[...] = acc[...].astype(out_dtype)

    (out,), moved = _call(
        body, name=name, grid=(2 * mi, n // tn, nk),
        in_specs=[pl.BlockSpec((tk, tm), lambda i, j, kk: (jnp.where(i < mi, kk, nk - 1), jnp.minimum(i, mi - 1))),
                  pl.BlockSpec((tk, tm), lambda i, j, kk: (jnp.where(i >= mi, kk, 0), jnp.maximum(i - mi, 0))),
                  pl.BlockSpec((tk, tn), lambda i, j, kk: (kk, j))],
        out_specs=[pl.BlockSpec((tm, tn), lambda i, j, kk: (i, j))],
        out_shape=[jax.ShapeDtypeStruct((2 * m, n), out_dtype)],
        scratch_shapes=[pltpu.VMEM((tm, tn), F32)],
        sem=("parallel", "parallel", "arbitrary"), args=(a0, a1, b), comm=comm)
    return out if comm is None else (out, moved)


def _residual_input(x_ref, prev_refs):
    if not prev_refs:
        return x_ref[...]
    nh, _ = _ln_stats(x_ref[...])
    return nh * prev_refs[0][...] + prev_refs[1][...]


def _matmul_res_ln(a, b, x, g, beta, name, tm, prev=None, comm=None):
    t, k = a.shape
    d = b.shape[1]
    tm = min(tm, t)
    assert t % tm == 0
    n_prev = 0 if prev is None else 2

    def body(a_ref, b_ref, x_ref, g_ref, beta_ref, *rest):
        z_ref, xo_ref = rest[n_prev:]
        z = ALPHA * _residual_input(x_ref, rest[:n_prev]) + _dot(a_ref[...], b_ref[...], NN)
        nh, _ = _ln_stats(z)
        z_ref[...] = z
        xo_ref[...] = (nh * g_ref[...] + beta_ref[...]).astype(BF16)

    row = pl.BlockSpec((tm, d), lambda i: (i, 0))
    vec = pl.BlockSpec((1, d), lambda i: (0, 0))
    outs, moved = _call(
        body, name=name, grid=(t // tm,),
        in_specs=[pl.BlockSpec((tm, k), lambda i: (i, 0)), pl.BlockSpec((k, d), lambda i: (0, 0)), row, vec, vec] + [vec] * n_prev,
        out_specs=[row, row],
        out_shape=[jax.ShapeDtypeStruct((t, d), F32), jax.ShapeDtypeStruct((t, d), BF16)],
        sem=("parallel",), args=(a, b, x, g, beta, *(prev or ())), comm=comm)
    return outs if comm is None else (outs, moved)


def _matmul_ln_bwd(parts, b, z, g, dres, name, tm, comm=None):
    m = parts[0][0].shape[0]
    d = b.shape[1]
    tm = min(tm, m)
    n = len(parts)
    assert m % tm == 0 and all(row % a.shape[1] == 0 for a, row in parts)

    def body(*refs):
        z_ref, g_ref, dres_ref = refs[2 * n:2 * n + 3]
        dz_ref, dzb_ref, dg_ref, db_ref = refs[-4:]

        @pl.when(pl.program_id(0) == 0)
        def _():
            dg_ref[...] = jnp.zeros_like(dg_ref)
            db_ref[...] = jnp.zeros_like(db_ref)

        dy = ALPHA * dres_ref[...]
        for p in range(n):
            dy = dy + _dot(refs[p][...], refs[n + p][...], NN)
        nh, r = _ln_stats(z_ref[...])
        dg_ref[...] += _colsum(dy * nh)
        db_ref[...] += _colsum(dy)
        dz = _ln_bwd_rows(dy * g_ref[...], nh, r)
        dz_ref[...] = dz
        dzb_ref[...] = dz.astype(BF16)

    def b_spec(a, row):
        blk = row // a.shape[1]
        return pl.BlockSpec((a.shape[1], d), lambda i: (blk, 0))

    row = pl.BlockSpec((tm, d), lambda i: (i, 0))
    vec = pl.BlockSpec((1, d), lambda i: (0, 0))
    vshape = jax.ShapeDtypeStruct((1, d), F32)
    outs, moved = _call(
        body, name=name, grid=(m // tm,),
        in_specs=[pl.BlockSpec((tm, a.shape[1]), lambda i: (i, 0)) for a, _ in parts] + [b_spec(a, r_) for a, r_ in parts]
        + [row, vec, row],
        out_specs=[row, row, vec, vec],
        out_shape=[jax.ShapeDtypeStruct((m, d), F32), jax.ShapeDtypeStruct((m, d), BF16), vshape, vshape],
        sem=("arbitrary",), args=(*[a for a, _ in parts], *([b] * n), z, g, dres), comm=comm)
    return outs if comm is None else (outs, moved)


def _matmul_res_ln_loss(a, b, x, g, beta, target, name, tm, prev):
    t, k = a.shape
    d = b.shape[1]
    tm = min(tm, t)

    def body(a_ref, b_ref, x_ref, g_ref, beta_ref, t_ref, gp_ref, bp_ref, dz_ref, dzb_ref, dg_ref, db_ref, loss_ref):
        @pl.when(pl.program_id(0) == 0)
        def _():
            dg_ref[...] = jnp.zeros_like(dg_ref)
            db_ref[...] = jnp.zeros_like(db_ref)
            loss_ref[...] = jnp.zeros_like(loss_ref)

        nh, r = _ln_stats(ALPHA * _residual_input(x_ref, (gp_ref, bp_ref)) + _dot(a_ref[...], b_ref[...], NN))
        err = nh * g_ref[...] + beta_ref[...] - t_ref[...]
        loss_ref[...] += _colsum(err * err)
        dy = err * (1.0 / d)
        dg_ref[...] += _colsum(dy * nh)
        db_ref[...] += _colsum(dy)
        dz = _ln_bwd_rows(dy * g_ref[...], nh, r)
        dz_ref[...] = dz
        dzb_ref[...] = dz.astype(BF16)

    row = pl.BlockSpec((tm, d), lambda i: (i, 0))
    vec = pl.BlockSpec((1, d), lambda i: (0, 0))
    vshape = jax.ShapeDtypeStruct((1, d), F32)
    return pl.pallas_call(
        body, name=name, grid=(t // tm,),
        in_specs=[pl.BlockSpec((tm, k), lambda i: (i, 0)), pl.BlockSpec((k, d), lambda i: (0, 0)), row, vec, vec, row, vec, vec],
        out_specs=[row, row, vec, vec, vec],
        out_shape=[jax.ShapeDtypeStruct((t, d), F32), jax.ShapeDtypeStruct((t, d), BF16), vshape, vshape, vshape],
        compiler_params=_params("arbitrary"),
    )(a, b, x, g, beta, target, *prev)


FFN_HALO = 16
FFN_CHUNK = 256
LANES = 128
SUBLANES = 8


def _rows_up(e, start, rows):
    if start % SUBLANES == 0:
        return e[start:start + rows]
    return pltpu.roll(e, e.shape[0] - start, 0)[0:rows]


def _fold(x):
    return jnp.sum(x.reshape(x.shape[0] // SUBLANES, SUBLANES, x.shape[1]), axis=0)


def _ffn_mid_fwd(h, cw, cb, name, tm=1024, tc=1408, comm=None):
    t, f2 = h.shape
    tm = min(tm, t)
    f = f2 // 2
    nj, nt, hb = f // tc, t // tm, tm // FFN_HALO

    ch = min(FFN_CHUNK, tm)

    def body(hg, hgp, hv, hvp, cwg, cwv, cbg, cbv, u_ref, cg_ref, cv_ref):
        i = pl.program_id(1)
        o = FFN_HALO - FFN_KERNEL + 1
        for lg in range(tc // LANES):
            cols = slice(lg * LANES, (lg + 1) * LANES)
            wg, wv = [cwg[k:k + 1, cols] for k in range(FFN_KERNEL)], [cwv[k:k + 1, cols] for k in range(FFN_KERNEL)]
            bg, bv = cbg[:, cols], cbv[:, cols]

            def emit(base, eg, ev):
                cg = wg[0] * _rows_up(eg, o, ch) + wg[1] * _rows_up(eg, o + 1, ch) + wg[2] * _rows_up(eg, o + 2, ch) + bg
                cv = wv[0] * _rows_up(ev, o, ch) + wv[1] * _rows_up(ev, o + 1, ch) + wv[2] * _rows_up(ev, o + 2, ch) + bv
                u_ref[pl.ds(base, ch), cols] = (_gelu(cg) * cv).astype(BF16)
                cg_ref[pl.ds(base, ch), cols] = cg.astype(BF16)
                cv_ref[pl.ds(base, ch), cols] = cv.astype(BF16)

            def first(main, prev):
                return jnp.concatenate([jnp.where(i > 0, prev[:, cols].astype(F32), 0.0), main[0:ch, cols].astype(F32)], axis=0)

            def inner(c, carry):
                base = pl.multiple_of(c * ch, ch)
                emit(base, hg[pl.ds(base - FFN_HALO, ch + FFN_HALO), cols].astype(F32),
                     hv[pl.ds(base - FFN_HALO, ch + FFN_HALO), cols].astype(F32))
                return carry

            emit(0, first(hg, hgp), first(hv, hvp))
            if tm > ch:
                lax.fori_loop(1, tm // ch, inner, 0)

    def main_spec(off):
        return pl.BlockSpec((tm, tc), lambda j, i: (i, j + off))

    def prev_spec(off):
        return pl.BlockSpec((FFN_HALO, tc), lambda j, i: (jnp.maximum(i * hb - 1, 0), j + off))

    def par_spec(rows, off):
        return pl.BlockSpec((rows, tc), lambda j, i: (0, j + off))

    outs, moved = _call(
        body, name=name, grid=(nj, nt),
        in_specs=[main_spec(0), prev_spec(0), main_spec(nj), prev_spec(nj),
                  par_spec(FFN_KERNEL, 0), par_spec(FFN_KERNEL, nj), par_spec(1, 0), par_spec(1, nj)],
        out_specs=[pl.BlockSpec((tm, tc), lambda j, i: (i, j))] * 3,
        out_shape=[jax.ShapeDtypeStruct((t, f), BF16)] * 3,
        sem=("parallel", "arbitrary"), args=(h, h, h, h, cw, cw, cb, cb), comm=comm)
    return outs if comm is None else (outs, moved)


def _ffn_mid_bwd(h, cg, cv, du, cw, name, tm=1024, tc=1408, comm=None):
    t, f2 = h.shape
    tm = min(tm, t)
    f = f2 // 2
    nj, nt, hb = f // tc, t // tm, tm // FFN_HALO

    ch = min(FFN_CHUNK, tm)
    ahead = ch + SUBLANES
    n_ch = tm // ch

    def body(hg, hv, cg_ref, cgn_ref, cv_ref, cvn_ref, du_ref, dun_ref, cwg, cwv,
             dhg_ref, dhv_ref, dcwg_ref, dcwv_ref, dcbg_ref, dcbv_ref):
        i = pl.program_id(1)

        @pl.when(i == 0)
        def _():
            for ref in (dcwg_ref, dcwv_ref, dcbg_ref, dcbv_ref):
                ref[...] = jnp.zeros_like(ref)

        for lg in range(tc // LANES):
            cols = slice(lg * LANES, (lg + 1) * LANES)
            wg, wv = [cwg[k:k + 1, cols] for k in range(FFN_KERNEL)], [cwv[k:k + 1, cols] for k in range(FFN_KERNEL)]

            def emit(base, cg_e, cv_e, du_e, acc):
                cg_a, cv_a, du_a = cg_e[0:ahead], cv_e[0:ahead], du_e[0:ahead]
                gl, dgl = _gelu_and_grad(cg_a)

                def back(d, h_ref, w, dh_ref):
                    later = [d[0:ch], _rows_up(d, 1, ch), _rows_up(d, 2, ch)]
                    dh_ref[pl.ds(base, ch), cols] = (w[2] * later[0] + w[1] * later[1] + w[0] * later[2]).astype(BF16)
                    h_own = h_ref[pl.ds(base, ch), cols].astype(F32)
                    return [_fold(later[0])] + [_fold(later[FFN_KERNEL - 1 - k] * h_own) for k in range(FFN_KERNEL)]

                sums = back(du_a * cv_a * dgl, hg, wg, dhg_ref) + back(du_a * gl, hv, wv, dhv_ref)
                return tuple(a + s_ for a, s_ in zip(acc, sums))

            def inner(c, acc):
                base = pl.multiple_of(c * ch, ch)
                rows = pl.ds(base, ch + FFN_HALO)
                return emit(base, cg_ref[rows, cols].astype(F32), cv_ref[rows, cols].astype(F32), du_ref[rows, cols].astype(F32), acc)

            def last(acc):
                def rows(main, after):
                    return jnp.concatenate([main[tm - ch:tm, cols].astype(F32), after], axis=0)

                du_next = jnp.where(i < nt - 1, dun_ref[:, cols].astype(F32), 0.0)
                return emit(tm - ch, rows(cg_ref, cgn_ref[:, cols].astype(F32)), rows(cv_ref, cvn_ref[:, cols].astype(F32)),
                            rows(du_ref, du_next), acc)

            acc = (jnp.zeros((SUBLANES, LANES), F32),) * (2 * (1 + FFN_KERNEL))
            if n_ch > 1:
                acc = lax.fori_loop(0, n_ch - 1, inner, acc)
            acc = last(acc)
            dcbg_ref[:, cols] += _colsum(acc[0])
            dcbv_ref[:, cols] += _colsum(acc[1 + FFN_KERNEL])
            for k in range(FFN_KERNEL):
                dcwg_ref[k:k + 1, cols] += _colsum(acc[1 + k])
                dcwv_ref[k:k + 1, cols] += _colsum(acc[2 + FFN_KERNEL + k])

    last_blk = t // FFN_HALO - 1

    def main_spec(off):
        return pl.BlockSpec((tm, tc), lambda j, i: (i, j + off))

    def next_spec(off):
        return pl.BlockSpec((FFN_HALO, tc), lambda j, i: (jnp.minimum((i + 1) * hb, last_blk), j + off))

    def par_spec(rows, off):
        return pl.BlockSpec((rows, tc), lambda j, i: (0, j + off))

    out_tile = pl.BlockSpec((tm, tc), lambda j, i: (i, j))
    outs, moved = _call(
        body, name=name, grid=(nj, nt),
        in_specs=[main_spec(0), main_spec(nj), main_spec(0), next_spec(0), main_spec(0), next_spec(0), main_spec(0), next_spec(0),
                  par_spec(FFN_KERNEL, 0), par_spec(FFN_KERNEL, nj)],
        out_specs=[out_tile, out_tile, par_spec(FFN_KERNEL, 0), par_spec(FFN_KERNEL, 0), par_spec(1, 0), par_spec(1, 0)],
        out_shape=[jax.ShapeDtypeStruct((t, f), BF16), jax.ShapeDtypeStruct((t, f), BF16),
                   jax.ShapeDtypeStruct((FFN_KERNEL, f), F32), jax.ShapeDtypeStruct((FFN_KERNEL, f), F32),
                   jax.ShapeDtypeStruct((1, f), F32), jax.ShapeDtypeStruct((1, f), F32)],
        sem=("parallel", "arbitrary"), args=(h, h, cg, cg, cv, cv, du, du, cw, cw), comm=comm)
    return outs if comm is None else (outs, moved)


MIX_HALO = 32


def _glu(hh):
    return hh[:, 0:A_WIDTH] * _sigmoid(hh[:, A_WIDTH:2 * A_WIDTH])


def _fill_row_shifts(s):
    rows = s.shape[1] - SUBLANES
    for j in range(1, SUBLANES):
        s[j, 0:rows, :] = s[0, pl.ds(j, rows), :]


def _rows_from(s, start, rows):
    j = start % SUBLANES
    return s[j, start - j:start - j + rows, :]


def _tril_mask():
    return lax.broadcasted_iota(jnp.int32, (B_CHUNK, B_CHUNK), 0) >= lax.broadcasted_iota(jnp.int32, (B_CHUNK, B_CHUNK), 1)


def _spatial_mix(q, ms_ref, sbt_ref, tm):
    mask = _tril_mask()
    ws = [jnp.where(mask, ms_ref[g], 0.0).astype(BF16) for g in range(B_GROUPS)]
    qb = q.astype(BF16)
    rows = []
    for c in range(tm // B_CHUNK):
        cols = [_dot(ws[g], qb[c * B_CHUNK:(c + 1) * B_CHUNK, g * 128:(g + 1) * 128], NN) + sbt_ref[:, g:g + 1]
                for g in range(B_GROUPS)]
        rows.append(jnp.concatenate(cols, axis=1))
    return jnp.concatenate(rows, axis=0)


def _mixer_mid_fwd(h, cw, cb, ag, ab, bg, bb, ms, sbt, name, tm=256, comm=None):
    t = h.shape[0]
    nt, hb = t // tm, tm // MIX_HALO
    o = MIX_HALO - A_KERNEL + 1

    def body(h_ref, hp_ref, cw_ref, cb_ref, ag_ref, ab_ref, bg_ref, bb_ref, ms_ref, sbt_ref, cat_ref, y_ref, sp):
        i = pl.program_id(0)
        sp[0, 0:MIX_HALO, :] = jnp.where(i > 0, _glu(hp_ref[:, 0:2 * A_WIDTH].astype(F32)), 0.0)
        sp[0, MIX_HALO:, :] = _glu(h_ref[:, 0:2 * A_WIDTH].astype(F32))
        _fill_row_shifts(sp)
        y = jnp.zeros((tm, A_WIDTH), F32) + cb_ref[...]
        for k in range(A_KERNEL):
            y = y + cw_ref[k:k + 1, :] * _rows_from(sp, o + k, tm)
        y_ref[...] = y.astype(BF16)
        nh, _ = _ln_stats(y)
        ln = nh * ag_ref[...] + ab_ref[...]
        cat_ref[:, 0:A_WIDTH] = (ln * _sigmoid(ln)).astype(BF16)
        u = _gelu(h_ref[:, 1024:1536].astype(F32))
        nb, _ = _ln_stats(_gelu(h_ref[:, 1536:2048].astype(F32)))
        mixed = _spatial_mix(nb * bg_ref[...] + bb_ref[...], ms_ref, sbt_ref, tm)
        cat_ref[:, A_WIDTH:] = (u * mixed).astype(BF16)

    vec = pl.BlockSpec((1, A_WIDTH), lambda i: (0, 0))
    outs, moved = _call(
        body, name=name, grid=(nt,),
        in_specs=[pl.BlockSpec((tm, 2048), lambda i: (i, 0)),
                  pl.BlockSpec((MIX_HALO, 2048), lambda i: (jnp.maximum(i * hb - 1, 0), 0)),
                  pl.BlockSpec((A_KERNEL, A_WIDTH), lambda i: (0, 0)), vec, vec, vec, vec, vec,
                  pl.BlockSpec((B_GROUPS, B_CHUNK, B_CHUNK), lambda i: (0, 0, 0)),
                  pl.BlockSpec((B_CHUNK, B_GROUPS), lambda i: (0, 0))],
        out_specs=[pl.BlockSpec((tm, D_MODEL), lambda i: (i, 0)), pl.BlockSpec((tm, A_WIDTH), lambda i: (i, 0))],
        out_shape=[jax.ShapeDtypeStruct((t, D_MODEL), BF16), jax.ShapeDtypeStruct((t, A_WIDTH), BF16)],
        scratch_shapes=[pltpu.VMEM((SUBLANES, tm + MIX_HALO, A_WIDTH), F32)],
        sem=("parallel",), args=(h, h, cw, cb, ag, ab, bg, bb, ms, sbt), comm=comm)
    return outs if comm is None else (outs, moved)


def _mixer_mid_bwd(h, y, dcat, cw, ag, ab, bg, bb, ms, mst, sbt, name, tm=256, comm=None):
    t = h.shape[0]
    nt, hb = t // tm, tm // MIX_HALO
    r = tm + MIX_HALO
    nchunk = tm // B_CHUNK

    def body(h_ref, y_ref, yn_ref, dc_ref, dcn_ref, cw_ref, ag_ref, ab_ref, bg_ref, bb_ref, ms_ref, mst_ref, sbt_ref,
             dh_ref, dcw_ref, dcb_ref, dag_ref, dab_ref, dbg_ref, dbb_ref, dms_ref, dsb_ref, sdy, sbacc):
        i = pl.program_id(0)

        @pl.when(i == 0)
        def _():
            for ref in (dcw_ref, dcb_ref, dag_ref, dab_ref, dbg_ref, dbb_ref, dms_ref, dsb_ref, sbacc):
                ref[...] = jnp.zeros_like(ref)

        nh, rs = _ln_stats(jnp.concatenate([y_ref[...].astype(F32), yn_ref[...].astype(F32)], axis=0))
        ln = nh * ag_ref[...] + ab_ref[...]
        sg = _sigmoid(ln)
        dao = jnp.concatenate([dc_ref[:, 0:A_WIDTH].astype(F32),
                               jnp.where(i < nt - 1, dcn_ref[:, 0:A_WIDTH].astype(F32), 0.0)], axis=0)
        dln = dao * (sg * (1.0 + ln * (1.0 - sg)))
        dag_ref[...] += _colsum(dln[0:tm] * nh[0:tm])
        dab_ref[...] += _colsum(dln[0:tm])
        sdy[0] = _ln_bwd_rows(dln * ag_ref[...], nh, rs)
        _fill_row_shifts(sdy)
        dcb_ref[...] += _colsum(sdy[0, 0:tm, :])
        av = h_ref[:, 0:A_WIDTH].astype(F32)
        s = _sigmoid(h_ref[:, A_WIDTH:2 * A_WIDTH].astype(F32))
        p_own = av * s
        dp = jnp.zeros((tm, A_WIDTH), F32)
        for k in range(A_KERNEL):
            later = _rows_from(sdy, A_KERNEL - 1 - k, tm)
            dcw_ref[k:k + 1, :] += _colsum(later * p_own)
            dp = dp + cw_ref[k:k + 1, :] * later
        dh_ref[:, 0:A_WIDTH] = (dp * s).astype(BF16)
        dh_ref[:, A_WIDTH:2 * A_WIDTH] = (dp * av * s * (1.0 - s)).astype(BF16)

        u, dgu = _gelu_and_grad(h_ref[:, 1024:1536].astype(F32))
        w, dgw = _gelu_and_grad(h_ref[:, 1536:2048].astype(F32))
        nb, rb = _ln_stats(w)
        q = nb * bg_ref[...] + bb_ref[...]
        mixed = _spatial_mix(q, ms_ref, sbt_ref, tm)
        dbo = dc_ref[:, A_WIDTH:].astype(F32)
        dh_ref[:, 1024:1536] = (dbo * mixed * dgu).astype(BF16)
        dmx = dbo * u
        mask = _tril_mask()
        wst = [jnp.where(mask.T, mst_ref[g], 0.0).astype(BF16) for g in range(B_GROUPS)]
        qb = q.astype(BF16)
        dmb = dmx.astype(BF16)
        rows = []
        for c in range(nchunk):
            cols = []
            for g in range(B_GROUPS):
                rs_, cs_ = slice(c * B_CHUNK, (c + 1) * B_CHUNK), slice(g * 128, (g + 1) * 128)
                sbacc[g] += dmx[rs_, cs_]
                dms_ref[g] += _dot(dmb[rs_, cs_], qb[rs_, cs_], NT)
                cols.append(_dot(wst[g], dmb[rs_, cs_], NN))
            rows.append(jnp.concatenate(cols, axis=1))
        dq = jnp.concatenate(rows, axis=0)
        dbg_ref[...] += _colsum(dq * nb)
        dbb_ref[...] += _colsum(dq)
        dh_ref[:, 1536:2048] = (_ln_bwd_rows(dq * bg_ref[...], nb, rb) * dgw).astype(BF16)

        @pl.when(i == nt - 1)
        def _():
            for g in range(B_GROUPS):
                dms_ref[g] = jnp.where(mask, dms_ref[g], 0.0)
                dsb_ref[g] = jnp.sum(sbacc[g], axis=1, keepdims=True)

    last_blk = t // MIX_HALO - 1
    vec = pl.BlockSpec((1, A_WIDTH), lambda i: (0, 0))
    mat = pl.BlockSpec((B_GROUPS, B_CHUNK, B_CHUNK), lambda i: (0, 0, 0))
    taps = pl.BlockSpec((A_KERNEL, A_WIDTH), lambda i: (0, 0))

    def halo(width):
        return pl.BlockSpec((MIX_HALO, width), lambda i: (jnp.minimum((i + 1) * hb, last_blk), 0))

    vshape = jax.ShapeDtypeStruct((1, A_WIDTH), F32)
    outs, moved = _call(
        body, name=name, grid=(nt,),
        in_specs=[pl.BlockSpec((tm, 2048), lambda i: (i, 0)), pl.BlockSpec((tm, A_WIDTH), lambda i: (i, 0)), halo(A_WIDTH),
                  pl.BlockSpec((tm, D_MODEL), lambda i: (i, 0)), halo(D_MODEL),
                  taps, vec, vec, vec, vec, mat, mat, pl.BlockSpec((B_CHUNK, B_GROUPS), lambda i: (0, 0))],
        out_specs=[pl.BlockSpec((tm, 2048), lambda i: (i, 0)), taps, vec, vec, vec, vec, vec, mat,
                   pl.BlockSpec((B_GROUPS, B_CHUNK, 1), lambda i: (0, 0, 0))],
        out_shape=[jax.ShapeDtypeStruct((t, 2048), BF16), jax.ShapeDtypeStruct((A_KERNEL, A_WIDTH), F32),
                   vshape, vshape, vshape, vshape, vshape,
                   jax.ShapeDtypeStruct((B_GROUPS, B_CHUNK, B_CHUNK), F32), jax.ShapeDtypeStruct((B_GROUPS, B_CHUNK, 1), F32)],
        scratch_shapes=[pltpu.VMEM((SUBLANES, r, A_WIDTH), F32), pltpu.VMEM((B_GROUPS, B_CHUNK, B_CHUNK), F32)],
        sem=("arbitrary",), args=(h, y, y, dcat, dcat, cw, ag, ab, bg, bb, ms, mst, sbt), comm=comm)
    return outs if comm is None else (outs, moved)


Q_WIDTH = N_Q_HEADS * HEAD_DIM
KV_WIDTH = 2 * N_KV_HEADS * HEAD_DIM
PAIRS_PER_KV = N_Q_HEADS // N_KV_HEADS // 2
ATT_SCALE = 1.0 / math.sqrt(HEAD_DIM)


def _dup_heads(pair_cols, kv_head):
    lane = lax.broadcasted_iota(jnp.int32, pair_cols.shape, 1)
    rolled = pltpu.roll(pair_cols, HEAD_DIM, 1)
    first = lane < HEAD_DIM
    return jnp.where(first, pair_cols, rolled) if kv_head == 0 else jnp.where(first, rolled, pair_cols)


HEADS_PER_KV = N_Q_HEADS // N_KV_HEADS


def _stack_heads(ref, kh):
    lane = lax.broadcasted_iota(jnp.int32, (ATT_BLOCK, 128), 1)
    rows = []
    for pr in range(PAIRS_PER_KV):
        c0 = (kh * PAIRS_PER_KV + pr) * 128
        pair = ref[:, c0:c0 + 128]
        rows += [jnp.where(lane < HEAD_DIM, pair, jnp.zeros_like(pair)), jnp.where(lane < HEAD_DIM, jnp.zeros_like(pair), pair)]
    return jnp.concatenate(rows, axis=0)


def _unstack_heads(stacked, kh, write):
    lane = lax.broadcasted_iota(jnp.int32, (ATT_BLOCK, 128), 1)
    for pr in range(PAIRS_PER_KV):
        first = stacked[(2 * pr) * ATT_BLOCK:(2 * pr + 1) * ATT_BLOCK]
        second = stacked[(2 * pr + 1) * ATT_BLOCK:(2 * pr + 2) * ATT_BLOCK]
        write((kh * PAIRS_PER_KV + pr) * 128, jnp.where(lane < HEAD_DIM, first, second))


def _sink_row(sink_ref, kh):
    return jnp.concatenate([jnp.full((1, ATT_BLOCK), sink_ref[0, kh * HEADS_PER_KV + h], F32) for h in range(HEADS_PER_KV)], axis=1)


def _att_window_bias():
    sj = lax.broadcasted_iota(jnp.int32, (2 * ATT_BLOCK, HEADS_PER_KV * ATT_BLOCK), 0)
    qi = lax.broadcasted_iota(jnp.int32, (2 * ATT_BLOCK, HEADS_PER_KV * ATT_BLOCK), 1) & (ATT_BLOCK - 1)
    diff = qi + ATT_BLOCK - sj
    return jnp.where((diff >= 0) & (diff < ATT_BLOCK), 0.0, -jnp.inf)


def _att_probs_t(q_all, k2, bias_ref, n, sink):
    st = _dot(k2, q_all, NT) * ATT_SCALE + bias_ref[...]
    st = jnp.concatenate([jnp.where(n > 0, st[0:ATT_BLOCK], -jnp.inf), st[ATT_BLOCK:]], axis=0)
    m = jnp.maximum(jnp.max(st, axis=0, keepdims=True), sink)
    e = jnp.exp(st - m)
    es = jnp.exp(sink - m)
    inv = 1.0 / (jnp.sum(e, axis=0, keepdims=True) + es)
    return e * inv, es * inv


def _attn_fwd(qkv, sinks, name, comm=None):
    t = qkv.shape[0]
    nb = t // ATT_BLOCK
    kvb = Q_WIDTH // KV_WIDTH

    def body(sink_ref, q_ref, kv_ref, kvp_ref, o_ref, bias):
        n = pl.program_id(0)

        @pl.when(n == 0)
        def _():
            bias[...] = _att_window_bias()

        kv = jnp.concatenate([kvp_ref[...], kv_ref[...]], axis=0).astype(F32)

        def write(c0, pair):
            o_ref[:, c0:c0 + 128] = pair.astype(BF16)

        for kh in range(N_KV_HEADS):
            k2 = _dup_heads(kv[:, 0:128], kh).astype(BF16)
            v2 = _dup_heads(kv[:, 128:256], kh).astype(BF16)
            pt, _ = _att_probs_t(_stack_heads(q_ref, kh), k2, bias, n, _sink_row(sink_ref, kh))
            _unstack_heads(_dot(v2, pt, TN).T, kh, write)

    (out,), moved = _call(
        body, name=name, grid=(nb,),
        in_specs=[pl.BlockSpec(memory_space=pltpu.SMEM),
                  pl.BlockSpec((ATT_BLOCK, Q_WIDTH), lambda n: (n, 0)),
                  pl.BlockSpec((ATT_BLOCK, KV_WIDTH), lambda n: (n, kvb)),
                  pl.BlockSpec((ATT_BLOCK, KV_WIDTH), lambda n: (jnp.maximum(n - 1, 0), kvb))],
        out_specs=[pl.BlockSpec((ATT_BLOCK, Q_WIDTH), lambda n: (n, 0))],
        out_shape=[jax.ShapeDtypeStruct((t, Q_WIDTH), BF16)],
        scratch_shapes=[pltpu.VMEM((2 * ATT_BLOCK, HEADS_PER_KV * ATT_BLOCK), F32)],
        sem=("arbitrary",), args=(sinks, qkv, qkv, qkv), comm=comm)
    return out if comm is None else (out, moved)


def _attn_bwd(qkv, d_o, sinks, name, comm=None):
    t = qkv.shape[0]
    nb = t // ATT_BLOCK
    kvb = Q_WIDTH // KV_WIDTH

    def body(sink_ref, q_ref, kv_ref, kvp_ref, do_ref, dq_ref, dkv_ref, dbq_ref, dbkv_ref, dsink_ref, carry, bias):
        n = pl.program_id(0)

        @pl.when(n == 0)
        def _():
            for ref in (dbq_ref, dbkv_ref, dsink_ref, carry):
                ref[...] = jnp.zeros_like(ref)
            dkv_ref[...] = jnp.zeros_like(dkv_ref)
            bias[...] = _att_window_bias()

        @pl.when(n < nb)
        def _():
            kv = jnp.concatenate([kvp_ref[...], kv_ref[...]], axis=0).astype(F32)
            lane2 = lax.broadcasted_iota(jnp.int32, (2 * ATT_BLOCK, 128), 1)
            sink_lane = lax.broadcasted_iota(jnp.int32, (1, 128), 1)
            dsink = jnp.zeros((1, 128), F32)
            dk_parts, dv_parts = [], []

            def write(c0, pair):
                dbq_ref[:, c0:c0 + 128] += _colsum(pair)
                dq_ref[:, c0:c0 + 128] = pair.astype(BF16)

            for kh in range(N_KV_HEADS):
                k2 = _dup_heads(kv[:, 0:128], kh).astype(BF16)
                v2 = _dup_heads(kv[:, 128:256], kh).astype(BF16)
                q_all = _stack_heads(q_ref, kh)
                do_all = _stack_heads(do_ref, kh)
                pt, ps = _att_probs_t(q_all, k2, bias, n, _sink_row(sink_ref, kh))
                dpt = _dot(v2, do_all, NT)
                delta = jnp.sum(pt * dpt, axis=0, keepdims=True)
                dst = pt * (dpt - delta) * ATT_SCALE
                psd = ps * delta
                for h in range(HEADS_PER_KV):
                    dsink = dsink + jnp.where(sink_lane == kh * HEADS_PER_KV + h,
                                              -jnp.sum(psd[:, h * ATT_BLOCK:(h + 1) * ATT_BLOCK]), 0.0)
                _unstack_heads(_dot(k2, dst, TN).T, kh, write)
                dk_acc = _dot(dst, q_all, NN)
                dv_acc = _dot(pt, do_all, NN)
                dk_parts.append(dk_acc + pltpu.roll(dk_acc, HEAD_DIM, 1))
                dv_parts.append(dv_acc + pltpu.roll(dv_acc, HEAD_DIM, 1))
            dk = jnp.where(lane2 < HEAD_DIM, dk_parts[0], dk_parts[1])
            dv = jnp.where(lane2 < HEAD_DIM, dv_parts[0], dv_parts[1])
            dkv_new = jnp.concatenate([dk, dv], axis=1)
            done = carry[...] + dkv_new[0:ATT_BLOCK]

            @pl.when(n > 0)
            def _():
                dkv_ref[...] = done.astype(BF16)
                dbkv_ref[...] += _colsum(done)

            carry[...] = dkv_new[ATT_BLOCK:]
            dsink_ref[...] += dsink

        @pl.when(n == nb)
        def _():
            dkv_ref[...] = carry[...].astype(BF16)
            dbkv_ref[...] += _colsum(carry[...])

    def clamp(n):
        return jnp.minimum(n, nb - 1)

    outs, moved = _call(
        body, name=name, grid=(nb + 1,),
        in_specs=[pl.BlockSpec(memory_space=pltpu.SMEM),
                  pl.BlockSpec((ATT_BLOCK, Q_WIDTH), lambda n: (clamp(n), 0)),
                  pl.BlockSpec((ATT_BLOCK, KV_WIDTH), lambda n: (clamp(n), kvb)),
                  pl.BlockSpec((ATT_BLOCK, KV_WIDTH), lambda n: (jnp.maximum(clamp(n) - 1, 0), kvb)),
                  pl.BlockSpec((ATT_BLOCK, Q_WIDTH), lambda n: (clamp(n), 0))],
        out_specs=[pl.BlockSpec((ATT_BLOCK, Q_WIDTH), lambda n: (clamp(n), 0)),
                   pl.BlockSpec((ATT_BLOCK, KV_WIDTH), lambda n: (jnp.maximum(n - 1, 0), 0)),
                   pl.BlockSpec((1, Q_WIDTH), lambda n: (0, 0)),
                   pl.BlockSpec((1, KV_WIDTH), lambda n: (0, 0)),
                   pl.BlockSpec((1, 128), lambda n: (0, 0))],
        out_shape=[jax.ShapeDtypeStruct((t, Q_WIDTH), BF16), jax.ShapeDtypeStruct((t, KV_WIDTH), BF16),
                   jax.ShapeDtypeStruct((1, Q_WIDTH), F32), jax.ShapeDtypeStruct((1, KV_WIDTH), F32),
                   jax.ShapeDtypeStruct((1, 128), F32)],
        scratch_shapes=[pltpu.VMEM((ATT_BLOCK, KV_WIDTH), F32), pltpu.VMEM((2 * ATT_BLOCK, HEADS_PER_KV * ATT_BLOCK), F32)],
        sem=("arbitrary",), args=(sinks, qkv, qkv, qkv, d_o), comm=comm)
    return outs if comm is None else (outs, moved)


def _adamw_math(g, w, m, v):
    m = ADAM_B1 * m + (1.0 - ADAM_B1) * g
    v = ADAM_B2 * v + (1.0 - ADAM_B2) * (g * g)
    m_hat = m / (1.0 - ADAM_B1 ** ADAM_STEP)
    v_hat = v / (1.0 - ADAM_B2 ** ADAM_STEP)
    delta = -ADAM_LR * (m_hat / (jnp.sqrt(v_hat) + ADAM_EPS) + ADAM_WD * w)
    return delta, m, v


def _sum_partials(p_ref):
    g = p_ref[0].astype(F32)
    for s in range(1, N_DEV):
        g = g + p_ref[s].astype(F32)
    return g


def _adamw_big(parts, w, m, v, name, tr):
    r, c = w.shape
    parts = [p if isinstance(p, tuple) else (p, 0, p.shape[1]) for p in parts]
    tiles = [rows // tr for _, _, rows in parts]
    starts = [sum(tiles[:l]) for l in range(len(parts))]
    assert all(lo % tr == 0 and rows % tr == 0 for _, lo, rows in parts) and sum(tiles) * tr == r

    def body(*refs):
        p_refs, (w_ref, m_ref, v_ref, g_out, d_out, m_out, v_out) = refs[:len(parts)], refs[len(parts):]
        i = pl.program_id(0)
        for l, p_ref in enumerate(p_refs):
            @pl.when((i >= starts[l]) & (i < starts[l] + tiles[l]))
            def _():
                g = _sum_partials(p_ref)
                g_out[...] = g
                d_out[...], m_out[...], v_out[...] = _adamw_math(g, w_ref[...], m_ref[...], v_ref[...])

    def part_spec(l):
        return pl.BlockSpec((N_DEV, tr, c), lambda i: (0, jnp.clip(i - starts[l], 0, tiles[l] - 1) + parts[l][1] // tr, 0))

    tile = pl.BlockSpec((tr, c), lambda i: (i, 0))
    shape = jax.ShapeDtypeStruct((r, c), F32)
    return pl.pallas_call(
        body, name=name, grid=(r // tr,),
        in_specs=[part_spec(l) for l in range(len(parts))] + [tile, tile, tile],
        out_specs=[tile] * 4, out_shape=[shape] * 4,
        compiler_params=_params("parallel"),
    )(*[p[0] for p in parts], w, m, v)


def _adamw_small(parts, ws, ms, vs, name):
    n = len(ws)

    def body(*refs):
        ins, outs = refs[:4 * n], refs[4 * n:]
        for a in range(n):
            g = _sum_partials(ins[a])
            outs[4 * a][...] = g
            outs[4 * a + 1][...], outs[4 * a + 2][...], outs[4 * a + 3][...] = _adamw_math(
                g, ins[n + a][...], ins[2 * n + a][...], ins[3 * n + a][...])

    out_shape = []
    for w in ws:
        out_shape += [jax.ShapeDtypeStruct(w.shape, F32)] * 4
    return pl.pallas_call(body, name=name, out_shape=out_shape, compiler_params=_params())(*parts, *ws, *ms, *vs)


PACK_LANES = 128
PACK_ROWS = 8


def _pack(arrs):
    flat = jnp.concatenate([a.reshape(-1).astype(F32) for a in arrs])
    unit = PACK_LANES * PACK_ROWS
    total = -(-flat.shape[0] // unit) * unit
    return jnp.pad(flat, (0, total - flat.shape[0])).reshape(-1, PACK_LANES)


def _unpack(buf, shapes):
    flat = buf.reshape(N_DEV, -1)
    out, pos = [], 0
    for s in shapes:
        size = math.prod(s)
        out.append(flat[:, pos:pos + size].reshape((N_DEV,) + tuple(s)))
        pos += size
    return out


def _interleave(g):
    return jnp.transpose(g, (1, 0, 2)).reshape(g.shape[1], -1)


def _ffn_backward(dz, dzb, x_in, z_in, g_in, h, cg, cv, u, w_up_t, cw, w_down, tag, exchange=(), exchange_late=(), own_rows=0):
    du = _matmul(dzb, w_down, "nt", BF16, f"ffn{tag}_du", 1024, 1408, 1024)
    d_w_down = _matmul(u, dzb, "tn", BF16, f"ffn{tag}_dwdown", 1408, 1024, 2048)
    (dhg, dhv, dcwg, dcwv, dcbg, dcbv), moved = _ffn_mid_bwd(
        h, cg, cv, du, cw, f"ffn{tag}_mid_bwd", comm=_Comm(exchange=[d_w_down.reshape(N_DEV, -1, D_MODEL), *exchange]))
    d_w_up_t = _matmul_tn_pair(dhg, dhv, x_in, BF16, f"ffn{tag}_dwup", 1408, 1024, 1024,
                               comm=_Comm(exchange=exchange_late) if exchange_late else None)
    if exchange_late:
        d_w_up_t, late = d_w_up_t
        moved = moved + late
    d_up_blocks = d_w_up_t.reshape(N_DEV, -1, D_MODEL)
    outs = _matmul_ln_bwd([(dhg, 0), (dhv, D_FF)], w_up_t, z_in, g_in, dz, f"ffn{tag}_dx_ln_bwd", 256,
                          comm=_Comm(exchange=[(d_up_blocks, 0, own_rows)]) if own_rows else None)
    (dz_in, dzb_in, dg_in, db_in), own = outs if own_rows else (outs, [])
    moved = moved + own
    return (dz_in, dzb_in, dg_in, db_in, d_up_blocks,
            jnp.concatenate([dcwg, dcwv], axis=1), jnp.concatenate([dcbg, dcbv], axis=1), moved)


def kernel(x, ab_w_in, a_conv_w, a_conv_b, a_norm_g, a_norm_b, b_norm_g, b_norm_b, b_spatial_w, b_spatial_b, ab_w_out, c_w_qkv, c_b_qkv, c_sinks, c_w_o, ffn_w_up, ffn_conv_w, ffn_conv_b, ffn_w_down, ln_g, ln_b, loss_target, m_ab_w_in, m_a_conv_w, m_a_conv_b, m_a_norm_g, m_a_norm_b, m_b_norm_g, m_b_norm_b, m_b_spatial_w, m_b_spatial_b, m_ab_w_out, m_c_w_qkv, m_c_b_qkv, m_c_sinks, m_c_w_o, m_ffn_w_up, m_ffn_conv_w, m_ffn_conv_b, m_ffn_w_down, m_ln_g, m_ln_b, v_ab_w_in, v_a_conv_w, v_a_conv_b, v_a_norm_g, v_a_norm_b, v_b_norm_g, v_b_norm_b, v_b_spatial_w, v_b_spatial_b, v_ab_w_out, v_c_w_qkv, v_c_b_qkv, v_c_sinks, v_c_w_o, v_ffn_w_up, v_ffn_conv_w, v_ffn_conv_b, v_ffn_w_down, v_ln_g, v_ln_b):
    me = 4 * lax.axis_index("x") + 2 * lax.axis_index("y") + lax.axis_index("c")
    xt = x[0]
    t = xt.shape[0]

    small_shard_shapes = [a_conv_w.shape, c_b_qkv.shape, ffn_conv_w.shape, ln_g.shape, ln_b.shape]
    up_shard = [jnp.swapaxes(ffn_w_up[l], 0, 1).astype(BF16) for l in range(2)]
    qkv_shard = jnp.swapaxes(c_w_qkv[0], 0, 1).astype(BF16)
    down_shard = [ffn_w_down[l].astype(BF16) for l in range(2)]
    g_win, g_small = _comm_only(
        _Comm(gather=[jnp.swapaxes(ab_w_in[0], 0, 1).astype(BF16), _pack([a_conv_w, c_b_qkv, ffn_conv_w, ln_g, ln_b])]),
        "gather_first")
    w_in = g_win.reshape(-1, D_MODEL)
    g_acw, g_bqkv, g_fcw, g_lng, g_lnb = _unpack(g_small, small_shard_shapes)
    acw = _interleave(g_acw[:, 0])
    bqkv = g_bqkv[:, 0].reshape(1, -1)
    fcw = [_interleave(g_fcw[:, l]) for l in range(2)]
    lng = jnp.transpose(g_lng, (1, 2, 0, 3)).reshape(2, 2, 1, D_MODEL)
    lnb = jnp.transpose(g_lnb, (1, 2, 0, 3)).reshape(2, 2, 1, D_MODEL)
    fcb = [ffn_conv_b[l:l + 1] for l in range(2)]
    ms = b_spatial_w[0]
    mst = jnp.swapaxes(ms, 1, 2)
    sbt = b_spatial_b[0].T

    h0, (g_wout,) = _matmul(xt, w_in, "nt", BF16, "mix_in", 1024, 1024, 1024, comm=_Comm(gather=[ab_w_out[0].astype(BF16)]))
    w_out = g_wout.reshape(D_MODEL, D_MODEL)
    (cat, y0), (g_wup0,) = _mixer_mid_fwd(h0, acw, a_conv_b, a_norm_g, a_norm_b, b_norm_g, b_norm_b, ms, sbt, "mix_mid_fwd",
                                    comm=_Comm(gather=[up_shard[0]]))
    w_up0 = g_wup0.reshape(2 * D_FF, D_MODEL)
    z1, x1 = _matmul_res_ln(cat, w_out, xt, lng[0, 0], lnb[0, 0], "mix_out_ln", 512)
    hf0, (g_wdown0, g_wqkv) = _matmul(x1, w_up0, "nt", BF16, "ffn0_up", 1024, 1408, 1024,
                                      comm=_Comm(gather=[down_shard[0], qkv_shard]))
    w_down0 = g_wdown0.reshape(D_FF, D_MODEL)
    w_qkv = g_wqkv.reshape(Q_WIDTH + KV_WIDTH, D_MODEL)
    (u0, cg0, cv0), (g_wup1,) = _ffn_mid_fwd(hf0, fcw[0], fcb[0], "ffn0_mid_fwd", comm=_Comm(gather=[up_shard[1]]))
    w_up1 = g_wup1.reshape(2 * D_FF, D_MODEL)
    (z2, x2), (g_wo,) = _matmul_res_ln(u0, w_down0, z1, lng[0, 1], lnb[0, 1], "ffn0_down_ln", 512, prev=(lng[0, 0], lnb[0, 0]),
                                       comm=_Comm(gather=[c_w_o[0].astype(BF16)]))
    w_o = g_wo.reshape(D_MODEL, D_MODEL)
    qkv = _matmul(x2, w_qkv, "nt", BF16, "att_qkv", 1024, 1280, 1024, bias=bqkv)
    att, (g_wdown1,) = _attn_fwd(qkv, c_sinks, "att_fwd", comm=_Comm(gather=[down_shard[1]]))
    w_down1 = g_wdown1.reshape(D_FF, D_MODEL)
    z3, x3 = _matmul_res_ln(att, w_o, z2, lng[1, 0], lnb[1, 0], "att_out_ln", 512, prev=(lng[0, 1], lnb[0, 1]))
    hf1 = _matmul(x3, w_up1, "nt", BF16, "ffn1_up", 1024, 1408, 1024)
    u1, cg1, cv1 = _ffn_mid_fwd(hf1, fcw[1], fcb[1], "ffn1_mid_fwd")

    dz4, dz4b, dg11, db11, loss_terms = _matmul_res_ln_loss(u1, w_down1, z3, lng[1, 1], lnb[1, 1], loss_target[0],
                                                      "ffn1_down_ln_loss", 512, prev=(lng[1, 0], lnb[1, 0]))
    dz3, dz3b, dg10, db10, d_wup1, d_fcw1, d_fcb1, (p_wdown1,) = _ffn_backward(
        dz4, dz4b, x3, z3, lng[1, 0], hf1, cg1, cv1, u1, w_up1, fcw[1], w_down1, 1)
    d_att = _matmul(dz3b, w_o, "nt", BF16, "att_dout", 1024, 1024, 1024)
    d_wo = _matmul(att, dz3b, "tn", BF16, "att_dwo", 1024, 1024, 512)
    rows_up = d_wup1.shape[1]
    first = 3 * rows_up // 4
    (dq, dkv, dbq, dbkv, dsinks), (p_wup1a,) = _attn_bwd(qkv, d_att, c_sinks, "att_bwd",
                                                        comm=_Comm(exchange=[(d_wup1, 0, first)]))
    d_wqkv = jnp.concatenate([_matmul(dq, x2, "tn", BF16, "att_dwq", 1024, 1024, 1024),
                              _matmul(dkv, x2, "tn", BF16, "att_dwkv", KV_WIDTH, 1024, 1024)], axis=0)
    dz2, dz2b, dg01, db01 = _matmul_ln_bwd([(dq, 0), (dkv, Q_WIDTH)], w_qkv, z2, lng[0, 1], dz3, "att_dx_ln_bwd", 512)
    early = rows_up // 4
    dz1, dz1b, dg00, db00, d_wup0, d_fcw0, d_fcb0, (p_wdown0, p_wup1b, p_wqkv, p_wo, p_wup0a) = _ffn_backward(
        dz2, dz2b, x1, z1, lng[0, 0], hf0, cg0, cv0, u0, w_up0, fcw[0], w_down0, 0, exchange=[(d_wup1, first, rows_up - first)],
        exchange_late=[d_wqkv.reshape(N_DEV, -1, D_MODEL), d_wo.reshape(N_DEV, -1, D_MODEL)], own_rows=early)
    dcat = _matmul(dz1b, w_out, "nt", BF16, "mix_dcat", 1024, 1024, 1024)
    d_wout = _matmul(cat, dz1b, "tn", BF16, "mix_dwout", 1024, 1024, 512)
    (dh0, d_acw, d_acb, d_ang, d_anb, d_bng, d_bnb, d_ms, d_sb), (p_wup0b, p_wout) = _mixer_mid_bwd(
        h0, y0, dcat, acw, a_norm_g, a_norm_b, b_norm_g, b_norm_b, ms, mst, sbt, "mix_mid_bwd",
        comm=_Comm(exchange=[(d_wup0, early, rows_up - early), d_wout.reshape(N_DEV, -1, D_MODEL)]))
    d_bqkv = jnp.concatenate([dbq, dbkv], axis=1)
    d_lng = jnp.stack([jnp.stack([dg00, dg01]), jnp.stack([dg10, dg11])])
    d_lnb = jnp.stack([jnp.stack([db00, db01]), jnp.stack([db10, db11])])
    small_full = [d_acb, d_ang, d_anb, d_bng, d_bnb, d_ms, d_sb, dsinks[:, :N_Q_HEADS], jnp.concatenate([d_fcb0, d_fcb1], axis=0),
                  d_acw, d_bqkv, jnp.stack([d_fcw0, d_fcw1]), d_lng, d_lnb, loss_terms]
    d_win, (g_small_grads,) = _matmul(dh0, xt, "tn", BF16, "mix_dwin", 1024, 1024, 512, comm=_Comm(gather=[_pack(small_full)]))
    grad_x, (p_win,) = _matmul(dh0, w_in, "nn", F32, "mix_dx", 1024, 1024, 1024, res=dz1, res_scale=ALPHA,
                               comm=_Comm(exchange=[d_win.reshape(N_DEV, -1, D_MODEL)]))


    big = {}
    for nm, p, w, m, v, tr, transposed in [
            ("ab_w_in", [p_win], ab_w_in, m_ab_w_in, v_ab_w_in, 256, True),
            ("ab_w_out", [p_wout], ab_w_out, m_ab_w_out, v_ab_w_out, 128, False),
            ("c_w_qkv", [p_wqkv], c_w_qkv, m_c_w_qkv, v_c_w_qkv, 160, True), ("c_w_o", [p_wo], c_w_o, m_c_w_o, v_c_w_o, 128, False),
            ("ffn_w_up", [(p_wup0a, 0, early), (p_wup0b, early, rows_up - early), (p_wup1a, 0, first), (p_wup1b, first, rows_up - first)], ffn_w_up, m_ffn_w_up, v_ffn_w_up, 176, True),
            ("ffn_w_down", [p_wdown0, p_wdown1], ffn_w_down, m_ffn_w_down, v_ffn_w_down, 176, False)]:
        def two_d(a):
            a = jnp.swapaxes(a, 1, 2) if transposed else a
            return a.reshape(-1, a.shape[-1])

        def back(o):
            return jnp.swapaxes(o.reshape(w.shape[0], w.shape[2], w.shape[1]), 1, 2) if transposed else o.reshape(w.shape)

        outs = _adamw_big(p, two_d(w), two_d(m), two_d(v), "adamw_" + nm, tr)
        big[nm] = [back(o) for o in outs]

    *gs, loss_parts = _unpack(g_small_grads, [a.shape for a in small_full])
    loss = 0.5 / D_MODEL * jnp.sum(loss_parts)

    def my_shard(g, width):
        g = g.reshape(g.shape[:-1] + (N_DEV, width))
        return lax.dynamic_index_in_dim(g, me, axis=g.ndim - 2, keepdims=False)

    small_names = ["a_conv_b", "a_norm_g", "a_norm_b", "b_norm_g", "b_norm_b", "b_spatial_w", "b_spatial_b", "c_sinks", "ffn_conv_b",
                   "a_conv_w", "c_b_qkv", "ffn_conv_w", "ln_g", "ln_b"]
    small_w = [a_conv_b, a_norm_g, a_norm_b, b_norm_g, b_norm_b, b_spatial_w, b_spatial_b, c_sinks, ffn_conv_b,
               a_conv_w, c_b_qkv, ffn_conv_w, ln_g, ln_b]
    small_m = [m_a_conv_b, m_a_norm_g, m_a_norm_b, m_b_norm_g, m_b_norm_b, m_b_spatial_w, m_b_spatial_b, m_c_sinks, m_ffn_conv_b,
               m_a_conv_w, m_c_b_qkv, m_ffn_conv_w, m_ln_g, m_ln_b]
    small_v = [v_a_conv_b, v_a_norm_g, v_a_norm_b, v_b_norm_g, v_b_norm_b, v_b_spatial_w, v_b_spatial_b, v_c_sinks, v_ffn_conv_b,
               v_a_conv_w, v_c_b_qkv, v_ffn_conv_w, v_ln_g, v_ln_b]
    gs[9:] = [my_shard(g, w.shape[-1]) for g, w in zip(gs[9:], small_w[9:])]
    two_d = [(-1, w.shape[-1]) for w in small_w]
    outs = _adamw_small([g.reshape((N_DEV,) + w.reshape(s).shape) for g, w, s in zip(gs, small_w, two_d)],
                        [w.reshape(s) for w, s in zip(small_w, two_d)], [m.reshape(s) for m, s in zip(small_m, two_d)],
                        [v.reshape(s) for v, s in zip(small_v, two_d)], "adamw_small")
    small = {nm: [o.reshape(w.shape) for o in outs[4 * a:4 * a + 4]] for a, (nm, w) in enumerate(zip(small_names, small_w))}

    res = {**big, **small}
    order = ["ab_w_in", "a_conv_w", "a_conv_b", "a_norm_g", "a_norm_b", "b_norm_g", "b_norm_b", "b_spatial_w", "b_spatial_b", "ab_w_out",
             "c_w_qkv", "c_b_qkv", "c_sinks", "c_w_o", "ffn_w_up", "ffn_conv_w", "ffn_conv_b", "ffn_w_down", "ln_g", "ln_b"]
    return (loss, grad_x[None], *[res[nm][0] for nm in order], *[res[nm][1] for nm in order],
            *[res[nm][2] for nm in order], *[res[nm][3] for nm in order])
```

```python
import functools
import math

import jax
import jax.numpy as jnp
from jax import lax
from jax.experimental import pallas as pl
from jax.experimental.pallas import tpu as pltpu

F32 = jnp.float32
BF16 = jnp.bfloat16

N_DEV = 8
D_MODEL = 1024
A_WIDTH = 512
A_KERNEL = 31
B_GROUPS = 4
B_CHUNK = 128
HEAD_DIM = 64
N_Q_HEADS = 16
N_KV_HEADS = 2
ATT_BLOCK = 128
D_FF = 2816
FFN_KERNEL = 3
ALPHA = (2.0 * 2) ** 0.25
LN_EPS = 1e-5
GELU_K = math.sqrt(2.0 / math.pi)
GELU_C = 0.044715
ADAM_LR = 0.001
ADAM_B1 = 0.9
ADAM_B2 = 0.999
ADAM_EPS = 1e-08
ADAM_WD = 0.01
ADAM_STEP = 10
VMEM_LIMIT = 56 * 1024 * 1024
MESH_ID = pl.DeviceIdType.MESH


def _params(*sem):
    return pltpu.CompilerParams(dimension_semantics=sem, vmem_limit_bytes=VMEM_LIMIT)


def _gelu(x):
    t = jnp.tanh(GELU_K * x * (1.0 + GELU_C * x * x))
    return 0.5 * x * (1.0 + t)


def _gelu_and_grad(x):
    x2 = x * x
    t = jnp.tanh(GELU_K * x * (1.0 + GELU_C * x2))
    g = 0.5 * x * (1.0 + t)
    dg = 0.5 * (1.0 + t) + 0.5 * x * (1.0 - t * t) * (GELU_K * (1.0 + 3.0 * GELU_C * x2))
    return g, dg


def _sigmoid(x):
    return 1.0 / (1.0 + jnp.exp(-x))


def _ln_stats(z):
    mu = jnp.mean(z, axis=-1, keepdims=True)
    zc = z - mu
    var = jnp.mean(zc * zc, axis=-1, keepdims=True)
    r = lax.rsqrt(var + LN_EPS)
    return zc * r, r


def _ln_bwd_rows(dn, nh, r):
    return r * (dn - jnp.mean(dn, axis=-1, keepdims=True) - nh * jnp.mean(dn * nh, axis=-1, keepdims=True))


def _colsum(x):
    return jnp.sum(x, axis=0, keepdims=True)


def _dot(a, b, dims):
    return lax.dot_general(a.astype(BF16), b.astype(BF16), (dims, ((), ())), preferred_element_type=F32)


NN = ((1,), (0,))
NT = ((1,), (1,))
TN = ((0,), (0,))


ANY = pl.BlockSpec(memory_space=pl.ANY)
N_RELATIONS = N_DEV - 1


def _my_place():
    return lax.axis_index("x"), lax.axis_index("y"), lax.axis_index("c")


class _Comm:
    def __init__(self, gather=(), exchange=()):
        gather = [e if isinstance(e, tuple) else (e, 0, e.shape[0], None) for e in gather]
        exchange = [e if isinstance(e, tuple) else (e, 0, e.shape[1]) for e in exchange]
        self.arrs = [e[0] for e in gather] + [e[0] for e in exchange]
        self.n_gather = len(gather)
        self.n = len(self.arrs)
        self.rows = [pl.ds(lo, n) for _, lo, n, _ in gather] + [pl.ds(lo, n) for _, lo, n in exchange]
        self.into = {i: e[3] for i, e in enumerate(gather) if e[3] is not None}

    def out_shape(self):
        return [jax.ShapeDtypeStruct(((N_DEV,) + a.shape) if i < self.n_gather else a.shape, a.dtype)
                for i, a in enumerate(self.arrs)]

    def sems(self):
        return [pltpu.SemaphoreType.DMA((self.n, N_RELATIONS)), pltpu.SemaphoreType.DMA((self.n, N_RELATIONS)),
                pltpu.SemaphoreType.DMA((self.n,))]

    def _gather_copy(self, ins, outs, sems, a, k, place, to, from_input=False):
        px, py, pc = place
        block = outs[a].at[4 * px + 2 * py + pc, self.rows[a]]
        return pltpu.make_async_remote_copy(
            src_ref=ins[a].at[self.rows[a]] if from_input else block, dst_ref=block,
            send_sem=sems[0].at[a, k], recv_sem=sems[1].at[a, k], device_id=to, device_id_type=MESH_ID)

    def _exchange_copy(self, ins, outs, sems, a, k, landing=False):
        x, y, c = _my_place()
        me = 4 * x + 2 * y + c
        peer = (x ^ (k >> 2), y ^ ((k >> 1) & 1), c ^ (k & 1))
        return pltpu.make_async_remote_copy(
            src_ref=ins[a].at[me ^ k, self.rows[a]], dst_ref=outs[a].at[(me ^ k) if landing else me, self.rows[a]],
            send_sem=sems[0].at[a, k - 1], recv_sem=sems[1].at[a, k - 1], device_id=peer, device_id_type=MESH_ID)

    def _local_copy(self, ins, outs, sems, a):
        x, y, c = _my_place()
        me = 4 * x + 2 * y + c
        if a < self.n_gather:
            return pltpu.make_async_copy(ins[a].at[self.rows[a]], outs[a].at[me, self.rows[a]], sems[2].at[a])
        return pltpu.make_async_copy(ins[a].at[me, self.rows[a]], outs[a].at[me, self.rows[a]], sems[2].at[a])

    def _first_stage(self, ins, outs, sems, a):
        x, y, c = _my_place()
        me = (x, y, c)
        chips = [(1 - x, y), (x, 1 - y), (1 - x, 1 - y)]
        return ([self._gather_copy(ins, outs, sems, a, 0, me, (x, y, 1 - c), from_input=True)]
                + [self._gather_copy(ins, outs, sems, a, 1 + j, me, (*chip, c), from_input=True) for j, chip in enumerate(chips)])

    def start(self, ins, outs, sems):
        for a in range(self.n):
            self._local_copy(ins, outs, sems, a).start()
        for a in range(self.n_gather):
            for cp in self._first_stage(ins, outs, sems, a):
                cp.start()
        for k in range(1, N_DEV):
            for a in range(self.n_gather, self.n):
                self._exchange_copy(ins, outs, sems, a, k).start()

    def forward(self, ins, outs, sems):
        x, y, c = _my_place()
        me, sibling = (x, y, c), (x, y, 1 - c)
        for j, chip in enumerate([(1 - x, y), (x, 1 - y), (1 - x, 1 - y)]):
            for a in range(self.n_gather):
                self._gather_copy(ins, outs, sems, a, 1 + j, (*chip, c), me).wait_recv()
                self._gather_copy(ins, outs, sems, a, 4 + j, (*chip, c), sibling).start()

    def finish(self, ins, outs, sems):
        x, y, c = _my_place()
        me, sibling = (x, y, c), (x, y, 1 - c)
        chips = [(1 - x, y), (x, 1 - y), (1 - x, 1 - y)]
        passed = [self._gather_copy(ins, outs, sems, a, 4 + j, (*chip, c), sibling)
                  for j, chip in enumerate(chips) for a in range(self.n_gather)]
        for a in range(self.n_gather):
            self._gather_copy(ins, outs, sems, a, 0, sibling, me).wait_recv()
            for j, chip in enumerate(chips):
                self._gather_copy(ins, outs, sems, a, 4 + j, (*chip, 1 - c), me).wait_recv()
        for k in range(1, N_DEV):
            for a in range(self.n_gather, self.n):
                self._exchange_copy(ins, outs, sems, a, k, landing=True).wait_recv()
        for a in range(self.n_gather):
            for cp in self._first_stage(ins, outs, sems, a):
                cp.wait_send()
        for cp in passed:
            cp.wait_send()
        for k in range(1, N_DEV):
            for a in range(self.n_gather, self.n):
                self._exchange_copy(ins, outs, sems, a, k).wait_send()
        for a in range(self.n):
            self._local_copy(ins, outs, sems, a).wait()


def _comm_only(comm, name):
    assert not comm.into

    def body(*refs):
        ins, outs, sems = refs[:comm.n], refs[comm.n:2 * comm.n], refs[2 * comm.n:]
        comm.start(ins, outs, sems)
        comm.forward(ins, outs, sems)
        comm.finish(ins, outs, sems)

    return pl.pallas_call(body, name=name, in_specs=[ANY] * comm.n, out_specs=[ANY] * comm.n,
                          out_shape=comm.out_shape(), scratch_shapes=comm.sems())(*comm.arrs)


def _call(body, *, name, grid, in_specs, out_specs, out_shape, args, sem, scratch_shapes=(), comm=None):
    in_specs, out_specs, out_shape, scratch_shapes = list(in_specs), list(out_specs), list(out_shape), list(scratch_shapes)
    if comm is None:
        outs = pl.pallas_call(body, name=name, grid=grid, in_specs=in_specs, out_specs=out_specs, out_shape=out_shape,
                              scratch_shapes=scratch_shapes, compiler_params=_params(*sem))(*args)
        return list(outs), []
    n_in, n_out, n_scr, nc = len(in_specs), len(out_specs), len(scratch_shapes), comm.n
    completed = sorted(comm.into)

    def wrapped(*refs):
        ins, refs = refs[:n_in], refs[n_in:]
        c_in, refs = refs[:nc], refs[nc + len(completed):]
        outs, refs = refs[:n_out], refs[n_out:]
        c_out, refs = refs[:nc], refs[nc:]
        scr, sems = refs[:n_scr], refs[n_scr:]
        step = functools.reduce(lambda acc, ax: acc * grid[ax] + pl.program_id(ax), range(len(grid)), 0)
        steps = math.prod(grid)

        @pl.when(step == 0)
        def _():
            comm.start(c_in, c_out, sems)

        @pl.when(step == steps - 1)
        def _():
            comm.forward(c_in, c_out, sems)

        body(*ins, *outs, *scr)

        @pl.when(step == steps - 1)
        def _():
            comm.finish(c_in, c_out, sems)

    outs = pl.pallas_call(
        wrapped, name=name, grid=grid, in_specs=in_specs + [ANY] * (nc + len(completed)), out_specs=out_specs + [ANY] * nc,
        out_shape=out_shape + comm.out_shape(), scratch_shapes=scratch_shapes + comm.sems(),
        input_output_aliases={n_in + nc + pos: n_out + item for pos, item in enumerate(completed)},
        compiler_params=_params(*(["arbitrary"] * len(grid))))(*args, *comm.arrs, *[comm.into[item] for item in completed])
    return list(outs[:n_out]), list(outs[n_out:])


def _matmul(a, b, mode, out_dtype, name, tm, tn, tk, *, bias=None, res=None, res_scale=1.0, b_off=0, comm=None):
    tm = min(tm, a.shape[1] if mode == "tn" else a.shape[0])
    tk = min(tk, a.shape[0] if mode == "tn" else a.shape[1])
    if mode == "nn":
        (m, k), n = a.shape, b.shape[1]
        a_spec = pl.BlockSpec((tm, tk), lambda i, j, kk: (i, kk))
        b_spec = pl.BlockSpec((tk, tn), lambda i, j, kk: (kk + b_off, j))
        dims = NN
    elif mode == "nt":
        (m, k), n = a.shape, b.shape[0]
        a_spec = pl.BlockSpec((tm, tk), lambda i, j, kk: (i, kk))
        b_spec = pl.BlockSpec((tn, tk), lambda i, j, kk: (j, kk + b_off))
        dims = NT
    else:
        (k, m), n = a.shape, b.shape[1]
        a_spec = pl.BlockSpec((tk, tm), lambda i, j, kk: (kk, i))
        b_spec = pl.BlockSpec((tk, tn), lambda i, j, kk: (kk, j))
        dims = TN
    assert m % tm == 0 and n % tn == 0 and k % tk == 0, (name, m, n, k)
    nk = k // tk
    in_specs = [a_spec, b_spec]
    args = [a, b]
    if bias is not None:
        in_specs.append(pl.BlockSpec((1, tn), lambda i, j, kk: (0, j)))
        args.append(bias)
    if res is not None:
        in_specs.append(pl.BlockSpec((tm, tn), lambda i, j, kk: (i, j)))
        args.append(res)

    def finish(out, refs, o_ref):
        pos = 2
        if bias is not None:
            out = out + refs[pos][...]
            pos += 1
        if res is not None:
            out = out + res_scale * refs[pos][...].astype(F32)
        o_ref[...] = out.astype(out_dtype)

    def body_one_step(*refs):
        finish(_dot(refs[0][...], refs[1][...], dims), refs, refs[-1])

    def body(*refs):
        a_ref, b_ref = refs[0], refs[1]
        o_ref, acc = refs[-2], refs[-1]
        kk = pl.program_id(2)

        @pl.when(kk == 0)
        def _():
            acc[...] = jnp.zeros_like(acc)

        acc[...] += _dot(a_ref[...], b_ref[...], dims)

        @pl.when(kk == nk - 1)
        def _():
            finish(acc[...], refs, o_ref)

    (out,), moved = _call(
        body_one_step if nk == 1 else body, name=name, grid=(m // tm, n // tn, nk),
        in_specs=in_specs, out_specs=[pl.BlockSpec((tm, tn), lambda i, j, kk: (i, j))],
        out_shape=[jax.ShapeDtypeStruct((m, n), out_dtype)],
        scratch_shapes=[] if nk == 1 else [pltpu.VMEM((tm, tn), F32)],
        sem=("parallel", "parallel", "arbitrary"), args=args, comm=comm)
    return out if comm is None else (out, moved)


def _matmul_tn_pair(a0, a1, b, out_dtype, name, tm, tn, tk, comm=None):
    (k, m), n = a0.shape, b.shape[1]
    tk = min(tk, k)
    assert a1.shape == a0.shape and m % tm == 0 and n % tn == 0 and k % tk == 0, (name, m, n, k)
    mi, nk = m // tm, k // tk

    def body(a0_ref, a1_ref, b_ref, o_ref, acc):
        i, kk = pl.program_id(0), pl.program_id(2)

        @pl.when(kk == 0)
        def _():
            acc[...] = jnp.zeros_like(acc)

        @pl.when(i < mi)
        def _():
            acc[...] += _dot(a0_ref[...], b_ref[...], TN)

        @pl.when(i >= mi)
        def _():
            acc[...] += _dot(a1_ref[...], b_ref[...], TN)

        @pl.when(kk == nk - 1)
        def _():
            o_ref[...] = acc[...].astype(out_dtype)

    (out,), moved = _call(
        body, name=name, grid=(2 * mi, n // tn, nk),
        in_specs=[pl.BlockSpec((tk, tm), lambda i, j, kk: (jnp.where(i < mi, kk, nk - 1), jnp.minimum(i, mi - 1))),
                  pl.BlockSpec((tk, tm), lambda i, j, kk: (jnp.where(i >= mi, kk, 0), jnp.maximum(i - mi, 0))),
                  pl.BlockSpec((tk, tn), lambda i, j, kk: (kk, j))],
        out_specs=[pl.BlockSpec((tm, tn), lambda i, j, kk: (i, j))],
        out_shape=[jax.ShapeDtypeStruct((2 * m, n), out_dtype)],
        scratch_shapes=[pltpu.VMEM((tm, tn), F32)],
        sem=("parallel", "parallel", "arbitrary"), args=(a0, a1, b), comm=comm)
    return out if comm is None else (out, moved)


def _residual_input(x_ref, prev_refs):
    if not prev_refs:
        return x_ref[...]
    nh, _ = _ln_stats(x_ref[...])
    return nh * prev_refs[0][...] + prev_refs[1][...]


def _matmul_res_ln(a, b, x, g, beta, name, tm, prev=None, comm=None):
    t, k = a.shape
    d = b.shape[1]
    tm = min(tm, t)
    assert t % tm == 0
    n_prev = 0 if prev is None else 2

    def body(a_ref, b_ref, x_ref, g_ref, beta_ref, *rest):
        z_ref, xo_ref = rest[n_prev:]
        z = ALPHA * _residual_input(x_ref, rest[:n_prev]) + _dot(a_ref[...], b_ref[...], NN)
        nh, _ = _ln_stats(z)
        z_ref[...] = z
        xo_ref[...] = (nh * g_ref[...] + beta_ref[...]).astype(BF16)

    row = pl.BlockSpec((tm, d), lambda i: (i, 0))
    vec = pl.BlockSpec((1, d), lambda i: (0, 0))
    outs, moved = _call(
        body, name=name, grid=(t // tm,),
        in_specs=[pl.BlockSpec((tm, k), lambda i: (i, 0)), pl.BlockSpec((k, d), lambda i: (0, 0)), row, vec, vec] + [vec] * n_prev,
        out_specs=[row, row],
        out_shape=[jax.ShapeDtypeStruct((t, d), F32), jax.ShapeDtypeStruct((t, d), BF16)],
        sem=("parallel",), args=(a, b, x, g, beta, *(prev or ())), comm=comm)
    return outs if comm is None else (outs, moved)


def _matmul_ln_bwd(parts, b, z, g, dres, name, tm, comm=None):
    m = parts[0][0].shape[0]
    d = b.shape[1]
    tm = min(tm, m)
    n = len(parts)
    assert m % tm == 0 and all(row % a.shape[1] == 0 for a, row in parts)

    def body(*refs):
        z_ref, g_ref, dres_ref = refs[2 * n:2 * n + 3]
        dz_ref, dzb_ref, dg_ref, db_ref = refs[-4:]

        @pl.when(pl.program_id(0) == 0)
        def _():
            dg_ref[...] = jnp.zeros_like(dg_ref)
            db_ref[...] = jnp.zeros_like(db_ref)

        dy = ALPHA * dres_ref[...]
        for p in range(n):
            dy = dy + _dot(refs[p][...], refs[n + p][...], NN)
        nh, r = _ln_stats(z_ref[...])
        dg_ref[...] += _colsum(dy * nh)
        db_ref[...] += _colsum(dy)
        dz = _ln_bwd_rows(dy * g_ref[...], nh, r)
        dz_ref[...] = dz
        dzb_ref[...] = dz.astype(BF16)

    def b_spec(a, row):
        blk = row // a.shape[1]
        return pl.BlockSpec((a.shape[1], d), lambda i: (blk, 0))

    row = pl.BlockSpec((tm, d), lambda i: (i, 0))
    vec = pl.BlockSpec((1, d), lambda i: (0, 0))
    vshape = jax.ShapeDtypeStruct((1, d), F32)
    outs, moved = _call(
        body, name=name, grid=(m // tm,),
        in_specs=[pl.BlockSpec((tm, a.shape[1]), lambda i: (i, 0)) for a, _ in parts] + [b_spec(a, r_) for a, r_ in parts]
        + [row, vec, row],
        out_specs=[row, row, vec, vec],
        out_shape=[jax.ShapeDtypeStruct((m, d), F32), jax.ShapeDtypeStruct((m, d), BF16), vshape, vshape],
        sem=("arbitrary",), args=(*[a for a, _ in parts], *([b] * n), z, g, dres), comm=comm)
    return outs if comm is None else (outs, moved)


def _matmul_res_ln_loss(a, b, x, g, beta, target, name, tm, prev):
    t, k = a.shape
    d = b.shape[1]
    tm = min(tm, t)

    def body(a_ref, b_ref, x_ref, g_ref, beta_ref, t_ref, gp_ref, bp_ref, dz_ref, dzb_ref, dg_ref, db_ref, loss_ref):
        @pl.when(pl.program_id(0) == 0)
        def _():
            dg_ref[...] = jnp.zeros_like(dg_ref)
            db_ref[...] = jnp.zeros_like(db_ref)
            loss_ref[...] = jnp.zeros_like(loss_ref)

        nh, r = _ln_stats(ALPHA * _residual_input(x_ref, (gp_ref, bp_ref)) + _dot(a_ref[...], b_ref[...], NN))
        err = nh * g_ref[...] + beta_ref[...] - t_ref[...]
        loss_ref[...] += _colsum(err * err)
        dy = err * (1.0 / d)
        dg_ref[...] += _colsum(dy * nh)
        db_ref[...] += _colsum(dy)
        dz = _ln_bwd_rows(dy * g_ref[...], nh, r)
        dz_ref[...] = dz
        dzb_ref[...] = dz.astype(BF16)

    row = pl.BlockSpec((tm, d), lambda i: (i, 0))
    vec = pl.BlockSpec((1, d), lambda i: (0, 0))
    vshape = jax.ShapeDtypeStruct((1, d), F32)
    return pl.pallas_call(
        body, name=name, grid=(t // tm,),
        in_specs=[pl.BlockSpec((tm, k), lambda i: (i, 0)), pl.BlockSpec((k, d), lambda i: (0, 0)), row, vec, vec, row, vec, vec],
        out_specs=[row, row, vec, vec, vec],
        out_shape=[jax.ShapeDtypeStruct((t, d), F32), jax.ShapeDtypeStruct((t, d), BF16), vshape, vshape, vshape],
        compiler_params=_params("arbitrary"),
    )(a, b, x, g, beta, target, *prev)


FFN_HALO = 16
FFN_CHUNK = 256
LANES = 128
SUBLANES = 8


def _rows_up(e, start, rows):
    if start % SUBLANES == 0:
        return e[start:start + rows]
    return pltpu.roll(e, e.shape[0] - start, 0)[0:rows]


def _fold(x):
    return jnp.sum(x.reshape(x.shape[0] // SUBLANES, SUBLANES, x.shape[1]), axis=0)


def _ffn_mid_fwd(h, cw, cb, name, tm=1024, tc=1408, comm=None):
    t, f2 = h.shape
    tm = min(tm, t)
    f = f2 // 2
    nj, nt, hb = f // tc, t // tm, tm // FFN_HALO

    ch = min(FFN_CHUNK, tm)

    def body(hg, hgp, hv, hvp, cwg, cwv, cbg, cbv, u_ref, cg_ref, cv_ref):
        i = pl.program_id(1)
        o = FFN_HALO - FFN_KERNEL + 1
        for lg in range(tc // LANES):
            cols = slice(lg * LANES, (lg + 1) * LANES)
            wg, wv = [cwg[k:k + 1, cols] for k in range(FFN_KERNEL)], [cwv[k:k + 1, cols] for k in range(FFN_KERNEL)]
            bg, bv = cbg[:, cols], cbv[:, cols]

            def emit(base, eg, ev):
                cg = wg[0] * _rows_up(eg, o, ch) + wg[1] * _rows_up(eg, o + 1, ch) + wg[2] * _rows_up(eg, o + 2, ch) + bg
                cv = wv[0] * _rows_up(ev, o, ch) + wv[1] * _rows_up(ev, o + 1, ch) + wv[2] * _rows_up(ev, o + 2, ch) + bv
                u_ref[pl.ds(base, ch), cols] = (_gelu(cg) * cv).astype(BF16)
                cg_ref[pl.ds(base, ch), cols] = cg.astype(BF16)
                cv_ref[pl.ds(base, ch), cols] = cv.astype(BF16)

            def first(main, prev):
                return jnp.concatenate([jnp.where(i > 0, prev[:, cols].astype(F32), 0.0), main[0:ch, cols].astype(F32)], axis=0)

            def inner(c, carry):
                base = pl.multiple_of(c * ch, ch)
                emit(base, hg[pl.ds(base - FFN_HALO, ch + FFN_HALO), cols].astype(F32),
                     hv[pl.ds(base - FFN_HALO, ch + FFN_HALO), cols].astype(F32))
                return carry

            emit(0, first(hg, hgp), first(hv, hvp))
            if tm > ch:
                lax.fori_loop(1, tm // ch, inner, 0)

    def main_spec(off):
        return pl.BlockSpec((tm, tc), lambda j, i: (i, j + off))

    def prev_spec(off):
        return pl.BlockSpec((FFN_HALO, tc), lambda j, i: (jnp.maximum(i * hb - 1, 0), j + off))

    def par_spec(rows, off):
        return pl.BlockSpec((rows, tc), lambda j, i: (0, j + off))

    outs, moved = _call(
        body, name=name, grid=(nj, nt),
        in_specs=[main_spec(0), prev_spec(0), main_spec(nj), prev_spec(nj),
                  par_spec(FFN_KERNEL, 0), par_spec(FFN_KERNEL, nj), par_spec(1, 0), par_spec(1, nj)],
        out_specs=[pl.BlockSpec((tm, tc), lambda j, i: (i, j))] * 3,
        out_shape=[jax.ShapeDtypeStruct((t, f), BF16)] * 3,
        sem=("parallel", "arbitrary"), args=(h, h, h, h, cw, cw, cb, cb), comm=comm)
    return outs if comm is None else (outs, moved)


def _ffn_mid_bwd(h, cg, cv, du, cw, name, tm=1024, tc=1408, comm=None):
    t, f2 = h.shape
    tm = min(tm, t)
    f = f2 // 2
    nj, nt, hb = f // tc, t // tm, tm // FFN_HALO

    ch = min(FFN_CHUNK, tm)
    ahead = ch + SUBLANES
    n_ch = tm // ch

    def body(hg, hv, cg_ref, cgn_ref, cv_ref, cvn_ref, du_ref, dun_ref, cwg, cwv,
             dhg_ref, dhv_ref, dcwg_ref, dcwv_ref, dcbg_ref, dcbv_ref):
        i = pl.program_id(1)

        @pl.when(i == 0)
        def _():
            for ref in (dcwg_ref, dcwv_ref, dcbg_ref, dcbv_ref):
                ref[...] = jnp.zeros_like(ref)

        for lg in range(tc // LANES):
            cols = slice(lg * LANES, (lg + 1) * LANES)
            wg, wv = [cwg[k:k + 1, cols] for k in range(FFN_KERNEL)], [cwv[k:k + 1, cols] for k in range(FFN_KERNEL)]

            def emit(base, cg_e, cv_e, du_e, acc):
                cg_a, cv_a, du_a = cg_e[0:ahead], cv_e[0:ahead], du_e[0:ahead]
                gl, dgl = _gelu_and_grad(cg_a)

                def back(d, h_ref, w, dh_ref):
                    later = [d[0:ch], _rows_up(d, 1, ch), _rows_up(d, 2, ch)]
                    dh_ref[pl.ds(base, ch), cols] = (w[2] * later[0] + w[1] * later[1] + w[0] * later[2]).astype(BF16)
                    h_own = h_ref[pl.ds(base, ch), cols].astype(F32)
                    return [_fold(later[0])] + [_fold(later[FFN_KERNEL - 1 - k] * h_own) for k in range(FFN_KERNEL)]

                sums = back(du_a * cv_a * dgl, hg, wg, dhg_ref) + back(du_a * gl, hv, wv, dhv_ref)
                return tuple(a + s_ for a, s_ in zip(acc, sums))

            def inner(c, acc):
                base = pl.multiple_of(c * ch, ch)
                rows = pl.ds(base, ch + FFN_HALO)
                return emit(base, cg_ref[rows, cols].astype(F32), cv_ref[rows, cols].astype(F32), du_ref[rows, cols].astype(F32), acc)

            def last(acc):
                def rows(main, after):
                    return jnp.concatenate([main[tm - ch:tm, cols].astype(F32), after], axis=0)

                du_next = jnp.where(i < nt - 1, dun_ref[:, cols].astype(F32), 0.0)
                return emit(tm - ch, rows(cg_ref, cgn_ref[:, cols].astype(F32)), rows(cv_ref, cvn_ref[:, cols].astype(F32)),
                            rows(du_ref, du_next), acc)

            acc = (jnp.zeros((SUBLANES, LANES), F32),) * (2 * (1 + FFN_KERNEL))
            if n_ch > 1:
                acc = lax.fori_loop(0, n_ch - 1, inner, acc)
            acc = last(acc)
            dcbg_ref[:, cols] += _colsum(acc[0])
            dcbv_ref[:, cols] += _colsum(acc[1 + FFN_KERNEL])
            for k in range(FFN_KERNEL):
                dcwg_ref[k:k + 1, cols] += _colsum(acc[1 + k])
                dcwv_ref[k:k + 1, cols] += _colsum(acc[2 + FFN_KERNEL + k])

    last_blk = t // FFN_HALO - 1

    def main_spec(off):
        return pl.BlockSpec((tm, tc), lambda j, i: (i, j + off))

    def next_spec(off):
        return pl.BlockSpec((FFN_HALO, tc), lambda j, i: (jnp.minimum((i + 1) * hb, last_blk), j + off))

    def par_spec(rows, off):
        return pl.BlockSpec((rows, tc), lambda j, i: (0, j + off))

    out_tile = pl.BlockSpec((tm, tc), lambda j, i: (i, j))
    outs, moved = _call(
        body, name=name, grid=(nj, nt),
        in_specs=[main_spec(0), main_spec(nj), main_spec(0), next_spec(0), main_spec(0), next_spec(0), main_spec(0), next_spec(0),
                  par_spec(FFN_KERNEL, 0), par_spec(FFN_KERNEL, nj)],
        out_specs=[out_tile, out_tile, par_spec(FFN_KERNEL, 0), par_spec(FFN_KERNEL, 0), par_spec(1, 0), par_spec(1, 0)],
        out_shape=[jax.ShapeDtypeStruct((t, f), BF16), jax.ShapeDtypeStruct((t, f), BF16),
                   jax.ShapeDtypeStruct((FFN_KERNEL, f), F32), jax.ShapeDtypeStruct((FFN_KERNEL, f), F32),
                   jax.ShapeDtypeStruct((1, f), F32), jax.ShapeDtypeStruct((1, f), F32)],
        sem=("parallel", "arbitrary"), args=(h, h, cg, cg, cv, cv, du, du, cw, cw), comm=comm)
    return outs if comm is None else (outs, moved)


MIX_HALO = 32


def _glu(hh):
    return hh[:, 0:A_WIDTH] * _sigmoid(hh[:, A_WIDTH:2 * A_WIDTH])


def _fill_row_shifts(s):
    rows = s.shape[1] - SUBLANES
    for j in range(1, SUBLANES):
        s[j, 0:rows, :] = s[0, pl.ds(j, rows), :]


def _rows_from(s, start, rows):
    j = start % SUBLANES
    return s[j, start - j:start - j + rows, :]


def _tril_mask():
    return lax.broadcasted_iota(jnp.int32, (B_CHUNK, B_CHUNK), 0) >= lax.broadcasted_iota(jnp.int32, (B_CHUNK, B_CHUNK), 1)


def _spatial_mix(q, ms_ref, sbt_ref, tm):
    mask = _tril_mask()
    ws = [jnp.where(mask, ms_ref[g], 0.0).astype(BF16) for g in range(B_GROUPS)]
    qb = q.astype(BF16)
    rows = []
    for c in range(tm // B_CHUNK):
        cols = [_dot(ws[g], qb[c * B_CHUNK:(c + 1) * B_CHUNK, g * 128:(g + 1) * 128], NN) + sbt_ref[:, g:g + 1]
                for g in range(B_GROUPS)]
        rows.append(jnp.concatenate(cols, axis=1))
    return jnp.concatenate(rows, axis=0)


def _mixer_mid_fwd(h, cw, cb, ag, ab, bg, bb, ms, sbt, name, tm=256, comm=None):
    t = h.shape[0]
    nt, hb = t // tm, tm // MIX_HALO
    o = MIX_HALO - A_KERNEL + 1

    def body(h_ref, hp_ref, cw_ref, cb_ref, ag_ref, ab_ref, bg_ref, bb_ref, ms_ref, sbt_ref, cat_ref, y_ref, sp):
        i = pl.program_id(0)
        sp[0, 0:MIX_HALO, :] = jnp.where(i > 0, _glu(hp_ref[:, 0:2 * A_WIDTH].astype(F32)), 0.0)
        sp[0, MIX_HALO:, :] = _glu(h_ref[:, 0:2 * A_WIDTH].astype(F32))
        _fill_row_shifts(sp)
        y = jnp.zeros((tm, A_WIDTH), F32) + cb_ref[...]
        for k in range(A_KERNEL):
            y = y + cw_ref[k:k + 1, :] * _rows_from(sp, o + k, tm)
        y_ref[...] = y.astype(BF16)
        nh, _ = _ln_stats(y)
        ln = nh * ag_ref[...] + ab_ref[...]
        cat_ref[:, 0:A_WIDTH] = (ln * _sigmoid(ln)).astype(BF16)
        u = _gelu(h_ref[:, 1024:1536].astype(F32))
        nb, _ = _ln_stats(_gelu(h_ref[:, 1536:2048].astype(F32)))
        mixed = _spatial_mix(nb * bg_ref[...] + bb_ref[...], ms_ref, sbt_ref, tm)
        cat_ref[:, A_WIDTH:] = (u * mixed).astype(BF16)

    vec = pl.BlockSpec((1, A_WIDTH), lambda i: (0, 0))
    outs, moved = _call(
        body, name=name, grid=(nt,),
        in_specs=[pl.BlockSpec((tm, 2048), lambda i: (i, 0)),
                  pl.BlockSpec((MIX_HALO, 2048), lambda i: (jnp.maximum(i * hb - 1, 0), 0)),
                  pl.BlockSpec((A_KERNEL, A_WIDTH), lambda i: (0, 0)), vec, vec, vec, vec, vec,
                  pl.BlockSpec((B_GROUPS, B_CHUNK, B_CHUNK), lambda i: (0, 0, 0)),
                  pl.BlockSpec((B_CHUNK, B_GROUPS), lambda i: (0, 0))],
        out_specs=[pl.BlockSpec((tm, D_MODEL), lambda i: (i, 0)), pl.BlockSpec((tm, A_WIDTH), lambda i: (i, 0))],
        out_shape=[jax.ShapeDtypeStruct((t, D_MODEL), BF16), jax.ShapeDtypeStruct((t, A_WIDTH), BF16)],
        scratch_shapes=[pltpu.VMEM((SUBLANES, tm + MIX_HALO, A_WIDTH), F32)],
        sem=("parallel",), args=(h, h, cw, cb, ag, ab, bg, bb, ms, sbt), comm=comm)
    return outs if comm is None else (outs, moved)


def _mixer_mid_bwd(h, y, dcat, cw, ag, ab, bg, bb, ms, mst, sbt, name, tm=256, comm=None):
    t = h.shape[0]
    nt, hb = t // tm, tm // MIX_HALO
    r = tm + MIX_HALO
    nchunk = tm // B_CHUNK

    def body(h_ref, y_ref, yn_ref, dc_ref, dcn_ref, cw_ref, ag_ref, ab_ref, bg_ref, bb_ref, ms_ref, mst_ref, sbt_ref,
             dh_ref, dcw_ref, dcb_ref, dag_ref, dab_ref, dbg_ref, dbb_ref, dms_ref, dsb_ref, sdy, sbacc):
        i = pl.program_id(0)

        @pl.when(i == 0)
        def _():
            for ref in (dcw_ref, dcb_ref, dag_ref, dab_ref, dbg_ref, dbb_ref, dms_ref, dsb_ref, sbacc):
                ref[...] = jnp.zeros_like(ref)

        nh, rs = _ln_stats(jnp.concatenate([y_ref[...].astype(F32), yn_ref[...].astype(F32)], axis=0))
        ln = nh * ag_ref[...] + ab_ref[...]
        sg = _sigmoid(ln)
        dao = jnp.concatenate([dc_ref[:, 0:A_WIDTH].astype(F32),
                               jnp.where(i < nt - 1, dcn_ref[:, 0:A_WIDTH].astype(F32), 0.0)], axis=0)
        dln = dao * (sg * (1.0 + ln * (1.0 - sg)))
        dag_ref[...] += _colsum(dln[0:tm] * nh[0:tm])
        dab_ref[...] += _colsum(dln[0:tm])
        sdy[0] = _ln_bwd_rows(dln * ag_ref[...], nh, rs)
        _fill_row_shifts(sdy)
        dcb_ref[...] += _colsum(sdy[0, 0:tm, :])
        av = h_ref[:, 0:A_WIDTH].astype(F32)
        s = _sigmoid(h_ref[:, A_WIDTH:2 * A_WIDTH].astype(F32))
        p_own = av * s
        dp = jnp.zeros((tm, A_WIDTH), F32)
        for k in range(A_KERNEL):
            later = _rows_from(sdy, A_KERNEL - 1 - k, tm)
            dcw_ref[k:k + 1, :] += _colsum(later * p_own)
            dp = dp + cw_ref[k:k + 1, :] * later
        dh_ref[:, 0:A_WIDTH] = (dp * s).astype(BF16)
        dh_ref[:, A_WIDTH:2 * A_WIDTH] = (dp * av * s * (1.0 - s)).astype(BF16)

        u, dgu = _gelu_and_grad(h_ref[:, 1024:1536].astype(F32))
        w, dgw = _gelu_and_grad(h_ref[:, 1536:2048].astype(F32))
        nb, rb = _ln_stats(w)
        q = nb * bg_ref[...] + bb_ref[...]
        mixed = _spatial_mix(q, ms_ref, sbt_ref, tm)
        dbo = dc_ref[:, A_WIDTH:].astype(F32)
        dh_ref[:, 1024:1536] = (dbo * mixed * dgu).astype(BF16)
        dmx = dbo * u
        mask = _tril_mask()
        wst = [jnp.where(mask.T, mst_ref[g], 0.0).astype(BF16) for g in range(B_GROUPS)]
        qb = q.astype(BF16)
        dmb = dmx.astype(BF16)
        rows = []
        for c in range(nchunk):
            cols = []
            for g in range(B_GROUPS):
                rs_, cs_ = slice(c * B_CHUNK, (c + 1) * B_CHUNK), slice(g * 128, (g + 1) * 128)
                sbacc[g] += dmx[rs_, cs_]
                dms_ref[g] += _dot(dmb[rs_, cs_], qb[rs_, cs_], NT)
                cols.append(_dot(wst[g], dmb[rs_, cs_], NN))
            rows.append(jnp.concatenate(cols, axis=1))
        dq = jnp.concatenate(rows, axis=0)
        dbg_ref[...] += _colsum(dq * nb)
        dbb_ref[...] += _colsum(dq)
        dh_ref[:, 1536:2048] = (_ln_bwd_rows(dq * bg_ref[...], nb, rb) * dgw).astype(BF16)

        @pl.when(i == nt - 1)
        def _():
            for g in range(B_GROUPS):
                dms_ref[g] = jnp.where(mask, dms_ref[g], 0.0)
                dsb_ref[g] = jnp.sum(sbacc[g], axis=1, keepdims=True)

    last_blk = t // MIX_HALO - 1
    vec = pl.BlockSpec((1, A_WIDTH), lambda i: (0, 0))
    mat = pl.BlockSpec((B_GROUPS, B_CHUNK, B_CHUNK), lambda i: (0, 0, 0))
    taps = pl.BlockSpec((A_KERNEL, A_WIDTH), lambda i: (0, 0))

    def halo(width):
        return pl.BlockSpec((MIX_HALO, width), lambda i: (jnp.minimum((i + 1) * hb, last_blk), 0))

    vshape = jax.ShapeDtypeStruct((1, A_WIDTH), F32)
    outs, moved = _call(
        body, name=name, grid=(nt,),
        in_specs=[pl.BlockSpec((tm, 2048), lambda i: (i, 0)), pl.BlockSpec((tm, A_WIDTH), lambda i: (i, 0)), halo(A_WIDTH),
                  pl.BlockSpec((tm, D_MODEL), lambda i: (i, 0)), halo(D_MODEL),
                  taps, vec, vec, vec, vec, mat, mat, pl.BlockSpec((B_CHUNK, B_GROUPS), lambda i: (0, 0))],
        out_specs=[pl.BlockSpec((tm, 2048), lambda i: (i, 0)), taps, vec, vec, vec, vec, vec, mat,
                   pl.BlockSpec((B_GROUPS, B_CHUNK, 1), lambda i: (0, 0, 0))],
        out_shape=[jax.ShapeDtypeStruct((t, 2048), BF16), jax.ShapeDtypeStruct((A_KERNEL, A_WIDTH), F32),
                   vshape, vshape, vshape, vshape, vshape,
                   jax.ShapeDtypeStruct((B_GROUPS, B_CHUNK, B_CHUNK), F32), jax.ShapeDtypeStruct((B_GROUPS, B_CHUNK, 1), F32)],
        scratch_shapes=[pltpu.VMEM((SUBLANES, r, A_WIDTH), F32), pltpu.VMEM((B_GROUPS, B_CHUNK, B_CHUNK), F32)],
        sem=("arbitrary",), args=(h, y, y, dcat, dcat, cw, ag, ab, bg, bb, ms, mst, sbt), comm=comm)
    return outs if comm is None else (outs, moved)


Q_WIDTH = N_Q_HEADS * HEAD_DIM
KV_WIDTH = 2 * N_KV_HEADS * HEAD_DIM
PAIRS_PER_KV = N_Q_HEADS // N_KV_HEADS // 2
ATT_SCALE = 1.0 / math.sqrt(HEAD_DIM)


def _dup_heads(pair_cols, kv_head):
    lane = lax.broadcasted_iota(jnp.int32, pair_cols.shape, 1)
    rolled = pltpu.roll(pair_cols, HEAD_DIM, 1)
    first = lane < HEAD_DIM
    return jnp.where(first, pair_cols, rolled) if kv_head == 0 else jnp.where(first, rolled, pair_cols)


HEADS_PER_KV = N_Q_HEADS // N_KV_HEADS


def _stack_heads(ref, kh):
    lane = lax.broadcasted_iota(jnp.int32, (ATT_BLOCK, 128), 1)
    rows = []
    for pr in range(PAIRS_PER_KV):
        c0 = (kh * PAIRS_PER_KV + pr) * 128
        pair = ref[:, c0:c0 + 128]
        rows += [jnp.where(lane < HEAD_DIM, pair, jnp.zeros_like(pair)), jnp.where(lane < HEAD_DIM, jnp.zeros_like(pair), pair)]
    return jnp.concatenate(rows, axis=0)


def _unstack_heads(stacked, kh, write):
    lane = lax.broadcasted_iota(jnp.int32, (ATT_BLOCK, 128), 1)
    for pr in range(PAIRS_PER_KV):
        first = stacked[(2 * pr) * ATT_BLOCK:(2 * pr + 1) * ATT_BLOCK]
        second = stacked[(2 * pr + 1) * ATT_BLOCK:(2 * pr + 2) * ATT_BLOCK]
        write((kh * PAIRS_PER_KV + pr) * 128, jnp.where(lane < HEAD_DIM, first, second))


def _sink_row(sink_ref, kh):
    return jnp.concatenate([jnp.full((1, ATT_BLOCK), sink_ref[0, kh * HEADS_PER_KV + h], F32) for h in range(HEADS_PER_KV)], axis=1)


def _att_window_bias():
    sj = lax.broadcasted_iota(jnp.int32, (2 * ATT_BLOCK, HEADS_PER_KV * ATT_BLOCK), 0)
    qi = lax.broadcasted_iota(jnp.int32, (2 * ATT_BLOCK, HEADS_PER_KV * ATT_BLOCK), 1) & (ATT_BLOCK - 1)
    diff = qi + ATT_BLOCK - sj
    return jnp.where((diff >= 0) & (diff < ATT_BLOCK), 0.0, -jnp.inf)


def _att_probs_t(q_all, k2, bias_ref, n, sink):
    st = _dot(k2, q_all, NT) * ATT_SCALE + bias_ref[...]
    st = jnp.concatenate([jnp.where(n > 0, st[0:ATT_BLOCK], -jnp.inf), st[ATT_BLOCK:]], axis=0)
    m = jnp.maximum(jnp.max(st, axis=0, keepdims=True), sink)
    e = jnp.exp(st - m)
    es = jnp.exp(sink - m)
    inv = 1.0 / (jnp.sum(e, axis=0, keepdims=True) + es)
    return e * inv, es * inv


def _attn_fwd(qkv, sinks, name, comm=None):
    t = qkv.shape[0]
    nb = t // ATT_BLOCK
    kvb = Q_WIDTH // KV_WIDTH

    def body(sink_ref, q_ref, kv_ref, kvp_ref, o_ref, bias):
        n = pl.program_id(0)

        @pl.when(n == 0)
        def _():
            bias[...] = _att_window_bias()

        kv = jnp.concatenate([kvp_ref[...], kv_ref[...]], axis=0).astype(F32)

        def write(c0, pair):
            o_ref[:, c0:c0 + 128] = pair.astype(BF16)

        for kh in range(N_KV_HEADS):
            k2 = _dup_heads(kv[:, 0:128], kh).astype(BF16)
            v2 = _dup_heads(kv[:, 128:256], kh).astype(BF16)
            pt, _ = _att_probs_t(_stack_heads(q_ref, kh), k2, bias, n, _sink_row(sink_ref, kh))
            _unstack_heads(_dot(v2, pt, TN).T, kh, write)

    (out,), moved = _call(
        body, name=name, grid=(nb,),
        in_specs=[pl.BlockSpec(memory_space=pltpu.SMEM),
                  pl.BlockSpec((ATT_BLOCK, Q_WIDTH), lambda n: (n, 0)),
                  pl.BlockSpec((ATT_BLOCK, KV_WIDTH), lambda n: (n, kvb)),
                  pl.BlockSpec((ATT_BLOCK, KV_WIDTH), lambda n: (jnp.maximum(n - 1, 0), kvb))],
        out_specs=[pl.BlockSpec((ATT_BLOCK, Q_WIDTH), lambda n: (n, 0))],
        out_shape=[jax.ShapeDtypeStruct((t, Q_WIDTH), BF16)],
        scratch_shapes=[pltpu.VMEM((2 * ATT_BLOCK, HEADS_PER_KV * ATT_BLOCK), F32)],
        sem=("arbitrary",), args=(sinks, qkv, qkv, qkv), comm=comm)
    return out if comm is None else (out, moved)


def _attn_bwd(qkv, d_o, sinks, name, comm=None):
    t = qkv.shape[0]
    nb = t // ATT_BLOCK
    kvb = Q_WIDTH // KV_WIDTH

    def body(sink_ref, q_ref, kv_ref, kvp_ref, do_ref, dq_ref, dkv_ref, dbq_ref, dbkv_ref, dsink_ref, carry, bias):
        n = pl.program_id(0)

        @pl.when(n == 0)
        def _():
            for ref in (dbq_ref, dbkv_ref, dsink_ref, carry):
                ref[...] = jnp.zeros_like(ref)
            dkv_ref[...] = jnp.zeros_like(dkv_ref)
            bias[...] = _att_window_bias()

        @pl.when(n < nb)
        def _():
            kv = jnp.concatenate([kvp_ref[...], kv_ref[...]], axis=0).astype(F32)
            lane2 = lax.broadcasted_iota(jnp.int32, (2 * ATT_BLOCK, 128), 1)
            sink_lane = lax.broadcasted_iota(jnp.int32, (1, 128), 1)
            dsink = jnp.zeros((1, 128), F32)
            dk_parts, dv_parts = [], []

            def write(c0, pair):
                dbq_ref[:, c0:c0 + 128] += _colsum(pair)
                dq_ref[:, c0:c0 + 128] = pair.astype(BF16)

            for kh in range(N_KV_HEADS):
                k2 = _dup_heads(kv[:, 0:128], kh).astype(BF16)
                v2 = _dup_heads(kv[:, 128:256], kh).astype(BF16)
                q_all = _stack_heads(q_ref, kh)
                do_all = _stack_heads(do_ref, kh)
                pt, ps = _att_probs_t(q_all, k2, bias, n, _sink_row(sink_ref, kh))
                dpt = _dot(v2, do_all, NT)
                delta = jnp.sum(pt * dpt, axis=0, keepdims=True)
                dst = pt * (dpt - delta) * ATT_SCALE
                psd = ps * delta
                for h in range(HEADS_PER_KV):
                    dsink = dsink + jnp.where(sink_lane == kh * HEADS_PER_KV + h,
                                              -jnp.sum(psd[:, h * ATT_BLOCK:(h + 1) * ATT_BLOCK]), 0.0)
                _unstack_heads(_dot(k2, dst, TN).T, kh, write)
                dk_acc = _dot(dst, q_all, NN)
                dv_acc = _dot(pt, do_all, NN)
                dk_parts.append(dk_acc + pltpu.roll(dk_acc, HEAD_DIM, 1))
                dv_parts.append(dv_acc + pltpu.roll(dv_acc, HEAD_DIM, 1))
            dk = jnp.where(lane2 < HEAD_DIM, dk_parts[0], dk_parts[1])
            dv = jnp.where(lane2 < HEAD_DIM, dv_parts[0], dv_parts[1])
            dkv_new = jnp.concatenate([dk, dv], axis=1)
            done = carry[...] + dkv_new[0:ATT_BLOCK]

            @pl.when(n > 0)
            def _():
                dkv_ref[...] = done.astype(BF16)
                dbkv_ref[...] += _colsum(done)

            carry[...] = dkv_new[ATT_BLOCK:]
            dsink_ref[...] += dsink

        @pl.when(n == nb)
        def _():
            dkv_ref[...] = carry[...].astype(BF16)
            dbkv_ref[...] += _colsum(carry[...])

    def clamp(n):
        return jnp.minimum(n, nb - 1)

    outs, moved = _call(
        body, name=name, grid=(nb + 1,),
        in_specs=[pl.BlockSpec(memory_space=pltpu.SMEM),
                  pl.BlockSpec((ATT_BLOCK, Q_WIDTH), lambda n: (clamp(n), 0)),
                  pl.BlockSpec((ATT_BLOCK, KV_WIDTH), lambda n: (clamp(n), kvb)),
                  pl.BlockSpec((ATT_BLOCK, KV_WIDTH), lambda n: (jnp.maximum(clamp(n) - 1, 0), kvb)),
                  pl.BlockSpec((ATT_BLOCK, Q_WIDTH), lambda n: (clamp(n), 0))],
        out_specs=[pl.BlockSpec((ATT_BLOCK, Q_WIDTH), lambda n: (clamp(n), 0)),
                   pl.BlockSpec((ATT_BLOCK, KV_WIDTH), lambda n: (jnp.maximum(n - 1, 0), 0)),
                   pl.BlockSpec((1, Q_WIDTH), lambda n: (0, 0)),
                   pl.BlockSpec((1, KV_WIDTH), lambda n: (0, 0)),
                   pl.BlockSpec((1, 128), lambda n: (0, 0))],
        out_shape=[jax.ShapeDtypeStruct((t, Q_WIDTH), BF16), jax.ShapeDtypeStruct((t, KV_WIDTH), BF16),
                   jax.ShapeDtypeStruct((1, Q_WIDTH), F32), jax.ShapeDtypeStruct((1, KV_WIDTH), F32),
                   jax.ShapeDtypeStruct((1, 128), F32)],
        scratch_shapes=[pltpu.VMEM((ATT_BLOCK, KV_WIDTH), F32), pltpu.VMEM((2 * ATT_BLOCK, HEADS_PER_KV * ATT_BLOCK), F32)],
        sem=("arbitrary",), args=(sinks, qkv, qkv, qkv, d_o), comm=comm)
    return outs if comm is None else (outs, moved)


def _adamw_math(g, w, m, v):
    m = ADAM_B1 * m + (1.0 - ADAM_B1) * g
    v = ADAM_B2 * v + (1.0 - ADAM_B2) * (g * g)
    m_hat = m / (1.0 - ADAM_B1 ** ADAM_STEP)
    v_hat = v / (1.0 - ADAM_B2 ** ADAM_STEP)
    delta = -ADAM_LR * (m_hat / (jnp.sqrt(v_hat) + ADAM_EPS) + ADAM_WD * w)
    return delta, m, v


def _sum_partials(p_ref):
    g = p_ref[0].astype(F32)
    for s in range(1, N_DEV):
        g = g + p_ref[s].astype(F32)
    return g


def _adamw_big(parts, w, m, v, name, tr):
    r, c = w.shape
    parts = [p if isinstance(p, tuple) else (p, 0, p.shape[1]) for p in parts]
    tiles = [rows // tr for _, _, rows in parts]
    starts = [sum(tiles[:l]) for l in range(len(parts))]
    assert all(lo % tr == 0 and rows % tr == 0 for _, lo, rows in parts) and sum(tiles) * tr == r

    def body(*refs):
        p_refs, (w_ref, m_ref, v_ref, g_out, d_out, m_out, v_out) = refs[:len(parts)], refs[len(parts):]
        i = pl.program_id(0)
        for l, p_ref in enumerate(p_refs):
            @pl.when((i >= starts[l]) & (i < starts[l] + tiles[l]))
            def _():
                g = _sum_partials(p_ref)
                g_out[...] = g
                d_out[...], m_out[...], v_out[...] = _adamw_math(g, w_ref[...], m_ref[...], v_ref[...])

    def part_spec(l):
        return pl.BlockSpec((N_DEV, tr, c), lambda i: (0, jnp.clip(i - starts[l], 0, tiles[l] - 1) + parts[l][1] // tr, 0))

    tile = pl.BlockSpec((tr, c), lambda i: (i, 0))
    shape = jax.ShapeDtypeStruct((r, c), F32)
    return pl.pallas_call(
        body, name=name, grid=(r // tr,),
        in_specs=[part_spec(l) for l in range(len(parts))] + [tile, tile, tile],
        out_specs=[tile] * 4, out_shape=[shape] * 4,
        compiler_params=_params("parallel"),
    )(*[p[0] for p in parts], w, m, v)


def _adamw_small(parts, ws, ms, vs, name):
    n = len(ws)

    def body(*refs):
        ins, outs = refs[:4 * n], refs[4 * n:]
        for a in range(n):
            g = _sum_partials(ins[a])
            outs[4 * a][...] = g
            outs[4 * a + 1][...], outs[4 * a + 2][...], outs[4 * a + 3][...] = _adamw_math(
                g, ins[n + a][...], ins[2 * n + a][...], ins[3 * n + a][...])

    out_shape = []
    for w in ws:
        out_shape += [jax.ShapeDtypeStruct(w.shape, F32)] * 4
    return pl.pallas_call(body, name=name, out_shape=out_shape, compiler_params=_params())(*parts, *ws, *ms, *vs)


PACK_LANES = 128
PACK_ROWS = 8


def _pack(arrs):
    flat = jnp.concatenate([a.reshape(-1).astype(F32) for a in arrs])
    unit = PACK_LANES * PACK_ROWS
    total = -(-flat.shape[0] // unit) * unit
    return jnp.pad(flat, (0, total - flat.shape[0])).reshape(-1, PACK_LANES)


def _unpack(buf, shapes):
    flat = buf.reshape(N_DEV, -1)
    out, pos = [], 0
    for s in shapes:
        size = math.prod(s)
        out.append(flat[:, pos:pos + size].reshape((N_DEV,) + tuple(s)))
        pos += size
    return out


def _interleave(g):
    return jnp.transpose(g, (1, 0, 2)).reshape(g.shape[1], -1)


def _ffn_backward(dz, dzb, x_in, z_in, g_in, h, cg, cv, u, w_up_t, cw, w_down, tag, exchange=(), exchange_late=(), own_rows=0):
    du = _matmul(dzb, w_down, "nt", BF16, f"ffn{tag}_du", 1024, 1408, 1024)
    d_w_down = _matmul(u, dzb, "tn", BF16, f"ffn{tag}_dwdown", 1408, 1024, 2048)
    (dhg, dhv, dcwg, dcwv, dcbg, dcbv), moved = _ffn_mid_bwd(
        h, cg, cv, du, cw, f"ffn{tag}_mid_bwd", comm=_Comm(exchange=[d_w_down.reshape(N_DEV, -1, D_MODEL), *exchange]))
    d_w_up_t = _matmul_tn_pair(dhg, dhv, x_in, BF16, f"ffn{tag}_dwup", 1408, 1024, 1024,
                               comm=_Comm(exchange=exchange_late) if exchange_late else None)
    if exchange_late:
        d_w_up_t, late = d_w_up_t
        moved = moved + late
    d_up_blocks = d_w_up_t.reshape(N_DEV, -1, D_MODEL)
    outs = _matmul_ln_bwd([(dhg, 0), (dhv, D_FF)], w_up_t, z_in, g_in, dz, f"ffn{tag}_dx_ln_bwd", 256,
                          comm=_Comm(exchange=[(d_up_blocks, 0, own_rows)]) if own_rows else None)
    (dz_in, dzb_in, dg_in, db_in), own = outs if own_rows else (outs, [])
    moved = moved + own
    return (dz_in, dzb_in, dg_in, db_in, d_up_blocks,
            jnp.concatenate([dcwg, dcwv], axis=1), jnp.concatenate([dcbg, dcbv], axis=1), moved)


def kernel(x, ab_w_in, a_conv_w, a_conv_b, a_norm_g, a_norm_b, b_norm_g, b_norm_b, b_spatial_w, b_spatial_b, ab_w_out, c_w_qkv, c_b_qkv, c_sinks, c_w_o, ffn_w_up, ffn_conv_w, ffn_conv_b, ffn_w_down, ln_g, ln_b, loss_target, m_ab_w_in, m_a_conv_w, m_a_conv_b, m_a_norm_g, m_a_norm_b, m_b_norm_g, m_b_norm_b, m_b_spatial_w, m_b_spatial_b, m_ab_w_out, m_c_w_qkv, m_c_b_qkv, m_c_sinks, m_c_w_o, m_ffn_w_up, m_ffn_conv_w, m_ffn_conv_b, m_ffn_w_down, m_ln_g, m_ln_b, v_ab_w_in, v_a_conv_w, v_a_conv_b, v_a_norm_g, v_a_norm_b, v_b_norm_g, v_b_norm_b, v_b_spatial_w, v_b_spatial_b, v_ab_w_out, v_c_w_qkv, v_c_b_qkv, v_c_sinks, v_c_w_o, v_ffn_w_up, v_ffn_conv_w, v_ffn_conv_b, v_ffn_w_down, v_ln_g, v_ln_b):
    me = 4 * lax.axis_index("x") + 2 * lax.axis_index("y") + lax.axis_index("c")
    xt = x[0]
    t = xt.shape[0]

    small_shard_shapes = [a_conv_w.shape, c_b_qkv.shape, ffn_conv_w.shape, ln_g.shape, ln_b.shape]
    up_shard = [jnp.swapaxes(ffn_w_up[l], 0, 1).astype(BF16) for l in range(2)]
    qkv_shard = jnp.swapaxes(c_w_qkv[0], 0, 1).astype(BF16)
    down_shard = [ffn_w_down[l].astype(BF16) for l in range(2)]
    g_win, g_small = _comm_only(
        _Comm(gather=[jnp.swapaxes(ab_w_in[0], 0, 1).astype(BF16), _pack([a_conv_w, c_b_qkv, ffn_conv_w, ln_g, ln_b])]),
        "gather_first")
    w_in = g_win.reshape(-1, D_MODEL)
    g_acw, g_bqkv, g_fcw, g_lng, g_lnb = _unpack(g_small, small_shard_shapes)
    acw = _interleave(g_acw[:, 0])
    bqkv = g_bqkv[:, 0].reshape(1, -1)
    fcw = [_interleave(g_fcw[:, l]) for l in range(2)]
    lng = jnp.transpose(g_lng, (1, 2, 0, 3)).reshape(2, 2, 1, D_MODEL)
    lnb = jnp.transpose(g_lnb, (1, 2, 0, 3)).reshape(2, 2, 1, D_MODEL)
    fcb = [ffn_conv_b[l:l + 1] for l in range(2)]
    ms = b_spatial_w[0]
    mst = jnp.swapaxes(ms, 1, 2)
    sbt = b_spatial_b[0].T

    q_up = up_shard[0].shape[0] // 4
    h0, (g_wout, g_wup0) = _matmul(xt, w_in, "nt", BF16, "mix_in", 1024, 1024, 1024,
                                   comm=_Comm(gather=[ab_w_out[0].astype(BF16), (up_shard[0], 0, q_up, None)]))
    w_out = g_wout.reshape(D_MODEL, D_MODEL)
    (cat, y0), (g_wup0,) = _mixer_mid_fwd(h0, acw, a_conv_b, a_norm_g, a_norm_b, b_norm_g, b_norm_b, ms, sbt, "mix_mid_fwd",
                                          comm=_Comm(gather=[(up_shard[0], q_up, 2 * q_up, g_wup0)]))
    (z1, x1), (g_wup0,) = _matmul_res_ln(cat, w_out, xt, lng[0, 0], lnb[0, 0], "mix_out_ln", 512,
                                         comm=_Comm(gather=[(up_shard[0], 3 * q_up, q_up, g_wup0)]))
    w_up0 = g_wup0.reshape(2 * D_FF, D_MODEL)
    hf0, (g_wdown0, g_wqkv) = _matmul(x1, w_up0, "nt", BF16, "ffn0_up", 1024, 1408, 1024,
                                      comm=_Comm(gather=[down_shard[0], qkv_shard]))
    w_down0 = g_wdown0.reshape(D_FF, D_MODEL)
    w_qkv = g_wqkv.reshape(Q_WIDTH + KV_WIDTH, D_MODEL)
    (u0, cg0, cv0), (g_wup1,) = _ffn_mid_fwd(hf0, fcw[0], fcb[0], "ffn0_mid_fwd",
                                             comm=_Comm(gather=[(up_shard[1], 0, 3 * q_up, None)]))
    (z2, x2), (g_wo, g_wup1) = _matmul_res_ln(
        u0, w_down0, z1, lng[0, 1], lnb[0, 1], "ffn0_down_ln", 512, prev=(lng[0, 0], lnb[0, 0]),
        comm=_Comm(gather=[c_w_o[0].astype(BF16), (up_shard[1], 3 * q_up, q_up, g_wup1)]))
    w_o = g_wo.reshape(D_MODEL, D_MODEL)
    w_up1 = g_wup1.reshape(2 * D_FF, D_MODEL)
    qkv = _matmul(x2, w_qkv, "nt", BF16, "att_qkv", 1024, 1280, 1024, bias=bqkv)
    att, (g_wdown1,) = _attn_fwd(qkv, c_sinks, "att_fwd", comm=_Comm(gather=[down_shard[1]]))
    w_down1 = g_wdown1.reshape(D_FF, D_MODEL)
    z3, x3 = _matmul_res_ln(att, w_o, z2, lng[1, 0], lnb[1, 0], "att_out_ln", 512, prev=(lng[0, 1], lnb[0, 1]))
    hf1 = _matmul(x3, w_up1, "nt", BF16, "ffn1_up", 1024, 1408, 1024)
    u1, cg1, cv1 = _ffn_mid_fwd(hf1, fcw[1], fcb[1], "ffn1_mid_fwd")

    dz4, dz4b, dg11, db11, loss_terms = _matmul_res_ln_loss(u1, w_down1, z3, lng[1, 1], lnb[1, 1], loss_target[0],
                                                      "ffn1_down_ln_loss", 512, prev=(lng[1, 0], lnb[1, 0]))
    dz3, dz3b, dg10, db10, d_wup1, d_fcw1, d_fcb1, (p_wdown1,) = _ffn_backward(
        dz4, dz4b, x3, z3, lng[1, 0], hf1, cg1, cv1, u1, w_up1, fcw[1], w_down1, 1)
    d_att = _matmul(dz3b, w_o, "nt", BF16, "att_dout", 1024, 1024, 1024)
    d_wo = _matmul(att, dz3b, "tn", BF16, "att_dwo", 1024, 1024, 512)
    rows_up = d_wup1.shape[1]
    first = 3 * rows_up // 4
    (dq, dkv, dbq, dbkv, dsinks), (p_wup1a,) = _attn_bwd(qkv, d_att, c_sinks, "att_bwd",
                                                        comm=_Comm(exchange=[(d_wup1, 0, first)]))
    d_wqkv = jnp.concatenate([_matmul(dq, x2, "tn", BF16, "att_dwq", 1024, 1024, 1024),
                              _matmul(dkv, x2, "tn", BF16, "att_dwkv", KV_WIDTH, 1024, 1024)], axis=0)
    dz2, dz2b, dg01, db01 = _matmul_ln_bwd([(dq, 0), (dkv, Q_WIDTH)], w_qkv, z2, lng[0, 1], dz3, "att_dx_ln_bwd", 512)
    early = rows_up // 4
    dz1, dz1b, dg00, db00, d_wup0, d_fcw0, d_fcb0, (p_wdown0, p_wup1b, p_wqkv, p_wo, p_wup0a) = _ffn_backward(
        dz2, dz2b, x1, z1, lng[0, 0], hf0, cg0, cv0, u0, w_up0, fcw[0], w_down0, 0, exchange=[(d_wup1, first, rows_up - first)],
        exchange_late=[d_wqkv.reshape(N_DEV, -1, D_MODEL), d_wo.reshape(N_DEV, -1, D_MODEL)], own_rows=early)
    dcat = _matmul(dz1b, w_out, "nt", BF16, "mix_dcat", 1024, 1024, 1024)
    d_wout = _matmul(cat, dz1b, "tn", BF16, "mix_dwout", 1024, 1024, 512)
    (dh0, d_acw, d_acb, d_ang, d_anb, d_bng, d_bnb, d_ms, d_sb), (p_wup0b, p_wout) = _mixer_mid_bwd(
        h0, y0, dcat, acw, a_norm_g, a_norm_b, b_norm_g, b_norm_b, ms, mst, sbt, "mix_mid_bwd",
        comm=_Comm(exchange=[(d_wup0, early, rows_up - early), d_wout.reshape(N_DEV, -1, D_MODEL)]))
    d_bqkv = jnp.concatenate([dbq, dbkv], axis=1)
    d_lng = jnp.stack([jnp.stack([dg00, dg01]), jnp.stack([dg10, dg11])])
    d_lnb = jnp.stack([jnp.stack([db00, db01]), jnp.stack([db10, db11])])
    small_full = [d_acb, d_ang, d_anb, d_bng, d_bnb, d_ms, d_sb, dsinks[:, :N_Q_HEADS], jnp.concatenate([d_fcb0, d_fcb1], axis=0),
                  d_acw, d_bqkv, jnp.stack([d_fcw0, d_fcw1]), d_lng, d_lnb, loss_terms]
    d_win, (g_small_grads,) = _matmul(dh0, xt, "tn", BF16, "mix_dwin", 1024, 1024, 512, comm=_Comm(gather=[_pack(small_full)]))
    grad_x, (p_win,) = _matmul(dh0, w_in, "nn", F32, "mix_dx", 1024, 1024, 1024, res=dz1, res_scale=ALPHA,
                               comm=_Comm(exchange=[d_win.reshape(N_DEV, -1, D_MODEL)]))


    big = {}
    for nm, p, w, m, v, tr, transposed in [
            ("ab_w_in", [p_win], ab_w_in, m_ab_w_in, v_ab_w_in, 256, True),
            ("ab_w_out", [p_wout], ab_w_out, m_ab_w_out, v_ab_w_out, 128, False),
            ("c_w_qkv", [p_wqkv], c_w_qkv, m_c_w_qkv, v_c_w_qkv, 160, True), ("c_w_o", [p_wo], c_w_o, m_c_w_o, v_c_w_o, 128, False),
            ("ffn_w_up", [(p_wup0a, 0, early), (p_wup0b, early, rows_up - early), (p_wup1a, 0, first), (p_wup1b, first, rows_up - first)], ffn_w_up, m_ffn_w_up, v_ffn_w_up, 176, True),
            ("ffn_w_down", [p_wdown0, p_wdown1], ffn_w_down, m_ffn_w_down, v_ffn_w_down, 176, False)]:
        def two_d(a):
            a = jnp.swapaxes(a, 1, 2) if transposed else a
            return a.reshape(-1, a.shape[-1])

        def back(o):
            return jnp.swapaxes(o.reshape(w.shape[0], w.shape[2], w.shape[1]), 1, 2) if transposed else o.reshape(w.shape)

        outs = _adamw_big(p, two_d(w), two_d(m), two_d(v), "adamw_" + nm, tr)
        big[nm] = [back(o) for o in outs]

    *gs, loss_parts = _unpack(g_small_grads, [a.shape for a in small_full])
    loss = 0.5 / D_MODEL * jnp.sum(loss_parts)

    def my_shard(g, width):
        g = g.reshape(g.shape[:-1] + (N_DEV, width))
        return lax.dynamic_index_in_dim(g, me, axis=g.ndim - 2, keepdims=False)

    small_names = ["a_conv_b", "a_norm_g", "a_norm_b", "b_norm_g", "b_norm_b", "b_spatial_w", "b_spatial_b", "c_sinks", "ffn_conv_b",
                   "a_conv_w", "c_b_qkv", "ffn_conv_w", "ln_g", "ln_b"]
    small_w = [a_conv_b, a_norm_g, a_norm_b, b_norm_g, b_norm_b, b_spatial_w, b_spatial_b, c_sinks, ffn_conv_b,
               a_conv_w, c_b_qkv, ffn_conv_w, ln_g, ln_b]
    small_m = [m_a_conv_b, m_a_norm_g, m_a_norm_b, m_b_norm_g, m_b_norm_b, m_b_spatial_w, m_b_spatial_b, m_c_sinks, m_ffn_conv_b,
               m_a_conv_w, m_c_b_qkv, m_ffn_conv_w, m_ln_g, m_ln_b]
    small_v = [v_a_conv_b, v_a_norm_g, v_a_norm_b, v_b_norm_g, v_b_norm_b, v_b_spatial_w, v_b_spatial_b, v_c_sinks, v_ffn_conv_b,
               v_a_conv_w, v_c_b_qkv, v_ffn_conv_w, v_ln_g, v_ln_b]
    gs[9:] = [my_shard(g, w.shape[-1]) for g, w in zip(gs[9:], small_w[9:])]
    two_d = [(-1, w.shape[-1]) for w in small_w]
    outs = _adamw_small([g.reshape((N_DEV,) + w.reshape(s).shape) for g, w, s in zip(gs, small_w, two_d)],
                        [w.reshape(s) for w, s in zip(small_w, two_d)], [m.reshape(s) for m, s in zip(small_m, two_d)],
                        [v.reshape(s) for v, s in zip(small_v, two_d)], "adamw_small")
    small = {nm: [o.reshape(w.shape) for o in outs[4 * a:4 * a + 4]] for a, (nm, w) in enumerate(zip(small_names, small_w))}

    res = {**big, **small}
    order = ["ab_w_in", "a_conv_w", "a_conv_b", "a_norm_g", "a_norm_b", "b_norm_g", "b_norm_b", "b_spatial_w", "b_spatial_b", "ab_w_out",
             "c_w_qkv", "c_b_qkv", "c_sinks", "c_w_o", "ffn_w_up", "ffn_conv_w", "ffn_conv_b", "ffn_w_down", "ln_g", "ln_b"]
    return (loss, grad_x[None], *[res[nm][0] for nm in order], *[res[nm][1] for nm in order],
            *[res[nm][2] for nm in order], *[res[nm][3] for nm in order])
```

```python
import functools
import math

import jax
import jax.numpy as jnp
from jax import lax
from jax.experimental import pallas as pl
from jax.experimental.pallas import tpu as pltpu

F32 = jnp.float32
BF16 = jnp.bfloat16

N_DEV = 8
D_MODEL = 1024
A_WIDTH = 512
A_KERNEL = 31
B_GROUPS = 4
B_CHUNK = 128
HEAD_DIM = 64
N_Q_HEADS = 16
N_KV_HEADS = 2
ATT_BLOCK = 128
D_FF = 2816
FFN_KERNEL = 3
ALPHA = (2.0 * 2) ** 0.25
LN_EPS = 1e-5
GELU_K = math.sqrt(2.0 / math.pi)
GELU_C = 0.044715
ADAM_LR = 0.001
ADAM_B1 = 0.9
ADAM_B2 = 0.999
ADAM_EPS = 1e-08
ADAM_WD = 0.01
ADAM_STEP = 10
VMEM_LIMIT = 56 * 1024 * 1024
MESH_ID = pl.DeviceIdType.MESH


def _params(*sem):
    return pltpu.CompilerParams(dimension_semantics=sem, vmem_limit_bytes=VMEM_LIMIT)


def _gelu(x):
    t = jnp.tanh(GELU_K * x * (1.0 + GELU_C * x * x))
    return 0.5 * x * (1.0 + t)


def _gelu_and_grad(x):
    x2 = x * x
    t = jnp.tanh(GELU_K * x * (1.0 + GELU_C * x2))
    g = 0.5 * x * (1.0 + t)
    dg = 0.5 * (1.0 + t) + 0.5 * x * (1.0 - t * t) * (GELU_K * (1.0 + 3.0 * GELU_C * x2))
    return g, dg


def _sigmoid(x):
    return 1.0 / (1.0 + jnp.exp(-x))


def _ln_stats(z):
    mu = jnp.mean(z, axis=-1, keepdims=True)
    zc = z - mu
    var = jnp.mean(zc * zc, axis=-1, keepdims=True)
    r = lax.rsqrt(var + LN_EPS)
    return zc * r, r


def _ln_bwd_rows(dn, nh, r):
    return r * (dn - jnp.mean(dn, axis=-1, keepdims=True) - nh * jnp.mean(dn * nh, axis=-1, keepdims=True))


def _colsum(x):
    return jnp.sum(x, axis=0, keepdims=True)


def _dot(a, b, dims):
    return lax.dot_general(a.astype(BF16), b.astype(BF16), (dims, ((), ())), preferred_element_type=F32)


NN = ((1,), (0,))
NT = ((1,), (1,))
TN = ((0,), (0,))


ANY = pl.BlockSpec(memory_space=pl.ANY)
N_RELATIONS = N_DEV - 1


def _my_place():
    return lax.axis_index("x"), lax.axis_index("y"), lax.axis_index("c")


class _Comm:
    def __init__(self, gather=(), exchange=()):
        gather = [e if isinstance(e, tuple) else (e, 0, e.shape[0], None) for e in gather]
        exchange = [e if isinstance(e, tuple) else (e, 0, e.shape[1]) for e in exchange]
        self.arrs = [e[0] for e in gather] + [e[0] for e in exchange]
        self.n_gather = len(gather)
        self.n = len(self.arrs)
        self.rows = [pl.ds(lo, n) for _, lo, n, _ in gather] + [pl.ds(lo, n) for _, lo, n in exchange]
        self.into = {i: e[3] for i, e in enumerate(gather) if e[3] is not None}

    def out_shape(self):
        return [jax.ShapeDtypeStruct(((N_DEV,) + a.shape) if i < self.n_gather else a.shape, a.dtype)
                for i, a in enumerate(self.arrs)]

    def sems(self):
        return [pltpu.SemaphoreType.DMA((self.n, N_RELATIONS)), pltpu.SemaphoreType.DMA((self.n, N_RELATIONS)),
                pltpu.SemaphoreType.DMA((self.n,))]

    def _gather_copy(self, ins, outs, sems, a, k, place, to, from_input=False):
        px, py, pc = place
        block = outs[a].at[4 * px + 2 * py + pc, self.rows[a]]
        return pltpu.make_async_remote_copy(
            src_ref=ins[a].at[self.rows[a]] if from_input else block, dst_ref=block,
            send_sem=sems[0].at[a, k], recv_sem=sems[1].at[a, k], device_id=to, device_id_type=MESH_ID)

    def _exchange_copy(self, ins, outs, sems, a, k, landing=False):
        x, y, c = _my_place()
        me = 4 * x + 2 * y + c
        peer = (x ^ (k >> 2), y ^ ((k >> 1) & 1), c ^ (k & 1))
        return pltpu.make_async_remote_copy(
            src_ref=ins[a].at[me ^ k, self.rows[a]], dst_ref=outs[a].at[(me ^ k) if landing else me, self.rows[a]],
            send_sem=sems[0].at[a, k - 1], recv_sem=sems[1].at[a, k - 1], device_id=peer, device_id_type=MESH_ID)

    def _local_copy(self, ins, outs, sems, a):
        x, y, c = _my_place()
        me = 4 * x + 2 * y + c
        if a < self.n_gather:
            return pltpu.make_async_copy(ins[a].at[self.rows[a]], outs[a].at[me, self.rows[a]], sems[2].at[a])
        return pltpu.make_async_copy(ins[a].at[me, self.rows[a]], outs[a].at[me, self.rows[a]], sems[2].at[a])

    def _first_stage(self, ins, outs, sems, a):
        x, y, c = _my_place()
        me = (x, y, c)
        chips = [(1 - x, y), (x, 1 - y), (1 - x, 1 - y)]
        return ([self._gather_copy(ins, outs, sems, a, 0, me, (x, y, 1 - c), from_input=True)]
                + [self._gather_copy(ins, outs, sems, a, 1 + j, me, (*chip, c), from_input=True) for j, chip in enumerate(chips)])

    def start(self, ins, outs, sems):
        for a in range(self.n):
            self._local_copy(ins, outs, sems, a).start()
        for a in range(self.n_gather):
            for cp in self._first_stage(ins, outs, sems, a):
                cp.start()
        for k in range(1, N_DEV):
            for a in range(self.n_gather, self.n):
                self._exchange_copy(ins, outs, sems, a, k).start()

    def forward(self, ins, outs, sems):
        x, y, c = _my_place()
        me, sibling = (x, y, c), (x, y, 1 - c)
        for j, chip in enumerate([(1 - x, y), (x, 1 - y), (1 - x, 1 - y)]):
            for a in range(self.n_gather):
                self._gather_copy(ins, outs, sems, a, 1 + j, (*chip, c), me).wait_recv()
                self._gather_copy(ins, outs, sems, a, 4 + j, (*chip, c), sibling).start()

    def finish(self, ins, outs, sems):
        x, y, c = _my_place()
        me, sibling = (x, y, c), (x, y, 1 - c)
        chips = [(1 - x, y), (x, 1 - y), (1 - x, 1 - y)]
        passed = [self._gather_copy(ins, outs, sems, a, 4 + j, (*chip, c), sibling)
                  for j, chip in enumerate(chips) for a in range(self.n_gather)]
        for a in range(self.n_gather):
            self._gather_copy(ins, outs, sems, a, 0, sibling, me).wait_recv()
            for j, chip in enumerate(chips):
                self._gather_copy(ins, outs, sems, a, 4 + j, (*chip, 1 - c), me).wait_recv()
        for k in range(1, N_DEV):
            for a in range(self.n_gather, self.n):
                self._exchange_copy(ins, outs, sems, a, k, landing=True).wait_recv()
        for a in range(self.n_gather):
            for cp in self._first_stage(ins, outs, sems, a):
                cp.wait_send()
        for cp in passed:
            cp.wait_send()
        for k in range(1, N_DEV):
            for a in range(self.n_gather, self.n):
                self._exchange_copy(ins, outs, sems, a, k).wait_send()
        for a in range(self.n):
            self._local_copy(ins, outs, sems, a).wait()


def _comm_only(comm, name):
    assert not comm.into

    def body(*refs):
        ins, outs, sems = refs[:comm.n], refs[comm.n:2 * comm.n], refs[2 * comm.n:]
        comm.start(ins, outs, sems)
        comm.forward(ins, outs, sems)
        comm.finish(ins, outs, sems)

    return pl.pallas_call(body, name=name, in_specs=[ANY] * comm.n, out_specs=[ANY] * comm.n,
                          out_shape=comm.out_shape(), scratch_shapes=comm.sems())(*comm.arrs)


def _call(body, *, name, grid, in_specs, out_specs, out_shape, args, sem, scratch_shapes=(), comm=None):
    in_specs, out_specs, out_shape, scratch_shapes = list(in_specs), list(out_specs), list(out_shape), list(scratch_shapes)
    if comm is None:
        outs = pl.pallas_call(body, name=name, grid=grid, in_specs=in_specs, out_specs=out_specs, out_shape=out_shape,
                              scratch_shapes=scratch_shapes, compiler_params=_params(*sem))(*args)
        return list(outs), []
    n_in, n_out, n_scr, nc = len(in_specs), len(out_specs), len(scratch_shapes), comm.n
    completed = sorted(comm.into)

    def wrapped(*refs):
        ins, refs = refs[:n_in], refs[n_in:]
        c_in, refs = refs[:nc], refs[nc + len(completed):]
        outs, refs = refs[:n_out], refs[n_out:]
        c_out, refs = refs[:nc], refs[nc:]
        scr, sems = refs[:n_scr], refs[n_scr:]
        step = functools.reduce(lambda acc, ax: acc * grid[ax] + pl.program_id(ax), range(len(grid)), 0)
        steps = math.prod(grid)

        @pl.when(step == 0)
        def _():
            comm.start(c_in, c_out, sems)

        @pl.when(step == steps - 1)
        def _():
            comm.forward(c_in, c_out, sems)

        body(*ins, *outs, *scr)

        @pl.when(step == steps - 1)
        def _():
            comm.finish(c_in, c_out, sems)

    outs = pl.pallas_call(
        wrapped, name=name, grid=grid, in_specs=in_specs + [ANY] * (nc + len(completed)), out_specs=out_specs + [ANY] * nc,
        out_shape=out_shape + comm.out_shape(), scratch_shapes=scratch_shapes + comm.sems(),
        input_output_aliases={n_in + nc + pos: n_out + item for pos, item in enumerate(completed)},
        compiler_params=_params(*(["arbitrary"] * len(grid))))(*args, *comm.arrs, *[comm.into[item] for item in completed])
    return list(outs[:n_out]), list(outs[n_out:])


def _matmul(a, b, mode, out_dtype, name, tm, tn, tk, *, bias=None, res=None, res_scale=1.0, b_off=0, comm=None):
    tm = min(tm, a.shape[1] if mode == "tn" else a.shape[0])
    tk = min(tk, a.shape[0] if mode == "tn" else a.shape[1])
    if mode == "nn":
        (m, k), n = a.shape, b.shape[1]
        a_spec = pl.BlockSpec((tm, tk), lambda i, j, kk: (i, kk))
        b_spec = pl.BlockSpec((tk, tn), lambda i, j, kk: (kk + b_off, j))
        dims = NN
    elif mode == "nt":
        (m, k), n = a.shape, b.shape[0]
        a_spec = pl.BlockSpec((tm, tk), lambda i, j, kk: (i, kk))
        b_spec = pl.BlockSpec((tn, tk), lambda i, j, kk: (j, kk + b_off))
        dims = NT
    else:
        (k, m), n = a.shape, b.shape[1]
        a_spec = pl.BlockSpec((tk, tm), lambda i, j, kk: (kk, i))
        b_spec = pl.BlockSpec((tk, tn), lambda i, j, kk: (kk, j))
        dims = TN
    assert m % tm == 0 and n % tn == 0 and k % tk == 0, (name, m, n, k)
    nk = k // tk
    in_specs = [a_spec, b_spec]
    args = [a, b]
    if bias is not None:
        in_specs.append(pl.BlockSpec((1, tn), lambda i, j, kk: (0, j)))
        args.append(bias)
    if res is not None:
        in_specs.append(pl.BlockSpec((tm, tn), lambda i, j, kk: (i, j)))
        args.append(res)

    def finish(out, refs, o_ref):
        pos = 2
        if bias is not None:
            out = out + refs[pos][...]
            pos += 1
        if res is not None:
            out = out + res_scale * refs[pos][...].astype(F32)
        o_ref[...] = out.astype(out_dtype)

    def body_one_step(*refs):
        finish(_dot(refs[0][...], refs[1][...], dims), refs, refs[-1])

    def body(*refs):
        a_ref, b_ref = refs[0], refs[1]
        o_ref, acc = refs[-2], refs[-1]
        kk = pl.program_id(2)

        @pl.when(kk == 0)
        def _():
            acc[...] = jnp.zeros_like(acc)

        acc[...] += _dot(a_ref[...], b_ref[...], dims)

        @pl.when(kk == nk - 1)
        def _():
            finish(acc[...], refs, o_ref)

    (out,), moved = _call(
        body_one_step if nk == 1 else body, name=name, grid=(m // tm, n // tn, nk),
        in_specs=in_specs, out_specs=[pl.BlockSpec((tm, tn), lambda i, j, kk: (i, j))],
        out_shape=[jax.ShapeDtypeStruct((m, n), out_dtype)],
        scratch_shapes=[] if nk == 1 else [pltpu.VMEM((tm, tn), F32)],
        sem=("parallel", "parallel", "arbitrary"), args=args, comm=comm)
    return out if comm is None else (out, moved)


def _matmul_tn_pair(a0, a1, b, out_dtype, name, tm, tn, tk, comm=None):
    (k, m), n = a0.shape, b.shape[1]
    tk = min(tk, k)
    assert a1.shape == a0.shape and m % tm == 0 and n % tn == 0 and k % tk == 0, (name, m, n, k)
    mi, nk = m // tm, k // tk

    def body(a0_ref, a1_ref, b_ref, o_ref, acc):
        i, kk = pl.program_id(0), pl.program_id(2)

        @pl.when(kk == 0)
        def _():
            acc[...] = jnp.zeros_like(acc)

        @pl.when(i < mi)
        def _():
            acc[...] += _dot(a0_ref[...], b_ref[...], TN)

        @pl.when(i >= mi)
        def _():
            acc[...] += _dot(a1_ref[...], b_ref[...], TN)

        @pl.when(kk == nk - 1)
        def _():
            o_ref[...] = acc[...].astype(out_dtype)

    (out,), moved = _call(
        body, name=name, grid=(2 * mi, n // tn, nk),
        in_specs=[pl.BlockSpec((tk, tm), lambda i, j, kk: (jnp.where(i < mi, kk, nk - 1), jnp.minimum(i, mi - 1))),
                  pl.BlockSpec((tk, tm), lambda i, j, kk: (jnp.where(i >= mi, kk, 0), jnp.maximum(i - mi, 0))),
                  pl.BlockSpec((tk, tn), lambda i, j, kk: (kk, j))],
        out_specs=[pl.BlockSpec((tm, tn), lambda i, j, kk: (i, j))],
        out_shape=[jax.ShapeDtypeStruct((2 * m, n), out_dtype)],
        scratch_shapes=[pltpu.VMEM((tm, tn), F32)],
        sem=("parallel", "parallel", "arbitrary"), args=(a0, a1, b), comm=comm)
    return out if comm is None else (out, moved)


def _residual_input(x_ref, prev_refs):
    if not prev_refs:
        return x_ref[...]
    nh, _ = _ln_stats(x_ref[...])
    return nh * prev_refs[0][...] + prev_refs[1][...]


def _matmul_res_ln(a, b, x, g, beta, name, tm, prev=None, comm=None):
    t, k = a.shape
    d = b.shape[1]
    tm = min(tm, t)
    assert t % tm == 0
    n_prev = 0 if prev is None else 2

    def body(a_ref, b_ref, x_ref, g_ref, beta_ref, *rest):
        z_ref, xo_ref = rest[n_prev:]
        z = ALPHA * _residual_input(x_ref, rest[:n_prev]) + _dot(a_ref[...], b_ref[...], NN)
        nh, _ = _ln_stats(z)
        z_ref[...] = z
        xo_ref[...] = (nh * g_ref[...] + beta_ref[...]).astype(BF16)

    row = pl.BlockSpec((tm, d), lambda i: (i, 0))
    vec = pl.BlockSpec((1, d), lambda i: (0, 0))
    outs, moved = _call(
        body, name=name, grid=(t // tm,),
        in_specs=[pl.BlockSpec((tm, k), lambda i: (i, 0)), pl.BlockSpec((k, d), lambda i: (0, 0)), row, vec, vec] + [vec] * n_prev,
        out_specs=[row, row],
        out_shape=[jax.ShapeDtypeStruct((t, d), F32), jax.ShapeDtypeStruct((t, d), BF16)],
        sem=("parallel",), args=(a, b, x, g, beta, *(prev or ())), comm=comm)
    return outs if comm is None else (outs, moved)


def _matmul_ln_bwd(parts, b, z, g, dres, name, tm, comm=None):
    m = parts[0][0].shape[0]
    d = b.shape[1]
    tm = min(tm, m)
    n = len(parts)
    assert m % tm == 0 and all(row % a.shape[1] == 0 for a, row in parts)

    def body(*refs):
        z_ref, g_ref, dres_ref = refs[2 * n:2 * n + 3]
        dz_ref, dzb_ref, dg_ref, db_ref = refs[-4:]

        @pl.when(pl.program_id(0) == 0)
        def _():
            dg_ref[...] = jnp.zeros_like(dg_ref)
            db_ref[...] = jnp.zeros_like(db_ref)

        dy = ALPHA * dres_ref[...]
        for p in range(n):
            dy = dy + _dot(refs[p][...], refs[n + p][...], NN)
        nh, r = _ln_stats(z_ref[...])
        dg_ref[...] += _colsum(dy * nh)
        db_ref[...] += _colsum(dy)
        dz = _ln_bwd_rows(dy * g_ref[...], nh, r)
        dz_ref[...] = dz
        dzb_ref[...] = dz.astype(BF16)

    def b_spec(a, row):
        blk = row // a.shape[1]
        return pl.BlockSpec((a.shape[1], d), lambda i: (blk, 0))

    row = pl.BlockSpec((tm, d), lambda i: (i, 0))
    vec = pl.BlockSpec((1, d), lambda i: (0, 0))
    vshape = jax.ShapeDtypeStruct((1, d), F32)
    outs, moved = _call(
        body, name=name, grid=(m // tm,),
        in_specs=[pl.BlockSpec((tm, a.shape[1]), lambda i: (i, 0)) for a, _ in parts] + [b_spec(a, r_) for a, r_ in parts]
        + [row, vec, row],
        out_specs=[row, row, vec, vec],
        out_shape=[jax.ShapeDtypeStruct((m, d), F32), jax.ShapeDtypeStruct((m, d), BF16), vshape, vshape],
        sem=("arbitrary",), args=(*[a for a, _ in parts], *([b] * n), z, g, dres), comm=comm)
    return outs if comm is None else (outs, moved)


def _matmul_res_ln_loss(a, b, x, g, beta, target, name, tm, prev):
    t, k = a.shape
    d = b.shape[1]
    tm = min(tm, t)

    def body(a_ref, b_ref, x_ref, g_ref, beta_ref, t_ref, gp_ref, bp_ref, dz_ref, dzb_ref, dg_ref, db_ref, loss_ref):
        @pl.when(pl.program_id(0) == 0)
        def _():
            dg_ref[...] = jnp.zeros_like(dg_ref)
            db_ref[...] = jnp.zeros_like(db_ref)
            loss_ref[...] = jnp.zeros_like(loss_ref)

        nh, r = _ln_stats(ALPHA * _residual_input(x_ref, (gp_ref, bp_ref)) + _dot(a_ref[...], b_ref[...], NN))
        err = nh * g_ref[...] + beta_ref[...] - t_ref[...]
        loss_ref[...] += _colsum(err * err)
        dy = err * (1.0 / d)
        dg_ref[...] += _colsum(dy * nh)
        db_ref[...] += _colsum(dy)
        dz = _ln_bwd_rows(dy * g_ref[...], nh, r)
        dz_ref[...] = dz
        dzb_ref[...] = dz.astype(BF16)

    row = pl.BlockSpec((tm, d), lambda i: (i, 0))
    vec = pl.BlockSpec((1, d), lambda i: (0, 0))
    vshape = jax.ShapeDtypeStruct((1, d), F32)
    return pl.pallas_call(
        body, name=name, grid=(t // tm,),
        in_specs=[pl.BlockSpec((tm, k), lambda i: (i, 0)), pl.BlockSpec((k, d), lambda i: (0, 0)), row, vec, vec, row, vec, vec],
        out_specs=[row, row, vec, vec, vec],
        out_shape=[jax.ShapeDtypeStruct((t, d), F32), jax.ShapeDtypeStruct((t, d), BF16), vshape, vshape, vshape],
        compiler_params=_params("arbitrary"),
    )(a, b, x, g, beta, target, *prev)


FFN_HALO = 16
FFN_CHUNK = 256
LANES = 128
SUBLANES = 8


def _rows_up(e, start, rows):
    if start % SUBLANES == 0:
        return e[start:start + rows]
    return pltpu.roll(e, e.shape[0] - start, 0)[0:rows]


def _fold(x):
    return jnp.sum(x.reshape(x.shape[0] // SUBLANES, SUBLANES, x.shape[1]), axis=0)


def _ffn_mid_fwd(h, cw, cb, name, tm=1024, tc=1408, comm=None):
    t, f2 = h.shape
    tm = min(tm, t)
    f = f2 // 2
    nj, nt, hb = f // tc, t // tm, tm // FFN_HALO

    ch = min(FFN_CHUNK, tm)

    def body(hg, hgp, hv, hvp, cwg, cwv, cbg, cbv, u_ref, cg_ref, cv_ref):
        i = pl.program_id(1)
        o = FFN_HALO - FFN_KERNEL + 1
        for lg in range(tc // LANES):
            cols = slice(lg * LANES, (lg + 1) * LANES)
            wg, wv = [cwg[k:k + 1, cols] for k in range(FFN_KERNEL)], [cwv[k:k + 1, cols] for k in range(FFN_KERNEL)]
            bg, bv = cbg[:, cols], cbv[:, cols]

            def emit(base, eg, ev):
                cg = wg[0] * _rows_up(eg, o, ch) + wg[1] * _rows_up(eg, o + 1, ch) + wg[2] * _rows_up(eg, o + 2, ch) + bg
                cv = wv[0] * _rows_up(ev, o, ch) + wv[1] * _rows_up(ev, o + 1, ch) + wv[2] * _rows_up(ev, o + 2, ch) + bv
                u_ref[pl.ds(base, ch), cols] = (_gelu(cg) * cv).astype(BF16)
                cg_ref[pl.ds(base, ch), cols] = cg.astype(BF16)
                cv_ref[pl.ds(base, ch), cols] = cv.astype(BF16)

            def first(main, prev):
                return jnp.concatenate([jnp.where(i > 0, prev[:, cols].astype(F32), 0.0), main[0:ch, cols].astype(F32)], axis=0)

            def inner(c, carry):
                base = pl.multiple_of(c * ch, ch)
                emit(base, hg[pl.ds(base - FFN_HALO, ch + FFN_HALO), cols].astype(F32),
                     hv[pl.ds(base - FFN_HALO, ch + FFN_HALO), cols].astype(F32))
                return carry

            emit(0, first(hg, hgp), first(hv, hvp))
            if tm > ch:
                lax.fori_loop(1, tm // ch, inner, 0)

    def main_spec(off):
        return pl.BlockSpec((tm, tc), lambda j, i: (i, j + off))

    def prev_spec(off):
        return pl.BlockSpec((FFN_HALO, tc), lambda j, i: (jnp.maximum(i * hb - 1, 0), j + off))

    def par_spec(rows, off):
        return pl.BlockSpec((rows, tc), lambda j, i: (0, j + off))

    outs, moved = _call(
        body, name=name, grid=(nj, nt),
        in_specs=[main_spec(0), prev_spec(0), main_spec(nj), prev_spec(nj),
                  par_spec(FFN_KERNEL, 0), par_spec(FFN_KERNEL, nj), par_spec(1, 0), par_spec(1, nj)],
        out_specs=[pl.BlockSpec((tm, tc), lambda j, i: (i, j))] * 3,
        out_shape=[jax.ShapeDtypeStruct((t, f), BF16)] * 3,
        sem=("parallel", "arbitrary"), args=(h, h, h, h, cw, cw, cb, cb), comm=comm)
    return outs if comm is None else (outs, moved)


def _ffn_mid_bwd(h, cg, cv, du, cw, name, tm=1024, tc=1408, comm=None):
    t, f2 = h.shape
    tm = min(tm, t)
    f = f2 // 2
    nj, nt, hb = f // tc, t // tm, tm // FFN_HALO

    ch = min(FFN_CHUNK, tm)
    ahead = ch + SUBLANES
    n_ch = tm // ch

    def body(hg, hv, cg_ref, cgn_ref, cv_ref, cvn_ref, du_ref, dun_ref, cwg, cwv,
             dhg_ref, dhv_ref, dcwg_ref, dcwv_ref, dcbg_ref, dcbv_ref):
        i = pl.program_id(1)

        @pl.when(i == 0)
        def _():
            for ref in (dcwg_ref, dcwv_ref, dcbg_ref, dcbv_ref):
                ref[...] = jnp.zeros_like(ref)

        for lg in range(tc // LANES):
            cols = slice(lg * LANES, (lg + 1) * LANES)
            wg, wv = [cwg[k:k + 1, cols] for k in range(FFN_KERNEL)], [cwv[k:k + 1, cols] for k in range(FFN_KERNEL)]

            def emit(base, cg_e, cv_e, du_e, acc):
                cg_a, cv_a, du_a = cg_e[0:ahead], cv_e[0:ahead], du_e[0:ahead]
                gl, dgl = _gelu_and_grad(cg_a)

                def back(d, h_ref, w, dh_ref):
                    later = [d[0:ch], _rows_up(d, 1, ch), _rows_up(d, 2, ch)]
                    dh_ref[pl.ds(base, ch), cols] = (w[2] * later[0] + w[1] * later[1] + w[0] * later[2]).astype(BF16)
                    h_own = h_ref[pl.ds(base, ch), cols].astype(F32)
                    return [_fold(later[0])] + [_fold(later[FFN_KERNEL - 1 - k] * h_own) for k in range(FFN_KERNEL)]

                sums = back(du_a * cv_a * dgl, hg, wg, dhg_ref) + back(du_a * gl, hv, wv, dhv_ref)
                return tuple(a + s_ for a, s_ in zip(acc, sums))

            def inner(c, acc):
                base = pl.multiple_of(c * ch, ch)
                rows = pl.ds(base, ch + FFN_HALO)
                return emit(base, cg_ref[rows, cols].astype(F32), cv_ref[rows, cols].astype(F32), du_ref[rows, cols].astype(F32), acc)

            def last(acc):
                def rows(main, after):
                    return jnp.concatenate([main[tm - ch:tm, cols].astype(F32), after], axis=0)

                du_next = jnp.where(i < nt - 1, dun_ref[:, cols].astype(F32), 0.0)
                return emit(tm - ch, rows(cg_ref, cgn_ref[:, cols].astype(F32)), rows(cv_ref, cvn_ref[:, cols].astype(F32)),
                            rows(du_ref, du_next), acc)

            acc = (jnp.zeros((SUBLANES, LANES), F32),) * (2 * (1 + FFN_KERNEL))
            if n_ch > 1:
                acc = lax.fori_loop(0, n_ch - 1, inner, acc)
            acc = last(acc)
            dcbg_ref[:, cols] += _colsum(acc[0])
            dcbv_ref[:, cols] += _colsum(acc[1 + FFN_KERNEL])
            for k in range(FFN_KERNEL):
                dcwg_ref[k:k + 1, cols] += _colsum(acc[1 + k])
                dcwv_ref[k:k + 1, cols] += _colsum(acc[2 + FFN_KERNEL + k])

    last_blk = t // FFN_HALO - 1

    def main_spec(off):
        return pl.BlockSpec((tm, tc), lambda j, i: (i, j + off))

    def next_spec(off):
        return pl.BlockSpec((FFN_HALO, tc), lambda j, i: (jnp.minimum((i + 1) * hb, last_blk), j + off))

    def par_spec(rows, off):
        return pl.BlockSpec((rows, tc), lambda j, i: (0, j + off))

    out_tile = pl.BlockSpec((tm, tc), lambda j, i: (i, j))
    outs, moved = _call(
        body, name=name, grid=(nj, nt),
        in_specs=[main_spec(0), main_spec(nj), main_spec(0), next_spec(0), main_spec(0), next_spec(0), main_spec(0), next_spec(0),
                  par_spec(FFN_KERNEL, 0), par_spec(FFN_KERNEL, nj)],
        out_specs=[out_tile, out_tile, par_spec(FFN_KERNEL, 0), par_spec(FFN_KERNEL, 0), par_spec(1, 0), par_spec(1, 0)],
        out_shape=[jax.ShapeDtypeStruct((t, f), BF16), jax.ShapeDtypeStruct((t, f), BF16),
                   jax.ShapeDtypeStruct((FFN_KERNEL, f), F32), jax.ShapeDtypeStruct((FFN_KERNEL, f), F32),
                   jax.ShapeDtypeStruct((1, f), F32), jax.ShapeDtypeStruct((1, f), F32)],
        sem=("parallel", "arbitrary"), args=(h, h, cg, cg, cv, cv, du, du, cw, cw), comm=comm)
    return outs if comm is None else (outs, moved)


MIX_HALO = 32


def _glu(hh):
    return hh[:, 0:A_WIDTH] * _sigmoid(hh[:, A_WIDTH:2 * A_WIDTH])


def _fill_row_shifts(s):
    rows = s.shape[1] - SUBLANES
    for j in range(1, SUBLANES):
        s[j, 0:rows, :] = s[0, pl.ds(j, rows), :]


def _rows_from(s, start, rows):
    j = start % SUBLANES
    return s[j, start - j:start - j + rows, :]


def _tril_mask():
    return lax.broadcasted_iota(jnp.int32, (B_CHUNK, B_CHUNK), 0) >= lax.broadcasted_iota(jnp.int32, (B_CHUNK, B_CHUNK), 1)


def _spatial_mix(q, ms_ref, sbt_ref, tm):
    mask = _tril_mask()
    ws = [jnp.where(mask, ms_ref[g], 0.0).astype(BF16) for g in range(B_GROUPS)]
    qb = q.astype(BF16)
    rows = []
    for c in range(tm // B_CHUNK):
        cols = [_dot(ws[g], qb[c * B_CHUNK:(c + 1) * B_CHUNK, g * 128:(g + 1) * 128], NN) + sbt_ref[:, g:g + 1]
                for g in range(B_GROUPS)]
        rows.append(jnp.concatenate(cols, axis=1))
    return jnp.concatenate(rows, axis=0)


def _mixer_mid_fwd(h, cw, cb, ag, ab, bg, bb, ms, sbt, name, tm=256, comm=None):
    t = h.shape[0]
    nt, hb = t // tm, tm // MIX_HALO
    o = MIX_HALO - A_KERNEL + 1

    def body(h_ref, hp_ref, cw_ref, cb_ref, ag_ref, ab_ref, bg_ref, bb_ref, ms_ref, sbt_ref, cat_ref, y_ref, sp):
        i = pl.program_id(0)
        sp[0, 0:MIX_HALO, :] = jnp.where(i > 0, _glu(hp_ref[:, 0:2 * A_WIDTH].astype(F32)), 0.0)
        sp[0, MIX_HALO:, :] = _glu(h_ref[:, 0:2 * A_WIDTH].astype(F32))
        _fill_row_shifts(sp)
        y = jnp.zeros((tm, A_WIDTH), F32) + cb_ref[...]
        for k in range(A_KERNEL):
            y = y + cw_ref[k:k + 1, :] * _rows_from(sp, o + k, tm)
        y_ref[...] = y.astype(BF16)
        nh, _ = _ln_stats(y)
        ln = nh * ag_ref[...] + ab_ref[...]
        cat_ref[:, 0:A_WIDTH] = (ln * _sigmoid(ln)).astype(BF16)
        u = _gelu(h_ref[:, 1024:1536].astype(F32))
        nb, _ = _ln_stats(_gelu(h_ref[:, 1536:2048].astype(F32)))
        mixed = _spatial_mix(nb * bg_ref[...] + bb_ref[...], ms_ref, sbt_ref, tm)
        cat_ref[:, A_WIDTH:] = (u * mixed).astype(BF16)

    vec = pl.BlockSpec((1, A_WIDTH), lambda i: (0, 0))
    outs, moved = _call(
        body, name=name, grid=(nt,),
        in_specs=[pl.BlockSpec((tm, 2048), lambda i: (i, 0)),
                  pl.BlockSpec((MIX_HALO, 2048), lambda i: (jnp.maximum(i * hb - 1, 0), 0)),
                  pl.BlockSpec((A_KERNEL, A_WIDTH), lambda i: (0, 0)), vec, vec, vec, vec, vec,
                  pl.BlockSpec((B_GROUPS, B_CHUNK, B_CHUNK), lambda i: (0, 0, 0)),
                  pl.BlockSpec((B_CHUNK, B_GROUPS), lambda i: (0, 0))],
        out_specs=[pl.BlockSpec((tm, D_MODEL), lambda i: (i, 0)), pl.BlockSpec((tm, A_WIDTH), lambda i: (i, 0))],
        out_shape=[jax.ShapeDtypeStruct((t, D_MODEL), BF16), jax.ShapeDtypeStruct((t, A_WIDTH), BF16)],
        scratch_shapes=[pltpu.VMEM((SUBLANES, tm + MIX_HALO, A_WIDTH), F32)],
        sem=("parallel",), args=(h, h, cw, cb, ag, ab, bg, bb, ms, sbt), comm=comm)
    return outs if comm is None else (outs, moved)


def _mixer_mid_bwd(h, y, dcat, cw, ag, ab, bg, bb, ms, mst, sbt, name, tm=256, comm=None):
    t = h.shape[0]
    nt, hb = t // tm, tm // MIX_HALO
    r = tm + MIX_HALO
    nchunk = tm // B_CHUNK

    def body(h_ref, y_ref, yn_ref, dc_ref, dcn_ref, cw_ref, ag_ref, ab_ref, bg_ref, bb_ref, ms_ref, mst_ref, sbt_ref,
             dh_ref, dcw_ref, dcb_ref, dag_ref, dab_ref, dbg_ref, dbb_ref, dms_ref, dsb_ref, sdy, sbacc):
        i = pl.program_id(0)

        @pl.when(i == 0)
        def _():
            for ref in (dcw_ref, dcb_ref, dag_ref, dab_ref, dbg_ref, dbb_ref, dms_ref, dsb_ref, sbacc):
                ref[...] = jnp.zeros_like(ref)

        nh, rs = _ln_stats(jnp.concatenate([y_ref[...].astype(F32), yn_ref[...].astype(F32)], axis=0))
        ln = nh * ag_ref[...] + ab_ref[...]
        sg = _sigmoid(ln)
        dao = jnp.concatenate([dc_ref[:, 0:A_WIDTH].astype(F32),
                               jnp.where(i < nt - 1, dcn_ref[:, 0:A_WIDTH].astype(F32), 0.0)], axis=0)
        dln = dao * (sg * (1.0 + ln * (1.0 - sg)))
        dag_ref[...] += _colsum(dln[0:tm] * nh[0:tm])
        dab_ref[...] += _colsum(dln[0:tm])
        sdy[0] = _ln_bwd_rows(dln * ag_ref[...], nh, rs)
        _fill_row_shifts(sdy)
        dcb_ref[...] += _colsum(sdy[0, 0:tm, :])
        av = h_ref[:, 0:A_WIDTH].astype(F32)
        s = _sigmoid(h_ref[:, A_WIDTH:2 * A_WIDTH].astype(F32))
        p_own = av * s
        dp = jnp.zeros((tm, A_WIDTH), F32)
        for k in range(A_KERNEL):
            later = _rows_from(sdy, A_KERNEL - 1 - k, tm)
            dcw_ref[k:k + 1, :] += _colsum(later * p_own)
            dp = dp + cw_ref[k:k + 1, :] * later
        dh_ref[:, 0:A_WIDTH] = (dp * s).astype(BF16)
        dh_ref[:, A_WIDTH:2 * A_WIDTH] = (dp * av * s * (1.0 - s)).astype(BF16)

        u, dgu = _gelu_and_grad(h_ref[:, 1024:1536].astype(F32))
        w, dgw = _gelu_and_grad(h_ref[:, 1536:2048].astype(F32))
        nb, rb = _ln_stats(w)
        q = nb * bg_ref[...] + bb_ref[...]
        mixed = _spatial_mix(q, ms_ref, sbt_ref, tm)
        dbo = dc_ref[:, A_WIDTH:].astype(F32)
        dh_ref[:, 1024:1536] = (dbo * mixed * dgu).astype(BF16)
        dmx = dbo * u
        mask = _tril_mask()
        wst = [jnp.where(mask.T, mst_ref[g], 0.0).astype(BF16) for g in range(B_GROUPS)]
        qb = q.astype(BF16)
        dmb = dmx.astype(BF16)
        rows = []
        for c in range(nchunk):
            cols = []
            for g in range(B_GROUPS):
                rs_, cs_ = slice(c * B_CHUNK, (c + 1) * B_CHUNK), slice(g * 128, (g + 1) * 128)
                sbacc[g] += dmx[rs_, cs_]
                dms_ref[g] += _dot(dmb[rs_, cs_], qb[rs_, cs_], NT)
                cols.append(_dot(wst[g], dmb[rs_, cs_], NN))
            rows.append(jnp.concatenate(cols, axis=1))
        dq = jnp.concatenate(rows, axis=0)
        dbg_ref[...] += _colsum(dq * nb)
        dbb_ref[...] += _colsum(dq)
        dh_ref[:, 1536:2048] = (_ln_bwd_rows(dq * bg_ref[...], nb, rb) * dgw).astype(BF16)

        @pl.when(i == nt - 1)
        def _():
            for g in range(B_GROUPS):
                dms_ref[g] = jnp.where(mask, dms_ref[g], 0.0)
                dsb_ref[g] = jnp.sum(sbacc[g], axis=1, keepdims=True)

    last_blk = t // MIX_HALO - 1
    vec = pl.BlockSpec((1, A_WIDTH), lambda i: (0, 0))
    mat = pl.BlockSpec((B_GROUPS, B_CHUNK, B_CHUNK), lambda i: (0, 0, 0))
    taps = pl.BlockSpec((A_KERNEL, A_WIDTH), lambda i: (0, 0))

    def halo(width):
        return pl.BlockSpec((MIX_HALO, width), lambda i: (jnp.minimum((i + 1) * hb, last_blk), 0))

    vshape = jax.ShapeDtypeStruct((1, A_WIDTH), F32)
    outs, moved = _call(
        body, name=name, grid=(nt,),
        in_specs=[pl.BlockSpec((tm, 2048), lambda i: (i, 0)), pl.BlockSpec((tm, A_WIDTH), lambda i: (i, 0)), halo(A_WIDTH),
                  pl.BlockSpec((tm, D_MODEL), lambda i: (i, 0)), halo(D_MODEL),
                  taps, vec, vec, vec, vec, mat, mat, pl.BlockSpec((B_CHUNK, B_GROUPS), lambda i: (0, 0))],
        out_specs=[pl.BlockSpec((tm, 2048), lambda i: (i, 0)), taps, vec, vec, vec, vec, vec, mat,
                   pl.BlockSpec((B_GROUPS, B_CHUNK, 1), lambda i: (0, 0, 0))],
        out_shape=[jax.ShapeDtypeStruct((t, 2048), BF16), jax.ShapeDtypeStruct((A_KERNEL, A_WIDTH), F32),
                   vshape, vshape, vshape, vshape, vshape,
                   jax.ShapeDtypeStruct((B_GROUPS, B_CHUNK, B_CHUNK), F32), jax.ShapeDtypeStruct((B_GROUPS, B_CHUNK, 1), F32)],
        scratch_shapes=[pltpu.VMEM((SUBLANES, r, A_WIDTH), F32), pltpu.VMEM((B_GROUPS, B_CHUNK, B_CHUNK), F32)],
        sem=("arbitrary",), args=(h, y, y, dcat, dcat, cw, ag, ab, bg, bb, ms, mst, sbt), comm=comm)
    return outs if comm is None else (outs, moved)


Q_WIDTH = N_Q_HEADS * HEAD_DIM
KV_WIDTH = 2 * N_KV_HEADS * HEAD_DIM
PAIRS_PER_KV = N_Q_HEADS // N_KV_HEADS // 2
ATT_SCALE = 1.0 / math.sqrt(HEAD_DIM)


def _dup_heads(pair_cols, kv_head):
    lane = lax.broadcasted_iota(jnp.int32, pair_cols.shape, 1)
    rolled = pltpu.roll(pair_cols, HEAD_DIM, 1)
    first = lane < HEAD_DIM
    return jnp.where(first, pair_cols, rolled) if kv_head == 0 else jnp.where(first, rolled, pair_cols)


HEADS_PER_KV = N_Q_HEADS // N_KV_HEADS


def _stack_heads(ref, kh):
    lane = lax.broadcasted_iota(jnp.int32, (ATT_BLOCK, 128), 1)
    rows = []
    for pr in range(PAIRS_PER_KV):
        c0 = (kh * PAIRS_PER_KV + pr) * 128
        pair = ref[:, c0:c0 + 128]
        rows += [jnp.where(lane < HEAD_DIM, pair, jnp.zeros_like(pair)), jnp.where(lane < HEAD_DIM, jnp.zeros_like(pair), pair)]
    return jnp.concatenate(rows, axis=0)


def _unstack_heads(stacked, kh, write):
    lane = lax.broadcasted_iota(jnp.int32, (ATT_BLOCK, 128), 1)
    for pr in range(PAIRS_PER_KV):
        first = stacked[(2 * pr) * ATT_BLOCK:(2 * pr + 1) * ATT_BLOCK]
        second = stacked[(2 * pr + 1) * ATT_BLOCK:(2 * pr + 2) * ATT_BLOCK]
        write((kh * PAIRS_PER_KV + pr) * 128, jnp.where(lane < HEAD_DIM, first, second))


def _sink_row(sink_ref, kh):
    return jnp.concatenate([jnp.full((1, ATT_BLOCK), sink_ref[0, kh * HEADS_PER_KV + h], F32) for h in range(HEADS_PER_KV)], axis=1)


def _att_window_bias():
    sj = lax.broadcasted_iota(jnp.int32, (2 * ATT_BLOCK, HEADS_PER_KV * ATT_BLOCK), 0)
    qi = lax.broadcasted_iota(jnp.int32, (2 * ATT_BLOCK, HEADS_PER_KV * ATT_BLOCK), 1) & (ATT_BLOCK - 1)
    diff = qi + ATT_BLOCK - sj
    return jnp.where((diff >= 0) & (diff < ATT_BLOCK), 0.0, -jnp.inf)


def _att_probs_t(q_all, k2, bias_ref, n, sink):
    st = _dot(k2, q_all, NT) * ATT_SCALE + bias_ref[...]
    st = jnp.concatenate([jnp.where(n > 0, st[0:ATT_BLOCK], -jnp.inf), st[ATT_BLOCK:]], axis=0)
    m = jnp.maximum(jnp.max(st, axis=0, keepdims=True), sink)
    e = jnp.exp(st - m)
    es = jnp.exp(sink - m)
    inv = 1.0 / (jnp.sum(e, axis=0, keepdims=True) + es)
    return e * inv, es * inv


def _attn_fwd(qkv, sinks, name, comm=None):
    t = qkv.shape[0]
    nb = t // ATT_BLOCK
    kvb = Q_WIDTH // KV_WIDTH

    def body(sink_ref, q_ref, kv_ref, kvp_ref, o_ref, bias):
        n = pl.program_id(0)

        @pl.when(n == 0)
        def _():
            bias[...] = _att_window_bias()

        kv = jnp.concatenate([kvp_ref[...], kv_ref[...]], axis=0).astype(F32)

        def write(c0, pair):
            o_ref[:, c0:c0 + 128] = pair.astype(BF16)

        for kh in range(N_KV_HEADS):
            k2 = _dup_heads(kv[:, 0:128], kh).astype(BF16)
            v2 = _dup_heads(kv[:, 128:256], kh).astype(BF16)
            pt, _ = _att_probs_t(_stack_heads(q_ref, kh), k2, bias, n, _sink_row(sink_ref, kh))
            _unstack_heads(_dot(v2, pt, TN).T, kh, write)

    (out,), moved = _call(
        body, name=name, grid=(nb,),
        in_specs=[pl.BlockSpec(memory_space=pltpu.SMEM),
                  pl.BlockSpec((ATT_BLOCK, Q_WIDTH), lambda n: (n, 0)),
                  pl.BlockSpec((ATT_BLOCK, KV_WIDTH), lambda n: (n, kvb)),
                  pl.BlockSpec((ATT_BLOCK, KV_WIDTH), lambda n: (jnp.maximum(n - 1, 0), kvb))],
        out_specs=[pl.BlockSpec((ATT_BLOCK, Q_WIDTH), lambda n: (n, 0))],
        out_shape=[jax.ShapeDtypeStruct((t, Q_WIDTH), BF16)],
        scratch_shapes=[pltpu.VMEM((2 * ATT_BLOCK, HEADS_PER_KV * ATT_BLOCK), F32)],
        sem=("arbitrary",), args=(sinks, qkv, qkv, qkv), comm=comm)
    return out if comm is None else (out, moved)


def _attn_bwd(qkv, d_o, sinks, name, comm=None):
    t = qkv.shape[0]
    nb = t // ATT_BLOCK
    kvb = Q_WIDTH // KV_WIDTH

    def body(sink_ref, q_ref, kv_ref, kvp_ref, do_ref, dq_ref, dkv_ref, dbq_ref, dbkv_ref, dsink_ref, carry, bias):
        n = pl.program_id(0)

        @pl.when(n == 0)
        def _():
            for ref in (dbq_ref, dbkv_ref, dsink_ref, carry):
                ref[...] = jnp.zeros_like(ref)
            dkv_ref[...] = jnp.zeros_like(dkv_ref)
            bias[...] = _att_window_bias()

        @pl.when(n < nb)
        def _():
            kv = jnp.concatenate([kvp_ref[...], kv_ref[...]], axis=0).astype(F32)
            lane2 = lax.broadcasted_iota(jnp.int32, (2 * ATT_BLOCK, 128), 1)
            sink_lane = lax.broadcasted_iota(jnp.int32, (1, 128), 1)
            dsink = jnp.zeros((1, 128), F32)
            dk_parts, dv_parts = [], []

            def write(c0, pair):
                dbq_ref[:, c0:c0 + 128] += _colsum(pair)
                dq_ref[:, c0:c0 + 128] = pair.astype(BF16)

            for kh in range(N_KV_HEADS):
                k2 = _dup_heads(kv[:, 0:128], kh).astype(BF16)
                v2 = _dup_heads(kv[:, 128:256], kh).astype(BF16)
                q_all = _stack_heads(q_ref, kh)
                do_all = _stack_heads(do_ref, kh)
                pt, ps = _att_probs_t(q_all, k2, bias, n, _sink_row(sink_ref, kh))
                dpt = _dot(v2, do_all, NT)
                delta = jnp.sum(pt * dpt, axis=0, keepdims=True)
                dst = pt * (dpt - delta) * ATT_SCALE
                psd = ps * delta
                for h in range(HEADS_PER_KV):
                    dsink = dsink + jnp.where(sink_lane == kh * HEADS_PER_KV + h,
                                              -jnp.sum(psd[:, h * ATT_BLOCK:(h + 1) * ATT_BLOCK]), 0.0)
                _unstack_heads(_dot(k2, dst, TN).T, kh, write)
                dk_acc = _dot(dst, q_all, NN)
                dv_acc = _dot(pt, do_all, NN)
                dk_parts.append(dk_acc + pltpu.roll(dk_acc, HEAD_DIM, 1))
                dv_parts.append(dv_acc + pltpu.roll(dv_acc, HEAD_DIM, 1))
            dk = jnp.where(lane2 < HEAD_DIM, dk_parts[0], dk_parts[1])
            dv = jnp.where(lane2 < HEAD_DIM, dv_parts[0], dv_parts[1])
            dkv_new = jnp.concatenate([dk, dv], axis=1)
            done = carry[...] + dkv_new[0:ATT_BLOCK]

            @pl.when(n > 0)
            def _():
                dkv_ref[...] = done.astype(BF16)
                dbkv_ref[...] += _colsum(done)

            carry[...] = dkv_new[ATT_BLOCK:]
            dsink_ref[...] += dsink

        @pl.when(n == nb)
        def _():
            dkv_ref[...] = carry[...].astype(BF16)
            dbkv_ref[...] += _colsum(carry[...])

    def clamp(n):
        return jnp.minimum(n, nb - 1)

    outs, moved = _call(
        body, name=name, grid=(nb + 1,),
        in_specs=[pl.BlockSpec(memory_space=pltpu.SMEM),
                  pl.BlockSpec((ATT_BLOCK, Q_WIDTH), lambda n: (clamp(n), 0)),
                  pl.BlockSpec((ATT_BLOCK, KV_WIDTH), lambda n: (clamp(n), kvb)),
                  pl.BlockSpec((ATT_BLOCK, KV_WIDTH), lambda n: (jnp.maximum(clamp(n) - 1, 0), kvb)),
                  pl.BlockSpec((ATT_BLOCK, Q_WIDTH), lambda n: (clamp(n), 0))],
        out_specs=[pl.BlockSpec((ATT_BLOCK, Q_WIDTH), lambda n: (clamp(n), 0)),
                   pl.BlockSpec((ATT_BLOCK, KV_WIDTH), lambda n: (jnp.maximum(n - 1, 0), 0)),
                   pl.BlockSpec((1, Q_WIDTH), lambda n: (0, 0)),
                   pl.BlockSpec((1, KV_WIDTH), lambda n: (0, 0)),
                   pl.BlockSpec((1, 128), lambda n: (0, 0))],
        out_shape=[jax.ShapeDtypeStruct((t, Q_WIDTH), BF16), jax.ShapeDtypeStruct((t, KV_WIDTH), BF16),
                   jax.ShapeDtypeStruct((1, Q_WIDTH), F32), jax.ShapeDtypeStruct((1, KV_WIDTH), F32),
                   jax.ShapeDtypeStruct((1, 128), F32)],
        scratch_shapes=[pltpu.VMEM((ATT_BLOCK, KV_WIDTH), F32), pltpu.VMEM((2 * ATT_BLOCK, HEADS_PER_KV * ATT_BLOCK), F32)],
        sem=("arbitrary",), args=(sinks, qkv, qkv, qkv, d_o), comm=comm)
    return outs if comm is None else (outs, moved)


def _adamw_math(g, w, m, v):
    m = ADAM_B1 * m + (1.0 - ADAM_B1) * g
    v = ADAM_B2 * v + (1.0 - ADAM_B2) * (g * g)
    m_hat = m / (1.0 - ADAM_B1 ** ADAM_STEP)
    v_hat = v / (1.0 - ADAM_B2 ** ADAM_STEP)
    delta = -ADAM_LR * (m_hat / (jnp.sqrt(v_hat) + ADAM_EPS) + ADAM_WD * w)
    return delta, m, v


def _sum_partials(p_ref):
    g = p_ref[0].astype(F32)
    for s in range(1, N_DEV):
        g = g + p_ref[s].astype(F32)
    return g


def _adamw_big(parts, w, m, v, name, tr):
    r, c = w.shape
    parts = [p if isinstance(p, tuple) else (p, 0, p.shape[1]) for p in parts]
    tiles = [rows // tr for _, _, rows in parts]
    starts = [sum(tiles[:l]) for l in range(len(parts))]
    assert all(lo % tr == 0 and rows % tr == 0 for _, lo, rows in parts) and sum(tiles) * tr == r

    def body(*refs):
        p_refs, (w_ref, m_ref, v_ref, g_out, d_out, m_out, v_out) = refs[:len(parts)], refs[len(parts):]
        i = pl.program_id(0)
        for l, p_ref in enumerate(p_refs):
            @pl.when((i >= starts[l]) & (i < starts[l] + tiles[l]))
            def _():
                g = _sum_partials(p_ref)
                g_out[...] = g
                d_out[...], m_out[...], v_out[...] = _adamw_math(g, w_ref[...], m_ref[...], v_ref[...])

    def part_spec(l):
        return pl.BlockSpec((N_DEV, tr, c), lambda i: (0, jnp.clip(i - starts[l], 0, tiles[l] - 1) + parts[l][1] // tr, 0))

    tile = pl.BlockSpec((tr, c), lambda i: (i, 0))
    shape = jax.ShapeDtypeStruct((r, c), F32)
    return pl.pallas_call(
        body, name=name, grid=(r // tr,),
        in_specs=[part_spec(l) for l in range(len(parts))] + [tile, tile, tile],
        out_specs=[tile] * 4, out_shape=[shape] * 4,
        compiler_params=_params("parallel"),
    )(*[p[0] for p in parts], w, m, v)


def _adamw_small(parts, ws, ms, vs, name):
    n = len(ws)

    def body(*refs):
        ins, outs = refs[:4 * n], refs[4 * n:]
        for a in range(n):
            g = _sum_partials(ins[a])
            outs[4 * a][...] = g
            outs[4 * a + 1][...], outs[4 * a + 2][...], outs[4 * a + 3][...] = _adamw_math(
                g, ins[n + a][...], ins[2 * n + a][...], ins[3 * n + a][...])

    out_shape = []
    for w in ws:
        out_shape += [jax.ShapeDtypeStruct(w.shape, F32)] * 4
    return pl.pallas_call(body, name=name, out_shape=out_shape, compiler_params=_params())(*parts, *ws, *ms, *vs)


PACK_LANES = 128
PACK_ROWS = 8


def _pack(arrs):
    flat = jnp.concatenate([a.reshape(-1).astype(F32) for a in arrs])
    unit = PACK_LANES * PACK_ROWS
    total = -(-flat.shape[0] // unit) * unit
    return jnp.pad(flat, (0, total - flat.shape[0])).reshape(-1, PACK_LANES)


def _unpack(buf, shapes):
    flat = buf.reshape(N_DEV, -1)
    out, pos = [], 0
    for s in shapes:
        size = math.prod(s)
        out.append(flat[:, pos:pos + size].reshape((N_DEV,) + tuple(s)))
        pos += size
    return out


def _interleave(g):
    return jnp.transpose(g, (1, 0, 2)).reshape(g.shape[1], -1)


def _ffn_backward(dz, dzb, x_in, z_in, g_in, h, cg, cv, u, w_up_t, cw, w_down, tag, exchange=(), exchange_late=(), own_rows=0):
    du = _matmul(dzb, w_down, "nt", BF16, f"ffn{tag}_du", 1024, 1408, 1024)
    d_w_down = _matmul(u, dzb, "tn", BF16, f"ffn{tag}_dwdown", 1408, 1024, 2048)
    (dhg, dhv, dcwg, dcwv, dcbg, dcbv), moved = _ffn_mid_bwd(
        h, cg, cv, du, cw, f"ffn{tag}_mid_bwd", comm=_Comm(exchange=[d_w_down.reshape(N_DEV, -1, D_MODEL), *exchange]))
    d_w_up_t = _matmul_tn_pair(dhg, dhv, x_in, BF16, f"ffn{tag}_dwup", 1408, 1024, 1024,
                               comm=_Comm(exchange=exchange_late) if exchange_late else None)
    if exchange_late:
        d_w_up_t, late = d_w_up_t
        moved = moved + late
    d_up_blocks = d_w_up_t.reshape(N_DEV, -1, D_MODEL)
    outs = _matmul_ln_bwd([(dhg, 0), (dhv, D_FF)], w_up_t, z_in, g_in, dz, f"ffn{tag}_dx_ln_bwd", 256,
                          comm=_Comm(exchange=[(d_up_blocks, 0, own_rows)]) if own_rows else None)
    (dz_in, dzb_in, dg_in, db_in), own = outs if own_rows else (outs, [])
    moved = moved + own
    return (dz_in, dzb_in, dg_in, db_in, d_up_blocks,
            jnp.concatenate([dcwg, dcwv], axis=1), jnp.concatenate([dcbg, dcbv], axis=1), moved)


def kernel(x, ab_w_in, a_conv_w, a_conv_b, a_norm_g, a_norm_b, b_norm_g, b_norm_b, b_spatial_w, b_spatial_b, ab_w_out, c_w_qkv, c_b_qkv, c_sinks, c_w_o, ffn_w_up, ffn_conv_w, ffn_conv_b, ffn_w_down, ln_g, ln_b, loss_target, m_ab_w_in, m_a_conv_w, m_a_conv_b, m_a_norm_g, m_a_norm_b, m_b_norm_g, m_b_norm_b, m_b_spatial_w, m_b_spatial_b, m_ab_w_out, m_c_w_qkv, m_c_b_qkv, m_c_sinks, m_c_w_o, m_ffn_w_up, m_ffn_conv_w, m_ffn_conv_b, m_ffn_w_down, m_ln_g, m_ln_b, v_ab_w_in, v_a_conv_w, v_a_conv_b, v_a_norm_g, v_a_norm_b, v_b_norm_g, v_b_norm_b, v_b_spatial_w, v_b_spatial_b, v_ab_w_out, v_c_w_qkv, v_c_b_qkv, v_c_sinks, v_c_w_o, v_ffn_w_up, v_ffn_conv_w, v_ffn_conv_b, v_ffn_w_down, v_ln_g, v_ln_b):
    me = 4 * lax.axis_index("x") + 2 * lax.axis_index("y") + lax.axis_index("c")
    xt = x[0]
    t = xt.shape[0]

    small_shard_shapes = [a_conv_w.shape, c_b_qkv.shape, ffn_conv_w.shape, ln_g.shape, ln_b.shape]
    up_shard = [jnp.swapaxes(ffn_w_up[l], 0, 1).astype(BF16) for l in range(2)]
    qkv_shard = jnp.swapaxes(c_w_qkv[0], 0, 1).astype(BF16)
    down_shard = [ffn_w_down[l].astype(BF16) for l in range(2)]
    g_win, g_small = _comm_only(
        _Comm(gather=[jnp.swapaxes(ab_w_in[0], 0, 1).astype(BF16), _pack([a_conv_w, c_b_qkv, ffn_conv_w, ln_g, ln_b])]),
        "gather_first")
    w_in = g_win.reshape(-1, D_MODEL)
    g_acw, g_bqkv, g_fcw, g_lng, g_lnb = _unpack(g_small, small_shard_shapes)
    acw = _interleave(g_acw[:, 0])
    bqkv = g_bqkv[:, 0].reshape(1, -1)
    fcw = [_interleave(g_fcw[:, l]) for l in range(2)]
    lng = jnp.transpose(g_lng, (1, 2, 0, 3)).reshape(2, 2, 1, D_MODEL)
    lnb = jnp.transpose(g_lnb, (1, 2, 0, 3)).reshape(2, 2, 1, D_MODEL)
    fcb = [ffn_conv_b[l:l + 1] for l in range(2)]
    ms = b_spatial_w[0]
    mst = jnp.swapaxes(ms, 1, 2)
    sbt = b_spatial_b[0].T

    q_up = up_shard[0].shape[0] // 4
    h0, (g_wout, g_wup0) = _matmul(xt, w_in, "nt", BF16, "mix_in", 1024, 1024, 1024,
                                   comm=_Comm(gather=[ab_w_out[0].astype(BF16), (up_shard[0], 0, q_up, None)]))
    w_out = g_wout.reshape(D_MODEL, D_MODEL)
    (cat, y0), (g_wup0,) = _mixer_mid_fwd(h0, acw, a_conv_b, a_norm_g, a_norm_b, b_norm_g, b_norm_b, ms, sbt, "mix_mid_fwd",
                                          comm=_Comm(gather=[(up_shard[0], q_up, 2 * q_up, g_wup0)]))
    (z1, x1), (g_wup0,) = _matmul_res_ln(cat, w_out, xt, lng[0, 0], lnb[0, 0], "mix_out_ln", 512,
                                         comm=_Comm(gather=[(up_shard[0], 3 * q_up, q_up, g_wup0)]))
    w_up0 = g_wup0.reshape(2 * D_FF, D_MODEL)
    hf0, (g_wdown0, g_wqkv) = _matmul(x1, w_up0, "nt", BF16, "ffn0_up", 1024, 1408, 1024,
                                      comm=_Comm(gather=[down_shard[0], qkv_shard]))
    w_down0 = g_wdown0.reshape(D_FF, D_MODEL)
    w_qkv = g_wqkv.reshape(Q_WIDTH + KV_WIDTH, D_MODEL)
    (u0, cg0, cv0), (g_wup1,) = _ffn_mid_fwd(hf0, fcw[0], fcb[0], "ffn0_mid_fwd",
                                             comm=_Comm(gather=[(up_shard[1], 0, 3 * q_up, None)]))
    (z2, x2), (g_wo, g_wup1) = _matmul_res_ln(
        u0, w_down0, z1, lng[0, 1], lnb[0, 1], "ffn0_down_ln", 512, prev=(lng[0, 0], lnb[0, 0]),
        comm=_Comm(gather=[c_w_o[0].astype(BF16), (up_shard[1], 3 * q_up, q_up, g_wup1)]))
    w_o = g_wo.reshape(D_MODEL, D_MODEL)
    w_up1 = g_wup1.reshape(2 * D_FF, D_MODEL)
    qkv = _matmul(x2, w_qkv, "nt", BF16, "att_qkv", 1024, 1280, 1024, bias=bqkv)
    att, (g_wdown1,) = _attn_fwd(qkv, c_sinks, "att_fwd", comm=_Comm(gather=[down_shard[1]]))
    w_down1 = g_wdown1.reshape(D_FF, D_MODEL)
    z3, x3 = _matmul_res_ln(att, w_o, z2, lng[1, 0], lnb[1, 0], "att_out_ln", 512, prev=(lng[0, 1], lnb[0, 1]))
    hf1 = _matmul(x3, w_up1, "nt", BF16, "ffn1_up", 1024, 1408, 1024)
    u1, cg1, cv1 = _ffn_mid_fwd(hf1, fcw[1], fcb[1], "ffn1_mid_fwd")

    dz4, dz4b, dg11, db11, loss_terms = _matmul_res_ln_loss(u1, w_down1, z3, lng[1, 1], lnb[1, 1], loss_target[0],
                                                      "ffn1_down_ln_loss", 512, prev=(lng[1, 0], lnb[1, 0]))
    dz3, dz3b, dg10, db10, d_wup1, d_fcw1, d_fcb1, (p_wdown1,) = _ffn_backward(
        dz4, dz4b, x3, z3, lng[1, 0], hf1, cg1, cv1, u1, w_up1, fcw[1], w_down1, 1)
    d_att = _matmul(dz3b, w_o, "nt", BF16, "att_dout", 1024, 1024, 1024)
    d_wo = _matmul(att, dz3b, "tn", BF16, "att_dwo", 1024, 1024, 512)
    rows_up = d_wup1.shape[1]
    first = 3 * rows_up // 4
    (dq, dkv, dbq, dbkv, dsinks), (p_wup1a,) = _attn_bwd(qkv, d_att, c_sinks, "att_bwd",
                                                        comm=_Comm(exchange=[(d_wup1, 0, first)]))
    d_wqkv = jnp.concatenate([_matmul(dq, x2, "tn", BF16, "att_dwq", 1024, 1024, 1024),
                              _matmul(dkv, x2, "tn", BF16, "att_dwkv", KV_WIDTH, 1024, 1024)], axis=0)
    dz2, dz2b, dg01, db01 = _matmul_ln_bwd([(dq, 0), (dkv, Q_WIDTH)], w_qkv, z2, lng[0, 1], dz3, "att_dx_ln_bwd", 512)
    early = rows_up // 2
    dz1, dz1b, dg00, db00, d_wup0, d_fcw0, d_fcb0, (p_wdown0, p_wup1b, p_wqkv, p_wo, p_wup0a) = _ffn_backward(
        dz2, dz2b, x1, z1, lng[0, 0], hf0, cg0, cv0, u0, w_up0, fcw[0], w_down0, 0, exchange=[(d_wup1, first, rows_up - first)],
        exchange_late=[d_wqkv.reshape(N_DEV, -1, D_MODEL), d_wo.reshape(N_DEV, -1, D_MODEL)], own_rows=early)
    dcat = _matmul(dz1b, w_out, "nt", BF16, "mix_dcat", 1024, 1024, 1024)
    d_wout = _matmul(cat, dz1b, "tn", BF16, "mix_dwout", 1024, 1024, 512)
    (dh0, d_acw, d_acb, d_ang, d_anb, d_bng, d_bnb, d_ms, d_sb), (p_wup0b, p_wout) = _mixer_mid_bwd(
        h0, y0, dcat, acw, a_norm_g, a_norm_b, b_norm_g, b_norm_b, ms, mst, sbt, "mix_mid_bwd",
        comm=_Comm(exchange=[(d_wup0, early, rows_up - early), d_wout.reshape(N_DEV, -1, D_MODEL)]))
    d_bqkv = jnp.concatenate([dbq, dbkv], axis=1)
    d_lng = jnp.stack([jnp.stack([dg00, dg01]), jnp.stack([dg10, dg11])])
    d_lnb = jnp.stack([jnp.stack([db00, db01]), jnp.stack([db10, db11])])
    small_full = [d_acb, d_ang, d_anb, d_bng, d_bnb, d_ms, d_sb, dsinks[:, :N_Q_HEADS], jnp.concatenate([d_fcb0, d_fcb1], axis=0),
                  d_acw, d_bqkv, jnp.stack([d_fcw0, d_fcw1]), d_lng, d_lnb, loss_terms]
    d_win, (g_small_grads,) = _matmul(dh0, xt, "tn", BF16, "mix_dwin", 1024, 1024, 512, comm=_Comm(gather=[_pack(small_full)]))
    grad_x, (p_win,) = _matmul(dh0, w_in, "nn", F32, "mix_dx", 1024, 1024, 1024, res=dz1, res_scale=ALPHA,
                               comm=_Comm(exchange=[d_win.reshape(N_DEV, -1, D_MODEL)]))


    big = {}
    for nm, p, w, m, v, tr, transposed in [
            ("ab_w_in", [p_win], ab_w_in, m_ab_w_in, v_ab_w_in, 256, True),
            ("ab_w_out", [p_wout], ab_w_out, m_ab_w_out, v_ab_w_out, 128, False),
            ("c_w_qkv", [p_wqkv], c_w_qkv, m_c_w_qkv, v_c_w_qkv, 160, True), ("c_w_o", [p_wo], c_w_o, m_c_w_o, v_c_w_o, 128, False),
            ("ffn_w_up", [(p_wup0a, 0, early), (p_wup0b, early, rows_up - early), (p_wup1a, 0, first), (p_wup1b, first, rows_up - first)], ffn_w_up, m_ffn_w_up, v_ffn_w_up, 176, True),
            ("ffn_w_down", [p_wdown0, p_wdown1], ffn_w_down, m_ffn_w_down, v_ffn_w_down, 176, False)]:
        def two_d(a):
            a = jnp.swapaxes(a, 1, 2) if transposed else a
            return a.reshape(-1, a.shape[-1])

        def back(o):
            return jnp.swapaxes(o.reshape(w.shape[0], w.shape[2], w.shape[1]), 1, 2) if transposed else o.reshape(w.shape)

        outs = _adamw_big(p, two_d(w), two_d(m), two_d(v), "adamw_" + nm, tr)
        big[nm] = [back(o) for o in outs]

    *gs, loss_parts = _unpack(g_small_grads, [a.shape for a in small_full])
    loss = 0.5 / D_MODEL * jnp.sum(loss_parts)

    def my_shard(g, width):
        g = g.reshape(g.shape[:-1] + (N_DEV, width))
        return lax.dynamic_index_in_dim(g, me, axis=g.ndim - 2, keepdims=False)

    small_names = ["a_conv_b", "a_norm_g", "a_norm_b", "b_norm_g", "b_norm_b", "b_spatial_w", "b_spatial_b", "c_sinks", "ffn_conv_b",
                   "a_conv_w", "c_b_qkv", "ffn_conv_w", "ln_g", "ln_b"]
    small_w = [a_conv_b, a_norm_g, a_norm_b, b_norm_g, b_norm_b, b_spatial_w, b_spatial_b, c_sinks, ffn_conv_b,
               a_conv_w, c_b_qkv, ffn_conv_w, ln_g, ln_b]
    small_m = [m_a_conv_b, m_a_norm_g, m_a_norm_b, m_b_norm_g, m_b_norm_b, m_b_spatial_w, m_b_spatial_b, m_c_sinks, m_ffn_conv_b,
               m_a_conv_w, m_c_b_qkv, m_ffn_conv_w, m_ln_g, m_ln_b]
    small_v = [v_a_conv_b, v_a_norm_g, v_a_norm_b, v_b_norm_g, v_b_norm_b, v_b_spatial_w, v_b_spatial_b, v_c_sinks, v_ffn_conv_b,
               v_a_conv_w, v_c_b_qkv, v_ffn_conv_w, v_ln_g, v_ln_b]
    gs[9:] = [my_shard(g, w.shape[-1]) for g, w in zip(gs[9:], small_w[9:])]
    two_d = [(-1, w.shape[-1]) for w in small_w]
    outs = _adamw_small([g.reshape((N_DEV,) + w.reshape(s).shape) for g, w, s in zip(gs, small_w, two_d)],
                        [w.reshape(s) for w, s in zip(small_w, two_d)], [m.reshape(s) for m, s in zip(small_m, two_d)],
                        [v.reshape(s) for v, s in zip(small_v, two_d)], "adamw_small")
    small = {nm: [o.reshape(w.shape) for o in outs[4 * a:4 * a + 4]] for a, (nm, w) in enumerate(zip(small_names, small_w))}

    res = {**big, **small}
    order = ["ab_w_in", "a_conv_w", "a_conv_b", "a_norm_g", "a_norm_b", "b_norm_g", "b_norm_b", "b_spatial_w", "b_spatial_b", "ab_w_out",
             "c_w_qkv", "c_b_qkv", "c_sinks", "c_w_o", "ffn_w_up", "ffn_conv_w", "ffn_conv_b", "ffn_w_down", "ln_g", "ln_b"]
    return (loss, grad_x[None], *[res[nm][0] for nm in order], *[res[nm][1] for nm in order],
            *[res[nm][2] for nm in order], *[res[nm][3] for nm in order])
```

```python
import functools
import math

import jax
import jax.numpy as jnp
from jax import lax
from jax.experimental import pallas as pl
from jax.experimental.pallas import tpu as pltpu

F32 = jnp.float32
BF16 = jnp.bfloat16

N_DEV = 8
D_MODEL = 1024
A_WIDTH = 512
A_KERNEL = 31
B_GROUPS = 4
B_CHUNK = 128
HEAD_DIM = 64
N_Q_HEADS = 16
N_KV_HEADS = 2
ATT_BLOCK = 128
D_FF = 2816
FFN_KERNEL = 3
ALPHA = (2.0 * 2) ** 0.25
LN_EPS = 1e-5
GELU_K = math.sqrt(2.0 / math.pi)
GELU_C = 0.044715
ADAM_LR = 0.001
ADAM_B1 = 0.9
ADAM_B2 = 0.999
ADAM_EPS = 1e-08
ADAM_WD = 0.01
ADAM_STEP = 10
VMEM_LIMIT = 56 * 1024 * 1024
MESH_ID = pl.DeviceIdType.MESH


def _params(*sem):
    return pltpu.CompilerParams(dimension_semantics=sem, vmem_limit_bytes=VMEM_LIMIT)


def _gelu(x):
    t = jnp.tanh(GELU_K * x * (1.0 + GELU_C * x * x))
    return 0.5 * x * (1.0 + t)


def _gelu_and_grad(x):
    x2 = x * x
    t = jnp.tanh(GELU_K * x * (1.0 + GELU_C * x2))
    g = 0.5 * x * (1.0 + t)
    dg = 0.5 * (1.0 + t) + 0.5 * x * (1.0 - t * t) * (GELU_K * (1.0 + 3.0 * GELU_C * x2))
    return g, dg


def _sigmoid(x):
    return 1.0 / (1.0 + jnp.exp(-x))


def _ln_stats(z):
    mu = jnp.mean(z, axis=-1, keepdims=True)
    zc = z - mu
    var = jnp.mean(zc * zc, axis=-1, keepdims=True)
    r = lax.rsqrt(var + LN_EPS)
    return zc * r, r


def _ln_bwd_rows(dn, nh, r):
    return r * (dn - jnp.mean(dn, axis=-1, keepdims=True) - nh * jnp.mean(dn * nh, axis=-1, keepdims=True))


def _colsum(x):
    return jnp.sum(x, axis=0, keepdims=True)


def _dot(a, b, dims):
    return lax.dot_general(a.astype(BF16), b.astype(BF16), (dims, ((), ())), preferred_element_type=F32)


NN = ((1,), (0,))
NT = ((1,), (1,))
TN = ((0,), (0,))


ANY = pl.BlockSpec(memory_space=pl.ANY)
N_RELATIONS = N_DEV - 1


def _my_place():
    return lax.axis_index("x"), lax.axis_index("y"), lax.axis_index("c")


class _Comm:
    def __init__(self, gather=(), exchange=()):
        gather = [e if isinstance(e, tuple) else (e, 0, e.shape[0], None) for e in gather]
        exchange = [e if isinstance(e, tuple) else (e, 0, e.shape[1]) for e in exchange]
        self.arrs = [e[0] for e in gather] + [e[0] for e in exchange]
        self.n_gather = len(gather)
        self.n = len(self.arrs)
        self.rows = [pl.ds(lo, n) for _, lo, n, _ in gather] + [pl.ds(lo, n) for _, lo, n in exchange]
        self.into = {i: e[3] for i, e in enumerate(gather) if e[3] is not None}

    def out_shape(self):
        return [jax.ShapeDtypeStruct(((N_DEV,) + a.shape) if i < self.n_gather else a.shape, a.dtype)
                for i, a in enumerate(self.arrs)]

    def sems(self):
        return [pltpu.SemaphoreType.DMA((self.n, N_RELATIONS)), pltpu.SemaphoreType.DMA((self.n, N_RELATIONS)),
                pltpu.SemaphoreType.DMA((self.n,))]

    def _gather_copy(self, ins, outs, sems, a, k, place, to, from_input=False):
        px, py, pc = place
        block = outs[a].at[4 * px + 2 * py + pc, self.rows[a]]
        return pltpu.make_async_remote_copy(
            src_ref=ins[a].at[self.rows[a]] if from_input else block, dst_ref=block,
            send_sem=sems[0].at[a, k], recv_sem=sems[1].at[a, k], device_id=to, device_id_type=MESH_ID)

    def _exchange_copy(self, ins, outs, sems, a, k, landing=False):
        x, y, c = _my_place()
        me = 4 * x + 2 * y + c
        peer = (x ^ (k >> 2), y ^ ((k >> 1) & 1), c ^ (k & 1))
        return pltpu.make_async_remote_copy(
            src_ref=ins[a].at[me ^ k, self.rows[a]], dst_ref=outs[a].at[(me ^ k) if landing else me, self.rows[a]],
            send_sem=sems[0].at[a, k - 1], recv_sem=sems[1].at[a, k - 1], device_id=peer, device_id_type=MESH_ID)

    def _local_copy(self, ins, outs, sems, a):
        x, y, c = _my_place()
        me = 4 * x + 2 * y + c
        if a < self.n_gather:
            return pltpu.make_async_copy(ins[a].at[self.rows[a]], outs[a].at[me, self.rows[a]], sems[2].at[a])
        return pltpu.make_async_copy(ins[a].at[me, self.rows[a]], outs[a].at[me, self.rows[a]], sems[2].at[a])

    def _first_stage(self, ins, outs, sems, a):
        x, y, c = _my_place()
        me = (x, y, c)
        chips = [(1 - x, y), (x, 1 - y), (1 - x, 1 - y)]
        return ([self._gather_copy(ins, outs, sems, a, 0, me, (x, y, 1 - c), from_input=True)]
                + [self._gather_copy(ins, outs, sems, a, 1 + j, me, (*chip, c), from_input=True) for j, chip in enumerate(chips)])

    def start(self, ins, outs, sems):
        for a in range(self.n):
            self._local_copy(ins, outs, sems, a).start()
        for a in range(self.n_gather):
            for cp in self._first_stage(ins, outs, sems, a):
                cp.start()
        for k in range(1, N_DEV):
            for a in range(self.n_gather, self.n):
                self._exchange_copy(ins, outs, sems, a, k).start()

    def forward(self, ins, outs, sems):
        x, y, c = _my_place()
        me, sibling = (x, y, c), (x, y, 1 - c)
        for j, chip in enumerate([(1 - x, y), (x, 1 - y), (1 - x, 1 - y)]):
            for a in range(self.n_gather):
                self._gather_copy(ins, outs, sems, a, 1 + j, (*chip, c), me).wait_recv()
                self._gather_copy(ins, outs, sems, a, 4 + j, (*chip, c), sibling).start()

    def finish(self, ins, outs, sems):
        x, y, c = _my_place()
        me, sibling = (x, y, c), (x, y, 1 - c)
        chips = [(1 - x, y), (x, 1 - y), (1 - x, 1 - y)]
        passed = [self._gather_copy(ins, outs, sems, a, 4 + j, (*chip, c), sibling)
                  for j, chip in enumerate(chips) for a in range(self.n_gather)]
        for a in range(self.n_gather):
            self._gather_copy(ins, outs, sems, a, 0, sibling, me).wait_recv()
            for j, chip in enumerate(chips):
                self._gather_copy(ins, outs, sems, a, 4 + j, (*chip, 1 - c), me).wait_recv()
        for k in range(1, N_DEV):
            for a in range(self.n_gather, self.n):
                self._exchange_copy(ins, outs, sems, a, k, landing=True).wait_recv()
        for a in range(self.n_gather):
            for cp in self._first_stage(ins, outs, sems, a):
                cp.wait_send()
        for cp in passed:
            cp.wait_send()
        for k in range(1, N_DEV):
            for a in range(self.n_gather, self.n):
                self._exchange_copy(ins, outs, sems, a, k).wait_send()
        for a in range(self.n):
            self._local_copy(ins, outs, sems, a).wait()


def _comm_only(comm, name):
    assert not comm.into

    def body(*refs):
        ins, outs, sems = refs[:comm.n], refs[comm.n:2 * comm.n], refs[2 * comm.n:]
        comm.start(ins, outs, sems)
        comm.forward(ins, outs, sems)
        comm.finish(ins, outs, sems)

    return pl.pallas_call(body, name=name, in_specs=[ANY] * comm.n, out_specs=[ANY] * comm.n,
                          out_shape=comm.out_shape(), scratch_shapes=comm.sems())(*comm.arrs)


def _call(body, *, name, grid, in_specs, out_specs, out_shape, args, sem, scratch_shapes=(), comm=None):
    in_specs, out_specs, out_shape, scratch_shapes = list(in_specs), list(out_specs), list(out_shape), list(scratch_shapes)
    if comm is None:
        outs = pl.pallas_call(body, name=name, grid=grid, in_specs=in_specs, out_specs=out_specs, out_shape=out_shape,
                              scratch_shapes=scratch_shapes, compiler_params=_params(*sem))(*args)
        return list(outs), []
    n_in, n_out, n_scr, nc = len(in_specs), len(out_specs), len(scratch_shapes), comm.n
    completed = sorted(comm.into)

    def wrapped(*refs):
        ins, refs = refs[:n_in], refs[n_in:]
        c_in, refs = refs[:nc], refs[nc + len(completed):]
        outs, refs = refs[:n_out], refs[n_out:]
        c_out, refs = refs[:nc], refs[nc:]
        scr, sems = refs[:n_scr], refs[n_scr:]
        step = functools.reduce(lambda acc, ax: acc * grid[ax] + pl.program_id(ax), range(len(grid)), 0)
        steps = math.prod(grid)

        @pl.when(step == 0)
        def _():
            comm.start(c_in, c_out, sems)

        @pl.when(step == steps - 1)
        def _():
            comm.forward(c_in, c_out, sems)

        body(*ins, *outs, *scr)

        @pl.when(step == steps - 1)
        def _():
            comm.finish(c_in, c_out, sems)

    outs = pl.pallas_call(
        wrapped, name=name, grid=grid, in_specs=in_specs + [ANY] * (nc + len(completed)), out_specs=out_specs + [ANY] * nc,
        out_shape=out_shape + comm.out_shape(), scratch_shapes=scratch_shapes + comm.sems(),
        input_output_aliases={n_in + nc + pos: n_out + item for pos, item in enumerate(completed)},
        compiler_params=_params(*(["arbitrary"] * len(grid))))(*args, *comm.arrs, *[comm.into[item] for item in completed])
    return list(outs[:n_out]), list(outs[n_out:])


def _matmul(a, b, mode, out_dtype, name, tm, tn, tk, *, bias=None, res=None, res_scale=1.0, b_off=0, comm=None):
    tm = min(tm, a.shape[1] if mode == "tn" else a.shape[0])
    tk = min(tk, a.shape[0] if mode == "tn" else a.shape[1])
    if mode == "nn":
        (m, k), n = a.shape, b.shape[1]
        a_spec = pl.BlockSpec((tm, tk), lambda i, j, kk: (i, kk))
        b_spec = pl.BlockSpec((tk, tn), lambda i, j, kk: (kk + b_off, j))
        dims = NN
    elif mode == "nt":
        (m, k), n = a.shape, b.shape[0]
        a_spec = pl.BlockSpec((tm, tk), lambda i, j, kk: (i, kk))
        b_spec = pl.BlockSpec((tn, tk), lambda i, j, kk: (j, kk + b_off))
        dims = NT
    else:
        (k, m), n = a.shape, b.shape[1]
        a_spec = pl.BlockSpec((tk, tm), lambda i, j, kk: (kk, i))
        b_spec = pl.BlockSpec((tk, tn), lambda i, j, kk: (kk, j))
        dims = TN
    assert m % tm == 0 and n % tn == 0 and k % tk == 0, (name, m, n, k)
    nk = k // tk
    in_specs = [a_spec, b_spec]
    args = [a, b]
    if bias is not None:
        in_specs.append(pl.BlockSpec((1, tn), lambda i, j, kk: (0, j)))
        args.append(bias)
    if res is not None:
        in_specs.append(pl.BlockSpec((tm, tn), lambda i, j, kk: (i, j)))
        args.append(res)

    def finish(out, refs, o_ref):
        pos = 2
        if bias is not None:
            out = out + refs[pos][...]
            pos += 1
        if res is not None:
            out = out + res_scale * refs[pos][...].astype(F32)
        o_ref[...] = out.astype(out_dtype)

    def body_one_step(*refs):
        finish(_dot(refs[0][...], refs[1][...], dims), refs, refs[-1])

    def body(*refs):
        a_ref, b_ref = refs[0], refs[1]
        o_ref, acc = refs[-2], refs[-1]
        kk = pl.program_id(2)

        @pl.when(kk == 0)
        def _():
            acc[...] = jnp.zeros_like(acc)

        acc[...] += _dot(a_ref[...], b_ref[...], dims)

        @pl.when(kk == nk - 1)
        def _():
            finish(acc[...], refs, o_ref)

    (out,), moved = _call(
        body_one_step if nk == 1 else body, name=name, grid=(m // tm, n // tn, nk),
        in_specs=in_specs, out_specs=[pl.BlockSpec((tm, tn), lambda i, j, kk: (i, j))],
        out_shape=[jax.ShapeDtypeStruct((m, n), out_dtype)],
        scratch_shapes=[] if nk == 1 else [pltpu.VMEM((tm, tn), F32)],
        sem=("parallel", "parallel", "arbitrary"), args=args, comm=comm)
    return out if comm is None else (out, moved)


def _matmul_tn_pair(a0, a1, b, out_dtype, name, tm, tn, tk, comm=None):
    (k, m), n = a0.shape, b.shape[1]
    tk = min(tk, k)
    assert a1.shape == a0.shape and m % tm == 0 and n % tn == 0 and k % tk == 0, (name, m, n, k)
    mi, nk = m // tm, k // tk

    def body(a0_ref, a1_ref, b_ref, o_ref, acc):
        i, kk = pl.program_id(0), pl.program_id(2)

        @pl.when(kk == 0)
        def _():
            acc[...] = jnp.zeros_like(acc)

        @pl.when(i < mi)
        def _():
            acc[...] += _dot(a0_ref[...], b_ref[...], TN)

        @pl.when(i >= mi)
        def _():
            acc[...] += _dot(a1_ref[...], b_ref[...], TN)

        @pl.when(kk == nk - 1)
        def _():
            o_ref[...] = acc[...].astype(out_dtype)

    (out,), moved = _call(
        body, name=name, grid=(2 * mi, n // tn, nk),
        in_specs=[pl.BlockSpec((tk, tm), lambda i, j, kk: (jnp.where(i < mi, kk, nk - 1), jnp.minimum(i, mi - 1))),
                  pl.BlockSpec((tk, tm), lambda i, j, kk: (jnp.where(i >= mi, kk, 0), jnp.maximum(i - mi, 0))),
                  pl.BlockSpec((tk, tn), lambda i, j, kk: (kk, j))],
        out_specs=[pl.BlockSpec((tm, tn), lambda i, j, kk: (i, j))],
        out_shape=[jax.ShapeDtypeStruct((2 * m, n), out_dtype)],
        scratch_shapes=[pltpu.VMEM((tm, tn), F32)],
        sem=("parallel", "parallel", "arbitrary"), args=(a0, a1, b), comm=comm)
    return out if comm is None else (out, moved)


EPILOGUE_ROWS = 32


def _row_pieces(tm, piece):
    def step(c, carry):
        return piece(pl.ds(pl.multiple_of(c * EPILOGUE_ROWS, EPILOGUE_ROWS), EPILOGUE_ROWS), carry)
    return step


def _residual_input(x, prev):
    if not prev:
        return x
    nh, _ = _ln_stats(x)
    return nh * prev[0] + prev[1]


def _matmul_res_ln(a, b, x, g, beta, name, tm, prev=None, comm=None):
    t, k = a.shape
    d = b.shape[1]
    tm = min(tm, t)
    assert t % tm == 0
    n_prev = 0 if prev is None else 2

    def body(a_ref, b_ref, x_ref, g_ref, beta_ref, *rest):
        z_ref, xo_ref, acc = rest[n_prev:]
        acc[...] = _dot(a_ref[...], b_ref[...], NN)
        g, beta = g_ref[...], beta_ref[...]
        prev_vals = [r[...] for r in rest[:n_prev]]

        def piece(rows, carry):
            z = ALPHA * _residual_input(x_ref[rows, :], prev_vals) + acc[rows, :]
            nh, _ = _ln_stats(z)
            z_ref[rows, :] = z
            xo_ref[rows, :] = (nh * g + beta).astype(BF16)
            return carry

        lax.fori_loop(0, tm // EPILOGUE_ROWS, _row_pieces(tm, piece), 0)

    row = pl.BlockSpec((tm, d), lambda i: (i, 0))
    vec = pl.BlockSpec((1, d), lambda i: (0, 0))
    outs, moved = _call(
        body, name=name, grid=(t // tm,),
        in_specs=[pl.BlockSpec((tm, k), lambda i: (i, 0)), pl.BlockSpec((k, d), lambda i: (0, 0)), row, vec, vec] + [vec] * n_prev,
        out_specs=[row, row],
        out_shape=[jax.ShapeDtypeStruct((t, d), F32), jax.ShapeDtypeStruct((t, d), BF16)],
        scratch_shapes=[pltpu.VMEM((tm, d), F32)],
        sem=("parallel",), args=(a, b, x, g, beta, *(prev or ())), comm=comm)
    return outs if comm is None else (outs, moved)


def _matmul_ln_bwd(parts, b, z, g, dres, name, tm, comm=None):
    m = parts[0][0].shape[0]
    d = b.shape[1]
    tm = min(tm, m)
    n = len(parts)
    assert m % tm == 0 and all(row % a.shape[1] == 0 for a, row in parts)

    def body(*refs):
        z_ref, g_ref, dres_ref = refs[2 * n:2 * n + 3]
        dz_ref, dzb_ref, dg_ref, db_ref, acc = refs[-5:]

        @pl.when(pl.program_id(0) == 0)
        def _():
            dg_ref[...] = jnp.zeros_like(dg_ref)
            db_ref[...] = jnp.zeros_like(db_ref)

        acc[...] = _dot(refs[0][...], refs[n][...], NN)
        for p in range(1, n):
            acc[...] += _dot(refs[p][...], refs[n + p][...], NN)
        g = g_ref[...]

        def piece(rows, carry):
            dy = ALPHA * dres_ref[rows, :] + acc[rows, :]
            nh, r = _ln_stats(z_ref[rows, :])
            dz = _ln_bwd_rows(dy * g, nh, r)
            dz_ref[rows, :] = dz
            dzb_ref[rows, :] = dz.astype(BF16)
            return carry[0] + _colsum(dy * nh), carry[1] + _colsum(dy)

        zero = jnp.zeros((1, d), F32)
        dg, db = lax.fori_loop(0, tm // EPILOGUE_ROWS, _row_pieces(tm, piece), (zero, zero))
        dg_ref[...] += dg
        db_ref[...] += db

    def b_spec(a, row):
        blk = row // a.shape[1]
        return pl.BlockSpec((a.shape[1], d), lambda i: (blk, 0))

    row = pl.BlockSpec((tm, d), lambda i: (i, 0))
    vec = pl.BlockSpec((1, d), lambda i: (0, 0))
    vshape = jax.ShapeDtypeStruct((1, d), F32)
    outs, moved = _call(
        body, name=name, grid=(m // tm,),
        in_specs=[pl.BlockSpec((tm, a.shape[1]), lambda i: (i, 0)) for a, _ in parts] + [b_spec(a, r_) for a, r_ in parts]
        + [row, vec, row],
        out_specs=[row, row, vec, vec],
        out_shape=[jax.ShapeDtypeStruct((m, d), F32), jax.ShapeDtypeStruct((m, d), BF16), vshape, vshape],
        scratch_shapes=[pltpu.VMEM((tm, d), F32)],
        sem=("arbitrary",), args=(*[a for a, _ in parts], *([b] * n), z, g, dres), comm=comm)
    return outs if comm is None else (outs, moved)


def _matmul_res_ln_loss(a, b, x, g, beta, target, name, tm, prev):
    t, k = a.shape
    d = b.shape[1]
    tm = min(tm, t)

    def body(a_ref, b_ref, x_ref, g_ref, beta_ref, t_ref, gp_ref, bp_ref, dz_ref, dzb_ref, dg_ref, db_ref, loss_ref, acc):
        @pl.when(pl.program_id(0) == 0)
        def _():
            dg_ref[...] = jnp.zeros_like(dg_ref)
            db_ref[...] = jnp.zeros_like(db_ref)
            loss_ref[...] = jnp.zeros_like(loss_ref)

        acc[...] = _dot(a_ref[...], b_ref[...], NN)
        g, beta, prev_vals = g_ref[...], beta_ref[...], [gp_ref[...], bp_ref[...]]

        def piece(rows, carry):
            nh, r = _ln_stats(ALPHA * _residual_input(x_ref[rows, :], prev_vals) + acc[rows, :])
            err = nh * g + beta - t_ref[rows, :]
            dy = err * (1.0 / d)
            dz = _ln_bwd_rows(dy * g, nh, r)
            dz_ref[rows, :] = dz
            dzb_ref[rows, :] = dz.astype(BF16)
            return carry[0] + _colsum(dy * nh), carry[1] + _colsum(dy), carry[2] + _colsum(err * err)

        zero = jnp.zeros((1, d), F32)
        dg, db, loss = lax.fori_loop(0, tm // EPILOGUE_ROWS, _row_pieces(tm, piece), (zero, zero, zero))
        dg_ref[...] += dg
        db_ref[...] += db
        loss_ref[...] += loss

    row = pl.BlockSpec((tm, d), lambda i: (i, 0))
    vec = pl.BlockSpec((1, d), lambda i: (0, 0))
    vshape = jax.ShapeDtypeStruct((1, d), F32)
    return pl.pallas_call(
        body, name=name, grid=(t // tm,),
        in_specs=[pl.BlockSpec((tm, k), lambda i: (i, 0)), pl.BlockSpec((k, d), lambda i: (0, 0)), row, vec, vec, row, vec, vec],
        out_specs=[row, row, vec, vec, vec],
        out_shape=[jax.ShapeDtypeStruct((t, d), F32), jax.ShapeDtypeStruct((t, d), BF16), vshape, vshape, vshape],
        scratch_shapes=[pltpu.VMEM((tm, d), F32)],
        compiler_params=_params("arbitrary"),
    )(a, b, x, g, beta, target, *prev)


FFN_HALO = 16
FFN_CHUNK = 256
LANES = 128
SUBLANES = 8


def _rows_up(e, start, rows):
    if start % SUBLANES == 0:
        return e[start:start + rows]
    return pltpu.roll(e, e.shape[0] - start, 0)[0:rows]


def _fold(x):
    return jnp.sum(x.reshape(x.shape[0] // SUBLANES, SUBLANES, x.shape[1]), axis=0)


def _ffn_mid_fwd(h, cw, cb, name, tm=1024, tc=1408, comm=None):
    t, f2 = h.shape
    tm = min(tm, t)
    f = f2 // 2
    nj, nt, hb = f // tc, t // tm, tm // FFN_HALO

    ch = min(FFN_CHUNK, tm)

    def body(hg, hgp, hv, hvp, cwg, cwv, cbg, cbv, u_ref, cg_ref, cv_ref):
        i = pl.program_id(1)
        o = FFN_HALO - FFN_KERNEL + 1
        for lg in range(tc // LANES):
            cols = slice(lg * LANES, (lg + 1) * LANES)
            wg, wv = [cwg[k:k + 1, cols] for k in range(FFN_KERNEL)], [cwv[k:k + 1, cols] for k in range(FFN_KERNEL)]
            bg, bv = cbg[:, cols], cbv[:, cols]

            def emit(base, eg, ev):
                cg = wg[0] * _rows_up(eg, o, ch) + wg[1] * _rows_up(eg, o + 1, ch) + wg[2] * _rows_up(eg, o + 2, ch) + bg
                cv = wv[0] * _rows_up(ev, o, ch) + wv[1] * _rows_up(ev, o + 1, ch) + wv[2] * _rows_up(ev, o + 2, ch) + bv
                u_ref[pl.ds(base, ch), cols] = (_gelu(cg) * cv).astype(BF16)
                cg_ref[pl.ds(base, ch), cols] = cg.astype(BF16)
                cv_ref[pl.ds(base, ch), cols] = cv.astype(BF16)

            def first(main, prev):
                return jnp.concatenate([jnp.where(i > 0, prev[:, cols].astype(F32), 0.0), main[0:ch, cols].astype(F32)], axis=0)

            def inner(c, carry):
                base = pl.multiple_of(c * ch, ch)
                emit(base, hg[pl.ds(base - FFN_HALO, ch + FFN_HALO), cols].astype(F32),
                     hv[pl.ds(base - FFN_HALO, ch + FFN_HALO), cols].astype(F32))
                return carry

            emit(0, first(hg, hgp), first(hv, hvp))
            if tm > ch:
                lax.fori_loop(1, tm // ch, inner, 0)

    def main_spec(off):
        return pl.BlockSpec((tm, tc), lambda j, i: (i, j + off))

    def prev_spec(off):
        return pl.BlockSpec((FFN_HALO, tc), lambda j, i: (jnp.maximum(i * hb - 1, 0), j + off))

    def par_spec(rows, off):
        return pl.BlockSpec((rows, tc), lambda j, i: (0, j + off))

    outs, moved = _call(
        body, name=name, grid=(nj, nt),
        in_specs=[main_spec(0), prev_spec(0), main_spec(nj), prev_spec(nj),
                  par_spec(FFN_KERNEL, 0), par_spec(FFN_KERNEL, nj), par_spec(1, 0), par_spec(1, nj)],
        out_specs=[pl.BlockSpec((tm, tc), lambda j, i: (i, j))] * 3,
        out_shape=[jax.ShapeDtypeStruct((t, f), BF16)] * 3,
        sem=("parallel", "arbitrary"), args=(h, h, h, h, cw, cw, cb, cb), comm=comm)
    return outs if comm is None else (outs, moved)


def _ffn_mid_bwd(h, cg, cv, du, cw, name, tm=1024, tc=1408, comm=None):
    t, f2 = h.shape
    tm = min(tm, t)
    f = f2 // 2
    nj, nt, hb = f // tc, t // tm, tm // FFN_HALO

    ch = min(FFN_CHUNK, tm)
    ahead = ch + SUBLANES
    n_ch = tm // ch

    def body(hg, hv, cg_ref, cgn_ref, cv_ref, cvn_ref, du_ref, dun_ref, cwg, cwv,
             dhg_ref, dhv_ref, dcwg_ref, dcwv_ref, dcbg_ref, dcbv_ref):
        i = pl.program_id(1)

        @pl.when(i == 0)
        def _():
            for ref in (dcwg_ref, dcwv_ref, dcbg_ref, dcbv_ref):
                ref[...] = jnp.zeros_like(ref)

        for lg in range(tc // LANES):
            cols = slice(lg * LANES, (lg + 1) * LANES)
            wg, wv = [cwg[k:k + 1, cols] for k in range(FFN_KERNEL)], [cwv[k:k + 1, cols] for k in range(FFN_KERNEL)]

            def emit(base, cg_e, cv_e, du_e, acc):
                cg_a, cv_a, du_a = cg_e[0:ahead], cv_e[0:ahead], du_e[0:ahead]
                gl, dgl = _gelu_and_grad(cg_a)

                def back(d, h_ref, w, dh_ref):
                    later = [d[0:ch], _rows_up(d, 1, ch), _rows_up(d, 2, ch)]
                    dh_ref[pl.ds(base, ch), cols] = (w[2] * later[0] + w[1] * later[1] + w[0] * later[2]).astype(BF16)
                    h_own = h_ref[pl.ds(base, ch), cols].astype(F32)
                    return [_fold(later[0])] + [_fold(later[FFN_KERNEL - 1 - k] * h_own) for k in range(FFN_KERNEL)]

                sums = back(du_a * cv_a * dgl, hg, wg, dhg_ref) + back(du_a * gl, hv, wv, dhv_ref)
                return tuple(a + s_ for a, s_ in zip(acc, sums))

            def inner(c, acc):
                base = pl.multiple_of(c * ch, ch)
                rows = pl.ds(base, ch + FFN_HALO)
                return emit(base, cg_ref[rows, cols].astype(F32), cv_ref[rows, cols].astype(F32), du_ref[rows, cols].astype(F32), acc)

            def last(acc):
                def rows(main, after):
                    return jnp.concatenate([main[tm - ch:tm, cols].astype(F32), after], axis=0)

                du_next = jnp.where(i < nt - 1, dun_ref[:, cols].astype(F32), 0.0)
                return emit(tm - ch, rows(cg_ref, cgn_ref[:, cols].astype(F32)), rows(cv_ref, cvn_ref[:, cols].astype(F32)),
                            rows(du_ref, du_next), acc)

            acc = (jnp.zeros((SUBLANES, LANES), F32),) * (2 * (1 + FFN_KERNEL))
            if n_ch > 1:
                acc = lax.fori_loop(0, n_ch - 1, inner, acc)
            acc = last(acc)
            dcbg_ref[:, cols] += _colsum(acc[0])
            dcbv_ref[:, cols] += _colsum(acc[1 + FFN_KERNEL])
            for k in range(FFN_KERNEL):
                dcwg_ref[k:k + 1, cols] += _colsum(acc[1 + k])
                dcwv_ref[k:k + 1, cols] += _colsum(acc[2 + FFN_KERNEL + k])

    last_blk = t // FFN_HALO - 1

    def main_spec(off):
        return pl.BlockSpec((tm, tc), lambda j, i: (i, j + off))

    def next_spec(off):
        return pl.BlockSpec((FFN_HALO, tc), lambda j, i: (jnp.minimum((i + 1) * hb, last_blk), j + off))

    def par_spec(rows, off):
        return pl.BlockSpec((rows, tc), lambda j, i: (0, j + off))

    out_tile = pl.BlockSpec((tm, tc), lambda j, i: (i, j))
    outs, moved = _call(
        body, name=name, grid=(nj, nt),
        in_specs=[main_spec(0), main_spec(nj), main_spec(0), next_spec(0), main_spec(0), next_spec(0), main_spec(0), next_spec(0),
                  par_spec(FFN_KERNEL, 0), par_spec(FFN_KERNEL, nj)],
        out_specs=[out_tile, out_tile, par_spec(FFN_KERNEL, 0), par_spec(FFN_KERNEL, 0), par_spec(1, 0), par_spec(1, 0)],
        out_shape=[jax.ShapeDtypeStruct((t, f), BF16), jax.ShapeDtypeStruct((t, f), BF16),
                   jax.ShapeDtypeStruct((FFN_KERNEL, f), F32), jax.ShapeDtypeStruct((FFN_KERNEL, f), F32),
                   jax.ShapeDtypeStruct((1, f), F32), jax.ShapeDtypeStruct((1, f), F32)],
        sem=("parallel", "arbitrary"), args=(h, h, cg, cg, cv, cv, du, du, cw, cw), comm=comm)
    return outs if comm is None else (outs, moved)


MIX_HALO = 32


def _glu(hh):
    return hh[:, 0:A_WIDTH] * _sigmoid(hh[:, A_WIDTH:2 * A_WIDTH])


def _fill_row_shifts(s):
    rows = s.shape[1] - SUBLANES
    for j in range(1, SUBLANES):
        s[j, 0:rows, :] = s[0, pl.ds(j, rows), :]


def _rows_from(s, start, rows):
    j = start % SUBLANES
    return s[j, start - j:start - j + rows, :]


def _tril_mask():
    return lax.broadcasted_iota(jnp.int32, (B_CHUNK, B_CHUNK), 0) >= lax.broadcasted_iota(jnp.int32, (B_CHUNK, B_CHUNK), 1)


def _spatial_mix(q, ms_ref, sbt_ref, tm):
    mask = _tril_mask()
    ws = [jnp.where(mask, ms_ref[g], 0.0).astype(BF16) for g in range(B_GROUPS)]
    qb = q.astype(BF16)
    rows = []
    for c in range(tm // B_CHUNK):
        cols = [_dot(ws[g], qb[c * B_CHUNK:(c + 1) * B_CHUNK, g * 128:(g + 1) * 128], NN) + sbt_ref[:, g:g + 1]
                for g in range(B_GROUPS)]
        rows.append(jnp.concatenate(cols, axis=1))
    return jnp.concatenate(rows, axis=0)


def _mixer_mid_fwd(h, cw, cb, ag, ab, bg, bb, ms, sbt, name, tm=256, comm=None):
    t = h.shape[0]
    nt, hb = t // tm, tm // MIX_HALO
    o = MIX_HALO - A_KERNEL + 1

    def body(h_ref, hp_ref, cw_ref, cb_ref, ag_ref, ab_ref, bg_ref, bb_ref, ms_ref, sbt_ref, cat_ref, y_ref, sp):
        i = pl.program_id(0)
        sp[0, 0:MIX_HALO, :] = jnp.where(i > 0, _glu(hp_ref[:, 0:2 * A_WIDTH].astype(F32)), 0.0)
        sp[0, MIX_HALO:, :] = _glu(h_ref[:, 0:2 * A_WIDTH].astype(F32))
        _fill_row_shifts(sp)
        y = jnp.zeros((tm, A_WIDTH), F32) + cb_ref[...]
        for k in range(A_KERNEL):
            y = y + cw_ref[k:k + 1, :] * _rows_from(sp, o + k, tm)
        y_ref[...] = y.astype(BF16)
        nh, _ = _ln_stats(y)
        ln = nh * ag_ref[...] + ab_ref[...]
        cat_ref[:, 0:A_WIDTH] = (ln * _sigmoid(ln)).astype(BF16)
        u = _gelu(h_ref[:, 1024:1536].astype(F32))
        nb, _ = _ln_stats(_gelu(h_ref[:, 1536:2048].astype(F32)))
        mixed = _spatial_mix(nb * bg_ref[...] + bb_ref[...], ms_ref, sbt_ref, tm)
        cat_ref[:, A_WIDTH:] = (u * mixed).astype(BF16)

    vec = pl.BlockSpec((1, A_WIDTH), lambda i: (0, 0))
    outs, moved = _call(
        body, name=name, grid=(nt,),
        in_specs=[pl.BlockSpec((tm, 2048), lambda i: (i, 0)),
                  pl.BlockSpec((MIX_HALO, 2048), lambda i: (jnp.maximum(i * hb - 1, 0), 0)),
                  pl.BlockSpec((A_KERNEL, A_WIDTH), lambda i: (0, 0)), vec, vec, vec, vec, vec,
                  pl.BlockSpec((B_GROUPS, B_CHUNK, B_CHUNK), lambda i: (0, 0, 0)),
                  pl.BlockSpec((B_CHUNK, B_GROUPS), lambda i: (0, 0))],
        out_specs=[pl.BlockSpec((tm, D_MODEL), lambda i: (i, 0)), pl.BlockSpec((tm, A_WIDTH), lambda i: (i, 0))],
        out_shape=[jax.ShapeDtypeStruct((t, D_MODEL), BF16), jax.ShapeDtypeStruct((t, A_WIDTH), BF16)],
        scratch_shapes=[pltpu.VMEM((SUBLANES, tm + MIX_HALO, A_WIDTH), F32)],
        sem=("parallel",), args=(h, h, cw, cb, ag, ab, bg, bb, ms, sbt), comm=comm)
    return outs if comm is None else (outs, moved)


def _mixer_mid_bwd(h, y, dcat, cw, ag, ab, bg, bb, ms, mst, sbt, name, tm=256, comm=None):
    t = h.shape[0]
    nt, hb = t // tm, tm // MIX_HALO
    r = tm + MIX_HALO
    nchunk = tm // B_CHUNK

    def body(h_ref, y_ref, yn_ref, dc_ref, dcn_ref, cw_ref, ag_ref, ab_ref, bg_ref, bb_ref, ms_ref, mst_ref, sbt_ref,
             dh_ref, dcw_ref, dcb_ref, dag_ref, dab_ref, dbg_ref, dbb_ref, dms_ref, dsb_ref, sdy, sbacc):
        i = pl.program_id(0)

        @pl.when(i == 0)
        def _():
            for ref in (dcw_ref, dcb_ref, dag_ref, dab_ref, dbg_ref, dbb_ref, dms_ref, dsb_ref, sbacc):
                ref[...] = jnp.zeros_like(ref)

        nh, rs = _ln_stats(jnp.concatenate([y_ref[...].astype(F32), yn_ref[...].astype(F32)], axis=0))
        ln = nh * ag_ref[...] + ab_ref[...]
        sg = _sigmoid(ln)
        dao = jnp.concatenate([dc_ref[:, 0:A_WIDTH].astype(F32),
                               jnp.where(i < nt - 1, dcn_ref[:, 0:A_WIDTH].astype(F32), 0.0)], axis=0)
        dln = dao * (sg * (1.0 + ln * (1.0 - sg)))
        dag_ref[...] += _colsum(dln[0:tm] * nh[0:tm])
        dab_ref[...] += _colsum(dln[0:tm])
        sdy[0] = _ln_bwd_rows(dln * ag_ref[...], nh, rs)
        _fill_row_shifts(sdy)
        dcb_ref[...] += _colsum(sdy[0, 0:tm, :])
        av = h_ref[:, 0:A_WIDTH].astype(F32)
        s = _sigmoid(h_ref[:, A_WIDTH:2 * A_WIDTH].astype(F32))
        p_own = av * s
        dp = jnp.zeros((tm, A_WIDTH), F32)
        for k in range(A_KERNEL):
            later = _rows_from(sdy, A_KERNEL - 1 - k, tm)
            dcw_ref[k:k + 1, :] += _colsum(later * p_own)
            dp = dp + cw_ref[k:k + 1, :] * later
        dh_ref[:, 0:A_WIDTH] = (dp * s).astype(BF16)
        dh_ref[:, A_WIDTH:2 * A_WIDTH] = (dp * av * s * (1.0 - s)).astype(BF16)

        u, dgu = _gelu_and_grad(h_ref[:, 1024:1536].astype(F32))
        w, dgw = _gelu_and_grad(h_ref[:, 1536:2048].astype(F32))
        nb, rb = _ln_stats(w)
        q = nb * bg_ref[...] + bb_ref[...]
        mixed = _spatial_mix(q, ms_ref, sbt_ref, tm)
        dbo = dc_ref[:, A_WIDTH:].astype(F32)
        dh_ref[:, 1024:1536] = (dbo * mixed * dgu).astype(BF16)
        dmx = dbo * u
        mask = _tril_mask()
        wst = [jnp.where(mask.T, mst_ref[g], 0.0).astype(BF16) for g in range(B_GROUPS)]
        qb = q.astype(BF16)
        dmb = dmx.astype(BF16)
        rows = []
        for c in range(nchunk):
            cols = []
            for g in range(B_GROUPS):
                rs_, cs_ = slice(c * B_CHUNK, (c + 1) * B_CHUNK), slice(g * 128, (g + 1) * 128)
                sbacc[g] += dmx[rs_, cs_]
                dms_ref[g] += _dot(dmb[rs_, cs_], qb[rs_, cs_], NT)
                cols.append(_dot(wst[g], dmb[rs_, cs_], NN))
            rows.append(jnp.concatenate(cols, axis=1))
        dq = jnp.concatenate(rows, axis=0)
        dbg_ref[...] += _colsum(dq * nb)
        dbb_ref[...] += _colsum(dq)
        dh_ref[:, 1536:2048] = (_ln_bwd_rows(dq * bg_ref[...], nb, rb) * dgw).astype(BF16)

        @pl.when(i == nt - 1)
        def _():
            for g in range(B_GROUPS):
                dms_ref[g] = jnp.where(mask, dms_ref[g], 0.0)
                dsb_ref[g] = jnp.sum(sbacc[g], axis=1, keepdims=True)

    last_blk = t // MIX_HALO - 1
    vec = pl.BlockSpec((1, A_WIDTH), lambda i: (0, 0))
    mat = pl.BlockSpec((B_GROUPS, B_CHUNK, B_CHUNK), lambda i: (0, 0, 0))
    taps = pl.BlockSpec((A_KERNEL, A_WIDTH), lambda i: (0, 0))

    def halo(width):
        return pl.BlockSpec((MIX_HALO, width), lambda i: (jnp.minimum((i + 1) * hb, last_blk), 0))

    vshape = jax.ShapeDtypeStruct((1, A_WIDTH), F32)
    outs, moved = _call(
        body, name=name, grid=(nt,),
        in_specs=[pl.BlockSpec((tm, 2048), lambda i: (i, 0)), pl.BlockSpec((tm, A_WIDTH), lambda i: (i, 0)), halo(A_WIDTH),
                  pl.BlockSpec((tm, D_MODEL), lambda i: (i, 0)), halo(D_MODEL),
                  taps, vec, vec, vec, vec, mat, mat, pl.BlockSpec((B_CHUNK, B_GROUPS), lambda i: (0, 0))],
        out_specs=[pl.BlockSpec((tm, 2048), lambda i: (i, 0)), taps, vec, vec, vec, vec, vec, mat,
                   pl.BlockSpec((B_GROUPS, B_CHUNK, 1), lambda i: (0, 0, 0))],
        out_shape=[jax.ShapeDtypeStruct((t, 2048), BF16), jax.ShapeDtypeStruct((A_KERNEL, A_WIDTH), F32),
                   vshape, vshape, vshape, vshape, vshape,
                   jax.ShapeDtypeStruct((B_GROUPS, B_CHUNK, B_CHUNK), F32), jax.ShapeDtypeStruct((B_GROUPS, B_CHUNK, 1), F32)],
        scratch_shapes=[pltpu.VMEM((SUBLANES, r, A_WIDTH), F32), pltpu.VMEM((B_GROUPS, B_CHUNK, B_CHUNK), F32)],
        sem=("arbitrary",), args=(h, y, y, dcat, dcat, cw, ag, ab, bg, bb, ms, mst, sbt), comm=comm)
    return outs if comm is None else (outs, moved)


Q_WIDTH = N_Q_HEADS * HEAD_DIM
KV_WIDTH = 2 * N_KV_HEADS * HEAD_DIM
PAIRS_PER_KV = N_Q_HEADS // N_KV_HEADS // 2
ATT_SCALE = 1.0 / math.sqrt(HEAD_DIM)


def _dup_heads(pair_cols, kv_head):
    lane = lax.broadcasted_iota(jnp.int32, pair_cols.shape, 1)
    rolled = pltpu.roll(pair_cols, HEAD_DIM, 1)
    first = lane < HEAD_DIM
    return jnp.where(first, pair_cols, rolled) if kv_head == 0 else jnp.where(first, rolled, pair_cols)


HEADS_PER_KV = N_Q_HEADS // N_KV_HEADS


def _stack_heads(ref, kh):
    lane = lax.broadcasted_iota(jnp.int32, (ATT_BLOCK, 128), 1)
    rows = []
    for pr in range(PAIRS_PER_KV):
        c0 = (kh * PAIRS_PER_KV + pr) * 128
        pair = ref[:, c0:c0 + 128]
        rows += [jnp.where(lane < HEAD_DIM, pair, jnp.zeros_like(pair)), jnp.where(lane < HEAD_DIM, jnp.zeros_like(pair), pair)]
    return jnp.concatenate(rows, axis=0)


def _unstack_heads(stacked, kh, write):
    lane = lax.broadcasted_iota(jnp.int32, (ATT_BLOCK, 128), 1)
    for pr in range(PAIRS_PER_KV):
        first = stacked[(2 * pr) * ATT_BLOCK:(2 * pr + 1) * ATT_BLOCK]
        second = stacked[(2 * pr + 1) * ATT_BLOCK:(2 * pr + 2) * ATT_BLOCK]
        write((kh * PAIRS_PER_KV + pr) * 128, jnp.where(lane < HEAD_DIM, first, second))


def _sink_row(sink_ref, kh):
    return jnp.concatenate([jnp.full((1, ATT_BLOCK), sink_ref[0, kh * HEADS_PER_KV + h], F32) for h in range(HEADS_PER_KV)], axis=1)


def _att_window_bias():
    sj = lax.broadcasted_iota(jnp.int32, (2 * ATT_BLOCK, HEADS_PER_KV * ATT_BLOCK), 0)
    qi = lax.broadcasted_iota(jnp.int32, (2 * ATT_BLOCK, HEADS_PER_KV * ATT_BLOCK), 1) & (ATT_BLOCK - 1)
    diff = qi + ATT_BLOCK - sj
    return jnp.where((diff >= 0) & (diff < ATT_BLOCK), 0.0, -jnp.inf)


def _att_probs_t(q_all, k2, bias_ref, n, sink):
    st = _dot(k2, q_all, NT) * ATT_SCALE + bias_ref[...]
    st = jnp.concatenate([jnp.where(n > 0, st[0:ATT_BLOCK], -jnp.inf), st[ATT_BLOCK:]], axis=0)
    m = jnp.maximum(jnp.max(st, axis=0, keepdims=True), sink)
    e = jnp.exp(st - m)
    es = jnp.exp(sink - m)
    inv = 1.0 / (jnp.sum(e, axis=0, keepdims=True) + es)
    return e * inv, es * inv


def _attn_fwd(qkv, sinks, name, comm=None):
    t = qkv.shape[0]
    nb = t // ATT_BLOCK
    kvb = Q_WIDTH // KV_WIDTH

    def body(sink_ref, q_ref, kv_ref, kvp_ref, o_ref, bias):
        n = pl.program_id(0)

        @pl.when(n == 0)
        def _():
            bias[...] = _att_window_bias()

        kv = jnp.concatenate([kvp_ref[...], kv_ref[...]], axis=0).astype(F32)

        def write(c0, pair):
            o_ref[:, c0:c0 + 128] = pair.astype(BF16)

        for kh in range(N_KV_HEADS):
            k2 = _dup_heads(kv[:, 0:128], kh).astype(BF16)
            v2 = _dup_heads(kv[:, 128:256], kh).astype(BF16)
            pt, _ = _att_probs_t(_stack_heads(q_ref, kh), k2, bias, n, _sink_row(sink_ref, kh))
            _unstack_heads(_dot(v2, pt, TN).T, kh, write)

    (out,), moved = _call(
        body, name=name, grid=(nb,),
        in_specs=[pl.BlockSpec(memory_space=pltpu.SMEM),
                  pl.BlockSpec((ATT_BLOCK, Q_WIDTH), lambda n: (n, 0)),
                  pl.BlockSpec((ATT_BLOCK, KV_WIDTH), lambda n: (n, kvb)),
                  pl.BlockSpec((ATT_BLOCK, KV_WIDTH), lambda n: (jnp.maximum(n - 1, 0), kvb))],
        out_specs=[pl.BlockSpec((ATT_BLOCK, Q_WIDTH), lambda n: (n, 0))],
        out_shape=[jax.ShapeDtypeStruct((t, Q_WIDTH), BF16)],
        scratch_shapes=[pltpu.VMEM((2 * ATT_BLOCK, HEADS_PER_KV * ATT_BLOCK), F32)],
        sem=("arbitrary",), args=(sinks, qkv, qkv, qkv), comm=comm)
    return out if comm is None else (out, moved)


def _attn_bwd(qkv, d_o, sinks, name, comm=None):
    t = qkv.shape[0]
    nb = t // ATT_BLOCK
    kvb = Q_WIDTH // KV_WIDTH

    def body(sink_ref, q_ref, kv_ref, kvp_ref, do_ref, dq_ref, dkv_ref, dbq_ref, dbkv_ref, dsink_ref, carry, bias):
        n = pl.program_id(0)

        @pl.when(n == 0)
        def _():
            for ref in (dbq_ref, dbkv_ref, dsink_ref, carry):
                ref[...] = jnp.zeros_like(ref)
            dkv_ref[...] = jnp.zeros_like(dkv_ref)
            bias[...] = _att_window_bias()

        @pl.when(n < nb)
        def _():
            kv = jnp.concatenate([kvp_ref[...], kv_ref[...]], axis=0).astype(F32)
            lane2 = lax.broadcasted_iota(jnp.int32, (2 * ATT_BLOCK, 128), 1)
            sink_lane = lax.broadcasted_iota(jnp.int32, (1, 128), 1)
            dsink = jnp.zeros((1, 128), F32)
            dk_parts, dv_parts = [], []

            def write(c0, pair):
                dbq_ref[:, c0:c0 + 128] += _colsum(pair)
                dq_ref[:, c0:c0 + 128] = pair.astype(BF16)

            for kh in range(N_KV_HEADS):
                k2 = _dup_heads(kv[:, 0:128], kh).astype(BF16)
                v2 = _dup_heads(kv[:, 128:256], kh).astype(BF16)
                q_all = _stack_heads(q_ref, kh)
                do_all = _stack_heads(do_ref, kh)
                pt, ps = _att_probs_t(q_all, k2, bias, n, _sink_row(sink_ref, kh))
                dpt = _dot(v2, do_all, NT)
                delta = jnp.sum(pt * dpt, axis=0, keepdims=True)
                dst = pt * (dpt - delta) * ATT_SCALE
                psd = ps * delta
                for h in range(HEADS_PER_KV):
                    dsink = dsink + jnp.where(sink_lane == kh * HEADS_PER_KV + h,
                                              -jnp.sum(psd[:, h * ATT_BLOCK:(h + 1) * ATT_BLOCK]), 0.0)
                _unstack_heads(_dot(k2, dst, TN).T, kh, write)
                dk_acc = _dot(dst, q_all, NN)
                dv_acc = _dot(pt, do_all, NN)
                dk_parts.append(dk_acc + pltpu.roll(dk_acc, HEAD_DIM, 1))
                dv_parts.append(dv_acc + pltpu.roll(dv_acc, HEAD_DIM, 1))
            dk = jnp.where(lane2 < HEAD_DIM, dk_parts[0], dk_parts[1])
            dv = jnp.where(lane2 < HEAD_DIM, dv_parts[0], dv_parts[1])
            dkv_new = jnp.concatenate([dk, dv], axis=1)
            done = carry[...] + dkv_new[0:ATT_BLOCK]

            @pl.when(n > 0)
            def _():
                dkv_ref[...] = done.astype(BF16)
                dbkv_ref[...] += _colsum(done)

            carry[...] = dkv_new[ATT_BLOCK:]
            dsink_ref[...] += dsink

        @pl.when(n == nb)
        def _():
            dkv_ref[...] = carry[...].astype(BF16)
            dbkv_ref[...] += _colsum(carry[...])

    def clamp(n):
        return jnp.minimum(n, nb - 1)

    outs, moved = _call(
        body, name=name, grid=(nb + 1,),
        in_specs=[pl.BlockSpec(memory_space=pltpu.SMEM),
                  pl.BlockSpec((ATT_BLOCK, Q_WIDTH), lambda n: (clamp(n), 0)),
                  pl.BlockSpec((ATT_BLOCK, KV_WIDTH), lambda n: (clamp(n), kvb)),
                  pl.BlockSpec((ATT_BLOCK, KV_WIDTH), lambda n: (jnp.maximum(clamp(n) - 1, 0), kvb)),
                  pl.BlockSpec((ATT_BLOCK, Q_WIDTH), lambda n: (clamp(n), 0))],
        out_specs=[pl.BlockSpec((ATT_BLOCK, Q_WIDTH), lambda n: (clamp(n), 0)),
                   pl.BlockSpec((ATT_BLOCK, KV_WIDTH), lambda n: (jnp.maximum(n - 1, 0), 0)),
                   pl.BlockSpec((1, Q_WIDTH), lambda n: (0, 0)),
                   pl.BlockSpec((1, KV_WIDTH), lambda n: (0, 0)),
                   pl.BlockSpec((1, 128), lambda n: (0, 0))],
        out_shape=[jax.ShapeDtypeStruct((t, Q_WIDTH), BF16), jax.ShapeDtypeStruct((t, KV_WIDTH), BF16),
                   jax.ShapeDtypeStruct((1, Q_WIDTH), F32), jax.ShapeDtypeStruct((1, KV_WIDTH), F32),
                   jax.ShapeDtypeStruct((1, 128), F32)],
        scratch_shapes=[pltpu.VMEM((ATT_BLOCK, KV_WIDTH), F32), pltpu.VMEM((2 * ATT_BLOCK, HEADS_PER_KV * ATT_BLOCK), F32)],
        sem=("arbitrary",), args=(sinks, qkv, qkv, qkv, d_o), comm=comm)
    return outs if comm is None else (outs, moved)


def _adamw_math(g, w, m, v):
    m = ADAM_B1 * m + (1.0 - ADAM_B1) * g
    v = ADAM_B2 * v + (1.0 - ADAM_B2) * (g * g)
    m_hat = m / (1.0 - ADAM_B1 ** ADAM_STEP)
    v_hat = v / (1.0 - ADAM_B2 ** ADAM_STEP)
    delta = -ADAM_LR * (m_hat / (jnp.sqrt(v_hat) + ADAM_EPS) + ADAM_WD * w)
    return delta, m, v


def _sum_partials(p_ref):
    g = p_ref[0].astype(F32)
    for s in range(1, N_DEV):
        g = g + p_ref[s].astype(F32)
    return g


def _adamw_big(parts, w, m, v, name, tr):
    r, c = w.shape
    parts = [p if isinstance(p, tuple) else (p, 0, p.shape[1]) for p in parts]
    tiles = [rows // tr for _, _, rows in parts]
    starts = [sum(tiles[:l]) for l in range(len(parts))]
    assert all(lo % tr == 0 and rows % tr == 0 for _, lo, rows in parts) and sum(tiles) * tr == r

    def body(*refs):
        p_refs, (w_ref, m_ref, v_ref, g_out, d_out, m_out, v_out) = refs[:len(parts)], refs[len(parts):]
        i = pl.program_id(0)
        for l, p_ref in enumerate(p_refs):
            @pl.when((i >= starts[l]) & (i < starts[l] + tiles[l]))
            def _():
                g = _sum_partials(p_ref)
                g_out[...] = g
                d_out[...], m_out[...], v_out[...] = _adamw_math(g, w_ref[...], m_ref[...], v_ref[...])

    def part_spec(l):
        return pl.BlockSpec((N_DEV, tr, c), lambda i: (0, jnp.clip(i - starts[l], 0, tiles[l] - 1) + parts[l][1] // tr, 0))

    tile = pl.BlockSpec((tr, c), lambda i: (i, 0))
    shape = jax.ShapeDtypeStruct((r, c), F32)
    return pl.pallas_call(
        body, name=name, grid=(r // tr,),
        in_specs=[part_spec(l) for l in range(len(parts))] + [tile, tile, tile],
        out_specs=[tile] * 4, out_shape=[shape] * 4,
        compiler_params=_params("parallel"),
    )(*[p[0] for p in parts], w, m, v)


def _adamw_small(parts, ws, ms, vs, name):
    n = len(ws)

    def body(*refs):
        ins, outs = refs[:4 * n], refs[4 * n:]
        for a in range(n):
            g = _sum_partials(ins[a])
            outs[4 * a][...] = g
            outs[4 * a + 1][...], outs[4 * a + 2][...], outs[4 * a + 3][...] = _adamw_math(
                g, ins[n + a][...], ins[2 * n + a][...], ins[3 * n + a][...])

    out_shape = []
    for w in ws:
        out_shape += [jax.ShapeDtypeStruct(w.shape, F32)] * 4
    return pl.pallas_call(body, name=name, out_shape=out_shape, compiler_params=_params())(*parts, *ws, *ms, *vs)


PACK_LANES = 128
PACK_ROWS = 8


def _pack(arrs):
    flat = jnp.concatenate([a.reshape(-1).astype(F32) for a in arrs])
    unit = PACK_LANES * PACK_ROWS
    total = -(-flat.shape[0] // unit) * unit
    return jnp.pad(flat, (0, total - flat.shape[0])).reshape(-1, PACK_LANES)


def _unpack(buf, shapes):
    flat = buf.reshape(N_DEV, -1)
    out, pos = [], 0
    for s in shapes:
        size = math.prod(s)
        out.append(flat[:, pos:pos + size].reshape((N_DEV,) + tuple(s)))
        pos += size
    return out


def _interleave(g):
    return jnp.transpose(g, (1, 0, 2)).reshape(g.shape[1], -1)


def _ffn_backward(dz, dzb, x_in, z_in, g_in, h, cg, cv, u, w_up_t, cw, w_down, tag, exchange=(), exchange_late=(), own_rows=0):
    du = _matmul(dzb, w_down, "nt", BF16, f"ffn{tag}_du", 1024, 1408, 1024)
    d_w_down = _matmul(u, dzb, "tn", BF16, f"ffn{tag}_dwdown", 1408, 1024, 2048)
    (dhg, dhv, dcwg, dcwv, dcbg, dcbv), moved = _ffn_mid_bwd(
        h, cg, cv, du, cw, f"ffn{tag}_mid_bwd", comm=_Comm(exchange=[d_w_down.reshape(N_DEV, -1, D_MODEL), *exchange]))
    d_w_up_t = _matmul_tn_pair(dhg, dhv, x_in, BF16, f"ffn{tag}_dwup", 1408, 1024, 1024,
                               comm=_Comm(exchange=exchange_late) if exchange_late else None)
    if exchange_late:
        d_w_up_t, late = d_w_up_t
        moved = moved + late
    d_up_blocks = d_w_up_t.reshape(N_DEV, -1, D_MODEL)
    outs = _matmul_ln_bwd([(dhg, 0), (dhv, D_FF)], w_up_t, z_in, g_in, dz, f"ffn{tag}_dx_ln_bwd", 256,
                          comm=_Comm(exchange=[(d_up_blocks, 0, own_rows)]) if own_rows else None)
    (dz_in, dzb_in, dg_in, db_in), own = outs if own_rows else (outs, [])
    moved = moved + own
    return (dz_in, dzb_in, dg_in, db_in, d_up_blocks,
            jnp.concatenate([dcwg, dcwv], axis=1), jnp.concatenate([dcbg, dcbv], axis=1), moved)


def kernel(x, ab_w_in, a_conv_w, a_conv_b, a_norm_g, a_norm_b, b_norm_g, b_norm_b, b_spatial_w, b_spatial_b, ab_w_out, c_w_qkv, c_b_qkv, c_sinks, c_w_o, ffn_w_up, ffn_conv_w, ffn_conv_b, ffn_w_down, ln_g, ln_b, loss_target, m_ab_w_in, m_a_conv_w, m_a_conv_b, m_a_norm_g, m_a_norm_b, m_b_norm_g, m_b_norm_b, m_b_spatial_w, m_b_spatial_b, m_ab_w_out, m_c_w_qkv, m_c_b_qkv, m_c_sinks, m_c_w_o, m_ffn_w_up, m_ffn_conv_w, m_ffn_conv_b, m_ffn_w_down, m_ln_g, m_ln_b, v_ab_w_in, v_a_conv_w, v_a_conv_b, v_a_norm_g, v_a_norm_b, v_b_norm_g, v_b_norm_b, v_b_spatial_w, v_b_spatial_b, v_ab_w_out, v_c_w_qkv, v_c_b_qkv, v_c_sinks, v_c_w_o, v_ffn_w_up, v_ffn_conv_w, v_ffn_conv_b, v_ffn_w_down, v_ln_g, v_ln_b):
    me = 4 * lax.axis_index("x") + 2 * lax.axis_index("y") + lax.axis_index("c")
    xt = x[0]
    t = xt.shape[0]

    small_shard_shapes = [a_conv_w.shape, c_b_qkv.shape, ffn_conv_w.shape, ln_g.shape, ln_b.shape]
    up_shard = [jnp.swapaxes(ffn_w_up[l], 0, 1).astype(BF16) for l in range(2)]
    qkv_shard = jnp.swapaxes(c_w_qkv[0], 0, 1).astype(BF16)
    down_shard = [ffn_w_down[l].astype(BF16) for l in range(2)]
    g_win, g_small = _comm_only(
        _Comm(gather=[jnp.swapaxes(ab_w_in[0], 0, 1).astype(BF16), _pack([a_conv_w, c_b_qkv, ffn_conv_w, ln_g, ln_b])]),
        "gather_first")
    w_in = g_win.reshape(-1, D_MODEL)
    g_acw, g_bqkv, g_fcw, g_lng, g_lnb = _unpack(g_small, small_shard_shapes)
    acw = _interleave(g_acw[:, 0])
    bqkv = g_bqkv[:, 0].reshape(1, -1)
    fcw = [_interleave(g_fcw[:, l]) for l in range(2)]
    lng = jnp.transpose(g_lng, (1, 2, 0, 3)).reshape(2, 2, 1, D_MODEL)
    lnb = jnp.transpose(g_lnb, (1, 2, 0, 3)).reshape(2, 2, 1, D_MODEL)
    fcb = [ffn_conv_b[l:l + 1] for l in range(2)]
    ms = b_spatial_w[0]
    mst = jnp.swapaxes(ms, 1, 2)
    sbt = b_spatial_b[0].T

    q_up = up_shard[0].shape[0] // 4
    h0, (g_wout, g_wup0) = _matmul(xt, w_in, "nt", BF16, "mix_in", 1024, 1024, 1024,
                                   comm=_Comm(gather=[ab_w_out[0].astype(BF16), (up_shard[0], 0, q_up, None)]))
    w_out = g_wout.reshape(D_MODEL, D_MODEL)
    (cat, y0), (g_wup0,) = _mixer_mid_fwd(h0, acw, a_conv_b, a_norm_g, a_norm_b, b_norm_g, b_norm_b, ms, sbt, "mix_mid_fwd",
                                          comm=_Comm(gather=[(up_shard[0], q_up, 2 * q_up, g_wup0)]))
    (z1, x1), (g_wup0,) = _matmul_res_ln(cat, w_out, xt, lng[0, 0], lnb[0, 0], "mix_out_ln", 512,
                                         comm=_Comm(gather=[(up_shard[0], 3 * q_up, q_up, g_wup0)]))
    w_up0 = g_wup0.reshape(2 * D_FF, D_MODEL)
    hf0, (g_wdown0, g_wqkv) = _matmul(x1, w_up0, "nt", BF16, "ffn0_up", 1024, 1408, 1024,
                                      comm=_Comm(gather=[down_shard[0], qkv_shard]))
    w_down0 = g_wdown0.reshape(D_FF, D_MODEL)
    w_qkv = g_wqkv.reshape(Q_WIDTH + KV_WIDTH, D_MODEL)
    (u0, cg0, cv0), (g_wup1,) = _ffn_mid_fwd(hf0, fcw[0], fcb[0], "ffn0_mid_fwd",
                                             comm=_Comm(gather=[(up_shard[1], 0, 3 * q_up, None)]))
    (z2, x2), (g_wo, g_wup1) = _matmul_res_ln(
        u0, w_down0, z1, lng[0, 1], lnb[0, 1], "ffn0_down_ln", 512, prev=(lng[0, 0], lnb[0, 0]),
        comm=_Comm(gather=[c_w_o[0].astype(BF16), (up_shard[1], 3 * q_up, q_up, g_wup1)]))
    w_o = g_wo.reshape(D_MODEL, D_MODEL)
    w_up1 = g_wup1.reshape(2 * D_FF, D_MODEL)
    qkv = _matmul(x2, w_qkv, "nt", BF16, "att_qkv", 1024, 1280, 1024, bias=bqkv)
    att, (g_wdown1,) = _attn_fwd(qkv, c_sinks, "att_fwd", comm=_Comm(gather=[down_shard[1]]))
    w_down1 = g_wdown1.reshape(D_FF, D_MODEL)
    z3, x3 = _matmul_res_ln(att, w_o, z2, lng[1, 0], lnb[1, 0], "att_out_ln", 512, prev=(lng[0, 1], lnb[0, 1]))
    hf1 = _matmul(x3, w_up1, "nt", BF16, "ffn1_up", 1024, 1408, 1024)
    u1, cg1, cv1 = _ffn_mid_fwd(hf1, fcw[1], fcb[1], "ffn1_mid_fwd")

    dz4, dz4b, dg11, db11, loss_terms = _matmul_res_ln_loss(u1, w_down1, z3, lng[1, 1], lnb[1, 1], loss_target[0],
                                                      "ffn1_down_ln_loss", 512, prev=(lng[1, 0], lnb[1, 0]))
    dz3, dz3b, dg10, db10, d_wup1, d_fcw1, d_fcb1, (p_wdown1,) = _ffn_backward(
        dz4, dz4b, x3, z3, lng[1, 0], hf1, cg1, cv1, u1, w_up1, fcw[1], w_down1, 1)
    d_att = _matmul(dz3b, w_o, "nt", BF16, "att_dout", 1024, 1024, 1024)
    d_wo = _matmul(att, dz3b, "tn", BF16, "att_dwo", 1024, 1024, 512)
    rows_up = d_wup1.shape[1]
    first = 3 * rows_up // 4
    (dq, dkv, dbq, dbkv, dsinks), (p_wup1a,) = _attn_bwd(qkv, d_att, c_sinks, "att_bwd",
                                                        comm=_Comm(exchange=[(d_wup1, 0, first)]))
    d_wqkv = jnp.concatenate([_matmul(dq, x2, "tn", BF16, "att_dwq", 1024, 1024, 1024),
                              _matmul(dkv, x2, "tn", BF16, "att_dwkv", KV_WIDTH, 1024, 1024)], axis=0)
    dz2, dz2b, dg01, db01 = _matmul_ln_bwd([(dq, 0), (dkv, Q_WIDTH)], w_qkv, z2, lng[0, 1], dz3, "att_dx_ln_bwd", 512)
    early = rows_up // 2
    dz1, dz1b, dg00, db00, d_wup0, d_fcw0, d_fcb0, (p_wdown0, p_wup1b, p_wqkv, p_wo, p_wup0a) = _ffn_backward(
        dz2, dz2b, x1, z1, lng[0, 0], hf0, cg0, cv0, u0, w_up0, fcw[0], w_down0, 0, exchange=[(d_wup1, first, rows_up - first)],
        exchange_late=[d_wqkv.reshape(N_DEV, -1, D_MODEL), d_wo.reshape(N_DEV, -1, D_MODEL)], own_rows=early)
    dcat = _matmul(dz1b, w_out, "nt", BF16, "mix_dcat", 1024, 1024, 1024)
    d_wout = _matmul(cat, dz1b, "tn", BF16, "mix_dwout", 1024, 1024, 512)
    (dh0, d_acw, d_acb, d_ang, d_anb, d_bng, d_bnb, d_ms, d_sb), (p_wup0b, p_wout) = _mixer_mid_bwd(
        h0, y0, dcat, acw, a_norm_g, a_norm_b, b_norm_g, b_norm_b, ms, mst, sbt, "mix_mid_bwd",
        comm=_Comm(exchange=[(d_wup0, early, rows_up - early), d_wout.reshape(N_DEV, -1, D_MODEL)]))
    d_bqkv = jnp.concatenate([dbq, dbkv], axis=1)
    d_lng = jnp.stack([jnp.stack([dg00, dg01]), jnp.stack([dg10, dg11])])
    d_lnb = jnp.stack([jnp.stack([db00, db01]), jnp.stack([db10, db11])])
    small_full = [d_acb, d_ang, d_anb, d_bng, d_bnb, d_ms, d_sb, dsinks[:, :N_Q_HEADS], jnp.concatenate([d_fcb0, d_fcb1], axis=0),
                  d_acw, d_bqkv, jnp.stack([d_fcw0, d_fcw1]), d_lng, d_lnb, loss_terms]
    d_win, (g_small_grads,) = _matmul(dh0, xt, "tn", BF16, "mix_dwin", 1024, 1024, 512, comm=_Comm(gather=[_pack(small_full)]))
    grad_x, (p_win,) = _matmul(dh0, w_in, "nn", F32, "mix_dx", 1024, 1024, 1024, res=dz1, res_scale=ALPHA,
                               comm=_Comm(exchange=[d_win.reshape(N_DEV, -1, D_MODEL)]))


    big = {}
    for nm, p, w, m, v, tr, transposed in [
            ("ab_w_in", [p_win], ab_w_in, m_ab_w_in, v_ab_w_in, 256, True),
            ("ab_w_out", [p_wout], ab_w_out, m_ab_w_out, v_ab_w_out, 128, False),
            ("c_w_qkv", [p_wqkv], c_w_qkv, m_c_w_qkv, v_c_w_qkv, 160, True), ("c_w_o", [p_wo], c_w_o, m_c_w_o, v_c_w_o, 128, False),
            ("ffn_w_up", [(p_wup0a, 0, early), (p_wup0b, early, rows_up - early), (p_wup1a, 0, first), (p_wup1b, first, rows_up - first)], ffn_w_up, m_ffn_w_up, v_ffn_w_up, 176, True),
            ("ffn_w_down", [p_wdown0, p_wdown1], ffn_w_down, m_ffn_w_down, v_ffn_w_down, 176, False)]:
        def two_d(a):
            a = jnp.swapaxes(a, 1, 2) if transposed else a
            return a.reshape(-1, a.shape[-1])

        def back(o):
            return jnp.swapaxes(o.reshape(w.shape[0], w.shape[2], w.shape[1]), 1, 2) if transposed else o.reshape(w.shape)

        outs = _adamw_big(p, two_d(w), two_d(m), two_d(v), "adamw_" + nm, tr)
        big[nm] = [back(o) for o in outs]

    *gs, loss_parts = _unpack(g_small_grads, [a.shape for a in small_full])
    loss = 0.5 / D_MODEL * jnp.sum(loss_parts)

    def my_shard(g, width):
        g = g.reshape(g.shape[:-1] + (N_DEV, width))
        return lax.dynamic_index_in_dim(g, me, axis=g.ndim - 2, keepdims=False)

    small_names = ["a_conv_b", "a_norm_g", "a_norm_b", "b_norm_g", "b_norm_b", "b_spatial_w", "b_spatial_b", "c_sinks", "ffn_conv_b",
                   "a_conv_w", "c_b_qkv", "ffn_conv_w", "ln_g", "ln_b"]
    small_w = [a_conv_b, a_norm_g, a_norm_b, b_norm_g, b_norm_b, b_spatial_w, b_spatial_b, c_sinks, ffn_conv_b,
               a_conv_w, c_b_qkv, ffn_conv_w, ln_g, ln_b]
    small_m = [m_a_conv_b, m_a_norm_g, m_a_norm_b, m_b_norm_g, m_b_norm_b, m_b_spatial_w, m_b_spatial_b, m_c_sinks, m_ffn_conv_b,
               m_a_conv_w, m_c_b_qkv, m_ffn_conv_w, m_ln_g, m_ln_b]
    small_v = [v_a_conv_b, v_a_norm_g, v_a_norm_b, v_b_norm_g, v_b_norm_b, v_b_spatial_w, v_b_spatial_b, v_c_sinks, v_ffn_conv_b,
               v_a_conv_w, v_c_b_qkv, v_ffn_conv_w, v_ln_g, v_ln_b]
    gs[9:] = [my_shard(g, w.shape[-1]) for g, w in zip(gs[9:], small_w[9:])]
    two_d = [(-1, w.shape[-1]) for w in small_w]
    outs = _adamw_small([g.reshape((N_DEV,) + w.reshape(s).shape) for g, w, s in zip(gs, small_w, two_d)],
                        [w.reshape(s) for w, s in zip(small_w, two_d)], [m.reshape(s) for m, s in zip(small_m, two_d)],
                        [v.reshape(s) for v, s in zip(small_v, two_d)], "adamw_small")
    small = {nm: [o.reshape(w.shape) for o in outs[4 * a:4 * a + 4]] for a, (nm, w) in enumerate(zip(small_names, small_w))}

    res = {**big, **small}
    order = ["ab_w_in", "a_conv_w", "a_conv_b", "a_norm_g", "a_norm_b", "b_norm_g", "b_norm_b", "b_spatial_w", "b_spatial_b", "ab_w_out",
             "c_w_qkv", "c_b_qkv", "c_sinks", "c_w_o", "ffn_w_up", "ffn_conv_w", "ffn_conv_b", "ffn_w_down", "ln_g", "ln_b"]
    return (loss, grad_x[None], *[res[nm][0] for nm in order], *[res[nm][1] for nm in order],
            *[res[nm][2] for nm in order], *[res[nm][3] for nm in order])
```

```python
import functools
import math

import jax
import jax.numpy as jnp
from jax import lax
from jax.experimental import pallas as pl
from jax.experimental.pallas import tpu as pltpu

F32 = jnp.float32
BF16 = jnp.bfloat16

N_DEV = 8
D_MODEL = 1024
A_WIDTH = 512
A_KERNEL = 31
B_GROUPS = 4
B_CHUNK = 128
HEAD_DIM = 64
N_Q_HEADS = 16
N_KV_HEADS = 2
ATT_BLOCK = 128
D_FF = 2816
FFN_KERNEL = 3
ALPHA = (2.0 * 2) ** 0.25
LN_EPS = 1e-5
GELU_K = math.sqrt(2.0 / math.pi)
GELU_C = 0.044715
ADAM_LR = 0.001
ADAM_B1 = 0.9
ADAM_B2 = 0.999
ADAM_EPS = 1e-08
ADAM_WD = 0.01
ADAM_STEP = 10
VMEM_LIMIT = 56 * 1024 * 1024
MESH_ID = pl.DeviceIdType.MESH


def _params(*sem):
    return pltpu.CompilerParams(dimension_semantics=sem, vmem_limit_bytes=VMEM_LIMIT)


def _gelu(x):
    t = jnp.tanh(GELU_K * x * (1.0 + GELU_C * x * x))
    return 0.5 * x * (1.0 + t)


def _gelu_and_grad(x):
    x2 = x * x
    t = jnp.tanh(GELU_K * x * (1.0 + GELU_C * x2))
    g = 0.5 * x * (1.0 + t)
    dg = 0.5 * (1.0 + t) + 0.5 * x * (1.0 - t * t) * (GELU_K * (1.0 + 3.0 * GELU_C * x2))
    return g, dg


def _sigmoid(x):
    return 1.0 / (1.0 + jnp.exp(-x))


def _ln_stats(z):
    mu = jnp.mean(z, axis=-1, keepdims=True)
    zc = z - mu
    var = jnp.mean(zc * zc, axis=-1, keepdims=True)
    r = lax.rsqrt(var + LN_EPS)
    return zc * r, r


def _ln_bwd_rows(dn, nh, r):
    return r * (dn - jnp.mean(dn, axis=-1, keepdims=True) - nh * jnp.mean(dn * nh, axis=-1, keepdims=True))


def _colsum(x):
    return jnp.sum(x, axis=0, keepdims=True)


def _dot(a, b, dims):
    return lax.dot_general(a.astype(BF16), b.astype(BF16), (dims, ((), ())), preferred_element_type=F32)


NN = ((1,), (0,))
NT = ((1,), (1,))
TN = ((0,), (0,))


ANY = pl.BlockSpec(memory_space=pl.ANY)
N_RELATIONS = N_DEV - 1


def _my_place():
    return lax.axis_index("x"), lax.axis_index("y"), lax.axis_index("c")


class _Comm:
    def __init__(self, gather=(), exchange=()):
        gather = [e if isinstance(e, tuple) else (e, 0, e.shape[0], None) for e in gather]
        exchange = [e if isinstance(e, tuple) else (e, 0, e.shape[1]) for e in exchange]
        self.arrs = [e[0] for e in gather] + [e[0] for e in exchange]
        self.n_gather = len(gather)
        self.n = len(self.arrs)
        self.rows = [pl.ds(lo, n) for _, lo, n, _ in gather] + [pl.ds(lo, n) for _, lo, n in exchange]
        self.into = {i: e[3] for i, e in enumerate(gather) if e[3] is not None}

    def out_shape(self):
        return [jax.ShapeDtypeStruct(((N_DEV,) + a.shape) if i < self.n_gather else a.shape, a.dtype)
                for i, a in enumerate(self.arrs)]

    def sems(self):
        return [pltpu.SemaphoreType.DMA((self.n, N_RELATIONS)), pltpu.SemaphoreType.DMA((self.n, N_RELATIONS)),
                pltpu.SemaphoreType.DMA((self.n,))]

    def _gather_copy(self, ins, outs, sems, a, k, place, to, from_input=False):
        px, py, pc = place
        block = outs[a].at[4 * px + 2 * py + pc, self.rows[a]]
        return pltpu.make_async_remote_copy(
            src_ref=ins[a].at[self.rows[a]] if from_input else block, dst_ref=block,
            send_sem=sems[0].at[a, k], recv_sem=sems[1].at[a, k], device_id=to, device_id_type=MESH_ID)

    def _exchange_copy(self, ins, outs, sems, a, k, landing=False):
        x, y, c = _my_place()
        me = 4 * x + 2 * y + c
        peer = (x ^ (k >> 2), y ^ ((k >> 1) & 1), c ^ (k & 1))
        return pltpu.make_async_remote_copy(
            src_ref=ins[a].at[me ^ k, self.rows[a]], dst_ref=outs[a].at[(me ^ k) if landing else me, self.rows[a]],
            send_sem=sems[0].at[a, k - 1], recv_sem=sems[1].at[a, k - 1], device_id=peer, device_id_type=MESH_ID)

    def _local_copy(self, ins, outs, sems, a):
        x, y, c = _my_place()
        me = 4 * x + 2 * y + c
        if a < self.n_gather:
            return pltpu.make_async_copy(ins[a].at[self.rows[a]], outs[a].at[me, self.rows[a]], sems[2].at[a])
        return pltpu.make_async_copy(ins[a].at[me, self.rows[a]], outs[a].at[me, self.rows[a]], sems[2].at[a])

    def _first_stage(self, ins, outs, sems, a):
        x, y, c = _my_place()
        me = (x, y, c)
        chips = [(1 - x, y), (x, 1 - y), (1 - x, 1 - y)]
        return ([self._gather_copy(ins, outs, sems, a, 0, me, (x, y, 1 - c), from_input=True)]
                + [self._gather_copy(ins, outs, sems, a, 1 + j, me, (*chip, c), from_input=True) for j, chip in enumerate(chips)])

    def start(self, ins, outs, sems):
        for a in range(self.n):
            self._local_copy(ins, outs, sems, a).start()
        for a in range(self.n_gather):
            for cp in self._first_stage(ins, outs, sems, a):
                cp.start()
        for k in range(1, N_DEV):
            for a in range(self.n_gather, self.n):
                self._exchange_copy(ins, outs, sems, a, k).start()

    def forward(self, ins, outs, sems):
        x, y, c = _my_place()
        me, sibling = (x, y, c), (x, y, 1 - c)
        for j, chip in enumerate([(1 - x, y), (x, 1 - y), (1 - x, 1 - y)]):
            for a in range(self.n_gather):
                self._gather_copy(ins, outs, sems, a, 1 + j, (*chip, c), me).wait_recv()
                self._gather_copy(ins, outs, sems, a, 4 + j, (*chip, c), sibling).start()

    def finish(self, ins, outs, sems):
        x, y, c = _my_place()
        me, sibling = (x, y, c), (x, y, 1 - c)
        chips = [(1 - x, y), (x, 1 - y), (1 - x, 1 - y)]
        passed = [self._gather_copy(ins, outs, sems, a, 4 + j, (*chip, c), sibling)
                  for j, chip in enumerate(chips) for a in range(self.n_gather)]
        for a in range(self.n_gather):
            self._gather_copy(ins, outs, sems, a, 0, sibling, me).wait_recv()
            for j, chip in enumerate(chips):
                self._gather_copy(ins, outs, sems, a, 4 + j, (*chip, 1 - c), me).wait_recv()
        for k in range(1, N_DEV):
            for a in range(self.n_gather, self.n):
                self._exchange_copy(ins, outs, sems, a, k, landing=True).wait_recv()
        for a in range(self.n_gather):
            for cp in self._first_stage(ins, outs, sems, a):
                cp.wait_send()
        for cp in passed:
            cp.wait_send()
        for k in range(1, N_DEV):
            for a in range(self.n_gather, self.n):
                self._exchange_copy(ins, outs, sems, a, k).wait_send()
        for a in range(self.n):
            self._local_copy(ins, outs, sems, a).wait()


def _comm_only(comm, name):
    assert not comm.into

    def body(*refs):
        ins, outs, sems = refs[:comm.n], refs[comm.n:2 * comm.n], refs[2 * comm.n:]
        comm.start(ins, outs, sems)
        comm.forward(ins, outs, sems)
        comm.finish(ins, outs, sems)

    return pl.pallas_call(body, name=name, in_specs=[ANY] * comm.n, out_specs=[ANY] * comm.n,
                          out_shape=comm.out_shape(), scratch_shapes=comm.sems())(*comm.arrs)


def _call(body, *, name, grid, in_specs, out_specs, out_shape, args, sem, scratch_shapes=(), comm=None):
    in_specs, out_specs, out_shape, scratch_shapes = list(in_specs), list(out_specs), list(out_shape), list(scratch_shapes)
    if comm is None:
        outs = pl.pallas_call(body, name=name, grid=grid, in_specs=in_specs, out_specs=out_specs, out_shape=out_shape,
                              scratch_shapes=scratch_shapes, compiler_params=_params(*sem))(*args)
        return list(outs), []
    n_in, n_out, n_scr, nc = len(in_specs), len(out_specs), len(scratch_shapes), comm.n
    completed = sorted(comm.into)

    def wrapped(*refs):
        ins, refs = refs[:n_in], refs[n_in:]
        c_in, refs = refs[:nc], refs[nc + len(completed):]
        outs, refs = refs[:n_out], refs[n_out:]
        c_out, refs = refs[:nc], refs[nc:]
        scr, sems = refs[:n_scr], refs[n_scr:]
        step = functools.reduce(lambda acc, ax: acc * grid[ax] + pl.program_id(ax), range(len(grid)), 0)
        steps = math.prod(grid)

        @pl.when(step == 0)
        def _():
            comm.start(c_in, c_out, sems)

        @pl.when(step == steps - 1)
        def _():
            comm.forward(c_in, c_out, sems)

        body(*ins, *outs, *scr)

        @pl.when(step == steps - 1)
        def _():
            comm.finish(c_in, c_out, sems)

    outs = pl.pallas_call(
        wrapped, name=name, grid=grid, in_specs=in_specs + [ANY] * (nc + len(completed)), out_specs=out_specs + [ANY] * nc,
        out_shape=out_shape + comm.out_shape(), scratch_shapes=scratch_shapes + comm.sems(),
        input_output_aliases={n_in + nc + pos: n_out + item for pos, item in enumerate(completed)},
        compiler_params=_params(*(["arbitrary"] * len(grid))))(*args, *comm.arrs, *[comm.into[item] for item in completed])
    return list(outs[:n_out]), list(outs[n_out:])


def _matmul(a, b, mode, out_dtype, name, tm, tn, tk, *, bias=None, res=None, res_scale=1.0, b_off=0, comm=None):
    tm = min(tm, a.shape[1] if mode == "tn" else a.shape[0])
    tk = min(tk, a.shape[0] if mode == "tn" else a.shape[1])
    if mode == "nn":
        (m, k), n = a.shape, b.shape[1]
        a_spec = pl.BlockSpec((tm, tk), lambda i, j, kk: (i, kk))
        b_spec = pl.BlockSpec((tk, tn), lambda i, j, kk: (kk + b_off, j))
        dims = NN
    elif mode == "nt":
        (m, k), n = a.shape, b.shape[0]
        a_spec = pl.BlockSpec((tm, tk), lambda i, j, kk: (i, kk))
        b_spec = pl.BlockSpec((tn, tk), lambda i, j, kk: (j, kk + b_off))
        dims = NT
    else:
        (k, m), n = a.shape, b.shape[1]
        a_spec = pl.BlockSpec((tk, tm), lambda i, j, kk: (kk, i))
        b_spec = pl.BlockSpec((tk, tn), lambda i, j, kk: (kk, j))
        dims = TN
    assert m % tm == 0 and n % tn == 0 and k % tk == 0, (name, m, n, k)
    nk = k // tk
    in_specs = [a_spec, b_spec]
    args = [a, b]
    if bias is not None:
        in_specs.append(pl.BlockSpec((1, tn), lambda i, j, kk: (0, j)))
        args.append(bias)
    if res is not None:
        in_specs.append(pl.BlockSpec((tm, tn), lambda i, j, kk: (i, j)))
        args.append(res)

    def finish(out, refs, o_ref):
        pos = 2
        if bias is not None:
            out = out + refs[pos][...]
            pos += 1
        if res is not None:
            out = out + res_scale * refs[pos][...].astype(F32)
        o_ref[...] = out.astype(out_dtype)

    def body_one_step(*refs):
        finish(_dot(refs[0][...], refs[1][...], dims), refs, refs[-1])

    def body(*refs):
        a_ref, b_ref = refs[0], refs[1]
        o_ref, acc = refs[-2], refs[-1]
        kk = pl.program_id(2)

        @pl.when(kk == 0)
        def _():
            acc[...] = jnp.zeros_like(acc)

        acc[...] += _dot(a_ref[...], b_ref[...], dims)

        @pl.when(kk == nk - 1)
        def _():
            finish(acc[...], refs, o_ref)

    (out,), moved = _call(
        body_one_step if nk == 1 else body, name=name, grid=(m // tm, n // tn, nk),
        in_specs=in_specs, out_specs=[pl.BlockSpec((tm, tn), lambda i, j, kk: (i, j))],
        out_shape=[jax.ShapeDtypeStruct((m, n), out_dtype)],
        scratch_shapes=[] if nk == 1 else [pltpu.VMEM((tm, tn), F32)],
        sem=("parallel", "parallel", "arbitrary"), args=args, comm=comm)
    return out if comm is None else (out, moved)


def _matmul_tn_pair(a0, a1, b, out_dtype, name, tm, tn, tk, comm=None):
    (k, m), n = a0.shape, b.shape[1]
    tk = min(tk, k)
    assert a1.shape == a0.shape and m % tm == 0 and n % tn == 0 and k % tk == 0, (name, m, n, k)
    mi, nk = m // tm, k // tk

    def body(a0_ref, a1_ref, b_ref, o_ref, acc):
        i, kk = pl.program_id(0), pl.program_id(2)

        @pl.when(kk == 0)
        def _():
            acc[...] = jnp.zeros_like(acc)

        @pl.when(i < mi)
        def _():
            acc[...] += _dot(a0_ref[...], b_ref[...], TN)

        @pl.when(i >= mi)
        def _():
            acc[...] += _dot(a1_ref[...], b_ref[...], TN)

        @pl.when(kk == nk - 1)
        def _():
            o_ref[...] = acc[...].astype(out_dtype)

    (out,), moved = _call(
        body, name=name, grid=(2 * mi, n // tn, nk),
        in_specs=[pl.BlockSpec((tk, tm), lambda i, j, kk: (jnp.where(i < mi, kk, nk - 1), jnp.minimum(i, mi - 1))),
                  pl.BlockSpec((tk, tm), lambda i, j, kk: (jnp.where(i >= mi, kk, 0), jnp.maximum(i - mi, 0))),
                  pl.BlockSpec((tk, tn), lambda i, j, kk: (kk, j))],
        out_specs=[pl.BlockSpec((tm, tn), lambda i, j, kk: (i, j))],
        out_shape=[jax.ShapeDtypeStruct((2 * m, n), out_dtype)],
        scratch_shapes=[pltpu.VMEM((tm, tn), F32)],
        sem=("parallel", "parallel", "arbitrary"), args=(a0, a1, b), comm=comm)
    return out if comm is None else (out, moved)


def _residual_input(x_ref, prev_refs):
    if not prev_refs:
        return x_ref[...]
    nh, _ = _ln_stats(x_ref[...])
    return nh * prev_refs[0][...] + prev_refs[1][...]


def _matmul_res_ln(a, b, x, g, beta, name, tm, prev=None, comm=None):
    t, k = a.shape
    d = b.shape[1]
    tm = min(tm, t)
    assert t % tm == 0
    n_prev = 0 if prev is None else 2

    def body(a_ref, b_ref, x_ref, g_ref, beta_ref, *rest):
        z_ref, xo_ref = rest[n_prev:]
        z = ALPHA * _residual_input(x_ref, rest[:n_prev]) + _dot(a_ref[...], b_ref[...], NN)
        nh, _ = _ln_stats(z)
        z_ref[...] = z
        xo_ref[...] = (nh * g_ref[...] + beta_ref[...]).astype(BF16)

    row = pl.BlockSpec((tm, d), lambda i: (i, 0))
    vec = pl.BlockSpec((1, d), lambda i: (0, 0))
    outs, moved = _call(
        body, name=name, grid=(t // tm,),
        in_specs=[pl.BlockSpec((tm, k), lambda i: (i, 0)), pl.BlockSpec((k, d), lambda i: (0, 0)), row, vec, vec] + [vec] * n_prev,
        out_specs=[row, row],
        out_shape=[jax.ShapeDtypeStruct((t, d), F32), jax.ShapeDtypeStruct((t, d), BF16)],
        sem=("parallel",), args=(a, b, x, g, beta, *(prev or ())), comm=comm)
    return outs if comm is None else (outs, moved)


def _matmul_ln_bwd(parts, b, z, g, dres, name, tm, comm=None):
    m = parts[0][0].shape[0]
    d = b.shape[1]
    tm = min(tm, m)
    n = len(parts)
    assert m % tm == 0 and all(row % a.shape[1] == 0 for a, row in parts)

    def body(*refs):
        z_ref, g_ref, dres_ref = refs[2 * n:2 * n + 3]
        dz_ref, dzb_ref, dg_ref, db_ref = refs[-4:]

        @pl.when(pl.program_id(0) == 0)
        def _():
            dg_ref[...] = jnp.zeros_like(dg_ref)
            db_ref[...] = jnp.zeros_like(db_ref)

        dy = ALPHA * dres_ref[...]
        for p in range(n):
            dy = dy + _dot(refs[p][...], refs[n + p][...], NN)
        nh, r = _ln_stats(z_ref[...])
        dg_ref[...] += _colsum(dy * nh)
        db_ref[...] += _colsum(dy)
        dz = _ln_bwd_rows(dy * g_ref[...], nh, r)
        dz_ref[...] = dz
        dzb_ref[...] = dz.astype(BF16)

    def b_spec(a, row):
        blk = row // a.shape[1]
        return pl.BlockSpec((a.shape[1], d), lambda i: (blk, 0))

    row = pl.BlockSpec((tm, d), lambda i: (i, 0))
    vec = pl.BlockSpec((1, d), lambda i: (0, 0))
    vshape = jax.ShapeDtypeStruct((1, d), F32)
    outs, moved = _call(
        body, name=name, grid=(m // tm,),
        in_specs=[pl.BlockSpec((tm, a.shape[1]), lambda i: (i, 0)) for a, _ in parts] + [b_spec(a, r_) for a, r_ in parts]
        + [row, vec, row],
        out_specs=[row, row, vec, vec],
        out_shape=[jax.ShapeDtypeStruct((m, d), F32), jax.ShapeDtypeStruct((m, d), BF16), vshape, vshape],
        sem=("arbitrary",), args=(*[a for a, _ in parts], *([b] * n), z, g, dres), comm=comm)
    return outs if comm is None else (outs, moved)


def _matmul_res_ln_loss(a, b, x, g, beta, target, name, tm, prev):
    t, k = a.shape
    d = b.shape[1]
    tm = min(tm, t)

    def body(a_ref, b_ref, x_ref, g_ref, beta_ref, t_ref, gp_ref, bp_ref, dz_ref, dzb_ref, dg_ref, db_ref, loss_ref):
        @pl.when(pl.program_id(0) == 0)
        def _():
            dg_ref[...] = jnp.zeros_like(dg_ref)
            db_ref[...] = jnp.zeros_like(db_ref)
            loss_ref[...] = jnp.zeros_like(loss_ref)

        nh, r = _ln_stats(ALPHA * _residual_input(x_ref, (gp_ref, bp_ref)) + _dot(a_ref[...], b_ref[...], NN))
        err = nh * g_ref[...] + beta_ref[...] - t_ref[...]
        loss_ref[...] += _colsum(err * err)
        dy = err * (1.0 / d)
        dg_ref[...] += _colsum(dy * nh)
        db_ref[...] += _colsum(dy)
        dz = _ln_bwd_rows(dy * g_ref[...], nh, r)
        dz_ref[...] = dz
        dzb_ref[...] = dz.astype(BF16)

    row = pl.BlockSpec((tm, d), lambda i: (i, 0))
    vec = pl.BlockSpec((1, d), lambda i: (0, 0))
    vshape = jax.ShapeDtypeStruct((1, d), F32)
    return pl.pallas_call(
        body, name=name, grid=(t // tm,),
        in_specs=[pl.BlockSpec((tm, k), lambda i: (i, 0)), pl.BlockSpec((k, d), lambda i: (0, 0)), row, vec, vec, row, vec, vec],
        out_specs=[row, row, vec, vec, vec],
        out_shape=[jax.ShapeDtypeStruct((t, d), F32), jax.ShapeDtypeStruct((t, d), BF16), vshape, vshape, vshape],
        compiler_params=_params("arbitrary"),
    )(a, b, x, g, beta, target, *prev)


FFN_HALO = 16
FFN_CHUNK = 256
LANES = 128
SUBLANES = 8


def _rows_up(e, start, rows):
    if start % SUBLANES == 0:
        return e[start:start + rows]
    return pltpu.roll(e, e.shape[0] - start, 0)[0:rows]


def _fold(x):
    return jnp.sum(x.reshape(x.shape[0] // SUBLANES, SUBLANES, x.shape[1]), axis=0)


def _ffn_mid_fwd(h, cw, cb, name, tm=1024, tc=1408, comm=None):
    t, f2 = h.shape
    tm = min(tm, t)
    f = f2 // 2
    nj, nt, hb = f // tc, t // tm, tm // FFN_HALO

    ch = min(FFN_CHUNK, tm)

    def body(hg, hgp, hv, hvp, cwg, cwv, cbg, cbv, u_ref, cg_ref, cv_ref):
        i = pl.program_id(1)
        o = FFN_HALO - FFN_KERNEL + 1
        for lg in range(tc // LANES):
            cols = slice(lg * LANES, (lg + 1) * LANES)
            wg, wv = [cwg[k:k + 1, cols] for k in range(FFN_KERNEL)], [cwv[k:k + 1, cols] for k in range(FFN_KERNEL)]
            bg, bv = cbg[:, cols], cbv[:, cols]

            def emit(base, eg, ev):
                cg = wg[0] * _rows_up(eg, o, ch) + wg[1] * _rows_up(eg, o + 1, ch) + wg[2] * _rows_up(eg, o + 2, ch) + bg
                cv = wv[0] * _rows_up(ev, o, ch) + wv[1] * _rows_up(ev, o + 1, ch) + wv[2] * _rows_up(ev, o + 2, ch) + bv
                u_ref[pl.ds(base, ch), cols] = (_gelu(cg) * cv).astype(BF16)
                cg_ref[pl.ds(base, ch), cols] = cg.astype(BF16)
                cv_ref[pl.ds(base, ch), cols] = cv.astype(BF16)

            def first(main, prev):
                return jnp.concatenate([jnp.where(i > 0, prev[:, cols].astype(F32), 0.0), main[0:ch, cols].astype(F32)], axis=0)

            def inner(c, carry):
                base = pl.multiple_of(c * ch, ch)
                emit(base, hg[pl.ds(base - FFN_HALO, ch + FFN_HALO), cols].astype(F32),
                     hv[pl.ds(base - FFN_HALO, ch + FFN_HALO), cols].astype(F32))
                return carry

            emit(0, first(hg, hgp), first(hv, hvp))
            if tm > ch:
                lax.fori_loop(1, tm // ch, inner, 0)

    def main_spec(off):
        return pl.BlockSpec((tm, tc), lambda j, i: (i, j + off))

    def prev_spec(off):
        return pl.BlockSpec((FFN_HALO, tc), lambda j, i: (jnp.maximum(i * hb - 1, 0), j + off))

    def par_spec(rows, off):
        return pl.BlockSpec((rows, tc), lambda j, i: (0, j + off))

    outs, moved = _call(
        body, name=name, grid=(nj, nt),
        in_specs=[main_spec(0), prev_spec(0), main_spec(nj), prev_spec(nj),
                  par_spec(FFN_KERNEL, 0), par_spec(FFN_KERNEL, nj), par_spec(1, 0), par_spec(1, nj)],
        out_specs=[pl.BlockSpec((tm, tc), lambda j, i: (i, j))] * 3,
        out_shape=[jax.ShapeDtypeStruct((t, f), BF16)] * 3,
        sem=("parallel", "arbitrary"), args=(h, h, h, h, cw, cw, cb, cb), comm=comm)
    return outs if comm is None else (outs, moved)


def _ffn_mid_bwd(h, cg, cv, du, cw, name, tm=1024, tc=1408, comm=None):
    t, f2 = h.shape
    tm = min(tm, t)
    f = f2 // 2
    nj, nt, hb = f // tc, t // tm, tm // FFN_HALO

    ch = min(FFN_CHUNK, tm)
    ahead = ch + SUBLANES
    n_ch = tm // ch

    def body(hg, hv, cg_ref, cgn_ref, cv_ref, cvn_ref, du_ref, dun_ref, cwg, cwv,
             dhg_ref, dhv_ref, dcwg_ref, dcwv_ref, dcbg_ref, dcbv_ref):
        i = pl.program_id(1)

        @pl.when(i == 0)
        def _():
            for ref in (dcwg_ref, dcwv_ref, dcbg_ref, dcbv_ref):
                ref[...] = jnp.zeros_like(ref)

        for lg in range(tc // LANES):
            cols = slice(lg * LANES, (lg + 1) * LANES)
            wg, wv = [cwg[k:k + 1, cols] for k in range(FFN_KERNEL)], [cwv[k:k + 1, cols] for k in range(FFN_KERNEL)]

            def emit(base, cg_e, cv_e, du_e, acc):
                cg_a, cv_a, du_a = cg_e[0:ahead], cv_e[0:ahead], du_e[0:ahead]
                gl, dgl = _gelu_and_grad(cg_a)

                def back(d, h_ref, w, dh_ref):
                    later = [d[0:ch], _rows_up(d, 1, ch), _rows_up(d, 2, ch)]
                    dh_ref[pl.ds(base, ch), cols] = (w[2] * later[0] + w[1] * later[1] + w[0] * later[2]).astype(BF16)
                    h_own = h_ref[pl.ds(base, ch), cols].astype(F32)
                    return [_fold(later[0])] + [_fold(later[FFN_KERNEL - 1 - k] * h_own) for k in range(FFN_KERNEL)]

                sums = back(du_a * cv_a * dgl, hg, wg, dhg_ref) + back(du_a * gl, hv, wv, dhv_ref)
                return tuple(a + s_ for a, s_ in zip(acc, sums))

            def inner(c, acc):
                base = pl.multiple_of(c * ch, ch)
                rows = pl.ds(base, ch + FFN_HALO)
                return emit(base, cg_ref[rows, cols].astype(F32), cv_ref[rows, cols].astype(F32), du_ref[rows, cols].astype(F32), acc)

            def last(acc):
                def rows(main, after):
                    return jnp.concatenate([main[tm - ch:tm, cols].astype(F32), after], axis=0)

                du_next = jnp.where(i < nt - 1, dun_ref[:, cols].astype(F32), 0.0)
                return emit(tm - ch, rows(cg_ref, cgn_ref[:, cols].astype(F32)), rows(cv_ref, cvn_ref[:, cols].astype(F32)),
                            rows(du_ref, du_next), acc)

            acc = (jnp.zeros((SUBLANES, LANES), F32),) * (2 * (1 + FFN_KERNEL))
            if n_ch > 1:
                acc = lax.fori_loop(0, n_ch - 1, inner, acc)
            acc = last(acc)
            dcbg_ref[:, cols] += _colsum(acc[0])
            dcbv_ref[:, cols] += _colsum(acc[1 + FFN_KERNEL])
            for k in range(FFN_KERNEL):
                dcwg_ref[k:k + 1, cols] += _colsum(acc[1 + k])
                dcwv_ref[k:k + 1, cols] += _colsum(acc[2 + FFN_KERNEL + k])

    last_blk = t // FFN_HALO - 1

    def main_spec(off):
        return pl.BlockSpec((tm, tc), lambda j, i: (i, j + off))

    def next_spec(off):
        return pl.BlockSpec((FFN_HALO, tc), lambda j, i: (jnp.minimum((i + 1) * hb, last_blk), j + off))

    def par_spec(rows, off):
        return pl.BlockSpec((rows, tc), lambda j, i: (0, j + off))

    out_tile = pl.BlockSpec((tm, tc), lambda j, i: (i, j))
    outs, moved = _call(
        body, name=name, grid=(nj, nt),
        in_specs=[main_spec(0), main_spec(nj), main_spec(0), next_spec(0), main_spec(0), next_spec(0), main_spec(0), next_spec(0),
                  par_spec(FFN_KERNEL, 0), par_spec(FFN_KERNEL, nj)],
        out_specs=[out_tile, out_tile, par_spec(FFN_KERNEL, 0), par_spec(FFN_KERNEL, 0), par_spec(1, 0), par_spec(1, 0)],
        out_shape=[jax.ShapeDtypeStruct((t, f), BF16), jax.ShapeDtypeStruct((t, f), BF16),
                   jax.ShapeDtypeStruct((FFN_KERNEL, f), F32), jax.ShapeDtypeStruct((FFN_KERNEL, f), F32),
                   jax.ShapeDtypeStruct((1, f), F32), jax.ShapeDtypeStruct((1, f), F32)],
        sem=("parallel", "arbitrary"), args=(h, h, cg, cg, cv, cv, du, du, cw, cw), comm=comm)
    return outs if comm is None else (outs, moved)


MIX_HALO = 32


def _glu(hh):
    return hh[:, 0:A_WIDTH] * _sigmoid(hh[:, A_WIDTH:2 * A_WIDTH])


def _fill_row_shifts(s):
    rows = s.shape[1] - SUBLANES
    for j in range(1, SUBLANES):
        s[j, 0:rows, :] = s[0, pl.ds(j, rows), :]


def _rows_from(s, start, rows):
    j = start % SUBLANES
    return s[j, start - j:start - j + rows, :]


def _tril_mask():
    return lax.broadcasted_iota(jnp.int32, (B_CHUNK, B_CHUNK), 0) >= lax.broadcasted_iota(jnp.int32, (B_CHUNK, B_CHUNK), 1)


def _spatial_mix(q, ms_ref, sbt_ref, tm):
    mask = _tril_mask()
    ws = [jnp.where(mask, ms_ref[g], 0.0).astype(BF16) for g in range(B_GROUPS)]
    qb = q.astype(BF16)
    rows = []
    for c in range(tm // B_CHUNK):
        cols = [_dot(ws[g], qb[c * B_CHUNK:(c + 1) * B_CHUNK, g * 128:(g + 1) * 128], NN) + sbt_ref[:, g:g + 1]
                for g in range(B_GROUPS)]
        rows.append(jnp.concatenate(cols, axis=1))
    return jnp.concatenate(rows, axis=0)


def _mixer_mid_fwd(h, cw, cb, ag, ab, bg, bb, ms, sbt, name, tm=256, comm=None):
    t = h.shape[0]
    nt, hb = t // tm, tm // MIX_HALO
    o = MIX_HALO - A_KERNEL + 1

    def body(h_ref, hp_ref, cw_ref, cb_ref, ag_ref, ab_ref, bg_ref, bb_ref, ms_ref, sbt_ref, cat_ref, y_ref, sp):
        i = pl.program_id(0)
        sp[0, 0:MIX_HALO, :] = jnp.where(i > 0, _glu(hp_ref[:, 0:2 * A_WIDTH].astype(F32)), 0.0)
        sp[0, MIX_HALO:, :] = _glu(h_ref[:, 0:2 * A_WIDTH].astype(F32))
        _fill_row_shifts(sp)
        y = jnp.zeros((tm, A_WIDTH), F32) + cb_ref[...]
        for k in range(A_KERNEL):
            y = y + cw_ref[k:k + 1, :] * _rows_from(sp, o + k, tm)
        y_ref[...] = y.astype(BF16)
        nh, _ = _ln_stats(y)
        ln = nh * ag_ref[...] + ab_ref[...]
        cat_ref[:, 0:A_WIDTH] = (ln * _sigmoid(ln)).astype(BF16)
        u = _gelu(h_ref[:, 1024:1536].astype(F32))
        nb, _ = _ln_stats(_gelu(h_ref[:, 1536:2048].astype(F32)))
        mixed = _spatial_mix(nb * bg_ref[...] + bb_ref[...], ms_ref, sbt_ref, tm)
        cat_ref[:, A_WIDTH:] = (u * mixed).astype(BF16)

    vec = pl.BlockSpec((1, A_WIDTH), lambda i: (0, 0))
    outs, moved = _call(
        body, name=name, grid=(nt,),
        in_specs=[pl.BlockSpec((tm, 2048), lambda i: (i, 0)),
                  pl.BlockSpec((MIX_HALO, 2048), lambda i: (jnp.maximum(i * hb - 1, 0), 0)),
                  pl.BlockSpec((A_KERNEL, A_WIDTH), lambda i: (0, 0)), vec, vec, vec, vec, vec,
                  pl.BlockSpec((B_GROUPS, B_CHUNK, B_CHUNK), lambda i: (0, 0, 0)),
                  pl.BlockSpec((B_CHUNK, B_GROUPS), lambda i: (0, 0))],
        out_specs=[pl.BlockSpec((tm, D_MODEL), lambda i: (i, 0)), pl.BlockSpec((tm, A_WIDTH), lambda i: (i, 0))],
        out_shape=[jax.ShapeDtypeStruct((t, D_MODEL), BF16), jax.ShapeDtypeStruct((t, A_WIDTH), BF16)],
        scratch_shapes=[pltpu.VMEM((SUBLANES, tm + MIX_HALO, A_WIDTH), F32)],
        sem=("parallel",), args=(h, h, cw, cb, ag, ab, bg, bb, ms, sbt), comm=comm)
    return outs if comm is None else (outs, moved)


def _mixer_mid_bwd(h, y, dcat, cw, ag, ab, bg, bb, ms, mst, sbt, name, tm=256, comm=None):
    t = h.shape[0]
    nt, hb = t // tm, tm // MIX_HALO
    r = tm + MIX_HALO
    nchunk = tm // B_CHUNK

    def body(h_ref, y_ref, yn_ref, dc_ref, dcn_ref, cw_ref, ag_ref, ab_ref, bg_ref, bb_ref, ms_ref, mst_ref, sbt_ref,
             dh_ref, dcw_ref, dcb_ref, dag_ref, dab_ref, dbg_ref, dbb_ref, dms_ref, dsb_ref, sdy, sbacc):
        i = pl.program_id(0)

        @pl.when(i == 0)
        def _():
            for ref in (dcw_ref, dcb_ref, dag_ref, dab_ref, dbg_ref, dbb_ref, dms_ref, dsb_ref, sbacc):
                ref[...] = jnp.zeros_like(ref)

        nh, rs = _ln_stats(jnp.concatenate([y_ref[...].astype(F32), yn_ref[...].astype(F32)], axis=0))
        ln = nh * ag_ref[...] + ab_ref[...]
        sg = _sigmoid(ln)
        dao = jnp.concatenate([dc_ref[:, 0:A_WIDTH].astype(F32),
                               jnp.where(i < nt - 1, dcn_ref[:, 0:A_WIDTH].astype(F32), 0.0)], axis=0)
        dln = dao * (sg * (1.0 + ln * (1.0 - sg)))
        dag_ref[...] += _colsum(dln[0:tm] * nh[0:tm])
        dab_ref[...] += _colsum(dln[0:tm])
        sdy[0] = _ln_bwd_rows(dln * ag_ref[...], nh, rs)
        _fill_row_shifts(sdy)
        dcb_ref[...] += _colsum(sdy[0, 0:tm, :])
        av = h_ref[:, 0:A_WIDTH].astype(F32)
        s = _sigmoid(h_ref[:, A_WIDTH:2 * A_WIDTH].astype(F32))
        p_own = av * s
        dp = jnp.zeros((tm, A_WIDTH), F32)
        for k in range(A_KERNEL):
            later = _rows_from(sdy, A_KERNEL - 1 - k, tm)
            dcw_ref[k:k + 1, :] += _colsum(later * p_own)
            dp = dp + cw_ref[k:k + 1, :] * later
        dh_ref[:, 0:A_WIDTH] = (dp * s).astype(BF16)
        dh_ref[:, A_WIDTH:2 * A_WIDTH] = (dp * av * s * (1.0 - s)).astype(BF16)

        u, dgu = _gelu_and_grad(h_ref[:, 1024:1536].astype(F32))
        w, dgw = _gelu_and_grad(h_ref[:, 1536:2048].astype(F32))
        nb, rb = _ln_stats(w)
        q = nb * bg_ref[...] + bb_ref[...]
        mixed = _spatial_mix(q, ms_ref, sbt_ref, tm)
        dbo = dc_ref[:, A_WIDTH:].astype(F32)
        dh_ref[:, 1024:1536] = (dbo * mixed * dgu).astype(BF16)
        dmx = dbo * u
        mask = _tril_mask()
        wst = [jnp.where(mask.T, mst_ref[g], 0.0).astype(BF16) for g in range(B_GROUPS)]
        qb = q.astype(BF16)
        dmb = dmx.astype(BF16)
        rows = []
        for c in range(nchunk):
            cols = []
            for g in range(B_GROUPS):
                rs_, cs_ = slice(c * B_CHUNK, (c + 1) * B_CHUNK), slice(g * 128, (g + 1) * 128)
                sbacc[g] += dmx[rs_, cs_]
                dms_ref[g] += _dot(dmb[rs_, cs_], qb[rs_, cs_], NT)
                cols.append(_dot(wst[g], dmb[rs_, cs_], NN))
            rows.append(jnp.concatenate(cols, axis=1))
        dq = jnp.concatenate(rows, axis=0)
        dbg_ref[...] += _colsum(dq * nb)
        dbb_ref[...] += _colsum(dq)
        dh_ref[:, 1536:2048] = (_ln_bwd_rows(dq * bg_ref[...], nb, rb) * dgw).astype(BF16)

        @pl.when(i == nt - 1)
        def _():
            for g in range(B_GROUPS):
                dms_ref[g] = jnp.where(mask, dms_ref[g], 0.0)
                dsb_ref[g] = jnp.sum(sbacc[g], axis=1, keepdims=True)

    last_blk = t // MIX_HALO - 1
    vec = pl.BlockSpec((1, A_WIDTH), lambda i: (0, 0))
    mat = pl.BlockSpec((B_GROUPS, B_CHUNK, B_CHUNK), lambda i: (0, 0, 0))
    taps = pl.BlockSpec((A_KERNEL, A_WIDTH), lambda i: (0, 0))

    def halo(width):
        return pl.BlockSpec((MIX_HALO, width), lambda i: (jnp.minimum((i + 1) * hb, last_blk), 0))

    vshape = jax.ShapeDtypeStruct((1, A_WIDTH), F32)
    outs, moved = _call(
        body, name=name, grid=(nt,),
        in_specs=[pl.BlockSpec((tm, 2048), lambda i: (i, 0)), pl.BlockSpec((tm, A_WIDTH), lambda i: (i, 0)), halo(A_WIDTH),
                  pl.BlockSpec((tm, D_MODEL), lambda i: (i, 0)), halo(D_MODEL),
                  taps, vec, vec, vec, vec, mat, mat, pl.BlockSpec((B_CHUNK, B_GROUPS), lambda i: (0, 0))],
        out_specs=[pl.BlockSpec((tm, 2048), lambda i: (i, 0)), taps, vec, vec, vec, vec, vec, mat,
                   pl.BlockSpec((B_GROUPS, B_CHUNK, 1), lambda i: (0, 0, 0))],
        out_shape=[jax.ShapeDtypeStruct((t, 2048), BF16), jax.ShapeDtypeStruct((A_KERNEL, A_WIDTH), F32),
                   vshape, vshape, vshape, vshape, vshape,
                   jax.ShapeDtypeStruct((B_GROUPS, B_CHUNK, B_CHUNK), F32), jax.ShapeDtypeStruct((B_GROUPS, B_CHUNK, 1), F32)],
        scratch_shapes=[pltpu.VMEM((SUBLANES, r, A_WIDTH), F32), pltpu.VMEM((B_GROUPS, B_CHUNK, B_CHUNK), F32)],
        sem=("arbitrary",), args=(h, y, y, dcat, dcat, cw, ag, ab, bg, bb, ms, mst, sbt), comm=comm)
    return outs if comm is None else (outs, moved)


Q_WIDTH = N_Q_HEADS * HEAD_DIM
KV_WIDTH = 2 * N_KV_HEADS * HEAD_DIM
PAIRS_PER_KV = N_Q_HEADS // N_KV_HEADS // 2
ATT_SCALE = 1.0 / math.sqrt(HEAD_DIM)


def _dup_heads(pair_cols, kv_head):
    lane = lax.broadcasted_iota(jnp.int32, pair_cols.shape, 1)
    rolled = pltpu.roll(pair_cols, HEAD_DIM, 1)
    first = lane < HEAD_DIM
    return jnp.where(first, pair_cols, rolled) if kv_head == 0 else jnp.where(first, rolled, pair_cols)


HEADS_PER_KV = N_Q_HEADS // N_KV_HEADS


def _stack_heads(ref, kh):
    lane = lax.broadcasted_iota(jnp.int32, (ATT_BLOCK, 128), 1)
    rows = []
    for pr in range(PAIRS_PER_KV):
        c0 = (kh * PAIRS_PER_KV + pr) * 128
        pair = ref[:, c0:c0 + 128]
        rows += [jnp.where(lane < HEAD_DIM, pair, jnp.zeros_like(pair)), jnp.where(lane < HEAD_DIM, jnp.zeros_like(pair), pair)]
    return jnp.concatenate(rows, axis=0)


def _unstack_heads(stacked, kh, write):
    lane = lax.broadcasted_iota(jnp.int32, (ATT_BLOCK, 128), 1)
    for pr in range(PAIRS_PER_KV):
        first = stacked[(2 * pr) * ATT_BLOCK:(2 * pr + 1) * ATT_BLOCK]
        second = stacked[(2 * pr + 1) * ATT_BLOCK:(2 * pr + 2) * ATT_BLOCK]
        write((kh * PAIRS_PER_KV + pr) * 128, jnp.where(lane < HEAD_DIM, first, second))


def _sink_row(sink_ref, kh):
    return jnp.concatenate([jnp.full((1, ATT_BLOCK), sink_ref[0, kh * HEADS_PER_KV + h], F32) for h in range(HEADS_PER_KV)], axis=1)


def _att_window_bias():
    sj = lax.broadcasted_iota(jnp.int32, (2 * ATT_BLOCK, HEADS_PER_KV * ATT_BLOCK), 0)
    qi = lax.broadcasted_iota(jnp.int32, (2 * ATT_BLOCK, HEADS_PER_KV * ATT_BLOCK), 1) & (ATT_BLOCK - 1)
    diff = qi + ATT_BLOCK - sj
    return jnp.where((diff >= 0) & (diff < ATT_BLOCK), 0.0, -jnp.inf)


def _att_probs_t(q_all, k2, bias_ref, n, sink):
    st = _dot(k2, q_all, NT) * ATT_SCALE + bias_ref[...]
    st = jnp.concatenate([jnp.where(n > 0, st[0:ATT_BLOCK], -jnp.inf), st[ATT_BLOCK:]], axis=0)
    m = jnp.maximum(jnp.max(st, axis=0, keepdims=True), sink)
    e = jnp.exp(st - m)
    es = jnp.exp(sink - m)
    inv = 1.0 / (jnp.sum(e, axis=0, keepdims=True) + es)
    return e * inv, es * inv


def _attn_fwd(qkv, sinks, name, comm=None):
    t = qkv.shape[0]
    nb = t // ATT_BLOCK
    kvb = Q_WIDTH // KV_WIDTH

    def body(sink_ref, q_ref, kv_ref, kvp_ref, o_ref, bias):
        n = pl.program_id(0)

        @pl.when(n == 0)
        def _():
            bias[...] = _att_window_bias()

        kv = jnp.concatenate([kvp_ref[...], kv_ref[...]], axis=0).astype(F32)

        def write(c0, pair):
            o_ref[:, c0:c0 + 128] = pair.astype(BF16)

        for kh in range(N_KV_HEADS):
            k2 = _dup_heads(kv[:, 0:128], kh).astype(BF16)
            v2 = _dup_heads(kv[:, 128:256], kh).astype(BF16)
            pt, _ = _att_probs_t(_stack_heads(q_ref, kh), k2, bias, n, _sink_row(sink_ref, kh))
            _unstack_heads(_dot(v2, pt, TN).T, kh, write)

    (out,), moved = _call(
        body, name=name, grid=(nb,),
        in_specs=[pl.BlockSpec(memory_space=pltpu.SMEM),
                  pl.BlockSpec((ATT_BLOCK, Q_WIDTH), lambda n: (n, 0)),
                  pl.BlockSpec((ATT_BLOCK, KV_WIDTH), lambda n: (n, kvb)),
                  pl.BlockSpec((ATT_BLOCK, KV_WIDTH), lambda n: (jnp.maximum(n - 1, 0), kvb))],
        out_specs=[pl.BlockSpec((ATT_BLOCK, Q_WIDTH), lambda n: (n, 0))],
        out_shape=[jax.ShapeDtypeStruct((t, Q_WIDTH), BF16)],
        scratch_shapes=[pltpu.VMEM((2 * ATT_BLOCK, HEADS_PER_KV * ATT_BLOCK), F32)],
        sem=("arbitrary",), args=(sinks, qkv, qkv, qkv), comm=comm)
    return out if comm is None else (out, moved)


def _attn_bwd(qkv, d_o, sinks, name, comm=None):
    t = qkv.shape[0]
    nb = t // ATT_BLOCK
    kvb = Q_WIDTH // KV_WIDTH

    def body(sink_ref, q_ref, kv_ref, kvp_ref, do_ref, dq_ref, dkv_ref, dbq_ref, dbkv_ref, dsink_ref, carry, bias):
        n = pl.program_id(0)

        @pl.when(n == 0)
        def _():
            for ref in (dbq_ref, dbkv_ref, dsink_ref, carry):
                ref[...] = jnp.zeros_like(ref)
            dkv_ref[...] = jnp.zeros_like(dkv_ref)
            bias[...] = _att_window_bias()

        @pl.when(n < nb)
        def _():
            kv = jnp.concatenate([kvp_ref[...], kv_ref[...]], axis=0).astype(F32)
            lane2 = lax.broadcasted_iota(jnp.int32, (2 * ATT_BLOCK, 128), 1)
            sink_lane = lax.broadcasted_iota(jnp.int32, (1, 128), 1)
            dsink = jnp.zeros((1, 128), F32)
            dk_parts, dv_parts = [], []

            def write(c0, pair):
                dbq_ref[:, c0:c0 + 128] += _colsum(pair)
                dq_ref[:, c0:c0 + 128] = pair.astype(BF16)

            for kh in range(N_KV_HEADS):
                k2 = _dup_heads(kv[:, 0:128], kh).astype(BF16)
                v2 = _dup_heads(kv[:, 128:256], kh).astype(BF16)
                q_all = _stack_heads(q_ref, kh)
                do_all = _stack_heads(do_ref, kh)
                pt, ps = _att_probs_t(q_all, k2, bias, n, _sink_row(sink_ref, kh))
                dpt = _dot(v2, do_all, NT)
                delta = jnp.sum(pt * dpt, axis=0, keepdims=True)
                dst = pt * (dpt - delta) * ATT_SCALE
                psd = ps * delta
                for h in range(HEADS_PER_KV):
                    dsink = dsink + jnp.where(sink_lane == kh * HEADS_PER_KV + h,
                                              -jnp.sum(psd[:, h * ATT_BLOCK:(h + 1) * ATT_BLOCK]), 0.0)
                _unstack_heads(_dot(k2, dst, TN).T, kh, write)
                dk_acc = _dot(dst, q_all, NN)
                dv_acc = _dot(pt, do_all, NN)
                dk_parts.append(dk_acc + pltpu.roll(dk_acc, HEAD_DIM, 1))
                dv_parts.append(dv_acc + pltpu.roll(dv_acc, HEAD_DIM, 1))
            dk = jnp.where(lane2 < HEAD_DIM, dk_parts[0], dk_parts[1])
            dv = jnp.where(lane2 < HEAD_DIM, dv_parts[0], dv_parts[1])
            dkv_new = jnp.concatenate([dk, dv], axis=1)
            done = carry[...] + dkv_new[0:ATT_BLOCK]

            @pl.when(n > 0)
            def _():
                dkv_ref[...] = done.astype(BF16)
                dbkv_ref[...] += _colsum(done)

            carry[...] = dkv_new[ATT_BLOCK:]
            dsink_ref[...] += dsink

        @pl.when(n == nb)
        def _():
            dkv_ref[...] = carry[...].astype(BF16)
            dbkv_ref[...] += _colsum(carry[...])

    def clamp(n):
        return jnp.minimum(n, nb - 1)

    outs, moved = _call(
        body, name=name, grid=(nb + 1,),
        in_specs=[pl.BlockSpec(memory_space=pltpu.SMEM),
                  pl.BlockSpec((ATT_BLOCK, Q_WIDTH), lambda n: (clamp(n), 0)),
                  pl.BlockSpec((ATT_BLOCK, KV_WIDTH), lambda n: (clamp(n), kvb)),
                  pl.BlockSpec((ATT_BLOCK, KV_WIDTH), lambda n: (jnp.maximum(clamp(n) - 1, 0), kvb)),
                  pl.BlockSpec((ATT_BLOCK, Q_WIDTH), lambda n: (clamp(n), 0))],
        out_specs=[pl.BlockSpec((ATT_BLOCK, Q_WIDTH), lambda n: (clamp(n), 0)),
                   pl.BlockSpec((ATT_BLOCK, KV_WIDTH), lambda n: (jnp.maximum(n - 1, 0), 0)),
                   pl.BlockSpec((1, Q_WIDTH), lambda n: (0, 0)),
                   pl.BlockSpec((1, KV_WIDTH), lambda n: (0, 0)),
                   pl.BlockSpec((1, 128), lambda n: (0, 0))],
        out_shape=[jax.ShapeDtypeStruct((t, Q_WIDTH), BF16), jax.ShapeDtypeStruct((t, KV_WIDTH), BF16),
                   jax.ShapeDtypeStruct((1, Q_WIDTH), F32), jax.ShapeDtypeStruct((1, KV_WIDTH), F32),
                   jax.ShapeDtypeStruct((1, 128), F32)],
        scratch_shapes=[pltpu.VMEM((ATT_BLOCK, KV_WIDTH), F32), pltpu.VMEM((2 * ATT_BLOCK, HEADS_PER_KV * ATT_BLOCK), F32)],
        sem=("arbitrary",), args=(sinks, qkv, qkv, qkv, d_o), comm=comm)
    return outs if comm is None else (outs, moved)


def _adamw_math(g, w, m, v):
    m = ADAM_B1 * m + (1.0 - ADAM_B1) * g
    v = ADAM_B2 * v + (1.0 - ADAM_B2) * (g * g)
    m_hat = m / (1.0 - ADAM_B1 ** ADAM_STEP)
    v_hat = v / (1.0 - ADAM_B2 ** ADAM_STEP)
    delta = -ADAM_LR * (m_hat / (jnp.sqrt(v_hat) + ADAM_EPS) + ADAM_WD * w)
    return delta, m, v


def _sum_partials(p_ref):
    g = p_ref[0].astype(F32)
    for s in range(1, N_DEV):
        g = g + p_ref[s].astype(F32)
    return g


def _adamw_big(parts, w, m, v, name, tr):
    r, c = w.shape
    parts = [p if isinstance(p, tuple) else (p, 0, p.shape[1]) for p in parts]
    tiles = [rows // tr for _, _, rows in parts]
    starts = [sum(tiles[:l]) for l in range(len(parts))]
    assert all(lo % tr == 0 and rows % tr == 0 for _, lo, rows in parts) and sum(tiles) * tr == r

    def body(*refs):
        p_refs, (w_ref, m_ref, v_ref, g_out, d_out, m_out, v_out) = refs[:len(parts)], refs[len(parts):]
        i = pl.program_id(0)
        for l, p_ref in enumerate(p_refs):
            @pl.when((i >= starts[l]) & (i < starts[l] + tiles[l]))
            def _():
                g = _sum_partials(p_ref)
                g_out[...] = g
                d_out[...], m_out[...], v_out[...] = _adamw_math(g, w_ref[...], m_ref[...], v_ref[...])

    def part_spec(l):
        return pl.BlockSpec((N_DEV, tr, c), lambda i: (0, jnp.clip(i - starts[l], 0, tiles[l] - 1) + parts[l][1] // tr, 0))

    tile = pl.BlockSpec((tr, c), lambda i: (i, 0))
    shape = jax.ShapeDtypeStruct((r, c), F32)
    return pl.pallas_call(
        body, name=name, grid=(r // tr,),
        in_specs=[part_spec(l) for l in range(len(parts))] + [tile, tile, tile],
        out_specs=[tile] * 4, out_shape=[shape] * 4,
        compiler_params=_params("parallel"),
    )(*[p[0] for p in parts], w, m, v)


def _adamw_small(parts, ws, ms, vs, name):
    n = len(ws)

    def body(*refs):
        ins, outs = refs[:4 * n], refs[4 * n:]
        for a in range(n):
            g = _sum_partials(ins[a])
            outs[4 * a][...] = g
            outs[4 * a + 1][...], outs[4 * a + 2][...], outs[4 * a + 3][...] = _adamw_math(
                g, ins[n + a][...], ins[2 * n + a][...], ins[3 * n + a][...])

    out_shape = []
    for w in ws:
        out_shape += [jax.ShapeDtypeStruct(w.shape, F32)] * 4
    return pl.pallas_call(body, name=name, out_shape=out_shape, compiler_params=_params())(*parts, *ws, *ms, *vs)


PACK_LANES = 128
PACK_ROWS = 8


def _pack(arrs):
    flat = jnp.concatenate([a.reshape(-1).astype(F32) for a in arrs])
    unit = PACK_LANES * PACK_ROWS
    total = -(-flat.shape[0] // unit) * unit
    return jnp.pad(flat, (0, total - flat.shape[0])).reshape(-1, PACK_LANES)


def _unpack(buf, shapes):
    flat = buf.reshape(N_DEV, -1)
    out, pos = [], 0
    for s in shapes:
        size = math.prod(s)
        out.append(flat[:, pos:pos + size].reshape((N_DEV,) + tuple(s)))
        pos += size
    return out


def _interleave(g):
    return jnp.transpose(g, (1, 0, 2)).reshape(g.shape[1], -1)


def _ffn_backward(dz, dzb, x_in, z_in, g_in, h, cg, cv, u, w_up_t, cw, w_down, tag, exchange=(), exchange_late=(), own_rows=0):
    du = _matmul(dzb, w_down, "nt", BF16, f"ffn{tag}_du", 2048, 1408, 1024)
    d_w_down = _matmul(u, dzb, "tn", BF16, f"ffn{tag}_dwdown", 1408, 1024, 2048)
    (dhg, dhv, dcwg, dcwv, dcbg, dcbv), moved = _ffn_mid_bwd(
        h, cg, cv, du, cw, f"ffn{tag}_mid_bwd", comm=_Comm(exchange=[d_w_down.reshape(N_DEV, -1, D_MODEL), *exchange]))
    d_w_up_t = _matmul_tn_pair(dhg, dhv, x_in, BF16, f"ffn{tag}_dwup", 1408, 1024, 1024,
                               comm=_Comm(exchange=exchange_late) if exchange_late else None)
    if exchange_late:
        d_w_up_t, late = d_w_up_t
        moved = moved + late
    d_up_blocks = d_w_up_t.reshape(N_DEV, -1, D_MODEL)
    outs = _matmul_ln_bwd([(dhg, 0), (dhv, D_FF)], w_up_t, z_in, g_in, dz, f"ffn{tag}_dx_ln_bwd", 256,
                          comm=_Comm(exchange=[(d_up_blocks, 0, own_rows)]) if own_rows else None)
    (dz_in, dzb_in, dg_in, db_in), own = outs if own_rows else (outs, [])
    moved = moved + own
    return (dz_in, dzb_in, dg_in, db_in, d_up_blocks,
            jnp.concatenate([dcwg, dcwv], axis=1), jnp.concatenate([dcbg, dcbv], axis=1), moved)


def kernel(x, ab_w_in, a_conv_w, a_conv_b, a_norm_g, a_norm_b, b_norm_g, b_norm_b, b_spatial_w, b_spatial_b, ab_w_out, c_w_qkv, c_b_qkv, c_sinks, c_w_o, ffn_w_up, ffn_conv_w, ffn_conv_b, ffn_w_down, ln_g, ln_b, loss_target, m_ab_w_in, m_a_conv_w, m_a_conv_b, m_a_norm_g, m_a_norm_b, m_b_norm_g, m_b_norm_b, m_b_spatial_w, m_b_spatial_b, m_ab_w_out, m_c_w_qkv, m_c_b_qkv, m_c_sinks, m_c_w_o, m_ffn_w_up, m_ffn_conv_w, m_ffn_conv_b, m_ffn_w_down, m_ln_g, m_ln_b, v_ab_w_in, v_a_conv_w, v_a_conv_b, v_a_norm_g, v_a_norm_b, v_b_norm_g, v_b_norm_b, v_b_spatial_w, v_b_spatial_b, v_ab_w_out, v_c_w_qkv, v_c_b_qkv, v_c_sinks, v_c_w_o, v_ffn_w_up, v_ffn_conv_w, v_ffn_conv_b, v_ffn_w_down, v_ln_g, v_ln_b):
    me = 4 * lax.axis_index("x") + 2 * lax.axis_index("y") + lax.axis_index("c")
    xt = x[0]
    t = xt.shape[0]

    small_shard_shapes = [a_conv_w.shape, c_b_qkv.shape, ffn_conv_w.shape, ln_g.shape, ln_b.shape]
    up_shard = [jnp.swapaxes(ffn_w_up[l], 0, 1).astype(BF16) for l in range(2)]
    qkv_shard = jnp.swapaxes(c_w_qkv[0], 0, 1).astype(BF16)
    down_shard = [ffn_w_down[l].astype(BF16) for l in range(2)]
    g_win, g_small = _comm_only(
        _Comm(gather=[jnp.swapaxes(ab_w_in[0], 0, 1).astype(BF16), _pack([a_conv_w, c_b_qkv, ffn_conv_w, ln_g, ln_b])]),
        "gather_first")
    w_in = g_win.reshape(-1, D_MODEL)
    g_acw, g_bqkv, g_fcw, g_lng, g_lnb = _unpack(g_small, small_shard_shapes)
    acw = _interleave(g_acw[:, 0])
    bqkv = g_bqkv[:, 0].reshape(1, -1)
    fcw = [_interleave(g_fcw[:, l]) for l in range(2)]
    lng = jnp.transpose(g_lng, (1, 2, 0, 3)).reshape(2, 2, 1, D_MODEL)
    lnb = jnp.transpose(g_lnb, (1, 2, 0, 3)).reshape(2, 2, 1, D_MODEL)
    fcb = [ffn_conv_b[l:l + 1] for l in range(2)]
    ms = b_spatial_w[0]
    mst = jnp.swapaxes(ms, 1, 2)
    sbt = b_spatial_b[0].T

    q_up = up_shard[0].shape[0] // 4
    h0, (g_wout, g_wup0) = _matmul(xt, w_in, "nt", BF16, "mix_in", 1024, 1024, 1024,
                                   comm=_Comm(gather=[ab_w_out[0].astype(BF16), (up_shard[0], 0, q_up, None)]))
    w_out = g_wout.reshape(D_MODEL, D_MODEL)
    (cat, y0), (g_wup0,) = _mixer_mid_fwd(h0, acw, a_conv_b, a_norm_g, a_norm_b, b_norm_g, b_norm_b, ms, sbt, "mix_mid_fwd",
                                          comm=_Comm(gather=[(up_shard[0], q_up, 2 * q_up, g_wup0)]))
    (z1, x1), (g_wup0,) = _matmul_res_ln(cat, w_out, xt, lng[0, 0], lnb[0, 0], "mix_out_ln", 512,
                                         comm=_Comm(gather=[(up_shard[0], 3 * q_up, q_up, g_wup0)]))
    w_up0 = g_wup0.reshape(2 * D_FF, D_MODEL)
    hf0, (g_wdown0, g_wqkv) = _matmul(x1, w_up0, "nt", BF16, "ffn0_up", 2048, 1408, 1024,
                                      comm=_Comm(gather=[down_shard[0], qkv_shard]))
    w_down0 = g_wdown0.reshape(D_FF, D_MODEL)
    w_qkv = g_wqkv.reshape(Q_WIDTH + KV_WIDTH, D_MODEL)
    (u0, cg0, cv0), (g_wup1,) = _ffn_mid_fwd(hf0, fcw[0], fcb[0], "ffn0_mid_fwd",
                                             comm=_Comm(gather=[(up_shard[1], 0, 3 * q_up, None)]))
    (z2, x2), (g_wo, g_wup1) = _matmul_res_ln(
        u0, w_down0, z1, lng[0, 1], lnb[0, 1], "ffn0_down_ln", 512, prev=(lng[0, 0], lnb[0, 0]),
        comm=_Comm(gather=[c_w_o[0].astype(BF16), (up_shard[1], 3 * q_up, q_up, g_wup1)]))
    w_o = g_wo.reshape(D_MODEL, D_MODEL)
    w_up1 = g_wup1.reshape(2 * D_FF, D_MODEL)
    qkv = _matmul(x2, w_qkv, "nt", BF16, "att_qkv", 1024, 1280, 1024, bias=bqkv)
    att, (g_wdown1,) = _attn_fwd(qkv, c_sinks, "att_fwd", comm=_Comm(gather=[down_shard[1]]))
    w_down1 = g_wdown1.reshape(D_FF, D_MODEL)
    z3, x3 = _matmul_res_ln(att, w_o, z2, lng[1, 0], lnb[1, 0], "att_out_ln", 512, prev=(lng[0, 1], lnb[0, 1]))
    hf1 = _matmul(x3, w_up1, "nt", BF16, "ffn1_up", 2048, 1408, 1024)
    u1, cg1, cv1 = _ffn_mid_fwd(hf1, fcw[1], fcb[1], "ffn1_mid_fwd")

    dz4, dz4b, dg11, db11, loss_terms = _matmul_res_ln_loss(u1, w_down1, z3, lng[1, 1], lnb[1, 1], loss_target[0],
                                                      "ffn1_down_ln_loss", 512, prev=(lng[1, 0], lnb[1, 0]))
    dz3, dz3b, dg10, db10, d_wup1, d_fcw1, d_fcb1, (p_wdown1,) = _ffn_backward(
        dz4, dz4b, x3, z3, lng[1, 0], hf1, cg1, cv1, u1, w_up1, fcw[1], w_down1, 1)
    d_att = _matmul(dz3b, w_o, "nt", BF16, "att_dout", 1024, 1024, 1024)
    d_wo = _matmul(att, dz3b, "tn", BF16, "att_dwo", 1024, 1024, 512)
    rows_up = d_wup1.shape[1]
    first = 3 * rows_up // 4
    (dq, dkv, dbq, dbkv, dsinks), (p_wup1a,) = _attn_bwd(qkv, d_att, c_sinks, "att_bwd",
                                                        comm=_Comm(exchange=[(d_wup1, 0, first)]))
    d_wqkv = jnp.concatenate([_matmul(dq, x2, "tn", BF16, "att_dwq", 1024, 1024, 1024),
                              _matmul(dkv, x2, "tn", BF16, "att_dwkv", KV_WIDTH, 1024, 1024)], axis=0)
    dz2, dz2b, dg01, db01 = _matmul_ln_bwd([(dq, 0), (dkv, Q_WIDTH)], w_qkv, z2, lng[0, 1], dz3, "att_dx_ln_bwd", 512)
    early = rows_up // 2
    dz1, dz1b, dg00, db00, d_wup0, d_fcw0, d_fcb0, (p_wdown0, p_wup1b, p_wqkv, p_wo, p_wup0a) = _ffn_backward(
        dz2, dz2b, x1, z1, lng[0, 0], hf0, cg0, cv0, u0, w_up0, fcw[0], w_down0, 0, exchange=[(d_wup1, first, rows_up - first)],
        exchange_late=[d_wqkv.reshape(N_DEV, -1, D_MODEL), d_wo.reshape(N_DEV, -1, D_MODEL)], own_rows=early)
    dcat = _matmul(dz1b, w_out, "nt", BF16, "mix_dcat", 1024, 1024, 1024)
    d_wout = _matmul(cat, dz1b, "tn", BF16, "mix_dwout", 1024, 1024, 512)
    (dh0, d_acw, d_acb, d_ang, d_anb, d_bng, d_bnb, d_ms, d_sb), (p_wup0b, p_wout) = _mixer_mid_bwd(
        h0, y0, dcat, acw, a_norm_g, a_norm_b, b_norm_g, b_norm_b, ms, mst, sbt, "mix_mid_bwd",
        comm=_Comm(exchange=[(d_wup0, early, rows_up - early), d_wout.reshape(N_DEV, -1, D_MODEL)]))
    d_bqkv = jnp.concatenate([dbq, dbkv], axis=1)
    d_lng = jnp.stack([jnp.stack([dg00, dg01]), jnp.stack([dg10, dg11])])
    d_lnb = jnp.stack([jnp.stack([db00, db01]), jnp.stack([db10, db11])])
    small_full = [d_acb, d_ang, d_anb, d_bng, d_bnb, d_ms, d_sb, dsinks[:, :N_Q_HEADS], jnp.concatenate([d_fcb0, d_fcb1], axis=0),
                  d_acw, d_bqkv, jnp.stack([d_fcw0, d_fcw1]), d_lng, d_lnb, loss_terms]
    d_win, (g_small_grads,) = _matmul(dh0, xt, "tn", BF16, "mix_dwin", 1024, 1024, 512, comm=_Comm(gather=[_pack(small_full)]))
    grad_x, (p_win,) = _matmul(dh0, w_in, "nn", F32, "mix_dx", 1024, 1024, 1024, res=dz1, res_scale=ALPHA,
                               comm=_Comm(exchange=[d_win.reshape(N_DEV, -1, D_MODEL)]))


    big = {}
    for nm, p, w, m, v, tr, transposed in [
            ("ab_w_in", [p_win], ab_w_in, m_ab_w_in, v_ab_w_in, 256, True),
            ("ab_w_out", [p_wout], ab_w_out, m_ab_w_out, v_ab_w_out, 128, False),
            ("c_w_qkv", [p_wqkv], c_w_qkv, m_c_w_qkv, v_c_w_qkv, 160, True), ("c_w_o", [p_wo], c_w_o, m_c_w_o, v_c_w_o, 128, False),
            ("ffn_w_up", [(p_wup0a, 0, early), (p_wup0b, early, rows_up - early), (p_wup1a, 0, first), (p_wup1b, first, rows_up - first)], ffn_w_up, m_ffn_w_up, v_ffn_w_up, 176, True),
            ("ffn_w_down", [p_wdown0, p_wdown1], ffn_w_down, m_ffn_w_down, v_ffn_w_down, 176, False)]:
        def two_d(a):
            a = jnp.swapaxes(a, 1, 2) if transposed else a
            return a.reshape(-1, a.shape[-1])

        def back(o):
            return jnp.swapaxes(o.reshape(w.shape[0], w.shape[2], w.shape[1]), 1, 2) if transposed else o.reshape(w.shape)

        outs = _adamw_big(p, two_d(w), two_d(m), two_d(v), "adamw_" + nm, tr)
        big[nm] = [back(o) for o in outs]

    *gs, loss_parts = _unpack(g_small_grads, [a.shape for a in small_full])
    loss = 0.5 / D_MODEL * jnp.sum(loss_parts)

    def my_shard(g, width):
        g = g.reshape(g.shape[:-1] + (N_DEV, width))
        return lax.dynamic_index_in_dim(g, me, axis=g.ndim - 2, keepdims=False)

    small_names = ["a_conv_b", "a_norm_g", "a_norm_b", "b_norm_g", "b_norm_b", "b_spatial_w", "b_spatial_b", "c_sinks", "ffn_conv_b",
                   "a_conv_w", "c_b_qkv", "ffn_conv_w", "ln_g", "ln_b"]
    small_w = [a_conv_b, a_norm_g, a_norm_b, b_norm_g, b_norm_b, b_spatial_w, b_spatial_b, c_sinks, ffn_conv_b,
               a_conv_w, c_b_qkv, ffn_conv_w, ln_g, ln_b]
    small_m = [m_a_conv_b, m_a_norm_g, m_a_norm_b, m_b_norm_g, m_b_norm_b, m_b_spatial_w, m_b_spatial_b, m_c_sinks, m_ffn_conv_b,
               m_a_conv_w, m_c_b_qkv, m_ffn_conv_w, m_ln_g, m_ln_b]
    small_v = [v_a_conv_b, v_a_norm_g, v_a_norm_b, v_b_norm_g, v_b_norm_b, v_b_spatial_w, v_b_spatial_b, v_c_sinks, v_ffn_conv_b,
               v_a_conv_w, v_c_b_qkv, v_ffn_conv_w, v_ln_g, v_ln_b]
    gs[9:] = [my_shard(g, w.shape[-1]) for g, w in zip(gs[9:], small_w[9:])]
    two_d = [(-1, w.shape[-1]) for w in small_w]
    outs = _adamw_small([g.reshape((N_DEV,) + w.reshape(s).shape) for g, w, s in zip(gs, small_w, two_d)],
                        [w.reshape(s) for w, s in zip(small_w, two_d)], [m.reshape(s) for m, s in zip(small_m, two_d)],
                        [v.reshape(s) for v, s in zip(small_v, two_d)], "adamw_small")
    small = {nm: [o.reshape(w.shape) for o in outs[4 * a:4 * a + 4]] for a, (nm, w) in enumerate(zip(small_names, small_w))}

    res = {**big, **small}
    order = ["ab_w_in", "a_conv_w", "a_conv_b", "a_norm_g", "a_norm_b", "b_norm_g", "b_norm_b", "b_spatial_w", "b_spatial_b", "ab_w_out",
             "c_w_qkv", "c_b_qkv", "c_sinks", "c_w_o", "ffn_w_up", "ffn_conv_w", "ffn_conv_b", "ffn_w_down", "ln_g", "ln_b"]
    return (loss, grad_x[None], *[res[nm][0] for nm in order], *[res[nm][1] for nm in order],
            *[res[nm][2] for nm in order], *[res[nm][3] for nm in order])
```

```python
import functools
import math

import jax
import jax.numpy as jnp
from jax import lax
from jax.experimental import pallas as pl
from jax.experimental.pallas import tpu as pltpu

F32 = jnp.float32
BF16 = jnp.bfloat16

N_DEV = 8
D_MODEL = 1024
A_WIDTH = 512
A_KERNEL = 31
B_GROUPS = 4
B_CHUNK = 128
HEAD_DIM = 64
N_Q_HEADS = 16
N_KV_HEADS = 2
ATT_BLOCK = 128
D_FF = 2816
FFN_KERNEL = 3
ALPHA = (2.0 * 2) ** 0.25
LN_EPS = 1e-5
GELU_K = math.sqrt(2.0 / math.pi)
GELU_C = 0.044715
ADAM_LR = 0.001
ADAM_B1 = 0.9
ADAM_B2 = 0.999
ADAM_EPS = 1e-08
ADAM_WD = 0.01
ADAM_STEP = 10
VMEM_LIMIT = 56 * 1024 * 1024
MESH_ID = pl.DeviceIdType.MESH


def _params(*sem):
    return pltpu.CompilerParams(dimension_semantics=sem, vmem_limit_bytes=VMEM_LIMIT)


def _gelu(x):
    t = jnp.tanh(GELU_K * x * (1.0 + GELU_C * x * x))
    return 0.5 * x * (1.0 + t)


def _gelu_and_grad(x):
    x2 = x * x
    t = jnp.tanh(GELU_K * x * (1.0 + GELU_C * x2))
    g = 0.5 * x * (1.0 + t)
    dg = 0.5 * (1.0 + t) + 0.5 * x * (1.0 - t * t) * (GELU_K * (1.0 + 3.0 * GELU_C * x2))
    return g, dg


def _sigmoid(x):
    return 1.0 / (1.0 + jnp.exp(-x))


def _ln_stats(z):
    mu = jnp.mean(z, axis=-1, keepdims=True)
    zc = z - mu
    var = jnp.mean(zc * zc, axis=-1, keepdims=True)
    r = lax.rsqrt(var + LN_EPS)
    return zc * r, r


def _ln_bwd_rows(dn, nh, r):
    return r * (dn - jnp.mean(dn, axis=-1, keepdims=True) - nh * jnp.mean(dn * nh, axis=-1, keepdims=True))


def _colsum(x):
    return jnp.sum(x, axis=0, keepdims=True)


def _dot(a, b, dims):
    return lax.dot_general(a.astype(BF16), b.astype(BF16), (dims, ((), ())), preferred_element_type=F32)


NN = ((1,), (0,))
NT = ((1,), (1,))
TN = ((0,), (0,))


ANY = pl.BlockSpec(memory_space=pl.ANY)
N_RELATIONS = N_DEV - 1


def _my_place():
    return lax.axis_index("x"), lax.axis_index("y"), lax.axis_index("c")


class _Comm:
    def __init__(self, gather=(), exchange=()):
        gather = [e if isinstance(e, tuple) else (e, 0, e.shape[0], None) for e in gather]
        exchange = [e if isinstance(e, tuple) else (e, 0, e.shape[1]) for e in exchange]
        self.arrs = [e[0] for e in gather] + [e[0] for e in exchange]
        self.n_gather = len(gather)
        self.n = len(self.arrs)
        self.rows = [pl.ds(lo, n) for _, lo, n, _ in gather] + [pl.ds(lo, n) for _, lo, n in exchange]
        self.into = {i: e[3] for i, e in enumerate(gather) if e[3] is not None}

    def out_shape(self):
        return [jax.ShapeDtypeStruct(((N_DEV,) + a.shape) if i < self.n_gather else a.shape, a.dtype)
                for i, a in enumerate(self.arrs)]

    def sems(self):
        return [pltpu.SemaphoreType.DMA((self.n, N_RELATIONS)), pltpu.SemaphoreType.DMA((self.n, N_RELATIONS)),
                pltpu.SemaphoreType.DMA((self.n,))]

    def _gather_copy(self, ins, outs, sems, a, k, place, to, from_input=False):
        px, py, pc = place
        block = outs[a].at[4 * px + 2 * py + pc, self.rows[a]]
        return pltpu.make_async_remote_copy(
            src_ref=ins[a].at[self.rows[a]] if from_input else block, dst_ref=block,
            send_sem=sems[0].at[a, k], recv_sem=sems[1].at[a, k], device_id=to, device_id_type=MESH_ID)

    def _exchange_copy(self, ins, outs, sems, a, k, landing=False):
        x, y, c = _my_place()
        me = 4 * x + 2 * y + c
        peer = (x ^ (k >> 2), y ^ ((k >> 1) & 1), c ^ (k & 1))
        return pltpu.make_async_remote_copy(
            src_ref=ins[a].at[me ^ k, self.rows[a]], dst_ref=outs[a].at[(me ^ k) if landing else me, self.rows[a]],
            send_sem=sems[0].at[a, k - 1], recv_sem=sems[1].at[a, k - 1], device_id=peer, device_id_type=MESH_ID)

    def _local_copy(self, ins, outs, sems, a):
        x, y, c = _my_place()
        me = 4 * x + 2 * y + c
        if a < self.n_gather:
            return pltpu.make_async_copy(ins[a].at[self.rows[a]], outs[a].at[me, self.rows[a]], sems[2].at[a])
        return pltpu.make_async_copy(ins[a].at[me, self.rows[a]], outs[a].at[me, self.rows[a]], sems[2].at[a])

    def _first_stage(self, ins, outs, sems, a):
        x, y, c = _my_place()
        me = (x, y, c)
        chips = [(1 - x, y), (x, 1 - y), (1 - x, 1 - y)]
        return ([self._gather_copy(ins, outs, sems, a, 0, me, (x, y, 1 - c), from_input=True)]
                + [self._gather_copy(ins, outs, sems, a, 1 + j, me, (*chip, c), from_input=True) for j, chip in enumerate(chips)])

    def start(self, ins, outs, sems):
        for a in range(self.n):
            self._local_copy(ins, outs, sems, a).start()
        for a in range(self.n_gather):
            for cp in self._first_stage(ins, outs, sems, a):
                cp.start()
        for k in range(1, N_DEV):
            for a in range(self.n_gather, self.n):
                self._exchange_copy(ins, outs, sems, a, k).start()

    def forward(self, ins, outs, sems):
        x, y, c = _my_place()
        me, sibling = (x, y, c), (x, y, 1 - c)
        for j, chip in enumerate([(1 - x, y), (x, 1 - y), (1 - x, 1 - y)]):
            for a in range(self.n_gather):
                self._gather_copy(ins, outs, sems, a, 1 + j, (*chip, c), me).wait_recv()
                self._gather_copy(ins, outs, sems, a, 4 + j, (*chip, c), sibling).start()

    def finish(self, ins, outs, sems):
        x, y, c = _my_place()
        me, sibling = (x, y, c), (x, y, 1 - c)
        chips = [(1 - x, y), (x, 1 - y), (1 - x, 1 - y)]
        passed = [self._gather_copy(ins, outs, sems, a, 4 + j, (*chip, c), sibling)
                  for j, chip in enumerate(chips) for a in range(self.n_gather)]
        for a in range(self.n_gather):
            self._gather_copy(ins, outs, sems, a, 0, sibling, me).wait_recv()
            for j, chip in enumerate(chips):
                self._gather_copy(ins, outs, sems, a, 4 + j, (*chip, 1 - c), me).wait_recv()
        for k in range(1, N_DEV):
            for a in range(self.n_gather, self.n):
                self._exchange_copy(ins, outs, sems, a, k, landing=True).wait_recv()
        for a in range(self.n_gather):
            for cp in self._first_stage(ins, outs, sems, a):
                cp.wait_send()
        for cp in passed:
            cp.wait_send()
        for k in range(1, N_DEV):
            for a in range(self.n_gather, self.n):
                self._exchange_copy(ins, outs, sems, a, k).wait_send()
        for a in range(self.n):
            self._local_copy(ins, outs, sems, a).wait()


def _comm_only(comm, name):
    assert not comm.into

    def body(*refs):
        ins, outs, sems = refs[:comm.n], refs[comm.n:2 * comm.n], refs[2 * comm.n:]
        comm.start(ins, outs, sems)
        comm.forward(ins, outs, sems)
        comm.finish(ins, outs, sems)

    return pl.pallas_call(body, name=name, in_specs=[ANY] * comm.n, out_specs=[ANY] * comm.n,
                          out_shape=comm.out_shape(), scratch_shapes=comm.sems())(*comm.arrs)


def _call(body, *, name, grid, in_specs, out_specs, out_shape, args, sem, scratch_shapes=(), comm=None):
    in_specs, out_specs, out_shape, scratch_shapes = list(in_specs), list(out_specs), list(out_shape), list(scratch_shapes)
    if comm is None:
        outs = pl.pallas_call(body, name=name, grid=grid, in_specs=in_specs, out_specs=out_specs, out_shape=out_shape,
                              scratch_shapes=scratch_shapes, compiler_params=_params(*sem))(*args)
        return list(outs), []
    n_in, n_out, n_scr, nc = len(in_specs), len(out_specs), len(scratch_shapes), comm.n
    completed = sorted(comm.into)

    def wrapped(*refs):
        ins, refs = refs[:n_in], refs[n_in:]
        c_in, refs = refs[:nc], refs[nc + len(completed):]
        outs, refs = refs[:n_out], refs[n_out:]
        c_out, refs = refs[:nc], refs[nc:]
        scr, sems = refs[:n_scr], refs[n_scr:]
        step = functools.reduce(lambda acc, ax: acc * grid[ax] + pl.program_id(ax), range(len(grid)), 0)
        steps = math.prod(grid)

        @pl.when(step == 0)
        def _():
            comm.start(c_in, c_out, sems)

        @pl.when(step == steps - 1)
        def _():
            comm.forward(c_in, c_out, sems)

        body(*ins, *outs, *scr)

        @pl.when(step == steps - 1)
        def _():
            comm.finish(c_in, c_out, sems)

    outs = pl.pallas_call(
        wrapped, name=name, grid=grid, in_specs=in_specs + [ANY] * (nc + len(completed)), out_specs=out_specs + [ANY] * nc,
        out_shape=out_shape + comm.out_shape(), scratch_shapes=scratch_shapes + comm.sems(),
        input_output_aliases={n_in + nc + pos: n_out + item for pos, item in enumerate(completed)},
        compiler_params=_params(*(["arbitrary"] * len(grid))))(*args, *comm.arrs, *[comm.into[item] for item in completed])
    return list(outs[:n_out]), list(outs[n_out:])


def _matmul(a, b, mode, out_dtype, name, tm, tn, tk, *, bias=None, res=None, res_scale=1.0, b_off=0, comm=None):
    tm = min(tm, a.shape[1] if mode == "tn" else a.shape[0])
    tk = min(tk, a.shape[0] if mode == "tn" else a.shape[1])
    if mode == "nn":
        (m, k), n = a.shape, b.shape[1]
        a_spec = pl.BlockSpec((tm, tk), lambda i, j, kk: (i, kk))
        b_spec = pl.BlockSpec((tk, tn), lambda i, j, kk: (kk + b_off, j))
        dims = NN
    elif mode == "nt":
        (m, k), n = a.shape, b.shape[0]
        a_spec = pl.BlockSpec((tm, tk), lambda i, j, kk: (i, kk))
        b_spec = pl.BlockSpec((tn, tk), lambda i, j, kk: (j, kk + b_off))
        dims = NT
    else:
        (k, m), n = a.shape, b.shape[1]
        a_spec = pl.BlockSpec((tk, tm), lambda i, j, kk: (kk, i))
        b_spec = pl.BlockSpec((tk, tn), lambda i, j, kk: (kk, j))
        dims = TN
    assert m % tm == 0 and n % tn == 0 and k % tk == 0, (name, m, n, k)
    nk = k // tk
    in_specs = [a_spec, b_spec]
    args = [a, b]
    if bias is not None:
        in_specs.append(pl.BlockSpec((1, tn), lambda i, j, kk: (0, j)))
        args.append(bias)
    if res is not None:
        in_specs.append(pl.BlockSpec((tm, tn), lambda i, j, kk: (i, j)))
        args.append(res)

    def finish(out, refs, o_ref):
        pos = 2
        if bias is not None:
            out = out + refs[pos][...]
            pos += 1
        if res is not None:
            out = out + res_scale * refs[pos][...].astype(F32)
        o_ref[...] = out.astype(out_dtype)

    def body_one_step(*refs):
        finish(_dot(refs[0][...], refs[1][...], dims), refs, refs[-1])

    def body(*refs):
        a_ref, b_ref = refs[0], refs[1]
        o_ref, acc = refs[-2], refs[-1]
        kk = pl.program_id(2)

        @pl.when(kk == 0)
        def _():
            acc[...] = jnp.zeros_like(acc)

        acc[...] += _dot(a_ref[...], b_ref[...], dims)

        @pl.when(kk == nk - 1)
        def _():
            finish(acc[...], refs, o_ref)

    (out,), moved = _call(
        body_one_step if nk == 1 else body, name=name, grid=(m // tm, n // tn, nk),
        in_specs=in_specs, out_specs=[pl.BlockSpec((tm, tn), lambda i, j, kk: (i, j))],
        out_shape=[jax.ShapeDtypeStruct((m, n), out_dtype)],
        scratch_shapes=[] if nk == 1 else [pltpu.VMEM((tm, tn), F32)],
        sem=("parallel", "parallel", "arbitrary"), args=args, comm=comm)
    return out if comm is None else (out, moved)


def _matmul_tn_pair(a0, a1, b, out_dtype, name, tm, tn, tk, comm=None):
    (k, m), n = a0.shape, b.shape[1]
    tk = min(tk, k)
    assert a1.shape == a0.shape and m % tm == 0 and n % tn == 0 and k % tk == 0, (name, m, n, k)
    mi, nk = m // tm, k // tk

    def body(a0_ref, a1_ref, b_ref, o_ref, acc):
        i, kk = pl.program_id(0), pl.program_id(2)

        @pl.when(kk == 0)
        def _():
            acc[...] = jnp.zeros_like(acc)

        @pl.when(i < mi)
        def _():
            acc[...] += _dot(a0_ref[...], b_ref[...], TN)

        @pl.when(i >= mi)
        def _():
            acc[...] += _dot(a1_ref[...], b_ref[...], TN)

        @pl.when(kk == nk - 1)
        def _():
            o_ref[...] = acc[...].astype(out_dtype)

    (out,), moved = _call(
        body, name=name, grid=(2 * mi, n // tn, nk),
        in_specs=[pl.BlockSpec((tk, tm), lambda i, j, kk: (jnp.where(i < mi, kk, nk - 1), jnp.minimum(i, mi - 1))),
                  pl.BlockSpec((tk, tm), lambda i, j, kk: (jnp.where(i >= mi, kk, 0), jnp.maximum(i - mi, 0))),
                  pl.BlockSpec((tk, tn), lambda i, j, kk: (kk, j))],
        out_specs=[pl.BlockSpec((tm, tn), lambda i, j, kk: (i, j))],
        out_shape=[jax.ShapeDtypeStruct((2 * m, n), out_dtype)],
        scratch_shapes=[pltpu.VMEM((tm, tn), F32)],
        sem=("parallel", "parallel", "arbitrary"), args=(a0, a1, b), comm=comm)
    return out if comm is None else (out, moved)


def _residual_input(x_ref, prev_refs):
    if not prev_refs:
        return x_ref[...]
    nh, _ = _ln_stats(x_ref[...])
    return nh * prev_refs[0][...] + prev_refs[1][...]


def _matmul_res_ln(a, b, x, g, beta, name, tm, prev=None, comm=None):
    t, k = a.shape
    d = b.shape[1]
    tm = min(tm, t)
    assert t % tm == 0
    n_prev = 0 if prev is None else 2

    def body(a_ref, b_ref, x_ref, g_ref, beta_ref, *rest):
        z_ref, xo_ref = rest[n_prev:]
        z = ALPHA * _residual_input(x_ref, rest[:n_prev]) + _dot(a_ref[...], b_ref[...], NN)
        nh, _ = _ln_stats(z)
        z_ref[...] = z
        xo_ref[...] = (nh * g_ref[...] + beta_ref[...]).astype(BF16)

    row = pl.BlockSpec((tm, d), lambda i: (i, 0))
    vec = pl.BlockSpec((1, d), lambda i: (0, 0))
    outs, moved = _call(
        body, name=name, grid=(t // tm,),
        in_specs=[pl.BlockSpec((tm, k), lambda i: (i, 0)), pl.BlockSpec((k, d), lambda i: (0, 0)), row, vec, vec] + [vec] * n_prev,
        out_specs=[row, row],
        out_shape=[jax.ShapeDtypeStruct((t, d), F32), jax.ShapeDtypeStruct((t, d), BF16)],
        sem=("parallel",), args=(a, b, x, g, beta, *(prev or ())), comm=comm)
    return outs if comm is None else (outs, moved)


def _matmul_ln_bwd(parts, b, z, g, dres, name, tm, comm=None):
    m = parts[0][0].shape[0]
    d = b.shape[1]
    tm = min(tm, m)
    n = len(parts)
    assert m % tm == 0 and all(row % a.shape[1] == 0 for a, row in parts)

    def body(*refs):
        z_ref, g_ref, dres_ref = refs[2 * n:2 * n + 3]
        dz_ref, dzb_ref, dg_ref, db_ref = refs[-4:]

        @pl.when(pl.program_id(0) == 0)
        def _():
            dg_ref[...] = jnp.zeros_like(dg_ref)
            db_ref[...] = jnp.zeros_like(db_ref)

        dy = ALPHA * dres_ref[...]
        for p in range(n):
            dy = dy + _dot(refs[p][...], refs[n + p][...], NN)
        nh, r = _ln_stats(z_ref[...])
        dg_ref[...] += _colsum(dy * nh)
        db_ref[...] += _colsum(dy)
        dz = _ln_bwd_rows(dy * g_ref[...], nh, r)
        dz_ref[...] = dz
        dzb_ref[...] = dz.astype(BF16)

    def b_spec(a, row):
        blk = row // a.shape[1]
        return pl.BlockSpec((a.shape[1], d), lambda i: (blk, 0))

    row = pl.BlockSpec((tm, d), lambda i: (i, 0))
    vec = pl.BlockSpec((1, d), lambda i: (0, 0))
    vshape = jax.ShapeDtypeStruct((1, d), F32)
    outs, moved = _call(
        body, name=name, grid=(m // tm,),
        in_specs=[pl.BlockSpec((tm, a.shape[1]), lambda i: (i, 0)) for a, _ in parts] + [b_spec(a, r_) for a, r_ in parts]
        + [row, vec, row],
        out_specs=[row, row, vec, vec],
        out_shape=[jax.ShapeDtypeStruct((m, d), F32), jax.ShapeDtypeStruct((m, d), BF16), vshape, vshape],
        sem=("arbitrary",), args=(*[a for a, _ in parts], *([b] * n), z, g, dres), comm=comm)
    return outs if comm is None else (outs, moved)


def _matmul_res_ln_loss(a, b, x, g, beta, target, name, tm, prev):
    t, k = a.shape
    d = b.shape[1]
    tm = min(tm, t)

    def body(a_ref, b_ref, x_ref, g_ref, beta_ref, t_ref, gp_ref, bp_ref, dz_ref, dzb_ref, dg_ref, db_ref, loss_ref):
        @pl.when(pl.program_id(0) == 0)
        def _():
            dg_ref[...] = jnp.zeros_like(dg_ref)
            db_ref[...] = jnp.zeros_like(db_ref)
            loss_ref[...] = jnp.zeros_like(loss_ref)

        nh, r = _ln_stats(ALPHA * _residual_input(x_ref, (gp_ref, bp_ref)) + _dot(a_ref[...], b_ref[...], NN))
        err = nh * g_ref[...] + beta_ref[...] - t_ref[...]
        loss_ref[...] += _colsum(err * err)
        dy = err * (1.0 / d)
        dg_ref[...] += _colsum(dy * nh)
        db_ref[...] += _colsum(dy)
        dz = _ln_bwd_rows(dy * g_ref[...], nh, r)
        dz_ref[...] = dz
        dzb_ref[...] = dz.astype(BF16)

    row = pl.BlockSpec((tm, d), lambda i: (i, 0))
    vec = pl.BlockSpec((1, d), lambda i: (0, 0))
    vshape = jax.ShapeDtypeStruct((1, d), F32)
    return pl.pallas_call(
        body, name=name, grid=(t // tm,),
        in_specs=[pl.BlockSpec((tm, k), lambda i: (i, 0)), pl.BlockSpec((k, d), lambda i: (0, 0)), row, vec, vec, row, vec, vec],
        out_specs=[row, row, vec, vec, vec],
        out_shape=[jax.ShapeDtypeStruct((t, d), F32), jax.ShapeDtypeStruct((t, d), BF16), vshape, vshape, vshape],
        compiler_params=_params("arbitrary"),
    )(a, b, x, g, beta, target, *prev)


FFN_HALO = 16
FFN_CHUNK = 256
LANES = 128
SUBLANES = 8


def _rows_up(e, start, rows):
    if start % SUBLANES == 0:
        return e[start:start + rows]
    return pltpu.roll(e, e.shape[0] - start, 0)[0:rows]


def _fold(x):
    return jnp.sum(x.reshape(x.shape[0] // SUBLANES, SUBLANES, x.shape[1]), axis=0)


def _ffn_mid_fwd(h, cw, cb, name, tm=1024, tc=1408, comm=None):
    t, f2 = h.shape
    tm = min(tm, t)
    f = f2 // 2
    nj, nt, hb = f // tc, t // tm, tm // FFN_HALO

    ch = min(FFN_CHUNK, tm)

    def body(hg, hgp, hv, hvp, cwg, cwv, cbg, cbv, u_ref, cg_ref, cv_ref):
        i = pl.program_id(1)
        o = FFN_HALO - FFN_KERNEL + 1
        for lg in range(tc // LANES):
            cols = slice(lg * LANES, (lg + 1) * LANES)
            wg, wv = [cwg[k:k + 1, cols] for k in range(FFN_KERNEL)], [cwv[k:k + 1, cols] for k in range(FFN_KERNEL)]
            bg, bv = cbg[:, cols], cbv[:, cols]

            def emit(base, eg, ev):
                cg = wg[0] * _rows_up(eg, o, ch) + wg[1] * _rows_up(eg, o + 1, ch) + wg[2] * _rows_up(eg, o + 2, ch) + bg
                cv = wv[0] * _rows_up(ev, o, ch) + wv[1] * _rows_up(ev, o + 1, ch) + wv[2] * _rows_up(ev, o + 2, ch) + bv
                u_ref[pl.ds(base, ch), cols] = (_gelu(cg) * cv).astype(BF16)
                cg_ref[pl.ds(base, ch), cols] = cg.astype(BF16)
                cv_ref[pl.ds(base, ch), cols] = cv.astype(BF16)

            def first(main, prev):
                return jnp.concatenate([jnp.where(i > 0, prev[:, cols].astype(F32), 0.0), main[0:ch, cols].astype(F32)], axis=0)

            def inner(c, carry):
                base = pl.multiple_of(c * ch, ch)
                emit(base, hg[pl.ds(base - FFN_HALO, ch + FFN_HALO), cols].astype(F32),
                     hv[pl.ds(base - FFN_HALO, ch + FFN_HALO), cols].astype(F32))
                return carry

            emit(0, first(hg, hgp), first(hv, hvp))
            if tm > ch:
                lax.fori_loop(1, tm // ch, inner, 0)

    def main_spec(off):
        return pl.BlockSpec((tm, tc), lambda j, i: (i, j + off))

    def prev_spec(off):
        return pl.BlockSpec((FFN_HALO, tc), lambda j, i: (jnp.maximum(i * hb - 1, 0), j + off))

    def par_spec(rows, off):
        return pl.BlockSpec((rows, tc), lambda j, i: (0, j + off))

    outs, moved = _call(
        body, name=name, grid=(nj, nt),
        in_specs=[main_spec(0), prev_spec(0), main_spec(nj), prev_spec(nj),
                  par_spec(FFN_KERNEL, 0), par_spec(FFN_KERNEL, nj), par_spec(1, 0), par_spec(1, nj)],
        out_specs=[pl.BlockSpec((tm, tc), lambda j, i: (i, j))] * 3,
        out_shape=[jax.ShapeDtypeStruct((t, f), BF16)] * 3,
        sem=("parallel", "arbitrary"), args=(h, h, h, h, cw, cw, cb, cb), comm=comm)
    return outs if comm is None else (outs, moved)


def _ffn_mid_bwd(h, cg, cv, du, cw, name, tm=1024, tc=1408, comm=None):
    t, f2 = h.shape
    tm = min(tm, t)
    f = f2 // 2
    nj, nt, hb = f // tc, t // tm, tm // FFN_HALO

    ch = min(FFN_CHUNK, tm)
    ahead = ch + SUBLANES
    n_ch = tm // ch

    def body(hg, hv, cg_ref, cgn_ref, cv_ref, cvn_ref, du_ref, dun_ref, cwg, cwv,
             dhg_ref, dhv_ref, dcwg_ref, dcwv_ref, dcbg_ref, dcbv_ref):
        i = pl.program_id(1)

        @pl.when(i == 0)
        def _():
            for ref in (dcwg_ref, dcwv_ref, dcbg_ref, dcbv_ref):
                ref[...] = jnp.zeros_like(ref)

        for lg in range(tc // LANES):
            cols = slice(lg * LANES, (lg + 1) * LANES)
            wg, wv = [cwg[k:k + 1, cols] for k in range(FFN_KERNEL)], [cwv[k:k + 1, cols] for k in range(FFN_KERNEL)]

            def emit(base, cg_e, cv_e, du_e, acc):
                cg_a, cv_a, du_a = cg_e[0:ahead], cv_e[0:ahead], du_e[0:ahead]
                gl, dgl = _gelu_and_grad(cg_a)

                def back(d, h_ref, w, dh_ref):
                    later = [d[0:ch], _rows_up(d, 1, ch), _rows_up(d, 2, ch)]
                    dh_ref[pl.ds(base, ch), cols] = (w[2] * later[0] + w[1] * later[1] + w[0] * later[2]).astype(BF16)
                    h_own = h_ref[pl.ds(base, ch), cols].astype(F32)
                    return [_fold(later[0])] + [_fold(later[FFN_KERNEL - 1 - k] * h_own) for k in range(FFN_KERNEL)]

                sums = back(du_a * cv_a * dgl, hg, wg, dhg_ref) + back(du_a * gl, hv, wv, dhv_ref)
                return tuple(a + s_ for a, s_ in zip(acc, sums))

            def inner(c, acc):
                base = pl.multiple_of(c * ch, ch)
                rows = pl.ds(base, ch + FFN_HALO)
                return emit(base, cg_ref[rows, cols].astype(F32), cv_ref[rows, cols].astype(F32), du_ref[rows, cols].astype(F32), acc)

            def last(acc):
                def rows(main, after):
                    return jnp.concatenate([main[tm - ch:tm, cols].astype(F32), after], axis=0)

                du_next = jnp.where(i < nt - 1, dun_ref[:, cols].astype(F32), 0.0)
                return emit(tm - ch, rows(cg_ref, cgn_ref[:, cols].astype(F32)), rows(cv_ref, cvn_ref[:, cols].astype(F32)),
                            rows(du_ref, du_next), acc)

            acc = (jnp.zeros((SUBLANES, LANES), F32),) * (2 * (1 + FFN_KERNEL))
            if n_ch > 1:
                acc = lax.fori_loop(0, n_ch - 1, inner, acc)
            acc = last(acc)
            dcbg_ref[:, cols] += _colsum(acc[0])
            dcbv_ref[:, cols] += _colsum(acc[1 + FFN_KERNEL])
            for k in range(FFN_KERNEL):
                dcwg_ref[k:k + 1, cols] += _colsum(acc[1 + k])
                dcwv_ref[k:k + 1, cols] += _colsum(acc[2 + FFN_KERNEL + k])

    last_blk = t // FFN_HALO - 1

    def main_spec(off):
        return pl.BlockSpec((tm, tc), lambda j, i: (i, j + off))

    def next_spec(off):
        return pl.BlockSpec((FFN_HALO, tc), lambda j, i: (jnp.minimum((i + 1) * hb, last_blk), j + off))

    def par_spec(rows, off):
        return pl.BlockSpec((rows, tc), lambda j, i: (0, j + off))

    out_tile = pl.BlockSpec((tm, tc), lambda j, i: (i, j))
    outs, moved = _call(
        body, name=name, grid=(nj, nt),
        in_specs=[main_spec(0), main_spec(nj), main_spec(0), next_spec(0), main_spec(0), next_spec(0), main_spec(0), next_spec(0),
                  par_spec(FFN_KERNEL, 0), par_spec(FFN_KERNEL, nj)],
        out_specs=[out_tile, out_tile, par_spec(FFN_KERNEL, 0), par_spec(FFN_KERNEL, 0), par_spec(1, 0), par_spec(1, 0)],
        out_shape=[jax.ShapeDtypeStruct((t, f), BF16), jax.ShapeDtypeStruct((t, f), BF16),
                   jax.ShapeDtypeStruct((FFN_KERNEL, f), F32), jax.ShapeDtypeStruct((FFN_KERNEL, f), F32),
                   jax.ShapeDtypeStruct((1, f), F32), jax.ShapeDtypeStruct((1, f), F32)],
        sem=("parallel", "arbitrary"), args=(h, h, cg, cg, cv, cv, du, du, cw, cw), comm=comm)
    return outs if comm is None else (outs, moved)


MIX_HALO = 32


def _glu(hh):
    return hh[:, 0:A_WIDTH] * _sigmoid(hh[:, A_WIDTH:2 * A_WIDTH])


def _fill_row_shifts(s):
    rows = s.shape[1] - SUBLANES
    for j in range(1, SUBLANES):
        s[j, 0:rows, :] = s[0, pl.ds(j, rows), :]


def _rows_from(s, start, rows):
    j = start % SUBLANES
    return s[j, start - j:start - j + rows, :]


def _tril_mask():
    return lax.broadcasted_iota(jnp.int32, (B_CHUNK, B_CHUNK), 0) >= lax.broadcasted_iota(jnp.int32, (B_CHUNK, B_CHUNK), 1)


def _spatial_mix(q, ms_ref, sbt_ref, tm):
    mask = _tril_mask()
    ws = [jnp.where(mask, ms_ref[g], 0.0).astype(BF16) for g in range(B_GROUPS)]
    qb = q.astype(BF16)
    rows = []
    for c in range(tm // B_CHUNK):
        cols = [_dot(ws[g], qb[c * B_CHUNK:(c + 1) * B_CHUNK, g * 128:(g + 1) * 128], NN) + sbt_ref[:, g:g + 1]
                for g in range(B_GROUPS)]
        rows.append(jnp.concatenate(cols, axis=1))
    return jnp.concatenate(rows, axis=0)


def _mixer_mid_fwd(h, cw, cb, ag, ab, bg, bb, ms, sbt, name, tm=256, comm=None):
    t = h.shape[0]
    nt, hb = t // tm, tm // MIX_HALO
    o = MIX_HALO - A_KERNEL + 1

    def body(h_ref, hp_ref, cw_ref, cb_ref, ag_ref, ab_ref, bg_ref, bb_ref, ms_ref, sbt_ref, cat_ref, y_ref, sp):
        i = pl.program_id(0)
        sp[0, 0:MIX_HALO, :] = jnp.where(i > 0, _glu(hp_ref[:, 0:2 * A_WIDTH].astype(F32)), 0.0)
        sp[0, MIX_HALO:, :] = _glu(h_ref[:, 0:2 * A_WIDTH].astype(F32))
        _fill_row_shifts(sp)
        y = jnp.zeros((tm, A_WIDTH), F32) + cb_ref[...]
        for k in range(A_KERNEL):
            y = y + cw_ref[k:k + 1, :] * _rows_from(sp, o + k, tm)
        y_ref[...] = y.astype(BF16)
        nh, _ = _ln_stats(y)
        ln = nh * ag_ref[...] + ab_ref[...]
        cat_ref[:, 0:A_WIDTH] = (ln * _sigmoid(ln)).astype(BF16)
        u = _gelu(h_ref[:, 1024:1536].astype(F32))
        nb, _ = _ln_stats(_gelu(h_ref[:, 1536:2048].astype(F32)))
        mixed = _spatial_mix(nb * bg_ref[...] + bb_ref[...], ms_ref, sbt_ref, tm)
        cat_ref[:, A_WIDTH:] = (u * mixed).astype(BF16)

    vec = pl.BlockSpec((1, A_WIDTH), lambda i: (0, 0))
    outs, moved = _call(
        body, name=name, grid=(nt,),
        in_specs=[pl.BlockSpec((tm, 2048), lambda i: (i, 0)),
                  pl.BlockSpec((MIX_HALO, 2048), lambda i: (jnp.maximum(i * hb - 1, 0), 0)),
                  pl.BlockSpec((A_KERNEL, A_WIDTH), lambda i: (0, 0)), vec, vec, vec, vec, vec,
                  pl.BlockSpec((B_GROUPS, B_CHUNK, B_CHUNK), lambda i: (0, 0, 0)),
                  pl.BlockSpec((B_CHUNK, B_GROUPS), lambda i: (0, 0))],
        out_specs=[pl.BlockSpec((tm, D_MODEL), lambda i: (i, 0)), pl.BlockSpec((tm, A_WIDTH), lambda i: (i, 0))],
        out_shape=[jax.ShapeDtypeStruct((t, D_MODEL), BF16), jax.ShapeDtypeStruct((t, A_WIDTH), BF16)],
        scratch_shapes=[pltpu.VMEM((SUBLANES, tm + MIX_HALO, A_WIDTH), F32)],
        sem=("parallel",), args=(h, h, cw, cb, ag, ab, bg, bb, ms, sbt), comm=comm)
    return outs if comm is None else (outs, moved)


def _mixer_mid_bwd(h, y, dcat, cw, ag, ab, bg, bb, ms, mst, sbt, name, tm=256, comm=None):
    t = h.shape[0]
    nt, hb = t // tm, tm // MIX_HALO
    r = tm + MIX_HALO
    nchunk = tm // B_CHUNK

    def body(h_ref, y_ref, yn_ref, dc_ref, dcn_ref, cw_ref, ag_ref, ab_ref, bg_ref, bb_ref, ms_ref, mst_ref, sbt_ref,
             dh_ref, dcw_ref, dcb_ref, dag_ref, dab_ref, dbg_ref, dbb_ref, dms_ref, dsb_ref, sdy, sbacc):
        i = pl.program_id(0)

        @pl.when(i == 0)
        def _():
            for ref in (dcw_ref, dcb_ref, dag_ref, dab_ref, dbg_ref, dbb_ref, dms_ref, dsb_ref, sbacc):
                ref[...] = jnp.zeros_like(ref)

        nh, rs = _ln_stats(jnp.concatenate([y_ref[...].astype(F32), yn_ref[...].astype(F32)], axis=0))
        ln = nh * ag_ref[...] + ab_ref[...]
        sg = _sigmoid(ln)
        dao = jnp.concatenate([dc_ref[:, 0:A_WIDTH].astype(F32),
                               jnp.where(i < nt - 1, dcn_ref[:, 0:A_WIDTH].astype(F32), 0.0)], axis=0)
        dln = dao * (sg * (1.0 + ln * (1.0 - sg)))
        dag_ref[...] += _colsum(dln[0:tm] * nh[0:tm])
        dab_ref[...] += _colsum(dln[0:tm])
        sdy[0] = _ln_bwd_rows(dln * ag_ref[...], nh, rs)
        _fill_row_shifts(sdy)
        dcb_ref[...] += _colsum(sdy[0, 0:tm, :])
        av = h_ref[:, 0:A_WIDTH].astype(F32)
        s = _sigmoid(h_ref[:, A_WIDTH:2 * A_WIDTH].astype(F32))
        p_own = av * s
        dp = jnp.zeros((tm, A_WIDTH), F32)
        for k in range(A_KERNEL):
            later = _rows_from(sdy, A_KERNEL - 1 - k, tm)
            dcw_ref[k:k + 1, :] += _colsum(later * p_own)
            dp = dp + cw_ref[k:k + 1, :] * later
        dh_ref[:, 0:A_WIDTH] = (dp * s).astype(BF16)
        dh_ref[:, A_WIDTH:2 * A_WIDTH] = (dp * av * s * (1.0 - s)).astype(BF16)

        u, dgu = _gelu_and_grad(h_ref[:, 1024:1536].astype(F32))
        w, dgw = _gelu_and_grad(h_ref[:, 1536:2048].astype(F32))
        nb, rb = _ln_stats(w)
        q = nb * bg_ref[...] + bb_ref[...]
        mixed = _spatial_mix(q, ms_ref, sbt_ref, tm)
        dbo = dc_ref[:, A_WIDTH:].astype(F32)
        dh_ref[:, 1024:1536] = (dbo * mixed * dgu).astype(BF16)
        dmx = dbo * u
        mask = _tril_mask()
        wst = [jnp.where(mask.T, mst_ref[g], 0.0).astype(BF16) for g in range(B_GROUPS)]
        qb = q.astype(BF16)
        dmb = dmx.astype(BF16)
        rows = []
        for c in range(nchunk):
            cols = []
            for g in range(B_GROUPS):
                rs_, cs_ = slice(c * B_CHUNK, (c + 1) * B_CHUNK), slice(g * 128, (g + 1) * 128)
                sbacc[g] += dmx[rs_, cs_]
                dms_ref[g] += _dot(dmb[rs_, cs_], qb[rs_, cs_], NT)
                cols.append(_dot(wst[g], dmb[rs_, cs_], NN))
            rows.append(jnp.concatenate(cols, axis=1))
        dq = jnp.concatenate(rows, axis=0)
        dbg_ref[...] += _colsum(dq * nb)
        dbb_ref[...] += _colsum(dq)
        dh_ref[:, 1536:2048] = (_ln_bwd_rows(dq * bg_ref[...], nb, rb) * dgw).astype(BF16)

        @pl.when(i == nt - 1)
        def _():
            for g in range(B_GROUPS):
                dms_ref[g] = jnp.where(mask, dms_ref[g], 0.0)
                dsb_ref[g] = jnp.sum(sbacc[g], axis=1, keepdims=True)

    last_blk = t // MIX_HALO - 1
    vec = pl.BlockSpec((1, A_WIDTH), lambda i: (0, 0))
    mat = pl.BlockSpec((B_GROUPS, B_CHUNK, B_CHUNK), lambda i: (0, 0, 0))
    taps = pl.BlockSpec((A_KERNEL, A_WIDTH), lambda i: (0, 0))

    def halo(width):
        return pl.BlockSpec((MIX_HALO, width), lambda i: (jnp.minimum((i + 1) * hb, last_blk), 0))

    vshape = jax.ShapeDtypeStruct((1, A_WIDTH), F32)
    outs, moved = _call(
        body, name=name, grid=(nt,),
        in_specs=[pl.BlockSpec((tm, 2048), lambda i: (i, 0)), pl.BlockSpec((tm, A_WIDTH), lambda i: (i, 0)), halo(A_WIDTH),
                  pl.BlockSpec((tm, D_MODEL), lambda i: (i, 0)), halo(D_MODEL),
                  taps, vec, vec, vec, vec, mat, mat, pl.BlockSpec((B_CHUNK, B_GROUPS), lambda i: (0, 0))],
        out_specs=[pl.BlockSpec((tm, 2048), lambda i: (i, 0)), taps, vec, vec, vec, vec, vec, mat,
                   pl.BlockSpec((B_GROUPS, B_CHUNK, 1), lambda i: (0, 0, 0))],
        out_shape=[jax.ShapeDtypeStruct((t, 2048), BF16), jax.ShapeDtypeStruct((A_KERNEL, A_WIDTH), F32),
                   vshape, vshape, vshape, vshape, vshape,
                   jax.ShapeDtypeStruct((B_GROUPS, B_CHUNK, B_CHUNK), F32), jax.ShapeDtypeStruct((B_GROUPS, B_CHUNK, 1), F32)],
        scratch_shapes=[pltpu.VMEM((SUBLANES, r, A_WIDTH), F32), pltpu.VMEM((B_GROUPS, B_CHUNK, B_CHUNK), F32)],
        sem=("arbitrary",), args=(h, y, y, dcat, dcat, cw, ag, ab, bg, bb, ms, mst, sbt), comm=comm)
    return outs if comm is None else (outs, moved)


Q_WIDTH = N_Q_HEADS * HEAD_DIM
KV_WIDTH = 2 * N_KV_HEADS * HEAD_DIM
PAIRS_PER_KV = N_Q_HEADS // N_KV_HEADS // 2
ATT_SCALE = 1.0 / math.sqrt(HEAD_DIM)


def _dup_heads(pair_cols, kv_head):
    lane = lax.broadcasted_iota(jnp.int32, pair_cols.shape, 1)
    rolled = pltpu.roll(pair_cols, HEAD_DIM, 1)
    first = lane < HEAD_DIM
    return jnp.where(first, pair_cols, rolled) if kv_head == 0 else jnp.where(first, rolled, pair_cols)


HEADS_PER_KV = N_Q_HEADS // N_KV_HEADS


def _stack_heads(ref, kh):
    lane = lax.broadcasted_iota(jnp.int32, (ATT_BLOCK, 128), 1)
    rows = []
    for pr in range(PAIRS_PER_KV):
        c0 = (kh * PAIRS_PER_KV + pr) * 128
        pair = ref[:, c0:c0 + 128]
        rows += [jnp.where(lane < HEAD_DIM, pair, jnp.zeros_like(pair)), jnp.where(lane < HEAD_DIM, jnp.zeros_like(pair), pair)]
    return jnp.concatenate(rows, axis=0)


def _unstack_heads(stacked, kh, write):
    lane = lax.broadcasted_iota(jnp.int32, (ATT_BLOCK, 128), 1)
    for pr in range(PAIRS_PER_KV):
        first = stacked[(2 * pr) * ATT_BLOCK:(2 * pr + 1) * ATT_BLOCK]
        second = stacked[(2 * pr + 1) * ATT_BLOCK:(2 * pr + 2) * ATT_BLOCK]
        write((kh * PAIRS_PER_KV + pr) * 128, jnp.where(lane < HEAD_DIM, first, second))


def _sink_row(sink_ref, kh):
    return jnp.concatenate([jnp.full((1, ATT_BLOCK), sink_ref[0, kh * HEADS_PER_KV + h], F32) for h in range(HEADS_PER_KV)], axis=1)


def _att_window_bias():
    sj = lax.broadcasted_iota(jnp.int32, (2 * ATT_BLOCK, HEADS_PER_KV * ATT_BLOCK), 0)
    qi = lax.broadcasted_iota(jnp.int32, (2 * ATT_BLOCK, HEADS_PER_KV * ATT_BLOCK), 1) & (ATT_BLOCK - 1)
    diff = qi + ATT_BLOCK - sj
    return jnp.where((diff >= 0) & (diff < ATT_BLOCK), 0.0, -jnp.inf)


def _att_probs_t(q_all, k2, bias_ref, n, sink):
    st = _dot(k2, q_all, NT) * ATT_SCALE + bias_ref[...]
    st = jnp.concatenate([jnp.where(n > 0, st[0:ATT_BLOCK], -jnp.inf), st[ATT_BLOCK:]], axis=0)
    m = jnp.maximum(jnp.max(st, axis=0, keepdims=True), sink)
    e = jnp.exp(st - m)
    es = jnp.exp(sink - m)
    inv = 1.0 / (jnp.sum(e, axis=0, keepdims=True) + es)
    return e * inv, es * inv


def _attn_fwd(qkv, sinks, name, comm=None):
    t = qkv.shape[0]
    nb = t // ATT_BLOCK
    kvb = Q_WIDTH // KV_WIDTH

    def body(sink_ref, q_ref, kv_ref, kvp_ref, o_ref, bias):
        n = pl.program_id(0)

        @pl.when(n == 0)
        def _():
            bias[...] = _att_window_bias()

        kv = jnp.concatenate([kvp_ref[...], kv_ref[...]], axis=0).astype(F32)

        def write(c0, pair):
            o_ref[:, c0:c0 + 128] = pair.astype(BF16)

        for kh in range(N_KV_HEADS):
            k2 = _dup_heads(kv[:, 0:128], kh).astype(BF16)
            v2 = _dup_heads(kv[:, 128:256], kh).astype(BF16)
            pt, _ = _att_probs_t(_stack_heads(q_ref, kh), k2, bias, n, _sink_row(sink_ref, kh))
            _unstack_heads(_dot(v2, pt, TN).T, kh, write)

    (out,), moved = _call(
        body, name=name, grid=(nb,),
        in_specs=[pl.BlockSpec(memory_space=pltpu.SMEM),
                  pl.BlockSpec((ATT_BLOCK, Q_WIDTH), lambda n: (n, 0)),
                  pl.BlockSpec((ATT_BLOCK, KV_WIDTH), lambda n: (n, kvb)),
                  pl.BlockSpec((ATT_BLOCK, KV_WIDTH), lambda n: (jnp.maximum(n - 1, 0), kvb))],
        out_specs=[pl.BlockSpec((ATT_BLOCK, Q_WIDTH), lambda n: (n, 0))],
        out_shape=[jax.ShapeDtypeStruct((t, Q_WIDTH), BF16)],
        scratch_shapes=[pltpu.VMEM((2 * ATT_BLOCK, HEADS_PER_KV * ATT_BLOCK), F32)],
        sem=("arbitrary",), args=(sinks, qkv, qkv, qkv), comm=comm)
    return out if comm is None else (out, moved)


def _attn_bwd(qkv, d_o, sinks, name, comm=None):
    t = qkv.shape[0]
    nb = t // ATT_BLOCK
    kvb = Q_WIDTH // KV_WIDTH

    def body(sink_ref, q_ref, kv_ref, kvp_ref, do_ref, dq_ref, dkv_ref, dbq_ref, dbkv_ref, dsink_ref, carry, bias):
        n = pl.program_id(0)

        @pl.when(n == 0)
        def _():
            for ref in (dbq_ref, dbkv_ref, dsink_ref, carry):
                ref[...] = jnp.zeros_like(ref)
            dkv_ref[...] = jnp.zeros_like(dkv_ref)
            bias[...] = _att_window_bias()

        @pl.when(n < nb)
        def _():
            kv = jnp.concatenate([kvp_ref[...], kv_ref[...]], axis=0).astype(F32)
            lane2 = lax.broadcasted_iota(jnp.int32, (2 * ATT_BLOCK, 128), 1)
            sink_lane = lax.broadcasted_iota(jnp.int32, (1, 128), 1)
            dsink = jnp.zeros((1, 128), F32)
            dk_parts, dv_parts = [], []

            def write(c0, pair):
                dbq_ref[:, c0:c0 + 128] += _colsum(pair)
                dq_ref[:, c0:c0 + 128] = pair.astype(BF16)

            for kh in range(N_KV_HEADS):
                k2 = _dup_heads(kv[:, 0:128], kh).astype(BF16)
                v2 = _dup_heads(kv[:, 128:256], kh).astype(BF16)
                q_all = _stack_heads(q_ref, kh)
                do_all = _stack_heads(do_ref, kh)
                pt, ps = _att_probs_t(q_all, k2, bias, n, _sink_row(sink_ref, kh))
                dpt = _dot(v2, do_all, NT)
                delta = jnp.sum(pt * dpt, axis=0, keepdims=True)
                dst = pt * (dpt - delta) * ATT_SCALE
                psd = ps * delta
                for h in range(HEADS_PER_KV):
                    dsink = dsink + jnp.where(sink_lane == kh * HEADS_PER_KV + h,
                                              -jnp.sum(psd[:, h * ATT_BLOCK:(h + 1) * ATT_BLOCK]), 0.0)
                _unstack_heads(_dot(k2, dst, TN).T, kh, write)
                dk_acc = _dot(dst, q_all, NN)
                dv_acc = _dot(pt, do_all, NN)
                dk_parts.append(dk_acc + pltpu.roll(dk_acc, HEAD_DIM, 1))
                dv_parts.append(dv_acc + pltpu.roll(dv_acc, HEAD_DIM, 1))
            dk = jnp.where(lane2 < HEAD_DIM, dk_parts[0], dk_parts[1])
            dv = jnp.where(lane2 < HEAD_DIM, dv_parts[0], dv_parts[1])
            dkv_new = jnp.concatenate([dk, dv], axis=1)
            done = carry[...] + dkv_new[0:ATT_BLOCK]

            @pl.when(n > 0)
            def _():
                dkv_ref[...] = done.astype(BF16)
                dbkv_ref[...] += _colsum(done)

            carry[...] = dkv_new[ATT_BLOCK:]
            dsink_ref[...] += dsink

        @pl.when(n == nb)
        def _():
            dkv_ref[...] = carry[...].astype(BF16)
            dbkv_ref[...] += _colsum(carry[...])

    def clamp(n):
        return jnp.minimum(n, nb - 1)

    outs, moved = _call(
        body, name=name, grid=(nb + 1,),
        in_specs=[pl.BlockSpec(memory_space=pltpu.SMEM),
                  pl.BlockSpec((ATT_BLOCK, Q_WIDTH), lambda n: (clamp(n), 0)),
                  pl.BlockSpec((ATT_BLOCK, KV_WIDTH), lambda n: (clamp(n), kvb)),
                  pl.BlockSpec((ATT_BLOCK, KV_WIDTH), lambda n: (jnp.maximum(clamp(n) - 1, 0), kvb)),
                  pl.BlockSpec((ATT_BLOCK, Q_WIDTH), lambda n: (clamp(n), 0))],
        out_specs=[pl.BlockSpec((ATT_BLOCK, Q_WIDTH), lambda n: (clamp(n), 0)),
                   pl.BlockSpec((ATT_BLOCK, KV_WIDTH), lambda n: (jnp.maximum(n - 1, 0), 0)),
                   pl.BlockSpec((1, Q_WIDTH), lambda n: (0, 0)),
                   pl.BlockSpec((1, KV_WIDTH), lambda n: (0, 0)),
                   pl.BlockSpec((1, 128), lambda n: (0, 0))],
        out_shape=[jax.ShapeDtypeStruct((t, Q_WIDTH), BF16), jax.ShapeDtypeStruct((t, KV_WIDTH), BF16),
                   jax.ShapeDtypeStruct((1, Q_WIDTH), F32), jax.ShapeDtypeStruct((1, KV_WIDTH), F32),
                   jax.ShapeDtypeStruct((1, 128), F32)],
        scratch_shapes=[pltpu.VMEM((ATT_BLOCK, KV_WIDTH), F32), pltpu.VMEM((2 * ATT_BLOCK, HEADS_PER_KV * ATT_BLOCK), F32)],
        sem=("arbitrary",), args=(sinks, qkv, qkv, qkv, d_o), comm=comm)
    return outs if comm is None else (outs, moved)


def _adamw_math(g, w, m, v):
    m = ADAM_B1 * m + (1.0 - ADAM_B1) * g
    v = ADAM_B2 * v + (1.0 - ADAM_B2) * (g * g)
    m_hat = m / (1.0 - ADAM_B1 ** ADAM_STEP)
    v_hat = v / (1.0 - ADAM_B2 ** ADAM_STEP)
    delta = -ADAM_LR * (m_hat / (jnp.sqrt(v_hat) + ADAM_EPS) + ADAM_WD * w)
    return delta, m, v


def _sum_partials(p_ref):
    g = p_ref[0].astype(F32)
    for s in range(1, N_DEV):
        g = g + p_ref[s].astype(F32)
    return g


def _adamw_big(parts, w, m, v, name, tr):
    r, c = w.shape
    parts = [p if isinstance(p, tuple) else (p, 0, p.shape[1]) for p in parts]
    tiles = [rows // tr for _, _, rows in parts]
    starts = [sum(tiles[:l]) for l in range(len(parts))]
    assert all(lo % tr == 0 and rows % tr == 0 for _, lo, rows in parts) and sum(tiles) * tr == r

    def body(*refs):
        p_refs, (w_ref, m_ref, v_ref, g_out, d_out, m_out, v_out) = refs[:len(parts)], refs[len(parts):]
        i = pl.program_id(0)
        for l, p_ref in enumerate(p_refs):
            @pl.when((i >= starts[l]) & (i < starts[l] + tiles[l]))
            def _():
                g = _sum_partials(p_ref)
                g_out[...] = g
                d_out[...], m_out[...], v_out[...] = _adamw_math(g, w_ref[...], m_ref[...], v_ref[...])

    def part_spec(l):
        return pl.BlockSpec((N_DEV, tr, c), lambda i: (0, jnp.clip(i - starts[l], 0, tiles[l] - 1) + parts[l][1] // tr, 0))

    tile = pl.BlockSpec((tr, c), lambda i: (i, 0))
    shape = jax.ShapeDtypeStruct((r, c), F32)
    return pl.pallas_call(
        body, name=name, grid=(r // tr,),
        in_specs=[part_spec(l) for l in range(len(parts))] + [tile, tile, tile],
        out_specs=[tile] * 4, out_shape=[shape] * 4,
        compiler_params=_params("parallel"),
    )(*[p[0] for p in parts], w, m, v)


def _adamw_small(parts, ws, ms, vs, name):
    n = len(ws)

    def body(*refs):
        ins, outs = refs[:4 * n], refs[4 * n:]
        for a in range(n):
            g = _sum_partials(ins[a])
            outs[4 * a][...] = g
            outs[4 * a + 1][...], outs[4 * a + 2][...], outs[4 * a + 3][...] = _adamw_math(
                g, ins[n + a][...], ins[2 * n + a][...], ins[3 * n + a][...])

    out_shape = []
    for w in ws:
        out_shape += [jax.ShapeDtypeStruct(w.shape, F32)] * 4
    return pl.pallas_call(body, name=name, out_shape=out_shape, compiler_params=_params())(*parts, *ws, *ms, *vs)


PACK_LANES = 128
PACK_ROWS = 8


def _pack(arrs):
    flat = jnp.concatenate([a.reshape(-1).astype(F32) for a in arrs])
    unit = PACK_LANES * PACK_ROWS
    total = -(-flat.shape[0] // unit) * unit
    return jnp.pad(flat, (0, total - flat.shape[0])).reshape(-1, PACK_LANES)


def _unpack(buf, shapes):
    flat = buf.reshape(N_DEV, -1)
    out, pos = [], 0
    for s in shapes:
        size = math.prod(s)
        out.append(flat[:, pos:pos + size].reshape((N_DEV,) + tuple(s)))
        pos += size
    return out


def _interleave(g):
    return jnp.transpose(g, (1, 0, 2)).reshape(g.shape[1], -1)


def _ffn_backward(dz, dzb, x_in, z_in, g_in, h, cg, cv, u, w_up_t, cw, w_down, tag, exchange=(), exchange_late=(), own_rows=0):
    du = _matmul(dzb, w_down, "nt", BF16, f"ffn{tag}_du", 2048, 1408, 1024)
    d_w_down = _matmul(u, dzb, "tn", BF16, f"ffn{tag}_dwdown", 1408, 1024, 2048)
    (dhg, dhv, dcwg, dcwv, dcbg, dcbv), moved = _ffn_mid_bwd(
        h, cg, cv, du, cw, f"ffn{tag}_mid_bwd", comm=_Comm(exchange=[d_w_down.reshape(N_DEV, -1, D_MODEL), *exchange]))
    d_w_up_t = _matmul_tn_pair(dhg, dhv, x_in, BF16, f"ffn{tag}_dwup", 1408, 1024, 2048,
                               comm=_Comm(exchange=exchange_late) if exchange_late else None)
    if exchange_late:
        d_w_up_t, late = d_w_up_t
        moved = moved + late
    d_up_blocks = d_w_up_t.reshape(N_DEV, -1, D_MODEL)
    outs = _matmul_ln_bwd([(dhg, 0), (dhv, D_FF)], w_up_t, z_in, g_in, dz, f"ffn{tag}_dx_ln_bwd", 256,
                          comm=_Comm(exchange=[(d_up_blocks, 0, own_rows)]) if own_rows else None)
    (dz_in, dzb_in, dg_in, db_in), own = outs if own_rows else (outs, [])
    moved = moved + own
    return (dz_in, dzb_in, dg_in, db_in, d_up_blocks,
            jnp.concatenate([dcwg, dcwv], axis=1), jnp.concatenate([dcbg, dcbv], axis=1), moved)


def kernel(x, ab_w_in, a_conv_w, a_conv_b, a_norm_g, a_norm_b, b_norm_g, b_norm_b, b_spatial_w, b_spatial_b, ab_w_out, c_w_qkv, c_b_qkv, c_sinks, c_w_o, ffn_w_up, ffn_conv_w, ffn_conv_b, ffn_w_down, ln_g, ln_b, loss_target, m_ab_w_in, m_a_conv_w, m_a_conv_b, m_a_norm_g, m_a_norm_b, m_b_norm_g, m_b_norm_b, m_b_spatial_w, m_b_spatial_b, m_ab_w_out, m_c_w_qkv, m_c_b_qkv, m_c_sinks, m_c_w_o, m_ffn_w_up, m_ffn_conv_w, m_ffn_conv_b, m_ffn_w_down, m_ln_g, m_ln_b, v_ab_w_in, v_a_conv_w, v_a_conv_b, v_a_norm_g, v_a_norm_b, v_b_norm_g, v_b_norm_b, v_b_spatial_w, v_b_spatial_b, v_ab_w_out, v_c_w_qkv, v_c_b_qkv, v_c_sinks, v_c_w_o, v_ffn_w_up, v_ffn_conv_w, v_ffn_conv_b, v_ffn_w_down, v_ln_g, v_ln_b):
    me = 4 * lax.axis_index("x") + 2 * lax.axis_index("y") + lax.axis_index("c")
    xt = x[0]
    t = xt.shape[0]

    small_shard_shapes = [a_conv_w.shape, c_b_qkv.shape, ffn_conv_w.shape, ln_g.shape, ln_b.shape]
    up_shard = [jnp.swapaxes(ffn_w_up[l], 0, 1).astype(BF16) for l in range(2)]
    qkv_shard = jnp.swapaxes(c_w_qkv[0], 0, 1).astype(BF16)
    down_shard = [ffn_w_down[l].astype(BF16) for l in range(2)]
    g_win, g_small = _comm_only(
        _Comm(gather=[jnp.swapaxes(ab_w_in[0], 0, 1).astype(BF16), _pack([a_conv_w, c_b_qkv, ffn_conv_w, ln_g, ln_b])]),
        "gather_first")
    w_in = g_win.reshape(-1, D_MODEL)
    g_acw, g_bqkv, g_fcw, g_lng, g_lnb = _unpack(g_small, small_shard_shapes)
    acw = _interleave(g_acw[:, 0])
    bqkv = g_bqkv[:, 0].reshape(1, -1)
    fcw = [_interleave(g_fcw[:, l]) for l in range(2)]
    lng = jnp.transpose(g_lng, (1, 2, 0, 3)).reshape(2, 2, 1, D_MODEL)
    lnb = jnp.transpose(g_lnb, (1, 2, 0, 3)).reshape(2, 2, 1, D_MODEL)
    fcb = [ffn_conv_b[l:l + 1] for l in range(2)]
    ms = b_spatial_w[0]
    mst = jnp.swapaxes(ms, 1, 2)
    sbt = b_spatial_b[0].T

    q_up = up_shard[0].shape[0] // 4
    h0, (g_wout, g_wup0) = _matmul(xt, w_in, "nt", BF16, "mix_in", 1024, 1024, 1024,
                                   comm=_Comm(gather=[ab_w_out[0].astype(BF16), (up_shard[0], 0, q_up, None)]))
    w_out = g_wout.reshape(D_MODEL, D_MODEL)
    (cat, y0), (g_wup0,) = _mixer_mid_fwd(h0, acw, a_conv_b, a_norm_g, a_norm_b, b_norm_g, b_norm_b, ms, sbt, "mix_mid_fwd",
                                          comm=_Comm(gather=[(up_shard[0], q_up, 2 * q_up, g_wup0)]))
    (z1, x1), (g_wup0,) = _matmul_res_ln(cat, w_out, xt, lng[0, 0], lnb[0, 0], "mix_out_ln", 512,
                                         comm=_Comm(gather=[(up_shard[0], 3 * q_up, q_up, g_wup0)]))
    w_up0 = g_wup0.reshape(2 * D_FF, D_MODEL)
    hf0, (g_wdown0, g_wqkv) = _matmul(x1, w_up0, "nt", BF16, "ffn0_up", 2048, 1408, 1024,
                                      comm=_Comm(gather=[down_shard[0], qkv_shard]))
    w_down0 = g_wdown0.reshape(D_FF, D_MODEL)
    w_qkv = g_wqkv.reshape(Q_WIDTH + KV_WIDTH, D_MODEL)
    (u0, cg0, cv0), (g_wup1,) = _ffn_mid_fwd(hf0, fcw[0], fcb[0], "ffn0_mid_fwd",
                                             comm=_Comm(gather=[(up_shard[1], 0, 3 * q_up, None)]))
    (z2, x2), (g_wo, g_wup1) = _matmul_res_ln(
        u0, w_down0, z1, lng[0, 1], lnb[0, 1], "ffn0_down_ln", 512, prev=(lng[0, 0], lnb[0, 0]),
        comm=_Comm(gather=[c_w_o[0].astype(BF16), (up_shard[1], 3 * q_up, q_up, g_wup1)]))
    w_o = g_wo.reshape(D_MODEL, D_MODEL)
    w_up1 = g_wup1.reshape(2 * D_FF, D_MODEL)
    qkv = _matmul(x2, w_qkv, "nt", BF16, "att_qkv", 1024, 1280, 1024, bias=bqkv)
    att, (g_wdown1,) = _attn_fwd(qkv, c_sinks, "att_fwd", comm=_Comm(gather=[down_shard[1]]))
    w_down1 = g_wdown1.reshape(D_FF, D_MODEL)
    z3, x3 = _matmul_res_ln(att, w_o, z2, lng[1, 0], lnb[1, 0], "att_out_ln", 512, prev=(lng[0, 1], lnb[0, 1]))
    hf1 = _matmul(x3, w_up1, "nt", BF16, "ffn1_up", 2048, 1408, 1024)
    u1, cg1, cv1 = _ffn_mid_fwd(hf1, fcw[1], fcb[1], "ffn1_mid_fwd")

    dz4, dz4b, dg11, db11, loss_terms = _matmul_res_ln_loss(u1, w_down1, z3, lng[1, 1], lnb[1, 1], loss_target[0],
                                                      "ffn1_down_ln_loss", 512, prev=(lng[1, 0], lnb[1, 0]))
    dz3, dz3b, dg10, db10, d_wup1, d_fcw1, d_fcb1, (p_wdown1,) = _ffn_backward(
        dz4, dz4b, x3, z3, lng[1, 0], hf1, cg1, cv1, u1, w_up1, fcw[1], w_down1, 1)
    d_att = _matmul(dz3b, w_o, "nt", BF16, "att_dout", 1024, 1024, 1024)
    d_wo = _matmul(att, dz3b, "tn", BF16, "att_dwo", 1024, 1024, 2048)
    rows_up = d_wup1.shape[1]
    first = 3 * rows_up // 4
    (dq, dkv, dbq, dbkv, dsinks), (p_wup1a,) = _attn_bwd(qkv, d_att, c_sinks, "att_bwd",
                                                        comm=_Comm(exchange=[(d_wup1, 0, first)]))
    d_wqkv = jnp.concatenate([_matmul(dq, x2, "tn", BF16, "att_dwq", 1024, 1024, 2048),
                              _matmul(dkv, x2, "tn", BF16, "att_dwkv", KV_WIDTH, 1024, 1024)], axis=0)
    dz2, dz2b, dg01, db01 = _matmul_ln_bwd([(dq, 0), (dkv, Q_WIDTH)], w_qkv, z2, lng[0, 1], dz3, "att_dx_ln_bwd", 512)
    early = rows_up // 2
    dz1, dz1b, dg00, db00, d_wup0, d_fcw0, d_fcb0, (p_wdown0, p_wup1b, p_wqkv, p_wo, p_wup0a) = _ffn_backward(
        dz2, dz2b, x1, z1, lng[0, 0], hf0, cg0, cv0, u0, w_up0, fcw[0], w_down0, 0, exchange=[(d_wup1, first, rows_up - first)],
        exchange_late=[d_wqkv.reshape(N_DEV, -1, D_MODEL), d_wo.reshape(N_DEV, -1, D_MODEL)], own_rows=early)
    dcat = _matmul(dz1b, w_out, "nt", BF16, "mix_dcat", 1024, 1024, 1024)
    d_wout = _matmul(cat, dz1b, "tn", BF16, "mix_dwout", 1024, 1024, 2048)
    (dh0, d_acw, d_acb, d_ang, d_anb, d_bng, d_bnb, d_ms, d_sb), (p_wup0b, p_wout) = _mixer_mid_bwd(
        h0, y0, dcat, acw, a_norm_g, a_norm_b, b_norm_g, b_norm_b, ms, mst, sbt, "mix_mid_bwd",
        comm=_Comm(exchange=[(d_wup0, early, rows_up - early), d_wout.reshape(N_DEV, -1, D_MODEL)]))
    d_bqkv = jnp.concatenate([dbq, dbkv], axis=1)
    d_lng = jnp.stack([jnp.stack([dg00, dg01]), jnp.stack([dg10, dg11])])
    d_lnb = jnp.stack([jnp.stack([db00, db01]), jnp.stack([db10, db11])])
    small_full = [d_acb, d_ang, d_anb, d_bng, d_bnb, d_ms, d_sb, dsinks[:, :N_Q_HEADS], jnp.concatenate([d_fcb0, d_fcb1], axis=0),
                  d_acw, d_bqkv, jnp.stack([d_fcw0, d_fcw1]), d_lng, d_lnb, loss_terms]
    d_win, (g_small_grads,) = _matmul(dh0, xt, "tn", BF16, "mix_dwin", 1024, 1024, 512, comm=_Comm(gather=[_pack(small_full)]))
    grad_x, (p_win,) = _matmul(dh0, w_in, "nn", F32, "mix_dx", 1024, 1024, 1024, res=dz1, res_scale=ALPHA,
                               comm=_Comm(exchange=[d_win.reshape(N_DEV, -1, D_MODEL)]))


    big = {}
    for nm, p, w, m, v, tr, transposed in [
            ("ab_w_in", [p_win], ab_w_in, m_ab_w_in, v_ab_w_in, 256, True),
            ("ab_w_out", [p_wout], ab_w_out, m_ab_w_out, v_ab_w_out, 128, False),
            ("c_w_qkv", [p_wqkv], c_w_qkv, m_c_w_qkv, v_c_w_qkv, 160, True), ("c_w_o", [p_wo], c_w_o, m_c_w_o, v_c_w_o, 128, False),
            ("ffn_w_up", [(p_wup0a, 0, early), (p_wup0b, early, rows_up - early), (p_wup1a, 0, first), (p_wup1b, first, rows_up - first)], ffn_w_up, m_ffn_w_up, v_ffn_w_up, 176, True),
            ("ffn_w_down", [p_wdown0, p_wdown1], ffn_w_down, m_ffn_w_down, v_ffn_w_down, 176, False)]:
        def two_d(a):
            a = jnp.swapaxes(a, 1, 2) if transposed else a
            return a.reshape(-1, a.shape[-1])

        def back(o):
            return jnp.swapaxes(o.reshape(w.shape[0], w.shape[2], w.shape[1]), 1, 2) if transposed else o.reshape(w.shape)

        outs = _adamw_big(p, two_d(w), two_d(m), two_d(v), "adamw_" + nm, tr)
        big[nm] = [back(o) for o in outs]

    *gs, loss_parts = _unpack(g_small_grads, [a.shape for a in small_full])
    loss = 0.5 / D_MODEL * jnp.sum(loss_parts)

    def my_shard(g, width):
        g = g.reshape(g.shape[:-1] + (N_DEV, width))
        return lax.dynamic_index_in_dim(g, me, axis=g.ndim - 2, keepdims=False)

    small_names = ["a_conv_b", "a_norm_g", "a_norm_b", "b_norm_g", "b_norm_b", "b_spatial_w", "b_spatial_b", "c_sinks", "ffn_conv_b",
                   "a_conv_w", "c_b_qkv", "ffn_conv_w", "ln_g", "ln_b"]
    small_w = [a_conv_b, a_norm_g, a_norm_b, b_norm_g, b_norm_b, b_spatial_w, b_spatial_b, c_sinks, ffn_conv_b,
               a_conv_w, c_b_qkv, ffn_conv_w, ln_g, ln_b]
    small_m = [m_a_conv_b, m_a_norm_g, m_a_norm_b, m_b_norm_g, m_b_norm_b, m_b_spatial_w, m_b_spatial_b, m_c_sinks, m_ffn_conv_b,
               m_a_conv_w, m_c_b_qkv, m_ffn_conv_w, m_ln_g, m_ln_b]
    small_v = [v_a_conv_b, v_a_norm_g, v_a_norm_b, v_b_norm_g, v_b_norm_b, v_b_spatial_w, v_b_spatial_b, v_c_sinks, v_ffn_conv_b,
               v_a_conv_w, v_c_b_qkv, v_ffn_conv_w, v_ln_g, v_ln_b]
    gs[9:] = [my_shard(g, w.shape[-1]) for g, w in zip(gs[9:], small_w[9:])]
    two_d = [(-1, w.shape[-1]) for w in small_w]
    outs = _adamw_small([g.reshape((N_DEV,) + w.reshape(s).shape) for g, w, s in zip(gs, small_w, two_d)],
                        [w.reshape(s) for w, s in zip(small_w, two_d)], [m.reshape(s) for m, s in zip(small_m, two_d)],
                        [v.reshape(s) for v, s in zip(small_v, two_d)], "adamw_small")
    small = {nm: [o.reshape(w.shape) for o in outs[4 * a:4 * a + 4]] for a, (nm, w) in enumerate(zip(small_names, small_w))}

    res = {**big, **small}
    order = ["ab_w_in", "a_conv_w", "a_conv_b", "a_norm_g", "a_norm_b", "b_norm_g", "b_norm_b", "b_spatial_w", "b_spatial_b", "ab_w_out",
             "c_w_qkv", "c_b_qkv", "c_sinks", "c_w_o", "ffn_w_up", "ffn_conv_w", "ffn_conv_b", "ffn_w_down", "ln_g", "ln_b"]
    return (loss, grad_x[None], *[res[nm][0] for nm in order], *[res[nm][1] for nm in order],
            *[res[nm][2] for nm in order], *[res[nm][3] for nm in order])
```

```python
import functools
import math

import jax
import jax.numpy as jnp
from jax import lax
from jax.experimental import pallas as pl
from jax.experimental.pallas import tpu as pltpu

F32 = jnp.float32
BF16 = jnp.bfloat16

N_DEV = 8
D_MODEL = 1024
A_WIDTH = 512
A_KERNEL = 31
B_GROUPS = 4
B_CHUNK = 128
HEAD_DIM = 64
N_Q_HEADS = 16
N_KV_HEADS = 2
ATT_BLOCK = 128
D_FF = 2816
FFN_KERNEL = 3
ALPHA = (2.0 * 2) ** 0.25
LN_EPS = 1e-5
GELU_K = math.sqrt(2.0 / math.pi)
GELU_C = 0.044715
ADAM_LR = 0.001
ADAM_B1 = 0.9
ADAM_B2 = 0.999
ADAM_EPS = 1e-08
ADAM_WD = 0.01
ADAM_STEP = 10
VMEM_LIMIT = 56 * 1024 * 1024
MESH_ID = pl.DeviceIdType.MESH


def _params(*sem):
    return pltpu.CompilerParams(dimension_semantics=sem, vmem_limit_bytes=VMEM_LIMIT)


def _gelu(x):
    t = jnp.tanh(GELU_K * x * (1.0 + GELU_C * x * x))
    return 0.5 * x * (1.0 + t)


def _gelu_and_grad(x):
    x2 = x * x
    t = jnp.tanh(GELU_K * x * (1.0 + GELU_C * x2))
    g = 0.5 * x * (1.0 + t)
    dg = 0.5 * (1.0 + t) + 0.5 * x * (1.0 - t * t) * (GELU_K * (1.0 + 3.0 * GELU_C * x2))
    return g, dg


def _sigmoid(x):
    return 1.0 / (1.0 + jnp.exp(-x))


def _ln_stats(z):
    mu = jnp.mean(z, axis=-1, keepdims=True)
    zc = z - mu
    var = jnp.mean(zc * zc, axis=-1, keepdims=True)
    r = lax.rsqrt(var + LN_EPS)
    return zc * r, r


def _ln_bwd_rows(dn, nh, r):
    return r * (dn - jnp.mean(dn, axis=-1, keepdims=True) - nh * jnp.mean(dn * nh, axis=-1, keepdims=True))


def _colsum(x):
    return jnp.sum(x, axis=0, keepdims=True)


def _dot(a, b, dims):
    return lax.dot_general(a.astype(BF16), b.astype(BF16), (dims, ((), ())), preferred_element_type=F32)


NN = ((1,), (0,))
NT = ((1,), (1,))
TN = ((0,), (0,))


ANY = pl.BlockSpec(memory_space=pl.ANY)
N_RELATIONS = N_DEV - 1


def _my_place():
    return lax.axis_index("x"), lax.axis_index("y"), lax.axis_index("c")


class _Comm:
    def __init__(self, gather=(), exchange=()):
        gather = [e if isinstance(e, tuple) else (e, 0, e.shape[0], None) for e in gather]
        exchange = [e if isinstance(e, tuple) else (e, 0, e.shape[1]) for e in exchange]
        self.arrs = [e[0] for e in gather] + [e[0] for e in exchange]
        self.n_gather = len(gather)
        self.n = len(self.arrs)
        self.rows = [pl.ds(lo, n) for _, lo, n, _ in gather] + [pl.ds(lo, n) for _, lo, n in exchange]
        self.into = {i: e[3] for i, e in enumerate(gather) if e[3] is not None}

    def out_shape(self):
        return [jax.ShapeDtypeStruct(((N_DEV,) + a.shape) if i < self.n_gather else a.shape, a.dtype)
                for i, a in enumerate(self.arrs)]

    def sems(self):
        return [pltpu.SemaphoreType.DMA((self.n, N_RELATIONS)), pltpu.SemaphoreType.DMA((self.n, N_RELATIONS)),
                pltpu.SemaphoreType.DMA((self.n,))]

    def _gather_copy(self, ins, outs, sems, a, k, place, to, from_input=False):
        px, py, pc = place
        block = outs[a].at[4 * px + 2 * py + pc, self.rows[a]]
        return pltpu.make_async_remote_copy(
            src_ref=ins[a].at[self.rows[a]] if from_input else block, dst_ref=block,
            send_sem=sems[0].at[a, k], recv_sem=sems[1].at[a, k], device_id=to, device_id_type=MESH_ID)

    def _exchange_copy(self, ins, outs, sems, a, k, landing=False):
        x, y, c = _my_place()
        me = 4 * x + 2 * y + c
        peer = (x ^ (k >> 2), y ^ ((k >> 1) & 1), c ^ (k & 1))
        return pltpu.make_async_remote_copy(
            src_ref=ins[a].at[me ^ k, self.rows[a]], dst_ref=outs[a].at[(me ^ k) if landing else me, self.rows[a]],
            send_sem=sems[0].at[a, k - 1], recv_sem=sems[1].at[a, k - 1], device_id=peer, device_id_type=MESH_ID)

    def _local_copy(self, ins, outs, sems, a):
        x, y, c = _my_place()
        me = 4 * x + 2 * y + c
        if a < self.n_gather:
            return pltpu.make_async_copy(ins[a].at[self.rows[a]], outs[a].at[me, self.rows[a]], sems[2].at[a])
        return pltpu.make_async_copy(ins[a].at[me, self.rows[a]], outs[a].at[me, self.rows[a]], sems[2].at[a])

    def _first_stage(self, ins, outs, sems, a):
        x, y, c = _my_place()
        me = (x, y, c)
        chips = [(1 - x, y), (x, 1 - y), (1 - x, 1 - y)]
        return ([self._gather_copy(ins, outs, sems, a, 0, me, (x, y, 1 - c), from_input=True)]
                + [self._gather_copy(ins, outs, sems, a, 1 + j, me, (*chip, c), from_input=True) for j, chip in enumerate(chips)])

    def start(self, ins, outs, sems):
        for a in range(self.n):
            self._local_copy(ins, outs, sems, a).start()
        for a in range(self.n_gather):
            for cp in self._first_stage(ins, outs, sems, a):
                cp.start()
        for k in range(1, N_DEV):
            for a in range(self.n_gather, self.n):
                self._exchange_copy(ins, outs, sems, a, k).start()

    def forward(self, ins, outs, sems):
        x, y, c = _my_place()
        me, sibling = (x, y, c), (x, y, 1 - c)
        for j, chip in enumerate([(1 - x, y), (x, 1 - y), (1 - x, 1 - y)]):
            for a in range(self.n_gather):
                self._gather_copy(ins, outs, sems, a, 1 + j, (*chip, c), me).wait_recv()
                self._gather_copy(ins, outs, sems, a, 4 + j, (*chip, c), sibling).start()

    def finish(self, ins, outs, sems):
        x, y, c = _my_place()
        me, sibling = (x, y, c), (x, y, 1 - c)
        chips = [(1 - x, y), (x, 1 - y), (1 - x, 1 - y)]
        passed = [self._gather_copy(ins, outs, sems, a, 4 + j, (*chip, c), sibling)
                  for j, chip in enumerate(chips) for a in range(self.n_gather)]
        for a in range(self.n_gather):
            self._gather_copy(ins, outs, sems, a, 0, sibling, me).wait_recv()
            for j, chip in enumerate(chips):
                self._gather_copy(ins, outs, sems, a, 4 + j, (*chip, 1 - c), me).wait_recv()
        for k in range(1, N_DEV):
            for a in range(self.n_gather, self.n):
                self._exchange_copy(ins, outs, sems, a, k, landing=True).wait_recv()
        for a in range(self.n_gather):
            for cp in self._first_stage(ins, outs, sems, a):
                cp.wait_send()
        for cp in passed:
            cp.wait_send()
        for k in range(1, N_DEV):
            for a in range(self.n_gather, self.n):
                self._exchange_copy(ins, outs, sems, a, k).wait_send()
        for a in range(self.n):
            self._local_copy(ins, outs, sems, a).wait()


def _comm_only(comm, name):
    assert not comm.into

    def body(*refs):
        ins, outs, sems = refs[:comm.n], refs[comm.n:2 * comm.n], refs[2 * comm.n:]
        comm.start(ins, outs, sems)
        comm.forward(ins, outs, sems)
        comm.finish(ins, outs, sems)

    return pl.pallas_call(body, name=name, in_specs=[ANY] * comm.n, out_specs=[ANY] * comm.n,
                          out_shape=comm.out_shape(), scratch_shapes=comm.sems())(*comm.arrs)


def _call(body, *, name, grid, in_specs, out_specs, out_shape, args, sem, scratch_shapes=(), comm=None):
    in_specs, out_specs, out_shape, scratch_shapes = list(in_specs), list(out_specs), list(out_shape), list(scratch_shapes)
    if comm is None:
        outs = pl.pallas_call(body, name=name, grid=grid, in_specs=in_specs, out_specs=out_specs, out_shape=out_shape,
                              scratch_shapes=scratch_shapes, compiler_params=_params(*sem))(*args)
        return list(outs), []
    n_in, n_out, n_scr, nc = len(in_specs), len(out_specs), len(scratch_shapes), comm.n
    completed = sorted(comm.into)

    def wrapped(*refs):
        ins, refs = refs[:n_in], refs[n_in:]
        c_in, refs = refs[:nc], refs[nc + len(completed):]
        outs, refs = refs[:n_out], refs[n_out:]
        c_out, refs = refs[:nc], refs[nc:]
        scr, sems = refs[:n_scr], refs[n_scr:]
        step = functools.reduce(lambda acc, ax: acc * grid[ax] + pl.program_id(ax), range(len(grid)), 0)
        steps = math.prod(grid)

        @pl.when(step == 0)
        def _():
            comm.start(c_in, c_out, sems)

        @pl.when(step == steps - 1)
        def _():
            comm.forward(c_in, c_out, sems)

        body(*ins, *outs, *scr)

        @pl.when(step == steps - 1)
        def _():
            comm.finish(c_in, c_out, sems)

    outs = pl.pallas_call(
        wrapped, name=name, grid=grid, in_specs=in_specs + [ANY] * (nc + len(completed)), out_specs=out_specs + [ANY] * nc,
        out_shape=out_shape + comm.out_shape(), scratch_shapes=scratch_shapes + comm.sems(),
        input_output_aliases={n_in + nc + pos: n_out + item for pos, item in enumerate(completed)},
        compiler_params=_params(*(["arbitrary"] * len(grid))))(*args, *comm.arrs, *[comm.into[item] for item in completed])
    return list(outs[:n_out]), list(outs[n_out:])


def _matmul(a, b, mode, out_dtype, name, tm, tn, tk, *, bias=None, res=None, res_scale=1.0, b_off=0, comm=None):
    tm = min(tm, a.shape[1] if mode == "tn" else a.shape[0])
    tk = min(tk, a.shape[0] if mode == "tn" else a.shape[1])
    if mode == "nn":
        (m, k), n = a.shape, b.shape[1]
        a_spec = pl.BlockSpec((tm, tk), lambda i, j, kk: (i, kk))
        b_spec = pl.BlockSpec((tk, tn), lambda i, j, kk: (kk + b_off, j))
        dims = NN
    elif mode == "nt":
        (m, k), n = a.shape, b.shape[0]
        a_spec = pl.BlockSpec((tm, tk), lambda i, j, kk: (i, kk))
        b_spec = pl.BlockSpec((tn, tk), lambda i, j, kk: (j, kk + b_off))
        dims = NT
    else:
        (k, m), n = a.shape, b.shape[1]
        a_spec = pl.BlockSpec((tk, tm), lambda i, j, kk: (kk, i))
        b_spec = pl.BlockSpec((tk, tn), lambda i, j, kk: (kk, j))
        dims = TN
    assert m % tm == 0 and n % tn == 0 and k % tk == 0, (name, m, n, k)
    nk = k // tk
    in_specs = [a_spec, b_spec]
    args = [a, b]
    if bias is not None:
        in_specs.append(pl.BlockSpec((1, tn), lambda i, j, kk: (0, j)))
        args.append(bias)
    if res is not None:
        in_specs.append(pl.BlockSpec((tm, tn), lambda i, j, kk: (i, j)))
        args.append(res)

    def finish(out, refs, o_ref):
        pos = 2
        if bias is not None:
            out = out + refs[pos][...]
            pos += 1
        if res is not None:
            out = out + res_scale * refs[pos][...].astype(F32)
        o_ref[...] = out.astype(out_dtype)

    def body_one_step(*refs):
        finish(_dot(refs[0][...], refs[1][...], dims), refs, refs[-1])

    def body(*refs):
        a_ref, b_ref = refs[0], refs[1]
        o_ref, acc = refs[-2], refs[-1]
        kk = pl.program_id(2)

        @pl.when(kk == 0)
        def _():
            acc[...] = jnp.zeros_like(acc)

        acc[...] += _dot(a_ref[...], b_ref[...], dims)

        @pl.when(kk == nk - 1)
        def _():
            finish(acc[...], refs, o_ref)

    (out,), moved = _call(
        body_one_step if nk == 1 else body, name=name, grid=(m // tm, n // tn, nk),
        in_specs=in_specs, out_specs=[pl.BlockSpec((tm, tn), lambda i, j, kk: (i, j))],
        out_shape=[jax.ShapeDtypeStruct((m, n), out_dtype)],
        scratch_shapes=[] if nk == 1 else [pltpu.VMEM((tm, tn), F32)],
        sem=("parallel", "parallel", "arbitrary"), args=args, comm=comm)
    return out if comm is None else (out, moved)


def _matmul_tn_pair(a0, a1, b, out_dtype, name, tm, tn, tk, comm=None):
    (k, m), n = a0.shape, b.shape[1]
    tk = min(tk, k)
    assert a1.shape == a0.shape and m % tm == 0 and n % tn == 0 and k % tk == 0, (name, m, n, k)
    mi, nk = m // tm, k // tk

    def body(a0_ref, a1_ref, b_ref, o_ref, acc):
        i, kk = pl.program_id(0), pl.program_id(2)

        @pl.when(kk == 0)
        def _():
            acc[...] = jnp.zeros_like(acc)

        @pl.when(i < mi)
        def _():
            acc[...] += _dot(a0_ref[...], b_ref[...], TN)

        @pl.when(i >= mi)
        def _():
            acc[...] += _dot(a1_ref[...], b_ref[...], TN)

        @pl.when(kk == nk - 1)
        def _():
            o_ref[...] = acc[...].astype(out_dtype)

    (out,), moved = _call(
        body, name=name, grid=(2 * mi, n // tn, nk),
        in_specs=[pl.BlockSpec((tk, tm), lambda i, j, kk: (jnp.where(i < mi, kk, nk - 1), jnp.minimum(i, mi - 1))),
                  pl.BlockSpec((tk, tm), lambda i, j, kk: (jnp.where(i >= mi, kk, 0), jnp.maximum(i - mi, 0))),
                  pl.BlockSpec((tk, tn), lambda i, j, kk: (kk, j))],
        out_specs=[pl.BlockSpec((tm, tn), lambda i, j, kk: (i, j))],
        out_shape=[jax.ShapeDtypeStruct((2 * m, n), out_dtype)],
        scratch_shapes=[pltpu.VMEM((tm, tn), F32)],
        sem=("parallel", "parallel", "arbitrary"), args=(a0, a1, b), comm=comm)
    return out if comm is None else (out, moved)


def _residual_input(x_ref, prev_refs):
    if not prev_refs:
        return x_ref[...]
    nh, _ = _ln_stats(x_ref[...])
    return nh * prev_refs[0][...] + prev_refs[1][...]


def _matmul_res_ln(a, b, x, g, beta, name, tm, prev=None, comm=None):
    t, k = a.shape
    d = b.shape[1]
    tm = min(tm, t)
    assert t % tm == 0
    n_prev = 0 if prev is None else 2

    def body(a_ref, b_ref, x_ref, g_ref, beta_ref, *rest):
        z_ref, xo_ref = rest[n_prev:]
        z = ALPHA * _residual_input(x_ref, rest[:n_prev]) + _dot(a_ref[...], b_ref[...], NN)
        nh, _ = _ln_stats(z)
        z_ref[...] = z
        xo_ref[...] = (nh * g_ref[...] + beta_ref[...]).astype(BF16)

    row = pl.BlockSpec((tm, d), lambda i: (i, 0))
    vec = pl.BlockSpec((1, d), lambda i: (0, 0))
    outs, moved = _call(
        body, name=name, grid=(t // tm,),
        in_specs=[pl.BlockSpec((tm, k), lambda i: (i, 0)), pl.BlockSpec((k, d), lambda i: (0, 0)), row, vec, vec] + [vec] * n_prev,
        out_specs=[row, row],
        out_shape=[jax.ShapeDtypeStruct((t, d), F32), jax.ShapeDtypeStruct((t, d), BF16)],
        sem=("parallel",), args=(a, b, x, g, beta, *(prev or ())), comm=comm)
    return outs if comm is None else (outs, moved)


def _matmul_ln_bwd(parts, b, z, g, dres, name, tm, comm=None):
    m = parts[0][0].shape[0]
    d = b.shape[1]
    tm = min(tm, m)
    n = len(parts)
    assert m % tm == 0 and all(row % a.shape[1] == 0 for a, row in parts)

    def body(*refs):
        z_ref, g_ref, dres_ref = refs[2 * n:2 * n + 3]
        dz_ref, dzb_ref, dg_ref, db_ref = refs[-4:]

        @pl.when(pl.program_id(0) == 0)
        def _():
            dg_ref[...] = jnp.zeros_like(dg_ref)
            db_ref[...] = jnp.zeros_like(db_ref)

        dy = ALPHA * dres_ref[...]
        for p in range(n):
            dy = dy + _dot(refs[p][...], refs[n + p][...], NN)
        nh, r = _ln_stats(z_ref[...])
        dg_ref[...] += _colsum(dy * nh)
        db_ref[...] += _colsum(dy)
        dz = _ln_bwd_rows(dy * g_ref[...], nh, r)
        dz_ref[...] = dz
        dzb_ref[...] = dz.astype(BF16)

    def b_spec(a, row):
        blk = row // a.shape[1]
        return pl.BlockSpec((a.shape[1], d), lambda i: (blk, 0))

    row = pl.BlockSpec((tm, d), lambda i: (i, 0))
    vec = pl.BlockSpec((1, d), lambda i: (0, 0))
    vshape = jax.ShapeDtypeStruct((1, d), F32)
    outs, moved = _call(
        body, name=name, grid=(m // tm,),
        in_specs=[pl.BlockSpec((tm, a.shape[1]), lambda i: (i, 0)) for a, _ in parts] + [b_spec(a, r_) for a, r_ in parts]
        + [row, vec, row],
        out_specs=[row, row, vec, vec],
        out_shape=[jax.ShapeDtypeStruct((m, d), F32), jax.ShapeDtypeStruct((m, d), BF16), vshape, vshape],
        sem=("arbitrary",), args=(*[a for a, _ in parts], *([b] * n), z, g, dres), comm=comm)
    return outs if comm is None else (outs, moved)


def _matmul_res_ln_loss(a, b, x, g, beta, target, name, tm, prev):
    t, k = a.shape
    d = b.shape[1]
    tm = min(tm, t)

    def body(a_ref, b_ref, x_ref, g_ref, beta_ref, t_ref, gp_ref, bp_ref, dz_ref, dzb_ref, dg_ref, db_ref, loss_ref):
        @pl.when(pl.program_id(0) == 0)
        def _():
            dg_ref[...] = jnp.zeros_like(dg_ref)
            db_ref[...] = jnp.zeros_like(db_ref)
            loss_ref[...] = jnp.zeros_like(loss_ref)

        nh, r = _ln_stats(ALPHA * _residual_input(x_ref, (gp_ref, bp_ref)) + _dot(a_ref[...], b_ref[...], NN))
        err = nh * g_ref[...] + beta_ref[...] - t_ref[...]
        loss_ref[...] += _colsum(err * err)
        dy = err * (1.0 / d)
        dg_ref[...] += _colsum(dy * nh)
        db_ref[...] += _colsum(dy)
        dz = _ln_bwd_rows(dy * g_ref[...], nh, r)
        dz_ref[...] = dz
        dzb_ref[...] = dz.astype(BF16)

    row = pl.BlockSpec((tm, d), lambda i: (i, 0))
    vec = pl.BlockSpec((1, d), lambda i: (0, 0))
    vshape = jax.ShapeDtypeStruct((1, d), F32)
    return pl.pallas_call(
        body, name=name, grid=(t // tm,),
        in_specs=[pl.BlockSpec((tm, k), lambda i: (i, 0)), pl.BlockSpec((k, d), lambda i: (0, 0)), row, vec, vec, row, vec, vec],
        out_specs=[row, row, vec, vec, vec],
        out_shape=[jax.ShapeDtypeStruct((t, d), F32), jax.ShapeDtypeStruct((t, d), BF16), vshape, vshape, vshape],
        compiler_params=_params("arbitrary"),
    )(a, b, x, g, beta, target, *prev)


FFN_HALO = 16
FFN_CHUNK = 256
LANES = 128
SUBLANES = 8


def _rows_up(e, start, rows):
    if start % SUBLANES == 0:
        return e[start:start + rows]
    return pltpu.roll(e, e.shape[0] - start, 0)[0:rows]


def _fold(x):
    return jnp.sum(x.reshape(x.shape[0] // SUBLANES, SUBLANES, x.shape[1]), axis=0)


def _ffn_mid_fwd(h, cw, cb, name, tm=1024, tc=1408, comm=None):
    t, f2 = h.shape
    tm = min(tm, t)
    f = f2 // 2
    nj, nt, hb = f // tc, t // tm, tm // FFN_HALO

    ch = min(FFN_CHUNK, tm)

    def body(hg, hgp, hv, hvp, cwg, cwv, cbg, cbv, u_ref, cg_ref, cv_ref):
        i = pl.program_id(1)
        o = FFN_HALO - FFN_KERNEL + 1
        for lg in range(tc // LANES):
            cols = slice(lg * LANES, (lg + 1) * LANES)
            wg, wv = [cwg[k:k + 1, cols] for k in range(FFN_KERNEL)], [cwv[k:k + 1, cols] for k in range(FFN_KERNEL)]
            bg, bv = cbg[:, cols], cbv[:, cols]

            def emit(base, eg, ev):
                cg = wg[0] * _rows_up(eg, o, ch) + wg[1] * _rows_up(eg, o + 1, ch) + wg[2] * _rows_up(eg, o + 2, ch) + bg
                cv = wv[0] * _rows_up(ev, o, ch) + wv[1] * _rows_up(ev, o + 1, ch) + wv[2] * _rows_up(ev, o + 2, ch) + bv
                u_ref[pl.ds(base, ch), cols] = (_gelu(cg) * cv).astype(BF16)
                cg_ref[pl.ds(base, ch), cols] = cg.astype(BF16)
                cv_ref[pl.ds(base, ch), cols] = cv.astype(BF16)

            def first(main, prev):
                return jnp.concatenate([jnp.where(i > 0, prev[:, cols].astype(F32), 0.0), main[0:ch, cols].astype(F32)], axis=0)

            def inner(c, carry):
                base = pl.multiple_of(c * ch, ch)
                emit(base, hg[pl.ds(base - FFN_HALO, ch + FFN_HALO), cols].astype(F32),
                     hv[pl.ds(base - FFN_HALO, ch + FFN_HALO), cols].astype(F32))
                return carry

            emit(0, first(hg, hgp), first(hv, hvp))
            if tm > ch:
                lax.fori_loop(1, tm // ch, inner, 0)

    def main_spec(off):
        return pl.BlockSpec((tm, tc), lambda j, i: (i, j + off))

    def prev_spec(off):
        return pl.BlockSpec((FFN_HALO, tc), lambda j, i: (jnp.maximum(i * hb - 1, 0), j + off))

    def par_spec(rows, off):
        return pl.BlockSpec((rows, tc), lambda j, i: (0, j + off))

    outs, moved = _call(
        body, name=name, grid=(nj, nt),
        in_specs=[main_spec(0), prev_spec(0), main_spec(nj), prev_spec(nj),
                  par_spec(FFN_KERNEL, 0), par_spec(FFN_KERNEL, nj), par_spec(1, 0), par_spec(1, nj)],
        out_specs=[pl.BlockSpec((tm, tc), lambda j, i: (i, j))] * 3,
        out_shape=[jax.ShapeDtypeStruct((t, f), BF16)] * 3,
        sem=("parallel", "arbitrary"), args=(h, h, h, h, cw, cw, cb, cb), comm=comm)
    return outs if comm is None else (outs, moved)


def _ffn_mid_bwd(h, cg, cv, du, cw, name, tm=1024, tc=1408, comm=None):
    t, f2 = h.shape
    tm = min(tm, t)
    f = f2 // 2
    nj, nt, hb = f // tc, t // tm, tm // FFN_HALO

    ch = min(FFN_CHUNK, tm)
    ahead = ch + SUBLANES
    n_ch = tm // ch

    def body(hg, hv, cg_ref, cgn_ref, cv_ref, cvn_ref, du_ref, dun_ref, cwg, cwv,
             dhg_ref, dhv_ref, dcwg_ref, dcwv_ref, dcbg_ref, dcbv_ref):
        i = pl.program_id(1)

        @pl.when(i == 0)
        def _():
            for ref in (dcwg_ref, dcwv_ref, dcbg_ref, dcbv_ref):
                ref[...] = jnp.zeros_like(ref)

        for lg in range(tc // LANES):
            cols = slice(lg * LANES, (lg + 1) * LANES)
            wg, wv = [cwg[k:k + 1, cols] for k in range(FFN_KERNEL)], [cwv[k:k + 1, cols] for k in range(FFN_KERNEL)]

            def emit(base, cg_e, cv_e, du_e, acc):
                cg_a, cv_a, du_a = cg_e[0:ahead], cv_e[0:ahead], du_e[0:ahead]
                gl, dgl = _gelu_and_grad(cg_a)

                def back(d, h_ref, w, dh_ref):
                    later = [d[0:ch], _rows_up(d, 1, ch), _rows_up(d, 2, ch)]
                    dh_ref[pl.ds(base, ch), cols] = (w[2] * later[0] + w[1] * later[1] + w[0] * later[2]).astype(BF16)
                    h_own = h_ref[pl.ds(base, ch), cols].astype(F32)
                    return [_fold(later[0])] + [_fold(later[FFN_KERNEL - 1 - k] * h_own) for k in range(FFN_KERNEL)]

                sums = back(du_a * cv_a * dgl, hg, wg, dhg_ref) + back(du_a * gl, hv, wv, dhv_ref)
                return tuple(a + s_ for a, s_ in zip(acc, sums))

            def inner(c, acc):
                base = pl.multiple_of(c * ch, ch)
                rows = pl.ds(base, ch + FFN_HALO)
                return emit(base, cg_ref[rows, cols].astype(F32), cv_ref[rows, cols].astype(F32), du_ref[rows, cols].astype(F32), acc)

            def last(acc):
                def rows(main, after):
                    return jnp.concatenate([main[tm - ch:tm, cols].astype(F32), after], axis=0)

                du_next = jnp.where(i < nt - 1, dun_ref[:, cols].astype(F32), 0.0)
                return emit(tm - ch, rows(cg_ref, cgn_ref[:, cols].astype(F32)), rows(cv_ref, cvn_ref[:, cols].astype(F32)),
                            rows(du_ref, du_next), acc)

            acc = (jnp.zeros((SUBLANES, LANES), F32),) * (2 * (1 + FFN_KERNEL))
            if n_ch > 1:
                acc = lax.fori_loop(0, n_ch - 1, inner, acc)
            acc = last(acc)
            dcbg_ref[:, cols] += _colsum(acc[0])
            dcbv_ref[:, cols] += _colsum(acc[1 + FFN_KERNEL])
            for k in range(FFN_KERNEL):
                dcwg_ref[k:k + 1, cols] += _colsum(acc[1 + k])
                dcwv_ref[k:k + 1, cols] += _colsum(acc[2 + FFN_KERNEL + k])

    last_blk = t // FFN_HALO - 1

    def main_spec(off):
        return pl.BlockSpec((tm, tc), lambda j, i: (i, j + off))

    def next_spec(off):
        return pl.BlockSpec((FFN_HALO, tc), lambda j, i: (jnp.minimum((i + 1) * hb, last_blk), j + off))

    def par_spec(rows, off):
        return pl.BlockSpec((rows, tc), lambda j, i: (0, j + off))

    out_tile = pl.BlockSpec((tm, tc), lambda j, i: (i, j))
    outs, moved = _call(
        body, name=name, grid=(nj, nt),
        in_specs=[main_spec(0), main_spec(nj), main_spec(0), next_spec(0), main_spec(0), next_spec(0), main_spec(0), next_spec(0),
                  par_spec(FFN_KERNEL, 0), par_spec(FFN_KERNEL, nj)],
        out_specs=[out_tile, out_tile, par_spec(FFN_KERNEL, 0), par_spec(FFN_KERNEL, 0), par_spec(1, 0), par_spec(1, 0)],
        out_shape=[jax.ShapeDtypeStruct((t, f), BF16), jax.ShapeDtypeStruct((t, f), BF16),
                   jax.ShapeDtypeStruct((FFN_KERNEL, f), F32), jax.ShapeDtypeStruct((FFN_KERNEL, f), F32),
                   jax.ShapeDtypeStruct((1, f), F32), jax.ShapeDtypeStruct((1, f), F32)],
        sem=("parallel", "arbitrary"), args=(h, h, cg, cg, cv, cv, du, du, cw, cw), comm=comm)
    return outs if comm is None else (outs, moved)


MIX_HALO = 32


def _glu(hh):
    return hh[:, 0:A_WIDTH] * _sigmoid(hh[:, A_WIDTH:2 * A_WIDTH])


def _fill_row_shifts(s):
    rows = s.shape[1] - SUBLANES
    for j in range(1, SUBLANES):
        s[j, 0:rows, :] = s[0, pl.ds(j, rows), :]


def _rows_from(s, start, rows):
    j = start % SUBLANES
    return s[j, start - j:start - j + rows, :]


def _tril_mask():
    return lax.broadcasted_iota(jnp.int32, (B_CHUNK, B_CHUNK), 0) >= lax.broadcasted_iota(jnp.int32, (B_CHUNK, B_CHUNK), 1)


def _spatial_mix(q, ms_ref, sbt_ref, tm):
    mask = _tril_mask()
    ws = [jnp.where(mask, ms_ref[g], 0.0).astype(BF16) for g in range(B_GROUPS)]
    qb = q.astype(BF16)
    rows = []
    for c in range(tm // B_CHUNK):
        cols = [_dot(ws[g], qb[c * B_CHUNK:(c + 1) * B_CHUNK, g * 128:(g + 1) * 128], NN) + sbt_ref[:, g:g + 1]
                for g in range(B_GROUPS)]
        rows.append(jnp.concatenate(cols, axis=1))
    return jnp.concatenate(rows, axis=0)


def _mixer_mid_fwd(h, cw, cb, ag, ab, bg, bb, ms, sbt, name, tm=256, comm=None):
    t = h.shape[0]
    nt, hb = t // tm, tm // MIX_HALO
    o = MIX_HALO - A_KERNEL + 1

    def body(h_ref, hp_ref, cw_ref, cb_ref, ag_ref, ab_ref, bg_ref, bb_ref, ms_ref, sbt_ref, cat_ref, y_ref, sp):
        i = pl.program_id(0)
        sp[0, 0:MIX_HALO, :] = jnp.where(i > 0, _glu(hp_ref[:, 0:2 * A_WIDTH].astype(F32)), 0.0)
        sp[0, MIX_HALO:, :] = _glu(h_ref[:, 0:2 * A_WIDTH].astype(F32))
        _fill_row_shifts(sp)
        y = jnp.zeros((tm, A_WIDTH), F32) + cb_ref[...]
        for k in range(A_KERNEL):
            y = y + cw_ref[k:k + 1, :] * _rows_from(sp, o + k, tm)
        y_ref[...] = y.astype(BF16)
        nh, _ = _ln_stats(y)
        ln = nh * ag_ref[...] + ab_ref[...]
        cat_ref[:, 0:A_WIDTH] = (ln * _sigmoid(ln)).astype(BF16)
        u = _gelu(h_ref[:, 1024:1536].astype(F32))
        nb, _ = _ln_stats(_gelu(h_ref[:, 1536:2048].astype(F32)))
        mixed = _spatial_mix(nb * bg_ref[...] + bb_ref[...], ms_ref, sbt_ref, tm)
        cat_ref[:, A_WIDTH:] = (u * mixed).astype(BF16)

    vec = pl.BlockSpec((1, A_WIDTH), lambda i: (0, 0))
    outs, moved = _call(
        body, name=name, grid=(nt,),
        in_specs=[pl.BlockSpec((tm, 2048), lambda i: (i, 0)),
                  pl.BlockSpec((MIX_HALO, 2048), lambda i: (jnp.maximum(i * hb - 1, 0), 0)),
                  pl.BlockSpec((A_KERNEL, A_WIDTH), lambda i: (0, 0)), vec, vec, vec, vec, vec,
                  pl.BlockSpec((B_GROUPS, B_CHUNK, B_CHUNK), lambda i: (0, 0, 0)),
                  pl.BlockSpec((B_CHUNK, B_GROUPS), lambda i: (0, 0))],
        out_specs=[pl.BlockSpec((tm, D_MODEL), lambda i: (i, 0)), pl.BlockSpec((tm, A_WIDTH), lambda i: (i, 0))],
        out_shape=[jax.ShapeDtypeStruct((t, D_MODEL), BF16), jax.ShapeDtypeStruct((t, A_WIDTH), BF16)],
        scratch_shapes=[pltpu.VMEM((SUBLANES, tm + MIX_HALO, A_WIDTH), F32)],
        sem=("parallel",), args=(h, h, cw, cb, ag, ab, bg, bb, ms, sbt), comm=comm)
    return outs if comm is None else (outs, moved)


def _mixer_mid_bwd(h, y, dcat, cw, ag, ab, bg, bb, ms, mst, sbt, name, tm=256, comm=None):
    t = h.shape[0]
    nt, hb = t // tm, tm // MIX_HALO
    r = tm + MIX_HALO
    nchunk = tm // B_CHUNK

    def body(h_ref, y_ref, yn_ref, dc_ref, dcn_ref, cw_ref, ag_ref, ab_ref, bg_ref, bb_ref, ms_ref, mst_ref, sbt_ref,
             dh_ref, dcw_ref, dcb_ref, dag_ref, dab_ref, dbg_ref, dbb_ref, dms_ref, dsb_ref, sdy, sbacc):
        i = pl.program_id(0)

        @pl.when(i == 0)
        def _():
            for ref in (dcw_ref, dcb_ref, dag_ref, dab_ref, dbg_ref, dbb_ref, dms_ref, dsb_ref, sbacc):
                ref[...] = jnp.zeros_like(ref)

        nh, rs = _ln_stats(jnp.concatenate([y_ref[...].astype(F32), yn_ref[...].astype(F32)], axis=0))
        ln = nh * ag_ref[...] + ab_ref[...]
        sg = _sigmoid(ln)
        dao = jnp.concatenate([dc_ref[:, 0:A_WIDTH].astype(F32),
                               jnp.where(i < nt - 1, dcn_ref[:, 0:A_WIDTH].astype(F32), 0.0)], axis=0)
        dln = dao * (sg * (1.0 + ln * (1.0 - sg)))
        dag_ref[...] += _colsum(dln[0:tm] * nh[0:tm])
        dab_ref[...] += _colsum(dln[0:tm])
        sdy[0] = _ln_bwd_rows(dln * ag_ref[...], nh, rs)
        _fill_row_shifts(sdy)
        dcb_ref[...] += _colsum(sdy[0, 0:tm, :])
        av = h_ref[:, 0:A_WIDTH].astype(F32)
        s = _sigmoid(h_ref[:, A_WIDTH:2 * A_WIDTH].astype(F32))
        p_own = av * s
        dp = jnp.zeros((tm, A_WIDTH), F32)
        for k in range(A_KERNEL):
            later = _rows_from(sdy, A_KERNEL - 1 - k, tm)
            dcw_ref[k:k + 1, :] += _colsum(later * p_own)
            dp = dp + cw_ref[k:k + 1, :] * later
        dh_ref[:, 0:A_WIDTH] = (dp * s).astype(BF16)
        dh_ref[:, A_WIDTH:2 * A_WIDTH] = (dp * av * s * (1.0 - s)).astype(BF16)

        u, dgu = _gelu_and_grad(h_ref[:, 1024:1536].astype(F32))
        w, dgw = _gelu_and_grad(h_ref[:, 1536:2048].astype(F32))
        nb, rb = _ln_stats(w)
        q = nb * bg_ref[...] + bb_ref[...]
        mixed = _spatial_mix(q, ms_ref, sbt_ref, tm)
        dbo = dc_ref[:, A_WIDTH:].astype(F32)
        dh_ref[:, 1024:1536] = (dbo * mixed * dgu).astype(BF16)
        dmx = dbo * u
        mask = _tril_mask()
        wst = [jnp.where(mask.T, mst_ref[g], 0.0).astype(BF16) for g in range(B_GROUPS)]
        qb = q.astype(BF16)
        dmb = dmx.astype(BF16)
        rows = []
        for c in range(nchunk):
            cols = []
            for g in range(B_GROUPS):
                rs_, cs_ = slice(c * B_CHUNK, (c + 1) * B_CHUNK), slice(g * 128, (g + 1) * 128)
                sbacc[g] += dmx[rs_, cs_]
                dms_ref[g] += _dot(dmb[rs_, cs_], qb[rs_, cs_], NT)
                cols.append(_dot(wst[g], dmb[rs_, cs_], NN))
            rows.append(jnp.concatenate(cols, axis=1))
        dq = jnp.concatenate(rows, axis=0)
        dbg_ref[...] += _colsum(dq * nb)
        dbb_ref[...] += _colsum(dq)
        dh_ref[:, 1536:2048] = (_ln_bwd_rows(dq * bg_ref[...], nb, rb) * dgw).astype(BF16)

        @pl.when(i == nt - 1)
        def _():
            for g in range(B_GROUPS):
                dms_ref[g] = jnp.where(mask, dms_ref[g], 0.0)
                dsb_ref[g] = jnp.sum(sbacc[g], axis=1, keepdims=True)

    last_blk = t // MIX_HALO - 1
    vec = pl.BlockSpec((1, A_WIDTH), lambda i: (0, 0))
    mat = pl.BlockSpec((B_GROUPS, B_CHUNK, B_CHUNK), lambda i: (0, 0, 0))
    taps = pl.BlockSpec((A_KERNEL, A_WIDTH), lambda i: (0, 0))

    def halo(width):
        return pl.BlockSpec((MIX_HALO, width), lambda i: (jnp.minimum((i + 1) * hb, last_blk), 0))

    vshape = jax.ShapeDtypeStruct((1, A_WIDTH), F32)
    outs, moved = _call(
        body, name=name, grid=(nt,),
        in_specs=[pl.BlockSpec((tm, 2048), lambda i: (i, 0)), pl.BlockSpec((tm, A_WIDTH), lambda i: (i, 0)), halo(A_WIDTH),
                  pl.BlockSpec((tm, D_MODEL), lambda i: (i, 0)), halo(D_MODEL),
                  taps, vec, vec, vec, vec, mat, mat, pl.BlockSpec((B_CHUNK, B_GROUPS), lambda i: (0, 0))],
        out_specs=[pl.BlockSpec((tm, 2048), lambda i: (i, 0)), taps, vec, vec, vec, vec, vec, mat,
                   pl.BlockSpec((B_GROUPS, B_CHUNK, 1), lambda i: (0, 0, 0))],
        out_shape=[jax.ShapeDtypeStruct((t, 2048), BF16), jax.ShapeDtypeStruct((A_KERNEL, A_WIDTH), F32),
                   vshape, vshape, vshape, vshape, vshape,
                   jax.ShapeDtypeStruct((B_GROUPS, B_CHUNK, B_CHUNK), F32), jax.ShapeDtypeStruct((B_GROUPS, B_CHUNK, 1), F32)],
        scratch_shapes=[pltpu.VMEM((SUBLANES, r, A_WIDTH), F32), pltpu.VMEM((B_GROUPS, B_CHUNK, B_CHUNK), F32)],
        sem=("arbitrary",), args=(h, y, y, dcat, dcat, cw, ag, ab, bg, bb, ms, mst, sbt), comm=comm)
    return outs if comm is None else (outs, moved)


Q_WIDTH = N_Q_HEADS * HEAD_DIM
KV_WIDTH = 2 * N_KV_HEADS * HEAD_DIM
PAIRS_PER_KV = N_Q_HEADS // N_KV_HEADS // 2
ATT_SCALE = 1.0 / math.sqrt(HEAD_DIM)


def _dup_heads(pair_cols, kv_head):
    lane = lax.broadcasted_iota(jnp.int32, pair_cols.shape, 1)
    rolled = pltpu.roll(pair_cols, HEAD_DIM, 1)
    first = lane < HEAD_DIM
    return jnp.where(first, pair_cols, rolled) if kv_head == 0 else jnp.where(first, rolled, pair_cols)


HEADS_PER_KV = N_Q_HEADS // N_KV_HEADS


def _stack_heads(ref, kh):
    lane = lax.broadcasted_iota(jnp.int32, (ATT_BLOCK, 128), 1)
    rows = []
    for pr in range(PAIRS_PER_KV):
        c0 = (kh * PAIRS_PER_KV + pr) * 128
        pair = ref[:, c0:c0 + 128]
        rows += [jnp.where(lane < HEAD_DIM, pair, jnp.zeros_like(pair)), jnp.where(lane < HEAD_DIM, jnp.zeros_like(pair), pair)]
    return jnp.concatenate(rows, axis=0)


def _unstack_heads(stacked, kh, write):
    lane = lax.broadcasted_iota(jnp.int32, (ATT_BLOCK, 128), 1)
    for pr in range(PAIRS_PER_KV):
        first = stacked[(2 * pr) * ATT_BLOCK:(2 * pr + 1) * ATT_BLOCK]
        second = stacked[(2 * pr + 1) * ATT_BLOCK:(2 * pr + 2) * ATT_BLOCK]
        write((kh * PAIRS_PER_KV + pr) * 128, jnp.where(lane < HEAD_DIM, first, second))


def _sink_row(sink_ref, kh):
    return jnp.concatenate([jnp.full((1, ATT_BLOCK), sink_ref[0, kh * HEADS_PER_KV + h], F32) for h in range(HEADS_PER_KV)], axis=1)


def _att_window_bias():
    sj = lax.broadcasted_iota(jnp.int32, (2 * ATT_BLOCK, HEADS_PER_KV * ATT_BLOCK), 0)
    qi = lax.broadcasted_iota(jnp.int32, (2 * ATT_BLOCK, HEADS_PER_KV * ATT_BLOCK), 1) & (ATT_BLOCK - 1)
    diff = qi + ATT_BLOCK - sj
    return jnp.where((diff >= 0) & (diff < ATT_BLOCK), 0.0, -jnp.inf)


def _att_probs_t(q_all, k2, bias_ref, n, sink):
    st = _dot(k2, q_all, NT) * ATT_SCALE + bias_ref[...]
    st = jnp.concatenate([jnp.where(n > 0, st[0:ATT_BLOCK], -jnp.inf), st[ATT_BLOCK:]], axis=0)
    m = jnp.maximum(jnp.max(st, axis=0, keepdims=True), sink)
    e = jnp.exp(st - m)
    es = jnp.exp(sink - m)
    inv = 1.0 / (jnp.sum(e, axis=0, keepdims=True) + es)
    return e * inv, es * inv


def _attn_fwd(qkv, sinks, name, comm=None):
    t = qkv.shape[0]
    nb = t // ATT_BLOCK
    kvb = Q_WIDTH // KV_WIDTH

    def body(sink_ref, q_ref, kv_ref, kvp_ref, o_ref, bias):
        n = pl.program_id(0)

        @pl.when(n == 0)
        def _():
            bias[...] = _att_window_bias()

        kv = jnp.concatenate([kvp_ref[...], kv_ref[...]], axis=0).astype(F32)

        def write(c0, pair):
            o_ref[:, c0:c0 + 128] = pair.astype(BF16)

        for kh in range(N_KV_HEADS):
            k2 = _dup_heads(kv[:, 0:128], kh).astype(BF16)
            v2 = _dup_heads(kv[:, 128:256], kh).astype(BF16)
            pt, _ = _att_probs_t(_stack_heads(q_ref, kh), k2, bias, n, _sink_row(sink_ref, kh))
            _unstack_heads(_dot(v2, pt, TN).T, kh, write)

    (out,), moved = _call(
        body, name=name, grid=(nb,),
        in_specs=[pl.BlockSpec(memory_space=pltpu.SMEM),
                  pl.BlockSpec((ATT_BLOCK, Q_WIDTH), lambda n: (n, 0)),
                  pl.BlockSpec((ATT_BLOCK, KV_WIDTH), lambda n: (n, kvb)),
                  pl.BlockSpec((ATT_BLOCK, KV_WIDTH), lambda n: (jnp.maximum(n - 1, 0), kvb))],
        out_specs=[pl.BlockSpec((ATT_BLOCK, Q_WIDTH), lambda n: (n, 0))],
        out_shape=[jax.ShapeDtypeStruct((t, Q_WIDTH), BF16)],
        scratch_shapes=[pltpu.VMEM((2 * ATT_BLOCK, HEADS_PER_KV * ATT_BLOCK), F32)],
        sem=("arbitrary",), args=(sinks, qkv, qkv, qkv), comm=comm)
    return out if comm is None else (out, moved)


def _attn_bwd(qkv, d_o, sinks, name, comm=None):
    t = qkv.shape[0]
    nb = t // ATT_BLOCK
    kvb = Q_WIDTH // KV_WIDTH

    def body(sink_ref, q_ref, kv_ref, kvp_ref, do_ref, dq_ref, dkv_ref, dbq_ref, dbkv_ref, dsink_ref, carry, bias):
        n = pl.program_id(0)

        @pl.when(n == 0)
        def _():
            for ref in (dbq_ref, dbkv_ref, dsink_ref, carry):
                ref[...] = jnp.zeros_like(ref)
            dkv_ref[...] = jnp.zeros_like(dkv_ref)
            bias[...] = _att_window_bias()

        @pl.when(n < nb)
        def _():
            kv = jnp.concatenate([kvp_ref[...], kv_ref[...]], axis=0).astype(F32)
            lane2 = lax.broadcasted_iota(jnp.int32, (2 * ATT_BLOCK, 128), 1)
            sink_lane = lax.broadcasted_iota(jnp.int32, (1, 128), 1)
            dsink = jnp.zeros((1, 128), F32)
            dk_parts, dv_parts = [], []

            def write(c0, pair):
                dbq_ref[:, c0:c0 + 128] += _colsum(pair)
                dq_ref[:, c0:c0 + 128] = pair.astype(BF16)

            for kh in range(N_KV_HEADS):
                k2 = _dup_heads(kv[:, 0:128], kh).astype(BF16)
                v2 = _dup_heads(kv[:, 128:256], kh).astype(BF16)
                q_all = _stack_heads(q_ref, kh)
                do_all = _stack_heads(do_ref, kh)
                pt, ps = _att_probs_t(q_all, k2, bias, n, _sink_row(sink_ref, kh))
                dpt = _dot(v2, do_all, NT)
                delta = jnp.sum(pt * dpt, axis=0, keepdims=True)
                dst = pt * (dpt - delta) * ATT_SCALE
                psd = ps * delta
                for h in range(HEADS_PER_KV):
                    dsink = dsink + jnp.where(sink_lane == kh * HEADS_PER_KV + h,
                                              -jnp.sum(psd[:, h * ATT_BLOCK:(h + 1) * ATT_BLOCK]), 0.0)
                _unstack_heads(_dot(k2, dst, TN).T, kh, write)
                dk_acc = _dot(dst, q_all, NN)
                dv_acc = _dot(pt, do_all, NN)
                dk_parts.append(dk_acc + pltpu.roll(dk_acc, HEAD_DIM, 1))
                dv_parts.append(dv_acc + pltpu.roll(dv_acc, HEAD_DIM, 1))
            dk = jnp.where(lane2 < HEAD_DIM, dk_parts[0], dk_parts[1])
            dv = jnp.where(lane2 < HEAD_DIM, dv_parts[0], dv_parts[1])
            dkv_new = jnp.concatenate([dk, dv], axis=1)
            done = carry[...] + dkv_new[0:ATT_BLOCK]

            @pl.when(n > 0)
            def _():
                dkv_ref[...] = done.astype(BF16)
                dbkv_ref[...] += _colsum(done)

            carry[...] = dkv_new[ATT_BLOCK:]
            dsink_ref[...] += dsink

        @pl.when(n == nb)
        def _():
            dkv_ref[...] = carry[...].astype(BF16)
            dbkv_ref[...] += _colsum(carry[...])

    def clamp(n):
        return jnp.minimum(n, nb - 1)

    outs, moved = _call(
        body, name=name, grid=(nb + 1,),
        in_specs=[pl.BlockSpec(memory_space=pltpu.SMEM),
                  pl.BlockSpec((ATT_BLOCK, Q_WIDTH), lambda n: (clamp(n), 0)),
                  pl.BlockSpec((ATT_BLOCK, KV_WIDTH), lambda n: (clamp(n), kvb)),
                  pl.BlockSpec((ATT_BLOCK, KV_WIDTH), lambda n: (jnp.maximum(clamp(n) - 1, 0), kvb)),
                  pl.BlockSpec((ATT_BLOCK, Q_WIDTH), lambda n: (clamp(n), 0))],
        out_specs=[pl.BlockSpec((ATT_BLOCK, Q_WIDTH), lambda n: (clamp(n), 0)),
                   pl.BlockSpec((ATT_BLOCK, KV_WIDTH), lambda n: (jnp.maximum(n - 1, 0), 0)),
                   pl.BlockSpec((1, Q_WIDTH), lambda n: (0, 0)),
                   pl.BlockSpec((1, KV_WIDTH), lambda n: (0, 0)),
                   pl.BlockSpec((1, 128), lambda n: (0, 0))],
        out_shape=[jax.ShapeDtypeStruct((t, Q_WIDTH), BF16), jax.ShapeDtypeStruct((t, KV_WIDTH), BF16),
                   jax.ShapeDtypeStruct((1, Q_WIDTH), F32), jax.ShapeDtypeStruct((1, KV_WIDTH), F32),
                   jax.ShapeDtypeStruct((1, 128), F32)],
        scratch_shapes=[pltpu.VMEM((ATT_BLOCK, KV_WIDTH), F32), pltpu.VMEM((2 * ATT_BLOCK, HEADS_PER_KV * ATT_BLOCK), F32)],
        sem=("arbitrary",), args=(sinks, qkv, qkv, qkv, d_o), comm=comm)
    return outs if comm is None else (outs, moved)


def _adamw_math(g, w, m, v):
    m = ADAM_B1 * m + (1.0 - ADAM_B1) * g
    v = ADAM_B2 * v + (1.0 - ADAM_B2) * (g * g)
    m_hat = m / (1.0 - ADAM_B1 ** ADAM_STEP)
    v_hat = v / (1.0 - ADAM_B2 ** ADAM_STEP)
    delta = -ADAM_LR * (m_hat / (jnp.sqrt(v_hat) + ADAM_EPS) + ADAM_WD * w)
    return delta, m, v


def _sum_partials(p_ref):
    g = p_ref[0].astype(F32)
    for s in range(1, N_DEV):
        g = g + p_ref[s].astype(F32)
    return g


def _adamw_big(parts, w, m, v, name, tr):
    r, c = w.shape
    parts = [p if isinstance(p, tuple) else (p, 0, p.shape[1]) for p in parts]
    tiles = [rows // tr for _, _, rows in parts]
    starts = [sum(tiles[:l]) for l in range(len(parts))]
    assert all(lo % tr == 0 and rows % tr == 0 for _, lo, rows in parts) and sum(tiles) * tr == r

    def body(*refs):
        p_refs, (w_ref, m_ref, v_ref, g_out, d_out, m_out, v_out) = refs[:len(parts)], refs[len(parts):]
        i = pl.program_id(0)
        for l, p_ref in enumerate(p_refs):
            @pl.when((i >= starts[l]) & (i < starts[l] + tiles[l]))
            def _():
                g = _sum_partials(p_ref)
                g_out[...] = g
                d_out[...], m_out[...], v_out[...] = _adamw_math(g, w_ref[...], m_ref[...], v_ref[...])

    def part_spec(l):
        return pl.BlockSpec((N_DEV, tr, c), lambda i: (0, jnp.clip(i - starts[l], 0, tiles[l] - 1) + parts[l][1] // tr, 0))

    tile = pl.BlockSpec((tr, c), lambda i: (i, 0))
    shape = jax.ShapeDtypeStruct((r, c), F32)
    return pl.pallas_call(
        body, name=name, grid=(r // tr,),
        in_specs=[part_spec(l) for l in range(len(parts))] + [tile, tile, tile],
        out_specs=[tile] * 4, out_shape=[shape] * 4,
        compiler_params=_params("parallel"),
    )(*[p[0] for p in parts], w, m, v)


def _adamw_small(parts, ws, ms, vs, name):
    n = len(ws)

    def body(*refs):
        ins, outs = refs[:4 * n], refs[4 * n:]
        for a in range(n):
            g = _sum_partials(ins[a])
            outs[4 * a][...] = g
            outs[4 * a + 1][...], outs[4 * a + 2][...], outs[4 * a + 3][...] = _adamw_math(
                g, ins[n + a][...], ins[2 * n + a][...], ins[3 * n + a][...])

    out_shape = []
    for w in ws:
        out_shape += [jax.ShapeDtypeStruct(w.shape, F32)] * 4
    return pl.pallas_call(body, name=name, out_shape=out_shape, compiler_params=_params())(*parts, *ws, *ms, *vs)


PACK_LANES = 128
PACK_ROWS = 8


def _pack(arrs):
    flat = jnp.concatenate([a.reshape(-1).astype(F32) for a in arrs])
    unit = PACK_LANES * PACK_ROWS
    total = -(-flat.shape[0] // unit) * unit
    return jnp.pad(flat, (0, total - flat.shape[0])).reshape(-1, PACK_LANES)


def _unpack(buf, shapes):
    flat = buf.reshape(N_DEV, -1)
    out, pos = [], 0
    for s in shapes:
        size = math.prod(s)
        out.append(flat[:, pos:pos + size].reshape((N_DEV,) + tuple(s)))
        pos += size
    return out


def _interleave(g):
    return jnp.transpose(g, (1, 0, 2)).reshape(g.shape[1], -1)


def _ffn_backward(dz, dzb, x_in, z_in, g_in, h, cg, cv, u, w_up_t, cw, w_down, tag, exchange=(), exchange_late=(), own_rows=0):
    du = _matmul(dzb, w_down, "nt", BF16, f"ffn{tag}_du", 2048, 1408, 1024)
    d_w_down = _matmul(u, dzb, "tn", BF16, f"ffn{tag}_dwdown", 1408, 1024, 2048)
    (dhg, dhv, dcwg, dcwv, dcbg, dcbv), moved = _ffn_mid_bwd(
        h, cg, cv, du, cw, f"ffn{tag}_mid_bwd", comm=_Comm(exchange=[d_w_down.reshape(N_DEV, -1, D_MODEL), *exchange]))
    d_w_up_t = _matmul_tn_pair(dhg, dhv, x_in, BF16, f"ffn{tag}_dwup", 1408, 1024, 2048,
                               comm=_Comm(exchange=exchange_late) if exchange_late else None)
    if exchange_late:
        d_w_up_t, late = d_w_up_t
        moved = moved + late
    d_up_blocks = d_w_up_t.reshape(N_DEV, -1, D_MODEL)
    outs = _matmul_ln_bwd([(dhg, 0), (dhv, D_FF)], w_up_t, z_in, g_in, dz, f"ffn{tag}_dx_ln_bwd", 256,
                          comm=_Comm(exchange=[(d_up_blocks, 0, own_rows)]) if own_rows else None)
    (dz_in, dzb_in, dg_in, db_in), own = outs if own_rows else (outs, [])
    moved = moved + own
    return (dz_in, dzb_in, dg_in, db_in, d_up_blocks,
            jnp.concatenate([dcwg, dcwv], axis=1), jnp.concatenate([dcbg, dcbv], axis=1), moved)


def kernel(x, ab_w_in, a_conv_w, a_conv_b, a_norm_g, a_norm_b, b_norm_g, b_norm_b, b_spatial_w, b_spatial_b, ab_w_out, c_w_qkv, c_b_qkv, c_sinks, c_w_o, ffn_w_up, ffn_conv_w, ffn_conv_b, ffn_w_down, ln_g, ln_b, loss_target, m_ab_w_in, m_a_conv_w, m_a_conv_b, m_a_norm_g, m_a_norm_b, m_b_norm_g, m_b_norm_b, m_b_spatial_w, m_b_spatial_b, m_ab_w_out, m_c_w_qkv, m_c_b_qkv, m_c_sinks, m_c_w_o, m_ffn_w_up, m_ffn_conv_w, m_ffn_conv_b, m_ffn_w_down, m_ln_g, m_ln_b, v_ab_w_in, v_a_conv_w, v_a_conv_b, v_a_norm_g, v_a_norm_b, v_b_norm_g, v_b_norm_b, v_b_spatial_w, v_b_spatial_b, v_ab_w_out, v_c_w_qkv, v_c_b_qkv, v_c_sinks, v_c_w_o, v_ffn_w_up, v_ffn_conv_w, v_ffn_conv_b, v_ffn_w_down, v_ln_g, v_ln_b):
    me = 4 * lax.axis_index("x") + 2 * lax.axis_index("y") + lax.axis_index("c")
    xt = x[0]
    t = xt.shape[0]

    small_shard_shapes = [a_conv_w.shape, c_b_qkv.shape, ffn_conv_w.shape, ln_g.shape, ln_b.shape]
    up_shard = [jnp.swapaxes(ffn_w_up[l], 0, 1).astype(BF16) for l in range(2)]
    qkv_shard = jnp.swapaxes(c_w_qkv[0], 0, 1).astype(BF16)
    down_shard = [ffn_w_down[l].astype(BF16) for l in range(2)]
    g_win, g_small = _comm_only(
        _Comm(gather=[jnp.swapaxes(ab_w_in[0], 0, 1).astype(BF16), _pack([a_conv_w, c_b_qkv, ffn_conv_w, ln_g, ln_b])]),
        "gather_first")
    w_in = g_win.reshape(-1, D_MODEL)
    g_acw, g_bqkv, g_fcw, g_lng, g_lnb = _unpack(g_small, small_shard_shapes)
    acw = _interleave(g_acw[:, 0])
    bqkv = g_bqkv[:, 0].reshape(1, -1)
    fcw = [_interleave(g_fcw[:, l]) for l in range(2)]
    lng = jnp.transpose(g_lng, (1, 2, 0, 3)).reshape(2, 2, 1, D_MODEL)
    lnb = jnp.transpose(g_lnb, (1, 2, 0, 3)).reshape(2, 2, 1, D_MODEL)
    fcb = [ffn_conv_b[l:l + 1] for l in range(2)]
    ms = b_spatial_w[0]
    mst = jnp.swapaxes(ms, 1, 2)
    sbt = b_spatial_b[0].T

    q_up = up_shard[0].shape[0] // 4
    h0, (g_wout, g_wup0) = _matmul(xt, w_in, "nt", BF16, "mix_in", 2048, 1024, 1024,
                                   comm=_Comm(gather=[ab_w_out[0].astype(BF16), (up_shard[0], 0, q_up, None)]))
    w_out = g_wout.reshape(D_MODEL, D_MODEL)
    (cat, y0), (g_wup0,) = _mixer_mid_fwd(h0, acw, a_conv_b, a_norm_g, a_norm_b, b_norm_g, b_norm_b, ms, sbt, "mix_mid_fwd",
                                          comm=_Comm(gather=[(up_shard[0], q_up, 2 * q_up, g_wup0)]))
    (z1, x1), (g_wup0,) = _matmul_res_ln(cat, w_out, xt, lng[0, 0], lnb[0, 0], "mix_out_ln", 512,
                                         comm=_Comm(gather=[(up_shard[0], 3 * q_up, q_up, g_wup0)]))
    w_up0 = g_wup0.reshape(2 * D_FF, D_MODEL)
    hf0, (g_wdown0, g_wqkv) = _matmul(x1, w_up0, "nt", BF16, "ffn0_up", 2048, 1408, 1024,
                                      comm=_Comm(gather=[down_shard[0], qkv_shard]))
    w_down0 = g_wdown0.reshape(D_FF, D_MODEL)
    w_qkv = g_wqkv.reshape(Q_WIDTH + KV_WIDTH, D_MODEL)
    (u0, cg0, cv0), (g_wup1,) = _ffn_mid_fwd(hf0, fcw[0], fcb[0], "ffn0_mid_fwd",
                                             comm=_Comm(gather=[(up_shard[1], 0, 3 * q_up, None)]))
    (z2, x2), (g_wo, g_wup1) = _matmul_res_ln(
        u0, w_down0, z1, lng[0, 1], lnb[0, 1], "ffn0_down_ln", 512, prev=(lng[0, 0], lnb[0, 0]),
        comm=_Comm(gather=[c_w_o[0].astype(BF16), (up_shard[1], 3 * q_up, q_up, g_wup1)]))
    w_o = g_wo.reshape(D_MODEL, D_MODEL)
    w_up1 = g_wup1.reshape(2 * D_FF, D_MODEL)
    qkv = _matmul(x2, w_qkv, "nt", BF16, "att_qkv", 1024, 1280, 1024, bias=bqkv)
    att, (g_wdown1,) = _attn_fwd(qkv, c_sinks, "att_fwd", comm=_Comm(gather=[down_shard[1]]))
    w_down1 = g_wdown1.reshape(D_FF, D_MODEL)
    z3, x3 = _matmul_res_ln(att, w_o, z2, lng[1, 0], lnb[1, 0], "att_out_ln", 512, prev=(lng[0, 1], lnb[0, 1]))
    hf1 = _matmul(x3, w_up1, "nt", BF16, "ffn1_up", 2048, 1408, 1024)
    u1, cg1, cv1 = _ffn_mid_fwd(hf1, fcw[1], fcb[1], "ffn1_mid_fwd")

    dz4, dz4b, dg11, db11, loss_terms = _matmul_res_ln_loss(u1, w_down1, z3, lng[1, 1], lnb[1, 1], loss_target[0],
                                                      "ffn1_down_ln_loss", 512, prev=(lng[1, 0], lnb[1, 0]))
    dz3, dz3b, dg10, db10, d_wup1, d_fcw1, d_fcb1, (p_wdown1,) = _ffn_backward(
        dz4, dz4b, x3, z3, lng[1, 0], hf1, cg1, cv1, u1, w_up1, fcw[1], w_down1, 1)
    d_att = _matmul(dz3b, w_o, "nt", BF16, "att_dout", 1024, 1024, 1024)
    d_wo = _matmul(att, dz3b, "tn", BF16, "att_dwo", 1024, 1024, 2048)
    rows_up = d_wup1.shape[1]
    first = 3 * rows_up // 4
    (dq, dkv, dbq, dbkv, dsinks), (p_wup1a,) = _attn_bwd(qkv, d_att, c_sinks, "att_bwd",
                                                        comm=_Comm(exchange=[(d_wup1, 0, first)]))
    d_wqkv = jnp.concatenate([_matmul(dq, x2, "tn", BF16, "att_dwq", 1024, 1024, 2048),
                              _matmul(dkv, x2, "tn", BF16, "att_dwkv", KV_WIDTH, 1024, 1024)], axis=0)
    dz2, dz2b, dg01, db01 = _matmul_ln_bwd([(dq, 0), (dkv, Q_WIDTH)], w_qkv, z2, lng[0, 1], dz3, "att_dx_ln_bwd", 512)
    early = rows_up // 2
    dz1, dz1b, dg00, db00, d_wup0, d_fcw0, d_fcb0, (p_wdown0, p_wup1b, p_wqkv, p_wo, p_wup0a) = _ffn_backward(
        dz2, dz2b, x1, z1, lng[0, 0], hf0, cg0, cv0, u0, w_up0, fcw[0], w_down0, 0, exchange=[(d_wup1, first, rows_up - first)],
        exchange_late=[d_wqkv.reshape(N_DEV, -1, D_MODEL), d_wo.reshape(N_DEV, -1, D_MODEL)], own_rows=early)
    dcat = _matmul(dz1b, w_out, "nt", BF16, "mix_dcat", 1024, 1024, 1024)
    d_wout = _matmul(cat, dz1b, "tn", BF16, "mix_dwout", 1024, 1024, 2048)
    (dh0, d_acw, d_acb, d_ang, d_anb, d_bng, d_bnb, d_ms, d_sb), (p_wup0b, p_wout) = _mixer_mid_bwd(
        h0, y0, dcat, acw, a_norm_g, a_norm_b, b_norm_g, b_norm_b, ms, mst, sbt, "mix_mid_bwd",
        comm=_Comm(exchange=[(d_wup0, early, rows_up - early), d_wout.reshape(N_DEV, -1, D_MODEL)]))
    d_bqkv = jnp.concatenate([dbq, dbkv], axis=1)
    d_lng = jnp.stack([jnp.stack([dg00, dg01]), jnp.stack([dg10, dg11])])
    d_lnb = jnp.stack([jnp.stack([db00, db01]), jnp.stack([db10, db11])])
    small_full = [d_acb, d_ang, d_anb, d_bng, d_bnb, d_ms, d_sb, dsinks[:, :N_Q_HEADS], jnp.concatenate([d_fcb0, d_fcb1], axis=0),
                  d_acw, d_bqkv, jnp.stack([d_fcw0, d_fcw1]), d_lng, d_lnb, loss_terms]
    d_win, (g_small_grads,) = _matmul(dh0, xt, "tn", BF16, "mix_dwin", 1024, 1024, 512, comm=_Comm(gather=[_pack(small_full)]))
    grad_x, (p_win,) = _matmul(dh0, w_in, "nn", F32, "mix_dx", 1024, 1024, 2048, res=dz1, res_scale=ALPHA,
                               comm=_Comm(exchange=[d_win.reshape(N_DEV, -1, D_MODEL)]))


    big = {}
    for nm, p, w, m, v, tr, transposed in [
            ("ab_w_in", [p_win], ab_w_in, m_ab_w_in, v_ab_w_in, 256, True),
            ("ab_w_out", [p_wout], ab_w_out, m_ab_w_out, v_ab_w_out, 128, False),
            ("c_w_qkv", [p_wqkv], c_w_qkv, m_c_w_qkv, v_c_w_qkv, 160, True), ("c_w_o", [p_wo], c_w_o, m_c_w_o, v_c_w_o, 128, False),
            ("ffn_w_up", [(p_wup0a, 0, early), (p_wup0b, early, rows_up - early), (p_wup1a, 0, first), (p_wup1b, first, rows_up - first)], ffn_w_up, m_ffn_w_up, v_ffn_w_up, 176, True),
            ("ffn_w_down", [p_wdown0, p_wdown1], ffn_w_down, m_ffn_w_down, v_ffn_w_down, 176, False)]:
        def two_d(a):
            a = jnp.swapaxes(a, 1, 2) if transposed else a
            return a.reshape(-1, a.shape[-1])

        def back(o):
            return jnp.swapaxes(o.reshape(w.shape[0], w.shape[2], w.shape[1]), 1, 2) if transposed else o.reshape(w.shape)

        outs = _adamw_big(p, two_d(w), two_d(m), two_d(v), "adamw_" + nm, tr)
        big[nm] = [back(o) for o in outs]

    *gs, loss_parts = _unpack(g_small_grads, [a.shape for a in small_full])
    loss = 0.5 / D_MODEL * jnp.sum(loss_parts)

    def my_shard(g, width):
        g = g.reshape(g.shape[:-1] + (N_DEV, width))
        return lax.dynamic_index_in_dim(g, me, axis=g.ndim - 2, keepdims=False)

    small_names = ["a_conv_b", "a_norm_g", "a_norm_b", "b_norm_g", "b_norm_b", "b_spatial_w", "b_spatial_b", "c_sinks", "ffn_conv_b",
                   "a_conv_w", "c_b_qkv", "ffn_conv_w", "ln_g", "ln_b"]
    small_w = [a_conv_b, a_norm_g, a_norm_b, b_norm_g, b_norm_b, b_spatial_w, b_spatial_b, c_sinks, ffn_conv_b,
               a_conv_w, c_b_qkv, ffn_conv_w, ln_g, ln_b]
    small_m = [m_a_conv_b, m_a_norm_g, m_a_norm_b, m_b_norm_g, m_b_norm_b, m_b_spatial_w, m_b_spatial_b, m_c_sinks, m_ffn_conv_b,
               m_a_conv_w, m_c_b_qkv, m_ffn_conv_w, m_ln_g, m_ln_b]
    small_v = [v_a_conv_b, v_a_norm_g, v_a_norm_b, v_b_norm_g, v_b_norm_b, v_b_spatial_w, v_b_spatial_b, v_c_sinks, v_ffn_conv_b,
               v_a_conv_w, v_c_b_qkv, v_ffn_conv_w, v_ln_g, v_ln_b]
    gs[9:] = [my_shard(g, w.shape[-1]) for g, w in zip(gs[9:], small_w[9:])]
    two_d = [(-1, w.shape[-1]) for w in small_w]
    outs = _adamw_small([g.reshape((N_DEV,) + w.reshape(s).shape) for g, w, s in zip(gs, small_w, two_d)],
                        [w.reshape(s) for w, s in zip(small_w, two_d)], [m.reshape(s) for m, s in zip(small_m, two_d)],
                        [v.reshape(s) for v, s in zip(small_v, two_d)], "adamw_small")
    small = {nm: [o.reshape(w.shape) for o in outs[4 * a:4 * a + 4]] for a, (nm, w) in enumerate(zip(small_names, small_w))}

    res = {**big, **small}
    order = ["ab_w_in", "a_conv_w", "a_conv_b", "a_norm_g", "a_norm_b", "b_norm_g", "b_norm_b", "b_spatial_w", "b_spatial_b", "ab_w_out",
             "c_w_qkv", "c_b_qkv", "c_sinks", "c_w_o", "ffn_w_up", "ffn_conv_w", "ffn_conv_b", "ffn_w_down", "ln_g", "ln_b"]
    return (loss, grad_x[None], *[res[nm][0] for nm in order], *[res[nm][1] for nm in order],
            *[res[nm][2] for nm in order], *[res[nm][3] for nm in order])
```

```python
import functools
import math

import jax
import jax.numpy as jnp
from jax import lax
from jax.experimental import pallas as pl
from jax.experimental.pallas import tpu as pltpu

F32 = jnp.float32
BF16 = jnp.bfloat16

N_DEV = 8
D_MODEL = 1024
A_WIDTH = 512
A_KERNEL = 31
B_GROUPS = 4
B_CHUNK = 128
HEAD_DIM = 64
N_Q_HEADS = 16
N_KV_HEADS = 2
ATT_BLOCK = 128
D_FF = 2816
FFN_KERNEL = 3
ALPHA = (2.0 * 2) ** 0.25
LN_EPS = 1e-5
GELU_K = math.sqrt(2.0 / math.pi)
GELU_C = 0.044715
ADAM_LR = 0.001
ADAM_B1 = 0.9
ADAM_B2 = 0.999
ADAM_EPS = 1e-08
ADAM_WD = 0.01
ADAM_STEP = 10
VMEM_LIMIT = 56 * 1024 * 1024
MESH_ID = pl.DeviceIdType.MESH


def _params(*sem):
    return pltpu.CompilerParams(dimension_semantics=sem, vmem_limit_bytes=VMEM_LIMIT)


def _gelu(x):
    t = jnp.tanh(GELU_K * x * (1.0 + GELU_C * x * x))
    return 0.5 * x * (1.0 + t)


def _gelu_and_grad(x):
    x2 = x * x
    t = jnp.tanh(GELU_K * x * (1.0 + GELU_C * x2))
    g = 0.5 * x * (1.0 + t)
    dg = 0.5 * (1.0 + t) + 0.5 * x * (1.0 - t * t) * (GELU_K * (1.0 + 3.0 * GELU_C * x2))
    return g, dg


def _sigmoid(x):
    return 1.0 / (1.0 + jnp.exp(-x))


def _ln_stats(z):
    mu = jnp.mean(z, axis=-1, keepdims=True)
    zc = z - mu
    var = jnp.mean(zc * zc, axis=-1, keepdims=True)
    r = lax.rsqrt(var + LN_EPS)
    return zc * r, r


def _ln_bwd_rows(dn, nh, r):
    return r * (dn - jnp.mean(dn, axis=-1, keepdims=True) - nh * jnp.mean(dn * nh, axis=-1, keepdims=True))


def _colsum(x):
    return jnp.sum(x, axis=0, keepdims=True)


def _dot(a, b, dims):
    return lax.dot_general(a.astype(BF16), b.astype(BF16), (dims, ((), ())), preferred_element_type=F32)


NN = ((1,), (0,))
NT = ((1,), (1,))
TN = ((0,), (0,))


ANY = pl.BlockSpec(memory_space=pl.ANY)
N_RELATIONS = N_DEV - 1


def _my_place():
    return lax.axis_index("x"), lax.axis_index("y"), lax.axis_index("c")


class _Comm:
    def __init__(self, gather=(), exchange=()):
        gather = [e if isinstance(e, tuple) else (e, 0, e.shape[0], None) for e in gather]
        exchange = [e if isinstance(e, tuple) else (e, 0, e.shape[1]) for e in exchange]
        self.arrs = [e[0] for e in gather] + [e[0] for e in exchange]
        self.n_gather = len(gather)
        self.n = len(self.arrs)
        self.rows = [pl.ds(lo, n) for _, lo, n, _ in gather] + [pl.ds(lo, n) for _, lo, n in exchange]
        self.into = {i: e[3] for i, e in enumerate(gather) if e[3] is not None}

    def out_shape(self):
        return [jax.ShapeDtypeStruct(((N_DEV,) + a.shape) if i < self.n_gather else a.shape, a.dtype)
                for i, a in enumerate(self.arrs)]

    def sems(self):
        return [pltpu.SemaphoreType.DMA((self.n, N_RELATIONS)), pltpu.SemaphoreType.DMA((self.n, N_RELATIONS)),
                pltpu.SemaphoreType.DMA((self.n,))]

    def _gather_copy(self, ins, outs, sems, a, k, place, to, from_input=False):
        px, py, pc = place
        block = outs[a].at[4 * px + 2 * py + pc, self.rows[a]]
        return pltpu.make_async_remote_copy(
            src_ref=ins[a].at[self.rows[a]] if from_input else block, dst_ref=block,
            send_sem=sems[0].at[a, k], recv_sem=sems[1].at[a, k], device_id=to, device_id_type=MESH_ID)

    def _exchange_copy(self, ins, outs, sems, a, k, landing=False):
        x, y, c = _my_place()
        me = 4 * x + 2 * y + c
        peer = (x ^ (k >> 2), y ^ ((k >> 1) & 1), c ^ (k & 1))
        return pltpu.make_async_remote_copy(
            src_ref=ins[a].at[me ^ k, self.rows[a]], dst_ref=outs[a].at[(me ^ k) if landing else me, self.rows[a]],
            send_sem=sems[0].at[a, k - 1], recv_sem=sems[1].at[a, k - 1], device_id=peer, device_id_type=MESH_ID)

    def _local_copy(self, ins, outs, sems, a):
        x, y, c = _my_place()
        me = 4 * x + 2 * y + c
        if a < self.n_gather:
            return pltpu.make_async_copy(ins[a].at[self.rows[a]], outs[a].at[me, self.rows[a]], sems[2].at[a])
        return pltpu.make_async_copy(ins[a].at[me, self.rows[a]], outs[a].at[me, self.rows[a]], sems[2].at[a])

    def _first_stage(self, ins, outs, sems, a):
        x, y, c = _my_place()
        me = (x, y, c)
        chips = [(1 - x, y), (x, 1 - y), (1 - x, 1 - y)]
        return ([self._gather_copy(ins, outs, sems, a, 0, me, (x, y, 1 - c), from_input=True)]
                + [self._gather_copy(ins, outs, sems, a, 1 + j, me, (*chip, c), from_input=True) for j, chip in enumerate(chips)])

    def start(self, ins, outs, sems):
        for a in range(self.n):
            self._local_copy(ins, outs, sems, a).start()
        for a in range(self.n_gather):
            for cp in self._first_stage(ins, outs, sems, a):
                cp.start()
        for k in range(1, N_DEV):
            for a in range(self.n_gather, self.n):
                self._exchange_copy(ins, outs, sems, a, k).start()

    def forward(self, ins, outs, sems):
        x, y, c = _my_place()
        me, sibling = (x, y, c), (x, y, 1 - c)
        for j, chip in enumerate([(1 - x, y), (x, 1 - y), (1 - x, 1 - y)]):
            for a in range(self.n_gather):
                self._gather_copy(ins, outs, sems, a, 1 + j, (*chip, c), me).wait_recv()
                self._gather_copy(ins, outs, sems, a, 4 + j, (*chip, c), sibling).start()

    def finish(self, ins, outs, sems):
        x, y, c = _my_place()
        me, sibling = (x, y, c), (x, y, 1 - c)
        chips = [(1 - x, y), (x, 1 - y), (1 - x, 1 - y)]
        passed = [self._gather_copy(ins, outs, sems, a, 4 + j, (*chip, c), sibling)
                  for j, chip in enumerate(chips) for a in range(self.n_gather)]
        for a in range(self.n_gather):
            self._gather_copy(ins, outs, sems, a, 0, sibling, me).wait_recv()
            for j, chip in enumerate(chips):
                self._gather_copy(ins, outs, sems, a, 4 + j, (*chip, 1 - c), me).wait_recv()
        for k in range(1, N_DEV):
            for a in range(self.n_gather, self.n):
                self._exchange_copy(ins, outs, sems, a, k, landing=True).wait_recv()
        for a in range(self.n_gather):
            for cp in self._first_stage(ins, outs, sems, a):
                cp.wait_send()
        for cp in passed:
            cp.wait_send()
        for k in range(1, N_DEV):
            for a in range(self.n_gather, self.n):
                self._exchange_copy(ins, outs, sems, a, k).wait_send()
        for a in range(self.n):
            self._local_copy(ins, outs, sems, a).wait()


def _comm_only(comm, name):
    assert not comm.into

    def body(*refs):
        ins, outs, sems = refs[:comm.n], refs[comm.n:2 * comm.n], refs[2 * comm.n:]
        comm.start(ins, outs, sems)
        comm.forward(ins, outs, sems)
        comm.finish(ins, outs, sems)

    return pl.pallas_call(body, name=name, in_specs=[ANY] * comm.n, out_specs=[ANY] * comm.n,
                          out_shape=comm.out_shape(), scratch_shapes=comm.sems())(*comm.arrs)


def _call(body, *, name, grid, in_specs, out_specs, out_shape, args, sem, scratch_shapes=(), comm=None):
    in_specs, out_specs, out_shape, scratch_shapes = list(in_specs), list(out_specs), list(out_shape), list(scratch_shapes)
    if comm is None:
        outs = pl.pallas_call(body, name=name, grid=grid, in_specs=in_specs, out_specs=out_specs, out_shape=out_shape,
                              scratch_shapes=scratch_shapes, compiler_params=_params(*sem))(*args)
        return list(outs), []
    n_in, n_out, n_scr, nc = len(in_specs), len(out_specs), len(scratch_shapes), comm.n
    completed = sorted(comm.into)

    def wrapped(*refs):
        ins, refs = refs[:n_in], refs[n_in:]
        c_in, refs = refs[:nc], refs[nc + len(completed):]
        outs, refs = refs[:n_out], refs[n_out:]
        c_out, refs = refs[:nc], refs[nc:]
        scr, sems = refs[:n_scr], refs[n_scr:]
        step = functools.reduce(lambda acc, ax: acc * grid[ax] + pl.program_id(ax), range(len(grid)), 0)
        steps = math.prod(grid)

        @pl.when(step == 0)
        def _():
            comm.start(c_in, c_out, sems)

        @pl.when(step == steps - 1)
        def _():
            comm.forward(c_in, c_out, sems)

        body(*ins, *outs, *scr)

        @pl.when(step == steps - 1)
        def _():
            comm.finish(c_in, c_out, sems)

    outs = pl.pallas_call(
        wrapped, name=name, grid=grid, in_specs=in_specs + [ANY] * (nc + len(completed)), out_specs=out_specs + [ANY] * nc,
        out_shape=out_shape + comm.out_shape(), scratch_shapes=scratch_shapes + comm.sems(),
        input_output_aliases={n_in + nc + pos: n_out + item for pos, item in enumerate(completed)},
        compiler_params=_params(*(["arbitrary"] * len(grid))))(*args, *comm.arrs, *[comm.into[item] for item in completed])
    return list(outs[:n_out]), list(outs[n_out:])


def _matmul(a, b, mode, out_dtype, name, tm, tn, tk, *, bias=None, res=None, res_scale=1.0, b_off=0, comm=None):
    tm = min(tm, a.shape[1] if mode == "tn" else a.shape[0])
    tk = min(tk, a.shape[0] if mode == "tn" else a.shape[1])
    if mode == "nn":
        (m, k), n = a.shape, b.shape[1]
        a_spec = pl.BlockSpec((tm, tk), lambda i, j, kk: (i, kk))
        b_spec = pl.BlockSpec((tk, tn), lambda i, j, kk: (kk + b_off, j))
        dims = NN
    elif mode == "nt":
        (m, k), n = a.shape, b.shape[0]
        a_spec = pl.BlockSpec((tm, tk), lambda i, j, kk: (i, kk))
        b_spec = pl.BlockSpec((tn, tk), lambda i, j, kk: (j, kk + b_off))
        dims = NT
    else:
        (k, m), n = a.shape, b.shape[1]
        a_spec = pl.BlockSpec((tk, tm), lambda i, j, kk: (kk, i))
        b_spec = pl.BlockSpec((tk, tn), lambda i, j, kk: (kk, j))
        dims = TN
    assert m % tm == 0 and n % tn == 0 and k % tk == 0, (name, m, n, k)
    nk = k // tk
    in_specs = [a_spec, b_spec]
    args = [a, b]
    if bias is not None:
        in_specs.append(pl.BlockSpec((1, tn), lambda i, j, kk: (0, j)))
        args.append(bias)
    if res is not None:
        in_specs.append(pl.BlockSpec((tm, tn), lambda i, j, kk: (i, j)))
        args.append(res)

    def finish(out, refs, o_ref):
        pos = 2
        if bias is not None:
            out = out + refs[pos][...]
            pos += 1
        if res is not None:
            out = out + res_scale * refs[pos][...].astype(F32)
        o_ref[...] = out.astype(out_dtype)

    def body_one_step(*refs):
        finish(_dot(refs[0][...], refs[1][...], dims), refs, refs[-1])

    def body(*refs):
        a_ref, b_ref = refs[0], refs[1]
        o_ref, acc = refs[-2], refs[-1]
        kk = pl.program_id(2)

        @pl.when(kk == 0)
        def _():
            acc[...] = jnp.zeros_like(acc)

        acc[...] += _dot(a_ref[...], b_ref[...], dims)

        @pl.when(kk == nk - 1)
        def _():
            finish(acc[...], refs, o_ref)

    (out,), moved = _call(
        body_one_step if nk == 1 else body, name=name, grid=(m // tm, n // tn, nk),
        in_specs=in_specs, out_specs=[pl.BlockSpec((tm, tn), lambda i, j, kk: (i, j))],
        out_shape=[jax.ShapeDtypeStruct((m, n), out_dtype)],
        scratch_shapes=[] if nk == 1 else [pltpu.VMEM((tm, tn), F32)],
        sem=("parallel", "parallel", "arbitrary"), args=args, comm=comm)
    return out if comm is None else (out, moved)


def _matmul_tn_pair(a0, a1, b, out_dtype, name, tm, tn, tk, comm=None):
    (k, m), n = a0.shape, b.shape[1]
    tk = min(tk, k)
    assert a1.shape == a0.shape and m % tm == 0 and n % tn == 0 and k % tk == 0, (name, m, n, k)
    mi, nk = m // tm, k // tk

    def body(a0_ref, a1_ref, b_ref, o_ref, acc):
        i, kk = pl.program_id(0), pl.program_id(2)

        @pl.when(kk == 0)
        def _():
            acc[...] = jnp.zeros_like(acc)

        @pl.when(i < mi)
        def _():
            acc[...] += _dot(a0_ref[...], b_ref[...], TN)

        @pl.when(i >= mi)
        def _():
            acc[...] += _dot(a1_ref[...], b_ref[...], TN)

        @pl.when(kk == nk - 1)
        def _():
            o_ref[...] = acc[...].astype(out_dtype)

    (out,), moved = _call(
        body, name=name, grid=(2 * mi, n // tn, nk),
        in_specs=[pl.BlockSpec((tk, tm), lambda i, j, kk: (jnp.where(i < mi, kk, nk - 1), jnp.minimum(i, mi - 1))),
                  pl.BlockSpec((tk, tm), lambda i, j, kk: (jnp.where(i >= mi, kk, 0), jnp.maximum(i - mi, 0))),
                  pl.BlockSpec((tk, tn), lambda i, j, kk: (kk, j))],
        out_specs=[pl.BlockSpec((tm, tn), lambda i, j, kk: (i, j))],
        out_shape=[jax.ShapeDtypeStruct((2 * m, n), out_dtype)],
        scratch_shapes=[pltpu.VMEM((tm, tn), F32)],
        sem=("parallel", "parallel", "arbitrary"), args=(a0, a1, b), comm=comm)
    return out if comm is None else (out, moved)


def _residual_input(x_ref, prev_refs):
    if not prev_refs:
        return x_ref[...]
    nh, _ = _ln_stats(x_ref[...])
    return nh * prev_refs[0][...] + prev_refs[1][...]


def _matmul_res_ln(a, b, x, g, beta, name, tm, prev=None, comm=None):
    t, k = a.shape
    d = b.shape[1]
    tm = min(tm, t)
    assert t % tm == 0
    n_prev = 0 if prev is None else 2

    def body(a_ref, b_ref, x_ref, g_ref, beta_ref, *rest):
        z_ref, xo_ref = rest[n_prev:]
        z = ALPHA * _residual_input(x_ref, rest[:n_prev]) + _dot(a_ref[...], b_ref[...], NN)
        nh, _ = _ln_stats(z)
        z_ref[...] = z
        xo_ref[...] = (nh * g_ref[...] + beta_ref[...]).astype(BF16)

    row = pl.BlockSpec((tm, d), lambda i: (i, 0))
    vec = pl.BlockSpec((1, d), lambda i: (0, 0))
    outs, moved = _call(
        body, name=name, grid=(t // tm,),
        in_specs=[pl.BlockSpec((tm, k), lambda i: (i, 0)), pl.BlockSpec((k, d), lambda i: (0, 0)), row, vec, vec] + [vec] * n_prev,
        out_specs=[row, row],
        out_shape=[jax.ShapeDtypeStruct((t, d), F32), jax.ShapeDtypeStruct((t, d), BF16)],
        sem=("parallel",), args=(a, b, x, g, beta, *(prev or ())), comm=comm)
    return outs if comm is None else (outs, moved)


def _matmul_ln_bwd(parts, b, z, g, dres, name, tm, comm=None):
    m = parts[0][0].shape[0]
    d = b.shape[1]
    tm = min(tm, m)
    n = len(parts)
    assert m % tm == 0 and all(row % a.shape[1] == 0 for a, row in parts)

    def body(*refs):
        z_ref, g_ref, dres_ref = refs[2 * n:2 * n + 3]
        dz_ref, dzb_ref, dg_ref, db_ref = refs[-4:]

        @pl.when(pl.program_id(0) == 0)
        def _():
            dg_ref[...] = jnp.zeros_like(dg_ref)
            db_ref[...] = jnp.zeros_like(db_ref)

        dy = ALPHA * dres_ref[...]
        for p in range(n):
            dy = dy + _dot(refs[p][...], refs[n + p][...], NN)
        nh, r = _ln_stats(z_ref[...])
        dg_ref[...] += _colsum(dy * nh)
        db_ref[...] += _colsum(dy)
        dz = _ln_bwd_rows(dy * g_ref[...], nh, r)
        dz_ref[...] = dz
        dzb_ref[...] = dz.astype(BF16)

    def b_spec(a, row):
        blk = row // a.shape[1]
        return pl.BlockSpec((a.shape[1], d), lambda i: (blk, 0))

    row = pl.BlockSpec((tm, d), lambda i: (i, 0))
    vec = pl.BlockSpec((1, d), lambda i: (0, 0))
    vshape = jax.ShapeDtypeStruct((1, d), F32)
    outs, moved = _call(
        body, name=name, grid=(m // tm,),
        in_specs=[pl.BlockSpec((tm, a.shape[1]), lambda i: (i, 0)) for a, _ in parts] + [b_spec(a, r_) for a, r_ in parts]
        + [row, vec, row],
        out_specs=[row, row, vec, vec],
        out_shape=[jax.ShapeDtypeStruct((m, d), F32), jax.ShapeDtypeStruct((m, d), BF16), vshape, vshape],
        sem=("arbitrary",), args=(*[a for a, _ in parts], *([b] * n), z, g, dres), comm=comm)
    return outs if comm is None else (outs, moved)


def _matmul_res_ln_loss(a, b, x, g, beta, target, name, tm, prev):
    t, k = a.shape
    d = b.shape[1]
    tm = min(tm, t)

    def body(a_ref, b_ref, x_ref, g_ref, beta_ref, t_ref, gp_ref, bp_ref, dz_ref, dzb_ref, dg_ref, db_ref, loss_ref):
        @pl.when(pl.program_id(0) == 0)
        def _():
            dg_ref[...] = jnp.zeros_like(dg_ref)
            db_ref[...] = jnp.zeros_like(db_ref)
            loss_ref[...] = jnp.zeros_like(loss_ref)

        nh, r = _ln_stats(ALPHA * _residual_input(x_ref, (gp_ref, bp_ref)) + _dot(a_ref[...], b_ref[...], NN))
        err = nh * g_ref[...] + beta_ref[...] - t_ref[...]
        loss_ref[...] += _colsum(err * err)
        dy = err * (1.0 / d)
        dg_ref[...] += _colsum(dy * nh)
        db_ref[...] += _colsum(dy)
        dz = _ln_bwd_rows(dy * g_ref[...], nh, r)
        dz_ref[...] = dz
        dzb_ref[...] = dz.astype(BF16)

    row = pl.BlockSpec((tm, d), lambda i: (i, 0))
    vec = pl.BlockSpec((1, d), lambda i: (0, 0))
    vshape = jax.ShapeDtypeStruct((1, d), F32)
    return pl.pallas_call(
        body, name=name, grid=(t // tm,),
        in_specs=[pl.BlockSpec((tm, k), lambda i: (i, 0)), pl.BlockSpec((k, d), lambda i: (0, 0)), row, vec, vec, row, vec, vec],
        out_specs=[row, row, vec, vec, vec],
        out_shape=[jax.ShapeDtypeStruct((t, d), F32), jax.ShapeDtypeStruct((t, d), BF16), vshape, vshape, vshape],
        compiler_params=_params("arbitrary"),
    )(a, b, x, g, beta, target, *prev)


FFN_HALO = 16
FFN_CHUNK = 256
LANES = 128
SUBLANES = 8


def _rows_up(e, start, rows):
    if start % SUBLANES == 0:
        return e[start:start + rows]
    return pltpu.roll(e, e.shape[0] - start, 0)[0:rows]


def _fold(x):
    return jnp.sum(x.reshape(x.shape[0] // SUBLANES, SUBLANES, x.shape[1]), axis=0)


def _ffn_mid_fwd(h, cw, cb, name, tm=1024, tc=1408, comm=None):
    t, f2 = h.shape
    tm = min(tm, t)
    f = f2 // 2
    nj, nt, hb = f // tc, t // tm, tm // FFN_HALO

    ch = min(FFN_CHUNK, tm)

    def body(hg, hgp, hv, hvp, cwg, cwv, cbg, cbv, u_ref, cg_ref, cv_ref):
        i = pl.program_id(1)
        o = FFN_HALO - FFN_KERNEL + 1
        for lg in range(tc // LANES):
            cols = slice(lg * LANES, (lg + 1) * LANES)
            wg, wv = [cwg[k:k + 1, cols] for k in range(FFN_KERNEL)], [cwv[k:k + 1, cols] for k in range(FFN_KERNEL)]
            bg, bv = cbg[:, cols], cbv[:, cols]

            def emit(base, eg, ev):
                cg = wg[0] * _rows_up(eg, o, ch) + wg[1] * _rows_up(eg, o + 1, ch) + wg[2] * _rows_up(eg, o + 2, ch) + bg
                cv = wv[0] * _rows_up(ev, o, ch) + wv[1] * _rows_up(ev, o + 1, ch) + wv[2] * _rows_up(ev, o + 2, ch) + bv
                u_ref[pl.ds(base, ch), cols] = (_gelu(cg) * cv).astype(BF16)
                cg_ref[pl.ds(base, ch), cols] = cg.astype(BF16)
                cv_ref[pl.ds(base, ch), cols] = cv.astype(BF16)

            def first(main, prev):
                return jnp.concatenate([jnp.where(i > 0, prev[:, cols].astype(F32), 0.0), main[0:ch, cols].astype(F32)], axis=0)

            def inner(c, carry):
                base = pl.multiple_of(c * ch, ch)
                emit(base, hg[pl.ds(base - FFN_HALO, ch + FFN_HALO), cols].astype(F32),
                     hv[pl.ds(base - FFN_HALO, ch + FFN_HALO), cols].astype(F32))
                return carry

            emit(0, first(hg, hgp), first(hv, hvp))
            if tm > ch:
                lax.fori_loop(1, tm // ch, inner, 0)

    def main_spec(off):
        return pl.BlockSpec((tm, tc), lambda j, i: (i, j + off))

    def prev_spec(off):
        return pl.BlockSpec((FFN_HALO, tc), lambda j, i: (jnp.maximum(i * hb - 1, 0), j + off))

    def par_spec(rows, off):
        return pl.BlockSpec((rows, tc), lambda j, i: (0, j + off))

    outs, moved = _call(
        body, name=name, grid=(nj, nt),
        in_specs=[main_spec(0), prev_spec(0), main_spec(nj), prev_spec(nj),
                  par_spec(FFN_KERNEL, 0), par_spec(FFN_KERNEL, nj), par_spec(1, 0), par_spec(1, nj)],
        out_specs=[pl.BlockSpec((tm, tc), lambda j, i: (i, j))] * 3,
        out_shape=[jax.ShapeDtypeStruct((t, f), BF16)] * 3,
        sem=("parallel", "arbitrary"), args=(h, h, h, h, cw, cw, cb, cb), comm=comm)
    return outs if comm is None else (outs, moved)


def _ffn_mid_bwd(h, cg, cv, du, cw, name, tm=1024, tc=1408, comm=None):
    t, f2 = h.shape
    tm = min(tm, t)
    f = f2 // 2
    nj, nt, hb = f // tc, t // tm, tm // FFN_HALO

    ch = min(FFN_CHUNK, tm)
    ahead = ch + SUBLANES
    n_ch = tm // ch

    def body(hg, hv, cg_ref, cgn_ref, cv_ref, cvn_ref, du_ref, dun_ref, cwg, cwv,
             dhg_ref, dhv_ref, dcwg_ref, dcwv_ref, dcbg_ref, dcbv_ref):
        i = pl.program_id(1)

        @pl.when(i == 0)
        def _():
            for ref in (dcwg_ref, dcwv_ref, dcbg_ref, dcbv_ref):
                ref[...] = jnp.zeros_like(ref)

        for lg in range(tc // LANES):
            cols = slice(lg * LANES, (lg + 1) * LANES)
            wg, wv = [cwg[k:k + 1, cols] for k in range(FFN_KERNEL)], [cwv[k:k + 1, cols] for k in range(FFN_KERNEL)]

            def emit(base, cg_e, cv_e, du_e, acc):
                cg_a, cv_a, du_a = cg_e[0:ahead], cv_e[0:ahead], du_e[0:ahead]
                gl, dgl = _gelu_and_grad(cg_a)

                def back(d, h_ref, w, dh_ref):
                    later = [d[0:ch], _rows_up(d, 1, ch), _rows_up(d, 2, ch)]
                    dh_ref[pl.ds(base, ch), cols] = (w[2] * later[0] + w[1] * later[1] + w[0] * later[2]).astype(BF16)
                    h_own = h_ref[pl.ds(base, ch), cols].astype(F32)
                    return [_fold(later[0])] + [_fold(later[FFN_KERNEL - 1 - k] * h_own) for k in range(FFN_KERNEL)]

                sums = back(du_a * cv_a * dgl, hg, wg, dhg_ref) + back(du_a * gl, hv, wv, dhv_ref)
                return tuple(a + s_ for a, s_ in zip(acc, sums))

            def inner(c, acc):
                base = pl.multiple_of(c * ch, ch)
                rows = pl.ds(base, ch + FFN_HALO)
                return emit(base, cg_ref[rows, cols].astype(F32), cv_ref[rows, cols].astype(F32), du_ref[rows, cols].astype(F32), acc)

            def last(acc):
                def rows(main, after):
                    return jnp.concatenate([main[tm - ch:tm, cols].astype(F32), after], axis=0)

                du_next = jnp.where(i < nt - 1, dun_ref[:, cols].astype(F32), 0.0)
                return emit(tm - ch, rows(cg_ref, cgn_ref[:, cols].astype(F32)), rows(cv_ref, cvn_ref[:, cols].astype(F32)),
                            rows(du_ref, du_next), acc)

            acc = (jnp.zeros((SUBLANES, LANES), F32),) * (2 * (1 + FFN_KERNEL))
            if n_ch > 1:
                acc = lax.fori_loop(0, n_ch - 1, inner, acc)
            acc = last(acc)
            dcbg_ref[:, cols] += _colsum(acc[0])
            dcbv_ref[:, cols] += _colsum(acc[1 + FFN_KERNEL])
            for k in range(FFN_KERNEL):
                dcwg_ref[k:k + 1, cols] += _colsum(acc[1 + k])
                dcwv_ref[k:k + 1, cols] += _colsum(acc[2 + FFN_KERNEL + k])

    last_blk = t // FFN_HALO - 1

    def main_spec(off):
        return pl.BlockSpec((tm, tc), lambda j, i: (i, j + off))

    def next_spec(off):
        return pl.BlockSpec((FFN_HALO, tc), lambda j, i: (jnp.minimum((i + 1) * hb, last_blk), j + off))

    def par_spec(rows, off):
        return pl.BlockSpec((rows, tc), lambda j, i: (0, j + off))

    out_tile = pl.BlockSpec((tm, tc), lambda j, i: (i, j))
    outs, moved = _call(
        body, name=name, grid=(nj, nt),
        in_specs=[main_spec(0), main_spec(nj), main_spec(0), next_spec(0), main_spec(0), next_spec(0), main_spec(0), next_spec(0),
                  par_spec(FFN_KERNEL, 0), par_spec(FFN_KERNEL, nj)],
        out_specs=[out_tile, out_tile, par_spec(FFN_KERNEL, 0), par_spec(FFN_KERNEL, 0), par_spec(1, 0), par_spec(1, 0)],
        out_shape=[jax.ShapeDtypeStruct((t, f), BF16), jax.ShapeDtypeStruct((t, f), BF16),
                   jax.ShapeDtypeStruct((FFN_KERNEL, f), F32), jax.ShapeDtypeStruct((FFN_KERNEL, f), F32),
                   jax.ShapeDtypeStruct((1, f), F32), jax.ShapeDtypeStruct((1, f), F32)],
        sem=("parallel", "arbitrary"), args=(h, h, cg, cg, cv, cv, du, du, cw, cw), comm=comm)
    return outs if comm is None else (outs, moved)


MIX_HALO = 32


def _glu(hh):
    return hh[:, 0:A_WIDTH] * _sigmoid(hh[:, A_WIDTH:2 * A_WIDTH])


def _fill_row_shifts(s):
    rows = s.shape[1] - SUBLANES
    for j in range(1, SUBLANES):
        s[j, 0:rows, :] = s[0, pl.ds(j, rows), :]


def _rows_from(s, start, rows):
    j = start % SUBLANES
    return s[j, start - j:start - j + rows, :]


def _tril_mask():
    return lax.broadcasted_iota(jnp.int32, (B_CHUNK, B_CHUNK), 0) >= lax.broadcasted_iota(jnp.int32, (B_CHUNK, B_CHUNK), 1)


def _spatial_mix(q, ms_ref, sbt_ref, tm):
    mask = _tril_mask()
    ws = [jnp.where(mask, ms_ref[g], 0.0).astype(BF16) for g in range(B_GROUPS)]
    qb = q.astype(BF16)
    rows = []
    for c in range(tm // B_CHUNK):
        cols = [_dot(ws[g], qb[c * B_CHUNK:(c + 1) * B_CHUNK, g * 128:(g + 1) * 128], NN) + sbt_ref[:, g:g + 1]
                for g in range(B_GROUPS)]
        rows.append(jnp.concatenate(cols, axis=1))
    return jnp.concatenate(rows, axis=0)


def _mixer_mid_fwd(h, cw, cb, ag, ab, bg, bb, ms, sbt, name, tm=256, comm=None):
    t = h.shape[0]
    nt, hb = t // tm, tm // MIX_HALO
    o = MIX_HALO - A_KERNEL + 1

    def body(h_ref, hp_ref, cw_ref, cb_ref, ag_ref, ab_ref, bg_ref, bb_ref, ms_ref, sbt_ref, cat_ref, y_ref, sp):
        i = pl.program_id(0)
        sp[0, 0:MIX_HALO, :] = jnp.where(i > 0, _glu(hp_ref[:, 0:2 * A_WIDTH].astype(F32)), 0.0)
        sp[0, MIX_HALO:, :] = _glu(h_ref[:, 0:2 * A_WIDTH].astype(F32))
        _fill_row_shifts(sp)
        y = jnp.zeros((tm, A_WIDTH), F32) + cb_ref[...]
        for k in range(A_KERNEL):
            y = y + cw_ref[k:k + 1, :] * _rows_from(sp, o + k, tm)
        y_ref[...] = y.astype(BF16)
        nh, _ = _ln_stats(y)
        ln = nh * ag_ref[...] + ab_ref[...]
        cat_ref[:, 0:A_WIDTH] = (ln * _sigmoid(ln)).astype(BF16)
        u = _gelu(h_ref[:, 1024:1536].astype(F32))
        nb, _ = _ln_stats(_gelu(h_ref[:, 1536:2048].astype(F32)))
        mixed = _spatial_mix(nb * bg_ref[...] + bb_ref[...], ms_ref, sbt_ref, tm)
        cat_ref[:, A_WIDTH:] = (u * mixed).astype(BF16)

    vec = pl.BlockSpec((1, A_WIDTH), lambda i: (0, 0))
    outs, moved = _call(
        body, name=name, grid=(nt,),
        in_specs=[pl.BlockSpec((tm, 2048), lambda i: (i, 0)),
                  pl.BlockSpec((MIX_HALO, 2048), lambda i: (jnp.maximum(i * hb - 1, 0), 0)),
                  pl.BlockSpec((A_KERNEL, A_WIDTH), lambda i: (0, 0)), vec, vec, vec, vec, vec,
                  pl.BlockSpec((B_GROUPS, B_CHUNK, B_CHUNK), lambda i: (0, 0, 0)),
                  pl.BlockSpec((B_CHUNK, B_GROUPS), lambda i: (0, 0))],
        out_specs=[pl.BlockSpec((tm, D_MODEL), lambda i: (i, 0)), pl.BlockSpec((tm, A_WIDTH), lambda i: (i, 0))],
        out_shape=[jax.ShapeDtypeStruct((t, D_MODEL), BF16), jax.ShapeDtypeStruct((t, A_WIDTH), BF16)],
        scratch_shapes=[pltpu.VMEM((SUBLANES, tm + MIX_HALO, A_WIDTH), F32)],
        sem=("parallel",), args=(h, h, cw, cb, ag, ab, bg, bb, ms, sbt), comm=comm)
    return outs if comm is None else (outs, moved)


def _mixer_mid_bwd(h, y, dcat, cw, ag, ab, bg, bb, ms, mst, sbt, name, tm=256, comm=None):
    t = h.shape[0]
    nt, hb = t // tm, tm // MIX_HALO
    r = tm + MIX_HALO
    nchunk = tm // B_CHUNK

    def body(h_ref, y_ref, yn_ref, dc_ref, dcn_ref, cw_ref, ag_ref, ab_ref, bg_ref, bb_ref, ms_ref, mst_ref, sbt_ref,
             dh_ref, dcw_ref, dcb_ref, dag_ref, dab_ref, dbg_ref, dbb_ref, dms_ref, dsb_ref, sdy, sbacc):
        i = pl.program_id(0)

        @pl.when(i == 0)
        def _():
            for ref in (dcw_ref, dcb_ref, dag_ref, dab_ref, dbg_ref, dbb_ref, dms_ref, dsb_ref, sbacc):
                ref[...] = jnp.zeros_like(ref)

        nh, rs = _ln_stats(jnp.concatenate([y_ref[...].astype(F32), yn_ref[...].astype(F32)], axis=0))
        ln = nh * ag_ref[...] + ab_ref[...]
        sg = _sigmoid(ln)
        dao = jnp.concatenate([dc_ref[:, 0:A_WIDTH].astype(F32),
                               jnp.where(i < nt - 1, dcn_ref[:, 0:A_WIDTH].astype(F32), 0.0)], axis=0)
        dln = dao * (sg * (1.0 + ln * (1.0 - sg)))
        dag_ref[...] += _colsum(dln[0:tm] * nh[0:tm])
        dab_ref[...] += _colsum(dln[0:tm])
        sdy[0] = _ln_bwd_rows(dln * ag_ref[...], nh, rs)
        _fill_row_shifts(sdy)
        dcb_ref[...] += _colsum(sdy[0, 0:tm, :])
        av = h_ref[:, 0:A_WIDTH].astype(F32)
        s = _sigmoid(h_ref[:, A_WIDTH:2 * A_WIDTH].astype(F32))
        p_own = av * s
        dp = jnp.zeros((tm, A_WIDTH), F32)
        for k in range(A_KERNEL):
            later = _rows_from(sdy, A_KERNEL - 1 - k, tm)
            dcw_ref[k:k + 1, :] += _colsum(later * p_own)
            dp = dp + cw_ref[k:k + 1, :] * later
        dh_ref[:, 0:A_WIDTH] = (dp * s).astype(BF16)
        dh_ref[:, A_WIDTH:2 * A_WIDTH] = (dp * av * s * (1.0 - s)).astype(BF16)

        u, dgu = _gelu_and_grad(h_ref[:, 1024:1536].astype(F32))
        w, dgw = _gelu_and_grad(h_ref[:, 1536:2048].astype(F32))
        nb, rb = _ln_stats(w)
        q = nb * bg_ref[...] + bb_ref[...]
        mixed = _spatial_mix(q, ms_ref, sbt_ref, tm)
        dbo = dc_ref[:, A_WIDTH:].astype(F32)
        dh_ref[:, 1024:1536] = (dbo * mixed * dgu).astype(BF16)
        dmx = dbo * u
        mask = _tril_mask()
        wst = [jnp.where(mask.T, mst_ref[g], 0.0).astype(BF16) for g in range(B_GROUPS)]
        qb = q.astype(BF16)
        dmb = dmx.astype(BF16)
        rows = []
        for c in range(nchunk):
            cols = []
            for g in range(B_GROUPS):
                rs_, cs_ = slice(c * B_CHUNK, (c + 1) * B_CHUNK), slice(g * 128, (g + 1) * 128)
                sbacc[g] += dmx[rs_, cs_]
                dms_ref[g] += _dot(dmb[rs_, cs_], qb[rs_, cs_], NT)
                cols.append(_dot(wst[g], dmb[rs_, cs_], NN))
            rows.append(jnp.concatenate(cols, axis=1))
        dq = jnp.concatenate(rows, axis=0)
        dbg_ref[...] += _colsum(dq * nb)
        dbb_ref[...] += _colsum(dq)
        dh_ref[:, 1536:2048] = (_ln_bwd_rows(dq * bg_ref[...], nb, rb) * dgw).astype(BF16)

        @pl.when(i == nt - 1)
        def _():
            for g in range(B_GROUPS):
                dms_ref[g] = jnp.where(mask, dms_ref[g], 0.0)
                dsb_ref[g] = jnp.sum(sbacc[g], axis=1, keepdims=True)

    last_blk = t // MIX_HALO - 1
    vec = pl.BlockSpec((1, A_WIDTH), lambda i: (0, 0))
    mat = pl.BlockSpec((B_GROUPS, B_CHUNK, B_CHUNK), lambda i: (0, 0, 0))
    taps = pl.BlockSpec((A_KERNEL, A_WIDTH), lambda i: (0, 0))

    def halo(width):
        return pl.BlockSpec((MIX_HALO, width), lambda i: (jnp.minimum((i + 1) * hb, last_blk), 0))

    vshape = jax.ShapeDtypeStruct((1, A_WIDTH), F32)
    outs, moved = _call(
        body, name=name, grid=(nt,),
        in_specs=[pl.BlockSpec((tm, 2048), lambda i: (i, 0)), pl.BlockSpec((tm, A_WIDTH), lambda i: (i, 0)), halo(A_WIDTH),
                  pl.BlockSpec((tm, D_MODEL), lambda i: (i, 0)), halo(D_MODEL),
                  taps, vec, vec, vec, vec, mat, mat, pl.BlockSpec((B_CHUNK, B_GROUPS), lambda i: (0, 0))],
        out_specs=[pl.BlockSpec((tm, 2048), lambda i: (i, 0)), taps, vec, vec, vec, vec, vec, mat,
                   pl.BlockSpec((B_GROUPS, B_CHUNK, 1), lambda i: (0, 0, 0))],
        out_shape=[jax.ShapeDtypeStruct((t, 2048), BF16), jax.ShapeDtypeStruct((A_KERNEL, A_WIDTH), F32),
                   vshape, vshape, vshape, vshape, vshape,
                   jax.ShapeDtypeStruct((B_GROUPS, B_CHUNK, B_CHUNK), F32), jax.ShapeDtypeStruct((B_GROUPS, B_CHUNK, 1), F32)],
        scratch_shapes=[pltpu.VMEM((SUBLANES, r, A_WIDTH), F32), pltpu.VMEM((B_GROUPS, B_CHUNK, B_CHUNK), F32)],
        sem=("arbitrary",), args=(h, y, y, dcat, dcat, cw, ag, ab, bg, bb, ms, mst, sbt), comm=comm)
    return outs if comm is None else (outs, moved)


Q_WIDTH = N_Q_HEADS * HEAD_DIM
KV_WIDTH = 2 * N_KV_HEADS * HEAD_DIM
PAIRS_PER_KV = N_Q_HEADS // N_KV_HEADS // 2
ATT_SCALE = 1.0 / math.sqrt(HEAD_DIM)


def _dup_heads(pair_cols, kv_head):
    lane = lax.broadcasted_iota(jnp.int32, pair_cols.shape, 1)
    rolled = pltpu.roll(pair_cols, HEAD_DIM, 1)
    first = lane < HEAD_DIM
    return jnp.where(first, pair_cols, rolled) if kv_head == 0 else jnp.where(first, rolled, pair_cols)


HEADS_PER_KV = N_Q_HEADS // N_KV_HEADS


def _stack_heads(ref, kh):
    lane = lax.broadcasted_iota(jnp.int32, (ATT_BLOCK, 128), 1)
    rows = []
    for pr in range(PAIRS_PER_KV):
        c0 = (kh * PAIRS_PER_KV + pr) * 128
        pair = ref[:, c0:c0 + 128]
        rows += [jnp.where(lane < HEAD_DIM, pair, jnp.zeros_like(pair)), jnp.where(lane < HEAD_DIM, jnp.zeros_like(pair), pair)]
    return jnp.concatenate(rows, axis=0)


def _unstack_heads(stacked, kh, write):
    lane = lax.broadcasted_iota(jnp.int32, (ATT_BLOCK, 128), 1)
    for pr in range(PAIRS_PER_KV):
        first = stacked[(2 * pr) * ATT_BLOCK:(2 * pr + 1) * ATT_BLOCK]
        second = stacked[(2 * pr + 1) * ATT_BLOCK:(2 * pr + 2) * ATT_BLOCK]
        write((kh * PAIRS_PER_KV + pr) * 128, jnp.where(lane < HEAD_DIM, first, second))


def _sink_row(sink_ref, kh):
    return jnp.concatenate([jnp.full((1, ATT_BLOCK), sink_ref[0, kh * HEADS_PER_KV + h], F32) for h in range(HEADS_PER_KV)], axis=1)


def _att_window_bias():
    sj = lax.broadcasted_iota(jnp.int32, (2 * ATT_BLOCK, HEADS_PER_KV * ATT_BLOCK), 0)
    qi = lax.broadcasted_iota(jnp.int32, (2 * ATT_BLOCK, HEADS_PER_KV * ATT_BLOCK), 1) & (ATT_BLOCK - 1)
    diff = qi + ATT_BLOCK - sj
    return jnp.where((diff >= 0) & (diff < ATT_BLOCK), 0.0, -jnp.inf)


def _att_probs_t(q_all, k2, bias_ref, n, sink):
    st = _dot(k2, q_all, NT) * ATT_SCALE + bias_ref[...]
    st = jnp.concatenate([jnp.where(n > 0, st[0:ATT_BLOCK], -jnp.inf), st[ATT_BLOCK:]], axis=0)
    m = jnp.maximum(jnp.max(st, axis=0, keepdims=True), sink)
    e = jnp.exp(st - m)
    es = jnp.exp(sink - m)
    inv = 1.0 / (jnp.sum(e, axis=0, keepdims=True) + es)
    return e * inv, es * inv


def _attn_fwd(qkv, sinks, name, comm=None):
    t = qkv.shape[0]
    nb = t // ATT_BLOCK
    kvb = Q_WIDTH // KV_WIDTH

    def body(sink_ref, q_ref, kv_ref, kvp_ref, o_ref, bias):
        n = pl.program_id(0)

        @pl.when(n == 0)
        def _():
            bias[...] = _att_window_bias()

        kv = jnp.concatenate([kvp_ref[...], kv_ref[...]], axis=0).astype(F32)

        def write(c0, pair):
            o_ref[:, c0:c0 + 128] = pair.astype(BF16)

        for kh in range(N_KV_HEADS):
            k2 = _dup_heads(kv[:, 0:128], kh).astype(BF16)
            v2 = _dup_heads(kv[:, 128:256], kh).astype(BF16)
            pt, _ = _att_probs_t(_stack_heads(q_ref, kh), k2, bias, n, _sink_row(sink_ref, kh))
            _unstack_heads(_dot(v2, pt, TN).T, kh, write)

    (out,), moved = _call(
        body, name=name, grid=(nb,),
        in_specs=[pl.BlockSpec(memory_space=pltpu.SMEM),
                  pl.BlockSpec((ATT_BLOCK, Q_WIDTH), lambda n: (n, 0)),
                  pl.BlockSpec((ATT_BLOCK, KV_WIDTH), lambda n: (n, kvb)),
                  pl.BlockSpec((ATT_BLOCK, KV_WIDTH), lambda n: (jnp.maximum(n - 1, 0), kvb))],
        out_specs=[pl.BlockSpec((ATT_BLOCK, Q_WIDTH), lambda n: (n, 0))],
        out_shape=[jax.ShapeDtypeStruct((t, Q_WIDTH), BF16)],
        scratch_shapes=[pltpu.VMEM((2 * ATT_BLOCK, HEADS_PER_KV * ATT_BLOCK), F32)],
        sem=("arbitrary",), args=(sinks, qkv, qkv, qkv), comm=comm)
    return out if comm is None else (out, moved)


def _attn_bwd(qkv, d_o, sinks, name, comm=None):
    t = qkv.shape[0]
    nb = t // ATT_BLOCK
    kvb = Q_WIDTH // KV_WIDTH

    def body(sink_ref, q_ref, kv_ref, kvp_ref, do_ref, dq_ref, dkv_ref, dbq_ref, dbkv_ref, dsink_ref, carry, bias):
        n = pl.program_id(0)

        @pl.when(n == 0)
        def _():
            for ref in (dbq_ref, dbkv_ref, dsink_ref, carry):
                ref[...] = jnp.zeros_like(ref)
            dkv_ref[...] = jnp.zeros_like(dkv_ref)
            bias[...] = _att_window_bias()

        @pl.when(n < nb)
        def _():
            kv = jnp.concatenate([kvp_ref[...], kv_ref[...]], axis=0).astype(F32)
            lane2 = lax.broadcasted_iota(jnp.int32, (2 * ATT_BLOCK, 128), 1)
            sink_lane = lax.broadcasted_iota(jnp.int32, (1, 128), 1)
            dsink = jnp.zeros((1, 128), F32)
            dk_parts, dv_parts = [], []

            def write(c0, pair):
                dbq_ref[:, c0:c0 + 128] += _colsum(pair)
                dq_ref[:, c0:c0 + 128] = pair.astype(BF16)

            for kh in range(N_KV_HEADS):
                k2 = _dup_heads(kv[:, 0:128], kh).astype(BF16)
                v2 = _dup_heads(kv[:, 128:256], kh).astype(BF16)
                q_all = _stack_heads(q_ref, kh)
                do_all = _stack_heads(do_ref, kh)
                pt, ps = _att_probs_t(q_all, k2, bias, n, _sink_row(sink_ref, kh))
                dpt = _dot(v2, do_all, NT)
                delta = jnp.sum(pt * dpt, axis=0, keepdims=True)
                dst = pt * (dpt - delta) * ATT_SCALE
                psd = ps * delta
                for h in range(HEADS_PER_KV):
                    dsink = dsink + jnp.where(sink_lane == kh * HEADS_PER_KV + h,
                                              -jnp.sum(psd[:, h * ATT_BLOCK:(h + 1) * ATT_BLOCK]), 0.0)
                _unstack_heads(_dot(k2, dst, TN).T, kh, write)
                dk_acc = _dot(dst, q_all, NN)
                dv_acc = _dot(pt, do_all, NN)
                dk_parts.append(dk_acc + pltpu.roll(dk_acc, HEAD_DIM, 1))
                dv_parts.append(dv_acc + pltpu.roll(dv_acc, HEAD_DIM, 1))
            dk = jnp.where(lane2 < HEAD_DIM, dk_parts[0], dk_parts[1])
            dv = jnp.where(lane2 < HEAD_DIM, dv_parts[0], dv_parts[1])
            dkv_new = jnp.concatenate([dk, dv], axis=1)
            done = carry[...] + dkv_new[0:ATT_BLOCK]

            @pl.when(n > 0)
            def _():
                dkv_ref[...] = done.astype(BF16)
                dbkv_ref[...] += _colsum(done)

            carry[...] = dkv_new[ATT_BLOCK:]
            dsink_ref[...] += dsink

        @pl.when(n == nb)
        def _():
            dkv_ref[...] = carry[...].astype(BF16)
            dbkv_ref[...] += _colsum(carry[...])

    def clamp(n):
        return jnp.minimum(n, nb - 1)

    outs, moved = _call(
        body, name=name, grid=(nb + 1,),
        in_specs=[pl.BlockSpec(memory_space=pltpu.SMEM),
                  pl.BlockSpec((ATT_BLOCK, Q_WIDTH), lambda n: (clamp(n), 0)),
                  pl.BlockSpec((ATT_BLOCK, KV_WIDTH), lambda n: (clamp(n), kvb)),
                  pl.BlockSpec((ATT_BLOCK, KV_WIDTH), lambda n: (jnp.maximum(clamp(n) - 1, 0), kvb)),
                  pl.BlockSpec((ATT_BLOCK, Q_WIDTH), lambda n: (clamp(n), 0))],
        out_specs=[pl.BlockSpec((ATT_BLOCK, Q_WIDTH), lambda n: (clamp(n), 0)),
                   pl.BlockSpec((ATT_BLOCK, KV_WIDTH), lambda n: (jnp.maximum(n - 1, 0), 0)),
                   pl.BlockSpec((1, Q_WIDTH), lambda n: (0, 0)),
                   pl.BlockSpec((1, KV_WIDTH), lambda n: (0, 0)),
                   pl.BlockSpec((1, 128), lambda n: (0, 0))],
        out_shape=[jax.ShapeDtypeStruct((t, Q_WIDTH), BF16), jax.ShapeDtypeStruct((t, KV_WIDTH), BF16),
                   jax.ShapeDtypeStruct((1, Q_WIDTH), F32), jax.ShapeDtypeStruct((1, KV_WIDTH), F32),
                   jax.ShapeDtypeStruct((1, 128), F32)],
        scratch_shapes=[pltpu.VMEM((ATT_BLOCK, KV_WIDTH), F32), pltpu.VMEM((2 * ATT_BLOCK, HEADS_PER_KV * ATT_BLOCK), F32)],
        sem=("arbitrary",), args=(sinks, qkv, qkv, qkv, d_o), comm=comm)
    return outs if comm is None else (outs, moved)


def _adamw_math(g, w, m, v):
    m = ADAM_B1 * m + (1.0 - ADAM_B1) * g
    v = ADAM_B2 * v + (1.0 - ADAM_B2) * (g * g)
    m_hat = m / (1.0 - ADAM_B1 ** ADAM_STEP)
    v_hat = v / (1.0 - ADAM_B2 ** ADAM_STEP)
    delta = -ADAM_LR * (m_hat / (jnp.sqrt(v_hat) + ADAM_EPS) + ADAM_WD * w)
    return delta, m, v


def _sum_partials(p_ref):
    g = p_ref[0].astype(F32)
    for s in range(1, N_DEV):
        g = g + p_ref[s].astype(F32)
    return g


def _adamw_big(parts, w, m, v, name, tr, comm=None):
    r, c = w.shape
    parts = [p if isinstance(p, tuple) else (p, 0, p.shape[1]) for p in parts]
    tiles = [rows // tr for _, _, rows in parts]
    starts = [sum(tiles[:l]) for l in range(len(parts))]
    assert all(lo % tr == 0 and rows % tr == 0 for _, lo, rows in parts) and sum(tiles) * tr == r

    def body(*refs):
        p_refs, (w_ref, m_ref, v_ref, g_out, d_out, m_out, v_out) = refs[:len(parts)], refs[len(parts):]
        i = pl.program_id(0)
        for l, p_ref in enumerate(p_refs):
            @pl.when((i >= starts[l]) & (i < starts[l] + tiles[l]))
            def _():
                g = _sum_partials(p_ref)
                g_out[...] = g
                d_out[...], m_out[...], v_out[...] = _adamw_math(g, w_ref[...], m_ref[...], v_ref[...])

    def part_spec(l):
        return pl.BlockSpec((N_DEV, tr, c), lambda i: (0, jnp.clip(i - starts[l], 0, tiles[l] - 1) + parts[l][1] // tr, 0))

    tile = pl.BlockSpec((tr, c), lambda i: (i, 0))
    shape = jax.ShapeDtypeStruct((r, c), F32)
    return _call(body, name=name, grid=(r // tr,),
                 in_specs=[part_spec(l) for l in range(len(parts))] + [tile, tile, tile],
                 out_specs=[tile] * 4, out_shape=[shape] * 4, args=[*[p[0] for p in parts], w, m, v],
                 sem=("parallel",), comm=comm)


def _adamw_small(parts, ws, ms, vs, name):
    n = len(ws)

    def body(*refs):
        ins, outs = refs[:4 * n], refs[4 * n:]
        for a in range(n):
            g = _sum_partials(ins[a])
            outs[4 * a][...] = g
            outs[4 * a + 1][...], outs[4 * a + 2][...], outs[4 * a + 3][...] = _adamw_math(
                g, ins[n + a][...], ins[2 * n + a][...], ins[3 * n + a][...])

    out_shape = []
    for w in ws:
        out_shape += [jax.ShapeDtypeStruct(w.shape, F32)] * 4
    return pl.pallas_call(body, name=name, out_shape=out_shape, compiler_params=_params())(*parts, *ws, *ms, *vs)


PACK_LANES = 128
PACK_ROWS = 8


def _pack(arrs):
    flat = jnp.concatenate([a.reshape(-1).astype(F32) for a in arrs])
    unit = PACK_LANES * PACK_ROWS
    total = -(-flat.shape[0] // unit) * unit
    return jnp.pad(flat, (0, total - flat.shape[0])).reshape(-1, PACK_LANES)


def _unpack(buf, shapes):
    flat = buf.reshape(N_DEV, -1)
    out, pos = [], 0
    for s in shapes:
        size = math.prod(s)
        out.append(flat[:, pos:pos + size].reshape((N_DEV,) + tuple(s)))
        pos += size
    return out


def _interleave(g):
    return jnp.transpose(g, (1, 0, 2)).reshape(g.shape[1], -1)


def _ffn_backward(dz, dzb, x_in, z_in, g_in, h, cg, cv, u, w_up_t, cw, w_down, tag, exchange=(), exchange_late=(), own_rows=0):
    du = _matmul(dzb, w_down, "nt", BF16, f"ffn{tag}_du", 2048, 1408, 1024)
    d_w_down = _matmul(u, dzb, "tn", BF16, f"ffn{tag}_dwdown", 1408, 1024, 2048)
    (dhg, dhv, dcwg, dcwv, dcbg, dcbv), moved = _ffn_mid_bwd(
        h, cg, cv, du, cw, f"ffn{tag}_mid_bwd", comm=_Comm(exchange=[d_w_down.reshape(N_DEV, -1, D_MODEL), *exchange]))
    d_w_up_t = _matmul_tn_pair(dhg, dhv, x_in, BF16, f"ffn{tag}_dwup", 1408, 1024, 2048,
                               comm=_Comm(exchange=exchange_late) if exchange_late else None)
    if exchange_late:
        d_w_up_t, late = d_w_up_t
        moved = moved + late
    d_up_blocks = d_w_up_t.reshape(N_DEV, -1, D_MODEL)
    outs = _matmul_ln_bwd([(dhg, 0), (dhv, D_FF)], w_up_t, z_in, g_in, dz, f"ffn{tag}_dx_ln_bwd", 256,
                          comm=_Comm(exchange=[(d_up_blocks, 0, own_rows)]) if own_rows else None)
    (dz_in, dzb_in, dg_in, db_in), own = outs if own_rows else (outs, [])
    moved = moved + own
    return (dz_in, dzb_in, dg_in, db_in, d_up_blocks,
            jnp.concatenate([dcwg, dcwv], axis=1), jnp.concatenate([dcbg, dcbv], axis=1), moved)


def kernel(x, ab_w_in, a_conv_w, a_conv_b, a_norm_g, a_norm_b, b_norm_g, b_norm_b, b_spatial_w, b_spatial_b, ab_w_out, c_w_qkv, c_b_qkv, c_sinks, c_w_o, ffn_w_up, ffn_conv_w, ffn_conv_b, ffn_w_down, ln_g, ln_b, loss_target, m_ab_w_in, m_a_conv_w, m_a_conv_b, m_a_norm_g, m_a_norm_b, m_b_norm_g, m_b_norm_b, m_b_spatial_w, m_b_spatial_b, m_ab_w_out, m_c_w_qkv, m_c_b_qkv, m_c_sinks, m_c_w_o, m_ffn_w_up, m_ffn_conv_w, m_ffn_conv_b, m_ffn_w_down, m_ln_g, m_ln_b, v_ab_w_in, v_a_conv_w, v_a_conv_b, v_a_norm_g, v_a_norm_b, v_b_norm_g, v_b_norm_b, v_b_spatial_w, v_b_spatial_b, v_ab_w_out, v_c_w_qkv, v_c_b_qkv, v_c_sinks, v_c_w_o, v_ffn_w_up, v_ffn_conv_w, v_ffn_conv_b, v_ffn_w_down, v_ln_g, v_ln_b):
    me = 4 * lax.axis_index("x") + 2 * lax.axis_index("y") + lax.axis_index("c")
    xt = x[0]
    t = xt.shape[0]

    small_shard_shapes = [a_conv_w.shape, c_b_qkv.shape, ffn_conv_w.shape, ln_g.shape, ln_b.shape]
    up_shard = [jnp.swapaxes(ffn_w_up[l], 0, 1).astype(BF16) for l in range(2)]
    qkv_shard = jnp.swapaxes(c_w_qkv[0], 0, 1).astype(BF16)
    down_shard = [ffn_w_down[l].astype(BF16) for l in range(2)]
    g_win, g_small = _comm_only(
        _Comm(gather=[jnp.swapaxes(ab_w_in[0], 0, 1).astype(BF16), _pack([a_conv_w, c_b_qkv, ffn_conv_w, ln_g, ln_b])]),
        "gather_first")
    w_in = g_win.reshape(-1, D_MODEL)
    g_acw, g_bqkv, g_fcw, g_lng, g_lnb = _unpack(g_small, small_shard_shapes)
    acw = _interleave(g_acw[:, 0])
    bqkv = g_bqkv[:, 0].reshape(1, -1)
    fcw = [_interleave(g_fcw[:, l]) for l in range(2)]
    lng = jnp.transpose(g_lng, (1, 2, 0, 3)).reshape(2, 2, 1, D_MODEL)
    lnb = jnp.transpose(g_lnb, (1, 2, 0, 3)).reshape(2, 2, 1, D_MODEL)
    fcb = [ffn_conv_b[l:l + 1] for l in range(2)]
    ms = b_spatial_w[0]
    mst = jnp.swapaxes(ms, 1, 2)
    sbt = b_spatial_b[0].T

    q_up = up_shard[0].shape[0] // 4
    h0, (g_wout, g_wup0) = _matmul(xt, w_in, "nt", BF16, "mix_in", 1024, 1024, 1024,
                                   comm=_Comm(gather=[ab_w_out[0].astype(BF16), (up_shard[0], 0, q_up, None)]))
    w_out = g_wout.reshape(D_MODEL, D_MODEL)
    (cat, y0), (g_wup0,) = _mixer_mid_fwd(h0, acw, a_conv_b, a_norm_g, a_norm_b, b_norm_g, b_norm_b, ms, sbt, "mix_mid_fwd",
                                          comm=_Comm(gather=[(up_shard[0], q_up, 2 * q_up, g_wup0)]))
    (z1, x1), (g_wup0,) = _matmul_res_ln(cat, w_out, xt, lng[0, 0], lnb[0, 0], "mix_out_ln", 512,
                                         comm=_Comm(gather=[(up_shard[0], 3 * q_up, q_up, g_wup0)]))
    w_up0 = g_wup0.reshape(2 * D_FF, D_MODEL)
    hf0, (g_wdown0, g_wqkv) = _matmul(x1, w_up0, "nt", BF16, "ffn0_up", 2048, 1408, 1024,
                                      comm=_Comm(gather=[down_shard[0], qkv_shard]))
    w_down0 = g_wdown0.reshape(D_FF, D_MODEL)
    w_qkv = g_wqkv.reshape(Q_WIDTH + KV_WIDTH, D_MODEL)
    (u0, cg0, cv0), (g_wup1,) = _ffn_mid_fwd(hf0, fcw[0], fcb[0], "ffn0_mid_fwd",
                                             comm=_Comm(gather=[(up_shard[1], 0, 3 * q_up, None)]))
    (z2, x2), (g_wo, g_wup1) = _matmul_res_ln(
        u0, w_down0, z1, lng[0, 1], lnb[0, 1], "ffn0_down_ln", 512, prev=(lng[0, 0], lnb[0, 0]),
        comm=_Comm(gather=[c_w_o[0].astype(BF16), (up_shard[1], 3 * q_up, q_up, g_wup1)]))
    w_o = g_wo.reshape(D_MODEL, D_MODEL)
    w_up1 = g_wup1.reshape(2 * D_FF, D_MODEL)
    qkv = _matmul(x2, w_qkv, "nt", BF16, "att_qkv", 1024, 1280, 1024, bias=bqkv)
    att, (g_wdown1,) = _attn_fwd(qkv, c_sinks, "att_fwd", comm=_Comm(gather=[down_shard[1]]))
    w_down1 = g_wdown1.reshape(D_FF, D_MODEL)
    z3, x3 = _matmul_res_ln(att, w_o, z2, lng[1, 0], lnb[1, 0], "att_out_ln", 512, prev=(lng[0, 1], lnb[0, 1]))
    hf1 = _matmul(x3, w_up1, "nt", BF16, "ffn1_up", 2048, 1408, 1024)
    u1, cg1, cv1 = _ffn_mid_fwd(hf1, fcw[1], fcb[1], "ffn1_mid_fwd")

    dz4, dz4b, dg11, db11, loss_terms = _matmul_res_ln_loss(u1, w_down1, z3, lng[1, 1], lnb[1, 1], loss_target[0],
                                                      "ffn1_down_ln_loss", 512, prev=(lng[1, 0], lnb[1, 0]))
    dz3, dz3b, dg10, db10, d_wup1, d_fcw1, d_fcb1, (p_wdown1,) = _ffn_backward(
        dz4, dz4b, x3, z3, lng[1, 0], hf1, cg1, cv1, u1, w_up1, fcw[1], w_down1, 1)
    d_att = _matmul(dz3b, w_o, "nt", BF16, "att_dout", 1024, 1024, 1024)
    d_wo = _matmul(att, dz3b, "tn", BF16, "att_dwo", 1024, 1024, 2048)
    rows_up = d_wup1.shape[1]
    first = 3 * rows_up // 4
    (dq, dkv, dbq, dbkv, dsinks), (p_wup1a,) = _attn_bwd(qkv, d_att, c_sinks, "att_bwd",
                                                        comm=_Comm(exchange=[(d_wup1, 0, first)]))
    d_wqkv = jnp.concatenate([_matmul(dq, x2, "tn", BF16, "att_dwq", 1024, 1024, 2048),
                              _matmul(dkv, x2, "tn", BF16, "att_dwkv", KV_WIDTH, 1024, 1024)], axis=0)
    dz2, dz2b, dg01, db01 = _matmul_ln_bwd([(dq, 0), (dkv, Q_WIDTH)], w_qkv, z2, lng[0, 1], dz3, "att_dx_ln_bwd", 512)
    early = rows_up // 2
    dz1, dz1b, dg00, db00, d_wup0, d_fcw0, d_fcb0, (p_wdown0, p_wup1b, p_wqkv, p_wo, p_wup0a) = _ffn_backward(
        dz2, dz2b, x1, z1, lng[0, 0], hf0, cg0, cv0, u0, w_up0, fcw[0], w_down0, 0, exchange=[(d_wup1, first, rows_up - first)],
        exchange_late=[d_wqkv.reshape(N_DEV, -1, D_MODEL), d_wo.reshape(N_DEV, -1, D_MODEL)], own_rows=early)
    dcat = _matmul(dz1b, w_out, "nt", BF16, "mix_dcat", 1024, 1024, 1024)
    d_wout = _matmul(cat, dz1b, "tn", BF16, "mix_dwout", 1024, 1024, 2048)
    (dh0, d_acw, d_acb, d_ang, d_anb, d_bng, d_bnb, d_ms, d_sb), (p_wup0b, p_wout) = _mixer_mid_bwd(
        h0, y0, dcat, acw, a_norm_g, a_norm_b, b_norm_g, b_norm_b, ms, mst, sbt, "mix_mid_bwd",
        comm=_Comm(exchange=[(d_wup0, early, rows_up - early), d_wout.reshape(N_DEV, -1, D_MODEL)]))
    d_bqkv = jnp.concatenate([dbq, dbkv], axis=1)
    d_lng = jnp.stack([jnp.stack([dg00, dg01]), jnp.stack([dg10, dg11])])
    d_lnb = jnp.stack([jnp.stack([db00, db01]), jnp.stack([db10, db11])])
    small_full = [d_acb, d_ang, d_anb, d_bng, d_bnb, d_ms, d_sb, dsinks[:, :N_Q_HEADS], jnp.concatenate([d_fcb0, d_fcb1], axis=0),
                  d_acw, d_bqkv, jnp.stack([d_fcw0, d_fcw1]), d_lng, d_lnb, loss_terms]
    d_win, (g_small_grads,) = _matmul(dh0, xt, "tn", BF16, "mix_dwin", 1024, 1024, 512, comm=_Comm(gather=[_pack(small_full)]))
    grad_x = _matmul(dh0, w_in, "nn", F32, "mix_dx", 1024, 1024, 1024, res=dz1, res_scale=ALPHA)
    d_win = d_win.reshape(N_DEV, -1, D_MODEL)
    half = d_win.shape[1] // 2

    big = {}
    p_win = []
    for nm, p, w, m, v, tr, transposed, carried in [
            ("ffn_w_up", [(p_wup0a, 0, early), (p_wup0b, early, rows_up - early), (p_wup1a, 0, first), (p_wup1b, first, rows_up - first)], ffn_w_up, m_ffn_w_up, v_ffn_w_up, 176, True,
             (d_win, 0, half)),
            ("ffn_w_down", [p_wdown0, p_wdown1], ffn_w_down, m_ffn_w_down, v_ffn_w_down, 176, False, (d_win, half, d_win.shape[1] - half)),
            ("ab_w_out", [p_wout], ab_w_out, m_ab_w_out, v_ab_w_out, 128, False, None),
            ("c_w_qkv", [p_wqkv], c_w_qkv, m_c_w_qkv, v_c_w_qkv, 160, True, None), ("c_w_o", [p_wo], c_w_o, m_c_w_o, v_c_w_o, 128, False, None),
            ("ab_w_in", p_win, ab_w_in, m_ab_w_in, v_ab_w_in, 128, True, None)]:
        def two_d(a):
            a = jnp.swapaxes(a, 1, 2) if transposed else a
            return a.reshape(-1, a.shape[-1])

        def back(o):
            return jnp.swapaxes(o.reshape(w.shape[0], w.shape[2], w.shape[1]), 1, 2) if transposed else o.reshape(w.shape)

        outs, got = _adamw_big(p, two_d(w), two_d(m), two_d(v), "adamw_" + nm, tr,
                               comm=None if carried is None else _Comm(exchange=[carried]))
        if carried is not None:
            p_win.append((got[0], carried[1], carried[2]))
        big[nm] = [back(o) for o in outs]

    *gs, loss_parts = _unpack(g_small_grads, [a.shape for a in small_full])
    loss = 0.5 / D_MODEL * jnp.sum(loss_parts)

    def my_shard(g, width):
        g = g.reshape(g.shape[:-1] + (N_DEV, width))
        return lax.dynamic_index_in_dim(g, me, axis=g.ndim - 2, keepdims=False)

    small_names = ["a_conv_b", "a_norm_g", "a_norm_b", "b_norm_g", "b_norm_b", "b_spatial_w", "b_spatial_b", "c_sinks", "ffn_conv_b",
                   "a_conv_w", "c_b_qkv", "ffn_conv_w", "ln_g", "ln_b"]
    small_w = [a_conv_b, a_norm_g, a_norm_b, b_norm_g, b_norm_b, b_spatial_w, b_spatial_b, c_sinks, ffn_conv_b,
               a_conv_w, c_b_qkv, ffn_conv_w, ln_g, ln_b]
    small_m = [m_a_conv_b, m_a_norm_g, m_a_norm_b, m_b_norm_g, m_b_norm_b, m_b_spatial_w, m_b_spatial_b, m_c_sinks, m_ffn_conv_b,
               m_a_conv_w, m_c_b_qkv, m_ffn_conv_w, m_ln_g, m_ln_b]
    small_v = [v_a_conv_b, v_a_norm_g, v_a_norm_b, v_b_norm_g, v_b_norm_b, v_b_spatial_w, v_b_spatial_b, v_c_sinks, v_ffn_conv_b,
               v_a_conv_w, v_c_b_qkv, v_ffn_conv_w, v_ln_g, v_ln_b]
    gs[9:] = [my_shard(g, w.shape[-1]) for g, w in zip(gs[9:], small_w[9:])]
    two_d = [(-1, w.shape[-1]) for w in small_w]
    outs = _adamw_small([g.reshape((N_DEV,) + w.reshape(s).shape) for g, w, s in zip(gs, small_w, two_d)],
                        [w.reshape(s) for w, s in zip(small_w, two_d)], [m.reshape(s) for m, s in zip(small_m, two_d)],
                        [v.reshape(s) for v, s in zip(small_v, two_d)], "adamw_small")
    small = {nm: [o.reshape(w.shape) for o in outs[4 * a:4 * a + 4]] for a, (nm, w) in enumerate(zip(small_names, small_w))}

    res = {**big, **small}
    order = ["ab_w_in", "a_conv_w", "a_conv_b", "a_norm_g", "a_norm_b", "b_norm_g", "b_norm_b", "b_spatial_w", "b_spatial_b", "ab_w_out",
             "c_w_qkv", "c_b_qkv", "c_sinks", "c_w_o", "ffn_w_up", "ffn_conv_w", "ffn_conv_b", "ffn_w_down", "ln_g", "ln_b"]
    return (loss, grad_x[None], *[res[nm][0] for nm in order], *[res[nm][1] for nm in order],
            *[res[nm][2] for nm in order], *[res[nm][3] for nm in order])
```

```python
import functools
import math

import jax
import jax.numpy as jnp
from jax import lax
from jax.experimental import pallas as pl
from jax.experimental.pallas import tpu as pltpu

F32 = jnp.float32
BF16 = jnp.bfloat16

N_DEV = 8
D_MODEL = 1024
A_WIDTH = 512
A_KERNEL = 31
B_GROUPS = 4
B_CHUNK = 128
HEAD_DIM = 64
N_Q_HEADS = 16
N_KV_HEADS = 2
ATT_BLOCK = 128
D_FF = 2816
FFN_KERNEL = 3
ALPHA = (2.0 * 2) ** 0.25
LN_EPS = 1e-5
GELU_K = math.sqrt(2.0 / math.pi)
GELU_C = 0.044715
ADAM_LR = 0.001
ADAM_B1 = 0.9
ADAM_B2 = 0.999
ADAM_EPS = 1e-08
ADAM_WD = 0.01
ADAM_STEP = 10
VMEM_LIMIT = 56 * 1024 * 1024
MESH_ID = pl.DeviceIdType.MESH


def _params(*sem):
    return pltpu.CompilerParams(dimension_semantics=sem, vmem_limit_bytes=VMEM_LIMIT)


def _gelu(x):
    t = jnp.tanh(GELU_K * x * (1.0 + GELU_C * x * x))
    return 0.5 * x * (1.0 + t)


def _gelu_and_grad(x):
    x2 = x * x
    t = jnp.tanh(GELU_K * x * (1.0 + GELU_C * x2))
    g = 0.5 * x * (1.0 + t)
    dg = 0.5 * (1.0 + t) + 0.5 * x * (1.0 - t * t) * (GELU_K * (1.0 + 3.0 * GELU_C * x2))
    return g, dg


def _sigmoid(x):
    return 1.0 / (1.0 + jnp.exp(-x))


def _ln_stats(z):
    mu = jnp.mean(z, axis=-1, keepdims=True)
    zc = z - mu
    var = jnp.mean(zc * zc, axis=-1, keepdims=True)
    r = lax.rsqrt(var + LN_EPS)
    return zc * r, r


def _ln_bwd_rows(dn, nh, r):
    return r * (dn - jnp.mean(dn, axis=-1, keepdims=True) - nh * jnp.mean(dn * nh, axis=-1, keepdims=True))


def _colsum(x):
    return jnp.sum(x, axis=0, keepdims=True)


def _dot(a, b, dims):
    return lax.dot_general(a.astype(BF16), b.astype(BF16), (dims, ((), ())), preferred_element_type=F32)


NN = ((1,), (0,))
NT = ((1,), (1,))
TN = ((0,), (0,))


ANY = pl.BlockSpec(memory_space=pl.ANY)
N_RELATIONS = N_DEV - 1


def _my_place():
    return lax.axis_index("x"), lax.axis_index("y"), lax.axis_index("c")


class _Comm:
    def __init__(self, gather=(), exchange=()):
        gather = [e if isinstance(e, tuple) else (e, 0, e.shape[0], None) for e in gather]
        exchange = [e if isinstance(e, tuple) else (e, 0, e.shape[1]) for e in exchange]
        self.arrs = [e[0] for e in gather] + [e[0] for e in exchange]
        self.n_gather = len(gather)
        self.n = len(self.arrs)
        self.rows = [pl.ds(lo, n) for _, lo, n, _ in gather] + [pl.ds(lo, n) for _, lo, n in exchange]
        self.into = {i: e[3] for i, e in enumerate(gather) if e[3] is not None}

    def out_shape(self):
        return [jax.ShapeDtypeStruct(((N_DEV,) + a.shape) if i < self.n_gather else a.shape, a.dtype)
                for i, a in enumerate(self.arrs)]

    def sems(self):
        return [pltpu.SemaphoreType.DMA((self.n, N_RELATIONS)), pltpu.SemaphoreType.DMA((self.n, N_RELATIONS)),
                pltpu.SemaphoreType.DMA((self.n,))]

    def _gather_copy(self, ins, outs, sems, a, k, place, to, from_input=False):
        px, py, pc = place
        block = outs[a].at[4 * px + 2 * py + pc, self.rows[a]]
        return pltpu.make_async_remote_copy(
            src_ref=ins[a].at[self.rows[a]] if from_input else block, dst_ref=block,
            send_sem=sems[0].at[a, k], recv_sem=sems[1].at[a, k], device_id=to, device_id_type=MESH_ID)

    def _exchange_copy(self, ins, outs, sems, a, k, landing=False):
        x, y, c = _my_place()
        me = 4 * x + 2 * y + c
        peer = (x ^ (k >> 2), y ^ ((k >> 1) & 1), c ^ (k & 1))
        return pltpu.make_async_remote_copy(
            src_ref=ins[a].at[me ^ k, self.rows[a]], dst_ref=outs[a].at[(me ^ k) if landing else me, self.rows[a]],
            send_sem=sems[0].at[a, k - 1], recv_sem=sems[1].at[a, k - 1], device_id=peer, device_id_type=MESH_ID)

    def _local_copy(self, ins, outs, sems, a):
        x, y, c = _my_place()
        me = 4 * x + 2 * y + c
        if a < self.n_gather:
            return pltpu.make_async_copy(ins[a].at[self.rows[a]], outs[a].at[me, self.rows[a]], sems[2].at[a])
        return pltpu.make_async_copy(ins[a].at[me, self.rows[a]], outs[a].at[me, self.rows[a]], sems[2].at[a])

    def _first_stage(self, ins, outs, sems, a):
        x, y, c = _my_place()
        me = (x, y, c)
        chips = [(1 - x, y), (x, 1 - y), (1 - x, 1 - y)]
        return ([self._gather_copy(ins, outs, sems, a, 0, me, (x, y, 1 - c), from_input=True)]
                + [self._gather_copy(ins, outs, sems, a, 1 + j, me, (*chip, c), from_input=True) for j, chip in enumerate(chips)])

    def start(self, ins, outs, sems):
        for a in range(self.n):
            self._local_copy(ins, outs, sems, a).start()
        for a in range(self.n_gather):
            for cp in self._first_stage(ins, outs, sems, a):
                cp.start()
        for k in range(1, N_DEV):
            for a in range(self.n_gather, self.n):
                self._exchange_copy(ins, outs, sems, a, k).start()

    def forward(self, ins, outs, sems):
        x, y, c = _my_place()
        me, sibling = (x, y, c), (x, y, 1 - c)
        for j, chip in enumerate([(1 - x, y), (x, 1 - y), (1 - x, 1 - y)]):
            for a in range(self.n_gather):
                self._gather_copy(ins, outs, sems, a, 1 + j, (*chip, c), me).wait_recv()
                self._gather_copy(ins, outs, sems, a, 4 + j, (*chip, c), sibling).start()

    def finish(self, ins, outs, sems):
        x, y, c = _my_place()
        me, sibling = (x, y, c), (x, y, 1 - c)
        chips = [(1 - x, y), (x, 1 - y), (1 - x, 1 - y)]
        passed = [self._gather_copy(ins, outs, sems, a, 4 + j, (*chip, c), sibling)
                  for j, chip in enumerate(chips) for a in range(self.n_gather)]
        for a in range(self.n_gather):
            self._gather_copy(ins, outs, sems, a, 0, sibling, me).wait_recv()
            for j, chip in enumerate(chips):
                self._gather_copy(ins, outs, sems, a, 4 + j, (*chip, 1 - c), me).wait_recv()
        for k in range(1, N_DEV):
            for a in range(self.n_gather, self.n):
                self._exchange_copy(ins, outs, sems, a, k, landing=True).wait_recv()
        for a in range(self.n_gather):
            for cp in self._first_stage(ins, outs, sems, a):
                cp.wait_send()
        for cp in passed:
            cp.wait_send()
        for k in range(1, N_DEV):
            for a in range(self.n_gather, self.n):
                self._exchange_copy(ins, outs, sems, a, k).wait_send()
        for a in range(self.n):
            self._local_copy(ins, outs, sems, a).wait()


def _comm_only(comm, name):
    assert not comm.into

    def body(*refs):
        ins, outs, sems = refs[:comm.n], refs[comm.n:2 * comm.n], refs[2 * comm.n:]
        comm.start(ins, outs, sems)
        comm.forward(ins, outs, sems)
        comm.finish(ins, outs, sems)

    return pl.pallas_call(body, name=name, in_specs=[ANY] * comm.n, out_specs=[ANY] * comm.n,
                          out_shape=comm.out_shape(), scratch_shapes=comm.sems())(*comm.arrs)


def _call(body, *, name, grid, in_specs, out_specs, out_shape, args, sem, scratch_shapes=(), comm=None):
    in_specs, out_specs, out_shape, scratch_shapes = list(in_specs), list(out_specs), list(out_shape), list(scratch_shapes)
    if comm is None:
        outs = pl.pallas_call(body, name=name, grid=grid, in_specs=in_specs, out_specs=out_specs, out_shape=out_shape,
                              scratch_shapes=scratch_shapes, compiler_params=_params(*sem))(*args)
        return list(outs), []
    n_in, n_out, n_scr, nc = len(in_specs), len(out_specs), len(scratch_shapes), comm.n
    completed = sorted(comm.into)

    def wrapped(*refs):
        ins, refs = refs[:n_in], refs[n_in:]
        c_in, refs = refs[:nc], refs[nc + len(completed):]
        outs, refs = refs[:n_out], refs[n_out:]
        c_out, refs = refs[:nc], refs[nc:]
        scr, sems = refs[:n_scr], refs[n_scr:]
        step = functools.reduce(lambda acc, ax: acc * grid[ax] + pl.program_id(ax), range(len(grid)), 0)
        steps = math.prod(grid)

        @pl.when(step == 0)
        def _():
            comm.start(c_in, c_out, sems)

        @pl.when(step == steps - 1)
        def _():
            comm.forward(c_in, c_out, sems)

        body(*ins, *outs, *scr)

        @pl.when(step == steps - 1)
        def _():
            comm.finish(c_in, c_out, sems)

    outs = pl.pallas_call(
        wrapped, name=name, grid=grid, in_specs=in_specs + [ANY] * (nc + len(completed)), out_specs=out_specs + [ANY] * nc,
        out_shape=out_shape + comm.out_shape(), scratch_shapes=scratch_shapes + comm.sems(),
        input_output_aliases={n_in + nc + pos: n_out + item for pos, item in enumerate(completed)},
        compiler_params=_params(*(["arbitrary"] * len(grid))))(*args, *comm.arrs, *[comm.into[item] for item in completed])
    return list(outs[:n_out]), list(outs[n_out:])


def _matmul(a, b, mode, out_dtype, name, tm, tn, tk, *, bias=None, res=None, res_scale=1.0, b_off=0, comm=None):
    tm = min(tm, a.shape[1] if mode == "tn" else a.shape[0])
    tk = min(tk, a.shape[0] if mode == "tn" else a.shape[1])
    if mode == "nn":
        (m, k), n = a.shape, b.shape[1]
        a_spec = pl.BlockSpec((tm, tk), lambda i, j, kk: (i, kk))
        b_spec = pl.BlockSpec((tk, tn), lambda i, j, kk: (kk + b_off, j))
        dims = NN
    elif mode == "nt":
        (m, k), n = a.shape, b.shape[0]
        a_spec = pl.BlockSpec((tm, tk), lambda i, j, kk: (i, kk))
        b_spec = pl.BlockSpec((tn, tk), lambda i, j, kk: (j, kk + b_off))
        dims = NT
    else:
        (k, m), n = a.shape, b.shape[1]
        a_spec = pl.BlockSpec((tk, tm), lambda i, j, kk: (kk, i))
        b_spec = pl.BlockSpec((tk, tn), lambda i, j, kk: (kk, j))
        dims = TN
    assert m % tm == 0 and n % tn == 0 and k % tk == 0, (name, m, n, k)
    nk = k // tk
    in_specs = [a_spec, b_spec]
    args = [a, b]
    if bias is not None:
        in_specs.append(pl.BlockSpec((1, tn), lambda i, j, kk: (0, j)))
        args.append(bias)
    if res is not None:
        in_specs.append(pl.BlockSpec((tm, tn), lambda i, j, kk: (i, j)))
        args.append(res)

    def finish(out, refs, o_ref):
        pos = 2
        if bias is not None:
            out = out + refs[pos][...]
            pos += 1
        if res is not None:
            out = out + res_scale * refs[pos][...].astype(F32)
        o_ref[...] = out.astype(out_dtype)

    def body_one_step(*refs):
        finish(_dot(refs[0][...], refs[1][...], dims), refs, refs[-1])

    def body(*refs):
        a_ref, b_ref = refs[0], refs[1]
        o_ref, acc = refs[-2], refs[-1]
        kk = pl.program_id(2)

        @pl.when(kk == 0)
        def _():
            acc[...] = jnp.zeros_like(acc)

        acc[...] += _dot(a_ref[...], b_ref[...], dims)

        @pl.when(kk == nk - 1)
        def _():
            finish(acc[...], refs, o_ref)

    (out,), moved = _call(
        body_one_step if nk == 1 else body, name=name, grid=(m // tm, n // tn, nk),
        in_specs=in_specs, out_specs=[pl.BlockSpec((tm, tn), lambda i, j, kk: (i, j))],
        out_shape=[jax.ShapeDtypeStruct((m, n), out_dtype)],
        scratch_shapes=[] if nk == 1 else [pltpu.VMEM((tm, tn), F32)],
        sem=("parallel", "parallel", "arbitrary"), args=args, comm=comm)
    return out if comm is None else (out, moved)


def _matmul_tn_pair(a0, a1, b, out_dtype, name, tm, tn, tk, comm=None):
    (k, m), n = a0.shape, b.shape[1]
    tk = min(tk, k)
    assert a1.shape == a0.shape and m % tm == 0 and n % tn == 0 and k % tk == 0, (name, m, n, k)
    mi, nk = m // tm, k // tk

    def body(a0_ref, a1_ref, b_ref, o_ref, acc):
        i, kk = pl.program_id(0), pl.program_id(2)

        @pl.when(kk == 0)
        def _():
            acc[...] = jnp.zeros_like(acc)

        @pl.when(i < mi)
        def _():
            acc[...] += _dot(a0_ref[...], b_ref[...], TN)

        @pl.when(i >= mi)
        def _():
            acc[...] += _dot(a1_ref[...], b_ref[...], TN)

        @pl.when(kk == nk - 1)
        def _():
            o_ref[...] = acc[...].astype(out_dtype)

    (out,), moved = _call(
        body, name=name, grid=(2 * mi, n // tn, nk),
        in_specs=[pl.BlockSpec((tk, tm), lambda i, j, kk: (jnp.where(i < mi, kk, nk - 1), jnp.minimum(i, mi - 1))),
                  pl.BlockSpec((tk, tm), lambda i, j, kk: (jnp.where(i >= mi, kk, 0), jnp.maximum(i - mi, 0))),
                  pl.BlockSpec((tk, tn), lambda i, j, kk: (kk, j))],
        out_specs=[pl.BlockSpec((tm, tn), lambda i, j, kk: (i, j))],
        out_shape=[jax.ShapeDtypeStruct((2 * m, n), out_dtype)],
        scratch_shapes=[pltpu.VMEM((tm, tn), F32)],
        sem=("parallel", "parallel", "arbitrary"), args=(a0, a1, b), comm=comm)
    return out if comm is None else (out, moved)


def _residual_input(x_ref, prev_refs):
    if not prev_refs:
        return x_ref[...]
    nh, _ = _ln_stats(x_ref[...])
    return nh * prev_refs[0][...] + prev_refs[1][...]


def _matmul_res_ln(a, b, x, g, beta, name, tm, prev=None, comm=None):
    t, k = a.shape
    d = b.shape[1]
    tm = min(tm, t)
    assert t % tm == 0
    n_prev = 0 if prev is None else 2

    def body(a_ref, b_ref, x_ref, g_ref, beta_ref, *rest):
        z_ref, xo_ref = rest[n_prev:]
        z = ALPHA * _residual_input(x_ref, rest[:n_prev]) + _dot(a_ref[...], b_ref[...], NN)
        nh, _ = _ln_stats(z)
        z_ref[...] = z
        xo_ref[...] = (nh * g_ref[...] + beta_ref[...]).astype(BF16)

    row = pl.BlockSpec((tm, d), lambda i: (i, 0))
    vec = pl.BlockSpec((1, d), lambda i: (0, 0))
    outs, moved = _call(
        body, name=name, grid=(t // tm,),
        in_specs=[pl.BlockSpec((tm, k), lambda i: (i, 0)), pl.BlockSpec((k, d), lambda i: (0, 0)), row, vec, vec] + [vec] * n_prev,
        out_specs=[row, row],
        out_shape=[jax.ShapeDtypeStruct((t, d), F32), jax.ShapeDtypeStruct((t, d), BF16)],
        sem=("parallel",), args=(a, b, x, g, beta, *(prev or ())), comm=comm)
    return outs if comm is None else (outs, moved)


def _matmul_ln_bwd(parts, b, z, g, dres, name, tm, comm=None):
    m = parts[0][0].shape[0]
    d = b.shape[1]
    tm = min(tm, m)
    n = len(parts)
    assert m % tm == 0 and all(row % a.shape[1] == 0 for a, row in parts)

    def body(*refs):
        z_ref, g_ref, dres_ref = refs[2 * n:2 * n + 3]
        dz_ref, dzb_ref, dg_ref, db_ref = refs[-4:]

        @pl.when(pl.program_id(0) == 0)
        def _():
            dg_ref[...] = jnp.zeros_like(dg_ref)
            db_ref[...] = jnp.zeros_like(db_ref)

        dy = ALPHA * dres_ref[...]
        for p in range(n):
            dy = dy + _dot(refs[p][...], refs[n + p][...], NN)
        nh, r = _ln_stats(z_ref[...])
        dg_ref[...] += _colsum(dy * nh)
        db_ref[...] += _colsum(dy)
        dz = _ln_bwd_rows(dy * g_ref[...], nh, r)
        dz_ref[...] = dz
        dzb_ref[...] = dz.astype(BF16)

    def b_spec(a, row):
        blk = row // a.shape[1]
        return pl.BlockSpec((a.shape[1], d), lambda i: (blk, 0))

    row = pl.BlockSpec((tm, d), lambda i: (i, 0))
    vec = pl.BlockSpec((1, d), lambda i: (0, 0))
    vshape = jax.ShapeDtypeStruct((1, d), F32)
    outs, moved = _call(
        body, name=name, grid=(m // tm,),
        in_specs=[pl.BlockSpec((tm, a.shape[1]), lambda i: (i, 0)) for a, _ in parts] + [b_spec(a, r_) for a, r_ in parts]
        + [row, vec, row],
        out_specs=[row, row, vec, vec],
        out_shape=[jax.ShapeDtypeStruct((m, d), F32), jax.ShapeDtypeStruct((m, d), BF16), vshape, vshape],
        sem=("arbitrary",), args=(*[a for a, _ in parts], *([b] * n), z, g, dres), comm=comm)
    return outs if comm is None else (outs, moved)


def _matmul_res_ln_loss(a, b, x, g, beta, target, name, tm, prev):
    t, k = a.shape
    d = b.shape[1]
    tm = min(tm, t)

    def body(a_ref, b_ref, x_ref, g_ref, beta_ref, t_ref, gp_ref, bp_ref, dz_ref, dzb_ref, dg_ref, db_ref, loss_ref):
        @pl.when(pl.program_id(0) == 0)
        def _():
            dg_ref[...] = jnp.zeros_like(dg_ref)
            db_ref[...] = jnp.zeros_like(db_ref)
            loss_ref[...] = jnp.zeros_like(loss_ref)

        nh, r = _ln_stats(ALPHA * _residual_input(x_ref, (gp_ref, bp_ref)) + _dot(a_ref[...], b_ref[...], NN))
        err = nh * g_ref[...] + beta_ref[...] - t_ref[...]
        loss_ref[...] += _colsum(err * err)
        dy = err * (1.0 / d)
        dg_ref[...] += _colsum(dy * nh)
        db_ref[...] += _colsum(dy)
        dz = _ln_bwd_rows(dy * g_ref[...], nh, r)
        dz_ref[...] = dz
        dzb_ref[...] = dz.astype(BF16)

    row = pl.BlockSpec((tm, d), lambda i: (i, 0))
    vec = pl.BlockSpec((1, d), lambda i: (0, 0))
    vshape = jax.ShapeDtypeStruct((1, d), F32)
    return pl.pallas_call(
        body, name=name, grid=(t // tm,),
        in_specs=[pl.BlockSpec((tm, k), lambda i: (i, 0)), pl.BlockSpec((k, d), lambda i: (0, 0)), row, vec, vec, row, vec, vec],
        out_specs=[row, row, vec, vec, vec],
        out_shape=[jax.ShapeDtypeStruct((t, d), F32), jax.ShapeDtypeStruct((t, d), BF16), vshape, vshape, vshape],
        compiler_params=_params("arbitrary"),
    )(a, b, x, g, beta, target, *prev)


FFN_HALO = 16
FFN_CHUNK = 256
LANES = 128
SUBLANES = 8


def _rows_up(e, start, rows):
    if start % SUBLANES == 0:
        return e[start:start + rows]
    return pltpu.roll(e, e.shape[0] - start, 0)[0:rows]


def _fold(x):
    return jnp.sum(x.reshape(x.shape[0] // SUBLANES, SUBLANES, x.shape[1]), axis=0)


def _ffn_mid_fwd(h, cw, cb, name, tm=1024, tc=1408, comm=None):
    t, f2 = h.shape
    tm = min(tm, t)
    f = f2 // 2
    nj, nt, hb = f // tc, t // tm, tm // FFN_HALO

    ch = min(FFN_CHUNK, tm)

    def body(hg, hgp, hv, hvp, cwg, cwv, cbg, cbv, u_ref, cg_ref, cv_ref):
        i = pl.program_id(1)
        o = FFN_HALO - FFN_KERNEL + 1
        for lg in range(tc // LANES):
            cols = slice(lg * LANES, (lg + 1) * LANES)
            wg, wv = [cwg[k:k + 1, cols] for k in range(FFN_KERNEL)], [cwv[k:k + 1, cols] for k in range(FFN_KERNEL)]
            bg, bv = cbg[:, cols], cbv[:, cols]

            def emit(base, eg, ev):
                cg = wg[0] * _rows_up(eg, o, ch) + wg[1] * _rows_up(eg, o + 1, ch) + wg[2] * _rows_up(eg, o + 2, ch) + bg
                cv = wv[0] * _rows_up(ev, o, ch) + wv[1] * _rows_up(ev, o + 1, ch) + wv[2] * _rows_up(ev, o + 2, ch) + bv
                u_ref[pl.ds(base, ch), cols] = (_gelu(cg) * cv).astype(BF16)
                cg_ref[pl.ds(base, ch), cols] = cg.astype(BF16)
                cv_ref[pl.ds(base, ch), cols] = cv.astype(BF16)

            def first(main, prev):
                return jnp.concatenate([jnp.where(i > 0, prev[:, cols].astype(F32), 0.0), main[0:ch, cols].astype(F32)], axis=0)

            def inner(c, carry):
                base = pl.multiple_of(c * ch, ch)
                emit(base, hg[pl.ds(base - FFN_HALO, ch + FFN_HALO), cols].astype(F32),
                     hv[pl.ds(base - FFN_HALO, ch + FFN_HALO), cols].astype(F32))
                return carry

            emit(0, first(hg, hgp), first(hv, hvp))
            if tm > ch:
                lax.fori_loop(1, tm // ch, inner, 0)

    def main_spec(off):
        return pl.BlockSpec((tm, tc), lambda j, i: (i, j + off))

    def prev_spec(off):
        return pl.BlockSpec((FFN_HALO, tc), lambda j, i: (jnp.maximum(i * hb - 1, 0), j + off))

    def par_spec(rows, off):
        return pl.BlockSpec((rows, tc), lambda j, i: (0, j + off))

    outs, moved = _call(
        body, name=name, grid=(nj, nt),
        in_specs=[main_spec(0), prev_spec(0), main_spec(nj), prev_spec(nj),
                  par_spec(FFN_KERNEL, 0), par_spec(FFN_KERNEL, nj), par_spec(1, 0), par_spec(1, nj)],
        out_specs=[pl.BlockSpec((tm, tc), lambda j, i: (i, j))] * 3,
        out_shape=[jax.ShapeDtypeStruct((t, f), BF16)] * 3,
        sem=("parallel", "arbitrary"), args=(h, h, h, h, cw, cw, cb, cb), comm=comm)
    return outs if comm is None else (outs, moved)


def _ffn_mid_bwd(h, cg, cv, du, cw, name, tm=1024, tc=1408, comm=None):
    t, f2 = h.shape
    tm = min(tm, t)
    f = f2 // 2
    nj, nt, hb = f // tc, t // tm, tm // FFN_HALO

    ch = min(FFN_CHUNK, tm)
    ahead = ch + SUBLANES
    n_ch = tm // ch

    def body(hg, hv, cg_ref, cgn_ref, cv_ref, cvn_ref, du_ref, dun_ref, cwg, cwv,
             dhg_ref, dhv_ref, dcwg_ref, dcwv_ref, dcbg_ref, dcbv_ref):
        i = pl.program_id(1)

        @pl.when(i == 0)
        def _():
            for ref in (dcwg_ref, dcwv_ref, dcbg_ref, dcbv_ref):
                ref[...] = jnp.zeros_like(ref)

        for lg in range(tc // LANES):
            cols = slice(lg * LANES, (lg + 1) * LANES)
            wg, wv = [cwg[k:k + 1, cols] for k in range(FFN_KERNEL)], [cwv[k:k + 1, cols] for k in range(FFN_KERNEL)]

            def emit(base, cg_e, cv_e, du_e, acc):
                cg_a, cv_a, du_a = cg_e[0:ahead], cv_e[0:ahead], du_e[0:ahead]
                gl, dgl = _gelu_and_grad(cg_a)

                def back(d, h_ref, w, dh_ref):
                    later = [d[0:ch], _rows_up(d, 1, ch), _rows_up(d, 2, ch)]
                    dh_ref[pl.ds(base, ch), cols] = (w[2] * later[0] + w[1] * later[1] + w[0] * later[2]).astype(BF16)
                    h_own = h_ref[pl.ds(base, ch), cols].astype(F32)
                    return [_fold(later[0])] + [_fold(later[FFN_KERNEL - 1 - k] * h_own) for k in range(FFN_KERNEL)]

                sums = back(du_a * cv_a * dgl, hg, wg, dhg_ref) + back(du_a * gl, hv, wv, dhv_ref)
                return tuple(a + s_ for a, s_ in zip(acc, sums))

            def inner(c, acc):
                base = pl.multiple_of(c * ch, ch)
                rows = pl.ds(base, ch + FFN_HALO)
                return emit(base, cg_ref[rows, cols].astype(F32), cv_ref[rows, cols].astype(F32), du_ref[rows, cols].astype(F32), acc)

            def last(acc):
                def rows(main, after):
                    return jnp.concatenate([main[tm - ch:tm, cols].astype(F32), after], axis=0)

                du_next = jnp.where(i < nt - 1, dun_ref[:, cols].astype(F32), 0.0)
                return emit(tm - ch, rows(cg_ref, cgn_ref[:, cols].astype(F32)), rows(cv_ref, cvn_ref[:, cols].astype(F32)),
                            rows(du_ref, du_next), acc)

            acc = (jnp.zeros((SUBLANES, LANES), F32),) * (2 * (1 + FFN_KERNEL))
            if n_ch > 1:
                acc = lax.fori_loop(0, n_ch - 1, inner, acc)
            acc = last(acc)
            dcbg_ref[:, cols] += _colsum(acc[0])
            dcbv_ref[:, cols] += _colsum(acc[1 + FFN_KERNEL])
            for k in range(FFN_KERNEL):
                dcwg_ref[k:k + 1, cols] += _colsum(acc[1 + k])
                dcwv_ref[k:k + 1, cols] += _colsum(acc[2 + FFN_KERNEL + k])

    last_blk = t // FFN_HALO - 1

    def main_spec(off):
        return pl.BlockSpec((tm, tc), lambda j, i: (i, j + off))

    def next_spec(off):
        return pl.BlockSpec((FFN_HALO, tc), lambda j, i: (jnp.minimum((i + 1) * hb, last_blk), j + off))

    def par_spec(rows, off):
        return pl.BlockSpec((rows, tc), lambda j, i: (0, j + off))

    out_tile = pl.BlockSpec((tm, tc), lambda j, i: (i, j))
    outs, moved = _call(
        body, name=name, grid=(nj, nt),
        in_specs=[main_spec(0), main_spec(nj), main_spec(0), next_spec(0), main_spec(0), next_spec(0), main_spec(0), next_spec(0),
                  par_spec(FFN_KERNEL, 0), par_spec(FFN_KERNEL, nj)],
        out_specs=[out_tile, out_tile, par_spec(FFN_KERNEL, 0), par_spec(FFN_KERNEL, 0), par_spec(1, 0), par_spec(1, 0)],
        out_shape=[jax.ShapeDtypeStruct((t, f), BF16), jax.ShapeDtypeStruct((t, f), BF16),
                   jax.ShapeDtypeStruct((FFN_KERNEL, f), F32), jax.ShapeDtypeStruct((FFN_KERNEL, f), F32),
                   jax.ShapeDtypeStruct((1, f), F32), jax.ShapeDtypeStruct((1, f), F32)],
        sem=("parallel", "arbitrary"), args=(h, h, cg, cg, cv, cv, du, du, cw, cw), comm=comm)
    return outs if comm is None else (outs, moved)


MIX_HALO = 32


def _glu(hh):
    return hh[:, 0:A_WIDTH] * _sigmoid(hh[:, A_WIDTH:2 * A_WIDTH])


def _fill_row_shifts(s):
    rows = s.shape[1] - SUBLANES
    for j in range(1, SUBLANES):
        s[j, 0:rows, :] = s[0, pl.ds(j, rows), :]


def _rows_from(s, start, rows):
    j = start % SUBLANES
    return s[j, start - j:start - j + rows, :]


def _tril_mask():
    return lax.broadcasted_iota(jnp.int32, (B_CHUNK, B_CHUNK), 0) >= lax.broadcasted_iota(jnp.int32, (B_CHUNK, B_CHUNK), 1)


def _spatial_mix(q, ms_ref, sbt_ref, tm):
    mask = _tril_mask()
    ws = [jnp.where(mask, ms_ref[g], 0.0).astype(BF16) for g in range(B_GROUPS)]
    qb = q.astype(BF16)
    rows = []
    for c in range(tm // B_CHUNK):
        cols = [_dot(ws[g], qb[c * B_CHUNK:(c + 1) * B_CHUNK, g * 128:(g + 1) * 128], NN) + sbt_ref[:, g:g + 1]
                for g in range(B_GROUPS)]
        rows.append(jnp.concatenate(cols, axis=1))
    return jnp.concatenate(rows, axis=0)


def _mixer_mid_fwd(h, cw, cb, ag, ab, bg, bb, ms, sbt, name, tm=256, comm=None):
    t = h.shape[0]
    nt, hb = t // tm, tm // MIX_HALO
    o = MIX_HALO - A_KERNEL + 1

    def body(h_ref, hp_ref, cw_ref, cb_ref, ag_ref, ab_ref, bg_ref, bb_ref, ms_ref, sbt_ref, cat_ref, y_ref, sp):
        i = pl.program_id(0)
        sp[0, 0:MIX_HALO, :] = jnp.where(i > 0, _glu(hp_ref[:, 0:2 * A_WIDTH].astype(F32)), 0.0)
        sp[0, MIX_HALO:, :] = _glu(h_ref[:, 0:2 * A_WIDTH].astype(F32))
        _fill_row_shifts(sp)
        y = jnp.zeros((tm, A_WIDTH), F32) + cb_ref[...]
        for k in range(A_KERNEL):
            y = y + cw_ref[k:k + 1, :] * _rows_from(sp, o + k, tm)
        y_ref[...] = y.astype(BF16)
        nh, _ = _ln_stats(y)
        ln = nh * ag_ref[...] + ab_ref[...]
        cat_ref[:, 0:A_WIDTH] = (ln * _sigmoid(ln)).astype(BF16)
        u = _gelu(h_ref[:, 1024:1536].astype(F32))
        nb, _ = _ln_stats(_gelu(h_ref[:, 1536:2048].astype(F32)))
        mixed = _spatial_mix(nb * bg_ref[...] + bb_ref[...], ms_ref, sbt_ref, tm)
        cat_ref[:, A_WIDTH:] = (u * mixed).astype(BF16)

    vec = pl.BlockSpec((1, A_WIDTH), lambda i: (0, 0))
    outs, moved = _call(
        body, name=name, grid=(nt,),
        in_specs=[pl.BlockSpec((tm, 2048), lambda i: (i, 0)),
                  pl.BlockSpec((MIX_HALO, 2048), lambda i: (jnp.maximum(i * hb - 1, 0), 0)),
                  pl.BlockSpec((A_KERNEL, A_WIDTH), lambda i: (0, 0)), vec, vec, vec, vec, vec,
                  pl.BlockSpec((B_GROUPS, B_CHUNK, B_CHUNK), lambda i: (0, 0, 0)),
                  pl.BlockSpec((B_CHUNK, B_GROUPS), lambda i: (0, 0))],
        out_specs=[pl.BlockSpec((tm, D_MODEL), lambda i: (i, 0)), pl.BlockSpec((tm, A_WIDTH), lambda i: (i, 0))],
        out_shape=[jax.ShapeDtypeStruct((t, D_MODEL), BF16), jax.ShapeDtypeStruct((t, A_WIDTH), BF16)],
        scratch_shapes=[pltpu.VMEM((SUBLANES, tm + MIX_HALO, A_WIDTH), F32)],
        sem=("parallel",), args=(h, h, cw, cb, ag, ab, bg, bb, ms, sbt), comm=comm)
    return outs if comm is None else (outs, moved)


def _mixer_mid_bwd(h, y, dcat, cw, ag, ab, bg, bb, ms, mst, sbt, name, tm=256, comm=None):
    t = h.shape[0]
    nt, hb = t // tm, tm // MIX_HALO
    r = tm + MIX_HALO
    nchunk = tm // B_CHUNK

    def body(h_ref, y_ref, yn_ref, dc_ref, dcn_ref, cw_ref, ag_ref, ab_ref, bg_ref, bb_ref, ms_ref, mst_ref, sbt_ref,
             dh_ref, dcw_ref, dcb_ref, dag_ref, dab_ref, dbg_ref, dbb_ref, dms_ref, dsb_ref, sdy, sbacc):
        i = pl.program_id(0)

        @pl.when(i == 0)
        def _():
            for ref in (dcw_ref, dcb_ref, dag_ref, dab_ref, dbg_ref, dbb_ref, dms_ref, dsb_ref, sbacc):
                ref[...] = jnp.zeros_like(ref)

        nh, rs = _ln_stats(jnp.concatenate([y_ref[...].astype(F32), yn_ref[...].astype(F32)], axis=0))
        ln = nh * ag_ref[...] + ab_ref[...]
        sg = _sigmoid(ln)
        dao = jnp.concatenate([dc_ref[:, 0:A_WIDTH].astype(F32),
                               jnp.where(i < nt - 1, dcn_ref[:, 0:A_WIDTH].astype(F32), 0.0)], axis=0)
        dln = dao * (sg * (1.0 + ln * (1.0 - sg)))
        dag_ref[...] += _colsum(dln[0:tm] * nh[0:tm])
        dab_ref[...] += _colsum(dln[0:tm])
        sdy[0] = _ln_bwd_rows(dln * ag_ref[...], nh, rs)
        _fill_row_shifts(sdy)
        dcb_ref[...] += _colsum(sdy[0, 0:tm, :])
        av = h_ref[:, 0:A_WIDTH].astype(F32)
        s = _sigmoid(h_ref[:, A_WIDTH:2 * A_WIDTH].astype(F32))
        p_own = av * s
        dp = jnp.zeros((tm, A_WIDTH), F32)
        for k in range(A_KERNEL):
            later = _rows_from(sdy, A_KERNEL - 1 - k, tm)
            dcw_ref[k:k + 1, :] += _colsum(later * p_own)
            dp = dp + cw_ref[k:k + 1, :] * later
        dh_ref[:, 0:A_WIDTH] = (dp * s).astype(BF16)
        dh_ref[:, A_WIDTH:2 * A_WIDTH] = (dp * av * s * (1.0 - s)).astype(BF16)

        u, dgu = _gelu_and_grad(h_ref[:, 1024:1536].astype(F32))
        w, dgw = _gelu_and_grad(h_ref[:, 1536:2048].astype(F32))
        nb, rb = _ln_stats(w)
        q = nb * bg_ref[...] + bb_ref[...]
        mixed = _spatial_mix(q, ms_ref, sbt_ref, tm)
        dbo = dc_ref[:, A_WIDTH:].astype(F32)
        dh_ref[:, 1024:1536] = (dbo * mixed * dgu).astype(BF16)
        dmx = dbo * u
        mask = _tril_mask()
        wst = [jnp.where(mask.T, mst_ref[g], 0.0).astype(BF16) for g in range(B_GROUPS)]
        qb = q.astype(BF16)
        dmb = dmx.astype(BF16)
        rows = []
        for c in range(nchunk):
            cols = []
            for g in range(B_GROUPS):
                rs_, cs_ = slice(c * B_CHUNK, (c + 1) * B_CHUNK), slice(g * 128, (g + 1) * 128)
                sbacc[g] += dmx[rs_, cs_]
                dms_ref[g] += _dot(dmb[rs_, cs_], qb[rs_, cs_], NT)
                cols.append(_dot(wst[g], dmb[rs_, cs_], NN))
            rows.append(jnp.concatenate(cols, axis=1))
        dq = jnp.concatenate(rows, axis=0)
        dbg_ref[...] += _colsum(dq * nb)
        dbb_ref[...] += _colsum(dq)
        dh_ref[:, 1536:2048] = (_ln_bwd_rows(dq * bg_ref[...], nb, rb) * dgw).astype(BF16)

        @pl.when(i == nt - 1)
        def _():
            for g in range(B_GROUPS):
                dms_ref[g] = jnp.where(mask, dms_ref[g], 0.0)
                dsb_ref[g] = jnp.sum(sbacc[g], axis=1, keepdims=True)

    last_blk = t // MIX_HALO - 1
    vec = pl.BlockSpec((1, A_WIDTH), lambda i: (0, 0))
    mat = pl.BlockSpec((B_GROUPS, B_CHUNK, B_CHUNK), lambda i: (0, 0, 0))
    taps = pl.BlockSpec((A_KERNEL, A_WIDTH), lambda i: (0, 0))

    def halo(width):
        return pl.BlockSpec((MIX_HALO, width), lambda i: (jnp.minimum((i + 1) * hb, last_blk), 0))

    vshape = jax.ShapeDtypeStruct((1, A_WIDTH), F32)
    outs, moved = _call(
        body, name=name, grid=(nt,),
        in_specs=[pl.BlockSpec((tm, 2048), lambda i: (i, 0)), pl.BlockSpec((tm, A_WIDTH), lambda i: (i, 0)), halo(A_WIDTH),
                  pl.BlockSpec((tm, D_MODEL), lambda i: (i, 0)), halo(D_MODEL),
                  taps, vec, vec, vec, vec, mat, mat, pl.BlockSpec((B_CHUNK, B_GROUPS), lambda i: (0, 0))],
        out_specs=[pl.BlockSpec((tm, 2048), lambda i: (i, 0)), taps, vec, vec, vec, vec, vec, mat,
                   pl.BlockSpec((B_GROUPS, B_CHUNK, 1), lambda i: (0, 0, 0))],
        out_shape=[jax.ShapeDtypeStruct((t, 2048), BF16), jax.ShapeDtypeStruct((A_KERNEL, A_WIDTH), F32),
                   vshape, vshape, vshape, vshape, vshape,
                   jax.ShapeDtypeStruct((B_GROUPS, B_CHUNK, B_CHUNK), F32), jax.ShapeDtypeStruct((B_GROUPS, B_CHUNK, 1), F32)],
        scratch_shapes=[pltpu.VMEM((SUBLANES, r, A_WIDTH), F32), pltpu.VMEM((B_GROUPS, B_CHUNK, B_CHUNK), F32)],
        sem=("arbitrary",), args=(h, y, y, dcat, dcat, cw, ag, ab, bg, bb, ms, mst, sbt), comm=comm)
    return outs if comm is None else (outs, moved)


Q_WIDTH = N_Q_HEADS * HEAD_DIM
KV_WIDTH = 2 * N_KV_HEADS * HEAD_DIM
PAIRS_PER_KV = N_Q_HEADS // N_KV_HEADS // 2
ATT_SCALE = 1.0 / math.sqrt(HEAD_DIM)


def _dup_heads(pair_cols, kv_head):
    lane = lax.broadcasted_iota(jnp.int32, pair_cols.shape, 1)
    rolled = pltpu.roll(pair_cols, HEAD_DIM, 1)
    first = lane < HEAD_DIM
    return jnp.where(first, pair_cols, rolled) if kv_head == 0 else jnp.where(first, rolled, pair_cols)


HEADS_PER_KV = N_Q_HEADS // N_KV_HEADS


def _stack_heads(ref, kh):
    lane = lax.broadcasted_iota(jnp.int32, (ATT_BLOCK, 128), 1)
    rows = []
    for pr in range(PAIRS_PER_KV):
        c0 = (kh * PAIRS_PER_KV + pr) * 128
        pair = ref[:, c0:c0 + 128]
        rows += [jnp.where(lane < HEAD_DIM, pair, jnp.zeros_like(pair)), jnp.where(lane < HEAD_DIM, jnp.zeros_like(pair), pair)]
    return jnp.concatenate(rows, axis=0)


def _unstack_heads(stacked, kh, write):
    lane = lax.broadcasted_iota(jnp.int32, (ATT_BLOCK, 128), 1)
    for pr in range(PAIRS_PER_KV):
        first = stacked[(2 * pr) * ATT_BLOCK:(2 * pr + 1) * ATT_BLOCK]
        second = stacked[(2 * pr + 1) * ATT_BLOCK:(2 * pr + 2) * ATT_BLOCK]
        write((kh * PAIRS_PER_KV + pr) * 128, jnp.where(lane < HEAD_DIM, first, second))


def _sink_row(sink_ref, kh):
    return jnp.concatenate([jnp.full((1, ATT_BLOCK), sink_ref[0, kh * HEADS_PER_KV + h], F32) for h in range(HEADS_PER_KV)], axis=1)


def _att_window_bias():
    sj = lax.broadcasted_iota(jnp.int32, (2 * ATT_BLOCK, HEADS_PER_KV * ATT_BLOCK), 0)
    qi = lax.broadcasted_iota(jnp.int32, (2 * ATT_BLOCK, HEADS_PER_KV * ATT_BLOCK), 1) & (ATT_BLOCK - 1)
    diff = qi + ATT_BLOCK - sj
    return jnp.where((diff >= 0) & (diff < ATT_BLOCK), 0.0, -jnp.inf)


def _att_probs_t(q_all, k2, bias_ref, n, sink):
    st = _dot(k2, q_all, NT) * ATT_SCALE + bias_ref[...]
    st = jnp.concatenate([jnp.where(n > 0, st[0:ATT_BLOCK], -jnp.inf), st[ATT_BLOCK:]], axis=0)
    m = jnp.maximum(jnp.max(st, axis=0, keepdims=True), sink)
    e = jnp.exp(st - m)
    es = jnp.exp(sink - m)
    inv = 1.0 / (jnp.sum(e, axis=0, keepdims=True) + es)
    return e * inv, es * inv


def _attn_fwd(qkv, sinks, name, comm=None):
    t = qkv.shape[0]
    nb = t // ATT_BLOCK
    kvb = Q_WIDTH // KV_WIDTH

    def body(sink_ref, q_ref, kv_ref, kvp_ref, o_ref, bias):
        n = pl.program_id(0)

        @pl.when(n == 0)
        def _():
            bias[...] = _att_window_bias()

        kv = jnp.concatenate([kvp_ref[...], kv_ref[...]], axis=0).astype(F32)

        def write(c0, pair):
            o_ref[:, c0:c0 + 128] = pair.astype(BF16)

        for kh in range(N_KV_HEADS):
            k2 = _dup_heads(kv[:, 0:128], kh).astype(BF16)
            v2 = _dup_heads(kv[:, 128:256], kh).astype(BF16)
            pt, _ = _att_probs_t(_stack_heads(q_ref, kh), k2, bias, n, _sink_row(sink_ref, kh))
            _unstack_heads(_dot(v2, pt, TN).T, kh, write)

    (out,), moved = _call(
        body, name=name, grid=(nb,),
        in_specs=[pl.BlockSpec(memory_space=pltpu.SMEM),
                  pl.BlockSpec((ATT_BLOCK, Q_WIDTH), lambda n: (n, 0)),
                  pl.BlockSpec((ATT_BLOCK, KV_WIDTH), lambda n: (n, kvb)),
                  pl.BlockSpec((ATT_BLOCK, KV_WIDTH), lambda n: (jnp.maximum(n - 1, 0), kvb))],
        out_specs=[pl.BlockSpec((ATT_BLOCK, Q_WIDTH), lambda n: (n, 0))],
        out_shape=[jax.ShapeDtypeStruct((t, Q_WIDTH), BF16)],
        scratch_shapes=[pltpu.VMEM((2 * ATT_BLOCK, HEADS_PER_KV * ATT_BLOCK), F32)],
        sem=("arbitrary",), args=(sinks, qkv, qkv, qkv), comm=comm)
    return out if comm is None else (out, moved)


def _attn_bwd(qkv, d_o, sinks, name, comm=None):
    t = qkv.shape[0]
    nb = t // ATT_BLOCK
    kvb = Q_WIDTH // KV_WIDTH

    def body(sink_ref, q_ref, kv_ref, kvp_ref, do_ref, dq_ref, dkv_ref, dbq_ref, dbkv_ref, dsink_ref, carry, bias):
        n = pl.program_id(0)

        @pl.when(n == 0)
        def _():
            for ref in (dbq_ref, dbkv_ref, dsink_ref, carry):
                ref[...] = jnp.zeros_like(ref)
            dkv_ref[...] = jnp.zeros_like(dkv_ref)
            bias[...] = _att_window_bias()

        @pl.when(n < nb)
        def _():
            kv = jnp.concatenate([kvp_ref[...], kv_ref[...]], axis=0).astype(F32)
            lane2 = lax.broadcasted_iota(jnp.int32, (2 * ATT_BLOCK, 128), 1)
            sink_lane = lax.broadcasted_iota(jnp.int32, (1, 128), 1)
            dsink = jnp.zeros((1, 128), F32)
            dk_parts, dv_parts = [], []

            def write(c0, pair):
                dbq_ref[:, c0:c0 + 128] += _colsum(pair)
                dq_ref[:, c0:c0 + 128] = pair.astype(BF16)

            for kh in range(N_KV_HEADS):
                k2 = _dup_heads(kv[:, 0:128], kh).astype(BF16)
                v2 = _dup_heads(kv[:, 128:256], kh).astype(BF16)
                q_all = _stack_heads(q_ref, kh)
                do_all = _stack_heads(do_ref, kh)
                pt, ps = _att_probs_t(q_all, k2, bias, n, _sink_row(sink_ref, kh))
                dpt = _dot(v2, do_all, NT)
                delta = jnp.sum(pt * dpt, axis=0, keepdims=True)
                dst = pt * (dpt - delta) * ATT_SCALE
                psd = ps * delta
                for h in range(HEADS_PER_KV):
                    dsink = dsink + jnp.where(sink_lane == kh * HEADS_PER_KV + h,
                                              -jnp.sum(psd[:, h * ATT_BLOCK:(h + 1) * ATT_BLOCK]), 0.0)
                _unstack_heads(_dot(k2, dst, TN).T, kh, write)
                dk_acc = _dot(dst, q_all, NN)
                dv_acc = _dot(pt, do_all, NN)
                dk_parts.append(dk_acc + pltpu.roll(dk_acc, HEAD_DIM, 1))
                dv_parts.append(dv_acc + pltpu.roll(dv_acc, HEAD_DIM, 1))
            dk = jnp.where(lane2 < HEAD_DIM, dk_parts[0], dk_parts[1])
            dv = jnp.where(lane2 < HEAD_DIM, dv_parts[0], dv_parts[1])
            dkv_new = jnp.concatenate([dk, dv], axis=1)
            done = carry[...] + dkv_new[0:ATT_BLOCK]

            @pl.when(n > 0)
            def _():
                dkv_ref[...] = done.astype(BF16)
                dbkv_ref[...] += _colsum(done)

            carry[...] = dkv_new[ATT_BLOCK:]
            dsink_ref[...] += dsink

        @pl.when(n == nb)
        def _():
            dkv_ref[...] = carry[...].astype(BF16)
            dbkv_ref[...] += _colsum(carry[...])

    def clamp(n):
        return jnp.minimum(n, nb - 1)

    outs, moved = _call(
        body, name=name, grid=(nb + 1,),
        in_specs=[pl.BlockSpec(memory_space=pltpu.SMEM),
                  pl.BlockSpec((ATT_BLOCK, Q_WIDTH), lambda n: (clamp(n), 0)),
                  pl.BlockSpec((ATT_BLOCK, KV_WIDTH), lambda n: (clamp(n), kvb)),
                  pl.BlockSpec((ATT_BLOCK, KV_WIDTH), lambda n: (jnp.maximum(clamp(n) - 1, 0), kvb)),
                  pl.BlockSpec((ATT_BLOCK, Q_WIDTH), lambda n: (clamp(n), 0))],
        out_specs=[pl.BlockSpec((ATT_BLOCK, Q_WIDTH), lambda n: (clamp(n), 0)),
                   pl.BlockSpec((ATT_BLOCK, KV_WIDTH), lambda n: (jnp.maximum(n - 1, 0), 0)),
                   pl.BlockSpec((1, Q_WIDTH), lambda n: (0, 0)),
                   pl.BlockSpec((1, KV_WIDTH), lambda n: (0, 0)),
                   pl.BlockSpec((1, 128), lambda n: (0, 0))],
        out_shape=[jax.ShapeDtypeStruct((t, Q_WIDTH), BF16), jax.ShapeDtypeStruct((t, KV_WIDTH), BF16),
                   jax.ShapeDtypeStruct((1, Q_WIDTH), F32), jax.ShapeDtypeStruct((1, KV_WIDTH), F32),
                   jax.ShapeDtypeStruct((1, 128), F32)],
        scratch_shapes=[pltpu.VMEM((ATT_BLOCK, KV_WIDTH), F32), pltpu.VMEM((2 * ATT_BLOCK, HEADS_PER_KV * ATT_BLOCK), F32)],
        sem=("arbitrary",), args=(sinks, qkv, qkv, qkv, d_o), comm=comm)
    return outs if comm is None else (outs, moved)


def _adamw_math(g, w, m, v):
    m = ADAM_B1 * m + (1.0 - ADAM_B1) * g
    v = ADAM_B2 * v + (1.0 - ADAM_B2) * (g * g)
    m_hat = m / (1.0 - ADAM_B1 ** ADAM_STEP)
    v_hat = v / (1.0 - ADAM_B2 ** ADAM_STEP)
    delta = -ADAM_LR * (m_hat / (jnp.sqrt(v_hat) + ADAM_EPS) + ADAM_WD * w)
    return delta, m, v


def _sum_partials(p_ref):
    g = p_ref[0].astype(F32)
    for s in range(1, N_DEV):
        g = g + p_ref[s].astype(F32)
    return g


def _adamw_big(parts, w, m, v, name, tr):
    r, c = w.shape
    parts = [p if isinstance(p, tuple) else (p, 0, p.shape[1]) for p in parts]
    tiles = [rows // tr for _, _, rows in parts]
    starts = [sum(tiles[:l]) for l in range(len(parts))]
    assert all(lo % tr == 0 and rows % tr == 0 for _, lo, rows in parts) and sum(tiles) * tr == r

    def body(*refs):
        p_refs, (w_ref, m_ref, v_ref, g_out, d_out, m_out, v_out) = refs[:len(parts)], refs[len(parts):]
        i = pl.program_id(0)
        for l, p_ref in enumerate(p_refs):
            @pl.when((i >= starts[l]) & (i < starts[l] + tiles[l]))
            def _():
                g = _sum_partials(p_ref)
                g_out[...] = g
                d_out[...], m_out[...], v_out[...] = _adamw_math(g, w_ref[...], m_ref[...], v_ref[...])

    def part_spec(l):
        return pl.BlockSpec((N_DEV, tr, c), lambda i: (0, jnp.clip(i - starts[l], 0, tiles[l] - 1) + parts[l][1] // tr, 0))

    tile = pl.BlockSpec((tr, c), lambda i: (i, 0))
    shape = jax.ShapeDtypeStruct((r, c), F32)
    return pl.pallas_call(
        body, name=name, grid=(r // tr,),
        in_specs=[part_spec(l) for l in range(len(parts))] + [tile, tile, tile],
        out_specs=[tile] * 4, out_shape=[shape] * 4,
        compiler_params=_params("parallel"),
    )(*[p[0] for p in parts], w, m, v)


def _adamw_small(parts, ws, ms, vs, name):
    n = len(ws)

    def body(*refs):
        ins, outs = refs[:4 * n], refs[4 * n:]
        for a in range(n):
            g = _sum_partials(ins[a])
            outs[4 * a][...] = g
            outs[4 * a + 1][...], outs[4 * a + 2][...], outs[4 * a + 3][...] = _adamw_math(
                g, ins[n + a][...], ins[2 * n + a][...], ins[3 * n + a][...])

    out_shape = []
    for w in ws:
        out_shape += [jax.ShapeDtypeStruct(w.shape, F32)] * 4
    return pl.pallas_call(body, name=name, out_shape=out_shape, compiler_params=_params())(*parts, *ws, *ms, *vs)


PACK_LANES = 128
PACK_ROWS = 8


def _pack(arrs):
    flat = jnp.concatenate([a.reshape(-1).astype(F32) for a in arrs])
    unit = PACK_LANES * PACK_ROWS
    total = -(-flat.shape[0] // unit) * unit
    return jnp.pad(flat, (0, total - flat.shape[0])).reshape(-1, PACK_LANES)


def _unpack(buf, shapes):
    flat = buf.reshape(N_DEV, -1)
    out, pos = [], 0
    for s in shapes:
        size = math.prod(s)
        out.append(flat[:, pos:pos + size].reshape((N_DEV,) + tuple(s)))
        pos += size
    return out


def _interleave(g):
    return jnp.transpose(g, (1, 0, 2)).reshape(g.shape[1], -1)


def _ffn_backward(dz, dzb, x_in, z_in, g_in, h, cg, cv, u, w_up_t, cw, w_down, tag, exchange=(), exchange_late=(), own_rows=0):
    du = _matmul(dzb, w_down, "nt", BF16, f"ffn{tag}_du", 2048, 1408, 1024)
    d_w_down = _matmul(u, dzb, "tn", BF16, f"ffn{tag}_dwdown", 1408, 1024, 2048)
    (dhg, dhv, dcwg, dcwv, dcbg, dcbv), moved = _ffn_mid_bwd(
        h, cg, cv, du, cw, f"ffn{tag}_mid_bwd", comm=_Comm(exchange=[d_w_down.reshape(N_DEV, -1, D_MODEL), *exchange]))
    d_w_up_t = _matmul_tn_pair(dhg, dhv, x_in, BF16, f"ffn{tag}_dwup", 1408, 1024, 2048,
                               comm=_Comm(exchange=exchange_late) if exchange_late else None)
    if exchange_late:
        d_w_up_t, late = d_w_up_t
        moved = moved + late
    d_up_blocks = d_w_up_t.reshape(N_DEV, -1, D_MODEL)
    outs = _matmul_ln_bwd([(dhg, 0), (dhv, D_FF)], w_up_t, z_in, g_in, dz, f"ffn{tag}_dx_ln_bwd", 256,
                          comm=_Comm(exchange=[(d_up_blocks, 0, own_rows)]) if own_rows else None)
    (dz_in, dzb_in, dg_in, db_in), own = outs if own_rows else (outs, [])
    moved = moved + own
    return (dz_in, dzb_in, dg_in, db_in, d_up_blocks,
            jnp.concatenate([dcwg, dcwv], axis=1), jnp.concatenate([dcbg, dcbv], axis=1), moved)


def kernel(x, ab_w_in, a_conv_w, a_conv_b, a_norm_g, a_norm_b, b_norm_g, b_norm_b, b_spatial_w, b_spatial_b, ab_w_out, c_w_qkv, c_b_qkv, c_sinks, c_w_o, ffn_w_up, ffn_conv_w, ffn_conv_b, ffn_w_down, ln_g, ln_b, loss_target, m_ab_w_in, m_a_conv_w, m_a_conv_b, m_a_norm_g, m_a_norm_b, m_b_norm_g, m_b_norm_b, m_b_spatial_w, m_b_spatial_b, m_ab_w_out, m_c_w_qkv, m_c_b_qkv, m_c_sinks, m_c_w_o, m_ffn_w_up, m_ffn_conv_w, m_ffn_conv_b, m_ffn_w_down, m_ln_g, m_ln_b, v_ab_w_in, v_a_conv_w, v_a_conv_b, v_a_norm_g, v_a_norm_b, v_b_norm_g, v_b_norm_b, v_b_spatial_w, v_b_spatial_b, v_ab_w_out, v_c_w_qkv, v_c_b_qkv, v_c_sinks, v_c_w_o, v_ffn_w_up, v_ffn_conv_w, v_ffn_conv_b, v_ffn_w_down, v_ln_g, v_ln_b):
    me = 4 * lax.axis_index("x") + 2 * lax.axis_index("y") + lax.axis_index("c")
    xt = x[0]
    t = xt.shape[0]

    small_shard_shapes = [a_conv_w.shape, c_b_qkv.shape, ffn_conv_w.shape, ln_g.shape, ln_b.shape]
    up_shard = [jnp.swapaxes(ffn_w_up[l], 0, 1).astype(BF16) for l in range(2)]
    qkv_shard = jnp.swapaxes(c_w_qkv[0], 0, 1).astype(BF16)
    down_shard = [ffn_w_down[l].astype(BF16) for l in range(2)]
    g_win, g_small = _comm_only(
        _Comm(gather=[jnp.swapaxes(ab_w_in[0], 0, 1).astype(BF16), _pack([a_conv_w, c_b_qkv, ffn_conv_w, ln_g, ln_b])]),
        "gather_first")
    w_in = g_win.reshape(-1, D_MODEL)
    g_acw, g_bqkv, g_fcw, g_lng, g_lnb = _unpack(g_small, small_shard_shapes)
    acw = _interleave(g_acw[:, 0])
    bqkv = g_bqkv[:, 0].reshape(1, -1)
    fcw = [_interleave(g_fcw[:, l]) for l in range(2)]
    lng = jnp.transpose(g_lng, (1, 2, 0, 3)).reshape(2, 2, 1, D_MODEL)
    lnb = jnp.transpose(g_lnb, (1, 2, 0, 3)).reshape(2, 2, 1, D_MODEL)
    fcb = [ffn_conv_b[l:l + 1] for l in range(2)]
    ms = b_spatial_w[0]
    mst = jnp.swapaxes(ms, 1, 2)
    sbt = b_spatial_b[0].T

    q_up = up_shard[0].shape[0] // 4
    h0, (g_wout, g_wup0) = _matmul(xt, w_in, "nt", BF16, "mix_in", 1024, 1024, 1024,
                                   comm=_Comm(gather=[ab_w_out[0].astype(BF16), (up_shard[0], 0, q_up, None)]))
    w_out = g_wout.reshape(D_MODEL, D_MODEL)
    (cat, y0), (g_wup0,) = _mixer_mid_fwd(h0, acw, a_conv_b, a_norm_g, a_norm_b, b_norm_g, b_norm_b, ms, sbt, "mix_mid_fwd",
                                          comm=_Comm(gather=[(up_shard[0], q_up, 2 * q_up, g_wup0)]))
    (z1, x1), (g_wup0,) = _matmul_res_ln(cat, w_out, xt, lng[0, 0], lnb[0, 0], "mix_out_ln", 512,
                                         comm=_Comm(gather=[(up_shard[0], 3 * q_up, q_up, g_wup0)]))
    w_up0 = g_wup0.reshape(2 * D_FF, D_MODEL)
    hf0, (g_wdown0, g_wqkv) = _matmul(x1, w_up0, "nt", BF16, "ffn0_up", 2048, 1408, 1024,
                                      comm=_Comm(gather=[down_shard[0], qkv_shard]))
    w_down0 = g_wdown0.reshape(D_FF, D_MODEL)
    w_qkv = g_wqkv.reshape(Q_WIDTH + KV_WIDTH, D_MODEL)
    (u0, cg0, cv0), (g_wup1,) = _ffn_mid_fwd(hf0, fcw[0], fcb[0], "ffn0_mid_fwd",
                                             comm=_Comm(gather=[(up_shard[1], 0, 3 * q_up, None)]))
    (z2, x2), (g_wo, g_wup1) = _matmul_res_ln(
        u0, w_down0, z1, lng[0, 1], lnb[0, 1], "ffn0_down_ln", 512, prev=(lng[0, 0], lnb[0, 0]),
        comm=_Comm(gather=[c_w_o[0].astype(BF16), (up_shard[1], 3 * q_up, q_up, g_wup1)]))
    w_o = g_wo.reshape(D_MODEL, D_MODEL)
    w_up1 = g_wup1.reshape(2 * D_FF, D_MODEL)
    qkv = _matmul(x2, w_qkv, "nt", BF16, "att_qkv", 1024, 1280, 1024, bias=bqkv)
    att, (g_wdown1,) = _attn_fwd(qkv, c_sinks, "att_fwd", comm=_Comm(gather=[down_shard[1]]))
    w_down1 = g_wdown1.reshape(D_FF, D_MODEL)
    z3, x3 = _matmul_res_ln(att, w_o, z2, lng[1, 0], lnb[1, 0], "att_out_ln", 512, prev=(lng[0, 1], lnb[0, 1]))
    hf1 = _matmul(x3, w_up1, "nt", BF16, "ffn1_up", 2048, 1408, 1024)
    u1, cg1, cv1 = _ffn_mid_fwd(hf1, fcw[1], fcb[1], "ffn1_mid_fwd")

    dz4, dz4b, dg11, db11, loss_terms = _matmul_res_ln_loss(u1, w_down1, z3, lng[1, 1], lnb[1, 1], loss_target[0],
                                                      "ffn1_down_ln_loss", 512, prev=(lng[1, 0], lnb[1, 0]))
    dz3, dz3b, dg10, db10, d_wup1, d_fcw1, d_fcb1, (p_wdown1,) = _ffn_backward(
        dz4, dz4b, x3, z3, lng[1, 0], hf1, cg1, cv1, u1, w_up1, fcw[1], w_down1, 1)
    d_att = _matmul(dz3b, w_o, "nt", BF16, "att_dout", 1024, 1024, 1024)
    d_wo = _matmul(att, dz3b, "tn", BF16, "att_dwo", 1024, 1024, 2048)
    rows_up = d_wup1.shape[1]
    first = 3 * rows_up // 4
    (dq, dkv, dbq, dbkv, dsinks), (p_wup1a,) = _attn_bwd(qkv, d_att, c_sinks, "att_bwd",
                                                        comm=_Comm(exchange=[(d_wup1, 0, first)]))
    d_wqkv = jnp.concatenate([_matmul(dq, x2, "tn", BF16, "att_dwq", 1024, 1024, 2048),
                              _matmul(dkv, x2, "tn", BF16, "att_dwkv", KV_WIDTH, 1024, 1024)], axis=0)
    dz2, dz2b, dg01, db01 = _matmul_ln_bwd([(dq, 0), (dkv, Q_WIDTH)], w_qkv, z2, lng[0, 1], dz3, "att_dx_ln_bwd", 512)
    early = rows_up // 2
    dz1, dz1b, dg00, db00, d_wup0, d_fcw0, d_fcb0, (p_wdown0, p_wup1b, p_wqkv, p_wo, p_wup0a) = _ffn_backward(
        dz2, dz2b, x1, z1, lng[0, 0], hf0, cg0, cv0, u0, w_up0, fcw[0], w_down0, 0, exchange=[(d_wup1, first, rows_up - first)],
        exchange_late=[d_wqkv.reshape(N_DEV, -1, D_MODEL), d_wo.reshape(N_DEV, -1, D_MODEL)], own_rows=early)
    dcat = _matmul(dz1b, w_out, "nt", BF16, "mix_dcat", 1024, 1024, 1024)
    d_wout = _matmul(cat, dz1b, "tn", BF16, "mix_dwout", 1024, 1024, 2048)
    (dh0, d_acw, d_acb, d_ang, d_anb, d_bng, d_bnb, d_ms, d_sb), (p_wup0b, p_wout) = _mixer_mid_bwd(
        h0, y0, dcat, acw, a_norm_g, a_norm_b, b_norm_g, b_norm_b, ms, mst, sbt, "mix_mid_bwd",
        comm=_Comm(exchange=[(d_wup0, early, rows_up - early), d_wout.reshape(N_DEV, -1, D_MODEL)]))
    d_bqkv = jnp.concatenate([dbq, dbkv], axis=1)
    d_lng = jnp.stack([jnp.stack([dg00, dg01]), jnp.stack([dg10, dg11])])
    d_lnb = jnp.stack([jnp.stack([db00, db01]), jnp.stack([db10, db11])])
    small_full = [d_acb, d_ang, d_anb, d_bng, d_bnb, d_ms, d_sb, dsinks[:, :N_Q_HEADS], jnp.concatenate([d_fcb0, d_fcb1], axis=0),
                  d_acw, d_bqkv, jnp.stack([d_fcw0, d_fcw1]), d_lng, d_lnb, loss_terms]
    d_win, (g_small_grads,) = _matmul(dh0, xt, "tn", BF16, "mix_dwin", 1024, 1024, 2048, comm=_Comm(gather=[_pack(small_full)]))
    grad_x, (p_win,) = _matmul(dh0, w_in, "nn", F32, "mix_dx", 1024, 1024, 1024, res=dz1, res_scale=ALPHA,
                               comm=_Comm(exchange=[d_win.reshape(N_DEV, -1, D_MODEL)]))


    big = {}
    for nm, p, w, m, v, tr, transposed in [
            ("ab_w_in", [p_win], ab_w_in, m_ab_w_in, v_ab_w_in, 256, True),
            ("ab_w_out", [p_wout], ab_w_out, m_ab_w_out, v_ab_w_out, 128, False),
            ("c_w_qkv", [p_wqkv], c_w_qkv, m_c_w_qkv, v_c_w_qkv, 160, True), ("c_w_o", [p_wo], c_w_o, m_c_w_o, v_c_w_o, 128, False),
            ("ffn_w_up", [(p_wup0a, 0, early), (p_wup0b, early, rows_up - early), (p_wup1a, 0, first), (p_wup1b, first, rows_up - first)], ffn_w_up, m_ffn_w_up, v_ffn_w_up, 176, True),
            ("ffn_w_down", [p_wdown0, p_wdown1], ffn_w_down, m_ffn_w_down, v_ffn_w_down, 176, False)]:
        def two_d(a):
            a = jnp.swapaxes(a, 1, 2) if transposed else a
            return a.reshape(-1, a.shape[-1])

        def back(o):
            return jnp.swapaxes(o.reshape(w.shape[0], w.shape[2], w.shape[1]), 1, 2) if transposed else o.reshape(w.shape)

        outs = _adamw_big(p, two_d(w), two_d(m), two_d(v), "adamw_" + nm, tr)
        big[nm] = [back(o) for o in outs]

    *gs, loss_parts = _unpack(g_small_grads, [a.shape for a in small_full])
    loss = 0.5 / D_MODEL * jnp.sum(loss_parts)

    def my_shard(g, width):
        g = g.reshape(g.shape[:-1] + (N_DEV, width))
        return lax.dynamic_index_in_dim(g, me, axis=g.ndim - 2, keepdims=False)

    small_names = ["a_conv_b", "a_norm_g", "a_norm_b", "b_norm_g", "b_norm_b", "b_spatial_w", "b_spatial_b", "c_sinks", "ffn_conv_b",
                   "a_conv_w", "c_b_qkv", "ffn_conv_w", "ln_g", "ln_b"]
    small_w = [a_conv_b, a_norm_g, a_norm_b, b_norm_g, b_norm_b, b_spatial_w, b_spatial_b, c_sinks, ffn_conv_b,
               a_conv_w, c_b_qkv, ffn_conv_w, ln_g, ln_b]
    small_m = [m_a_conv_b, m_a_norm_g, m_a_norm_b, m_b_norm_g, m_b_norm_b, m_b_spatial_w, m_b_spatial_b, m_c_sinks, m_ffn_conv_b,
               m_a_conv_w, m_c_b_qkv, m_ffn_conv_w, m_ln_g, m_ln_b]
    small_v = [v_a_conv_b, v_a_norm_g, v_a_norm_b, v_b_norm_g, v_b_norm_b, v_b_spatial_w, v_b_spatial_b, v_c_sinks, v_ffn_conv_b,
               v_a_conv_w, v_c_b_qkv, v_ffn_conv_w, v_ln_g, v_ln_b]
    gs[9:] = [my_shard(g, w.shape[-1]) for g, w in zip(gs[9:], small_w[9:])]
    two_d = [(-1, w.shape[-1]) for w in small_w]
    outs = _adamw_small([g.reshape((N_DEV,) + w.reshape(s).shape) for g, w, s in zip(gs, small_w, two_d)],
                        [w.reshape(s) for w, s in zip(small_w, two_d)], [m.reshape(s) for m, s in zip(small_m, two_d)],
                        [v.reshape(s) for v, s in zip(small_v, two_d)], "adamw_small")
    small = {nm: [o.reshape(w.shape) for o in outs[4 * a:4 * a + 4]] for a, (nm, w) in enumerate(zip(small_names, small_w))}

    res = {**big, **small}
    order = ["ab_w_in", "a_conv_w", "a_conv_b", "a_norm_g", "a_norm_b", "b_norm_g", "b_norm_b", "b_spatial_w", "b_spatial_b", "ab_w_out",
             "c_w_qkv", "c_b_qkv", "c_sinks", "c_w_o", "ffn_w_up", "ffn_conv_w", "ffn_conv_b", "ffn_w_down", "ln_g", "ln_b"]
    return (loss, grad_x[None], *[res[nm][0] for nm in order], *[res[nm][1] for nm in order],
            *[res[nm][2] for nm in order], *[res[nm][3] for nm in order])
```

```python
import functools
import math

import jax
import jax.numpy as jnp
from jax import lax
from jax.experimental import pallas as pl
from jax.experimental.pallas import tpu as pltpu

F32 = jnp.float32
BF16 = jnp.bfloat16

N_DEV = 8
D_MODEL = 1024
A_WIDTH = 512
A_KERNEL = 31
B_GROUPS = 4
B_CHUNK = 128
HEAD_DIM = 64
N_Q_HEADS = 16
N_KV_HEADS = 2
ATT_BLOCK = 128
D_FF = 2816
FFN_KERNEL = 3
ALPHA = (2.0 * 2) ** 0.25
LN_EPS = 1e-5
GELU_K = math.sqrt(2.0 / math.pi)
GELU_C = 0.044715
ADAM_LR = 0.001
ADAM_B1 = 0.9
ADAM_B2 = 0.999
ADAM_EPS = 1e-08
ADAM_WD = 0.01
ADAM_STEP = 10
VMEM_LIMIT = 56 * 1024 * 1024
MESH_ID = pl.DeviceIdType.MESH


def _params(*sem):
    return pltpu.CompilerParams(dimension_semantics=sem, vmem_limit_bytes=VMEM_LIMIT)


def _gelu(x):
    t = jnp.tanh(GELU_K * x * (1.0 + GELU_C * x * x))
    return 0.5 * x * (1.0 + t)


def _gelu_and_grad(x):
    x2 = x * x
    t = jnp.tanh(GELU_K * x * (1.0 + GELU_C * x2))
    g = 0.5 * x * (1.0 + t)
    dg = 0.5 * (1.0 + t) + 0.5 * x * (1.0 - t * t) * (GELU_K * (1.0 + 3.0 * GELU_C * x2))
    return g, dg


def _sigmoid(x):
    return 1.0 / (1.0 + jnp.exp(-x))


def _ln_stats(z):
    mu = jnp.mean(z, axis=-1, keepdims=True)
    zc = z - mu
    var = jnp.mean(zc * zc, axis=-1, keepdims=True)
    r = lax.rsqrt(var + LN_EPS)
    return zc * r, r


def _ln_bwd_rows(dn, nh, r):
    return r * (dn - jnp.mean(dn, axis=-1, keepdims=True) - nh * jnp.mean(dn * nh, axis=-1, keepdims=True))


def _colsum(x):
    return jnp.sum(x, axis=0, keepdims=True)


def _dot(a, b, dims):
    return lax.dot_general(a.astype(BF16), b.astype(BF16), (dims, ((), ())), preferred_element_type=F32)


NN = ((1,), (0,))
NT = ((1,), (1,))
TN = ((0,), (0,))


ANY = pl.BlockSpec(memory_space=pl.ANY)
N_RELATIONS = N_DEV - 1


def _my_place():
    return lax.axis_index("x"), lax.axis_index("y"), lax.axis_index("c")


class _Comm:
    def __init__(self, gather=(), exchange=()):
        gather = [e if isinstance(e, tuple) else (e, 0, e.shape[0], None) for e in gather]
        exchange = [e if isinstance(e, tuple) else (e, 0, e.shape[1]) for e in exchange]
        self.arrs = [e[0] for e in gather] + [e[0] for e in exchange]
        self.n_gather = len(gather)
        self.n = len(self.arrs)
        self.rows = [pl.ds(lo, n) for _, lo, n, _ in gather] + [pl.ds(lo, n) for _, lo, n in exchange]
        self.into = {i: e[3] for i, e in enumerate(gather) if e[3] is not None}

    def out_shape(self):
        return [jax.ShapeDtypeStruct(((N_DEV,) + a.shape) if i < self.n_gather else a.shape, a.dtype)
                for i, a in enumerate(self.arrs)]

    def sems(self):
        return [pltpu.SemaphoreType.DMA((self.n, N_RELATIONS)), pltpu.SemaphoreType.DMA((self.n, N_RELATIONS)),
                pltpu.SemaphoreType.DMA((self.n,))]

    def _gather_copy(self, ins, outs, sems, a, k, place, to, from_input=False):
        px, py, pc = place
        block = outs[a].at[4 * px + 2 * py + pc, self.rows[a]]
        return pltpu.make_async_remote_copy(
            src_ref=ins[a].at[self.rows[a]] if from_input else block, dst_ref=block,
            send_sem=sems[0].at[a, k], recv_sem=sems[1].at[a, k], device_id=to, device_id_type=MESH_ID)

    def _exchange_copy(self, ins, outs, sems, a, k, landing=False):
        x, y, c = _my_place()
        me = 4 * x + 2 * y + c
        peer = (x ^ (k >> 2), y ^ ((k >> 1) & 1), c ^ (k & 1))
        return pltpu.make_async_remote_copy(
            src_ref=ins[a].at[me ^ k, self.rows[a]], dst_ref=outs[a].at[(me ^ k) if landing else me, self.rows[a]],
            send_sem=sems[0].at[a, k - 1], recv_sem=sems[1].at[a, k - 1], device_id=peer, device_id_type=MESH_ID)

    def _local_copy(self, ins, outs, sems, a):
        x, y, c = _my_place()
        me = 4 * x + 2 * y + c
        if a < self.n_gather:
            return pltpu.make_async_copy(ins[a].at[self.rows[a]], outs[a].at[me, self.rows[a]], sems[2].at[a])
        return pltpu.make_async_copy(ins[a].at[me, self.rows[a]], outs[a].at[me, self.rows[a]], sems[2].at[a])

    def _first_stage(self, ins, outs, sems, a):
        x, y, c = _my_place()
        me = (x, y, c)
        chips = [(1 - x, y), (x, 1 - y), (1 - x, 1 - y)]
        return ([self._gather_copy(ins, outs, sems, a, 0, me, (x, y, 1 - c), from_input=True)]
                + [self._gather_copy(ins, outs, sems, a, 1 + j, me, (*chip, c), from_input=True) for j, chip in enumerate(chips)])

    def start(self, ins, outs, sems):
        for a in range(self.n):
            self._local_copy(ins, outs, sems, a).start()
        for a in range(self.n_gather):
            for cp in self._first_stage(ins, outs, sems, a):
                cp.start()
        for k in range(1, N_DEV):
            for a in range(self.n_gather, self.n):
                self._exchange_copy(ins, outs, sems, a, k).start()

    def forward(self, ins, outs, sems):
        x, y, c = _my_place()
        me, sibling = (x, y, c), (x, y, 1 - c)
        for j, chip in enumerate([(1 - x, y), (x, 1 - y), (1 - x, 1 - y)]):
            for a in range(self.n_gather):
                self._gather_copy(ins, outs, sems, a, 1 + j, (*chip, c), me).wait_recv()
                self._gather_copy(ins, outs, sems, a, 4 + j, (*chip, c), sibling).start()

    def finish(self, ins, outs, sems):
        x, y, c = _my_place()
        me, sibling = (x, y, c), (x, y, 1 - c)
        chips = [(1 - x, y), (x, 1 - y), (1 - x, 1 - y)]
        passed = [self._gather_copy(ins, outs, sems, a, 4 + j, (*chip, c), sibling)
                  for j, chip in enumerate(chips) for a in range(self.n_gather)]
        for a in range(self.n_gather):
            self._gather_copy(ins, outs, sems, a, 0, sibling, me).wait_recv()
            for j, chip in enumerate(chips):
                self._gather_copy(ins, outs, sems, a, 4 + j, (*chip, 1 - c), me).wait_recv()
        for k in range(1, N_DEV):
            for a in range(self.n_gather, self.n):
                self._exchange_copy(ins, outs, sems, a, k, landing=True).wait_recv()
        for a in range(self.n_gather):
            for cp in self._first_stage(ins, outs, sems, a):
                cp.wait_send()
        for cp in passed:
            cp.wait_send()
        for k in range(1, N_DEV):
            for a in range(self.n_gather, self.n):
                self._exchange_copy(ins, outs, sems, a, k).wait_send()
        for a in range(self.n):
            self._local_copy(ins, outs, sems, a).wait()


def _comm_only(comm, name):
    assert not comm.into

    def body(*refs):
        ins, outs, sems = refs[:comm.n], refs[comm.n:2 * comm.n], refs[2 * comm.n:]
        comm.start(ins, outs, sems)
        comm.forward(ins, outs, sems)
        comm.finish(ins, outs, sems)

    return pl.pallas_call(body, name=name, in_specs=[ANY] * comm.n, out_specs=[ANY] * comm.n,
                          out_shape=comm.out_shape(), scratch_shapes=comm.sems())(*comm.arrs)


def _call(body, *, name, grid, in_specs, out_specs, out_shape, args, sem, scratch_shapes=(), comm=None):
    in_specs, out_specs, out_shape, scratch_shapes = list(in_specs), list(out_specs), list(out_shape), list(scratch_shapes)
    if comm is None:
        outs = pl.pallas_call(body, name=name, grid=grid, in_specs=in_specs, out_specs=out_specs, out_shape=out_shape,
                              scratch_shapes=scratch_shapes, compiler_params=_params(*sem))(*args)
        return list(outs), []
    n_in, n_out, n_scr, nc = len(in_specs), len(out_specs), len(scratch_shapes), comm.n
    completed = sorted(comm.into)

    def wrapped(*refs):
        ins, refs = refs[:n_in], refs[n_in:]
        c_in, refs = refs[:nc], refs[nc + len(completed):]
        outs, refs = refs[:n_out], refs[n_out:]
        c_out, refs = refs[:nc], refs[nc:]
        scr, sems = refs[:n_scr], refs[n_scr:]
        step = functools.reduce(lambda acc, ax: acc * grid[ax] + pl.program_id(ax), range(len(grid)), 0)
        steps = math.prod(grid)

        @pl.when(step == 0)
        def _():
            comm.start(c_in, c_out, sems)

        @pl.when(step == steps - 1)
        def _():
            comm.forward(c_in, c_out, sems)

        body(*ins, *outs, *scr)

        @pl.when(step == steps - 1)
        def _():
            comm.finish(c_in, c_out, sems)

    outs = pl.pallas_call(
        wrapped, name=name, grid=grid, in_specs=in_specs + [ANY] * (nc + len(completed)), out_specs=out_specs + [ANY] * nc,
        out_shape=out_shape + comm.out_shape(), scratch_shapes=scratch_shapes + comm.sems(),
        input_output_aliases={n_in + nc + pos: n_out + item for pos, item in enumerate(completed)},
        compiler_params=_params(*(["arbitrary"] * len(grid))))(*args, *comm.arrs, *[comm.into[item] for item in completed])
    return list(outs[:n_out]), list(outs[n_out:])


def _matmul(a, b, mode, out_dtype, name, tm, tn, tk, *, bias=None, res=None, res_scale=1.0, b_off=0, comm=None):
    tm = min(tm, a.shape[1] if mode == "tn" else a.shape[0])
    tk = min(tk, a.shape[0] if mode == "tn" else a.shape[1])
    if mode == "nn":
        (m, k), n = a.shape, b.shape[1]
        a_spec = pl.BlockSpec((tm, tk), lambda i, j, kk: (i, kk))
        b_spec = pl.BlockSpec((tk, tn), lambda i, j, kk: (kk + b_off, j))
        dims = NN
    elif mode == "nt":
        (m, k), n = a.shape, b.shape[0]
        a_spec = pl.BlockSpec((tm, tk), lambda i, j, kk: (i, kk))
        b_spec = pl.BlockSpec((tn, tk), lambda i, j, kk: (j, kk + b_off))
        dims = NT
    else:
        (k, m), n = a.shape, b.shape[1]
        a_spec = pl.BlockSpec((tk, tm), lambda i, j, kk: (kk, i))
        b_spec = pl.BlockSpec((tk, tn), lambda i, j, kk: (kk, j))
        dims = TN
    assert m % tm == 0 and n % tn == 0 and k % tk == 0, (name, m, n, k)
    nk = k // tk
    in_specs = [a_spec, b_spec]
    args = [a, b]
    if bias is not None:
        in_specs.append(pl.BlockSpec((1, tn), lambda i, j, kk: (0, j)))
        args.append(bias)
    if res is not None:
        in_specs.append(pl.BlockSpec((tm, tn), lambda i, j, kk: (i, j)))
        args.append(res)

    def finish(out, refs, o_ref):
        pos = 2
        if bias is not None:
            out = out + refs[pos][...]
            pos += 1
        if res is not None:
            out = out + res_scale * refs[pos][...].astype(F32)
        o_ref[...] = out.astype(out_dtype)

    def body_one_step(*refs):
        finish(_dot(refs[0][...], refs[1][...], dims), refs, refs[-1])

    def body(*refs):
        a_ref, b_ref = refs[0], refs[1]
        o_ref, acc = refs[-2], refs[-1]
        kk = pl.program_id(2)

        @pl.when(kk == 0)
        def _():
            acc[...] = jnp.zeros_like(acc)

        acc[...] += _dot(a_ref[...], b_ref[...], dims)

        @pl.when(kk == nk - 1)
        def _():
            finish(acc[...], refs, o_ref)

    (out,), moved = _call(
        body_one_step if nk == 1 else body, name=name, grid=(m // tm, n // tn, nk),
        in_specs=in_specs, out_specs=[pl.BlockSpec((tm, tn), lambda i, j, kk: (i, j))],
        out_shape=[jax.ShapeDtypeStruct((m, n), out_dtype)],
        scratch_shapes=[] if nk == 1 else [pltpu.VMEM((tm, tn), F32)],
        sem=("parallel", "parallel", "arbitrary"), args=args, comm=comm)
    return out if comm is None else (out, moved)


def _matmul_tn_pair(a0, a1, b, out_dtype, name, tm, tn, tk, comm=None):
    (k, m), n = a0.shape, b.shape[1]
    tk = min(tk, k)
    assert a1.shape == a0.shape and m % tm == 0 and n % tn == 0 and k % tk == 0, (name, m, n, k)
    mi, nk = m // tm, k // tk

    def body(a0_ref, a1_ref, b_ref, o_ref, acc):
        i, kk = pl.program_id(0), pl.program_id(2)

        @pl.when(kk == 0)
        def _():
            acc[...] = jnp.zeros_like(acc)

        @pl.when(i < mi)
        def _():
            acc[...] += _dot(a0_ref[...], b_ref[...], TN)

        @pl.when(i >= mi)
        def _():
            acc[...] += _dot(a1_ref[...], b_ref[...], TN)

        @pl.when(kk == nk - 1)
        def _():
            o_ref[...] = acc[...].astype(out_dtype)

    (out,), moved = _call(
        body, name=name, grid=(2 * mi, n // tn, nk),
        in_specs=[pl.BlockSpec((tk, tm), lambda i, j, kk: (jnp.where(i < mi, kk, nk - 1), jnp.minimum(i, mi - 1))),
                  pl.BlockSpec((tk, tm), lambda i, j, kk: (jnp.where(i >= mi, kk, 0), jnp.maximum(i - mi, 0))),
                  pl.BlockSpec((tk, tn), lambda i, j, kk: (kk, j))],
        out_specs=[pl.BlockSpec((tm, tn), lambda i, j, kk: (i, j))],
        out_shape=[jax.ShapeDtypeStruct((2 * m, n), out_dtype)],
        scratch_shapes=[pltpu.VMEM((tm, tn), F32)],
        sem=("parallel", "parallel", "arbitrary"), args=(a0, a1, b), comm=comm)
    return out if comm is None else (out, moved)


def _residual_input(x_ref, prev_refs):
    if not prev_refs:
        return x_ref[...]
    nh, _ = _ln_stats(x_ref[...])
    return nh * prev_refs[0][...] + prev_refs[1][...]


def _matmul_res_ln(a, b, x, g, beta, name, tm, prev=None, comm=None):
    t, k = a.shape
    d = b.shape[1]
    tm = min(tm, t)
    assert t % tm == 0
    n_prev = 0 if prev is None else 2

    def body(a_ref, b_ref, x_ref, g_ref, beta_ref, *rest):
        z_ref, xo_ref = rest[n_prev:]
        z = ALPHA * _residual_input(x_ref, rest[:n_prev]) + _dot(a_ref[...], b_ref[...], NN)
        nh, _ = _ln_stats(z)
        z_ref[...] = z
        xo_ref[...] = (nh * g_ref[...] + beta_ref[...]).astype(BF16)

    row = pl.BlockSpec((tm, d), lambda i: (i, 0))
    vec = pl.BlockSpec((1, d), lambda i: (0, 0))
    outs, moved = _call(
        body, name=name, grid=(t // tm,),
        in_specs=[pl.BlockSpec((tm, k), lambda i: (i, 0)), pl.BlockSpec((k, d), lambda i: (0, 0)), row, vec, vec] + [vec] * n_prev,
        out_specs=[row, row],
        out_shape=[jax.ShapeDtypeStruct((t, d), F32), jax.ShapeDtypeStruct((t, d), BF16)],
        sem=("parallel",), args=(a, b, x, g, beta, *(prev or ())), comm=comm)
    return outs if comm is None else (outs, moved)


def _matmul_ln_bwd(parts, b, z, g, dres, name, tm, comm=None):
    m = parts[0][0].shape[0]
    d = b.shape[1]
    tm = min(tm, m)
    n = len(parts)
    assert m % tm == 0 and all(row % a.shape[1] == 0 for a, row in parts)

    def body(*refs):
        z_ref, g_ref, dres_ref = refs[2 * n:2 * n + 3]
        dz_ref, dzb_ref, dg_ref, db_ref = refs[-4:]

        @pl.when(pl.program_id(0) == 0)
        def _():
            dg_ref[...] = jnp.zeros_like(dg_ref)
            db_ref[...] = jnp.zeros_like(db_ref)

        dy = ALPHA * dres_ref[...]
        for p in range(n):
            dy = dy + _dot(refs[p][...], refs[n + p][...], NN)
        nh, r = _ln_stats(z_ref[...])
        dg_ref[...] += _colsum(dy * nh)
        db_ref[...] += _colsum(dy)
        dz = _ln_bwd_rows(dy * g_ref[...], nh, r)
        dz_ref[...] = dz
        dzb_ref[...] = dz.astype(BF16)

    def b_spec(a, row):
        blk = row // a.shape[1]
        return pl.BlockSpec((a.shape[1], d), lambda i: (blk, 0))

    row = pl.BlockSpec((tm, d), lambda i: (i, 0))
    vec = pl.BlockSpec((1, d), lambda i: (0, 0))
    vshape = jax.ShapeDtypeStruct((1, d), F32)
    outs, moved = _call(
        body, name=name, grid=(m // tm,),
        in_specs=[pl.BlockSpec((tm, a.shape[1]), lambda i: (i, 0)) for a, _ in parts] + [b_spec(a, r_) for a, r_ in parts]
        + [row, vec, row],
        out_specs=[row, row, vec, vec],
        out_shape=[jax.ShapeDtypeStruct((m, d), F32), jax.ShapeDtypeStruct((m, d), BF16), vshape, vshape],
        sem=("arbitrary",), args=(*[a for a, _ in parts], *([b] * n), z, g, dres), comm=comm)
    return outs if comm is None else (outs, moved)


def _matmul_res_ln_loss(a, b, x, g, beta, target, name, tm, prev):
    t, k = a.shape
    d = b.shape[1]
    tm = min(tm, t)

    def body(a_ref, b_ref, x_ref, g_ref, beta_ref, t_ref, gp_ref, bp_ref, dz_ref, dzb_ref, dg_ref, db_ref, loss_ref):
        @pl.when(pl.program_id(0) == 0)
        def _():
            dg_ref[...] = jnp.zeros_like(dg_ref)
            db_ref[...] = jnp.zeros_like(db_ref)
            loss_ref[...] = jnp.zeros_like(loss_ref)

        nh, r = _ln_stats(ALPHA * _residual_input(x_ref, (gp_ref, bp_ref)) + _dot(a_ref[...], b_ref[...], NN))
        err = nh * g_ref[...] + beta_ref[...] - t_ref[...]
        loss_ref[...] += _colsum(err * err)
        dy = err * (1.0 / d)
        dg_ref[...] += _colsum(dy * nh)
        db_ref[...] += _colsum(dy)
        dz = _ln_bwd_rows(dy * g_ref[...], nh, r)
        dz_ref[...] = dz
        dzb_ref[...] = dz.astype(BF16)

    row = pl.BlockSpec((tm, d), lambda i: (i, 0))
    vec = pl.BlockSpec((1, d), lambda i: (0, 0))
    vshape = jax.ShapeDtypeStruct((1, d), F32)
    return pl.pallas_call(
        body, name=name, grid=(t // tm,),
        in_specs=[pl.BlockSpec((tm, k), lambda i: (i, 0)), pl.BlockSpec((k, d), lambda i: (0, 0)), row, vec, vec, row, vec, vec],
        out_specs=[row, row, vec, vec, vec],
        out_shape=[jax.ShapeDtypeStruct((t, d), F32), jax.ShapeDtypeStruct((t, d), BF16), vshape, vshape, vshape],
        compiler_params=_params("arbitrary"),
    )(a, b, x, g, beta, target, *prev)


FFN_HALO = 16
FFN_CHUNK = 256
LANES = 128
SUBLANES = 8


def _rows_up(e, start, rows):
    if start % SUBLANES == 0:
        return e[start:start + rows]
    return pltpu.roll(e, e.shape[0] - start, 0)[0:rows]


def _fold(x):
    return jnp.sum(x.reshape(x.shape[0] // SUBLANES, SUBLANES, x.shape[1]), axis=0)


def _ffn_mid_fwd(h, cw, cb, name, tm=1024, tc=1408, comm=None):
    t, f2 = h.shape
    tm = min(tm, t)
    f = f2 // 2
    nj, nt, hb = f // tc, t // tm, tm // FFN_HALO

    ch = min(FFN_CHUNK, tm)

    def body(hg, hgp, hv, hvp, cwg, cwv, cbg, cbv, u_ref, cg_ref, cv_ref):
        i = pl.program_id(1)
        o = FFN_HALO - FFN_KERNEL + 1
        for lg in range(tc // LANES):
            cols = slice(lg * LANES, (lg + 1) * LANES)
            wg, wv = [cwg[k:k + 1, cols] for k in range(FFN_KERNEL)], [cwv[k:k + 1, cols] for k in range(FFN_KERNEL)]
            bg, bv = cbg[:, cols], cbv[:, cols]

            def emit(base, eg, ev):
                cg = wg[0] * _rows_up(eg, o, ch) + wg[1] * _rows_up(eg, o + 1, ch) + wg[2] * _rows_up(eg, o + 2, ch) + bg
                cv = wv[0] * _rows_up(ev, o, ch) + wv[1] * _rows_up(ev, o + 1, ch) + wv[2] * _rows_up(ev, o + 2, ch) + bv
                u_ref[pl.ds(base, ch), cols] = (_gelu(cg) * cv).astype(BF16)
                cg_ref[pl.ds(base, ch), cols] = cg.astype(BF16)
                cv_ref[pl.ds(base, ch), cols] = cv.astype(BF16)

            def first(main, prev):
                return jnp.concatenate([jnp.where(i > 0, prev[:, cols].astype(F32), 0.0), main[0:ch, cols].astype(F32)], axis=0)

            def inner(c, carry):
                base = pl.multiple_of(c * ch, ch)
                emit(base, hg[pl.ds(base - FFN_HALO, ch + FFN_HALO), cols].astype(F32),
                     hv[pl.ds(base - FFN_HALO, ch + FFN_HALO), cols].astype(F32))
                return carry

            emit(0, first(hg, hgp), first(hv, hvp))
            if tm > ch:
                lax.fori_loop(1, tm // ch, inner, 0)

    def main_spec(off):
        return pl.BlockSpec((tm, tc), lambda j, i: (i, j + off))

    def prev_spec(off):
        return pl.BlockSpec((FFN_HALO, tc), lambda j, i: (jnp.maximum(i * hb - 1, 0), j + off))

    def par_spec(rows, off):
        return pl.BlockSpec((rows, tc), lambda j, i: (0, j + off))

    outs, moved = _call(
        body, name=name, grid=(nj, nt),
        in_specs=[main_spec(0), prev_spec(0), main_spec(nj), prev_spec(nj),
                  par_spec(FFN_KERNEL, 0), par_spec(FFN_KERNEL, nj), par_spec(1, 0), par_spec(1, nj)],
        out_specs=[pl.BlockSpec((tm, tc), lambda j, i: (i, j))] * 3,
        out_shape=[jax.ShapeDtypeStruct((t, f), BF16)] * 3,
        sem=("parallel", "arbitrary"), args=(h, h, h, h, cw, cw, cb, cb), comm=comm)
    return outs if comm is None else (outs, moved)


def _ffn_mid_bwd(h, cg, cv, du, cw, name, tm=1024, tc=1408, comm=None):
    t, f2 = h.shape
    tm = min(tm, t)
    f = f2 // 2
    nj, nt, hb = f // tc, t // tm, tm // FFN_HALO

    ch = min(FFN_CHUNK, tm)
    ahead = ch + SUBLANES
    n_ch = tm // ch

    def body(hg, hv, cg_ref, cgn_ref, cv_ref, cvn_ref, du_ref, dun_ref, cwg, cwv,
             dhg_ref, dhv_ref, dcwg_ref, dcwv_ref, dcbg_ref, dcbv_ref):
        i = pl.program_id(1)

        @pl.when(i == 0)
        def _():
            for ref in (dcwg_ref, dcwv_ref, dcbg_ref, dcbv_ref):
                ref[...] = jnp.zeros_like(ref)

        for lg in range(tc // LANES):
            cols = slice(lg * LANES, (lg + 1) * LANES)
            wg, wv = [cwg[k:k + 1, cols] for k in range(FFN_KERNEL)], [cwv[k:k + 1, cols] for k in range(FFN_KERNEL)]

            def emit(base, cg_e, cv_e, du_e, acc):
                cg_a, cv_a, du_a = cg_e[0:ahead], cv_e[0:ahead], du_e[0:ahead]
                gl, dgl = _gelu_and_grad(cg_a)

                def back(d, h_ref, w, dh_ref):
                    later = [d[0:ch], _rows_up(d, 1, ch), _rows_up(d, 2, ch)]
                    dh_ref[pl.ds(base, ch), cols] = (w[2] * later[0] + w[1] * later[1] + w[0] * later[2]).astype(BF16)
                    h_own = h_ref[pl.ds(base, ch), cols].astype(F32)
                    return [_fold(later[0])] + [_fold(later[FFN_KERNEL - 1 - k] * h_own) for k in range(FFN_KERNEL)]

                sums = back(du_a * cv_a * dgl, hg, wg, dhg_ref) + back(du_a * gl, hv, wv, dhv_ref)
                return tuple(a + s_ for a, s_ in zip(acc, sums))

            def inner(c, acc):
                base = pl.multiple_of(c * ch, ch)
                rows = pl.ds(base, ch + FFN_HALO)
                return emit(base, cg_ref[rows, cols].astype(F32), cv_ref[rows, cols].astype(F32), du_ref[rows, cols].astype(F32), acc)

            def last(acc):
                def rows(main, after):
                    return jnp.concatenate([main[tm - ch:tm, cols].astype(F32), after], axis=0)

                du_next = jnp.where(i < nt - 1, dun_ref[:, cols].astype(F32), 0.0)
                return emit(tm - ch, rows(cg_ref, cgn_ref[:, cols].astype(F32)), rows(cv_ref, cvn_ref[:, cols].astype(F32)),
                            rows(du_ref, du_next), acc)

            acc = (jnp.zeros((SUBLANES, LANES), F32),) * (2 * (1 + FFN_KERNEL))
            if n_ch > 1:
                acc = lax.fori_loop(0, n_ch - 1, inner, acc)
            acc = last(acc)
            dcbg_ref[:, cols] += _colsum(acc[0])
            dcbv_ref[:, cols] += _colsum(acc[1 + FFN_KERNEL])
            for k in range(FFN_KERNEL):
                dcwg_ref[k:k + 1, cols] += _colsum(acc[1 + k])
                dcwv_ref[k:k + 1, cols] += _colsum(acc[2 + FFN_KERNEL + k])

    last_blk = t // FFN_HALO - 1

    def main_spec(off):
        return pl.BlockSpec((tm, tc), lambda j, i: (i, j + off))

    def next_spec(off):
        return pl.BlockSpec((FFN_HALO, tc), lambda j, i: (jnp.minimum((i + 1) * hb, last_blk), j + off))

    def par_spec(rows, off):
        return pl.BlockSpec((rows, tc), lambda j, i: (0, j + off))

    out_tile = pl.BlockSpec((tm, tc), lambda j, i: (i, j))
    outs, moved = _call(
        body, name=name, grid=(nj, nt),
        in_specs=[main_spec(0), main_spec(nj), main_spec(0), next_spec(0), main_spec(0), next_spec(0), main_spec(0), next_spec(0),
                  par_spec(FFN_KERNEL, 0), par_spec(FFN_KERNEL, nj)],
        out_specs=[out_tile, out_tile, par_spec(FFN_KERNEL, 0), par_spec(FFN_KERNEL, 0), par_spec(1, 0), par_spec(1, 0)],
        out_shape=[jax.ShapeDtypeStruct((t, f), BF16), jax.ShapeDtypeStruct((t, f), BF16),
                   jax.ShapeDtypeStruct((FFN_KERNEL, f), F32), jax.ShapeDtypeStruct((FFN_KERNEL, f), F32),
                   jax.ShapeDtypeStruct((1, f), F32), jax.ShapeDtypeStruct((1, f), F32)],
        sem=("parallel", "arbitrary"), args=(h, h, cg, cg, cv, cv, du, du, cw, cw), comm=comm)
    return outs if comm is None else (outs, moved)


MIX_HALO = 32


def _glu(hh):
    return hh[:, 0:A_WIDTH] * _sigmoid(hh[:, A_WIDTH:2 * A_WIDTH])


def _fill_row_shifts(s):
    rows = s.shape[1] - SUBLANES
    for j in range(1, SUBLANES):
        s[j, 0:rows, :] = s[0, pl.ds(j, rows), :]


def _rows_from(s, start, rows):
    j = start % SUBLANES
    return s[j, start - j:start - j + rows, :]


def _tril_mask():
    return lax.broadcasted_iota(jnp.int32, (B_CHUNK, B_CHUNK), 0) >= lax.broadcasted_iota(jnp.int32, (B_CHUNK, B_CHUNK), 1)


def _spatial_mix(q, ms_ref, sbt_ref, tm):
    mask = _tril_mask()
    ws = [jnp.where(mask, ms_ref[g], 0.0).astype(BF16) for g in range(B_GROUPS)]
    qb = q.astype(BF16)
    rows = []
    for c in range(tm // B_CHUNK):
        cols = [_dot(ws[g], qb[c * B_CHUNK:(c + 1) * B_CHUNK, g * 128:(g + 1) * 128], NN) + sbt_ref[:, g:g + 1]
                for g in range(B_GROUPS)]
        rows.append(jnp.concatenate(cols, axis=1))
    return jnp.concatenate(rows, axis=0)


def _mixer_mid_fwd(h, cw, cb, ag, ab, bg, bb, ms, sbt, name, tm=256, comm=None):
    t = h.shape[0]
    nt, hb = t // tm, tm // MIX_HALO
    o = MIX_HALO - A_KERNEL + 1

    def body(h_ref, hp_ref, cw_ref, cb_ref, ag_ref, ab_ref, bg_ref, bb_ref, ms_ref, sbt_ref, cat_ref, y_ref, sp):
        i = pl.program_id(0)
        sp[0, 0:MIX_HALO, :] = jnp.where(i > 0, _glu(hp_ref[:, 0:2 * A_WIDTH].astype(F32)), 0.0)
        sp[0, MIX_HALO:, :] = _glu(h_ref[:, 0:2 * A_WIDTH].astype(F32))
        _fill_row_shifts(sp)
        y = jnp.zeros((tm, A_WIDTH), F32) + cb_ref[...]
        for k in range(A_KERNEL):
            y = y + cw_ref[k:k + 1, :] * _rows_from(sp, o + k, tm)
        y_ref[...] = y.astype(BF16)
        nh, _ = _ln_stats(y)
        ln = nh * ag_ref[...] + ab_ref[...]
        cat_ref[:, 0:A_WIDTH] = (ln * _sigmoid(ln)).astype(BF16)
        u = _gelu(h_ref[:, 1024:1536].astype(F32))
        nb, _ = _ln_stats(_gelu(h_ref[:, 1536:2048].astype(F32)))
        mixed = _spatial_mix(nb * bg_ref[...] + bb_ref[...], ms_ref, sbt_ref, tm)
        cat_ref[:, A_WIDTH:] = (u * mixed).astype(BF16)

    vec = pl.BlockSpec((1, A_WIDTH), lambda i: (0, 0))
    outs, moved = _call(
        body, name=name, grid=(nt,),
        in_specs=[pl.BlockSpec((tm, 2048), lambda i: (i, 0)),
                  pl.BlockSpec((MIX_HALO, 2048), lambda i: (jnp.maximum(i * hb - 1, 0), 0)),
                  pl.BlockSpec((A_KERNEL, A_WIDTH), lambda i: (0, 0)), vec, vec, vec, vec, vec,
                  pl.BlockSpec((B_GROUPS, B_CHUNK, B_CHUNK), lambda i: (0, 0, 0)),
                  pl.BlockSpec((B_CHUNK, B_GROUPS), lambda i: (0, 0))],
        out_specs=[pl.BlockSpec((tm, D_MODEL), lambda i: (i, 0)), pl.BlockSpec((tm, A_WIDTH), lambda i: (i, 0))],
        out_shape=[jax.ShapeDtypeStruct((t, D_MODEL), BF16), jax.ShapeDtypeStruct((t, A_WIDTH), BF16)],
        scratch_shapes=[pltpu.VMEM((SUBLANES, tm + MIX_HALO, A_WIDTH), F32)],
        sem=("parallel",), args=(h, h, cw, cb, ag, ab, bg, bb, ms, sbt), comm=comm)
    return outs if comm is None else (outs, moved)


def _mixer_mid_bwd(h, y, dcat, cw, ag, ab, bg, bb, ms, mst, sbt, name, tm=256, comm=None):
    t = h.shape[0]
    nt, hb = t // tm, tm // MIX_HALO
    r = tm + MIX_HALO
    nchunk = tm // B_CHUNK

    def body(h_ref, y_ref, yn_ref, dc_ref, dcn_ref, cw_ref, ag_ref, ab_ref, bg_ref, bb_ref, ms_ref, mst_ref, sbt_ref,
             dh_ref, dcw_ref, dcb_ref, dag_ref, dab_ref, dbg_ref, dbb_ref, dms_ref, dsb_ref, sdy, sbacc):
        i = pl.program_id(0)

        @pl.when(i == 0)
        def _():
            for ref in (dcw_ref, dcb_ref, dag_ref, dab_ref, dbg_ref, dbb_ref, dms_ref, dsb_ref, sbacc):
                ref[...] = jnp.zeros_like(ref)

        nh, rs = _ln_stats(jnp.concatenate([y_ref[...].astype(F32), yn_ref[...].astype(F32)], axis=0))
        ln = nh * ag_ref[...] + ab_ref[...]
        sg = _sigmoid(ln)
        dao = jnp.concatenate([dc_ref[:, 0:A_WIDTH].astype(F32),
                               jnp.where(i < nt - 1, dcn_ref[:, 0:A_WIDTH].astype(F32), 0.0)], axis=0)
        dln = dao * (sg * (1.0 + ln * (1.0 - sg)))
        dag_ref[...] += _colsum(dln[0:tm] * nh[0:tm])
        dab_ref[...] += _colsum(dln[0:tm])
        sdy[0] = _ln_bwd_rows(dln * ag_ref[...], nh, rs)
        _fill_row_shifts(sdy)
        dcb_ref[...] += _colsum(sdy[0, 0:tm, :])
        av = h_ref[:, 0:A_WIDTH].astype(F32)
        s = _sigmoid(h_ref[:, A_WIDTH:2 * A_WIDTH].astype(F32))
        p_own = av * s
        dp = jnp.zeros((tm, A_WIDTH), F32)
        for k in range(A_KERNEL):
            later = _rows_from(sdy, A_KERNEL - 1 - k, tm)
            dcw_ref[k:k + 1, :] += _colsum(later * p_own)
            dp = dp + cw_ref[k:k + 1, :] * later
        dh_ref[:, 0:A_WIDTH] = (dp * s).astype(BF16)
        dh_ref[:, A_WIDTH:2 * A_WIDTH] = (dp * av * s * (1.0 - s)).astype(BF16)

        u, dgu = _gelu_and_grad(h_ref[:, 1024:1536].astype(F32))
        w, dgw = _gelu_and_grad(h_ref[:, 1536:2048].astype(F32))
        nb, rb = _ln_stats(w)
        q = nb * bg_ref[...] + bb_ref[...]
        mixed = _spatial_mix(q, ms_ref, sbt_ref, tm)
        dbo = dc_ref[:, A_WIDTH:].astype(F32)
        dh_ref[:, 1024:1536] = (dbo * mixed * dgu).astype(BF16)
        dmx = dbo * u
        mask = _tril_mask()
        wst = [jnp.where(mask.T, mst_ref[g], 0.0).astype(BF16) for g in range(B_GROUPS)]
        qb = q.astype(BF16)
        dmb = dmx.astype(BF16)
        rows = []
        for c in range(nchunk):
            cols = []
            for g in range(B_GROUPS):
                rs_, cs_ = slice(c * B_CHUNK, (c + 1) * B_CHUNK), slice(g * 128, (g + 1) * 128)
                sbacc[g] += dmx[rs_, cs_]
                dms_ref[g] += _dot(dmb[rs_, cs_], qb[rs_, cs_], NT)
                cols.append(_dot(wst[g], dmb[rs_, cs_], NN))
            rows.append(jnp.concatenate(cols, axis=1))
        dq = jnp.concatenate(rows, axis=0)
        dbg_ref[...] += _colsum(dq * nb)
        dbb_ref[...] += _colsum(dq)
        dh_ref[:, 1536:2048] = (_ln_bwd_rows(dq * bg_ref[...], nb, rb) * dgw).astype(BF16)

        @pl.when(i == nt - 1)
        def _():
            for g in range(B_GROUPS):
                dms_ref[g] = jnp.where(mask, dms_ref[g], 0.0)
                dsb_ref[g] = jnp.sum(sbacc[g], axis=1, keepdims=True)

    last_blk = t // MIX_HALO - 1
    vec = pl.BlockSpec((1, A_WIDTH), lambda i: (0, 0))
    mat = pl.BlockSpec((B_GROUPS, B_CHUNK, B_CHUNK), lambda i: (0, 0, 0))
    taps = pl.BlockSpec((A_KERNEL, A_WIDTH), lambda i: (0, 0))

    def halo(width):
        return pl.BlockSpec((MIX_HALO, width), lambda i: (jnp.minimum((i + 1) * hb, last_blk), 0))

    vshape = jax.ShapeDtypeStruct((1, A_WIDTH), F32)
    outs, moved = _call(
        body, name=name, grid=(nt,),
        in_specs=[pl.BlockSpec((tm, 2048), lambda i: (i, 0)), pl.BlockSpec((tm, A_WIDTH), lambda i: (i, 0)), halo(A_WIDTH),
                  pl.BlockSpec((tm, D_MODEL), lambda i: (i, 0)), halo(D_MODEL),
                  taps, vec, vec, vec, vec, mat, mat, pl.BlockSpec((B_CHUNK, B_GROUPS), lambda i: (0, 0))],
        out_specs=[pl.BlockSpec((tm, 2048), lambda i: (i, 0)), taps, vec, vec, vec, vec, vec, mat,
                   pl.BlockSpec((B_GROUPS, B_CHUNK, 1), lambda i: (0, 0, 0))],
        out_shape=[jax.ShapeDtypeStruct((t, 2048), BF16), jax.ShapeDtypeStruct((A_KERNEL, A_WIDTH), F32),
                   vshape, vshape, vshape, vshape, vshape,
                   jax.ShapeDtypeStruct((B_GROUPS, B_CHUNK, B_CHUNK), F32), jax.ShapeDtypeStruct((B_GROUPS, B_CHUNK, 1), F32)],
        scratch_shapes=[pltpu.VMEM((SUBLANES, r, A_WIDTH), F32), pltpu.VMEM((B_GROUPS, B_CHUNK, B_CHUNK), F32)],
        sem=("arbitrary",), args=(h, y, y, dcat, dcat, cw, ag, ab, bg, bb, ms, mst, sbt), comm=comm)
    return outs if comm is None else (outs, moved)


Q_WIDTH = N_Q_HEADS * HEAD_DIM
KV_WIDTH = 2 * N_KV_HEADS * HEAD_DIM
PAIRS_PER_KV = N_Q_HEADS // N_KV_HEADS // 2
ATT_SCALE = 1.0 / math.sqrt(HEAD_DIM)


def _dup_heads(pair_cols, kv_head):
    lane = lax.broadcasted_iota(jnp.int32, pair_cols.shape, 1)
    rolled = pltpu.roll(pair_cols, HEAD_DIM, 1)
    first = lane < HEAD_DIM
    return jnp.where(first, pair_cols, rolled) if kv_head == 0 else jnp.where(first, rolled, pair_cols)


HEADS_PER_KV = N_Q_HEADS // N_KV_HEADS


def _stack_heads(ref, kh):
    lane = lax.broadcasted_iota(jnp.int32, (ATT_BLOCK, 128), 1)
    rows = []
    for pr in range(PAIRS_PER_KV):
        c0 = (kh * PAIRS_PER_KV + pr) * 128
        pair = ref[:, c0:c0 + 128]
        rows += [jnp.where(lane < HEAD_DIM, pair, jnp.zeros_like(pair)), jnp.where(lane < HEAD_DIM, jnp.zeros_like(pair), pair)]
    return jnp.concatenate(rows, axis=0)


def _unstack_heads(stacked, kh, write):
    lane = lax.broadcasted_iota(jnp.int32, (ATT_BLOCK, 128), 1)
    for pr in range(PAIRS_PER_KV):
        first = stacked[(2 * pr) * ATT_BLOCK:(2 * pr + 1) * ATT_BLOCK]
        second = stacked[(2 * pr + 1) * ATT_BLOCK:(2 * pr + 2) * ATT_BLOCK]
        write((kh * PAIRS_PER_KV + pr) * 128, jnp.where(lane < HEAD_DIM, first, second))


def _sink_row(sink_ref, kh):
    return jnp.concatenate([jnp.full((1, ATT_BLOCK), sink_ref[0, kh * HEADS_PER_KV + h], F32) for h in range(HEADS_PER_KV)], axis=1)


def _att_window_bias():
    sj = lax.broadcasted_iota(jnp.int32, (2 * ATT_BLOCK, HEADS_PER_KV * ATT_BLOCK), 0)
    qi = lax.broadcasted_iota(jnp.int32, (2 * ATT_BLOCK, HEADS_PER_KV * ATT_BLOCK), 1) & (ATT_BLOCK - 1)
    diff = qi + ATT_BLOCK - sj
    return jnp.where((diff >= 0) & (diff < ATT_BLOCK), 0.0, -jnp.inf)


def _att_probs_t(q_all, k2, bias_ref, n, sink):
    st = _dot(k2, q_all, NT) * ATT_SCALE + bias_ref[...]
    st = jnp.concatenate([jnp.where(n > 0, st[0:ATT_BLOCK], -jnp.inf), st[ATT_BLOCK:]], axis=0)
    m = jnp.maximum(jnp.max(st, axis=0, keepdims=True), sink)
    e = jnp.exp(st - m)
    es = jnp.exp(sink - m)
    inv = 1.0 / (jnp.sum(e, axis=0, keepdims=True) + es)
    return e * inv, es * inv


def _attn_fwd(qkv, sinks, name, comm=None):
    t = qkv.shape[0]
    nb = t // ATT_BLOCK
    kvb = Q_WIDTH // KV_WIDTH

    def body(sink_ref, q_ref, kv_ref, kvp_ref, o_ref, bias):
        n = pl.program_id(0)

        @pl.when(n == 0)
        def _():
            bias[...] = _att_window_bias()

        kv = jnp.concatenate([kvp_ref[...], kv_ref[...]], axis=0).astype(F32)

        def write(c0, pair):
            o_ref[:, c0:c0 + 128] = pair.astype(BF16)

        for kh in range(N_KV_HEADS):
            k2 = _dup_heads(kv[:, 0:128], kh).astype(BF16)
            v2 = _dup_heads(kv[:, 128:256], kh).astype(BF16)
            pt, _ = _att_probs_t(_stack_heads(q_ref, kh), k2, bias, n, _sink_row(sink_ref, kh))
            _unstack_heads(_dot(v2, pt, TN).T, kh, write)

    (out,), moved = _call(
        body, name=name, grid=(nb,),
        in_specs=[pl.BlockSpec(memory_space=pltpu.SMEM),
                  pl.BlockSpec((ATT_BLOCK, Q_WIDTH), lambda n: (n, 0)),
                  pl.BlockSpec((ATT_BLOCK, KV_WIDTH), lambda n: (n, kvb)),
                  pl.BlockSpec((ATT_BLOCK, KV_WIDTH), lambda n: (jnp.maximum(n - 1, 0), kvb))],
        out_specs=[pl.BlockSpec((ATT_BLOCK, Q_WIDTH), lambda n: (n, 0))],
        out_shape=[jax.ShapeDtypeStruct((t, Q_WIDTH), BF16)],
        scratch_shapes=[pltpu.VMEM((2 * ATT_BLOCK, HEADS_PER_KV * ATT_BLOCK), F32)],
        sem=("arbitrary",), args=(sinks, qkv, qkv, qkv), comm=comm)
    return out if comm is None else (out, moved)


def _attn_bwd(qkv, d_o, sinks, name, comm=None):
    t = qkv.shape[0]
    nb = t // ATT_BLOCK
    kvb = Q_WIDTH // KV_WIDTH

    def body(sink_ref, q_ref, kv_ref, kvp_ref, do_ref, dq_ref, dkv_ref, dbq_ref, dbkv_ref, dsink_ref, carry, bias):
        n = pl.program_id(0)

        @pl.when(n == 0)
        def _():
            for ref in (dbq_ref, dbkv_ref, dsink_ref, carry):
                ref[...] = jnp.zeros_like(ref)
            dkv_ref[...] = jnp.zeros_like(dkv_ref)
            bias[...] = _att_window_bias()

        @pl.when(n < nb)
        def _():
            kv = jnp.concatenate([kvp_ref[...], kv_ref[...]], axis=0).astype(F32)
            lane2 = lax.broadcasted_iota(jnp.int32, (2 * ATT_BLOCK, 128), 1)
            sink_lane = lax.broadcasted_iota(jnp.int32, (1, 128), 1)
            dsink = jnp.zeros((1, 128), F32)
            dk_parts, dv_parts = [], []

            def write(c0, pair):
                dbq_ref[:, c0:c0 + 128] += _colsum(pair)
                dq_ref[:, c0:c0 + 128] = pair.astype(BF16)

            for kh in range(N_KV_HEADS):
                k2 = _dup_heads(kv[:, 0:128], kh).astype(BF16)
                v2 = _dup_heads(kv[:, 128:256], kh).astype(BF16)
                q_all = _stack_heads(q_ref, kh)
                do_all = _stack_heads(do_ref, kh)
                pt, ps = _att_probs_t(q_all, k2, bias, n, _sink_row(sink_ref, kh))
                dpt = _dot(v2, do_all, NT)
                delta = jnp.sum(pt * dpt, axis=0, keepdims=True)
                dst = pt * (dpt - delta) * ATT_SCALE
                psd = ps * delta
                for h in range(HEADS_PER_KV):
                    dsink = dsink + jnp.where(sink_lane == kh * HEADS_PER_KV + h,
                                              -jnp.sum(psd[:, h * ATT_BLOCK:(h + 1) * ATT_BLOCK]), 0.0)
                _unstack_heads(_dot(k2, dst, TN).T, kh, write)
                dk_acc = _dot(dst, q_all, NN)
                dv_acc = _dot(pt, do_all, NN)
                dk_parts.append(dk_acc + pltpu.roll(dk_acc, HEAD_DIM, 1))
                dv_parts.append(dv_acc + pltpu.roll(dv_acc, HEAD_DIM, 1))
            dk = jnp.where(lane2 < HEAD_DIM, dk_parts[0], dk_parts[1])
            dv = jnp.where(lane2 < HEAD_DIM, dv_parts[0], dv_parts[1])
            dkv_new = jnp.concatenate([dk, dv], axis=1)
            done = carry[...] + dkv_new[0:ATT_BLOCK]

            @pl.when(n > 0)
            def _():
                dkv_ref[...] = done.astype(BF16)
                dbkv_ref[...] += _colsum(done)

            carry[...] = dkv_new[ATT_BLOCK:]
            dsink_ref[...] += dsink

        @pl.when(n == nb)
        def _():
            dkv_ref[...] = carry[...].astype(BF16)
            dbkv_ref[...] += _colsum(carry[...])

    def clamp(n):
        return jnp.minimum(n, nb - 1)

    outs, moved = _call(
        body, name=name, grid=(nb + 1,),
        in_specs=[pl.BlockSpec(memory_space=pltpu.SMEM),
                  pl.BlockSpec((ATT_BLOCK, Q_WIDTH), lambda n: (clamp(n), 0)),
                  pl.BlockSpec((ATT_BLOCK, KV_WIDTH), lambda n: (clamp(n), kvb)),
                  pl.BlockSpec((ATT_BLOCK, KV_WIDTH), lambda n: (jnp.maximum(clamp(n) - 1, 0), kvb)),
                  pl.BlockSpec((ATT_BLOCK, Q_WIDTH), lambda n: (clamp(n), 0))],
        out_specs=[pl.BlockSpec((ATT_BLOCK, Q_WIDTH), lambda n: (clamp(n), 0)),
                   pl.BlockSpec((ATT_BLOCK, KV_WIDTH), lambda n: (jnp.maximum(n - 1, 0), 0)),
                   pl.BlockSpec((1, Q_WIDTH), lambda n: (0, 0)),
                   pl.BlockSpec((1, KV_WIDTH), lambda n: (0, 0)),
                   pl.BlockSpec((1, 128), lambda n: (0, 0))],
        out_shape=[jax.ShapeDtypeStruct((t, Q_WIDTH), BF16), jax.ShapeDtypeStruct((t, KV_WIDTH), BF16),
                   jax.ShapeDtypeStruct((1, Q_WIDTH), F32), jax.ShapeDtypeStruct((1, KV_WIDTH), F32),
                   jax.ShapeDtypeStruct((1, 128), F32)],
        scratch_shapes=[pltpu.VMEM((ATT_BLOCK, KV_WIDTH), F32), pltpu.VMEM((2 * ATT_BLOCK, HEADS_PER_KV * ATT_BLOCK), F32)],
        sem=("arbitrary",), args=(sinks, qkv, qkv, qkv, d_o), comm=comm)
    return outs if comm is None else (outs, moved)


def _adamw_math(g, w, m, v):
    m = ADAM_B1 * m + (1.0 - ADAM_B1) * g
    v = ADAM_B2 * v + (1.0 - ADAM_B2) * (g * g)
    m_hat = m / (1.0 - ADAM_B1 ** ADAM_STEP)
    v_hat = v / (1.0 - ADAM_B2 ** ADAM_STEP)
    delta = -ADAM_LR * (m_hat / (jnp.sqrt(v_hat) + ADAM_EPS) + ADAM_WD * w)
    return delta, m, v


def _sum_partials(p_ref):
    g = p_ref[0].astype(F32)
    for s in range(1, N_DEV):
        g = g + p_ref[s].astype(F32)
    return g


def _adamw_big(parts, w, m, v, name, tr):
    r, c = w.shape
    parts = [p if isinstance(p, tuple) else (p, 0, p.shape[1]) for p in parts]
    tiles = [rows // tr for _, _, rows in parts]
    starts = [sum(tiles[:l]) for l in range(len(parts))]
    assert all(lo % tr == 0 and rows % tr == 0 for _, lo, rows in parts) and sum(tiles) * tr == r

    def body(*refs):
        p_refs, (w_ref, m_ref, v_ref, g_out, d_out, m_out, v_out) = refs[:len(parts)], refs[len(parts):]
        i = pl.program_id(0)
        for l, p_ref in enumerate(p_refs):
            @pl.when((i >= starts[l]) & (i < starts[l] + tiles[l]))
            def _():
                g = _sum_partials(p_ref)
                g_out[...] = g
                d_out[...], m_out[...], v_out[...] = _adamw_math(g, w_ref[...], m_ref[...], v_ref[...])

    def part_spec(l):
        return pl.BlockSpec((N_DEV, tr, c), lambda i: (0, jnp.clip(i - starts[l], 0, tiles[l] - 1) + parts[l][1] // tr, 0))

    tile = pl.BlockSpec((tr, c), lambda i: (i, 0))
    shape = jax.ShapeDtypeStruct((r, c), F32)
    return pl.pallas_call(
        body, name=name, grid=(r // tr,),
        in_specs=[part_spec(l) for l in range(len(parts))] + [tile, tile, tile],
        out_specs=[tile] * 4, out_shape=[shape] * 4,
        compiler_params=_params("parallel"),
    )(*[p[0] for p in parts], w, m, v)


def _adamw_small(parts, ws, ms, vs, name):
    n = len(ws)

    def body(*refs):
        ins, outs = refs[:4 * n], refs[4 * n:]
        for a in range(n):
            g = _sum_partials(ins[a])
            outs[4 * a][...] = g
            outs[4 * a + 1][...], outs[4 * a + 2][...], outs[4 * a + 3][...] = _adamw_math(
                g, ins[n + a][...], ins[2 * n + a][...], ins[3 * n + a][...])

    out_shape = []
    for w in ws:
        out_shape += [jax.ShapeDtypeStruct(w.shape, F32)] * 4
    return pl.pallas_call(body, name=name, out_shape=out_shape, compiler_params=_params())(*parts, *ws, *ms, *vs)


PACK_LANES = 128
PACK_ROWS = 8


def _pack(arrs):
    flat = jnp.concatenate([a.reshape(-1).astype(F32) for a in arrs])
    unit = PACK_LANES * PACK_ROWS
    total = -(-flat.shape[0] // unit) * unit
    return jnp.pad(flat, (0, total - flat.shape[0])).reshape(-1, PACK_LANES)


def _unpack(buf, shapes):
    flat = buf.reshape(N_DEV, -1)
    out, pos = [], 0
    for s in shapes:
        size = math.prod(s)
        out.append(flat[:, pos:pos + size].reshape((N_DEV,) + tuple(s)))
        pos += size
    return out


def _interleave(g):
    return jnp.transpose(g, (1, 0, 2)).reshape(g.shape[1], -1)


def _ffn_backward(dz, dzb, x_in, z_in, g_in, h, cg, cv, u, w_up_t, cw, w_down, tag, exchange=(), exchange_late=(), own_rows=0):
    du = _matmul(dzb, w_down, "nt", BF16, f"ffn{tag}_du", 2048, 1408, 1024)
    d_w_down = _matmul(u, dzb, "tn", BF16, f"ffn{tag}_dwdown", 1408, 1024, 2048)
    (dhg, dhv, dcwg, dcwv, dcbg, dcbv), moved = _ffn_mid_bwd(
        h, cg, cv, du, cw, f"ffn{tag}_mid_bwd", comm=_Comm(exchange=[d_w_down.reshape(N_DEV, -1, D_MODEL), *exchange]))
    d_w_up_t = _matmul_tn_pair(dhg, dhv, x_in, BF16, f"ffn{tag}_dwup", 1408, 1024, 2048,
                               comm=_Comm(exchange=exchange_late) if exchange_late else None)
    if exchange_late:
        d_w_up_t, late = d_w_up_t
        moved = moved + late
    d_up_blocks = d_w_up_t.reshape(N_DEV, -1, D_MODEL)
    outs = _matmul_ln_bwd([(dhg, 0), (dhv, D_FF)], w_up_t, z_in, g_in, dz, f"ffn{tag}_dx_ln_bwd", 256,
                          comm=_Comm(exchange=[(d_up_blocks, 0, own_rows)]) if own_rows else None)
    (dz_in, dzb_in, dg_in, db_in), own = outs if own_rows else (outs, [])
    moved = moved + own
    return (dz_in, dzb_in, dg_in, db_in, d_up_blocks,
            jnp.concatenate([dcwg, dcwv], axis=1), jnp.concatenate([dcbg, dcbv], axis=1), moved)


def kernel(x, ab_w_in, a_conv_w, a_conv_b, a_norm_g, a_norm_b, b_norm_g, b_norm_b, b_spatial_w, b_spatial_b, ab_w_out, c_w_qkv, c_b_qkv, c_sinks, c_w_o, ffn_w_up, ffn_conv_w, ffn_conv_b, ffn_w_down, ln_g, ln_b, loss_target, m_ab_w_in, m_a_conv_w, m_a_conv_b, m_a_norm_g, m_a_norm_b, m_b_norm_g, m_b_norm_b, m_b_spatial_w, m_b_spatial_b, m_ab_w_out, m_c_w_qkv, m_c_b_qkv, m_c_sinks, m_c_w_o, m_ffn_w_up, m_ffn_conv_w, m_ffn_conv_b, m_ffn_w_down, m_ln_g, m_ln_b, v_ab_w_in, v_a_conv_w, v_a_conv_b, v_a_norm_g, v_a_norm_b, v_b_norm_g, v_b_norm_b, v_b_spatial_w, v_b_spatial_b, v_ab_w_out, v_c_w_qkv, v_c_b_qkv, v_c_sinks, v_c_w_o, v_ffn_w_up, v_ffn_conv_w, v_ffn_conv_b, v_ffn_w_down, v_ln_g, v_ln_b):
    me = 4 * lax.axis_index("x") + 2 * lax.axis_index("y") + lax.axis_index("c")
    xt = x[0]
    t = xt.shape[0]

    small_shard_shapes = [a_conv_w.shape, c_b_qkv.shape, ffn_conv_w.shape, ln_g.shape, ln_b.shape]
    up_shard = [jnp.swapaxes(ffn_w_up[l], 0, 1).astype(BF16) for l in range(2)]
    qkv_shard = jnp.swapaxes(c_w_qkv[0], 0, 1).astype(BF16)
    down_shard = [ffn_w_down[l].astype(BF16) for l in range(2)]
    g_win, g_small = _comm_only(
        _Comm(gather=[jnp.swapaxes(ab_w_in[0], 0, 1).astype(BF16), _pack([a_conv_w, c_b_qkv, ffn_conv_w, ln_g, ln_b])]),
        "gather_first")
    w_in = g_win.reshape(-1, D_MODEL)
    g_acw, g_bqkv, g_fcw, g_lng, g_lnb = _unpack(g_small, small_shard_shapes)
    acw = _interleave(g_acw[:, 0])
    bqkv = g_bqkv[:, 0].reshape(1, -1)
    fcw = [_interleave(g_fcw[:, l]) for l in range(2)]
    lng = jnp.transpose(g_lng, (1, 2, 0, 3)).reshape(2, 2, 1, D_MODEL)
    lnb = jnp.transpose(g_lnb, (1, 2, 0, 3)).reshape(2, 2, 1, D_MODEL)
    fcb = [ffn_conv_b[l:l + 1] for l in range(2)]
    ms = b_spatial_w[0]
    mst = jnp.swapaxes(ms, 1, 2)
    sbt = b_spatial_b[0].T

    q_up = up_shard[0].shape[0] // 4
    h0, (g_wout, g_wup0) = _matmul(xt, w_in, "nt", BF16, "mix_in", 1024, 1024, 1024,
                                   comm=_Comm(gather=[ab_w_out[0].astype(BF16), (up_shard[0], 0, q_up, None)]))
    w_out = g_wout.reshape(D_MODEL, D_MODEL)
    (cat, y0), (g_wup0,) = _mixer_mid_fwd(h0, acw, a_conv_b, a_norm_g, a_norm_b, b_norm_g, b_norm_b, ms, sbt, "mix_mid_fwd",
                                          comm=_Comm(gather=[(up_shard[0], q_up, 2 * q_up, g_wup0)]))
    (z1, x1), (g_wup0,) = _matmul_res_ln(cat, w_out, xt, lng[0, 0], lnb[0, 0], "mix_out_ln", 512,
                                         comm=_Comm(gather=[(up_shard[0], 3 * q_up, q_up, g_wup0)]))
    w_up0 = g_wup0.reshape(2 * D_FF, D_MODEL)
    hf0, (g_wdown0, g_wqkv) = _matmul(x1, w_up0, "nt", BF16, "ffn0_up", 2048, 1408, 1024,
                                      comm=_Comm(gather=[down_shard[0], qkv_shard]))
    w_down0 = g_wdown0.reshape(D_FF, D_MODEL)
    w_qkv = g_wqkv.reshape(Q_WIDTH + KV_WIDTH, D_MODEL)
    (u0, cg0, cv0), (g_wup1,) = _ffn_mid_fwd(hf0, fcw[0], fcb[0], "ffn0_mid_fwd",
                                             comm=_Comm(gather=[(up_shard[1], 0, 3 * q_up, None)]))
    (z2, x2), (g_wo, g_wup1) = _matmul_res_ln(
        u0, w_down0, z1, lng[0, 1], lnb[0, 1], "ffn0_down_ln", 512, prev=(lng[0, 0], lnb[0, 0]),
        comm=_Comm(gather=[c_w_o[0].astype(BF16), (up_shard[1], 3 * q_up, q_up, g_wup1)]))
    w_o = g_wo.reshape(D_MODEL, D_MODEL)
    w_up1 = g_wup1.reshape(2 * D_FF, D_MODEL)
    qkv = _matmul(x2, w_qkv, "nt", BF16, "att_qkv", 1024, 1280, 1024, bias=bqkv)
    att, (g_wdown1,) = _attn_fwd(qkv, c_sinks, "att_fwd", comm=_Comm(gather=[down_shard[1]]))
    w_down1 = g_wdown1.reshape(D_FF, D_MODEL)
    z3, x3 = _matmul_res_ln(att, w_o, z2, lng[1, 0], lnb[1, 0], "att_out_ln", 512, prev=(lng[0, 1], lnb[0, 1]))
    hf1 = _matmul(x3, w_up1, "nt", BF16, "ffn1_up", 2048, 1408, 1024)
    u1, cg1, cv1 = _ffn_mid_fwd(hf1, fcw[1], fcb[1], "ffn1_mid_fwd")

    dz4, dz4b, dg11, db11, loss_terms = _matmul_res_ln_loss(u1, w_down1, z3, lng[1, 1], lnb[1, 1], loss_target[0],
                                                      "ffn1_down_ln_loss", 512, prev=(lng[1, 0], lnb[1, 0]))
    dz3, dz3b, dg10, db10, d_wup1, d_fcw1, d_fcb1, (p_wdown1,) = _ffn_backward(
        dz4, dz4b, x3, z3, lng[1, 0], hf1, cg1, cv1, u1, w_up1, fcw[1], w_down1, 1)
    d_att = _matmul(dz3b, w_o, "nt", BF16, "att_dout", 1024, 1024, 1024)
    d_wo = _matmul(att, dz3b, "tn", BF16, "att_dwo", 1024, 1024, 2048)
    rows_up = d_wup1.shape[1]
    first = 3 * rows_up // 4
    (dq, dkv, dbq, dbkv, dsinks), (p_wup1a,) = _attn_bwd(qkv, d_att, c_sinks, "att_bwd",
                                                        comm=_Comm(exchange=[(d_wup1, 0, first)]))
    d_wqkv = jnp.concatenate([_matmul(dq, x2, "tn", BF16, "att_dwq", 1024, 1024, 2048),
                              _matmul(dkv, x2, "tn", BF16, "att_dwkv", KV_WIDTH, 1024, 1024)], axis=0)
    dz2, dz2b, dg01, db01 = _matmul_ln_bwd([(dq, 0), (dkv, Q_WIDTH)], w_qkv, z2, lng[0, 1], dz3, "att_dx_ln_bwd", 512)
    early = rows_up // 2
    dz1, dz1b, dg00, db00, d_wup0, d_fcw0, d_fcb0, (p_wdown0, p_wup1b, p_wqkv, p_wo, p_wup0a) = _ffn_backward(
        dz2, dz2b, x1, z1, lng[0, 0], hf0, cg0, cv0, u0, w_up0, fcw[0], w_down0, 0, exchange=[(d_wup1, first, rows_up - first)],
        exchange_late=[d_wqkv.reshape(N_DEV, -1, D_MODEL), d_wo.reshape(N_DEV, -1, D_MODEL)], own_rows=early)
    dcat = _matmul(dz1b, w_out, "nt", BF16, "mix_dcat", 1024, 1024, 1024)
    d_wout = _matmul(cat, dz1b, "tn", BF16, "mix_dwout", 1024, 1024, 2048)
    (dh0, d_acw, d_acb, d_ang, d_anb, d_bng, d_bnb, d_ms, d_sb), (p_wup0b, p_wout) = _mixer_mid_bwd(
        h0, y0, dcat, acw, a_norm_g, a_norm_b, b_norm_g, b_norm_b, ms, mst, sbt, "mix_mid_bwd",
        comm=_Comm(exchange=[(d_wup0, early, rows_up - early), d_wout.reshape(N_DEV, -1, D_MODEL)]))
    d_bqkv = jnp.concatenate([dbq, dbkv], axis=1)
    d_lng = jnp.stack([jnp.stack([dg00, dg01]), jnp.stack([dg10, dg11])])
    d_lnb = jnp.stack([jnp.stack([db00, db01]), jnp.stack([db10, db11])])
    small_full = [d_acb, d_ang, d_anb, d_bng, d_bnb, d_ms, d_sb, dsinks[:, :N_Q_HEADS], jnp.concatenate([d_fcb0, d_fcb1], axis=0),
                  d_acw, d_bqkv, jnp.stack([d_fcw0, d_fcw1]), d_lng, d_lnb, loss_terms]
    d_win = _matmul(dh0, xt, "tn", BF16, "mix_dwin", 1024, 1024, 512)
    grad_x, (g_small_grads, p_win) = _matmul(dh0, w_in, "nn", F32, "mix_dx", 1024, 1024, 1024, res=dz1, res_scale=ALPHA,
                                             comm=_Comm(gather=[_pack(small_full)], exchange=[d_win.reshape(N_DEV, -1, D_MODEL)]))


    big = {}
    for nm, p, w, m, v, tr, transposed in [
            ("ab_w_in", [p_win], ab_w_in, m_ab_w_in, v_ab_w_in, 256, True),
            ("ab_w_out", [p_wout], ab_w_out, m_ab_w_out, v_ab_w_out, 128, False),
            ("c_w_qkv", [p_wqkv], c_w_qkv, m_c_w_qkv, v_c_w_qkv, 160, True), ("c_w_o", [p_wo], c_w_o, m_c_w_o, v_c_w_o, 128, False),
            ("ffn_w_up", [(p_wup0a, 0, early), (p_wup0b, early, rows_up - early), (p_wup1a, 0, first), (p_wup1b, first, rows_up - first)], ffn_w_up, m_ffn_w_up, v_ffn_w_up, 176, True),
            ("ffn_w_down", [p_wdown0, p_wdown1], ffn_w_down, m_ffn_w_down, v_ffn_w_down, 176, False)]:
        def two_d(a):
            a = jnp.swapaxes(a, 1, 2) if transposed else a
            return a.reshape(-1, a.shape[-1])

        def back(o):
            return jnp.swapaxes(o.reshape(w.shape[0], w.shape[2], w.shape[1]), 1, 2) if transposed else o.reshape(w.shape)

        outs = _adamw_big(p, two_d(w), two_d(m), two_d(v), "adamw_" + nm, tr)
        big[nm] = [back(o) for o in outs]

    *gs, loss_parts = _unpack(g_small_grads, [a.shape for a in small_full])
    loss = 0.5 / D_MODEL * jnp.sum(loss_parts)

    def my_shard(g, width):
        g = g.reshape(g.shape[:-1] + (N_DEV, width))
        return lax.dynamic_index_in_dim(g, me, axis=g.ndim - 2, keepdims=False)

    small_names = ["a_conv_b", "a_norm_g", "a_norm_b", "b_norm_g", "b_norm_b", "b_spatial_w", "b_spatial_b", "c_sinks", "ffn_conv_b",
                   "a_conv_w", "c_b_qkv", "ffn_conv_w", "ln_g", "ln_b"]
    small_w = [a_conv_b, a_norm_g, a_norm_b, b_norm_g, b_norm_b, b_spatial_w, b_spatial_b, c_sinks, ffn_conv_b,
               a_conv_w, c_b_qkv, ffn_conv_w, ln_g, ln_b]
    small_m = [m_a_conv_b, m_a_norm_g, m_a_norm_b, m_b_norm_g, m_b_norm_b, m_b_spatial_w, m_b_spatial_b, m_c_sinks, m_ffn_conv_b,
               m_a_conv_w, m_c_b_qkv, m_ffn_conv_w, m_ln_g, m_ln_b]
    small_v = [v_a_conv_b, v_a_norm_g, v_a_norm_b, v_b_norm_g, v_b_norm_b, v_b_spatial_w, v_b_spatial_b, v_c_sinks, v_ffn_conv_b,
               v_a_conv_w, v_c_b_qkv, v_ffn_conv_w, v_ln_g, v_ln_b]
    gs[9:] = [my_shard(g, w.shape[-1]) for g, w in zip(gs[9:], small_w[9:])]
    two_d = [(-1, w.shape[-1]) for w in small_w]
    outs = _adamw_small([g.reshape((N_DEV,) + w.reshape(s).shape) for g, w, s in zip(gs, small_w, two_d)],
                        [w.reshape(s) for w, s in zip(small_w, two_d)], [m.reshape(s) for m, s in zip(small_m, two_d)],
                        [v.reshape(s) for v, s in zip(small_v, two_d)], "adamw_small")
    small = {nm: [o.reshape(w.shape) for o in outs[4 * a:4 * a + 4]] for a, (nm, w) in enumerate(zip(small_names, small_w))}

    res = {**big, **small}
    order = ["ab_w_in", "a_conv_w", "a_conv_b", "a_norm_g", "a_norm_b", "b_norm_g", "b_norm_b", "b_spatial_w", "b_spatial_b", "ab_w_out",
             "c_w_qkv", "c_b_qkv", "c_sinks", "c_w_o", "ffn_w_up", "ffn_conv_w", "ffn_conv_b", "ffn_w_down", "ln_g", "ln_b"]
    return (loss, grad_x[None], *[res[nm][0] for nm in order], *[res[nm][1] for nm in order],
            *[res[nm][2] for nm in order], *[res[nm][3] for nm in order])
```

```python
import functools
import math

import jax
import jax.numpy as jnp
from jax import lax
from jax.experimental import pallas as pl
from jax.experimental.pallas import tpu as pltpu

F32 = jnp.float32
BF16 = jnp.bfloat16

N_DEV = 8
D_MODEL = 1024
A_WIDTH = 512
A_KERNEL = 31
B_GROUPS = 4
B_CHUNK = 128
HEAD_DIM = 64
N_Q_HEADS = 16
N_KV_HEADS = 2
ATT_BLOCK = 128
D_FF = 2816
FFN_KERNEL = 3
ALPHA = (2.0 * 2) ** 0.25
LN_EPS = 1e-5
GELU_K = math.sqrt(2.0 / math.pi)
GELU_C = 0.044715
ADAM_LR = 0.001
ADAM_B1 = 0.9
ADAM_B2 = 0.999
ADAM_EPS = 1e-08
ADAM_WD = 0.01
ADAM_STEP = 10
VMEM_LIMIT = 56 * 1024 * 1024
MESH_ID = pl.DeviceIdType.MESH


def _params(*sem):
    return pltpu.CompilerParams(dimension_semantics=sem, vmem_limit_bytes=VMEM_LIMIT)


def _gelu(x):
    t = jnp.tanh(GELU_K * x * (1.0 + GELU_C * x * x))
    return 0.5 * x * (1.0 + t)


def _gelu_and_grad(x):
    x2 = x * x
    t = jnp.tanh(GELU_K * x * (1.0 + GELU_C * x2))
    g = 0.5 * x * (1.0 + t)
    dg = 0.5 * (1.0 + t) + 0.5 * x * (1.0 - t * t) * (GELU_K * (1.0 + 3.0 * GELU_C * x2))
    return g, dg


def _sigmoid(x):
    return 1.0 / (1.0 + jnp.exp(-x))


def _ln_stats(z):
    mu = jnp.mean(z, axis=-1, keepdims=True)
    zc = z - mu
    var = jnp.mean(zc * zc, axis=-1, keepdims=True)
    r = lax.rsqrt(var + LN_EPS)
    return zc * r, r


def _ln_bwd_rows(dn, nh, r):
    return r * (dn - jnp.mean(dn, axis=-1, keepdims=True) - nh * jnp.mean(dn * nh, axis=-1, keepdims=True))


def _colsum(x):
    return jnp.sum(x, axis=0, keepdims=True)


def _dot(a, b, dims):
    return lax.dot_general(a.astype(BF16), b.astype(BF16), (dims, ((), ())), preferred_element_type=F32)


NN = ((1,), (0,))
NT = ((1,), (1,))
TN = ((0,), (0,))


ANY = pl.BlockSpec(memory_space=pl.ANY)
N_RELATIONS = N_DEV - 1


def _my_place():
    return lax.axis_index("x"), lax.axis_index("y"), lax.axis_index("c")


class _Comm:
    def __init__(self, gather=(), exchange=()):
        gather = [e if isinstance(e, tuple) else (e, 0, e.shape[0], None) for e in gather]
        exchange = [e if isinstance(e, tuple) else (e, 0, e.shape[1]) for e in exchange]
        self.arrs = [e[0] for e in gather] + [e[0] for e in exchange]
        self.n_gather = len(gather)
        self.n = len(self.arrs)
        self.rows = [pl.ds(lo, n) for _, lo, n, _ in gather] + [pl.ds(lo, n) for _, lo, n in exchange]
        self.into = {i: e[3] for i, e in enumerate(gather) if e[3] is not None}

    def out_shape(self):
        return [jax.ShapeDtypeStruct(((N_DEV,) + a.shape) if i < self.n_gather else a.shape, a.dtype)
                for i, a in enumerate(self.arrs)]

    def sems(self):
        return [pltpu.SemaphoreType.DMA((self.n, N_RELATIONS)), pltpu.SemaphoreType.DMA((self.n, N_RELATIONS)),
                pltpu.SemaphoreType.DMA((self.n,))]

    def _gather_copy(self, ins, outs, sems, a, k, place, to, from_input=False):
        px, py, pc = place
        block = outs[a].at[4 * px + 2 * py + pc, self.rows[a]]
        return pltpu.make_async_remote_copy(
            src_ref=ins[a].at[self.rows[a]] if from_input else block, dst_ref=block,
            send_sem=sems[0].at[a, k], recv_sem=sems[1].at[a, k], device_id=to, device_id_type=MESH_ID)

    def _exchange_copy(self, ins, outs, sems, a, k, landing=False):
        x, y, c = _my_place()
        me = 4 * x + 2 * y + c
        peer = (x ^ (k >> 2), y ^ ((k >> 1) & 1), c ^ (k & 1))
        return pltpu.make_async_remote_copy(
            src_ref=ins[a].at[me ^ k, self.rows[a]], dst_ref=outs[a].at[(me ^ k) if landing else me, self.rows[a]],
            send_sem=sems[0].at[a, k - 1], recv_sem=sems[1].at[a, k - 1], device_id=peer, device_id_type=MESH_ID)

    def _local_copy(self, ins, outs, sems, a):
        x, y, c = _my_place()
        me = 4 * x + 2 * y + c
        if a < self.n_gather:
            return pltpu.make_async_copy(ins[a].at[self.rows[a]], outs[a].at[me, self.rows[a]], sems[2].at[a])
        return pltpu.make_async_copy(ins[a].at[me, self.rows[a]], outs[a].at[me, self.rows[a]], sems[2].at[a])

    def _first_stage(self, ins, outs, sems, a):
        x, y, c = _my_place()
        me = (x, y, c)
        chips = [(1 - x, y), (x, 1 - y), (1 - x, 1 - y)]
        return ([self._gather_copy(ins, outs, sems, a, 0, me, (x, y, 1 - c), from_input=True)]
                + [self._gather_copy(ins, outs, sems, a, 1 + j, me, (*chip, c), from_input=True) for j, chip in enumerate(chips)])

    def start(self, ins, outs, sems):
        for a in range(self.n):
            self._local_copy(ins, outs, sems, a).start()
        for a in range(self.n_gather):
            for cp in self._first_stage(ins, outs, sems, a):
                cp.start()
        for k in range(1, N_DEV):
            for a in range(self.n_gather, self.n):
                self._exchange_copy(ins, outs, sems, a, k).start()

    def forward(self, ins, outs, sems):
        x, y, c = _my_place()
        me, sibling = (x, y, c), (x, y, 1 - c)
        for j, chip in enumerate([(1 - x, y), (x, 1 - y), (1 - x, 1 - y)]):
            for a in range(self.n_gather):
                self._gather_copy(ins, outs, sems, a, 1 + j, (*chip, c), me).wait_recv()
                self._gather_copy(ins, outs, sems, a, 4 + j, (*chip, c), sibling).start()

    def finish(self, ins, outs, sems):
        x, y, c = _my_place()
        me, sibling = (x, y, c), (x, y, 1 - c)
        chips = [(1 - x, y), (x, 1 - y), (1 - x, 1 - y)]
        passed = [self._gather_copy(ins, outs, sems, a, 4 + j, (*chip, c), sibling)
                  for j, chip in enumerate(chips) for a in range(self.n_gather)]
        for a in range(self.n_gather):
            self._gather_copy(ins, outs, sems, a, 0, sibling, me).wait_recv()
            for j, chip in enumerate(chips):
                self._gather_copy(ins, outs, sems, a, 4 + j, (*chip, 1 - c), me).wait_recv()
        for k in range(1, N_DEV):
            for a in range(self.n_gather, self.n):
                self._exchange_copy(ins, outs, sems, a, k, landing=True).wait_recv()
        for a in range(self.n_gather):
            for cp in self._first_stage(ins, outs, sems, a):
                cp.wait_send()
        for cp in passed:
            cp.wait_send()
        for k in range(1, N_DEV):
            for a in range(self.n_gather, self.n):
                self._exchange_copy(ins, outs, sems, a, k).wait_send()
        for a in range(self.n):
            self._local_copy(ins, outs, sems, a).wait()


def _comm_only(comm, name):
    assert not comm.into

    def body(*refs):
        ins, outs, sems = refs[:comm.n], refs[comm.n:2 * comm.n], refs[2 * comm.n:]
        comm.start(ins, outs, sems)
        comm.forward(ins, outs, sems)
        comm.finish(ins, outs, sems)

    return pl.pallas_call(body, name=name, in_specs=[ANY] * comm.n, out_specs=[ANY] * comm.n,
                          out_shape=comm.out_shape(), scratch_shapes=comm.sems())(*comm.arrs)


def _call(body, *, name, grid, in_specs, out_specs, out_shape, args, sem, scratch_shapes=(), comm=None):
    in_specs, out_specs, out_shape, scratch_shapes = list(in_specs), list(out_specs), list(out_shape), list(scratch_shapes)
    if comm is None:
        outs = pl.pallas_call(body, name=name, grid=grid, in_specs=in_specs, out_specs=out_specs, out_shape=out_shape,
                              scratch_shapes=scratch_shapes, compiler_params=_params(*sem))(*args)
        return list(outs), []
    n_in, n_out, n_scr, nc = len(in_specs), len(out_specs), len(scratch_shapes), comm.n
    completed = sorted(comm.into)

    def wrapped(*refs):
        ins, refs = refs[:n_in], refs[n_in:]
        c_in, refs = refs[:nc], refs[nc + len(completed):]
        outs, refs = refs[:n_out], refs[n_out:]
        c_out, refs = refs[:nc], refs[nc:]
        scr, sems = refs[:n_scr], refs[n_scr:]
        step = functools.reduce(lambda acc, ax: acc * grid[ax] + pl.program_id(ax), range(len(grid)), 0)
        steps = math.prod(grid)

        @pl.when(step == 0)
        def _():
            comm.start(c_in, c_out, sems)

        @pl.when(step == steps - 1)
        def _():
            comm.forward(c_in, c_out, sems)

        body(*ins, *outs, *scr)

        @pl.when(step == steps - 1)
        def _():
            comm.finish(c_in, c_out, sems)

    outs = pl.pallas_call(
        wrapped, name=name, grid=grid, in_specs=in_specs + [ANY] * (nc + len(completed)), out_specs=out_specs + [ANY] * nc,
        out_shape=out_shape + comm.out_shape(), scratch_shapes=scratch_shapes + comm.sems(),
        input_output_aliases={n_in + nc + pos: n_out + item for pos, item in enumerate(completed)},
        compiler_params=_params(*(["arbitrary"] * len(grid))))(*args, *comm.arrs, *[comm.into[item] for item in completed])
    return list(outs[:n_out]), list(outs[n_out:])


def _matmul(a, b, mode, out_dtype, name, tm, tn, tk, *, bias=None, res=None, res_scale=1.0, b_off=0, comm=None):
    tm = min(tm, a.shape[1] if mode == "tn" else a.shape[0])
    tk = min(tk, a.shape[0] if mode == "tn" else a.shape[1])
    if mode == "nn":
        (m, k), n = a.shape, b.shape[1]
        a_spec = pl.BlockSpec((tm, tk), lambda i, j, kk: (i, kk))
        b_spec = pl.BlockSpec((tk, tn), lambda i, j, kk: (kk + b_off, j))
        dims = NN
    elif mode == "nt":
        (m, k), n = a.shape, b.shape[0]
        a_spec = pl.BlockSpec((tm, tk), lambda i, j, kk: (i, kk))
        b_spec = pl.BlockSpec((tn, tk), lambda i, j, kk: (j, kk + b_off))
        dims = NT
    else:
        (k, m), n = a.shape, b.shape[1]
        a_spec = pl.BlockSpec((tk, tm), lambda i, j, kk: (kk, i))
        b_spec = pl.BlockSpec((tk, tn), lambda i, j, kk: (kk, j))
        dims = TN
    assert m % tm == 0 and n % tn == 0 and k % tk == 0, (name, m, n, k)
    nk = k // tk
    in_specs = [a_spec, b_spec]
    args = [a, b]
    if bias is not None:
        in_specs.append(pl.BlockSpec((1, tn), lambda i, j, kk: (0, j)))
        args.append(bias)
    if res is not None:
        in_specs.append(pl.BlockSpec((tm, tn), lambda i, j, kk: (i, j)))
        args.append(res)

    def finish(out, refs, o_ref):
        pos = 2
        if bias is not None:
            out = out + refs[pos][...]
            pos += 1
        if res is not None:
            out = out + res_scale * refs[pos][...].astype(F32)
        o_ref[...] = out.astype(out_dtype)

    def body_one_step(*refs):
        finish(_dot(refs[0][...], refs[1][...], dims), refs, refs[-1])

    def body(*refs):
        a_ref, b_ref = refs[0], refs[1]
        o_ref, acc = refs[-2], refs[-1]
        kk = pl.program_id(2)

        @pl.when(kk == 0)
        def _():
            acc[...] = jnp.zeros_like(acc)

        acc[...] += _dot(a_ref[...], b_ref[...], dims)

        @pl.when(kk == nk - 1)
        def _():
            finish(acc[...], refs, o_ref)

    (out,), moved = _call(
        body_one_step if nk == 1 else body, name=name, grid=(m // tm, n // tn, nk),
        in_specs=in_specs, out_specs=[pl.BlockSpec((tm, tn), lambda i, j, kk: (i, j))],
        out_shape=[jax.ShapeDtypeStruct((m, n), out_dtype)],
        scratch_shapes=[] if nk == 1 else [pltpu.VMEM((tm, tn), F32)],
        sem=("parallel", "parallel", "arbitrary"), args=args, comm=comm)
    return out if comm is None else (out, moved)


def _matmul_tn_pair(a0, a1, b, out_dtype, name, tm, tn, tk, comm=None):
    (k, m), n = a0.shape, b.shape[1]
    tk = min(tk, k)
    assert a1.shape == a0.shape and m % tm == 0 and n % tn == 0 and k % tk == 0, (name, m, n, k)
    mi, nk = m // tm, k // tk

    def body(a0_ref, a1_ref, b_ref, o_ref, acc):
        i, kk = pl.program_id(0), pl.program_id(2)

        @pl.when(kk == 0)
        def _():
            acc[...] = jnp.zeros_like(acc)

        @pl.when(i < mi)
        def _():
            acc[...] += _dot(a0_ref[...], b_ref[...], TN)

        @pl.when(i >= mi)
        def _():
            acc[...] += _dot(a1_ref[...], b_ref[...], TN)

        @pl.when(kk == nk - 1)
        def _():
            o_ref[...] = acc[...].astype(out_dtype)

    (out,), moved = _call(
        body, name=name, grid=(2 * mi, n // tn, nk),
        in_specs=[pl.BlockSpec((tk, tm), lambda i, j, kk: (jnp.where(i < mi, kk, nk - 1), jnp.minimum(i, mi - 1))),
                  pl.BlockSpec((tk, tm), lambda i, j, kk: (jnp.where(i >= mi, kk, 0), jnp.maximum(i - mi, 0))),
                  pl.BlockSpec((tk, tn), lambda i, j, kk: (kk, j))],
        out_specs=[pl.BlockSpec((tm, tn), lambda i, j, kk: (i, j))],
        out_shape=[jax.ShapeDtypeStruct((2 * m, n), out_dtype)],
        scratch_shapes=[pltpu.VMEM((tm, tn), F32)],
        sem=("parallel", "parallel", "arbitrary"), args=(a0, a1, b), comm=comm)
    return out if comm is None else (out, moved)


def _residual_input(x_ref, prev_refs):
    if not prev_refs:
        return x_ref[...]
    nh, _ = _ln_stats(x_ref[...])
    return nh * prev_refs[0][...] + prev_refs[1][...]


def _matmul_res_ln(a, b, x, g, beta, name, tm, prev=None, comm=None):
    t, k = a.shape
    d = b.shape[1]
    tm = min(tm, t)
    assert t % tm == 0
    n_prev = 0 if prev is None else 2

    def body(a_ref, b_ref, x_ref, g_ref, beta_ref, *rest):
        z_ref, xo_ref = rest[n_prev:]
        z = ALPHA * _residual_input(x_ref, rest[:n_prev]) + _dot(a_ref[...], b_ref[...], NN)
        nh, _ = _ln_stats(z)
        z_ref[...] = z
        xo_ref[...] = (nh * g_ref[...] + beta_ref[...]).astype(BF16)

    row = pl.BlockSpec((tm, d), lambda i: (i, 0))
    vec = pl.BlockSpec((1, d), lambda i: (0, 0))
    outs, moved = _call(
        body, name=name, grid=(t // tm,),
        in_specs=[pl.BlockSpec((tm, k), lambda i: (i, 0)), pl.BlockSpec((k, d), lambda i: (0, 0)), row, vec, vec] + [vec] * n_prev,
        out_specs=[row, row],
        out_shape=[jax.ShapeDtypeStruct((t, d), F32), jax.ShapeDtypeStruct((t, d), BF16)],
        sem=("parallel",), args=(a, b, x, g, beta, *(prev or ())), comm=comm)
    return outs if comm is None else (outs, moved)


def _matmul_ln_bwd(parts, b, z, g, dres, name, tm, comm=None):
    m = parts[0][0].shape[0]
    d = b.shape[1]
    tm = min(tm, m)
    n = len(parts)
    assert m % tm == 0 and all(row % a.shape[1] == 0 for a, row in parts)

    def body(*refs):
        z_ref, g_ref, dres_ref = refs[2 * n:2 * n + 3]
        dz_ref, dzb_ref, dg_ref, db_ref = refs[-4:]

        @pl.when(pl.program_id(0) == 0)
        def _():
            dg_ref[...] = jnp.zeros_like(dg_ref)
            db_ref[...] = jnp.zeros_like(db_ref)

        dy = ALPHA * dres_ref[...]
        for p in range(n):
            dy = dy + _dot(refs[p][...], refs[n + p][...], NN)
        nh, r = _ln_stats(z_ref[...])
        dg_ref[...] += _colsum(dy * nh)
        db_ref[...] += _colsum(dy)
        dz = _ln_bwd_rows(dy * g_ref[...], nh, r)
        dz_ref[...] = dz
        dzb_ref[...] = dz.astype(BF16)

    def b_spec(a, row):
        blk = row // a.shape[1]
        return pl.BlockSpec((a.shape[1], d), lambda i: (blk, 0))

    row = pl.BlockSpec((tm, d), lambda i: (i, 0))
    vec = pl.BlockSpec((1, d), lambda i: (0, 0))
    vshape = jax.ShapeDtypeStruct((1, d), F32)
    outs, moved = _call(
        body, name=name, grid=(m // tm,),
        in_specs=[pl.BlockSpec((tm, a.shape[1]), lambda i: (i, 0)) for a, _ in parts] + [b_spec(a, r_) for a, r_ in parts]
        + [row, vec, row],
        out_specs=[row, row, vec, vec],
        out_shape=[jax.ShapeDtypeStruct((m, d), F32), jax.ShapeDtypeStruct((m, d), BF16), vshape, vshape],
        sem=("arbitrary",), args=(*[a for a, _ in parts], *([b] * n), z, g, dres), comm=comm)
    return outs if comm is None else (outs, moved)


def _matmul_res_ln_loss(a, b, x, g, beta, target, name, tm, prev):
    t, k = a.shape
    d = b.shape[1]
    tm = min(tm, t)

    def body(a_ref, b_ref, x_ref, g_ref, beta_ref, t_ref, gp_ref, bp_ref, dz_ref, dzb_ref, dg_ref, db_ref, loss_ref):
        @pl.when(pl.program_id(0) == 0)
        def _():
            dg_ref[...] = jnp.zeros_like(dg_ref)
            db_ref[...] = jnp.zeros_like(db_ref)
            loss_ref[...] = jnp.zeros_like(loss_ref)

        nh, r = _ln_stats(ALPHA * _residual_input(x_ref, (gp_ref, bp_ref)) + _dot(a_ref[...], b_ref[...], NN))
        err = nh * g_ref[...] + beta_ref[...] - t_ref[...]
        loss_ref[...] += _colsum(err * err)
        dy = err * (1.0 / d)
        dg_ref[...] += _colsum(dy * nh)
        db_ref[...] += _colsum(dy)
        dz = _ln_bwd_rows(dy * g_ref[...], nh, r)
        dz_ref[...] = dz
        dzb_ref[...] = dz.astype(BF16)

    row = pl.BlockSpec((tm, d), lambda i: (i, 0))
    vec = pl.BlockSpec((1, d), lambda i: (0, 0))
    vshape = jax.ShapeDtypeStruct((1, d), F32)
    return pl.pallas_call(
        body, name=name, grid=(t // tm,),
        in_specs=[pl.BlockSpec((tm, k), lambda i: (i, 0)), pl.BlockSpec((k, d), lambda i: (0, 0)), row, vec, vec, row, vec, vec],
        out_specs=[row, row, vec, vec, vec],
        out_shape=[jax.ShapeDtypeStruct((t, d), F32), jax.ShapeDtypeStruct((t, d), BF16), vshape, vshape, vshape],
        compiler_params=_params("arbitrary"),
    )(a, b, x, g, beta, target, *prev)


FFN_HALO = 16
FFN_CHUNK = 256
LANES = 128
SUBLANES = 8


def _rows_up(e, start, rows):
    if start % SUBLANES == 0:
        return e[start:start + rows]
    return pltpu.roll(e, e.shape[0] - start, 0)[0:rows]


def _fold(x):
    return jnp.sum(x.reshape(x.shape[0] // SUBLANES, SUBLANES, x.shape[1]), axis=0)


def _ffn_mid_fwd(h, cw, cb, name, tm=1024, tc=1408, comm=None):
    t, f2 = h.shape
    tm = min(tm, t)
    f = f2 // 2
    nj, nt, hb = f // tc, t // tm, tm // FFN_HALO

    ch = min(FFN_CHUNK, tm)

    def body(hg, hgp, hv, hvp, cwg, cwv, cbg, cbv, u_ref, cg_ref, cv_ref):
        i = pl.program_id(1)
        o = FFN_HALO - FFN_KERNEL + 1
        for lg in range(tc // LANES):
            cols = slice(lg * LANES, (lg + 1) * LANES)
            wg, wv = [cwg[k:k + 1, cols] for k in range(FFN_KERNEL)], [cwv[k:k + 1, cols] for k in range(FFN_KERNEL)]
            bg, bv = cbg[:, cols], cbv[:, cols]

            def emit(base, eg, ev):
                cg = wg[0] * _rows_up(eg, o, ch) + wg[1] * _rows_up(eg, o + 1, ch) + wg[2] * _rows_up(eg, o + 2, ch) + bg
                cv = wv[0] * _rows_up(ev, o, ch) + wv[1] * _rows_up(ev, o + 1, ch) + wv[2] * _rows_up(ev, o + 2, ch) + bv
                u_ref[pl.ds(base, ch), cols] = (_gelu(cg) * cv).astype(BF16)
                cg_ref[pl.ds(base, ch), cols] = cg.astype(BF16)
                cv_ref[pl.ds(base, ch), cols] = cv.astype(BF16)

            def first(main, prev):
                return jnp.concatenate([jnp.where(i > 0, prev[:, cols].astype(F32), 0.0), main[0:ch, cols].astype(F32)], axis=0)

            def inner(c, carry):
                base = pl.multiple_of(c * ch, ch)
                emit(base, hg[pl.ds(base - FFN_HALO, ch + FFN_HALO), cols].astype(F32),
                     hv[pl.ds(base - FFN_HALO, ch + FFN_HALO), cols].astype(F32))
                return carry

            emit(0, first(hg, hgp), first(hv, hvp))
            if tm > ch:
                lax.fori_loop(1, tm // ch, inner, 0)

    def main_spec(off):
        return pl.BlockSpec((tm, tc), lambda j, i: (i, j + off))

    def prev_spec(off):
        return pl.BlockSpec((FFN_HALO, tc), lambda j, i: (jnp.maximum(i * hb - 1, 0), j + off))

    def par_spec(rows, off):
        return pl.BlockSpec((rows, tc), lambda j, i: (0, j + off))

    outs, moved = _call(
        body, name=name, grid=(nj, nt),
        in_specs=[main_spec(0), prev_spec(0), main_spec(nj), prev_spec(nj),
                  par_spec(FFN_KERNEL, 0), par_spec(FFN_KERNEL, nj), par_spec(1, 0), par_spec(1, nj)],
        out_specs=[pl.BlockSpec((tm, tc), lambda j, i: (i, j))] * 3,
        out_shape=[jax.ShapeDtypeStruct((t, f), BF16)] * 3,
        sem=("parallel", "arbitrary"), args=(h, h, h, h, cw, cw, cb, cb), comm=comm)
    return outs if comm is None else (outs, moved)


def _ffn_mid_bwd(h, cg, cv, du, cw, name, tm=1024, tc=1408, comm=None):
    t, f2 = h.shape
    tm = min(tm, t)
    f = f2 // 2
    nj, nt, hb = f // tc, t // tm, tm // FFN_HALO

    ch = min(FFN_CHUNK, tm)
    ahead = ch + SUBLANES
    n_ch = tm // ch

    def body(hg, hv, cg_ref, cgn_ref, cv_ref, cvn_ref, du_ref, dun_ref, cwg, cwv,
             dhg_ref, dhv_ref, dcwg_ref, dcwv_ref, dcbg_ref, dcbv_ref):
        i = pl.program_id(1)

        @pl.when(i == 0)
        def _():
            for ref in (dcwg_ref, dcwv_ref, dcbg_ref, dcbv_ref):
                ref[...] = jnp.zeros_like(ref)

        for lg in range(tc // LANES):
            cols = slice(lg * LANES, (lg + 1) * LANES)
            wg, wv = [cwg[k:k + 1, cols] for k in range(FFN_KERNEL)], [cwv[k:k + 1, cols] for k in range(FFN_KERNEL)]

            def emit(base, cg_e, cv_e, du_e, acc):
                cg_a, cv_a, du_a = cg_e[0:ahead], cv_e[0:ahead], du_e[0:ahead]
                gl, dgl = _gelu_and_grad(cg_a)

                def back(d, h_ref, w, dh_ref):
                    later = [d[0:ch], _rows_up(d, 1, ch), _rows_up(d, 2, ch)]
                    dh_ref[pl.ds(base, ch), cols] = (w[2] * later[0] + w[1] * later[1] + w[0] * later[2]).astype(BF16)
                    h_own = h_ref[pl.ds(base, ch), cols].astype(F32)
                    return [_fold(later[0])] + [_fold(later[FFN_KERNEL - 1 - k] * h_own) for k in range(FFN_KERNEL)]

                sums = back(du_a * cv_a * dgl, hg, wg, dhg_ref) + back(du_a * gl, hv, wv, dhv_ref)
                return tuple(a + s_ for a, s_ in zip(acc, sums))

            def inner(c, acc):
                base = pl.multiple_of(c * ch, ch)
                rows = pl.ds(base, ch + FFN_HALO)
                return emit(base, cg_ref[rows, cols].astype(F32), cv_ref[rows, cols].astype(F32), du_ref[rows, cols].astype(F32), acc)

            def last(acc):
                def rows(main, after):
                    return jnp.concatenate([main[tm - ch:tm, cols].astype(F32), after], axis=0)

                du_next = jnp.where(i < nt - 1, dun_ref[:, cols].astype(F32), 0.0)
                return emit(tm - ch, rows(cg_ref, cgn_ref[:, cols].astype(F32)), rows(cv_ref, cvn_ref[:, cols].astype(F32)),
                            rows(du_ref, du_next), acc)

            acc = (jnp.zeros((SUBLANES, LANES), F32),) * (2 * (1 + FFN_KERNEL))
            if n_ch > 1:
                acc = lax.fori_loop(0, n_ch - 1, inner, acc)
            acc = last(acc)
            dcbg_ref[:, cols] += _colsum(acc[0])
            dcbv_ref[:, cols] += _colsum(acc[1 + FFN_KERNEL])
            for k in range(FFN_KERNEL):
                dcwg_ref[k:k + 1, cols] += _colsum(acc[1 + k])
                dcwv_ref[k:k + 1, cols] += _colsum(acc[2 + FFN_KERNEL + k])

    last_blk = t // FFN_HALO - 1

    def main_spec(off):
        return pl.BlockSpec((tm, tc), lambda j, i: (i, j + off))

    def next_spec(off):
        return pl.BlockSpec((FFN_HALO, tc), lambda j, i: (jnp.minimum((i + 1) * hb, last_blk), j + off))

    def par_spec(rows, off):
        return pl.BlockSpec((rows, tc), lambda j, i: (0, j + off))

    out_tile = pl.BlockSpec((tm, tc), lambda j, i: (i, j))
    outs, moved = _call(
        body, name=name, grid=(nj, nt),
        in_specs=[main_spec(0), main_spec(nj), main_spec(0), next_spec(0), main_spec(0), next_spec(0), main_spec(0), next_spec(0),
                  par_spec(FFN_KERNEL, 0), par_spec(FFN_KERNEL, nj)],
        out_specs=[out_tile, out_tile, par_spec(FFN_KERNEL, 0), par_spec(FFN_KERNEL, 0), par_spec(1, 0), par_spec(1, 0)],
        out_shape=[jax.ShapeDtypeStruct((t, f), BF16), jax.ShapeDtypeStruct((t, f), BF16),
                   jax.ShapeDtypeStruct((FFN_KERNEL, f), F32), jax.ShapeDtypeStruct((FFN_KERNEL, f), F32),
                   jax.ShapeDtypeStruct((1, f), F32), jax.ShapeDtypeStruct((1, f), F32)],
        sem=("parallel", "arbitrary"), args=(h, h, cg, cg, cv, cv, du, du, cw, cw), comm=comm)
    return outs if comm is None else (outs, moved)


MIX_HALO = 32


def _glu(hh):
    return hh[:, 0:A_WIDTH] * _sigmoid(hh[:, A_WIDTH:2 * A_WIDTH])


def _fill_row_shifts(s):
    rows = s.shape[1] - SUBLANES
    for j in range(1, SUBLANES):
        s[j, 0:rows, :] = s[0, pl.ds(j, rows), :]


def _rows_from(s, start, rows):
    j = start % SUBLANES
    return s[j, start - j:start - j + rows, :]


def _tril_mask():
    return lax.broadcasted_iota(jnp.int32, (B_CHUNK, B_CHUNK), 0) >= lax.broadcasted_iota(jnp.int32, (B_CHUNK, B_CHUNK), 1)


def _spatial_mix(q, ms_ref, sbt_ref, tm):
    mask = _tril_mask()
    ws = [jnp.where(mask, ms_ref[g], 0.0).astype(BF16) for g in range(B_GROUPS)]
    qb = q.astype(BF16)
    rows = []
    for c in range(tm // B_CHUNK):
        cols = [_dot(ws[g], qb[c * B_CHUNK:(c + 1) * B_CHUNK, g * 128:(g + 1) * 128], NN) + sbt_ref[:, g:g + 1]
                for g in range(B_GROUPS)]
        rows.append(jnp.concatenate(cols, axis=1))
    return jnp.concatenate(rows, axis=0)


def _mixer_mid_fwd(h, cw, cb, ag, ab, bg, bb, ms, sbt, name, tm=256, comm=None):
    t = h.shape[0]
    nt, hb = t // tm, tm // MIX_HALO
    o = MIX_HALO - A_KERNEL + 1

    def body(h_ref, hp_ref, cw_ref, cb_ref, ag_ref, ab_ref, bg_ref, bb_ref, ms_ref, sbt_ref, cat_ref, y_ref, sp):
        i = pl.program_id(0)
        sp[0, 0:MIX_HALO, :] = jnp.where(i > 0, _glu(hp_ref[:, 0:2 * A_WIDTH].astype(F32)), 0.0)
        sp[0, MIX_HALO:, :] = _glu(h_ref[:, 0:2 * A_WIDTH].astype(F32))
        _fill_row_shifts(sp)
        y = jnp.zeros((tm, A_WIDTH), F32) + cb_ref[...]
        for k in range(A_KERNEL):
            y = y + cw_ref[k:k + 1, :] * _rows_from(sp, o + k, tm)
        y_ref[...] = y.astype(BF16)
        nh, _ = _ln_stats(y)
        ln = nh * ag_ref[...] + ab_ref[...]
        cat_ref[:, 0:A_WIDTH] = (ln * _sigmoid(ln)).astype(BF16)
        u = _gelu(h_ref[:, 1024:1536].astype(F32))
        nb, _ = _ln_stats(_gelu(h_ref[:, 1536:2048].astype(F32)))
        mixed = _spatial_mix(nb * bg_ref[...] + bb_ref[...], ms_ref, sbt_ref, tm)
        cat_ref[:, A_WIDTH:] = (u * mixed).astype(BF16)

    vec = pl.BlockSpec((1, A_WIDTH), lambda i: (0, 0))
    outs, moved = _call(
        body, name=name, grid=(nt,),
        in_specs=[pl.BlockSpec((tm, 2048), lambda i: (i, 0)),
                  pl.BlockSpec((MIX_HALO, 2048), lambda i: (jnp.maximum(i * hb - 1, 0), 0)),
                  pl.BlockSpec((A_KERNEL, A_WIDTH), lambda i: (0, 0)), vec, vec, vec, vec, vec,
                  pl.BlockSpec((B_GROUPS, B_CHUNK, B_CHUNK), lambda i: (0, 0, 0)),
                  pl.BlockSpec((B_CHUNK, B_GROUPS), lambda i: (0, 0))],
        out_specs=[pl.BlockSpec((tm, D_MODEL), lambda i: (i, 0)), pl.BlockSpec((tm, A_WIDTH), lambda i: (i, 0))],
        out_shape=[jax.ShapeDtypeStruct((t, D_MODEL), BF16), jax.ShapeDtypeStruct((t, A_WIDTH), BF16)],
        scratch_shapes=[pltpu.VMEM((SUBLANES, tm + MIX_HALO, A_WIDTH), F32)],
        sem=("parallel",), args=(h, h, cw, cb, ag, ab, bg, bb, ms, sbt), comm=comm)
    return outs if comm is None else (outs, moved)


def _mixer_mid_bwd(h, y, dcat, cw, ag, ab, bg, bb, ms, mst, sbt, name, tm=256, comm=None):
    t = h.shape[0]
    nt, hb = t // tm, tm // MIX_HALO
    r = tm + MIX_HALO
    nchunk = tm // B_CHUNK

    def body(h_ref, y_ref, yn_ref, dc_ref, dcn_ref, cw_ref, ag_ref, ab_ref, bg_ref, bb_ref, ms_ref, mst_ref, sbt_ref,
             dh_ref, dcw_ref, dcb_ref, dag_ref, dab_ref, dbg_ref, dbb_ref, dms_ref, dsb_ref, sdy, sbacc):
        i = pl.program_id(0)

        @pl.when(i == 0)
        def _():
            for ref in (dcw_ref, dcb_ref, dag_ref, dab_ref, dbg_ref, dbb_ref, dms_ref, dsb_ref, sbacc):
                ref[...] = jnp.zeros_like(ref)

        nh, rs = _ln_stats(jnp.concatenate([y_ref[...].astype(F32), yn_ref[...].astype(F32)], axis=0))
        ln = nh * ag_ref[...] + ab_ref[...]
        sg = _sigmoid(ln)
        dao = jnp.concatenate([dc_ref[:, 0:A_WIDTH].astype(F32),
                               jnp.where(i < nt - 1, dcn_ref[:, 0:A_WIDTH].astype(F32), 0.0)], axis=0)
        dln = dao * (sg * (1.0 + ln * (1.0 - sg)))
        dag_ref[...] += _colsum(dln[0:tm] * nh[0:tm])
        dab_ref[...] += _colsum(dln[0:tm])
        sdy[0] = _ln_bwd_rows(dln * ag_ref[...], nh, rs)
        _fill_row_shifts(sdy)
        dcb_ref[...] += _colsum(sdy[0, 0:tm, :])
        av = h_ref[:, 0:A_WIDTH].astype(F32)
        s = _sigmoid(h_ref[:, A_WIDTH:2 * A_WIDTH].astype(F32))
        p_own = av * s
        dp = jnp.zeros((tm, A_WIDTH), F32)
        for k in range(A_KERNEL):
            later = _rows_from(sdy, A_KERNEL - 1 - k, tm)
            dcw_ref[k:k + 1, :] += _colsum(later * p_own)
            dp = dp + cw_ref[k:k + 1, :] * later
        dh_ref[:, 0:A_WIDTH] = (dp * s).astype(BF16)
        dh_ref[:, A_WIDTH:2 * A_WIDTH] = (dp * av * s * (1.0 - s)).astype(BF16)

        u, dgu = _gelu_and_grad(h_ref[:, 1024:1536].astype(F32))
        w, dgw = _gelu_and_grad(h_ref[:, 1536:2048].astype(F32))
        nb, rb = _ln_stats(w)
        q = nb * bg_ref[...] + bb_ref[...]
        mixed = _spatial_mix(q, ms_ref, sbt_ref, tm)
        dbo = dc_ref[:, A_WIDTH:].astype(F32)
        dh_ref[:, 1024:1536] = (dbo * mixed * dgu).astype(BF16)
        dmx = dbo * u
        mask = _tril_mask()
        wst = [jnp.where(mask.T, mst_ref[g], 0.0).astype(BF16) for g in range(B_GROUPS)]
        qb = q.astype(BF16)
        dmb = dmx.astype(BF16)
        rows = []
        for c in range(nchunk):
            cols = []
            for g in range(B_GROUPS):
                rs_, cs_ = slice(c * B_CHUNK, (c + 1) * B_CHUNK), slice(g * 128, (g + 1) * 128)
                sbacc[g] += dmx[rs_, cs_]
                dms_ref[g] += _dot(dmb[rs_, cs_], qb[rs_, cs_], NT)
                cols.append(_dot(wst[g], dmb[rs_, cs_], NN))
            rows.append(jnp.concatenate(cols, axis=1))
        dq = jnp.concatenate(rows, axis=0)
        dbg_ref[...] += _colsum(dq * nb)
        dbb_ref[...] += _colsum(dq)
        dh_ref[:, 1536:2048] = (_ln_bwd_rows(dq * bg_ref[...], nb, rb) * dgw).astype(BF16)

        @pl.when(i == nt - 1)
        def _():
            for g in range(B_GROUPS):
                dms_ref[g] = jnp.where(mask, dms_ref[g], 0.0)
                dsb_ref[g] = jnp.sum(sbacc[g], axis=1, keepdims=True)

    last_blk = t // MIX_HALO - 1
    vec = pl.BlockSpec((1, A_WIDTH), lambda i: (0, 0))
    mat = pl.BlockSpec((B_GROUPS, B_CHUNK, B_CHUNK), lambda i: (0, 0, 0))
    taps = pl.BlockSpec((A_KERNEL, A_WIDTH), lambda i: (0, 0))

    def halo(width):
        return pl.BlockSpec((MIX_HALO, width), lambda i: (jnp.minimum((i + 1) * hb, last_blk), 0))

    vshape = jax.ShapeDtypeStruct((1, A_WIDTH), F32)
    outs, moved = _call(
        body, name=name, grid=(nt,),
        in_specs=[pl.BlockSpec((tm, 2048), lambda i: (i, 0)), pl.BlockSpec((tm, A_WIDTH), lambda i: (i, 0)), halo(A_WIDTH),
                  pl.BlockSpec((tm, D_MODEL), lambda i: (i, 0)), halo(D_MODEL),
                  taps, vec, vec, vec, vec, mat, mat, pl.BlockSpec((B_CHUNK, B_GROUPS), lambda i: (0, 0))],
        out_specs=[pl.BlockSpec((tm, 2048), lambda i: (i, 0)), taps, vec, vec, vec, vec, vec, mat,
                   pl.BlockSpec((B_GROUPS, B_CHUNK, 1), lambda i: (0, 0, 0))],
        out_shape=[jax.ShapeDtypeStruct((t, 2048), BF16), jax.ShapeDtypeStruct((A_KERNEL, A_WIDTH), F32),
                   vshape, vshape, vshape, vshape, vshape,
                   jax.ShapeDtypeStruct((B_GROUPS, B_CHUNK, B_CHUNK), F32), jax.ShapeDtypeStruct((B_GROUPS, B_CHUNK, 1), F32)],
        scratch_shapes=[pltpu.VMEM((SUBLANES, r, A_WIDTH), F32), pltpu.VMEM((B_GROUPS, B_CHUNK, B_CHUNK), F32)],
        sem=("arbitrary",), args=(h, y, y, dcat, dcat, cw, ag, ab, bg, bb, ms, mst, sbt), comm=comm)
    return outs if comm is None else (outs, moved)


Q_WIDTH = N_Q_HEADS * HEAD_DIM
KV_WIDTH = 2 * N_KV_HEADS * HEAD_DIM
PAIRS_PER_KV = N_Q_HEADS // N_KV_HEADS // 2
ATT_SCALE = 1.0 / math.sqrt(HEAD_DIM)


def _dup_heads(pair_cols, kv_head):
    lane = lax.broadcasted_iota(jnp.int32, pair_cols.shape, 1)
    rolled = pltpu.roll(pair_cols, HEAD_DIM, 1)
    first = lane < HEAD_DIM
    return jnp.where(first, pair_cols, rolled) if kv_head == 0 else jnp.where(first, rolled, pair_cols)


HEADS_PER_KV = N_Q_HEADS // N_KV_HEADS


def _stack_heads(ref, kh):
    lane = lax.broadcasted_iota(jnp.int32, (ATT_BLOCK, 128), 1)
    rows = []
    for pr in range(PAIRS_PER_KV):
        c0 = (kh * PAIRS_PER_KV + pr) * 128
        pair = ref[:, c0:c0 + 128]
        rows += [jnp.where(lane < HEAD_DIM, pair, jnp.zeros_like(pair)), jnp.where(lane < HEAD_DIM, jnp.zeros_like(pair), pair)]
    return jnp.concatenate(rows, axis=0)


def _unstack_heads(stacked, kh, write):
    lane = lax.broadcasted_iota(jnp.int32, (ATT_BLOCK, 128), 1)
    for pr in range(PAIRS_PER_KV):
        first = stacked[(2 * pr) * ATT_BLOCK:(2 * pr + 1) * ATT_BLOCK]
        second = stacked[(2 * pr + 1) * ATT_BLOCK:(2 * pr + 2) * ATT_BLOCK]
        write((kh * PAIRS_PER_KV + pr) * 128, jnp.where(lane < HEAD_DIM, first, second))


def _sink_row(sink_ref, kh):
    return jnp.concatenate([jnp.full((1, ATT_BLOCK), sink_ref[0, kh * HEADS_PER_KV + h], F32) for h in range(HEADS_PER_KV)], axis=1)


def _att_window_bias():
    sj = lax.broadcasted_iota(jnp.int32, (2 * ATT_BLOCK, HEADS_PER_KV * ATT_BLOCK), 0)
    qi = lax.broadcasted_iota(jnp.int32, (2 * ATT_BLOCK, HEADS_PER_KV * ATT_BLOCK), 1) & (ATT_BLOCK - 1)
    diff = qi + ATT_BLOCK - sj
    return jnp.where((diff >= 0) & (diff < ATT_BLOCK), 0.0, -jnp.inf)


def _att_probs_t(q_all, k2, bias_ref, n, sink):
    st = _dot(k2, q_all, NT) * ATT_SCALE + bias_ref[...]
    st = jnp.concatenate([jnp.where(n > 0, st[0:ATT_BLOCK], -jnp.inf), st[ATT_BLOCK:]], axis=0)
    m = jnp.maximum(jnp.max(st, axis=0, keepdims=True), sink)
    e = jnp.exp(st - m)
    es = jnp.exp(sink - m)
    inv = 1.0 / (jnp.sum(e, axis=0, keepdims=True) + es)
    return e * inv, es * inv


def _attn_fwd(qkv, sinks, name, comm=None):
    t = qkv.shape[0]
    nb = t // ATT_BLOCK
    kvb = Q_WIDTH // KV_WIDTH

    def body(sink_ref, q_ref, kv_ref, kvp_ref, o_ref, bias):
        n = pl.program_id(0)

        @pl.when(n == 0)
        def _():
            bias[...] = _att_window_bias()

        kv = jnp.concatenate([kvp_ref[...], kv_ref[...]], axis=0).astype(F32)

        def write(c0, pair):
            o_ref[:, c0:c0 + 128] = pair.astype(BF16)

        for kh in range(N_KV_HEADS):
            k2 = _dup_heads(kv[:, 0:128], kh).astype(BF16)
            v2 = _dup_heads(kv[:, 128:256], kh).astype(BF16)
            pt, _ = _att_probs_t(_stack_heads(q_ref, kh), k2, bias, n, _sink_row(sink_ref, kh))
            _unstack_heads(_dot(v2, pt, TN).T, kh, write)

    (out,), moved = _call(
        body, name=name, grid=(nb,),
        in_specs=[pl.BlockSpec(memory_space=pltpu.SMEM),
                  pl.BlockSpec((ATT_BLOCK, Q_WIDTH), lambda n: (n, 0)),
                  pl.BlockSpec((ATT_BLOCK, KV_WIDTH), lambda n: (n, kvb)),
                  pl.BlockSpec((ATT_BLOCK, KV_WIDTH), lambda n: (jnp.maximum(n - 1, 0), kvb))],
        out_specs=[pl.BlockSpec((ATT_BLOCK, Q_WIDTH), lambda n: (n, 0))],
        out_shape=[jax.ShapeDtypeStruct((t, Q_WIDTH), BF16)],
        scratch_shapes=[pltpu.VMEM((2 * ATT_BLOCK, HEADS_PER_KV * ATT_BLOCK), F32)],
        sem=("arbitrary",), args=(sinks, qkv, qkv, qkv), comm=comm)
    return out if comm is None else (out, moved)


def _attn_bwd(qkv, d_o, sinks, name, comm=None):
    t = qkv.shape[0]
    nb = t // ATT_BLOCK
    kvb = Q_WIDTH // KV_WIDTH

    def body(sink_ref, q_ref, kv_ref, kvp_ref, do_ref, dq_ref, dkv_ref, dbq_ref, dbkv_ref, dsink_ref, carry, bias):
        n = pl.program_id(0)

        @pl.when(n == 0)
        def _():
            for ref in (dbq_ref, dbkv_ref, dsink_ref, carry):
                ref[...] = jnp.zeros_like(ref)
            dkv_ref[...] = jnp.zeros_like(dkv_ref)
            bias[...] = _att_window_bias()

        @pl.when(n < nb)
        def _():
            kv = jnp.concatenate([kvp_ref[...], kv_ref[...]], axis=0).astype(F32)
            lane2 = lax.broadcasted_iota(jnp.int32, (2 * ATT_BLOCK, 128), 1)
            sink_lane = lax.broadcasted_iota(jnp.int32, (1, 128), 1)
            dsink = jnp.zeros((1, 128), F32)
            dk_parts, dv_parts = [], []

            def write(c0, pair):
                dbq_ref[:, c0:c0 + 128] += _colsum(pair)
                dq_ref[:, c0:c0 + 128] = pair.astype(BF16)

            for kh in range(N_KV_HEADS):
                k2 = _dup_heads(kv[:, 0:128], kh).astype(BF16)
                v2 = _dup_heads(kv[:, 128:256], kh).astype(BF16)
                q_all = _stack_heads(q_ref, kh)
                do_all = _stack_heads(do_ref, kh)
                pt, ps = _att_probs_t(q_all, k2, bias, n, _sink_row(sink_ref, kh))
                dpt = _dot(v2, do_all, NT)
                delta = jnp.sum(pt * dpt, axis=0, keepdims=True)
                dst = pt * (dpt - delta) * ATT_SCALE
                psd = ps * delta
                for h in range(HEADS_PER_KV):
                    dsink = dsink + jnp.where(sink_lane == kh * HEADS_PER_KV + h,
                                              -jnp.sum(psd[:, h * ATT_BLOCK:(h + 1) * ATT_BLOCK]), 0.0)
                _unstack_heads(_dot(k2, dst, TN).T, kh, write)
                dk_acc = _dot(dst, q_all, NN)
                dv_acc = _dot(pt, do_all, NN)
                dk_parts.append(dk_acc + pltpu.roll(dk_acc, HEAD_DIM, 1))
                dv_parts.append(dv_acc + pltpu.roll(dv_acc, HEAD_DIM, 1))
            dk = jnp.where(lane2 < HEAD_DIM, dk_parts[0], dk_parts[1])
            dv = jnp.where(lane2 < HEAD_DIM, dv_parts[0], dv_parts[1])
            dkv_new = jnp.concatenate([dk, dv], axis=1)
            done = carry[...] + dkv_new[0:ATT_BLOCK]

            @pl.when(n > 0)
            def _():
                dkv_ref[...] = done.astype(BF16)
                dbkv_ref[...] += _colsum(done)

            carry[...] = dkv_new[ATT_BLOCK:]
            dsink_ref[...] += dsink

        @pl.when(n == nb)
        def _():
            dkv_ref[...] = carry[...].astype(BF16)
            dbkv_ref[...] += _colsum(carry[...])

    def clamp(n):
        return jnp.minimum(n, nb - 1)

    outs, moved = _call(
        body, name=name, grid=(nb + 1,),
        in_specs=[pl.BlockSpec(memory_space=pltpu.SMEM),
                  pl.BlockSpec((ATT_BLOCK, Q_WIDTH), lambda n: (clamp(n), 0)),
                  pl.BlockSpec((ATT_BLOCK, KV_WIDTH), lambda n: (clamp(n), kvb)),
                  pl.BlockSpec((ATT_BLOCK, KV_WIDTH), lambda n: (jnp.maximum(clamp(n) - 1, 0), kvb)),
                  pl.BlockSpec((ATT_BLOCK, Q_WIDTH), lambda n: (clamp(n), 0))],
        out_specs=[pl.BlockSpec((ATT_BLOCK, Q_WIDTH), lambda n: (clamp(n), 0)),
                   pl.BlockSpec((ATT_BLOCK, KV_WIDTH), lambda n: (jnp.maximum(n - 1, 0), 0)),
                   pl.BlockSpec((1, Q_WIDTH), lambda n: (0, 0)),
                   pl.BlockSpec((1, KV_WIDTH), lambda n: (0, 0)),
                   pl.BlockSpec((1, 128), lambda n: (0, 0))],
        out_shape=[jax.ShapeDtypeStruct((t, Q_WIDTH), BF16), jax.ShapeDtypeStruct((t, KV_WIDTH), BF16),
                   jax.ShapeDtypeStruct((1, Q_WIDTH), F32), jax.ShapeDtypeStruct((1, KV_WIDTH), F32),
                   jax.ShapeDtypeStruct((1, 128), F32)],
        scratch_shapes=[pltpu.VMEM((ATT_BLOCK, KV_WIDTH), F32), pltpu.VMEM((2 * ATT_BLOCK, HEADS_PER_KV * ATT_BLOCK), F32)],
        sem=("arbitrary",), args=(sinks, qkv, qkv, qkv, d_o), comm=comm)
    return outs if comm is None else (outs, moved)


def _adamw_math(g, w, m, v):
    m = ADAM_B1 * m + (1.0 - ADAM_B1) * g
    v = ADAM_B2 * v + (1.0 - ADAM_B2) * (g * g)
    m_hat = m / (1.0 - ADAM_B1 ** ADAM_STEP)
    v_hat = v / (1.0 - ADAM_B2 ** ADAM_STEP)
    delta = -ADAM_LR * (m_hat / (jnp.sqrt(v_hat) + ADAM_EPS) + ADAM_WD * w)
    return delta, m, v


def _sum_partials(p_ref):
    g = p_ref[0].astype(F32)
    for s in range(1, N_DEV):
        g = g + p_ref[s].astype(F32)
    return g


def _adamw_big(parts, w, m, v, name, tr):
    r, c = w.shape
    parts = [p if isinstance(p, tuple) else (p, 0, p.shape[1]) for p in parts]
    tiles = [rows // tr for _, _, rows in parts]
    starts = [sum(tiles[:l]) for l in range(len(parts))]
    assert all(lo % tr == 0 and rows % tr == 0 for _, lo, rows in parts) and sum(tiles) * tr == r

    def body(*refs):
        p_refs, (w_ref, m_ref, v_ref, g_out, d_out, m_out, v_out) = refs[:len(parts)], refs[len(parts):]
        i = pl.program_id(0)
        for l, p_ref in enumerate(p_refs):
            @pl.when((i >= starts[l]) & (i < starts[l] + tiles[l]))
            def _():
                g = _sum_partials(p_ref)
                g_out[...] = g
                d_out[...], m_out[...], v_out[...] = _adamw_math(g, w_ref[...], m_ref[...], v_ref[...])

    def part_spec(l):
        return pl.BlockSpec((N_DEV, tr, c), lambda i: (0, jnp.clip(i - starts[l], 0, tiles[l] - 1) + parts[l][1] // tr, 0))

    tile = pl.BlockSpec((tr, c), lambda i: (i, 0))
    shape = jax.ShapeDtypeStruct((r, c), F32)
    return pl.pallas_call(
        body, name=name, grid=(r // tr,),
        in_specs=[part_spec(l) for l in range(len(parts))] + [tile, tile, tile],
        out_specs=[tile] * 4, out_shape=[shape] * 4,
        compiler_params=_params("parallel"),
    )(*[p[0] for p in parts], w, m, v)


def _adamw_small(parts, ws, ms, vs, name):
    n = len(ws)

    def body(*refs):
        ins, outs = refs[:4 * n], refs[4 * n:]
        for a in range(n):
            g = _sum_partials(ins[a])
            outs[4 * a][...] = g
            outs[4 * a + 1][...], outs[4 * a + 2][...], outs[4 * a + 3][...] = _adamw_math(
                g, ins[n + a][...], ins[2 * n + a][...], ins[3 * n + a][...])

    out_shape = []
    for w in ws:
        out_shape += [jax.ShapeDtypeStruct(w.shape, F32)] * 4
    return pl.pallas_call(body, name=name, out_shape=out_shape, compiler_params=_params())(*parts, *ws, *ms, *vs)


PACK_LANES = 128
PACK_ROWS = 8


def _pack(arrs):
    flat = jnp.concatenate([a.reshape(-1).astype(F32) for a in arrs])
    unit = PACK_LANES * PACK_ROWS
    total = -(-flat.shape[0] // unit) * unit
    return jnp.pad(flat, (0, total - flat.shape[0])).reshape(-1, PACK_LANES)


def _unpack(buf, shapes):
    flat = buf.reshape(N_DEV, -1)
    out, pos = [], 0
    for s in shapes:
        size = math.prod(s)
        out.append(flat[:, pos:pos + size].reshape((N_DEV,) + tuple(s)))
        pos += size
    return out


def _interleave(g):
    return jnp.transpose(g, (1, 0, 2)).reshape(g.shape[1], -1)


def _ffn_backward(dz, dzb, x_in, z_in, g_in, h, cg, cv, u, w_up_t, cw, w_down, tag, exchange=(), exchange_late=(), own_rows=0):
    du = _matmul(dzb, w_down, "nt", BF16, f"ffn{tag}_du", 2048, 1408, 1024)
    d_w_down = _matmul(u, dzb, "tn", BF16, f"ffn{tag}_dwdown", 1408, 1024, 2048)
    (dhg, dhv, dcwg, dcwv, dcbg, dcbv), moved = _ffn_mid_bwd(
        h, cg, cv, du, cw, f"ffn{tag}_mid_bwd", comm=_Comm(exchange=[d_w_down.reshape(N_DEV, -1, D_MODEL), *exchange]))
    d_w_up_t = _matmul_tn_pair(dhg, dhv, x_in, BF16, f"ffn{tag}_dwup", 1408, 1024, 2048,
                               comm=_Comm(exchange=exchange_late) if exchange_late else None)
    if exchange_late:
        d_w_up_t, late = d_w_up_t
        moved = moved + late
    d_up_blocks = d_w_up_t.reshape(N_DEV, -1, D_MODEL)
    outs = _matmul_ln_bwd([(dhg, 0), (dhv, D_FF)], w_up_t, z_in, g_in, dz, f"ffn{tag}_dx_ln_bwd", 256,
                          comm=_Comm(exchange=[(d_up_blocks, 0, own_rows)]) if own_rows else None)
    (dz_in, dzb_in, dg_in, db_in), own = outs if own_rows else (outs, [])
    moved = moved + own
    return (dz_in, dzb_in, dg_in, db_in, d_up_blocks,
            jnp.concatenate([dcwg, dcwv], axis=1), jnp.concatenate([dcbg, dcbv], axis=1), moved)


def kernel(x, ab_w_in, a_conv_w, a_conv_b, a_norm_g, a_norm_b, b_norm_g, b_norm_b, b_spatial_w, b_spatial_b, ab_w_out, c_w_qkv, c_b_qkv, c_sinks, c_w_o, ffn_w_up, ffn_conv_w, ffn_conv_b, ffn_w_down, ln_g, ln_b, loss_target, m_ab_w_in, m_a_conv_w, m_a_conv_b, m_a_norm_g, m_a_norm_b, m_b_norm_g, m_b_norm_b, m_b_spatial_w, m_b_spatial_b, m_ab_w_out, m_c_w_qkv, m_c_b_qkv, m_c_sinks, m_c_w_o, m_ffn_w_up, m_ffn_conv_w, m_ffn_conv_b, m_ffn_w_down, m_ln_g, m_ln_b, v_ab_w_in, v_a_conv_w, v_a_conv_b, v_a_norm_g, v_a_norm_b, v_b_norm_g, v_b_norm_b, v_b_spatial_w, v_b_spatial_b, v_ab_w_out, v_c_w_qkv, v_c_b_qkv, v_c_sinks, v_c_w_o, v_ffn_w_up, v_ffn_conv_w, v_ffn_conv_b, v_ffn_w_down, v_ln_g, v_ln_b):
    me = 4 * lax.axis_index("x") + 2 * lax.axis_index("y") + lax.axis_index("c")
    xt = x[0]
    t = xt.shape[0]

    small_shard_shapes = [a_conv_w.shape, c_b_qkv.shape, ffn_conv_w.shape, ln_g.shape, ln_b.shape]
    up_shard = [jnp.swapaxes(ffn_w_up[l], 0, 1).astype(BF16) for l in range(2)]
    qkv_shard = jnp.swapaxes(c_w_qkv[0], 0, 1).astype(BF16)
    down_shard = [ffn_w_down[l].astype(BF16) for l in range(2)]
    g_win, g_small = _comm_only(
        _Comm(gather=[jnp.swapaxes(ab_w_in[0], 0, 1).astype(BF16), _pack([a_conv_w, c_b_qkv, ffn_conv_w, ln_g, ln_b])]),
        "gather_first")
    w_in = g_win.reshape(-1, D_MODEL)
    g_acw, g_bqkv, g_fcw, g_lng, g_lnb = _unpack(g_small, small_shard_shapes)
    acw = _interleave(g_acw[:, 0])
    bqkv = g_bqkv[:, 0].reshape(1, -1)
    fcw = [_interleave(g_fcw[:, l]) for l in range(2)]
    lng = jnp.transpose(g_lng, (1, 2, 0, 3)).reshape(2, 2, 1, D_MODEL)
    lnb = jnp.transpose(g_lnb, (1, 2, 0, 3)).reshape(2, 2, 1, D_MODEL)
    fcb = [ffn_conv_b[l:l + 1] for l in range(2)]
    ms = b_spatial_w[0]
    mst = jnp.swapaxes(ms, 1, 2)
    sbt = b_spatial_b[0].T

    q_up = up_shard[0].shape[0] // 4
    h0, (g_wout, g_wup0) = _matmul(xt, w_in, "nt", BF16, "mix_in", 1024, 1024, 1024,
                                   comm=_Comm(gather=[ab_w_out[0].astype(BF16), (up_shard[0], 0, q_up, None)]))
    w_out = g_wout.reshape(D_MODEL, D_MODEL)
    (cat, y0), (g_wup0,) = _mixer_mid_fwd(h0, acw, a_conv_b, a_norm_g, a_norm_b, b_norm_g, b_norm_b, ms, sbt, "mix_mid_fwd",
                                          comm=_Comm(gather=[(up_shard[0], q_up, 2 * q_up, g_wup0)]))
    (z1, x1), (g_wup0,) = _matmul_res_ln(cat, w_out, xt, lng[0, 0], lnb[0, 0], "mix_out_ln", 512,
                                         comm=_Comm(gather=[(up_shard[0], 3 * q_up, q_up, g_wup0)]))
    w_up0 = g_wup0.reshape(2 * D_FF, D_MODEL)
    hf0, (g_wdown0, g_wqkv) = _matmul(x1, w_up0, "nt", BF16, "ffn0_up", 2048, 1408, 1024,
                                      comm=_Comm(gather=[down_shard[0], qkv_shard]))
    w_down0 = g_wdown0.reshape(D_FF, D_MODEL)
    w_qkv = g_wqkv.reshape(Q_WIDTH + KV_WIDTH, D_MODEL)
    (u0, cg0, cv0), (g_wup1,) = _ffn_mid_fwd(hf0, fcw[0], fcb[0], "ffn0_mid_fwd",
                                             comm=_Comm(gather=[(up_shard[1], 0, 3 * q_up, None)]))
    (z2, x2), (g_wo, g_wup1) = _matmul_res_ln(
        u0, w_down0, z1, lng[0, 1], lnb[0, 1], "ffn0_down_ln", 512, prev=(lng[0, 0], lnb[0, 0]),
        comm=_Comm(gather=[c_w_o[0].astype(BF16), (up_shard[1], 3 * q_up, q_up, g_wup1)]))
    w_o = g_wo.reshape(D_MODEL, D_MODEL)
    w_up1 = g_wup1.reshape(2 * D_FF, D_MODEL)
    qkv = _matmul(x2, w_qkv, "nt", BF16, "att_qkv", 1024, 1280, 1024, bias=bqkv)
    att, (g_wdown1,) = _attn_fwd(qkv, c_sinks, "att_fwd", comm=_Comm(gather=[down_shard[1]]))
    w_down1 = g_wdown1.reshape(D_FF, D_MODEL)
    z3, x3 = _matmul_res_ln(att, w_o, z2, lng[1, 0], lnb[1, 0], "att_out_ln", 512, prev=(lng[0, 1], lnb[0, 1]))
    hf1 = _matmul(x3, w_up1, "nt", BF16, "ffn1_up", 2048, 1408, 1024)
    u1, cg1, cv1 = _ffn_mid_fwd(hf1, fcw[1], fcb[1], "ffn1_mid_fwd")

    dz4, dz4b, dg11, db11, loss_terms = _matmul_res_ln_loss(u1, w_down1, z3, lng[1, 1], lnb[1, 1], loss_target[0],
                                                      "ffn1_down_ln_loss", 512, prev=(lng[1, 0], lnb[1, 0]))
    head = w_up1.shape[0] // N_DEV // 4
    dz3, dz3b, dg10, db10, d_wup1, d_fcw1, d_fcb1, (p_wdown1, p_wup1h) = _ffn_backward(
        dz4, dz4b, x3, z3, lng[1, 0], hf1, cg1, cv1, u1, w_up1, fcw[1], w_down1, 1, own_rows=head)
    d_att = _matmul(dz3b, w_o, "nt", BF16, "att_dout", 1024, 1024, 1024)
    d_wo = _matmul(att, dz3b, "tn", BF16, "att_dwo", 1024, 1024, 2048)
    rows_up = d_wup1.shape[1]
    first = 3 * rows_up // 4
    (dq, dkv, dbq, dbkv, dsinks), (p_wup1a,) = _attn_bwd(qkv, d_att, c_sinks, "att_bwd",
                                                        comm=_Comm(exchange=[(d_wup1, head, first - head)]))
    d_wqkv = jnp.concatenate([_matmul(dq, x2, "tn", BF16, "att_dwq", 1024, 1024, 2048),
                              _matmul(dkv, x2, "tn", BF16, "att_dwkv", KV_WIDTH, 1024, 1024)], axis=0)
    dz2, dz2b, dg01, db01 = _matmul_ln_bwd([(dq, 0), (dkv, Q_WIDTH)], w_qkv, z2, lng[0, 1], dz3, "att_dx_ln_bwd", 512)
    early = rows_up // 2
    dz1, dz1b, dg00, db00, d_wup0, d_fcw0, d_fcb0, (p_wdown0, p_wup1b, p_wqkv, p_wo, p_wup0a) = _ffn_backward(
        dz2, dz2b, x1, z1, lng[0, 0], hf0, cg0, cv0, u0, w_up0, fcw[0], w_down0, 0, exchange=[(d_wup1, first, rows_up - first)],
        exchange_late=[d_wqkv.reshape(N_DEV, -1, D_MODEL), d_wo.reshape(N_DEV, -1, D_MODEL)], own_rows=early)
    dcat = _matmul(dz1b, w_out, "nt", BF16, "mix_dcat", 1024, 1024, 1024)
    d_wout = _matmul(cat, dz1b, "tn", BF16, "mix_dwout", 1024, 1024, 2048)
    (dh0, d_acw, d_acb, d_ang, d_anb, d_bng, d_bnb, d_ms, d_sb), (p_wup0b, p_wout) = _mixer_mid_bwd(
        h0, y0, dcat, acw, a_norm_g, a_norm_b, b_norm_g, b_norm_b, ms, mst, sbt, "mix_mid_bwd",
        comm=_Comm(exchange=[(d_wup0, early, rows_up - early), d_wout.reshape(N_DEV, -1, D_MODEL)]))
    d_bqkv = jnp.concatenate([dbq, dbkv], axis=1)
    d_lng = jnp.stack([jnp.stack([dg00, dg01]), jnp.stack([dg10, dg11])])
    d_lnb = jnp.stack([jnp.stack([db00, db01]), jnp.stack([db10, db11])])
    small_full = [d_acb, d_ang, d_anb, d_bng, d_bnb, d_ms, d_sb, dsinks[:, :N_Q_HEADS], jnp.concatenate([d_fcb0, d_fcb1], axis=0),
                  d_acw, d_bqkv, jnp.stack([d_fcw0, d_fcw1]), d_lng, d_lnb, loss_terms]
    d_win, (g_small_grads,) = _matmul(dh0, xt, "tn", BF16, "mix_dwin", 1024, 1024, 512, comm=_Comm(gather=[_pack(small_full)]))
    grad_x, (p_win,) = _matmul(dh0, w_in, "nn", F32, "mix_dx", 1024, 1024, 1024, res=dz1, res_scale=ALPHA,
                               comm=_Comm(exchange=[d_win.reshape(N_DEV, -1, D_MODEL)]))


    big = {}
    for nm, p, w, m, v, tr, transposed in [
            ("ab_w_in", [p_win], ab_w_in, m_ab_w_in, v_ab_w_in, 256, True),
            ("ab_w_out", [p_wout], ab_w_out, m_ab_w_out, v_ab_w_out, 128, False),
            ("c_w_qkv", [p_wqkv], c_w_qkv, m_c_w_qkv, v_c_w_qkv, 160, True), ("c_w_o", [p_wo], c_w_o, m_c_w_o, v_c_w_o, 128, False),
            ("ffn_w_up", [(p_wup0a, 0, early), (p_wup0b, early, rows_up - early), (p_wup1h, 0, head), (p_wup1a, head, first - head), (p_wup1b, first, rows_up - first)], ffn_w_up, m_ffn_w_up, v_ffn_w_up, 176, True),
            ("ffn_w_down", [p_wdown0, p_wdown1], ffn_w_down, m_ffn_w_down, v_ffn_w_down, 176, False)]:
        def two_d(a):
            a = jnp.swapaxes(a, 1, 2) if transposed else a
            return a.reshape(-1, a.shape[-1])

        def back(o):
            return jnp.swapaxes(o.reshape(w.shape[0], w.shape[2], w.shape[1]), 1, 2) if transposed else o.reshape(w.shape)

        outs = _adamw_big(p, two_d(w), two_d(m), two_d(v), "adamw_" + nm, tr)
        big[nm] = [back(o) for o in outs]

    *gs, loss_parts = _unpack(g_small_grads, [a.shape for a in small_full])
    loss = 0.5 / D_MODEL * jnp.sum(loss_parts)

    def my_shard(g, width):
        g = g.reshape(g.shape[:-1] + (N_DEV, width))
        return lax.dynamic_index_in_dim(g, me, axis=g.ndim - 2, keepdims=False)

    small_names = ["a_conv_b", "a_norm_g", "a_norm_b", "b_norm_g", "b_norm_b", "b_spatial_w", "b_spatial_b", "c_sinks", "ffn_conv_b",
                   "a_conv_w", "c_b_qkv", "ffn_conv_w", "ln_g", "ln_b"]
    small_w = [a_conv_b, a_norm_g, a_norm_b, b_norm_g, b_norm_b, b_spatial_w, b_spatial_b, c_sinks, ffn_conv_b,
               a_conv_w, c_b_qkv, ffn_conv_w, ln_g, ln_b]
    small_m = [m_a_conv_b, m_a_norm_g, m_a_norm_b, m_b_norm_g, m_b_norm_b, m_b_spatial_w, m_b_spatial_b, m_c_sinks, m_ffn_conv_b,
               m_a_conv_w, m_c_b_qkv, m_ffn_conv_w, m_ln_g, m_ln_b]
    small_v = [v_a_conv_b, v_a_norm_g, v_a_norm_b, v_b_norm_g, v_b_norm_b, v_b_spatial_w, v_b_spatial_b, v_c_sinks, v_ffn_conv_b,
               v_a_conv_w, v_c_b_qkv, v_ffn_conv_w, v_ln_g, v_ln_b]
    gs[9:] = [my_shard(g, w.shape[-1]) for g, w in zip(gs[9:], small_w[9:])]
    two_d = [(-1, w.shape[-1]) for w in small_w]
    outs = _adamw_small([g.reshape((N_DEV,) + w.reshape(s).shape) for g, w, s in zip(gs, small_w, two_d)],
                        [w.reshape(s) for w, s in zip(small_w, two_d)], [m.reshape(s) for m, s in zip(small_m, two_d)],
                        [v.reshape(s) for v, s in zip(small_v, two_d)], "adamw_small")
    small = {nm: [o.reshape(w.shape) for o in outs[4 * a:4 * a + 4]] for a, (nm, w) in enumerate(zip(small_names, small_w))}

    res = {**big, **small}
    order = ["ab_w_in", "a_conv_w", "a_conv_b", "a_norm_g", "a_norm_b", "b_norm_g", "b_norm_b", "b_spatial_w", "b_spatial_b", "ab_w_out",
             "c_w_qkv", "c_b_qkv", "c_sinks", "c_w_o", "ffn_w_up", "ffn_conv_w", "ffn_conv_b", "ffn_w_down", "ln_g", "ln_b"]
    return (loss, grad_x[None], *[res[nm][0] for nm in order], *[res[nm][1] for nm in order],
            *[res[nm][2] for nm in order], *[res[nm][3] for nm in order])
```
